```python
import math
import jax, jax.numpy as jnp
from jax import lax
import numpy as np

D_MODEL = 1024
BATCH = 16
SEQ = 2048
DEPTH = 2

N_MIXERS = 2
N_A_LAYERS = (DEPTH + 1) // 2
N_B_LAYERS = DEPTH // 2
EPS = 1e-6

A_HEADS = D_MODEL // 128
A_DK = 128
A_DV = 256
A_QK = A_HEADS * A_DK
A_VW = A_HEADS * A_DV
A_CONV = 4
A_CHUNK = 64
A_IN = 2 * A_QK + 2 * A_VW + 2 * A_HEADS

B_WINDOWS = (128, 512, 2048)
B_DILATIONS = (1, 4, 16)
B_GROUPS = len(B_WINDOWS)
B_HEADS = D_MODEL // 128
B_DH = 128
B_W = B_HEADS * B_DH
B_BLOCK = 128
B_IN = 3 * B_GROUPS * B_W + B_W
ROPE_THETA = 500000.0
ROPE_DIMS = B_DH // 4

kernel_name = "hybrid_gdn_dilated_swa"


def rms_norm(x, g):
    xf = x.astype(jnp.float32)
    y = xf * lax.rsqrt(jnp.mean(xf * xf, axis=-1, keepdims=True) + EPS)
    return (y * g.astype(jnp.float32)).astype(x.dtype)


def l2_norm(x):
    xf = x.astype(jnp.float32)
    return xf * lax.rsqrt(jnp.sum(xf * xf, axis=-1, keepdims=True) + EPS)


def causal_dwconv(x, w):
    K = w.shape[0]
    return lax.conv_general_dilated(
        x, w[:, None, :].astype(x.dtype), window_strides=(1,), padding=((K - 1, 0),),
        dimension_numbers=("NWC", "WIO", "NWC"), feature_group_count=x.shape[-1])


def gated_delta_rule(q, k, v, g, beta):
    Bn, H, S, DK = q.shape
    DV = v.shape[-1]
    C = A_CHUNK
    N = S // C
    f32 = jnp.float32
    q, k, v, g, beta = (t.astype(f32) for t in (q, k, v, g, beta))
    q = q * (DK ** -0.5)

    def chunk(t):
        return t.reshape((Bn, H, N, C) + t.shape[3:])

    q, k, v, g, beta = map(chunk, (q, k, v, g, beta))
    g = jnp.cumsum(g, axis=-1)
    idx = jnp.arange(C)
    causal = idx[:, None] >= idx[None, :]
    strict = idx[:, None] > idx[None, :]
    decay = jnp.exp(jnp.where(causal, g[..., :, None] - g[..., None, :], -jnp.inf))
    k_beta = k * beta[..., None]
    lower = jnp.where(strict, jnp.einsum("bhnid,bhnjd->bhnij", k_beta, k) * decay, 0.0)
    rhs = jnp.concatenate([v * beta[..., None], k_beta * jnp.exp(g)[..., None]], axis=-1)
    sol = lax.linalg.triangular_solve(lower + jnp.eye(C, dtype=f32), rhs,
                                      left_side=True, lower=True, unit_diagonal=True)
    u, w = sol[..., :DV], sol[..., DV:]
    qk = jnp.einsum("bhnid,bhnjd->bhnij", q, k) * decay

    def step(state, inp):
        q_c, k_c, u_c, w_c, g_c, qk_c = inp
        v_new = u_c - jnp.einsum("bhck,bhkv->bhcv", w_c, state)
        o = (jnp.einsum("bhck,bhkv->bhcv", q_c * jnp.exp(g_c)[..., None], state)
             + jnp.einsum("bhij,bhjv->bhiv", qk_c, v_new))
        g_last = g_c[..., -1:]
        state = (state * jnp.exp(g_last)[..., None]
                 + jnp.einsum("bhck,bhcv->bhkv", k_c * jnp.exp(g_last - g_c)[..., None], v_new))
        return state, o

    xs = tuple(jnp.moveaxis(t, 2, 0) for t in (q, k, u, w, g, qk))
    state0 = jnp.zeros((Bn, H, DK, DV), f32)
    _, o = lax.scan(step, state0, xs)
    return jnp.moveaxis(o, 0, 2).reshape(Bn, H, S, DV)


def gated_deltanet_mixer(h, w_in, conv_w, a_log, dt_bias, out_norm_g, w_out):
    Bn, S, _ = h.shape
    proj = h @ w_in
    n_qkv = 2 * A_QK + A_VW
    qkv = proj[..., :n_qkv]
    z = proj[..., n_qkv:n_qkv + A_VW]
    b_logit = proj[..., n_qkv + A_VW:n_qkv + A_VW + A_HEADS]
    a_logit = proj[..., n_qkv + A_VW + A_HEADS:]
    qkv = jax.nn.silu(causal_dwconv(qkv, conv_w))
    q = qkv[..., :A_QK].reshape(Bn, S, A_HEADS, A_DK).transpose(0, 2, 1, 3)
    k = qkv[..., A_QK:2 * A_QK].reshape(Bn, S, A_HEADS, A_DK).transpose(0, 2, 1, 3)
    v = qkv[..., 2 * A_QK:].reshape(Bn, S, A_HEADS, A_DV).transpose(0, 2, 1, 3)
    q, k = l2_norm(q), l2_norm(k)
    beta = jax.nn.sigmoid(b_logit.astype(jnp.float32)).transpose(0, 2, 1)
    g = (-jnp.exp(a_log.astype(jnp.float32))
         * jax.nn.softplus(a_logit.astype(jnp.float32) + dt_bias.astype(jnp.float32))).transpose(0, 2, 1)
    o = gated_delta_rule(q, k, v, g, beta).transpose(0, 2, 1, 3)
    o = rms_norm(o, out_norm_g).astype(h.dtype)
    o = o * jax.nn.silu(z.reshape(Bn, S, A_HEADS, A_DV))
    return o.reshape(Bn, S, A_VW) @ w_out


def partial_rope(x, cos, sin):
    half = ROPE_DIMS // 2
    x1, x2, xp = x[..., :half], x[..., half:ROPE_DIMS], x[..., ROPE_DIMS:]
    xf1, xf2 = x1.astype(jnp.float32), x2.astype(jnp.float32)
    r1 = (xf1 * cos - xf2 * sin).astype(x.dtype)
    r2 = (xf2 * cos + xf1 * sin).astype(x.dtype)
    return jnp.concatenate([r1, r2, xp], axis=-1)


def dilated_window_attention(q, k, v, dilation, span):
    Bn, S, H, DH = q.shape
    L = S // dilation
    nb = -(-L // B_BLOCK)
    Lp = nb * B_BLOCK

    def to_sub(t):
        t = t.reshape(Bn, L, dilation, H, DH).transpose(0, 2, 3, 1, 4)
        return jnp.pad(t, ((0, 0), (0, 0), (0, 0), (0, Lp - L), (0, 0)))

    qs, ks, vs = map(to_sub, (q, k, v))
    qb = qs.reshape(Bn, dilation, H, nb, B_BLOCK, DH)

    def band(t):
        tp = jnp.pad(t, ((0, 0), (0, 0), (0, 0), (B_BLOCK, 0), (0, 0)))
        prev = tp[..., :Lp, :].reshape(Bn, dilation, H, nb, B_BLOCK, DH)
        cur = tp[..., B_BLOCK:, :].reshape(Bn, dilation, H, nb, B_BLOCK, DH)
        return jnp.concatenate([prev, cur], axis=-2)

    kb, vb = band(ks), band(vs)
    qi = jnp.arange(B_BLOCK)[:, None]
    kj = jnp.arange(2 * B_BLOCK)[None, :]
    dist = qi + B_BLOCK - kj
    key_pos = jnp.arange(nb)[:, None, None] * B_BLOCK + kj[None] - B_BLOCK
    mask = (dist >= 0) & (dist <= span) & (key_pos >= 0)
    s = jnp.einsum("bdhnqe,bdhnke->bdhnqk", qb, kb,
                   preferred_element_type=jnp.float32) * (DH ** -0.5)
    s = jnp.where(mask, s, -jnp.inf)
    lse = jax.nn.logsumexp(s, axis=-1)
    p = jnp.exp(s - lse[..., None])
    o = jnp.einsum("bdhnqk,bdhnke->bdhnqe", p.astype(v.dtype), vb,
                   preferred_element_type=jnp.float32)

    def from_sub(t):
        t = t.reshape((Bn, dilation, H, Lp) + t.shape[5:])[:, :, :, :L]
        t = jnp.moveaxis(t, 3, 1)
        return t.reshape((Bn, S, H) + t.shape[4:])

    return from_sub(o), from_sub(lse)


def dilated_attention_mixer(h, positions, w_in, q_norm_g, k_norm_g, w_out):
    Bn, S, _ = h.shape
    proj = h @ w_in
    n_qkv = 3 * B_GROUPS * B_W
    qkv = proj[..., :n_qkv].reshape(Bn, S, 3, B_GROUPS, B_HEADS, B_DH)
    z = proj[..., n_qkv:]
    q, k, v = qkv[:, :, 0], qkv[:, :, 1], qkv[:, :, 2]
    q = rms_norm(q, q_norm_g[:, None, :])
    k = rms_norm(k, k_norm_g[:, None, :])
    inv_freq = ROPE_THETA ** (-jnp.arange(0, ROPE_DIMS, 2, dtype=jnp.float32) / ROPE_DIMS)
    ang = positions.astype(jnp.float32)[..., None] * inv_freq
    cos = jnp.cos(ang)[:, :, None, None, :]
    sin = jnp.sin(ang)[:, :, None, None, :]
    q, k = partial_rope(q, cos, sin), partial_rope(k, cos, sin)
    outs, lses = [], []
    for gi in range(B_GROUPS):
        o_g, lse_g = dilated_window_attention(q[:, :, gi], k[:, :, gi], v[:, :, gi],
                                              B_DILATIONS[gi], B_WINDOWS[gi] // B_DILATIONS[gi])
        outs.append(o_g)
        lses.append(lse_g)
    wts = jax.nn.softmax(jnp.stack(lses, axis=0), axis=0)
    o = jnp.sum(wts[..., None] * jnp.stack(outs, axis=0), axis=0).astype(h.dtype)
    o = o.reshape(Bn, S, B_W) * jax.nn.silu(z)
    return o @ w_out


def _fwd_setup_inputs(seed: int = 0) -> dict:
    key = jax.random.key(seed)
    ks = jax.random.split(key, 16)
    f32 = jnp.float32
    x = jax.random.normal(ks[0], (BATCH, SEQ, D_MODEL), f32)
    positions = jnp.broadcast_to(jnp.arange(SEQ, dtype=jnp.int32)[None, :], (BATCH, SEQ))
    norm_g = 1.0 + 0.02 * jax.random.normal(ks[1], (DEPTH, D_MODEL), f32)
    a_w_in = jax.random.normal(ks[2], (N_A_LAYERS, D_MODEL, A_IN), f32) * D_MODEL ** -0.5
    a_conv_w = jax.random.normal(ks[3], (N_A_LAYERS, A_CONV, 2 * A_QK + A_VW), f32) * A_CONV ** -0.5
    a_log = jnp.log(jax.random.uniform(ks[4], (N_A_LAYERS, A_HEADS), f32, 1.0, 16.0))
    dt = jnp.exp(jax.random.uniform(ks[5], (N_A_LAYERS, A_HEADS), f32,
                                    math.log(1e-3), math.log(1e-1)))
    a_dt_bias = dt + jnp.log(-jnp.expm1(-dt))
    a_norm_g = 1.0 + 0.02 * jax.random.normal(ks[6], (N_A_LAYERS, A_DV), f32)
    a_w_out = jax.random.normal(ks[7], (N_A_LAYERS, A_VW, D_MODEL), f32) * A_VW ** -0.5
    b_w_in = jax.random.normal(ks[8], (N_B_LAYERS, D_MODEL, B_IN), f32) * D_MODEL ** -0.5
    b_q_norm_g = 1.0 + 0.02 * jax.random.normal(ks[9], (N_B_LAYERS, B_GROUPS, B_DH), f32)
    b_k_norm_g = 1.0 + 0.02 * jax.random.normal(ks[10], (N_B_LAYERS, B_GROUPS, B_DH), f32)
    b_w_out = jax.random.normal(ks[11], (N_B_LAYERS, B_W, D_MODEL), f32) * B_W ** -0.5
    return {"x": x, "positions": positions, "norm_g": norm_g,
            "a_w_in": a_w_in, "a_conv_w": a_conv_w, "a_log": a_log, "a_dt_bias": a_dt_bias,
            "a_norm_g": a_norm_g, "a_w_out": a_w_out,
            "b_w_in": b_w_in, "b_q_norm_g": b_q_norm_g, "b_k_norm_g": b_k_norm_g,
            "b_w_out": b_w_out}


def _fwd_reference(x, positions, norm_g, a_w_in, a_conv_w, a_log, a_dt_bias, a_norm_g, a_w_out,
              b_w_in, b_q_norm_g, b_k_norm_g, b_w_out):
    for i in range(DEPTH):
        h = rms_norm(x, norm_g[i])
        j = i // N_MIXERS
        if i % N_MIXERS == 0:
            y = gated_deltanet_mixer(h, a_w_in[j], a_conv_w[j], a_log[j], a_dt_bias[j],
                                     a_norm_g[j], a_w_out[j])
        else:
            y = dilated_attention_mixer(h, positions, b_w_in[j], b_q_norm_g[j],
                                        b_k_norm_g[j], b_w_out[j])
        x = x + y.astype(x.dtype)
    return x


import jax as _jax
import jax.numpy as _jnp

TWIN_FORMAT = 'train_step'
FWD_PARAMS = ['x', 'positions', 'norm_g', 'a_w_in', 'a_conv_w', 'a_log', 'a_dt_bias', 'a_norm_g', 'a_w_out', 'b_w_in', 'b_q_norm_g', 'b_k_norm_g', 'b_w_out']
TWIN_WEIGHTS = ['norm_g', 'a_w_in', 'a_conv_w', 'a_log', 'a_dt_bias', 'a_norm_g', 'a_w_out', 'b_w_in', 'b_q_norm_g', 'b_k_norm_g', 'b_w_out']
TWIN_DIFF_INPUT = 'x'
TWIN_INPUTS = ['x', 'positions', 'norm_g', 'a_w_in', 'a_conv_w', 'a_log', 'a_dt_bias', 'a_norm_g', 'a_w_out', 'b_w_in', 'b_q_norm_g', 'b_k_norm_g', 'b_w_out', 'loss_target', 'm_norm_g', 'm_a_w_in', 'm_a_conv_w', 'm_a_log', 'm_a_dt_bias', 'm_a_norm_g', 'm_a_w_out', 'm_b_w_in', 'm_b_q_norm_g', 'm_b_k_norm_g', 'm_b_w_out', 'v_norm_g', 'v_a_w_in', 'v_a_conv_w', 'v_a_log', 'v_a_dt_bias', 'v_a_norm_g', 'v_a_w_out', 'v_b_w_in', 'v_b_q_norm_g', 'v_b_k_norm_g', 'v_b_w_out']
TWIN_OUTPUTS = ['loss', 'grad_x', 'grad_norm_g', 'grad_a_w_in', 'grad_a_conv_w', 'grad_a_log', 'grad_a_dt_bias', 'grad_a_norm_g', 'grad_a_w_out', 'grad_b_w_in', 'grad_b_q_norm_g', 'grad_b_k_norm_g', 'grad_b_w_out', 'delta_norm_g', 'delta_a_w_in', 'delta_a_conv_w', 'delta_a_log', 'delta_a_dt_bias', 'delta_a_norm_g', 'delta_a_w_out', 'delta_b_w_in', 'delta_b_q_norm_g', 'delta_b_k_norm_g', 'delta_b_w_out', 'new_m_norm_g', 'new_m_a_w_in', 'new_m_a_conv_w', 'new_m_a_log', 'new_m_a_dt_bias', 'new_m_a_norm_g', 'new_m_a_w_out', 'new_m_b_w_in', 'new_m_b_q_norm_g', 'new_m_b_k_norm_g', 'new_m_b_w_out', 'new_v_norm_g', 'new_v_a_w_in', 'new_v_a_conv_w', 'new_v_a_log', 'new_v_a_dt_bias', 'new_v_a_norm_g', 'new_v_a_w_out', 'new_v_b_w_in', 'new_v_b_q_norm_g', 'new_v_b_k_norm_g', 'new_v_b_w_out']
TWIN_LEAF_KINDS = {'loss': 'loss', 'grad_x': 'grad_x', 'grad_norm_g': 'grad_w', 'grad_a_w_in': 'grad_w', 'grad_a_conv_w': 'grad_w', 'grad_a_log': 'grad_w', 'grad_a_dt_bias': 'grad_w', 'grad_a_norm_g': 'grad_w', 'grad_a_w_out': 'grad_w', 'grad_b_w_in': 'grad_w', 'grad_b_q_norm_g': 'grad_w', 'grad_b_k_norm_g': 'grad_w', 'grad_b_w_out': 'grad_w', 'delta_norm_g': 'delta_w', 'delta_a_w_in': 'delta_w', 'delta_a_conv_w': 'delta_w', 'delta_a_log': 'delta_w', 'delta_a_dt_bias': 'delta_w', 'delta_a_norm_g': 'delta_w', 'delta_a_w_out': 'delta_w', 'delta_b_w_in': 'delta_w', 'delta_b_q_norm_g': 'delta_w', 'delta_b_k_norm_g': 'delta_w', 'delta_b_w_out': 'delta_w', 'new_m_norm_g': 'new_m', 'new_m_a_w_in': 'new_m', 'new_m_a_conv_w': 'new_m', 'new_m_a_log': 'new_m', 'new_m_a_dt_bias': 'new_m', 'new_m_a_norm_g': 'new_m', 'new_m_a_w_out': 'new_m', 'new_m_b_w_in': 'new_m', 'new_m_b_q_norm_g': 'new_m', 'new_m_b_k_norm_g': 'new_m', 'new_m_b_w_out': 'new_m', 'new_v_norm_g': 'new_v', 'new_v_a_w_in': 'new_v', 'new_v_a_conv_w': 'new_v', 'new_v_a_log': 'new_v', 'new_v_a_dt_bias': 'new_v', 'new_v_a_norm_g': 'new_v', 'new_v_a_w_out': 'new_v', 'new_v_b_w_in': 'new_v', 'new_v_b_q_norm_g': 'new_v', 'new_v_b_k_norm_g': 'new_v', 'new_v_b_w_out': 'new_v'}


def _forward(args):
    return _fwd_reference(*[args[k] for k in FWD_PARAMS])


def _output_shape():
    out = _jax.eval_shape(lambda: _forward(_fwd_setup_inputs(0)))
    return out.shape, out.dtype

N_MICROBATCH = 1
ADAM_LR = 0.001
ADAM_B1 = 0.9
ADAM_B2 = 0.999
ADAM_EPS = 1e-08
ADAM_WD = 0.01
ADAM_STEP = 10
PER_EXAMPLE_BATCH_AXIS = {'x': 0, 'positions': 0, 'loss_target': 0}
SHARED_INPUTS = []
_WEIGHT_DTYPES = {'norm_g': _jnp.float32, 'a_w_in': _jnp.float32, 'a_conv_w': _jnp.float32, 'a_log': _jnp.float32, 'a_dt_bias': _jnp.float32, 'a_norm_g': _jnp.float32, 'a_w_out': _jnp.float32, 'b_w_in': _jnp.float32, 'b_q_norm_g': _jnp.float32, 'b_k_norm_g': _jnp.float32, 'b_w_out': _jnp.float32}
MOMENT_SCALE = {'norm_g': 9.499114e+00, 'a_w_in': 2.043422e-01, 'a_conv_w': 2.881479e-01, 'a_log': 4.117151e+01, 'a_dt_bias': 3.923641e+01, 'a_norm_g': 4.580041e+01, 'a_w_out': 5.227350e-01, 'b_w_in': 2.263073e-02, 'b_q_norm_g': 1.672374e-01, 'b_k_norm_g': 1.671981e-01, 'b_w_out': 3.678405e-02}


def _to_microbatches(a, axis):
    t = _jnp.moveaxis(a, axis, 0)
    t = t.reshape((N_MICROBATCH, t.shape[0] // N_MICROBATCH) + t.shape[1:])
    return _jnp.moveaxis(t, 1, axis + 1)


def setup_inputs(seed: int = 0) -> dict:
    inp = _fwd_setup_inputs(seed)
    key = _jax.random.fold_in(_jax.random.key(seed), 7919)
    shape, _ = _output_shape()
    out = dict(inp)
    out["loss_target"] = _jax.random.normal(_jax.random.fold_in(key, 0), shape, _jnp.float32)
    for i, name in enumerate(TWIN_WEIGHTS):
        w = inp[name].astype(_jnp.float32)
        if MOMENT_SCALE is None:
            s = _jnp.sqrt(_jnp.mean(_jnp.square(w)) + 1e-30)
        else:
            s = MOMENT_SCALE[name]
        km, kv = _jax.random.split(_jax.random.fold_in(key, i + 1))
        out[name] = w
        out["m_" + name] = s * _jax.random.normal(km, w.shape, _jnp.float32)
        out["v_" + name] = (s * s) * _jax.random.uniform(kv, w.shape, _jnp.float32, 0.5, 1.5)
    if N_MICROBATCH > 1:
        for name, axis in PER_EXAMPLE_BATCH_AXIS.items():
            out[name] = _to_microbatches(out[name], axis)
    return {'x': out['x'], 'positions': out['positions'], 'norm_g': out['norm_g'], 'a_w_in': out['a_w_in'], 'a_conv_w': out['a_conv_w'], 'a_log': out['a_log'], 'a_dt_bias': out['a_dt_bias'], 'a_norm_g': out['a_norm_g'], 'a_w_out': out['a_w_out'], 'b_w_in': out['b_w_in'], 'b_q_norm_g': out['b_q_norm_g'], 'b_k_norm_g': out['b_k_norm_g'], 'b_w_out': out['b_w_out'], 'loss_target': out['loss_target'], 'm_norm_g': out['m_norm_g'], 'm_a_w_in': out['m_a_w_in'], 'm_a_conv_w': out['m_a_conv_w'], 'm_a_log': out['m_a_log'], 'm_a_dt_bias': out['m_a_dt_bias'], 'm_a_norm_g': out['m_a_norm_g'], 'm_a_w_out': out['m_a_w_out'], 'm_b_w_in': out['m_b_w_in'], 'm_b_q_norm_g': out['m_b_q_norm_g'], 'm_b_k_norm_g': out['m_b_k_norm_g'], 'm_b_w_out': out['m_b_w_out'], 'v_norm_g': out['v_norm_g'], 'v_a_w_in': out['v_a_w_in'], 'v_a_conv_w': out['v_a_conv_w'], 'v_a_log': out['v_a_log'], 'v_a_dt_bias': out['v_a_dt_bias'], 'v_a_norm_g': out['v_a_norm_g'], 'v_a_w_out': out['v_a_w_out'], 'v_b_w_in': out['v_b_w_in'], 'v_b_q_norm_g': out['v_b_q_norm_g'], 'v_b_k_norm_g': out['v_b_k_norm_g'], 'v_b_w_out': out['v_b_w_out']}


def _loss(weights, diff, rest, loss_target):
    with _jax.named_scope("forward"):
        args = {**rest, TWIN_DIFF_INPUT: diff, **{k: w.astype(_WEIGHT_DTYPES[k]) for k, w in weights.items()}}
        y = _forward(args)
    with _jax.named_scope("loss_head"):
        err = _jnp.square(y.astype(_jnp.float32) - loss_target)
        return 0.5 * _jnp.sum(_jnp.mean(err, axis=-1)) if err.ndim else 0.5 * err


def _adamw(w, g, m, v):
    m = ADAM_B1 * m + (1.0 - ADAM_B1) * g
    v = ADAM_B2 * v + (1.0 - ADAM_B2) * _jnp.square(g)
    m_hat = m / (1.0 - ADAM_B1 ** ADAM_STEP)
    v_hat = v / (1.0 - ADAM_B2 ** ADAM_STEP)
    delta = -ADAM_LR * (m_hat / (_jnp.sqrt(v_hat) + ADAM_EPS) + ADAM_WD * w)
    return delta, m, v


def reference(x, positions, norm_g, a_w_in, a_conv_w, a_log, a_dt_bias, a_norm_g, a_w_out, b_w_in, b_q_norm_g, b_k_norm_g, b_w_out, loss_target, m_norm_g, m_a_w_in, m_a_conv_w, m_a_log, m_a_dt_bias, m_a_norm_g, m_a_w_out, m_b_w_in, m_b_q_norm_g, m_b_k_norm_g, m_b_w_out, v_norm_g, v_a_w_in, v_a_conv_w, v_a_log, v_a_dt_bias, v_a_norm_g, v_a_w_out, v_b_w_in, v_b_q_norm_g, v_b_k_norm_g, v_b_w_out):
    given = dict(x=x, positions=positions, norm_g=norm_g, a_w_in=a_w_in, a_conv_w=a_conv_w, a_log=a_log, a_dt_bias=a_dt_bias, a_norm_g=a_norm_g, a_w_out=a_w_out, b_w_in=b_w_in, b_q_norm_g=b_q_norm_g, b_k_norm_g=b_k_norm_g, b_w_out=b_w_out, loss_target=loss_target, m_norm_g=m_norm_g, m_a_w_in=m_a_w_in, m_a_conv_w=m_a_conv_w, m_a_log=m_a_log, m_a_dt_bias=m_a_dt_bias, m_a_norm_g=m_a_norm_g, m_a_w_out=m_a_w_out, m_b_w_in=m_b_w_in, m_b_q_norm_g=m_b_q_norm_g, m_b_k_norm_g=m_b_k_norm_g, m_b_w_out=m_b_w_out, v_norm_g=v_norm_g, v_a_w_in=v_a_w_in, v_a_conv_w=v_a_conv_w, v_a_log=v_a_log, v_a_dt_bias=v_a_dt_bias, v_a_norm_g=v_a_norm_g, v_a_w_out=v_a_w_out, v_b_w_in=v_b_w_in, v_b_q_norm_g=v_b_q_norm_g, v_b_k_norm_g=v_b_k_norm_g, v_b_w_out=v_b_w_out)
    weights = {n: given[n] for n in TWIN_WEIGHTS}
    shared = {n: given[n] for n in SHARED_INPUTS}
    per_example = {n: given[n] for n in ['x', 'positions']}
    grad_fn = _jax.value_and_grad(_loss, argnums=(0, 1))

    def one_microbatch(ex, loss_target):
        ex = dict(ex)
        diff = ex.pop(TWIN_DIFF_INPUT)
        return grad_fn(weights, diff, {**shared, **ex}, loss_target)

    if N_MICROBATCH == 1:
        loss, (grad_w, grad_x) = one_microbatch(per_example, given["loss_target"])
    else:
        def body(carry, xs):
            loss_sum, grad_sum = carry
            l_k, (gw_k, gx_k) = one_microbatch(xs[0], xs[1])
            with _jax.named_scope("update"):
                return (loss_sum + l_k, _jax.tree.map(_jnp.add, grad_sum, gw_k)), gx_k

        init = (_jnp.zeros((), _jnp.float32), _jax.tree.map(_jnp.zeros_like, weights))
        (loss, grad_w), grad_x = _jax.lax.scan(body, init, (per_example, given["loss_target"]))
    with _jax.named_scope("update"):
        delta_w, new_m, new_v = {}, {}, {}
        for n in TWIN_WEIGHTS:
            delta_w[n], new_m[n], new_v[n] = _adamw(weights[n], grad_w[n], given["m_" + n], given["v_" + n])
    return (loss, grad_x, *[grad_w[n] for n in TWIN_WEIGHTS], *[delta_w[n] for n in TWIN_WEIGHTS],
            *[new_m[n] for n in TWIN_WEIGHTS], *[new_v[n] for n in TWIN_WEIGHTS])
```

```python
import functools
import math

import jax
import jax.numpy as jnp
from jax import lax
from jax.experimental import pallas as pl
from jax.experimental.pallas import tpu as pltpu

F32 = jnp.float32
BF16 = jnp.bfloat16

D_MODEL = 1024
EPS = 1e-6
N_DEV = 8

A_HEADS = 8
A_DK = 128
A_DV = 256
A_QK = A_HEADS * A_DK
A_VW = A_HEADS * A_DV
A_CONV = 4
CHUNK = 64
A_IN = 2 * A_QK + 2 * A_VW + 2 * A_HEADS
A_IN_PAD = 2 * A_QK + 2 * A_VW + 128
A_GATE_COL = (2 * A_QK + 2 * A_VW) // 128

B_DILATIONS = (1, 4, 16)
B_GROUPS = 3
B_HEADS = 8
B_DH = 128
B_W = B_HEADS * B_DH
B_BLOCK = 128
B_PIECES = 3 * B_GROUPS + 1
ROPE_THETA = 500000.0
ROPE_DIMS = B_DH // 4
ROPE_HALF = ROPE_DIMS // 2

ADAM_LR = 0.001
ADAM_B1 = 0.9
ADAM_B2 = 0.999
ADAM_EPS = 1e-08
ADAM_WD = 0.01
ADAM_STEP = 10

VMEM_LIMIT = 56 * 1024 * 1024


def _params(sem):
    return pltpu.CompilerParams(dimension_semantics=sem, vmem_limit_bytes=VMEM_LIMIT)


def _bf(x):
    return x.astype(BF16)


def _mm(a, b):
    return jnp.dot(_bf(a), _bf(b), preferred_element_type=F32)


def _mm_nt(a, b):
    return lax.dot_general(_bf(a), _bf(b), (((1,), (1,)), ((), ())), preferred_element_type=F32)


def _mm_tn(a, b):
    return lax.dot_general(_bf(a), _bf(b), (((0,), (0,)), ((), ())), preferred_element_type=F32)


def _split(x):
    hi = _bf(x)
    return hi, _bf(x - hi.astype(F32))


def _mm3(a, b):
    ah, al = _split(a)
    bh, bl = _split(b)
    d = functools.partial(jnp.dot, preferred_element_type=F32)
    return d(ah, bh) + (d(ah, bl) + d(al, bh))


def _colsum_as_col(z):
    zh, zl = _split(z)
    ones = jnp.ones((z.shape[0], 128), BF16)
    tn = functools.partial(lax.dot_general, dimension_numbers=(((0,), (0,)), ((), ())),
                           preferred_element_type=F32)
    return (tn(zh, ones) + tn(zl, ones))[:, 0:1]


def _sigmoid(x):
    return 1.0 / (1.0 + jnp.exp(-x))


def _unit_lower_inverse(m, eye):
    x = eye - m
    p = _mm3(m, m)
    steps = int(math.log2(CHUNK)) - 1
    for i in range(steps):
        x = x + _mm3(x, p)
        if i + 1 < steps:
            p = _mm3(p, p)
    return x


def _gdn_chunk_common(q, k, v, gcb, gr, head, tinv=None):
    c = CHUNK
    lane = lax.broadcasted_iota(jnp.int32, (c, 128), 1)
    row = lax.broadcasted_iota(jnp.int32, (c, c), 0)
    col = lax.broadcasted_iota(jnp.int32, (c, c), 1)
    beta = jnp.sum(jnp.where(lane == head, gcb, 0.0), axis=1, keepdims=True)
    gc = jnp.sum(jnp.where(lane == A_HEADS + head, gcb, 0.0), axis=1, keepdims=True)
    g_last = gr[:, c - 1:c]
    gamma = jnp.exp(gc)
    decay = jnp.where(row >= col, jnp.exp(jnp.minimum(gc - gr, 0.0)), 0.0)
    kb = k * beta
    m = jnp.where(row > col, _mm_nt(kb, k) * decay, 0.0)
    if tinv is None:
        eye = (row == col).astype(F32)
        tinv = _unit_lower_inverse(m, eye)
    u = _mm(tinv, v * beta)
    w = _mm(tinv, kb * gamma)
    p = jnp.where(row >= col, _mm_nt(q, k) * decay, 0.0)
    e_tail = jnp.exp(g_last - gc)
    return dict(beta=beta, gc=gc, g_last=g_last, gamma=gamma, decay=decay, kb=kb, m=m,
                tinv=tinv, u=u, w=w, p=p, e_tail=e_tail, row=row, col=col, lane=lane)


def _gdn_fwd_body(q_ref, k_ref, v_ref, gc_ref, gr_ref, o_ref, tinv_ref, st_ref, s_scr):
    head = pl.program_id(1)
    n_chunks = q_ref.shape[0] // CHUNK
    s_scr[...] = jnp.zeros_like(s_scr)

    def chunk(n, carry):
        rows = pl.ds(pl.multiple_of(n * CHUNK, CHUNK), CHUNK)
        q, k, v = q_ref[rows, :], k_ref[rows, :], v_ref[rows, :]
        t = _gdn_chunk_common(q, k, v, gc_ref[rows, :], gr_ref[n], head)
        s = s_scr[...]
        st_ref[n] = s
        tinv_ref[rows, :] = t["tinv"]
        v_new = t["u"] - _mm(t["w"], s)
        o_ref[rows, :] = _mm(q * t["gamma"], s) + _mm(t["p"], v_new)
        s_scr[...] = s * jnp.exp(t["g_last"]) + _mm_tn(k * t["e_tail"], v_new)
        return carry

    lax.fori_loop(0, n_chunks, chunk, 0)


A_K_COL = A_QK // A_DK
A_V_COL = 2 * A_QK // A_DV
A_HEAD_W = 2 * A_DK + A_DV


def gdn_fwd(qkv, gates_col, gates_row, n_seq):
    t_rows = qkv.shape[0]
    s_len = t_rows // n_seq
    n_chunks = s_len // CHUNK
    grid = (n_seq, A_HEADS)
    return pl.pallas_call(
        _gdn_fwd_body, name="gdn_fwd", grid=grid,
        in_specs=[
            pl.BlockSpec((s_len, A_DK), lambda b, h: (b, h)),
            pl.BlockSpec((s_len, A_DK), lambda b, h: (b, A_K_COL + h)),
            pl.BlockSpec((s_len, A_DV), lambda b, h: (b, A_V_COL + h)),
            pl.BlockSpec((s_len, 128), lambda b, h: (b, 0)),
            pl.BlockSpec((None, None, n_chunks, 1, CHUNK), lambda b, h: (b, A_HEADS + h, 0, 0, 0)),
        ],
        out_specs=[
            pl.BlockSpec((s_len, A_DV), lambda b, h: (b, h)),
            pl.BlockSpec((s_len, CHUNK), lambda b, h: (b * A_HEADS + h, 0)),
            pl.BlockSpec((None, n_chunks, A_DK, A_DV), lambda b, h: (b * A_HEADS + h, 0, 0, 0)),
        ],
        out_shape=[
            jax.ShapeDtypeStruct((t_rows, A_VW), F32),
            jax.ShapeDtypeStruct((n_seq * A_HEADS * s_len, CHUNK), F32),
            jax.ShapeDtypeStruct((n_seq * A_HEADS, n_chunks, A_DK, A_DV), F32),
        ],
        scratch_shapes=[pltpu.VMEM((A_DK, A_DV), F32)],
        compiler_params=_params(("arbitrary", "arbitrary")),
    )(qkv, qkv, qkv, gates_col, gates_row)


def _gdn_bwd_body(q_ref, k_ref, v_ref, gc_ref, gr_ref, tinv_ref, st_ref, do_ref,
                  dqkv_ref, dgc_ref, ds_scr):
    head = pl.program_id(1)
    n_chunks = q_ref.shape[0] // CHUNK
    ds_scr[...] = jnp.zeros_like(ds_scr)

    @pl.when(head == 0)
    def _():
        dgc_ref[...] = jnp.zeros_like(dgc_ref)

    def chunk(i, carry):
        n = n_chunks - 1 - i
        rows = pl.ds(pl.multiple_of(n * CHUNK, CHUNK), CHUNK)
        q, k, v = q_ref[rows, :], k_ref[rows, :], v_ref[rows, :]
        t = _gdn_chunk_common(q, k, v, gc_ref[rows, :], gr_ref[n], head, tinv=tinv_ref[rows, :])
        beta, gamma, decay, kb, e_tail = t["beta"], t["gamma"], t["decay"], t["kb"], t["e_tail"]
        row, col, lane = t["row"], t["col"], t["lane"]
        s = st_ref[n]
        d_o = do_ref[rows, :]
        ds_next = ds_scr[...]
        gamma_last = jnp.exp(t["g_last"])
        v_new = t["u"] - _mm(t["w"], s)
        k_tail = k * e_tail

        d_vn = _mm_tn(t["p"], d_o) + _mm(k_tail, ds_next)
        d_p = jnp.where(row >= col, _mm_nt(d_o, v_new), 0.0)
        d_qg = _mm_nt(d_o, s)
        d_w = -_mm_nt(d_vn, s)
        d_kt = _mm_nt(v_new, ds_next)
        ds_scr[...] = _mm_tn(q * gamma, d_o) + gamma_last * ds_next - _mm_tn(t["w"], d_vn)

        d_ru = _mm_tn(t["tinv"], d_vn)
        d_rw = _mm_tn(t["tinv"], d_w)
        d_m = jnp.where(row > col, -(_mm_nt(d_ru, t["u"]) + _mm_nt(d_rw, t["w"])), 0.0)

        x_p = d_p * decay
        y_m = d_m * decay
        d_kb = _mm(y_m, k) + d_rw * gamma
        dqkv_ref[rows, 0:A_DK] = _mm(x_p, k) + d_qg * gamma
        dqkv_ref[rows, A_DK:2 * A_DK] = _mm_tn(x_p, q) + _mm_tn(y_m, kb) + d_kb * beta + d_kt * e_tail
        dqkv_ref[rows, 2 * A_DK:A_HEAD_W] = d_ru * beta

        d_beta = (jnp.sum(d_ru * v, axis=1, keepdims=True)
                  + jnp.sum(d_kb * k, axis=1, keepdims=True))
        z = d_p * t["p"] + d_m * t["m"]
        eps_tail = jnp.sum(d_kt * k, axis=1, keepdims=True) * e_tail
        d_gc = (jnp.sum(z, axis=1, keepdims=True) - _colsum_as_col(z)
                + jnp.sum(d_qg * q, axis=1, keepdims=True) * gamma
                + jnp.sum(d_rw * kb, axis=1, keepdims=True) * gamma
                - eps_tail)
        d_glast = (jnp.sum(eps_tail, axis=0, keepdims=True)
                   + gamma_last * jnp.sum(jnp.sum(s * ds_next, axis=1, keepdims=True), axis=0, keepdims=True))
        rcol = lax.broadcasted_iota(jnp.int32, (CHUNK, 1), 0)
        d_gc = d_gc + jnp.where(rcol == CHUNK - 1, d_glast, 0.0)
        dgc_ref[rows, :] += (jnp.where(lane == head, d_beta, 0.0)
                             + jnp.where(lane == A_HEADS + head, d_gc, 0.0))
        return carry

    lax.fori_loop(0, n_chunks, chunk, 0)


def gdn_bwd(qkv, gates_col, gates_row, tinv, states, d_o, n_seq):
    t_rows = qkv.shape[0]
    s_len = t_rows // n_seq
    n_chunks = s_len // CHUNK
    grid = (n_seq, A_HEADS)
    v_spec = pl.BlockSpec((s_len, A_DV), lambda b, h: (b, h))
    gate_spec = pl.BlockSpec((s_len, 128), lambda b, h: (b, 0))
    return pl.pallas_call(
        _gdn_bwd_body, name="gdn_bwd", grid=grid,
        in_specs=[
            pl.BlockSpec((s_len, A_DK), lambda b, h: (b, h)),
            pl.BlockSpec((s_len, A_DK), lambda b, h: (b, A_K_COL + h)),
            pl.BlockSpec((s_len, A_DV), lambda b, h: (b, A_V_COL + h)),
            gate_spec,
            pl.BlockSpec((None, None, n_chunks, 1, CHUNK), lambda b, h: (b, A_HEADS + h, 0, 0, 0)),
            pl.BlockSpec((s_len, CHUNK), lambda b, h: (b * A_HEADS + h, 0)),
            pl.BlockSpec((None, n_chunks, A_DK, A_DV), lambda b, h: (b * A_HEADS + h, 0, 0, 0)),
            v_spec,
        ],
        out_specs=[pl.BlockSpec((s_len, A_HEAD_W), lambda b, h: (b, h)), gate_spec],
        out_shape=[
            jax.ShapeDtypeStruct((t_rows, A_HEADS * A_HEAD_W), F32),
            jax.ShapeDtypeStruct((t_rows, 128), F32),
        ],
        scratch_shapes=[pltpu.VMEM((A_DK, A_DV), F32)],
        compiler_params=_params(("arbitrary", "arbitrary")),
    )(qkv, qkv, qkv, gates_col, gates_row, tinv, states, d_o)


def _shift_down(x, j, rows):
    return jnp.where(rows >= j, pltpu.roll(x, j, 0), 0.0)


def _shift_up(x, j, rows):
    n = x.shape[0]
    return jnp.where(rows < n - j, pltpu.roll(x, n - j, 0), 0.0)


def _conv_silu_norm(x, w, cb):
    rows = lax.broadcasted_iota(jnp.int32, (x.shape[0], 1), 0)
    c = x * w[A_CONV - 1:A_CONV, :]
    for j in range(1, A_CONV):
        c = c + _shift_down(x, j, rows) * w[A_CONV - 1 - j:A_CONV - j, :]
    sig = _sigmoid(c)
    a = c * sig
    rn = lax.rsqrt(jnp.sum(a * a, axis=1, keepdims=True) + EPS)
    is_q = cb < A_QK // 128
    is_qk = cb < 2 * A_QK // 128
    q_scale = jnp.where(is_q, A_DK ** -0.5, 1.0).astype(F32)
    return rows, c, sig, a, rn, is_qk, q_scale


def _a_pre_fwd_body(x_ref, w_ref, o_ref):
    cb = pl.program_id(1)
    _, _, _, a, rn, is_qk, q_scale = _conv_silu_norm(x_ref[...], w_ref[...], cb)
    o_ref[...] = a * jnp.where(is_qk, rn * q_scale, 1.0)


def a_pre_fwd(proj_a, conv_w8, n_seq):
    t_rows = proj_a.shape[0]
    s_len = t_rows // n_seq
    n_cb = (2 * A_QK + A_VW) // 128
    blk = pl.BlockSpec((s_len, 128), lambda b, c: (b, c))
    return pl.pallas_call(
        _a_pre_fwd_body, name="a_pre_fwd", grid=(n_seq, n_cb),
        in_specs=[blk, pl.BlockSpec((8, 128), lambda b, c: (0, c))],
        out_specs=blk,
        out_shape=jax.ShapeDtypeStruct((t_rows, n_cb * 128), F32),
        compiler_params=_params(("arbitrary", "arbitrary")),
    )(proj_a, conv_w8)


def _a_pre_bwd_body(x_ref, w_ref, dy_ref, dx_ref, dw_ref):
    b, cb = pl.program_id(1), pl.program_id(0)
    x, w = x_ref[...], w_ref[...]
    rows, c, sig, a, rn, is_qk, q_scale = _conv_silu_norm(x, w, cb)
    dy = dy_ref[...]
    da_n = q_scale * (rn * dy - a * (rn * rn * rn) * jnp.sum(dy * a, axis=1, keepdims=True))
    da = jnp.where(is_qk, da_n, dy)
    dc = da * (sig * (1.0 + c * (1.0 - sig)))
    dx = dc * w[A_CONV - 1:A_CONV, :]
    for j in range(1, A_CONV):
        dx = dx + _shift_up(dc, j, rows) * w[A_CONV - 1 - j:A_CONV - j, :]
    dx_ref[...] = dx

    @pl.when(b == 0)
    def _():
        dw_ref[...] = jnp.zeros_like(dw_ref)

    for j in range(A_CONV):
        xs = x if j == 0 else _shift_down(x, j, rows)
        dw_ref[A_CONV - 1 - j:A_CONV - j, :] += jnp.sum(dc * xs, axis=0, keepdims=True)


def a_pre_bwd(proj_a, conv_w8, dqkv_hm, n_seq):
    t_rows = proj_a.shape[0]
    s_len = t_rows // n_seq
    n_cb = (2 * A_QK + A_VW) // 128
    blk = pl.BlockSpec((s_len, 128), lambda c, b: (b, c))
    wblk = pl.BlockSpec((8, 128), lambda c, b: (0, c))
    per_head = A_HEAD_W // 128
    n_q = A_QK // 128

    def head_major(c, b):
        v_blk = jnp.maximum(c - 2 * n_q, 0)
        col = jnp.where(c < n_q, c * per_head,
                        jnp.where(c < 2 * n_q, (c - n_q) * per_head + 1,
                                  (v_blk // 2) * per_head + 2 + v_blk % 2))
        return (b, col)

    return pl.pallas_call(
        _a_pre_bwd_body, name="a_pre_bwd", grid=(n_cb, n_seq),
        in_specs=[blk, wblk, pl.BlockSpec((s_len, 128), head_major)],
        out_specs=[blk, wblk],
        out_shape=[jax.ShapeDtypeStruct((t_rows, n_cb * 128), F32),
                   jax.ShapeDtypeStruct((8, n_cb * 128), F32)],
        compiler_params=_params(("arbitrary", "arbitrary")),
    )(proj_a, conv_w8, dqkv_hm)


GATE_TILE = 512


def _softplus(y):
    return jnp.maximum(y, 0.0) + jnp.log1p(jnp.exp(-jnp.abs(y)))


def _gate_values(x, prm):
    beta = _sigmoid(x)
    y = x + prm[1:2, :]
    neg_a = -jnp.exp(prm[0:1, :])
    g = neg_a * _softplus(y)
    return beta, y, neg_a, g


def _a_gates_fwd_body(x_ref, prm_ref, gc_ref, gr_ref):
    x = x_ref[...]
    tm = x.shape[0]
    beta, _, _, g = _gate_values(x, prm_ref[...])
    in_chunk = lax.broadcasted_iota(jnp.int32, (tm, 1), 0) % CHUNK
    s = 1
    while s < CHUNK:
        g = g + jnp.where(in_chunk >= s, pltpu.roll(g, s, 0), 0.0)
        s *= 2
    lane = lax.broadcasted_iota(jnp.int32, x.shape, 1)
    out = jnp.where(lane < A_HEADS, beta, jnp.where(lane < 2 * A_HEADS, g, 0.0))
    gc_ref[...] = out
    gr_ref[...] = out.T[0:2 * A_HEADS, :]


def a_gates_fwd(proj_a, prm, n_seq):
    t_rows = proj_a.shape[0]
    s_len = t_rows // n_seq
    tm = min(GATE_TILE, s_len)
    n_t = s_len // tm
    return pl.pallas_call(
        _a_gates_fwd_body, name="a_gates_fwd", grid=(n_seq, n_t),
        in_specs=[pl.BlockSpec((tm, 128), lambda b, i: (b * n_t + i, A_GATE_COL)),
                  pl.BlockSpec((8, 128), lambda b, i: (0, 0))],
        out_specs=[pl.BlockSpec((tm, 128), lambda b, i: (b * n_t + i, 0)),
                   pl.BlockSpec((None, 2 * A_HEADS, tm), lambda b, i: (b, 0, i))],
        out_shape=[jax.ShapeDtypeStruct((t_rows, 128), F32),
                   jax.ShapeDtypeStruct((n_seq, 2 * A_HEADS, s_len), F32)],
        compiler_params=_params(("arbitrary", "arbitrary")),
    )(proj_a, prm)


def _a_gates_bwd_body(x_ref, prm_ref, dgc_ref, dx_ref, dprm_ref):
    first = (pl.program_id(0) == 0) & (pl.program_id(1) == 0)
    x = x_ref[...]
    tm = x.shape[0]
    beta, y, neg_a, g = _gate_values(x, prm_ref[...])
    d = dgc_ref[...]
    in_chunk = lax.broadcasted_iota(jnp.int32, (tm, 1), 0) % CHUNK
    dg = d
    s = 1
    while s < CHUNK:
        dg = dg + jnp.where(in_chunk < CHUNK - s, pltpu.roll(dg, tm - s, 0), 0.0)
        s *= 2
    lane = lax.broadcasted_iota(jnp.int32, x.shape, 1)
    is_decay = (lane >= A_HEADS) & (lane < 2 * A_HEADS)
    d_alogit = jnp.where(is_decay, dg * neg_a * _sigmoid(y), 0.0)
    dx_ref[...] = jnp.where(lane < A_HEADS, d * beta * (1.0 - beta), d_alogit)

    @pl.when(first)
    def _():
        dprm_ref[...] = jnp.zeros_like(dprm_ref)

    dprm_ref[0:1, :] += jnp.sum(jnp.where(is_decay, dg * g, 0.0), axis=0, keepdims=True)
    dprm_ref[1:2, :] += jnp.sum(d_alogit, axis=0, keepdims=True)


def a_gates_bwd(proj_a, prm, dgates_col, n_seq):
    t_rows = proj_a.shape[0]
    s_len = t_rows // n_seq
    tm = min(GATE_TILE, s_len)
    n_t = s_len // tm
    return pl.pallas_call(
        _a_gates_bwd_body, name="a_gates_bwd", grid=(n_seq, n_t),
        in_specs=[pl.BlockSpec((tm, 128), lambda b, i: (b * n_t + i, A_GATE_COL)),
                  pl.BlockSpec((8, 128), lambda b, i: (0, 0)),
                  pl.BlockSpec((tm, 128), lambda b, i: (b * n_t + i, 0))],
        out_specs=[pl.BlockSpec((tm, 128), lambda b, i: (b * n_t + i, 0)),
                   pl.BlockSpec((8, 128), lambda b, i: (0, 0))],
        out_shape=[jax.ShapeDtypeStruct((t_rows, 128), F32),
                   jax.ShapeDtypeStruct((8, 128), F32)],
        compiler_params=_params(("arbitrary", "arbitrary")),
    )(proj_a, prm, dgates_col)


ROW_TILE = 512
A_Z_COL = (2 * A_QK + A_VW) // A_DV


def _silu_parts(z):
    sig = _sigmoid(z)
    return z * sig, sig * (1.0 + z * (1.0 - sig))


def _a_post_fwd_body(o_ref, z_ref, g_ref, og_ref):
    o = o_ref[...]
    r = lax.rsqrt(jnp.mean(o * o, axis=1, keepdims=True) + EPS)
    silu, _ = _silu_parts(z_ref[...])
    og_ref[...] = ((o * r * g_ref[0:1, :]) * silu).astype(og_ref.dtype)


def a_post_fwd(o, proj_a, norm_g8):
    t_rows = o.shape[0]
    tm = min(ROW_TILE, t_rows)
    blk = pl.BlockSpec((tm, A_DV), lambda i, h: (i, h))
    return pl.pallas_call(
        _a_post_fwd_body, name="a_post_fwd", grid=(t_rows // tm, A_HEADS),
        in_specs=[blk, pl.BlockSpec((tm, A_DV), lambda i, h: (i, A_Z_COL + h)),
                  pl.BlockSpec((8, A_DV), lambda i, h: (0, 0))],
        out_specs=blk,
        out_shape=jax.ShapeDtypeStruct((t_rows, A_VW), BF16),
        compiler_params=_params(("arbitrary", "arbitrary")),
    )(o, proj_a, norm_g8)


def _a_post_bwd_body(o_ref, z_ref, g_ref, dog_ref, do_ref, dz_ref, dg_ref):
    first = (pl.program_id(0) == 0) & (pl.program_id(1) == 0)
    o, z, d_og = o_ref[...], z_ref[...], dog_ref[...]
    gain = g_ref[0:1, :]
    r = lax.rsqrt(jnp.mean(o * o, axis=1, keepdims=True) + EPS)
    silu, dsilu = _silu_parts(z)
    xr = o * r
    d_on = d_og * silu
    dz_ref[...] = d_og * (xr * gain) * dsilu
    u = d_on * gain
    do_ref[...] = r * u - xr * (r * r) * jnp.mean(u * o, axis=1, keepdims=True)

    @pl.when(first)
    def _():
        dg_ref[...] = jnp.zeros_like(dg_ref)

    dg_ref[0:1, :] += jnp.sum(d_on * xr, axis=0, keepdims=True)


def a_post_bwd(o, proj_a, norm_g8, d_og):
    t_rows = o.shape[0]
    tm = min(ROW_TILE, t_rows)
    blk = pl.BlockSpec((tm, A_DV), lambda i, h: (i, h))
    gblk = pl.BlockSpec((8, A_DV), lambda i, h: (0, 0))
    return pl.pallas_call(
        _a_post_bwd_body, name="a_post_bwd", grid=(t_rows // tm, A_HEADS),
        in_specs=[blk, pl.BlockSpec((tm, A_DV), lambda i, h: (i, A_Z_COL + h)), gblk, blk],
        out_specs=[blk, blk, gblk],
        out_shape=[jax.ShapeDtypeStruct((t_rows, A_VW), F32),
                   jax.ShapeDtypeStruct((t_rows, A_VW), F32),
                   jax.ShapeDtypeStruct((8, A_DV), F32)],
        compiler_params=_params(("arbitrary", "arbitrary")),
    )(o, proj_a, norm_g8, d_og)


NEG_BIG = -1e30
ATT_SCALE = B_DH ** -0.5


def _swap_rope_halves(x):
    lane = lax.broadcasted_iota(jnp.int32, x.shape, 1)
    return jnp.where(lane < ROPE_HALF, pltpu.roll(x, B_DH - ROPE_HALF, 1),
                     jnp.where(lane < ROPE_DIMS, pltpu.roll(x, ROPE_HALF, 1), 0.0))


def _norm_rope(x, gain, cos_t, sin_t):
    r = lax.rsqrt(jnp.mean(x * x, axis=1, keepdims=True) + EPS)
    xn = x * r * gain
    return xn * cos_t + _swap_rope_halves(xn) * sin_t, r


def _norm_rope_bwd(x, r, gain, cos_t, sin_t, dy):
    d_xn = dy * cos_t + _swap_rope_halves(dy * sin_t)
    xr = x * r
    u = d_xn * gain
    dx = r * u - xr * (r * r) * jnp.mean(u * x, axis=1, keepdims=True)
    return dx, jnp.sum(d_xn * xr, axis=0, keepdims=True)


def _stream_rows(idx, dilation, s_len):
    nb = s_len // dilation // B_BLOCK
    r = idx // nb
    m = idx % nb
    cur = r + m * (B_BLOCK * dilation)
    prev = r + jnp.maximum(m - 1, 0) * (B_BLOCK * dilation)
    return cur, prev, m > 0


def _rows(start, dilation):
    if dilation == 1:
        return pl.ds(start, B_BLOCK)
    return pl.ds(start, B_BLOCK, stride=dilation)


def _block_scores(qb, kc, kp, has_prev):
    qi = lax.broadcasted_iota(jnp.int32, (B_BLOCK, B_BLOCK), 0)
    kj = lax.broadcasted_iota(jnp.int32, (B_BLOCK, B_BLOCK), 1)
    s_c = jnp.where(qi >= kj, _mm_nt(qb, kc) * ATT_SCALE, NEG_BIG)
    s_p = jnp.where((kj >= qi) & has_prev, _mm_nt(qb, kp) * ATT_SCALE, NEG_BIG)
    return s_c, s_p


def _attn_fwd_body(qkv_ref, z_ref, cos_ref, sin_ref, gain_ref, og_ref, o_ref, lse_ref,
                   qn_scr, kn_scr, og_scr, lg_scr):
    head, grp = pl.program_id(1), pl.program_id(2)
    s_len = z_ref.shape[0]
    n_blocks = s_len // B_BLOCK
    cos_t, sin_t = cos_ref[...], sin_ref[...]

    for gi, dil in enumerate(B_DILATIONS):
        @pl.when(grp == gi)
        def _(gi=gi, dil=dil):
            qn_scr[...], _ = _norm_rope(qkv_ref[0], gain_ref[gi:gi + 1, :], cos_t, sin_t)
            kn_scr[...], _ = _norm_rope(qkv_ref[1], gain_ref[B_GROUPS + gi:B_GROUPS + gi + 1, :], cos_t, sin_t)

            def block(idx, carry):
                cur, prev, has_prev = _stream_rows(idx, dil, s_len)
                rc, rp = _rows(cur, dil), _rows(prev, dil)
                s_c, s_p = _block_scores(qn_scr[rc, :], kn_scr[rc, :], kn_scr[rp, :], has_prev)
                mx = jnp.maximum(jnp.max(s_c, axis=1, keepdims=True), jnp.max(s_p, axis=1, keepdims=True))
                p_c, p_p = jnp.exp(s_c - mx), jnp.exp(s_p - mx)
                den = jnp.sum(p_c, axis=1, keepdims=True) + jnp.sum(p_p, axis=1, keepdims=True)
                acc = _mm(p_c, qkv_ref.at[2][rc, :]) + _mm(p_p, qkv_ref.at[2][rp, :])
                og_scr.at[gi][rc, :] = acc / den
                lg_scr.at[gi][rc, :] = jnp.broadcast_to(mx + jnp.log(den), (B_BLOCK, B_DH))
                return carry

            lax.fori_loop(0, n_blocks, block, 0)

    @pl.when(grp == B_GROUPS - 1)
    def _():
        l0, l1, l2 = lg_scr[0], lg_scr[1], lg_scr[2]
        mx = jnp.maximum(jnp.maximum(l0, l1), l2)
        w0, w1, w2 = jnp.exp(l0 - mx), jnp.exp(l1 - mx), jnp.exp(l2 - mx)
        den = w0 + w1 + w2
        o = (w0 * og_scr[0] + w1 * og_scr[1] + w2 * og_scr[2]) / den
        silu, _ = _silu_parts(z_ref[...])
        o_ref[...] = o
        og_ref[...] = (o * silu).astype(og_ref.dtype)
        @pl.when(head == 0)
        def _():
            lse_ref[...] = jnp.zeros_like(lse_ref)

        lane = lax.broadcasted_iota(jnp.int32, o.shape, 1)
        lse_ref[...] = jnp.where(lane == head, mx + jnp.log(den), lse_ref[...])


def attn_fwd(proj_b, cos_t, sin_t, gains8, n_seq):
    t_rows = proj_b.shape[1]
    s_len = t_rows // n_seq
    head_blk = pl.BlockSpec((s_len, B_DH), lambda b, h, g: (b, h))
    seq_blk = pl.BlockSpec((s_len, 128), lambda b, h, g: (b, 0))
    return pl.pallas_call(
        _attn_fwd_body, name="attn_fwd", grid=(n_seq, B_HEADS, B_GROUPS),
        in_specs=[
            pl.BlockSpec((3, s_len, B_DH), lambda b, h, g: (g, b, h)),
            pl.BlockSpec((None, s_len, B_DH), lambda b, h, g: (B_PIECES - 1, b, h)),
            seq_blk, seq_blk,
            pl.BlockSpec((8, 128), lambda b, h, g: (0, 0)),
        ],
        out_specs=[head_blk, head_blk, seq_blk],
        out_shape=[jax.ShapeDtypeStruct((t_rows, B_W), BF16),
                   jax.ShapeDtypeStruct((t_rows, B_W), F32),
                   jax.ShapeDtypeStruct((t_rows, 128), F32)],
        scratch_shapes=[pltpu.VMEM((s_len, B_DH), F32), pltpu.VMEM((s_len, B_DH), F32),
                        pltpu.VMEM((B_GROUPS, s_len, B_DH), F32), pltpu.VMEM((B_GROUPS, s_len, B_DH), F32)],
        compiler_params=_params(("arbitrary", "arbitrary", "arbitrary")),
    )(proj_b, proj_b, cos_t, sin_t, gains8)


def _attn_bwd_body(qkv_ref, z_ref, cos_ref, sin_ref, gain_ref, dog_ref, o_ref, lse_ref,
                   dqkv_ref, dz_ref, dgain_ref,
                   qn_scr, kn_scr, dqn_scr, dkn_scr, do_scr, dl_scr, ls_scr):
    head, grp = pl.program_id(1), pl.program_id(2)
    first = (pl.program_id(0) == 0) & (head == 0) & (grp == 0)
    s_len = z_ref.shape[0]
    n_blocks = s_len // B_BLOCK
    cos_t, sin_t = cos_ref[...], sin_ref[...]

    @pl.when(first)
    def _():
        dgain_ref[...] = jnp.zeros_like(dgain_ref)

    @pl.when(grp == 0)
    def _():
        d_og, o = dog_ref[...], o_ref[...]
        silu, dsilu = _silu_parts(z_ref[...])
        d_o = d_og * silu
        dz_ref[...] = d_og * o * dsilu
        do_scr[...] = d_o
        dl_scr[...] = jnp.broadcast_to(jnp.sum(d_o * o, axis=1, keepdims=True), o.shape)
        lane = lax.broadcasted_iota(jnp.int32, o.shape, 1)
        ls_scr[...] = jnp.broadcast_to(
            jnp.sum(jnp.where(lane == head, lse_ref[...], 0.0), axis=1, keepdims=True), o.shape)

    for gi, dil in enumerate(B_DILATIONS):
        @pl.when(grp == gi)
        def _(gi=gi, dil=dil):
            q_raw, k_raw = qkv_ref[0], qkv_ref[1]
            gq = gain_ref[gi:gi + 1, :]
            gk = gain_ref[B_GROUPS + gi:B_GROUPS + gi + 1, :]
            qn_scr[...], rq = _norm_rope(q_raw, gq, cos_t, sin_t)
            kn_scr[...], rk = _norm_rope(k_raw, gk, cos_t, sin_t)
            dkn_scr[...] = jnp.zeros_like(dkn_scr)
            dqkv_ref[2] = jnp.zeros((s_len, B_DH), F32)

            def block(idx, carry):
                cur, prev, has_prev = _stream_rows(idx, dil, s_len)
                rc, rp = _rows(cur, dil), _rows(prev, dil)
                qb, kc, kp = qn_scr[rc, :], kn_scr[rc, :], kn_scr[rp, :]
                vc, vp = qkv_ref.at[2][rc, :], qkv_ref.at[2][rp, :]
                d_o = do_scr[rc, :]
                lse = ls_scr[rc, :][:, 0:1]
                delta = dl_scr[rc, :][:, 0:1]
                s_c, s_p = _block_scores(qb, kc, kp, has_prev)
                p_c, p_p = jnp.exp(s_c - lse), jnp.exp(s_p - lse)
                ds_c = p_c * (_mm_nt(d_o, vc) - delta)
                ds_p = p_p * (_mm_nt(d_o, vp) - delta)
                dqn_scr[rc, :] = (_mm(ds_c, kc) + _mm(ds_p, kp)) * ATT_SCALE
                dkn_scr[rc, :] += _mm_tn(ds_c, qb) * ATT_SCALE
                dqkv_ref.at[2][rc, :] += _mm_tn(p_c, d_o)

                @pl.when(has_prev)
                def _():
                    dkn_scr[rp, :] += _mm_tn(ds_p, qb) * ATT_SCALE
                    dqkv_ref.at[2][rp, :] += _mm_tn(p_p, d_o)

                return carry

            lax.fori_loop(0, n_blocks, block, 0)
            dq, dgq = _norm_rope_bwd(q_raw, rq, gq, cos_t, sin_t, dqn_scr[...])
            dk, dgk = _norm_rope_bwd(k_raw, rk, gk, cos_t, sin_t, dkn_scr[...])
            dqkv_ref[0] = dq
            dqkv_ref[1] = dk
            dgain_ref[gi:gi + 1, :] += dgq
            dgain_ref[B_GROUPS + gi:B_GROUPS + gi + 1, :] += dgk


def attn_bwd(proj_b, cos_t, sin_t, gains8, d_og, o, lse, n_seq):
    t_rows = proj_b.shape[1]
    s_len = t_rows // n_seq
    head_blk = pl.BlockSpec((s_len, B_DH), lambda b, h, g: (b, h))
    seq_blk = pl.BlockSpec((s_len, 128), lambda b, h, g: (b, 0))
    grp_blk = pl.BlockSpec((3, s_len, B_DH), lambda b, h, g: (g, b, h))
    gain_blk = pl.BlockSpec((8, 128), lambda b, h, g: (0, 0))
    return pl.pallas_call(
        _attn_bwd_body, name="attn_bwd", grid=(n_seq, B_HEADS, B_GROUPS),
        in_specs=[
            grp_blk,
            pl.BlockSpec((None, s_len, B_DH), lambda b, h, g: (B_PIECES - 1, b, h)),
            seq_blk, seq_blk, gain_blk, head_blk, head_blk, seq_blk,
        ],
        out_specs=[grp_blk, head_blk, gain_blk],
        out_shape=[jax.ShapeDtypeStruct((3 * B_GROUPS, t_rows, B_W), F32),
                   jax.ShapeDtypeStruct((t_rows, B_W), F32),
                   jax.ShapeDtypeStruct((8, 128), F32)],
        scratch_shapes=[pltpu.VMEM((s_len, B_DH), F32) for _ in range(7)],
        compiler_params=_params(("arbitrary", "arbitrary", "arbitrary")),
    )(proj_b, proj_b, cos_t, sin_t, gains8, d_og, o, lse)


def rope_tables(positions):
    inv_freq = ROPE_THETA ** (-jnp.arange(0, ROPE_DIMS, 2, dtype=F32) / ROPE_DIMS)
    ang = positions.astype(F32)[:, None] * inv_freq
    cos, sin = jnp.cos(ang), jnp.sin(ang)
    t_rows = positions.shape[0]
    rest = B_DH - ROPE_DIMS
    cos_t = jnp.concatenate([cos, cos, jnp.ones((t_rows, rest), F32)], axis=1)
    sin_t = jnp.concatenate([-sin, sin, jnp.zeros((t_rows, rest), F32)], axis=1)
    return cos_t, sin_t


def _rms_fwd_body(x_ref, g_ref, h_ref, *, layer):
    x = x_ref[...]
    r = lax.rsqrt(jnp.mean(x * x, axis=1, keepdims=True) + EPS)
    h_ref[...] = (x * r * g_ref[layer:layer + 1, :]).astype(h_ref.dtype)


def rms_fwd(x, gains8, layer):
    t_rows, d = x.shape
    tm = min(ROW_TILE, t_rows)
    return pl.pallas_call(
        functools.partial(_rms_fwd_body, layer=layer), name=f"rms_fwd_{layer}", grid=(t_rows // tm,),
        in_specs=[pl.BlockSpec((tm, d), lambda i: (i, 0)), pl.BlockSpec((8, d), lambda i: (0, 0))],
        out_specs=pl.BlockSpec((tm, d), lambda i: (i, 0)),
        out_shape=jax.ShapeDtypeStruct((t_rows, d), BF16),
        compiler_params=_params(("arbitrary",)),
    )(x, gains8)


def _rms_bwd_body(x_ref, g_ref, dh_ref, res_ref, dx_ref, dg_ref, *, layer):
    x, dh = x_ref[...], dh_ref[...]
    r = lax.rsqrt(jnp.mean(x * x, axis=1, keepdims=True) + EPS)
    xr = x * r
    u = dh * g_ref[layer:layer + 1, :]
    dx_ref[...] = res_ref[...] + r * u - xr * (r * r) * jnp.mean(u * x, axis=1, keepdims=True)

    @pl.when(pl.program_id(0) == 0)
    def _():
        dg_ref[...] = jnp.zeros_like(dg_ref)

    dg_ref[0:1, :] += jnp.sum(dh * xr, axis=0, keepdims=True)


def rms_bwd(x, gains8, layer, dh, d_res):
    t_rows, d = x.shape
    tm = min(ROW_TILE, t_rows)
    blk = pl.BlockSpec((tm, d), lambda i: (i, 0))
    gblk = pl.BlockSpec((8, d), lambda i: (0, 0))
    return pl.pallas_call(
        functools.partial(_rms_bwd_body, layer=layer), name=f"rms_bwd_{layer}", grid=(t_rows // tm,),
        in_specs=[blk, gblk, blk, blk],
        out_specs=[blk, gblk],
        out_shape=[jax.ShapeDtypeStruct((t_rows, d), F32), jax.ShapeDtypeStruct((8, d), F32)],
        compiler_params=_params(("arbitrary",)),
    )(x, gains8, dh, d_res)


def _piece_col(p):
    return jnp.where(p < 3 * B_GROUPS, (p % 3) * B_GROUPS + p // 3, 3 * B_GROUPS)


def _mm_nn_body(a_ref, w_ref, *rest, has_res):
    o_ref = rest[-1]
    acc = jnp.dot(a_ref[...], w_ref[...], preferred_element_type=F32)
    if has_res:
        acc = acc + rest[0][...]
    o_ref[...] = acc


def mm_nn(a, w, residual=None, *, tn, name):
    m, k = a.shape
    n = w.shape[1]
    tm = min(ROW_TILE, m)
    in_specs = [pl.BlockSpec((tm, k), lambda j, i: (i, 0)), pl.BlockSpec((k, tn), lambda j, i: (0, j))]
    args = [a, w]
    if residual is not None:
        in_specs.append(pl.BlockSpec((tm, tn), lambda j, i: (i, j)))
        args.append(residual)
    return pl.pallas_call(
        functools.partial(_mm_nn_body, has_res=residual is not None), name=name, grid=(n // tn, m // tm),
        in_specs=in_specs,
        out_specs=pl.BlockSpec((tm, tn), lambda j, i: (i, j)),
        out_shape=jax.ShapeDtypeStruct((m, n), F32),
        compiler_params=_params(("arbitrary", "arbitrary")),
    )(*args)


def mm_nn_pieces(a, w, *, name):
    m, k = a.shape
    tm = min(ROW_TILE, m)
    return pl.pallas_call(
        functools.partial(_mm_nn_body, has_res=False), name=name, grid=(B_PIECES, m // tm),
        in_specs=[pl.BlockSpec((tm, k), lambda p, i: (i, 0)),
                  pl.BlockSpec((k, B_W), lambda p, i: (0, _piece_col(p)))],
        out_specs=pl.BlockSpec((None, tm, B_W), lambda p, i: (p, i, 0)),
        out_shape=jax.ShapeDtypeStruct((B_PIECES, m, B_W), F32),
        compiler_params=_params(("arbitrary", "arbitrary")),
    )(a, w)


NT_ROW_TILE = 1024


def _mm_nt_body(g_ref, w_ref, *rest, has_init):
    o_ref = rest[-1]
    j = pl.program_id(1)
    part = lax.dot_general(_bf(g_ref[...]), w_ref[...], (((1,), (1,)), ((), ())), preferred_element_type=F32)

    @pl.when(j == 0)
    def _():
        o_ref[...] = part + rest[0][...] if has_init else part

    @pl.when(j > 0)
    def _():
        o_ref[...] += part


def mm_nt(g, w, init=None, *, tn, col_off=0, name):
    m, n = g.shape
    k = w.shape[0]
    tm = min(NT_ROW_TILE, m)
    in_specs = [pl.BlockSpec((tm, tn), lambda i, j: (i, j)),
                pl.BlockSpec((k, tn), lambda i, j: (0, col_off + j))]
    args = [g, w]
    if init is not None:
        in_specs.append(pl.BlockSpec((tm, k), lambda i, j: (i, 0)))
        args.append(init)
    return pl.pallas_call(
        functools.partial(_mm_nt_body, has_init=init is not None), name=name, grid=(m // tm, n // tn),
        in_specs=in_specs,
        out_specs=pl.BlockSpec((tm, k), lambda i, j: (i, 0)),
        out_shape=jax.ShapeDtypeStruct((m, k), F32),
        compiler_params=_params(("arbitrary", "arbitrary")),
    )(*args)


def mm_nt_pieces(g9, w, *, name):
    n_p, m, _ = g9.shape
    k = w.shape[0]
    tm = min(NT_ROW_TILE, m)
    return pl.pallas_call(
        functools.partial(_mm_nt_body, has_init=False), name=name, grid=(m // tm, n_p),
        in_specs=[pl.BlockSpec((None, tm, B_W), lambda i, p: (p, i, 0)),
                  pl.BlockSpec((k, B_W), lambda i, p: (0, _piece_col(p)))],
        out_specs=pl.BlockSpec((tm, k), lambda i, p: (i, 0)),
        out_shape=jax.ShapeDtypeStruct((m, k), F32),
        compiler_params=_params(("arbitrary", "arbitrary")),
    )(g9, w)


def _mm_tn_body(a_ref, g_ref, o_ref):
    o_ref[...] = lax.dot_general(a_ref[...], _bf(g_ref[...]), (((0,), (0,)), ((), ())),
                                 preferred_element_type=F32).astype(o_ref.dtype)


def mm_tn(a, g, *, tn, out_dtype, name):
    m, k = a.shape
    n = g.shape[1]
    return pl.pallas_call(
        _mm_tn_body, name=name, grid=(n // tn,),
        in_specs=[pl.BlockSpec((m, k), lambda j: (0, 0)), pl.BlockSpec((m, tn), lambda j: (0, j))],
        out_specs=pl.BlockSpec((k, tn), lambda j: (0, j)),
        out_shape=jax.ShapeDtypeStruct((k, n), out_dtype),
        compiler_params=_params(("arbitrary",)),
    )(a, g)


B_UNIT = 256
B_IN_COLS = B_PIECES * B_W
B_SHARD_UNITS = B_IN_COLS // N_DEV // B_UNIT


def mm_tn_b_in(a, g9, gz, *, out_dtype, name):
    m, k = a.shape
    per_piece = B_W // B_UNIT
    n_units = B_IN_COLS // B_UNIT

    def g_map(u):
        nat = jnp.minimum(u // per_piece, 3 * B_GROUPS - 1)
        piece = (nat % B_GROUPS) * 3 + nat // B_GROUPS
        return (piece, 0, u % per_piece)

    def body(a_ref, g_ref, z_ref, o_ref):
        u = pl.program_id(0)

        @pl.when(u < 3 * B_GROUPS * per_piece)
        def _():
            _mm_tn_body(a_ref, g_ref, o_ref)

        @pl.when(u >= 3 * B_GROUPS * per_piece)
        def _():
            _mm_tn_body(a_ref, z_ref, o_ref)

    return pl.pallas_call(
        body, name=name, grid=(n_units,),
        in_specs=[pl.BlockSpec((m, k), lambda u: (0, 0)),
                  pl.BlockSpec((None, m, B_UNIT), g_map),
                  pl.BlockSpec((m, B_UNIT), lambda u: (0, jnp.where(u < 3 * B_GROUPS * per_piece, 0, u % per_piece)))],
        out_specs=pl.BlockSpec((None, k, B_UNIT), lambda u: (u // B_SHARD_UNITS, 0, u % B_SHARD_UNITS)),
        out_shape=jax.ShapeDtypeStruct((N_DEV, k, B_IN_COLS // N_DEV), out_dtype),
        compiler_params=_params(("arbitrary",)),
    )(a, g9, gz)


def _loss_body(y_ref, t_ref, dy_ref, loss_ref, acc):
    i = pl.program_id(0)
    d = y_ref.shape[1]
    err = y_ref[...] - t_ref[...]
    dy_ref[...] = err * (1.0 / d)

    @pl.when(i == 0)
    def _():
        acc[...] = jnp.zeros_like(acc)

    acc[...] += jnp.sum(err * err, axis=0, keepdims=True)

    @pl.when(i == pl.num_programs(0) - 1)
    def _():
        total = jnp.sum(acc[...], axis=1, keepdims=True) * (0.5 / d)
        loss_ref[...] = jnp.broadcast_to(total, loss_ref.shape)


def loss_head(y, target):
    t_rows, d = y.shape
    tm = min(ROW_TILE, t_rows)
    blk = pl.BlockSpec((tm, d), lambda i: (i, 0))
    return pl.pallas_call(
        _loss_body, name="loss_head", grid=(t_rows // tm,),
        in_specs=[blk, blk],
        out_specs=[blk, pl.BlockSpec((8, 128), lambda i: (0, 0))],
        out_shape=[jax.ShapeDtypeStruct((t_rows, d), F32), jax.ShapeDtypeStruct((8, 128), F32)],
        scratch_shapes=[pltpu.VMEM((1, d), F32)],
        compiler_params=_params(("arbitrary",)),
    )(y, target)


def _adamw_body(p_ref, w_ref, m_ref, v_ref, g_ref, d_ref, nm_ref, nv_ref):
    g = p_ref[0].astype(F32)
    for s in range(1, N_DEV):
        g = g + p_ref[s].astype(F32)
    w = w_ref[...]
    m = ADAM_B1 * m_ref[...] + (1.0 - ADAM_B1) * g
    v = ADAM_B2 * v_ref[...] + (1.0 - ADAM_B2) * (g * g)
    m_hat = m / (1.0 - ADAM_B1 ** ADAM_STEP)
    v_hat = v / (1.0 - ADAM_B2 ** ADAM_STEP)
    g_ref[...] = g
    d_ref[...] = -ADAM_LR * (m_hat / (jnp.sqrt(v_hat) + ADAM_EPS) + ADAM_WD * w)
    nm_ref[...] = m
    nv_ref[...] = v


def adamw(parts, w, m, v, *, name):
    r, c = w.shape
    tr = r if r <= 256 else 256
    blk = pl.BlockSpec((tr, c), lambda i: (i, 0))
    out = jax.ShapeDtypeStruct((r, c), F32)
    return pl.pallas_call(
        _adamw_body, name=name, grid=(r // tr,),
        in_specs=[pl.BlockSpec((N_DEV, tr, c), lambda i: (0, i, 0)), blk, blk, blk],
        out_specs=[blk, blk, blk, blk],
        out_shape=[out, out, out, out],
        compiler_params=_params(("arbitrary",)),
    )(parts, w, m, v)


MESH_ID = pl.DeviceIdType.MESH
HBM_SPEC = pl.BlockSpec(memory_space=pl.ANY)


def _my_place():
    return lax.axis_index("x"), lax.axis_index("y"), lax.axis_index("c")


def _flat(x, y, c):
    return 4 * x + 2 * y + c


def _all_gather_body(*refs, n):
    ins, outs = refs[:n], refs[n:2 * n]
    send_sems, recv_sems, local_sems = refs[2 * n:]
    x, y, c = _my_place()
    me, sibling = (x, y, c), (x, y, 1 - c)
    chips = [(1 - x, y), (x, 1 - y), (1 - x, 1 - y)]
    pending = []
    for a in range(n):
        src, out = ins[a], outs[a]

        def copy(k, block, to, from_input=False, a=a, src=src, out=out):
            slot = out.at[_flat(*block)]
            return pltpu.make_async_remote_copy(
                src_ref=src if from_input else slot, dst_ref=slot,
                send_sem=send_sems.at[7 * a + k], recv_sem=recv_sems.at[7 * a + k],
                device_id=to, device_id_type=MESH_ID)

        mine = pltpu.make_async_copy(src, out.at[_flat(*me)], local_sems.at[a])
        mine.start()
        first = [copy(0, me, sibling, True)] + [copy(1 + j, me, (*chip, c), True) for j, chip in enumerate(chips)]
        for cp in first:
            cp.start()
        pending.append((copy, mine, first))
    for copy, mine, first in pending:
        passed = [copy(4 + j, (*chip, c), sibling) for j, chip in enumerate(chips)]
        for j, chip in enumerate(chips):
            copy(1 + j, (*chip, c), me).wait_recv()
            passed[j].start()
        copy(0, sibling, me).wait_recv()
        for j, chip in enumerate(chips):
            copy(4 + j, (*chip, 1 - c), me).wait_recv()
        for cp in first + passed:
            cp.wait_send()
        mine.wait()


def all_gather(shards, *, name):
    n = len(shards)
    return pl.pallas_call(
        functools.partial(_all_gather_body, n=n), name=name,
        in_specs=[HBM_SPEC] * n, out_specs=[HBM_SPEC] * n,
        out_shape=[jax.ShapeDtypeStruct((N_DEV,) + s.shape, s.dtype) for s in shards],
        scratch_shapes=[pltpu.SemaphoreType.DMA((7 * n,)), pltpu.SemaphoreType.DMA((7 * n,)),
                        pltpu.SemaphoreType.DMA((n,))],
    )(*shards)


PEER_FLIPS = [(0, 0, 1), (1, 0, 0), (0, 1, 0), (1, 1, 0), (1, 0, 1), (0, 1, 1), (1, 1, 1)]


def _all_to_all_body(*refs, n):
    ins, outs = refs[:n], refs[n:2 * n]
    send_sems, recv_sems, local_sems = refs[2 * n:]
    x, y, c = _my_place()
    me = _flat(x, y, c)
    waits = []
    for a in range(n):
        src, out = ins[a], outs[a]
        mine = pltpu.make_async_copy(src.at[me], out.at[me], local_sems.at[a])
        mine.start()
        waits.append(mine)
        for k, (fx, fy, fc) in enumerate(PEER_FLIPS):
            peer = (1 - x if fx else x, 1 - y if fy else y, 1 - c if fc else c)
            theirs = _flat(*peer)
            sems = dict(send_sem=send_sems.at[7 * a + k], recv_sem=recv_sems.at[7 * a + k],
                        device_id=peer, device_id_type=MESH_ID)
            send = pltpu.make_async_remote_copy(src_ref=src.at[theirs], dst_ref=out.at[me], **sems)
            send.start()
            recv = pltpu.make_async_remote_copy(src_ref=src.at[theirs], dst_ref=out.at[theirs], **sems)
            waits.append((send, recv))
    for w in waits:
        if isinstance(w, tuple):
            w[0].wait_send()
            w[1].wait_recv()
        else:
            w.wait()


def all_to_all(parts, *, name):
    n = len(parts)
    return pl.pallas_call(
        functools.partial(_all_to_all_body, n=n), name=name,
        in_specs=[HBM_SPEC] * n, out_specs=[HBM_SPEC] * n,
        out_shape=[jax.ShapeDtypeStruct(p.shape, p.dtype) for p in parts],
        scratch_shapes=[pltpu.SemaphoreType.DMA((7 * n,)), pltpu.SemaphoreType.DMA((7 * n,)),
                        pltpu.SemaphoreType.DMA((n,))],
    )(*parts)


def _pad_rows(a, rows=8):
    return jnp.pad(a, ((0, rows - a.shape[0]), (0, 0)))


def _gate_rows(a_log, dt_bias):
    z = jnp.zeros((8, 128), F32)
    return z.at[0, A_HEADS:2 * A_HEADS].set(a_log[0]).at[1, A_HEADS:2 * A_HEADS].set(dt_bias[0])


def _pack_small(norm_g, a_log, a_dt_bias, a_norm_g, b_q_norm_g, b_k_norm_g):
    return jnp.concatenate([
        norm_g[0].reshape(8, 128), norm_g[1].reshape(8, 128),
        _gate_rows(a_log, a_dt_bias),
        _pad_rows(a_norm_g[0].reshape(2, 128)),
        _pad_rows(jnp.concatenate([b_q_norm_g[0], b_k_norm_g[0]], axis=0)),
    ], axis=0)


def _unpack_small(p):
    return (p[0:16].reshape(2, D_MODEL), p[16:17, A_HEADS:2 * A_HEADS], p[17:18, A_HEADS:2 * A_HEADS],
            p[24:26].reshape(1, A_DV), p[32:35][None], p[35:38][None])


def kernel(x, positions, norm_g, a_w_in, a_conv_w, a_log, a_dt_bias, a_norm_g, a_w_out, b_w_in, b_q_norm_g, b_k_norm_g, b_w_out, loss_target, m_norm_g, m_a_w_in, m_a_conv_w, m_a_log, m_a_dt_bias, m_a_norm_g, m_a_w_out, m_b_w_in, m_b_q_norm_g, m_b_k_norm_g, m_b_w_out, v_norm_g, v_a_w_in, v_a_conv_w, v_a_log, v_a_dt_bias, v_a_norm_g, v_a_w_out, v_b_w_in, v_b_q_norm_g, v_b_k_norm_g, v_b_w_out):
    n_seq, s_len, d = x.shape
    t_rows = n_seq * s_len
    n_chunks = s_len // CHUNK
    x0 = x.reshape(t_rows, d)
    target = loss_target.reshape(t_rows, d)
    my_slot = _flat(*_my_place())

    g_a_in, g_a_out, g_b_in, g_b_out, g_conv = all_gather(
        [a_w_in[0].astype(BF16), a_w_out[0].astype(BF16), b_w_in[0].astype(BF16), b_w_out[0].astype(BF16),
         _pad_rows(a_conv_w[0])], name="gather_weights")
    w_a_in = jnp.pad(g_a_in.transpose(1, 0, 2).reshape(d, A_IN), ((0, 0), (0, A_IN_PAD - A_IN)))
    w_a_out = g_a_out.reshape(A_VW, d)
    w_b_in = g_b_in.transpose(1, 0, 2).reshape(d, B_IN_COLS)
    w_b_out = g_b_out.reshape(B_W, d)
    conv_w8 = g_conv.transpose(1, 0, 2).reshape(8, 2 * A_QK + A_VW)

    gains_model = _pad_rows(norm_g)
    gate_prm = _gate_rows(a_log, a_dt_bias)
    gain_a_out = _pad_rows(a_norm_g)
    gains_qk = _pad_rows(jnp.concatenate([b_q_norm_g[0], b_k_norm_g[0]], axis=0))
    cos_t, sin_t = rope_tables(positions.reshape(t_rows))

    h0 = rms_fwd(x0, gains_model, 0)
    proj_a = mm_nn(h0, w_a_in, tn=896, name="proj_a")
    qkv = a_pre_fwd(proj_a, conv_w8, n_seq)
    gates_col, gates_row = a_gates_fwd(proj_a, gate_prm, n_seq)
    gates_row = gates_row.reshape(n_seq, 2 * A_HEADS, n_chunks, 1, CHUNK)
    o_a, tinv, states = gdn_fwd(qkv, gates_col, gates_row, n_seq)
    og_a = a_post_fwd(o_a, proj_a, gain_a_out)
    x1 = mm_nn(og_a, w_a_out, x0, tn=512, name="out_a")

    h1 = rms_fwd(x1, gains_model, 1)
    proj_b = mm_nn_pieces(h1, w_b_in, name="proj_b")
    og_b, o_b, lse = attn_fwd(proj_b, cos_t, sin_t, gains_qk, n_seq)
    y = mm_nn(og_b, w_b_out, x1, tn=512, name="out_b")

    dy, loss_blk = loss_head(y, target)
    loss = lax.psum(loss_blk[0, 0], ("x", "y", "c"))

    d_og_b = mm_nt(dy, w_b_out, tn=512, name="d_og_b")
    dw_b_out = mm_tn(og_b, dy, tn=256, out_dtype=BF16, name="dw_b_out")
    dqkv_b, dz_b, d_gains_qk = attn_bwd(proj_b, cos_t, sin_t, gains_qk, d_og_b, o_b, lse, n_seq)
    dh1 = mm_nt_pieces(dqkv_b, w_b_in, name="dh1_qkv")
    dh1 = mm_nt(dz_b, w_b_in, dh1, tn=B_W, col_off=3 * B_GROUPS, name="dh1_z")
    dw_b_in = mm_tn_b_in(h1, dqkv_b, dz_b, out_dtype=BF16, name="dw_b_in")
    dx1, d_gain1 = rms_bwd(x1, gains_model, 1, dh1, dy)

    d_og_a = mm_nt(dx1, w_a_out, tn=512, name="d_og_a")
    dw_a_out = mm_tn(og_a, dx1, tn=128, out_dtype=BF16, name="dw_a_out")
    do_a, dz_a, d_gain_a_out = a_post_bwd(o_a, proj_a, gain_a_out, d_og_a)
    dqkv_a, dgates = gdn_bwd(qkv, gates_col, gates_row, tinv, states, do_a, n_seq)
    d_pre, d_conv = a_pre_bwd(proj_a, conv_w8, dqkv_a, n_seq)
    d_gate_logits, d_gate_prm = a_gates_bwd(proj_a, gate_prm, dgates, n_seq)
    dh0 = mm_nt(d_pre, w_a_in, tn=512, name="dh0_qkv")
    dh0 = mm_nt(dz_a, w_a_in, dh0, tn=512, col_off=(2 * A_QK + A_VW) // 512, name="dh0_z")
    dh0 = mm_nt(d_gate_logits, w_a_in, dh0, tn=128, col_off=A_GATE_COL, name="dh0_gates")
    dw_a_in = jnp.concatenate([
        mm_tn(h0, d_pre, tn=256, out_dtype=BF16, name="dw_a_in_qkv"),
        mm_tn(h0, dz_a, tn=256, out_dtype=BF16, name="dw_a_in_z"),
        mm_tn(h0, d_gate_logits, tn=128, out_dtype=BF16, name="dw_a_in_gates"),
    ], axis=1)[:, :A_IN]
    dx0, d_gain0 = rms_bwd(x0, gains_model, 0, dh0, dx1)

    shard_a_in = A_IN // N_DEV
    r_a_in, r_a_out, r_b_in, r_b_out = all_to_all(
        [dw_a_in.reshape(d, N_DEV, shard_a_in).transpose(1, 0, 2),
         dw_a_out.reshape(N_DEV, A_VW // N_DEV, d),
         dw_b_in,
         dw_b_out.reshape(N_DEV, B_W // N_DEV, d)], name="scatter_grads")
    small = jnp.concatenate([
        d_gain0[0].reshape(8, 128), d_gain1[0].reshape(8, 128), d_gate_prm,
        _pad_rows(d_gain_a_out[0].reshape(2, 128)), d_gains_qk], axis=0)
    r_small, r_conv = all_gather([small, d_conv], name="gather_small_grads")
    conv_cols = a_conv_w.shape[2]
    r_conv = lax.dynamic_slice(r_conv, (0, 0, my_slot * conv_cols), (N_DEV, 8, conv_cols))

    upd = {}
    upd["a_w_in"] = adamw(r_a_in, a_w_in[0], m_a_w_in[0], v_a_w_in[0], name="adamw_a_w_in")
    upd["a_w_out"] = adamw(r_a_out, a_w_out[0], m_a_w_out[0], v_a_w_out[0], name="adamw_a_w_out")
    upd["b_w_in"] = adamw(r_b_in, b_w_in[0], m_b_w_in[0], v_b_w_in[0], name="adamw_b_w_in")
    upd["b_w_out"] = adamw(r_b_out, b_w_out[0], m_b_w_out[0], v_b_w_out[0], name="adamw_b_w_out")
    upd["a_conv_w"] = [t[:A_CONV] for t in adamw(
        r_conv, _pad_rows(a_conv_w[0]), _pad_rows(m_a_conv_w[0]), _pad_rows(v_a_conv_w[0]), name="adamw_a_conv_w")]
    small_upd = adamw(
        r_small,
        _pack_small(norm_g, a_log, a_dt_bias, a_norm_g, b_q_norm_g, b_k_norm_g),
        _pack_small(m_norm_g, m_a_log, m_a_dt_bias, m_a_norm_g, m_b_q_norm_g, m_b_k_norm_g),
        _pack_small(v_norm_g, v_a_log, v_a_dt_bias, v_a_norm_g, v_b_q_norm_g, v_b_k_norm_g),
        name="adamw_small")
    small_names = ("norm_g", "a_log", "a_dt_bias", "a_norm_g", "b_q_norm_g", "b_k_norm_g")
    unpacked = [_unpack_small(t) for t in small_upd]
    for i, nm in enumerate(small_names):
        upd[nm] = [u[i] for u in unpacked]

    order = ("norm_g", "a_w_in", "a_conv_w", "a_log", "a_dt_bias", "a_norm_g", "a_w_out",
             "b_w_in", "b_q_norm_g", "b_k_norm_g", "b_w_out")
    big = ("a_w_in", "a_conv_w", "a_w_out", "b_w_in", "b_w_out")
    outs = [loss, dx0.reshape(n_seq, s_len, d)]
    for kind in range(4):
        for nm in order:
            t = upd[nm][kind]
            outs.append(t[None] if nm in big else t)
    return tuple(outs)
```

```python
import functools
import math

import jax
import jax.numpy as jnp
from jax import lax
from jax.experimental import pallas as pl
from jax.experimental.pallas import tpu as pltpu

F32 = jnp.float32
BF16 = jnp.bfloat16

D_MODEL = 1024
EPS = 1e-6
N_DEV = 8

A_HEADS = 8
A_DK = 128
A_DV = 256
A_QK = A_HEADS * A_DK
A_VW = A_HEADS * A_DV
A_CONV = 4
CHUNK = 64
A_IN = 2 * A_QK + 2 * A_VW + 2 * A_HEADS
A_IN_PAD = 2 * A_QK + 2 * A_VW + 128
A_GATE_COL = (2 * A_QK + 2 * A_VW) // 128

B_DILATIONS = (1, 4, 16)
B_GROUPS = 3
B_HEADS = 8
B_DH = 128
B_W = B_HEADS * B_DH
B_BLOCK = 128
B_PIECES = 3 * B_GROUPS + 1
ROPE_THETA = 500000.0
ROPE_DIMS = B_DH // 4
ROPE_HALF = ROPE_DIMS // 2

ADAM_LR = 0.001
ADAM_B1 = 0.9
ADAM_B2 = 0.999
ADAM_EPS = 1e-08
ADAM_WD = 0.01
ADAM_STEP = 10

VMEM_LIMIT = 56 * 1024 * 1024


def _params(sem):
    return pltpu.CompilerParams(dimension_semantics=sem, vmem_limit_bytes=VMEM_LIMIT)


def _bf(x):
    return x.astype(BF16)


def _mm(a, b):
    return jnp.dot(_bf(a), _bf(b), preferred_element_type=F32)


def _mm_nt(a, b):
    return lax.dot_general(_bf(a), _bf(b), (((1,), (1,)), ((), ())), preferred_element_type=F32)


def _mm_tn(a, b):
    return lax.dot_general(_bf(a), _bf(b), (((0,), (0,)), ((), ())), preferred_element_type=F32)


def _split(x):
    hi = _bf(x)
    return hi, _bf(x - hi.astype(F32))


def _mm3(a, b):
    ah, al = _split(a)
    bh, bl = _split(b)
    d = functools.partial(jnp.dot, preferred_element_type=F32)
    return d(ah, bh) + (d(ah, bl) + d(al, bh))


def _colsum_as_col(z):
    zh, zl = _split(z)
    ones = jnp.ones((z.shape[0], 128), BF16)
    tn = functools.partial(lax.dot_general, dimension_numbers=(((0,), (0,)), ((), ())),
                           preferred_element_type=F32)
    return (tn(zh, ones) + tn(zl, ones))[:, 0:1]


def _sigmoid(x):
    return 1.0 / (1.0 + jnp.exp(-x))


INV_BASE = 8
GDN_GROUP = 4
SUPER = GDN_GROUP * CHUNK

A_K_COL = A_QK // A_DK
A_V_COL = 2 * A_QK // A_DV
A_HEAD_W = 2 * A_DK + A_DV


def _unit_lower_inverse(m, row, col):
    eye = (row == col).astype(F32)
    d = jnp.where(row // INV_BASE == col // INV_BASE, m, 0.0)
    x = eye - d
    p = _mm3(d, d)
    steps = int(math.log2(INV_BASE)) - 1
    for i in range(steps):
        x = x + _mm3(x, p)
        if i + 1 < steps:
            p = _mm3(p, p)
    size = INV_BASE
    while size < CHUNK:
        c = jnp.where((row // (2 * size) == col // (2 * size)) & (row // size != col // size), m, 0.0)
        x = x - _mm3(_mm3(x, c), x)
        size *= 2
    return x


def _diag_blocks_tall(x):
    return jnp.concatenate([x[i * CHUNK:(i + 1) * CHUNK, i * CHUNK:(i + 1) * CHUNK] for i in range(GDN_GROUP)], axis=0)


def _tall_to_block_diag(t, same):
    return jnp.where(same, jnp.concatenate([t] * GDN_GROUP, axis=1), 0.0)


def _block_sum(same, x):
    xh, xl = _split(jnp.broadcast_to(x, (SUPER, 128)))
    ones = same.astype(BF16)
    d = functools.partial(jnp.dot, preferred_element_type=F32)
    return (d(ones, xh) + d(ones, xl))[:, 0:1]


def _super_rows(i):
    return pl.ds(pl.multiple_of(i * SUPER, SUPER), SUPER)


def _chunk_rows(n):
    return pl.ds(pl.multiple_of(n * CHUNK, CHUNK), CHUNK)


def _gdn_super_common(q, k, v, gcb, gr, head, tinv_tall=None):
    lane = lax.broadcasted_iota(jnp.int32, (SUPER, 128), 1)
    row = lax.broadcasted_iota(jnp.int32, (SUPER, SUPER), 0)
    col = lax.broadcasted_iota(jnp.int32, (SUPER, SUPER), 1)
    same = row // CHUNK == col // CHUNK
    beta = jnp.sum(jnp.where(lane == head, gcb, 0.0), axis=1, keepdims=True)
    gc = jnp.sum(jnp.where(lane == A_HEADS + head, gcb, 0.0), axis=1, keepdims=True)
    g_last = jnp.sum(jnp.where(col == (row // CHUNK) * CHUNK + (CHUNK - 1), gr, 0.0), axis=1, keepdims=True)
    gamma = jnp.exp(gc)
    decay = jnp.where(same & (row >= col), jnp.exp(jnp.minimum(gc - gr, 0.0)), 0.0)
    kb = k * beta
    m = jnp.where(same & (row > col), _mm_nt(kb, k) * decay, 0.0)
    if tinv_tall is None:
        tinv = _unit_lower_inverse(m, row, col)
    else:
        tinv = _tall_to_block_diag(tinv_tall, same)
    u = _mm(tinv, v * beta)
    w = _mm(tinv, kb * gamma)
    p = jnp.where(same & (row >= col), _mm_nt(q, k) * decay, 0.0)
    e_tail = jnp.exp(g_last - gc)
    return dict(beta=beta, gc=gc, g_last=g_last, gamma=gamma, decay=decay, kb=kb, m=m,
                tinv=tinv, u=u, w=w, p=p, e_tail=e_tail, row=row, col=col, lane=lane, same=same)


def _store_scan_operands(rows, q, k, t, u_scr, w_scr, p_scr, qg_scr, ke_scr, gl_scr):
    u_scr[rows, :] = t["u"]
    w_scr[rows, :] = _bf(t["w"])
    p_scr[rows, :] = _bf(_diag_blocks_tall(t["p"]))
    qg_scr[rows, :] = _bf(q * t["gamma"])
    ke_scr[rows, :] = _bf(k * t["e_tail"])
    gl_scr[rows, :] = jnp.broadcast_to(jnp.exp(t["g_last"]), (SUPER, 128))


def _gdn_fwd_body(q_ref, k_ref, v_ref, gc_ref, gr_ref, o_ref, tinv_ref, st_ref,
                  s_scr, u_scr, w_scr, p_scr, qg_scr, ke_scr, gl_scr):
    head = pl.program_id(1)
    n_chunks = q_ref.shape[0] // CHUNK

    def prepare(i, carry):
        rows = _super_rows(i)
        q, k = q_ref[rows, :], k_ref[rows, :]
        t = _gdn_super_common(q, k, v_ref[rows, :], gc_ref[rows, :], gr_ref[i], head)
        tinv_ref[rows, :] = _diag_blocks_tall(t["tinv"])
        _store_scan_operands(rows, q, k, t, u_scr, w_scr, p_scr, qg_scr, ke_scr, gl_scr)
        return carry

    lax.fori_loop(0, n_chunks // GDN_GROUP, prepare, 0)
    s_scr[...] = jnp.zeros_like(s_scr)

    def scan(n, carry):
        rows = _chunk_rows(n)
        s = s_scr[...]
        st_ref[n] = s
        sb = _bf(s)
        v_new = u_scr[rows, :] - jnp.dot(w_scr[rows, :], sb, preferred_element_type=F32)
        vb = _bf(v_new)
        o_ref[rows, :] = (jnp.dot(qg_scr[rows, :], sb, preferred_element_type=F32)
                          + jnp.dot(p_scr[rows, :], vb, preferred_element_type=F32))
        s_scr[...] = s * gl_scr[rows, :][0:1, 0:1] + lax.dot_general(
            ke_scr[rows, :], vb, (((0,), (0,)), ((), ())), preferred_element_type=F32)
        return carry

    lax.fori_loop(0, n_chunks, scan, 0)


def _gdn_in_specs(s_len, n_super):
    return [
        pl.BlockSpec((s_len, A_DK), lambda b, h: (b, h)),
        pl.BlockSpec((s_len, A_DK), lambda b, h: (b, A_K_COL + h)),
        pl.BlockSpec((s_len, A_DV), lambda b, h: (b, A_V_COL + h)),
        pl.BlockSpec((s_len, 128), lambda b, h: (b, 0)),
        pl.BlockSpec((None, None, n_super, 1, SUPER), lambda b, h: (b, A_HEADS + h, 0, 0, 0)),
    ]


def _gdn_scan_scratch(s_len):
    return [pltpu.VMEM((A_DK, A_DV), F32), pltpu.VMEM((s_len, A_DV), F32),
            pltpu.VMEM((s_len, A_DK), BF16), pltpu.VMEM((s_len, CHUNK), BF16),
            pltpu.VMEM((s_len, A_DK), BF16), pltpu.VMEM((s_len, A_DK), BF16),
            pltpu.VMEM((s_len, 128), F32)]


def gdn_fwd(qkv, gates_col, gates_row, n_seq):
    t_rows = qkv.shape[0]
    s_len = t_rows // n_seq
    n_chunks = s_len // CHUNK
    return pl.pallas_call(
        _gdn_fwd_body, name="gdn_fwd", grid=(n_seq, A_HEADS),
        in_specs=_gdn_in_specs(s_len, s_len // SUPER),
        out_specs=[
            pl.BlockSpec((s_len, A_DV), lambda b, h: (b, h)),
            pl.BlockSpec((s_len, CHUNK), lambda b, h: (b * A_HEADS + h, 0)),
            pl.BlockSpec((None, n_chunks, A_DK, A_DV), lambda b, h: (b * A_HEADS + h, 0, 0, 0)),
        ],
        out_shape=[
            jax.ShapeDtypeStruct((t_rows, A_VW), F32),
            jax.ShapeDtypeStruct((n_seq * A_HEADS * s_len, CHUNK), F32),
            jax.ShapeDtypeStruct((n_seq * A_HEADS, n_chunks, A_DK, A_DV), F32),
        ],
        scratch_shapes=_gdn_scan_scratch(s_len),
        compiler_params=_params(("arbitrary", "arbitrary")),
    )(qkv, qkv, qkv, gates_col, gates_row)


def _gdn_bwd_body(q_ref, k_ref, v_ref, gc_ref, gr_ref, tinv_ref, st_ref, do_ref,
                  dqkv_ref, dgc_ref,
                  ds_scr, u_scr, w_scr, p_scr, qg_scr, ke_scr, gl_scr,
                  vn_scr, dvn_scr, dqg_scr, dw_scr, dkt_scr, sds_scr):
    head = pl.program_id(1)
    n_chunks = q_ref.shape[0] // CHUNK

    @pl.when(head == 0)
    def _():
        dgc_ref[...] = jnp.zeros_like(dgc_ref)

    def common(i):
        rows = _super_rows(i)
        q, k, v = q_ref[rows, :], k_ref[rows, :], v_ref[rows, :]
        return rows, q, k, v, _gdn_super_common(q, k, v, gc_ref[rows, :], gr_ref[i], head,
                                                tinv_tall=tinv_ref[rows, :])

    def prepare(i, carry):
        rows, q, k, _, t = common(i)
        _store_scan_operands(rows, q, k, t, u_scr, w_scr, p_scr, qg_scr, ke_scr, gl_scr)
        return carry

    lax.fori_loop(0, n_chunks // GDN_GROUP, prepare, 0)
    ds_scr[...] = jnp.zeros_like(ds_scr)
    tn = functools.partial(lax.dot_general, dimension_numbers=(((0,), (0,)), ((), ())), preferred_element_type=F32)
    nt = functools.partial(lax.dot_general, dimension_numbers=(((1,), (1,)), ((), ())), preferred_element_type=F32)

    def scan(i, carry):
        n = n_chunks - 1 - i
        rows = _chunk_rows(n)
        ds_next = ds_scr[...]
        dsb = _bf(ds_next)
        s = st_ref[n]
        sb = _bf(s)
        d_ob = _bf(do_ref[rows, :])
        v_new = u_scr[rows, :] - jnp.dot(w_scr[rows, :], sb, preferred_element_type=F32)
        d_vn = tn(p_scr[rows, :], d_ob) + jnp.dot(ke_scr[rows, :], dsb, preferred_element_type=F32)
        d_vnb = _bf(d_vn)
        vn_scr[rows, :] = v_new
        dvn_scr[rows, :] = d_vn
        dqg_scr[rows, :] = nt(d_ob, sb)
        dw_scr[rows, :] = -nt(d_vnb, sb)
        dkt_scr[rows, :] = nt(_bf(v_new), dsb)
        sds = jnp.sum(jnp.sum(s * ds_next, axis=1, keepdims=True), axis=0, keepdims=True)
        sds_scr[rows, :] = jnp.broadcast_to(sds, (CHUNK, 128))
        ds_scr[...] = (tn(qg_scr[rows, :], d_ob) + gl_scr[rows, :][0:1, 0:1] * ds_next
                       - tn(w_scr[rows, :], d_vnb))
        return carry

    lax.fori_loop(0, n_chunks, scan, 0)

    def finish(i, carry):
        rows, q, k, v, t = common(i)
        beta, gamma, decay, kb, e_tail = t["beta"], t["gamma"], t["decay"], t["kb"], t["e_tail"]
        row, col, lane, same = t["row"], t["col"], t["lane"], t["same"]
        d_o = do_ref[rows, :]
        v_new, d_vn = vn_scr[rows, :], dvn_scr[rows, :]
        d_qg, d_w, d_kt = dqg_scr[rows, :], dw_scr[rows, :], dkt_scr[rows, :]
        gamma_last = jnp.exp(t["g_last"])

        d_p = jnp.where(same & (row >= col), _mm_nt(d_o, v_new), 0.0)
        d_ru = _mm_tn(t["tinv"], d_vn)
        d_rw = _mm_tn(t["tinv"], d_w)
        d_m = jnp.where(same & (row > col), -(_mm_nt(d_ru, t["u"]) + _mm_nt(d_rw, t["w"])), 0.0)

        x_p = d_p * decay
        y_m = d_m * decay
        d_kb = _mm(y_m, k) + d_rw * gamma
        dqkv_ref[rows, 0:A_DK] = _mm(x_p, k) + d_qg * gamma
        dqkv_ref[rows, A_DK:2 * A_DK] = _mm_tn(x_p, q) + _mm_tn(y_m, kb) + d_kb * beta + d_kt * e_tail
        dqkv_ref[rows, 2 * A_DK:A_HEAD_W] = d_ru * beta

        d_beta = (jnp.sum(d_ru * v, axis=1, keepdims=True)
                  + jnp.sum(d_kb * k, axis=1, keepdims=True))
        z = d_p * t["p"] + d_m * t["m"]
        eps_tail = jnp.sum(d_kt * k, axis=1, keepdims=True) * e_tail
        d_gc = (jnp.sum(z, axis=1, keepdims=True) - _colsum_as_col(z)
                + jnp.sum(d_qg * q, axis=1, keepdims=True) * gamma
                + jnp.sum(d_rw * kb, axis=1, keepdims=True) * gamma
                - eps_tail)
        d_glast = _block_sum(same, eps_tail) + gamma_last * sds_scr[rows, :][:, 0:1]
        rcol = lax.broadcasted_iota(jnp.int32, (SUPER, 1), 0)
        d_gc = d_gc + jnp.where(rcol % CHUNK == CHUNK - 1, d_glast, 0.0)
        dgc_ref[rows, :] += (jnp.where(lane == head, d_beta, 0.0)
                             + jnp.where(lane == A_HEADS + head, d_gc, 0.0))
        return carry

    lax.fori_loop(0, n_chunks // GDN_GROUP, finish, 0)


def gdn_bwd(qkv, gates_col, gates_row, tinv, states, d_o, n_seq):
    t_rows = qkv.shape[0]
    s_len = t_rows // n_seq
    n_chunks = s_len // CHUNK
    v_spec = pl.BlockSpec((s_len, A_DV), lambda b, h: (b, h))
    gate_spec = pl.BlockSpec((s_len, 128), lambda b, h: (b, 0))
    return pl.pallas_call(
        _gdn_bwd_body, name="gdn_bwd", grid=(n_seq, A_HEADS),
        in_specs=_gdn_in_specs(s_len, s_len // SUPER) + [
            pl.BlockSpec((s_len, CHUNK), lambda b, h: (b * A_HEADS + h, 0)),
            pl.BlockSpec((None, n_chunks, A_DK, A_DV), lambda b, h: (b * A_HEADS + h, 0, 0, 0)),
            v_spec,
        ],
        out_specs=[pl.BlockSpec((s_len, A_HEAD_W), lambda b, h: (b, h)), gate_spec],
        out_shape=[
            jax.ShapeDtypeStruct((t_rows, A_HEADS * A_HEAD_W), F32),
            jax.ShapeDtypeStruct((t_rows, 128), F32),
        ],
        scratch_shapes=_gdn_scan_scratch(s_len) + [
            pltpu.VMEM((s_len, A_DV), F32), pltpu.VMEM((s_len, A_DV), F32),
            pltpu.VMEM((s_len, A_DK), F32), pltpu.VMEM((s_len, A_DK), F32), pltpu.VMEM((s_len, A_DK), F32),
            pltpu.VMEM((s_len, 128), F32)],
        compiler_params=_params(("arbitrary", "arbitrary")),
    )(qkv, qkv, qkv, gates_col, gates_row, tinv, states, d_o)


def _shift_down(x, j, rows):
    return jnp.where(rows >= j, pltpu.roll(x, j, 0), 0.0)


def _shift_up(x, j, rows):
    n = x.shape[0]
    return jnp.where(rows < n - j, pltpu.roll(x, n - j, 0), 0.0)


def _conv_silu_norm(x, w, cb):
    rows = lax.broadcasted_iota(jnp.int32, (x.shape[0], 1), 0)
    c = x * w[A_CONV - 1:A_CONV, :]
    for j in range(1, A_CONV):
        c = c + _shift_down(x, j, rows) * w[A_CONV - 1 - j:A_CONV - j, :]
    sig = _sigmoid(c)
    a = c * sig
    rn = lax.rsqrt(jnp.sum(a * a, axis=1, keepdims=True) + EPS)
    is_q = cb < A_QK // 128
    is_qk = cb < 2 * A_QK // 128
    q_scale = jnp.where(is_q, A_DK ** -0.5, 1.0).astype(F32)
    return rows, c, sig, a, rn, is_qk, q_scale


def _a_pre_fwd_body(x_ref, w_ref, o_ref):
    cb = pl.program_id(1)
    _, _, _, a, rn, is_qk, q_scale = _conv_silu_norm(x_ref[...], w_ref[...], cb)
    o_ref[...] = a * jnp.where(is_qk, rn * q_scale, 1.0)


def a_pre_fwd(proj_a, conv_w8, n_seq):
    t_rows = proj_a.shape[0]
    s_len = t_rows // n_seq
    n_cb = (2 * A_QK + A_VW) // 128
    blk = pl.BlockSpec((s_len, 128), lambda b, c: (b, c))
    return pl.pallas_call(
        _a_pre_fwd_body, name="a_pre_fwd", grid=(n_seq, n_cb),
        in_specs=[blk, pl.BlockSpec((8, 128), lambda b, c: (0, c))],
        out_specs=blk,
        out_shape=jax.ShapeDtypeStruct((t_rows, n_cb * 128), F32),
        compiler_params=_params(("arbitrary", "arbitrary")),
    )(proj_a, conv_w8)


def _a_pre_bwd_body(x_ref, w_ref, dy_ref, dx_ref, dw_ref):
    b, cb = pl.program_id(1), pl.program_id(0)
    x, w = x_ref[...], w_ref[...]
    rows, c, sig, a, rn, is_qk, q_scale = _conv_silu_norm(x, w, cb)
    dy = dy_ref[...]
    da_n = q_scale * (rn * dy - a * (rn * rn * rn) * jnp.sum(dy * a, axis=1, keepdims=True))
    da = jnp.where(is_qk, da_n, dy)
    dc = da * (sig * (1.0 + c * (1.0 - sig)))
    dx = dc * w[A_CONV - 1:A_CONV, :]
    for j in range(1, A_CONV):
        dx = dx + _shift_up(dc, j, rows) * w[A_CONV - 1 - j:A_CONV - j, :]
    dx_ref[...] = dx

    @pl.when(b == 0)
    def _():
        dw_ref[...] = jnp.zeros_like(dw_ref)

    for j in range(A_CONV):
        xs = x if j == 0 else _shift_down(x, j, rows)
        dw_ref[A_CONV - 1 - j:A_CONV - j, :] += jnp.sum(dc * xs, axis=0, keepdims=True)


def a_pre_bwd(proj_a, conv_w8, dqkv_hm, n_seq):
    t_rows = proj_a.shape[0]
    s_len = t_rows // n_seq
    n_cb = (2 * A_QK + A_VW) // 128
    blk = pl.BlockSpec((s_len, 128), lambda c, b: (b, c))
    wblk = pl.BlockSpec((8, 128), lambda c, b: (0, c))
    per_head = A_HEAD_W // 128
    n_q = A_QK // 128

    def head_major(c, b):
        v_blk = jnp.maximum(c - 2 * n_q, 0)
        col = jnp.where(c < n_q, c * per_head,
                        jnp.where(c < 2 * n_q, (c - n_q) * per_head + 1,
                                  (v_blk // 2) * per_head + 2 + v_blk % 2))
        return (b, col)

    return pl.pallas_call(
        _a_pre_bwd_body, name="a_pre_bwd", grid=(n_cb, n_seq),
        in_specs=[blk, wblk, pl.BlockSpec((s_len, 128), head_major)],
        out_specs=[blk, wblk],
        out_shape=[jax.ShapeDtypeStruct((t_rows, n_cb * 128), F32),
                   jax.ShapeDtypeStruct((8, n_cb * 128), F32)],
        compiler_params=_params(("arbitrary", "arbitrary")),
    )(proj_a, conv_w8, dqkv_hm)


GATE_TILE = 512


def _softplus(y):
    return jnp.maximum(y, 0.0) + jnp.log1p(jnp.exp(-jnp.abs(y)))


def _gate_values(x, prm):
    beta = _sigmoid(x)
    y = x + prm[1:2, :]
    neg_a = -jnp.exp(prm[0:1, :])
    g = neg_a * _softplus(y)
    return beta, y, neg_a, g


def _a_gates_fwd_body(x_ref, prm_ref, gc_ref, gr_ref):
    x = x_ref[...]
    tm = x.shape[0]
    beta, _, _, g = _gate_values(x, prm_ref[...])
    in_chunk = lax.broadcasted_iota(jnp.int32, (tm, 1), 0) % CHUNK
    s = 1
    while s < CHUNK:
        g = g + jnp.where(in_chunk >= s, pltpu.roll(g, s, 0), 0.0)
        s *= 2
    lane = lax.broadcasted_iota(jnp.int32, x.shape, 1)
    out = jnp.where(lane < A_HEADS, beta, jnp.where(lane < 2 * A_HEADS, g, 0.0))
    gc_ref[...] = out
    gr_ref[...] = out.T[0:2 * A_HEADS, :]


def a_gates_fwd(proj_a, prm, n_seq):
    t_rows = proj_a.shape[0]
    s_len = t_rows // n_seq
    tm = min(GATE_TILE, s_len)
    n_t = s_len // tm
    return pl.pallas_call(
        _a_gates_fwd_body, name="a_gates_fwd", grid=(n_seq, n_t),
        in_specs=[pl.BlockSpec((tm, 128), lambda b, i: (b * n_t + i, A_GATE_COL)),
                  pl.BlockSpec((8, 128), lambda b, i: (0, 0))],
        out_specs=[pl.BlockSpec((tm, 128), lambda b, i: (b * n_t + i, 0)),
                   pl.BlockSpec((None, 2 * A_HEADS, tm), lambda b, i: (b, 0, i))],
        out_shape=[jax.ShapeDtypeStruct((t_rows, 128), F32),
                   jax.ShapeDtypeStruct((n_seq, 2 * A_HEADS, s_len), F32)],
        compiler_params=_params(("arbitrary", "arbitrary")),
    )(proj_a, prm)


def _a_gates_bwd_body(x_ref, prm_ref, dgc_ref, dx_ref, dprm_ref):
    first = (pl.program_id(0) == 0) & (pl.program_id(1) == 0)
    x = x_ref[...]
    tm = x.shape[0]
    beta, y, neg_a, g = _gate_values(x, prm_ref[...])
    d = dgc_ref[...]
    in_chunk = lax.broadcasted_iota(jnp.int32, (tm, 1), 0) % CHUNK
    dg = d
    s = 1
    while s < CHUNK:
        dg = dg + jnp.where(in_chunk < CHUNK - s, pltpu.roll(dg, tm - s, 0), 0.0)
        s *= 2
    lane = lax.broadcasted_iota(jnp.int32, x.shape, 1)
    is_decay = (lane >= A_HEADS) & (lane < 2 * A_HEADS)
    d_alogit = jnp.where(is_decay, dg * neg_a * _sigmoid(y), 0.0)
    dx_ref[...] = jnp.where(lane < A_HEADS, d * beta * (1.0 - beta), d_alogit)

    @pl.when(first)
    def _():
        dprm_ref[...] = jnp.zeros_like(dprm_ref)

    dprm_ref[0:1, :] += jnp.sum(jnp.where(is_decay, dg * g, 0.0), axis=0, keepdims=True)
    dprm_ref[1:2, :] += jnp.sum(d_alogit, axis=0, keepdims=True)


def a_gates_bwd(proj_a, prm, dgates_col, n_seq):
    t_rows = proj_a.shape[0]
    s_len = t_rows // n_seq
    tm = min(GATE_TILE, s_len)
    n_t = s_len // tm
    return pl.pallas_call(
        _a_gates_bwd_body, name="a_gates_bwd", grid=(n_seq, n_t),
        in_specs=[pl.BlockSpec((tm, 128), lambda b, i: (b * n_t + i, A_GATE_COL)),
                  pl.BlockSpec((8, 128), lambda b, i: (0, 0)),
                  pl.BlockSpec((tm, 128), lambda b, i: (b * n_t + i, 0))],
        out_specs=[pl.BlockSpec((tm, 128), lambda b, i: (b * n_t + i, 0)),
                   pl.BlockSpec((8, 128), lambda b, i: (0, 0))],
        out_shape=[jax.ShapeDtypeStruct((t_rows, 128), F32),
                   jax.ShapeDtypeStruct((8, 128), F32)],
        compiler_params=_params(("arbitrary", "arbitrary")),
    )(proj_a, prm, dgates_col)


ROW_TILE = 512
A_Z_COL = (2 * A_QK + A_VW) // A_DV


def _silu_parts(z):
    sig = _sigmoid(z)
    return z * sig, sig * (1.0 + z * (1.0 - sig))


def _a_post_fwd_body(o_ref, z_ref, g_ref, og_ref):
    o = o_ref[...]
    r = lax.rsqrt(jnp.mean(o * o, axis=1, keepdims=True) + EPS)
    silu, _ = _silu_parts(z_ref[...])
    og_ref[...] = ((o * r * g_ref[0:1, :]) * silu).astype(og_ref.dtype)


def a_post_fwd(o, proj_a, norm_g8):
    t_rows = o.shape[0]
    tm = min(ROW_TILE, t_rows)
    blk = pl.BlockSpec((tm, A_DV), lambda i, h: (i, h))
    return pl.pallas_call(
        _a_post_fwd_body, name="a_post_fwd", grid=(t_rows // tm, A_HEADS),
        in_specs=[blk, pl.BlockSpec((tm, A_DV), lambda i, h: (i, A_Z_COL + h)),
                  pl.BlockSpec((8, A_DV), lambda i, h: (0, 0))],
        out_specs=blk,
        out_shape=jax.ShapeDtypeStruct((t_rows, A_VW), BF16),
        compiler_params=_params(("arbitrary", "arbitrary")),
    )(o, proj_a, norm_g8)


def _a_post_bwd_body(o_ref, z_ref, g_ref, dog_ref, do_ref, dz_ref, dg_ref):
    first = (pl.program_id(0) == 0) & (pl.program_id(1) == 0)
    o, z, d_og = o_ref[...], z_ref[...], dog_ref[...]
    gain = g_ref[0:1, :]
    r = lax.rsqrt(jnp.mean(o * o, axis=1, keepdims=True) + EPS)
    silu, dsilu = _silu_parts(z)
    xr = o * r
    d_on = d_og * silu
    dz_ref[...] = d_og * (xr * gain) * dsilu
    u = d_on * gain
    do_ref[...] = r * u - xr * (r * r) * jnp.mean(u * o, axis=1, keepdims=True)

    @pl.when(first)
    def _():
        dg_ref[...] = jnp.zeros_like(dg_ref)

    dg_ref[0:1, :] += jnp.sum(d_on * xr, axis=0, keepdims=True)


def a_post_bwd(o, proj_a, norm_g8, d_og):
    t_rows = o.shape[0]
    tm = min(ROW_TILE, t_rows)
    blk = pl.BlockSpec((tm, A_DV), lambda i, h: (i, h))
    gblk = pl.BlockSpec((8, A_DV), lambda i, h: (0, 0))
    return pl.pallas_call(
        _a_post_bwd_body, name="a_post_bwd", grid=(t_rows // tm, A_HEADS),
        in_specs=[blk, pl.BlockSpec((tm, A_DV), lambda i, h: (i, A_Z_COL + h)), gblk, blk],
        out_specs=[blk, blk, gblk],
        out_shape=[jax.ShapeDtypeStruct((t_rows, A_VW), F32),
                   jax.ShapeDtypeStruct((t_rows, A_VW), F32),
                   jax.ShapeDtypeStruct((8, A_DV), F32)],
        compiler_params=_params(("arbitrary", "arbitrary")),
    )(o, proj_a, norm_g8, d_og)


NEG_BIG = -1e30
ATT_SCALE = B_DH ** -0.5


def _swap_rope_halves(x):
    lane = lax.broadcasted_iota(jnp.int32, x.shape, 1)
    return jnp.where(lane < ROPE_HALF, pltpu.roll(x, B_DH - ROPE_HALF, 1),
                     jnp.where(lane < ROPE_DIMS, pltpu.roll(x, ROPE_HALF, 1), 0.0))


def _norm_rope(x, gain, cos_t, sin_t):
    r = lax.rsqrt(jnp.mean(x * x, axis=1, keepdims=True) + EPS)
    xn = x * r * gain
    return xn * cos_t + _swap_rope_halves(xn) * sin_t, r


def _norm_rope_bwd(x, r, gain, cos_t, sin_t, dy):
    d_xn = dy * cos_t + _swap_rope_halves(dy * sin_t)
    xr = x * r
    u = d_xn * gain
    dx = r * u - xr * (r * r) * jnp.mean(u * x, axis=1, keepdims=True)
    return dx, jnp.sum(d_xn * xr, axis=0, keepdims=True)


def _stream_rows(idx, dilation, s_len):
    nb = s_len // dilation // B_BLOCK
    r = idx // nb
    m = idx % nb
    cur = r + m * (B_BLOCK * dilation)
    prev = r + jnp.maximum(m - 1, 0) * (B_BLOCK * dilation)
    return cur, prev, m > 0


def _rows(start, dilation):
    if dilation == 1:
        return pl.ds(start, B_BLOCK)
    return pl.ds(start, B_BLOCK, stride=dilation)


def _block_scores(qb, kc, kp, has_prev):
    qi = lax.broadcasted_iota(jnp.int32, (B_BLOCK, B_BLOCK), 0)
    kj = lax.broadcasted_iota(jnp.int32, (B_BLOCK, B_BLOCK), 1)
    s_c = jnp.where(qi >= kj, _mm_nt(qb, kc) * ATT_SCALE, NEG_BIG)
    s_p = jnp.where((kj >= qi) & has_prev, _mm_nt(qb, kp) * ATT_SCALE, NEG_BIG)
    return s_c, s_p


def _attn_fwd_body(qkv_ref, z_ref, cos_ref, sin_ref, gain_ref, og_ref, o_ref, lse_ref,
                   qn_scr, kn_scr, og_scr, lg_scr):
    head, grp = pl.program_id(1), pl.program_id(2)
    s_len = z_ref.shape[0]
    n_blocks = s_len // B_BLOCK
    cos_t, sin_t = cos_ref[...], sin_ref[...]

    for gi, dil in enumerate(B_DILATIONS):
        @pl.when(grp == gi)
        def _(gi=gi, dil=dil):
            qn_scr[...], _ = _norm_rope(qkv_ref[0], gain_ref[gi:gi + 1, :], cos_t, sin_t)
            kn_scr[...], _ = _norm_rope(qkv_ref[1], gain_ref[B_GROUPS + gi:B_GROUPS + gi + 1, :], cos_t, sin_t)

            def block(idx, carry):
                cur, prev, has_prev = _stream_rows(idx, dil, s_len)
                rc, rp = _rows(cur, dil), _rows(prev, dil)
                s_c, s_p = _block_scores(qn_scr[rc, :], kn_scr[rc, :], kn_scr[rp, :], has_prev)
                mx = jnp.maximum(jnp.max(s_c, axis=1, keepdims=True), jnp.max(s_p, axis=1, keepdims=True))
                p_c, p_p = jnp.exp(s_c - mx), jnp.exp(s_p - mx)
                den = jnp.sum(p_c, axis=1, keepdims=True) + jnp.sum(p_p, axis=1, keepdims=True)
                acc = _mm(p_c, qkv_ref.at[2][rc, :]) + _mm(p_p, qkv_ref.at[2][rp, :])
                og_scr.at[gi][rc, :] = acc / den
                lg_scr.at[gi][rc, :] = jnp.broadcast_to(mx + jnp.log(den), (B_BLOCK, B_DH))
                return carry

            lax.fori_loop(0, n_blocks, block, 0)

    @pl.when(grp == B_GROUPS - 1)
    def _():
        l0, l1, l2 = lg_scr[0], lg_scr[1], lg_scr[2]
        mx = jnp.maximum(jnp.maximum(l0, l1), l2)
        w0, w1, w2 = jnp.exp(l0 - mx), jnp.exp(l1 - mx), jnp.exp(l2 - mx)
        den = w0 + w1 + w2
        o = (w0 * og_scr[0] + w1 * og_scr[1] + w2 * og_scr[2]) / den
        silu, _ = _silu_parts(z_ref[...])
        o_ref[...] = o
        og_ref[...] = (o * silu).astype(og_ref.dtype)
        @pl.when(head == 0)
        def _():
            lse_ref[...] = jnp.zeros_like(lse_ref)

        lane = lax.broadcasted_iota(jnp.int32, o.shape, 1)
        lse_ref[...] = jnp.where(lane == head, mx + jnp.log(den), lse_ref[...])


def attn_fwd(proj_b, cos_t, sin_t, gains8, n_seq):
    t_rows = proj_b.shape[1]
    s_len = t_rows // n_seq
    head_blk = pl.BlockSpec((s_len, B_DH), lambda b, h, g: (b, h))
    seq_blk = pl.BlockSpec((s_len, 128), lambda b, h, g: (b, 0))
    return pl.pallas_call(
        _attn_fwd_body, name="attn_fwd", grid=(n_seq, B_HEADS, B_GROUPS),
        in_specs=[
            pl.BlockSpec((3, s_len, B_DH), lambda b, h, g: (g, b, h)),
            pl.BlockSpec((None, s_len, B_DH), lambda b, h, g: (B_PIECES - 1, b, h)),
            seq_blk, seq_blk,
            pl.BlockSpec((8, 128), lambda b, h, g: (0, 0)),
        ],
        out_specs=[head_blk, head_blk, seq_blk],
        out_shape=[jax.ShapeDtypeStruct((t_rows, B_W), BF16),
                   jax.ShapeDtypeStruct((t_rows, B_W), F32),
                   jax.ShapeDtypeStruct((t_rows, 128), F32)],
        scratch_shapes=[pltpu.VMEM((s_len, B_DH), F32), pltpu.VMEM((s_len, B_DH), F32),
                        pltpu.VMEM((B_GROUPS, s_len, B_DH), F32), pltpu.VMEM((B_GROUPS, s_len, B_DH), F32)],
        compiler_params=_params(("arbitrary", "arbitrary", "arbitrary")),
    )(proj_b, proj_b, cos_t, sin_t, gains8)


def _attn_bwd_body(qkv_ref, z_ref, cos_ref, sin_ref, gain_ref, dog_ref, o_ref, lse_ref,
                   dqkv_ref, dz_ref, dgain_ref,
                   qn_scr, kn_scr, dqn_scr, dkn_scr, do_scr, dl_scr, ls_scr):
    head, grp = pl.program_id(1), pl.program_id(2)
    first = (pl.program_id(0) == 0) & (head == 0) & (grp == 0)
    s_len = z_ref.shape[0]
    n_blocks = s_len // B_BLOCK
    cos_t, sin_t = cos_ref[...], sin_ref[...]

    @pl.when(first)
    def _():
        dgain_ref[...] = jnp.zeros_like(dgain_ref)

    @pl.when(grp == 0)
    def _():
        d_og, o = dog_ref[...], o_ref[...]
        silu, dsilu = _silu_parts(z_ref[...])
        d_o = d_og * silu
        dz_ref[...] = d_og * o * dsilu
        do_scr[...] = d_o
        dl_scr[...] = jnp.broadcast_to(jnp.sum(d_o * o, axis=1, keepdims=True), o.shape)
        lane = lax.broadcasted_iota(jnp.int32, o.shape, 1)
        ls_scr[...] = jnp.broadcast_to(
            jnp.sum(jnp.where(lane == head, lse_ref[...], 0.0), axis=1, keepdims=True), o.shape)

    for gi, dil in enumerate(B_DILATIONS):
        @pl.when(grp == gi)
        def _(gi=gi, dil=dil):
            q_raw, k_raw = qkv_ref[0], qkv_ref[1]
            gq = gain_ref[gi:gi + 1, :]
            gk = gain_ref[B_GROUPS + gi:B_GROUPS + gi + 1, :]
            qn_scr[...], rq = _norm_rope(q_raw, gq, cos_t, sin_t)
            kn_scr[...], rk = _norm_rope(k_raw, gk, cos_t, sin_t)
            dkn_scr[...] = jnp.zeros_like(dkn_scr)
            dqkv_ref[2] = jnp.zeros((s_len, B_DH), F32)

            def block(idx, carry):
                cur, prev, has_prev = _stream_rows(idx, dil, s_len)
                rc, rp = _rows(cur, dil), _rows(prev, dil)
                qb, kc, kp = qn_scr[rc, :], kn_scr[rc, :], kn_scr[rp, :]
                vc, vp = qkv_ref.at[2][rc, :], qkv_ref.at[2][rp, :]
                d_o = do_scr[rc, :]
                lse = ls_scr[rc, :][:, 0:1]
                delta = dl_scr[rc, :][:, 0:1]
                s_c, s_p = _block_scores(qb, kc, kp, has_prev)
                p_c, p_p = jnp.exp(s_c - lse), jnp.exp(s_p - lse)
                ds_c = p_c * (_mm_nt(d_o, vc) - delta)
                ds_p = p_p * (_mm_nt(d_o, vp) - delta)
                dqn_scr[rc, :] = (_mm(ds_c, kc) + _mm(ds_p, kp)) * ATT_SCALE
                dkn_scr[rc, :] += _mm_tn(ds_c, qb) * ATT_SCALE
                dqkv_ref.at[2][rc, :] += _mm_tn(p_c, d_o)

                @pl.when(has_prev)
                def _():
                    dkn_scr[rp, :] += _mm_tn(ds_p, qb) * ATT_SCALE
                    dqkv_ref.at[2][rp, :] += _mm_tn(p_p, d_o)

                return carry

            lax.fori_loop(0, n_blocks, block, 0)
            dq, dgq = _norm_rope_bwd(q_raw, rq, gq, cos_t, sin_t, dqn_scr[...])
            dk, dgk = _norm_rope_bwd(k_raw, rk, gk, cos_t, sin_t, dkn_scr[...])
            dqkv_ref[0] = dq
            dqkv_ref[1] = dk
            dgain_ref[gi:gi + 1, :] += dgq
            dgain_ref[B_GROUPS + gi:B_GROUPS + gi + 1, :] += dgk


def attn_bwd(proj_b, cos_t, sin_t, gains8, d_og, o, lse, n_seq):
    t_rows = proj_b.shape[1]
    s_len = t_rows // n_seq
    head_blk = pl.BlockSpec((s_len, B_DH), lambda b, h, g: (b, h))
    seq_blk = pl.BlockSpec((s_len, 128), lambda b, h, g: (b, 0))
    grp_blk = pl.BlockSpec((3, s_len, B_DH), lambda b, h, g: (g, b, h))
    gain_blk = pl.BlockSpec((8, 128), lambda b, h, g: (0, 0))
    return pl.pallas_call(
        _attn_bwd_body, name="attn_bwd", grid=(n_seq, B_HEADS, B_GROUPS),
        in_specs=[
            grp_blk,
            pl.BlockSpec((None, s_len, B_DH), lambda b, h, g: (B_PIECES - 1, b, h)),
            seq_blk, seq_blk, gain_blk, head_blk, head_blk, seq_blk,
        ],
        out_specs=[grp_blk, head_blk, gain_blk],
        out_shape=[jax.ShapeDtypeStruct((3 * B_GROUPS, t_rows, B_W), F32),
                   jax.ShapeDtypeStruct((t_rows, B_W), F32),
                   jax.ShapeDtypeStruct((8, 128), F32)],
        scratch_shapes=[pltpu.VMEM((s_len, B_DH), F32) for _ in range(7)],
        compiler_params=_params(("arbitrary", "arbitrary", "arbitrary")),
    )(proj_b, proj_b, cos_t, sin_t, gains8, d_og, o, lse)


def rope_tables(positions):
    inv_freq = ROPE_THETA ** (-jnp.arange(0, ROPE_DIMS, 2, dtype=F32) / ROPE_DIMS)
    ang = positions.astype(F32)[:, None] * inv_freq
    cos, sin = jnp.cos(ang), jnp.sin(ang)
    t_rows = positions.shape[0]
    rest = B_DH - ROPE_DIMS
    cos_t = jnp.concatenate([cos, cos, jnp.ones((t_rows, rest), F32)], axis=1)
    sin_t = jnp.concatenate([-sin, sin, jnp.zeros((t_rows, rest), F32)], axis=1)
    return cos_t, sin_t


def _rms_fwd_body(x_ref, g_ref, h_ref, *, layer):
    x = x_ref[...]
    r = lax.rsqrt(jnp.mean(x * x, axis=1, keepdims=True) + EPS)
    h_ref[...] = (x * r * g_ref[layer:layer + 1, :]).astype(h_ref.dtype)


def rms_fwd(x, gains8, layer):
    t_rows, d = x.shape
    tm = min(ROW_TILE, t_rows)
    return pl.pallas_call(
        functools.partial(_rms_fwd_body, layer=layer), name=f"rms_fwd_{layer}", grid=(t_rows // tm,),
        in_specs=[pl.BlockSpec((tm, d), lambda i: (i, 0)), pl.BlockSpec((8, d), lambda i: (0, 0))],
        out_specs=pl.BlockSpec((tm, d), lambda i: (i, 0)),
        out_shape=jax.ShapeDtypeStruct((t_rows, d), BF16),
        compiler_params=_params(("arbitrary",)),
    )(x, gains8)


def _rms_bwd_body(x_ref, g_ref, dh_ref, res_ref, dx_ref, dg_ref, *, layer):
    x, dh = x_ref[...], dh_ref[...]
    r = lax.rsqrt(jnp.mean(x * x, axis=1, keepdims=True) + EPS)
    xr = x * r
    u = dh * g_ref[layer:layer + 1, :]
    dx_ref[...] = res_ref[...] + r * u - xr * (r * r) * jnp.mean(u * x, axis=1, keepdims=True)

    @pl.when(pl.program_id(0) == 0)
    def _():
        dg_ref[...] = jnp.zeros_like(dg_ref)

    dg_ref[0:1, :] += jnp.sum(dh * xr, axis=0, keepdims=True)


def rms_bwd(x, gains8, layer, dh, d_res):
    t_rows, d = x.shape
    tm = min(ROW_TILE, t_rows)
    blk = pl.BlockSpec((tm, d), lambda i: (i, 0))
    gblk = pl.BlockSpec((8, d), lambda i: (0, 0))
    return pl.pallas_call(
        functools.partial(_rms_bwd_body, layer=layer), name=f"rms_bwd_{layer}", grid=(t_rows // tm,),
        in_specs=[blk, gblk, blk, blk],
        out_specs=[blk, gblk],
        out_shape=[jax.ShapeDtypeStruct((t_rows, d), F32), jax.ShapeDtypeStruct((8, d), F32)],
        compiler_params=_params(("arbitrary",)),
    )(x, gains8, dh, d_res)


def _piece_col(p):
    return jnp.where(p < 3 * B_GROUPS, (p % 3) * B_GROUPS + p // 3, 3 * B_GROUPS)


def _mm_nn_body(a_ref, w_ref, *rest, has_res):
    o_ref = rest[-1]
    acc = jnp.dot(a_ref[...], w_ref[...], preferred_element_type=F32)
    if has_res:
        acc = acc + rest[0][...]
    o_ref[...] = acc


def mm_nn(a, w, residual=None, *, tn, name):
    m, k = a.shape
    n = w.shape[1]
    tm = min(ROW_TILE, m)
    in_specs = [pl.BlockSpec((tm, k), lambda j, i: (i, 0)), pl.BlockSpec((k, tn), lambda j, i: (0, j))]
    args = [a, w]
    if residual is not None:
        in_specs.append(pl.BlockSpec((tm, tn), lambda j, i: (i, j)))
        args.append(residual)
    return pl.pallas_call(
        functools.partial(_mm_nn_body, has_res=residual is not None), name=name, grid=(n // tn, m // tm),
        in_specs=in_specs,
        out_specs=pl.BlockSpec((tm, tn), lambda j, i: (i, j)),
        out_shape=jax.ShapeDtypeStruct((m, n), F32),
        compiler_params=_params(("arbitrary", "arbitrary")),
    )(*args)


def mm_nn_pieces(a, w, *, name):
    m, k = a.shape
    tm = min(ROW_TILE, m)
    return pl.pallas_call(
        functools.partial(_mm_nn_body, has_res=False), name=name, grid=(B_PIECES, m // tm),
        in_specs=[pl.BlockSpec((tm, k), lambda p, i: (i, 0)),
                  pl.BlockSpec((k, B_W), lambda p, i: (0, _piece_col(p)))],
        out_specs=pl.BlockSpec((None, tm, B_W), lambda p, i: (p, i, 0)),
        out_shape=jax.ShapeDtypeStruct((B_PIECES, m, B_W), F32),
        compiler_params=_params(("arbitrary", "arbitrary")),
    )(a, w)


NT_ROW_TILE = 1024


def _mm_nt_body(g_ref, w_ref, *rest, has_init):
    o_ref = rest[-1]
    j = pl.program_id(1)
    part = lax.dot_general(_bf(g_ref[...]), w_ref[...], (((1,), (1,)), ((), ())), preferred_element_type=F32)

    @pl.when(j == 0)
    def _():
        o_ref[...] = part + rest[0][...] if has_init else part

    @pl.when(j > 0)
    def _():
        o_ref[...] += part


def mm_nt(g, w, init=None, *, tn, col_off=0, name):
    m, n = g.shape
    k = w.shape[0]
    tm = min(NT_ROW_TILE, m)
    in_specs = [pl.BlockSpec((tm, tn), lambda i, j: (i, j)),
                pl.BlockSpec((k, tn), lambda i, j: (0, col_off + j))]
    args = [g, w]
    if init is not None:
        in_specs.append(pl.BlockSpec((tm, k), lambda i, j: (i, 0)))
        args.append(init)
    return pl.pallas_call(
        functools.partial(_mm_nt_body, has_init=init is not None), name=name, grid=(m // tm, n // tn),
        in_specs=in_specs,
        out_specs=pl.BlockSpec((tm, k), lambda i, j: (i, 0)),
        out_shape=jax.ShapeDtypeStruct((m, k), F32),
        compiler_params=_params(("arbitrary", "arbitrary")),
    )(*args)


def mm_nt_pieces(g9, w, *, name):
    n_p, m, _ = g9.shape
    k = w.shape[0]
    tm = min(NT_ROW_TILE, m)
    return pl.pallas_call(
        functools.partial(_mm_nt_body, has_init=False), name=name, grid=(m // tm, n_p),
        in_specs=[pl.BlockSpec((None, tm, B_W), lambda i, p: (p, i, 0)),
                  pl.BlockSpec((k, B_W), lambda i, p: (0, _piece_col(p)))],
        out_specs=pl.BlockSpec((tm, k), lambda i, p: (i, 0)),
        out_shape=jax.ShapeDtypeStruct((m, k), F32),
        compiler_params=_params(("arbitrary", "arbitrary")),
    )(g9, w)


def _mm_tn_body(a_ref, g_ref, o_ref):
    o_ref[...] = lax.dot_general(a_ref[...], _bf(g_ref[...]), (((0,), (0,)), ((), ())),
                                 preferred_element_type=F32).astype(o_ref.dtype)


def mm_tn(a, g, *, tn, out_dtype, name):
    m, k = a.shape
    n = g.shape[1]
    return pl.pallas_call(
        _mm_tn_body, name=name, grid=(n // tn,),
        in_specs=[pl.BlockSpec((m, k), lambda j: (0, 0)), pl.BlockSpec((m, tn), lambda j: (0, j))],
        out_specs=pl.BlockSpec((k, tn), lambda j: (0, j)),
        out_shape=jax.ShapeDtypeStruct((k, n), out_dtype),
        compiler_params=_params(("arbitrary",)),
    )(a, g)


B_UNIT = 256
B_IN_COLS = B_PIECES * B_W
B_SHARD_UNITS = B_IN_COLS // N_DEV // B_UNIT


def mm_tn_b_in(a, g9, gz, *, out_dtype, name):
    m, k = a.shape
    per_piece = B_W // B_UNIT
    n_units = B_IN_COLS // B_UNIT

    def g_map(u):
        nat = jnp.minimum(u // per_piece, 3 * B_GROUPS - 1)
        piece = (nat % B_GROUPS) * 3 + nat // B_GROUPS
        return (piece, 0, u % per_piece)

    def body(a_ref, g_ref, z_ref, o_ref):
        u = pl.program_id(0)

        @pl.when(u < 3 * B_GROUPS * per_piece)
        def _():
            _mm_tn_body(a_ref, g_ref, o_ref)

        @pl.when(u >= 3 * B_GROUPS * per_piece)
        def _():
            _mm_tn_body(a_ref, z_ref, o_ref)

    return pl.pallas_call(
        body, name=name, grid=(n_units,),
        in_specs=[pl.BlockSpec((m, k), lambda u: (0, 0)),
                  pl.BlockSpec((None, m, B_UNIT), g_map),
                  pl.BlockSpec((m, B_UNIT), lambda u: (0, jnp.where(u < 3 * B_GROUPS * per_piece, 0, u % per_piece)))],
        out_specs=pl.BlockSpec((None, k, B_UNIT), lambda u: (u // B_SHARD_UNITS, 0, u % B_SHARD_UNITS)),
        out_shape=jax.ShapeDtypeStruct((N_DEV, k, B_IN_COLS // N_DEV), out_dtype),
        compiler_params=_params(("arbitrary",)),
    )(a, g9, gz)


def _loss_body(y_ref, t_ref, dy_ref, loss_ref, acc):
    i = pl.program_id(0)
    d = y_ref.shape[1]
    err = y_ref[...] - t_ref[...]
    dy_ref[...] = err * (1.0 / d)

    @pl.when(i == 0)
    def _():
        acc[...] = jnp.zeros_like(acc)

    acc[...] += jnp.sum(err * err, axis=0, keepdims=True)

    @pl.when(i == pl.num_programs(0) - 1)
    def _():
        total = jnp.sum(acc[...], axis=1, keepdims=True) * (0.5 / d)
        loss_ref[...] = jnp.broadcast_to(total, loss_ref.shape)


def loss_head(y, target):
    t_rows, d = y.shape
    tm = min(ROW_TILE, t_rows)
    blk = pl.BlockSpec((tm, d), lambda i: (i, 0))
    return pl.pallas_call(
        _loss_body, name="loss_head", grid=(t_rows // tm,),
        in_specs=[blk, blk],
        out_specs=[blk, pl.BlockSpec((8, 128), lambda i: (0, 0))],
        out_shape=[jax.ShapeDtypeStruct((t_rows, d), F32), jax.ShapeDtypeStruct((8, 128), F32)],
        scratch_shapes=[pltpu.VMEM((1, d), F32)],
        compiler_params=_params(("arbitrary",)),
    )(y, target)


def _adamw_body(p_ref, w_ref, m_ref, v_ref, g_ref, d_ref, nm_ref, nv_ref):
    g = p_ref[0].astype(F32)
    for s in range(1, N_DEV):
        g = g + p_ref[s].astype(F32)
    w = w_ref[...]
    m = ADAM_B1 * m_ref[...] + (1.0 - ADAM_B1) * g
    v = ADAM_B2 * v_ref[...] + (1.0 - ADAM_B2) * (g * g)
    m_hat = m / (1.0 - ADAM_B1 ** ADAM_STEP)
    v_hat = v / (1.0 - ADAM_B2 ** ADAM_STEP)
    g_ref[...] = g
    d_ref[...] = -ADAM_LR * (m_hat / (jnp.sqrt(v_hat) + ADAM_EPS) + ADAM_WD * w)
    nm_ref[...] = m
    nv_ref[...] = v


def adamw(parts, w, m, v, *, name):
    r, c = w.shape
    tr = r if r <= 256 else 256
    blk = pl.BlockSpec((tr, c), lambda i: (i, 0))
    out = jax.ShapeDtypeStruct((r, c), F32)
    return pl.pallas_call(
        _adamw_body, name=name, grid=(r // tr,),
        in_specs=[pl.BlockSpec((N_DEV, tr, c), lambda i: (0, i, 0)), blk, blk, blk],
        out_specs=[blk, blk, blk, blk],
        out_shape=[out, out, out, out],
        compiler_params=_params(("arbitrary",)),
    )(parts, w, m, v)


MESH_ID = pl.DeviceIdType.MESH
HBM_SPEC = pl.BlockSpec(memory_space=pl.ANY)


def _my_place():
    return lax.axis_index("x"), lax.axis_index("y"), lax.axis_index("c")


def _flat(x, y, c):
    return 4 * x + 2 * y + c


def _all_gather_body(*refs, n):
    ins, outs = refs[:n], refs[n:2 * n]
    send_sems, recv_sems, local_sems = refs[2 * n:]
    x, y, c = _my_place()
    me, sibling = (x, y, c), (x, y, 1 - c)
    chips = [(1 - x, y), (x, 1 - y), (1 - x, 1 - y)]
    pending = []
    for a in range(n):
        src, out = ins[a], outs[a]

        def copy(k, block, to, from_input=False, a=a, src=src, out=out):
            slot = out.at[_flat(*block)]
            return pltpu.make_async_remote_copy(
                src_ref=src if from_input else slot, dst_ref=slot,
                send_sem=send_sems.at[7 * a + k], recv_sem=recv_sems.at[7 * a + k],
                device_id=to, device_id_type=MESH_ID)

        mine = pltpu.make_async_copy(src, out.at[_flat(*me)], local_sems.at[a])
        mine.start()
        first = [copy(0, me, sibling, True)] + [copy(1 + j, me, (*chip, c), True) for j, chip in enumerate(chips)]
        for cp in first:
            cp.start()
        pending.append((copy, mine, first))
    for copy, mine, first in pending:
        passed = [copy(4 + j, (*chip, c), sibling) for j, chip in enumerate(chips)]
        for j, chip in enumerate(chips):
            copy(1 + j, (*chip, c), me).wait_recv()
            passed[j].start()
        copy(0, sibling, me).wait_recv()
        for j, chip in enumerate(chips):
            copy(4 + j, (*chip, 1 - c), me).wait_recv()
        for cp in first + passed:
            cp.wait_send()
        mine.wait()


def all_gather(shards, *, name):
    n = len(shards)
    return pl.pallas_call(
        functools.partial(_all_gather_body, n=n), name=name,
        in_specs=[HBM_SPEC] * n, out_specs=[HBM_SPEC] * n,
        out_shape=[jax.ShapeDtypeStruct((N_DEV,) + s.shape, s.dtype) for s in shards],
        scratch_shapes=[pltpu.SemaphoreType.DMA((7 * n,)), pltpu.SemaphoreType.DMA((7 * n,)),
                        pltpu.SemaphoreType.DMA((n,))],
    )(*shards)


PEER_FLIPS = [(0, 0, 1), (1, 0, 0), (0, 1, 0), (1, 1, 0), (1, 0, 1), (0, 1, 1), (1, 1, 1)]


def _all_to_all_body(*refs, n):
    ins, outs = refs[:n], refs[n:2 * n]
    send_sems, recv_sems, local_sems = refs[2 * n:]
    x, y, c = _my_place()
    me = _flat(x, y, c)
    waits = []
    for a in range(n):
        src, out = ins[a], outs[a]
        mine = pltpu.make_async_copy(src.at[me], out.at[me], local_sems.at[a])
        mine.start()
        waits.append(mine)
        for k, (fx, fy, fc) in enumerate(PEER_FLIPS):
            peer = (1 - x if fx else x, 1 - y if fy else y, 1 - c if fc else c)
            theirs = _flat(*peer)
            sems = dict(send_sem=send_sems.at[7 * a + k], recv_sem=recv_sems.at[7 * a + k],
                        device_id=peer, device_id_type=MESH_ID)
            send = pltpu.make_async_remote_copy(src_ref=src.at[theirs], dst_ref=out.at[me], **sems)
            send.start()
            recv = pltpu.make_async_remote_copy(src_ref=src.at[theirs], dst_ref=out.at[theirs], **sems)
            waits.append((send, recv))
    for w in waits:
        if isinstance(w, tuple):
            w[0].wait_send()
            w[1].wait_recv()
        else:
            w.wait()


def all_to_all(parts, *, name):
    n = len(parts)
    return pl.pallas_call(
        functools.partial(_all_to_all_body, n=n), name=name,
        in_specs=[HBM_SPEC] * n, out_specs=[HBM_SPEC] * n,
        out_shape=[jax.ShapeDtypeStruct(p.shape, p.dtype) for p in parts],
        scratch_shapes=[pltpu.SemaphoreType.DMA((7 * n,)), pltpu.SemaphoreType.DMA((7 * n,)),
                        pltpu.SemaphoreType.DMA((n,))],
    )(*parts)


def _pad_rows(a, rows=8):
    return jnp.pad(a, ((0, rows - a.shape[0]), (0, 0)))


def _gate_rows(a_log, dt_bias):
    z = jnp.zeros((8, 128), F32)
    return z.at[0, A_HEADS:2 * A_HEADS].set(a_log[0]).at[1, A_HEADS:2 * A_HEADS].set(dt_bias[0])


def _pack_small(norm_g, a_log, a_dt_bias, a_norm_g, b_q_norm_g, b_k_norm_g):
    return jnp.concatenate([
        norm_g[0].reshape(8, 128), norm_g[1].reshape(8, 128),
        _gate_rows(a_log, a_dt_bias),
        _pad_rows(a_norm_g[0].reshape(2, 128)),
        _pad_rows(jnp.concatenate([b_q_norm_g[0], b_k_norm_g[0]], axis=0)),
    ], axis=0)


def _unpack_small(p):
    return (p[0:16].reshape(2, D_MODEL), p[16:17, A_HEADS:2 * A_HEADS], p[17:18, A_HEADS:2 * A_HEADS],
            p[24:26].reshape(1, A_DV), p[32:35][None], p[35:38][None])


def kernel(x, positions, norm_g, a_w_in, a_conv_w, a_log, a_dt_bias, a_norm_g, a_w_out, b_w_in, b_q_norm_g, b_k_norm_g, b_w_out, loss_target, m_norm_g, m_a_w_in, m_a_conv_w, m_a_log, m_a_dt_bias, m_a_norm_g, m_a_w_out, m_b_w_in, m_b_q_norm_g, m_b_k_norm_g, m_b_w_out, v_norm_g, v_a_w_in, v_a_conv_w, v_a_log, v_a_dt_bias, v_a_norm_g, v_a_w_out, v_b_w_in, v_b_q_norm_g, v_b_k_norm_g, v_b_w_out):
    n_seq, s_len, d = x.shape
    t_rows = n_seq * s_len
    n_chunks = s_len // CHUNK
    x0 = x.reshape(t_rows, d)
    target = loss_target.reshape(t_rows, d)
    my_slot = _flat(*_my_place())

    g_a_in, g_a_out, g_b_in, g_b_out, g_conv = all_gather(
        [a_w_in[0].astype(BF16), a_w_out[0].astype(BF16), b_w_in[0].astype(BF16), b_w_out[0].astype(BF16),
         _pad_rows(a_conv_w[0])], name="gather_weights")
    w_a_in = jnp.pad(g_a_in.transpose(1, 0, 2).reshape(d, A_IN), ((0, 0), (0, A_IN_PAD - A_IN)))
    w_a_out = g_a_out.reshape(A_VW, d)
    w_b_in = g_b_in.transpose(1, 0, 2).reshape(d, B_IN_COLS)
    w_b_out = g_b_out.reshape(B_W, d)
    conv_w8 = g_conv.transpose(1, 0, 2).reshape(8, 2 * A_QK + A_VW)

    gains_model = _pad_rows(norm_g)
    gate_prm = _gate_rows(a_log, a_dt_bias)
    gain_a_out = _pad_rows(a_norm_g)
    gains_qk = _pad_rows(jnp.concatenate([b_q_norm_g[0], b_k_norm_g[0]], axis=0))
    cos_t, sin_t = rope_tables(positions.reshape(t_rows))

    h0 = rms_fwd(x0, gains_model, 0)
    proj_a = mm_nn(h0, w_a_in, tn=896, name="proj_a")
    qkv = a_pre_fwd(proj_a, conv_w8, n_seq)
    gates_col, gates_row = a_gates_fwd(proj_a, gate_prm, n_seq)
    gates_row = gates_row.reshape(n_seq, 2 * A_HEADS, s_len // SUPER, 1, SUPER)
    o_a, tinv, states = gdn_fwd(qkv, gates_col, gates_row, n_seq)
    og_a = a_post_fwd(o_a, proj_a, gain_a_out)
    x1 = mm_nn(og_a, w_a_out, x0, tn=512, name="out_a")

    h1 = rms_fwd(x1, gains_model, 1)
    proj_b = mm_nn_pieces(h1, w_b_in, name="proj_b")
    og_b, o_b, lse = attn_fwd(proj_b, cos_t, sin_t, gains_qk, n_seq)
    y = mm_nn(og_b, w_b_out, x1, tn=512, name="out_b")

    dy, loss_blk = loss_head(y, target)
    loss = lax.psum(loss_blk[0, 0], ("x", "y", "c"))

    d_og_b = mm_nt(dy, w_b_out, tn=512, name="d_og_b")
    dw_b_out = mm_tn(og_b, dy, tn=256, out_dtype=BF16, name="dw_b_out")
    dqkv_b, dz_b, d_gains_qk = attn_bwd(proj_b, cos_t, sin_t, gains_qk, d_og_b, o_b, lse, n_seq)
    dh1 = mm_nt_pieces(dqkv_b, w_b_in, name="dh1_qkv")
    dh1 = mm_nt(dz_b, w_b_in, dh1, tn=B_W, col_off=3 * B_GROUPS, name="dh1_z")
    dw_b_in = mm_tn_b_in(h1, dqkv_b, dz_b, out_dtype=BF16, name="dw_b_in")
    dx1, d_gain1 = rms_bwd(x1, gains_model, 1, dh1, dy)

    d_og_a = mm_nt(dx1, w_a_out, tn=512, name="d_og_a")
    dw_a_out = mm_tn(og_a, dx1, tn=128, out_dtype=BF16, name="dw_a_out")
    do_a, dz_a, d_gain_a_out = a_post_bwd(o_a, proj_a, gain_a_out, d_og_a)
    dqkv_a, dgates = gdn_bwd(qkv, gates_col, gates_row, tinv, states, do_a, n_seq)
    d_pre, d_conv = a_pre_bwd(proj_a, conv_w8, dqkv_a, n_seq)
    d_gate_logits, d_gate_prm = a_gates_bwd(proj_a, gate_prm, dgates, n_seq)
    dh0 = mm_nt(d_pre, w_a_in, tn=512, name="dh0_qkv")
    dh0 = mm_nt(dz_a, w_a_in, dh0, tn=512, col_off=(2 * A_QK + A_VW) // 512, name="dh0_z")
    dh0 = mm_nt(d_gate_logits, w_a_in, dh0, tn=128, col_off=A_GATE_COL, name="dh0_gates")
    dw_a_in = jnp.concatenate([
        mm_tn(h0, d_pre, tn=256, out_dtype=BF16, name="dw_a_in_qkv"),
        mm_tn(h0, dz_a, tn=256, out_dtype=BF16, name="dw_a_in_z"),
        mm_tn(h0, d_gate_logits, tn=128, out_dtype=BF16, name="dw_a_in_gates"),
    ], axis=1)[:, :A_IN]
    dx0, d_gain0 = rms_bwd(x0, gains_model, 0, dh0, dx1)

    shard_a_in = A_IN // N_DEV
    r_a_in, r_a_out, r_b_in, r_b_out = all_to_all(
        [dw_a_in.reshape(d, N_DEV, shard_a_in).transpose(1, 0, 2),
         dw_a_out.reshape(N_DEV, A_VW // N_DEV, d),
         dw_b_in,
         dw_b_out.reshape(N_DEV, B_W // N_DEV, d)], name="scatter_grads")
    small = jnp.concatenate([
        d_gain0[0].reshape(8, 128), d_gain1[0].reshape(8, 128), d_gate_prm,
        _pad_rows(d_gain_a_out[0].reshape(2, 128)), d_gains_qk], axis=0)
    r_small, r_conv = all_gather([small, d_conv], name="gather_small_grads")
    conv_cols = a_conv_w.shape[2]
    r_conv = lax.dynamic_slice(r_conv, (0, 0, my_slot * conv_cols), (N_DEV, 8, conv_cols))

    upd = {}
    upd["a_w_in"] = adamw(r_a_in, a_w_in[0], m_a_w_in[0], v_a_w_in[0], name="adamw_a_w_in")
    upd["a_w_out"] = adamw(r_a_out, a_w_out[0], m_a_w_out[0], v_a_w_out[0], name="adamw_a_w_out")
    upd["b_w_in"] = adamw(r_b_in, b_w_in[0], m_b_w_in[0], v_b_w_in[0], name="adamw_b_w_in")
    upd["b_w_out"] = adamw(r_b_out, b_w_out[0], m_b_w_out[0], v_b_w_out[0], name="adamw_b_w_out")
    upd["a_conv_w"] = [t[:A_CONV] for t in adamw(
        r_conv, _pad_rows(a_conv_w[0]), _pad_rows(m_a_conv_w[0]), _pad_rows(v_a_conv_w[0]), name="adamw_a_conv_w")]
    small_upd = adamw(
        r_small,
        _pack_small(norm_g, a_log, a_dt_bias, a_norm_g, b_q_norm_g, b_k_norm_g),
        _pack_small(m_norm_g, m_a_log, m_a_dt_bias, m_a_norm_g, m_b_q_norm_g, m_b_k_norm_g),
        _pack_small(v_norm_g, v_a_log, v_a_dt_bias, v_a_norm_g, v_b_q_norm_g, v_b_k_norm_g),
        name="adamw_small")
    small_names = ("norm_g", "a_log", "a_dt_bias", "a_norm_g", "b_q_norm_g", "b_k_norm_g")
    unpacked = [_unpack_small(t) for t in small_upd]
    for i, nm in enumerate(small_names):
        upd[nm] = [u[i] for u in unpacked]

    order = ("norm_g", "a_w_in", "a_conv_w", "a_log", "a_dt_bias", "a_norm_g", "a_w_out",
             "b_w_in", "b_q_norm_g", "b_k_norm_g", "b_w_out")
    big = ("a_w_in", "a_conv_w", "a_w_out", "b_w_in", "b_w_out")
    outs = [loss, dx0.reshape(n_seq, s_len, d)]
    for kind in range(4):
        for nm in order:
            t = upd[nm][kind]
            outs.append(t[None] if nm in big else t)
    return tuple(outs)
```

```python
import functools
import math

import jax
import jax.numpy as jnp
from jax import lax
from jax.experimental import pallas as pl
from jax.experimental.pallas import tpu as pltpu

F32 = jnp.float32
BF16 = jnp.bfloat16

D_MODEL = 1024
EPS = 1e-6
N_DEV = 8

A_HEADS = 8
A_DK = 128
A_DV = 256
A_QK = A_HEADS * A_DK
A_VW = A_HEADS * A_DV
A_CONV = 4
CHUNK = 64
A_IN = 2 * A_QK + 2 * A_VW + 2 * A_HEADS
A_IN_PAD = 2 * A_QK + 2 * A_VW + 128
A_GATE_COL = (2 * A_QK + 2 * A_VW) // 128

B_DILATIONS = (1, 4, 16)
B_GROUPS = 3
B_HEADS = 8
B_DH = 128
B_W = B_HEADS * B_DH
B_BLOCK = 128
B_PIECES = 3 * B_GROUPS + 1
ROPE_THETA = 500000.0
ROPE_DIMS = B_DH // 4
ROPE_HALF = ROPE_DIMS // 2

ADAM_LR = 0.001
ADAM_B1 = 0.9
ADAM_B2 = 0.999
ADAM_EPS = 1e-08
ADAM_WD = 0.01
ADAM_STEP = 10

VMEM_LIMIT = 56 * 1024 * 1024


def _params(sem):
    return pltpu.CompilerParams(dimension_semantics=sem, vmem_limit_bytes=VMEM_LIMIT)


def _bf(x):
    return x.astype(BF16)


def _mm(a, b):
    return jnp.dot(_bf(a), _bf(b), preferred_element_type=F32)


def _mm_nt(a, b):
    return lax.dot_general(_bf(a), _bf(b), (((1,), (1,)), ((), ())), preferred_element_type=F32)


def _mm_tn(a, b):
    return lax.dot_general(_bf(a), _bf(b), (((0,), (0,)), ((), ())), preferred_element_type=F32)


def _split(x):
    hi = _bf(x)
    return hi, _bf(x - hi.astype(F32))


def _mm3(a, b):
    ah, al = _split(a)
    bh, bl = _split(b)
    d = functools.partial(jnp.dot, preferred_element_type=F32)
    return d(ah, bh) + (d(ah, bl) + d(al, bh))


def _colsum_as_col(z):
    zh, zl = _split(z)
    ones = jnp.ones((z.shape[0], 128), BF16)
    tn = functools.partial(lax.dot_general, dimension_numbers=(((0,), (0,)), ((), ())),
                           preferred_element_type=F32)
    return (tn(zh, ones) + tn(zl, ones))[:, 0:1]


def _sigmoid(x):
    return 1.0 / (1.0 + jnp.exp(-x))


INV_BASE = 8
GDN_GROUP = 4
SUPER = GDN_GROUP * CHUNK

A_K_COL = A_QK // A_DK
A_V_COL = 2 * A_QK // A_DV
A_HEAD_W = 2 * A_DK + A_DV


def _unit_lower_inverse(m, row, col):
    eye = (row == col).astype(F32)
    d = jnp.where(row // INV_BASE == col // INV_BASE, m, 0.0)
    x = eye - d
    p = _mm3(d, d)
    steps = int(math.log2(INV_BASE)) - 1
    for i in range(steps):
        x = x + _mm3(x, p)
        if i + 1 < steps:
            p = _mm3(p, p)
    size = INV_BASE
    while size < CHUNK:
        c = jnp.where((row // (2 * size) == col // (2 * size)) & (row // size != col // size), m, 0.0)
        x = x - _mm3(_mm3(x, c), x)
        size *= 2
    return x


def _diag_blocks_tall(x):
    return jnp.concatenate([x[i * CHUNK:(i + 1) * CHUNK, i * CHUNK:(i + 1) * CHUNK] for i in range(GDN_GROUP)], axis=0)


def _tall_to_block_diag(t, same):
    return jnp.where(same, jnp.concatenate([t] * GDN_GROUP, axis=1), 0.0)


def _block_sum(same, x):
    xh, xl = _split(jnp.broadcast_to(x, (SUPER, 128)))
    ones = same.astype(BF16)
    d = functools.partial(jnp.dot, preferred_element_type=F32)
    return (d(ones, xh) + d(ones, xl))[:, 0:1]


def _super_rows(i):
    return pl.ds(pl.multiple_of(i * SUPER, SUPER), SUPER)


def _chunk_rows(n):
    return pl.ds(pl.multiple_of(n * CHUNK, CHUNK), CHUNK)


def _gdn_super_common(q, k, v, gcb, gr, head, tinv_tall=None):
    lane = lax.broadcasted_iota(jnp.int32, (SUPER, 128), 1)
    row = lax.broadcasted_iota(jnp.int32, (SUPER, SUPER), 0)
    col = lax.broadcasted_iota(jnp.int32, (SUPER, SUPER), 1)
    same = row // CHUNK == col // CHUNK
    beta = jnp.sum(jnp.where(lane == head, gcb, 0.0), axis=1, keepdims=True)
    gc = jnp.sum(jnp.where(lane == A_HEADS + head, gcb, 0.0), axis=1, keepdims=True)
    g_last = jnp.sum(jnp.where(col == (row // CHUNK) * CHUNK + (CHUNK - 1), gr, 0.0), axis=1, keepdims=True)
    gamma = jnp.exp(gc)
    decay = jnp.where(same & (row >= col), jnp.exp(jnp.minimum(gc - gr, 0.0)), 0.0)
    kb = k * beta
    m = jnp.where(same & (row > col), _mm_nt(kb, k) * decay, 0.0)
    if tinv_tall is None:
        tinv = _unit_lower_inverse(m, row, col)
    else:
        tinv = _tall_to_block_diag(tinv_tall, same)
    u = _mm(tinv, v * beta)
    w = _mm(tinv, kb * gamma)
    p = jnp.where(same & (row >= col), _mm_nt(q, k) * decay, 0.0)
    e_tail = jnp.exp(g_last - gc)
    return dict(beta=beta, gc=gc, g_last=g_last, gamma=gamma, decay=decay, kb=kb, m=m,
                tinv=tinv, u=u, w=w, p=p, e_tail=e_tail, row=row, col=col, lane=lane, same=same)


def _store_scan_operands(rows, q, k, t, u_scr, w_scr, p_scr, qg_scr, ke_scr, gl_scr):
    u_scr[rows, :] = t["u"]
    w_scr[rows, :] = _bf(t["w"])
    p_scr[rows, :] = _bf(_diag_blocks_tall(t["p"]))
    qg_scr[rows, :] = _bf(q * t["gamma"])
    ke_scr[rows, :] = _bf(k * t["e_tail"])
    gl_scr[rows, :] = jnp.broadcast_to(jnp.exp(t["g_last"]), (SUPER, 128))


def _gdn_fwd_body(q_ref, k_ref, v_ref, gc_ref, gr_ref, o_ref, tinv_ref, st_ref,
                  s_scr, u_scr, w_scr, p_scr, qg_scr, ke_scr, gl_scr):
    head = pl.program_id(1)
    n_chunks = q_ref.shape[0] // CHUNK

    def prepare(i, carry):
        rows = _super_rows(i)
        q, k = q_ref[rows, :], k_ref[rows, :]
        t = _gdn_super_common(q, k, v_ref[rows, :], gc_ref[rows, :], gr_ref[i], head)
        tinv_ref[rows, :] = _diag_blocks_tall(t["tinv"])
        _store_scan_operands(rows, q, k, t, u_scr, w_scr, p_scr, qg_scr, ke_scr, gl_scr)
        return carry

    lax.fori_loop(0, n_chunks // GDN_GROUP, prepare, 0)
    s_scr[...] = jnp.zeros_like(s_scr)

    def scan(n, carry):
        rows = _chunk_rows(n)
        s = s_scr[...]
        st_ref[n] = s
        sb = _bf(s)
        v_new = u_scr[rows, :] - jnp.dot(w_scr[rows, :], sb, preferred_element_type=F32)
        vb = _bf(v_new)
        o_ref[rows, :] = (jnp.dot(qg_scr[rows, :], sb, preferred_element_type=F32)
                          + jnp.dot(p_scr[rows, :], vb, preferred_element_type=F32))
        s_scr[...] = s * gl_scr[rows, :][0:1, 0:1] + lax.dot_general(
            ke_scr[rows, :], vb, (((0,), (0,)), ((), ())), preferred_element_type=F32)
        return carry

    lax.fori_loop(0, n_chunks, scan, 0)


def _gdn_in_specs(s_len, n_super):
    return [
        pl.BlockSpec((s_len, A_DK), lambda b, h: (b, h)),
        pl.BlockSpec((s_len, A_DK), lambda b, h: (b, A_K_COL + h)),
        pl.BlockSpec((s_len, A_DV), lambda b, h: (b, A_V_COL + h)),
        pl.BlockSpec((s_len, 128), lambda b, h: (b, 0)),
        pl.BlockSpec((None, None, n_super, 1, SUPER), lambda b, h: (b, A_HEADS + h, 0, 0, 0)),
    ]


def _gdn_scan_scratch(s_len):
    return [pltpu.VMEM((A_DK, A_DV), F32), pltpu.VMEM((s_len, A_DV), F32),
            pltpu.VMEM((s_len, A_DK), BF16), pltpu.VMEM((s_len, CHUNK), BF16),
            pltpu.VMEM((s_len, A_DK), BF16), pltpu.VMEM((s_len, A_DK), BF16),
            pltpu.VMEM((s_len, 128), F32)]


def gdn_fwd(qkv, gates_col, gates_row, n_seq):
    t_rows = qkv.shape[0]
    s_len = t_rows // n_seq
    n_chunks = s_len // CHUNK
    return pl.pallas_call(
        _gdn_fwd_body, name="gdn_fwd", grid=(n_seq, A_HEADS),
        in_specs=_gdn_in_specs(s_len, s_len // SUPER),
        out_specs=[
            pl.BlockSpec((s_len, A_DV), lambda b, h: (b, h)),
            pl.BlockSpec((s_len, CHUNK), lambda b, h: (b * A_HEADS + h, 0)),
            pl.BlockSpec((None, n_chunks, A_DK, A_DV), lambda b, h: (b * A_HEADS + h, 0, 0, 0)),
        ],
        out_shape=[
            jax.ShapeDtypeStruct((t_rows, A_VW), F32),
            jax.ShapeDtypeStruct((n_seq * A_HEADS * s_len, CHUNK), F32),
            jax.ShapeDtypeStruct((n_seq * A_HEADS, n_chunks, A_DK, A_DV), F32),
        ],
        scratch_shapes=_gdn_scan_scratch(s_len),
        compiler_params=_params(("arbitrary", "arbitrary")),
    )(qkv, qkv, qkv, gates_col, gates_row)


def _gdn_bwd_body(q_ref, k_ref, v_ref, gc_ref, gr_ref, tinv_ref, st_ref, do_ref,
                  dqkv_ref, dgc_ref,
                  ds_scr, u_scr, w_scr, p_scr, qg_scr, ke_scr, gl_scr,
                  vn_scr, dvn_scr, dqg_scr, dw_scr, dkt_scr, sds_scr):
    head = pl.program_id(1)
    n_chunks = q_ref.shape[0] // CHUNK

    @pl.when(head == 0)
    def _():
        dgc_ref[...] = jnp.zeros_like(dgc_ref)

    def common(i):
        rows = _super_rows(i)
        q, k, v = q_ref[rows, :], k_ref[rows, :], v_ref[rows, :]
        return rows, q, k, v, _gdn_super_common(q, k, v, gc_ref[rows, :], gr_ref[i], head,
                                                tinv_tall=tinv_ref[rows, :])

    def prepare(i, carry):
        rows, q, k, _, t = common(i)
        _store_scan_operands(rows, q, k, t, u_scr, w_scr, p_scr, qg_scr, ke_scr, gl_scr)
        return carry

    lax.fori_loop(0, n_chunks // GDN_GROUP, prepare, 0)
    ds_scr[...] = jnp.zeros_like(ds_scr)
    tn = functools.partial(lax.dot_general, dimension_numbers=(((0,), (0,)), ((), ())), preferred_element_type=F32)
    nt = functools.partial(lax.dot_general, dimension_numbers=(((1,), (1,)), ((), ())), preferred_element_type=F32)

    def scan(i, carry):
        n = n_chunks - 1 - i
        rows = _chunk_rows(n)
        ds_next = ds_scr[...]
        dsb = _bf(ds_next)
        s = st_ref[n]
        sb = _bf(s)
        d_ob = _bf(do_ref[rows, :])
        v_new = u_scr[rows, :] - jnp.dot(w_scr[rows, :], sb, preferred_element_type=F32)
        d_vn = tn(p_scr[rows, :], d_ob) + jnp.dot(ke_scr[rows, :], dsb, preferred_element_type=F32)
        d_vnb = _bf(d_vn)
        vn_scr[rows, :] = v_new
        dvn_scr[rows, :] = d_vn
        dqg_scr[rows, :] = nt(d_ob, sb)
        dw_scr[rows, :] = -nt(d_vnb, sb)
        dkt_scr[rows, :] = nt(_bf(v_new), dsb)
        sds = jnp.sum(jnp.sum(s * ds_next, axis=1, keepdims=True), axis=0, keepdims=True)
        sds_scr[rows, :] = jnp.broadcast_to(sds, (CHUNK, 128))
        ds_scr[...] = (tn(qg_scr[rows, :], d_ob) + gl_scr[rows, :][0:1, 0:1] * ds_next
                       - tn(w_scr[rows, :], d_vnb))
        return carry

    lax.fori_loop(0, n_chunks, scan, 0)

    def finish(i, carry):
        rows, q, k, v, t = common(i)
        beta, gamma, decay, kb, e_tail = t["beta"], t["gamma"], t["decay"], t["kb"], t["e_tail"]
        row, col, lane, same = t["row"], t["col"], t["lane"], t["same"]
        d_o = do_ref[rows, :]
        v_new, d_vn = vn_scr[rows, :], dvn_scr[rows, :]
        d_qg, d_w, d_kt = dqg_scr[rows, :], dw_scr[rows, :], dkt_scr[rows, :]
        gamma_last = jnp.exp(t["g_last"])

        d_p = jnp.where(same & (row >= col), _mm_nt(d_o, v_new), 0.0)
        d_ru = _mm_tn(t["tinv"], d_vn)
        d_rw = _mm_tn(t["tinv"], d_w)
        d_m = jnp.where(same & (row > col), -(_mm_nt(d_ru, t["u"]) + _mm_nt(d_rw, t["w"])), 0.0)

        x_p = d_p * decay
        y_m = d_m * decay
        d_kb = _mm(y_m, k) + d_rw * gamma
        dqkv_ref[rows, 0:A_DK] = _mm(x_p, k) + d_qg * gamma
        dqkv_ref[rows, A_DK:2 * A_DK] = _mm_tn(x_p, q) + _mm_tn(y_m, kb) + d_kb * beta + d_kt * e_tail
        dqkv_ref[rows, 2 * A_DK:A_HEAD_W] = d_ru * beta

        d_beta = (jnp.sum(d_ru * v, axis=1, keepdims=True)
                  + jnp.sum(d_kb * k, axis=1, keepdims=True))
        z = d_p * t["p"] + d_m * t["m"]
        eps_tail = jnp.sum(d_kt * k, axis=1, keepdims=True) * e_tail
        d_gc = (jnp.sum(z, axis=1, keepdims=True) - _colsum_as_col(z)
                + jnp.sum(d_qg * q, axis=1, keepdims=True) * gamma
                + jnp.sum(d_rw * kb, axis=1, keepdims=True) * gamma
                - eps_tail)
        d_glast = _block_sum(same, eps_tail) + gamma_last * sds_scr[rows, :][:, 0:1]
        rcol = lax.broadcasted_iota(jnp.int32, (SUPER, 1), 0)
        d_gc = d_gc + jnp.where(rcol % CHUNK == CHUNK - 1, d_glast, 0.0)
        dgc_ref[rows, :] += (jnp.where(lane == head, d_beta, 0.0)
                             + jnp.where(lane == A_HEADS + head, d_gc, 0.0))
        return carry

    lax.fori_loop(0, n_chunks // GDN_GROUP, finish, 0)


def gdn_bwd(qkv, gates_col, gates_row, tinv, states, d_o, n_seq):
    t_rows = qkv.shape[0]
    s_len = t_rows // n_seq
    n_chunks = s_len // CHUNK
    v_spec = pl.BlockSpec((s_len, A_DV), lambda b, h: (b, h))
    gate_spec = pl.BlockSpec((s_len, 128), lambda b, h: (b, 0))
    return pl.pallas_call(
        _gdn_bwd_body, name="gdn_bwd", grid=(n_seq, A_HEADS),
        in_specs=_gdn_in_specs(s_len, s_len // SUPER) + [
            pl.BlockSpec((s_len, CHUNK), lambda b, h: (b * A_HEADS + h, 0)),
            pl.BlockSpec((None, n_chunks, A_DK, A_DV), lambda b, h: (b * A_HEADS + h, 0, 0, 0)),
            v_spec,
        ],
        out_specs=[pl.BlockSpec((s_len, A_HEAD_W), lambda b, h: (b, h)), gate_spec],
        out_shape=[
            jax.ShapeDtypeStruct((t_rows, A_HEADS * A_HEAD_W), F32),
            jax.ShapeDtypeStruct((t_rows, 128), F32),
        ],
        scratch_shapes=_gdn_scan_scratch(s_len) + [
            pltpu.VMEM((s_len, A_DV), F32), pltpu.VMEM((s_len, A_DV), F32),
            pltpu.VMEM((s_len, A_DK), F32), pltpu.VMEM((s_len, A_DK), F32), pltpu.VMEM((s_len, A_DK), F32),
            pltpu.VMEM((s_len, 128), F32)],
        compiler_params=_params(("arbitrary", "arbitrary")),
    )(qkv, qkv, qkv, gates_col, gates_row, tinv, states, d_o)


def _shift_down(x, j, rows):
    return jnp.where(rows >= j, pltpu.roll(x, j, 0), 0.0)


def _shift_up(x, j, rows):
    n = x.shape[0]
    return jnp.where(rows < n - j, pltpu.roll(x, n - j, 0), 0.0)


def _conv_silu_norm(x, w, cb):
    rows = lax.broadcasted_iota(jnp.int32, (x.shape[0], 1), 0)
    c = x * w[A_CONV - 1:A_CONV, :]
    for j in range(1, A_CONV):
        c = c + _shift_down(x, j, rows) * w[A_CONV - 1 - j:A_CONV - j, :]
    sig = _sigmoid(c)
    a = c * sig
    rn = lax.rsqrt(jnp.sum(a * a, axis=1, keepdims=True) + EPS)
    is_q = cb < A_QK // 128
    is_qk = cb < 2 * A_QK // 128
    q_scale = jnp.where(is_q, A_DK ** -0.5, 1.0).astype(F32)
    return rows, c, sig, a, rn, is_qk, q_scale


def _a_pre_fwd_body(x_ref, w_ref, o_ref):
    cb = pl.program_id(1)
    _, _, _, a, rn, is_qk, q_scale = _conv_silu_norm(x_ref[...], w_ref[...], cb)
    o_ref[...] = a * jnp.where(is_qk, rn * q_scale, 1.0)


def a_pre_fwd(proj_a, conv_w8, n_seq):
    t_rows = proj_a.shape[0]
    s_len = t_rows // n_seq
    n_cb = (2 * A_QK + A_VW) // 128
    blk = pl.BlockSpec((s_len, 128), lambda b, c: (b, c))
    return pl.pallas_call(
        _a_pre_fwd_body, name="a_pre_fwd", grid=(n_seq, n_cb),
        in_specs=[blk, pl.BlockSpec((8, 128), lambda b, c: (0, c))],
        out_specs=blk,
        out_shape=jax.ShapeDtypeStruct((t_rows, n_cb * 128), F32),
        compiler_params=_params(("arbitrary", "arbitrary")),
    )(proj_a, conv_w8)


def _a_pre_bwd_body(x_ref, w_ref, dy_ref, dx_ref, dw_ref):
    b, cb = pl.program_id(1), pl.program_id(0)
    x, w = x_ref[...], w_ref[...]
    rows, c, sig, a, rn, is_qk, q_scale = _conv_silu_norm(x, w, cb)
    dy = dy_ref[...]
    da_n = q_scale * (rn * dy - a * (rn * rn * rn) * jnp.sum(dy * a, axis=1, keepdims=True))
    da = jnp.where(is_qk, da_n, dy)
    dc = da * (sig * (1.0 + c * (1.0 - sig)))
    dx = dc * w[A_CONV - 1:A_CONV, :]
    for j in range(1, A_CONV):
        dx = dx + _shift_up(dc, j, rows) * w[A_CONV - 1 - j:A_CONV - j, :]
    dx_ref[...] = dx

    @pl.when(b == 0)
    def _():
        dw_ref[...] = jnp.zeros_like(dw_ref)

    for j in range(A_CONV):
        xs = x if j == 0 else _shift_down(x, j, rows)
        dw_ref[A_CONV - 1 - j:A_CONV - j, :] += jnp.sum(dc * xs, axis=0, keepdims=True)


def a_pre_bwd(proj_a, conv_w8, dqkv_hm, n_seq):
    t_rows = proj_a.shape[0]
    s_len = t_rows // n_seq
    n_cb = (2 * A_QK + A_VW) // 128
    blk = pl.BlockSpec((s_len, 128), lambda c, b: (b, c))
    wblk = pl.BlockSpec((8, 128), lambda c, b: (0, c))
    per_head = A_HEAD_W // 128
    n_q = A_QK // 128

    def head_major(c, b):
        v_blk = jnp.maximum(c - 2 * n_q, 0)
        col = jnp.where(c < n_q, c * per_head,
                        jnp.where(c < 2 * n_q, (c - n_q) * per_head + 1,
                                  (v_blk // 2) * per_head + 2 + v_blk % 2))
        return (b, col)

    return pl.pallas_call(
        _a_pre_bwd_body, name="a_pre_bwd", grid=(n_cb, n_seq),
        in_specs=[blk, wblk, pl.BlockSpec((s_len, 128), head_major)],
        out_specs=[blk, wblk],
        out_shape=[jax.ShapeDtypeStruct((t_rows, n_cb * 128), F32),
                   jax.ShapeDtypeStruct((8, n_cb * 128), F32)],
        compiler_params=_params(("arbitrary", "arbitrary")),
    )(proj_a, conv_w8, dqkv_hm)


GATE_TILE = 512


def _softplus(y):
    return jnp.maximum(y, 0.0) + jnp.log1p(jnp.exp(-jnp.abs(y)))


def _gate_values(x, prm):
    beta = _sigmoid(x)
    y = x + prm[1:2, :]
    neg_a = -jnp.exp(prm[0:1, :])
    g = neg_a * _softplus(y)
    return beta, y, neg_a, g


def _a_gates_fwd_body(x_ref, prm_ref, gc_ref, gr_ref):
    x = x_ref[...]
    tm = x.shape[0]
    beta, _, _, g = _gate_values(x, prm_ref[...])
    in_chunk = lax.broadcasted_iota(jnp.int32, (tm, 1), 0) % CHUNK
    s = 1
    while s < CHUNK:
        g = g + jnp.where(in_chunk >= s, pltpu.roll(g, s, 0), 0.0)
        s *= 2
    lane = lax.broadcasted_iota(jnp.int32, x.shape, 1)
    out = jnp.where(lane < A_HEADS, beta, jnp.where(lane < 2 * A_HEADS, g, 0.0))
    gc_ref[...] = out
    gr_ref[...] = out.T[0:2 * A_HEADS, :]


def a_gates_fwd(proj_a, prm, n_seq):
    t_rows = proj_a.shape[0]
    s_len = t_rows // n_seq
    tm = min(GATE_TILE, s_len)
    n_t = s_len // tm
    return pl.pallas_call(
        _a_gates_fwd_body, name="a_gates_fwd", grid=(n_seq, n_t),
        in_specs=[pl.BlockSpec((tm, 128), lambda b, i: (b * n_t + i, A_GATE_COL)),
                  pl.BlockSpec((8, 128), lambda b, i: (0, 0))],
        out_specs=[pl.BlockSpec((tm, 128), lambda b, i: (b * n_t + i, 0)),
                   pl.BlockSpec((None, 2 * A_HEADS, tm), lambda b, i: (b, 0, i))],
        out_shape=[jax.ShapeDtypeStruct((t_rows, 128), F32),
                   jax.ShapeDtypeStruct((n_seq, 2 * A_HEADS, s_len), F32)],
        compiler_params=_params(("arbitrary", "arbitrary")),
    )(proj_a, prm)


def _a_gates_bwd_body(x_ref, prm_ref, dgc_ref, dx_ref, dprm_ref):
    first = (pl.program_id(0) == 0) & (pl.program_id(1) == 0)
    x = x_ref[...]
    tm = x.shape[0]
    beta, y, neg_a, g = _gate_values(x, prm_ref[...])
    d = dgc_ref[...]
    in_chunk = lax.broadcasted_iota(jnp.int32, (tm, 1), 0) % CHUNK
    dg = d
    s = 1
    while s < CHUNK:
        dg = dg + jnp.where(in_chunk < CHUNK - s, pltpu.roll(dg, tm - s, 0), 0.0)
        s *= 2
    lane = lax.broadcasted_iota(jnp.int32, x.shape, 1)
    is_decay = (lane >= A_HEADS) & (lane < 2 * A_HEADS)
    d_alogit = jnp.where(is_decay, dg * neg_a * _sigmoid(y), 0.0)
    dx_ref[...] = jnp.where(lane < A_HEADS, d * beta * (1.0 - beta), d_alogit)

    @pl.when(first)
    def _():
        dprm_ref[...] = jnp.zeros_like(dprm_ref)

    dprm_ref[0:1, :] += jnp.sum(jnp.where(is_decay, dg * g, 0.0), axis=0, keepdims=True)
    dprm_ref[1:2, :] += jnp.sum(d_alogit, axis=0, keepdims=True)


def a_gates_bwd(proj_a, prm, dgates_col, n_seq):
    t_rows = proj_a.shape[0]
    s_len = t_rows // n_seq
    tm = min(GATE_TILE, s_len)
    n_t = s_len // tm
    return pl.pallas_call(
        _a_gates_bwd_body, name="a_gates_bwd", grid=(n_seq, n_t),
        in_specs=[pl.BlockSpec((tm, 128), lambda b, i: (b * n_t + i, A_GATE_COL)),
                  pl.BlockSpec((8, 128), lambda b, i: (0, 0)),
                  pl.BlockSpec((tm, 128), lambda b, i: (b * n_t + i, 0))],
        out_specs=[pl.BlockSpec((tm, 128), lambda b, i: (b * n_t + i, 0)),
                   pl.BlockSpec((8, 128), lambda b, i: (0, 0))],
        out_shape=[jax.ShapeDtypeStruct((t_rows, 128), F32),
                   jax.ShapeDtypeStruct((8, 128), F32)],
        compiler_params=_params(("arbitrary", "arbitrary")),
    )(proj_a, prm, dgates_col)


ROW_TILE = 512
A_Z_COL = (2 * A_QK + A_VW) // A_DV


def _silu_parts(z):
    sig = _sigmoid(z)
    return z * sig, sig * (1.0 + z * (1.0 - sig))


def _a_post_fwd_body(o_ref, z_ref, g_ref, og_ref):
    o = o_ref[...]
    r = lax.rsqrt(jnp.mean(o * o, axis=1, keepdims=True) + EPS)
    silu, _ = _silu_parts(z_ref[...])
    og_ref[...] = ((o * r * g_ref[0:1, :]) * silu).astype(og_ref.dtype)


def a_post_fwd(o, proj_a, norm_g8):
    t_rows = o.shape[0]
    tm = min(ROW_TILE, t_rows)
    blk = pl.BlockSpec((tm, A_DV), lambda i, h: (i, h))
    return pl.pallas_call(
        _a_post_fwd_body, name="a_post_fwd", grid=(t_rows // tm, A_HEADS),
        in_specs=[blk, pl.BlockSpec((tm, A_DV), lambda i, h: (i, A_Z_COL + h)),
                  pl.BlockSpec((8, A_DV), lambda i, h: (0, 0))],
        out_specs=blk,
        out_shape=jax.ShapeDtypeStruct((t_rows, A_VW), BF16),
        compiler_params=_params(("arbitrary", "arbitrary")),
    )(o, proj_a, norm_g8)


def _a_post_bwd_body(o_ref, z_ref, g_ref, dog_ref, do_ref, dz_ref, dg_ref):
    first = (pl.program_id(0) == 0) & (pl.program_id(1) == 0)
    o, z, d_og = o_ref[...], z_ref[...], dog_ref[...]
    gain = g_ref[0:1, :]
    r = lax.rsqrt(jnp.mean(o * o, axis=1, keepdims=True) + EPS)
    silu, dsilu = _silu_parts(z)
    xr = o * r
    d_on = d_og * silu
    dz_ref[...] = d_og * (xr * gain) * dsilu
    u = d_on * gain
    do_ref[...] = r * u - xr * (r * r) * jnp.mean(u * o, axis=1, keepdims=True)

    @pl.when(first)
    def _():
        dg_ref[...] = jnp.zeros_like(dg_ref)

    dg_ref[0:1, :] += jnp.sum(d_on * xr, axis=0, keepdims=True)


def a_post_bwd(o, proj_a, norm_g8, d_og):
    t_rows = o.shape[0]
    tm = min(ROW_TILE, t_rows)
    blk = pl.BlockSpec((tm, A_DV), lambda i, h: (i, h))
    gblk = pl.BlockSpec((8, A_DV), lambda i, h: (0, 0))
    return pl.pallas_call(
        _a_post_bwd_body, name="a_post_bwd", grid=(t_rows // tm, A_HEADS),
        in_specs=[blk, pl.BlockSpec((tm, A_DV), lambda i, h: (i, A_Z_COL + h)), gblk, blk],
        out_specs=[blk, blk, gblk],
        out_shape=[jax.ShapeDtypeStruct((t_rows, A_VW), F32),
                   jax.ShapeDtypeStruct((t_rows, A_VW), F32),
                   jax.ShapeDtypeStruct((8, A_DV), F32)],
        compiler_params=_params(("arbitrary", "arbitrary")),
    )(o, proj_a, norm_g8, d_og)


NEG_BIG = -1e30
ATT_SCALE = B_DH ** -0.5


def _swap_rope_halves(x):
    lane = lax.broadcasted_iota(jnp.int32, x.shape, 1)
    return jnp.where(lane < ROPE_HALF, pltpu.roll(x, B_DH - ROPE_HALF, 1),
                     jnp.where(lane < ROPE_DIMS, pltpu.roll(x, ROPE_HALF, 1), 0.0))


def _norm_rope(x, gain, cos_t, sin_t):
    r = lax.rsqrt(jnp.mean(x * x, axis=1, keepdims=True) + EPS)
    xn = x * r * gain
    return xn * cos_t + _swap_rope_halves(xn) * sin_t, r


def _norm_rope_bwd(x, r, gain, cos_t, sin_t, dy):
    d_xn = dy * cos_t + _swap_rope_halves(dy * sin_t)
    xr = x * r
    u = d_xn * gain
    dx = r * u - xr * (r * r) * jnp.mean(u * x, axis=1, keepdims=True)
    return dx, jnp.sum(d_xn * xr, axis=0, keepdims=True)


def _stream_rows(idx, dilation, s_len):
    nb = s_len // dilation // B_BLOCK
    r = idx // nb
    m = idx % nb
    cur = r + m * (B_BLOCK * dilation)
    prev = r + jnp.maximum(m - 1, 0) * (B_BLOCK * dilation)
    return cur, prev, m > 0


def _rows(start, dilation):
    if dilation == 1:
        return pl.ds(start, B_BLOCK)
    return pl.ds(start, B_BLOCK, stride=dilation)


def _block_scores(qb, kc, kp, has_prev):
    qi = lax.broadcasted_iota(jnp.int32, (B_BLOCK, B_BLOCK), 0)
    kj = lax.broadcasted_iota(jnp.int32, (B_BLOCK, B_BLOCK), 1)
    s_c = jnp.where(qi >= kj, _mm_nt(qb, kc) * ATT_SCALE, NEG_BIG)
    s_p = jnp.where((kj >= qi) & has_prev, _mm_nt(qb, kp) * ATT_SCALE, NEG_BIG)
    return s_c, s_p


def _attn_fwd_body(qkv_ref, z_ref, cos_ref, sin_ref, gain_ref, og_ref, o_ref, lse_ref,
                   qn_scr, kn_scr, og_scr, lg_scr):
    head, grp = pl.program_id(1), pl.program_id(2)
    s_len = z_ref.shape[0]
    n_blocks = s_len // B_BLOCK
    cos_t, sin_t = cos_ref[...], sin_ref[...]

    for gi, dil in enumerate(B_DILATIONS):
        @pl.when(grp == gi)
        def _(gi=gi, dil=dil):
            qn_scr[...], _ = _norm_rope(qkv_ref[0], gain_ref[gi:gi + 1, :], cos_t, sin_t)
            kn_scr[...], _ = _norm_rope(qkv_ref[1], gain_ref[B_GROUPS + gi:B_GROUPS + gi + 1, :], cos_t, sin_t)

            def block(idx, carry):
                cur, prev, has_prev = _stream_rows(idx, dil, s_len)
                rc, rp = _rows(cur, dil), _rows(prev, dil)
                s_c, s_p = _block_scores(qn_scr[rc, :], kn_scr[rc, :], kn_scr[rp, :], has_prev)
                mx = jnp.maximum(jnp.max(s_c, axis=1, keepdims=True), jnp.max(s_p, axis=1, keepdims=True))
                p_c, p_p = jnp.exp(s_c - mx), jnp.exp(s_p - mx)
                den = jnp.sum(p_c, axis=1, keepdims=True) + jnp.sum(p_p, axis=1, keepdims=True)
                acc = _mm(p_c, qkv_ref.at[2][rc, :]) + _mm(p_p, qkv_ref.at[2][rp, :])
                og_scr.at[gi][rc, :] = acc / den
                lg_scr.at[gi][rc, :] = jnp.broadcast_to(mx + jnp.log(den), (B_BLOCK, B_DH))
                return carry

            lax.fori_loop(0, n_blocks, block, 0)

    @pl.when(grp == B_GROUPS - 1)
    def _():
        l0, l1, l2 = lg_scr[0], lg_scr[1], lg_scr[2]
        mx = jnp.maximum(jnp.maximum(l0, l1), l2)
        w0, w1, w2 = jnp.exp(l0 - mx), jnp.exp(l1 - mx), jnp.exp(l2 - mx)
        den = w0 + w1 + w2
        o = (w0 * og_scr[0] + w1 * og_scr[1] + w2 * og_scr[2]) / den
        silu, _ = _silu_parts(z_ref[...])
        o_ref[...] = o
        og_ref[...] = (o * silu).astype(og_ref.dtype)
        @pl.when(head == 0)
        def _():
            lse_ref[...] = jnp.zeros_like(lse_ref)

        lane = lax.broadcasted_iota(jnp.int32, o.shape, 1)
        lse_ref[...] = jnp.where(lane == head, mx + jnp.log(den), lse_ref[...])


def attn_fwd(proj_b, cos_t, sin_t, gains8, n_seq):
    t_rows = proj_b.shape[1]
    s_len = t_rows // n_seq
    head_blk = pl.BlockSpec((s_len, B_DH), lambda b, h, g: (b, h))
    seq_blk = pl.BlockSpec((s_len, 128), lambda b, h, g: (b, 0))
    return pl.pallas_call(
        _attn_fwd_body, name="attn_fwd", grid=(n_seq, B_HEADS, B_GROUPS),
        in_specs=[
            pl.BlockSpec((3, s_len, B_DH), lambda b, h, g: (g, b, h)),
            pl.BlockSpec((None, s_len, B_DH), lambda b, h, g: (B_PIECES - 1, b, h)),
            seq_blk, seq_blk,
            pl.BlockSpec((8, 128), lambda b, h, g: (0, 0)),
        ],
        out_specs=[head_blk, head_blk, seq_blk],
        out_shape=[jax.ShapeDtypeStruct((t_rows, B_W), BF16),
                   jax.ShapeDtypeStruct((t_rows, B_W), F32),
                   jax.ShapeDtypeStruct((t_rows, 128), F32)],
        scratch_shapes=[pltpu.VMEM((s_len, B_DH), F32), pltpu.VMEM((s_len, B_DH), F32),
                        pltpu.VMEM((B_GROUPS, s_len, B_DH), F32), pltpu.VMEM((B_GROUPS, s_len, B_DH), F32)],
        compiler_params=_params(("arbitrary", "arbitrary", "arbitrary")),
    )(proj_b, proj_b, cos_t, sin_t, gains8)


def _attn_bwd_body(qkv_ref, z_ref, cos_ref, sin_ref, gain_ref, dog_ref, o_ref, lse_ref,
                   dqkv_ref, dz_ref, dgain_ref,
                   qn_scr, kn_scr, dqn_scr, dkn_scr, do_scr, dl_scr, ls_scr):
    head, grp = pl.program_id(1), pl.program_id(2)
    first = (pl.program_id(0) == 0) & (head == 0) & (grp == 0)
    s_len = z_ref.shape[0]
    n_blocks = s_len // B_BLOCK
    cos_t, sin_t = cos_ref[...], sin_ref[...]

    @pl.when(first)
    def _():
        dgain_ref[...] = jnp.zeros_like(dgain_ref)

    @pl.when(grp == 0)
    def _():
        d_og, o = dog_ref[...], o_ref[...]
        silu, dsilu = _silu_parts(z_ref[...])
        d_o = d_og * silu
        dz_ref[...] = d_og * o * dsilu
        do_scr[...] = d_o
        dl_scr[...] = jnp.broadcast_to(jnp.sum(d_o * o, axis=1, keepdims=True), o.shape)
        lane = lax.broadcasted_iota(jnp.int32, o.shape, 1)
        ls_scr[...] = jnp.broadcast_to(
            jnp.sum(jnp.where(lane == head, lse_ref[...], 0.0), axis=1, keepdims=True), o.shape)

    for gi, dil in enumerate(B_DILATIONS):
        @pl.when(grp == gi)
        def _(gi=gi, dil=dil):
            q_raw, k_raw = qkv_ref[0], qkv_ref[1]
            gq = gain_ref[gi:gi + 1, :]
            gk = gain_ref[B_GROUPS + gi:B_GROUPS + gi + 1, :]
            qn_scr[...], rq = _norm_rope(q_raw, gq, cos_t, sin_t)
            kn_scr[...], rk = _norm_rope(k_raw, gk, cos_t, sin_t)
            dkn_scr[...] = jnp.zeros_like(dkn_scr)
            dqkv_ref[2] = jnp.zeros((s_len, B_DH), F32)

            def block(idx, carry):
                cur, prev, has_prev = _stream_rows(idx, dil, s_len)
                rc, rp = _rows(cur, dil), _rows(prev, dil)
                qb, kc, kp = qn_scr[rc, :], kn_scr[rc, :], kn_scr[rp, :]
                vc, vp = qkv_ref.at[2][rc, :], qkv_ref.at[2][rp, :]
                d_o = do_scr[rc, :]
                lse = ls_scr[rc, :][:, 0:1]
                delta = dl_scr[rc, :][:, 0:1]
                s_c, s_p = _block_scores(qb, kc, kp, has_prev)
                p_c, p_p = jnp.exp(s_c - lse), jnp.exp(s_p - lse)
                ds_c = p_c * (_mm_nt(d_o, vc) - delta)
                ds_p = p_p * (_mm_nt(d_o, vp) - delta)
                dqn_scr[rc, :] = (_mm(ds_c, kc) + _mm(ds_p, kp)) * ATT_SCALE
                dkn_scr[rc, :] += _mm_tn(ds_c, qb) * ATT_SCALE
                dqkv_ref.at[2][rc, :] += _mm_tn(p_c, d_o)

                @pl.when(has_prev)
                def _():
                    dkn_scr[rp, :] += _mm_tn(ds_p, qb) * ATT_SCALE
                    dqkv_ref.at[2][rp, :] += _mm_tn(p_p, d_o)

                return carry

            lax.fori_loop(0, n_blocks, block, 0)
            dq, dgq = _norm_rope_bwd(q_raw, rq, gq, cos_t, sin_t, dqn_scr[...])
            dk, dgk = _norm_rope_bwd(k_raw, rk, gk, cos_t, sin_t, dkn_scr[...])
            dqkv_ref[0] = dq
            dqkv_ref[1] = dk
            dgain_ref[gi:gi + 1, :] += dgq
            dgain_ref[B_GROUPS + gi:B_GROUPS + gi + 1, :] += dgk


def attn_bwd(proj_b, cos_t, sin_t, gains8, d_og, o, lse, n_seq):
    t_rows = proj_b.shape[1]
    s_len = t_rows // n_seq
    head_blk = pl.BlockSpec((s_len, B_DH), lambda b, h, g: (b, h))
    seq_blk = pl.BlockSpec((s_len, 128), lambda b, h, g: (b, 0))
    grp_blk = pl.BlockSpec((3, s_len, B_DH), lambda b, h, g: (g, b, h))
    gain_blk = pl.BlockSpec((8, 128), lambda b, h, g: (0, 0))
    return pl.pallas_call(
        _attn_bwd_body, name="attn_bwd", grid=(n_seq, B_HEADS, B_GROUPS),
        in_specs=[
            grp_blk,
            pl.BlockSpec((None, s_len, B_DH), lambda b, h, g: (B_PIECES - 1, b, h)),
            seq_blk, seq_blk, gain_blk, head_blk, head_blk, seq_blk,
        ],
        out_specs=[grp_blk, head_blk, gain_blk],
        out_shape=[jax.ShapeDtypeStruct((3 * B_GROUPS, t_rows, B_W), F32),
                   jax.ShapeDtypeStruct((t_rows, B_W), F32),
                   jax.ShapeDtypeStruct((8, 128), F32)],
        scratch_shapes=[pltpu.VMEM((s_len, B_DH), F32) for _ in range(7)],
        compiler_params=_params(("arbitrary", "arbitrary", "arbitrary")),
    )(proj_b, proj_b, cos_t, sin_t, gains8, d_og, o, lse)


def rope_tables(positions):
    inv_freq = ROPE_THETA ** (-jnp.arange(0, ROPE_DIMS, 2, dtype=F32) / ROPE_DIMS)
    ang = positions.astype(F32)[:, None] * inv_freq
    cos, sin = jnp.cos(ang), jnp.sin(ang)
    t_rows = positions.shape[0]
    rest = B_DH - ROPE_DIMS
    cos_t = jnp.concatenate([cos, cos, jnp.ones((t_rows, rest), F32)], axis=1)
    sin_t = jnp.concatenate([-sin, sin, jnp.zeros((t_rows, rest), F32)], axis=1)
    return cos_t, sin_t


def _rms_fwd_body(x_ref, g_ref, *rest, layer):
    h_ref = rest[-1]
    x = x_ref[...]
    r = lax.rsqrt(jnp.mean(x * x, axis=1, keepdims=True) + EPS)
    h_ref[...] = (x * r * g_ref[layer:layer + 1, :]).astype(h_ref.dtype)


def rms_fwd(x, gains8, layer, after=None):
    t_rows, d = x.shape
    tm = min(ROW_TILE, t_rows)
    in_specs = [pl.BlockSpec((tm, d), lambda i: (i, 0)), pl.BlockSpec((8, d), lambda i: (0, 0))]
    args = [x, gains8]
    if after is not None:
        in_specs.append(HBM_SPEC)
        args.append(after)
    return pl.pallas_call(
        functools.partial(_rms_fwd_body, layer=layer), name=f"rms_fwd_{layer}", grid=(t_rows // tm,),
        in_specs=in_specs,
        out_specs=pl.BlockSpec((tm, d), lambda i: (i, 0)),
        out_shape=jax.ShapeDtypeStruct((t_rows, d), BF16),
        compiler_params=_params(("arbitrary",)),
    )(*args)


def _rms_bwd_body(x_ref, g_ref, dh_ref, res_ref, dx_ref, dg_ref, *, layer):
    x, dh = x_ref[...], dh_ref[...]
    r = lax.rsqrt(jnp.mean(x * x, axis=1, keepdims=True) + EPS)
    xr = x * r
    u = dh * g_ref[layer:layer + 1, :]
    dx_ref[...] = res_ref[...] + r * u - xr * (r * r) * jnp.mean(u * x, axis=1, keepdims=True)

    @pl.when(pl.program_id(0) == 0)
    def _():
        dg_ref[...] = jnp.zeros_like(dg_ref)

    dg_ref[0:1, :] += jnp.sum(dh * xr, axis=0, keepdims=True)


def rms_bwd(x, gains8, layer, dh, d_res):
    t_rows, d = x.shape
    tm = min(ROW_TILE, t_rows)
    blk = pl.BlockSpec((tm, d), lambda i: (i, 0))
    gblk = pl.BlockSpec((8, d), lambda i: (0, 0))
    return pl.pallas_call(
        functools.partial(_rms_bwd_body, layer=layer), name=f"rms_bwd_{layer}", grid=(t_rows // tm,),
        in_specs=[blk, gblk, blk, blk],
        out_specs=[blk, gblk],
        out_shape=[jax.ShapeDtypeStruct((t_rows, d), F32), jax.ShapeDtypeStruct((8, d), F32)],
        compiler_params=_params(("arbitrary",)),
    )(x, gains8, dh, d_res)


def _piece_col(p):
    return jnp.where(p < 3 * B_GROUPS, (p % 3) * B_GROUPS + p // 3, 3 * B_GROUPS)


def _mm_nn_body(a_ref, w_ref, *rest, has_res):
    o_ref = rest[-1]
    acc = jnp.dot(a_ref[...], w_ref[...], preferred_element_type=F32)
    if has_res:
        acc = acc + rest[0][...]
    o_ref[...] = acc


def mm_nn(a, w, residual=None, *, tn, name):
    m, k = a.shape
    n = w.shape[1]
    tm = min(ROW_TILE, m)
    in_specs = [pl.BlockSpec((tm, k), lambda j, i: (i, 0)), pl.BlockSpec((k, tn), lambda j, i: (0, j))]
    args = [a, w]
    if residual is not None:
        in_specs.append(pl.BlockSpec((tm, tn), lambda j, i: (i, j)))
        args.append(residual)
    return pl.pallas_call(
        functools.partial(_mm_nn_body, has_res=residual is not None), name=name, grid=(n // tn, m // tm),
        in_specs=in_specs,
        out_specs=pl.BlockSpec((tm, tn), lambda j, i: (i, j)),
        out_shape=jax.ShapeDtypeStruct((m, n), F32),
        compiler_params=_params(("arbitrary", "arbitrary")),
    )(*args)


def mm_nn_pieces(a, w, *, name):
    m, k = a.shape
    tm = min(ROW_TILE, m)
    return pl.pallas_call(
        functools.partial(_mm_nn_body, has_res=False), name=name, grid=(B_PIECES, m // tm),
        in_specs=[pl.BlockSpec((tm, k), lambda p, i: (i, 0)),
                  pl.BlockSpec((k, B_W), lambda p, i: (0, _piece_col(p)))],
        out_specs=pl.BlockSpec((None, tm, B_W), lambda p, i: (p, i, 0)),
        out_shape=jax.ShapeDtypeStruct((B_PIECES, m, B_W), F32),
        compiler_params=_params(("arbitrary", "arbitrary")),
    )(a, w)


NT_ROW_TILE = 1024


def _mm_nt_body(g_ref, w_ref, *rest, has_init):
    o_ref = rest[-1]
    j = pl.program_id(1)
    part = lax.dot_general(_bf(g_ref[...]), w_ref[...], (((1,), (1,)), ((), ())), preferred_element_type=F32)

    @pl.when(j == 0)
    def _():
        o_ref[...] = part + rest[0][...] if has_init else part

    @pl.when(j > 0)
    def _():
        o_ref[...] += part


def mm_nt(g, w, init=None, *, tn, col_off=0, name, after=None):
    m, n = g.shape
    k = w.shape[0]
    tm = min(NT_ROW_TILE, m)
    in_specs = [pl.BlockSpec((tm, tn), lambda i, j: (i, j)),
                pl.BlockSpec((k, tn), lambda i, j: (0, col_off + j))]
    args = [g, w]
    if init is not None:
        in_specs.append(pl.BlockSpec((tm, k), lambda i, j: (i, 0)))
        args.append(init)
    if after is not None:
        in_specs.append(HBM_SPEC)
        args.append(after)
    return pl.pallas_call(
        functools.partial(_mm_nt_body, has_init=init is not None), name=name, grid=(m // tm, n // tn),
        in_specs=in_specs,
        out_specs=pl.BlockSpec((tm, k), lambda i, j: (i, 0)),
        out_shape=jax.ShapeDtypeStruct((m, k), F32),
        compiler_params=_params(("arbitrary", "arbitrary")),
    )(*args)


def mm_nt_pieces(g9, w, *, name):
    n_p, m, _ = g9.shape
    k = w.shape[0]
    tm = min(NT_ROW_TILE, m)
    return pl.pallas_call(
        functools.partial(_mm_nt_body, has_init=False), name=name, grid=(m // tm, n_p),
        in_specs=[pl.BlockSpec((None, tm, B_W), lambda i, p: (p, i, 0)),
                  pl.BlockSpec((k, B_W), lambda i, p: (0, _piece_col(p)))],
        out_specs=pl.BlockSpec((tm, k), lambda i, p: (i, 0)),
        out_shape=jax.ShapeDtypeStruct((m, k), F32),
        compiler_params=_params(("arbitrary", "arbitrary")),
    )(g9, w)


def _mm_tn_body(a_ref, g_ref, o_ref):
    o_ref[...] = lax.dot_general(a_ref[...], _bf(g_ref[...]), (((0,), (0,)), ((), ())),
                                 preferred_element_type=F32).astype(o_ref.dtype)


def mm_tn(a, g, *, tn, out_dtype, name):
    m, k = a.shape
    n = g.shape[1]
    return pl.pallas_call(
        _mm_tn_body, name=name, grid=(n // tn,),
        in_specs=[pl.BlockSpec((m, k), lambda j: (0, 0)), pl.BlockSpec((m, tn), lambda j: (0, j))],
        out_specs=pl.BlockSpec((k, tn), lambda j: (0, j)),
        out_shape=jax.ShapeDtypeStruct((k, n), out_dtype),
        compiler_params=_params(("arbitrary",)),
    )(a, g)


B_UNIT = 256
B_IN_COLS = B_PIECES * B_W
B_SHARD_UNITS = B_IN_COLS // N_DEV // B_UNIT


def mm_tn_b_in(a, g9, gz, *, out_dtype, name):
    m, k = a.shape
    per_piece = B_W // B_UNIT
    n_units = B_IN_COLS // B_UNIT

    def g_map(u):
        nat = jnp.minimum(u // per_piece, 3 * B_GROUPS - 1)
        piece = (nat % B_GROUPS) * 3 + nat // B_GROUPS
        return (piece, 0, u % per_piece)

    def body(a_ref, g_ref, z_ref, o_ref):
        u = pl.program_id(0)

        @pl.when(u < 3 * B_GROUPS * per_piece)
        def _():
            _mm_tn_body(a_ref, g_ref, o_ref)

        @pl.when(u >= 3 * B_GROUPS * per_piece)
        def _():
            _mm_tn_body(a_ref, z_ref, o_ref)

    return pl.pallas_call(
        body, name=name, grid=(n_units,),
        in_specs=[pl.BlockSpec((m, k), lambda u: (0, 0)),
                  pl.BlockSpec((None, m, B_UNIT), g_map),
                  pl.BlockSpec((m, B_UNIT), lambda u: (0, jnp.where(u < 3 * B_GROUPS * per_piece, 0, u % per_piece)))],
        out_specs=pl.BlockSpec((None, k, B_UNIT), lambda u: (u // B_SHARD_UNITS, 0, u % B_SHARD_UNITS)),
        out_shape=jax.ShapeDtypeStruct((N_DEV, k, B_IN_COLS // N_DEV), out_dtype),
        compiler_params=_params(("arbitrary",)),
    )(a, g9, gz)


def _loss_body(y_ref, t_ref, dy_ref, loss_ref, acc):
    i = pl.program_id(0)
    d = y_ref.shape[1]
    err = y_ref[...] - t_ref[...]
    dy_ref[...] = err * (1.0 / d)

    @pl.when(i == 0)
    def _():
        acc[...] = jnp.zeros_like(acc)

    acc[...] += jnp.sum(err * err, axis=0, keepdims=True)

    @pl.when(i == pl.num_programs(0) - 1)
    def _():
        total = jnp.sum(acc[...], axis=1, keepdims=True) * (0.5 / d)
        loss_ref[...] = jnp.broadcast_to(total, loss_ref.shape)


def loss_head(y, target):
    t_rows, d = y.shape
    tm = min(ROW_TILE, t_rows)
    blk = pl.BlockSpec((tm, d), lambda i: (i, 0))
    return pl.pallas_call(
        _loss_body, name="loss_head", grid=(t_rows // tm,),
        in_specs=[blk, blk],
        out_specs=[blk, pl.BlockSpec((8, 128), lambda i: (0, 0))],
        out_shape=[jax.ShapeDtypeStruct((t_rows, d), F32), jax.ShapeDtypeStruct((8, 128), F32)],
        scratch_shapes=[pltpu.VMEM((1, d), F32)],
        compiler_params=_params(("arbitrary",)),
    )(y, target)


def _adamw_body(p_ref, w_ref, m_ref, v_ref, g_ref, d_ref, nm_ref, nv_ref):
    g = p_ref[0].astype(F32)
    for s in range(1, N_DEV):
        g = g + p_ref[s].astype(F32)
    w = w_ref[...]
    m = ADAM_B1 * m_ref[...] + (1.0 - ADAM_B1) * g
    v = ADAM_B2 * v_ref[...] + (1.0 - ADAM_B2) * (g * g)
    m_hat = m / (1.0 - ADAM_B1 ** ADAM_STEP)
    v_hat = v / (1.0 - ADAM_B2 ** ADAM_STEP)
    g_ref[...] = g
    d_ref[...] = -ADAM_LR * (m_hat / (jnp.sqrt(v_hat) + ADAM_EPS) + ADAM_WD * w)
    nm_ref[...] = m
    nv_ref[...] = v


def adamw(parts, w, m, v, *, name):
    r, c = w.shape
    tr = r if r <= 256 else 256
    blk = pl.BlockSpec((tr, c), lambda i: (i, 0))
    out = jax.ShapeDtypeStruct((r, c), F32)
    return pl.pallas_call(
        _adamw_body, name=name, grid=(r // tr,),
        in_specs=[pl.BlockSpec((N_DEV, tr, c), lambda i: (0, i, 0)), blk, blk, blk],
        out_specs=[blk, blk, blk, blk],
        out_shape=[out, out, out, out],
        compiler_params=_params(("arbitrary",)),
    )(parts, w, m, v)


MESH_ID = pl.DeviceIdType.MESH
HBM_SPEC = pl.BlockSpec(memory_space=pl.ANY)


def _my_place():
    return lax.axis_index("x"), lax.axis_index("y"), lax.axis_index("c")


def _flat(x, y, c):
    return 4 * x + 2 * y + c


def _all_gather_body(*refs, n):
    ins, outs = refs[:n], refs[n:2 * n]
    send_sems, recv_sems, local_sems = refs[2 * n:]
    x, y, c = _my_place()
    me, sibling = (x, y, c), (x, y, 1 - c)
    chips = [(1 - x, y), (x, 1 - y), (1 - x, 1 - y)]
    pending = []
    for a in range(n):
        src, out = ins[a], outs[a]

        def copy(k, block, to, from_input=False, a=a, src=src, out=out):
            slot = out.at[_flat(*block)]
            return pltpu.make_async_remote_copy(
                src_ref=src if from_input else slot, dst_ref=slot,
                send_sem=send_sems.at[7 * a + k], recv_sem=recv_sems.at[7 * a + k],
                device_id=to, device_id_type=MESH_ID)

        mine = pltpu.make_async_copy(src, out.at[_flat(*me)], local_sems.at[a])
        mine.start()
        first = [copy(0, me, sibling, True)] + [copy(1 + j, me, (*chip, c), True) for j, chip in enumerate(chips)]
        for cp in first:
            cp.start()
        pending.append((copy, mine, first))
    for copy, mine, first in pending:
        passed = [copy(4 + j, (*chip, c), sibling) for j, chip in enumerate(chips)]
        for j, chip in enumerate(chips):
            copy(1 + j, (*chip, c), me).wait_recv()
            passed[j].start()
        copy(0, sibling, me).wait_recv()
        for j, chip in enumerate(chips):
            copy(4 + j, (*chip, 1 - c), me).wait_recv()
        for cp in first + passed:
            cp.wait_send()
        mine.wait()


def all_gather(shards, *, name):
    n = len(shards)
    return pl.pallas_call(
        functools.partial(_all_gather_body, n=n), name=name,
        in_specs=[HBM_SPEC] * n, out_specs=[HBM_SPEC] * n,
        out_shape=[jax.ShapeDtypeStruct((N_DEV,) + s.shape, s.dtype) for s in shards],
        scratch_shapes=[pltpu.SemaphoreType.DMA((7 * n,)), pltpu.SemaphoreType.DMA((7 * n,)),
                        pltpu.SemaphoreType.DMA((n,))],
    )(*shards)


PEER_FLIPS = [(0, 0, 1), (1, 0, 0), (0, 1, 0), (1, 1, 0), (1, 0, 1), (0, 1, 1), (1, 1, 1)]


def _all_to_all_body(*refs, n):
    ins, outs = refs[:n], refs[n:2 * n]
    send_sems, recv_sems, local_sems = refs[2 * n:]
    x, y, c = _my_place()
    me = _flat(x, y, c)
    waits = []
    for a in range(n):
        src, out = ins[a], outs[a]
        mine = pltpu.make_async_copy(src.at[me], out.at[me], local_sems.at[a])
        mine.start()
        waits.append(mine)
        for k, (fx, fy, fc) in enumerate(PEER_FLIPS):
            peer = (1 - x if fx else x, 1 - y if fy else y, 1 - c if fc else c)
            theirs = _flat(*peer)
            sems = dict(send_sem=send_sems.at[7 * a + k], recv_sem=recv_sems.at[7 * a + k],
                        device_id=peer, device_id_type=MESH_ID)
            send = pltpu.make_async_remote_copy(src_ref=src.at[theirs], dst_ref=out.at[me], **sems)
            send.start()
            recv = pltpu.make_async_remote_copy(src_ref=src.at[theirs], dst_ref=out.at[theirs], **sems)
            waits.append((send, recv))
    for w in waits:
        if isinstance(w, tuple):
            w[0].wait_send()
            w[1].wait_recv()
        else:
            w.wait()


def all_to_all(parts, *, name):
    n = len(parts)
    return pl.pallas_call(
        functools.partial(_all_to_all_body, n=n), name=name,
        in_specs=[HBM_SPEC] * n, out_specs=[HBM_SPEC] * n,
        out_shape=[jax.ShapeDtypeStruct(p.shape, p.dtype) for p in parts],
        scratch_shapes=[pltpu.SemaphoreType.DMA((7 * n,)), pltpu.SemaphoreType.DMA((7 * n,)),
                        pltpu.SemaphoreType.DMA((n,))],
    )(*parts)


HBM_ONLY = pl.BlockSpec(memory_space=pltpu.HBM)
SEM_SPEC = pl.BlockSpec(memory_space=pltpu.SEMAPHORE)
DATAFLOW_EFFECT = pltpu.SideEffectType.DATAFLOW_SIDE_EFFECTING


def _split_copies(srcs, lands, send_sems, recv_sems, n, scatter):
    x, y, c = _my_place()
    me = _flat(x, y, c)
    pairs = []
    for a in range(n):
        for k, (fx, fy, fc) in enumerate(PEER_FLIPS):
            peer = (1 - x if fx else x, 1 - y if fy else y, 1 - c if fc else c)
            theirs = _flat(*peer)
            src = srcs[a].at[theirs] if scatter else srcs[a]
            sems = dict(send_sem=send_sems.at[7 * a + k], recv_sem=recv_sems.at[7 * a + k],
                        device_id=peer, device_id_type=MESH_ID)
            pairs.append((pltpu.make_async_remote_copy(src_ref=src, dst_ref=lands[a].at[me], **sems),
                          pltpu.make_async_remote_copy(src_ref=src, dst_ref=lands[a].at[theirs], **sems)))
    return pairs


def _exchange_start_body(*refs, n, scatter):
    srcs, lands = refs[:n], refs[n:2 * n]
    send_sems, recv_sems = refs[2 * n], refs[2 * n + 1]
    token = refs[-1]
    for send, _ in _split_copies(srcs, lands, send_sems, recv_sems, n, scatter):
        send.start()
    token[...] = jnp.zeros_like(token)


def exchange_start(srcs, lands, *, scatter, name):
    n = len(srcs)
    args = [pltpu.with_memory_space_constraint(t, pltpu.HBM) for t in list(srcs) + list(lands)]
    outs = pl.pallas_call(
        functools.partial(_exchange_start_body, n=n, scatter=scatter), name=name,
        out_shape=(pltpu.SemaphoreType.DMA((7 * n,)), pltpu.SemaphoreType.DMA((7 * n,)),
                   *[pltpu.HBM(t.shape, t.dtype) for t in args],
                   jax.ShapeDtypeStruct((8, 128), F32)),
        in_specs=[HBM_ONLY] * (2 * n),
        out_specs=(SEM_SPEC, SEM_SPEC, *[HBM_ONLY] * (2 * n), pl.BlockSpec(memory_space=pltpu.VMEM)),
        input_output_aliases={i: 2 + i for i in range(2 * n)},
        compiler_params=pltpu.CompilerParams(has_side_effects=DATAFLOW_EFFECT),
    )(*args)
    return outs[0], outs[1], outs[2:2 + n], outs[2 + n:2 + 2 * n], outs[-1]


def _exchange_wait_body(*refs, n, scatter):
    srcs, lands = refs[:n], refs[n:2 * n]
    send_sems, recv_sems = refs[2 * n], refs[2 * n + 1]
    for send, recv in _split_copies(srcs, lands, send_sems, recv_sems, n, scatter):
        send.wait_send()
        recv.wait_recv()


def exchange_wait(send_sems, recv_sems, srcs, lands, after, *, scatter, name):
    n = len(srcs)
    outs = pl.pallas_call(
        functools.partial(_exchange_wait_body, n=n, scatter=scatter), name=name,
        out_shape=tuple(pltpu.HBM(t.shape, t.dtype) for t in list(srcs) + list(lands)),
        in_specs=[HBM_ONLY] * (2 * n) + [SEM_SPEC, SEM_SPEC, HBM_SPEC],
        out_specs=tuple([HBM_ONLY] * (2 * n)),
        input_output_aliases={i: i for i in range(2 * n)},
        compiler_params=pltpu.CompilerParams(has_side_effects=DATAFLOW_EFFECT),
    )(*srcs, *lands, send_sems, recv_sems, after)
    return outs[n:]


def _own_slot_only(shape_dtype, own, slot):
    land = lax.empty(shape_dtype.shape, shape_dtype.dtype)
    return lax.dynamic_update_slice(land, own[None], (slot,) + (0,) * own.ndim)


def _pad_rows(a, rows=8):
    return jnp.pad(a, ((0, rows - a.shape[0]), (0, 0)))


def _gate_rows(a_log, dt_bias):
    z = jnp.zeros((8, 128), F32)
    return z.at[0, A_HEADS:2 * A_HEADS].set(a_log[0]).at[1, A_HEADS:2 * A_HEADS].set(dt_bias[0])


def _pack_small(norm_g, a_log, a_dt_bias, a_norm_g, b_q_norm_g, b_k_norm_g):
    return jnp.concatenate([
        norm_g[0].reshape(8, 128), norm_g[1].reshape(8, 128),
        _gate_rows(a_log, a_dt_bias),
        _pad_rows(a_norm_g[0].reshape(2, 128)),
        _pad_rows(jnp.concatenate([b_q_norm_g[0], b_k_norm_g[0]], axis=0)),
    ], axis=0)


def _unpack_small(p):
    return (p[0:16].reshape(2, D_MODEL), p[16:17, A_HEADS:2 * A_HEADS], p[17:18, A_HEADS:2 * A_HEADS],
            p[24:26].reshape(1, A_DV), p[32:35][None], p[35:38][None])


def kernel(x, positions, norm_g, a_w_in, a_conv_w, a_log, a_dt_bias, a_norm_g, a_w_out, b_w_in, b_q_norm_g, b_k_norm_g, b_w_out, loss_target, m_norm_g, m_a_w_in, m_a_conv_w, m_a_log, m_a_dt_bias, m_a_norm_g, m_a_w_out, m_b_w_in, m_b_q_norm_g, m_b_k_norm_g, m_b_w_out, v_norm_g, v_a_w_in, v_a_conv_w, v_a_log, v_a_dt_bias, v_a_norm_g, v_a_w_out, v_b_w_in, v_b_q_norm_g, v_b_k_norm_g, v_b_w_out):
    n_seq, s_len, d = x.shape
    t_rows = n_seq * s_len
    n_chunks = s_len // CHUNK
    x0 = x.reshape(t_rows, d)
    target = loss_target.reshape(t_rows, d)
    my_slot = _flat(*_my_place())

    g_a_in, g_conv = all_gather([a_w_in[0].astype(BF16), _pad_rows(a_conv_w[0])], name="gather_weights_first")
    later = [a_w_out[0].astype(BF16), b_w_in[0].astype(BF16), b_w_out[0].astype(BF16)]
    lands = [_own_slot_only(jax.ShapeDtypeStruct((N_DEV,) + t.shape, t.dtype), t, my_slot) for t in later]
    w_send, w_recv, later, lands, w_token = exchange_start(later, lands, scatter=False, name="gather_weights_start")
    w_a_in = jnp.pad(g_a_in.transpose(1, 0, 2).reshape(d, A_IN), ((0, 0), (0, A_IN_PAD - A_IN)))
    conv_w8 = g_conv.transpose(1, 0, 2).reshape(8, 2 * A_QK + A_VW)

    gains_model = _pad_rows(norm_g)
    gate_prm = _gate_rows(a_log, a_dt_bias)
    gain_a_out = _pad_rows(a_norm_g)
    gains_qk = _pad_rows(jnp.concatenate([b_q_norm_g[0], b_k_norm_g[0]], axis=0))
    cos_t, sin_t = rope_tables(positions.reshape(t_rows))

    h0 = rms_fwd(x0, gains_model, 0, after=w_token)
    proj_a = mm_nn(h0, w_a_in, tn=896, name="proj_a")
    qkv = a_pre_fwd(proj_a, conv_w8, n_seq)
    gates_col, gates_row = a_gates_fwd(proj_a, gate_prm, n_seq)
    gates_row = gates_row.reshape(n_seq, 2 * A_HEADS, s_len // SUPER, 1, SUPER)
    o_a, tinv, states = gdn_fwd(qkv, gates_col, gates_row, n_seq)
    og_a = a_post_fwd(o_a, proj_a, gain_a_out)
    g_a_out, g_b_in, g_b_out = exchange_wait(w_send, w_recv, later, lands, og_a, scatter=False,
                                             name="gather_weights_wait")
    w_a_out = g_a_out.reshape(A_VW, d)
    w_b_in = g_b_in.transpose(1, 0, 2).reshape(d, B_IN_COLS)
    w_b_out = g_b_out.reshape(B_W, d)
    x1 = mm_nn(og_a, w_a_out, x0, tn=512, name="out_a")

    h1 = rms_fwd(x1, gains_model, 1)
    proj_b = mm_nn_pieces(h1, w_b_in, name="proj_b")
    og_b, o_b, lse = attn_fwd(proj_b, cos_t, sin_t, gains_qk, n_seq)
    y = mm_nn(og_b, w_b_out, x1, tn=512, name="out_b")

    dy, loss_blk = loss_head(y, target)
    loss = lax.psum(loss_blk[0, 0], ("x", "y", "c"))

    d_og_b = mm_nt(dy, w_b_out, tn=512, name="d_og_b")
    dw_b_out = mm_tn(og_b, dy, tn=256, out_dtype=BF16, name="dw_b_out")
    dqkv_b, dz_b, d_gains_qk = attn_bwd(proj_b, cos_t, sin_t, gains_qk, d_og_b, o_b, lse, n_seq)
    dh1 = mm_nt_pieces(dqkv_b, w_b_in, name="dh1_qkv")
    dh1 = mm_nt(dz_b, w_b_in, dh1, tn=B_W, col_off=3 * B_GROUPS, name="dh1_z")
    dw_b_in = mm_tn_b_in(h1, dqkv_b, dz_b, out_dtype=BF16, name="dw_b_in")
    dx1, d_gain1 = rms_bwd(x1, gains_model, 1, dh1, dy)

    dw_a_out = mm_tn(og_a, dx1, tn=128, out_dtype=BF16, name="dw_a_out")
    early = [dw_b_in, dw_b_out.reshape(N_DEV, B_W // N_DEV, d), dw_a_out.reshape(N_DEV, A_VW // N_DEV, d)]
    lands = [_own_slot_only(t, lax.dynamic_index_in_dim(t, my_slot, 0, keepdims=False), my_slot) for t in early]
    g_send, g_recv, early, lands, g_token = exchange_start(early, lands, scatter=True, name="scatter_grads_start")

    d_og_a = mm_nt(dx1, w_a_out, tn=512, name="d_og_a", after=g_token)
    do_a, dz_a, d_gain_a_out = a_post_bwd(o_a, proj_a, gain_a_out, d_og_a)
    dqkv_a, dgates = gdn_bwd(qkv, gates_col, gates_row, tinv, states, do_a, n_seq)
    d_pre, d_conv = a_pre_bwd(proj_a, conv_w8, dqkv_a, n_seq)
    d_gate_logits, d_gate_prm = a_gates_bwd(proj_a, gate_prm, dgates, n_seq)
    dh0 = mm_nt(d_pre, w_a_in, tn=512, name="dh0_qkv")
    dh0 = mm_nt(dz_a, w_a_in, dh0, tn=512, col_off=(2 * A_QK + A_VW) // 512, name="dh0_z")
    dh0 = mm_nt(d_gate_logits, w_a_in, dh0, tn=128, col_off=A_GATE_COL, name="dh0_gates")
    dw_a_in = jnp.concatenate([
        mm_tn(h0, d_pre, tn=256, out_dtype=BF16, name="dw_a_in_qkv"),
        mm_tn(h0, dz_a, tn=256, out_dtype=BF16, name="dw_a_in_z"),
        mm_tn(h0, d_gate_logits, tn=128, out_dtype=BF16, name="dw_a_in_gates"),
    ], axis=1)[:, :A_IN]
    dx0, d_gain0 = rms_bwd(x0, gains_model, 0, dh0, dx1)

    shard_a_in = A_IN // N_DEV
    (r_a_in,) = all_to_all([dw_a_in.reshape(d, N_DEV, shard_a_in).transpose(1, 0, 2)], name="scatter_grads_last")
    r_b_in, r_b_out, r_a_out = exchange_wait(g_send, g_recv, early, lands, r_a_in, scatter=True,
                                             name="scatter_grads_wait")
    small = jnp.concatenate([
        d_gain0[0].reshape(8, 128), d_gain1[0].reshape(8, 128), d_gate_prm,
        _pad_rows(d_gain_a_out[0].reshape(2, 128)), d_gains_qk], axis=0)
    r_small, r_conv = all_gather([small, d_conv], name="gather_small_grads")
    conv_cols = a_conv_w.shape[2]
    r_conv = lax.dynamic_slice(r_conv, (0, 0, my_slot * conv_cols), (N_DEV, 8, conv_cols))

    upd = {}
    upd["a_w_in"] = adamw(r_a_in, a_w_in[0], m_a_w_in[0], v_a_w_in[0], name="adamw_a_w_in")
    upd["a_w_out"] = adamw(r_a_out, a_w_out[0], m_a_w_out[0], v_a_w_out[0], name="adamw_a_w_out")
    upd["b_w_in"] = adamw(r_b_in, b_w_in[0], m_b_w_in[0], v_b_w_in[0], name="adamw_b_w_in")
    upd["b_w_out"] = adamw(r_b_out, b_w_out[0], m_b_w_out[0], v_b_w_out[0], name="adamw_b_w_out")
    upd["a_conv_w"] = [t[:A_CONV] for t in adamw(
        r_conv, _pad_rows(a_conv_w[0]), _pad_rows(m_a_conv_w[0]), _pad_rows(v_a_conv_w[0]), name="adamw_a_conv_w")]
    small_upd = adamw(
        r_small,
        _pack_small(norm_g, a_log, a_dt_bias, a_norm_g, b_q_norm_g, b_k_norm_g),
        _pack_small(m_norm_g, m_a_log, m_a_dt_bias, m_a_norm_g, m_b_q_norm_g, m_b_k_norm_g),
        _pack_small(v_norm_g, v_a_log, v_a_dt_bias, v_a_norm_g, v_b_q_norm_g, v_b_k_norm_g),
        name="adamw_small")
    small_names = ("norm_g", "a_log", "a_dt_bias", "a_norm_g", "b_q_norm_g", "b_k_norm_g")
    unpacked = [_unpack_small(t) for t in small_upd]
    for i, nm in enumerate(small_names):
        upd[nm] = [u[i] for u in unpacked]

    order = ("norm_g", "a_w_in", "a_conv_w", "a_log", "a_dt_bias", "a_norm_g", "a_w_out",
             "b_w_in", "b_q_norm_g", "b_k_norm_g", "b_w_out")
    big = ("a_w_in", "a_conv_w", "a_w_out", "b_w_in", "b_w_out")
    outs = [loss, dx0.reshape(n_seq, s_len, d)]
    for kind in range(4):
        for nm in order:
            t = upd[nm][kind]
            outs.append(t[None] if nm in big else t)
    return tuple(outs)
```

```python
import functools
import math

import jax
import jax.numpy as jnp
from jax import lax
from jax.experimental import pallas as pl
from jax.experimental.pallas import tpu as pltpu

F32 = jnp.float32
BF16 = jnp.bfloat16

D_MODEL = 1024
EPS = 1e-6
N_DEV = 8

A_HEADS = 8
A_DK = 128
A_DV = 256
A_QK = A_HEADS * A_DK
A_VW = A_HEADS * A_DV
A_CONV = 4
CHUNK = 64
A_IN = 2 * A_QK + 2 * A_VW + 2 * A_HEADS
A_IN_PAD = 2 * A_QK + 2 * A_VW + 128
A_GATE_COL = (2 * A_QK + 2 * A_VW) // 128

B_DILATIONS = (1, 4, 16)
B_GROUPS = 3
B_HEADS = 8
B_DH = 128
B_W = B_HEADS * B_DH
B_BLOCK = 128
B_PIECES = 3 * B_GROUPS + 1
ROPE_THETA = 500000.0
ROPE_DIMS = B_DH // 4
ROPE_HALF = ROPE_DIMS // 2

ADAM_LR = 0.001
ADAM_B1 = 0.9
ADAM_B2 = 0.999
ADAM_EPS = 1e-08
ADAM_WD = 0.01
ADAM_STEP = 10

VMEM_LIMIT = 56 * 1024 * 1024


def _params(sem):
    return pltpu.CompilerParams(dimension_semantics=sem, vmem_limit_bytes=VMEM_LIMIT)


def _bf(x):
    return x.astype(BF16)


def _mm(a, b):
    return jnp.dot(_bf(a), _bf(b), preferred_element_type=F32)


def _mm_nt(a, b):
    return lax.dot_general(_bf(a), _bf(b), (((1,), (1,)), ((), ())), preferred_element_type=F32)


def _mm_tn(a, b):
    return lax.dot_general(_bf(a), _bf(b), (((0,), (0,)), ((), ())), preferred_element_type=F32)


def _split(x):
    hi = _bf(x)
    return hi, _bf(x - hi.astype(F32))


def _mm3(a, b):
    ah, al = _split(a)
    bh, bl = _split(b)
    d = functools.partial(jnp.dot, preferred_element_type=F32)
    return d(ah, bh) + (d(ah, bl) + d(al, bh))


def _colsum_as_col(z):
    zh, zl = _split(z)
    ones = jnp.ones((z.shape[0], 128), BF16)
    tn = functools.partial(lax.dot_general, dimension_numbers=(((0,), (0,)), ((), ())),
                           preferred_element_type=F32)
    return (tn(zh, ones) + tn(zl, ones))[:, 0:1]


def _sigmoid(x):
    return 1.0 / (1.0 + jnp.exp(-x))


INV_BASE = 8
GDN_GROUP = 4
SUPER = GDN_GROUP * CHUNK

A_K_COL = A_QK // A_DK
A_V_COL = 2 * A_QK // A_DV
A_HEAD_W = 2 * A_DK + A_DV


def _unit_lower_inverse(m, row, col):
    eye = (row == col).astype(F32)
    d = jnp.where(row // INV_BASE == col // INV_BASE, m, 0.0)
    x = eye - d
    p = _mm3(d, d)
    steps = int(math.log2(INV_BASE)) - 1
    for i in range(steps):
        x = x + _mm3(x, p)
        if i + 1 < steps:
            p = _mm3(p, p)
    size = INV_BASE
    while size < CHUNK:
        c = jnp.where((row // (2 * size) == col // (2 * size)) & (row // size != col // size), m, 0.0)
        x = x - _mm3(_mm3(x, c), x)
        size *= 2
    return x


def _diag_blocks_tall(x):
    return jnp.concatenate([x[i * CHUNK:(i + 1) * CHUNK, i * CHUNK:(i + 1) * CHUNK] for i in range(GDN_GROUP)], axis=0)


def _tall_to_block_diag(t, same):
    return jnp.where(same, jnp.concatenate([t] * GDN_GROUP, axis=1), 0.0)


def _block_sum(same, x):
    xh, xl = _split(jnp.broadcast_to(x, (SUPER, 128)))
    ones = same.astype(BF16)
    d = functools.partial(jnp.dot, preferred_element_type=F32)
    return (d(ones, xh) + d(ones, xl))[:, 0:1]


def _super_rows(i):
    return pl.ds(pl.multiple_of(i * SUPER, SUPER), SUPER)


def _chunk_rows(n):
    return pl.ds(pl.multiple_of(n * CHUNK, CHUNK), CHUNK)


def _gdn_super_common(q, k, v, gcb, gr, head, tinv_tall=None):
    lane = lax.broadcasted_iota(jnp.int32, (SUPER, 128), 1)
    row = lax.broadcasted_iota(jnp.int32, (SUPER, SUPER), 0)
    col = lax.broadcasted_iota(jnp.int32, (SUPER, SUPER), 1)
    same = row // CHUNK == col // CHUNK
    beta = jnp.sum(jnp.where(lane == head, gcb, 0.0), axis=1, keepdims=True)
    gc = jnp.sum(jnp.where(lane == A_HEADS + head, gcb, 0.0), axis=1, keepdims=True)
    g_last = jnp.sum(jnp.where(col == (row // CHUNK) * CHUNK + (CHUNK - 1), gr, 0.0), axis=1, keepdims=True)
    gamma = jnp.exp(gc)
    decay = jnp.where(same & (row >= col), jnp.exp(jnp.minimum(gc - gr, 0.0)), 0.0)
    kb = k * beta
    m = jnp.where(same & (row > col), _mm_nt(kb, k) * decay, 0.0)
    if tinv_tall is None:
        tinv = _unit_lower_inverse(m, row, col)
    else:
        tinv = _tall_to_block_diag(tinv_tall, same)
    u = _mm(tinv, v * beta)
    w = _mm(tinv, kb * gamma)
    p = jnp.where(same & (row >= col), _mm_nt(q, k) * decay, 0.0)
    e_tail = jnp.exp(g_last - gc)
    return dict(beta=beta, gc=gc, g_last=g_last, gamma=gamma, decay=decay, kb=kb, m=m,
                tinv=tinv, u=u, w=w, p=p, e_tail=e_tail, row=row, col=col, lane=lane, same=same)


def _store_scan_operands(rows, q, k, t, u_scr, w_scr, p_scr, qg_scr, ke_scr, gl_scr):
    u_scr[rows, :] = t["u"]
    w_scr[rows, :] = _bf(t["w"])
    p_scr[rows, :] = _bf(_diag_blocks_tall(t["p"]))
    qg_scr[rows, :] = _bf(q * t["gamma"])
    ke_scr[rows, :] = _bf(k * t["e_tail"])
    gl_scr[rows, :] = jnp.broadcast_to(jnp.exp(t["g_last"]), (SUPER, 128))


def _gdn_fwd_body(q_ref, k_ref, v_ref, gc_ref, gr_ref, o_ref, tinv_ref, st_ref,
                  s_scr, u_scr, w_scr, p_scr, qg_scr, ke_scr, gl_scr):
    head = pl.program_id(1)
    n_chunks = q_ref.shape[0] // CHUNK

    def prepare(i, carry):
        rows = _super_rows(i)
        q, k = q_ref[rows, :], k_ref[rows, :]
        t = _gdn_super_common(q, k, v_ref[rows, :], gc_ref[rows, :], gr_ref[i], head)
        tinv_ref[rows, :] = _diag_blocks_tall(t["tinv"])
        _store_scan_operands(rows, q, k, t, u_scr, w_scr, p_scr, qg_scr, ke_scr, gl_scr)
        return carry

    lax.fori_loop(0, n_chunks // GDN_GROUP, prepare, 0)
    s_scr[...] = jnp.zeros_like(s_scr)

    def scan(n, carry):
        rows = _chunk_rows(n)
        s = s_scr[...]
        st_ref[n] = s
        sb = _bf(s)
        v_new = u_scr[rows, :] - jnp.dot(w_scr[rows, :], sb, preferred_element_type=F32)
        vb = _bf(v_new)
        o_ref[rows, :] = (jnp.dot(qg_scr[rows, :], sb, preferred_element_type=F32)
                          + jnp.dot(p_scr[rows, :], vb, preferred_element_type=F32))
        s_scr[...] = s * gl_scr[rows, :][0:1, 0:1] + lax.dot_general(
            ke_scr[rows, :], vb, (((0,), (0,)), ((), ())), preferred_element_type=F32)
        return carry

    lax.fori_loop(0, n_chunks, scan, 0)


def _gdn_in_specs(s_len, n_super):
    return [
        pl.BlockSpec((s_len, A_DK), lambda b, h: (b, h)),
        pl.BlockSpec((s_len, A_DK), lambda b, h: (b, A_K_COL + h)),
        pl.BlockSpec((s_len, A_DV), lambda b, h: (b, A_V_COL + h)),
        pl.BlockSpec((s_len, 128), lambda b, h: (b, 0)),
        pl.BlockSpec((None, None, n_super, 1, SUPER), lambda b, h: (b, A_HEADS + h, 0, 0, 0)),
    ]


def _gdn_scan_scratch(s_len):
    return [pltpu.VMEM((A_DK, A_DV), F32), pltpu.VMEM((s_len, A_DV), F32),
            pltpu.VMEM((s_len, A_DK), BF16), pltpu.VMEM((s_len, CHUNK), BF16),
            pltpu.VMEM((s_len, A_DK), BF16), pltpu.VMEM((s_len, A_DK), BF16),
            pltpu.VMEM((s_len, 128), F32)]


def gdn_fwd(qkv, gates_col, gates_row, n_seq):
    t_rows = qkv.shape[0]
    s_len = t_rows // n_seq
    n_chunks = s_len // CHUNK
    return pl.pallas_call(
        _gdn_fwd_body, name="gdn_fwd", grid=(n_seq, A_HEADS),
        in_specs=_gdn_in_specs(s_len, s_len // SUPER),
        out_specs=[
            pl.BlockSpec((s_len, A_DV), lambda b, h: (b, h)),
            pl.BlockSpec((s_len, CHUNK), lambda b, h: (b * A_HEADS + h, 0)),
            pl.BlockSpec((None, n_chunks, A_DK, A_DV), lambda b, h: (b * A_HEADS + h, 0, 0, 0)),
        ],
        out_shape=[
            jax.ShapeDtypeStruct((t_rows, A_VW), F32),
            jax.ShapeDtypeStruct((n_seq * A_HEADS * s_len, CHUNK), F32),
            jax.ShapeDtypeStruct((n_seq * A_HEADS, n_chunks, A_DK, A_DV), F32),
        ],
        scratch_shapes=_gdn_scan_scratch(s_len),
        compiler_params=_params(("arbitrary", "arbitrary")),
    )(qkv, qkv, qkv, gates_col, gates_row)


def _gdn_bwd_body(q_ref, k_ref, v_ref, gc_ref, gr_ref, tinv_ref, st_ref, do_ref,
                  dqkv_ref, dgc_ref,
                  ds_scr, u_scr, w_scr, p_scr, qg_scr, ke_scr, gl_scr,
                  vn_scr, dvn_scr, dqg_scr, dw_scr, dkt_scr, sds_scr):
    head = pl.program_id(1)
    n_chunks = q_ref.shape[0] // CHUNK

    @pl.when(head == 0)
    def _():
        dgc_ref[...] = jnp.zeros_like(dgc_ref)

    def common(i):
        rows = _super_rows(i)
        q, k, v = q_ref[rows, :], k_ref[rows, :], v_ref[rows, :]
        return rows, q, k, v, _gdn_super_common(q, k, v, gc_ref[rows, :], gr_ref[i], head,
                                                tinv_tall=tinv_ref[rows, :])

    def prepare(i, carry):
        rows, q, k, _, t = common(i)
        _store_scan_operands(rows, q, k, t, u_scr, w_scr, p_scr, qg_scr, ke_scr, gl_scr)
        return carry

    lax.fori_loop(0, n_chunks // GDN_GROUP, prepare, 0)
    ds_scr[...] = jnp.zeros_like(ds_scr)
    tn = functools.partial(lax.dot_general, dimension_numbers=(((0,), (0,)), ((), ())), preferred_element_type=F32)
    nt = functools.partial(lax.dot_general, dimension_numbers=(((1,), (1,)), ((), ())), preferred_element_type=F32)

    def scan(i, carry):
        n = n_chunks - 1 - i
        rows = _chunk_rows(n)
        ds_next = ds_scr[...]
        dsb = _bf(ds_next)
        s = st_ref[n]
        sb = _bf(s)
        d_ob = _bf(do_ref[rows, :])
        v_new = u_scr[rows, :] - jnp.dot(w_scr[rows, :], sb, preferred_element_type=F32)
        d_vn = tn(p_scr[rows, :], d_ob) + jnp.dot(ke_scr[rows, :], dsb, preferred_element_type=F32)
        d_vnb = _bf(d_vn)
        vn_scr[rows, :] = v_new
        dvn_scr[rows, :] = d_vn
        dqg_scr[rows, :] = nt(d_ob, sb)
        dw_scr[rows, :] = -nt(d_vnb, sb)
        dkt_scr[rows, :] = nt(_bf(v_new), dsb)
        sds = jnp.sum(jnp.sum(s * ds_next, axis=1, keepdims=True), axis=0, keepdims=True)
        sds_scr[rows, :] = jnp.broadcast_to(sds, (CHUNK, 128))
        ds_scr[...] = (tn(qg_scr[rows, :], d_ob) + gl_scr[rows, :][0:1, 0:1] * ds_next
                       - tn(w_scr[rows, :], d_vnb))
        return carry

    lax.fori_loop(0, n_chunks, scan, 0)

    def finish(i, carry):
        rows, q, k, v, t = common(i)
        beta, gamma, decay, kb, e_tail = t["beta"], t["gamma"], t["decay"], t["kb"], t["e_tail"]
        row, col, lane, same = t["row"], t["col"], t["lane"], t["same"]
        d_o = do_ref[rows, :]
        v_new, d_vn = vn_scr[rows, :], dvn_scr[rows, :]
        d_qg, d_w, d_kt = dqg_scr[rows, :], dw_scr[rows, :], dkt_scr[rows, :]
        gamma_last = jnp.exp(t["g_last"])

        d_p = jnp.where(same & (row >= col), _mm_nt(d_o, v_new), 0.0)
        d_ru = _mm_tn(t["tinv"], d_vn)
        d_rw = _mm_tn(t["tinv"], d_w)
        d_m = jnp.where(same & (row > col), -(_mm_nt(d_ru, t["u"]) + _mm_nt(d_rw, t["w"])), 0.0)

        x_p = d_p * decay
        y_m = d_m * decay
        d_kb = _mm(y_m, k) + d_rw * gamma
        dqkv_ref[rows, 0:A_DK] = _mm(x_p, k) + d_qg * gamma
        dqkv_ref[rows, A_DK:2 * A_DK] = _mm_tn(x_p, q) + _mm_tn(y_m, kb) + d_kb * beta + d_kt * e_tail
        dqkv_ref[rows, 2 * A_DK:A_HEAD_W] = d_ru * beta

        d_beta = (jnp.sum(d_ru * v, axis=1, keepdims=True)
                  + jnp.sum(d_kb * k, axis=1, keepdims=True))
        z = d_p * t["p"] + d_m * t["m"]
        eps_tail = jnp.sum(d_kt * k, axis=1, keepdims=True) * e_tail
        d_gc = (jnp.sum(z, axis=1, keepdims=True) - _colsum_as_col(z)
                + jnp.sum(d_qg * q, axis=1, keepdims=True) * gamma
                + jnp.sum(d_rw * kb, axis=1, keepdims=True) * gamma
                - eps_tail)
        d_glast = _block_sum(same, eps_tail) + gamma_last * sds_scr[rows, :][:, 0:1]
        rcol = lax.broadcasted_iota(jnp.int32, (SUPER, 1), 0)
        d_gc = d_gc + jnp.where(rcol % CHUNK == CHUNK - 1, d_glast, 0.0)
        dgc_ref[rows, :] += (jnp.where(lane == head, d_beta, 0.0)
                             + jnp.where(lane == A_HEADS + head, d_gc, 0.0))
        return carry

    lax.fori_loop(0, n_chunks // GDN_GROUP, finish, 0)


def gdn_bwd(qkv, gates_col, gates_row, tinv, states, d_o, n_seq):
    t_rows = qkv.shape[0]
    s_len = t_rows // n_seq
    n_chunks = s_len // CHUNK
    v_spec = pl.BlockSpec((s_len, A_DV), lambda b, h: (b, h))
    gate_spec = pl.BlockSpec((s_len, 128), lambda b, h: (b, 0))
    return pl.pallas_call(
        _gdn_bwd_body, name="gdn_bwd", grid=(n_seq, A_HEADS),
        in_specs=_gdn_in_specs(s_len, s_len // SUPER) + [
            pl.BlockSpec((s_len, CHUNK), lambda b, h: (b * A_HEADS + h, 0)),
            pl.BlockSpec((None, n_chunks, A_DK, A_DV), lambda b, h: (b * A_HEADS + h, 0, 0, 0)),
            v_spec,
        ],
        out_specs=[pl.BlockSpec((s_len, A_HEAD_W), lambda b, h: (b, h)), gate_spec],
        out_shape=[
            jax.ShapeDtypeStruct((t_rows, A_HEADS * A_HEAD_W), F32),
            jax.ShapeDtypeStruct((t_rows, 128), F32),
        ],
        scratch_shapes=_gdn_scan_scratch(s_len) + [
            pltpu.VMEM((s_len, A_DV), F32), pltpu.VMEM((s_len, A_DV), F32),
            pltpu.VMEM((s_len, A_DK), F32), pltpu.VMEM((s_len, A_DK), F32), pltpu.VMEM((s_len, A_DK), F32),
            pltpu.VMEM((s_len, 128), F32)],
        compiler_params=_params(("arbitrary", "arbitrary")),
    )(qkv, qkv, qkv, gates_col, gates_row, tinv, states, d_o)


def _shift_down(x, j, rows):
    return jnp.where(rows >= j, pltpu.roll(x, j, 0), 0.0)


def _shift_up(x, j, rows):
    n = x.shape[0]
    return jnp.where(rows < n - j, pltpu.roll(x, n - j, 0), 0.0)


def _conv_silu_norm(x, w, cb):
    rows = lax.broadcasted_iota(jnp.int32, (x.shape[0], 1), 0)
    c = x * w[A_CONV - 1:A_CONV, :]
    for j in range(1, A_CONV):
        c = c + _shift_down(x, j, rows) * w[A_CONV - 1 - j:A_CONV - j, :]
    sig = _sigmoid(c)
    a = c * sig
    rn = lax.rsqrt(jnp.sum(a * a, axis=1, keepdims=True) + EPS)
    is_q = cb < A_QK // 128
    is_qk = cb < 2 * A_QK // 128
    q_scale = jnp.where(is_q, A_DK ** -0.5, 1.0).astype(F32)
    return rows, c, sig, a, rn, is_qk, q_scale


def _a_pre_fwd_body(x_ref, w_ref, o_ref):
    cb = pl.program_id(1)
    _, _, _, a, rn, is_qk, q_scale = _conv_silu_norm(x_ref[...], w_ref[...], cb)
    o_ref[...] = a * jnp.where(is_qk, rn * q_scale, 1.0)


def a_pre_fwd(proj_a, conv_w8, n_seq):
    t_rows = proj_a.shape[0]
    s_len = t_rows // n_seq
    n_cb = (2 * A_QK + A_VW) // 128
    blk = pl.BlockSpec((s_len, 128), lambda b, c: (b, c))
    return pl.pallas_call(
        _a_pre_fwd_body, name="a_pre_fwd", grid=(n_seq, n_cb),
        in_specs=[blk, pl.BlockSpec((8, 128), lambda b, c: (0, c))],
        out_specs=blk,
        out_shape=jax.ShapeDtypeStruct((t_rows, n_cb * 128), F32),
        compiler_params=_params(("arbitrary", "arbitrary")),
    )(proj_a, conv_w8)


def _a_pre_bwd_body(x_ref, w_ref, dy_ref, dx_ref, dw_ref):
    b, cb = pl.program_id(1), pl.program_id(0)
    x, w = x_ref[...], w_ref[...]
    rows, c, sig, a, rn, is_qk, q_scale = _conv_silu_norm(x, w, cb)
    dy = dy_ref[...]
    da_n = q_scale * (rn * dy - a * (rn * rn * rn) * jnp.sum(dy * a, axis=1, keepdims=True))
    da = jnp.where(is_qk, da_n, dy)
    dc = da * (sig * (1.0 + c * (1.0 - sig)))
    dx = dc * w[A_CONV - 1:A_CONV, :]
    for j in range(1, A_CONV):
        dx = dx + _shift_up(dc, j, rows) * w[A_CONV - 1 - j:A_CONV - j, :]
    dx_ref[...] = dx.astype(dx_ref.dtype)

    @pl.when(b == 0)
    def _():
        dw_ref[...] = jnp.zeros_like(dw_ref)

    for j in range(A_CONV):
        xs = x if j == 0 else _shift_down(x, j, rows)
        dw_ref[A_CONV - 1 - j:A_CONV - j, :] += jnp.sum(dc * xs, axis=0, keepdims=True)


def a_pre_bwd(proj_a, conv_w8, dqkv_hm, n_seq):
    t_rows = proj_a.shape[0]
    s_len = t_rows // n_seq
    n_cb = (2 * A_QK + A_VW) // 128
    blk = pl.BlockSpec((s_len, 128), lambda c, b: (b, c))
    wblk = pl.BlockSpec((8, 128), lambda c, b: (0, c))
    per_head = A_HEAD_W // 128
    n_q = A_QK // 128

    def head_major(c, b):
        v_blk = jnp.maximum(c - 2 * n_q, 0)
        col = jnp.where(c < n_q, c * per_head,
                        jnp.where(c < 2 * n_q, (c - n_q) * per_head + 1,
                                  (v_blk // 2) * per_head + 2 + v_blk % 2))
        return (b, col)

    return pl.pallas_call(
        _a_pre_bwd_body, name="a_pre_bwd", grid=(n_cb, n_seq),
        in_specs=[blk, wblk, pl.BlockSpec((s_len, 128), head_major)],
        out_specs=[blk, wblk],
        out_shape=[jax.ShapeDtypeStruct((t_rows, n_cb * 128), BF16),
                   jax.ShapeDtypeStruct((8, n_cb * 128), F32)],
        compiler_params=_params(("arbitrary", "arbitrary")),
    )(proj_a, conv_w8, dqkv_hm)


GATE_TILE = 512


def _softplus(y):
    return jnp.maximum(y, 0.0) + jnp.log1p(jnp.exp(-jnp.abs(y)))


def _gate_values(x, prm):
    beta = _sigmoid(x)
    y = x + prm[1:2, :]
    neg_a = -jnp.exp(prm[0:1, :])
    g = neg_a * _softplus(y)
    return beta, y, neg_a, g


def _a_gates_fwd_body(x_ref, prm_ref, gc_ref, gr_ref):
    x = x_ref[...]
    tm = x.shape[0]
    beta, _, _, g = _gate_values(x, prm_ref[...])
    in_chunk = lax.broadcasted_iota(jnp.int32, (tm, 1), 0) % CHUNK
    s = 1
    while s < CHUNK:
        g = g + jnp.where(in_chunk >= s, pltpu.roll(g, s, 0), 0.0)
        s *= 2
    lane = lax.broadcasted_iota(jnp.int32, x.shape, 1)
    out = jnp.where(lane < A_HEADS, beta, jnp.where(lane < 2 * A_HEADS, g, 0.0))
    gc_ref[...] = out
    gr_ref[...] = out.T[0:2 * A_HEADS, :]


def a_gates_fwd(proj_a, prm, n_seq):
    t_rows = proj_a.shape[0]
    s_len = t_rows // n_seq
    tm = min(GATE_TILE, s_len)
    n_t = s_len // tm
    return pl.pallas_call(
        _a_gates_fwd_body, name="a_gates_fwd", grid=(n_seq, n_t),
        in_specs=[pl.BlockSpec((tm, 128), lambda b, i: (b * n_t + i, A_GATE_COL)),
                  pl.BlockSpec((8, 128), lambda b, i: (0, 0))],
        out_specs=[pl.BlockSpec((tm, 128), lambda b, i: (b * n_t + i, 0)),
                   pl.BlockSpec((None, 2 * A_HEADS, tm), lambda b, i: (b, 0, i))],
        out_shape=[jax.ShapeDtypeStruct((t_rows, 128), F32),
                   jax.ShapeDtypeStruct((n_seq, 2 * A_HEADS, s_len), F32)],
        compiler_params=_params(("arbitrary", "arbitrary")),
    )(proj_a, prm)


def _a_gates_bwd_body(x_ref, prm_ref, dgc_ref, dx_ref, dprm_ref):
    first = (pl.program_id(0) == 0) & (pl.program_id(1) == 0)
    x = x_ref[...]
    tm = x.shape[0]
    beta, y, neg_a, g = _gate_values(x, prm_ref[...])
    d = dgc_ref[...]
    in_chunk = lax.broadcasted_iota(jnp.int32, (tm, 1), 0) % CHUNK
    dg = d
    s = 1
    while s < CHUNK:
        dg = dg + jnp.where(in_chunk < CHUNK - s, pltpu.roll(dg, tm - s, 0), 0.0)
        s *= 2
    lane = lax.broadcasted_iota(jnp.int32, x.shape, 1)
    is_decay = (lane >= A_HEADS) & (lane < 2 * A_HEADS)
    d_alogit = jnp.where(is_decay, dg * neg_a * _sigmoid(y), 0.0)
    dx_ref[...] = jnp.where(lane < A_HEADS, d * beta * (1.0 - beta), d_alogit).astype(dx_ref.dtype)

    @pl.when(first)
    def _():
        dprm_ref[...] = jnp.zeros_like(dprm_ref)

    dprm_ref[0:1, :] += jnp.sum(jnp.where(is_decay, dg * g, 0.0), axis=0, keepdims=True)
    dprm_ref[1:2, :] += jnp.sum(d_alogit, axis=0, keepdims=True)


def a_gates_bwd(proj_a, prm, dgates_col, n_seq):
    t_rows = proj_a.shape[0]
    s_len = t_rows // n_seq
    tm = min(GATE_TILE, s_len)
    n_t = s_len // tm
    return pl.pallas_call(
        _a_gates_bwd_body, name="a_gates_bwd", grid=(n_seq, n_t),
        in_specs=[pl.BlockSpec((tm, 128), lambda b, i: (b * n_t + i, A_GATE_COL)),
                  pl.BlockSpec((8, 128), lambda b, i: (0, 0)),
                  pl.BlockSpec((tm, 128), lambda b, i: (b * n_t + i, 0))],
        out_specs=[pl.BlockSpec((tm, 128), lambda b, i: (b * n_t + i, 0)),
                   pl.BlockSpec((8, 128), lambda b, i: (0, 0))],
        out_shape=[jax.ShapeDtypeStruct((t_rows, 128), BF16),
                   jax.ShapeDtypeStruct((8, 128), F32)],
        compiler_params=_params(("arbitrary", "arbitrary")),
    )(proj_a, prm, dgates_col)


ROW_TILE = 512
A_Z_COL = (2 * A_QK + A_VW) // A_DV


def _silu_parts(z):
    sig = _sigmoid(z)
    return z * sig, sig * (1.0 + z * (1.0 - sig))


def _a_post_fwd_body(o_ref, z_ref, g_ref, og_ref):
    o = o_ref[...]
    r = lax.rsqrt(jnp.mean(o * o, axis=1, keepdims=True) + EPS)
    silu, _ = _silu_parts(z_ref[...])
    og_ref[...] = ((o * r * g_ref[0:1, :]) * silu).astype(og_ref.dtype)


def a_post_fwd(o, proj_a, norm_g8):
    t_rows = o.shape[0]
    tm = min(ROW_TILE, t_rows)
    blk = pl.BlockSpec((tm, A_DV), lambda i, h: (i, h))
    return pl.pallas_call(
        _a_post_fwd_body, name="a_post_fwd", grid=(t_rows // tm, A_HEADS),
        in_specs=[blk, pl.BlockSpec((tm, A_DV), lambda i, h: (i, A_Z_COL + h)),
                  pl.BlockSpec((8, A_DV), lambda i, h: (0, 0))],
        out_specs=blk,
        out_shape=jax.ShapeDtypeStruct((t_rows, A_VW), BF16),
        compiler_params=_params(("arbitrary", "arbitrary")),
    )(o, proj_a, norm_g8)


def _a_post_bwd_body(o_ref, z_ref, g_ref, dog_ref, do_ref, dz_ref, dg_ref):
    first = (pl.program_id(0) == 0) & (pl.program_id(1) == 0)
    o, z, d_og = o_ref[...], z_ref[...], dog_ref[...]
    gain = g_ref[0:1, :]
    r = lax.rsqrt(jnp.mean(o * o, axis=1, keepdims=True) + EPS)
    silu, dsilu = _silu_parts(z)
    xr = o * r
    d_on = d_og * silu
    dz_ref[...] = (d_og * (xr * gain) * dsilu).astype(dz_ref.dtype)
    u = d_on * gain
    do_ref[...] = r * u - xr * (r * r) * jnp.mean(u * o, axis=1, keepdims=True)

    @pl.when(first)
    def _():
        dg_ref[...] = jnp.zeros_like(dg_ref)

    dg_ref[0:1, :] += jnp.sum(d_on * xr, axis=0, keepdims=True)


def a_post_bwd(o, proj_a, norm_g8, d_og):
    t_rows = o.shape[0]
    tm = min(ROW_TILE, t_rows)
    blk = pl.BlockSpec((tm, A_DV), lambda i, h: (i, h))
    gblk = pl.BlockSpec((8, A_DV), lambda i, h: (0, 0))
    return pl.pallas_call(
        _a_post_bwd_body, name="a_post_bwd", grid=(t_rows // tm, A_HEADS),
        in_specs=[blk, pl.BlockSpec((tm, A_DV), lambda i, h: (i, A_Z_COL + h)), gblk, blk],
        out_specs=[blk, blk, gblk],
        out_shape=[jax.ShapeDtypeStruct((t_rows, A_VW), F32),
                   jax.ShapeDtypeStruct((t_rows, A_VW), BF16),
                   jax.ShapeDtypeStruct((8, A_DV), F32)],
        compiler_params=_params(("arbitrary", "arbitrary")),
    )(o, proj_a, norm_g8, d_og)


NEG_BIG = -1e30
ATT_SCALE = B_DH ** -0.5


def _swap_rope_halves(x):
    lane = lax.broadcasted_iota(jnp.int32, x.shape, 1)
    return jnp.where(lane < ROPE_HALF, pltpu.roll(x, B_DH - ROPE_HALF, 1),
                     jnp.where(lane < ROPE_DIMS, pltpu.roll(x, ROPE_HALF, 1), 0.0))


def _norm_rope(x, gain, cos_t, sin_t):
    r = lax.rsqrt(jnp.mean(x * x, axis=1, keepdims=True) + EPS)
    xn = x * r * gain
    return xn * cos_t + _swap_rope_halves(xn) * sin_t, r


def _norm_rope_bwd(x, r, gain, cos_t, sin_t, dy):
    d_xn = dy * cos_t + _swap_rope_halves(dy * sin_t)
    xr = x * r
    u = d_xn * gain
    dx = r * u - xr * (r * r) * jnp.mean(u * x, axis=1, keepdims=True)
    return dx, jnp.sum(d_xn * xr, axis=0, keepdims=True)


def _stream_rows(idx, dilation, s_len):
    nb = s_len // dilation // B_BLOCK
    r = idx // nb
    m = idx % nb
    cur = r + m * (B_BLOCK * dilation)
    prev = r + jnp.maximum(m - 1, 0) * (B_BLOCK * dilation)
    return cur, prev, m > 0


def _rows(start, dilation):
    if dilation == 1:
        return pl.ds(start, B_BLOCK)
    return pl.ds(start, B_BLOCK, stride=dilation)


ATT_UNROLL = 4


def _band_mask(has_prev):
    qi = lax.broadcasted_iota(jnp.int32, (B_BLOCK, 2 * B_BLOCK), 0)
    kj = lax.broadcasted_iota(jnp.int32, (B_BLOCK, 2 * B_BLOCK), 1)
    return ((kj < B_BLOCK) & (kj >= qi) & has_prev) | ((kj >= B_BLOCK) & (kj - B_BLOCK <= qi))


def _block_scores(qb, kc, kp, has_prev):
    qi = lax.broadcasted_iota(jnp.int32, (B_BLOCK, B_BLOCK), 0)
    kj = lax.broadcasted_iota(jnp.int32, (B_BLOCK, B_BLOCK), 1)
    s_c = jnp.where(qi >= kj, _mm_nt(qb, kc) * ATT_SCALE, NEG_BIG)
    s_p = jnp.where((kj >= qi) & has_prev, _mm_nt(qb, kp) * ATT_SCALE, NEG_BIG)
    return s_c, s_p


def _attn_fwd_body(qkv_ref, z_ref, cos_ref, sin_ref, gain_ref, og_ref, o_ref, lse_ref,
                   qn_scr, kn_scr, og_scr, lg_scr):
    head, grp = pl.program_id(1), pl.program_id(2)
    s_len = z_ref.shape[0]
    n_blocks = s_len // B_BLOCK
    cos_t, sin_t = cos_ref[...], sin_ref[...]

    for gi, dil in enumerate(B_DILATIONS):
        @pl.when(grp == gi)
        def _(gi=gi, dil=dil):
            qn_scr[...], _ = _norm_rope(qkv_ref[0], gain_ref[gi:gi + 1, :], cos_t, sin_t)
            kn_scr[...], _ = _norm_rope(qkv_ref[1], gain_ref[B_GROUPS + gi:B_GROUPS + gi + 1, :], cos_t, sin_t)

            ones = jnp.ones((2 * B_BLOCK, B_DH), BF16)

            def blocks(it, carry):
                scored = []
                for j in range(ATT_UNROLL):
                    cur, prev, has_prev = _stream_rows(it * ATT_UNROLL + j, dil, s_len)
                    rc, rp = _rows(cur, dil), _rows(prev, dil)
                    k2 = jnp.concatenate([kn_scr[rp, :], kn_scr[rc, :]], axis=0)
                    scored.append((rc, rp, has_prev, _mm_nt(qn_scr[rc, :], k2) * ATT_SCALE))
                summed = []
                for rc, rp, has_prev, s in scored:
                    s = jnp.where(_band_mask(has_prev), s, NEG_BIG)
                    mx = jnp.max(s, axis=1, keepdims=True)
                    v2 = jnp.concatenate([qkv_ref.at[2][rp, :], qkv_ref.at[2][rc, :]], axis=0)
                    acc = jnp.dot(_bf(jnp.exp(s - mx)), jnp.concatenate([_bf(v2), ones], axis=1),
                                  preferred_element_type=F32)
                    summed.append((rc, mx, acc))
                for rc, mx, acc in summed:
                    den = acc[:, B_DH:B_DH + 1]
                    og_scr.at[gi][rc, :] = acc[:, :B_DH] / den
                    lg_scr.at[gi][rc, :] = jnp.broadcast_to(mx + jnp.log(den), (B_BLOCK, B_DH))
                return carry

            lax.fori_loop(0, n_blocks // ATT_UNROLL, blocks, 0)

    @pl.when(grp == B_GROUPS - 1)
    def _():
        l0, l1, l2 = lg_scr[0], lg_scr[1], lg_scr[2]
        mx = jnp.maximum(jnp.maximum(l0, l1), l2)
        w0, w1, w2 = jnp.exp(l0 - mx), jnp.exp(l1 - mx), jnp.exp(l2 - mx)
        den = w0 + w1 + w2
        o = (w0 * og_scr[0] + w1 * og_scr[1] + w2 * og_scr[2]) / den
        silu, _ = _silu_parts(z_ref[...])
        o_ref[...] = o
        og_ref[...] = (o * silu).astype(og_ref.dtype)
        @pl.when(head == 0)
        def _():
            lse_ref[...] = jnp.zeros_like(lse_ref)

        lane = lax.broadcasted_iota(jnp.int32, o.shape, 1)
        lse_ref[...] = jnp.where(lane == head, mx + jnp.log(den), lse_ref[...])


def attn_fwd(proj_b, cos_t, sin_t, gains8, n_seq):
    t_rows = proj_b.shape[1]
    s_len = t_rows // n_seq
    head_blk = pl.BlockSpec((s_len, B_DH), lambda b, h, g: (b, h))
    seq_blk = pl.BlockSpec((s_len, 128), lambda b, h, g: (b, 0))
    return pl.pallas_call(
        _attn_fwd_body, name="attn_fwd", grid=(n_seq, B_HEADS, B_GROUPS),
        in_specs=[
            pl.BlockSpec((3, s_len, B_DH), lambda b, h, g: (g, b, h)),
            pl.BlockSpec((None, s_len, B_DH), lambda b, h, g: (B_PIECES - 1, b, h)),
            seq_blk, seq_blk,
            pl.BlockSpec((8, 128), lambda b, h, g: (0, 0)),
        ],
        out_specs=[head_blk, head_blk, seq_blk],
        out_shape=[jax.ShapeDtypeStruct((t_rows, B_W), BF16),
                   jax.ShapeDtypeStruct((t_rows, B_W), F32),
                   jax.ShapeDtypeStruct((t_rows, 128), F32)],
        scratch_shapes=[pltpu.VMEM((s_len, B_DH), F32), pltpu.VMEM((s_len, B_DH), F32),
                        pltpu.VMEM((B_GROUPS, s_len, B_DH), F32), pltpu.VMEM((B_GROUPS, s_len, B_DH), F32)],
        compiler_params=_params(("arbitrary", "arbitrary", "arbitrary")),
    )(proj_b, proj_b, cos_t, sin_t, gains8)


def _attn_bwd_body(qkv_ref, z_ref, cos_ref, sin_ref, gain_ref, dog_ref, o_ref, lse_ref,
                   dqkv_ref, dz_ref, dgain_ref,
                   qn_scr, kn_scr, dqn_scr, dkn_scr, do_scr, dl_scr, ls_scr, dv_scr):
    head, grp = pl.program_id(1), pl.program_id(2)
    first = (pl.program_id(0) == 0) & (head == 0) & (grp == 0)
    s_len = z_ref.shape[0]
    n_blocks = s_len // B_BLOCK
    cos_t, sin_t = cos_ref[...], sin_ref[...]

    @pl.when(first)
    def _():
        dgain_ref[...] = jnp.zeros_like(dgain_ref)

    @pl.when(grp == 0)
    def _():
        d_og, o = dog_ref[...], o_ref[...]
        silu, dsilu = _silu_parts(z_ref[...])
        d_o = d_og * silu
        dz_ref[...] = (d_og * o * dsilu).astype(dz_ref.dtype)
        do_scr[...] = d_o
        dl_scr[...] = jnp.broadcast_to(jnp.sum(d_o * o, axis=1, keepdims=True), o.shape)
        lane = lax.broadcasted_iota(jnp.int32, o.shape, 1)
        ls_scr[...] = jnp.broadcast_to(
            jnp.sum(jnp.where(lane == head, lse_ref[...], 0.0), axis=1, keepdims=True), o.shape)

    for gi, dil in enumerate(B_DILATIONS):
        @pl.when(grp == gi)
        def _(gi=gi, dil=dil):
            q_raw, k_raw = qkv_ref[0], qkv_ref[1]
            gq = gain_ref[gi:gi + 1, :]
            gk = gain_ref[B_GROUPS + gi:B_GROUPS + gi + 1, :]
            qn_scr[...], rq = _norm_rope(q_raw, gq, cos_t, sin_t)
            kn_scr[...], rk = _norm_rope(k_raw, gk, cos_t, sin_t)
            def blocks(it, carry):
                scored = []
                for j in range(ATT_UNROLL):
                    cur, prev, has_prev = _stream_rows(it * ATT_UNROLL + j, dil, s_len)
                    rc, rp = _rows(cur, dil), _rows(prev, dil)
                    qb, d_ob = _bf(qn_scr[rc, :]), _bf(do_scr[rc, :])
                    k2 = _bf(jnp.concatenate([kn_scr[rp, :], kn_scr[rc, :]], axis=0))
                    v2 = _bf(jnp.concatenate([qkv_ref.at[2][rp, :], qkv_ref.at[2][rc, :]], axis=0))
                    scored.append((rc, rp, has_prev, qb, d_ob, k2,
                                   _mm_nt(qb, k2) * ATT_SCALE, _mm_nt(d_ob, v2)))
                grads = []
                for rc, rp, has_prev, qb, d_ob, k2, s, d_p in scored:
                    p = jnp.exp(jnp.where(_band_mask(has_prev), s - ls_scr[rc, :][:, 0:1], NEG_BIG))
                    ds = _bf(p * (d_p - dl_scr[rc, :][:, 0:1]))
                    grads.append((rc, rp, has_prev,
                                  _mm(ds, k2) * ATT_SCALE, _mm_tn(ds, qb) * ATT_SCALE, _mm_tn(_bf(p), d_ob)))
                for j, (rc, rp, has_prev, dq, dk2, dv2) in enumerate(grads):
                    dqn_scr[rc, :] = dq
                    if j == 0:
                        @pl.when(has_prev)
                        def _():
                            dkn_scr[rp, :] += dk2[:B_BLOCK]
                            dv_scr[rp, :] += dv2[:B_BLOCK]
                    if j + 1 < ATT_UNROLL:
                        dkn_scr[rc, :] = dk2[B_BLOCK:] + grads[j + 1][4][:B_BLOCK]
                        dv_scr[rc, :] = dv2[B_BLOCK:] + grads[j + 1][5][:B_BLOCK]
                    else:
                        dkn_scr[rc, :] = dk2[B_BLOCK:]
                        dv_scr[rc, :] = dv2[B_BLOCK:]
                return carry

            lax.fori_loop(0, n_blocks // ATT_UNROLL, blocks, 0)
            dq, dgq = _norm_rope_bwd(q_raw, rq, gq, cos_t, sin_t, dqn_scr[...])
            dk, dgk = _norm_rope_bwd(k_raw, rk, gk, cos_t, sin_t, dkn_scr[...])
            dqkv_ref[0] = dq.astype(dqkv_ref.dtype)
            dqkv_ref[1] = dk.astype(dqkv_ref.dtype)
            dqkv_ref[2] = dv_scr[...].astype(dqkv_ref.dtype)
            dgain_ref[gi:gi + 1, :] += dgq
            dgain_ref[B_GROUPS + gi:B_GROUPS + gi + 1, :] += dgk


def attn_bwd(proj_b, cos_t, sin_t, gains8, d_og, o, lse, n_seq):
    t_rows = proj_b.shape[1]
    s_len = t_rows // n_seq
    head_blk = pl.BlockSpec((s_len, B_DH), lambda b, h, g: (b, h))
    seq_blk = pl.BlockSpec((s_len, 128), lambda b, h, g: (b, 0))
    grp_blk = pl.BlockSpec((3, s_len, B_DH), lambda b, h, g: (g, b, h))
    gain_blk = pl.BlockSpec((8, 128), lambda b, h, g: (0, 0))
    return pl.pallas_call(
        _attn_bwd_body, name="attn_bwd", grid=(n_seq, B_HEADS, B_GROUPS),
        in_specs=[
            grp_blk,
            pl.BlockSpec((None, s_len, B_DH), lambda b, h, g: (B_PIECES - 1, b, h)),
            seq_blk, seq_blk, gain_blk, head_blk, head_blk, seq_blk,
        ],
        out_specs=[grp_blk, head_blk, gain_blk],
        out_shape=[jax.ShapeDtypeStruct((3 * B_GROUPS, t_rows, B_W), BF16),
                   jax.ShapeDtypeStruct((t_rows, B_W), BF16),
                   jax.ShapeDtypeStruct((8, 128), F32)],
        scratch_shapes=[pltpu.VMEM((s_len, B_DH), F32) for _ in range(8)],
        compiler_params=_params(("arbitrary", "arbitrary", "arbitrary")),
    )(proj_b, proj_b, cos_t, sin_t, gains8, d_og, o, lse)


def rope_tables(positions):
    inv_freq = ROPE_THETA ** (-jnp.arange(0, ROPE_DIMS, 2, dtype=F32) / ROPE_DIMS)
    ang = positions.astype(F32)[:, None] * inv_freq
    cos, sin = jnp.cos(ang), jnp.sin(ang)
    t_rows = positions.shape[0]
    rest = B_DH - ROPE_DIMS
    cos_t = jnp.concatenate([cos, cos, jnp.ones((t_rows, rest), F32)], axis=1)
    sin_t = jnp.concatenate([-sin, sin, jnp.zeros((t_rows, rest), F32)], axis=1)
    return cos_t, sin_t


def _rms_fwd_body(x_ref, g_ref, *rest, layer):
    h_ref = rest[-1]
    x = x_ref[...]
    r = lax.rsqrt(jnp.mean(x * x, axis=1, keepdims=True) + EPS)
    h_ref[...] = (x * r * g_ref[layer:layer + 1, :]).astype(h_ref.dtype)


def rms_fwd(x, gains8, layer, after=None):
    t_rows, d = x.shape
    tm = min(ROW_TILE, t_rows)
    in_specs = [pl.BlockSpec((tm, d), lambda i: (i, 0)), pl.BlockSpec((8, d), lambda i: (0, 0))]
    args = [x, gains8]
    if after is not None:
        in_specs.append(HBM_SPEC)
        args.append(after)
    return pl.pallas_call(
        functools.partial(_rms_fwd_body, layer=layer), name=f"rms_fwd_{layer}", grid=(t_rows // tm,),
        in_specs=in_specs,
        out_specs=pl.BlockSpec((tm, d), lambda i: (i, 0)),
        out_shape=jax.ShapeDtypeStruct((t_rows, d), BF16),
        compiler_params=_params(("arbitrary",)),
    )(*args)


def _rms_bwd_body(x_ref, g_ref, dh_ref, res_ref, dx_ref, dg_ref, *, layer):
    x, dh = x_ref[...], dh_ref[...]
    r = lax.rsqrt(jnp.mean(x * x, axis=1, keepdims=True) + EPS)
    xr = x * r
    u = dh * g_ref[layer:layer + 1, :]
    dx_ref[...] = res_ref[...] + r * u - xr * (r * r) * jnp.mean(u * x, axis=1, keepdims=True)

    @pl.when(pl.program_id(0) == 0)
    def _():
        dg_ref[...] = jnp.zeros_like(dg_ref)

    dg_ref[0:1, :] += jnp.sum(dh * xr, axis=0, keepdims=True)


def rms_bwd(x, gains8, layer, dh, d_res):
    t_rows, d = x.shape
    tm = min(ROW_TILE, t_rows)
    blk = pl.BlockSpec((tm, d), lambda i: (i, 0))
    gblk = pl.BlockSpec((8, d), lambda i: (0, 0))
    return pl.pallas_call(
        functools.partial(_rms_bwd_body, layer=layer), name=f"rms_bwd_{layer}", grid=(t_rows // tm,),
        in_specs=[blk, gblk, blk, blk],
        out_specs=[blk, gblk],
        out_shape=[jax.ShapeDtypeStruct((t_rows, d), F32), jax.ShapeDtypeStruct((8, d), F32)],
        compiler_params=_params(("arbitrary",)),
    )(x, gains8, dh, d_res)


def _piece_col(p):
    return jnp.where(p < 3 * B_GROUPS, (p % 3) * B_GROUPS + p // 3, 3 * B_GROUPS)


def _mm_nn_body(a_ref, w_ref, *rest, has_res):
    o_ref = rest[-1]
    acc = jnp.dot(a_ref[...], w_ref[...], preferred_element_type=F32)
    if has_res:
        acc = acc + rest[0][...]
    o_ref[...] = acc


def mm_nn(a, w, residual=None, *, tn, name):
    m, k = a.shape
    n = w.shape[1]
    tm = min(ROW_TILE, m)
    in_specs = [pl.BlockSpec((tm, k), lambda j, i: (i, 0)), pl.BlockSpec((k, tn), lambda j, i: (0, j))]
    args = [a, w]
    if residual is not None:
        in_specs.append(pl.BlockSpec((tm, tn), lambda j, i: (i, j)))
        args.append(residual)
    return pl.pallas_call(
        functools.partial(_mm_nn_body, has_res=residual is not None), name=name, grid=(n // tn, m // tm),
        in_specs=in_specs,
        out_specs=pl.BlockSpec((tm, tn), lambda j, i: (i, j)),
        out_shape=jax.ShapeDtypeStruct((m, n), F32),
        compiler_params=_params(("arbitrary", "arbitrary")),
    )(*args)


def mm_nn_pieces(a, w, *, name):
    m, k = a.shape
    tm = min(ROW_TILE, m)
    return pl.pallas_call(
        functools.partial(_mm_nn_body, has_res=False), name=name, grid=(B_PIECES, m // tm),
        in_specs=[pl.BlockSpec((tm, k), lambda p, i: (i, 0)),
                  pl.BlockSpec((k, B_W), lambda p, i: (0, _piece_col(p)))],
        out_specs=pl.BlockSpec((None, tm, B_W), lambda p, i: (p, i, 0)),
        out_shape=jax.ShapeDtypeStruct((B_PIECES, m, B_W), F32),
        compiler_params=_params(("arbitrary", "arbitrary")),
    )(a, w)


NT_ROW_TILE = 1024


def _mm_nt_body(g_ref, w_ref, *rest, has_init):
    o_ref = rest[-1]
    j = pl.program_id(1)
    part = lax.dot_general(_bf(g_ref[...]), w_ref[...], (((1,), (1,)), ((), ())), preferred_element_type=F32)

    @pl.when(j == 0)
    def _():
        o_ref[...] = part + rest[0][...] if has_init else part

    @pl.when(j > 0)
    def _():
        o_ref[...] += part


def mm_nt(g, w, init=None, *, tn, col_off=0, name, after=None):
    m, n = g.shape
    k = w.shape[0]
    tm = min(NT_ROW_TILE, m)
    in_specs = [pl.BlockSpec((tm, tn), lambda i, j: (i, j)),
                pl.BlockSpec((k, tn), lambda i, j: (0, col_off + j))]
    args = [g, w]
    if init is not None:
        in_specs.append(pl.BlockSpec((tm, k), lambda i, j: (i, 0)))
        args.append(init)
    if after is not None:
        in_specs.append(HBM_SPEC)
        args.append(after)
    return pl.pallas_call(
        functools.partial(_mm_nt_body, has_init=init is not None), name=name, grid=(m // tm, n // tn),
        in_specs=in_specs,
        out_specs=pl.BlockSpec((tm, k), lambda i, j: (i, 0)),
        out_shape=jax.ShapeDtypeStruct((m, k), F32),
        compiler_params=_params(("arbitrary", "arbitrary")),
    )(*args)


def mm_nt_pieces(g9, w, *, name):
    n_p, m, _ = g9.shape
    k = w.shape[0]
    tm = min(NT_ROW_TILE, m)
    return pl.pallas_call(
        functools.partial(_mm_nt_body, has_init=False), name=name, grid=(m // tm, n_p),
        in_specs=[pl.BlockSpec((None, tm, B_W), lambda i, p: (p, i, 0)),
                  pl.BlockSpec((k, B_W), lambda i, p: (0, _piece_col(p)))],
        out_specs=pl.BlockSpec((tm, k), lambda i, p: (i, 0)),
        out_shape=jax.ShapeDtypeStruct((m, k), F32),
        compiler_params=_params(("arbitrary", "arbitrary")),
    )(g9, w)


def _mm_tn_body(a_ref, g_ref, o_ref):
    o_ref[...] = lax.dot_general(a_ref[...], _bf(g_ref[...]), (((0,), (0,)), ((), ())),
                                 preferred_element_type=F32).astype(o_ref.dtype)


def mm_tn(a, g, *, tn, out_dtype, name):
    m, k = a.shape
    n = g.shape[1]
    return pl.pallas_call(
        _mm_tn_body, name=name, grid=(n // tn,),
        in_specs=[pl.BlockSpec((m, k), lambda j: (0, 0)), pl.BlockSpec((m, tn), lambda j: (0, j))],
        out_specs=pl.BlockSpec((k, tn), lambda j: (0, j)),
        out_shape=jax.ShapeDtypeStruct((k, n), out_dtype),
        compiler_params=_params(("arbitrary",)),
    )(a, g)


B_UNIT = 256
B_IN_COLS = B_PIECES * B_W
B_SHARD_UNITS = B_IN_COLS // N_DEV // B_UNIT


def mm_tn_b_in(a, g9, gz, *, out_dtype, name):
    m, k = a.shape
    per_piece = B_W // B_UNIT
    n_units = B_IN_COLS // B_UNIT

    def g_map(u):
        nat = jnp.minimum(u // per_piece, 3 * B_GROUPS - 1)
        piece = (nat % B_GROUPS) * 3 + nat // B_GROUPS
        return (piece, 0, u % per_piece)

    def body(a_ref, g_ref, z_ref, o_ref):
        u = pl.program_id(0)

        @pl.when(u < 3 * B_GROUPS * per_piece)
        def _():
            _mm_tn_body(a_ref, g_ref, o_ref)

        @pl.when(u >= 3 * B_GROUPS * per_piece)
        def _():
            _mm_tn_body(a_ref, z_ref, o_ref)

    return pl.pallas_call(
        body, name=name, grid=(n_units,),
        in_specs=[pl.BlockSpec((m, k), lambda u: (0, 0)),
                  pl.BlockSpec((None, m, B_UNIT), g_map),
                  pl.BlockSpec((m, B_UNIT), lambda u: (0, jnp.where(u < 3 * B_GROUPS * per_piece, 0, u % per_piece)))],
        out_specs=pl.BlockSpec((None, k, B_UNIT), lambda u: (u // B_SHARD_UNITS, 0, u % B_SHARD_UNITS)),
        out_shape=jax.ShapeDtypeStruct((N_DEV, k, B_IN_COLS // N_DEV), out_dtype),
        compiler_params=_params(("arbitrary",)),
    )(a, g9, gz)


def _loss_body(y_ref, t_ref, dy_ref, loss_ref, acc):
    i = pl.program_id(0)
    d = y_ref.shape[1]
    err = y_ref[...] - t_ref[...]
    dy_ref[...] = err * (1.0 / d)

    @pl.when(i == 0)
    def _():
        acc[...] = jnp.zeros_like(acc)

    acc[...] += jnp.sum(err * err, axis=0, keepdims=True)

    @pl.when(i == pl.num_programs(0) - 1)
    def _():
        total = jnp.sum(acc[...], axis=1, keepdims=True) * (0.5 / d)
        loss_ref[...] = jnp.broadcast_to(total, loss_ref.shape)


def loss_head(y, target):
    t_rows, d = y.shape
    tm = min(ROW_TILE, t_rows)
    blk = pl.BlockSpec((tm, d), lambda i: (i, 0))
    return pl.pallas_call(
        _loss_body, name="loss_head", grid=(t_rows // tm,),
        in_specs=[blk, blk],
        out_specs=[blk, pl.BlockSpec((8, 128), lambda i: (0, 0))],
        out_shape=[jax.ShapeDtypeStruct((t_rows, d), F32), jax.ShapeDtypeStruct((8, 128), F32)],
        scratch_shapes=[pltpu.VMEM((1, d), F32)],
        compiler_params=_params(("arbitrary",)),
    )(y, target)


def _adamw_body(p_ref, w_ref, m_ref, v_ref, g_ref, d_ref, nm_ref, nv_ref):
    g = p_ref[0].astype(F32)
    for s in range(1, N_DEV):
        g = g + p_ref[s].astype(F32)
    w = w_ref[...]
    m = ADAM_B1 * m_ref[...] + (1.0 - ADAM_B1) * g
    v = ADAM_B2 * v_ref[...] + (1.0 - ADAM_B2) * (g * g)
    m_hat = m / (1.0 - ADAM_B1 ** ADAM_STEP)
    v_hat = v / (1.0 - ADAM_B2 ** ADAM_STEP)
    g_ref[...] = g
    d_ref[...] = -ADAM_LR * (m_hat / (jnp.sqrt(v_hat) + ADAM_EPS) + ADAM_WD * w)
    nm_ref[...] = m
    nv_ref[...] = v


def adamw(parts, w, m, v, *, name):
    r, c = w.shape
    tr = r if r <= 256 else 256
    blk = pl.BlockSpec((tr, c), lambda i: (i, 0))
    out = jax.ShapeDtypeStruct((r, c), F32)
    return pl.pallas_call(
        _adamw_body, name=name, grid=(r // tr,),
        in_specs=[pl.BlockSpec((N_DEV, tr, c), lambda i: (0, i, 0)), blk, blk, blk],
        out_specs=[blk, blk, blk, blk],
        out_shape=[out, out, out, out],
        compiler_params=_params(("arbitrary",)),
    )(parts, w, m, v)


MESH_ID = pl.DeviceIdType.MESH
HBM_SPEC = pl.BlockSpec(memory_space=pl.ANY)


def _my_place():
    return lax.axis_index("x"), lax.axis_index("y"), lax.axis_index("c")


def _flat(x, y, c):
    return 4 * x + 2 * y + c


def _all_gather_body(*refs, n):
    ins, outs = refs[:n], refs[n:2 * n]
    send_sems, recv_sems, local_sems = refs[2 * n:]
    x, y, c = _my_place()
    me, sibling = (x, y, c), (x, y, 1 - c)
    chips = [(1 - x, y), (x, 1 - y), (1 - x, 1 - y)]
    pending = []
    for a in range(n):
        src, out = ins[a], outs[a]

        def copy(k, block, to, from_input=False, a=a, src=src, out=out):
            slot = out.at[_flat(*block)]
            return pltpu.make_async_remote_copy(
                src_ref=src if from_input else slot, dst_ref=slot,
                send_sem=send_sems.at[7 * a + k], recv_sem=recv_sems.at[7 * a + k],
                device_id=to, device_id_type=MESH_ID)

        mine = pltpu.make_async_copy(src, out.at[_flat(*me)], local_sems.at[a])
        mine.start()
        first = [copy(0, me, sibling, True)] + [copy(1 + j, me, (*chip, c), True) for j, chip in enumerate(chips)]
        for cp in first:
            cp.start()
        pending.append((copy, mine, first))
    for copy, mine, first in pending:
        passed = [copy(4 + j, (*chip, c), sibling) for j, chip in enumerate(chips)]
        for j, chip in enumerate(chips):
            copy(1 + j, (*chip, c), me).wait_recv()
            passed[j].start()
        copy(0, sibling, me).wait_recv()
        for j, chip in enumerate(chips):
            copy(4 + j, (*chip, 1 - c), me).wait_recv()
        for cp in first + passed:
            cp.wait_send()
        mine.wait()


def all_gather(shards, *, name):
    n = len(shards)
    return pl.pallas_call(
        functools.partial(_all_gather_body, n=n), name=name,
        in_specs=[HBM_SPEC] * n, out_specs=[HBM_SPEC] * n,
        out_shape=[jax.ShapeDtypeStruct((N_DEV,) + s.shape, s.dtype) for s in shards],
        scratch_shapes=[pltpu.SemaphoreType.DMA((7 * n,)), pltpu.SemaphoreType.DMA((7 * n,)),
                        pltpu.SemaphoreType.DMA((n,))],
    )(*shards)


PEER_FLIPS = [(0, 0, 1), (1, 0, 0), (0, 1, 0), (1, 1, 0), (1, 0, 1), (0, 1, 1), (1, 1, 1)]


def _all_to_all_body(*refs, n):
    ins, outs = refs[:n], refs[n:2 * n]
    send_sems, recv_sems, local_sems = refs[2 * n:]
    x, y, c = _my_place()
    me = _flat(x, y, c)
    waits = []
    for a in range(n):
        src, out = ins[a], outs[a]
        mine = pltpu.make_async_copy(src.at[me], out.at[me], local_sems.at[a])
        mine.start()
        waits.append(mine)
        for k, (fx, fy, fc) in enumerate(PEER_FLIPS):
            peer = (1 - x if fx else x, 1 - y if fy else y, 1 - c if fc else c)
            theirs = _flat(*peer)
            sems = dict(send_sem=send_sems.at[7 * a + k], recv_sem=recv_sems.at[7 * a + k],
                        device_id=peer, device_id_type=MESH_ID)
            send = pltpu.make_async_remote_copy(src_ref=src.at[theirs], dst_ref=out.at[me], **sems)
            send.start()
            recv = pltpu.make_async_remote_copy(src_ref=src.at[theirs], dst_ref=out.at[theirs], **sems)
            waits.append((send, recv))
    for w in waits:
        if isinstance(w, tuple):
            w[0].wait_send()
            w[1].wait_recv()
        else:
            w.wait()


def all_to_all(parts, *, name):
    n = len(parts)
    return pl.pallas_call(
        functools.partial(_all_to_all_body, n=n), name=name,
        in_specs=[HBM_SPEC] * n, out_specs=[HBM_SPEC] * n,
        out_shape=[jax.ShapeDtypeStruct(p.shape, p.dtype) for p in parts],
        scratch_shapes=[pltpu.SemaphoreType.DMA((7 * n,)), pltpu.SemaphoreType.DMA((7 * n,)),
                        pltpu.SemaphoreType.DMA((n,))],
    )(*parts)


HBM_ONLY = pl.BlockSpec(memory_space=pltpu.HBM)
SEM_SPEC = pl.BlockSpec(memory_space=pltpu.SEMAPHORE)
DATAFLOW_EFFECT = pltpu.SideEffectType.DATAFLOW_SIDE_EFFECTING


def _split_copies(srcs, lands, send_sems, recv_sems, n, scatter):
    x, y, c = _my_place()
    me = _flat(x, y, c)
    pairs = []
    for a in range(n):
        for k, (fx, fy, fc) in enumerate(PEER_FLIPS):
            peer = (1 - x if fx else x, 1 - y if fy else y, 1 - c if fc else c)
            theirs = _flat(*peer)
            src = srcs[a].at[theirs] if scatter else srcs[a]
            sems = dict(send_sem=send_sems.at[7 * a + k], recv_sem=recv_sems.at[7 * a + k],
                        device_id=peer, device_id_type=MESH_ID)
            pairs.append((pltpu.make_async_remote_copy(src_ref=src, dst_ref=lands[a].at[me], **sems),
                          pltpu.make_async_remote_copy(src_ref=src, dst_ref=lands[a].at[theirs], **sems)))
    return pairs


def _exchange_start_body(*refs, n, scatter):
    srcs, lands = refs[:n], refs[n:2 * n]
    send_sems, recv_sems = refs[2 * n], refs[2 * n + 1]
    token = refs[-1]
    for send, _ in _split_copies(srcs, lands, send_sems, recv_sems, n, scatter):
        send.start()
    token[...] = jnp.zeros_like(token)


def exchange_start(srcs, lands, *, scatter, name):
    n = len(srcs)
    args = [pltpu.with_memory_space_constraint(t, pltpu.HBM) for t in list(srcs) + list(lands)]
    outs = pl.pallas_call(
        functools.partial(_exchange_start_body, n=n, scatter=scatter), name=name,
        out_shape=(pltpu.SemaphoreType.DMA((7 * n,)), pltpu.SemaphoreType.DMA((7 * n,)),
                   *[pltpu.HBM(t.shape, t.dtype) for t in args],
                   jax.ShapeDtypeStruct((8, 128), F32)),
        in_specs=[HBM_ONLY] * (2 * n),
        out_specs=(SEM_SPEC, SEM_SPEC, *[HBM_ONLY] * (2 * n), pl.BlockSpec(memory_space=pltpu.VMEM)),
        input_output_aliases={i: 2 + i for i in range(2 * n)},
        compiler_params=pltpu.CompilerParams(has_side_effects=DATAFLOW_EFFECT),
    )(*args)
    return outs[0], outs[1], outs[2:2 + n], outs[2 + n:2 + 2 * n], outs[-1]


def _exchange_wait_body(*refs, n, scatter):
    srcs, lands = refs[:n], refs[n:2 * n]
    send_sems, recv_sems = refs[2 * n], refs[2 * n + 1]
    for send, recv in _split_copies(srcs, lands, send_sems, recv_sems, n, scatter):
        send.wait_send()
        recv.wait_recv()


def exchange_wait(send_sems, recv_sems, srcs, lands, after, *, scatter, name):
    n = len(srcs)
    outs = pl.pallas_call(
        functools.partial(_exchange_wait_body, n=n, scatter=scatter), name=name,
        out_shape=tuple(pltpu.HBM(t.shape, t.dtype) for t in list(srcs) + list(lands)),
        in_specs=[HBM_ONLY] * (2 * n) + [SEM_SPEC, SEM_SPEC, HBM_SPEC],
        out_specs=tuple([HBM_ONLY] * (2 * n)),
        input_output_aliases={i: i for i in range(2 * n)},
        compiler_params=pltpu.CompilerParams(has_side_effects=DATAFLOW_EFFECT),
    )(*srcs, *lands, send_sems, recv_sems, after)
    return outs[n:]


def _own_slot_only(shape_dtype, own, slot):
    land = lax.empty(shape_dtype.shape, shape_dtype.dtype)
    return lax.dynamic_update_slice(land, own[None], (slot,) + (0,) * own.ndim)


def _pad_rows(a, rows=8):
    return jnp.pad(a, ((0, rows - a.shape[0]), (0, 0)))


def _gate_rows(a_log, dt_bias):
    z = jnp.zeros((8, 128), F32)
    return z.at[0, A_HEADS:2 * A_HEADS].set(a_log[0]).at[1, A_HEADS:2 * A_HEADS].set(dt_bias[0])


def _pack_small(norm_g, a_log, a_dt_bias, a_norm_g, b_q_norm_g, b_k_norm_g):
    return jnp.concatenate([
        norm_g[0].reshape(8, 128), norm_g[1].reshape(8, 128),
        _gate_rows(a_log, a_dt_bias),
        _pad_rows(a_norm_g[0].reshape(2, 128)),
        _pad_rows(jnp.concatenate([b_q_norm_g[0], b_k_norm_g[0]], axis=0)),
    ], axis=0)


def _unpack_small(p):
    return (p[0:16].reshape(2, D_MODEL), p[16:17, A_HEADS:2 * A_HEADS], p[17:18, A_HEADS:2 * A_HEADS],
            p[24:26].reshape(1, A_DV), p[32:35][None], p[35:38][None])


def kernel(x, positions, norm_g, a_w_in, a_conv_w, a_log, a_dt_bias, a_norm_g, a_w_out, b_w_in, b_q_norm_g, b_k_norm_g, b_w_out, loss_target, m_norm_g, m_a_w_in, m_a_conv_w, m_a_log, m_a_dt_bias, m_a_norm_g, m_a_w_out, m_b_w_in, m_b_q_norm_g, m_b_k_norm_g, m_b_w_out, v_norm_g, v_a_w_in, v_a_conv_w, v_a_log, v_a_dt_bias, v_a_norm_g, v_a_w_out, v_b_w_in, v_b_q_norm_g, v_b_k_norm_g, v_b_w_out):
    n_seq, s_len, d = x.shape
    t_rows = n_seq * s_len
    n_chunks = s_len // CHUNK
    x0 = x.reshape(t_rows, d)
    target = loss_target.reshape(t_rows, d)
    my_slot = _flat(*_my_place())

    g_a_in, g_conv = all_gather([a_w_in[0].astype(BF16), _pad_rows(a_conv_w[0])], name="gather_weights_first")
    later = [a_w_out[0].astype(BF16), b_w_in[0].astype(BF16), b_w_out[0].astype(BF16)]
    lands = [_own_slot_only(jax.ShapeDtypeStruct((N_DEV,) + t.shape, t.dtype), t, my_slot) for t in later]
    w_send, w_recv, later, lands, w_token = exchange_start(later, lands, scatter=False, name="gather_weights_start")
    w_a_in = jnp.pad(g_a_in.transpose(1, 0, 2).reshape(d, A_IN), ((0, 0), (0, A_IN_PAD - A_IN)))
    conv_w8 = g_conv.transpose(1, 0, 2).reshape(8, 2 * A_QK + A_VW)

    gains_model = _pad_rows(norm_g)
    gate_prm = _gate_rows(a_log, a_dt_bias)
    gain_a_out = _pad_rows(a_norm_g)
    gains_qk = _pad_rows(jnp.concatenate([b_q_norm_g[0], b_k_norm_g[0]], axis=0))
    cos_t, sin_t = rope_tables(positions.reshape(t_rows))

    h0 = rms_fwd(x0, gains_model, 0, after=w_token)
    proj_a = mm_nn(h0, w_a_in, tn=896, name="proj_a")
    qkv = a_pre_fwd(proj_a, conv_w8, n_seq)
    gates_col, gates_row = a_gates_fwd(proj_a, gate_prm, n_seq)
    gates_row = gates_row.reshape(n_seq, 2 * A_HEADS, s_len // SUPER, 1, SUPER)
    o_a, tinv, states = gdn_fwd(qkv, gates_col, gates_row, n_seq)
    og_a = a_post_fwd(o_a, proj_a, gain_a_out)
    g_a_out, g_b_in, g_b_out = exchange_wait(w_send, w_recv, later, lands, og_a, scatter=False,
                                             name="gather_weights_wait")
    w_a_out = g_a_out.reshape(A_VW, d)
    w_b_in = g_b_in.transpose(1, 0, 2).reshape(d, B_IN_COLS)
    w_b_out = g_b_out.reshape(B_W, d)
    x1 = mm_nn(og_a, w_a_out, x0, tn=512, name="out_a")

    h1 = rms_fwd(x1, gains_model, 1)
    proj_b = mm_nn_pieces(h1, w_b_in, name="proj_b")
    og_b, o_b, lse = attn_fwd(proj_b, cos_t, sin_t, gains_qk, n_seq)
    y = mm_nn(og_b, w_b_out, x1, tn=512, name="out_b")

    dy, loss_blk = loss_head(y, target)
    loss = lax.psum(loss_blk[0, 0], ("x", "y", "c"))

    d_og_b = mm_nt(dy, w_b_out, tn=512, name="d_og_b")
    dw_b_out = mm_tn(og_b, dy, tn=256, out_dtype=BF16, name="dw_b_out")
    dqkv_b, dz_b, d_gains_qk = attn_bwd(proj_b, cos_t, sin_t, gains_qk, d_og_b, o_b, lse, n_seq)
    dh1 = mm_nt_pieces(dqkv_b, w_b_in, name="dh1_qkv")
    dh1 = mm_nt(dz_b, w_b_in, dh1, tn=B_W, col_off=3 * B_GROUPS, name="dh1_z")
    dw_b_in = mm_tn_b_in(h1, dqkv_b, dz_b, out_dtype=BF16, name="dw_b_in")
    dx1, d_gain1 = rms_bwd(x1, gains_model, 1, dh1, dy)

    dw_a_out = mm_tn(og_a, dx1, tn=128, out_dtype=BF16, name="dw_a_out")
    early = [dw_b_in, dw_b_out.reshape(N_DEV, B_W // N_DEV, d), dw_a_out.reshape(N_DEV, A_VW // N_DEV, d)]
    lands = [_own_slot_only(t, lax.dynamic_index_in_dim(t, my_slot, 0, keepdims=False), my_slot) for t in early]
    g_send, g_recv, early, lands, g_token = exchange_start(early, lands, scatter=True, name="scatter_grads_start")

    d_og_a = mm_nt(dx1, w_a_out, tn=512, name="d_og_a", after=g_token)
    do_a, dz_a, d_gain_a_out = a_post_bwd(o_a, proj_a, gain_a_out, d_og_a)
    dqkv_a, dgates = gdn_bwd(qkv, gates_col, gates_row, tinv, states, do_a, n_seq)
    d_pre, d_conv = a_pre_bwd(proj_a, conv_w8, dqkv_a, n_seq)
    d_gate_logits, d_gate_prm = a_gates_bwd(proj_a, gate_prm, dgates, n_seq)
    dh0 = mm_nt(d_pre, w_a_in, tn=512, name="dh0_qkv")
    dh0 = mm_nt(dz_a, w_a_in, dh0, tn=512, col_off=(2 * A_QK + A_VW) // 512, name="dh0_z")
    dh0 = mm_nt(d_gate_logits, w_a_in, dh0, tn=128, col_off=A_GATE_COL, name="dh0_gates")
    dw_a_in = jnp.concatenate([
        mm_tn(h0, d_pre, tn=256, out_dtype=BF16, name="dw_a_in_qkv"),
        mm_tn(h0, dz_a, tn=256, out_dtype=BF16, name="dw_a_in_z"),
        mm_tn(h0, d_gate_logits, tn=128, out_dtype=BF16, name="dw_a_in_gates"),
    ], axis=1)[:, :A_IN]
    dx0, d_gain0 = rms_bwd(x0, gains_model, 0, dh0, dx1)

    shard_a_in = A_IN // N_DEV
    (r_a_in,) = all_to_all([dw_a_in.reshape(d, N_DEV, shard_a_in).transpose(1, 0, 2)], name="scatter_grads_last")
    r_b_in, r_b_out, r_a_out = exchange_wait(g_send, g_recv, early, lands, r_a_in, scatter=True,
                                             name="scatter_grads_wait")
    small = jnp.concatenate([
        d_gain0[0].reshape(8, 128), d_gain1[0].reshape(8, 128), d_gate_prm,
        _pad_rows(d_gain_a_out[0].reshape(2, 128)), d_gains_qk], axis=0)
    r_small, r_conv = all_gather([small, d_conv], name="gather_small_grads")
    conv_cols = a_conv_w.shape[2]
    r_conv = lax.dynamic_slice(r_conv, (0, 0, my_slot * conv_cols), (N_DEV, 8, conv_cols))

    upd = {}
    upd["a_w_in"] = adamw(r_a_in, a_w_in[0], m_a_w_in[0], v_a_w_in[0], name="adamw_a_w_in")
    upd["a_w_out"] = adamw(r_a_out, a_w_out[0], m_a_w_out[0], v_a_w_out[0], name="adamw_a_w_out")
    upd["b_w_in"] = adamw(r_b_in, b_w_in[0], m_b_w_in[0], v_b_w_in[0], name="adamw_b_w_in")
    upd["b_w_out"] = adamw(r_b_out, b_w_out[0], m_b_w_out[0], v_b_w_out[0], name="adamw_b_w_out")
    upd["a_conv_w"] = [t[:A_CONV] for t in adamw(
        r_conv, _pad_rows(a_conv_w[0]), _pad_rows(m_a_conv_w[0]), _pad_rows(v_a_conv_w[0]), name="adamw_a_conv_w")]
    small_upd = adamw(
        r_small,
        _pack_small(norm_g, a_log, a_dt_bias, a_norm_g, b_q_norm_g, b_k_norm_g),
        _pack_small(m_norm_g, m_a_log, m_a_dt_bias, m_a_norm_g, m_b_q_norm_g, m_b_k_norm_g),
        _pack_small(v_norm_g, v_a_log, v_a_dt_bias, v_a_norm_g, v_b_q_norm_g, v_b_k_norm_g),
        name="adamw_small")
    small_names = ("norm_g", "a_log", "a_dt_bias", "a_norm_g", "b_q_norm_g", "b_k_norm_g")
    unpacked = [_unpack_small(t) for t in small_upd]
    for i, nm in enumerate(small_names):
        upd[nm] = [u[i] for u in unpacked]

    order = ("norm_g", "a_w_in", "a_conv_w", "a_log", "a_dt_bias", "a_norm_g", "a_w_out",
             "b_w_in", "b_q_norm_g", "b_k_norm_g", "b_w_out")
    big = ("a_w_in", "a_conv_w", "a_w_out", "b_w_in", "b_w_out")
    outs = [loss, dx0.reshape(n_seq, s_len, d)]
    for kind in range(4):
        for nm in order:
            t = upd[nm][kind]
            outs.append(t[None] if nm in big else t)
    return tuple(outs)
```

```python
import functools
import math

import jax
import jax.numpy as jnp
from jax import lax
from jax.experimental import pallas as pl
from jax.experimental.pallas import tpu as pltpu

F32 = jnp.float32
BF16 = jnp.bfloat16

D_MODEL = 1024
EPS = 1e-6
N_DEV = 8

A_HEADS = 8
A_DK = 128
A_DV = 256
A_QK = A_HEADS * A_DK
A_VW = A_HEADS * A_DV
A_CONV = 4
CHUNK = 64
A_IN = 2 * A_QK + 2 * A_VW + 2 * A_HEADS
A_IN_PAD = 2 * A_QK + 2 * A_VW + 128
A_GATE_COL = (2 * A_QK + 2 * A_VW) // 128

B_DILATIONS = (1, 4, 16)
B_GROUPS = 3
B_HEADS = 8
B_DH = 128
B_W = B_HEADS * B_DH
B_BLOCK = 128
B_PIECES = 3 * B_GROUPS + 1
ROPE_THETA = 500000.0
ROPE_DIMS = B_DH // 4
ROPE_HALF = ROPE_DIMS // 2

ADAM_LR = 0.001
ADAM_B1 = 0.9
ADAM_B2 = 0.999
ADAM_EPS = 1e-08
ADAM_WD = 0.01
ADAM_STEP = 10

VMEM_LIMIT = 56 * 1024 * 1024


def _params(sem):
    return pltpu.CompilerParams(dimension_semantics=sem, vmem_limit_bytes=VMEM_LIMIT)


def _bf(x):
    return x.astype(BF16)


def _mm(a, b):
    return jnp.dot(_bf(a), _bf(b), preferred_element_type=F32)


def _mm_nt(a, b):
    return lax.dot_general(_bf(a), _bf(b), (((1,), (1,)), ((), ())), preferred_element_type=F32)


def _mm_tn(a, b):
    return lax.dot_general(_bf(a), _bf(b), (((0,), (0,)), ((), ())), preferred_element_type=F32)


def _split(x):
    hi = _bf(x)
    return hi, _bf(x - hi.astype(F32))


def _mm3(a, b):
    ah, al = _split(a)
    bh, bl = _split(b)
    d = functools.partial(jnp.dot, preferred_element_type=F32)
    return d(ah, bh) + (d(ah, bl) + d(al, bh))


def _colsum_as_col(z):
    zh, zl = _split(z)
    ones = jnp.ones((z.shape[0], 128), BF16)
    tn = functools.partial(lax.dot_general, dimension_numbers=(((0,), (0,)), ((), ())),
                           preferred_element_type=F32)
    return (tn(zh, ones) + tn(zl, ones))[:, 0:1]


def _sigmoid(x):
    return 1.0 / (1.0 + jnp.exp(-x))


INV_BASE = 8
GDN_GROUP = 4
SUPER = GDN_GROUP * CHUNK

A_K_COL = A_QK // A_DK
A_V_COL = 2 * A_QK // A_DV
A_HEAD_W = 2 * A_DK + A_DV


def _inverse_steps(m, row, col):
    eye = (row == col).astype(F32)
    d = jnp.where(row // INV_BASE == col // INV_BASE, m, 0.0)
    x = eye - d
    p = _mm3(d, d)
    yield
    steps = int(math.log2(INV_BASE)) - 1
    for i in range(steps):
        x = x + _mm3(x, p)
        if i + 1 < steps:
            p = _mm3(p, p)
        yield
    size = INV_BASE
    while size < CHUNK:
        c = jnp.where((row // (2 * size) == col // (2 * size)) & (row // size != col // size), m, 0.0)
        xc = _mm3(x, c)
        yield
        x = x - _mm3(xc, x)
        yield
        size *= 2
    return x


def _drain(gen):
    while True:
        try:
            next(gen)
        except StopIteration as stop:
            return stop.value


def _interleave(*gens):
    live = list(gens)
    while live:
        for g in list(live):
            try:
                next(g)
            except StopIteration:
                live.remove(g)


def _diag_blocks_tall(x):
    return jnp.concatenate([x[i * CHUNK:(i + 1) * CHUNK, i * CHUNK:(i + 1) * CHUNK] for i in range(GDN_GROUP)], axis=0)


def _tall_to_block_diag(t, same):
    return jnp.where(same, jnp.concatenate([t] * GDN_GROUP, axis=1), 0.0)


def _block_sum(same, x):
    xh, xl = _split(jnp.broadcast_to(x, (SUPER, 128)))
    ones = same.astype(BF16)
    d = functools.partial(jnp.dot, preferred_element_type=F32)
    return (d(ones, xh) + d(ones, xl))[:, 0:1]


def _super_rows(i):
    return pl.ds(pl.multiple_of(i * SUPER, SUPER), SUPER)


def _chunk_rows(n):
    return pl.ds(pl.multiple_of(n * CHUNK, CHUNK), CHUNK)


def _gdn_super_steps(q, k, v, gcb, gr, head, tinv_tall=None):
    lane = lax.broadcasted_iota(jnp.int32, (SUPER, 128), 1)
    row = lax.broadcasted_iota(jnp.int32, (SUPER, SUPER), 0)
    col = lax.broadcasted_iota(jnp.int32, (SUPER, SUPER), 1)
    same = row // CHUNK == col // CHUNK
    beta = jnp.sum(jnp.where(lane == head, gcb, 0.0), axis=1, keepdims=True)
    gc = jnp.sum(jnp.where(lane == A_HEADS + head, gcb, 0.0), axis=1, keepdims=True)
    g_last = jnp.sum(jnp.where(col == (row // CHUNK) * CHUNK + (CHUNK - 1), gr, 0.0), axis=1, keepdims=True)
    gamma = jnp.exp(gc)
    decay = jnp.where(same & (row >= col), jnp.exp(jnp.minimum(gc - gr, 0.0)), 0.0)
    kb = k * beta
    m = jnp.where(same & (row > col), _mm_nt(kb, k) * decay, 0.0)
    p = jnp.where(same & (row >= col), _mm_nt(q, k) * decay, 0.0)
    yield
    if tinv_tall is None:
        tinv = yield from _inverse_steps(m, row, col)
    else:
        tinv = _tall_to_block_diag(tinv_tall, same)
    u = _mm(tinv, v * beta)
    w = _mm(tinv, kb * gamma)
    yield
    e_tail = jnp.exp(g_last - gc)
    return dict(beta=beta, gc=gc, g_last=g_last, gamma=gamma, decay=decay, kb=kb, m=m,
                tinv=tinv, u=u, w=w, p=p, e_tail=e_tail, row=row, col=col, lane=lane, same=same)


def _gdn_super_common(q, k, v, gcb, gr, head, tinv_tall=None):
    return _drain(_gdn_super_steps(q, k, v, gcb, gr, head, tinv_tall))


def _store_scan_operands(rows, q, k, t, u_scr, w_scr, p_scr, qg_scr, ke_scr, gl_scr):
    u_scr[rows, :] = t["u"]
    w_scr[rows, :] = _bf(t["w"])
    p_scr[rows, :] = _bf(_diag_blocks_tall(t["p"]))
    qg_scr[rows, :] = _bf(q * t["gamma"])
    ke_scr[rows, :] = _bf(k * t["e_tail"])
    gl_scr[rows, :] = jnp.broadcast_to(jnp.exp(t["g_last"]), (SUPER, 128))


def _gdn_fwd_body(q_ref, k_ref, v_ref, gc_ref, gr_ref, o_ref, tinv_ref, st_ref, s_scr, *sets):
    head = pl.program_id(1)
    n_super = q_ref.shape[0] // SUPER
    set_a, set_b = sets[:6], sets[6:]
    whole = pl.ds(0, SUPER)

    def prepare_steps(i, dst):
        rows = _super_rows(i)
        q, k = q_ref[rows, :], k_ref[rows, :]
        t = yield from _gdn_super_steps(q, k, v_ref[rows, :], gc_ref[rows, :], gr_ref[i], head)
        tinv_ref[rows, :] = _diag_blocks_tall(t["tinv"])
        _store_scan_operands(whole, q, k, t, *dst)

    def scan_steps(i, src):
        u_scr, w_scr, p_scr, qg_scr, ke_scr, gl_scr = src
        for j in range(GDN_GROUP):
            n = i * GDN_GROUP + j
            local = pl.ds(j * CHUNK, CHUNK)
            s = s_scr[...]
            st_ref[n] = s
            sb = _bf(s)
            ws = jnp.dot(w_scr[local, :], sb, preferred_element_type=F32)
            yield
            vb = _bf(u_scr[local, :] - ws)
            o = (jnp.dot(qg_scr[local, :], sb, preferred_element_type=F32)
                 + jnp.dot(p_scr[local, :], vb, preferred_element_type=F32))
            s_new = s * gl_scr[local, :][0:1, 0:1] + lax.dot_general(
                ke_scr[local, :], vb, (((0,), (0,)), ((), ())), preferred_element_type=F32)
            yield
            o_ref[_chunk_rows(n), :] = o
            s_scr[...] = s_new

    _drain(prepare_steps(0, set_a))
    s_scr[...] = jnp.zeros_like(s_scr)

    def pair(k, carry):
        _interleave(scan_steps(2 * k, set_a), prepare_steps(2 * k + 1, set_b))
        _interleave(scan_steps(2 * k + 1, set_b), prepare_steps(2 * k + 2, set_a))
        return carry

    lax.fori_loop(0, n_super // 2 - 1, pair, 0)
    _interleave(scan_steps(n_super - 2, set_a), prepare_steps(n_super - 1, set_b))
    _drain(scan_steps(n_super - 1, set_b))


def _gdn_in_specs(s_len, n_super):
    return [
        pl.BlockSpec((s_len, A_DK), lambda b, h: (b, h)),
        pl.BlockSpec((s_len, A_DK), lambda b, h: (b, A_K_COL + h)),
        pl.BlockSpec((s_len, A_DV), lambda b, h: (b, A_V_COL + h)),
        pl.BlockSpec((s_len, 128), lambda b, h: (b, 0)),
        pl.BlockSpec((None, None, n_super, 1, SUPER), lambda b, h: (b, A_HEADS + h, 0, 0, 0)),
    ]


def _gdn_scan_scratch(s_len):
    return [pltpu.VMEM((A_DK, A_DV), F32), pltpu.VMEM((s_len, A_DV), F32),
            pltpu.VMEM((s_len, A_DK), BF16), pltpu.VMEM((s_len, CHUNK), BF16),
            pltpu.VMEM((s_len, A_DK), BF16), pltpu.VMEM((s_len, A_DK), BF16),
            pltpu.VMEM((s_len, 128), F32)]


def gdn_fwd(qkv, gates_col, gates_row, n_seq):
    t_rows = qkv.shape[0]
    s_len = t_rows // n_seq
    n_chunks = s_len // CHUNK
    return pl.pallas_call(
        _gdn_fwd_body, name="gdn_fwd", grid=(n_seq, A_HEADS),
        in_specs=_gdn_in_specs(s_len, s_len // SUPER),
        out_specs=[
            pl.BlockSpec((s_len, A_DV), lambda b, h: (b, h)),
            pl.BlockSpec((s_len, CHUNK), lambda b, h: (b * A_HEADS + h, 0)),
            pl.BlockSpec((None, n_chunks, A_DK, A_DV), lambda b, h: (b * A_HEADS + h, 0, 0, 0)),
        ],
        out_shape=[
            jax.ShapeDtypeStruct((t_rows, A_VW), F32),
            jax.ShapeDtypeStruct((n_seq * A_HEADS * s_len, CHUNK), F32),
            jax.ShapeDtypeStruct((n_seq * A_HEADS, n_chunks, A_DK, A_DV), F32),
        ],
        scratch_shapes=_gdn_scan_scratch(SUPER) + _gdn_scan_scratch(SUPER)[1:],
        compiler_params=_params(("arbitrary", "arbitrary")),
    )(qkv, qkv, qkv, gates_col, gates_row)


def _gdn_bwd_body(q_ref, k_ref, v_ref, gc_ref, gr_ref, tinv_ref, st_ref, do_ref,
                  dqkv_ref, dgc_ref, ds_scr, *sets):
    head = pl.program_id(1)
    n_super = q_ref.shape[0] // SUPER
    ops = (sets[0:6], sets[6:12])
    res = (sets[12:18], sets[18:24])
    whole = pl.ds(0, SUPER)
    tn = functools.partial(lax.dot_general, dimension_numbers=(((0,), (0,)), ((), ())), preferred_element_type=F32)
    nt = functools.partial(lax.dot_general, dimension_numbers=(((1,), (1,)), ((), ())), preferred_element_type=F32)

    @pl.when(head == 0)
    def _():
        dgc_ref[...] = jnp.zeros_like(dgc_ref)

    def common_steps(i):
        rows = _super_rows(i)
        q, k, v = q_ref[rows, :], k_ref[rows, :], v_ref[rows, :]
        t = yield from _gdn_super_steps(q, k, v, gc_ref[rows, :], gr_ref[i], head, tinv_tall=tinv_ref[rows, :])
        return rows, q, k, v, t

    def stage_p(i, parity):
        _, q, k, _, t = yield from common_steps(i)
        _store_scan_operands(whole, q, k, t, *ops[parity])

    def stage_s(i, parity):
        u_scr, w_scr, p_scr, qg_scr, ke_scr, gl_scr = ops[parity]
        vn_scr, dvn_scr, dqg_scr, dw_scr, dkt_scr, sds_scr = res[parity]
        for j in reversed(range(GDN_GROUP)):
            n = i * GDN_GROUP + j
            local = pl.ds(j * CHUNK, CHUNK)
            ds_next = ds_scr[...]
            dsb = _bf(ds_next)
            s = st_ref[n]
            sb = _bf(s)
            d_ob = _bf(do_ref[_chunk_rows(n), :])
            w_s = jnp.dot(w_scr[local, :], sb, preferred_element_type=F32)
            d_vn = tn(p_scr[local, :], d_ob) + jnp.dot(ke_scr[local, :], dsb, preferred_element_type=F32)
            d_qg = nt(d_ob, sb)
            qg_do = tn(qg_scr[local, :], d_ob)
            yield
            v_new = u_scr[local, :] - w_s
            d_vnb = _bf(d_vn)
            d_w = -nt(d_vnb, sb)
            d_kt = nt(_bf(v_new), dsb)
            w_dvn = tn(w_scr[local, :], d_vnb)
            yield
            vn_scr[local, :] = v_new
            dvn_scr[local, :] = d_vn
            dqg_scr[local, :] = d_qg
            dw_scr[local, :] = d_w
            dkt_scr[local, :] = d_kt
            sds = jnp.sum(jnp.sum(s * ds_next, axis=1, keepdims=True), axis=0, keepdims=True)
            sds_scr[local, :] = jnp.broadcast_to(sds, (CHUNK, 128))
            ds_scr[...] = qg_do + gl_scr[local, :][0:1, 0:1] * ds_next - w_dvn

    def stage_f(i, parity):
        vn_scr, dvn_scr, dqg_scr, dw_scr, dkt_scr, sds_scr = res[parity]
        rows, q, k, v, t = yield from common_steps(i)
        beta, gamma, decay, kb, e_tail = t["beta"], t["gamma"], t["decay"], t["kb"], t["e_tail"]
        row, col, lane, same = t["row"], t["col"], t["lane"], t["same"]
        d_o = do_ref[rows, :]
        v_new, d_vn = vn_scr[...], dvn_scr[...]
        d_qg, d_w, d_kt = dqg_scr[...], dw_scr[...], dkt_scr[...]
        gamma_last = jnp.exp(t["g_last"])

        d_p = jnp.where(same & (row >= col), _mm_nt(d_o, v_new), 0.0)
        d_ru = _mm_tn(t["tinv"], d_vn)
        d_rw = _mm_tn(t["tinv"], d_w)
        yield
        d_m = jnp.where(same & (row > col), -(_mm_nt(d_ru, t["u"]) + _mm_nt(d_rw, t["w"])), 0.0)
        yield

        x_p = d_p * decay
        y_m = d_m * decay
        d_kb = _mm(y_m, k) + d_rw * gamma
        dqkv_ref[rows, 0:A_DK] = _mm(x_p, k) + d_qg * gamma
        dqkv_ref[rows, A_DK:2 * A_DK] = _mm_tn(x_p, q) + _mm_tn(y_m, kb) + d_kb * beta + d_kt * e_tail
        dqkv_ref[rows, 2 * A_DK:A_HEAD_W] = d_ru * beta

        d_beta = (jnp.sum(d_ru * v, axis=1, keepdims=True)
                  + jnp.sum(d_kb * k, axis=1, keepdims=True))
        z = d_p * t["p"] + d_m * t["m"]
        eps_tail = jnp.sum(d_kt * k, axis=1, keepdims=True) * e_tail
        d_gc = (jnp.sum(z, axis=1, keepdims=True) - _colsum_as_col(z)
                + jnp.sum(d_qg * q, axis=1, keepdims=True) * gamma
                + jnp.sum(d_rw * kb, axis=1, keepdims=True) * gamma
                - eps_tail)
        d_glast = _block_sum(same, eps_tail) + gamma_last * sds_scr[...][:, 0:1]
        yield
        rcol = lax.broadcasted_iota(jnp.int32, (SUPER, 1), 0)
        d_gc = d_gc + jnp.where(rcol % CHUNK == CHUNK - 1, d_glast, 0.0)
        dgc_ref[rows, :] += (jnp.where(lane == head, d_beta, 0.0)
                             + jnp.where(lane == A_HEADS + head, d_gc, 0.0))

    last = n_super - 1
    _drain(stage_p(last, 1))
    ds_scr[...] = jnp.zeros_like(ds_scr)
    _interleave(stage_s(last, 1), stage_p(last - 1, 0))

    def pair(k, carry):
        i = last - 1 - 2 * k
        _interleave(stage_s(i, 0), stage_f(i + 1, 1), stage_p(i - 1, 1))
        _interleave(stage_s(i - 1, 1), stage_f(i, 0), stage_p(i - 2, 0))
        return carry

    lax.fori_loop(0, n_super // 2 - 1, pair, 0)
    _interleave(stage_s(0, 0), stage_f(1, 1))
    _drain(stage_f(0, 0))


def gdn_bwd(qkv, gates_col, gates_row, tinv, states, d_o, n_seq):
    t_rows = qkv.shape[0]
    s_len = t_rows // n_seq
    n_chunks = s_len // CHUNK
    v_spec = pl.BlockSpec((s_len, A_DV), lambda b, h: (b, h))
    gate_spec = pl.BlockSpec((s_len, 128), lambda b, h: (b, 0))
    return pl.pallas_call(
        _gdn_bwd_body, name="gdn_bwd", grid=(n_seq, A_HEADS),
        in_specs=_gdn_in_specs(s_len, s_len // SUPER) + [
            pl.BlockSpec((s_len, CHUNK), lambda b, h: (b * A_HEADS + h, 0)),
            pl.BlockSpec((None, n_chunks, A_DK, A_DV), lambda b, h: (b * A_HEADS + h, 0, 0, 0)),
            v_spec,
        ],
        out_specs=[pl.BlockSpec((s_len, A_HEAD_W), lambda b, h: (b, h)), gate_spec],
        out_shape=[
            jax.ShapeDtypeStruct((t_rows, A_HEADS * A_HEAD_W), F32),
            jax.ShapeDtypeStruct((t_rows, 128), F32),
        ],
        scratch_shapes=(_gdn_scan_scratch(SUPER) + _gdn_scan_scratch(SUPER)[1:]
                        + 2 * [pltpu.VMEM((SUPER, A_DV), F32), pltpu.VMEM((SUPER, A_DV), F32),
                               pltpu.VMEM((SUPER, A_DK), F32), pltpu.VMEM((SUPER, A_DK), F32),
                               pltpu.VMEM((SUPER, A_DK), F32), pltpu.VMEM((SUPER, 128), F32)]),
        compiler_params=_params(("arbitrary", "arbitrary")),
    )(qkv, qkv, qkv, gates_col, gates_row, tinv, states, d_o)


def _shift_down(x, j, rows):
    return jnp.where(rows >= j, pltpu.roll(x, j, 0), 0.0)


def _shift_up(x, j, rows):
    n = x.shape[0]
    return jnp.where(rows < n - j, pltpu.roll(x, n - j, 0), 0.0)


def _conv_silu_norm(x, w, cb):
    rows = lax.broadcasted_iota(jnp.int32, (x.shape[0], 1), 0)
    c = x * w[A_CONV - 1:A_CONV, :]
    for j in range(1, A_CONV):
        c = c + _shift_down(x, j, rows) * w[A_CONV - 1 - j:A_CONV - j, :]
    sig = _sigmoid(c)
    a = c * sig
    rn = lax.rsqrt(jnp.sum(a * a, axis=1, keepdims=True) + EPS)
    is_q = cb < A_QK // 128
    is_qk = cb < 2 * A_QK // 128
    q_scale = jnp.where(is_q, A_DK ** -0.5, 1.0).astype(F32)
    return rows, c, sig, a, rn, is_qk, q_scale


def _a_pre_fwd_body(x_ref, w_ref, o_ref):
    cb = pl.program_id(1)
    _, _, _, a, rn, is_qk, q_scale = _conv_silu_norm(x_ref[...], w_ref[...], cb)
    o_ref[...] = a * jnp.where(is_qk, rn * q_scale, 1.0)


def a_pre_fwd(proj_a, conv_w8, n_seq):
    t_rows = proj_a.shape[0]
    s_len = t_rows // n_seq
    n_cb = (2 * A_QK + A_VW) // 128
    blk = pl.BlockSpec((s_len, 128), lambda b, c: (b, c))
    return pl.pallas_call(
        _a_pre_fwd_body, name="a_pre_fwd", grid=(n_seq, n_cb),
        in_specs=[blk, pl.BlockSpec((8, 128), lambda b, c: (0, c))],
        out_specs=blk,
        out_shape=jax.ShapeDtypeStruct((t_rows, n_cb * 128), F32),
        compiler_params=_params(("arbitrary", "arbitrary")),
    )(proj_a, conv_w8)


def _a_pre_bwd_body(x_ref, w_ref, dy_ref, dx_ref, dw_ref):
    b, cb = pl.program_id(1), pl.program_id(0)
    x, w = x_ref[...], w_ref[...]
    rows, c, sig, a, rn, is_qk, q_scale = _conv_silu_norm(x, w, cb)
    dy = dy_ref[...]
    da_n = q_scale * (rn * dy - a * (rn * rn * rn) * jnp.sum(dy * a, axis=1, keepdims=True))
    da = jnp.where(is_qk, da_n, dy)
    dc = da * (sig * (1.0 + c * (1.0 - sig)))
    dx = dc * w[A_CONV - 1:A_CONV, :]
    for j in range(1, A_CONV):
        dx = dx + _shift_up(dc, j, rows) * w[A_CONV - 1 - j:A_CONV - j, :]
    dx_ref[...] = dx.astype(dx_ref.dtype)

    @pl.when(b == 0)
    def _():
        dw_ref[...] = jnp.zeros_like(dw_ref)

    for j in range(A_CONV):
        xs = x if j == 0 else _shift_down(x, j, rows)
        dw_ref[A_CONV - 1 - j:A_CONV - j, :] += jnp.sum(dc * xs, axis=0, keepdims=True)


def a_pre_bwd(proj_a, conv_w8, dqkv_hm, n_seq):
    t_rows = proj_a.shape[0]
    s_len = t_rows // n_seq
    n_cb = (2 * A_QK + A_VW) // 128
    blk = pl.BlockSpec((s_len, 128), lambda c, b: (b, c))
    wblk = pl.BlockSpec((8, 128), lambda c, b: (0, c))
    per_head = A_HEAD_W // 128
    n_q = A_QK // 128

    def head_major(c, b):
        v_blk = jnp.maximum(c - 2 * n_q, 0)
        col = jnp.where(c < n_q, c * per_head,
                        jnp.where(c < 2 * n_q, (c - n_q) * per_head + 1,
                                  (v_blk // 2) * per_head + 2 + v_blk % 2))
        return (b, col)

    return pl.pallas_call(
        _a_pre_bwd_body, name="a_pre_bwd", grid=(n_cb, n_seq),
        in_specs=[blk, wblk, pl.BlockSpec((s_len, 128), head_major)],
        out_specs=[blk, wblk],
        out_shape=[jax.ShapeDtypeStruct((t_rows, n_cb * 128), BF16),
                   jax.ShapeDtypeStruct((8, n_cb * 128), F32)],
        compiler_params=_params(("arbitrary", "arbitrary")),
    )(proj_a, conv_w8, dqkv_hm)


GATE_TILE = 512


def _softplus(y):
    return jnp.maximum(y, 0.0) + jnp.log1p(jnp.exp(-jnp.abs(y)))


def _gate_values(x, prm):
    beta = _sigmoid(x)
    y = x + prm[1:2, :]
    neg_a = -jnp.exp(prm[0:1, :])
    g = neg_a * _softplus(y)
    return beta, y, neg_a, g


def _a_gates_fwd_body(x_ref, prm_ref, gc_ref, gr_ref):
    x = x_ref[...]
    tm = x.shape[0]
    beta, _, _, g = _gate_values(x, prm_ref[...])
    in_chunk = lax.broadcasted_iota(jnp.int32, (tm, 1), 0) % CHUNK
    s = 1
    while s < CHUNK:
        g = g + jnp.where(in_chunk >= s, pltpu.roll(g, s, 0), 0.0)
        s *= 2
    lane = lax.broadcasted_iota(jnp.int32, x.shape, 1)
    out = jnp.where(lane < A_HEADS, beta, jnp.where(lane < 2 * A_HEADS, g, 0.0))
    gc_ref[...] = out
    gr_ref[...] = out.T[0:2 * A_HEADS, :]


def a_gates_fwd(proj_a, prm, n_seq):
    t_rows = proj_a.shape[0]
    s_len = t_rows // n_seq
    tm = min(GATE_TILE, s_len)
    n_t = s_len // tm
    return pl.pallas_call(
        _a_gates_fwd_body, name="a_gates_fwd", grid=(n_seq, n_t),
        in_specs=[pl.BlockSpec((tm, 128), lambda b, i: (b * n_t + i, A_GATE_COL)),
                  pl.BlockSpec((8, 128), lambda b, i: (0, 0))],
        out_specs=[pl.BlockSpec((tm, 128), lambda b, i: (b * n_t + i, 0)),
                   pl.BlockSpec((None, 2 * A_HEADS, tm), lambda b, i: (b, 0, i))],
        out_shape=[jax.ShapeDtypeStruct((t_rows, 128), F32),
                   jax.ShapeDtypeStruct((n_seq, 2 * A_HEADS, s_len), F32)],
        compiler_params=_params(("arbitrary", "arbitrary")),
    )(proj_a, prm)


def _a_gates_bwd_body(x_ref, prm_ref, dgc_ref, dx_ref, dprm_ref):
    first = (pl.program_id(0) == 0) & (pl.program_id(1) == 0)
    x = x_ref[...]
    tm = x.shape[0]
    beta, y, neg_a, g = _gate_values(x, prm_ref[...])
    d = dgc_ref[...]
    in_chunk = lax.broadcasted_iota(jnp.int32, (tm, 1), 0) % CHUNK
    dg = d
    s = 1
    while s < CHUNK:
        dg = dg + jnp.where(in_chunk < CHUNK - s, pltpu.roll(dg, tm - s, 0), 0.0)
        s *= 2
    lane = lax.broadcasted_iota(jnp.int32, x.shape, 1)
    is_decay = (lane >= A_HEADS) & (lane < 2 * A_HEADS)
    d_alogit = jnp.where(is_decay, dg * neg_a * _sigmoid(y), 0.0)
    dx_ref[...] = jnp.where(lane < A_HEADS, d * beta * (1.0 - beta), d_alogit).astype(dx_ref.dtype)

    @pl.when(first)
    def _():
        dprm_ref[...] = jnp.zeros_like(dprm_ref)

    dprm_ref[0:1, :] += jnp.sum(jnp.where(is_decay, dg * g, 0.0), axis=0, keepdims=True)
    dprm_ref[1:2, :] += jnp.sum(d_alogit, axis=0, keepdims=True)


def a_gates_bwd(proj_a, prm, dgates_col, n_seq):
    t_rows = proj_a.shape[0]
    s_len = t_rows // n_seq
    tm = min(GATE_TILE, s_len)
    n_t = s_len // tm
    return pl.pallas_call(
        _a_gates_bwd_body, name="a_gates_bwd", grid=(n_seq, n_t),
        in_specs=[pl.BlockSpec((tm, 128), lambda b, i: (b * n_t + i, A_GATE_COL)),
                  pl.BlockSpec((8, 128), lambda b, i: (0, 0)),
                  pl.BlockSpec((tm, 128), lambda b, i: (b * n_t + i, 0))],
        out_specs=[pl.BlockSpec((tm, 128), lambda b, i: (b * n_t + i, 0)),
                   pl.BlockSpec((8, 128), lambda b, i: (0, 0))],
        out_shape=[jax.ShapeDtypeStruct((t_rows, 128), BF16),
                   jax.ShapeDtypeStruct((8, 128), F32)],
        compiler_params=_params(("arbitrary", "arbitrary")),
    )(proj_a, prm, dgates_col)


ROW_TILE = 512
A_Z_COL = (2 * A_QK + A_VW) // A_DV


def _silu_parts(z):
    sig = _sigmoid(z)
    return z * sig, sig * (1.0 + z * (1.0 - sig))


def _a_post_fwd_body(o_ref, z_ref, g_ref, og_ref):
    o = o_ref[...]
    r = lax.rsqrt(jnp.mean(o * o, axis=1, keepdims=True) + EPS)
    silu, _ = _silu_parts(z_ref[...])
    og_ref[...] = ((o * r * g_ref[0:1, :]) * silu).astype(og_ref.dtype)


def a_post_fwd(o, proj_a, norm_g8):
    t_rows = o.shape[0]
    tm = min(ROW_TILE, t_rows)
    blk = pl.BlockSpec((tm, A_DV), lambda i, h: (i, h))
    return pl.pallas_call(
        _a_post_fwd_body, name="a_post_fwd", grid=(t_rows // tm, A_HEADS),
        in_specs=[blk, pl.BlockSpec((tm, A_DV), lambda i, h: (i, A_Z_COL + h)),
                  pl.BlockSpec((8, A_DV), lambda i, h: (0, 0))],
        out_specs=blk,
        out_shape=jax.ShapeDtypeStruct((t_rows, A_VW), BF16),
        compiler_params=_params(("arbitrary", "arbitrary")),
    )(o, proj_a, norm_g8)


def _a_post_bwd_body(o_ref, z_ref, g_ref, dog_ref, do_ref, dz_ref, dg_ref):
    first = (pl.program_id(0) == 0) & (pl.program_id(1) == 0)
    o, z, d_og = o_ref[...], z_ref[...], dog_ref[...]
    gain = g_ref[0:1, :]
    r = lax.rsqrt(jnp.mean(o * o, axis=1, keepdims=True) + EPS)
    silu, dsilu = _silu_parts(z)
    xr = o * r
    d_on = d_og * silu
    dz_ref[...] = (d_og * (xr * gain) * dsilu).astype(dz_ref.dtype)
    u = d_on * gain
    do_ref[...] = r * u - xr * (r * r) * jnp.mean(u * o, axis=1, keepdims=True)

    @pl.when(first)
    def _():
        dg_ref[...] = jnp.zeros_like(dg_ref)

    dg_ref[0:1, :] += jnp.sum(d_on * xr, axis=0, keepdims=True)


def a_post_bwd(o, proj_a, norm_g8, d_og):
    t_rows = o.shape[0]
    tm = min(ROW_TILE, t_rows)
    blk = pl.BlockSpec((tm, A_DV), lambda i, h: (i, h))
    gblk = pl.BlockSpec((8, A_DV), lambda i, h: (0, 0))
    return pl.pallas_call(
        _a_post_bwd_body, name="a_post_bwd", grid=(t_rows // tm, A_HEADS),
        in_specs=[blk, pl.BlockSpec((tm, A_DV), lambda i, h: (i, A_Z_COL + h)), gblk, blk],
        out_specs=[blk, blk, gblk],
        out_shape=[jax.ShapeDtypeStruct((t_rows, A_VW), F32),
                   jax.ShapeDtypeStruct((t_rows, A_VW), BF16),
                   jax.ShapeDtypeStruct((8, A_DV), F32)],
        compiler_params=_params(("arbitrary", "arbitrary")),
    )(o, proj_a, norm_g8, d_og)


NEG_BIG = -1e30
ATT_SCALE = B_DH ** -0.5


def _swap_rope_halves(x):
    lane = lax.broadcasted_iota(jnp.int32, x.shape, 1)
    return jnp.where(lane < ROPE_HALF, pltpu.roll(x, B_DH - ROPE_HALF, 1),
                     jnp.where(lane < ROPE_DIMS, pltpu.roll(x, ROPE_HALF, 1), 0.0))


def _norm_rope(x, gain, cos_t, sin_t):
    r = lax.rsqrt(jnp.mean(x * x, axis=1, keepdims=True) + EPS)
    xn = x * r * gain
    return xn * cos_t + _swap_rope_halves(xn) * sin_t, r


def _norm_rope_bwd(x, r, gain, cos_t, sin_t, dy):
    d_xn = dy * cos_t + _swap_rope_halves(dy * sin_t)
    xr = x * r
    u = d_xn * gain
    dx = r * u - xr * (r * r) * jnp.mean(u * x, axis=1, keepdims=True)
    return dx, jnp.sum(d_xn * xr, axis=0, keepdims=True)


def _stream_rows(idx, dilation, s_len):
    nb = s_len // dilation // B_BLOCK
    r = idx // nb
    m = idx % nb
    cur = r + m * (B_BLOCK * dilation)
    prev = r + jnp.maximum(m - 1, 0) * (B_BLOCK * dilation)
    return cur, prev, m > 0


def _rows(start, dilation):
    if dilation == 1:
        return pl.ds(start, B_BLOCK)
    return pl.ds(start, B_BLOCK, stride=dilation)


ATT_UNROLL = 4


def _band_mask(has_prev):
    qi = lax.broadcasted_iota(jnp.int32, (B_BLOCK, 2 * B_BLOCK), 0)
    kj = lax.broadcasted_iota(jnp.int32, (B_BLOCK, 2 * B_BLOCK), 1)
    return ((kj < B_BLOCK) & (kj >= qi) & has_prev) | ((kj >= B_BLOCK) & (kj - B_BLOCK <= qi))


def _block_scores(qb, kc, kp, has_prev):
    qi = lax.broadcasted_iota(jnp.int32, (B_BLOCK, B_BLOCK), 0)
    kj = lax.broadcasted_iota(jnp.int32, (B_BLOCK, B_BLOCK), 1)
    s_c = jnp.where(qi >= kj, _mm_nt(qb, kc) * ATT_SCALE, NEG_BIG)
    s_p = jnp.where((kj >= qi) & has_prev, _mm_nt(qb, kp) * ATT_SCALE, NEG_BIG)
    return s_c, s_p


def _attn_fwd_body(qkv_ref, z_ref, cos_ref, sin_ref, gain_ref, og_ref, o_ref, lse_ref,
                   qn_scr, kn_scr, og_scr, lg_scr):
    head, grp = pl.program_id(1), pl.program_id(2)
    s_len = z_ref.shape[0]
    n_blocks = s_len // B_BLOCK
    cos_t, sin_t = cos_ref[...], sin_ref[...]

    for gi, dil in enumerate(B_DILATIONS):
        @pl.when(grp == gi)
        def _(gi=gi, dil=dil):
            qn_scr[...], _ = _norm_rope(qkv_ref[0], gain_ref[gi:gi + 1, :], cos_t, sin_t)
            kn_scr[...], _ = _norm_rope(qkv_ref[1], gain_ref[B_GROUPS + gi:B_GROUPS + gi + 1, :], cos_t, sin_t)

            ones = jnp.ones((2 * B_BLOCK, B_DH), BF16)

            def blocks(it, carry):
                scored = []
                for j in range(ATT_UNROLL):
                    cur, prev, has_prev = _stream_rows(it * ATT_UNROLL + j, dil, s_len)
                    rc, rp = _rows(cur, dil), _rows(prev, dil)
                    k2 = jnp.concatenate([kn_scr[rp, :], kn_scr[rc, :]], axis=0)
                    scored.append((rc, rp, has_prev, _mm_nt(qn_scr[rc, :], k2) * ATT_SCALE))
                summed = []
                for rc, rp, has_prev, s in scored:
                    s = jnp.where(_band_mask(has_prev), s, NEG_BIG)
                    mx = jnp.max(s, axis=1, keepdims=True)
                    v2 = jnp.concatenate([qkv_ref.at[2][rp, :], qkv_ref.at[2][rc, :]], axis=0)
                    acc = jnp.dot(_bf(jnp.exp(s - mx)), jnp.concatenate([_bf(v2), ones], axis=1),
                                  preferred_element_type=F32)
                    summed.append((rc, mx, acc))
                for rc, mx, acc in summed:
                    den = acc[:, B_DH:B_DH + 1]
                    og_scr.at[gi][rc, :] = acc[:, :B_DH] / den
                    lg_scr.at[gi][rc, :] = jnp.broadcast_to(mx + jnp.log(den), (B_BLOCK, B_DH))
                return carry

            lax.fori_loop(0, n_blocks // ATT_UNROLL, blocks, 0)

    @pl.when(grp == B_GROUPS - 1)
    def _():
        l0, l1, l2 = lg_scr[0], lg_scr[1], lg_scr[2]
        mx = jnp.maximum(jnp.maximum(l0, l1), l2)
        w0, w1, w2 = jnp.exp(l0 - mx), jnp.exp(l1 - mx), jnp.exp(l2 - mx)
        den = w0 + w1 + w2
        o = (w0 * og_scr[0] + w1 * og_scr[1] + w2 * og_scr[2]) / den
        silu, _ = _silu_parts(z_ref[...])
        o_ref[...] = o
        og_ref[...] = (o * silu).astype(og_ref.dtype)
        @pl.when(head == 0)
        def _():
            lse_ref[...] = jnp.zeros_like(lse_ref)

        lane = lax.broadcasted_iota(jnp.int32, o.shape, 1)
        lse_ref[...] = jnp.where(lane == head, mx + jnp.log(den), lse_ref[...])


def attn_fwd(proj_b, cos_t, sin_t, gains8, n_seq):
    t_rows = proj_b.shape[1]
    s_len = t_rows // n_seq
    head_blk = pl.BlockSpec((s_len, B_DH), lambda b, h, g: (b, h))
    seq_blk = pl.BlockSpec((s_len, 128), lambda b, h, g: (b, 0))
    return pl.pallas_call(
        _attn_fwd_body, name="attn_fwd", grid=(n_seq, B_HEADS, B_GROUPS),
        in_specs=[
            pl.BlockSpec((3, s_len, B_DH), lambda b, h, g: (g, b, h)),
            pl.BlockSpec((None, s_len, B_DH), lambda b, h, g: (B_PIECES - 1, b, h)),
            seq_blk, seq_blk,
            pl.BlockSpec((8, 128), lambda b, h, g: (0, 0)),
        ],
        out_specs=[head_blk, head_blk, seq_blk],
        out_shape=[jax.ShapeDtypeStruct((t_rows, B_W), BF16),
                   jax.ShapeDtypeStruct((t_rows, B_W), F32),
                   jax.ShapeDtypeStruct((t_rows, 128), F32)],
        scratch_shapes=[pltpu.VMEM((s_len, B_DH), F32), pltpu.VMEM((s_len, B_DH), F32),
                        pltpu.VMEM((B_GROUPS, s_len, B_DH), F32), pltpu.VMEM((B_GROUPS, s_len, B_DH), F32)],
        compiler_params=_params(("arbitrary", "arbitrary", "arbitrary")),
    )(proj_b, proj_b, cos_t, sin_t, gains8)


def _attn_bwd_body(qkv_ref, z_ref, cos_ref, sin_ref, gain_ref, dog_ref, o_ref, lse_ref,
                   dqkv_ref, dz_ref, dgain_ref,
                   qn_scr, kn_scr, dqn_scr, dkn_scr, do_scr, dl_scr, ls_scr, dv_scr):
    head, grp = pl.program_id(1), pl.program_id(2)
    first = (pl.program_id(0) == 0) & (head == 0) & (grp == 0)
    s_len = z_ref.shape[0]
    n_blocks = s_len // B_BLOCK
    cos_t, sin_t = cos_ref[...], sin_ref[...]

    @pl.when(first)
    def _():
        dgain_ref[...] = jnp.zeros_like(dgain_ref)

    @pl.when(grp == 0)
    def _():
        d_og, o = dog_ref[...], o_ref[...]
        silu, dsilu = _silu_parts(z_ref[...])
        d_o = d_og * silu
        dz_ref[...] = (d_og * o * dsilu).astype(dz_ref.dtype)
        do_scr[...] = d_o
        dl_scr[...] = jnp.broadcast_to(jnp.sum(d_o * o, axis=1, keepdims=True), o.shape)
        lane = lax.broadcasted_iota(jnp.int32, o.shape, 1)
        ls_scr[...] = jnp.broadcast_to(
            jnp.sum(jnp.where(lane == head, lse_ref[...], 0.0), axis=1, keepdims=True), o.shape)

    for gi, dil in enumerate(B_DILATIONS):
        @pl.when(grp == gi)
        def _(gi=gi, dil=dil):
            q_raw, k_raw = qkv_ref[0], qkv_ref[1]
            gq = gain_ref[gi:gi + 1, :]
            gk = gain_ref[B_GROUPS + gi:B_GROUPS + gi + 1, :]
            qn_scr[...], rq = _norm_rope(q_raw, gq, cos_t, sin_t)
            kn_scr[...], rk = _norm_rope(k_raw, gk, cos_t, sin_t)
            def blocks(it, carry):
                scored = []
                for j in range(ATT_UNROLL):
                    cur, prev, has_prev = _stream_rows(it * ATT_UNROLL + j, dil, s_len)
                    rc, rp = _rows(cur, dil), _rows(prev, dil)
                    qb, d_ob = _bf(qn_scr[rc, :]), _bf(do_scr[rc, :])
                    k2 = _bf(jnp.concatenate([kn_scr[rp, :], kn_scr[rc, :]], axis=0))
                    v2 = _bf(jnp.concatenate([qkv_ref.at[2][rp, :], qkv_ref.at[2][rc, :]], axis=0))
                    scored.append((rc, rp, has_prev, qb, d_ob, k2,
                                   _mm_nt(qb, k2) * ATT_SCALE, _mm_nt(d_ob, v2)))
                grads = []
                for rc, rp, has_prev, qb, d_ob, k2, s, d_p in scored:
                    p = jnp.exp(jnp.where(_band_mask(has_prev), s - ls_scr[rc, :][:, 0:1], NEG_BIG))
                    ds = _bf(p * (d_p - dl_scr[rc, :][:, 0:1]))
                    grads.append((rc, rp, has_prev,
                                  _mm(ds, k2) * ATT_SCALE, _mm_tn(ds, qb) * ATT_SCALE, _mm_tn(_bf(p), d_ob)))
                for j, (rc, rp, has_prev, dq, dk2, dv2) in enumerate(grads):
                    dqn_scr[rc, :] = dq
                    if j == 0:
                        @pl.when(has_prev)
                        def _():
                            dkn_scr[rp, :] += dk2[:B_BLOCK]
                            dv_scr[rp, :] += dv2[:B_BLOCK]
                    if j + 1 < ATT_UNROLL:
                        dkn_scr[rc, :] = dk2[B_BLOCK:] + grads[j + 1][4][:B_BLOCK]
                        dv_scr[rc, :] = dv2[B_BLOCK:] + grads[j + 1][5][:B_BLOCK]
                    else:
                        dkn_scr[rc, :] = dk2[B_BLOCK:]
                        dv_scr[rc, :] = dv2[B_BLOCK:]
                return carry

            lax.fori_loop(0, n_blocks // ATT_UNROLL, blocks, 0)
            dq, dgq = _norm_rope_bwd(q_raw, rq, gq, cos_t, sin_t, dqn_scr[...])
            dk, dgk = _norm_rope_bwd(k_raw, rk, gk, cos_t, sin_t, dkn_scr[...])
            dqkv_ref[0] = dq.astype(dqkv_ref.dtype)
            dqkv_ref[1] = dk.astype(dqkv_ref.dtype)
            dqkv_ref[2] = dv_scr[...].astype(dqkv_ref.dtype)
            dgain_ref[gi:gi + 1, :] += dgq
            dgain_ref[B_GROUPS + gi:B_GROUPS + gi + 1, :] += dgk


def attn_bwd(proj_b, cos_t, sin_t, gains8, d_og, o, lse, n_seq):
    t_rows = proj_b.shape[1]
    s_len = t_rows // n_seq
    head_blk = pl.BlockSpec((s_len, B_DH), lambda b, h, g: (b, h))
    seq_blk = pl.BlockSpec((s_len, 128), lambda b, h, g: (b, 0))
    grp_blk = pl.BlockSpec((3, s_len, B_DH), lambda b, h, g: (g, b, h))
    gain_blk = pl.BlockSpec((8, 128), lambda b, h, g: (0, 0))
    return pl.pallas_call(
        _attn_bwd_body, name="attn_bwd", grid=(n_seq, B_HEADS, B_GROUPS),
        in_specs=[
            grp_blk,
            pl.BlockSpec((None, s_len, B_DH), lambda b, h, g: (B_PIECES - 1, b, h)),
            seq_blk, seq_blk, gain_blk, head_blk, head_blk, seq_blk,
        ],
        out_specs=[grp_blk, head_blk, gain_blk],
        out_shape=[jax.ShapeDtypeStruct((3 * B_GROUPS, t_rows, B_W), BF16),
                   jax.ShapeDtypeStruct((t_rows, B_W), BF16),
                   jax.ShapeDtypeStruct((8, 128), F32)],
        scratch_shapes=[pltpu.VMEM((s_len, B_DH), F32) for _ in range(8)],
        compiler_params=_params(("arbitrary", "arbitrary", "arbitrary")),
    )(proj_b, proj_b, cos_t, sin_t, gains8, d_og, o, lse)


def rope_tables(positions):
    inv_freq = ROPE_THETA ** (-jnp.arange(0, ROPE_DIMS, 2, dtype=F32) / ROPE_DIMS)
    ang = positions.astype(F32)[:, None] * inv_freq
    cos, sin = jnp.cos(ang), jnp.sin(ang)
    t_rows = positions.shape[0]
    rest = B_DH - ROPE_DIMS
    cos_t = jnp.concatenate([cos, cos, jnp.ones((t_rows, rest), F32)], axis=1)
    sin_t = jnp.concatenate([-sin, sin, jnp.zeros((t_rows, rest), F32)], axis=1)
    return cos_t, sin_t


def _rms_fwd_body(x_ref, g_ref, *rest, layer):
    h_ref = rest[-1]
    x = x_ref[...]
    r = lax.rsqrt(jnp.mean(x * x, axis=1, keepdims=True) + EPS)
    h_ref[...] = (x * r * g_ref[layer:layer + 1, :]).astype(h_ref.dtype)


def rms_fwd(x, gains8, layer, after=None):
    t_rows, d = x.shape
    tm = min(ROW_TILE, t_rows)
    in_specs = [pl.BlockSpec((tm, d), lambda i: (i, 0)), pl.BlockSpec((8, d), lambda i: (0, 0))]
    args = [x, gains8]
    if after is not None:
        in_specs.append(HBM_SPEC)
        args.append(after)
    return pl.pallas_call(
        functools.partial(_rms_fwd_body, layer=layer), name=f"rms_fwd_{layer}", grid=(t_rows // tm,),
        in_specs=in_specs,
        out_specs=pl.BlockSpec((tm, d), lambda i: (i, 0)),
        out_shape=jax.ShapeDtypeStruct((t_rows, d), BF16),
        compiler_params=_params(("arbitrary",)),
    )(*args)


def _rms_bwd_body(x_ref, g_ref, dh_ref, res_ref, dx_ref, dg_ref, *, layer):
    x, dh = x_ref[...], dh_ref[...]
    r = lax.rsqrt(jnp.mean(x * x, axis=1, keepdims=True) + EPS)
    xr = x * r
    u = dh * g_ref[layer:layer + 1, :]
    dx_ref[...] = res_ref[...] + r * u - xr * (r * r) * jnp.mean(u * x, axis=1, keepdims=True)

    @pl.when(pl.program_id(0) == 0)
    def _():
        dg_ref[...] = jnp.zeros_like(dg_ref)

    dg_ref[0:1, :] += jnp.sum(dh * xr, axis=0, keepdims=True)


def rms_bwd(x, gains8, layer, dh, d_res):
    t_rows, d = x.shape
    tm = min(ROW_TILE, t_rows)
    blk = pl.BlockSpec((tm, d), lambda i: (i, 0))
    gblk = pl.BlockSpec((8, d), lambda i: (0, 0))
    return pl.pallas_call(
        functools.partial(_rms_bwd_body, layer=layer), name=f"rms_bwd_{layer}", grid=(t_rows // tm,),
        in_specs=[blk, gblk, blk, blk],
        out_specs=[blk, gblk],
        out_shape=[jax.ShapeDtypeStruct((t_rows, d), F32), jax.ShapeDtypeStruct((8, d), F32)],
        compiler_params=_params(("arbitrary",)),
    )(x, gains8, dh, d_res)


def _piece_col(p):
    return jnp.where(p < 3 * B_GROUPS, (p % 3) * B_GROUPS + p // 3, 3 * B_GROUPS)


def _mm_nn_body(a_ref, w_ref, *rest, has_res):
    o_ref = rest[-1]
    acc = jnp.dot(a_ref[...], w_ref[...], preferred_element_type=F32)
    if has_res:
        acc = acc + rest[0][...]
    o_ref[...] = acc


def mm_nn(a, w, residual=None, *, tn, name):
    m, k = a.shape
    n = w.shape[1]
    tm = min(ROW_TILE, m)
    in_specs = [pl.BlockSpec((tm, k), lambda j, i: (i, 0)), pl.BlockSpec((k, tn), lambda j, i: (0, j))]
    args = [a, w]
    if residual is not None:
        in_specs.append(pl.BlockSpec((tm, tn), lambda j, i: (i, j)))
        args.append(residual)
    return pl.pallas_call(
        functools.partial(_mm_nn_body, has_res=residual is not None), name=name, grid=(n // tn, m // tm),
        in_specs=in_specs,
        out_specs=pl.BlockSpec((tm, tn), lambda j, i: (i, j)),
        out_shape=jax.ShapeDtypeStruct((m, n), F32),
        compiler_params=_params(("arbitrary", "arbitrary")),
    )(*args)


def mm_nn_pieces(a, w, *, name):
    m, k = a.shape
    tm = min(ROW_TILE, m)
    return pl.pallas_call(
        functools.partial(_mm_nn_body, has_res=False), name=name, grid=(B_PIECES, m // tm),
        in_specs=[pl.BlockSpec((tm, k), lambda p, i: (i, 0)),
                  pl.BlockSpec((k, B_W), lambda p, i: (0, _piece_col(p)))],
        out_specs=pl.BlockSpec((None, tm, B_W), lambda p, i: (p, i, 0)),
        out_shape=jax.ShapeDtypeStruct((B_PIECES, m, B_W), F32),
        compiler_params=_params(("arbitrary", "arbitrary")),
    )(a, w)


NT_ROW_TILE = 1024


def _mm_nt_body(g_ref, w_ref, *rest, has_init):
    o_ref = rest[-1]
    j = pl.program_id(1)
    part = lax.dot_general(_bf(g_ref[...]), w_ref[...], (((1,), (1,)), ((), ())), preferred_element_type=F32)

    @pl.when(j == 0)
    def _():
        o_ref[...] = part + rest[0][...] if has_init else part

    @pl.when(j > 0)
    def _():
        o_ref[...] += part


def mm_nt(g, w, init=None, *, tn, col_off=0, name, after=None):
    m, n = g.shape
    k = w.shape[0]
    tm = min(NT_ROW_TILE, m)
    in_specs = [pl.BlockSpec((tm, tn), lambda i, j: (i, j)),
                pl.BlockSpec((k, tn), lambda i, j: (0, col_off + j))]
    args = [g, w]
    if init is not None:
        in_specs.append(pl.BlockSpec((tm, k), lambda i, j: (i, 0)))
        args.append(init)
    if after is not None:
        in_specs.append(HBM_SPEC)
        args.append(after)
    return pl.pallas_call(
        functools.partial(_mm_nt_body, has_init=init is not None), name=name, grid=(m // tm, n // tn),
        in_specs=in_specs,
        out_specs=pl.BlockSpec((tm, k), lambda i, j: (i, 0)),
        out_shape=jax.ShapeDtypeStruct((m, k), F32),
        compiler_params=_params(("arbitrary", "arbitrary")),
    )(*args)


def mm_nt_pieces(g9, w, *, name):
    n_p, m, _ = g9.shape
    k = w.shape[0]
    tm = min(NT_ROW_TILE, m)
    return pl.pallas_call(
        functools.partial(_mm_nt_body, has_init=False), name=name, grid=(m // tm, n_p),
        in_specs=[pl.BlockSpec((None, tm, B_W), lambda i, p: (p, i, 0)),
                  pl.BlockSpec((k, B_W), lambda i, p: (0, _piece_col(p)))],
        out_specs=pl.BlockSpec((tm, k), lambda i, p: (i, 0)),
        out_shape=jax.ShapeDtypeStruct((m, k), F32),
        compiler_params=_params(("arbitrary", "arbitrary")),
    )(g9, w)


def _mm_tn_body(a_ref, g_ref, o_ref):
    o_ref[...] = lax.dot_general(a_ref[...], _bf(g_ref[...]), (((0,), (0,)), ((), ())),
                                 preferred_element_type=F32).astype(o_ref.dtype)


def mm_tn(a, g, *, tn, out_dtype, name):
    m, k = a.shape
    n = g.shape[1]
    return pl.pallas_call(
        _mm_tn_body, name=name, grid=(n // tn,),
        in_specs=[pl.BlockSpec((m, k), lambda j: (0, 0)), pl.BlockSpec((m, tn), lambda j: (0, j))],
        out_specs=pl.BlockSpec((k, tn), lambda j: (0, j)),
        out_shape=jax.ShapeDtypeStruct((k, n), out_dtype),
        compiler_params=_params(("arbitrary",)),
    )(a, g)


B_UNIT = 256
B_IN_COLS = B_PIECES * B_W
B_SHARD_UNITS = B_IN_COLS // N_DEV // B_UNIT


def mm_tn_b_in(a, g9, gz, *, out_dtype, name):
    m, k = a.shape
    per_piece = B_W // B_UNIT
    n_units = B_IN_COLS // B_UNIT

    def g_map(u):
        nat = jnp.minimum(u // per_piece, 3 * B_GROUPS - 1)
        piece = (nat % B_GROUPS) * 3 + nat // B_GROUPS
        return (piece, 0, u % per_piece)

    def body(a_ref, g_ref, z_ref, o_ref):
        u = pl.program_id(0)

        @pl.when(u < 3 * B_GROUPS * per_piece)
        def _():
            _mm_tn_body(a_ref, g_ref, o_ref)

        @pl.when(u >= 3 * B_GROUPS * per_piece)
        def _():
            _mm_tn_body(a_ref, z_ref, o_ref)

    return pl.pallas_call(
        body, name=name, grid=(n_units,),
        in_specs=[pl.BlockSpec((m, k), lambda u: (0, 0)),
                  pl.BlockSpec((None, m, B_UNIT), g_map),
                  pl.BlockSpec((m, B_UNIT), lambda u: (0, jnp.where(u < 3 * B_GROUPS * per_piece, 0, u % per_piece)))],
        out_specs=pl.BlockSpec((None, k, B_UNIT), lambda u: (u // B_SHARD_UNITS, 0, u % B_SHARD_UNITS)),
        out_shape=jax.ShapeDtypeStruct((N_DEV, k, B_IN_COLS // N_DEV), out_dtype),
        compiler_params=_params(("arbitrary",)),
    )(a, g9, gz)


def _loss_body(y_ref, t_ref, dy_ref, loss_ref, acc):
    i = pl.program_id(0)
    d = y_ref.shape[1]
    err = y_ref[...] - t_ref[...]
    dy_ref[...] = err * (1.0 / d)

    @pl.when(i == 0)
    def _():
        acc[...] = jnp.zeros_like(acc)

    acc[...] += jnp.sum(err * err, axis=0, keepdims=True)

    @pl.when(i == pl.num_programs(0) - 1)
    def _():
        total = jnp.sum(acc[...], axis=1, keepdims=True) * (0.5 / d)
        loss_ref[...] = jnp.broadcast_to(total, loss_ref.shape)


def loss_head(y, target):
    t_rows, d = y.shape
    tm = min(ROW_TILE, t_rows)
    blk = pl.BlockSpec((tm, d), lambda i: (i, 0))
    return pl.pallas_call(
        _loss_body, name="loss_head", grid=(t_rows // tm,),
        in_specs=[blk, blk],
        out_specs=[blk, pl.BlockSpec((8, 128), lambda i: (0, 0))],
        out_shape=[jax.ShapeDtypeStruct((t_rows, d), F32), jax.ShapeDtypeStruct((8, 128), F32)],
        scratch_shapes=[pltpu.VMEM((1, d), F32)],
        compiler_params=_params(("arbitrary",)),
    )(y, target)


def _adamw_body(p_ref, w_ref, m_ref, v_ref, g_ref, d_ref, nm_ref, nv_ref):
    g = p_ref[0].astype(F32)
    for s in range(1, N_DEV):
        g = g + p_ref[s].astype(F32)
    w = w_ref[...]
    m = ADAM_B1 * m_ref[...] + (1.0 - ADAM_B1) * g
    v = ADAM_B2 * v_ref[...] + (1.0 - ADAM_B2) * (g * g)
    m_hat = m / (1.0 - ADAM_B1 ** ADAM_STEP)
    v_hat = v / (1.0 - ADAM_B2 ** ADAM_STEP)
    g_ref[...] = g
    d_ref[...] = -ADAM_LR * (m_hat / (jnp.sqrt(v_hat) + ADAM_EPS) + ADAM_WD * w)
    nm_ref[...] = m
    nv_ref[...] = v


def adamw(parts, w, m, v, *, name):
    _, r, c = w.shape
    tr = r if r <= 256 else 256
    blk = pl.BlockSpec((None, tr, c), lambda i: (0, i, 0))
    out = jax.ShapeDtypeStruct((1, r, c), F32)
    return pl.pallas_call(
        _adamw_body, name=name, grid=(r // tr,),
        in_specs=[pl.BlockSpec((N_DEV, tr, c), lambda i: (0, i, 0)), blk, blk, blk],
        out_specs=[blk, blk, blk, blk],
        out_shape=[out, out, out, out],
        compiler_params=_params(("arbitrary",)),
    )(parts, w, m, v)


MESH_ID = pl.DeviceIdType.MESH
HBM_SPEC = pl.BlockSpec(memory_space=pl.ANY)


def _my_place():
    return lax.axis_index("x"), lax.axis_index("y"), lax.axis_index("c")


def _flat(x, y, c):
    return 4 * x + 2 * y + c


def _all_gather_body(*refs, n):
    ins, outs = refs[:n], refs[n:2 * n]
    send_sems, recv_sems, local_sems = refs[2 * n:]
    x, y, c = _my_place()
    me, sibling = (x, y, c), (x, y, 1 - c)
    chips = [(1 - x, y), (x, 1 - y), (1 - x, 1 - y)]
    pending = []
    for a in range(n):
        src, out = ins[a], outs[a]

        def copy(k, block, to, from_input=False, a=a, src=src, out=out):
            slot = out.at[_flat(*block)]
            return pltpu.make_async_remote_copy(
                src_ref=src if from_input else slot, dst_ref=slot,
                send_sem=send_sems.at[7 * a + k], recv_sem=recv_sems.at[7 * a + k],
                device_id=to, device_id_type=MESH_ID)

        mine = pltpu.make_async_copy(src, out.at[_flat(*me)], local_sems.at[a])
        mine.start()
        first = [copy(0, me, sibling, True)] + [copy(1 + j, me, (*chip, c), True) for j, chip in enumerate(chips)]
        for cp in first:
            cp.start()
        pending.append((copy, mine, first))
    for copy, mine, first in pending:
        passed = [copy(4 + j, (*chip, c), sibling) for j, chip in enumerate(chips)]
        for j, chip in enumerate(chips):
            copy(1 + j, (*chip, c), me).wait_recv()
            passed[j].start()
        copy(0, sibling, me).wait_recv()
        for j, chip in enumerate(chips):
            copy(4 + j, (*chip, 1 - c), me).wait_recv()
        for cp in first + passed:
            cp.wait_send()
        mine.wait()


def all_gather(shards, *, name):
    n = len(shards)
    return pl.pallas_call(
        functools.partial(_all_gather_body, n=n), name=name,
        in_specs=[HBM_SPEC] * n, out_specs=[HBM_SPEC] * n,
        out_shape=[jax.ShapeDtypeStruct((N_DEV,) + s.shape, s.dtype) for s in shards],
        scratch_shapes=[pltpu.SemaphoreType.DMA((7 * n,)), pltpu.SemaphoreType.DMA((7 * n,)),
                        pltpu.SemaphoreType.DMA((n,))],
    )(*shards)


PEER_FLIPS = [(0, 0, 1), (1, 0, 0), (0, 1, 0), (1, 1, 0), (1, 0, 1), (0, 1, 1), (1, 1, 1)]


def _all_to_all_body(*refs, n):
    ins, outs = refs[:n], refs[n:2 * n]
    send_sems, recv_sems, local_sems = refs[2 * n:]
    x, y, c = _my_place()
    me = _flat(x, y, c)
    waits = []
    for a in range(n):
        src, out = ins[a], outs[a]
        mine = pltpu.make_async_copy(src.at[me], out.at[me], local_sems.at[a])
        mine.start()
        waits.append(mine)
        for k, (fx, fy, fc) in enumerate(PEER_FLIPS):
            peer = (1 - x if fx else x, 1 - y if fy else y, 1 - c if fc else c)
            theirs = _flat(*peer)
            sems = dict(send_sem=send_sems.at[7 * a + k], recv_sem=recv_sems.at[7 * a + k],
                        device_id=peer, device_id_type=MESH_ID)
            send = pltpu.make_async_remote_copy(src_ref=src.at[theirs], dst_ref=out.at[me], **sems)
            send.start()
            recv = pltpu.make_async_remote_copy(src_ref=src.at[theirs], dst_ref=out.at[theirs], **sems)
            waits.append((send, recv))
    for w in waits:
        if isinstance(w, tuple):
            w[0].wait_send()
            w[1].wait_recv()
        else:
            w.wait()


def all_to_all(parts, *, name):
    n = len(parts)
    return pl.pallas_call(
        functools.partial(_all_to_all_body, n=n), name=name,
        in_specs=[HBM_SPEC] * n, out_specs=[HBM_SPEC] * n,
        out_shape=[jax.ShapeDtypeStruct(p.shape, p.dtype) for p in parts],
        scratch_shapes=[pltpu.SemaphoreType.DMA((7 * n,)), pltpu.SemaphoreType.DMA((7 * n,)),
                        pltpu.SemaphoreType.DMA((n,))],
    )(*parts)


HBM_ONLY = pl.BlockSpec(memory_space=pltpu.HBM)
SEM_SPEC = pl.BlockSpec(memory_space=pltpu.SEMAPHORE)
DATAFLOW_EFFECT = pltpu.SideEffectType.DATAFLOW_SIDE_EFFECTING


def _split_copies(srcs, lands, send_sems, recv_sems, n, scatter):
    x, y, c = _my_place()
    me = _flat(x, y, c)
    pairs = []
    for a in range(n):
        for k, (fx, fy, fc) in enumerate(PEER_FLIPS):
            peer = (1 - x if fx else x, 1 - y if fy else y, 1 - c if fc else c)
            theirs = _flat(*peer)
            src = srcs[a].at[theirs] if scatter else srcs[a]
            sems = dict(send_sem=send_sems.at[7 * a + k], recv_sem=recv_sems.at[7 * a + k],
                        device_id=peer, device_id_type=MESH_ID)
            pairs.append((pltpu.make_async_remote_copy(src_ref=src, dst_ref=lands[a].at[me], **sems),
                          pltpu.make_async_remote_copy(src_ref=src, dst_ref=lands[a].at[theirs], **sems)))
    return pairs


def _exchange_start_body(*refs, n, scatter):
    srcs, lands = refs[:n], refs[n:2 * n]
    send_sems, recv_sems = refs[2 * n], refs[2 * n + 1]
    token = refs[-1]
    for send, _ in _split_copies(srcs, lands, send_sems, recv_sems, n, scatter):
        send.start()
    token[...] = jnp.zeros_like(token)


def exchange_start(srcs, lands, *, scatter, name):
    n = len(srcs)
    args = [pltpu.with_memory_space_constraint(t, pltpu.HBM) for t in list(srcs) + list(lands)]
    outs = pl.pallas_call(
        functools.partial(_exchange_start_body, n=n, scatter=scatter), name=name,
        out_shape=(pltpu.SemaphoreType.DMA((7 * n,)), pltpu.SemaphoreType.DMA((7 * n,)),
                   *[pltpu.HBM(t.shape, t.dtype) for t in args],
                   jax.ShapeDtypeStruct((8, 128), F32)),
        in_specs=[HBM_ONLY] * (2 * n),
        out_specs=(SEM_SPEC, SEM_SPEC, *[HBM_ONLY] * (2 * n), pl.BlockSpec(memory_space=pltpu.VMEM)),
        input_output_aliases={i: 2 + i for i in range(2 * n)},
        compiler_params=pltpu.CompilerParams(has_side_effects=DATAFLOW_EFFECT),
    )(*args)
    return outs[0], outs[1], outs[2:2 + n], outs[2 + n:2 + 2 * n], outs[-1]


def _exchange_wait_body(*refs, n, scatter):
    srcs, lands = refs[:n], refs[n:2 * n]
    send_sems, recv_sems = refs[2 * n], refs[2 * n + 1]
    for send, recv in _split_copies(srcs, lands, send_sems, recv_sems, n, scatter):
        send.wait_send()
        recv.wait_recv()


def exchange_wait(send_sems, recv_sems, srcs, lands, after, *, scatter, name):
    n = len(srcs)
    outs = pl.pallas_call(
        functools.partial(_exchange_wait_body, n=n, scatter=scatter), name=name,
        out_shape=tuple(pltpu.HBM(t.shape, t.dtype) for t in list(srcs) + list(lands)),
        in_specs=[HBM_ONLY] * (2 * n) + [SEM_SPEC, SEM_SPEC, HBM_SPEC],
        out_specs=tuple([HBM_ONLY] * (2 * n)),
        input_output_aliases={i: i for i in range(2 * n)},
        compiler_params=pltpu.CompilerParams(has_side_effects=DATAFLOW_EFFECT),
    )(*srcs, *lands, send_sems, recv_sems, after)
    return outs[n:]


def _own_slot_only(shape_dtype, own, slot):
    land = lax.empty(shape_dtype.shape, shape_dtype.dtype)
    return lax.dynamic_update_slice(land, own[None], (slot,) + (0,) * own.ndim)


def _pad_rows(a, rows=8):
    return jnp.pad(a, ((0, rows - a.shape[0]), (0, 0)))


def _gate_rows(a_log, dt_bias):
    z = jnp.zeros((8, 128), F32)
    return z.at[0, A_HEADS:2 * A_HEADS].set(a_log[0]).at[1, A_HEADS:2 * A_HEADS].set(dt_bias[0])


def _pack_small(norm_g, a_log, a_dt_bias, a_norm_g, b_q_norm_g, b_k_norm_g):
    return jnp.concatenate([
        norm_g[0].reshape(8, 128), norm_g[1].reshape(8, 128),
        _gate_rows(a_log, a_dt_bias),
        _pad_rows(a_norm_g[0].reshape(2, 128)),
        _pad_rows(jnp.concatenate([b_q_norm_g[0], b_k_norm_g[0]], axis=0)),
    ], axis=0)


def _unpack_small(p):
    return (p[0:16].reshape(2, D_MODEL), p[16:17, A_HEADS:2 * A_HEADS], p[17:18, A_HEADS:2 * A_HEADS],
            p[24:26].reshape(1, A_DV), p[32:35][None], p[35:38][None])


def kernel(x, positions, norm_g, a_w_in, a_conv_w, a_log, a_dt_bias, a_norm_g, a_w_out, b_w_in, b_q_norm_g, b_k_norm_g, b_w_out, loss_target, m_norm_g, m_a_w_in, m_a_conv_w, m_a_log, m_a_dt_bias, m_a_norm_g, m_a_w_out, m_b_w_in, m_b_q_norm_g, m_b_k_norm_g, m_b_w_out, v_norm_g, v_a_w_in, v_a_conv_w, v_a_log, v_a_dt_bias, v_a_norm_g, v_a_w_out, v_b_w_in, v_b_q_norm_g, v_b_k_norm_g, v_b_w_out):
    n_seq, s_len, d = x.shape
    t_rows = n_seq * s_len
    n_chunks = s_len // CHUNK
    x0 = x.reshape(t_rows, d)
    target = loss_target.reshape(t_rows, d)
    my_slot = _flat(*_my_place())

    g_a_in, g_conv = all_gather([a_w_in[0].astype(BF16), _pad_rows(a_conv_w[0])], name="gather_weights_first")
    later = [a_w_out[0].astype(BF16), b_w_in[0].astype(BF16), b_w_out[0].astype(BF16)]
    lands = [_own_slot_only(jax.ShapeDtypeStruct((N_DEV,) + t.shape, t.dtype), t, my_slot) for t in later]
    w_send, w_recv, later, lands, w_token = exchange_start(later, lands, scatter=False, name="gather_weights_start")
    w_a_in = jnp.pad(g_a_in.transpose(1, 0, 2).reshape(d, A_IN), ((0, 0), (0, A_IN_PAD - A_IN)))
    conv_w8 = g_conv.transpose(1, 0, 2).reshape(8, 2 * A_QK + A_VW)

    gains_model = _pad_rows(norm_g)
    gate_prm = _gate_rows(a_log, a_dt_bias)
    gain_a_out = _pad_rows(a_norm_g)
    gains_qk = _pad_rows(jnp.concatenate([b_q_norm_g[0], b_k_norm_g[0]], axis=0))
    cos_t, sin_t = rope_tables(positions.reshape(t_rows))

    h0 = rms_fwd(x0, gains_model, 0, after=w_token)
    proj_a = mm_nn(h0, w_a_in, tn=896, name="proj_a")
    qkv = a_pre_fwd(proj_a, conv_w8, n_seq)
    gates_col, gates_row = a_gates_fwd(proj_a, gate_prm, n_seq)
    gates_row = gates_row.reshape(n_seq, 2 * A_HEADS, s_len // SUPER, 1, SUPER)
    o_a, tinv, states = gdn_fwd(qkv, gates_col, gates_row, n_seq)
    og_a = a_post_fwd(o_a, proj_a, gain_a_out)
    g_a_out, g_b_in, g_b_out = exchange_wait(w_send, w_recv, later, lands, og_a, scatter=False,
                                             name="gather_weights_wait")
    w_a_out = g_a_out.reshape(A_VW, d)
    w_b_in = g_b_in.transpose(1, 0, 2).reshape(d, B_IN_COLS)
    w_b_out = g_b_out.reshape(B_W, d)
    x1 = mm_nn(og_a, w_a_out, x0, tn=512, name="out_a")

    h1 = rms_fwd(x1, gains_model, 1)
    proj_b = mm_nn_pieces(h1, w_b_in, name="proj_b")
    og_b, o_b, lse = attn_fwd(proj_b, cos_t, sin_t, gains_qk, n_seq)
    y = mm_nn(og_b, w_b_out, x1, tn=512, name="out_b")

    dy, loss_blk = loss_head(y, target)
    loss = lax.psum(loss_blk[0, 0], ("x", "y", "c"))

    d_og_b = mm_nt(dy, w_b_out, tn=512, name="d_og_b")
    dw_b_out = mm_tn(og_b, dy, tn=256, out_dtype=BF16, name="dw_b_out")
    dqkv_b, dz_b, d_gains_qk = attn_bwd(proj_b, cos_t, sin_t, gains_qk, d_og_b, o_b, lse, n_seq)
    dh1 = mm_nt_pieces(dqkv_b, w_b_in, name="dh1_qkv")
    dh1 = mm_nt(dz_b, w_b_in, dh1, tn=B_W, col_off=3 * B_GROUPS, name="dh1_z")
    dw_b_in = mm_tn_b_in(h1, dqkv_b, dz_b, out_dtype=BF16, name="dw_b_in")
    dx1, d_gain1 = rms_bwd(x1, gains_model, 1, dh1, dy)

    dw_a_out = mm_tn(og_a, dx1, tn=128, out_dtype=BF16, name="dw_a_out")
    early = [dw_b_in, dw_b_out.reshape(N_DEV, B_W // N_DEV, d), dw_a_out.reshape(N_DEV, A_VW // N_DEV, d)]
    lands = [_own_slot_only(t, lax.dynamic_index_in_dim(t, my_slot, 0, keepdims=False), my_slot) for t in early]
    g_send, g_recv, early, lands, g_token = exchange_start(early, lands, scatter=True, name="scatter_grads_start")

    d_og_a = mm_nt(dx1, w_a_out, tn=512, name="d_og_a", after=g_token)
    do_a, dz_a, d_gain_a_out = a_post_bwd(o_a, proj_a, gain_a_out, d_og_a)
    dqkv_a, dgates = gdn_bwd(qkv, gates_col, gates_row, tinv, states, do_a, n_seq)
    d_pre, d_conv = a_pre_bwd(proj_a, conv_w8, dqkv_a, n_seq)
    d_gate_logits, d_gate_prm = a_gates_bwd(proj_a, gate_prm, dgates, n_seq)
    dw_a_in = jnp.concatenate([
        mm_tn(h0, d_pre, tn=256, out_dtype=BF16, name="dw_a_in_qkv"),
        mm_tn(h0, dz_a, tn=256, out_dtype=BF16, name="dw_a_in_z"),
        mm_tn(h0, d_gate_logits, tn=128, out_dtype=BF16, name="dw_a_in_gates"),
    ], axis=1)[:, :A_IN]
    shard_a_in = A_IN // N_DEV
    last = [dw_a_in.reshape(d, N_DEV, shard_a_in).transpose(1, 0, 2)]
    last_lands = [_own_slot_only(t, lax.dynamic_index_in_dim(t, my_slot, 0, keepdims=False), my_slot) for t in last]
    l_send, l_recv, last, last_lands, l_token = exchange_start(last, last_lands, scatter=True,
                                                               name="scatter_last_start")
    dh0 = mm_nt(d_pre, w_a_in, tn=512, name="dh0_qkv", after=l_token)
    dh0 = mm_nt(dz_a, w_a_in, dh0, tn=512, col_off=(2 * A_QK + A_VW) // 512, name="dh0_z")
    dh0 = mm_nt(d_gate_logits, w_a_in, dh0, tn=128, col_off=A_GATE_COL, name="dh0_gates")
    dx0, d_gain0 = rms_bwd(x0, gains_model, 0, dh0, dx1)

    small = jnp.concatenate([
        d_gain0[0].reshape(8, 128), d_gain1[0].reshape(8, 128), d_gate_prm,
        _pad_rows(d_gain_a_out[0].reshape(2, 128)), d_gains_qk], axis=0)
    r_small, r_conv = all_gather([small, d_conv], name="gather_small_grads")
    conv_cols = a_conv_w.shape[2]
    r_conv = lax.dynamic_slice(r_conv, (0, 0, my_slot * conv_cols), (N_DEV, 8, conv_cols))

    r_b_in, r_b_out, r_a_out = exchange_wait(g_send, g_recv, early, lands, r_small, scatter=True,
                                             name="scatter_grads_wait")
    (r_a_in,) = exchange_wait(l_send, l_recv, last, last_lands, r_small, scatter=True, name="scatter_last_wait")

    upd = {}
    upd["a_w_in"] = adamw(r_a_in, a_w_in, m_a_w_in, v_a_w_in, name="adamw_a_w_in")
    upd["a_w_out"] = adamw(r_a_out, a_w_out, m_a_w_out, v_a_w_out, name="adamw_a_w_out")
    upd["b_w_in"] = adamw(r_b_in, b_w_in, m_b_w_in, v_b_w_in, name="adamw_b_w_in")
    upd["b_w_out"] = adamw(r_b_out, b_w_out, m_b_w_out, v_b_w_out, name="adamw_b_w_out")
    upd["a_conv_w"] = [t[:, :A_CONV] for t in adamw(
        r_conv, _pad_rows(a_conv_w[0])[None], _pad_rows(m_a_conv_w[0])[None], _pad_rows(v_a_conv_w[0])[None],
        name="adamw_a_conv_w")]
    small_upd = adamw(
        r_small,
        _pack_small(norm_g, a_log, a_dt_bias, a_norm_g, b_q_norm_g, b_k_norm_g)[None],
        _pack_small(m_norm_g, m_a_log, m_a_dt_bias, m_a_norm_g, m_b_q_norm_g, m_b_k_norm_g)[None],
        _pack_small(v_norm_g, v_a_log, v_a_dt_bias, v_a_norm_g, v_b_q_norm_g, v_b_k_norm_g)[None],
        name="adamw_small")
    small_names = ("norm_g", "a_log", "a_dt_bias", "a_norm_g", "b_q_norm_g", "b_k_norm_g")
    unpacked = [_unpack_small(t[0]) for t in small_upd]
    for i, nm in enumerate(small_names):
        upd[nm] = [u[i] for u in unpacked]

    order = ("norm_g", "a_w_in", "a_conv_w", "a_log", "a_dt_bias", "a_norm_g", "a_w_out",
             "b_w_in", "b_q_norm_g", "b_k_norm_g", "b_w_out")
    outs = [loss, dx0.reshape(n_seq, s_len, d)]
    for kind in range(4):
        for nm in order:
            outs.append(upd[nm][kind])
    return tuple(outs)
```

```python
import functools
import math

import jax
import jax.numpy as jnp
from jax import lax
from jax.experimental import pallas as pl
from jax.experimental.pallas import tpu as pltpu

F32 = jnp.float32
BF16 = jnp.bfloat16

D_MODEL = 1024
EPS = 1e-6
N_DEV = 8

A_HEADS = 8
A_DK = 128
A_DV = 256
A_QK = A_HEADS * A_DK
A_VW = A_HEADS * A_DV
A_CONV = 4
CHUNK = 64
A_IN = 2 * A_QK + 2 * A_VW + 2 * A_HEADS
A_IN_PAD = 2 * A_QK + 2 * A_VW + 128
A_GATE_COL = (2 * A_QK + 2 * A_VW) // 128

B_DILATIONS = (1, 4, 16)
B_GROUPS = 3
B_HEADS = 8
B_DH = 128
B_W = B_HEADS * B_DH
B_BLOCK = 128
B_PIECES = 3 * B_GROUPS + 1
ROPE_THETA = 500000.0
ROPE_DIMS = B_DH // 4
ROPE_HALF = ROPE_DIMS // 2

ADAM_LR = 0.001
ADAM_B1 = 0.9
ADAM_B2 = 0.999
ADAM_EPS = 1e-08
ADAM_WD = 0.01
ADAM_STEP = 10

VMEM_LIMIT = 56 * 1024 * 1024


def _params(sem):
    return pltpu.CompilerParams(dimension_semantics=sem, vmem_limit_bytes=VMEM_LIMIT)


def _bf(x):
    return x.astype(BF16)


def _mm(a, b):
    return jnp.dot(_bf(a), _bf(b), preferred_element_type=F32)


def _mm_nt(a, b):
    return lax.dot_general(_bf(a), _bf(b), (((1,), (1,)), ((), ())), preferred_element_type=F32)


def _mm_tn(a, b):
    return lax.dot_general(_bf(a), _bf(b), (((0,), (0,)), ((), ())), preferred_element_type=F32)


def _split(x):
    hi = _bf(x)
    return hi, _bf(x - hi.astype(F32))


def _mm3(a, b):
    ah, al = _split(a)
    bh, bl = _split(b)
    d = functools.partial(jnp.dot, preferred_element_type=F32)
    return d(ah, bh) + (d(ah, bl) + d(al, bh))


def _colsum_as_col(z):
    zh, zl = _split(z)
    ones = jnp.ones((z.shape[0], 128), BF16)
    tn = functools.partial(lax.dot_general, dimension_numbers=(((0,), (0,)), ((), ())),
                           preferred_element_type=F32)
    return (tn(zh, ones) + tn(zl, ones))[:, 0:1]


def _sigmoid(x):
    return 0.5 * jnp.tanh(0.5 * x) + 0.5


INV_BASE = 8
GDN_GROUP = 4
SUPER = GDN_GROUP * CHUNK

A_K_COL = A_QK // A_DK
A_V_COL = 2 * A_QK // A_DV
A_HEAD_W = 2 * A_DK + A_DV


def _inverse_steps(m, row, col):
    eye = (row == col).astype(F32)
    d = jnp.where(row // INV_BASE == col // INV_BASE, m, 0.0)
    x = eye - d
    p = _mm3(d, d)
    yield
    steps = int(math.log2(INV_BASE)) - 1
    for i in range(steps):
        x = x + _mm3(x, p)
        if i + 1 < steps:
            p = _mm3(p, p)
        yield
    size = INV_BASE
    while size < CHUNK:
        c = jnp.where((row // (2 * size) == col // (2 * size)) & (row // size != col // size), m, 0.0)
        xc = _mm3(x, c)
        yield
        x = x - _mm3(xc, x)
        yield
        size *= 2
    return x


def _drain(gen):
    while True:
        try:
            next(gen)
        except StopIteration as stop:
            return stop.value


def _interleave(*gens):
    live = list(gens)
    while live:
        for g in list(live):
            try:
                next(g)
            except StopIteration:
                live.remove(g)


def _diag_blocks_tall(x):
    return jnp.concatenate([x[i * CHUNK:(i + 1) * CHUNK, i * CHUNK:(i + 1) * CHUNK] for i in range(GDN_GROUP)], axis=0)


def _tall_to_block_diag(t, same):
    return jnp.where(same, jnp.concatenate([t] * GDN_GROUP, axis=1), 0.0)


def _block_sum(same, x):
    xh, xl = _split(jnp.broadcast_to(x, (SUPER, 128)))
    ones = same.astype(BF16)
    d = functools.partial(jnp.dot, preferred_element_type=F32)
    return (d(ones, xh) + d(ones, xl))[:, 0:1]


def _super_rows(i):
    return pl.ds(pl.multiple_of(i * SUPER, SUPER), SUPER)


def _chunk_rows(n):
    return pl.ds(pl.multiple_of(n * CHUNK, CHUNK), CHUNK)


def _gdn_super_steps(q, k, v, gcb, gr, head, tinv_tall=None):
    lane = lax.broadcasted_iota(jnp.int32, (SUPER, 128), 1)
    row = lax.broadcasted_iota(jnp.int32, (SUPER, SUPER), 0)
    col = lax.broadcasted_iota(jnp.int32, (SUPER, SUPER), 1)
    same = row // CHUNK == col // CHUNK
    beta = jnp.sum(jnp.where(lane == head, gcb, 0.0), axis=1, keepdims=True)
    gc = jnp.sum(jnp.where(lane == A_HEADS + head, gcb, 0.0), axis=1, keepdims=True)
    g_last = jnp.sum(jnp.where(col == (row // CHUNK) * CHUNK + (CHUNK - 1), gr, 0.0), axis=1, keepdims=True)
    gamma = jnp.exp(gc)
    decay = jnp.where(same & (row >= col), jnp.exp(jnp.minimum(gc - gr, 0.0)), 0.0)
    kb = k * beta
    m = jnp.where(same & (row > col), _mm_nt(kb, k) * decay, 0.0)
    p = jnp.where(same & (row >= col), _mm_nt(q, k) * decay, 0.0)
    yield
    if tinv_tall is None:
        tinv = yield from _inverse_steps(m, row, col)
    else:
        tinv = _tall_to_block_diag(tinv_tall, same)
    u = _mm(tinv, v * beta)
    w = _mm(tinv, kb * gamma)
    yield
    e_tail = jnp.exp(g_last - gc)
    return dict(beta=beta, gc=gc, g_last=g_last, gamma=gamma, decay=decay, kb=kb, m=m,
                tinv=tinv, u=u, w=w, p=p, e_tail=e_tail, row=row, col=col, lane=lane, same=same)


def _gdn_super_common(q, k, v, gcb, gr, head, tinv_tall=None):
    return _drain(_gdn_super_steps(q, k, v, gcb, gr, head, tinv_tall))


def _store_scan_operands(rows, q, k, t, u_scr, w_scr, p_scr, qg_scr, ke_scr, gl_scr):
    u_scr[rows, :] = t["u"]
    w_scr[rows, :] = _bf(t["w"])
    p_scr[rows, :] = _bf(_diag_blocks_tall(t["p"]))
    qg_scr[rows, :] = _bf(q * t["gamma"])
    ke_scr[rows, :] = _bf(k * t["e_tail"])
    gl_scr[rows, :] = jnp.broadcast_to(jnp.exp(t["g_last"]), (SUPER, 128))


def _gdn_fwd_body(q_ref, k_ref, v_ref, gc_ref, gr_ref, o_ref, tinv_ref, st_ref, s_scr, *sets):
    head = pl.program_id(1)
    n_super = q_ref.shape[0] // SUPER
    set_a, set_b, set_c, set_d = (sets[6 * i:6 * i + 6] for i in range(4))
    whole = pl.ds(0, SUPER)

    def prepare_steps(i, dst):
        rows = _super_rows(i)
        q, k = q_ref[rows, :], k_ref[rows, :]
        t = yield from _gdn_super_steps(q, k, v_ref[rows, :], gc_ref[rows, :], gr_ref[i], head)
        tinv_ref[rows, :] = _diag_blocks_tall(t["tinv"])
        _store_scan_operands(whole, q, k, t, *dst)

    def scan_steps(i, src):
        u_scr, w_scr, p_scr, qg_scr, ke_scr, gl_scr = src
        for j in range(GDN_GROUP):
            n = i * GDN_GROUP + j
            local = pl.ds(j * CHUNK, CHUNK)
            s = s_scr[...]
            st_ref[n] = s
            sb = _bf(s)
            ws = jnp.dot(w_scr[local, :], sb, preferred_element_type=F32)
            yield
            vb = _bf(u_scr[local, :] - ws)
            o = (jnp.dot(qg_scr[local, :], sb, preferred_element_type=F32)
                 + jnp.dot(p_scr[local, :], vb, preferred_element_type=F32))
            s_new = s * gl_scr[local, :][0:1, 0:1] + lax.dot_general(
                ke_scr[local, :], vb, (((0,), (0,)), ((), ())), preferred_element_type=F32)
            yield
            o_ref[_chunk_rows(n), :] = o
            s_scr[...] = s_new

    def scan_two(i, src0, src1):
        yield from scan_steps(i, src0)
        yield from scan_steps(i + 1, src1)

    _interleave(prepare_steps(0, set_a), prepare_steps(1, set_b))
    s_scr[...] = jnp.zeros_like(s_scr)

    def four(m, carry):
        i = 4 * m
        _interleave(scan_two(i, set_a, set_b), prepare_steps(i + 2, set_c), prepare_steps(i + 3, set_d))
        _interleave(scan_two(i + 2, set_c, set_d), prepare_steps(i + 4, set_a), prepare_steps(i + 5, set_b))
        return carry

    lax.fori_loop(0, n_super // 4 - 1, four, 0)
    i = n_super - 4
    _interleave(scan_two(i, set_a, set_b), prepare_steps(i + 2, set_c), prepare_steps(i + 3, set_d))
    _drain(scan_two(i + 2, set_c, set_d))


def _gdn_in_specs(s_len, n_super):
    return [
        pl.BlockSpec((s_len, A_DK), lambda b, h: (b, h)),
        pl.BlockSpec((s_len, A_DK), lambda b, h: (b, A_K_COL + h)),
        pl.BlockSpec((s_len, A_DV), lambda b, h: (b, A_V_COL + h)),
        pl.BlockSpec((s_len, 128), lambda b, h: (b, 0)),
        pl.BlockSpec((None, None, n_super, 1, SUPER), lambda b, h: (b, A_HEADS + h, 0, 0, 0)),
    ]


def _gdn_scan_scratch(s_len):
    return [pltpu.VMEM((A_DK, A_DV), F32), pltpu.VMEM((s_len, A_DV), F32),
            pltpu.VMEM((s_len, A_DK), BF16), pltpu.VMEM((s_len, CHUNK), BF16),
            pltpu.VMEM((s_len, A_DK), BF16), pltpu.VMEM((s_len, A_DK), BF16),
            pltpu.VMEM((s_len, 128), F32)]


def gdn_fwd(qkv, gates_col, gates_row, n_seq):
    t_rows = qkv.shape[0]
    s_len = t_rows // n_seq
    n_chunks = s_len // CHUNK
    return pl.pallas_call(
        _gdn_fwd_body, name="gdn_fwd", grid=(n_seq, A_HEADS),
        in_specs=_gdn_in_specs(s_len, s_len // SUPER),
        out_specs=[
            pl.BlockSpec((s_len, A_DV), lambda b, h: (b, h)),
            pl.BlockSpec((s_len, CHUNK), lambda b, h: (b * A_HEADS + h, 0)),
            pl.BlockSpec((None, n_chunks, A_DK, A_DV), lambda b, h: (b * A_HEADS + h, 0, 0, 0)),
        ],
        out_shape=[
            jax.ShapeDtypeStruct((t_rows, A_VW), F32),
            jax.ShapeDtypeStruct((n_seq * A_HEADS * s_len, CHUNK), F32),
            jax.ShapeDtypeStruct((n_seq * A_HEADS, n_chunks, A_DK, A_DV), F32),
        ],
        scratch_shapes=_gdn_scan_scratch(SUPER) + 3 * _gdn_scan_scratch(SUPER)[1:],
        compiler_params=_params(("arbitrary", "arbitrary")),
    )(qkv, qkv, qkv, gates_col, gates_row)


def _gdn_bwd_body(q_ref, k_ref, v_ref, gc_ref, gr_ref, tinv_ref, st_ref, do_ref,
                  dqkv_ref, dgc_ref, ds_scr, *sets):
    head = pl.program_id(1)
    n_super = q_ref.shape[0] // SUPER
    ops = (sets[0:6], sets[6:12])
    res = (sets[12:18], sets[18:24])
    whole = pl.ds(0, SUPER)
    tn = functools.partial(lax.dot_general, dimension_numbers=(((0,), (0,)), ((), ())), preferred_element_type=F32)
    nt = functools.partial(lax.dot_general, dimension_numbers=(((1,), (1,)), ((), ())), preferred_element_type=F32)

    @pl.when(head == 0)
    def _():
        dgc_ref[...] = jnp.zeros_like(dgc_ref)

    def common_steps(i):
        rows = _super_rows(i)
        q, k, v = q_ref[rows, :], k_ref[rows, :], v_ref[rows, :]
        t = yield from _gdn_super_steps(q, k, v, gc_ref[rows, :], gr_ref[i], head, tinv_tall=tinv_ref[rows, :])
        return rows, q, k, v, t

    def stage_p(i, parity):
        _, q, k, _, t = yield from common_steps(i)
        _store_scan_operands(whole, q, k, t, *ops[parity])

    def stage_s(i, parity):
        u_scr, w_scr, p_scr, qg_scr, ke_scr, gl_scr = ops[parity]
        vn_scr, dvn_scr, dqg_scr, dw_scr, dkt_scr, sds_scr = res[parity]
        for j in reversed(range(GDN_GROUP)):
            n = i * GDN_GROUP + j
            local = pl.ds(j * CHUNK, CHUNK)
            ds_next = ds_scr[...]
            dsb = _bf(ds_next)
            s = st_ref[n]
            sb = _bf(s)
            d_ob = _bf(do_ref[_chunk_rows(n), :])
            w_s = jnp.dot(w_scr[local, :], sb, preferred_element_type=F32)
            d_vn = tn(p_scr[local, :], d_ob) + jnp.dot(ke_scr[local, :], dsb, preferred_element_type=F32)
            d_qg = nt(d_ob, sb)
            qg_do = tn(qg_scr[local, :], d_ob)
            yield
            v_new = u_scr[local, :] - w_s
            d_vnb = _bf(d_vn)
            d_w = -nt(d_vnb, sb)
            d_kt = nt(_bf(v_new), dsb)
            w_dvn = tn(w_scr[local, :], d_vnb)
            yield
            vn_scr[local, :] = v_new
            dvn_scr[local, :] = d_vn
            dqg_scr[local, :] = d_qg
            dw_scr[local, :] = d_w
            dkt_scr[local, :] = d_kt
            sds = jnp.sum(jnp.sum(s * ds_next, axis=1, keepdims=True), axis=0, keepdims=True)
            sds_scr[local, :] = jnp.broadcast_to(sds, (CHUNK, 128))
            ds_scr[...] = qg_do + gl_scr[local, :][0:1, 0:1] * ds_next - w_dvn

    def stage_f(i, parity):
        vn_scr, dvn_scr, dqg_scr, dw_scr, dkt_scr, sds_scr = res[parity]
        rows, q, k, v, t = yield from common_steps(i)
        beta, gamma, decay, kb, e_tail = t["beta"], t["gamma"], t["decay"], t["kb"], t["e_tail"]
        row, col, lane, same = t["row"], t["col"], t["lane"], t["same"]
        d_o = do_ref[rows, :]
        v_new, d_vn = vn_scr[...], dvn_scr[...]
        d_qg, d_w, d_kt = dqg_scr[...], dw_scr[...], dkt_scr[...]
        gamma_last = jnp.exp(t["g_last"])

        d_p = jnp.where(same & (row >= col), _mm_nt(d_o, v_new), 0.0)
        d_ru = _mm_tn(t["tinv"], d_vn)
        d_rw = _mm_tn(t["tinv"], d_w)
        yield
        d_m = jnp.where(same & (row > col), -(_mm_nt(d_ru, t["u"]) + _mm_nt(d_rw, t["w"])), 0.0)
        yield

        x_p = d_p * decay
        y_m = d_m * decay
        d_kb = _mm(y_m, k) + d_rw * gamma
        dqkv_ref[rows, 0:A_DK] = _mm(x_p, k) + d_qg * gamma
        dqkv_ref[rows, A_DK:2 * A_DK] = _mm_tn(x_p, q) + _mm_tn(y_m, kb) + d_kb * beta + d_kt * e_tail
        dqkv_ref[rows, 2 * A_DK:A_HEAD_W] = d_ru * beta

        d_beta = (jnp.sum(d_ru * v, axis=1, keepdims=True)
                  + jnp.sum(d_kb * k, axis=1, keepdims=True))
        z = d_p * t["p"] + d_m * t["m"]
        eps_tail = jnp.sum(d_kt * k, axis=1, keepdims=True) * e_tail
        d_gc = (jnp.sum(z, axis=1, keepdims=True) - _colsum_as_col(z)
                + jnp.sum(d_qg * q, axis=1, keepdims=True) * gamma
                + jnp.sum(d_rw * kb, axis=1, keepdims=True) * gamma
                - eps_tail)
        d_glast = _block_sum(same, eps_tail) + gamma_last * sds_scr[...][:, 0:1]
        yield
        rcol = lax.broadcasted_iota(jnp.int32, (SUPER, 1), 0)
        d_gc = d_gc + jnp.where(rcol % CHUNK == CHUNK - 1, d_glast, 0.0)
        dgc_ref[rows, :] += (jnp.where(lane == head, d_beta, 0.0)
                             + jnp.where(lane == A_HEADS + head, d_gc, 0.0))

    last = n_super - 1
    _drain(stage_p(last, 1))
    ds_scr[...] = jnp.zeros_like(ds_scr)
    _interleave(stage_s(last, 1), stage_p(last - 1, 0))

    def pair(k, carry):
        i = last - 1 - 2 * k
        _interleave(stage_s(i, 0), stage_f(i + 1, 1), stage_p(i - 1, 1))
        _interleave(stage_s(i - 1, 1), stage_f(i, 0), stage_p(i - 2, 0))
        return carry

    lax.fori_loop(0, n_super // 2 - 1, pair, 0)
    _interleave(stage_s(0, 0), stage_f(1, 1))
    _drain(stage_f(0, 0))


def gdn_bwd(qkv, gates_col, gates_row, tinv, states, d_o, n_seq):
    t_rows = qkv.shape[0]
    s_len = t_rows // n_seq
    n_chunks = s_len // CHUNK
    v_spec = pl.BlockSpec((s_len, A_DV), lambda b, h: (b, h))
    gate_spec = pl.BlockSpec((s_len, 128), lambda b, h: (b, 0))
    return pl.pallas_call(
        _gdn_bwd_body, name="gdn_bwd", grid=(n_seq, A_HEADS),
        in_specs=_gdn_in_specs(s_len, s_len // SUPER) + [
            pl.BlockSpec((s_len, CHUNK), lambda b, h: (b * A_HEADS + h, 0)),
            pl.BlockSpec((None, n_chunks, A_DK, A_DV), lambda b, h: (b * A_HEADS + h, 0, 0, 0)),
            v_spec,
        ],
        out_specs=[pl.BlockSpec((s_len, A_HEAD_W), lambda b, h: (b, h)), gate_spec],
        out_shape=[
            jax.ShapeDtypeStruct((t_rows, A_HEADS * A_HEAD_W), F32),
            jax.ShapeDtypeStruct((t_rows, 128), F32),
        ],
        scratch_shapes=(_gdn_scan_scratch(SUPER) + _gdn_scan_scratch(SUPER)[1:]
                        + 2 * [pltpu.VMEM((SUPER, A_DV), F32), pltpu.VMEM((SUPER, A_DV), F32),
                               pltpu.VMEM((SUPER, A_DK), F32), pltpu.VMEM((SUPER, A_DK), F32),
                               pltpu.VMEM((SUPER, A_DK), F32), pltpu.VMEM((SUPER, 128), F32)]),
        compiler_params=_params(("arbitrary", "arbitrary")),
    )(qkv, qkv, qkv, gates_col, gates_row, tinv, states, d_o)


def _shift_down(x, j, rows):
    r = pltpu.roll(x, j, 0)
    return jnp.concatenate([jnp.where(rows[:8] >= j, r[:8], 0.0), r[8:]], axis=0)


def _shift_up(x, j, rows):
    n = x.shape[0]
    r = pltpu.roll(x, n - j, 0)
    return jnp.concatenate([r[:n - 8], jnp.where(rows[n - 8:] < n - j, r[n - 8:], 0.0)], axis=0)


def _conv_silu_norm(x, w, cb):
    rows = lax.broadcasted_iota(jnp.int32, (x.shape[0], 1), 0)
    c = x * w[A_CONV - 1:A_CONV, :]
    for j in range(1, A_CONV):
        c = c + _shift_down(x, j, rows) * w[A_CONV - 1 - j:A_CONV - j, :]
    sig = _sigmoid(c)
    a = c * sig
    rn = lax.rsqrt(jnp.sum(a * a, axis=1, keepdims=True) + EPS)
    is_q = cb < A_QK // 128
    is_qk = cb < 2 * A_QK // 128
    q_scale = jnp.where(is_q, A_DK ** -0.5, 1.0).astype(F32)
    return rows, c, sig, a, rn, is_qk, q_scale


def _a_pre_fwd_body(x_ref, w_ref, o_ref):
    cb = pl.program_id(1)
    _, _, _, a, rn, is_qk, q_scale = _conv_silu_norm(x_ref[...], w_ref[...], cb)
    o_ref[...] = a * jnp.where(is_qk, rn * q_scale, 1.0)


def a_pre_fwd(proj_a, conv_w8, n_seq):
    t_rows = proj_a.shape[0]
    s_len = t_rows // n_seq
    n_cb = (2 * A_QK + A_VW) // 128
    blk = pl.BlockSpec((s_len, 128), lambda b, c: (b, c))
    return pl.pallas_call(
        _a_pre_fwd_body, name="a_pre_fwd", grid=(n_seq, n_cb),
        in_specs=[blk, pl.BlockSpec((8, 128), lambda b, c: (0, c))],
        out_specs=blk,
        out_shape=jax.ShapeDtypeStruct((t_rows, n_cb * 128), F32),
        compiler_params=_params(("arbitrary", "arbitrary")),
    )(proj_a, conv_w8)


def _a_pre_bwd_body(x_ref, w_ref, dy_ref, dx_ref, dw_ref):
    b, cb = pl.program_id(1), pl.program_id(0)
    x, w = x_ref[...], w_ref[...]
    rows, c, sig, a, rn, is_qk, q_scale = _conv_silu_norm(x, w, cb)
    dy = dy_ref[...]
    da_n = q_scale * (rn * dy - a * (rn * rn * rn) * jnp.sum(dy * a, axis=1, keepdims=True))
    da = jnp.where(is_qk, da_n, dy)
    dc = da * (sig * (1.0 + c * (1.0 - sig)))
    @pl.when(b == 0)
    def _():
        dw_ref[...] = jnp.zeros_like(dw_ref)

    dx = dc * w[A_CONV - 1:A_CONV, :]
    dw_ref[A_CONV - 1:A_CONV, :] += jnp.sum(dc * x, axis=0, keepdims=True)
    for j in range(1, A_CONV):
        dcs = _shift_up(dc, j, rows)
        dx = dx + dcs * w[A_CONV - 1 - j:A_CONV - j, :]
        dw_ref[A_CONV - 1 - j:A_CONV - j, :] += jnp.sum(dcs * x, axis=0, keepdims=True)
    dx_ref[...] = dx.astype(dx_ref.dtype)


def a_pre_bwd(proj_a, conv_w8, dqkv_hm, n_seq):
    t_rows = proj_a.shape[0]
    s_len = t_rows // n_seq
    n_cb = (2 * A_QK + A_VW) // 128
    blk = pl.BlockSpec((s_len, 128), lambda c, b: (b, c))
    wblk = pl.BlockSpec((8, 128), lambda c, b: (0, c))
    per_head = A_HEAD_W // 128
    n_q = A_QK // 128

    def head_major(c, b):
        v_blk = jnp.maximum(c - 2 * n_q, 0)
        col = jnp.where(c < n_q, c * per_head,
                        jnp.where(c < 2 * n_q, (c - n_q) * per_head + 1,
                                  (v_blk // 2) * per_head + 2 + v_blk % 2))
        return (b, col)

    return pl.pallas_call(
        _a_pre_bwd_body, name="a_pre_bwd", grid=(n_cb, n_seq),
        in_specs=[blk, wblk, pl.BlockSpec((s_len, 128), head_major)],
        out_specs=[blk, wblk],
        out_shape=[jax.ShapeDtypeStruct((t_rows, n_cb * 128), BF16),
                   jax.ShapeDtypeStruct((8, n_cb * 128), F32)],
        compiler_params=_params(("arbitrary", "arbitrary")),
    )(proj_a, conv_w8, dqkv_hm)


GATE_TILE = 512


def _softplus(y):
    return jnp.maximum(y, 0.0) + jnp.log1p(jnp.exp(-jnp.abs(y)))


def _gate_values(x, prm):
    beta = _sigmoid(x)
    y = x + prm[1:2, :]
    neg_a = -jnp.exp(prm[0:1, :])
    g = neg_a * _softplus(y)
    return beta, y, neg_a, g


def _a_gates_fwd_body(x_ref, prm_ref, gc_ref, gr_ref):
    x = x_ref[...]
    tm = x.shape[0]
    beta, _, _, g = _gate_values(x, prm_ref[...])
    in_chunk = lax.broadcasted_iota(jnp.int32, (tm, 1), 0) % CHUNK
    s = 1
    while s < CHUNK:
        g = g + jnp.where(in_chunk >= s, pltpu.roll(g, s, 0), 0.0)
        s *= 2
    lane = lax.broadcasted_iota(jnp.int32, x.shape, 1)
    out = jnp.where(lane < A_HEADS, beta, jnp.where(lane < 2 * A_HEADS, g, 0.0))
    gc_ref[...] = out
    gr_ref[...] = out.T[0:2 * A_HEADS, :]


def a_gates_fwd(proj_a, prm, n_seq):
    t_rows = proj_a.shape[0]
    s_len = t_rows // n_seq
    tm = min(GATE_TILE, s_len)
    n_t = s_len // tm
    return pl.pallas_call(
        _a_gates_fwd_body, name="a_gates_fwd", grid=(n_seq, n_t),
        in_specs=[pl.BlockSpec((tm, 128), lambda b, i: (b * n_t + i, A_GATE_COL)),
                  pl.BlockSpec((8, 128), lambda b, i: (0, 0))],
        out_specs=[pl.BlockSpec((tm, 128), lambda b, i: (b * n_t + i, 0)),
                   pl.BlockSpec((None, 2 * A_HEADS, tm), lambda b, i: (b, 0, i))],
        out_shape=[jax.ShapeDtypeStruct((t_rows, 128), F32),
                   jax.ShapeDtypeStruct((n_seq, 2 * A_HEADS, s_len), F32)],
        compiler_params=_params(("arbitrary", "arbitrary")),
    )(proj_a, prm)


def _a_gates_bwd_body(x_ref, prm_ref, dgc_ref, dx_ref, dprm_ref):
    first = (pl.program_id(0) == 0) & (pl.program_id(1) == 0)
    x = x_ref[...]
    tm = x.shape[0]
    beta, y, neg_a, g = _gate_values(x, prm_ref[...])
    d = dgc_ref[...]
    in_chunk = lax.broadcasted_iota(jnp.int32, (tm, 1), 0) % CHUNK
    dg = d
    s = 1
    while s < CHUNK:
        dg = dg + jnp.where(in_chunk < CHUNK - s, pltpu.roll(dg, tm - s, 0), 0.0)
        s *= 2
    lane = lax.broadcasted_iota(jnp.int32, x.shape, 1)
    is_decay = (lane >= A_HEADS) & (lane < 2 * A_HEADS)
    d_alogit = jnp.where(is_decay, dg * neg_a * _sigmoid(y), 0.0)
    dx_ref[...] = jnp.where(lane < A_HEADS, d * beta * (1.0 - beta), d_alogit).astype(dx_ref.dtype)

    @pl.when(first)
    def _():
        dprm_ref[...] = jnp.zeros_like(dprm_ref)

    dprm_ref[0:1, :] += jnp.sum(jnp.where(is_decay, dg * g, 0.0), axis=0, keepdims=True)
    dprm_ref[1:2, :] += jnp.sum(d_alogit, axis=0, keepdims=True)


def a_gates_bwd(proj_a, prm, dgates_col, n_seq):
    t_rows = proj_a.shape[0]
    s_len = t_rows // n_seq
    tm = min(GATE_TILE, s_len)
    n_t = s_len // tm
    return pl.pallas_call(
        _a_gates_bwd_body, name="a_gates_bwd", grid=(n_seq, n_t),
        in_specs=[pl.BlockSpec((tm, 128), lambda b, i: (b * n_t + i, A_GATE_COL)),
                  pl.BlockSpec((8, 128), lambda b, i: (0, 0)),
                  pl.BlockSpec((tm, 128), lambda b, i: (b * n_t + i, 0))],
        out_specs=[pl.BlockSpec((tm, 128), lambda b, i: (b * n_t + i, 0)),
                   pl.BlockSpec((8, 128), lambda b, i: (0, 0))],
        out_shape=[jax.ShapeDtypeStruct((t_rows, 128), BF16),
                   jax.ShapeDtypeStruct((8, 128), F32)],
        compiler_params=_params(("arbitrary", "arbitrary")),
    )(proj_a, prm, dgates_col)


ROW_TILE = 512
A_Z_COL = (2 * A_QK + A_VW) // A_DV


def _silu_parts(z):
    sig = _sigmoid(z)
    return z * sig, sig * (1.0 + z * (1.0 - sig))


def _a_post_fwd_body(o_ref, z_ref, g_ref, og_ref):
    o = o_ref[...]
    r = lax.rsqrt(jnp.mean(o * o, axis=1, keepdims=True) + EPS)
    silu, _ = _silu_parts(z_ref[...])
    og_ref[...] = ((o * r * g_ref[0:1, :]) * silu).astype(og_ref.dtype)


def a_post_fwd(o, proj_a, norm_g8):
    t_rows = o.shape[0]
    tm = min(ROW_TILE, t_rows)
    blk = pl.BlockSpec((tm, A_DV), lambda i, h: (i, h))
    return pl.pallas_call(
        _a_post_fwd_body, name="a_post_fwd", grid=(t_rows // tm, A_HEADS),
        in_specs=[blk, pl.BlockSpec((tm, A_DV), lambda i, h: (i, A_Z_COL + h)),
                  pl.BlockSpec((8, A_DV), lambda i, h: (0, 0))],
        out_specs=blk,
        out_shape=jax.ShapeDtypeStruct((t_rows, A_VW), BF16),
        compiler_params=_params(("arbitrary", "arbitrary")),
    )(o, proj_a, norm_g8)


def _a_post_bwd_body(o_ref, z_ref, g_ref, dog_ref, do_ref, dz_ref, dg_ref):
    first = (pl.program_id(0) == 0) & (pl.program_id(1) == 0)
    o, z, d_og = o_ref[...], z_ref[...], dog_ref[...]
    gain = g_ref[0:1, :]
    r = lax.rsqrt(jnp.mean(o * o, axis=1, keepdims=True) + EPS)
    silu, dsilu = _silu_parts(z)
    xr = o * r
    d_on = d_og * silu
    dz_ref[...] = (d_og * (xr * gain) * dsilu).astype(dz_ref.dtype)
    u = d_on * gain
    do_ref[...] = r * u - xr * (r * r) * jnp.mean(u * o, axis=1, keepdims=True)

    @pl.when(first)
    def _():
        dg_ref[...] = jnp.zeros_like(dg_ref)

    dg_ref[0:1, :] += jnp.sum(d_on * xr, axis=0, keepdims=True)


def a_post_bwd(o, proj_a, norm_g8, d_og):
    t_rows = o.shape[0]
    tm = min(ROW_TILE, t_rows)
    blk = pl.BlockSpec((tm, A_DV), lambda i, h: (i, h))
    gblk = pl.BlockSpec((8, A_DV), lambda i, h: (0, 0))
    return pl.pallas_call(
        _a_post_bwd_body, name="a_post_bwd", grid=(t_rows // tm, A_HEADS),
        in_specs=[blk, pl.BlockSpec((tm, A_DV), lambda i, h: (i, A_Z_COL + h)), gblk, blk],
        out_specs=[blk, blk, gblk],
        out_shape=[jax.ShapeDtypeStruct((t_rows, A_VW), F32),
                   jax.ShapeDtypeStruct((t_rows, A_VW), BF16),
                   jax.ShapeDtypeStruct((8, A_DV), F32)],
        compiler_params=_params(("arbitrary", "arbitrary")),
    )(o, proj_a, norm_g8, d_og)


NEG_BIG = -1e30
ATT_SCALE = B_DH ** -0.5


def _swap_rope_halves(x):
    lane = lax.broadcasted_iota(jnp.int32, x.shape, 1)
    return jnp.where(lane < ROPE_HALF, pltpu.roll(x, B_DH - ROPE_HALF, 1),
                     jnp.where(lane < ROPE_DIMS, pltpu.roll(x, ROPE_HALF, 1), 0.0))


def _norm_rope(x, gain, cos_t, sin_t):
    r = lax.rsqrt(jnp.mean(x * x, axis=1, keepdims=True) + EPS)
    xn = x * r * gain
    return xn * cos_t + _swap_rope_halves(xn) * sin_t, r


def _norm_rope_bwd(x, r, gain, cos_t, sin_t, dy):
    d_xn = dy * cos_t + _swap_rope_halves(dy * sin_t)
    xr = x * r
    u = d_xn * gain
    dx = r * u - xr * (r * r) * jnp.mean(u * x, axis=1, keepdims=True)
    return dx, jnp.sum(d_xn * xr, axis=0, keepdims=True)


def _stream_rows(idx, dilation, s_len):
    nb = s_len // dilation // B_BLOCK
    r = idx // nb
    m = idx % nb
    cur = r + m * (B_BLOCK * dilation)
    prev = r + jnp.maximum(m - 1, 0) * (B_BLOCK * dilation)
    return cur, prev, m > 0


def _rows(start, dilation):
    if dilation == 1:
        return pl.ds(start, B_BLOCK)
    return pl.ds(start, B_BLOCK, stride=dilation)


ATT_UNROLL = 4


def _band_mask(has_prev):
    qi = lax.broadcasted_iota(jnp.int32, (B_BLOCK, 2 * B_BLOCK), 0)
    kj = lax.broadcasted_iota(jnp.int32, (B_BLOCK, 2 * B_BLOCK), 1)
    return ((kj < B_BLOCK) & (kj >= qi) & has_prev) | ((kj >= B_BLOCK) & (kj - B_BLOCK <= qi))


def _block_scores(qb, kc, kp, has_prev):
    qi = lax.broadcasted_iota(jnp.int32, (B_BLOCK, B_BLOCK), 0)
    kj = lax.broadcasted_iota(jnp.int32, (B_BLOCK, B_BLOCK), 1)
    s_c = jnp.where(qi >= kj, _mm_nt(qb, kc) * ATT_SCALE, NEG_BIG)
    s_p = jnp.where((kj >= qi) & has_prev, _mm_nt(qb, kp) * ATT_SCALE, NEG_BIG)
    return s_c, s_p


def _attn_fwd_body(qkv_ref, z_ref, cos_ref, sin_ref, gain_ref, og_ref, o_ref, lse_ref,
                   qn_scr, kn_scr, og_scr, lg_scr):
    head, grp = pl.program_id(1), pl.program_id(2)
    s_len = z_ref.shape[0]
    n_blocks = s_len // B_BLOCK
    cos_t, sin_t = cos_ref[...], sin_ref[...]

    for gi, dil in enumerate(B_DILATIONS):
        @pl.when(grp == gi)
        def _(gi=gi, dil=dil):
            qn_scr[...], _ = _norm_rope(qkv_ref[0], gain_ref[gi:gi + 1, :], cos_t, sin_t)
            kn_scr[...], _ = _norm_rope(qkv_ref[1], gain_ref[B_GROUPS + gi:B_GROUPS + gi + 1, :], cos_t, sin_t)

            ones = jnp.ones((2 * B_BLOCK, B_DH), BF16)

            def blocks(it, carry):
                scored = []
                for j in range(ATT_UNROLL):
                    cur, prev, has_prev = _stream_rows(it * ATT_UNROLL + j, dil, s_len)
                    rc, rp = _rows(cur, dil), _rows(prev, dil)
                    k2 = jnp.concatenate([kn_scr[rp, :], kn_scr[rc, :]], axis=0)
                    scored.append((rc, rp, has_prev, _mm_nt(qn_scr[rc, :], k2) * ATT_SCALE))
                summed = []
                for rc, rp, has_prev, s in scored:
                    s = jnp.where(_band_mask(has_prev), s, NEG_BIG)
                    mx = jnp.max(s, axis=1, keepdims=True)
                    v2 = jnp.concatenate([qkv_ref.at[2][rp, :], qkv_ref.at[2][rc, :]], axis=0)
                    acc = jnp.dot(_bf(jnp.exp(s - mx)), jnp.concatenate([_bf(v2), ones], axis=1),
                                  preferred_element_type=F32)
                    summed.append((rc, mx, acc))
                for rc, mx, acc in summed:
                    den = acc[:, B_DH:B_DH + 1]
                    og_scr.at[gi][rc, :] = acc[:, :B_DH] / den
                    lg_scr.at[gi][rc, :] = jnp.broadcast_to(mx + jnp.log(den), (B_BLOCK, B_DH))
                return carry

            lax.fori_loop(0, n_blocks // ATT_UNROLL, blocks, 0)

    @pl.when(grp == B_GROUPS - 1)
    def _():
        l0, l1, l2 = lg_scr[0], lg_scr[1], lg_scr[2]
        mx = jnp.maximum(jnp.maximum(l0, l1), l2)
        w0, w1, w2 = jnp.exp(l0 - mx), jnp.exp(l1 - mx), jnp.exp(l2 - mx)
        den = w0 + w1 + w2
        o = (w0 * og_scr[0] + w1 * og_scr[1] + w2 * og_scr[2]) / den
        silu, _ = _silu_parts(z_ref[...])
        o_ref[...] = o
        og_ref[...] = (o * silu).astype(og_ref.dtype)
        @pl.when(head == 0)
        def _():
            lse_ref[...] = jnp.zeros_like(lse_ref)

        lane = lax.broadcasted_iota(jnp.int32, o.shape, 1)
        lse_ref[...] = jnp.where(lane == head, mx + jnp.log(den), lse_ref[...])


def attn_fwd(proj_b, cos_t, sin_t, gains8, n_seq):
    t_rows = proj_b.shape[1]
    s_len = t_rows // n_seq
    head_blk = pl.BlockSpec((s_len, B_DH), lambda b, h, g: (b, h))
    seq_blk = pl.BlockSpec((s_len, 128), lambda b, h, g: (b, 0))
    return pl.pallas_call(
        _attn_fwd_body, name="attn_fwd", grid=(n_seq, B_HEADS, B_GROUPS),
        in_specs=[
            pl.BlockSpec((3, s_len, B_DH), lambda b, h, g: (g, b, h)),
            pl.BlockSpec((None, s_len, B_DH), lambda b, h, g: (B_PIECES - 1, b, h)),
            seq_blk, seq_blk,
            pl.BlockSpec((8, 128), lambda b, h, g: (0, 0)),
        ],
        out_specs=[head_blk, head_blk, seq_blk],
        out_shape=[jax.ShapeDtypeStruct((t_rows, B_W), BF16),
                   jax.ShapeDtypeStruct((t_rows, B_W), F32),
                   jax.ShapeDtypeStruct((t_rows, 128), F32)],
        scratch_shapes=[pltpu.VMEM((s_len, B_DH), F32), pltpu.VMEM((s_len, B_DH), F32),
                        pltpu.VMEM((B_GROUPS, s_len, B_DH), F32), pltpu.VMEM((B_GROUPS, s_len, B_DH), F32)],
        compiler_params=_params(("arbitrary", "arbitrary", "arbitrary")),
    )(proj_b, proj_b, cos_t, sin_t, gains8)


def _attn_bwd_body(qkv_ref, z_ref, cos_ref, sin_ref, gain_ref, dog_ref, o_ref, lse_ref,
                   dqkv_ref, dz_ref, dgain_ref,
                   qn_scr, kn_scr, dqn_scr, dkn_scr, do_scr, dl_scr, ls_scr, dv_scr):
    head, grp = pl.program_id(1), pl.program_id(2)
    first = (pl.program_id(0) == 0) & (head == 0) & (grp == 0)
    s_len = z_ref.shape[0]
    n_blocks = s_len // B_BLOCK
    cos_t, sin_t = cos_ref[...], sin_ref[...]

    @pl.when(first)
    def _():
        dgain_ref[...] = jnp.zeros_like(dgain_ref)

    @pl.when(grp == 0)
    def _():
        d_og, o = dog_ref[...], o_ref[...]
        silu, dsilu = _silu_parts(z_ref[...])
        d_o = d_og * silu
        dz_ref[...] = (d_og * o * dsilu).astype(dz_ref.dtype)
        do_scr[...] = d_o
        dl_scr[...] = jnp.broadcast_to(jnp.sum(d_o * o, axis=1, keepdims=True), o.shape)
        lane = lax.broadcasted_iota(jnp.int32, o.shape, 1)
        ls_scr[...] = jnp.broadcast_to(
            jnp.sum(jnp.where(lane == head, lse_ref[...], 0.0), axis=1, keepdims=True), o.shape)

    for gi, dil in enumerate(B_DILATIONS):
        @pl.when(grp == gi)
        def _(gi=gi, dil=dil):
            q_raw, k_raw = qkv_ref[0], qkv_ref[1]
            gq = gain_ref[gi:gi + 1, :]
            gk = gain_ref[B_GROUPS + gi:B_GROUPS + gi + 1, :]
            qn_scr[...], rq = _norm_rope(q_raw, gq, cos_t, sin_t)
            kn_scr[...], rk = _norm_rope(k_raw, gk, cos_t, sin_t)
            def blocks(it, carry):
                scored = []
                for j in range(ATT_UNROLL):
                    cur, prev, has_prev = _stream_rows(it * ATT_UNROLL + j, dil, s_len)
                    rc, rp = _rows(cur, dil), _rows(prev, dil)
                    qb, d_ob = _bf(qn_scr[rc, :]), _bf(do_scr[rc, :])
                    k2 = _bf(jnp.concatenate([kn_scr[rp, :], kn_scr[rc, :]], axis=0))
                    v2 = _bf(jnp.concatenate([qkv_ref.at[2][rp, :], qkv_ref.at[2][rc, :]], axis=0))
                    scored.append((rc, rp, has_prev, qb, d_ob, k2,
                                   _mm_nt(qb, k2) * ATT_SCALE, _mm_nt(d_ob, v2)))
                grads = []
                for rc, rp, has_prev, qb, d_ob, k2, s, d_p in scored:
                    p = jnp.exp(jnp.where(_band_mask(has_prev), s - ls_scr[rc, :][:, 0:1], NEG_BIG))
                    ds = _bf(p * (d_p - dl_scr[rc, :][:, 0:1]))
                    grads.append((rc, rp, has_prev,
                                  _mm(ds, k2) * ATT_SCALE, _mm_tn(ds, qb) * ATT_SCALE, _mm_tn(_bf(p), d_ob)))
                for j, (rc, rp, has_prev, dq, dk2, dv2) in enumerate(grads):
                    dqn_scr[rc, :] = dq
                    if j == 0:
                        @pl.when(has_prev)
                        def _():
                            dkn_scr[rp, :] += dk2[:B_BLOCK]
                            dv_scr[rp, :] += dv2[:B_BLOCK]
                    if j + 1 < ATT_UNROLL:
                        dkn_scr[rc, :] = dk2[B_BLOCK:] + grads[j + 1][4][:B_BLOCK]
                        dv_scr[rc, :] = dv2[B_BLOCK:] + grads[j + 1][5][:B_BLOCK]
                    else:
                        dkn_scr[rc, :] = dk2[B_BLOCK:]
                        dv_scr[rc, :] = dv2[B_BLOCK:]
                return carry

            lax.fori_loop(0, n_blocks // ATT_UNROLL, blocks, 0)
            dq, dgq = _norm_rope_bwd(q_raw, rq, gq, cos_t, sin_t, dqn_scr[...])
            dk, dgk = _norm_rope_bwd(k_raw, rk, gk, cos_t, sin_t, dkn_scr[...])
            dqkv_ref[0] = dq.astype(dqkv_ref.dtype)
            dqkv_ref[1] = dk.astype(dqkv_ref.dtype)
            dqkv_ref[2] = dv_scr[...].astype(dqkv_ref.dtype)
            dgain_ref[gi:gi + 1, :] += dgq
            dgain_ref[B_GROUPS + gi:B_GROUPS + gi + 1, :] += dgk


def attn_bwd(proj_b, cos_t, sin_t, gains8, d_og, o, lse, n_seq):
    t_rows = proj_b.shape[1]
    s_len = t_rows // n_seq
    head_blk = pl.BlockSpec((s_len, B_DH), lambda b, h, g: (b, h))
    seq_blk = pl.BlockSpec((s_len, 128), lambda b, h, g: (b, 0))
    grp_blk = pl.BlockSpec((3, s_len, B_DH), lambda b, h, g: (g, b, h))
    gain_blk = pl.BlockSpec((8, 128), lambda b, h, g: (0, 0))
    return pl.pallas_call(
        _attn_bwd_body, name="attn_bwd", grid=(n_seq, B_HEADS, B_GROUPS),
        in_specs=[
            grp_blk,
            pl.BlockSpec((None, s_len, B_DH), lambda b, h, g: (B_PIECES - 1, b, h)),
            seq_blk, seq_blk, gain_blk, head_blk, head_blk, seq_blk,
        ],
        out_specs=[grp_blk, head_blk, gain_blk],
        out_shape=[jax.ShapeDtypeStruct((3 * B_GROUPS, t_rows, B_W), BF16),
                   jax.ShapeDtypeStruct((t_rows, B_W), BF16),
                   jax.ShapeDtypeStruct((8, 128), F32)],
        scratch_shapes=[pltpu.VMEM((s_len, B_DH), F32) for _ in range(8)],
        compiler_params=_params(("arbitrary", "arbitrary", "arbitrary")),
    )(proj_b, proj_b, cos_t, sin_t, gains8, d_og, o, lse)


def rope_tables(positions):
    inv_freq = ROPE_THETA ** (-jnp.arange(0, ROPE_DIMS, 2, dtype=F32) / ROPE_DIMS)
    ang = positions.astype(F32)[:, None] * inv_freq
    cos, sin = jnp.cos(ang), jnp.sin(ang)
    t_rows = positions.shape[0]
    rest = B_DH - ROPE_DIMS
    cos_t = jnp.concatenate([cos, cos, jnp.ones((t_rows, rest), F32)], axis=1)
    sin_t = jnp.concatenate([-sin, sin, jnp.zeros((t_rows, rest), F32)], axis=1)
    return cos_t, sin_t


def _rms_fwd_body(x_ref, g_ref, *rest, layer):
    h_ref = rest[-1]
    x = x_ref[...]
    r = lax.rsqrt(jnp.mean(x * x, axis=1, keepdims=True) + EPS)
    h_ref[...] = (x * r * g_ref[layer:layer + 1, :]).astype(h_ref.dtype)


def rms_fwd(x, gains8, layer, after=None):
    t_rows, d = x.shape
    tm = min(ROW_TILE, t_rows)
    in_specs = [pl.BlockSpec((tm, d), lambda i: (i, 0)), pl.BlockSpec((8, d), lambda i: (0, 0))]
    args = [x, gains8]
    if after is not None:
        in_specs.append(HBM_SPEC)
        args.append(after)
    return pl.pallas_call(
        functools.partial(_rms_fwd_body, layer=layer), name=f"rms_fwd_{layer}", grid=(t_rows // tm,),
        in_specs=in_specs,
        out_specs=pl.BlockSpec((tm, d), lambda i: (i, 0)),
        out_shape=jax.ShapeDtypeStruct((t_rows, d), BF16),
        compiler_params=_params(("arbitrary",)),
    )(*args)


def _rms_bwd_body(x_ref, g_ref, dh_ref, res_ref, dx_ref, dg_ref, *, layer):
    x, dh = x_ref[...], dh_ref[...]
    r = lax.rsqrt(jnp.mean(x * x, axis=1, keepdims=True) + EPS)
    xr = x * r
    u = dh * g_ref[layer:layer + 1, :]
    dx_ref[...] = res_ref[...] + r * u - xr * (r * r) * jnp.mean(u * x, axis=1, keepdims=True)

    @pl.when(pl.program_id(0) == 0)
    def _():
        dg_ref[...] = jnp.zeros_like(dg_ref)

    dg_ref[0:1, :] += jnp.sum(dh * xr, axis=0, keepdims=True)


def rms_bwd(x, gains8, layer, dh, d_res):
    t_rows, d = x.shape
    tm = min(ROW_TILE, t_rows)
    blk = pl.BlockSpec((tm, d), lambda i: (i, 0))
    gblk = pl.BlockSpec((8, d), lambda i: (0, 0))
    return pl.pallas_call(
        functools.partial(_rms_bwd_body, layer=layer), name=f"rms_bwd_{layer}", grid=(t_rows // tm,),
        in_specs=[blk, gblk, blk, blk],
        out_specs=[blk, gblk],
        out_shape=[jax.ShapeDtypeStruct((t_rows, d), F32), jax.ShapeDtypeStruct((8, d), F32)],
        compiler_params=_params(("arbitrary",)),
    )(x, gains8, dh, d_res)


def _piece_col(p):
    return jnp.where(p < 3 * B_GROUPS, (p % 3) * B_GROUPS + p // 3, 3 * B_GROUPS)


def _mm_nn_body(a_ref, w_ref, *rest, has_res):
    o_ref = rest[-1]
    acc = jnp.dot(a_ref[...], w_ref[...], preferred_element_type=F32)
    if has_res:
        acc = acc + rest[0][...]
    o_ref[...] = acc


def mm_nn(a, w, residual=None, *, tn, name):
    m, k = a.shape
    n = w.shape[1]
    tm = min(ROW_TILE, m)
    in_specs = [pl.BlockSpec((tm, k), lambda j, i: (i, 0)), pl.BlockSpec((k, tn), lambda j, i: (0, j))]
    args = [a, w]
    if residual is not None:
        in_specs.append(pl.BlockSpec((tm, tn), lambda j, i: (i, j)))
        args.append(residual)
    return pl.pallas_call(
        functools.partial(_mm_nn_body, has_res=residual is not None), name=name, grid=(n // tn, m // tm),
        in_specs=in_specs,
        out_specs=pl.BlockSpec((tm, tn), lambda j, i: (i, j)),
        out_shape=jax.ShapeDtypeStruct((m, n), F32),
        compiler_params=_params(("arbitrary", "arbitrary")),
    )(*args)


def mm_nn_pieces(a, w, *, name):
    m, k = a.shape
    tm = min(ROW_TILE, m)
    return pl.pallas_call(
        functools.partial(_mm_nn_body, has_res=False), name=name, grid=(B_PIECES, m // tm),
        in_specs=[pl.BlockSpec((tm, k), lambda p, i: (i, 0)),
                  pl.BlockSpec((k, B_W), lambda p, i: (0, _piece_col(p)))],
        out_specs=pl.BlockSpec((None, tm, B_W), lambda p, i: (p, i, 0)),
        out_shape=jax.ShapeDtypeStruct((B_PIECES, m, B_W), F32),
        compiler_params=_params(("arbitrary", "arbitrary")),
    )(a, w)


NT_ROW_TILE = 1024


def _mm_nt_body(g_ref, w_ref, *rest, has_init):
    o_ref = rest[-1]
    j = pl.program_id(1)
    part = lax.dot_general(_bf(g_ref[...]), w_ref[...], (((1,), (1,)), ((), ())), preferred_element_type=F32)

    @pl.when(j == 0)
    def _():
        o_ref[...] = part + rest[0][...] if has_init else part

    @pl.when(j > 0)
    def _():
        o_ref[...] += part


def mm_nt(g, w, init=None, *, tn, col_off=0, name, after=None):
    m, n = g.shape
    k = w.shape[0]
    tm = min(NT_ROW_TILE, m)
    in_specs = [pl.BlockSpec((tm, tn), lambda i, j: (i, j)),
                pl.BlockSpec((k, tn), lambda i, j: (0, col_off + j))]
    args = [g, w]
    if init is not None:
        in_specs.append(pl.BlockSpec((tm, k), lambda i, j: (i, 0)))
        args.append(init)
    if after is not None:
        in_specs.append(HBM_SPEC)
        args.append(after)
    return pl.pallas_call(
        functools.partial(_mm_nt_body, has_init=init is not None), name=name, grid=(m // tm, n // tn),
        in_specs=in_specs,
        out_specs=pl.BlockSpec((tm, k), lambda i, j: (i, 0)),
        out_shape=jax.ShapeDtypeStruct((m, k), F32),
        compiler_params=_params(("arbitrary", "arbitrary")),
    )(*args)


def mm_nt_pieces(g9, w, *, name):
    n_p, m, _ = g9.shape
    k = w.shape[0]
    tm = min(NT_ROW_TILE, m)
    return pl.pallas_call(
        functools.partial(_mm_nt_body, has_init=False), name=name, grid=(m // tm, n_p),
        in_specs=[pl.BlockSpec((None, tm, B_W), lambda i, p: (p, i, 0)),
                  pl.BlockSpec((k, B_W), lambda i, p: (0, _piece_col(p)))],
        out_specs=pl.BlockSpec((tm, k), lambda i, p: (i, 0)),
        out_shape=jax.ShapeDtypeStruct((m, k), F32),
        compiler_params=_params(("arbitrary", "arbitrary")),
    )(g9, w)


def _mm_tn_body(a_ref, g_ref, o_ref):
    o_ref[...] = lax.dot_general(a_ref[...], _bf(g_ref[...]), (((0,), (0,)), ((), ())),
                                 preferred_element_type=F32).astype(o_ref.dtype)


def mm_tn(a, g, *, tn, out_dtype, name):
    m, k = a.shape
    n = g.shape[1]
    return pl.pallas_call(
        _mm_tn_body, name=name, grid=(n // tn,),
        in_specs=[pl.BlockSpec((m, k), lambda j: (0, 0)), pl.BlockSpec((m, tn), lambda j: (0, j))],
        out_specs=pl.BlockSpec((k, tn), lambda j: (0, j)),
        out_shape=jax.ShapeDtypeStruct((k, n), out_dtype),
        compiler_params=_params(("arbitrary",)),
    )(a, g)


B_UNIT = 256
B_IN_COLS = B_PIECES * B_W
B_SHARD_UNITS = B_IN_COLS // N_DEV // B_UNIT


def mm_tn_b_in(a, g9, gz, *, out_dtype, name):
    m, k = a.shape
    per_piece = B_W // B_UNIT
    n_units = B_IN_COLS // B_UNIT

    def g_map(u):
        nat = jnp.minimum(u // per_piece, 3 * B_GROUPS - 1)
        piece = (nat % B_GROUPS) * 3 + nat // B_GROUPS
        return (piece, 0, u % per_piece)

    def body(a_ref, g_ref, z_ref, o_ref):
        u = pl.program_id(0)

        @pl.when(u < 3 * B_GROUPS * per_piece)
        def _():
            _mm_tn_body(a_ref, g_ref, o_ref)

        @pl.when(u >= 3 * B_GROUPS * per_piece)
        def _():
            _mm_tn_body(a_ref, z_ref, o_ref)

    return pl.pallas_call(
        body, name=name, grid=(n_units,),
        in_specs=[pl.BlockSpec((m, k), lambda u: (0, 0)),
                  pl.BlockSpec((None, m, B_UNIT), g_map),
                  pl.BlockSpec((m, B_UNIT), lambda u: (0, jnp.where(u < 3 * B_GROUPS * per_piece, 0, u % per_piece)))],
        out_specs=pl.BlockSpec((None, k, B_UNIT), lambda u: (u // B_SHARD_UNITS, 0, u % B_SHARD_UNITS)),
        out_shape=jax.ShapeDtypeStruct((N_DEV, k, B_IN_COLS // N_DEV), out_dtype),
        compiler_params=_params(("arbitrary",)),
    )(a, g9, gz)


def _loss_body(y_ref, t_ref, dy_ref, loss_ref, acc):
    i = pl.program_id(0)
    d = y_ref.shape[1]
    err = y_ref[...] - t_ref[...]
    dy_ref[...] = err * (1.0 / d)

    @pl.when(i == 0)
    def _():
        acc[...] = jnp.zeros_like(acc)

    acc[...] += jnp.sum(err * err, axis=0, keepdims=True)

    @pl.when(i == pl.num_programs(0) - 1)
    def _():
        total = jnp.sum(acc[...], axis=1, keepdims=True) * (0.5 / d)
        loss_ref[...] = jnp.broadcast_to(total, loss_ref.shape)


def loss_head(y, target):
    t_rows, d = y.shape
    tm = min(ROW_TILE, t_rows)
    blk = pl.BlockSpec((tm, d), lambda i: (i, 0))
    return pl.pallas_call(
        _loss_body, name="loss_head", grid=(t_rows // tm,),
        in_specs=[blk, blk],
        out_specs=[blk, pl.BlockSpec((8, 128), lambda i: (0, 0))],
        out_shape=[jax.ShapeDtypeStruct((t_rows, d), F32), jax.ShapeDtypeStruct((8, 128), F32)],
        scratch_shapes=[pltpu.VMEM((1, d), F32)],
        compiler_params=_params(("arbitrary",)),
    )(y, target)


def _adamw_body(p_ref, w_ref, m_ref, v_ref, g_ref, d_ref, nm_ref, nv_ref):
    g = p_ref[0].astype(F32)
    for s in range(1, N_DEV):
        g = g + p_ref[s].astype(F32)
    w = w_ref[...]
    m = ADAM_B1 * m_ref[...] + (1.0 - ADAM_B1) * g
    v = ADAM_B2 * v_ref[...] + (1.0 - ADAM_B2) * (g * g)
    m_hat = m / (1.0 - ADAM_B1 ** ADAM_STEP)
    v_hat = v / (1.0 - ADAM_B2 ** ADAM_STEP)
    g_ref[...] = g
    d_ref[...] = -ADAM_LR * (m_hat / (jnp.sqrt(v_hat) + ADAM_EPS) + ADAM_WD * w)
    nm_ref[...] = m
    nv_ref[...] = v


def adamw(parts, w, m, v, *, name):
    _, r, c = w.shape
    tr = r if r <= 256 else 256
    blk = pl.BlockSpec((None, tr, c), lambda i: (0, i, 0))
    out = jax.ShapeDtypeStruct((1, r, c), F32)
    return pl.pallas_call(
        _adamw_body, name=name, grid=(r // tr,),
        in_specs=[pl.BlockSpec((N_DEV, tr, c), lambda i: (0, i, 0)), blk, blk, blk],
        out_specs=[blk, blk, blk, blk],
        out_shape=[out, out, out, out],
        compiler_params=_params(("arbitrary",)),
    )(parts, w, m, v)


MESH_ID = pl.DeviceIdType.MESH
HBM_SPEC = pl.BlockSpec(memory_space=pl.ANY)


def _my_place():
    return lax.axis_index("x"), lax.axis_index("y"), lax.axis_index("c")


def _flat(x, y, c):
    return 4 * x + 2 * y + c


def _all_gather_body(*refs, n):
    ins, outs = refs[:n], refs[n:2 * n]
    send_sems, recv_sems, local_sems = refs[2 * n:]
    x, y, c = _my_place()
    me, sibling = (x, y, c), (x, y, 1 - c)
    chips = [(1 - x, y), (x, 1 - y), (1 - x, 1 - y)]
    pending = []
    for a in range(n):
        src, out = ins[a], outs[a]

        def copy(k, block, to, from_input=False, a=a, src=src, out=out):
            slot = out.at[_flat(*block)]
            return pltpu.make_async_remote_copy(
                src_ref=src if from_input else slot, dst_ref=slot,
                send_sem=send_sems.at[7 * a + k], recv_sem=recv_sems.at[7 * a + k],
                device_id=to, device_id_type=MESH_ID)

        mine = pltpu.make_async_copy(src, out.at[_flat(*me)], local_sems.at[a])
        mine.start()
        first = [copy(0, me, sibling, True)] + [copy(1 + j, me, (*chip, c), True) for j, chip in enumerate(chips)]
        for cp in first:
            cp.start()
        pending.append((copy, mine, first))
    for copy, mine, first in pending:
        passed = [copy(4 + j, (*chip, c), sibling) for j, chip in enumerate(chips)]
        for j, chip in enumerate(chips):
            copy(1 + j, (*chip, c), me).wait_recv()
            passed[j].start()
        copy(0, sibling, me).wait_recv()
        for j, chip in enumerate(chips):
            copy(4 + j, (*chip, 1 - c), me).wait_recv()
        for cp in first + passed:
            cp.wait_send()
        mine.wait()


def all_gather(shards, *, name):
    n = len(shards)
    return pl.pallas_call(
        functools.partial(_all_gather_body, n=n), name=name,
        in_specs=[HBM_SPEC] * n, out_specs=[HBM_SPEC] * n,
        out_shape=[jax.ShapeDtypeStruct((N_DEV,) + s.shape, s.dtype) for s in shards],
        scratch_shapes=[pltpu.SemaphoreType.DMA((7 * n,)), pltpu.SemaphoreType.DMA((7 * n,)),
                        pltpu.SemaphoreType.DMA((n,))],
    )(*shards)


PEER_FLIPS = [(0, 0, 1), (1, 0, 0), (0, 1, 0), (1, 1, 0), (1, 0, 1), (0, 1, 1), (1, 1, 1)]


def _all_to_all_body(*refs, n):
    ins, outs = refs[:n], refs[n:2 * n]
    send_sems, recv_sems, local_sems = refs[2 * n:]
    x, y, c = _my_place()
    me = _flat(x, y, c)
    waits = []
    for a in range(n):
        src, out = ins[a], outs[a]
        mine = pltpu.make_async_copy(src.at[me], out.at[me], local_sems.at[a])
        mine.start()
        waits.append(mine)
        for k, (fx, fy, fc) in enumerate(PEER_FLIPS):
            peer = (1 - x if fx else x, 1 - y if fy else y, 1 - c if fc else c)
            theirs = _flat(*peer)
            sems = dict(send_sem=send_sems.at[7 * a + k], recv_sem=recv_sems.at[7 * a + k],
                        device_id=peer, device_id_type=MESH_ID)
            send = pltpu.make_async_remote_copy(src_ref=src.at[theirs], dst_ref=out.at[me], **sems)
            send.start()
            recv = pltpu.make_async_remote_copy(src_ref=src.at[theirs], dst_ref=out.at[theirs], **sems)
            waits.append((send, recv))
    for w in waits:
        if isinstance(w, tuple):
            w[0].wait_send()
            w[1].wait_recv()
        else:
            w.wait()


def all_to_all(parts, *, name):
    n = len(parts)
    return pl.pallas_call(
        functools.partial(_all_to_all_body, n=n), name=name,
        in_specs=[HBM_SPEC] * n, out_specs=[HBM_SPEC] * n,
        out_shape=[jax.ShapeDtypeStruct(p.shape, p.dtype) for p in parts],
        scratch_shapes=[pltpu.SemaphoreType.DMA((7 * n,)), pltpu.SemaphoreType.DMA((7 * n,)),
                        pltpu.SemaphoreType.DMA((n,))],
    )(*parts)


HBM_ONLY = pl.BlockSpec(memory_space=pltpu.HBM)
SEM_SPEC = pl.BlockSpec(memory_space=pltpu.SEMAPHORE)
DATAFLOW_EFFECT = pltpu.SideEffectType.DATAFLOW_SIDE_EFFECTING


def _split_copies(srcs, lands, send_sems, recv_sems, n, scatter):
    x, y, c = _my_place()
    me = _flat(x, y, c)
    pairs = []
    for a in range(n):
        for k, (fx, fy, fc) in enumerate(PEER_FLIPS):
            peer = (1 - x if fx else x, 1 - y if fy else y, 1 - c if fc else c)
            theirs = _flat(*peer)
            src = srcs[a].at[theirs] if scatter else srcs[a]
            sems = dict(send_sem=send_sems.at[7 * a + k], recv_sem=recv_sems.at[7 * a + k],
                        device_id=peer, device_id_type=MESH_ID)
            pairs.append((pltpu.make_async_remote_copy(src_ref=src, dst_ref=lands[a].at[me], **sems),
                          pltpu.make_async_remote_copy(src_ref=src, dst_ref=lands[a].at[theirs], **sems)))
    return pairs


def _exchange_start_body(*refs, n, scatter):
    srcs, lands = refs[:n], refs[n:2 * n]
    send_sems, recv_sems = refs[2 * n], refs[2 * n + 1]
    token = refs[-1]
    for send, _ in _split_copies(srcs, lands, send_sems, recv_sems, n, scatter):
        send.start()
    token[...] = jnp.zeros_like(token)


def exchange_start(srcs, lands, *, scatter, name):
    n = len(srcs)
    args = [pltpu.with_memory_space_constraint(t, pltpu.HBM) for t in list(srcs) + list(lands)]
    outs = pl.pallas_call(
        functools.partial(_exchange_start_body, n=n, scatter=scatter), name=name,
        out_shape=(pltpu.SemaphoreType.DMA((7 * n,)), pltpu.SemaphoreType.DMA((7 * n,)),
                   *[pltpu.HBM(t.shape, t.dtype) for t in args],
                   jax.ShapeDtypeStruct((8, 128), F32)),
        in_specs=[HBM_ONLY] * (2 * n),
        out_specs=(SEM_SPEC, SEM_SPEC, *[HBM_ONLY] * (2 * n), pl.BlockSpec(memory_space=pltpu.VMEM)),
        input_output_aliases={i: 2 + i for i in range(2 * n)},
        compiler_params=pltpu.CompilerParams(has_side_effects=DATAFLOW_EFFECT),
    )(*args)
    return outs[0], outs[1], outs[2:2 + n], outs[2 + n:2 + 2 * n], outs[-1]


def _exchange_wait_body(*refs, n, scatter):
    srcs, lands = refs[:n], refs[n:2 * n]
    send_sems, recv_sems = refs[2 * n], refs[2 * n + 1]
    for send, recv in _split_copies(srcs, lands, send_sems, recv_sems, n, scatter):
        send.wait_send()
        recv.wait_recv()


def exchange_wait(send_sems, recv_sems, srcs, lands, after, *, scatter, name):
    n = len(srcs)
    outs = pl.pallas_call(
        functools.partial(_exchange_wait_body, n=n, scatter=scatter), name=name,
        out_shape=tuple(pltpu.HBM(t.shape, t.dtype) for t in list(srcs) + list(lands)),
        in_specs=[HBM_ONLY] * (2 * n) + [SEM_SPEC, SEM_SPEC, HBM_SPEC],
        out_specs=tuple([HBM_ONLY] * (2 * n)),
        input_output_aliases={i: i for i in range(2 * n)},
        compiler_params=pltpu.CompilerParams(has_side_effects=DATAFLOW_EFFECT),
    )(*srcs, *lands, send_sems, recv_sems, after)
    return outs[n:]


def _own_slot_only(shape_dtype, own, slot):
    land = lax.empty(shape_dtype.shape, shape_dtype.dtype)
    return lax.dynamic_update_slice(land, own[None], (slot,) + (0,) * own.ndim)


def _pad_rows(a, rows=8):
    return jnp.pad(a, ((0, rows - a.shape[0]), (0, 0)))


def _gate_rows(a_log, dt_bias):
    z = jnp.zeros((8, 128), F32)
    return z.at[0, A_HEADS:2 * A_HEADS].set(a_log[0]).at[1, A_HEADS:2 * A_HEADS].set(dt_bias[0])


def _pack_small(norm_g, a_log, a_dt_bias, a_norm_g, b_q_norm_g, b_k_norm_g):
    return jnp.concatenate([
        norm_g[0].reshape(8, 128), norm_g[1].reshape(8, 128),
        _gate_rows(a_log, a_dt_bias),
        _pad_rows(a_norm_g[0].reshape(2, 128)),
        _pad_rows(jnp.concatenate([b_q_norm_g[0], b_k_norm_g[0]], axis=0)),
    ], axis=0)


def _unpack_small(p):
    return (p[0:16].reshape(2, D_MODEL), p[16:17, A_HEADS:2 * A_HEADS], p[17:18, A_HEADS:2 * A_HEADS],
            p[24:26].reshape(1, A_DV), p[32:35][None], p[35:38][None])


def kernel(x, positions, norm_g, a_w_in, a_conv_w, a_log, a_dt_bias, a_norm_g, a_w_out, b_w_in, b_q_norm_g, b_k_norm_g, b_w_out, loss_target, m_norm_g, m_a_w_in, m_a_conv_w, m_a_log, m_a_dt_bias, m_a_norm_g, m_a_w_out, m_b_w_in, m_b_q_norm_g, m_b_k_norm_g, m_b_w_out, v_norm_g, v_a_w_in, v_a_conv_w, v_a_log, v_a_dt_bias, v_a_norm_g, v_a_w_out, v_b_w_in, v_b_q_norm_g, v_b_k_norm_g, v_b_w_out):
    n_seq, s_len, d = x.shape
    t_rows = n_seq * s_len
    n_chunks = s_len // CHUNK
    x0 = x.reshape(t_rows, d)
    target = loss_target.reshape(t_rows, d)
    my_slot = _flat(*_my_place())

    g_a_in, g_conv = all_gather([a_w_in[0].astype(BF16), _pad_rows(a_conv_w[0])], name="gather_weights_first")
    later = [a_w_out[0].astype(BF16), b_w_in[0].astype(BF16), b_w_out[0].astype(BF16)]
    lands = [_own_slot_only(jax.ShapeDtypeStruct((N_DEV,) + t.shape, t.dtype), t, my_slot) for t in later]
    w_send, w_recv, later, lands, w_token = exchange_start(later, lands, scatter=False, name="gather_weights_start")
    w_a_in = jnp.pad(g_a_in.transpose(1, 0, 2).reshape(d, A_IN), ((0, 0), (0, A_IN_PAD - A_IN)))
    conv_w8 = g_conv.transpose(1, 0, 2).reshape(8, 2 * A_QK + A_VW)

    gains_model = _pad_rows(norm_g)
    gate_prm = _gate_rows(a_log, a_dt_bias)
    gain_a_out = _pad_rows(a_norm_g)
    gains_qk = _pad_rows(jnp.concatenate([b_q_norm_g[0], b_k_norm_g[0]], axis=0))
    cos_t, sin_t = rope_tables(positions.reshape(t_rows))

    h0 = rms_fwd(x0, gains_model, 0, after=w_token)
    proj_a = mm_nn(h0, w_a_in, tn=896, name="proj_a")
    qkv = a_pre_fwd(proj_a, conv_w8, n_seq)
    gates_col, gates_row = a_gates_fwd(proj_a, gate_prm, n_seq)
    gates_row = gates_row.reshape(n_seq, 2 * A_HEADS, s_len // SUPER, 1, SUPER)
    o_a, tinv, states = gdn_fwd(qkv, gates_col, gates_row, n_seq)
    og_a = a_post_fwd(o_a, proj_a, gain_a_out)
    g_a_out, g_b_in, g_b_out = exchange_wait(w_send, w_recv, later, lands, og_a, scatter=False,
                                             name="gather_weights_wait")
    w_a_out = g_a_out.reshape(A_VW, d)
    w_b_in = g_b_in.transpose(1, 0, 2).reshape(d, B_IN_COLS)
    w_b_out = g_b_out.reshape(B_W, d)
    x1 = mm_nn(og_a, w_a_out, x0, tn=512, name="out_a")

    h1 = rms_fwd(x1, gains_model, 1)
    proj_b = mm_nn_pieces(h1, w_b_in, name="proj_b")
    og_b, o_b, lse = attn_fwd(proj_b, cos_t, sin_t, gains_qk, n_seq)
    y = mm_nn(og_b, w_b_out, x1, tn=512, name="out_b")

    dy, loss_blk = loss_head(y, target)
    loss = lax.psum(loss_blk[0, 0], ("x", "y", "c"))

    d_og_b = mm_nt(dy, w_b_out, tn=512, name="d_og_b")
    dw_b_out = mm_tn(og_b, dy, tn=256, out_dtype=BF16, name="dw_b_out")
    dqkv_b, dz_b, d_gains_qk = attn_bwd(proj_b, cos_t, sin_t, gains_qk, d_og_b, o_b, lse, n_seq)
    dh1 = mm_nt_pieces(dqkv_b, w_b_in, name="dh1_qkv")
    dh1 = mm_nt(dz_b, w_b_in, dh1, tn=B_W, col_off=3 * B_GROUPS, name="dh1_z")
    dw_b_in = mm_tn_b_in(h1, dqkv_b, dz_b, out_dtype=BF16, name="dw_b_in")
    dx1, d_gain1 = rms_bwd(x1, gains_model, 1, dh1, dy)

    dw_a_out = mm_tn(og_a, dx1, tn=128, out_dtype=BF16, name="dw_a_out")
    early = [dw_b_in, dw_b_out.reshape(N_DEV, B_W // N_DEV, d), dw_a_out.reshape(N_DEV, A_VW // N_DEV, d)]
    lands = [_own_slot_only(t, lax.dynamic_index_in_dim(t, my_slot, 0, keepdims=False), my_slot) for t in early]
    g_send, g_recv, early, lands, g_token = exchange_start(early, lands, scatter=True, name="scatter_grads_start")

    d_og_a = mm_nt(dx1, w_a_out, tn=512, name="d_og_a", after=g_token)
    do_a, dz_a, d_gain_a_out = a_post_bwd(o_a, proj_a, gain_a_out, d_og_a)
    dqkv_a, dgates = gdn_bwd(qkv, gates_col, gates_row, tinv, states, do_a, n_seq)
    d_pre, d_conv = a_pre_bwd(proj_a, conv_w8, dqkv_a, n_seq)
    d_gate_logits, d_gate_prm = a_gates_bwd(proj_a, gate_prm, dgates, n_seq)
    dw_a_in = jnp.concatenate([
        mm_tn(h0, d_pre, tn=256, out_dtype=BF16, name="dw_a_in_qkv"),
        mm_tn(h0, dz_a, tn=256, out_dtype=BF16, name="dw_a_in_z"),
        mm_tn(h0, d_gate_logits, tn=128, out_dtype=BF16, name="dw_a_in_gates"),
    ], axis=1)[:, :A_IN]
    shard_a_in = A_IN // N_DEV
    last = [dw_a_in.reshape(d, N_DEV, shard_a_in).transpose(1, 0, 2)]
    last_lands = [_own_slot_only(t, lax.dynamic_index_in_dim(t, my_slot, 0, keepdims=False), my_slot) for t in last]
    l_send, l_recv, last, last_lands, l_token = exchange_start(last, last_lands, scatter=True,
                                                               name="scatter_last_start")
    dh0 = mm_nt(d_pre, w_a_in, tn=512, name="dh0_qkv", after=l_token)
    dh0 = mm_nt(dz_a, w_a_in, dh0, tn=512, col_off=(2 * A_QK + A_VW) // 512, name="dh0_z")
    dh0 = mm_nt(d_gate_logits, w_a_in, dh0, tn=128, col_off=A_GATE_COL, name="dh0_gates")
    dx0, d_gain0 = rms_bwd(x0, gains_model, 0, dh0, dx1)

    small = jnp.concatenate([
        d_gain0[0].reshape(8, 128), d_gain1[0].reshape(8, 128), d_gate_prm,
        _pad_rows(d_gain_a_out[0].reshape(2, 128)), d_gains_qk], axis=0)
    r_small, r_conv = all_gather([small, d_conv], name="gather_small_grads")
    conv_cols = a_conv_w.shape[2]
    r_conv = lax.dynamic_slice(r_conv, (0, 0, my_slot * conv_cols), (N_DEV, 8, conv_cols))

    r_b_in, r_b_out, r_a_out = exchange_wait(g_send, g_recv, early, lands, r_small, scatter=True,
                                             name="scatter_grads_wait")
    (r_a_in,) = exchange_wait(l_send, l_recv, last, last_lands, r_small, scatter=True, name="scatter_last_wait")

    upd = {}
    upd["a_w_in"] = adamw(r_a_in, a_w_in, m_a_w_in, v_a_w_in, name="adamw_a_w_in")
    upd["a_w_out"] = adamw(r_a_out, a_w_out, m_a_w_out, v_a_w_out, name="adamw_a_w_out")
    upd["b_w_in"] = adamw(r_b_in, b_w_in, m_b_w_in, v_b_w_in, name="adamw_b_w_in")
    upd["b_w_out"] = adamw(r_b_out, b_w_out, m_b_w_out, v_b_w_out, name="adamw_b_w_out")
    upd["a_conv_w"] = [t[:, :A_CONV] for t in adamw(
        r_conv, _pad_rows(a_conv_w[0])[None], _pad_rows(m_a_conv_w[0])[None], _pad_rows(v_a_conv_w[0])[None],
        name="adamw_a_conv_w")]
    small_upd = adamw(
        r_small,
        _pack_small(norm_g, a_log, a_dt_bias, a_norm_g, b_q_norm_g, b_k_norm_g)[None],
        _pack_small(m_norm_g, m_a_log, m_a_dt_bias, m_a_norm_g, m_b_q_norm_g, m_b_k_norm_g)[None],
        _pack_small(v_norm_g, v_a_log, v_a_dt_bias, v_a_norm_g, v_b_q_norm_g, v_b_k_norm_g)[None],
        name="adamw_small")
    small_names = ("norm_g", "a_log", "a_dt_bias", "a_norm_g", "b_q_norm_g", "b_k_norm_g")
    unpacked = [_unpack_small(t[0]) for t in small_upd]
    for i, nm in enumerate(small_names):
        upd[nm] = [u[i] for u in unpacked]

    order = ("norm_g", "a_w_in", "a_conv_w", "a_log", "a_dt_bias", "a_norm_g", "a_w_out",
             "b_w_in", "b_q_norm_g", "b_k_norm_g", "b_w_out")
    outs = [loss, dx0.reshape(n_seq, s_len, d)]
    for kind in range(4):
        for nm in order:
            outs.append(upd[nm][kind])
    return tuple(outs)
```

```python
import functools
import math

import jax
import jax.numpy as jnp
from jax import lax
from jax.experimental import pallas as pl
from jax.experimental.pallas import tpu as pltpu

F32 = jnp.float32
BF16 = jnp.bfloat16

D_MODEL = 1024
EPS = 1e-6
N_DEV = 8

A_HEADS = 8
A_DK = 128
A_DV = 256
A_QK = A_HEADS * A_DK
A_VW = A_HEADS * A_DV
A_CONV = 4
CHUNK = 64
A_IN = 2 * A_QK + 2 * A_VW + 2 * A_HEADS
A_IN_PAD = 2 * A_QK + 2 * A_VW + 128
A_GATE_COL = (2 * A_QK + 2 * A_VW) // 128

B_DILATIONS = (1, 4, 16)
B_GROUPS = 3
B_HEADS = 8
B_DH = 128
B_W = B_HEADS * B_DH
B_BLOCK = 128
B_PIECES = 3 * B_GROUPS + 1
ROPE_THETA = 500000.0
ROPE_DIMS = B_DH // 4
ROPE_HALF = ROPE_DIMS // 2

ADAM_LR = 0.001
ADAM_B1 = 0.9
ADAM_B2 = 0.999
ADAM_EPS = 1e-08
ADAM_WD = 0.01
ADAM_STEP = 10

VMEM_LIMIT = 56 * 1024 * 1024


def _params(sem):
    return pltpu.CompilerParams(dimension_semantics=sem, vmem_limit_bytes=VMEM_LIMIT)


def _bf(x):
    return x.astype(BF16)


def _mm(a, b):
    return jnp.dot(_bf(a), _bf(b), preferred_element_type=F32)


def _mm_nt(a, b):
    return lax.dot_general(_bf(a), _bf(b), (((1,), (1,)), ((), ())), preferred_element_type=F32)


def _mm_tn(a, b):
    return lax.dot_general(_bf(a), _bf(b), (((0,), (0,)), ((), ())), preferred_element_type=F32)


def _split(x):
    hi = _bf(x)
    return hi, _bf(x - hi.astype(F32))


def _mm3(a, b):
    ah, al = _split(a)
    bh, bl = _split(b)
    d = functools.partial(jnp.dot, preferred_element_type=F32)
    return d(ah, bh) + (d(ah, bl) + d(al, bh))


def _colsum_as_col(z):
    zh, zl = _split(z)
    ones = jnp.ones((z.shape[0], 128), BF16)
    tn = functools.partial(lax.dot_general, dimension_numbers=(((0,), (0,)), ((), ())),
                           preferred_element_type=F32)
    return (tn(zh, ones) + tn(zl, ones))[:, 0:1]


def _sigmoid(x):
    return 0.5 * jnp.tanh(0.5 * x) + 0.5


INV_BASE = 8
GDN_GROUP = 4
SUPER = GDN_GROUP * CHUNK
GDN_WIDTH = 2

A_K_COL = A_QK // A_DK
A_V_COL = 2 * A_QK // A_DV
A_HEAD_W = 2 * A_DK + A_DV


def _inverse_steps(m, row, col):
    eye = (row == col).astype(F32)
    d = jnp.where(row // INV_BASE == col // INV_BASE, m, 0.0)
    x = eye - d
    p = _mm3(d, d)
    yield
    steps = int(math.log2(INV_BASE)) - 1
    for i in range(steps):
        x = x + _mm3(x, p)
        if i + 1 < steps:
            p = _mm3(p, p)
        yield
    size = INV_BASE
    while size < CHUNK:
        c = jnp.where((row // (2 * size) == col // (2 * size)) & (row // size != col // size), m, 0.0)
        xc = _mm3(x, c)
        yield
        x = x - _mm3(xc, x)
        yield
        size *= 2
    return x


def _drain(gen):
    while True:
        try:
            next(gen)
        except StopIteration as stop:
            return stop.value


def _interleave(*gens):
    live = list(gens)
    while live:
        for g in list(live):
            try:
                next(g)
            except StopIteration:
                live.remove(g)


def _diag_blocks_tall(x):
    return jnp.concatenate([x[i * CHUNK:(i + 1) * CHUNK, i * CHUNK:(i + 1) * CHUNK] for i in range(GDN_GROUP)], axis=0)


def _tall_to_block_diag(t, same):
    return jnp.where(same, jnp.concatenate([t] * GDN_GROUP, axis=1), 0.0)


def _block_sum(same, x):
    xh, xl = _split(jnp.broadcast_to(x, (SUPER, 128)))
    ones = same.astype(BF16)
    d = functools.partial(jnp.dot, preferred_element_type=F32)
    return (d(ones, xh) + d(ones, xl))[:, 0:1]


def _aligned_rows(index, size):
    start = index * size
    return pl.ds(start if isinstance(start, int) else pl.multiple_of(start, size), size)


def _super_rows(i):
    return _aligned_rows(i, SUPER)


def _chunk_rows(n):
    return _aligned_rows(n, CHUNK)


def _gdn_super_steps(q, k, v, gcb, gr, head, tinv_tall=None):
    lane = lax.broadcasted_iota(jnp.int32, (SUPER, 128), 1)
    row = lax.broadcasted_iota(jnp.int32, (SUPER, SUPER), 0)
    col = lax.broadcasted_iota(jnp.int32, (SUPER, SUPER), 1)
    same = row // CHUNK == col // CHUNK
    beta = jnp.sum(jnp.where(lane == head, gcb, 0.0), axis=1, keepdims=True)
    gc = jnp.sum(jnp.where(lane == A_HEADS + head, gcb, 0.0), axis=1, keepdims=True)
    g_last = jnp.sum(jnp.where(col == (row // CHUNK) * CHUNK + (CHUNK - 1), gr, 0.0), axis=1, keepdims=True)
    gamma = jnp.exp(gc)
    decay = jnp.where(same & (row >= col), jnp.exp(jnp.minimum(gc - gr, 0.0)), 0.0)
    kb = k * beta
    m = jnp.where(same & (row > col), _mm_nt(kb, k) * decay, 0.0)
    p = jnp.where(same & (row >= col), _mm_nt(q, k) * decay, 0.0)
    yield
    if tinv_tall is None:
        tinv = yield from _inverse_steps(m, row, col)
    else:
        tinv = _tall_to_block_diag(tinv_tall, same)
    u = _mm(tinv, v * beta)
    w = _mm(tinv, kb * gamma)
    yield
    e_tail = jnp.exp(g_last - gc)
    return dict(beta=beta, gc=gc, g_last=g_last, gamma=gamma, decay=decay, kb=kb, m=m,
                tinv=tinv, u=u, w=w, p=p, e_tail=e_tail, row=row, col=col, lane=lane, same=same)


def _gdn_super_common(q, k, v, gcb, gr, head, tinv_tall=None):
    return _drain(_gdn_super_steps(q, k, v, gcb, gr, head, tinv_tall))


def _store_scan_operands(rows, q, k, t, u_scr, w_scr, p_scr, qg_scr, ke_scr, gl_scr):
    u_scr[rows, :] = t["u"]
    w_scr[rows, :] = _bf(t["w"])
    p_scr[rows, :] = _bf(_diag_blocks_tall(t["p"]))
    qg_scr[rows, :] = _bf(q * t["gamma"])
    ke_scr[rows, :] = _bf(k * t["e_tail"])
    gl_scr[rows, :] = jnp.broadcast_to(jnp.exp(t["g_last"]), (SUPER, 128))


def _gdn_fwd_body(q_ref, k_ref, v_ref, gc_ref, gr_ref, z_ref, gn_ref,
                  o_ref, tinv_ref, st_ref, og_ref, s_scr, *sets):
    head = pl.program_id(1)
    n_super = q_ref.shape[0] // SUPER
    all_sets = [sets[6 * i:6 * i + 6] for i in range(2 * GDN_WIDTH)]
    whole = pl.ds(0, SUPER)

    def prepare_steps(i, dst):
        rows = _super_rows(i)
        q, k = q_ref[rows, :], k_ref[rows, :]
        t = yield from _gdn_super_steps(q, k, v_ref[rows, :], gc_ref[rows, :], gr_ref[i], head)
        tinv_ref[rows, :] = _diag_blocks_tall(t["tinv"])
        _store_scan_operands(whole, q, k, t, *dst)

    def scan_steps(i, src):
        u_scr, w_scr, p_scr, qg_scr, ke_scr, gl_scr = src
        for j in range(GDN_GROUP):
            n = i * GDN_GROUP + j
            local = pl.ds(j * CHUNK, CHUNK)
            s = s_scr[...]
            st_ref[n] = s
            sb = _bf(s)
            ws = jnp.dot(w_scr[local, :], sb, preferred_element_type=F32)
            yield
            vb = _bf(u_scr[local, :] - ws)
            o = (jnp.dot(qg_scr[local, :], sb, preferred_element_type=F32)
                 + jnp.dot(p_scr[local, :], vb, preferred_element_type=F32))
            s_new = s * gl_scr[local, :][0:1, 0:1] + lax.dot_general(
                ke_scr[local, :], vb, (((0,), (0,)), ((), ())), preferred_element_type=F32)
            yield
            rows = _chunk_rows(n)
            o_ref[rows, :] = o
            s_scr[...] = s_new
            silu, _ = _silu_parts(z_ref[rows, :])
            r = lax.rsqrt(jnp.mean(o * o, axis=1, keepdims=True) + EPS)
            og_ref[rows, :] = ((o * r * gn_ref[0:1, :]) * silu).astype(og_ref.dtype)

    def scan_many(first, srcs):
        for j, src in enumerate(srcs):
            yield from scan_steps(first + j, src)

    groups = [all_sets[:GDN_WIDTH], all_sets[GDN_WIDTH:]]
    _interleave(*[prepare_steps(j, groups[0][j]) for j in range(GDN_WIDTH)])
    s_scr[...] = jnp.zeros_like(s_scr)
    for g in range(n_super // GDN_WIDTH):
        cur, nxt = groups[g % 2], groups[(g + 1) % 2]
        first = g * GDN_WIDTH
        following = [prepare_steps(first + GDN_WIDTH + j, nxt[j]) for j in range(GDN_WIDTH)
                     if first + GDN_WIDTH + j < n_super]
        _interleave(scan_many(first, cur), *following)


def _gdn_in_specs(s_len, n_super):
    return [
        pl.BlockSpec((s_len, A_DK), lambda b, h: (b, h)),
        pl.BlockSpec((s_len, A_DK), lambda b, h: (b, A_K_COL + h)),
        pl.BlockSpec((s_len, A_DV), lambda b, h: (b, A_V_COL + h)),
        pl.BlockSpec((s_len, 128), lambda b, h: (b, 0)),
        pl.BlockSpec((None, None, n_super, 1, SUPER), lambda b, h: (b, A_HEADS + h, 0, 0, 0)),
    ]


def _gdn_scan_scratch(s_len):
    return [pltpu.VMEM((A_DK, A_DV), F32), pltpu.VMEM((s_len, A_DV), F32),
            pltpu.VMEM((s_len, A_DK), BF16), pltpu.VMEM((s_len, CHUNK), BF16),
            pltpu.VMEM((s_len, A_DK), BF16), pltpu.VMEM((s_len, A_DK), BF16),
            pltpu.VMEM((s_len, 128), F32)]


def gdn_fwd(qkv, gates_col, gates_row, proj_a, norm_g8, n_seq):
    t_rows = qkv.shape[0]
    s_len = t_rows // n_seq
    n_chunks = s_len // CHUNK
    v_spec = pl.BlockSpec((s_len, A_DV), lambda b, h: (b, h))
    return pl.pallas_call(
        _gdn_fwd_body, name="gdn_fwd", grid=(n_seq, A_HEADS),
        in_specs=_gdn_in_specs(s_len, s_len // SUPER) + [
            pl.BlockSpec((s_len, A_DV), lambda b, h: (b, A_Z_COL + h)),
            pl.BlockSpec((8, A_DV), lambda b, h: (0, 0)),
        ],
        out_specs=[
            v_spec,
            pl.BlockSpec((s_len, CHUNK), lambda b, h: (b * A_HEADS + h, 0)),
            pl.BlockSpec((None, n_chunks, A_DK, A_DV), lambda b, h: (b * A_HEADS + h, 0, 0, 0)),
            v_spec,
        ],
        out_shape=[
            jax.ShapeDtypeStruct((t_rows, A_VW), F32),
            jax.ShapeDtypeStruct((n_seq * A_HEADS * s_len, CHUNK), F32),
            jax.ShapeDtypeStruct((n_seq * A_HEADS, n_chunks, A_DK, A_DV), F32),
            jax.ShapeDtypeStruct((t_rows, A_VW), BF16),
        ],
        scratch_shapes=_gdn_scan_scratch(SUPER) + (2 * GDN_WIDTH - 1) * _gdn_scan_scratch(SUPER)[1:],
        compiler_params=_params(("arbitrary", "arbitrary")),
    )(qkv, qkv, qkv, gates_col, gates_row, proj_a, norm_g8)


def _gdn_bwd_body(q_ref, k_ref, v_ref, gc_ref, gr_ref, tinv_ref, st_ref, dog_ref, oa_ref, z_ref, gn_ref,
                  dqkv_ref, dgc_ref, dz_ref, dgn_ref, ds_scr, *sets):
    head = pl.program_id(1)
    n_super = q_ref.shape[0] // SUPER
    ops = (sets[0:7], sets[7:14])
    res = (sets[14:21], sets[21:28])
    whole = pl.ds(0, SUPER)
    tn = functools.partial(lax.dot_general, dimension_numbers=(((0,), (0,)), ((), ())), preferred_element_type=F32)
    nt = functools.partial(lax.dot_general, dimension_numbers=(((1,), (1,)), ((), ())), preferred_element_type=F32)

    @pl.when(head == 0)
    def _():
        dgc_ref[...] = jnp.zeros_like(dgc_ref)

    @pl.when((head == 0) & (pl.program_id(0) == 0))
    def _():
        dgn_ref[...] = jnp.zeros_like(dgn_ref)

    def common_steps(i):
        rows = _super_rows(i)
        q, k, v = q_ref[rows, :], k_ref[rows, :], v_ref[rows, :]
        t = yield from _gdn_super_steps(q, k, v, gc_ref[rows, :], gr_ref[i], head, tinv_tall=tinv_ref[rows, :])
        return rows, q, k, v, t

    def stage_p(i, parity):
        rows, q, k, _, t = yield from common_steps(i)
        _store_scan_operands(whole, q, k, t, *ops[parity][:6])
        o, d_og, gain = oa_ref[rows, :], dog_ref[rows, :], gn_ref[0:1, :]
        r = lax.rsqrt(jnp.mean(o * o, axis=1, keepdims=True) + EPS)
        silu, dsilu = _silu_parts(z_ref[rows, :])
        xr = o * r
        d_on = d_og * silu
        dz_ref[rows, :] = (d_og * (xr * gain) * dsilu).astype(dz_ref.dtype)
        u = d_on * gain
        ops[parity][6][...] = r * u - xr * (r * r) * jnp.mean(u * o, axis=1, keepdims=True)
        dgn_ref[0:1, :] += jnp.sum(d_on * xr, axis=0, keepdims=True)

    def stage_s(i, parity):
        u_scr, w_scr, p_scr, qg_scr, ke_scr, gl_scr, do_scr = ops[parity]
        vn_scr, dvn_scr, dqg_scr, dw_scr, dkt_scr, sds_scr, dof_scr = res[parity]
        for j in reversed(range(GDN_GROUP)):
            n = i * GDN_GROUP + j
            local = pl.ds(j * CHUNK, CHUNK)
            ds_next = ds_scr[...]
            dsb = _bf(ds_next)
            s = st_ref[n]
            sb = _bf(s)
            d_o = do_scr[local, :]
            dof_scr[local, :] = d_o
            d_ob = _bf(d_o)
            w_s = jnp.dot(w_scr[local, :], sb, preferred_element_type=F32)
            d_vn = tn(p_scr[local, :], d_ob) + jnp.dot(ke_scr[local, :], dsb, preferred_element_type=F32)
            d_qg = nt(d_ob, sb)
            qg_do = tn(qg_scr[local, :], d_ob)
            yield
            v_new = u_scr[local, :] - w_s
            d_vnb = _bf(d_vn)
            d_w = -nt(d_vnb, sb)
            d_kt = nt(_bf(v_new), dsb)
            w_dvn = tn(w_scr[local, :], d_vnb)
            yield
            vn_scr[local, :] = v_new
            dvn_scr[local, :] = d_vn
            dqg_scr[local, :] = d_qg
            dw_scr[local, :] = d_w
            dkt_scr[local, :] = d_kt
            sds = jnp.sum(jnp.sum(s * ds_next, axis=1, keepdims=True), axis=0, keepdims=True)
            sds_scr[local, :] = jnp.broadcast_to(sds, (CHUNK, 128))
            ds_scr[...] = qg_do + gl_scr[local, :][0:1, 0:1] * ds_next - w_dvn

    def stage_f(i, parity):
        vn_scr, dvn_scr, dqg_scr, dw_scr, dkt_scr, sds_scr, dof_scr = res[parity]
        rows, q, k, v, t = yield from common_steps(i)
        beta, gamma, decay, kb, e_tail = t["beta"], t["gamma"], t["decay"], t["kb"], t["e_tail"]
        row, col, lane, same = t["row"], t["col"], t["lane"], t["same"]
        d_o = dof_scr[...]
        v_new, d_vn = vn_scr[...], dvn_scr[...]
        d_qg, d_w, d_kt = dqg_scr[...], dw_scr[...], dkt_scr[...]
        gamma_last = jnp.exp(t["g_last"])

        d_p = jnp.where(same & (row >= col), _mm_nt(d_o, v_new), 0.0)
        d_ru = _mm_tn(t["tinv"], d_vn)
        d_rw = _mm_tn(t["tinv"], d_w)
        yield
        d_m = jnp.where(same & (row > col), -(_mm_nt(d_ru, t["u"]) + _mm_nt(d_rw, t["w"])), 0.0)
        yield

        x_p = d_p * decay
        y_m = d_m * decay
        d_kb = _mm(y_m, k) + d_rw * gamma
        dqkv_ref[rows, 0:A_DK] = _mm(x_p, k) + d_qg * gamma
        dqkv_ref[rows, A_DK:2 * A_DK] = _mm_tn(x_p, q) + _mm_tn(y_m, kb) + d_kb * beta + d_kt * e_tail
        dqkv_ref[rows, 2 * A_DK:A_HEAD_W] = d_ru * beta

        d_beta = (jnp.sum(d_ru * v, axis=1, keepdims=True)
                  + jnp.sum(d_kb * k, axis=1, keepdims=True))
        z = d_p * t["p"] + d_m * t["m"]
        eps_tail = jnp.sum(d_kt * k, axis=1, keepdims=True) * e_tail
        d_gc = (jnp.sum(z, axis=1, keepdims=True) - _colsum_as_col(z)
                + jnp.sum(d_qg * q, axis=1, keepdims=True) * gamma
                + jnp.sum(d_rw * kb, axis=1, keepdims=True) * gamma
                - eps_tail)
        d_glast = _block_sum(same, eps_tail) + gamma_last * sds_scr[...][:, 0:1]
        yield
        rcol = lax.broadcasted_iota(jnp.int32, (SUPER, 1), 0)
        d_gc = d_gc + jnp.where(rcol % CHUNK == CHUNK - 1, d_glast, 0.0)
        dgc_ref[rows, :] += (jnp.where(lane == head, d_beta, 0.0)
                             + jnp.where(lane == A_HEADS + head, d_gc, 0.0))

    last = n_super - 1
    _drain(stage_p(last, 1))
    ds_scr[...] = jnp.zeros_like(ds_scr)
    _interleave(stage_s(last, 1), stage_p(last - 1, 0))

    def pair(k, carry):
        i = last - 1 - 2 * k
        _interleave(stage_s(i, 0), stage_f(i + 1, 1), stage_p(i - 1, 1))
        _interleave(stage_s(i - 1, 1), stage_f(i, 0), stage_p(i - 2, 0))
        return carry

    lax.fori_loop(0, n_super // 2 - 1, pair, 0)
    _interleave(stage_s(0, 0), stage_f(1, 1))
    _drain(stage_f(0, 0))


def gdn_bwd(qkv, gates_col, gates_row, tinv, states, d_og, o, proj_a, norm_g8, n_seq):
    t_rows = qkv.shape[0]
    s_len = t_rows // n_seq
    n_chunks = s_len // CHUNK
    v_spec = pl.BlockSpec((s_len, A_DV), lambda b, h: (b, h))
    gate_spec = pl.BlockSpec((s_len, 128), lambda b, h: (b, 0))
    gain_spec = pl.BlockSpec((8, A_DV), lambda b, h: (0, 0))
    ops_set = _gdn_scan_scratch(SUPER)[1:] + [pltpu.VMEM((SUPER, A_DV), F32)]
    res_set = [pltpu.VMEM((SUPER, A_DV), F32), pltpu.VMEM((SUPER, A_DV), F32),
               pltpu.VMEM((SUPER, A_DK), F32), pltpu.VMEM((SUPER, A_DK), F32),
               pltpu.VMEM((SUPER, A_DK), F32), pltpu.VMEM((SUPER, 128), F32), pltpu.VMEM((SUPER, A_DV), F32)]
    return pl.pallas_call(
        _gdn_bwd_body, name="gdn_bwd", grid=(n_seq, A_HEADS),
        in_specs=_gdn_in_specs(s_len, s_len // SUPER) + [
            pl.BlockSpec((s_len, CHUNK), lambda b, h: (b * A_HEADS + h, 0)),
            pl.BlockSpec((None, n_chunks, A_DK, A_DV), lambda b, h: (b * A_HEADS + h, 0, 0, 0)),
            v_spec, v_spec,
            pl.BlockSpec((s_len, A_DV), lambda b, h: (b, A_Z_COL + h)),
            gain_spec,
        ],
        out_specs=[pl.BlockSpec((s_len, A_HEAD_W), lambda b, h: (b, h)), gate_spec, v_spec, gain_spec],
        out_shape=[
            jax.ShapeDtypeStruct((t_rows, A_HEADS * A_HEAD_W), F32),
            jax.ShapeDtypeStruct((t_rows, 128), F32),
            jax.ShapeDtypeStruct((t_rows, A_VW), BF16),
            jax.ShapeDtypeStruct((8, A_DV), F32),
        ],
        scratch_shapes=_gdn_scan_scratch(SUPER)[:1] + 2 * ops_set + 2 * res_set,
        compiler_params=_params(("arbitrary", "arbitrary")),
    )(qkv, qkv, qkv, gates_col, gates_row, tinv, states, d_og, o, proj_a, norm_g8)


def _shift_down(x, j, rows):
    r = pltpu.roll(x, j, 0)
    return jnp.concatenate([jnp.where(rows[:8] >= j, r[:8], 0.0), r[8:]], axis=0)


def _shift_up(x, j, rows):
    n = x.shape[0]
    r = pltpu.roll(x, n - j, 0)
    return jnp.concatenate([r[:n - 8], jnp.where(rows[n - 8:] < n - j, r[n - 8:], 0.0)], axis=0)


def _conv_silu_norm(x, w, cb):
    rows = lax.broadcasted_iota(jnp.int32, (x.shape[0], 1), 0)
    c = x * w[A_CONV - 1:A_CONV, :]
    for j in range(1, A_CONV):
        c = c + _shift_down(x, j, rows) * w[A_CONV - 1 - j:A_CONV - j, :]
    sig = _sigmoid(c)
    a = c * sig
    rn = lax.rsqrt(jnp.sum(a * a, axis=1, keepdims=True) + EPS)
    is_q = cb < A_QK // 128
    is_qk = cb < 2 * A_QK // 128
    q_scale = jnp.where(is_q, A_DK ** -0.5, 1.0).astype(F32)
    return rows, c, sig, a, rn, is_qk, q_scale


def _a_pre_fwd_body(x_ref, w_ref, o_ref):
    cb = pl.program_id(1)
    _, _, _, a, rn, is_qk, q_scale = _conv_silu_norm(x_ref[...], w_ref[...], cb)
    o_ref[...] = a * jnp.where(is_qk, rn * q_scale, 1.0)


def a_pre_fwd(proj_a, conv_w8, n_seq):
    t_rows = proj_a.shape[0]
    s_len = t_rows // n_seq
    n_cb = (2 * A_QK + A_VW) // 128
    blk = pl.BlockSpec((s_len, 128), lambda b, c: (b, c))
    return pl.pallas_call(
        _a_pre_fwd_body, name="a_pre_fwd", grid=(n_seq, n_cb),
        in_specs=[blk, pl.BlockSpec((8, 128), lambda b, c: (0, c))],
        out_specs=blk,
        out_shape=jax.ShapeDtypeStruct((t_rows, n_cb * 128), F32),
        compiler_params=_params(("arbitrary", "arbitrary")),
    )(proj_a, conv_w8)


def _a_pre_bwd_body(x_ref, w_ref, dy_ref, dx_ref, dw_ref):
    b, cb = pl.program_id(1), pl.program_id(0)
    x, w = x_ref[...], w_ref[...]
    rows, c, sig, a, rn, is_qk, q_scale = _conv_silu_norm(x, w, cb)
    dy = dy_ref[...]
    da_n = q_scale * (rn * dy - a * (rn * rn * rn) * jnp.sum(dy * a, axis=1, keepdims=True))
    da = jnp.where(is_qk, da_n, dy)
    dc = da * (sig * (1.0 + c * (1.0 - sig)))
    @pl.when(b == 0)
    def _():
        dw_ref[...] = jnp.zeros_like(dw_ref)

    dx = dc * w[A_CONV - 1:A_CONV, :]
    dw_ref[A_CONV - 1:A_CONV, :] += jnp.sum(dc * x, axis=0, keepdims=True)
    for j in range(1, A_CONV):
        dcs = _shift_up(dc, j, rows)
        dx = dx + dcs * w[A_CONV - 1 - j:A_CONV - j, :]
        dw_ref[A_CONV - 1 - j:A_CONV - j, :] += jnp.sum(dcs * x, axis=0, keepdims=True)
    dx_ref[...] = dx.astype(dx_ref.dtype)


def a_pre_bwd(proj_a, conv_w8, dqkv_hm, n_seq):
    t_rows = proj_a.shape[0]
    s_len = t_rows // n_seq
    n_cb = (2 * A_QK + A_VW) // 128
    blk = pl.BlockSpec((s_len, 128), lambda c, b: (b, c))
    wblk = pl.BlockSpec((8, 128), lambda c, b: (0, c))
    per_head = A_HEAD_W // 128
    n_q = A_QK // 128

    def head_major(c, b):
        v_blk = jnp.maximum(c - 2 * n_q, 0)
        col = jnp.where(c < n_q, c * per_head,
                        jnp.where(c < 2 * n_q, (c - n_q) * per_head + 1,
                                  (v_blk // 2) * per_head + 2 + v_blk % 2))
        return (b, col)

    return pl.pallas_call(
        _a_pre_bwd_body, name="a_pre_bwd", grid=(n_cb, n_seq),
        in_specs=[blk, wblk, pl.BlockSpec((s_len, 128), head_major)],
        out_specs=[blk, wblk],
        out_shape=[jax.ShapeDtypeStruct((t_rows, n_cb * 128), BF16),
                   jax.ShapeDtypeStruct((8, n_cb * 128), F32)],
        compiler_params=_params(("arbitrary", "arbitrary")),
    )(proj_a, conv_w8, dqkv_hm)


GATE_TILE = 512


def _softplus(y):
    return jnp.maximum(y, 0.0) + jnp.log1p(jnp.exp(-jnp.abs(y)))


def _gate_values(x, prm):
    beta = _sigmoid(x)
    y = x + prm[1:2, :]
    neg_a = -jnp.exp(prm[0:1, :])
    g = neg_a * _softplus(y)
    return beta, y, neg_a, g


def _a_gates_fwd_body(x_ref, prm_ref, gc_ref, gr_ref):
    x = x_ref[...]
    tm = x.shape[0]
    beta, _, _, g = _gate_values(x, prm_ref[...])
    in_chunk = lax.broadcasted_iota(jnp.int32, (tm, 1), 0) % CHUNK
    s = 1
    while s < CHUNK:
        g = g + jnp.where(in_chunk >= s, pltpu.roll(g, s, 0), 0.0)
        s *= 2
    lane = lax.broadcasted_iota(jnp.int32, x.shape, 1)
    out = jnp.where(lane < A_HEADS, beta, jnp.where(lane < 2 * A_HEADS, g, 0.0))
    gc_ref[...] = out
    gr_ref[...] = out.T[0:2 * A_HEADS, :]


def a_gates_fwd(proj_a, prm, n_seq):
    t_rows = proj_a.shape[0]
    s_len = t_rows // n_seq
    tm = min(GATE_TILE, s_len)
    n_t = s_len // tm
    return pl.pallas_call(
        _a_gates_fwd_body, name="a_gates_fwd", grid=(n_seq, n_t),
        in_specs=[pl.BlockSpec((tm, 128), lambda b, i: (b * n_t + i, A_GATE_COL)),
                  pl.BlockSpec((8, 128), lambda b, i: (0, 0))],
        out_specs=[pl.BlockSpec((tm, 128), lambda b, i: (b * n_t + i, 0)),
                   pl.BlockSpec((None, 2 * A_HEADS, tm), lambda b, i: (b, 0, i))],
        out_shape=[jax.ShapeDtypeStruct((t_rows, 128), F32),
                   jax.ShapeDtypeStruct((n_seq, 2 * A_HEADS, s_len), F32)],
        compiler_params=_params(("arbitrary", "arbitrary")),
    )(proj_a, prm)


def _a_gates_bwd_body(x_ref, prm_ref, dgc_ref, dx_ref, dprm_ref):
    first = (pl.program_id(0) == 0) & (pl.program_id(1) == 0)
    x = x_ref[...]
    tm = x.shape[0]
    beta, y, neg_a, g = _gate_values(x, prm_ref[...])
    d = dgc_ref[...]
    in_chunk = lax.broadcasted_iota(jnp.int32, (tm, 1), 0) % CHUNK
    dg = d
    s = 1
    while s < CHUNK:
        dg = dg + jnp.where(in_chunk < CHUNK - s, pltpu.roll(dg, tm - s, 0), 0.0)
        s *= 2
    lane = lax.broadcasted_iota(jnp.int32, x.shape, 1)
    is_decay = (lane >= A_HEADS) & (lane < 2 * A_HEADS)
    d_alogit = jnp.where(is_decay, dg * neg_a * _sigmoid(y), 0.0)
    dx_ref[...] = jnp.where(lane < A_HEADS, d * beta * (1.0 - beta), d_alogit).astype(dx_ref.dtype)

    @pl.when(first)
    def _():
        dprm_ref[...] = jnp.zeros_like(dprm_ref)

    dprm_ref[0:1, :] += jnp.sum(jnp.where(is_decay, dg * g, 0.0), axis=0, keepdims=True)
    dprm_ref[1:2, :] += jnp.sum(d_alogit, axis=0, keepdims=True)


def a_gates_bwd(proj_a, prm, dgates_col, n_seq):
    t_rows = proj_a.shape[0]
    s_len = t_rows // n_seq
    tm = min(GATE_TILE, s_len)
    n_t = s_len // tm
    return pl.pallas_call(
        _a_gates_bwd_body, name="a_gates_bwd", grid=(n_seq, n_t),
        in_specs=[pl.BlockSpec((tm, 128), lambda b, i: (b * n_t + i, A_GATE_COL)),
                  pl.BlockSpec((8, 128), lambda b, i: (0, 0)),
                  pl.BlockSpec((tm, 128), lambda b, i: (b * n_t + i, 0))],
        out_specs=[pl.BlockSpec((tm, 128), lambda b, i: (b * n_t + i, 0)),
                   pl.BlockSpec((8, 128), lambda b, i: (0, 0))],
        out_shape=[jax.ShapeDtypeStruct((t_rows, 128), BF16),
                   jax.ShapeDtypeStruct((8, 128), F32)],
        compiler_params=_params(("arbitrary", "arbitrary")),
    )(proj_a, prm, dgates_col)


ROW_TILE = 512
A_Z_COL = (2 * A_QK + A_VW) // A_DV


def _silu_parts(z):
    sig = _sigmoid(z)
    return z * sig, sig * (1.0 + z * (1.0 - sig))


def _a_post_fwd_body(o_ref, z_ref, g_ref, og_ref):
    o = o_ref[...]
    r = lax.rsqrt(jnp.mean(o * o, axis=1, keepdims=True) + EPS)
    silu, _ = _silu_parts(z_ref[...])
    og_ref[...] = ((o * r * g_ref[0:1, :]) * silu).astype(og_ref.dtype)


def a_post_fwd(o, proj_a, norm_g8):
    t_rows = o.shape[0]
    tm = min(ROW_TILE, t_rows)
    blk = pl.BlockSpec((tm, A_DV), lambda i, h: (i, h))
    return pl.pallas_call(
        _a_post_fwd_body, name="a_post_fwd", grid=(t_rows // tm, A_HEADS),
        in_specs=[blk, pl.BlockSpec((tm, A_DV), lambda i, h: (i, A_Z_COL + h)),
                  pl.BlockSpec((8, A_DV), lambda i, h: (0, 0))],
        out_specs=blk,
        out_shape=jax.ShapeDtypeStruct((t_rows, A_VW), BF16),
        compiler_params=_params(("arbitrary", "arbitrary")),
    )(o, proj_a, norm_g8)


def _a_post_bwd_body(o_ref, z_ref, g_ref, dog_ref, do_ref, dz_ref, dg_ref):
    first = (pl.program_id(0) == 0) & (pl.program_id(1) == 0)
    o, z, d_og = o_ref[...], z_ref[...], dog_ref[...]
    gain = g_ref[0:1, :]
    r = lax.rsqrt(jnp.mean(o * o, axis=1, keepdims=True) + EPS)
    silu, dsilu = _silu_parts(z)
    xr = o * r
    d_on = d_og * silu
    dz_ref[...] = (d_og * (xr * gain) * dsilu).astype(dz_ref.dtype)
    u = d_on * gain
    do_ref[...] = r * u - xr * (r * r) * jnp.mean(u * o, axis=1, keepdims=True)

    @pl.when(first)
    def _():
        dg_ref[...] = jnp.zeros_like(dg_ref)

    dg_ref[0:1, :] += jnp.sum(d_on * xr, axis=0, keepdims=True)


def a_post_bwd(o, proj_a, norm_g8, d_og):
    t_rows = o.shape[0]
    tm = min(ROW_TILE, t_rows)
    blk = pl.BlockSpec((tm, A_DV), lambda i, h: (i, h))
    gblk = pl.BlockSpec((8, A_DV), lambda i, h: (0, 0))
    return pl.pallas_call(
        _a_post_bwd_body, name="a_post_bwd", grid=(t_rows // tm, A_HEADS),
        in_specs=[blk, pl.BlockSpec((tm, A_DV), lambda i, h: (i, A_Z_COL + h)), gblk, blk],
        out_specs=[blk, blk, gblk],
        out_shape=[jax.ShapeDtypeStruct((t_rows, A_VW), F32),
                   jax.ShapeDtypeStruct((t_rows, A_VW), BF16),
                   jax.ShapeDtypeStruct((8, A_DV), F32)],
        compiler_params=_params(("arbitrary", "arbitrary")),
    )(o, proj_a, norm_g8, d_og)


NEG_BIG = -1e30
ATT_SCALE = B_DH ** -0.5


def _swap_rope_halves(x):
    src = lax.broadcasted_iota(jnp.int32, (B_DH, B_DH), 0)
    dst = lax.broadcasted_iota(jnp.int32, (B_DH, B_DH), 1)
    pick = ((dst < ROPE_HALF) & (src == dst + ROPE_HALF)) | (
        (dst >= ROPE_HALF) & (dst < ROPE_DIMS) & (src == dst - ROPE_HALF))
    return jnp.dot(_bf(x), pick.astype(BF16), preferred_element_type=F32)


def _norm_rope(x, gain, cos_t, sin_t):
    r = lax.rsqrt(jnp.mean(x * x, axis=1, keepdims=True) + EPS)
    xn = x * r * gain
    return xn * cos_t + _swap_rope_halves(xn) * sin_t, r


def _norm_rope_bwd(x, r, gain, cos_t, sin_t, dy):
    d_xn = dy * cos_t + _swap_rope_halves(dy * sin_t)
    xr = x * r
    u = d_xn * gain
    dx = r * u - xr * (r * r) * jnp.mean(u * x, axis=1, keepdims=True)
    return dx, jnp.sum(d_xn * xr, axis=0, keepdims=True)


def _stream_rows(idx, dilation, s_len):
    nb = s_len // dilation // B_BLOCK
    r = idx // nb
    m = idx % nb
    cur = r + m * (B_BLOCK * dilation)
    prev = r + jnp.maximum(m - 1, 0) * (B_BLOCK * dilation)
    return cur, prev, m > 0


def _rows(start, dilation):
    if dilation == 1:
        return pl.ds(start, B_BLOCK)
    return pl.ds(start, B_BLOCK, stride=dilation)


ATT_UNROLL = 4


def _band_mask(has_prev):
    qi = lax.broadcasted_iota(jnp.int32, (B_BLOCK, 2 * B_BLOCK), 0)
    kj = lax.broadcasted_iota(jnp.int32, (B_BLOCK, 2 * B_BLOCK), 1)
    return ((kj < B_BLOCK) & (kj >= qi) & has_prev) | ((kj >= B_BLOCK) & (kj - B_BLOCK <= qi))


def _block_scores(qb, kc, kp, has_prev):
    qi = lax.broadcasted_iota(jnp.int32, (B_BLOCK, B_BLOCK), 0)
    kj = lax.broadcasted_iota(jnp.int32, (B_BLOCK, B_BLOCK), 1)
    s_c = jnp.where(qi >= kj, _mm_nt(qb, kc) * ATT_SCALE, NEG_BIG)
    s_p = jnp.where((kj >= qi) & has_prev, _mm_nt(qb, kp) * ATT_SCALE, NEG_BIG)
    return s_c, s_p


def _attn_fwd_body(qkv_ref, z_ref, cos_ref, sin_ref, gain_ref, og_ref, o_ref, lse_ref,
                   qn_scr, kn_scr, og_scr, lg_scr):
    head, grp = pl.program_id(1), pl.program_id(2)
    s_len = z_ref.shape[0]
    n_blocks = s_len // B_BLOCK
    cos_t, sin_t = cos_ref[...], sin_ref[...]

    for gi, dil in enumerate(B_DILATIONS):
        @pl.when(grp == gi)
        def _(gi=gi, dil=dil):
            qn_scr[...], _ = _norm_rope(qkv_ref[0], gain_ref[gi:gi + 1, :], cos_t, sin_t)
            kn_scr[...], _ = _norm_rope(qkv_ref[1], gain_ref[B_GROUPS + gi:B_GROUPS + gi + 1, :], cos_t, sin_t)

            ones = jnp.ones((2 * B_BLOCK, B_DH), BF16)

            def blocks(it, carry):
                scored = []
                for j in range(ATT_UNROLL):
                    cur, prev, has_prev = _stream_rows(it * ATT_UNROLL + j, dil, s_len)
                    rc, rp = _rows(cur, dil), _rows(prev, dil)
                    k2 = jnp.concatenate([kn_scr[rp, :], kn_scr[rc, :]], axis=0)
                    scored.append((rc, rp, has_prev, _mm_nt(qn_scr[rc, :], k2) * ATT_SCALE))
                summed = []
                for rc, rp, has_prev, s in scored:
                    s = jnp.where(_band_mask(has_prev), s, NEG_BIG)
                    mx = jnp.max(s, axis=1, keepdims=True)
                    v2 = jnp.concatenate([qkv_ref.at[2][rp, :], qkv_ref.at[2][rc, :]], axis=0)
                    acc = jnp.dot(_bf(jnp.exp(s - mx)), jnp.concatenate([_bf(v2), ones], axis=1),
                                  preferred_element_type=F32)
                    summed.append((rc, mx, acc))
                for rc, mx, acc in summed:
                    den = acc[:, B_DH:B_DH + 1]
                    og_scr.at[gi][rc, :] = acc[:, :B_DH] / den
                    lg_scr.at[gi][rc, :] = jnp.broadcast_to(mx + jnp.log(den), (B_BLOCK, B_DH))
                return carry

            lax.fori_loop(0, n_blocks // ATT_UNROLL, blocks, 0)

    @pl.when(grp == B_GROUPS - 1)
    def _():
        l0, l1, l2 = lg_scr[0], lg_scr[1], lg_scr[2]
        mx = jnp.maximum(jnp.maximum(l0, l1), l2)
        w0, w1, w2 = jnp.exp(l0 - mx), jnp.exp(l1 - mx), jnp.exp(l2 - mx)
        den = w0 + w1 + w2
        o = (w0 * og_scr[0] + w1 * og_scr[1] + w2 * og_scr[2]) / den
        silu, _ = _silu_parts(z_ref[...])
        o_ref[...] = o
        og_ref[...] = (o * silu).astype(og_ref.dtype)
        @pl.when(head == 0)
        def _():
            lse_ref[...] = jnp.zeros_like(lse_ref)

        lane = lax.broadcasted_iota(jnp.int32, o.shape, 1)
        lse_ref[...] = jnp.where(lane == head, mx + jnp.log(den), lse_ref[...])


def attn_fwd(proj_b, cos_t, sin_t, gains8, n_seq):
    t_rows = proj_b.shape[1]
    s_len = t_rows // n_seq
    head_blk = pl.BlockSpec((s_len, B_DH), lambda b, h, g: (b, h))
    seq_blk = pl.BlockSpec((s_len, 128), lambda b, h, g: (b, 0))
    return pl.pallas_call(
        _attn_fwd_body, name="attn_fwd", grid=(n_seq, B_HEADS, B_GROUPS),
        in_specs=[
            pl.BlockSpec((3, s_len, B_DH), lambda b, h, g: (g, b, h)),
            pl.BlockSpec((None, s_len, B_DH), lambda b, h, g: (B_PIECES - 1, b, h)),
            seq_blk, seq_blk,
            pl.BlockSpec((8, 128), lambda b, h, g: (0, 0)),
        ],
        out_specs=[head_blk, head_blk, seq_blk],
        out_shape=[jax.ShapeDtypeStruct((t_rows, B_W), BF16),
                   jax.ShapeDtypeStruct((t_rows, B_W), F32),
                   jax.ShapeDtypeStruct((t_rows, 128), F32)],
        scratch_shapes=[pltpu.VMEM((s_len, B_DH), F32), pltpu.VMEM((s_len, B_DH), F32),
                        pltpu.VMEM((B_GROUPS, s_len, B_DH), F32), pltpu.VMEM((B_GROUPS, s_len, B_DH), F32)],
        compiler_params=_params(("arbitrary", "arbitrary", "arbitrary")),
    )(proj_b, proj_b, cos_t, sin_t, gains8)


def _attn_bwd_body(qkv_ref, z_ref, cos_ref, sin_ref, gain_ref, dog_ref, o_ref, lse_ref,
                   dqkv_ref, dz_ref, dgain_ref,
                   qn_scr, kn_scr, dqn_scr, dkn_scr, do_scr, dl_scr, ls_scr, dv_scr):
    head, grp = pl.program_id(1), pl.program_id(2)
    first = (pl.program_id(0) == 0) & (head == 0) & (grp == 0)
    s_len = z_ref.shape[0]
    n_blocks = s_len // B_BLOCK
    cos_t, sin_t = cos_ref[...], sin_ref[...]

    @pl.when(first)
    def _():
        dgain_ref[...] = jnp.zeros_like(dgain_ref)

    @pl.when(grp == 0)
    def _():
        d_og, o = dog_ref[...], o_ref[...]
        silu, dsilu = _silu_parts(z_ref[...])
        d_o = d_og * silu
        dz_ref[...] = (d_og * o * dsilu).astype(dz_ref.dtype)
        do_scr[...] = d_o
        dl_scr[...] = jnp.broadcast_to(jnp.sum(d_o * o, axis=1, keepdims=True), o.shape)
        lane = lax.broadcasted_iota(jnp.int32, o.shape, 1)
        ls_scr[...] = jnp.broadcast_to(
            jnp.sum(jnp.where(lane == head, lse_ref[...], 0.0), axis=1, keepdims=True), o.shape)

    for gi, dil in enumerate(B_DILATIONS):
        @pl.when(grp == gi)
        def _(gi=gi, dil=dil):
            q_raw, k_raw = qkv_ref[0], qkv_ref[1]
            gq = gain_ref[gi:gi + 1, :]
            gk = gain_ref[B_GROUPS + gi:B_GROUPS + gi + 1, :]
            qn_scr[...], rq = _norm_rope(q_raw, gq, cos_t, sin_t)
            kn_scr[...], rk = _norm_rope(k_raw, gk, cos_t, sin_t)
            def blocks(it, carry):
                scored = []
                for j in range(ATT_UNROLL):
                    cur, prev, has_prev = _stream_rows(it * ATT_UNROLL + j, dil, s_len)
                    rc, rp = _rows(cur, dil), _rows(prev, dil)
                    qb, d_ob = _bf(qn_scr[rc, :]), _bf(do_scr[rc, :])
                    k2 = _bf(jnp.concatenate([kn_scr[rp, :], kn_scr[rc, :]], axis=0))
                    v2 = _bf(jnp.concatenate([qkv_ref.at[2][rp, :], qkv_ref.at[2][rc, :]], axis=0))
                    scored.append((rc, rp, has_prev, qb, d_ob, k2,
                                   _mm_nt(qb, k2) * ATT_SCALE, _mm_nt(d_ob, v2)))
                grads = []
                for rc, rp, has_prev, qb, d_ob, k2, s, d_p in scored:
                    p = jnp.exp(jnp.where(_band_mask(has_prev), s - ls_scr[rc, :][:, 0:1], NEG_BIG))
                    ds = _bf(p * (d_p - dl_scr[rc, :][:, 0:1]))
                    grads.append((rc, rp, has_prev,
                                  _mm(ds, k2) * ATT_SCALE, _mm_tn(ds, qb) * ATT_SCALE, _mm_tn(_bf(p), d_ob)))
                for j, (rc, rp, has_prev, dq, dk2, dv2) in enumerate(grads):
                    dqn_scr[rc, :] = dq
                    if j == 0:
                        @pl.when(has_prev)
                        def _():
                            dkn_scr[rp, :] += dk2[:B_BLOCK]
                            dv_scr[rp, :] += dv2[:B_BLOCK]
                    if j + 1 < ATT_UNROLL:
                        dkn_scr[rc, :] = dk2[B_BLOCK:] + grads[j + 1][4][:B_BLOCK]
                        dv_scr[rc, :] = dv2[B_BLOCK:] + grads[j + 1][5][:B_BLOCK]
                    else:
                        dkn_scr[rc, :] = dk2[B_BLOCK:]
                        dv_scr[rc, :] = dv2[B_BLOCK:]
                return carry

            lax.fori_loop(0, n_blocks // ATT_UNROLL, blocks, 0)
            dq, dgq = _norm_rope_bwd(q_raw, rq, gq, cos_t, sin_t, dqn_scr[...])
            dk, dgk = _norm_rope_bwd(k_raw, rk, gk, cos_t, sin_t, dkn_scr[...])
            dqkv_ref[0] = dq.astype(dqkv_ref.dtype)
            dqkv_ref[1] = dk.astype(dqkv_ref.dtype)
            dqkv_ref[2] = dv_scr[...].astype(dqkv_ref.dtype)
            dgain_ref[gi:gi + 1, :] += dgq
            dgain_ref[B_GROUPS + gi:B_GROUPS + gi + 1, :] += dgk


def attn_bwd(proj_b, cos_t, sin_t, gains8, d_og, o, lse, n_seq):
    t_rows = proj_b.shape[1]
    s_len = t_rows // n_seq
    head_blk = pl.BlockSpec((s_len, B_DH), lambda b, h, g: (b, h))
    seq_blk = pl.BlockSpec((s_len, 128), lambda b, h, g: (b, 0))
    grp_blk = pl.BlockSpec((3, s_len, B_DH), lambda b, h, g: (g, b, h))
    gain_blk = pl.BlockSpec((8, 128), lambda b, h, g: (0, 0))
    return pl.pallas_call(
        _attn_bwd_body, name="attn_bwd", grid=(n_seq, B_HEADS, B_GROUPS),
        in_specs=[
            grp_blk,
            pl.BlockSpec((None, s_len, B_DH), lambda b, h, g: (B_PIECES - 1, b, h)),
            seq_blk, seq_blk, gain_blk, head_blk, head_blk, seq_blk,
        ],
        out_specs=[grp_blk, head_blk, gain_blk],
        out_shape=[jax.ShapeDtypeStruct((3 * B_GROUPS, t_rows, B_W), BF16),
                   jax.ShapeDtypeStruct((t_rows, B_W), BF16),
                   jax.ShapeDtypeStruct((8, 128), F32)],
        scratch_shapes=[pltpu.VMEM((s_len, B_DH), F32) for _ in range(8)],
        compiler_params=_params(("arbitrary", "arbitrary", "arbitrary")),
    )(proj_b, proj_b, cos_t, sin_t, gains8, d_og, o, lse)


def rope_tables(positions):
    inv_freq = ROPE_THETA ** (-jnp.arange(0, ROPE_DIMS, 2, dtype=F32) / ROPE_DIMS)
    ang = positions.astype(F32)[:, None] * inv_freq
    cos, sin = jnp.cos(ang), jnp.sin(ang)
    t_rows = positions.shape[0]
    rest = B_DH - ROPE_DIMS
    cos_t = jnp.concatenate([cos, cos, jnp.ones((t_rows, rest), F32)], axis=1)
    sin_t = jnp.concatenate([-sin, sin, jnp.zeros((t_rows, rest), F32)], axis=1)
    return cos_t, sin_t


def _rms_fwd_body(x_ref, g_ref, *rest, layer):
    h_ref = rest[-1]
    x = x_ref[...]
    r = lax.rsqrt(jnp.mean(x * x, axis=1, keepdims=True) + EPS)
    h_ref[...] = (x * r * g_ref[layer:layer + 1, :]).astype(h_ref.dtype)


def rms_fwd(x, gains8, layer, after=None):
    t_rows, d = x.shape
    tm = min(ROW_TILE, t_rows)
    in_specs = [pl.BlockSpec((tm, d), lambda i: (i, 0)), pl.BlockSpec((8, d), lambda i: (0, 0))]
    args = [x, gains8]
    if after is not None:
        in_specs.append(HBM_SPEC)
        args.append(after)
    return pl.pallas_call(
        functools.partial(_rms_fwd_body, layer=layer), name=f"rms_fwd_{layer}", grid=(t_rows // tm,),
        in_specs=in_specs,
        out_specs=pl.BlockSpec((tm, d), lambda i: (i, 0)),
        out_shape=jax.ShapeDtypeStruct((t_rows, d), BF16),
        compiler_params=_params(("arbitrary",)),
    )(*args)


def _rms_bwd_body(x_ref, g_ref, dh_ref, res_ref, dx_ref, dg_ref, *, layer):
    x, dh = x_ref[...], dh_ref[...]
    r = lax.rsqrt(jnp.mean(x * x, axis=1, keepdims=True) + EPS)
    xr = x * r
    u = dh * g_ref[layer:layer + 1, :]
    dx_ref[...] = res_ref[...] + r * u - xr * (r * r) * jnp.mean(u * x, axis=1, keepdims=True)

    @pl.when(pl.program_id(0) == 0)
    def _():
        dg_ref[...] = jnp.zeros_like(dg_ref)

    dg_ref[0:1, :] += jnp.sum(dh * xr, axis=0, keepdims=True)


def rms_bwd(x, gains8, layer, dh, d_res):
    t_rows, d = x.shape
    tm = min(ROW_TILE, t_rows)
    blk = pl.BlockSpec((tm, d), lambda i: (i, 0))
    gblk = pl.BlockSpec((8, d), lambda i: (0, 0))
    return pl.pallas_call(
        functools.partial(_rms_bwd_body, layer=layer), name=f"rms_bwd_{layer}", grid=(t_rows // tm,),
        in_specs=[blk, gblk, blk, blk],
        out_specs=[blk, gblk],
        out_shape=[jax.ShapeDtypeStruct((t_rows, d), F32), jax.ShapeDtypeStruct((8, d), F32)],
        compiler_params=_params(("arbitrary",)),
    )(x, gains8, dh, d_res)


def _piece_col(p):
    return jnp.where(p < 3 * B_GROUPS, (p % 3) * B_GROUPS + p // 3, 3 * B_GROUPS)


def _mm_nn_body(a_ref, w_ref, *rest, has_res):
    o_ref = rest[-1]
    acc = jnp.dot(a_ref[...], w_ref[...], preferred_element_type=F32)
    if has_res:
        acc = acc + rest[0][...]
    o_ref[...] = acc


def mm_nn(a, w, residual=None, *, tn, name):
    m, k = a.shape
    n = w.shape[1]
    tm = min(ROW_TILE, m)
    in_specs = [pl.BlockSpec((tm, k), lambda j, i: (i, 0)), pl.BlockSpec((k, tn), lambda j, i: (0, j))]
    args = [a, w]
    if residual is not None:
        in_specs.append(pl.BlockSpec((tm, tn), lambda j, i: (i, j)))
        args.append(residual)
    return pl.pallas_call(
        functools.partial(_mm_nn_body, has_res=residual is not None), name=name, grid=(n // tn, m // tm),
        in_specs=in_specs,
        out_specs=pl.BlockSpec((tm, tn), lambda j, i: (i, j)),
        out_shape=jax.ShapeDtypeStruct((m, n), F32),
        compiler_params=_params(("arbitrary", "arbitrary")),
    )(*args)


def mm_nn_pieces(a, w, *, name):
    m, k = a.shape
    tm = min(ROW_TILE, m)
    return pl.pallas_call(
        functools.partial(_mm_nn_body, has_res=False), name=name, grid=(B_PIECES, m // tm),
        in_specs=[pl.BlockSpec((tm, k), lambda p, i: (i, 0)),
                  pl.BlockSpec((k, B_W), lambda p, i: (0, _piece_col(p)))],
        out_specs=pl.BlockSpec((None, tm, B_W), lambda p, i: (p, i, 0)),
        out_shape=jax.ShapeDtypeStruct((B_PIECES, m, B_W), F32),
        compiler_params=_params(("arbitrary", "arbitrary")),
    )(a, w)


NT_ROW_TILE = 1024


def _mm_nt_body(g_ref, w_ref, *rest, has_init):
    o_ref = rest[-1]
    j = pl.program_id(1)
    part = lax.dot_general(_bf(g_ref[...]), w_ref[...], (((1,), (1,)), ((), ())), preferred_element_type=F32)

    @pl.when(j == 0)
    def _():
        o_ref[...] = part + rest[0][...] if has_init else part

    @pl.when(j > 0)
    def _():
        o_ref[...] += part


def mm_nt(g, w, init=None, *, tn, col_off=0, name, after=None):
    m, n = g.shape
    k = w.shape[0]
    tm = min(NT_ROW_TILE, m)
    in_specs = [pl.BlockSpec((tm, tn), lambda i, j: (i, j)),
                pl.BlockSpec((k, tn), lambda i, j: (0, col_off + j))]
    args = [g, w]
    if init is not None:
        in_specs.append(pl.BlockSpec((tm, k), lambda i, j: (i, 0)))
        args.append(init)
    if after is not None:
        in_specs.append(HBM_SPEC)
        args.append(after)
    return pl.pallas_call(
        functools.partial(_mm_nt_body, has_init=init is not None), name=name, grid=(m // tm, n // tn),
        in_specs=in_specs,
        out_specs=pl.BlockSpec((tm, k), lambda i, j: (i, 0)),
        out_shape=jax.ShapeDtypeStruct((m, k), F32),
        compiler_params=_params(("arbitrary", "arbitrary")),
    )(*args)


def mm_nt_pieces(g9, w, *, name):
    n_p, m, _ = g9.shape
    k = w.shape[0]
    tm = min(NT_ROW_TILE, m)
    return pl.pallas_call(
        functools.partial(_mm_nt_body, has_init=False), name=name, grid=(m // tm, n_p),
        in_specs=[pl.BlockSpec((None, tm, B_W), lambda i, p: (p, i, 0)),
                  pl.BlockSpec((k, B_W), lambda i, p: (0, _piece_col(p)))],
        out_specs=pl.BlockSpec((tm, k), lambda i, p: (i, 0)),
        out_shape=jax.ShapeDtypeStruct((m, k), F32),
        compiler_params=_params(("arbitrary", "arbitrary")),
    )(g9, w)


def _mm_tn_body(a_ref, g_ref, o_ref):
    o_ref[...] = lax.dot_general(a_ref[...], _bf(g_ref[...]), (((0,), (0,)), ((), ())),
                                 preferred_element_type=F32).astype(o_ref.dtype)


def mm_tn(a, g, *, tn, out_dtype, name):
    m, k = a.shape
    n = g.shape[1]
    return pl.pallas_call(
        _mm_tn_body, name=name, grid=(n // tn,),
        in_specs=[pl.BlockSpec((m, k), lambda j: (0, 0)), pl.BlockSpec((m, tn), lambda j: (0, j))],
        out_specs=pl.BlockSpec((k, tn), lambda j: (0, j)),
        out_shape=jax.ShapeDtypeStruct((k, n), out_dtype),
        compiler_params=_params(("arbitrary",)),
    )(a, g)


B_UNIT = 256
B_IN_COLS = B_PIECES * B_W
B_SHARD_UNITS = B_IN_COLS // N_DEV // B_UNIT


def mm_tn_b_in(a, g9, gz, *, out_dtype, name):
    m, k = a.shape
    per_piece = B_W // B_UNIT
    n_units = B_IN_COLS // B_UNIT

    def g_map(u):
        nat = jnp.minimum(u // per_piece, 3 * B_GROUPS - 1)
        piece = (nat % B_GROUPS) * 3 + nat // B_GROUPS
        return (piece, 0, u % per_piece)

    def body(a_ref, g_ref, z_ref, o_ref):
        u = pl.program_id(0)

        @pl.when(u < 3 * B_GROUPS * per_piece)
        def _():
            _mm_tn_body(a_ref, g_ref, o_ref)

        @pl.when(u >= 3 * B_GROUPS * per_piece)
        def _():
            _mm_tn_body(a_ref, z_ref, o_ref)

    return pl.pallas_call(
        body, name=name, grid=(n_units,),
        in_specs=[pl.BlockSpec((m, k), lambda u: (0, 0)),
                  pl.BlockSpec((None, m, B_UNIT), g_map),
                  pl.BlockSpec((m, B_UNIT), lambda u: (0, jnp.where(u < 3 * B_GROUPS * per_piece, 0, u % per_piece)))],
        out_specs=pl.BlockSpec((None, k, B_UNIT), lambda u: (u // B_SHARD_UNITS, 0, u % B_SHARD_UNITS)),
        out_shape=jax.ShapeDtypeStruct((N_DEV, k, B_IN_COLS // N_DEV), out_dtype),
        compiler_params=_params(("arbitrary",)),
    )(a, g9, gz)


def _loss_body(y_ref, t_ref, dy_ref, loss_ref, acc):
    i = pl.program_id(0)
    d = y_ref.shape[1]
    err = y_ref[...] - t_ref[...]
    dy_ref[...] = err * (1.0 / d)

    @pl.when(i == 0)
    def _():
        acc[...] = jnp.zeros_like(acc)

    acc[...] += jnp.sum(err * err, axis=0, keepdims=True)

    @pl.when(i == pl.num_programs(0) - 1)
    def _():
        total = jnp.sum(acc[...], axis=1, keepdims=True) * (0.5 / d)
        loss_ref[...] = jnp.broadcast_to(total, loss_ref.shape)


def loss_head(y, target):
    t_rows, d = y.shape
    tm = min(ROW_TILE, t_rows)
    blk = pl.BlockSpec((tm, d), lambda i: (i, 0))
    return pl.pallas_call(
        _loss_body, name="loss_head", grid=(t_rows // tm,),
        in_specs=[blk, blk],
        out_specs=[blk, pl.BlockSpec((8, 128), lambda i: (0, 0))],
        out_shape=[jax.ShapeDtypeStruct((t_rows, d), F32), jax.ShapeDtypeStruct((8, 128), F32)],
        scratch_shapes=[pltpu.VMEM((1, d), F32)],
        compiler_params=_params(("arbitrary",)),
    )(y, target)


def _adamw_body(p_ref, w_ref, m_ref, v_ref, g_ref, d_ref, nm_ref, nv_ref):
    g = p_ref[0].astype(F32)
    for s in range(1, N_DEV):
        g = g + p_ref[s].astype(F32)
    w = w_ref[...]
    m = ADAM_B1 * m_ref[...] + (1.0 - ADAM_B1) * g
    v = ADAM_B2 * v_ref[...] + (1.0 - ADAM_B2) * (g * g)
    m_hat = m / (1.0 - ADAM_B1 ** ADAM_STEP)
    v_hat = v / (1.0 - ADAM_B2 ** ADAM_STEP)
    g_ref[...] = g
    d_ref[...] = -ADAM_LR * (m_hat / (jnp.sqrt(v_hat) + ADAM_EPS) + ADAM_WD * w)
    nm_ref[...] = m
    nv_ref[...] = v


def adamw(parts, w, m, v, *, name):
    _, r, c = w.shape
    tr = r if r <= 256 else 256
    blk = pl.BlockSpec((None, tr, c), lambda i: (0, i, 0))
    out = jax.ShapeDtypeStruct((1, r, c), F32)
    return pl.pallas_call(
        _adamw_body, name=name, grid=(r // tr,),
        in_specs=[pl.BlockSpec((N_DEV, tr, c), lambda i: (0, i, 0)), blk, blk, blk],
        out_specs=[blk, blk, blk, blk],
        out_shape=[out, out, out, out],
        compiler_params=_params(("arbitrary",)),
    )(parts, w, m, v)


MESH_ID = pl.DeviceIdType.MESH
HBM_SPEC = pl.BlockSpec(memory_space=pl.ANY)


def _my_place():
    return lax.axis_index("x"), lax.axis_index("y"), lax.axis_index("c")


def _flat(x, y, c):
    return 4 * x + 2 * y + c


def _all_gather_body(*refs, n):
    ins, outs = refs[:n], refs[n:2 * n]
    send_sems, recv_sems, local_sems = refs[2 * n:]
    x, y, c = _my_place()
    me, sibling = (x, y, c), (x, y, 1 - c)
    chips = [(1 - x, y), (x, 1 - y), (1 - x, 1 - y)]
    pending = []
    for a in range(n):
        src, out = ins[a], outs[a]

        def copy(k, block, to, from_input=False, a=a, src=src, out=out):
            slot = out.at[_flat(*block)]
            return pltpu.make_async_remote_copy(
                src_ref=src if from_input else slot, dst_ref=slot,
                send_sem=send_sems.at[7 * a + k], recv_sem=recv_sems.at[7 * a + k],
                device_id=to, device_id_type=MESH_ID)

        mine = pltpu.make_async_copy(src, out.at[_flat(*me)], local_sems.at[a])
        mine.start()
        first = [copy(0, me, sibling, True)] + [copy(1 + j, me, (*chip, c), True) for j, chip in enumerate(chips)]
        for cp in first:
            cp.start()
        pending.append((copy, mine, first))
    for copy, mine, first in pending:
        passed = [copy(4 + j, (*chip, c), sibling) for j, chip in enumerate(chips)]
        for j, chip in enumerate(chips):
            copy(1 + j, (*chip, c), me).wait_recv()
            passed[j].start()
        copy(0, sibling, me).wait_recv()
        for j, chip in enumerate(chips):
            copy(4 + j, (*chip, 1 - c), me).wait_recv()
        for cp in first + passed:
            cp.wait_send()
        mine.wait()


def all_gather(shards, *, name):
    n = len(shards)
    return pl.pallas_call(
        functools.partial(_all_gather_body, n=n), name=name,
        in_specs=[HBM_SPEC] * n, out_specs=[HBM_SPEC] * n,
        out_shape=[jax.ShapeDtypeStruct((N_DEV,) + s.shape, s.dtype) for s in shards],
        scratch_shapes=[pltpu.SemaphoreType.DMA((7 * n,)), pltpu.SemaphoreType.DMA((7 * n,)),
                        pltpu.SemaphoreType.DMA((n,))],
    )(*shards)


PEER_FLIPS = [(0, 0, 1), (1, 0, 0), (0, 1, 0), (1, 1, 0), (1, 0, 1), (0, 1, 1), (1, 1, 1)]


def _all_to_all_body(*refs, n):
    ins, outs = refs[:n], refs[n:2 * n]
    send_sems, recv_sems, local_sems = refs[2 * n:]
    x, y, c = _my_place()
    me = _flat(x, y, c)
    waits = []
    for a in range(n):
        src, out = ins[a], outs[a]
        mine = pltpu.make_async_copy(src.at[me], out.at[me], local_sems.at[a])
        mine.start()
        waits.append(mine)
        for k, (fx, fy, fc) in enumerate(PEER_FLIPS):
            peer = (1 - x if fx else x, 1 - y if fy else y, 1 - c if fc else c)
            theirs = _flat(*peer)
            sems = dict(send_sem=send_sems.at[7 * a + k], recv_sem=recv_sems.at[7 * a + k],
                        device_id=peer, device_id_type=MESH_ID)
            send = pltpu.make_async_remote_copy(src_ref=src.at[theirs], dst_ref=out.at[me], **sems)
            send.start()
            recv = pltpu.make_async_remote_copy(src_ref=src.at[theirs], dst_ref=out.at[theirs], **sems)
            waits.append((send, recv))
    for w in waits:
        if isinstance(w, tuple):
            w[0].wait_send()
            w[1].wait_recv()
        else:
            w.wait()


def all_to_all(parts, *, name):
    n = len(parts)
    return pl.pallas_call(
        functools.partial(_all_to_all_body, n=n), name=name,
        in_specs=[HBM_SPEC] * n, out_specs=[HBM_SPEC] * n,
        out_shape=[jax.ShapeDtypeStruct(p.shape, p.dtype) for p in parts],
        scratch_shapes=[pltpu.SemaphoreType.DMA((7 * n,)), pltpu.SemaphoreType.DMA((7 * n,)),
                        pltpu.SemaphoreType.DMA((n,))],
    )(*parts)


HBM_ONLY = pl.BlockSpec(memory_space=pltpu.HBM)
SEM_SPEC = pl.BlockSpec(memory_space=pltpu.SEMAPHORE)
DATAFLOW_EFFECT = pltpu.SideEffectType.DATAFLOW_SIDE_EFFECTING


def _split_copies(srcs, lands, send_sems, recv_sems, n, scatter):
    x, y, c = _my_place()
    me = _flat(x, y, c)
    pairs = []
    for a in range(n):
        for k, (fx, fy, fc) in enumerate(PEER_FLIPS):
            peer = (1 - x if fx else x, 1 - y if fy else y, 1 - c if fc else c)
            theirs = _flat(*peer)
            src = srcs[a].at[theirs] if scatter else srcs[a]
            sems = dict(send_sem=send_sems.at[7 * a + k], recv_sem=recv_sems.at[7 * a + k],
                        device_id=peer, device_id_type=MESH_ID)
            pairs.append((pltpu.make_async_remote_copy(src_ref=src, dst_ref=lands[a].at[me], **sems),
                          pltpu.make_async_remote_copy(src_ref=src, dst_ref=lands[a].at[theirs], **sems)))
    return pairs


def _exchange_start_body(*refs, n, scatter):
    srcs, lands = refs[:n], refs[n:2 * n]
    send_sems, recv_sems = refs[2 * n], refs[2 * n + 1]
    token = refs[-1]
    for send, _ in _split_copies(srcs, lands, send_sems, recv_sems, n, scatter):
        send.start()
    token[...] = jnp.zeros_like(token)


def exchange_start(srcs, lands, *, scatter, name):
    n = len(srcs)
    args = [pltpu.with_memory_space_constraint(t, pltpu.HBM) for t in list(srcs) + list(lands)]
    outs = pl.pallas_call(
        functools.partial(_exchange_start_body, n=n, scatter=scatter), name=name,
        out_shape=(pltpu.SemaphoreType.DMA((7 * n,)), pltpu.SemaphoreType.DMA((7 * n,)),
                   *[pltpu.HBM(t.shape, t.dtype) for t in args],
                   jax.ShapeDtypeStruct((8, 128), F32)),
        in_specs=[HBM_ONLY] * (2 * n),
        out_specs=(SEM_SPEC, SEM_SPEC, *[HBM_ONLY] * (2 * n), pl.BlockSpec(memory_space=pltpu.VMEM)),
        input_output_aliases={i: 2 + i for i in range(2 * n)},
        compiler_params=pltpu.CompilerParams(has_side_effects=DATAFLOW_EFFECT),
    )(*args)
    return outs[0], outs[1], outs[2:2 + n], outs[2 + n:2 + 2 * n], outs[-1]


def _exchange_wait_body(*refs, n, scatter):
    srcs, lands = refs[:n], refs[n:2 * n]
    send_sems, recv_sems = refs[2 * n], refs[2 * n + 1]
    for send, recv in _split_copies(srcs, lands, send_sems, recv_sems, n, scatter):
        send.wait_send()
        recv.wait_recv()


def exchange_wait(send_sems, recv_sems, srcs, lands, after, *, scatter, name):
    n = len(srcs)
    outs = pl.pallas_call(
        functools.partial(_exchange_wait_body, n=n, scatter=scatter), name=name,
        out_shape=tuple(pltpu.HBM(t.shape, t.dtype) for t in list(srcs) + list(lands)),
        in_specs=[HBM_ONLY] * (2 * n) + [SEM_SPEC, SEM_SPEC, HBM_SPEC],
        out_specs=tuple([HBM_ONLY] * (2 * n)),
        input_output_aliases={i: i for i in range(2 * n)},
        compiler_params=pltpu.CompilerParams(has_side_effects=DATAFLOW_EFFECT),
    )(*srcs, *lands, send_sems, recv_sems, after)
    return outs[n:]


def _own_slot_only(shape_dtype, own, slot):
    land = lax.empty(shape_dtype.shape, shape_dtype.dtype)
    return lax.dynamic_update_slice(land, own[None], (slot,) + (0,) * own.ndim)


def _pad_rows(a, rows=8):
    return jnp.pad(a, ((0, rows - a.shape[0]), (0, 0)))


def _gate_rows(a_log, dt_bias):
    z = jnp.zeros((8, 128), F32)
    return z.at[0, A_HEADS:2 * A_HEADS].set(a_log[0]).at[1, A_HEADS:2 * A_HEADS].set(dt_bias[0])


def _pack_small(norm_g, a_log, a_dt_bias, a_norm_g, b_q_norm_g, b_k_norm_g):
    return jnp.concatenate([
        norm_g[0].reshape(8, 128), norm_g[1].reshape(8, 128),
        _gate_rows(a_log, a_dt_bias),
        _pad_rows(a_norm_g[0].reshape(2, 128)),
        _pad_rows(jnp.concatenate([b_q_norm_g[0], b_k_norm_g[0]], axis=0)),
    ], axis=0)


def _unpack_small(p):
    return (p[0:16].reshape(2, D_MODEL), p[16:17, A_HEADS:2 * A_HEADS], p[17:18, A_HEADS:2 * A_HEADS],
            p[24:26].reshape(1, A_DV), p[32:35][None], p[35:38][None])


def kernel(x, positions, norm_g, a_w_in, a_conv_w, a_log, a_dt_bias, a_norm_g, a_w_out, b_w_in, b_q_norm_g, b_k_norm_g, b_w_out, loss_target, m_norm_g, m_a_w_in, m_a_conv_w, m_a_log, m_a_dt_bias, m_a_norm_g, m_a_w_out, m_b_w_in, m_b_q_norm_g, m_b_k_norm_g, m_b_w_out, v_norm_g, v_a_w_in, v_a_conv_w, v_a_log, v_a_dt_bias, v_a_norm_g, v_a_w_out, v_b_w_in, v_b_q_norm_g, v_b_k_norm_g, v_b_w_out):
    n_seq, s_len, d = x.shape
    t_rows = n_seq * s_len
    n_chunks = s_len // CHUNK
    x0 = x.reshape(t_rows, d)
    target = loss_target.reshape(t_rows, d)
    my_slot = _flat(*_my_place())

    g_a_in, g_conv = all_gather([a_w_in[0].astype(BF16), _pad_rows(a_conv_w[0])], name="gather_weights_first")
    later = [a_w_out[0].astype(BF16), b_w_in[0].astype(BF16), b_w_out[0].astype(BF16)]
    lands = [_own_slot_only(jax.ShapeDtypeStruct((N_DEV,) + t.shape, t.dtype), t, my_slot) for t in later]
    w_send, w_recv, later, lands, w_token = exchange_start(later, lands, scatter=False, name="gather_weights_start")
    w_a_in = jnp.pad(g_a_in.transpose(1, 0, 2).reshape(d, A_IN), ((0, 0), (0, A_IN_PAD - A_IN)))
    conv_w8 = g_conv.transpose(1, 0, 2).reshape(8, 2 * A_QK + A_VW)

    gains_model = _pad_rows(norm_g)
    gate_prm = _gate_rows(a_log, a_dt_bias)
    gain_a_out = _pad_rows(a_norm_g)
    gains_qk = _pad_rows(jnp.concatenate([b_q_norm_g[0], b_k_norm_g[0]], axis=0))
    cos_t, sin_t = rope_tables(positions.reshape(t_rows))

    h0 = rms_fwd(x0, gains_model, 0, after=w_token)
    proj_a = mm_nn(h0, w_a_in, tn=896, name="proj_a")
    qkv = a_pre_fwd(proj_a, conv_w8, n_seq)
    gates_col, gates_row = a_gates_fwd(proj_a, gate_prm, n_seq)
    gates_row = gates_row.reshape(n_seq, 2 * A_HEADS, s_len // SUPER, 1, SUPER)
    o_a, tinv, states, og_a = gdn_fwd(qkv, gates_col, gates_row, proj_a, gain_a_out, n_seq)
    g_a_out, g_b_in, g_b_out = exchange_wait(w_send, w_recv, later, lands, og_a, scatter=False,
                                             name="gather_weights_wait")
    w_a_out = g_a_out.reshape(A_VW, d)
    w_b_in = g_b_in.transpose(1, 0, 2).reshape(d, B_IN_COLS)
    w_b_out = g_b_out.reshape(B_W, d)
    x1 = mm_nn(og_a, w_a_out, x0, tn=512, name="out_a")

    h1 = rms_fwd(x1, gains_model, 1)
    proj_b = mm_nn_pieces(h1, w_b_in, name="proj_b")
    og_b, o_b, lse = attn_fwd(proj_b, cos_t, sin_t, gains_qk, n_seq)
    y = mm_nn(og_b, w_b_out, x1, tn=512, name="out_b")

    dy, loss_blk = loss_head(y, target)
    loss = lax.psum(loss_blk[0, 0], ("x", "y", "c"))

    d_og_b = mm_nt(dy, w_b_out, tn=512, name="d_og_b")
    dw_b_out = mm_tn(og_b, dy, tn=256, out_dtype=BF16, name="dw_b_out")
    dqkv_b, dz_b, d_gains_qk = attn_bwd(proj_b, cos_t, sin_t, gains_qk, d_og_b, o_b, lse, n_seq)
    dh1 = mm_nt_pieces(dqkv_b, w_b_in, name="dh1_qkv")
    dh1 = mm_nt(dz_b, w_b_in, dh1, tn=B_W, col_off=3 * B_GROUPS, name="dh1_z")
    dw_b_in = mm_tn_b_in(h1, dqkv_b, dz_b, out_dtype=BF16, name="dw_b_in")
    dx1, d_gain1 = rms_bwd(x1, gains_model, 1, dh1, dy)

    dw_a_out = mm_tn(og_a, dx1, tn=128, out_dtype=BF16, name="dw_a_out")
    early = [dw_b_in, dw_b_out.reshape(N_DEV, B_W // N_DEV, d), dw_a_out.reshape(N_DEV, A_VW // N_DEV, d)]
    lands = [_own_slot_only(t, lax.dynamic_index_in_dim(t, my_slot, 0, keepdims=False), my_slot) for t in early]
    g_send, g_recv, early, lands, g_token = exchange_start(early, lands, scatter=True, name="scatter_grads_start")

    d_og_a = mm_nt(dx1, w_a_out, tn=512, name="d_og_a", after=g_token)
    dqkv_a, dgates, dz_a, d_gain_a_out = gdn_bwd(qkv, gates_col, gates_row, tinv, states, d_og_a, o_a,
                                                 proj_a, gain_a_out, n_seq)
    d_pre, d_conv = a_pre_bwd(proj_a, conv_w8, dqkv_a, n_seq)
    d_gate_logits, d_gate_prm = a_gates_bwd(proj_a, gate_prm, dgates, n_seq)
    dw_a_in = jnp.concatenate([
        mm_tn(h0, d_pre, tn=256, out_dtype=BF16, name="dw_a_in_qkv"),
        mm_tn(h0, dz_a, tn=256, out_dtype=BF16, name="dw_a_in_z"),
        mm_tn(h0, d_gate_logits, tn=128, out_dtype=BF16, name="dw_a_in_gates"),
    ], axis=1)[:, :A_IN]
    shard_a_in = A_IN // N_DEV
    last = [dw_a_in.reshape(d, N_DEV, shard_a_in).transpose(1, 0, 2)]
    last_lands = [_own_slot_only(t, lax.dynamic_index_in_dim(t, my_slot, 0, keepdims=False), my_slot) for t in last]
    l_send, l_recv, last, last_lands, l_token = exchange_start(last, last_lands, scatter=True,
                                                               name="scatter_last_start")
    dh0 = mm_nt(d_pre, w_a_in, tn=512, name="dh0_qkv", after=l_token)
    dh0 = mm_nt(dz_a, w_a_in, dh0, tn=512, col_off=(2 * A_QK + A_VW) // 512, name="dh0_z")
    dh0 = mm_nt(d_gate_logits, w_a_in, dh0, tn=128, col_off=A_GATE_COL, name="dh0_gates")
    dx0, d_gain0 = rms_bwd(x0, gains_model, 0, dh0, dx1)

    small = jnp.concatenate([
        d_gain0[0].reshape(8, 128), d_gain1[0].reshape(8, 128), d_gate_prm,
        _pad_rows(d_gain_a_out[0].reshape(2, 128)), d_gains_qk], axis=0)
    r_small, r_conv = all_gather([small, d_conv], name="gather_small_grads")
    conv_cols = a_conv_w.shape[2]
    r_conv = lax.dynamic_slice(r_conv, (0, 0, my_slot * conv_cols), (N_DEV, 8, conv_cols))

    r_b_in, r_b_out, r_a_out = exchange_wait(g_send, g_recv, early, lands, r_small, scatter=True,
                                             name="scatter_grads_wait")
    (r_a_in,) = exchange_wait(l_send, l_recv, last, last_lands, r_small, scatter=True, name="scatter_last_wait")

    upd = {}
    upd["a_w_in"] = adamw(r_a_in, a_w_in, m_a_w_in, v_a_w_in, name="adamw_a_w_in")
    upd["a_w_out"] = adamw(r_a_out, a_w_out, m_a_w_out, v_a_w_out, name="adamw_a_w_out")
    upd["b_w_in"] = adamw(r_b_in, b_w_in, m_b_w_in, v_b_w_in, name="adamw_b_w_in")
    upd["b_w_out"] = adamw(r_b_out, b_w_out, m_b_w_out, v_b_w_out, name="adamw_b_w_out")
    upd["a_conv_w"] = [t[:, :A_CONV] for t in adamw(
        r_conv, _pad_rows(a_conv_w[0])[None], _pad_rows(m_a_conv_w[0])[None], _pad_rows(v_a_conv_w[0])[None],
        name="adamw_a_conv_w")]
    small_upd = adamw(
        r_small,
        _pack_small(norm_g, a_log, a_dt_bias, a_norm_g, b_q_norm_g, b_k_norm_g)[None],
        _pack_small(m_norm_g, m_a_log, m_a_dt_bias, m_a_norm_g, m_b_q_norm_g, m_b_k_norm_g)[None],
        _pack_small(v_norm_g, v_a_log, v_a_dt_bias, v_a_norm_g, v_b_q_norm_g, v_b_k_norm_g)[None],
        name="adamw_small")
    small_names = ("norm_g", "a_log", "a_dt_bias", "a_norm_g", "b_q_norm_g", "b_k_norm_g")
    unpacked = [_unpack_small(t[0]) for t in small_upd]
    for i, nm in enumerate(small_names):
        upd[nm] = [u[i] for u in unpacked]

    order = ("norm_g", "a_w_in", "a_conv_w", "a_log", "a_dt_bias", "a_norm_g", "a_w_out",
             "b_w_in", "b_q_norm_g", "b_k_norm_g", "b_w_out")
    outs = [loss, dx0.reshape(n_seq, s_len, d)]
    for kind in range(4):
        for nm in order:
            outs.append(upd[nm][kind])
    return tuple(outs)
```

```python
import functools
import math

import jax
import jax.numpy as jnp
from jax import lax
from jax.experimental import pallas as pl
from jax.experimental.pallas import tpu as pltpu

F32 = jnp.float32
BF16 = jnp.bfloat16

D_MODEL = 1024
EPS = 1e-6
N_DEV = 8

A_HEADS = 8
A_DK = 128
A_DV = 256
A_QK = A_HEADS * A_DK
A_VW = A_HEADS * A_DV
A_CONV = 4
CHUNK = 64
A_IN = 2 * A_QK + 2 * A_VW + 2 * A_HEADS
A_IN_PAD = 2 * A_QK + 2 * A_VW + 128
A_GATE_COL = (2 * A_QK + 2 * A_VW) // 128

B_DILATIONS = (1, 4, 16)
B_GROUPS = 3
B_HEADS = 8
B_DH = 128
B_W = B_HEADS * B_DH
B_BLOCK = 128
B_PIECES = 3 * B_GROUPS + 1
ROPE_THETA = 500000.0
ROPE_DIMS = B_DH // 4
ROPE_HALF = ROPE_DIMS // 2

ADAM_LR = 0.001
ADAM_B1 = 0.9
ADAM_B2 = 0.999
ADAM_EPS = 1e-08
ADAM_WD = 0.01
ADAM_STEP = 10

VMEM_LIMIT = 56 * 1024 * 1024


def _params(sem):
    return pltpu.CompilerParams(dimension_semantics=sem, vmem_limit_bytes=VMEM_LIMIT)


def _bf(x):
    return x.astype(BF16)


def _mm(a, b):
    return jnp.dot(_bf(a), _bf(b), preferred_element_type=F32)


def _mm_nt(a, b):
    return lax.dot_general(_bf(a), _bf(b), (((1,), (1,)), ((), ())), preferred_element_type=F32)


def _mm_tn(a, b):
    return lax.dot_general(_bf(a), _bf(b), (((0,), (0,)), ((), ())), preferred_element_type=F32)


def _split(x):
    hi = _bf(x)
    return hi, _bf(x - hi.astype(F32))


def _mm3(a, b):
    ah, al = _split(a)
    bh, bl = _split(b)
    d = functools.partial(jnp.dot, preferred_element_type=F32)
    return d(ah, bh) + (d(ah, bl) + d(al, bh))


def _colsum_as_col(z):
    zh, zl = _split(z)
    ones = jnp.ones((z.shape[0], 128), BF16)
    tn = functools.partial(lax.dot_general, dimension_numbers=(((0,), (0,)), ((), ())),
                           preferred_element_type=F32)
    return (tn(zh, ones) + tn(zl, ones))[:, 0:1]


def _sigmoid(x):
    return 0.5 * jnp.tanh(0.5 * x) + 0.5


INV_BASE = 8
GDN_GROUP = 4
SUPER = GDN_GROUP * CHUNK
GDN_WIDTH = 2

A_K_COL = A_QK // A_DK
A_V_COL = 2 * A_QK // A_DV
A_HEAD_W = 2 * A_DK + A_DV


def _inverse_steps(m, row, col):
    eye = (row == col).astype(F32)
    d = jnp.where(row // INV_BASE == col // INV_BASE, m, 0.0)
    x = eye - d
    p = _mm3(d, d)
    yield
    steps = int(math.log2(INV_BASE)) - 1
    for i in range(steps):
        x = x + _mm3(x, p)
        if i + 1 < steps:
            p = _mm3(p, p)
        yield
    size = INV_BASE
    while size < CHUNK:
        c = jnp.where((row // (2 * size) == col // (2 * size)) & (row // size != col // size), m, 0.0)
        xc = _mm3(x, c)
        yield
        x = x - _mm3(xc, x)
        yield
        size *= 2
    return x


def _drain(gen):
    while True:
        try:
            next(gen)
        except StopIteration as stop:
            return stop.value


def _interleave(*gens):
    live = list(gens)
    while live:
        for g in list(live):
            try:
                next(g)
            except StopIteration:
                live.remove(g)


def _diag_blocks_tall(x):
    return jnp.concatenate([x[i * CHUNK:(i + 1) * CHUNK, i * CHUNK:(i + 1) * CHUNK] for i in range(GDN_GROUP)], axis=0)


def _tall_to_block_diag(t, same):
    return jnp.where(same, jnp.concatenate([t] * GDN_GROUP, axis=1), 0.0)


def _block_sum(same, x):
    xh, xl = _split(jnp.broadcast_to(x, (SUPER, 128)))
    ones = same.astype(BF16)
    d = functools.partial(jnp.dot, preferred_element_type=F32)
    return (d(ones, xh) + d(ones, xl))[:, 0:1]


def _aligned_rows(index, size):
    start = index * size
    return pl.ds(start if isinstance(start, int) else pl.multiple_of(start, size), size)


def _super_rows(i):
    return _aligned_rows(i, SUPER)


def _chunk_rows(n):
    return _aligned_rows(n, CHUNK)


def _gdn_super_steps(q, k, v, gcb, gr, head, tinv_tall=None):
    lane = lax.broadcasted_iota(jnp.int32, (SUPER, 128), 1)
    row = lax.broadcasted_iota(jnp.int32, (SUPER, SUPER), 0)
    col = lax.broadcasted_iota(jnp.int32, (SUPER, SUPER), 1)
    same = row // CHUNK == col // CHUNK
    beta = jnp.sum(jnp.where(lane == head, gcb, 0.0), axis=1, keepdims=True)
    gc = jnp.sum(jnp.where(lane == A_HEADS + head, gcb, 0.0), axis=1, keepdims=True)
    g_last = jnp.sum(jnp.where(col == (row // CHUNK) * CHUNK + (CHUNK - 1), gr, 0.0), axis=1, keepdims=True)
    gamma = jnp.exp(gc)
    decay = jnp.where(same & (row >= col), jnp.exp(jnp.minimum(gc - gr, 0.0)), 0.0)
    kb = k * beta
    m = jnp.where(same & (row > col), _mm_nt(kb, k) * decay, 0.0)
    p = jnp.where(same & (row >= col), _mm_nt(q, k) * decay, 0.0)
    yield
    if tinv_tall is None:
        tinv = yield from _inverse_steps(m, row, col)
    else:
        tinv = _tall_to_block_diag(tinv_tall, same)
    u = _mm(tinv, v * beta)
    w = _mm(tinv, kb * gamma)
    yield
    e_tail = jnp.exp(g_last - gc)
    return dict(beta=beta, gc=gc, g_last=g_last, gamma=gamma, decay=decay, kb=kb, m=m,
                tinv=tinv, u=u, w=w, p=p, e_tail=e_tail, row=row, col=col, lane=lane, same=same)


def _gdn_super_common(q, k, v, gcb, gr, head, tinv_tall=None):
    return _drain(_gdn_super_steps(q, k, v, gcb, gr, head, tinv_tall))


def _store_scan_operands(rows, q, k, t, u_scr, w_scr, p_scr, qg_scr, ke_scr, gl_scr):
    u_scr[rows, :] = t["u"]
    w_scr[rows, :] = _bf(t["w"])
    p_scr[rows, :] = _bf(_diag_blocks_tall(t["p"]))
    qg_scr[rows, :] = _bf(q * t["gamma"])
    ke_scr[rows, :] = _bf(k * t["e_tail"])
    gl_scr[rows, :] = jnp.broadcast_to(jnp.exp(t["g_last"]), (SUPER, 128))


def _gdn_fwd_body(q_ref, k_ref, v_ref, gc_ref, gr_ref, wq_ref, wk_ref, wv_ref, z_ref, gn_ref,
                  o_ref, tinv_ref, st_ref, og_ref, qo_ref, ko_ref, vo_ref, s_scr, *sets):
    head = pl.program_id(1)
    n_super = q_ref.shape[0] // SUPER
    all_sets = [sets[6 * i:6 * i + 6] for i in range(2 * GDN_WIDTH)]
    whole = pl.ds(0, SUPER)

    def conv_silu(x_ref, w_ref, i):
        rows = _super_rows(i)
        x, w = x_ref[rows, :], w_ref[...]
        halo = jnp.zeros((8, x.shape[1]), F32) if i == 0 else x_ref[pl.ds(i * SUPER - 8, 8), :]
        ext = jnp.concatenate([halo, x], axis=0)
        c = x * w[A_CONV - 1:A_CONV, :]
        for j in range(1, A_CONV):
            c = c + pltpu.roll(ext, j, 0)[8:, :] * w[A_CONV - 1 - j:A_CONV - j, :]
        return c * _sigmoid(c)

    def unit(a):
        return a * lax.rsqrt(jnp.sum(a * a, axis=1, keepdims=True) + EPS)

    def prepare_steps(i, dst):
        rows = _super_rows(i)
        q = unit(conv_silu(q_ref, wq_ref, i)) * A_DK ** -0.5
        k = unit(conv_silu(k_ref, wk_ref, i))
        v = conv_silu(v_ref, wv_ref, i)
        qo_ref[rows, :], ko_ref[rows, :], vo_ref[rows, :] = q, k, v
        t = yield from _gdn_super_steps(q, k, v, gc_ref[rows, :], gr_ref[i], head)
        tinv_ref[rows, :] = _diag_blocks_tall(t["tinv"])
        _store_scan_operands(whole, q, k, t, *dst)

    def scan_steps(i, src):
        u_scr, w_scr, p_scr, qg_scr, ke_scr, gl_scr = src
        for j in range(GDN_GROUP):
            n = i * GDN_GROUP + j
            local = pl.ds(j * CHUNK, CHUNK)
            s = s_scr[...]
            st_ref[n] = s
            sb = _bf(s)
            ws = jnp.dot(w_scr[local, :], sb, preferred_element_type=F32)
            yield
            vb = _bf(u_scr[local, :] - ws)
            o = (jnp.dot(qg_scr[local, :], sb, preferred_element_type=F32)
                 + jnp.dot(p_scr[local, :], vb, preferred_element_type=F32))
            s_new = s * gl_scr[local, :][0:1, 0:1] + lax.dot_general(
                ke_scr[local, :], vb, (((0,), (0,)), ((), ())), preferred_element_type=F32)
            yield
            rows = _chunk_rows(n)
            o_ref[rows, :] = o
            s_scr[...] = s_new
            silu, _ = _silu_parts(z_ref[rows, :])
            r = lax.rsqrt(jnp.mean(o * o, axis=1, keepdims=True) + EPS)
            og_ref[rows, :] = ((o * r * gn_ref[0:1, :]) * silu).astype(og_ref.dtype)

    def scan_many(first, srcs):
        for j, src in enumerate(srcs):
            yield from scan_steps(first + j, src)

    groups = [all_sets[:GDN_WIDTH], all_sets[GDN_WIDTH:]]
    _interleave(*[prepare_steps(j, groups[0][j]) for j in range(GDN_WIDTH)])
    s_scr[...] = jnp.zeros_like(s_scr)
    for g in range(n_super // GDN_WIDTH):
        cur, nxt = groups[g % 2], groups[(g + 1) % 2]
        first = g * GDN_WIDTH
        following = [prepare_steps(first + GDN_WIDTH + j, nxt[j]) for j in range(GDN_WIDTH)
                     if first + GDN_WIDTH + j < n_super]
        _interleave(scan_many(first, cur), *following)


def _gdn_in_specs(s_len, n_super, from_proj):
    k_col, v_col = (A_K_COL, A_V_COL) if from_proj else (0, 0)
    return [
        pl.BlockSpec((s_len, A_DK), lambda b, h: (b, h)),
        pl.BlockSpec((s_len, A_DK), lambda b, h: (b, k_col + h)),
        pl.BlockSpec((s_len, A_DV), lambda b, h: (b, v_col + h)),
        pl.BlockSpec((s_len, 128), lambda b, h: (b, 0)),
        pl.BlockSpec((None, None, n_super, 1, SUPER), lambda b, h: (b, A_HEADS + h, 0, 0, 0)),
    ]


def _gdn_scan_scratch(s_len):
    return [pltpu.VMEM((A_DK, A_DV), F32), pltpu.VMEM((s_len, A_DV), F32),
            pltpu.VMEM((s_len, A_DK), BF16), pltpu.VMEM((s_len, CHUNK), BF16),
            pltpu.VMEM((s_len, A_DK), BF16), pltpu.VMEM((s_len, A_DK), BF16),
            pltpu.VMEM((s_len, 128), F32)]


def gdn_fwd(proj_a, conv_w8, gates_col, gates_row, norm_g8, n_seq):
    t_rows = proj_a.shape[0]
    s_len = t_rows // n_seq
    n_chunks = s_len // CHUNK
    qk_spec = pl.BlockSpec((s_len, A_DK), lambda b, h: (b, h))
    v_spec = pl.BlockSpec((s_len, A_DV), lambda b, h: (b, h))
    return pl.pallas_call(
        _gdn_fwd_body, name="gdn_fwd", grid=(n_seq, A_HEADS),
        in_specs=_gdn_in_specs(s_len, s_len // SUPER, True) + [
            pl.BlockSpec((8, A_DK), lambda b, h: (0, h)),
            pl.BlockSpec((8, A_DK), lambda b, h: (0, A_K_COL + h)),
            pl.BlockSpec((8, A_DV), lambda b, h: (0, A_V_COL + h)),
            pl.BlockSpec((s_len, A_DV), lambda b, h: (b, A_Z_COL + h)),
            pl.BlockSpec((8, A_DV), lambda b, h: (0, 0)),
        ],
        out_specs=[
            v_spec,
            pl.BlockSpec((s_len, CHUNK), lambda b, h: (b * A_HEADS + h, 0)),
            pl.BlockSpec((None, n_chunks, A_DK, A_DV), lambda b, h: (b * A_HEADS + h, 0, 0, 0)),
            v_spec, qk_spec, qk_spec, v_spec,
        ],
        out_shape=[
            jax.ShapeDtypeStruct((t_rows, A_VW), F32),
            jax.ShapeDtypeStruct((n_seq * A_HEADS * s_len, CHUNK), F32),
            jax.ShapeDtypeStruct((n_seq * A_HEADS, n_chunks, A_DK, A_DV), F32),
            jax.ShapeDtypeStruct((t_rows, A_VW), BF16),
            jax.ShapeDtypeStruct((t_rows, A_QK), F32),
            jax.ShapeDtypeStruct((t_rows, A_QK), F32),
            jax.ShapeDtypeStruct((t_rows, A_VW), F32),
        ],
        scratch_shapes=_gdn_scan_scratch(SUPER) + (2 * GDN_WIDTH - 1) * _gdn_scan_scratch(SUPER)[1:],
        compiler_params=_params(("arbitrary", "arbitrary")),
    )(proj_a, proj_a, proj_a, gates_col, gates_row, conv_w8, conv_w8, conv_w8, proj_a, norm_g8)


def _gdn_bwd_body(q_ref, k_ref, v_ref, gc_ref, gr_ref, tinv_ref, st_ref, dog_ref, oa_ref, z_ref, gn_ref,
                  dqkv_ref, dgc_ref, dz_ref, dgn_ref, ds_scr, *sets):
    head = pl.program_id(1)
    n_super = q_ref.shape[0] // SUPER
    ops = (sets[0:7], sets[7:14])
    res = (sets[14:21], sets[21:28])
    whole = pl.ds(0, SUPER)
    tn = functools.partial(lax.dot_general, dimension_numbers=(((0,), (0,)), ((), ())), preferred_element_type=F32)
    nt = functools.partial(lax.dot_general, dimension_numbers=(((1,), (1,)), ((), ())), preferred_element_type=F32)

    @pl.when(head == 0)
    def _():
        dgc_ref[...] = jnp.zeros_like(dgc_ref)

    @pl.when((head == 0) & (pl.program_id(0) == 0))
    def _():
        dgn_ref[...] = jnp.zeros_like(dgn_ref)

    def common_steps(i):
        rows = _super_rows(i)
        q, k, v = q_ref[rows, :], k_ref[rows, :], v_ref[rows, :]
        t = yield from _gdn_super_steps(q, k, v, gc_ref[rows, :], gr_ref[i], head, tinv_tall=tinv_ref[rows, :])
        return rows, q, k, v, t

    def stage_p(i, parity):
        rows, q, k, _, t = yield from common_steps(i)
        _store_scan_operands(whole, q, k, t, *ops[parity][:6])
        o, d_og, gain = oa_ref[rows, :], dog_ref[rows, :], gn_ref[0:1, :]
        r = lax.rsqrt(jnp.mean(o * o, axis=1, keepdims=True) + EPS)
        silu, dsilu = _silu_parts(z_ref[rows, :])
        xr = o * r
        d_on = d_og * silu
        dz_ref[rows, :] = (d_og * (xr * gain) * dsilu).astype(dz_ref.dtype)
        u = d_on * gain
        ops[parity][6][...] = r * u - xr * (r * r) * jnp.mean(u * o, axis=1, keepdims=True)
        dgn_ref[0:1, :] += jnp.sum(d_on * xr, axis=0, keepdims=True)

    def stage_s(i, parity):
        u_scr, w_scr, p_scr, qg_scr, ke_scr, gl_scr, do_scr = ops[parity]
        vn_scr, dvn_scr, dqg_scr, dw_scr, dkt_scr, sds_scr, dof_scr = res[parity]
        for j in reversed(range(GDN_GROUP)):
            n = i * GDN_GROUP + j
            local = pl.ds(j * CHUNK, CHUNK)
            ds_next = ds_scr[...]
            dsb = _bf(ds_next)
            s = st_ref[n]
            sb = _bf(s)
            d_o = do_scr[local, :]
            dof_scr[local, :] = d_o
            d_ob = _bf(d_o)
            w_s = jnp.dot(w_scr[local, :], sb, preferred_element_type=F32)
            d_vn = tn(p_scr[local, :], d_ob) + jnp.dot(ke_scr[local, :], dsb, preferred_element_type=F32)
            d_qg = nt(d_ob, sb)
            qg_do = tn(qg_scr[local, :], d_ob)
            yield
            v_new = u_scr[local, :] - w_s
            d_vnb = _bf(d_vn)
            d_w = -nt(d_vnb, sb)
            d_kt = nt(_bf(v_new), dsb)
            w_dvn = tn(w_scr[local, :], d_vnb)
            yield
            vn_scr[local, :] = v_new
            dvn_scr[local, :] = d_vn
            dqg_scr[local, :] = d_qg
            dw_scr[local, :] = d_w
            dkt_scr[local, :] = d_kt
            sds = jnp.sum(jnp.sum(s * ds_next, axis=1, keepdims=True), axis=0, keepdims=True)
            sds_scr[local, :] = jnp.broadcast_to(sds, (CHUNK, 128))
            ds_scr[...] = qg_do + gl_scr[local, :][0:1, 0:1] * ds_next - w_dvn

    def stage_f(i, parity):
        vn_scr, dvn_scr, dqg_scr, dw_scr, dkt_scr, sds_scr, dof_scr = res[parity]
        rows, q, k, v, t = yield from common_steps(i)
        beta, gamma, decay, kb, e_tail = t["beta"], t["gamma"], t["decay"], t["kb"], t["e_tail"]
        row, col, lane, same = t["row"], t["col"], t["lane"], t["same"]
        d_o = dof_scr[...]
        v_new, d_vn = vn_scr[...], dvn_scr[...]
        d_qg, d_w, d_kt = dqg_scr[...], dw_scr[...], dkt_scr[...]
        gamma_last = jnp.exp(t["g_last"])

        d_p = jnp.where(same & (row >= col), _mm_nt(d_o, v_new), 0.0)
        d_ru = _mm_tn(t["tinv"], d_vn)
        d_rw = _mm_tn(t["tinv"], d_w)
        yield
        d_m = jnp.where(same & (row > col), -(_mm_nt(d_ru, t["u"]) + _mm_nt(d_rw, t["w"])), 0.0)
        yield

        x_p = d_p * decay
        y_m = d_m * decay
        d_kb = _mm(y_m, k) + d_rw * gamma
        dqkv_ref[rows, 0:A_DK] = _mm(x_p, k) + d_qg * gamma
        dqkv_ref[rows, A_DK:2 * A_DK] = _mm_tn(x_p, q) + _mm_tn(y_m, kb) + d_kb * beta + d_kt * e_tail
        dqkv_ref[rows, 2 * A_DK:A_HEAD_W] = d_ru * beta

        d_beta = (jnp.sum(d_ru * v, axis=1, keepdims=True)
                  + jnp.sum(d_kb * k, axis=1, keepdims=True))
        z = d_p * t["p"] + d_m * t["m"]
        eps_tail = jnp.sum(d_kt * k, axis=1, keepdims=True) * e_tail
        d_gc = (jnp.sum(z, axis=1, keepdims=True) - _colsum_as_col(z)
                + jnp.sum(d_qg * q, axis=1, keepdims=True) * gamma
                + jnp.sum(d_rw * kb, axis=1, keepdims=True) * gamma
                - eps_tail)
        d_glast = _block_sum(same, eps_tail) + gamma_last * sds_scr[...][:, 0:1]
        yield
        rcol = lax.broadcasted_iota(jnp.int32, (SUPER, 1), 0)
        d_gc = d_gc + jnp.where(rcol % CHUNK == CHUNK - 1, d_glast, 0.0)
        dgc_ref[rows, :] += (jnp.where(lane == head, d_beta, 0.0)
                             + jnp.where(lane == A_HEADS + head, d_gc, 0.0))

    last = n_super - 1
    _drain(stage_p(last, 1))
    ds_scr[...] = jnp.zeros_like(ds_scr)
    _interleave(stage_s(last, 1), stage_p(last - 1, 0))

    def pair(k, carry):
        i = last - 1 - 2 * k
        _interleave(stage_s(i, 0), stage_f(i + 1, 1), stage_p(i - 1, 1))
        _interleave(stage_s(i - 1, 1), stage_f(i, 0), stage_p(i - 2, 0))
        return carry

    lax.fori_loop(0, n_super // 2 - 1, pair, 0)
    _interleave(stage_s(0, 0), stage_f(1, 1))
    _drain(stage_f(0, 0))


def gdn_bwd(q, k, v, gates_col, gates_row, tinv, states, d_og, o, proj_a, norm_g8, n_seq):
    t_rows = q.shape[0]
    s_len = t_rows // n_seq
    n_chunks = s_len // CHUNK
    v_spec = pl.BlockSpec((s_len, A_DV), lambda b, h: (b, h))
    gate_spec = pl.BlockSpec((s_len, 128), lambda b, h: (b, 0))
    gain_spec = pl.BlockSpec((8, A_DV), lambda b, h: (0, 0))
    ops_set = _gdn_scan_scratch(SUPER)[1:] + [pltpu.VMEM((SUPER, A_DV), F32)]
    res_set = [pltpu.VMEM((SUPER, A_DV), F32), pltpu.VMEM((SUPER, A_DV), F32),
               pltpu.VMEM((SUPER, A_DK), F32), pltpu.VMEM((SUPER, A_DK), F32),
               pltpu.VMEM((SUPER, A_DK), F32), pltpu.VMEM((SUPER, 128), F32), pltpu.VMEM((SUPER, A_DV), F32)]
    return pl.pallas_call(
        _gdn_bwd_body, name="gdn_bwd", grid=(n_seq, A_HEADS),
        in_specs=_gdn_in_specs(s_len, s_len // SUPER, False) + [
            pl.BlockSpec((s_len, CHUNK), lambda b, h: (b * A_HEADS + h, 0)),
            pl.BlockSpec((None, n_chunks, A_DK, A_DV), lambda b, h: (b * A_HEADS + h, 0, 0, 0)),
            v_spec, v_spec,
            pl.BlockSpec((s_len, A_DV), lambda b, h: (b, A_Z_COL + h)),
            gain_spec,
        ],
        out_specs=[pl.BlockSpec((s_len, A_HEAD_W), lambda b, h: (b, h)), gate_spec, v_spec, gain_spec],
        out_shape=[
            jax.ShapeDtypeStruct((t_rows, A_HEADS * A_HEAD_W), F32),
            jax.ShapeDtypeStruct((t_rows, 128), F32),
            jax.ShapeDtypeStruct((t_rows, A_VW), BF16),
            jax.ShapeDtypeStruct((8, A_DV), F32),
        ],
        scratch_shapes=_gdn_scan_scratch(SUPER)[:1] + 2 * ops_set + 2 * res_set,
        compiler_params=_params(("arbitrary", "arbitrary")),
    )(q, k, v, gates_col, gates_row, tinv, states, d_og, o, proj_a, norm_g8)


def _shift_down(x, j, rows):
    r = pltpu.roll(x, j, 0)
    return jnp.concatenate([jnp.where(rows[:8] >= j, r[:8], 0.0), r[8:]], axis=0)


def _shift_up(x, j, rows):
    n = x.shape[0]
    r = pltpu.roll(x, n - j, 0)
    return jnp.concatenate([r[:n - 8], jnp.where(rows[n - 8:] < n - j, r[n - 8:], 0.0)], axis=0)


def _conv_silu_norm(x, w, cb):
    rows = lax.broadcasted_iota(jnp.int32, (x.shape[0], 1), 0)
    c = x * w[A_CONV - 1:A_CONV, :]
    for j in range(1, A_CONV):
        c = c + _shift_down(x, j, rows) * w[A_CONV - 1 - j:A_CONV - j, :]
    sig = _sigmoid(c)
    a = c * sig
    rn = lax.rsqrt(jnp.sum(a * a, axis=1, keepdims=True) + EPS)
    is_q = cb < A_QK // 128
    is_qk = cb < 2 * A_QK // 128
    q_scale = jnp.where(is_q, A_DK ** -0.5, 1.0).astype(F32)
    return rows, c, sig, a, rn, is_qk, q_scale


def _a_pre_fwd_body(x_ref, w_ref, o_ref):
    cb = pl.program_id(1)
    _, _, _, a, rn, is_qk, q_scale = _conv_silu_norm(x_ref[...], w_ref[...], cb)
    o_ref[...] = a * jnp.where(is_qk, rn * q_scale, 1.0)


def a_pre_fwd(proj_a, conv_w8, n_seq):
    t_rows = proj_a.shape[0]
    s_len = t_rows // n_seq
    n_cb = (2 * A_QK + A_VW) // 128
    blk = pl.BlockSpec((s_len, 128), lambda b, c: (b, c))
    return pl.pallas_call(
        _a_pre_fwd_body, name="a_pre_fwd", grid=(n_seq, n_cb),
        in_specs=[blk, pl.BlockSpec((8, 128), lambda b, c: (0, c))],
        out_specs=blk,
        out_shape=jax.ShapeDtypeStruct((t_rows, n_cb * 128), F32),
        compiler_params=_params(("arbitrary", "arbitrary")),
    )(proj_a, conv_w8)


def _a_pre_bwd_body(x_ref, w_ref, dy_ref, dx_ref, dw_ref):
    b, cb = pl.program_id(1), pl.program_id(0)
    x, w = x_ref[...], w_ref[...]
    rows, c, sig, a, rn, is_qk, q_scale = _conv_silu_norm(x, w, cb)
    dy = dy_ref[...]
    da_n = q_scale * (rn * dy - a * (rn * rn * rn) * jnp.sum(dy * a, axis=1, keepdims=True))
    da = jnp.where(is_qk, da_n, dy)
    dc = da * (sig * (1.0 + c * (1.0 - sig)))
    @pl.when(b == 0)
    def _():
        dw_ref[...] = jnp.zeros_like(dw_ref)

    dx = dc * w[A_CONV - 1:A_CONV, :]
    dw_ref[A_CONV - 1:A_CONV, :] += jnp.sum(dc * x, axis=0, keepdims=True)
    for j in range(1, A_CONV):
        dcs = _shift_up(dc, j, rows)
        dx = dx + dcs * w[A_CONV - 1 - j:A_CONV - j, :]
        dw_ref[A_CONV - 1 - j:A_CONV - j, :] += jnp.sum(dcs * x, axis=0, keepdims=True)
    dx_ref[...] = dx.astype(dx_ref.dtype)


def a_pre_bwd(proj_a, conv_w8, dqkv_hm, n_seq):
    t_rows = proj_a.shape[0]
    s_len = t_rows // n_seq
    n_cb = (2 * A_QK + A_VW) // 128
    blk = pl.BlockSpec((s_len, 128), lambda c, b: (b, c))
    wblk = pl.BlockSpec((8, 128), lambda c, b: (0, c))
    per_head = A_HEAD_W // 128
    n_q = A_QK // 128

    def head_major(c, b):
        v_blk = jnp.maximum(c - 2 * n_q, 0)
        col = jnp.where(c < n_q, c * per_head,
                        jnp.where(c < 2 * n_q, (c - n_q) * per_head + 1,
                                  (v_blk // 2) * per_head + 2 + v_blk % 2))
        return (b, col)

    return pl.pallas_call(
        _a_pre_bwd_body, name="a_pre_bwd", grid=(n_cb, n_seq),
        in_specs=[blk, wblk, pl.BlockSpec((s_len, 128), head_major)],
        out_specs=[blk, wblk],
        out_shape=[jax.ShapeDtypeStruct((t_rows, n_cb * 128), BF16),
                   jax.ShapeDtypeStruct((8, n_cb * 128), F32)],
        compiler_params=_params(("arbitrary", "arbitrary")),
    )(proj_a, conv_w8, dqkv_hm)


GATE_TILE = 512


def _softplus(y):
    return jnp.maximum(y, 0.0) + jnp.log1p(jnp.exp(-jnp.abs(y)))


def _gate_values(x, prm):
    beta = _sigmoid(x)
    y = x + prm[1:2, :]
    neg_a = -jnp.exp(prm[0:1, :])
    g = neg_a * _softplus(y)
    return beta, y, neg_a, g


def _a_gates_fwd_body(x_ref, prm_ref, gc_ref, gr_ref):
    x = x_ref[...]
    tm = x.shape[0]
    beta, _, _, g = _gate_values(x, prm_ref[...])
    in_chunk = lax.broadcasted_iota(jnp.int32, (tm, 1), 0) % CHUNK
    s = 1
    while s < CHUNK:
        g = g + jnp.where(in_chunk >= s, pltpu.roll(g, s, 0), 0.0)
        s *= 2
    lane = lax.broadcasted_iota(jnp.int32, x.shape, 1)
    out = jnp.where(lane < A_HEADS, beta, jnp.where(lane < 2 * A_HEADS, g, 0.0))
    gc_ref[...] = out
    gr_ref[...] = out.T[0:2 * A_HEADS, :]


def a_gates_fwd(proj_a, prm, n_seq):
    t_rows = proj_a.shape[0]
    s_len = t_rows // n_seq
    tm = min(GATE_TILE, s_len)
    n_t = s_len // tm
    return pl.pallas_call(
        _a_gates_fwd_body, name="a_gates_fwd", grid=(n_seq, n_t),
        in_specs=[pl.BlockSpec((tm, 128), lambda b, i: (b * n_t + i, A_GATE_COL)),
                  pl.BlockSpec((8, 128), lambda b, i: (0, 0))],
        out_specs=[pl.BlockSpec((tm, 128), lambda b, i: (b * n_t + i, 0)),
                   pl.BlockSpec((None, 2 * A_HEADS, tm), lambda b, i: (b, 0, i))],
        out_shape=[jax.ShapeDtypeStruct((t_rows, 128), F32),
                   jax.ShapeDtypeStruct((n_seq, 2 * A_HEADS, s_len), F32)],
        compiler_params=_params(("arbitrary", "arbitrary")),
    )(proj_a, prm)


def _a_gates_bwd_body(x_ref, prm_ref, dgc_ref, dx_ref, dprm_ref):
    first = (pl.program_id(0) == 0) & (pl.program_id(1) == 0)
    x = x_ref[...]
    tm = x.shape[0]
    beta, y, neg_a, g = _gate_values(x, prm_ref[...])
    d = dgc_ref[...]
    in_chunk = lax.broadcasted_iota(jnp.int32, (tm, 1), 0) % CHUNK
    dg = d
    s = 1
    while s < CHUNK:
        dg = dg + jnp.where(in_chunk < CHUNK - s, pltpu.roll(dg, tm - s, 0), 0.0)
        s *= 2
    lane = lax.broadcasted_iota(jnp.int32, x.shape, 1)
    is_decay = (lane >= A_HEADS) & (lane < 2 * A_HEADS)
    d_alogit = jnp.where(is_decay, dg * neg_a * _sigmoid(y), 0.0)
    dx_ref[...] = jnp.where(lane < A_HEADS, d * beta * (1.0 - beta), d_alogit).astype(dx_ref.dtype)

    @pl.when(first)
    def _():
        dprm_ref[...] = jnp.zeros_like(dprm_ref)

    dprm_ref[0:1, :] += jnp.sum(jnp.where(is_decay, dg * g, 0.0), axis=0, keepdims=True)
    dprm_ref[1:2, :] += jnp.sum(d_alogit, axis=0, keepdims=True)


def a_gates_bwd(proj_a, prm, dgates_col, n_seq):
    t_rows = proj_a.shape[0]
    s_len = t_rows // n_seq
    tm = min(GATE_TILE, s_len)
    n_t = s_len // tm
    return pl.pallas_call(
        _a_gates_bwd_body, name="a_gates_bwd", grid=(n_seq, n_t),
        in_specs=[pl.BlockSpec((tm, 128), lambda b, i: (b * n_t + i, A_GATE_COL)),
                  pl.BlockSpec((8, 128), lambda b, i: (0, 0)),
                  pl.BlockSpec((tm, 128), lambda b, i: (b * n_t + i, 0))],
        out_specs=[pl.BlockSpec((tm, 128), lambda b, i: (b * n_t + i, 0)),
                   pl.BlockSpec((8, 128), lambda b, i: (0, 0))],
        out_shape=[jax.ShapeDtypeStruct((t_rows, 128), BF16),
                   jax.ShapeDtypeStruct((8, 128), F32)],
        compiler_params=_params(("arbitrary", "arbitrary")),
    )(proj_a, prm, dgates_col)


ROW_TILE = 512
A_Z_COL = (2 * A_QK + A_VW) // A_DV


def _silu_parts(z):
    sig = _sigmoid(z)
    return z * sig, sig * (1.0 + z * (1.0 - sig))


def _a_post_fwd_body(o_ref, z_ref, g_ref, og_ref):
    o = o_ref[...]
    r = lax.rsqrt(jnp.mean(o * o, axis=1, keepdims=True) + EPS)
    silu, _ = _silu_parts(z_ref[...])
    og_ref[...] = ((o * r * g_ref[0:1, :]) * silu).astype(og_ref.dtype)


def a_post_fwd(o, proj_a, norm_g8):
    t_rows = o.shape[0]
    tm = min(ROW_TILE, t_rows)
    blk = pl.BlockSpec((tm, A_DV), lambda i, h: (i, h))
    return pl.pallas_call(
        _a_post_fwd_body, name="a_post_fwd", grid=(t_rows // tm, A_HEADS),
        in_specs=[blk, pl.BlockSpec((tm, A_DV), lambda i, h: (i, A_Z_COL + h)),
                  pl.BlockSpec((8, A_DV), lambda i, h: (0, 0))],
        out_specs=blk,
        out_shape=jax.ShapeDtypeStruct((t_rows, A_VW), BF16),
        compiler_params=_params(("arbitrary", "arbitrary")),
    )(o, proj_a, norm_g8)


def _a_post_bwd_body(o_ref, z_ref, g_ref, dog_ref, do_ref, dz_ref, dg_ref):
    first = (pl.program_id(0) == 0) & (pl.program_id(1) == 0)
    o, z, d_og = o_ref[...], z_ref[...], dog_ref[...]
    gain = g_ref[0:1, :]
    r = lax.rsqrt(jnp.mean(o * o, axis=1, keepdims=True) + EPS)
    silu, dsilu = _silu_parts(z)
    xr = o * r
    d_on = d_og * silu
    dz_ref[...] = (d_og * (xr * gain) * dsilu).astype(dz_ref.dtype)
    u = d_on * gain
    do_ref[...] = r * u - xr * (r * r) * jnp.mean(u * o, axis=1, keepdims=True)

    @pl.when(first)
    def _():
        dg_ref[...] = jnp.zeros_like(dg_ref)

    dg_ref[0:1, :] += jnp.sum(d_on * xr, axis=0, keepdims=True)


def a_post_bwd(o, proj_a, norm_g8, d_og):
    t_rows = o.shape[0]
    tm = min(ROW_TILE, t_rows)
    blk = pl.BlockSpec((tm, A_DV), lambda i, h: (i, h))
    gblk = pl.BlockSpec((8, A_DV), lambda i, h: (0, 0))
    return pl.pallas_call(
        _a_post_bwd_body, name="a_post_bwd", grid=(t_rows // tm, A_HEADS),
        in_specs=[blk, pl.BlockSpec((tm, A_DV), lambda i, h: (i, A_Z_COL + h)), gblk, blk],
        out_specs=[blk, blk, gblk],
        out_shape=[jax.ShapeDtypeStruct((t_rows, A_VW), F32),
                   jax.ShapeDtypeStruct((t_rows, A_VW), BF16),
                   jax.ShapeDtypeStruct((8, A_DV), F32)],
        compiler_params=_params(("arbitrary", "arbitrary")),
    )(o, proj_a, norm_g8, d_og)


NEG_BIG = -1e30
ATT_SCALE = B_DH ** -0.5


def _swap_rope_halves(x):
    src = lax.broadcasted_iota(jnp.int32, (B_DH, B_DH), 0)
    dst = lax.broadcasted_iota(jnp.int32, (B_DH, B_DH), 1)
    pick = ((dst < ROPE_HALF) & (src == dst + ROPE_HALF)) | (
        (dst >= ROPE_HALF) & (dst < ROPE_DIMS) & (src == dst - ROPE_HALF))
    return jnp.dot(_bf(x), pick.astype(BF16), preferred_element_type=F32)


def _norm_rope(x, gain, cos_t, sin_t):
    r = lax.rsqrt(jnp.mean(x * x, axis=1, keepdims=True) + EPS)
    xn = x * r * gain
    return xn * cos_t + _swap_rope_halves(xn) * sin_t, r


def _norm_rope_bwd(x, r, gain, cos_t, sin_t, dy):
    d_xn = dy * cos_t + _swap_rope_halves(dy * sin_t)
    xr = x * r
    u = d_xn * gain
    dx = r * u - xr * (r * r) * jnp.mean(u * x, axis=1, keepdims=True)
    return dx, jnp.sum(d_xn * xr, axis=0, keepdims=True)


def _stream_rows(idx, dilation, s_len):
    nb = s_len // dilation // B_BLOCK
    r = idx // nb
    m = idx % nb
    cur = r + m * (B_BLOCK * dilation)
    prev = r + jnp.maximum(m - 1, 0) * (B_BLOCK * dilation)
    return cur, prev, m > 0


def _rows(start, dilation):
    if dilation == 1:
        return pl.ds(start, B_BLOCK)
    return pl.ds(start, B_BLOCK, stride=dilation)


ATT_UNROLL = 8


def _band_mask(has_prev):
    qi = lax.broadcasted_iota(jnp.int32, (B_BLOCK, 2 * B_BLOCK), 0)
    kj = lax.broadcasted_iota(jnp.int32, (B_BLOCK, 2 * B_BLOCK), 1)
    return ((kj < B_BLOCK) & (kj >= qi) & has_prev) | ((kj >= B_BLOCK) & (kj - B_BLOCK <= qi))


def _block_scores(qb, kc, kp, has_prev):
    qi = lax.broadcasted_iota(jnp.int32, (B_BLOCK, B_BLOCK), 0)
    kj = lax.broadcasted_iota(jnp.int32, (B_BLOCK, B_BLOCK), 1)
    s_c = jnp.where(qi >= kj, _mm_nt(qb, kc) * ATT_SCALE, NEG_BIG)
    s_p = jnp.where((kj >= qi) & has_prev, _mm_nt(qb, kp) * ATT_SCALE, NEG_BIG)
    return s_c, s_p


def _attn_fwd_body(qkv_ref, z_ref, cos_ref, sin_ref, gain_ref, og_ref, o_ref, lse_ref,
                   qn_scr, kn_scr, og_scr, lg_scr):
    head, grp = pl.program_id(1), pl.program_id(2)
    s_len = z_ref.shape[0]
    n_blocks = s_len // B_BLOCK
    cos_t, sin_t = cos_ref[...], sin_ref[...]

    for gi, dil in enumerate(B_DILATIONS):
        @pl.when(grp == gi)
        def _(gi=gi, dil=dil):
            qn_scr[...], _ = _norm_rope(qkv_ref[0], gain_ref[gi:gi + 1, :], cos_t, sin_t)
            kn_scr[...], _ = _norm_rope(qkv_ref[1], gain_ref[B_GROUPS + gi:B_GROUPS + gi + 1, :], cos_t, sin_t)

            ones = jnp.ones((2 * B_BLOCK, B_DH), BF16)

            def blocks(it, carry):
                scored = []
                for j in range(ATT_UNROLL):
                    cur, prev, has_prev = _stream_rows(it * ATT_UNROLL + j, dil, s_len)
                    rc, rp = _rows(cur, dil), _rows(prev, dil)
                    k2 = jnp.concatenate([kn_scr[rp, :], kn_scr[rc, :]], axis=0)
                    scored.append((rc, rp, has_prev, _mm_nt(qn_scr[rc, :], k2) * ATT_SCALE))
                summed = []
                for rc, rp, has_prev, s in scored:
                    s = jnp.where(_band_mask(has_prev), s, NEG_BIG)
                    mx = jnp.max(s, axis=1, keepdims=True)
                    v2 = jnp.concatenate([qkv_ref.at[2][rp, :], qkv_ref.at[2][rc, :]], axis=0)
                    acc = jnp.dot(_bf(jnp.exp(s - mx)), jnp.concatenate([_bf(v2), ones], axis=1),
                                  preferred_element_type=F32)
                    summed.append((rc, mx, acc))
                for rc, mx, acc in summed:
                    den = acc[:, B_DH:B_DH + 1]
                    og_scr.at[gi][rc, :] = acc[:, :B_DH] / den
                    lg_scr.at[gi][rc, :] = jnp.broadcast_to(mx + jnp.log(den), (B_BLOCK, B_DH))
                return carry

            lax.fori_loop(0, n_blocks // ATT_UNROLL, blocks, 0)

    @pl.when(grp == B_GROUPS - 1)
    def _():
        l0, l1, l2 = lg_scr[0], lg_scr[1], lg_scr[2]
        mx = jnp.maximum(jnp.maximum(l0, l1), l2)
        w0, w1, w2 = jnp.exp(l0 - mx), jnp.exp(l1 - mx), jnp.exp(l2 - mx)
        den = w0 + w1 + w2
        o = (w0 * og_scr[0] + w1 * og_scr[1] + w2 * og_scr[2]) / den
        silu, _ = _silu_parts(z_ref[...])
        o_ref[...] = o
        og_ref[...] = (o * silu).astype(og_ref.dtype)
        @pl.when(head == 0)
        def _():
            lse_ref[...] = jnp.zeros_like(lse_ref)

        lane = lax.broadcasted_iota(jnp.int32, o.shape, 1)
        lse_ref[...] = jnp.where(lane == head, mx + jnp.log(den), lse_ref[...])


def attn_fwd(proj_b, cos_t, sin_t, gains8, n_seq):
    t_rows = proj_b.shape[1]
    s_len = t_rows // n_seq
    head_blk = pl.BlockSpec((s_len, B_DH), lambda b, h, g: (b, h))
    seq_blk = pl.BlockSpec((s_len, 128), lambda b, h, g: (b, 0))
    return pl.pallas_call(
        _attn_fwd_body, name="attn_fwd", grid=(n_seq, B_HEADS, B_GROUPS),
        in_specs=[
            pl.BlockSpec((3, s_len, B_DH), lambda b, h, g: (g, b, h)),
            pl.BlockSpec((None, s_len, B_DH), lambda b, h, g: (B_PIECES - 1, b, h)),
            seq_blk, seq_blk,
            pl.BlockSpec((8, 128), lambda b, h, g: (0, 0)),
        ],
        out_specs=[head_blk, head_blk, seq_blk],
        out_shape=[jax.ShapeDtypeStruct((t_rows, B_W), BF16),
                   jax.ShapeDtypeStruct((t_rows, B_W), F32),
                   jax.ShapeDtypeStruct((t_rows, 128), F32)],
        scratch_shapes=[pltpu.VMEM((s_len, B_DH), F32), pltpu.VMEM((s_len, B_DH), F32),
                        pltpu.VMEM((B_GROUPS, s_len, B_DH), F32), pltpu.VMEM((B_GROUPS, s_len, B_DH), F32)],
        compiler_params=_params(("arbitrary", "arbitrary", "arbitrary")),
    )(proj_b, proj_b, cos_t, sin_t, gains8)


def _attn_bwd_body(qkv_ref, z_ref, cos_ref, sin_ref, gain_ref, dog_ref, o_ref, lse_ref,
                   dqkv_ref, dz_ref, dgain_ref,
                   qn_scr, kn_scr, dqn_scr, dkn_scr, do_scr, dl_scr, ls_scr, dv_scr):
    head, grp = pl.program_id(1), pl.program_id(2)
    first = (pl.program_id(0) == 0) & (head == 0) & (grp == 0)
    s_len = z_ref.shape[0]
    n_blocks = s_len // B_BLOCK
    cos_t, sin_t = cos_ref[...], sin_ref[...]

    @pl.when(first)
    def _():
        dgain_ref[...] = jnp.zeros_like(dgain_ref)

    @pl.when(grp == 0)
    def _():
        d_og, o = dog_ref[...], o_ref[...]
        silu, dsilu = _silu_parts(z_ref[...])
        d_o = d_og * silu
        dz_ref[...] = (d_og * o * dsilu).astype(dz_ref.dtype)
        do_scr[...] = d_o
        dl_scr[...] = jnp.broadcast_to(jnp.sum(d_o * o, axis=1, keepdims=True), o.shape)
        lane = lax.broadcasted_iota(jnp.int32, o.shape, 1)
        ls_scr[...] = jnp.broadcast_to(
            jnp.sum(jnp.where(lane == head, lse_ref[...], 0.0), axis=1, keepdims=True), o.shape)

    for gi, dil in enumerate(B_DILATIONS):
        @pl.when(grp == gi)
        def _(gi=gi, dil=dil):
            q_raw, k_raw = qkv_ref[0], qkv_ref[1]
            gq = gain_ref[gi:gi + 1, :]
            gk = gain_ref[B_GROUPS + gi:B_GROUPS + gi + 1, :]
            qn_scr[...], rq = _norm_rope(q_raw, gq, cos_t, sin_t)
            kn_scr[...], rk = _norm_rope(k_raw, gk, cos_t, sin_t)
            def blocks(it, carry):
                scored = []
                for j in range(ATT_UNROLL):
                    cur, prev, has_prev = _stream_rows(it * ATT_UNROLL + j, dil, s_len)
                    rc, rp = _rows(cur, dil), _rows(prev, dil)
                    qb, d_ob = _bf(qn_scr[rc, :]), _bf(do_scr[rc, :])
                    k2 = _bf(jnp.concatenate([kn_scr[rp, :], kn_scr[rc, :]], axis=0))
                    v2 = _bf(jnp.concatenate([qkv_ref.at[2][rp, :], qkv_ref.at[2][rc, :]], axis=0))
                    scored.append((rc, rp, has_prev, qb, d_ob, k2,
                                   _mm_nt(qb, k2) * ATT_SCALE, _mm_nt(d_ob, v2)))
                grads = []
                for rc, rp, has_prev, qb, d_ob, k2, s, d_p in scored:
                    p = jnp.exp(jnp.where(_band_mask(has_prev), s - ls_scr[rc, :][:, 0:1], NEG_BIG))
                    ds = _bf(p * (d_p - dl_scr[rc, :][:, 0:1]))
                    grads.append((rc, rp, has_prev,
                                  _mm(ds, k2) * ATT_SCALE, _mm_tn(ds, qb) * ATT_SCALE, _mm_tn(_bf(p), d_ob)))
                for j, (rc, rp, has_prev, dq, dk2, dv2) in enumerate(grads):
                    dqn_scr[rc, :] = dq
                    if j == 0:
                        @pl.when(has_prev)
                        def _():
                            dkn_scr[rp, :] += dk2[:B_BLOCK]
                            dv_scr[rp, :] += dv2[:B_BLOCK]
                    if j + 1 < ATT_UNROLL:
                        dkn_scr[rc, :] = dk2[B_BLOCK:] + grads[j + 1][4][:B_BLOCK]
                        dv_scr[rc, :] = dv2[B_BLOCK:] + grads[j + 1][5][:B_BLOCK]
                    else:
                        dkn_scr[rc, :] = dk2[B_BLOCK:]
                        dv_scr[rc, :] = dv2[B_BLOCK:]
                return carry

            lax.fori_loop(0, n_blocks // ATT_UNROLL, blocks, 0)
            dq, dgq = _norm_rope_bwd(q_raw, rq, gq, cos_t, sin_t, dqn_scr[...])
            dk, dgk = _norm_rope_bwd(k_raw, rk, gk, cos_t, sin_t, dkn_scr[...])
            dqkv_ref[0] = dq.astype(dqkv_ref.dtype)
            dqkv_ref[1] = dk.astype(dqkv_ref.dtype)
            dqkv_ref[2] = dv_scr[...].astype(dqkv_ref.dtype)
            dgain_ref[gi:gi + 1, :] += dgq
            dgain_ref[B_GROUPS + gi:B_GROUPS + gi + 1, :] += dgk


def attn_bwd(proj_b, cos_t, sin_t, gains8, d_og, o, lse, n_seq):
    t_rows = proj_b.shape[1]
    s_len = t_rows // n_seq
    head_blk = pl.BlockSpec((s_len, B_DH), lambda b, h, g: (b, h))
    seq_blk = pl.BlockSpec((s_len, 128), lambda b, h, g: (b, 0))
    grp_blk = pl.BlockSpec((3, s_len, B_DH), lambda b, h, g: (g, b, h))
    gain_blk = pl.BlockSpec((8, 128), lambda b, h, g: (0, 0))
    return pl.pallas_call(
        _attn_bwd_body, name="attn_bwd", grid=(n_seq, B_HEADS, B_GROUPS),
        in_specs=[
            grp_blk,
            pl.BlockSpec((None, s_len, B_DH), lambda b, h, g: (B_PIECES - 1, b, h)),
            seq_blk, seq_blk, gain_blk, head_blk, head_blk, seq_blk,
        ],
        out_specs=[grp_blk, head_blk, gain_blk],
        out_shape=[jax.ShapeDtypeStruct((3 * B_GROUPS, t_rows, B_W), BF16),
                   jax.ShapeDtypeStruct((t_rows, B_W), BF16),
                   jax.ShapeDtypeStruct((8, 128), F32)],
        scratch_shapes=[pltpu.VMEM((s_len, B_DH), F32) for _ in range(8)],
        compiler_params=_params(("arbitrary", "arbitrary", "arbitrary")),
    )(proj_b, proj_b, cos_t, sin_t, gains8, d_og, o, lse)


def rope_tables(positions):
    inv_freq = ROPE_THETA ** (-jnp.arange(0, ROPE_DIMS, 2, dtype=F32) / ROPE_DIMS)
    ang = positions.astype(F32)[:, None] * inv_freq
    cos, sin = jnp.cos(ang), jnp.sin(ang)
    t_rows = positions.shape[0]
    rest = B_DH - ROPE_DIMS
    cos_t = jnp.concatenate([cos, cos, jnp.ones((t_rows, rest), F32)], axis=1)
    sin_t = jnp.concatenate([-sin, sin, jnp.zeros((t_rows, rest), F32)], axis=1)
    return cos_t, sin_t


def _rms_fwd_body(x_ref, g_ref, *rest, layer):
    h_ref = rest[-1]
    x = x_ref[...]
    r = lax.rsqrt(jnp.mean(x * x, axis=1, keepdims=True) + EPS)
    h_ref[...] = (x * r * g_ref[layer:layer + 1, :]).astype(h_ref.dtype)


def rms_fwd(x, gains8, layer, after=None):
    t_rows, d = x.shape
    tm = min(ROW_TILE, t_rows)
    in_specs = [pl.BlockSpec((tm, d), lambda i: (i, 0)), pl.BlockSpec((8, d), lambda i: (0, 0))]
    args = [x, gains8]
    if after is not None:
        in_specs.append(HBM_SPEC)
        args.append(after)
    return pl.pallas_call(
        functools.partial(_rms_fwd_body, layer=layer), name=f"rms_fwd_{layer}", grid=(t_rows // tm,),
        in_specs=in_specs,
        out_specs=pl.BlockSpec((tm, d), lambda i: (i, 0)),
        out_shape=jax.ShapeDtypeStruct((t_rows, d), BF16),
        compiler_params=_params(("arbitrary",)),
    )(*args)


def _rms_bwd_body(x_ref, g_ref, dh_ref, res_ref, dx_ref, dg_ref, *, layer):
    x, dh = x_ref[...], dh_ref[...]
    r = lax.rsqrt(jnp.mean(x * x, axis=1, keepdims=True) + EPS)
    xr = x * r
    u = dh * g_ref[layer:layer + 1, :]
    dx_ref[...] = res_ref[...] + r * u - xr * (r * r) * jnp.mean(u * x, axis=1, keepdims=True)

    @pl.when(pl.program_id(0) == 0)
    def _():
        dg_ref[...] = jnp.zeros_like(dg_ref)

    dg_ref[0:1, :] += jnp.sum(dh * xr, axis=0, keepdims=True)


def rms_bwd(x, gains8, layer, dh, d_res):
    t_rows, d = x.shape
    tm = min(ROW_TILE, t_rows)
    blk = pl.BlockSpec((tm, d), lambda i: (i, 0))
    gblk = pl.BlockSpec((8, d), lambda i: (0, 0))
    return pl.pallas_call(
        functools.partial(_rms_bwd_body, layer=layer), name=f"rms_bwd_{layer}", grid=(t_rows // tm,),
        in_specs=[blk, gblk, blk, blk],
        out_specs=[blk, gblk],
        out_shape=[jax.ShapeDtypeStruct((t_rows, d), F32), jax.ShapeDtypeStruct((8, d), F32)],
        compiler_params=_params(("arbitrary",)),
    )(x, gains8, dh, d_res)


def _piece_col(p):
    return jnp.where(p < 3 * B_GROUPS, (p % 3) * B_GROUPS + p // 3, 3 * B_GROUPS)


def _mm_nn_body(a_ref, w_ref, *rest, has_res):
    o_ref = rest[-1]
    acc = jnp.dot(a_ref[...], w_ref[...], preferred_element_type=F32)
    if has_res:
        acc = acc + rest[0][...]
    o_ref[...] = acc


def mm_nn(a, w, residual=None, *, tn, name):
    m, k = a.shape
    n = w.shape[1]
    tm = min(ROW_TILE, m)
    in_specs = [pl.BlockSpec((tm, k), lambda j, i: (i, 0)), pl.BlockSpec((k, tn), lambda j, i: (0, j))]
    args = [a, w]
    if residual is not None:
        in_specs.append(pl.BlockSpec((tm, tn), lambda j, i: (i, j)))
        args.append(residual)
    return pl.pallas_call(
        functools.partial(_mm_nn_body, has_res=residual is not None), name=name, grid=(n // tn, m // tm),
        in_specs=in_specs,
        out_specs=pl.BlockSpec((tm, tn), lambda j, i: (i, j)),
        out_shape=jax.ShapeDtypeStruct((m, n), F32),
        compiler_params=_params(("arbitrary", "arbitrary")),
    )(*args)


def mm_nn_pieces(a, w, *, name):
    m, k = a.shape
    tm = min(ROW_TILE, m)
    return pl.pallas_call(
        functools.partial(_mm_nn_body, has_res=False), name=name, grid=(B_PIECES, m // tm),
        in_specs=[pl.BlockSpec((tm, k), lambda p, i: (i, 0)),
                  pl.BlockSpec((k, B_W), lambda p, i: (0, _piece_col(p)))],
        out_specs=pl.BlockSpec((None, tm, B_W), lambda p, i: (p, i, 0)),
        out_shape=jax.ShapeDtypeStruct((B_PIECES, m, B_W), F32),
        compiler_params=_params(("arbitrary", "arbitrary")),
    )(a, w)


NT_ROW_TILE = 1024


def _mm_nt_body(g_ref, w_ref, *rest, has_init):
    o_ref = rest[-1]
    j = pl.program_id(1)
    part = lax.dot_general(_bf(g_ref[...]), w_ref[...], (((1,), (1,)), ((), ())), preferred_element_type=F32)

    @pl.when(j == 0)
    def _():
        o_ref[...] = part + rest[0][...] if has_init else part

    @pl.when(j > 0)
    def _():
        o_ref[...] += part


def mm_nt(g, w, init=None, *, tn, col_off=0, name, after=None):
    m, n = g.shape
    k = w.shape[0]
    tm = min(NT_ROW_TILE, m)
    in_specs = [pl.BlockSpec((tm, tn), lambda i, j: (i, j)),
                pl.BlockSpec((k, tn), lambda i, j: (0, col_off + j))]
    args = [g, w]
    if init is not None:
        in_specs.append(pl.BlockSpec((tm, k), lambda i, j: (i, 0)))
        args.append(init)
    if after is not None:
        in_specs.append(HBM_SPEC)
        args.append(after)
    return pl.pallas_call(
        functools.partial(_mm_nt_body, has_init=init is not None), name=name, grid=(m // tm, n // tn),
        in_specs=in_specs,
        out_specs=pl.BlockSpec((tm, k), lambda i, j: (i, 0)),
        out_shape=jax.ShapeDtypeStruct((m, k), F32),
        compiler_params=_params(("arbitrary", "arbitrary")),
    )(*args)


def mm_nt_pieces(g9, w, *, name):
    n_p, m, _ = g9.shape
    k = w.shape[0]
    tm = min(NT_ROW_TILE, m)
    return pl.pallas_call(
        functools.partial(_mm_nt_body, has_init=False), name=name, grid=(m // tm, n_p),
        in_specs=[pl.BlockSpec((None, tm, B_W), lambda i, p: (p, i, 0)),
                  pl.BlockSpec((k, B_W), lambda i, p: (0, _piece_col(p)))],
        out_specs=pl.BlockSpec((tm, k), lambda i, p: (i, 0)),
        out_shape=jax.ShapeDtypeStruct((m, k), F32),
        compiler_params=_params(("arbitrary", "arbitrary")),
    )(g9, w)


def _mm_tn_body(a_ref, g_ref, o_ref):
    o_ref[...] = lax.dot_general(a_ref[...], _bf(g_ref[...]), (((0,), (0,)), ((), ())),
                                 preferred_element_type=F32).astype(o_ref.dtype)


def mm_tn(a, g, *, tn, out_dtype, name):
    m, k = a.shape
    n = g.shape[1]
    return pl.pallas_call(
        _mm_tn_body, name=name, grid=(n // tn,),
        in_specs=[pl.BlockSpec((m, k), lambda j: (0, 0)), pl.BlockSpec((m, tn), lambda j: (0, j))],
        out_specs=pl.BlockSpec((k, tn), lambda j: (0, j)),
        out_shape=jax.ShapeDtypeStruct((k, n), out_dtype),
        compiler_params=_params(("arbitrary",)),
    )(a, g)


B_UNIT = 256
B_IN_COLS = B_PIECES * B_W
B_SHARD_UNITS = B_IN_COLS // N_DEV // B_UNIT


def mm_tn_b_in(a, g9, gz, *, out_dtype, name):
    m, k = a.shape
    per_piece = B_W // B_UNIT
    n_units = B_IN_COLS // B_UNIT

    def g_map(u):
        nat = jnp.minimum(u // per_piece, 3 * B_GROUPS - 1)
        piece = (nat % B_GROUPS) * 3 + nat // B_GROUPS
        return (piece, 0, u % per_piece)

    def body(a_ref, g_ref, z_ref, o_ref):
        u = pl.program_id(0)

        @pl.when(u < 3 * B_GROUPS * per_piece)
        def _():
            _mm_tn_body(a_ref, g_ref, o_ref)

        @pl.when(u >= 3 * B_GROUPS * per_piece)
        def _():
            _mm_tn_body(a_ref, z_ref, o_ref)

    return pl.pallas_call(
        body, name=name, grid=(n_units,),
        in_specs=[pl.BlockSpec((m, k), lambda u: (0, 0)),
                  pl.BlockSpec((None, m, B_UNIT), g_map),
                  pl.BlockSpec((m, B_UNIT), lambda u: (0, jnp.where(u < 3 * B_GROUPS * per_piece, 0, u % per_piece)))],
        out_specs=pl.BlockSpec((None, k, B_UNIT), lambda u: (u // B_SHARD_UNITS, 0, u % B_SHARD_UNITS)),
        out_shape=jax.ShapeDtypeStruct((N_DEV, k, B_IN_COLS // N_DEV), out_dtype),
        compiler_params=_params(("arbitrary",)),
    )(a, g9, gz)


def _loss_body(y_ref, t_ref, dy_ref, loss_ref, acc):
    i = pl.program_id(0)
    d = y_ref.shape[1]
    err = y_ref[...] - t_ref[...]
    dy_ref[...] = err * (1.0 / d)

    @pl.when(i == 0)
    def _():
        acc[...] = jnp.zeros_like(acc)

    acc[...] += jnp.sum(err * err, axis=0, keepdims=True)

    @pl.when(i == pl.num_programs(0) - 1)
    def _():
        total = jnp.sum(acc[...], axis=1, keepdims=True) * (0.5 / d)
        loss_ref[...] = jnp.broadcast_to(total, loss_ref.shape)


def loss_head(y, target):
    t_rows, d = y.shape
    tm = min(ROW_TILE, t_rows)
    blk = pl.BlockSpec((tm, d), lambda i: (i, 0))
    return pl.pallas_call(
        _loss_body, name="loss_head", grid=(t_rows // tm,),
        in_specs=[blk, blk],
        out_specs=[blk, pl.BlockSpec((8, 128), lambda i: (0, 0))],
        out_shape=[jax.ShapeDtypeStruct((t_rows, d), F32), jax.ShapeDtypeStruct((8, 128), F32)],
        scratch_shapes=[pltpu.VMEM((1, d), F32)],
        compiler_params=_params(("arbitrary",)),
    )(y, target)


def _adamw_body(p_ref, w_ref, m_ref, v_ref, g_ref, d_ref, nm_ref, nv_ref):
    g = p_ref[0].astype(F32)
    for s in range(1, N_DEV):
        g = g + p_ref[s].astype(F32)
    w = w_ref[...]
    m = ADAM_B1 * m_ref[...] + (1.0 - ADAM_B1) * g
    v = ADAM_B2 * v_ref[...] + (1.0 - ADAM_B2) * (g * g)
    m_hat = m / (1.0 - ADAM_B1 ** ADAM_STEP)
    v_hat = v / (1.0 - ADAM_B2 ** ADAM_STEP)
    g_ref[...] = g
    d_ref[...] = -ADAM_LR * (m_hat / (jnp.sqrt(v_hat) + ADAM_EPS) + ADAM_WD * w)
    nm_ref[...] = m
    nv_ref[...] = v


def adamw(parts, w, m, v, *, name):
    _, r, c = w.shape
    tr = r if r <= 256 else 256
    blk = pl.BlockSpec((None, tr, c), lambda i: (0, i, 0))
    out = jax.ShapeDtypeStruct((1, r, c), F32)
    return pl.pallas_call(
        _adamw_body, name=name, grid=(r // tr,),
        in_specs=[pl.BlockSpec((N_DEV, tr, c), lambda i: (0, i, 0)), blk, blk, blk],
        out_specs=[blk, blk, blk, blk],
        out_shape=[out, out, out, out],
        compiler_params=_params(("arbitrary",)),
    )(parts, w, m, v)


MESH_ID = pl.DeviceIdType.MESH
HBM_SPEC = pl.BlockSpec(memory_space=pl.ANY)


def _my_place():
    return lax.axis_index("x"), lax.axis_index("y"), lax.axis_index("c")


def _flat(x, y, c):
    return 4 * x + 2 * y + c


def _all_gather_body(*refs, n):
    ins, outs = refs[:n], refs[n:2 * n]
    send_sems, recv_sems, local_sems = refs[2 * n:]
    x, y, c = _my_place()
    me, sibling = (x, y, c), (x, y, 1 - c)
    chips = [(1 - x, y), (x, 1 - y), (1 - x, 1 - y)]
    pending = []
    for a in range(n):
        src, out = ins[a], outs[a]

        def copy(k, block, to, from_input=False, a=a, src=src, out=out):
            slot = out.at[_flat(*block)]
            return pltpu.make_async_remote_copy(
                src_ref=src if from_input else slot, dst_ref=slot,
                send_sem=send_sems.at[7 * a + k], recv_sem=recv_sems.at[7 * a + k],
                device_id=to, device_id_type=MESH_ID)

        mine = pltpu.make_async_copy(src, out.at[_flat(*me)], local_sems.at[a])
        mine.start()
        first = [copy(0, me, sibling, True)] + [copy(1 + j, me, (*chip, c), True) for j, chip in enumerate(chips)]
        for cp in first:
            cp.start()
        pending.append((copy, mine, first))
    for copy, mine, first in pending:
        passed = [copy(4 + j, (*chip, c), sibling) for j, chip in enumerate(chips)]
        for j, chip in enumerate(chips):
            copy(1 + j, (*chip, c), me).wait_recv()
            passed[j].start()
        copy(0, sibling, me).wait_recv()
        for j, chip in enumerate(chips):
            copy(4 + j, (*chip, 1 - c), me).wait_recv()
        for cp in first + passed:
            cp.wait_send()
        mine.wait()


def all_gather(shards, *, name):
    n = len(shards)
    return pl.pallas_call(
        functools.partial(_all_gather_body, n=n), name=name,
        in_specs=[HBM_SPEC] * n, out_specs=[HBM_SPEC] * n,
        out_shape=[jax.ShapeDtypeStruct((N_DEV,) + s.shape, s.dtype) for s in shards],
        scratch_shapes=[pltpu.SemaphoreType.DMA((7 * n,)), pltpu.SemaphoreType.DMA((7 * n,)),
                        pltpu.SemaphoreType.DMA((n,))],
    )(*shards)


PEER_FLIPS = [(0, 0, 1), (1, 0, 0), (0, 1, 0), (1, 1, 0), (1, 0, 1), (0, 1, 1), (1, 1, 1)]


def _all_to_all_body(*refs, n):
    ins, outs = refs[:n], refs[n:2 * n]
    send_sems, recv_sems, local_sems = refs[2 * n:]
    x, y, c = _my_place()
    me = _flat(x, y, c)
    waits = []
    for a in range(n):
        src, out = ins[a], outs[a]
        mine = pltpu.make_async_copy(src.at[me], out.at[me], local_sems.at[a])
        mine.start()
        waits.append(mine)
        for k, (fx, fy, fc) in enumerate(PEER_FLIPS):
            peer = (1 - x if fx else x, 1 - y if fy else y, 1 - c if fc else c)
            theirs = _flat(*peer)
            sems = dict(send_sem=send_sems.at[7 * a + k], recv_sem=recv_sems.at[7 * a + k],
                        device_id=peer, device_id_type=MESH_ID)
            send = pltpu.make_async_remote_copy(src_ref=src.at[theirs], dst_ref=out.at[me], **sems)
            send.start()
            recv = pltpu.make_async_remote_copy(src_ref=src.at[theirs], dst_ref=out.at[theirs], **sems)
            waits.append((send, recv))
    for w in waits:
        if isinstance(w, tuple):
            w[0].wait_send()
            w[1].wait_recv()
        else:
            w.wait()


def all_to_all(parts, *, name):
    n = len(parts)
    return pl.pallas_call(
        functools.partial(_all_to_all_body, n=n), name=name,
        in_specs=[HBM_SPEC] * n, out_specs=[HBM_SPEC] * n,
        out_shape=[jax.ShapeDtypeStruct(p.shape, p.dtype) for p in parts],
        scratch_shapes=[pltpu.SemaphoreType.DMA((7 * n,)), pltpu.SemaphoreType.DMA((7 * n,)),
                        pltpu.SemaphoreType.DMA((n,))],
    )(*parts)


HBM_ONLY = pl.BlockSpec(memory_space=pltpu.HBM)
SEM_SPEC = pl.BlockSpec(memory_space=pltpu.SEMAPHORE)
DATAFLOW_EFFECT = pltpu.SideEffectType.DATAFLOW_SIDE_EFFECTING


def _split_copies(srcs, lands, send_sems, recv_sems, n, scatter):
    x, y, c = _my_place()
    me = _flat(x, y, c)
    pairs = []
    for a in range(n):
        for k, (fx, fy, fc) in enumerate(PEER_FLIPS):
            peer = (1 - x if fx else x, 1 - y if fy else y, 1 - c if fc else c)
            theirs = _flat(*peer)
            src = srcs[a].at[theirs] if scatter else srcs[a]
            sems = dict(send_sem=send_sems.at[7 * a + k], recv_sem=recv_sems.at[7 * a + k],
                        device_id=peer, device_id_type=MESH_ID)
            pairs.append((pltpu.make_async_remote_copy(src_ref=src, dst_ref=lands[a].at[me], **sems),
                          pltpu.make_async_remote_copy(src_ref=src, dst_ref=lands[a].at[theirs], **sems)))
    return pairs


def _exchange_start_body(*refs, n, scatter):
    srcs, lands = refs[:n], refs[n:2 * n]
    send_sems, recv_sems = refs[2 * n], refs[2 * n + 1]
    token = refs[-1]
    for send, _ in _split_copies(srcs, lands, send_sems, recv_sems, n, scatter):
        send.start()
    token[...] = jnp.zeros_like(token)


def exchange_start(srcs, lands, *, scatter, name):
    n = len(srcs)
    args = [pltpu.with_memory_space_constraint(t, pltpu.HBM) for t in list(srcs) + list(lands)]
    outs = pl.pallas_call(
        functools.partial(_exchange_start_body, n=n, scatter=scatter), name=name,
        out_shape=(pltpu.SemaphoreType.DMA((7 * n,)), pltpu.SemaphoreType.DMA((7 * n,)),
                   *[pltpu.HBM(t.shape, t.dtype) for t in args],
                   jax.ShapeDtypeStruct((8, 128), F32)),
        in_specs=[HBM_ONLY] * (2 * n),
        out_specs=(SEM_SPEC, SEM_SPEC, *[HBM_ONLY] * (2 * n), pl.BlockSpec(memory_space=pltpu.VMEM)),
        input_output_aliases={i: 2 + i for i in range(2 * n)},
        compiler_params=pltpu.CompilerParams(has_side_effects=DATAFLOW_EFFECT),
    )(*args)
    return outs[0], outs[1], outs[2:2 + n], outs[2 + n:2 + 2 * n], outs[-1]


def _exchange_wait_body(*refs, n, scatter):
    srcs, lands = refs[:n], refs[n:2 * n]
    send_sems, recv_sems = refs[2 * n], refs[2 * n + 1]
    for send, recv in _split_copies(srcs, lands, send_sems, recv_sems, n, scatter):
        send.wait_send()
        recv.wait_recv()


def exchange_wait(send_sems, recv_sems, srcs, lands, after, *, scatter, name):
    n = len(srcs)
    outs = pl.pallas_call(
        functools.partial(_exchange_wait_body, n=n, scatter=scatter), name=name,
        out_shape=tuple(pltpu.HBM(t.shape, t.dtype) for t in list(srcs) + list(lands)),
        in_specs=[HBM_ONLY] * (2 * n) + [SEM_SPEC, SEM_SPEC, HBM_SPEC],
        out_specs=tuple([HBM_ONLY] * (2 * n)),
        input_output_aliases={i: i for i in range(2 * n)},
        compiler_params=pltpu.CompilerParams(has_side_effects=DATAFLOW_EFFECT),
    )(*srcs, *lands, send_sems, recv_sems, after)
    return outs[n:]


def _own_slot_only(shape_dtype, own, slot):
    land = lax.empty(shape_dtype.shape, shape_dtype.dtype)
    return lax.dynamic_update_slice(land, own[None], (slot,) + (0,) * own.ndim)


def _pad_rows(a, rows=8):
    return jnp.pad(a, ((0, rows - a.shape[0]), (0, 0)))


def _gate_rows(a_log, dt_bias):
    z = jnp.zeros((8, 128), F32)
    return z.at[0, A_HEADS:2 * A_HEADS].set(a_log[0]).at[1, A_HEADS:2 * A_HEADS].set(dt_bias[0])


def _pack_small(norm_g, a_log, a_dt_bias, a_norm_g, b_q_norm_g, b_k_norm_g):
    return jnp.concatenate([
        norm_g[0].reshape(8, 128), norm_g[1].reshape(8, 128),
        _gate_rows(a_log, a_dt_bias),
        _pad_rows(a_norm_g[0].reshape(2, 128)),
        _pad_rows(jnp.concatenate([b_q_norm_g[0], b_k_norm_g[0]], axis=0)),
    ], axis=0)


def _unpack_small(p):
    return (p[0:16].reshape(2, D_MODEL), p[16:17, A_HEADS:2 * A_HEADS], p[17:18, A_HEADS:2 * A_HEADS],
            p[24:26].reshape(1, A_DV), p[32:35][None], p[35:38][None])


def kernel(x, positions, norm_g, a_w_in, a_conv_w, a_log, a_dt_bias, a_norm_g, a_w_out, b_w_in, b_q_norm_g, b_k_norm_g, b_w_out, loss_target, m_norm_g, m_a_w_in, m_a_conv_w, m_a_log, m_a_dt_bias, m_a_norm_g, m_a_w_out, m_b_w_in, m_b_q_norm_g, m_b_k_norm_g, m_b_w_out, v_norm_g, v_a_w_in, v_a_conv_w, v_a_log, v_a_dt_bias, v_a_norm_g, v_a_w_out, v_b_w_in, v_b_q_norm_g, v_b_k_norm_g, v_b_w_out):
    n_seq, s_len, d = x.shape
    t_rows = n_seq * s_len
    n_chunks = s_len // CHUNK
    x0 = x.reshape(t_rows, d)
    target = loss_target.reshape(t_rows, d)
    my_slot = _flat(*_my_place())

    g_a_in, g_conv = all_gather([a_w_in[0].astype(BF16), _pad_rows(a_conv_w[0])], name="gather_weights_first")
    later = [a_w_out[0].astype(BF16), b_w_in[0].astype(BF16), b_w_out[0].astype(BF16)]
    lands = [_own_slot_only(jax.ShapeDtypeStruct((N_DEV,) + t.shape, t.dtype), t, my_slot) for t in later]
    w_send, w_recv, later, lands, w_token = exchange_start(later, lands, scatter=False, name="gather_weights_start")
    w_a_in = jnp.pad(g_a_in.transpose(1, 0, 2).reshape(d, A_IN), ((0, 0), (0, A_IN_PAD - A_IN)))
    conv_w8 = g_conv.transpose(1, 0, 2).reshape(8, 2 * A_QK + A_VW)

    gains_model = _pad_rows(norm_g)
    gate_prm = _gate_rows(a_log, a_dt_bias)
    gain_a_out = _pad_rows(a_norm_g)
    gains_qk = _pad_rows(jnp.concatenate([b_q_norm_g[0], b_k_norm_g[0]], axis=0))
    cos_t, sin_t = rope_tables(positions.reshape(t_rows))

    h0 = rms_fwd(x0, gains_model, 0, after=w_token)
    proj_a = mm_nn(h0, w_a_in, tn=896, name="proj_a")
    gates_col, gates_row = a_gates_fwd(proj_a, gate_prm, n_seq)
    gates_row = gates_row.reshape(n_seq, 2 * A_HEADS, s_len // SUPER, 1, SUPER)
    o_a, tinv, states, og_a, q_a, k_a, v_a = gdn_fwd(proj_a, conv_w8, gates_col, gates_row, gain_a_out, n_seq)
    g_a_out, g_b_in, g_b_out = exchange_wait(w_send, w_recv, later, lands, og_a, scatter=False,
                                             name="gather_weights_wait")
    w_a_out = g_a_out.reshape(A_VW, d)
    w_b_in = g_b_in.transpose(1, 0, 2).reshape(d, B_IN_COLS)
    w_b_out = g_b_out.reshape(B_W, d)
    x1 = mm_nn(og_a, w_a_out, x0, tn=512, name="out_a")

    h1 = rms_fwd(x1, gains_model, 1)
    proj_b = mm_nn_pieces(h1, w_b_in, name="proj_b")
    og_b, o_b, lse = attn_fwd(proj_b, cos_t, sin_t, gains_qk, n_seq)
    y = mm_nn(og_b, w_b_out, x1, tn=512, name="out_b")

    dy, loss_blk = loss_head(y, target)
    loss = lax.psum(loss_blk[0, 0], ("x", "y", "c"))

    d_og_b = mm_nt(dy, w_b_out, tn=512, name="d_og_b")
    dw_b_out = mm_tn(og_b, dy, tn=256, out_dtype=BF16, name="dw_b_out")
    dqkv_b, dz_b, d_gains_qk = attn_bwd(proj_b, cos_t, sin_t, gains_qk, d_og_b, o_b, lse, n_seq)
    dh1 = mm_nt_pieces(dqkv_b, w_b_in, name="dh1_qkv")
    dh1 = mm_nt(dz_b, w_b_in, dh1, tn=B_W, col_off=3 * B_GROUPS, name="dh1_z")
    dw_b_in = mm_tn_b_in(h1, dqkv_b, dz_b, out_dtype=BF16, name="dw_b_in")
    dx1, d_gain1 = rms_bwd(x1, gains_model, 1, dh1, dy)

    dw_a_out = mm_tn(og_a, dx1, tn=128, out_dtype=BF16, name="dw_a_out")
    early = [dw_b_in, dw_b_out.reshape(N_DEV, B_W // N_DEV, d), dw_a_out.reshape(N_DEV, A_VW // N_DEV, d)]
    lands = [_own_slot_only(t, lax.dynamic_index_in_dim(t, my_slot, 0, keepdims=False), my_slot) for t in early]
    g_send, g_recv, early, lands, g_token = exchange_start(early, lands, scatter=True, name="scatter_grads_start")

    d_og_a = mm_nt(dx1, w_a_out, tn=512, name="d_og_a", after=g_token)
    dqkv_a, dgates, dz_a, d_gain_a_out = gdn_bwd(q_a, k_a, v_a, gates_col, gates_row, tinv, states, d_og_a, o_a,
                                                 proj_a, gain_a_out, n_seq)
    d_pre, d_conv = a_pre_bwd(proj_a, conv_w8, dqkv_a, n_seq)
    d_gate_logits, d_gate_prm = a_gates_bwd(proj_a, gate_prm, dgates, n_seq)
    dw_a_in = jnp.concatenate([
        mm_tn(h0, d_pre, tn=256, out_dtype=BF16, name="dw_a_in_qkv"),
        mm_tn(h0, dz_a, tn=256, out_dtype=BF16, name="dw_a_in_z"),
        mm_tn(h0, d_gate_logits, tn=128, out_dtype=BF16, name="dw_a_in_gates"),
    ], axis=1)[:, :A_IN]
    shard_a_in = A_IN // N_DEV
    last = [dw_a_in.reshape(d, N_DEV, shard_a_in).transpose(1, 0, 2)]
    last_lands = [_own_slot_only(t, lax.dynamic_index_in_dim(t, my_slot, 0, keepdims=False), my_slot) for t in last]
    l_send, l_recv, last, last_lands, l_token = exchange_start(last, last_lands, scatter=True,
                                                               name="scatter_last_start")
    dh0 = mm_nt(d_pre, w_a_in, tn=512, name="dh0_qkv", after=l_token)
    dh0 = mm_nt(dz_a, w_a_in, dh0, tn=512, col_off=(2 * A_QK + A_VW) // 512, name="dh0_z")
    dh0 = mm_nt(d_gate_logits, w_a_in, dh0, tn=128, col_off=A_GATE_COL, name="dh0_gates")
    dx0, d_gain0 = rms_bwd(x0, gains_model, 0, dh0, dx1)

    small = jnp.concatenate([
        d_gain0[0].reshape(8, 128), d_gain1[0].reshape(8, 128), d_gate_prm,
        _pad_rows(d_gain_a_out[0].reshape(2, 128)), d_gains_qk], axis=0)
    r_small, r_conv = all_gather([small, d_conv], name="gather_small_grads")
    conv_cols = a_conv_w.shape[2]
    r_conv = lax.dynamic_slice(r_conv, (0, 0, my_slot * conv_cols), (N_DEV, 8, conv_cols))

    r_b_in, r_b_out, r_a_out = exchange_wait(g_send, g_recv, early, lands, r_small, scatter=True,
                                             name="scatter_grads_wait")
    (r_a_in,) = exchange_wait(l_send, l_recv, last, last_lands, r_small, scatter=True, name="scatter_last_wait")

    upd = {}
    upd["a_w_in"] = adamw(r_a_in, a_w_in, m_a_w_in, v_a_w_in, name="adamw_a_w_in")
    upd["a_w_out"] = adamw(r_a_out, a_w_out, m_a_w_out, v_a_w_out, name="adamw_a_w_out")
    upd["b_w_in"] = adamw(r_b_in, b_w_in, m_b_w_in, v_b_w_in, name="adamw_b_w_in")
    upd["b_w_out"] = adamw(r_b_out, b_w_out, m_b_w_out, v_b_w_out, name="adamw_b_w_out")
    upd["a_conv_w"] = [t[:, :A_CONV] for t in adamw(
        r_conv, _pad_rows(a_conv_w[0])[None], _pad_rows(m_a_conv_w[0])[None], _pad_rows(v_a_conv_w[0])[None],
        name="adamw_a_conv_w")]
    small_upd = adamw(
        r_small,
        _pack_small(norm_g, a_log, a_dt_bias, a_norm_g, b_q_norm_g, b_k_norm_g)[None],
        _pack_small(m_norm_g, m_a_log, m_a_dt_bias, m_a_norm_g, m_b_q_norm_g, m_b_k_norm_g)[None],
        _pack_small(v_norm_g, v_a_log, v_a_dt_bias, v_a_norm_g, v_b_q_norm_g, v_b_k_norm_g)[None],
        name="adamw_small")
    small_names = ("norm_g", "a_log", "a_dt_bias", "a_norm_g", "b_q_norm_g", "b_k_norm_g")
    unpacked = [_unpack_small(t[0]) for t in small_upd]
    for i, nm in enumerate(small_names):
        upd[nm] = [u[i] for u in unpacked]

    order = ("norm_g", "a_w_in", "a_conv_w", "a_log", "a_dt_bias", "a_norm_g", "a_w_out",
             "b_w_in", "b_q_norm_g", "b_k_norm_g", "b_w_out")
    outs = [loss, dx0.reshape(n_seq, s_len, d)]
    for kind in range(4):
        for nm in order:
            outs.append(upd[nm][kind])
    return tuple(outs)
```

```python
import functools
import math

import jax
import jax.numpy as jnp
from jax import lax
from jax.experimental import pallas as pl
from jax.experimental.pallas import tpu as pltpu

F32 = jnp.float32
BF16 = jnp.bfloat16

D_MODEL = 1024
EPS = 1e-6
N_DEV = 8

A_HEADS = 8
A_DK = 128
A_DV = 256
A_QK = A_HEADS * A_DK
A_VW = A_HEADS * A_DV
A_CONV = 4
CHUNK = 64
A_IN = 2 * A_QK + 2 * A_VW + 2 * A_HEADS
A_IN_PAD = 2 * A_QK + 2 * A_VW + 128
A_GATE_COL = (2 * A_QK + 2 * A_VW) // 128

B_DILATIONS = (1, 4, 16)
B_GROUPS = 3
B_HEADS = 8
B_DH = 128
B_W = B_HEADS * B_DH
B_BLOCK = 128
B_PIECES = 3 * B_GROUPS + 1
ROPE_THETA = 500000.0
ROPE_DIMS = B_DH // 4
ROPE_HALF = ROPE_DIMS // 2

ADAM_LR = 0.001
ADAM_B1 = 0.9
ADAM_B2 = 0.999
ADAM_EPS = 1e-08
ADAM_WD = 0.01
ADAM_STEP = 10

VMEM_LIMIT = 60 * 1024 * 1024


def _params(sem):
    return pltpu.CompilerParams(dimension_semantics=sem, vmem_limit_bytes=VMEM_LIMIT)


def _bf(x):
    return x.astype(BF16)


def _mm(a, b):
    return jnp.dot(_bf(a), _bf(b), preferred_element_type=F32)


def _mm_nt(a, b):
    return lax.dot_general(_bf(a), _bf(b), (((1,), (1,)), ((), ())), preferred_element_type=F32)


def _mm_tn(a, b):
    return lax.dot_general(_bf(a), _bf(b), (((0,), (0,)), ((), ())), preferred_element_type=F32)


def _split(x):
    hi = _bf(x)
    return hi, _bf(x - hi.astype(F32))


def _mm3(a, b):
    ah, al = _split(a)
    bh, bl = _split(b)
    d = functools.partial(jnp.dot, preferred_element_type=F32)
    return d(ah, bh) + (d(ah, bl) + d(al, bh))


def _colsum_as_col(z):
    zh, zl = _split(z)
    ones = jnp.ones((z.shape[0], 128), BF16)
    tn = functools.partial(lax.dot_general, dimension_numbers=(((0,), (0,)), ((), ())),
                           preferred_element_type=F32)
    return (tn(zh, ones) + tn(zl, ones))[:, 0:1]


def _sigmoid(x):
    return 0.5 * jnp.tanh(0.5 * x) + 0.5


INV_BASE = 8
INV_NEWTON = 2
GDN_GROUP = 4
SUPER = GDN_GROUP * CHUNK
GDN_WIDTH = 2

A_K_COL = A_QK // A_DK
A_V_COL = 2 * A_QK // A_DV
A_HEAD_W = 2 * A_DK + A_DV


def _inverse_steps(m, row, col):
    eye = (row == col).astype(F32)
    d = jnp.where(row // INV_BASE == col // INV_BASE, m, 0.0)
    x = eye - d
    p = _mm(d, d)
    yield
    steps = int(math.log2(INV_BASE)) - 1
    for i in range(steps):
        x = x + _mm(x, p)
        if i + 1 < steps:
            p = _mm(p, p)
        yield
    size = INV_BASE
    while size < CHUNK:
        c = jnp.where((row // (2 * size) == col // (2 * size)) & (row // size != col // size), m, 0.0)
        xc = _mm(x, c)
        yield
        x = x - _mm(xc, x)
        yield
        size *= 2
    for _ in range(INV_NEWTON):
        r = eye - x - _mm3(m, x)
        yield
        x = x + _mm(x, r)
        yield
    return x


def _drain(gen):
    while True:
        try:
            next(gen)
        except StopIteration as stop:
            return stop.value


def _interleave(*gens):
    live = list(gens)
    while live:
        for g in list(live):
            try:
                next(g)
            except StopIteration:
                live.remove(g)


def _diag_blocks_tall(x):
    return jnp.concatenate([x[i * CHUNK:(i + 1) * CHUNK, i * CHUNK:(i + 1) * CHUNK] for i in range(GDN_GROUP)], axis=0)


def _tall_to_block_diag(t, same):
    return jnp.where(same, jnp.concatenate([t] * GDN_GROUP, axis=1), 0.0)


def _block_sum(same, x):
    xh, xl = _split(jnp.broadcast_to(x, (SUPER, 128)))
    ones = same.astype(BF16)
    d = functools.partial(jnp.dot, preferred_element_type=F32)
    return (d(ones, xh) + d(ones, xl))[:, 0:1]


def _aligned_rows(index, size):
    start = index * size
    return pl.ds(start if isinstance(start, int) else pl.multiple_of(start, size), size)


def _super_rows(i):
    return _aligned_rows(i, SUPER)


def _chunk_rows(n):
    return _aligned_rows(n, CHUNK)


def _gdn_super_steps(q, k, v, gcb, gr, head, tinv_tall=None):
    lane = lax.broadcasted_iota(jnp.int32, (SUPER, 128), 1)
    row = lax.broadcasted_iota(jnp.int32, (SUPER, SUPER), 0)
    col = lax.broadcasted_iota(jnp.int32, (SUPER, SUPER), 1)
    same = row // CHUNK == col // CHUNK
    beta = jnp.sum(jnp.where(lane == head, gcb, 0.0), axis=1, keepdims=True)
    gc = jnp.sum(jnp.where(lane == A_HEADS + head, gcb, 0.0), axis=1, keepdims=True)
    g_last = jnp.sum(jnp.where(col == (row // CHUNK) * CHUNK + (CHUNK - 1), gr, 0.0), axis=1, keepdims=True)
    gamma = jnp.exp(gc)
    decay = jnp.where(same & (row >= col), jnp.exp(jnp.minimum(gc - gr, 0.0)), 0.0)
    kb = k * beta
    m = jnp.where(same & (row > col), _mm_nt(kb, k) * decay, 0.0)
    p = jnp.where(same & (row >= col), _mm_nt(q, k) * decay, 0.0)
    yield
    if tinv_tall is None:
        tinv = yield from _inverse_steps(m, row, col)
    else:
        tinv = _tall_to_block_diag(tinv_tall, same)
    u = _mm(tinv, v * beta)
    w = _mm(tinv, kb * gamma)
    yield
    e_tail = jnp.exp(g_last - gc)
    return dict(beta=beta, gc=gc, g_last=g_last, gamma=gamma, decay=decay, kb=kb, m=m,
                tinv=tinv, u=u, w=w, p=p, e_tail=e_tail, row=row, col=col, lane=lane, same=same)


def _gdn_super_common(q, k, v, gcb, gr, head, tinv_tall=None):
    return _drain(_gdn_super_steps(q, k, v, gcb, gr, head, tinv_tall))


def _store_scan_operands(rows, q, k, t, u_scr, w_scr, p_scr, qg_scr, ke_scr, gl_scr):
    u_scr[rows, :] = t["u"]
    w_scr[rows, :] = _bf(t["w"])
    p_scr[rows, :] = _bf(_diag_blocks_tall(t["p"]))
    qg_scr[rows, :] = _bf(q * t["gamma"])
    ke_scr[rows, :] = _bf(k * t["e_tail"])
    gl_scr[rows, :] = jnp.broadcast_to(jnp.exp(t["g_last"]), (SUPER, 128))


def _gdn_fwd_body(q_ref, k_ref, v_ref, gc_ref, gr_ref, wq_ref, wk_ref, wv_ref, z_ref, gn_ref,
                  o_ref, tinv_ref, st_ref, og_ref, qo_ref, ko_ref, vo_ref, s_scr, *sets):
    head = pl.program_id(1)
    n_super = q_ref.shape[0] // SUPER
    all_sets = [sets[6 * i:6 * i + 6] for i in range(2 * GDN_WIDTH)]
    whole = pl.ds(0, SUPER)

    def conv_silu(x_ref, w_ref, i):
        rows = _super_rows(i)
        x, w = x_ref[rows, :], w_ref[...]
        halo = jnp.zeros((8, x.shape[1]), F32) if i == 0 else x_ref[pl.ds(i * SUPER - 8, 8), :]
        ext = jnp.concatenate([halo, x], axis=0)
        c = x * w[A_CONV - 1:A_CONV, :]
        for j in range(1, A_CONV):
            c = c + pltpu.roll(ext, j, 0)[8:, :] * w[A_CONV - 1 - j:A_CONV - j, :]
        return c * _sigmoid(c)

    def unit(a):
        return a * lax.rsqrt(jnp.sum(a * a, axis=1, keepdims=True) + EPS)

    def prepare_steps(i, dst):
        rows = _super_rows(i)
        q = unit(conv_silu(q_ref, wq_ref, i)) * A_DK ** -0.5
        k = unit(conv_silu(k_ref, wk_ref, i))
        v = conv_silu(v_ref, wv_ref, i)
        qo_ref[rows, :], ko_ref[rows, :], vo_ref[rows, :] = q, k, v
        t = yield from _gdn_super_steps(q, k, v, gc_ref[rows, :], gr_ref[i], head)
        tinv_ref[rows, :] = _diag_blocks_tall(t["tinv"])
        _store_scan_operands(whole, q, k, t, *dst)

    def scan_steps(i, src):
        u_scr, w_scr, p_scr, qg_scr, ke_scr, gl_scr = src
        for j in range(GDN_GROUP):
            n = i * GDN_GROUP + j
            local = pl.ds(j * CHUNK, CHUNK)
            s = s_scr[...]
            sb = _bf(s)
            st_ref[n] = sb
            ws = jnp.dot(w_scr[local, :], sb, preferred_element_type=F32)
            yield
            vb = _bf(u_scr[local, :] - ws)
            o = (jnp.dot(qg_scr[local, :], sb, preferred_element_type=F32)
                 + jnp.dot(p_scr[local, :], vb, preferred_element_type=F32))
            s_new = s * gl_scr[local, :][0:1, 0:1] + lax.dot_general(
                ke_scr[local, :], vb, (((0,), (0,)), ((), ())), preferred_element_type=F32)
            yield
            rows = _chunk_rows(n)
            o_ref[rows, :] = o
            s_scr[...] = s_new
            silu, _ = _silu_parts(z_ref[rows, :])
            r = lax.rsqrt(jnp.mean(o * o, axis=1, keepdims=True) + EPS)
            og_ref[rows, :] = ((o * r * gn_ref[0:1, :]) * silu).astype(og_ref.dtype)

    def scan_many(first, srcs):
        for j, src in enumerate(srcs):
            yield from scan_steps(first + j, src)

    groups = [all_sets[:GDN_WIDTH], all_sets[GDN_WIDTH:]]
    _interleave(*[prepare_steps(j, groups[0][j]) for j in range(GDN_WIDTH)])
    s_scr[...] = jnp.zeros_like(s_scr)
    for g in range(n_super // GDN_WIDTH):
        cur, nxt = groups[g % 2], groups[(g + 1) % 2]
        first = g * GDN_WIDTH
        following = [prepare_steps(first + GDN_WIDTH + j, nxt[j]) for j in range(GDN_WIDTH)
                     if first + GDN_WIDTH + j < n_super]
        _interleave(scan_many(first, cur), *following)


def _gdn_in_specs(s_len, n_super, from_proj):
    k_col, v_col = (A_K_COL, A_V_COL) if from_proj else (0, 0)
    return [
        pl.BlockSpec((s_len, A_DK), lambda b, h: (b, h)),
        pl.BlockSpec((s_len, A_DK), lambda b, h: (b, k_col + h)),
        pl.BlockSpec((s_len, A_DV), lambda b, h: (b, v_col + h)),
        pl.BlockSpec((s_len, 128), lambda b, h: (b, 0)),
        pl.BlockSpec((None, None, n_super, 1, SUPER), lambda b, h: (b, A_HEADS + h, 0, 0, 0)),
    ]


def _gdn_scan_scratch(s_len):
    return [pltpu.VMEM((A_DK, A_DV), F32), pltpu.VMEM((s_len, A_DV), F32),
            pltpu.VMEM((s_len, A_DK), BF16), pltpu.VMEM((s_len, CHUNK), BF16),
            pltpu.VMEM((s_len, A_DK), BF16), pltpu.VMEM((s_len, A_DK), BF16),
            pltpu.VMEM((s_len, 128), F32)]


def gdn_fwd(proj_a, conv_w8, gates_col, gates_row, norm_g8, n_seq):
    t_rows = proj_a.shape[0]
    s_len = t_rows // n_seq
    n_chunks = s_len // CHUNK
    qk_spec = pl.BlockSpec((s_len, A_DK), lambda b, h: (b, h))
    v_spec = pl.BlockSpec((s_len, A_DV), lambda b, h: (b, h))
    return pl.pallas_call(
        _gdn_fwd_body, name="gdn_fwd", grid=(n_seq, A_HEADS),
        in_specs=_gdn_in_specs(s_len, s_len // SUPER, True) + [
            pl.BlockSpec((8, A_DK), lambda b, h: (0, h)),
            pl.BlockSpec((8, A_DK), lambda b, h: (0, A_K_COL + h)),
            pl.BlockSpec((8, A_DV), lambda b, h: (0, A_V_COL + h)),
            pl.BlockSpec((s_len, A_DV), lambda b, h: (b, A_Z_COL + h)),
            pl.BlockSpec((8, A_DV), lambda b, h: (0, 0)),
        ],
        out_specs=[
            v_spec,
            pl.BlockSpec((s_len, CHUNK), lambda b, h: (b * A_HEADS + h, 0)),
            pl.BlockSpec((None, n_chunks, A_DK, A_DV), lambda b, h: (b * A_HEADS + h, 0, 0, 0)),
            v_spec, qk_spec, qk_spec, v_spec,
        ],
        out_shape=[
            jax.ShapeDtypeStruct((t_rows, A_VW), F32),
            jax.ShapeDtypeStruct((n_seq * A_HEADS * s_len, CHUNK), F32),
            jax.ShapeDtypeStruct((n_seq * A_HEADS, n_chunks, A_DK, A_DV), BF16),
            jax.ShapeDtypeStruct((t_rows, A_VW), BF16),
            jax.ShapeDtypeStruct((t_rows, A_QK), F32),
            jax.ShapeDtypeStruct((t_rows, A_QK), F32),
            jax.ShapeDtypeStruct((t_rows, A_VW), F32),
        ],
        scratch_shapes=_gdn_scan_scratch(SUPER) + (2 * GDN_WIDTH - 1) * _gdn_scan_scratch(SUPER)[1:],
        compiler_params=_params(("arbitrary", "arbitrary")),
    )(proj_a, proj_a, proj_a, gates_col, gates_row, conv_w8, conv_w8, conv_w8, proj_a, norm_g8)


def _gdn_bwd_body(q_ref, k_ref, v_ref, gc_ref, gr_ref, tinv_ref, st_ref, dog_ref, oa_ref, z_ref, gn_ref,
                  xq_ref, xk_ref, xv_ref, wq_ref, wk_ref, wv_ref,
                  dxq_ref, dxk_ref, dxv_ref, dgc_ref, dz_ref, dgn_ref, dwq_ref, dwk_ref, dwv_ref,
                  ds_scr, cq_scr, ck_scr, cv_scr, *sets):
    head = pl.program_id(1)
    n_super = q_ref.shape[0] // SUPER
    ops = (sets[0:7], sets[7:14])
    res = (sets[14:21], sets[21:28])
    whole = pl.ds(0, SUPER)
    tn = functools.partial(lax.dot_general, dimension_numbers=(((0,), (0,)), ((), ())), preferred_element_type=F32)
    nt = functools.partial(lax.dot_general, dimension_numbers=(((1,), (1,)), ((), ())), preferred_element_type=F32)

    @pl.when(head == 0)
    def _():
        dgc_ref[...] = jnp.zeros_like(dgc_ref)

    @pl.when((head == 0) & (pl.program_id(0) == 0))
    def _():
        dgn_ref[...] = jnp.zeros_like(dgn_ref)

    carry = (cq_scr, ck_scr, cv_scr)
    for ref in carry + (dwq_ref, dwk_ref, dwv_ref):
        ref[...] = jnp.zeros_like(ref)

    def common_steps(i):
        rows = _super_rows(i)
        q, k, v = q_ref[rows, :], k_ref[rows, :], v_ref[rows, :]
        t = yield from _gdn_super_steps(q, k, v, gc_ref[rows, :], gr_ref[i], head, tinv_tall=tinv_ref[rows, :])
        return rows, q, k, v, t

    def stage_p(i, parity):
        rows, q, k, _, t = yield from common_steps(i)
        _store_scan_operands(whole, q, k, t, *ops[parity][:6])
        o, d_og, gain = oa_ref[rows, :], dog_ref[rows, :], gn_ref[0:1, :]
        r = lax.rsqrt(jnp.mean(o * o, axis=1, keepdims=True) + EPS)
        silu, dsilu = _silu_parts(z_ref[rows, :])
        xr = o * r
        d_on = d_og * silu
        dz_ref[rows, :] = (d_og * (xr * gain) * dsilu).astype(dz_ref.dtype)
        u = d_on * gain
        ops[parity][6][...] = r * u - xr * (r * r) * jnp.mean(u * o, axis=1, keepdims=True)
        dgn_ref[0:1, :] += jnp.sum(d_on * xr, axis=0, keepdims=True)

    def stage_s(i, parity):
        u_scr, w_scr, p_scr, qg_scr, ke_scr, gl_scr, do_scr = ops[parity]
        vn_scr, dvn_scr, dqg_scr, dw_scr, dkt_scr, sds_scr, dof_scr = res[parity]
        for j in reversed(range(GDN_GROUP)):
            n = i * GDN_GROUP + j
            local = pl.ds(j * CHUNK, CHUNK)
            ds_next = ds_scr[...]
            dsb = _bf(ds_next)
            sb = st_ref[n]
            s = sb.astype(F32)
            d_o = do_scr[local, :]
            dof_scr[local, :] = d_o
            d_ob = _bf(d_o)
            w_s = jnp.dot(w_scr[local, :], sb, preferred_element_type=F32)
            d_vn = tn(p_scr[local, :], d_ob) + jnp.dot(ke_scr[local, :], dsb, preferred_element_type=F32)
            d_qg = nt(d_ob, sb)
            qg_do = tn(qg_scr[local, :], d_ob)
            yield
            v_new = u_scr[local, :] - w_s
            d_vnb = _bf(d_vn)
            d_w = -nt(d_vnb, sb)
            d_kt = nt(_bf(v_new), dsb)
            w_dvn = tn(w_scr[local, :], d_vnb)
            yield
            vn_scr[local, :] = v_new
            dvn_scr[local, :] = d_vn
            dqg_scr[local, :] = d_qg
            dw_scr[local, :] = d_w
            dkt_scr[local, :] = d_kt
            sds = jnp.sum(jnp.sum(s * ds_next, axis=1, keepdims=True), axis=0, keepdims=True)
            sds_scr[local, :] = jnp.broadcast_to(sds, (CHUNK, 128))
            ds_scr[...] = qg_do + gl_scr[local, :][0:1, 0:1] * ds_next - w_dvn

    def conv_bwd(i, rows, x_ref, w_ref, dy, norm_scale, dx_ref, dw_ref, dc_above):
        x, w = x_ref[rows, :], w_ref[...]
        above = x_ref[pl.ds(pl.multiple_of(jnp.maximum(i * SUPER - 8, 0), 8), 8), :]
        ext = jnp.concatenate([jnp.where(i > 0, above, 0.0), x], axis=0)
        c = x * w[A_CONV - 1:A_CONV, :]
        for j in range(1, A_CONV):
            c = c + pltpu.roll(ext, j, 0)[8:, :] * w[A_CONV - 1 - j:A_CONV - j, :]
        sig = _sigmoid(c)
        a = c * sig
        if norm_scale is None:
            da = dy
        else:
            rn = lax.rsqrt(jnp.sum(a * a, axis=1, keepdims=True) + EPS)
            da = norm_scale * (rn * dy - a * (rn * rn * rn) * jnp.sum(dy * a, axis=1, keepdims=True))
        dc = da * (sig * (1.0 + c * (1.0 - sig)))
        ext_dc = jnp.concatenate([dc, dc_above[...]], axis=0)
        dc_above[...] = dc[0:8, :]
        dx = dc * w[A_CONV - 1:A_CONV, :]
        dw_ref[A_CONV - 1:A_CONV, :] += jnp.sum(dc * x, axis=0, keepdims=True)
        for j in range(1, A_CONV):
            dcs = pltpu.roll(ext_dc, SUPER + 8 - j, 0)[:SUPER, :]
            dx = dx + dcs * w[A_CONV - 1 - j:A_CONV - j, :]
            dw_ref[A_CONV - 1 - j:A_CONV - j, :] += jnp.sum(dcs * x, axis=0, keepdims=True)
        dx_ref[rows, :] = dx.astype(dx_ref.dtype)

    def stage_f(i, parity):
        vn_scr, dvn_scr, dqg_scr, dw_scr, dkt_scr, sds_scr, dof_scr = res[parity]
        rows, q, k, v, t = yield from common_steps(i)
        beta, gamma, decay, kb, e_tail = t["beta"], t["gamma"], t["decay"], t["kb"], t["e_tail"]
        row, col, lane, same = t["row"], t["col"], t["lane"], t["same"]
        d_o = dof_scr[...]
        v_new, d_vn = vn_scr[...], dvn_scr[...]
        d_qg, d_w, d_kt = dqg_scr[...], dw_scr[...], dkt_scr[...]
        gamma_last = jnp.exp(t["g_last"])

        d_p = jnp.where(same & (row >= col), _mm_nt(d_o, v_new), 0.0)
        d_ru = _mm_tn(t["tinv"], d_vn)
        d_rw = _mm_tn(t["tinv"], d_w)
        yield
        d_m = jnp.where(same & (row > col), -(_mm_nt(d_ru, t["u"]) + _mm_nt(d_rw, t["w"])), 0.0)
        yield

        x_p = d_p * decay
        y_m = d_m * decay
        d_kb = _mm(y_m, k) + d_rw * gamma
        d_q = _mm(x_p, k) + d_qg * gamma
        d_k = _mm_tn(x_p, q) + _mm_tn(y_m, kb) + d_kb * beta + d_kt * e_tail
        d_v = d_ru * beta
        conv_bwd(i, rows, xq_ref, wq_ref, d_q, A_DK ** -0.5, dxq_ref, dwq_ref, carry[0])
        conv_bwd(i, rows, xk_ref, wk_ref, d_k, 1.0, dxk_ref, dwk_ref, carry[1])
        conv_bwd(i, rows, xv_ref, wv_ref, d_v, None, dxv_ref, dwv_ref, carry[2])

        d_beta = (jnp.sum(d_ru * v, axis=1, keepdims=True)
                  + jnp.sum(d_kb * k, axis=1, keepdims=True))
        z = d_p * t["p"] + d_m * t["m"]
        eps_tail = jnp.sum(d_kt * k, axis=1, keepdims=True) * e_tail
        d_gc = (jnp.sum(z, axis=1, keepdims=True) - _colsum_as_col(z)
                + jnp.sum(d_qg * q, axis=1, keepdims=True) * gamma
                + jnp.sum(d_rw * kb, axis=1, keepdims=True) * gamma
                - eps_tail)
        d_glast = _block_sum(same, eps_tail) + gamma_last * sds_scr[...][:, 0:1]
        yield
        rcol = lax.broadcasted_iota(jnp.int32, (SUPER, 1), 0)
        d_gc = d_gc + jnp.where(rcol % CHUNK == CHUNK - 1, d_glast, 0.0)
        dgc_ref[rows, :] += (jnp.where(lane == head, d_beta, 0.0)
                             + jnp.where(lane == A_HEADS + head, d_gc, 0.0))

    last = n_super - 1
    _drain(stage_p(last, 1))
    ds_scr[...] = jnp.zeros_like(ds_scr)
    _interleave(stage_s(last, 1), stage_p(last - 1, 0))

    def pair(k, carry):
        i = last - 1 - 2 * k
        _interleave(stage_s(i, 0), stage_f(i + 1, 1), stage_p(i - 1, 1))
        _interleave(stage_s(i - 1, 1), stage_f(i, 0), stage_p(i - 2, 0))
        return carry

    lax.fori_loop(0, n_super // 2 - 1, pair, 0)
    _interleave(stage_s(0, 0), stage_f(1, 1))
    _drain(stage_f(0, 0))


def gdn_bwd(q, k, v, gates_col, gates_row, tinv, states, d_og, o, proj_a, conv_w8, norm_g8, n_seq):
    t_rows = q.shape[0]
    s_len = t_rows // n_seq
    n_chunks = s_len // CHUNK
    qk_spec = pl.BlockSpec((s_len, A_DK), lambda b, h: (b, h))
    v_spec = pl.BlockSpec((s_len, A_DV), lambda b, h: (b, h))
    gate_spec = pl.BlockSpec((s_len, 128), lambda b, h: (b, 0))
    gain_spec = pl.BlockSpec((8, A_DV), lambda b, h: (0, 0))
    dw_qk_spec = pl.BlockSpec((None, 8, A_DK), lambda b, h: (b, 0, h))
    dw_v_spec = pl.BlockSpec((None, 8, A_DV), lambda b, h: (b, 0, h))
    ops_set = _gdn_scan_scratch(SUPER)[1:] + [pltpu.VMEM((SUPER, A_DV), F32)]
    res_set = [pltpu.VMEM((SUPER, A_DV), F32), pltpu.VMEM((SUPER, A_DV), F32),
               pltpu.VMEM((SUPER, A_DK), F32), pltpu.VMEM((SUPER, A_DK), F32),
               pltpu.VMEM((SUPER, A_DK), F32), pltpu.VMEM((SUPER, 128), F32), pltpu.VMEM((SUPER, A_DV), F32)]
    return pl.pallas_call(
        _gdn_bwd_body, name="gdn_bwd", grid=(n_seq, A_HEADS),
        in_specs=_gdn_in_specs(s_len, s_len // SUPER, False) + [
            pl.BlockSpec((s_len, CHUNK), lambda b, h: (b * A_HEADS + h, 0)),
            pl.BlockSpec((None, n_chunks, A_DK, A_DV), lambda b, h: (b * A_HEADS + h, 0, 0, 0)),
            v_spec, v_spec,
            pl.BlockSpec((s_len, A_DV), lambda b, h: (b, A_Z_COL + h)),
            gain_spec,
            pl.BlockSpec((s_len, A_DK), lambda b, h: (b, h)),
            pl.BlockSpec((s_len, A_DK), lambda b, h: (b, A_K_COL + h)),
            pl.BlockSpec((s_len, A_DV), lambda b, h: (b, A_V_COL + h)),
            pl.BlockSpec((8, A_DK), lambda b, h: (0, h)),
            pl.BlockSpec((8, A_DK), lambda b, h: (0, A_K_COL + h)),
            pl.BlockSpec((8, A_DV), lambda b, h: (0, A_V_COL + h)),
        ],
        out_specs=[qk_spec, qk_spec, v_spec, gate_spec, v_spec, gain_spec, dw_qk_spec, dw_qk_spec, dw_v_spec],
        out_shape=[
            jax.ShapeDtypeStruct((t_rows, A_QK), BF16),
            jax.ShapeDtypeStruct((t_rows, A_QK), BF16),
            jax.ShapeDtypeStruct((t_rows, A_VW), BF16),
            jax.ShapeDtypeStruct((t_rows, 128), F32),
            jax.ShapeDtypeStruct((t_rows, A_VW), BF16),
            jax.ShapeDtypeStruct((8, A_DV), F32),
            jax.ShapeDtypeStruct((n_seq, 8, A_QK), F32),
            jax.ShapeDtypeStruct((n_seq, 8, A_QK), F32),
            jax.ShapeDtypeStruct((n_seq, 8, A_VW), F32),
        ],
        scratch_shapes=(_gdn_scan_scratch(SUPER)[:1]
                        + [pltpu.VMEM((8, A_DK), F32), pltpu.VMEM((8, A_DK), F32), pltpu.VMEM((8, A_DV), F32)]
                        + 2 * ops_set + 2 * res_set),
        compiler_params=_params(("arbitrary", "arbitrary")),
    )(q, k, v, gates_col, gates_row, tinv, states, d_og, o, proj_a, norm_g8,
      proj_a, proj_a, proj_a, conv_w8, conv_w8, conv_w8)


def _shift_down(x, j, rows):
    r = pltpu.roll(x, j, 0)
    return jnp.concatenate([jnp.where(rows[:8] >= j, r[:8], 0.0), r[8:]], axis=0)


def _shift_up(x, j, rows):
    n = x.shape[0]
    r = pltpu.roll(x, n - j, 0)
    return jnp.concatenate([r[:n - 8], jnp.where(rows[n - 8:] < n - j, r[n - 8:], 0.0)], axis=0)


def _conv_silu_norm(x, w, cb):
    rows = lax.broadcasted_iota(jnp.int32, (x.shape[0], 1), 0)
    c = x * w[A_CONV - 1:A_CONV, :]
    for j in range(1, A_CONV):
        c = c + _shift_down(x, j, rows) * w[A_CONV - 1 - j:A_CONV - j, :]
    sig = _sigmoid(c)
    a = c * sig
    rn = lax.rsqrt(jnp.sum(a * a, axis=1, keepdims=True) + EPS)
    is_q = cb < A_QK // 128
    is_qk = cb < 2 * A_QK // 128
    q_scale = jnp.where(is_q, A_DK ** -0.5, 1.0).astype(F32)
    return rows, c, sig, a, rn, is_qk, q_scale


def _a_pre_fwd_body(x_ref, w_ref, o_ref):
    cb = pl.program_id(1)
    _, _, _, a, rn, is_qk, q_scale = _conv_silu_norm(x_ref[...], w_ref[...], cb)
    o_ref[...] = a * jnp.where(is_qk, rn * q_scale, 1.0)


def a_pre_fwd(proj_a, conv_w8, n_seq):
    t_rows = proj_a.shape[0]
    s_len = t_rows // n_seq
    n_cb = (2 * A_QK + A_VW) // 128
    blk = pl.BlockSpec((s_len, 128), lambda b, c: (b, c))
    return pl.pallas_call(
        _a_pre_fwd_body, name="a_pre_fwd", grid=(n_seq, n_cb),
        in_specs=[blk, pl.BlockSpec((8, 128), lambda b, c: (0, c))],
        out_specs=blk,
        out_shape=jax.ShapeDtypeStruct((t_rows, n_cb * 128), F32),
        compiler_params=_params(("arbitrary", "arbitrary")),
    )(proj_a, conv_w8)


def _a_pre_bwd_body(x_ref, w_ref, dy_ref, dx_ref, dw_ref):
    b, cb = pl.program_id(1), pl.program_id(0)
    x, w = x_ref[...], w_ref[...]
    rows, c, sig, a, rn, is_qk, q_scale = _conv_silu_norm(x, w, cb)
    dy = dy_ref[...]
    da_n = q_scale * (rn * dy - a * (rn * rn * rn) * jnp.sum(dy * a, axis=1, keepdims=True))
    da = jnp.where(is_qk, da_n, dy)
    dc = da * (sig * (1.0 + c * (1.0 - sig)))
    @pl.when(b == 0)
    def _():
        dw_ref[...] = jnp.zeros_like(dw_ref)

    dx = dc * w[A_CONV - 1:A_CONV, :]
    dw_ref[A_CONV - 1:A_CONV, :] += jnp.sum(dc * x, axis=0, keepdims=True)
    for j in range(1, A_CONV):
        dcs = _shift_up(dc, j, rows)
        dx = dx + dcs * w[A_CONV - 1 - j:A_CONV - j, :]
        dw_ref[A_CONV - 1 - j:A_CONV - j, :] += jnp.sum(dcs * x, axis=0, keepdims=True)
    dx_ref[...] = dx.astype(dx_ref.dtype)


def a_pre_bwd(proj_a, conv_w8, dqkv_hm, n_seq):
    t_rows = proj_a.shape[0]
    s_len = t_rows // n_seq
    n_cb = (2 * A_QK + A_VW) // 128
    blk = pl.BlockSpec((s_len, 128), lambda c, b: (b, c))
    wblk = pl.BlockSpec((8, 128), lambda c, b: (0, c))
    per_head = A_HEAD_W // 128
    n_q = A_QK // 128

    def head_major(c, b):
        v_blk = jnp.maximum(c - 2 * n_q, 0)
        col = jnp.where(c < n_q, c * per_head,
                        jnp.where(c < 2 * n_q, (c - n_q) * per_head + 1,
                                  (v_blk // 2) * per_head + 2 + v_blk % 2))
        return (b, col)

    return pl.pallas_call(
        _a_pre_bwd_body, name="a_pre_bwd", grid=(n_cb, n_seq),
        in_specs=[blk, wblk, pl.BlockSpec((s_len, 128), head_major)],
        out_specs=[blk, wblk],
        out_shape=[jax.ShapeDtypeStruct((t_rows, n_cb * 128), BF16),
                   jax.ShapeDtypeStruct((8, n_cb * 128), F32)],
        compiler_params=_params(("arbitrary", "arbitrary")),
    )(proj_a, conv_w8, dqkv_hm)


GATE_TILE = 512


def _softplus(y):
    return jnp.maximum(y, 0.0) + jnp.log1p(jnp.exp(-jnp.abs(y)))


def _gate_values(x, prm):
    beta = _sigmoid(x)
    y = x + prm[1:2, :]
    neg_a = -jnp.exp(prm[0:1, :])
    g = neg_a * _softplus(y)
    return beta, y, neg_a, g


def _a_gates_fwd_body(x_ref, prm_ref, gc_ref, gr_ref):
    x = x_ref[...]
    tm = x.shape[0]
    beta, _, _, g = _gate_values(x, prm_ref[...])
    in_chunk = lax.broadcasted_iota(jnp.int32, (tm, 1), 0) % CHUNK
    s = 1
    while s < CHUNK:
        g = g + jnp.where(in_chunk >= s, pltpu.roll(g, s, 0), 0.0)
        s *= 2
    lane = lax.broadcasted_iota(jnp.int32, x.shape, 1)
    out = jnp.where(lane < A_HEADS, beta, jnp.where(lane < 2 * A_HEADS, g, 0.0))
    gc_ref[...] = out
    gr_ref[...] = out.T[0:2 * A_HEADS, :]


def a_gates_fwd(proj_a, prm, n_seq):
    t_rows = proj_a.shape[0]
    s_len = t_rows // n_seq
    tm = min(GATE_TILE, s_len)
    n_t = s_len // tm
    return pl.pallas_call(
        _a_gates_fwd_body, name="a_gates_fwd", grid=(n_seq, n_t),
        in_specs=[pl.BlockSpec((tm, 128), lambda b, i: (b * n_t + i, A_GATE_COL)),
                  pl.BlockSpec((8, 128), lambda b, i: (0, 0))],
        out_specs=[pl.BlockSpec((tm, 128), lambda b, i: (b * n_t + i, 0)),
                   pl.BlockSpec((None, 2 * A_HEADS, tm), lambda b, i: (b, 0, i))],
        out_shape=[jax.ShapeDtypeStruct((t_rows, 128), F32),
                   jax.ShapeDtypeStruct((n_seq, 2 * A_HEADS, s_len), F32)],
        compiler_params=_params(("arbitrary", "arbitrary")),
    )(proj_a, prm)


def _a_gates_bwd_body(x_ref, prm_ref, dgc_ref, dx_ref, dprm_ref):
    first = (pl.program_id(0) == 0) & (pl.program_id(1) == 0)
    x = x_ref[...]
    tm = x.shape[0]
    beta, y, neg_a, g = _gate_values(x, prm_ref[...])
    d = dgc_ref[...]
    in_chunk = lax.broadcasted_iota(jnp.int32, (tm, 1), 0) % CHUNK
    dg = d
    s = 1
    while s < CHUNK:
        dg = dg + jnp.where(in_chunk < CHUNK - s, pltpu.roll(dg, tm - s, 0), 0.0)
        s *= 2
    lane = lax.broadcasted_iota(jnp.int32, x.shape, 1)
    is_decay = (lane >= A_HEADS) & (lane < 2 * A_HEADS)
    d_alogit = jnp.where(is_decay, dg * neg_a * _sigmoid(y), 0.0)
    dx_ref[...] = jnp.where(lane < A_HEADS, d * beta * (1.0 - beta), d_alogit).astype(dx_ref.dtype)

    @pl.when(first)
    def _():
        dprm_ref[...] = jnp.zeros_like(dprm_ref)

    dprm_ref[0:1, :] += jnp.sum(jnp.where(is_decay, dg * g, 0.0), axis=0, keepdims=True)
    dprm_ref[1:2, :] += jnp.sum(d_alogit, axis=0, keepdims=True)


def a_gates_bwd(proj_a, prm, dgates_col, n_seq):
    t_rows = proj_a.shape[0]
    s_len = t_rows // n_seq
    tm = min(GATE_TILE, s_len)
    n_t = s_len // tm
    return pl.pallas_call(
        _a_gates_bwd_body, name="a_gates_bwd", grid=(n_seq, n_t),
        in_specs=[pl.BlockSpec((tm, 128), lambda b, i: (b * n_t + i, A_GATE_COL)),
                  pl.BlockSpec((8, 128), lambda b, i: (0, 0)),
                  pl.BlockSpec((tm, 128), lambda b, i: (b * n_t + i, 0))],
        out_specs=[pl.BlockSpec((tm, 128), lambda b, i: (b * n_t + i, 0)),
                   pl.BlockSpec((8, 128), lambda b, i: (0, 0))],
        out_shape=[jax.ShapeDtypeStruct((t_rows, 128), BF16),
                   jax.ShapeDtypeStruct((8, 128), F32)],
        compiler_params=_params(("arbitrary", "arbitrary")),
    )(proj_a, prm, dgates_col)


ROW_TILE = 512
A_Z_COL = (2 * A_QK + A_VW) // A_DV


def _silu_parts(z):
    sig = _sigmoid(z)
    return z * sig, sig * (1.0 + z * (1.0 - sig))


def _a_post_fwd_body(o_ref, z_ref, g_ref, og_ref):
    o = o_ref[...]
    r = lax.rsqrt(jnp.mean(o * o, axis=1, keepdims=True) + EPS)
    silu, _ = _silu_parts(z_ref[...])
    og_ref[...] = ((o * r * g_ref[0:1, :]) * silu).astype(og_ref.dtype)


def a_post_fwd(o, proj_a, norm_g8):
    t_rows = o.shape[0]
    tm = min(ROW_TILE, t_rows)
    blk = pl.BlockSpec((tm, A_DV), lambda i, h: (i, h))
    return pl.pallas_call(
        _a_post_fwd_body, name="a_post_fwd", grid=(t_rows // tm, A_HEADS),
        in_specs=[blk, pl.BlockSpec((tm, A_DV), lambda i, h: (i, A_Z_COL + h)),
                  pl.BlockSpec((8, A_DV), lambda i, h: (0, 0))],
        out_specs=blk,
        out_shape=jax.ShapeDtypeStruct((t_rows, A_VW), BF16),
        compiler_params=_params(("arbitrary", "arbitrary")),
    )(o, proj_a, norm_g8)


def _a_post_bwd_body(o_ref, z_ref, g_ref, dog_ref, do_ref, dz_ref, dg_ref):
    first = (pl.program_id(0) == 0) & (pl.program_id(1) == 0)
    o, z, d_og = o_ref[...], z_ref[...], dog_ref[...]
    gain = g_ref[0:1, :]
    r = lax.rsqrt(jnp.mean(o * o, axis=1, keepdims=True) + EPS)
    silu, dsilu = _silu_parts(z)
    xr = o * r
    d_on = d_og * silu
    dz_ref[...] = (d_og * (xr * gain) * dsilu).astype(dz_ref.dtype)
    u = d_on * gain
    do_ref[...] = r * u - xr * (r * r) * jnp.mean(u * o, axis=1, keepdims=True)

    @pl.when(first)
    def _():
        dg_ref[...] = jnp.zeros_like(dg_ref)

    dg_ref[0:1, :] += jnp.sum(d_on * xr, axis=0, keepdims=True)


def a_post_bwd(o, proj_a, norm_g8, d_og):
    t_rows = o.shape[0]
    tm = min(ROW_TILE, t_rows)
    blk = pl.BlockSpec((tm, A_DV), lambda i, h: (i, h))
    gblk = pl.BlockSpec((8, A_DV), lambda i, h: (0, 0))
    return pl.pallas_call(
        _a_post_bwd_body, name="a_post_bwd", grid=(t_rows // tm, A_HEADS),
        in_specs=[blk, pl.BlockSpec((tm, A_DV), lambda i, h: (i, A_Z_COL + h)), gblk, blk],
        out_specs=[blk, blk, gblk],
        out_shape=[jax.ShapeDtypeStruct((t_rows, A_VW), F32),
                   jax.ShapeDtypeStruct((t_rows, A_VW), BF16),
                   jax.ShapeDtypeStruct((8, A_DV), F32)],
        compiler_params=_params(("arbitrary", "arbitrary")),
    )(o, proj_a, norm_g8, d_og)


NEG_BIG = -1e30
ATT_SCALE = B_DH ** -0.5


def _swap_rope_halves(x):
    src = lax.broadcasted_iota(jnp.int32, (B_DH, B_DH), 0)
    dst = lax.broadcasted_iota(jnp.int32, (B_DH, B_DH), 1)
    pick = ((dst < ROPE_HALF) & (src == dst + ROPE_HALF)) | (
        (dst >= ROPE_HALF) & (dst < ROPE_DIMS) & (src == dst - ROPE_HALF))
    return jnp.dot(_bf(x), pick.astype(BF16), preferred_element_type=F32)


def _norm_rope(x, gain, cos_t, sin_t):
    r = lax.rsqrt(jnp.mean(x * x, axis=1, keepdims=True) + EPS)
    xn = x * r * gain
    return xn * cos_t + _swap_rope_halves(xn) * sin_t, r


def _norm_rope_bwd(x, r, gain, cos_t, sin_t, dy):
    d_xn = dy * cos_t + _swap_rope_halves(dy * sin_t)
    xr = x * r
    u = d_xn * gain
    dx = r * u - xr * (r * r) * jnp.mean(u * x, axis=1, keepdims=True)
    return dx, jnp.sum(d_xn * xr, axis=0, keepdims=True)


def _stream_rows(idx, dilation, s_len):
    nb = s_len // dilation // B_BLOCK
    r = idx // nb
    m = idx % nb
    cur = r + m * (B_BLOCK * dilation)
    prev = r + jnp.maximum(m - 1, 0) * (B_BLOCK * dilation)
    return cur, prev, m > 0


def _rows(start, dilation):
    if dilation == 1:
        return pl.ds(start, B_BLOCK)
    return pl.ds(start, B_BLOCK, stride=dilation)


ATT_UNROLL = 8


def _band_mask(has_prev):
    qi = lax.broadcasted_iota(jnp.int32, (B_BLOCK, 2 * B_BLOCK), 0)
    kj = lax.broadcasted_iota(jnp.int32, (B_BLOCK, 2 * B_BLOCK), 1)
    return ((kj < B_BLOCK) & (kj >= qi) & has_prev) | ((kj >= B_BLOCK) & (kj - B_BLOCK <= qi))


def _block_scores(qb, kc, kp, has_prev):
    qi = lax.broadcasted_iota(jnp.int32, (B_BLOCK, B_BLOCK), 0)
    kj = lax.broadcasted_iota(jnp.int32, (B_BLOCK, B_BLOCK), 1)
    s_c = jnp.where(qi >= kj, _mm_nt(qb, kc) * ATT_SCALE, NEG_BIG)
    s_p = jnp.where((kj >= qi) & has_prev, _mm_nt(qb, kp) * ATT_SCALE, NEG_BIG)
    return s_c, s_p


def _attn_fwd_body(qkv_ref, z_ref, cos_ref, sin_ref, gain_ref, og_ref, o_ref, lse_ref,
                   qn_scr, kn_scr, og_scr, lg_scr):
    head, grp = pl.program_id(1), pl.program_id(2)
    s_len = z_ref.shape[0]
    n_blocks = s_len // B_BLOCK
    cos_t, sin_t = cos_ref[...], sin_ref[...]

    for gi, dil in enumerate(B_DILATIONS):
        @pl.when(grp == gi)
        def _(gi=gi, dil=dil):
            qn_scr[...], _ = _norm_rope(qkv_ref[0], gain_ref[gi:gi + 1, :], cos_t, sin_t)
            kn_scr[...], _ = _norm_rope(qkv_ref[1], gain_ref[B_GROUPS + gi:B_GROUPS + gi + 1, :], cos_t, sin_t)

            ones = jnp.ones((2 * B_BLOCK, B_DH), BF16)

            def blocks(it, carry):
                scored = []
                for j in range(ATT_UNROLL):
                    cur, prev, has_prev = _stream_rows(it * ATT_UNROLL + j, dil, s_len)
                    rc, rp = _rows(cur, dil), _rows(prev, dil)
                    k2 = jnp.concatenate([kn_scr[rp, :], kn_scr[rc, :]], axis=0)
                    scored.append((rc, rp, has_prev, _mm_nt(qn_scr[rc, :], k2) * ATT_SCALE))
                summed = []
                for rc, rp, has_prev, s in scored:
                    s = jnp.where(_band_mask(has_prev), s, NEG_BIG)
                    mx = jnp.max(s, axis=1, keepdims=True)
                    v2 = jnp.concatenate([qkv_ref.at[2][rp, :], qkv_ref.at[2][rc, :]], axis=0)
                    acc = jnp.dot(_bf(jnp.exp(s - mx)), jnp.concatenate([_bf(v2), ones], axis=1),
                                  preferred_element_type=F32)
                    summed.append((rc, mx, acc))
                for rc, mx, acc in summed:
                    den = acc[:, B_DH:B_DH + 1]
                    og_scr.at[gi][rc, :] = acc[:, :B_DH] / den
                    lg_scr.at[gi][rc, :] = jnp.broadcast_to(mx + jnp.log(den), (B_BLOCK, B_DH))
                return carry

            lax.fori_loop(0, n_blocks // ATT_UNROLL, blocks, 0)

    @pl.when(grp == B_GROUPS - 1)
    def _():
        l0, l1, l2 = lg_scr[0], lg_scr[1], lg_scr[2]
        mx = jnp.maximum(jnp.maximum(l0, l1), l2)
        w0, w1, w2 = jnp.exp(l0 - mx), jnp.exp(l1 - mx), jnp.exp(l2 - mx)
        den = w0 + w1 + w2
        o = (w0 * og_scr[0] + w1 * og_scr[1] + w2 * og_scr[2]) / den
        silu, _ = _silu_parts(z_ref[...])
        o_ref[...] = o
        og_ref[...] = (o * silu).astype(og_ref.dtype)
        @pl.when(head == 0)
        def _():
            lse_ref[...] = jnp.zeros_like(lse_ref)

        lane = lax.broadcasted_iota(jnp.int32, o.shape, 1)
        lse_ref[...] = jnp.where(lane == head, mx + jnp.log(den), lse_ref[...])


def attn_fwd(proj_b, cos_t, sin_t, gains8, n_seq):
    t_rows = proj_b.shape[1]
    s_len = t_rows // n_seq
    head_blk = pl.BlockSpec((s_len, B_DH), lambda b, h, g: (b, h))
    seq_blk = pl.BlockSpec((s_len, 128), lambda b, h, g: (b, 0))
    return pl.pallas_call(
        _attn_fwd_body, name="attn_fwd", grid=(n_seq, B_HEADS, B_GROUPS),
        in_specs=[
            pl.BlockSpec((3, s_len, B_DH), lambda b, h, g: (g, b, h)),
            pl.BlockSpec((None, s_len, B_DH), lambda b, h, g: (B_PIECES - 1, b, h)),
            seq_blk, seq_blk,
            pl.BlockSpec((8, 128), lambda b, h, g: (0, 0)),
        ],
        out_specs=[head_blk, head_blk, seq_blk],
        out_shape=[jax.ShapeDtypeStruct((t_rows, B_W), BF16),
                   jax.ShapeDtypeStruct((t_rows, B_W), F32),
                   jax.ShapeDtypeStruct((t_rows, 128), F32)],
        scratch_shapes=[pltpu.VMEM((s_len, B_DH), F32), pltpu.VMEM((s_len, B_DH), F32),
                        pltpu.VMEM((B_GROUPS, s_len, B_DH), F32), pltpu.VMEM((B_GROUPS, s_len, B_DH), F32)],
        compiler_params=_params(("arbitrary", "arbitrary", "arbitrary")),
    )(proj_b, proj_b, cos_t, sin_t, gains8)


def _attn_bwd_body(qkv_ref, z_ref, cos_ref, sin_ref, gain_ref, dog_ref, o_ref, lse_ref,
                   dqkv_ref, dz_ref, dgain_ref,
                   qn_scr, kn_scr, dqn_scr, dkn_scr, do_scr, dl_scr, ls_scr, dv_scr):
    head, grp = pl.program_id(1), pl.program_id(2)
    first = (pl.program_id(0) == 0) & (head == 0) & (grp == 0)
    s_len = z_ref.shape[0]
    n_blocks = s_len // B_BLOCK
    cos_t, sin_t = cos_ref[...], sin_ref[...]

    @pl.when(first)
    def _():
        dgain_ref[...] = jnp.zeros_like(dgain_ref)

    @pl.when(grp == 0)
    def _():
        d_og, o = dog_ref[...], o_ref[...]
        silu, dsilu = _silu_parts(z_ref[...])
        d_o = d_og * silu
        dz_ref[...] = (d_og * o * dsilu).astype(dz_ref.dtype)
        do_scr[...] = d_o
        dl_scr[...] = jnp.broadcast_to(jnp.sum(d_o * o, axis=1, keepdims=True), o.shape)
        lane = lax.broadcasted_iota(jnp.int32, o.shape, 1)
        ls_scr[...] = jnp.broadcast_to(
            jnp.sum(jnp.where(lane == head, lse_ref[...], 0.0), axis=1, keepdims=True), o.shape)

    for gi, dil in enumerate(B_DILATIONS):
        @pl.when(grp == gi)
        def _(gi=gi, dil=dil):
            q_raw, k_raw = qkv_ref[0], qkv_ref[1]
            gq = gain_ref[gi:gi + 1, :]
            gk = gain_ref[B_GROUPS + gi:B_GROUPS + gi + 1, :]
            qn_scr[...], rq = _norm_rope(q_raw, gq, cos_t, sin_t)
            kn_scr[...], rk = _norm_rope(k_raw, gk, cos_t, sin_t)
            def blocks(it, carry):
                scored = []
                for j in range(ATT_UNROLL):
                    cur, prev, has_prev = _stream_rows(it * ATT_UNROLL + j, dil, s_len)
                    rc, rp = _rows(cur, dil), _rows(prev, dil)
                    qb, d_ob = _bf(qn_scr[rc, :]), _bf(do_scr[rc, :])
                    k2 = _bf(jnp.concatenate([kn_scr[rp, :], kn_scr[rc, :]], axis=0))
                    v2 = _bf(jnp.concatenate([qkv_ref.at[2][rp, :], qkv_ref.at[2][rc, :]], axis=0))
                    scored.append((rc, rp, has_prev, qb, d_ob, k2,
                                   _mm_nt(qb, k2) * ATT_SCALE, _mm_nt(d_ob, v2)))
                grads = []
                for rc, rp, has_prev, qb, d_ob, k2, s, d_p in scored:
                    p = jnp.exp(jnp.where(_band_mask(has_prev), s - ls_scr[rc, :][:, 0:1], NEG_BIG))
                    ds = _bf(p * (d_p - dl_scr[rc, :][:, 0:1]))
                    grads.append((rc, rp, has_prev,
                                  _mm(ds, k2) * ATT_SCALE, _mm_tn(ds, qb) * ATT_SCALE, _mm_tn(_bf(p), d_ob)))
                for j, (rc, rp, has_prev, dq, dk2, dv2) in enumerate(grads):
                    dqn_scr[rc, :] = dq
                    if j == 0:
                        @pl.when(has_prev)
                        def _():
                            dkn_scr[rp, :] += dk2[:B_BLOCK]
                            dv_scr[rp, :] += dv2[:B_BLOCK]
                    if j + 1 < ATT_UNROLL:
                        dkn_scr[rc, :] = dk2[B_BLOCK:] + grads[j + 1][4][:B_BLOCK]
                        dv_scr[rc, :] = dv2[B_BLOCK:] + grads[j + 1][5][:B_BLOCK]
                    else:
                        dkn_scr[rc, :] = dk2[B_BLOCK:]
                        dv_scr[rc, :] = dv2[B_BLOCK:]
                return carry

            lax.fori_loop(0, n_blocks // ATT_UNROLL, blocks, 0)
            dq, dgq = _norm_rope_bwd(q_raw, rq, gq, cos_t, sin_t, dqn_scr[...])
            dk, dgk = _norm_rope_bwd(k_raw, rk, gk, cos_t, sin_t, dkn_scr[...])
            dqkv_ref[0] = dq.astype(dqkv_ref.dtype)
            dqkv_ref[1] = dk.astype(dqkv_ref.dtype)
            dqkv_ref[2] = dv_scr[...].astype(dqkv_ref.dtype)
            dgain_ref[gi:gi + 1, :] += dgq
            dgain_ref[B_GROUPS + gi:B_GROUPS + gi + 1, :] += dgk


def attn_bwd(proj_b, cos_t, sin_t, gains8, d_og, o, lse, n_seq):
    t_rows = proj_b.shape[1]
    s_len = t_rows // n_seq
    head_blk = pl.BlockSpec((s_len, B_DH), lambda b, h, g: (b, h))
    seq_blk = pl.BlockSpec((s_len, 128), lambda b, h, g: (b, 0))
    grp_blk = pl.BlockSpec((3, s_len, B_DH), lambda b, h, g: (g, b, h))
    gain_blk = pl.BlockSpec((8, 128), lambda b, h, g: (0, 0))
    return pl.pallas_call(
        _attn_bwd_body, name="attn_bwd", grid=(n_seq, B_HEADS, B_GROUPS),
        in_specs=[
            grp_blk,
            pl.BlockSpec((None, s_len, B_DH), lambda b, h, g: (B_PIECES - 1, b, h)),
            seq_blk, seq_blk, gain_blk, head_blk, head_blk, seq_blk,
        ],
        out_specs=[grp_blk, head_blk, gain_blk],
        out_shape=[jax.ShapeDtypeStruct((3 * B_GROUPS, t_rows, B_W), BF16),
                   jax.ShapeDtypeStruct((t_rows, B_W), BF16),
                   jax.ShapeDtypeStruct((8, 128), F32)],
        scratch_shapes=[pltpu.VMEM((s_len, B_DH), F32) for _ in range(8)],
        compiler_params=_params(("arbitrary", "arbitrary", "arbitrary")),
    )(proj_b, proj_b, cos_t, sin_t, gains8, d_og, o, lse)


def rope_tables(positions):
    inv_freq = ROPE_THETA ** (-jnp.arange(0, ROPE_DIMS, 2, dtype=F32) / ROPE_DIMS)
    ang = positions.astype(F32)[:, None] * inv_freq
    cos, sin = jnp.cos(ang), jnp.sin(ang)
    t_rows = positions.shape[0]
    rest = B_DH - ROPE_DIMS
    cos_t = jnp.concatenate([cos, cos, jnp.ones((t_rows, rest), F32)], axis=1)
    sin_t = jnp.concatenate([-sin, sin, jnp.zeros((t_rows, rest), F32)], axis=1)
    return cos_t, sin_t


def _rms_fwd_body(x_ref, g_ref, *rest, layer):
    h_ref = rest[-1]
    x = x_ref[...]
    r = lax.rsqrt(jnp.mean(x * x, axis=1, keepdims=True) + EPS)
    h_ref[...] = (x * r * g_ref[layer:layer + 1, :]).astype(h_ref.dtype)


def rms_fwd(x, gains8, layer, after=None):
    t_rows, d = x.shape
    tm = min(ROW_TILE, t_rows)
    in_specs = [pl.BlockSpec((tm, d), lambda i: (i, 0)), pl.BlockSpec((8, d), lambda i: (0, 0))]
    args = [x, gains8]
    if after is not None:
        in_specs.append(HBM_SPEC)
        args.append(after)
    return pl.pallas_call(
        functools.partial(_rms_fwd_body, layer=layer), name=f"rms_fwd_{layer}", grid=(t_rows // tm,),
        in_specs=in_specs,
        out_specs=pl.BlockSpec((tm, d), lambda i: (i, 0)),
        out_shape=jax.ShapeDtypeStruct((t_rows, d), BF16),
        compiler_params=_params(("arbitrary",)),
    )(*args)


def _rms_bwd_body(x_ref, g_ref, dh_ref, res_ref, dx_ref, dg_ref, *, layer):
    x, dh = x_ref[...], dh_ref[...]
    r = lax.rsqrt(jnp.mean(x * x, axis=1, keepdims=True) + EPS)
    xr = x * r
    u = dh * g_ref[layer:layer + 1, :]
    dx_ref[...] = res_ref[...] + r * u - xr * (r * r) * jnp.mean(u * x, axis=1, keepdims=True)

    @pl.when(pl.program_id(0) == 0)
    def _():
        dg_ref[...] = jnp.zeros_like(dg_ref)

    dg_ref[0:1, :] += jnp.sum(dh * xr, axis=0, keepdims=True)


def rms_bwd(x, gains8, layer, dh, d_res):
    t_rows, d = x.shape
    tm = min(ROW_TILE, t_rows)
    blk = pl.BlockSpec((tm, d), lambda i: (i, 0))
    gblk = pl.BlockSpec((8, d), lambda i: (0, 0))
    return pl.pallas_call(
        functools.partial(_rms_bwd_body, layer=layer), name=f"rms_bwd_{layer}", grid=(t_rows // tm,),
        in_specs=[blk, gblk, blk, blk],
        out_specs=[blk, gblk],
        out_shape=[jax.ShapeDtypeStruct((t_rows, d), F32), jax.ShapeDtypeStruct((8, d), F32)],
        compiler_params=_params(("arbitrary",)),
    )(x, gains8, dh, d_res)


def _piece_col(p):
    return jnp.where(p < 3 * B_GROUPS, (p % 3) * B_GROUPS + p // 3, 3 * B_GROUPS)


def _mm_nn_body(a_ref, w_ref, *rest, has_res):
    o_ref = rest[-1]
    acc = jnp.dot(a_ref[...], w_ref[...], preferred_element_type=F32)
    if has_res:
        acc = acc + rest[0][...]
    o_ref[...] = acc


def mm_nn(a, w, residual=None, *, tn, name):
    m, k = a.shape
    n = w.shape[1]
    tm = min(ROW_TILE, m)
    in_specs = [pl.BlockSpec((tm, k), lambda j, i: (i, 0)), pl.BlockSpec((k, tn), lambda j, i: (0, j))]
    args = [a, w]
    if residual is not None:
        in_specs.append(pl.BlockSpec((tm, tn), lambda j, i: (i, j)))
        args.append(residual)
    return pl.pallas_call(
        functools.partial(_mm_nn_body, has_res=residual is not None), name=name, grid=(n // tn, m // tm),
        in_specs=in_specs,
        out_specs=pl.BlockSpec((tm, tn), lambda j, i: (i, j)),
        out_shape=jax.ShapeDtypeStruct((m, n), F32),
        compiler_params=_params(("arbitrary", "arbitrary")),
    )(*args)


def mm_nn_pieces(a, w, *, name):
    m, k = a.shape
    tm = min(ROW_TILE, m)
    return pl.pallas_call(
        functools.partial(_mm_nn_body, has_res=False), name=name, grid=(B_PIECES, m // tm),
        in_specs=[pl.BlockSpec((tm, k), lambda p, i: (i, 0)),
                  pl.BlockSpec((k, B_W), lambda p, i: (0, _piece_col(p)))],
        out_specs=pl.BlockSpec((None, tm, B_W), lambda p, i: (p, i, 0)),
        out_shape=jax.ShapeDtypeStruct((B_PIECES, m, B_W), F32),
        compiler_params=_params(("arbitrary", "arbitrary")),
    )(a, w)


NT_ROW_TILE = 1024


def _mm_nt_body(g_ref, w_ref, *rest, has_init):
    o_ref = rest[-1]
    j = pl.program_id(1)
    part = lax.dot_general(_bf(g_ref[...]), w_ref[...], (((1,), (1,)), ((), ())), preferred_element_type=F32)

    @pl.when(j == 0)
    def _():
        o_ref[...] = part + rest[0][...] if has_init else part

    @pl.when(j > 0)
    def _():
        o_ref[...] += part


def mm_nt(g, w, init=None, *, tn, col_off=0, name, after=None):
    m, n = g.shape
    k = w.shape[0]
    tm = min(NT_ROW_TILE, m)
    in_specs = [pl.BlockSpec((tm, tn), lambda i, j: (i, j)),
                pl.BlockSpec((k, tn), lambda i, j: (0, col_off + j))]
    args = [g, w]
    if init is not None:
        in_specs.append(pl.BlockSpec((tm, k), lambda i, j: (i, 0)))
        args.append(init)
    if after is not None:
        in_specs.append(HBM_SPEC)
        args.append(after)
    return pl.pallas_call(
        functools.partial(_mm_nt_body, has_init=init is not None), name=name, grid=(m // tm, n // tn),
        in_specs=in_specs,
        out_specs=pl.BlockSpec((tm, k), lambda i, j: (i, 0)),
        out_shape=jax.ShapeDtypeStruct((m, k), F32),
        compiler_params=_params(("arbitrary", "arbitrary")),
    )(*args)


def mm_nt_pieces(g9, w, *, name):
    n_p, m, _ = g9.shape
    k = w.shape[0]
    tm = min(NT_ROW_TILE, m)
    return pl.pallas_call(
        functools.partial(_mm_nt_body, has_init=False), name=name, grid=(m // tm, n_p),
        in_specs=[pl.BlockSpec((None, tm, B_W), lambda i, p: (p, i, 0)),
                  pl.BlockSpec((k, B_W), lambda i, p: (0, _piece_col(p)))],
        out_specs=pl.BlockSpec((tm, k), lambda i, p: (i, 0)),
        out_shape=jax.ShapeDtypeStruct((m, k), F32),
        compiler_params=_params(("arbitrary", "arbitrary")),
    )(g9, w)


def _mm_tn_body(a_ref, g_ref, o_ref):
    o_ref[...] = lax.dot_general(a_ref[...], _bf(g_ref[...]), (((0,), (0,)), ((), ())),
                                 preferred_element_type=F32).astype(o_ref.dtype)


def mm_tn(a, g, *, tn, out_dtype, name):
    m, k = a.shape
    n = g.shape[1]
    return pl.pallas_call(
        _mm_tn_body, name=name, grid=(n // tn,),
        in_specs=[pl.BlockSpec((m, k), lambda j: (0, 0)), pl.BlockSpec((m, tn), lambda j: (0, j))],
        out_specs=pl.BlockSpec((k, tn), lambda j: (0, j)),
        out_shape=jax.ShapeDtypeStruct((k, n), out_dtype),
        compiler_params=_params(("arbitrary",)),
    )(a, g)


B_UNIT = 256
B_IN_COLS = B_PIECES * B_W
B_SHARD_UNITS = B_IN_COLS // N_DEV // B_UNIT


def mm_tn_b_in(a, g9, gz, *, out_dtype, name):
    m, k = a.shape
    per_piece = B_W // B_UNIT
    n_units = B_IN_COLS // B_UNIT

    def g_map(u):
        nat = jnp.minimum(u // per_piece, 3 * B_GROUPS - 1)
        piece = (nat % B_GROUPS) * 3 + nat // B_GROUPS
        return (piece, 0, u % per_piece)

    def body(a_ref, g_ref, z_ref, o_ref):
        u = pl.program_id(0)

        @pl.when(u < 3 * B_GROUPS * per_piece)
        def _():
            _mm_tn_body(a_ref, g_ref, o_ref)

        @pl.when(u >= 3 * B_GROUPS * per_piece)
        def _():
            _mm_tn_body(a_ref, z_ref, o_ref)

    return pl.pallas_call(
        body, name=name, grid=(n_units,),
        in_specs=[pl.BlockSpec((m, k), lambda u: (0, 0)),
                  pl.BlockSpec((None, m, B_UNIT), g_map),
                  pl.BlockSpec((m, B_UNIT), lambda u: (0, jnp.where(u < 3 * B_GROUPS * per_piece, 0, u % per_piece)))],
        out_specs=pl.BlockSpec((None, k, B_UNIT), lambda u: (u // B_SHARD_UNITS, 0, u % B_SHARD_UNITS)),
        out_shape=jax.ShapeDtypeStruct((N_DEV, k, B_IN_COLS // N_DEV), out_dtype),
        compiler_params=_params(("arbitrary",)),
    )(a, g9, gz)


def _loss_body(y_ref, t_ref, dy_ref, loss_ref, acc):
    i = pl.program_id(0)
    d = y_ref.shape[1]
    err = y_ref[...] - t_ref[...]
    dy_ref[...] = err * (1.0 / d)

    @pl.when(i == 0)
    def _():
        acc[...] = jnp.zeros_like(acc)

    acc[...] += jnp.sum(err * err, axis=0, keepdims=True)

    @pl.when(i == pl.num_programs(0) - 1)
    def _():
        total = jnp.sum(acc[...], axis=1, keepdims=True) * (0.5 / d)
        loss_ref[...] = jnp.broadcast_to(total, loss_ref.shape)


def loss_head(y, target):
    t_rows, d = y.shape
    tm = min(ROW_TILE, t_rows)
    blk = pl.BlockSpec((tm, d), lambda i: (i, 0))
    return pl.pallas_call(
        _loss_body, name="loss_head", grid=(t_rows // tm,),
        in_specs=[blk, blk],
        out_specs=[blk, pl.BlockSpec((8, 128), lambda i: (0, 0))],
        out_shape=[jax.ShapeDtypeStruct((t_rows, d), F32), jax.ShapeDtypeStruct((8, 128), F32)],
        scratch_shapes=[pltpu.VMEM((1, d), F32)],
        compiler_params=_params(("arbitrary",)),
    )(y, target)


def _adamw_body(p_ref, w_ref, m_ref, v_ref, g_ref, d_ref, nm_ref, nv_ref):
    g = p_ref[0].astype(F32)
    for s in range(1, N_DEV):
        g = g + p_ref[s].astype(F32)
    w = w_ref[...]
    m = ADAM_B1 * m_ref[...] + (1.0 - ADAM_B1) * g
    v = ADAM_B2 * v_ref[...] + (1.0 - ADAM_B2) * (g * g)
    m_hat = m / (1.0 - ADAM_B1 ** ADAM_STEP)
    v_hat = v / (1.0 - ADAM_B2 ** ADAM_STEP)
    g_ref[...] = g
    d_ref[...] = -ADAM_LR * (m_hat / (jnp.sqrt(v_hat) + ADAM_EPS) + ADAM_WD * w)
    nm_ref[...] = m
    nv_ref[...] = v


def adamw(parts, w, m, v, *, name):
    _, r, c = w.shape
    tr = r if r <= 256 else 256
    blk = pl.BlockSpec((None, tr, c), lambda i: (0, i, 0))
    out = jax.ShapeDtypeStruct((1, r, c), F32)
    return pl.pallas_call(
        _adamw_body, name=name, grid=(r // tr,),
        in_specs=[pl.BlockSpec((N_DEV, tr, c), lambda i: (0, i, 0)), blk, blk, blk],
        out_specs=[blk, blk, blk, blk],
        out_shape=[out, out, out, out],
        compiler_params=_params(("arbitrary",)),
    )(parts, w, m, v)


MESH_ID = pl.DeviceIdType.MESH
HBM_SPEC = pl.BlockSpec(memory_space=pl.ANY)


def _my_place():
    return lax.axis_index("x"), lax.axis_index("y"), lax.axis_index("c")


def _flat(x, y, c):
    return 4 * x + 2 * y + c


def _all_gather_body(*refs, n):
    ins, outs = refs[:n], refs[n:2 * n]
    send_sems, recv_sems, local_sems = refs[2 * n:]
    x, y, c = _my_place()
    me, sibling = (x, y, c), (x, y, 1 - c)
    chips = [(1 - x, y), (x, 1 - y), (1 - x, 1 - y)]
    pending = []
    for a in range(n):
        src, out = ins[a], outs[a]

        def copy(k, block, to, from_input=False, a=a, src=src, out=out):
            slot = out.at[_flat(*block)]
            return pltpu.make_async_remote_copy(
                src_ref=src if from_input else slot, dst_ref=slot,
                send_sem=send_sems.at[7 * a + k], recv_sem=recv_sems.at[7 * a + k],
                device_id=to, device_id_type=MESH_ID)

        mine = pltpu.make_async_copy(src, out.at[_flat(*me)], local_sems.at[a])
        mine.start()
        first = [copy(0, me, sibling, True)] + [copy(1 + j, me, (*chip, c), True) for j, chip in enumerate(chips)]
        for cp in first:
            cp.start()
        pending.append((copy, mine, first))
    for copy, mine, first in pending:
        passed = [copy(4 + j, (*chip, c), sibling) for j, chip in enumerate(chips)]
        for j, chip in enumerate(chips):
            copy(1 + j, (*chip, c), me).wait_recv()
            passed[j].start()
        copy(0, sibling, me).wait_recv()
        for j, chip in enumerate(chips):
            copy(4 + j, (*chip, 1 - c), me).wait_recv()
        for cp in first + passed:
            cp.wait_send()
        mine.wait()


def all_gather(shards, *, name):
    n = len(shards)
    return pl.pallas_call(
        functools.partial(_all_gather_body, n=n), name=name,
        in_specs=[HBM_SPEC] * n, out_specs=[HBM_SPEC] * n,
        out_shape=[jax.ShapeDtypeStruct((N_DEV,) + s.shape, s.dtype) for s in shards],
        scratch_shapes=[pltpu.SemaphoreType.DMA((7 * n,)), pltpu.SemaphoreType.DMA((7 * n,)),
                        pltpu.SemaphoreType.DMA((n,))],
    )(*shards)


PEER_FLIPS = [(0, 0, 1), (1, 0, 0), (0, 1, 0), (1, 1, 0), (1, 0, 1), (0, 1, 1), (1, 1, 1)]


def _all_to_all_body(*refs, n):
    ins, outs = refs[:n], refs[n:2 * n]
    send_sems, recv_sems, local_sems = refs[2 * n:]
    x, y, c = _my_place()
    me = _flat(x, y, c)
    waits = []
    for a in range(n):
        src, out = ins[a], outs[a]
        mine = pltpu.make_async_copy(src.at[me], out.at[me], local_sems.at[a])
        mine.start()
        waits.append(mine)
        for k, (fx, fy, fc) in enumerate(PEER_FLIPS):
            peer = (1 - x if fx else x, 1 - y if fy else y, 1 - c if fc else c)
            theirs = _flat(*peer)
            sems = dict(send_sem=send_sems.at[7 * a + k], recv_sem=recv_sems.at[7 * a + k],
                        device_id=peer, device_id_type=MESH_ID)
            send = pltpu.make_async_remote_copy(src_ref=src.at[theirs], dst_ref=out.at[me], **sems)
            send.start()
            recv = pltpu.make_async_remote_copy(src_ref=src.at[theirs], dst_ref=out.at[theirs], **sems)
            waits.append((send, recv))
    for w in waits:
        if isinstance(w, tuple):
            w[0].wait_send()
            w[1].wait_recv()
        else:
            w.wait()


def all_to_all(parts, *, name):
    n = len(parts)
    return pl.pallas_call(
        functools.partial(_all_to_all_body, n=n), name=name,
        in_specs=[HBM_SPEC] * n, out_specs=[HBM_SPEC] * n,
        out_shape=[jax.ShapeDtypeStruct(p.shape, p.dtype) for p in parts],
        scratch_shapes=[pltpu.SemaphoreType.DMA((7 * n,)), pltpu.SemaphoreType.DMA((7 * n,)),
                        pltpu.SemaphoreType.DMA((n,))],
    )(*parts)


HBM_ONLY = pl.BlockSpec(memory_space=pltpu.HBM)
SEM_SPEC = pl.BlockSpec(memory_space=pltpu.SEMAPHORE)
DATAFLOW_EFFECT = pltpu.SideEffectType.DATAFLOW_SIDE_EFFECTING


def _split_copies(srcs, lands, send_sems, recv_sems, n, scatter):
    x, y, c = _my_place()
    me = _flat(x, y, c)
    pairs = []
    for a in range(n):
        for k, (fx, fy, fc) in enumerate(PEER_FLIPS):
            peer = (1 - x if fx else x, 1 - y if fy else y, 1 - c if fc else c)
            theirs = _flat(*peer)
            src = srcs[a].at[theirs] if scatter else srcs[a]
            sems = dict(send_sem=send_sems.at[7 * a + k], recv_sem=recv_sems.at[7 * a + k],
                        device_id=peer, device_id_type=MESH_ID)
            pairs.append((pltpu.make_async_remote_copy(src_ref=src, dst_ref=lands[a].at[me], **sems),
                          pltpu.make_async_remote_copy(src_ref=src, dst_ref=lands[a].at[theirs], **sems)))
    return pairs


def _exchange_start_body(*refs, n, scatter):
    srcs, lands = refs[:n], refs[n:2 * n]
    send_sems, recv_sems = refs[2 * n], refs[2 * n + 1]
    token = refs[-1]
    for send, _ in _split_copies(srcs, lands, send_sems, recv_sems, n, scatter):
        send.start()
    token[...] = jnp.zeros_like(token)


def exchange_start(srcs, lands, *, scatter, name):
    n = len(srcs)
    args = [pltpu.with_memory_space_constraint(t, pltpu.HBM) for t in list(srcs) + list(lands)]
    outs = pl.pallas_call(
        functools.partial(_exchange_start_body, n=n, scatter=scatter), name=name,
        out_shape=(pltpu.SemaphoreType.DMA((7 * n,)), pltpu.SemaphoreType.DMA((7 * n,)),
                   *[pltpu.HBM(t.shape, t.dtype) for t in args],
                   jax.ShapeDtypeStruct((8, 128), F32)),
        in_specs=[HBM_ONLY] * (2 * n),
        out_specs=(SEM_SPEC, SEM_SPEC, *[HBM_ONLY] * (2 * n), pl.BlockSpec(memory_space=pltpu.VMEM)),
        input_output_aliases={i: 2 + i for i in range(2 * n)},
        compiler_params=pltpu.CompilerParams(has_side_effects=DATAFLOW_EFFECT),
    )(*args)
    return outs[0], outs[1], outs[2:2 + n], outs[2 + n:2 + 2 * n], outs[-1]


def _exchange_wait_body(*refs, n, scatter):
    srcs, lands = refs[:n], refs[n:2 * n]
    send_sems, recv_sems = refs[2 * n], refs[2 * n + 1]
    for send, recv in _split_copies(srcs, lands, send_sems, recv_sems, n, scatter):
        send.wait_send()
        recv.wait_recv()


def exchange_wait(send_sems, recv_sems, srcs, lands, after, *, scatter, name):
    n = len(srcs)
    outs = pl.pallas_call(
        functools.partial(_exchange_wait_body, n=n, scatter=scatter), name=name,
        out_shape=tuple(pltpu.HBM(t.shape, t.dtype) for t in list(srcs) + list(lands)),
        in_specs=[HBM_ONLY] * (2 * n) + [SEM_SPEC, SEM_SPEC, HBM_SPEC],
        out_specs=tuple([HBM_ONLY] * (2 * n)),
        input_output_aliases={i: i for i in range(2 * n)},
        compiler_params=pltpu.CompilerParams(has_side_effects=DATAFLOW_EFFECT),
    )(*srcs, *lands, send_sems, recv_sems, after)
    return outs[n:]


def _own_slot_only(shape_dtype, own, slot):
    land = lax.empty(shape_dtype.shape, shape_dtype.dtype)
    return lax.dynamic_update_slice(land, own[None], (slot,) + (0,) * own.ndim)


def _pad_rows(a, rows=8):
    return jnp.pad(a, ((0, rows - a.shape[0]), (0, 0)))


def _gate_rows(a_log, dt_bias):
    z = jnp.zeros((8, 128), F32)
    return z.at[0, A_HEADS:2 * A_HEADS].set(a_log[0]).at[1, A_HEADS:2 * A_HEADS].set(dt_bias[0])


def _pack_small(norm_g, a_log, a_dt_bias, a_norm_g, b_q_norm_g, b_k_norm_g):
    return jnp.concatenate([
        norm_g[0].reshape(8, 128), norm_g[1].reshape(8, 128),
        _gate_rows(a_log, a_dt_bias),
        _pad_rows(a_norm_g[0].reshape(2, 128)),
        _pad_rows(jnp.concatenate([b_q_norm_g[0], b_k_norm_g[0]], axis=0)),
    ], axis=0)


def _unpack_small(p):
    return (p[0:16].reshape(2, D_MODEL), p[16:17, A_HEADS:2 * A_HEADS], p[17:18, A_HEADS:2 * A_HEADS],
            p[24:26].reshape(1, A_DV), p[32:35][None], p[35:38][None])


def kernel(x, positions, norm_g, a_w_in, a_conv_w, a_log, a_dt_bias, a_norm_g, a_w_out, b_w_in, b_q_norm_g, b_k_norm_g, b_w_out, loss_target, m_norm_g, m_a_w_in, m_a_conv_w, m_a_log, m_a_dt_bias, m_a_norm_g, m_a_w_out, m_b_w_in, m_b_q_norm_g, m_b_k_norm_g, m_b_w_out, v_norm_g, v_a_w_in, v_a_conv_w, v_a_log, v_a_dt_bias, v_a_norm_g, v_a_w_out, v_b_w_in, v_b_q_norm_g, v_b_k_norm_g, v_b_w_out):
    n_seq, s_len, d = x.shape
    t_rows = n_seq * s_len
    n_chunks = s_len // CHUNK
    x0 = x.reshape(t_rows, d)
    target = loss_target.reshape(t_rows, d)
    my_slot = _flat(*_my_place())

    g_a_in, g_conv = all_gather([a_w_in[0].astype(BF16), _pad_rows(a_conv_w[0])], name="gather_weights_first")
    later = [a_w_out[0].astype(BF16), b_w_in[0].astype(BF16), b_w_out[0].astype(BF16)]
    lands = [_own_slot_only(jax.ShapeDtypeStruct((N_DEV,) + t.shape, t.dtype), t, my_slot) for t in later]
    w_send, w_recv, later, lands, w_token = exchange_start(later, lands, scatter=False, name="gather_weights_start")
    w_a_in = jnp.pad(g_a_in.transpose(1, 0, 2).reshape(d, A_IN), ((0, 0), (0, A_IN_PAD - A_IN)))
    conv_w8 = g_conv.transpose(1, 0, 2).reshape(8, 2 * A_QK + A_VW)

    gains_model = _pad_rows(norm_g)
    gate_prm = _gate_rows(a_log, a_dt_bias)
    gain_a_out = _pad_rows(a_norm_g)
    gains_qk = _pad_rows(jnp.concatenate([b_q_norm_g[0], b_k_norm_g[0]], axis=0))
    cos_t, sin_t = rope_tables(positions.reshape(t_rows))

    h0 = rms_fwd(x0, gains_model, 0, after=w_token)
    proj_a = mm_nn(h0, w_a_in, tn=896, name="proj_a")
    gates_col, gates_row = a_gates_fwd(proj_a, gate_prm, n_seq)
    gates_row = gates_row.reshape(n_seq, 2 * A_HEADS, s_len // SUPER, 1, SUPER)
    o_a, tinv, states, og_a, q_a, k_a, v_a = gdn_fwd(proj_a, conv_w8, gates_col, gates_row, gain_a_out, n_seq)
    g_a_out, g_b_in, g_b_out = exchange_wait(w_send, w_recv, later, lands, og_a, scatter=False,
                                             name="gather_weights_wait")
    w_a_out = g_a_out.reshape(A_VW, d)
    w_b_in = g_b_in.transpose(1, 0, 2).reshape(d, B_IN_COLS)
    w_b_out = g_b_out.reshape(B_W, d)
    x1 = mm_nn(og_a, w_a_out, x0, tn=1024, name="out_a")

    h1 = rms_fwd(x1, gains_model, 1)
    proj_b = mm_nn_pieces(h1, w_b_in, name="proj_b")
    og_b, o_b, lse = attn_fwd(proj_b, cos_t, sin_t, gains_qk, n_seq)
    y = mm_nn(og_b, w_b_out, x1, tn=1024, name="out_b")

    dy, loss_blk = loss_head(y, target)
    loss = lax.psum(loss_blk[0, 0], ("x", "y", "c"))

    d_og_b = mm_nt(dy, w_b_out, tn=512, name="d_og_b")
    dw_b_out = mm_tn(og_b, dy, tn=256, out_dtype=BF16, name="dw_b_out")
    dqkv_b, dz_b, d_gains_qk = attn_bwd(proj_b, cos_t, sin_t, gains_qk, d_og_b, o_b, lse, n_seq)
    dh1 = mm_nt_pieces(dqkv_b, w_b_in, name="dh1_qkv")
    dh1 = mm_nt(dz_b, w_b_in, dh1, tn=B_W, col_off=3 * B_GROUPS, name="dh1_z")
    dw_b_in = mm_tn_b_in(h1, dqkv_b, dz_b, out_dtype=BF16, name="dw_b_in")
    dx1, d_gain1 = rms_bwd(x1, gains_model, 1, dh1, dy)

    dw_a_out = mm_tn(og_a, dx1, tn=256, out_dtype=BF16, name="dw_a_out")
    early = [dw_b_in, dw_b_out.reshape(N_DEV, B_W // N_DEV, d), dw_a_out.reshape(N_DEV, A_VW // N_DEV, d)]
    lands = [_own_slot_only(t, lax.dynamic_index_in_dim(t, my_slot, 0, keepdims=False), my_slot) for t in early]
    g_send, g_recv, early, lands, g_token = exchange_start(early, lands, scatter=True, name="scatter_grads_start")

    d_og_a = mm_nt(dx1, w_a_out, tn=512, name="d_og_a", after=g_token)
    d_xq, d_xk, d_xv, dgates, dz_a, d_gain_a_out, d_cq, d_ck, d_cv = gdn_bwd(
        q_a, k_a, v_a, gates_col, gates_row, tinv, states, d_og_a, o_a, proj_a, conv_w8, gain_a_out, n_seq)
    d_conv = jnp.concatenate([d_cq.sum(axis=0), d_ck.sum(axis=0), d_cv.sum(axis=0)], axis=1)
    d_gate_logits, d_gate_prm = a_gates_bwd(proj_a, gate_prm, dgates, n_seq)
    dw_a_in = jnp.concatenate([
        mm_tn(h0, d_xq, tn=256, out_dtype=BF16, name="dw_a_in_q"),
        mm_tn(h0, d_xk, tn=256, out_dtype=BF16, name="dw_a_in_k"),
        mm_tn(h0, d_xv, tn=256, out_dtype=BF16, name="dw_a_in_v"),
        mm_tn(h0, dz_a, tn=256, out_dtype=BF16, name="dw_a_in_z"),
        mm_tn(h0, d_gate_logits, tn=128, out_dtype=BF16, name="dw_a_in_gates"),
    ], axis=1)[:, :A_IN]
    shard_a_in = A_IN // N_DEV
    last = [dw_a_in.reshape(d, N_DEV, shard_a_in).transpose(1, 0, 2)]
    last_lands = [_own_slot_only(t, lax.dynamic_index_in_dim(t, my_slot, 0, keepdims=False), my_slot) for t in last]
    l_send, l_recv, last, last_lands, l_token = exchange_start(last, last_lands, scatter=True,
                                                               name="scatter_last_start")
    dh0 = mm_nt(d_xq, w_a_in, tn=512, name="dh0_q", after=l_token)
    dh0 = mm_nt(d_xk, w_a_in, dh0, tn=512, col_off=A_QK // 512, name="dh0_k")
    dh0 = mm_nt(d_xv, w_a_in, dh0, tn=512, col_off=2 * A_QK // 512, name="dh0_v")
    dh0 = mm_nt(dz_a, w_a_in, dh0, tn=512, col_off=(2 * A_QK + A_VW) // 512, name="dh0_z")
    dh0 = mm_nt(d_gate_logits, w_a_in, dh0, tn=128, col_off=A_GATE_COL, name="dh0_gates")
    dx0, d_gain0 = rms_bwd(x0, gains_model, 0, dh0, dx1)

    small = jnp.concatenate([
        d_gain0[0].reshape(8, 128), d_gain1[0].reshape(8, 128), d_gate_prm,
        _pad_rows(d_gain_a_out[0].reshape(2, 128)), d_gains_qk], axis=0)
    r_small, r_conv = all_gather([small, d_conv], name="gather_small_grads")
    conv_cols = a_conv_w.shape[2]
    r_conv = lax.dynamic_slice(r_conv, (0, 0, my_slot * conv_cols), (N_DEV, 8, conv_cols))

    r_b_in, r_b_out, r_a_out = exchange_wait(g_send, g_recv, early, lands, r_small, scatter=True,
                                             name="scatter_grads_wait")
    (r_a_in,) = exchange_wait(l_send, l_recv, last, last_lands, r_small, scatter=True, name="scatter_last_wait")

    upd = {}
    upd["a_w_in"] = adamw(r_a_in, a_w_in, m_a_w_in, v_a_w_in, name="adamw_a_w_in")
    upd["a_w_out"] = adamw(r_a_out, a_w_out, m_a_w_out, v_a_w_out, name="adamw_a_w_out")
    upd["b_w_in"] = adamw(r_b_in, b_w_in, m_b_w_in, v_b_w_in, name="adamw_b_w_in")
    upd["b_w_out"] = adamw(r_b_out, b_w_out, m_b_w_out, v_b_w_out, name="adamw_b_w_out")
    upd["a_conv_w"] = [t[:, :A_CONV] for t in adamw(
        r_conv, _pad_rows(a_conv_w[0])[None], _pad_rows(m_a_conv_w[0])[None], _pad_rows(v_a_conv_w[0])[None],
        name="adamw_a_conv_w")]
    small_upd = adamw(
        r_small,
        _pack_small(norm_g, a_log, a_dt_bias, a_norm_g, b_q_norm_g, b_k_norm_g)[None],
        _pack_small(m_norm_g, m_a_log, m_a_dt_bias, m_a_norm_g, m_b_q_norm_g, m_b_k_norm_g)[None],
        _pack_small(v_norm_g, v_a_log, v_a_dt_bias, v_a_norm_g, v_b_q_norm_g, v_b_k_norm_g)[None],
        name="adamw_small")
    small_names = ("norm_g", "a_log", "a_dt_bias", "a_norm_g", "b_q_norm_g", "b_k_norm_g")
    unpacked = [_unpack_small(t[0]) for t in small_upd]
    for i, nm in enumerate(small_names):
        upd[nm] = [u[i] for u in unpacked]

    order = ("norm_g", "a_w_in", "a_conv_w", "a_log", "a_dt_bias", "a_norm_g", "a_w_out",
             "b_w_in", "b_q_norm_g", "b_k_norm_g", "b_w_out")
    outs = [loss, dx0.reshape(n_seq, s_len, d)]
    for kind in range(4):
        for nm in order:
            outs.append(upd[nm][kind])
    return tuple(outs)
```

```python
import functools
import math

import jax
import jax.numpy as jnp
from jax import lax
from jax.experimental import pallas as pl
from jax.experimental.pallas import tpu as pltpu

F32 = jnp.float32
BF16 = jnp.bfloat16

D_MODEL = 1024
EPS = 1e-6
N_DEV = 8

A_HEADS = 8
A_DK = 128
A_DV = 256
A_QK = A_HEADS * A_DK
A_VW = A_HEADS * A_DV
A_CONV = 4
CHUNK = 64
A_IN = 2 * A_QK + 2 * A_VW + 2 * A_HEADS
A_IN_PAD = 2 * A_QK + 2 * A_VW + 128
A_GATE_COL = (2 * A_QK + 2 * A_VW) // 128

B_DILATIONS = (1, 4, 16)
B_GROUPS = 3
B_HEADS = 8
B_DH = 128
B_W = B_HEADS * B_DH
B_BLOCK = 128
B_PIECES = 3 * B_GROUPS + 1
ROPE_THETA = 500000.0
ROPE_DIMS = B_DH // 4
ROPE_HALF = ROPE_DIMS // 2

ADAM_LR = 0.001
ADAM_B1 = 0.9
ADAM_B2 = 0.999
ADAM_EPS = 1e-08
ADAM_WD = 0.01
ADAM_STEP = 10

VMEM_LIMIT = 60 * 1024 * 1024


def _params(sem):
    return pltpu.CompilerParams(dimension_semantics=sem, vmem_limit_bytes=VMEM_LIMIT)


def _bf(x):
    return x.astype(BF16)


def _mm(a, b):
    return jnp.dot(_bf(a), _bf(b), preferred_element_type=F32)


def _mm_nt(a, b):
    return lax.dot_general(_bf(a), _bf(b), (((1,), (1,)), ((), ())), preferred_element_type=F32)


def _mm_tn(a, b):
    return lax.dot_general(_bf(a), _bf(b), (((0,), (0,)), ((), ())), preferred_element_type=F32)


def _split(x):
    hi = _bf(x)
    return hi, _bf(x - hi.astype(F32))


def _mm3(a, b):
    ah, al = _split(a)
    bh, bl = _split(b)
    d = functools.partial(jnp.dot, preferred_element_type=F32)
    return d(ah, bh) + (d(ah, bl) + d(al, bh))


def _colsum_as_col(z):
    zh, zl = _split(z)
    ones = jnp.ones((z.shape[0], 128), BF16)
    tn = functools.partial(lax.dot_general, dimension_numbers=(((0,), (0,)), ((), ())),
                           preferred_element_type=F32)
    return (tn(zh, ones) + tn(zl, ones))[:, 0:1]


def _sigmoid(x):
    return 0.5 * jnp.tanh(0.5 * x) + 0.5


INV_BASE = 8
INV_NEWTON = 2
GDN_GROUP = 4
SUPER = GDN_GROUP * CHUNK
GDN_WIDTH = 2

A_K_COL = A_QK // A_DK
A_V_COL = 2 * A_QK // A_DV


def _inverse_steps(m, row, col):
    eye = (row == col).astype(F32)
    d = jnp.where(row // INV_BASE == col // INV_BASE, m, 0.0)
    x = eye - d
    p = _mm(d, d)
    yield
    steps = int(math.log2(INV_BASE)) - 1
    for i in range(steps):
        x = x + _mm(x, p)
        if i + 1 < steps:
            p = _mm(p, p)
        yield
    size = INV_BASE
    while size < CHUNK:
        c = jnp.where((row // (2 * size) == col // (2 * size)) & (row // size != col // size), m, 0.0)
        xc = _mm(x, c)
        yield
        x = x - _mm(xc, x)
        yield
        size *= 2
    for _ in range(INV_NEWTON):
        r = eye - x - _mm3(m, x)
        yield
        x = x + _mm(x, r)
        yield
    return x


def _drain(gen):
    while True:
        try:
            next(gen)
        except StopIteration as stop:
            return stop.value


def _interleave(*gens):
    live = list(gens)
    while live:
        for g in list(live):
            try:
                next(g)
            except StopIteration:
                live.remove(g)


def _diag_blocks_tall(x):
    return jnp.concatenate([x[i * CHUNK:(i + 1) * CHUNK, i * CHUNK:(i + 1) * CHUNK] for i in range(GDN_GROUP)], axis=0)


def _tall_to_block_diag(t, same):
    return jnp.where(same, jnp.concatenate([t] * GDN_GROUP, axis=1), 0.0)


def _block_sum(same, x):
    xh, xl = _split(jnp.broadcast_to(x, (SUPER, 128)))
    ones = same.astype(BF16)
    d = functools.partial(jnp.dot, preferred_element_type=F32)
    return (d(ones, xh) + d(ones, xl))[:, 0:1]


def _aligned_rows(index, size):
    start = index * size
    return pl.ds(start if isinstance(start, int) else pl.multiple_of(start, size), size)


def _super_rows(i):
    return _aligned_rows(i, SUPER)


def _chunk_rows(n):
    return _aligned_rows(n, CHUNK)


def _gdn_super_steps(q, k, v, gcb, gr, head, tinv_tall=None):
    lane = lax.broadcasted_iota(jnp.int32, (SUPER, 128), 1)
    row = lax.broadcasted_iota(jnp.int32, (SUPER, SUPER), 0)
    col = lax.broadcasted_iota(jnp.int32, (SUPER, SUPER), 1)
    same = row // CHUNK == col // CHUNK
    beta = jnp.sum(jnp.where(lane == head, gcb, 0.0), axis=1, keepdims=True)
    gc = jnp.sum(jnp.where(lane == A_HEADS + head, gcb, 0.0), axis=1, keepdims=True)
    g_last = jnp.sum(jnp.where(col == (row // CHUNK) * CHUNK + (CHUNK - 1), gr, 0.0), axis=1, keepdims=True)
    gamma = jnp.exp(gc)
    decay = jnp.where(same & (row >= col), jnp.exp(jnp.minimum(gc - gr, 0.0)), 0.0)
    kb = k * beta
    m = jnp.where(same & (row > col), _mm_nt(kb, k) * decay, 0.0)
    p = jnp.where(same & (row >= col), _mm_nt(q, k) * decay, 0.0)
    yield
    if tinv_tall is None:
        tinv = yield from _inverse_steps(m, row, col)
    else:
        tinv = _tall_to_block_diag(tinv_tall, same)
    u = _mm(tinv, v * beta)
    w = _mm(tinv, kb * gamma)
    yield
    e_tail = jnp.exp(g_last - gc)
    return dict(beta=beta, gc=gc, g_last=g_last, gamma=gamma, decay=decay, kb=kb, m=m,
                tinv=tinv, u=u, w=w, p=p, e_tail=e_tail, row=row, col=col, lane=lane, same=same)


def _gdn_super_common(q, k, v, gcb, gr, head, tinv_tall=None):
    return _drain(_gdn_super_steps(q, k, v, gcb, gr, head, tinv_tall))


def _store_scan_operands(rows, q, k, t, u_scr, w_scr, p_scr, qg_scr, ke_scr, gl_scr):
    u_scr[rows, :] = t["u"]
    w_scr[rows, :] = _bf(t["w"])
    p_scr[rows, :] = _bf(_diag_blocks_tall(t["p"]))
    qg_scr[rows, :] = _bf(q * t["gamma"])
    ke_scr[rows, :] = _bf(k * t["e_tail"])
    gl_scr[rows, :] = jnp.broadcast_to(jnp.exp(t["g_last"]), (SUPER, 128))


def _gdn_fwd_body(q_ref, k_ref, v_ref, gc_ref, gr_ref, wq_ref, wk_ref, wv_ref, z_ref, gn_ref,
                  o_ref, tinv_ref, st_ref, og_ref, qo_ref, ko_ref, vo_ref, s_scr, *sets):
    head = pl.program_id(1)
    n_super = q_ref.shape[0] // SUPER
    all_sets = [sets[6 * i:6 * i + 6] for i in range(2 * GDN_WIDTH)]
    whole = pl.ds(0, SUPER)

    def conv_silu(x_ref, w_ref, i):
        rows = _super_rows(i)
        x, w = x_ref[rows, :], w_ref[...]
        halo = jnp.zeros((8, x.shape[1]), F32) if i == 0 else x_ref[pl.ds(i * SUPER - 8, 8), :]
        ext = jnp.concatenate([halo, x], axis=0)
        c = x * w[A_CONV - 1:A_CONV, :]
        for j in range(1, A_CONV):
            c = c + pltpu.roll(ext, j, 0)[8:, :] * w[A_CONV - 1 - j:A_CONV - j, :]
        return c * _sigmoid(c)

    def unit(a):
        return a * lax.rsqrt(jnp.sum(a * a, axis=1, keepdims=True) + EPS)

    def prepare_steps(i, dst):
        rows = _super_rows(i)
        q = unit(conv_silu(q_ref, wq_ref, i)) * A_DK ** -0.5
        k = unit(conv_silu(k_ref, wk_ref, i))
        v = conv_silu(v_ref, wv_ref, i)
        qo_ref[rows, :], ko_ref[rows, :], vo_ref[rows, :] = q, k, v
        t = yield from _gdn_super_steps(q, k, v, gc_ref[rows, :], gr_ref[i], head)
        tinv_ref[rows, :] = _diag_blocks_tall(t["tinv"])
        _store_scan_operands(whole, q, k, t, *dst)

    def scan_steps(i, src):
        u_scr, w_scr, p_scr, qg_scr, ke_scr, gl_scr = src
        for j in range(GDN_GROUP):
            n = i * GDN_GROUP + j
            local = pl.ds(j * CHUNK, CHUNK)
            s = s_scr[...]
            sb = _bf(s)
            st_ref[n] = sb
            ws = jnp.dot(w_scr[local, :], sb, preferred_element_type=F32)
            yield
            vb = _bf(u_scr[local, :] - ws)
            o = (jnp.dot(qg_scr[local, :], sb, preferred_element_type=F32)
                 + jnp.dot(p_scr[local, :], vb, preferred_element_type=F32))
            s_new = s * gl_scr[local, :][0:1, 0:1] + lax.dot_general(
                ke_scr[local, :], vb, (((0,), (0,)), ((), ())), preferred_element_type=F32)
            yield
            rows = _chunk_rows(n)
            o_ref[rows, :] = o
            s_scr[...] = s_new
            silu, _ = _silu_parts(z_ref[rows, :])
            r = lax.rsqrt(jnp.mean(o * o, axis=1, keepdims=True) + EPS)
            og_ref[rows, :] = ((o * r * gn_ref[0:1, :]) * silu).astype(og_ref.dtype)

    def scan_many(first, srcs):
        for j, src in enumerate(srcs):
            yield from scan_steps(first + j, src)

    groups = [all_sets[:GDN_WIDTH], all_sets[GDN_WIDTH:]]
    _interleave(*[prepare_steps(j, groups[0][j]) for j in range(GDN_WIDTH)])
    s_scr[...] = jnp.zeros_like(s_scr)
    for g in range(n_super // GDN_WIDTH):
        cur, nxt = groups[g % 2], groups[(g + 1) % 2]
        first = g * GDN_WIDTH
        following = [prepare_steps(first + GDN_WIDTH + j, nxt[j]) for j in range(GDN_WIDTH)
                     if first + GDN_WIDTH + j < n_super]
        _interleave(scan_many(first, cur), *following)


def _gdn_in_specs(s_len, n_super, from_proj):
    k_col, v_col = (A_K_COL, A_V_COL) if from_proj else (0, 0)
    return [
        pl.BlockSpec((s_len, A_DK), lambda b, h: (b, h)),
        pl.BlockSpec((s_len, A_DK), lambda b, h: (b, k_col + h)),
        pl.BlockSpec((s_len, A_DV), lambda b, h: (b, v_col + h)),
        pl.BlockSpec((s_len, 128), lambda b, h: (b, 0)),
        pl.BlockSpec((None, None, n_super, 1, SUPER), lambda b, h: (b, A_HEADS + h, 0, 0, 0)),
    ]


def _gdn_scan_scratch(s_len):
    return [pltpu.VMEM((A_DK, A_DV), F32), pltpu.VMEM((s_len, A_DV), F32),
            pltpu.VMEM((s_len, A_DK), BF16), pltpu.VMEM((s_len, CHUNK), BF16),
            pltpu.VMEM((s_len, A_DK), BF16), pltpu.VMEM((s_len, A_DK), BF16),
            pltpu.VMEM((s_len, 128), F32)]


def gdn_fwd(proj_a, conv_w8, gates_col, gates_row, norm_g8, n_seq):
    t_rows = proj_a.shape[0]
    s_len = t_rows // n_seq
    n_chunks = s_len // CHUNK
    qk_spec = pl.BlockSpec((s_len, A_DK), lambda b, h: (b, h))
    v_spec = pl.BlockSpec((s_len, A_DV), lambda b, h: (b, h))
    return pl.pallas_call(
        _gdn_fwd_body, name="gdn_fwd", grid=(n_seq, A_HEADS),
        in_specs=_gdn_in_specs(s_len, s_len // SUPER, True) + [
            pl.BlockSpec((8, A_DK), lambda b, h: (0, h)),
            pl.BlockSpec((8, A_DK), lambda b, h: (0, A_K_COL + h)),
            pl.BlockSpec((8, A_DV), lambda b, h: (0, A_V_COL + h)),
            pl.BlockSpec((s_len, A_DV), lambda b, h: (b, A_Z_COL + h)),
            pl.BlockSpec((8, A_DV), lambda b, h: (0, 0)),
        ],
        out_specs=[
            v_spec,
            pl.BlockSpec((s_len, CHUNK), lambda b, h: (b * A_HEADS + h, 0)),
            pl.BlockSpec((None, n_chunks, A_DK, A_DV), lambda b, h: (b * A_HEADS + h, 0, 0, 0)),
            v_spec, qk_spec, qk_spec, v_spec,
        ],
        out_shape=[
            jax.ShapeDtypeStruct((t_rows, A_VW), F32),
            jax.ShapeDtypeStruct((n_seq * A_HEADS * s_len, CHUNK), F32),
            jax.ShapeDtypeStruct((n_seq * A_HEADS, n_chunks, A_DK, A_DV), BF16),
            jax.ShapeDtypeStruct((t_rows, A_VW), BF16),
            jax.ShapeDtypeStruct((t_rows, A_QK), F32),
            jax.ShapeDtypeStruct((t_rows, A_QK), F32),
            jax.ShapeDtypeStruct((t_rows, A_VW), F32),
        ],
        scratch_shapes=_gdn_scan_scratch(SUPER) + (2 * GDN_WIDTH - 1) * _gdn_scan_scratch(SUPER)[1:],
        compiler_params=_params(("arbitrary", "arbitrary")),
    )(proj_a, proj_a, proj_a, gates_col, gates_row, conv_w8, conv_w8, conv_w8, proj_a, norm_g8)


def _gdn_bwd_body(q_ref, k_ref, v_ref, gc_ref, gr_ref, tinv_ref, st_ref, dog_ref, oa_ref, z_ref, gn_ref,
                  xq_ref, xk_ref, xv_ref, wq_ref, wk_ref, wv_ref,
                  dxq_ref, dxk_ref, dxv_ref, dgc_ref, dz_ref, dgn_ref, dwq_ref, dwk_ref, dwv_ref,
                  ds_scr, cq_scr, ck_scr, cv_scr, *sets):
    head = pl.program_id(1)
    n_super = q_ref.shape[0] // SUPER
    ops = (sets[0:7], sets[7:14])
    res = (sets[14:21], sets[21:28])
    whole = pl.ds(0, SUPER)
    tn = functools.partial(lax.dot_general, dimension_numbers=(((0,), (0,)), ((), ())), preferred_element_type=F32)
    nt = functools.partial(lax.dot_general, dimension_numbers=(((1,), (1,)), ((), ())), preferred_element_type=F32)

    @pl.when(head == 0)
    def _():
        dgc_ref[...] = jnp.zeros_like(dgc_ref)

    @pl.when((head == 0) & (pl.program_id(0) == 0))
    def _():
        dgn_ref[...] = jnp.zeros_like(dgn_ref)

    carry = (cq_scr, ck_scr, cv_scr)
    for ref in carry + (dwq_ref, dwk_ref, dwv_ref):
        ref[...] = jnp.zeros_like(ref)

    def common_steps(i):
        rows = _super_rows(i)
        q, k, v = q_ref[rows, :], k_ref[rows, :], v_ref[rows, :]
        t = yield from _gdn_super_steps(q, k, v, gc_ref[rows, :], gr_ref[i], head, tinv_tall=tinv_ref[rows, :])
        return rows, q, k, v, t

    def stage_p(i, parity):
        rows, q, k, _, t = yield from common_steps(i)
        _store_scan_operands(whole, q, k, t, *ops[parity][:6])
        o, d_og, gain = oa_ref[rows, :], dog_ref[rows, :], gn_ref[0:1, :]
        r = lax.rsqrt(jnp.mean(o * o, axis=1, keepdims=True) + EPS)
        silu, dsilu = _silu_parts(z_ref[rows, :])
        xr = o * r
        d_on = d_og * silu
        dz_ref[rows, :] = (d_og * (xr * gain) * dsilu).astype(dz_ref.dtype)
        u = d_on * gain
        ops[parity][6][...] = r * u - xr * (r * r) * jnp.mean(u * o, axis=1, keepdims=True)
        dgn_ref[0:1, :] += jnp.sum(d_on * xr, axis=0, keepdims=True)

    def stage_s(i, parity):
        u_scr, w_scr, p_scr, qg_scr, ke_scr, gl_scr, do_scr = ops[parity]
        vn_scr, dvn_scr, dqg_scr, dw_scr, dkt_scr, sds_scr, dof_scr = res[parity]
        for j in reversed(range(GDN_GROUP)):
            n = i * GDN_GROUP + j
            local = pl.ds(j * CHUNK, CHUNK)
            ds_next = ds_scr[...]
            dsb = _bf(ds_next)
            sb = st_ref[n]
            s = sb.astype(F32)
            d_o = do_scr[local, :]
            dof_scr[local, :] = d_o
            d_ob = _bf(d_o)
            w_s = jnp.dot(w_scr[local, :], sb, preferred_element_type=F32)
            d_vn = tn(p_scr[local, :], d_ob) + jnp.dot(ke_scr[local, :], dsb, preferred_element_type=F32)
            d_qg = nt(d_ob, sb)
            qg_do = tn(qg_scr[local, :], d_ob)
            yield
            v_new = u_scr[local, :] - w_s
            d_vnb = _bf(d_vn)
            d_w = -nt(d_vnb, sb)
            d_kt = nt(_bf(v_new), dsb)
            w_dvn = tn(w_scr[local, :], d_vnb)
            yield
            vn_scr[local, :] = v_new
            dvn_scr[local, :] = d_vn
            dqg_scr[local, :] = d_qg
            dw_scr[local, :] = d_w
            dkt_scr[local, :] = d_kt
            sds = jnp.sum(jnp.sum(s * ds_next, axis=1, keepdims=True), axis=0, keepdims=True)
            sds_scr[local, :] = jnp.broadcast_to(sds, (CHUNK, 128))
            ds_scr[...] = qg_do + gl_scr[local, :][0:1, 0:1] * ds_next - w_dvn

    def conv_bwd(i, rows, x_ref, w_ref, dy, norm_scale, dx_ref, dw_ref, dc_above):
        x, w = x_ref[rows, :], w_ref[...]
        above = x_ref[pl.ds(pl.multiple_of(jnp.maximum(i * SUPER - 8, 0), 8), 8), :]
        ext = jnp.concatenate([jnp.where(i > 0, above, 0.0), x], axis=0)
        c = x * w[A_CONV - 1:A_CONV, :]
        for j in range(1, A_CONV):
            c = c + pltpu.roll(ext, j, 0)[8:, :] * w[A_CONV - 1 - j:A_CONV - j, :]
        sig = _sigmoid(c)
        a = c * sig
        if norm_scale is None:
            da = dy
        else:
            rn = lax.rsqrt(jnp.sum(a * a, axis=1, keepdims=True) + EPS)
            da = norm_scale * (rn * dy - a * (rn * rn * rn) * jnp.sum(dy * a, axis=1, keepdims=True))
        dc = da * (sig * (1.0 + c * (1.0 - sig)))
        ext_dc = jnp.concatenate([dc, dc_above[...]], axis=0)
        dc_above[...] = dc[0:8, :]
        dx = dc * w[A_CONV - 1:A_CONV, :]
        dw_ref[A_CONV - 1:A_CONV, :] += jnp.sum(dc * x, axis=0, keepdims=True)
        for j in range(1, A_CONV):
            dcs = pltpu.roll(ext_dc, SUPER + 8 - j, 0)[:SUPER, :]
            dx = dx + dcs * w[A_CONV - 1 - j:A_CONV - j, :]
            dw_ref[A_CONV - 1 - j:A_CONV - j, :] += jnp.sum(dcs * x, axis=0, keepdims=True)
        dx_ref[rows, :] = dx.astype(dx_ref.dtype)

    def stage_f(i, parity):
        vn_scr, dvn_scr, dqg_scr, dw_scr, dkt_scr, sds_scr, dof_scr = res[parity]
        rows, q, k, v, t = yield from common_steps(i)
        beta, gamma, decay, kb, e_tail = t["beta"], t["gamma"], t["decay"], t["kb"], t["e_tail"]
        row, col, lane, same = t["row"], t["col"], t["lane"], t["same"]
        d_o = dof_scr[...]
        v_new, d_vn = vn_scr[...], dvn_scr[...]
        d_qg, d_w, d_kt = dqg_scr[...], dw_scr[...], dkt_scr[...]
        gamma_last = jnp.exp(t["g_last"])

        d_p = jnp.where(same & (row >= col), _mm_nt(d_o, v_new), 0.0)
        d_ru = _mm_tn(t["tinv"], d_vn)
        d_rw = _mm_tn(t["tinv"], d_w)
        yield
        d_m = jnp.where(same & (row > col), -(_mm_nt(d_ru, t["u"]) + _mm_nt(d_rw, t["w"])), 0.0)
        yield

        x_p = d_p * decay
        y_m = d_m * decay
        d_kb = _mm(y_m, k) + d_rw * gamma
        d_q = _mm(x_p, k) + d_qg * gamma
        d_k = _mm_tn(x_p, q) + _mm_tn(y_m, kb) + d_kb * beta + d_kt * e_tail
        d_v = d_ru * beta
        conv_bwd(i, rows, xq_ref, wq_ref, d_q, A_DK ** -0.5, dxq_ref, dwq_ref, carry[0])
        conv_bwd(i, rows, xk_ref, wk_ref, d_k, 1.0, dxk_ref, dwk_ref, carry[1])
        conv_bwd(i, rows, xv_ref, wv_ref, d_v, None, dxv_ref, dwv_ref, carry[2])

        d_beta = (jnp.sum(d_ru * v, axis=1, keepdims=True)
                  + jnp.sum(d_kb * k, axis=1, keepdims=True))
        z = d_p * t["p"] + d_m * t["m"]
        eps_tail = jnp.sum(d_kt * k, axis=1, keepdims=True) * e_tail
        d_gc = (jnp.sum(z, axis=1, keepdims=True) - _colsum_as_col(z)
                + jnp.sum(d_qg * q, axis=1, keepdims=True) * gamma
                + jnp.sum(d_rw * kb, axis=1, keepdims=True) * gamma
                - eps_tail)
        d_glast = _block_sum(same, eps_tail) + gamma_last * sds_scr[...][:, 0:1]
        yield
        rcol = lax.broadcasted_iota(jnp.int32, (SUPER, 1), 0)
        d_gc = d_gc + jnp.where(rcol % CHUNK == CHUNK - 1, d_glast, 0.0)
        dgc_ref[rows, :] += (jnp.where(lane == head, d_beta, 0.0)
                             + jnp.where(lane == A_HEADS + head, d_gc, 0.0))

    last = n_super - 1
    _drain(stage_p(last, 1))
    ds_scr[...] = jnp.zeros_like(ds_scr)
    _interleave(stage_s(last, 1), stage_p(last - 1, 0))

    def pair(k, carry):
        i = last - 1 - 2 * k
        _interleave(stage_s(i, 0), stage_f(i + 1, 1), stage_p(i - 1, 1))
        _interleave(stage_s(i - 1, 1), stage_f(i, 0), stage_p(i - 2, 0))
        return carry

    lax.fori_loop(0, n_super // 2 - 1, pair, 0)
    _interleave(stage_s(0, 0), stage_f(1, 1))
    _drain(stage_f(0, 0))


def gdn_bwd(q, k, v, gates_col, gates_row, tinv, states, d_og, o, proj_a, conv_w8, norm_g8, n_seq):
    t_rows = q.shape[0]
    s_len = t_rows // n_seq
    n_chunks = s_len // CHUNK
    qk_spec = pl.BlockSpec((s_len, A_DK), lambda b, h: (b, h))
    v_spec = pl.BlockSpec((s_len, A_DV), lambda b, h: (b, h))
    gate_spec = pl.BlockSpec((s_len, 128), lambda b, h: (b, 0))
    gain_spec = pl.BlockSpec((8, A_DV), lambda b, h: (0, 0))
    dw_qk_spec = pl.BlockSpec((None, 8, A_DK), lambda b, h: (b, 0, h))
    dw_v_spec = pl.BlockSpec((None, 8, A_DV), lambda b, h: (b, 0, h))
    ops_set = _gdn_scan_scratch(SUPER)[1:] + [pltpu.VMEM((SUPER, A_DV), F32)]
    res_set = [pltpu.VMEM((SUPER, A_DV), F32), pltpu.VMEM((SUPER, A_DV), F32),
               pltpu.VMEM((SUPER, A_DK), F32), pltpu.VMEM((SUPER, A_DK), F32),
               pltpu.VMEM((SUPER, A_DK), F32), pltpu.VMEM((SUPER, 128), F32), pltpu.VMEM((SUPER, A_DV), F32)]
    return pl.pallas_call(
        _gdn_bwd_body, name="gdn_bwd", grid=(n_seq, A_HEADS),
        in_specs=_gdn_in_specs(s_len, s_len // SUPER, False) + [
            pl.BlockSpec((s_len, CHUNK), lambda b, h: (b * A_HEADS + h, 0)),
            pl.BlockSpec((None, n_chunks, A_DK, A_DV), lambda b, h: (b * A_HEADS + h, 0, 0, 0)),
            v_spec, v_spec,
            pl.BlockSpec((s_len, A_DV), lambda b, h: (b, A_Z_COL + h)),
            gain_spec,
            pl.BlockSpec((s_len, A_DK), lambda b, h: (b, h)),
            pl.BlockSpec((s_len, A_DK), lambda b, h: (b, A_K_COL + h)),
            pl.BlockSpec((s_len, A_DV), lambda b, h: (b, A_V_COL + h)),
            pl.BlockSpec((8, A_DK), lambda b, h: (0, h)),
            pl.BlockSpec((8, A_DK), lambda b, h: (0, A_K_COL + h)),
            pl.BlockSpec((8, A_DV), lambda b, h: (0, A_V_COL + h)),
        ],
        out_specs=[qk_spec, qk_spec, v_spec, gate_spec, v_spec, gain_spec, dw_qk_spec, dw_qk_spec, dw_v_spec],
        out_shape=[
            jax.ShapeDtypeStruct((t_rows, A_QK), BF16),
            jax.ShapeDtypeStruct((t_rows, A_QK), BF16),
            jax.ShapeDtypeStruct((t_rows, A_VW), BF16),
            jax.ShapeDtypeStruct((t_rows, 128), F32),
            jax.ShapeDtypeStruct((t_rows, A_VW), BF16),
            jax.ShapeDtypeStruct((8, A_DV), F32),
            jax.ShapeDtypeStruct((n_seq, 8, A_QK), F32),
            jax.ShapeDtypeStruct((n_seq, 8, A_QK), F32),
            jax.ShapeDtypeStruct((n_seq, 8, A_VW), F32),
        ],
        scratch_shapes=(_gdn_scan_scratch(SUPER)[:1]
                        + [pltpu.VMEM((8, A_DK), F32), pltpu.VMEM((8, A_DK), F32), pltpu.VMEM((8, A_DV), F32)]
                        + 2 * ops_set + 2 * res_set),
        compiler_params=_params(("arbitrary", "arbitrary")),
    )(q, k, v, gates_col, gates_row, tinv, states, d_og, o, proj_a, norm_g8,
      proj_a, proj_a, proj_a, conv_w8, conv_w8, conv_w8)


GATE_TILE = 512


def _softplus(y):
    return jnp.maximum(y, 0.0) + jnp.log1p(jnp.exp(-jnp.abs(y)))


def _gate_values(x, prm):
    beta = _sigmoid(x)
    y = x + prm[1:2, :]
    neg_a = -jnp.exp(prm[0:1, :])
    g = neg_a * _softplus(y)
    return beta, y, neg_a, g


def _a_gates_fwd_body(x_ref, prm_ref, gc_ref, gr_ref):
    x = x_ref[...]
    tm = x.shape[0]
    beta, _, _, g = _gate_values(x, prm_ref[...])
    in_chunk = lax.broadcasted_iota(jnp.int32, (tm, 1), 0) % CHUNK
    s = 1
    while s < CHUNK:
        g = g + jnp.where(in_chunk >= s, pltpu.roll(g, s, 0), 0.0)
        s *= 2
    lane = lax.broadcasted_iota(jnp.int32, x.shape, 1)
    out = jnp.where(lane < A_HEADS, beta, jnp.where(lane < 2 * A_HEADS, g, 0.0))
    gc_ref[...] = out
    gr_ref[...] = out.T[0:2 * A_HEADS, :]


def a_gates_fwd(proj_a, prm, n_seq):
    t_rows = proj_a.shape[0]
    s_len = t_rows // n_seq
    tm = min(GATE_TILE, s_len)
    n_t = s_len // tm
    return pl.pallas_call(
        _a_gates_fwd_body, name="a_gates_fwd", grid=(n_seq, n_t),
        in_specs=[pl.BlockSpec((tm, 128), lambda b, i: (b * n_t + i, A_GATE_COL)),
                  pl.BlockSpec((8, 128), lambda b, i: (0, 0))],
        out_specs=[pl.BlockSpec((tm, 128), lambda b, i: (b * n_t + i, 0)),
                   pl.BlockSpec((None, 2 * A_HEADS, tm), lambda b, i: (b, 0, i))],
        out_shape=[jax.ShapeDtypeStruct((t_rows, 128), F32),
                   jax.ShapeDtypeStruct((n_seq, 2 * A_HEADS, s_len), F32)],
        compiler_params=_params(("arbitrary", "arbitrary")),
    )(proj_a, prm)


def _a_gates_bwd_body(x_ref, prm_ref, dgc_ref, dx_ref, dprm_ref):
    first = (pl.program_id(0) == 0) & (pl.program_id(1) == 0)
    x = x_ref[...]
    tm = x.shape[0]
    beta, y, neg_a, g = _gate_values(x, prm_ref[...])
    d = dgc_ref[...]
    in_chunk = lax.broadcasted_iota(jnp.int32, (tm, 1), 0) % CHUNK
    dg = d
    s = 1
    while s < CHUNK:
        dg = dg + jnp.where(in_chunk < CHUNK - s, pltpu.roll(dg, tm - s, 0), 0.0)
        s *= 2
    lane = lax.broadcasted_iota(jnp.int32, x.shape, 1)
    is_decay = (lane >= A_HEADS) & (lane < 2 * A_HEADS)
    d_alogit = jnp.where(is_decay, dg * neg_a * _sigmoid(y), 0.0)
    dx_ref[...] = jnp.where(lane < A_HEADS, d * beta * (1.0 - beta), d_alogit).astype(dx_ref.dtype)

    @pl.when(first)
    def _():
        dprm_ref[...] = jnp.zeros_like(dprm_ref)

    dprm_ref[0:1, :] += jnp.sum(jnp.where(is_decay, dg * g, 0.0), axis=0, keepdims=True)
    dprm_ref[1:2, :] += jnp.sum(d_alogit, axis=0, keepdims=True)


def a_gates_bwd(proj_a, prm, dgates_col, n_seq):
    t_rows = proj_a.shape[0]
    s_len = t_rows // n_seq
    tm = min(GATE_TILE, s_len)
    n_t = s_len // tm
    return pl.pallas_call(
        _a_gates_bwd_body, name="a_gates_bwd", grid=(n_seq, n_t),
        in_specs=[pl.BlockSpec((tm, 128), lambda b, i: (b * n_t + i, A_GATE_COL)),
                  pl.BlockSpec((8, 128), lambda b, i: (0, 0)),
                  pl.BlockSpec((tm, 128), lambda b, i: (b * n_t + i, 0))],
        out_specs=[pl.BlockSpec((tm, 128), lambda b, i: (b * n_t + i, 0)),
                   pl.BlockSpec((8, 128), lambda b, i: (0, 0))],
        out_shape=[jax.ShapeDtypeStruct((t_rows, 128), BF16),
                   jax.ShapeDtypeStruct((8, 128), F32)],
        compiler_params=_params(("arbitrary", "arbitrary")),
    )(proj_a, prm, dgates_col)


ROW_TILE = 512
A_Z_COL = (2 * A_QK + A_VW) // A_DV


def _silu_parts(z):
    sig = _sigmoid(z)
    return z * sig, sig * (1.0 + z * (1.0 - sig))


NEG_BIG = -1e30
ATT_SCALE = B_DH ** -0.5


def _swap_rope_halves(x):
    src = lax.broadcasted_iota(jnp.int32, (B_DH, B_DH), 0)
    dst = lax.broadcasted_iota(jnp.int32, (B_DH, B_DH), 1)
    pick = ((dst < ROPE_HALF) & (src == dst + ROPE_HALF)) | (
        (dst >= ROPE_HALF) & (dst < ROPE_DIMS) & (src == dst - ROPE_HALF))
    return jnp.dot(_bf(x), pick.astype(BF16), preferred_element_type=F32)


def _norm_rope(x, gain, cos_t, sin_t):
    r = lax.rsqrt(jnp.mean(x * x, axis=1, keepdims=True) + EPS)
    xn = x * r * gain
    return xn * cos_t + _swap_rope_halves(xn) * sin_t, r


def _norm_rope_bwd(x, r, gain, cos_t, sin_t, dy):
    d_xn = dy * cos_t + _swap_rope_halves(dy * sin_t)
    xr = x * r
    u = d_xn * gain
    dx = r * u - xr * (r * r) * jnp.mean(u * x, axis=1, keepdims=True)
    return dx, jnp.sum(d_xn * xr, axis=0, keepdims=True)


def _stream_rows(idx, dilation, s_len):
    nb = s_len // dilation // B_BLOCK
    r = idx // nb
    m = idx % nb
    cur = r + m * (B_BLOCK * dilation)
    prev = r + jnp.maximum(m - 1, 0) * (B_BLOCK * dilation)
    return cur, prev, m > 0


def _rows(start, dilation):
    if dilation == 1:
        return pl.ds(start, B_BLOCK)
    return pl.ds(start, B_BLOCK, stride=dilation)


ATT_UNROLL = 16


def _band_mask(has_prev):
    qi = lax.broadcasted_iota(jnp.int32, (B_BLOCK, 2 * B_BLOCK), 0)
    kj = lax.broadcasted_iota(jnp.int32, (B_BLOCK, 2 * B_BLOCK), 1)
    return ((kj < B_BLOCK) & (kj >= qi) & has_prev) | ((kj >= B_BLOCK) & (kj - B_BLOCK <= qi))


def _attn_fwd_body(qkv_ref, z_ref, cos_ref, sin_ref, gain_ref, og_ref, o_ref, lse_ref,
                   qn_scr, kn_scr, og_scr, lg_scr):
    head, grp = pl.program_id(1), pl.program_id(2)
    s_len = z_ref.shape[0]
    n_blocks = s_len // B_BLOCK
    cos_t, sin_t = cos_ref[...], sin_ref[...]

    for gi, dil in enumerate(B_DILATIONS):
        @pl.when(grp == gi)
        def _(gi=gi, dil=dil):
            qn_scr[...], _ = _norm_rope(qkv_ref[0], gain_ref[gi:gi + 1, :], cos_t, sin_t)
            kn_scr[...], _ = _norm_rope(qkv_ref[1], gain_ref[B_GROUPS + gi:B_GROUPS + gi + 1, :], cos_t, sin_t)

            ones = jnp.ones((2 * B_BLOCK, B_DH), BF16)

            def blocks(it, carry):
                scored = []
                for j in range(ATT_UNROLL):
                    cur, prev, has_prev = _stream_rows(it * ATT_UNROLL + j, dil, s_len)
                    rc, rp = _rows(cur, dil), _rows(prev, dil)
                    k2 = jnp.concatenate([kn_scr[rp, :], kn_scr[rc, :]], axis=0)
                    scored.append((rc, rp, has_prev, _mm_nt(qn_scr[rc, :], k2) * ATT_SCALE))
                summed = []
                for rc, rp, has_prev, s in scored:
                    s = jnp.where(_band_mask(has_prev), s, NEG_BIG)
                    mx = jnp.max(s, axis=1, keepdims=True)
                    v2 = jnp.concatenate([qkv_ref.at[2][rp, :], qkv_ref.at[2][rc, :]], axis=0)
                    acc = jnp.dot(_bf(jnp.exp(s - mx)), jnp.concatenate([_bf(v2), ones], axis=1),
                                  preferred_element_type=F32)
                    summed.append((rc, mx, acc))
                for rc, mx, acc in summed:
                    den = acc[:, B_DH:B_DH + 1]
                    og_scr.at[gi][rc, :] = acc[:, :B_DH] / den
                    lg_scr.at[gi][rc, :] = jnp.broadcast_to(mx + jnp.log(den), (B_BLOCK, B_DH))
                return carry

            lax.fori_loop(0, n_blocks // ATT_UNROLL, blocks, 0)

    @pl.when(grp == B_GROUPS - 1)
    def _():
        l0, l1, l2 = lg_scr[0], lg_scr[1], lg_scr[2]
        mx = jnp.maximum(jnp.maximum(l0, l1), l2)
        w0, w1, w2 = jnp.exp(l0 - mx), jnp.exp(l1 - mx), jnp.exp(l2 - mx)
        den = w0 + w1 + w2
        o = (w0 * og_scr[0] + w1 * og_scr[1] + w2 * og_scr[2]) / den
        silu, _ = _silu_parts(z_ref[...])
        o_ref[...] = o
        og_ref[...] = (o * silu).astype(og_ref.dtype)
        @pl.when(head == 0)
        def _():
            lse_ref[...] = jnp.zeros_like(lse_ref)

        lane = lax.broadcasted_iota(jnp.int32, o.shape, 1)
        lse_ref[...] = jnp.where(lane == head, mx + jnp.log(den), lse_ref[...])


def attn_fwd(proj_b, cos_t, sin_t, gains8, n_seq):
    t_rows = proj_b.shape[1]
    s_len = t_rows // n_seq
    head_blk = pl.BlockSpec((s_len, B_DH), lambda b, h, g: (b, h))
    seq_blk = pl.BlockSpec((s_len, 128), lambda b, h, g: (b, 0))
    return pl.pallas_call(
        _attn_fwd_body, name="attn_fwd", grid=(n_seq, B_HEADS, B_GROUPS),
        in_specs=[
            pl.BlockSpec((3, s_len, B_DH), lambda b, h, g: (g, b, h)),
            pl.BlockSpec((None, s_len, B_DH), lambda b, h, g: (B_PIECES - 1, b, h)),
            seq_blk, seq_blk,
            pl.BlockSpec((8, 128), lambda b, h, g: (0, 0)),
        ],
        out_specs=[head_blk, head_blk, seq_blk],
        out_shape=[jax.ShapeDtypeStruct((t_rows, B_W), BF16),
                   jax.ShapeDtypeStruct((t_rows, B_W), F32),
                   jax.ShapeDtypeStruct((t_rows, 128), F32)],
        scratch_shapes=[pltpu.VMEM((s_len, B_DH), F32), pltpu.VMEM((s_len, B_DH), F32),
                        pltpu.VMEM((B_GROUPS, s_len, B_DH), F32), pltpu.VMEM((B_GROUPS, s_len, B_DH), F32)],
        compiler_params=_params(("arbitrary", "arbitrary", "arbitrary")),
    )(proj_b, proj_b, cos_t, sin_t, gains8)


def _attn_bwd_body(qkv_ref, z_ref, cos_ref, sin_ref, gain_ref, dog_ref, o_ref, lse_ref,
                   dqkv_ref, dz_ref, dgain_ref,
                   qn_scr, kn_scr, dqn_scr, dkn_scr, do_scr, dl_scr, ls_scr, dv_scr):
    head, grp = pl.program_id(1), pl.program_id(2)
    first = (pl.program_id(0) == 0) & (head == 0) & (grp == 0)
    s_len = z_ref.shape[0]
    n_blocks = s_len // B_BLOCK
    cos_t, sin_t = cos_ref[...], sin_ref[...]

    @pl.when(first)
    def _():
        dgain_ref[...] = jnp.zeros_like(dgain_ref)

    @pl.when(grp == 0)
    def _():
        d_og, o = dog_ref[...], o_ref[...]
        silu, dsilu = _silu_parts(z_ref[...])
        d_o = d_og * silu
        dz_ref[...] = (d_og * o * dsilu).astype(dz_ref.dtype)
        do_scr[...] = d_o
        dl_scr[...] = jnp.broadcast_to(jnp.sum(d_o * o, axis=1, keepdims=True), o.shape)
        lane = lax.broadcasted_iota(jnp.int32, o.shape, 1)
        ls_scr[...] = jnp.broadcast_to(
            jnp.sum(jnp.where(lane == head, lse_ref[...], 0.0), axis=1, keepdims=True), o.shape)

    for gi, dil in enumerate(B_DILATIONS):
        @pl.when(grp == gi)
        def _(gi=gi, dil=dil):
            q_raw, k_raw = qkv_ref[0], qkv_ref[1]
            gq = gain_ref[gi:gi + 1, :]
            gk = gain_ref[B_GROUPS + gi:B_GROUPS + gi + 1, :]
            qn_scr[...], rq = _norm_rope(q_raw, gq, cos_t, sin_t)
            kn_scr[...], rk = _norm_rope(k_raw, gk, cos_t, sin_t)
            def blocks(it, carry):
                scored = []
                for j in range(ATT_UNROLL):
                    cur, prev, has_prev = _stream_rows(it * ATT_UNROLL + j, dil, s_len)
                    rc, rp = _rows(cur, dil), _rows(prev, dil)
                    qb, d_ob = _bf(qn_scr[rc, :]), _bf(do_scr[rc, :])
                    k2 = _bf(jnp.concatenate([kn_scr[rp, :], kn_scr[rc, :]], axis=0))
                    v2 = _bf(jnp.concatenate([qkv_ref.at[2][rp, :], qkv_ref.at[2][rc, :]], axis=0))
                    scored.append((rc, rp, has_prev, qb, d_ob, k2,
                                   _mm_nt(qb, k2) * ATT_SCALE, _mm_nt(d_ob, v2)))
                grads = []
                for rc, rp, has_prev, qb, d_ob, k2, s, d_p in scored:
                    p = jnp.exp(jnp.where(_band_mask(has_prev), s - ls_scr[rc, :][:, 0:1], NEG_BIG))
                    ds = _bf(p * (d_p - dl_scr[rc, :][:, 0:1]))
                    grads.append((rc, rp, has_prev,
                                  _mm(ds, k2) * ATT_SCALE, _mm_tn(ds, qb) * ATT_SCALE, _mm_tn(_bf(p), d_ob)))
                for j, (rc, rp, has_prev, dq, dk2, dv2) in enumerate(grads):
                    dqn_scr[rc, :] = dq
                    if j == 0:
                        @pl.when(has_prev)
                        def _():
                            dkn_scr[rp, :] += dk2[:B_BLOCK]
                            dv_scr[rp, :] += dv2[:B_BLOCK]
                    if j + 1 < ATT_UNROLL:
                        dkn_scr[rc, :] = dk2[B_BLOCK:] + grads[j + 1][4][:B_BLOCK]
                        dv_scr[rc, :] = dv2[B_BLOCK:] + grads[j + 1][5][:B_BLOCK]
                    else:
                        dkn_scr[rc, :] = dk2[B_BLOCK:]
                        dv_scr[rc, :] = dv2[B_BLOCK:]
                return carry

            lax.fori_loop(0, n_blocks // ATT_UNROLL, blocks, 0)
            dq, dgq = _norm_rope_bwd(q_raw, rq, gq, cos_t, sin_t, dqn_scr[...])
            dk, dgk = _norm_rope_bwd(k_raw, rk, gk, cos_t, sin_t, dkn_scr[...])
            dqkv_ref[0] = dq.astype(dqkv_ref.dtype)
            dqkv_ref[1] = dk.astype(dqkv_ref.dtype)
            dqkv_ref[2] = dv_scr[...].astype(dqkv_ref.dtype)
            dgain_ref[gi:gi + 1, :] += dgq
            dgain_ref[B_GROUPS + gi:B_GROUPS + gi + 1, :] += dgk


def attn_bwd(proj_b, cos_t, sin_t, gains8, d_og, o, lse, n_seq):
    t_rows = proj_b.shape[1]
    s_len = t_rows // n_seq
    head_blk = pl.BlockSpec((s_len, B_DH), lambda b, h, g: (b, h))
    seq_blk = pl.BlockSpec((s_len, 128), lambda b, h, g: (b, 0))
    grp_blk = pl.BlockSpec((3, s_len, B_DH), lambda b, h, g: (g, b, h))
    gain_blk = pl.BlockSpec((8, 128), lambda b, h, g: (0, 0))
    return pl.pallas_call(
        _attn_bwd_body, name="attn_bwd", grid=(n_seq, B_HEADS, B_GROUPS),
        in_specs=[
            grp_blk,
            pl.BlockSpec((None, s_len, B_DH), lambda b, h, g: (B_PIECES - 1, b, h)),
            seq_blk, seq_blk, gain_blk, head_blk, head_blk, seq_blk,
        ],
        out_specs=[grp_blk, head_blk, gain_blk],
        out_shape=[jax.ShapeDtypeStruct((3 * B_GROUPS, t_rows, B_W), BF16),
                   jax.ShapeDtypeStruct((t_rows, B_W), BF16),
                   jax.ShapeDtypeStruct((8, 128), F32)],
        scratch_shapes=[pltpu.VMEM((s_len, B_DH), F32) for _ in range(8)],
        compiler_params=_params(("arbitrary", "arbitrary", "arbitrary")),
    )(proj_b, proj_b, cos_t, sin_t, gains8, d_og, o, lse)


def rope_tables(positions):
    inv_freq = ROPE_THETA ** (-jnp.arange(0, ROPE_DIMS, 2, dtype=F32) / ROPE_DIMS)
    ang = positions.astype(F32)[:, None] * inv_freq
    cos, sin = jnp.cos(ang), jnp.sin(ang)
    t_rows = positions.shape[0]
    rest = B_DH - ROPE_DIMS
    cos_t = jnp.concatenate([cos, cos, jnp.ones((t_rows, rest), F32)], axis=1)
    sin_t = jnp.concatenate([-sin, sin, jnp.zeros((t_rows, rest), F32)], axis=1)
    return cos_t, sin_t


def _rms_fwd_body(x_ref, g_ref, *rest, layer):
    h_ref = rest[-1]
    x = x_ref[...]
    r = lax.rsqrt(jnp.mean(x * x, axis=1, keepdims=True) + EPS)
    h_ref[...] = (x * r * g_ref[layer:layer + 1, :]).astype(h_ref.dtype)


def rms_fwd(x, gains8, layer, after=None):
    t_rows, d = x.shape
    tm = min(ROW_TILE, t_rows)
    in_specs = [pl.BlockSpec((tm, d), lambda i: (i, 0)), pl.BlockSpec((8, d), lambda i: (0, 0))]
    args = [x, gains8]
    if after is not None:
        in_specs.append(HBM_SPEC)
        args.append(after)
    return pl.pallas_call(
        functools.partial(_rms_fwd_body, layer=layer), name=f"rms_fwd_{layer}", grid=(t_rows // tm,),
        in_specs=in_specs,
        out_specs=pl.BlockSpec((tm, d), lambda i: (i, 0)),
        out_shape=jax.ShapeDtypeStruct((t_rows, d), BF16),
        compiler_params=_params(("arbitrary",)),
    )(*args)


def _rms_bwd_body(x_ref, g_ref, dh_ref, res_ref, dx_ref, dg_ref, *, layer):
    x, dh = x_ref[...], dh_ref[...]
    r = lax.rsqrt(jnp.mean(x * x, axis=1, keepdims=True) + EPS)
    xr = x * r
    u = dh * g_ref[layer:layer + 1, :]
    dx_ref[...] = res_ref[...] + r * u - xr * (r * r) * jnp.mean(u * x, axis=1, keepdims=True)

    @pl.when(pl.program_id(0) == 0)
    def _():
        dg_ref[...] = jnp.zeros_like(dg_ref)

    dg_ref[0:1, :] += jnp.sum(dh * xr, axis=0, keepdims=True)


def rms_bwd(x, gains8, layer, dh, d_res):
    t_rows, d = x.shape
    tm = min(ROW_TILE, t_rows)
    blk = pl.BlockSpec((tm, d), lambda i: (i, 0))
    gblk = pl.BlockSpec((8, d), lambda i: (0, 0))
    return pl.pallas_call(
        functools.partial(_rms_bwd_body, layer=layer), name=f"rms_bwd_{layer}", grid=(t_rows // tm,),
        in_specs=[blk, gblk, blk, blk],
        out_specs=[blk, gblk],
        out_shape=[jax.ShapeDtypeStruct((t_rows, d), F32), jax.ShapeDtypeStruct((8, d), F32)],
        compiler_params=_params(("arbitrary",)),
    )(x, gains8, dh, d_res)


def _piece_col(p):
    return jnp.where(p < 3 * B_GROUPS, (p % 3) * B_GROUPS + p // 3, 3 * B_GROUPS)


def _mm_nn_body(a_ref, w_ref, *rest, has_res):
    o_ref = rest[-1]
    acc = jnp.dot(a_ref[...], w_ref[...], preferred_element_type=F32)
    if has_res:
        acc = acc + rest[0][...]
    o_ref[...] = acc


def mm_nn(a, w, residual=None, *, tn, name):
    m, k = a.shape
    n = w.shape[1]
    tm = min(NT_ROW_TILE, m)
    in_specs = [pl.BlockSpec((tm, k), lambda j, i: (i, 0)), pl.BlockSpec((k, tn), lambda j, i: (0, j))]
    args = [a, w]
    if residual is not None:
        in_specs.append(pl.BlockSpec((tm, tn), lambda j, i: (i, j)))
        args.append(residual)
    return pl.pallas_call(
        functools.partial(_mm_nn_body, has_res=residual is not None), name=name, grid=(n // tn, m // tm),
        in_specs=in_specs,
        out_specs=pl.BlockSpec((tm, tn), lambda j, i: (i, j)),
        out_shape=jax.ShapeDtypeStruct((m, n), F32),
        compiler_params=_params(("arbitrary", "arbitrary")),
    )(*args)


def mm_nn_pieces(a, w, *, name):
    m, k = a.shape
    tm = min(NT_ROW_TILE, m)
    return pl.pallas_call(
        functools.partial(_mm_nn_body, has_res=False), name=name, grid=(B_PIECES, m // tm),
        in_specs=[pl.BlockSpec((tm, k), lambda p, i: (i, 0)),
                  pl.BlockSpec((k, B_W), lambda p, i: (0, _piece_col(p)))],
        out_specs=pl.BlockSpec((None, tm, B_W), lambda p, i: (p, i, 0)),
        out_shape=jax.ShapeDtypeStruct((B_PIECES, m, B_W), F32),
        compiler_params=_params(("arbitrary", "arbitrary")),
    )(a, w)


NT_ROW_TILE = 1024


def _mm_nt_body(g_ref, w_ref, *rest, has_init):
    o_ref = rest[-1]
    j = pl.program_id(1)
    part = lax.dot_general(_bf(g_ref[...]), w_ref[...], (((1,), (1,)), ((), ())), preferred_element_type=F32)

    @pl.when(j == 0)
    def _():
        o_ref[...] = part + rest[0][...] if has_init else part

    @pl.when(j > 0)
    def _():
        o_ref[...] += part


def mm_nt(g, w, init=None, *, tn, col_off=0, name, after=None):
    m, n = g.shape
    k = w.shape[0]
    tm = min(NT_ROW_TILE, m)
    in_specs = [pl.BlockSpec((tm, tn), lambda i, j: (i, j)),
                pl.BlockSpec((k, tn), lambda i, j: (0, col_off + j))]
    args = [g, w]
    if init is not None:
        in_specs.append(pl.BlockSpec((tm, k), lambda i, j: (i, 0)))
        args.append(init)
    if after is not None:
        in_specs.append(HBM_SPEC)
        args.append(after)
    return pl.pallas_call(
        functools.partial(_mm_nt_body, has_init=init is not None), name=name, grid=(m // tm, n // tn),
        in_specs=in_specs,
        out_specs=pl.BlockSpec((tm, k), lambda i, j: (i, 0)),
        out_shape=jax.ShapeDtypeStruct((m, k), F32),
        compiler_params=_params(("arbitrary", "arbitrary")),
    )(*args)


def mm_nt_pieces(g9, w, *, name):
    n_p, m, _ = g9.shape
    k = w.shape[0]
    tm = min(NT_ROW_TILE, m)
    return pl.pallas_call(
        functools.partial(_mm_nt_body, has_init=False), name=name, grid=(m // tm, n_p),
        in_specs=[pl.BlockSpec((None, tm, B_W), lambda i, p: (p, i, 0)),
                  pl.BlockSpec((k, B_W), lambda i, p: (0, _piece_col(p)))],
        out_specs=pl.BlockSpec((tm, k), lambda i, p: (i, 0)),
        out_shape=jax.ShapeDtypeStruct((m, k), F32),
        compiler_params=_params(("arbitrary", "arbitrary")),
    )(g9, w)


def _mm_tn_body(a_ref, g_ref, o_ref):
    o_ref[...] = lax.dot_general(a_ref[...], _bf(g_ref[...]), (((0,), (0,)), ((), ())),
                                 preferred_element_type=F32).astype(o_ref.dtype)


def mm_tn(a, g, *, tn, out_dtype, name):
    m, k = a.shape
    n = g.shape[1]
    return pl.pallas_call(
        _mm_tn_body, name=name, grid=(n // tn,),
        in_specs=[pl.BlockSpec((m, k), lambda j: (0, 0)), pl.BlockSpec((m, tn), lambda j: (0, j))],
        out_specs=pl.BlockSpec((k, tn), lambda j: (0, j)),
        out_shape=jax.ShapeDtypeStruct((k, n), out_dtype),
        compiler_params=_params(("arbitrary",)),
    )(a, g)


B_UNIT = 256
B_IN_COLS = B_PIECES * B_W
B_SHARD_UNITS = B_IN_COLS // N_DEV // B_UNIT


def mm_tn_b_in(a, g9, gz, *, out_dtype, name):
    m, k = a.shape
    per_piece = B_W // B_UNIT
    n_units = B_IN_COLS // B_UNIT

    def g_map(u):
        nat = jnp.minimum(u // per_piece, 3 * B_GROUPS - 1)
        piece = (nat % B_GROUPS) * 3 + nat // B_GROUPS
        return (piece, 0, u % per_piece)

    def body(a_ref, g_ref, z_ref, o_ref):
        u = pl.program_id(0)

        @pl.when(u < 3 * B_GROUPS * per_piece)
        def _():
            _mm_tn_body(a_ref, g_ref, o_ref)

        @pl.when(u >= 3 * B_GROUPS * per_piece)
        def _():
            _mm_tn_body(a_ref, z_ref, o_ref)

    return pl.pallas_call(
        body, name=name, grid=(n_units,),
        in_specs=[pl.BlockSpec((m, k), lambda u: (0, 0)),
                  pl.BlockSpec((None, m, B_UNIT), g_map),
                  pl.BlockSpec((m, B_UNIT), lambda u: (0, jnp.where(u < 3 * B_GROUPS * per_piece, 0, u % per_piece)))],
        out_specs=pl.BlockSpec((None, k, B_UNIT), lambda u: (u // B_SHARD_UNITS, 0, u % B_SHARD_UNITS)),
        out_shape=jax.ShapeDtypeStruct((N_DEV, k, B_IN_COLS // N_DEV), out_dtype),
        compiler_params=_params(("arbitrary",)),
    )(a, g9, gz)


def _loss_body(y_ref, t_ref, dy_ref, loss_ref, acc):
    i = pl.program_id(0)
    d = y_ref.shape[1]
    err = y_ref[...] - t_ref[...]
    dy_ref[...] = err * (1.0 / d)

    @pl.when(i == 0)
    def _():
        acc[...] = jnp.zeros_like(acc)

    acc[...] += jnp.sum(err * err, axis=0, keepdims=True)

    @pl.when(i == pl.num_programs(0) - 1)
    def _():
        total = jnp.sum(acc[...], axis=1, keepdims=True) * (0.5 / d)
        loss_ref[...] = jnp.broadcast_to(total, loss_ref.shape)


def loss_head(y, target):
    t_rows, d = y.shape
    tm = min(ROW_TILE, t_rows)
    blk = pl.BlockSpec((tm, d), lambda i: (i, 0))
    return pl.pallas_call(
        _loss_body, name="loss_head", grid=(t_rows // tm,),
        in_specs=[blk, blk],
        out_specs=[blk, pl.BlockSpec((8, 128), lambda i: (0, 0))],
        out_shape=[jax.ShapeDtypeStruct((t_rows, d), F32), jax.ShapeDtypeStruct((8, 128), F32)],
        scratch_shapes=[pltpu.VMEM((1, d), F32)],
        compiler_params=_params(("arbitrary",)),
    )(y, target)


def _adamw_body(p_ref, w_ref, m_ref, v_ref, g_ref, d_ref, nm_ref, nv_ref):
    g = p_ref[0].astype(F32)
    for s in range(1, N_DEV):
        g = g + p_ref[s].astype(F32)
    w = w_ref[...]
    m = ADAM_B1 * m_ref[...] + (1.0 - ADAM_B1) * g
    v = ADAM_B2 * v_ref[...] + (1.0 - ADAM_B2) * (g * g)
    m_hat = m / (1.0 - ADAM_B1 ** ADAM_STEP)
    v_hat = v / (1.0 - ADAM_B2 ** ADAM_STEP)
    g_ref[...] = g
    d_ref[...] = -ADAM_LR * (m_hat / (jnp.sqrt(v_hat) + ADAM_EPS) + ADAM_WD * w)
    nm_ref[...] = m
    nv_ref[...] = v


def adamw(parts, w, m, v, *, name):
    _, r, c = w.shape
    tr = r if r <= 256 else 256
    blk = pl.BlockSpec((None, tr, c), lambda i: (0, i, 0))
    out = jax.ShapeDtypeStruct((1, r, c), F32)
    return pl.pallas_call(
        _adamw_body, name=name, grid=(r // tr,),
        in_specs=[pl.BlockSpec((N_DEV, tr, c), lambda i: (0, i, 0)), blk, blk, blk],
        out_specs=[blk, blk, blk, blk],
        out_shape=[out, out, out, out],
        compiler_params=_params(("arbitrary",)),
    )(parts, w, m, v)


MESH_ID = pl.DeviceIdType.MESH
HBM_SPEC = pl.BlockSpec(memory_space=pl.ANY)


def _my_place():
    return lax.axis_index("x"), lax.axis_index("y"), lax.axis_index("c")


def _flat(x, y, c):
    return 4 * x + 2 * y + c


def _all_gather_body(*refs, n):
    ins, outs = refs[:n], refs[n:2 * n]
    send_sems, recv_sems, local_sems = refs[2 * n:]
    x, y, c = _my_place()
    me, sibling = (x, y, c), (x, y, 1 - c)
    chips = [(1 - x, y), (x, 1 - y), (1 - x, 1 - y)]
    pending = []
    for a in range(n):
        src, out = ins[a], outs[a]

        def copy(k, block, to, from_input=False, a=a, src=src, out=out):
            slot = out.at[_flat(*block)]
            return pltpu.make_async_remote_copy(
                src_ref=src if from_input else slot, dst_ref=slot,
                send_sem=send_sems.at[7 * a + k], recv_sem=recv_sems.at[7 * a + k],
                device_id=to, device_id_type=MESH_ID)

        mine = pltpu.make_async_copy(src, out.at[_flat(*me)], local_sems.at[a])
        mine.start()
        first = [copy(0, me, sibling, True)] + [copy(1 + j, me, (*chip, c), True) for j, chip in enumerate(chips)]
        for cp in first:
            cp.start()
        pending.append((copy, mine, first))
    for copy, mine, first in pending:
        passed = [copy(4 + j, (*chip, c), sibling) for j, chip in enumerate(chips)]
        for j, chip in enumerate(chips):
            copy(1 + j, (*chip, c), me).wait_recv()
            passed[j].start()
        copy(0, sibling, me).wait_recv()
        for j, chip in enumerate(chips):
            copy(4 + j, (*chip, 1 - c), me).wait_recv()
        for cp in first + passed:
            cp.wait_send()
        mine.wait()


def all_gather(shards, *, name):
    n = len(shards)
    return pl.pallas_call(
        functools.partial(_all_gather_body, n=n), name=name,
        in_specs=[HBM_SPEC] * n, out_specs=[HBM_SPEC] * n,
        out_shape=[jax.ShapeDtypeStruct((N_DEV,) + s.shape, s.dtype) for s in shards],
        scratch_shapes=[pltpu.SemaphoreType.DMA((7 * n,)), pltpu.SemaphoreType.DMA((7 * n,)),
                        pltpu.SemaphoreType.DMA((n,))],
    )(*shards)


PEER_FLIPS = [(0, 0, 1), (1, 0, 0), (0, 1, 0), (1, 1, 0), (1, 0, 1), (0, 1, 1), (1, 1, 1)]


def _all_to_all_body(*refs, n):
    ins, outs = refs[:n], refs[n:2 * n]
    send_sems, recv_sems, local_sems = refs[2 * n:]
    x, y, c = _my_place()
    me = _flat(x, y, c)
    waits = []
    for a in range(n):
        src, out = ins[a], outs[a]
        mine = pltpu.make_async_copy(src.at[me], out.at[me], local_sems.at[a])
        mine.start()
        waits.append(mine)
        for k, (fx, fy, fc) in enumerate(PEER_FLIPS):
            peer = (1 - x if fx else x, 1 - y if fy else y, 1 - c if fc else c)
            theirs = _flat(*peer)
            sems = dict(send_sem=send_sems.at[7 * a + k], recv_sem=recv_sems.at[7 * a + k],
                        device_id=peer, device_id_type=MESH_ID)
            send = pltpu.make_async_remote_copy(src_ref=src.at[theirs], dst_ref=out.at[me], **sems)
            send.start()
            recv = pltpu.make_async_remote_copy(src_ref=src.at[theirs], dst_ref=out.at[theirs], **sems)
            waits.append((send, recv))
    for w in waits:
        if isinstance(w, tuple):
            w[0].wait_send()
            w[1].wait_recv()
        else:
            w.wait()


def all_to_all(parts, *, name):
    n = len(parts)
    return pl.pallas_call(
        functools.partial(_all_to_all_body, n=n), name=name,
        in_specs=[HBM_SPEC] * n, out_specs=[HBM_SPEC] * n,
        out_shape=[jax.ShapeDtypeStruct(p.shape, p.dtype) for p in parts],
        scratch_shapes=[pltpu.SemaphoreType.DMA((7 * n,)), pltpu.SemaphoreType.DMA((7 * n,)),
                        pltpu.SemaphoreType.DMA((n,))],
    )(*parts)


HBM_ONLY = pl.BlockSpec(memory_space=pltpu.HBM)
SEM_SPEC = pl.BlockSpec(memory_space=pltpu.SEMAPHORE)
DATAFLOW_EFFECT = pltpu.SideEffectType.DATAFLOW_SIDE_EFFECTING


def _split_copies(srcs, lands, send_sems, recv_sems, n, scatter):
    x, y, c = _my_place()
    me = _flat(x, y, c)
    pairs = []
    for a in range(n):
        for k, (fx, fy, fc) in enumerate(PEER_FLIPS):
            peer = (1 - x if fx else x, 1 - y if fy else y, 1 - c if fc else c)
            theirs = _flat(*peer)
            src = srcs[a].at[theirs] if scatter else srcs[a]
            sems = dict(send_sem=send_sems.at[7 * a + k], recv_sem=recv_sems.at[7 * a + k],
                        device_id=peer, device_id_type=MESH_ID)
            pairs.append((pltpu.make_async_remote_copy(src_ref=src, dst_ref=lands[a].at[me], **sems),
                          pltpu.make_async_remote_copy(src_ref=src, dst_ref=lands[a].at[theirs], **sems)))
    return pairs


def _exchange_start_body(*refs, n, scatter):
    srcs, lands = refs[:n], refs[n:2 * n]
    send_sems, recv_sems = refs[2 * n], refs[2 * n + 1]
    token = refs[-1]
    for send, _ in _split_copies(srcs, lands, send_sems, recv_sems, n, scatter):
        send.start()
    token[...] = jnp.zeros_like(token)


def exchange_start(srcs, lands, *, scatter, name):
    n = len(srcs)
    args = [pltpu.with_memory_space_constraint(t, pltpu.HBM) for t in list(srcs) + list(lands)]
    outs = pl.pallas_call(
        functools.partial(_exchange_start_body, n=n, scatter=scatter), name=name,
        out_shape=(pltpu.SemaphoreType.DMA((7 * n,)), pltpu.SemaphoreType.DMA((7 * n,)),
                   *[pltpu.HBM(t.shape, t.dtype) for t in args],
                   jax.ShapeDtypeStruct((8, 128), F32)),
        in_specs=[HBM_ONLY] * (2 * n),
        out_specs=(SEM_SPEC, SEM_SPEC, *[HBM_ONLY] * (2 * n), pl.BlockSpec(memory_space=pltpu.VMEM)),
        input_output_aliases={i: 2 + i for i in range(2 * n)},
        compiler_params=pltpu.CompilerParams(has_side_effects=DATAFLOW_EFFECT),
    )(*args)
    return outs[0], outs[1], outs[2:2 + n], outs[2 + n:2 + 2 * n], outs[-1]


def _exchange_wait_body(*refs, n, scatter):
    srcs, lands = refs[:n], refs[n:2 * n]
    send_sems, recv_sems = refs[2 * n], refs[2 * n + 1]
    for send, recv in _split_copies(srcs, lands, send_sems, recv_sems, n, scatter):
        send.wait_send()
        recv.wait_recv()


def exchange_wait(send_sems, recv_sems, srcs, lands, after, *, scatter, name):
    n = len(srcs)
    outs = pl.pallas_call(
        functools.partial(_exchange_wait_body, n=n, scatter=scatter), name=name,
        out_shape=tuple(pltpu.HBM(t.shape, t.dtype) for t in list(srcs) + list(lands)),
        in_specs=[HBM_ONLY] * (2 * n) + [SEM_SPEC, SEM_SPEC, HBM_SPEC],
        out_specs=tuple([HBM_ONLY] * (2 * n)),
        input_output_aliases={i: i for i in range(2 * n)},
        compiler_params=pltpu.CompilerParams(has_side_effects=DATAFLOW_EFFECT),
    )(*srcs, *lands, send_sems, recv_sems, after)
    return outs[n:]


def _own_slot_only(shape_dtype, own, slot):
    land = lax.empty(shape_dtype.shape, shape_dtype.dtype)
    return lax.dynamic_update_slice(land, own[None], (slot,) + (0,) * own.ndim)


def _pad_rows(a, rows=8):
    return jnp.pad(a, ((0, rows - a.shape[0]), (0, 0)))


def _gate_rows(a_log, dt_bias):
    z = jnp.zeros((8, 128), F32)
    return z.at[0, A_HEADS:2 * A_HEADS].set(a_log[0]).at[1, A_HEADS:2 * A_HEADS].set(dt_bias[0])


def _pack_small(norm_g, a_log, a_dt_bias, a_norm_g, b_q_norm_g, b_k_norm_g):
    return jnp.concatenate([
        norm_g[0].reshape(8, 128), norm_g[1].reshape(8, 128),
        _gate_rows(a_log, a_dt_bias),
        _pad_rows(a_norm_g[0].reshape(2, 128)),
        _pad_rows(jnp.concatenate([b_q_norm_g[0], b_k_norm_g[0]], axis=0)),
    ], axis=0)


def _unpack_small(p):
    return (p[0:16].reshape(2, D_MODEL), p[16:17, A_HEADS:2 * A_HEADS], p[17:18, A_HEADS:2 * A_HEADS],
            p[24:26].reshape(1, A_DV), p[32:35][None], p[35:38][None])


def kernel(x, positions, norm_g, a_w_in, a_conv_w, a_log, a_dt_bias, a_norm_g, a_w_out, b_w_in, b_q_norm_g, b_k_norm_g, b_w_out, loss_target, m_norm_g, m_a_w_in, m_a_conv_w, m_a_log, m_a_dt_bias, m_a_norm_g, m_a_w_out, m_b_w_in, m_b_q_norm_g, m_b_k_norm_g, m_b_w_out, v_norm_g, v_a_w_in, v_a_conv_w, v_a_log, v_a_dt_bias, v_a_norm_g, v_a_w_out, v_b_w_in, v_b_q_norm_g, v_b_k_norm_g, v_b_w_out):
    n_seq, s_len, d = x.shape
    t_rows = n_seq * s_len
    n_chunks = s_len // CHUNK
    x0 = x.reshape(t_rows, d)
    target = loss_target.reshape(t_rows, d)
    my_slot = _flat(*_my_place())

    g_a_in, g_conv = all_gather([a_w_in[0].astype(BF16), _pad_rows(a_conv_w[0])], name="gather_weights_first")
    later = [a_w_out[0].astype(BF16), b_w_in[0].astype(BF16), b_w_out[0].astype(BF16)]
    lands = [_own_slot_only(jax.ShapeDtypeStruct((N_DEV,) + t.shape, t.dtype), t, my_slot) for t in later]
    w_send, w_recv, later, lands, w_token = exchange_start(later, lands, scatter=False, name="gather_weights_start")
    w_a_in = jnp.pad(g_a_in.transpose(1, 0, 2).reshape(d, A_IN), ((0, 0), (0, A_IN_PAD - A_IN)))
    conv_w8 = g_conv.transpose(1, 0, 2).reshape(8, 2 * A_QK + A_VW)

    gains_model = _pad_rows(norm_g)
    gate_prm = _gate_rows(a_log, a_dt_bias)
    gain_a_out = _pad_rows(a_norm_g)
    gains_qk = _pad_rows(jnp.concatenate([b_q_norm_g[0], b_k_norm_g[0]], axis=0))
    cos_t, sin_t = rope_tables(positions.reshape(t_rows))

    h0 = rms_fwd(x0, gains_model, 0, after=w_token)
    proj_a = mm_nn(h0, w_a_in, tn=896, name="proj_a")
    gates_col, gates_row = a_gates_fwd(proj_a, gate_prm, n_seq)
    gates_row = gates_row.reshape(n_seq, 2 * A_HEADS, s_len // SUPER, 1, SUPER)
    o_a, tinv, states, og_a, q_a, k_a, v_a = gdn_fwd(proj_a, conv_w8, gates_col, gates_row, gain_a_out, n_seq)
    g_a_out, g_b_in, g_b_out = exchange_wait(w_send, w_recv, later, lands, og_a, scatter=False,
                                             name="gather_weights_wait")
    w_a_out = g_a_out.reshape(A_VW, d)
    w_b_in = g_b_in.transpose(1, 0, 2).reshape(d, B_IN_COLS)
    w_b_out = g_b_out.reshape(B_W, d)
    x1 = mm_nn(og_a, w_a_out, x0, tn=1024, name="out_a")

    h1 = rms_fwd(x1, gains_model, 1)
    proj_b = mm_nn_pieces(h1, w_b_in, name="proj_b")
    og_b, o_b, lse = attn_fwd(proj_b, cos_t, sin_t, gains_qk, n_seq)
    y = mm_nn(og_b, w_b_out, x1, tn=1024, name="out_b")

    dy, loss_blk = loss_head(y, target)
    loss = lax.psum(loss_blk[0, 0], ("x", "y", "c"))

    d_og_b = mm_nt(dy, w_b_out, tn=512, name="d_og_b")
    dw_b_out = mm_tn(og_b, dy, tn=256, out_dtype=BF16, name="dw_b_out")
    dqkv_b, dz_b, d_gains_qk = attn_bwd(proj_b, cos_t, sin_t, gains_qk, d_og_b, o_b, lse, n_seq)
    dh1 = mm_nt_pieces(dqkv_b, w_b_in, name="dh1_qkv")
    dh1 = mm_nt(dz_b, w_b_in, dh1, tn=B_W, col_off=3 * B_GROUPS, name="dh1_z")
    dw_b_in = mm_tn_b_in(h1, dqkv_b, dz_b, out_dtype=BF16, name="dw_b_in")
    dx1, d_gain1 = rms_bwd(x1, gains_model, 1, dh1, dy)

    dw_a_out = mm_tn(og_a, dx1, tn=256, out_dtype=BF16, name="dw_a_out")
    early = [dw_b_in, dw_b_out.reshape(N_DEV, B_W // N_DEV, d), dw_a_out.reshape(N_DEV, A_VW // N_DEV, d)]
    lands = [_own_slot_only(t, lax.dynamic_index_in_dim(t, my_slot, 0, keepdims=False), my_slot) for t in early]
    g_send, g_recv, early, lands, g_token = exchange_start(early, lands, scatter=True, name="scatter_grads_start")

    d_og_a = mm_nt(dx1, w_a_out, tn=512, name="d_og_a", after=g_token)
    d_xq, d_xk, d_xv, dgates, dz_a, d_gain_a_out, d_cq, d_ck, d_cv = gdn_bwd(
        q_a, k_a, v_a, gates_col, gates_row, tinv, states, d_og_a, o_a, proj_a, conv_w8, gain_a_out, n_seq)
    d_conv = jnp.concatenate([d_cq.sum(axis=0), d_ck.sum(axis=0), d_cv.sum(axis=0)], axis=1)
    d_gate_logits, d_gate_prm = a_gates_bwd(proj_a, gate_prm, dgates, n_seq)
    dw_a_in = jnp.concatenate([
        mm_tn(h0, d_xq, tn=256, out_dtype=BF16, name="dw_a_in_q"),
        mm_tn(h0, d_xk, tn=256, out_dtype=BF16, name="dw_a_in_k"),
        mm_tn(h0, d_xv, tn=256, out_dtype=BF16, name="dw_a_in_v"),
        mm_tn(h0, dz_a, tn=256, out_dtype=BF16, name="dw_a_in_z"),
        mm_tn(h0, d_gate_logits, tn=128, out_dtype=BF16, name="dw_a_in_gates"),
    ], axis=1)[:, :A_IN]
    shard_a_in = A_IN // N_DEV
    last = [dw_a_in.reshape(d, N_DEV, shard_a_in).transpose(1, 0, 2)]
    last_lands = [_own_slot_only(t, lax.dynamic_index_in_dim(t, my_slot, 0, keepdims=False), my_slot) for t in last]
    l_send, l_recv, last, last_lands, l_token = exchange_start(last, last_lands, scatter=True,
                                                               name="scatter_last_start")
    dh0 = mm_nt(d_xq, w_a_in, tn=512, name="dh0_q", after=l_token)
    dh0 = mm_nt(d_xk, w_a_in, dh0, tn=512, col_off=A_QK // 512, name="dh0_k")
    dh0 = mm_nt(d_xv, w_a_in, dh0, tn=512, col_off=2 * A_QK // 512, name="dh0_v")
    dh0 = mm_nt(dz_a, w_a_in, dh0, tn=512, col_off=(2 * A_QK + A_VW) // 512, name="dh0_z")
    dh0 = mm_nt(d_gate_logits, w_a_in, dh0, tn=128, col_off=A_GATE_COL, name="dh0_gates")
    dx0, d_gain0 = rms_bwd(x0, gains_model, 0, dh0, dx1)

    small = jnp.concatenate([
        d_gain0[0].reshape(8, 128), d_gain1[0].reshape(8, 128), d_gate_prm,
        _pad_rows(d_gain_a_out[0].reshape(2, 128)), d_gains_qk], axis=0)
    r_small, r_conv = all_gather([small, d_conv], name="gather_small_grads")
    conv_cols = a_conv_w.shape[2]
    r_conv = lax.dynamic_slice(r_conv, (0, 0, my_slot * conv_cols), (N_DEV, 8, conv_cols))

    r_b_in, r_b_out, r_a_out = exchange_wait(g_send, g_recv, early, lands, r_small, scatter=True,
                                             name="scatter_grads_wait")
    (r_a_in,) = exchange_wait(l_send, l_recv, last, last_lands, r_small, scatter=True, name="scatter_last_wait")

    upd = {}
    upd["a_w_in"] = adamw(r_a_in, a_w_in, m_a_w_in, v_a_w_in, name="adamw_a_w_in")
    upd["a_w_out"] = adamw(r_a_out, a_w_out, m_a_w_out, v_a_w_out, name="adamw_a_w_out")
    upd["b_w_in"] = adamw(r_b_in, b_w_in, m_b_w_in, v_b_w_in, name="adamw_b_w_in")
    upd["b_w_out"] = adamw(r_b_out, b_w_out, m_b_w_out, v_b_w_out, name="adamw_b_w_out")
    upd["a_conv_w"] = [t[:, :A_CONV] for t in adamw(
        r_conv, _pad_rows(a_conv_w[0])[None], _pad_rows(m_a_conv_w[0])[None], _pad_rows(v_a_conv_w[0])[None],
        name="adamw_a_conv_w")]
    small_upd = adamw(
        r_small,
        _pack_small(norm_g, a_log, a_dt_bias, a_norm_g, b_q_norm_g, b_k_norm_g)[None],
        _pack_small(m_norm_g, m_a_log, m_a_dt_bias, m_a_norm_g, m_b_q_norm_g, m_b_k_norm_g)[None],
        _pack_small(v_norm_g, v_a_log, v_a_dt_bias, v_a_norm_g, v_b_q_norm_g, v_b_k_norm_g)[None],
        name="adamw_small")
    small_names = ("norm_g", "a_log", "a_dt_bias", "a_norm_g", "b_q_norm_g", "b_k_norm_g")
    unpacked = [_unpack_small(t[0]) for t in small_upd]
    for i, nm in enumerate(small_names):
        upd[nm] = [u[i] for u in unpacked]

    order = ("norm_g", "a_w_in", "a_conv_w", "a_log", "a_dt_bias", "a_norm_g", "a_w_out",
             "b_w_in", "b_q_norm_g", "b_k_norm_g", "b_w_out")
    outs = [loss, dx0.reshape(n_seq, s_len, d)]
    for kind in range(4):
        for nm in order:
            outs.append(upd[nm][kind])
    return tuple(outs)
```

```python
import functools
import math

import jax
import jax.numpy as jnp
from jax import lax
from jax.experimental import pallas as pl
from jax.experimental.pallas import tpu as pltpu

F32 = jnp.float32
BF16 = jnp.bfloat16

D_MODEL = 1024
EPS = 1e-6
N_DEV = 8

A_HEADS = 8
A_DK = 128
A_DV = 256
A_QK = A_HEADS * A_DK
A_VW = A_HEADS * A_DV
A_CONV = 4
CHUNK = 64
A_IN = 2 * A_QK + 2 * A_VW + 2 * A_HEADS
A_IN_PAD = 2 * A_QK + 2 * A_VW + 128
A_GATE_COL = (2 * A_QK + 2 * A_VW) // 128

B_DILATIONS = (1, 4, 16)
B_GROUPS = 3
B_HEADS = 8
B_DH = 128
B_W = B_HEADS * B_DH
B_BLOCK = 128
B_PIECES = 3 * B_GROUPS + 1
ROPE_THETA = 500000.0
ROPE_DIMS = B_DH // 4
ROPE_HALF = ROPE_DIMS // 2

ADAM_LR = 0.001
ADAM_B1 = 0.9
ADAM_B2 = 0.999
ADAM_EPS = 1e-08
ADAM_WD = 0.01
ADAM_STEP = 10

VMEM_LIMIT = 60 * 1024 * 1024


def _params(sem):
    return pltpu.CompilerParams(dimension_semantics=sem, vmem_limit_bytes=VMEM_LIMIT)


def _bf(x):
    return x.astype(BF16)


def _mm(a, b):
    return jnp.dot(_bf(a), _bf(b), preferred_element_type=F32)


def _mm_nt(a, b):
    return lax.dot_general(_bf(a), _bf(b), (((1,), (1,)), ((), ())), preferred_element_type=F32)


def _mm_tn(a, b):
    return lax.dot_general(_bf(a), _bf(b), (((0,), (0,)), ((), ())), preferred_element_type=F32)


def _split(x):
    hi = _bf(x)
    return hi, _bf(x - hi.astype(F32))


def _mm3(a, b):
    ah, al = _split(a)
    bh, bl = _split(b)
    d = functools.partial(jnp.dot, preferred_element_type=F32)
    return d(ah, bh) + (d(ah, bl) + d(al, bh))


def _colsum_as_col(z):
    zh, zl = _split(z)
    ones = jnp.ones((z.shape[0], 128), BF16)
    tn = functools.partial(lax.dot_general, dimension_numbers=(((0,), (0,)), ((), ())),
                           preferred_element_type=F32)
    return (tn(zh, ones) + tn(zl, ones))[:, 0:1]


def _sigmoid(x):
    return 0.5 * jnp.tanh(0.5 * x) + 0.5


INV_BASE = 8
INV_NEWTON = 2
GDN_GROUP = 4
SUPER = GDN_GROUP * CHUNK
GDN_WIDTH = 2

A_K_COL = A_QK // A_DK
A_V_COL = 2 * A_QK // A_DV


def _inverse_steps(m, row, col):
    eye = (row == col).astype(F32)
    d = jnp.where(row // INV_BASE == col // INV_BASE, m, 0.0)
    x = eye - d
    p = _mm(d, d)
    yield
    steps = int(math.log2(INV_BASE)) - 1
    for i in range(steps):
        x = x + _mm(x, p)
        if i + 1 < steps:
            p = _mm(p, p)
        yield
    size = INV_BASE
    while size < CHUNK:
        c = jnp.where((row // (2 * size) == col // (2 * size)) & (row // size != col // size), m, 0.0)
        xc = _mm(x, c)
        yield
        x = x - _mm(xc, x)
        yield
        size *= 2
    for _ in range(INV_NEWTON):
        r = eye - x - _mm3(m, x)
        yield
        x = x + _mm(x, r)
        yield
    return x


def _drain(gen):
    while True:
        try:
            next(gen)
        except StopIteration as stop:
            return stop.value


def _interleave(*gens):
    live = list(gens)
    while live:
        for g in list(live):
            try:
                next(g)
            except StopIteration:
                live.remove(g)


def _diag_blocks_tall(x):
    return jnp.concatenate([x[i * CHUNK:(i + 1) * CHUNK, i * CHUNK:(i + 1) * CHUNK] for i in range(GDN_GROUP)], axis=0)


def _tall_to_block_diag(t, same):
    return jnp.where(same, jnp.concatenate([t] * GDN_GROUP, axis=1), 0.0)


def _block_sum(same, x):
    xh, xl = _split(jnp.broadcast_to(x, (SUPER, 128)))
    ones = same.astype(BF16)
    d = functools.partial(jnp.dot, preferred_element_type=F32)
    return (d(ones, xh) + d(ones, xl))[:, 0:1]


def _aligned_rows(index, size):
    start = index * size
    return pl.ds(start if isinstance(start, int) else pl.multiple_of(start, size), size)


def _super_rows(i):
    return _aligned_rows(i, SUPER)


def _chunk_rows(n):
    return _aligned_rows(n, CHUNK)


def _gdn_super_steps(q, k, v, gcb, gr, head, tinv_tall=None):
    lane = lax.broadcasted_iota(jnp.int32, (SUPER, 128), 1)
    row = lax.broadcasted_iota(jnp.int32, (SUPER, SUPER), 0)
    col = lax.broadcasted_iota(jnp.int32, (SUPER, SUPER), 1)
    same = row // CHUNK == col // CHUNK
    beta = jnp.sum(jnp.where(lane == head, gcb, 0.0), axis=1, keepdims=True)
    gc = jnp.sum(jnp.where(lane == A_HEADS + head, gcb, 0.0), axis=1, keepdims=True)
    g_last = jnp.sum(jnp.where(col == (row // CHUNK) * CHUNK + (CHUNK - 1), gr, 0.0), axis=1, keepdims=True)
    gamma = jnp.exp(gc)
    decay = jnp.where(same & (row >= col), jnp.exp(jnp.minimum(gc - gr, 0.0)), 0.0)
    kb = k * beta
    m = jnp.where(same & (row > col), _mm_nt(kb, k) * decay, 0.0)
    p = jnp.where(same & (row >= col), _mm_nt(q, k) * decay, 0.0)
    yield
    if tinv_tall is None:
        tinv = yield from _inverse_steps(m, row, col)
    else:
        tinv = _tall_to_block_diag(tinv_tall, same)
    u = _mm(tinv, v * beta)
    w = _mm(tinv, kb * gamma)
    yield
    e_tail = jnp.exp(g_last - gc)
    return dict(beta=beta, gc=gc, g_last=g_last, gamma=gamma, decay=decay, kb=kb, m=m,
                tinv=tinv, u=u, w=w, p=p, e_tail=e_tail, row=row, col=col, lane=lane, same=same)


def _gdn_super_common(q, k, v, gcb, gr, head, tinv_tall=None):
    return _drain(_gdn_super_steps(q, k, v, gcb, gr, head, tinv_tall))


def _store_scan_operands(rows, q, k, t, u_scr, w_scr, p_scr, qg_scr, ke_scr, gl_scr):
    u_scr[rows, :] = t["u"]
    w_scr[rows, :] = _bf(t["w"])
    p_scr[rows, :] = _bf(_diag_blocks_tall(t["p"]))
    qg_scr[rows, :] = _bf(q * t["gamma"])
    ke_scr[rows, :] = _bf(k * t["e_tail"])
    gl_scr[rows, :] = jnp.broadcast_to(jnp.exp(t["g_last"]), (SUPER, 128))


def _gdn_fwd_body(q_ref, k_ref, v_ref, gc_ref, gr_ref, wq_ref, wk_ref, wv_ref, z_ref, gn_ref,
                  o_ref, tinv_ref, st_ref, og_ref, qo_ref, ko_ref, vo_ref, s_scr, *sets):
    head = pl.program_id(1)
    n_super = q_ref.shape[0] // SUPER
    all_sets = [sets[6 * i:6 * i + 6] for i in range(2 * GDN_WIDTH)]
    whole = pl.ds(0, SUPER)

    def conv_silu(x_ref, w_ref, i):
        rows = _super_rows(i)
        x, w = x_ref[rows, :], w_ref[...]
        halo = jnp.zeros((8, x.shape[1]), F32) if i == 0 else x_ref[pl.ds(i * SUPER - 8, 8), :]
        ext = jnp.concatenate([halo, x], axis=0)
        c = x * w[A_CONV - 1:A_CONV, :]
        for j in range(1, A_CONV):
            c = c + pltpu.roll(ext, j, 0)[8:, :] * w[A_CONV - 1 - j:A_CONV - j, :]
        return c * _sigmoid(c)

    def unit(a):
        return a * lax.rsqrt(jnp.sum(a * a, axis=1, keepdims=True) + EPS)

    def prepare_steps(i, dst):
        rows = _super_rows(i)
        q = unit(conv_silu(q_ref, wq_ref, i)) * A_DK ** -0.5
        k = unit(conv_silu(k_ref, wk_ref, i))
        v = conv_silu(v_ref, wv_ref, i)
        qo_ref[rows, :], ko_ref[rows, :], vo_ref[rows, :] = q, k, v
        t = yield from _gdn_super_steps(q, k, v, gc_ref[rows, :], gr_ref[i], head)
        tinv_ref[rows, :] = _diag_blocks_tall(t["tinv"])
        _store_scan_operands(whole, q, k, t, *dst)

    def scan_steps(i, src):
        u_scr, w_scr, p_scr, qg_scr, ke_scr, gl_scr = src
        for j in range(GDN_GROUP):
            n = i * GDN_GROUP + j
            local = pl.ds(j * CHUNK, CHUNK)
            s = s_scr[...]
            sb = _bf(s)
            st_ref[n] = sb
            ws = jnp.dot(w_scr[local, :], sb, preferred_element_type=F32)
            yield
            vb = _bf(u_scr[local, :] - ws)
            o = (jnp.dot(qg_scr[local, :], sb, preferred_element_type=F32)
                 + jnp.dot(p_scr[local, :], vb, preferred_element_type=F32))
            s_new = s * gl_scr[local, :][0:1, 0:1] + lax.dot_general(
                ke_scr[local, :], vb, (((0,), (0,)), ((), ())), preferred_element_type=F32)
            yield
            rows = _chunk_rows(n)
            o_ref[rows, :] = o
            s_scr[...] = s_new
            silu, _ = _silu_parts(z_ref[rows, :])
            r = lax.rsqrt(jnp.mean(o * o, axis=1, keepdims=True) + EPS)
            og_ref[rows, :] = ((o * r * gn_ref[0:1, :]) * silu).astype(og_ref.dtype)

    def scan_many(first, srcs):
        for j, src in enumerate(srcs):
            yield from scan_steps(first + j, src)

    groups = [all_sets[:GDN_WIDTH], all_sets[GDN_WIDTH:]]
    _interleave(*[prepare_steps(j, groups[0][j]) for j in range(GDN_WIDTH)])
    s_scr[...] = jnp.zeros_like(s_scr)
    for g in range(n_super // GDN_WIDTH):
        cur, nxt = groups[g % 2], groups[(g + 1) % 2]
        first = g * GDN_WIDTH
        following = [prepare_steps(first + GDN_WIDTH + j, nxt[j]) for j in range(GDN_WIDTH)
                     if first + GDN_WIDTH + j < n_super]
        _interleave(scan_many(first, cur), *following)


def _gdn_in_specs(s_len, n_super, from_proj):
    k_col, v_col = (A_K_COL, A_V_COL) if from_proj else (0, 0)
    return [
        pl.BlockSpec((s_len, A_DK), lambda b, h: (b, h)),
        pl.BlockSpec((s_len, A_DK), lambda b, h: (b, k_col + h)),
        pl.BlockSpec((s_len, A_DV), lambda b, h: (b, v_col + h)),
        pl.BlockSpec((s_len, 128), lambda b, h: (b, 0)),
        pl.BlockSpec((None, None, n_super, 1, SUPER), lambda b, h: (b, A_HEADS + h, 0, 0, 0)),
    ]


def _gdn_scan_scratch(s_len):
    return [pltpu.VMEM((A_DK, A_DV), F32), pltpu.VMEM((s_len, A_DV), F32),
            pltpu.VMEM((s_len, A_DK), BF16), pltpu.VMEM((s_len, CHUNK), BF16),
            pltpu.VMEM((s_len, A_DK), BF16), pltpu.VMEM((s_len, A_DK), BF16),
            pltpu.VMEM((s_len, 128), F32)]


def gdn_fwd(proj_a, conv_w8, gates_col, gates_row, norm_g8, n_seq):
    t_rows = proj_a.shape[0]
    s_len = t_rows // n_seq
    n_chunks = s_len // CHUNK
    qk_spec = pl.BlockSpec((s_len, A_DK), lambda b, h: (b, h))
    v_spec = pl.BlockSpec((s_len, A_DV), lambda b, h: (b, h))
    return pl.pallas_call(
        _gdn_fwd_body, name="gdn_fwd", grid=(n_seq, A_HEADS),
        in_specs=_gdn_in_specs(s_len, s_len // SUPER, True) + [
            pl.BlockSpec((8, A_DK), lambda b, h: (0, h)),
            pl.BlockSpec((8, A_DK), lambda b, h: (0, A_K_COL + h)),
            pl.BlockSpec((8, A_DV), lambda b, h: (0, A_V_COL + h)),
            pl.BlockSpec((s_len, A_DV), lambda b, h: (b, A_Z_COL + h)),
            pl.BlockSpec((8, A_DV), lambda b, h: (0, 0)),
        ],
        out_specs=[
            v_spec,
            pl.BlockSpec((s_len, CHUNK), lambda b, h: (b * A_HEADS + h, 0)),
            pl.BlockSpec((None, n_chunks, A_DK, A_DV), lambda b, h: (b * A_HEADS + h, 0, 0, 0)),
            v_spec, qk_spec, qk_spec, v_spec,
        ],
        out_shape=[
            jax.ShapeDtypeStruct((t_rows, A_VW), F32),
            jax.ShapeDtypeStruct((n_seq * A_HEADS * s_len, CHUNK), F32),
            jax.ShapeDtypeStruct((n_seq * A_HEADS, n_chunks, A_DK, A_DV), BF16),
            jax.ShapeDtypeStruct((t_rows, A_VW), BF16),
            jax.ShapeDtypeStruct((t_rows, A_QK), F32),
            jax.ShapeDtypeStruct((t_rows, A_QK), F32),
            jax.ShapeDtypeStruct((t_rows, A_VW), F32),
        ],
        scratch_shapes=_gdn_scan_scratch(SUPER) + (2 * GDN_WIDTH - 1) * _gdn_scan_scratch(SUPER)[1:],
        compiler_params=_params(("arbitrary", "arbitrary")),
    )(proj_a, proj_a, proj_a, gates_col, gates_row, conv_w8, conv_w8, conv_w8, proj_a, norm_g8)


def _gdn_bwd_body(q_ref, k_ref, v_ref, gc_ref, gr_ref, tinv_ref, st_ref, dog_ref, oa_ref, z_ref, gn_ref,
                  xq_ref, xk_ref, xv_ref, wq_ref, wk_ref, wv_ref,
                  dxq_ref, dxk_ref, dxv_ref, dgc_ref, dz_ref, dgn_ref, dwq_ref, dwk_ref, dwv_ref,
                  ds_scr, cq_scr, ck_scr, cv_scr, *sets):
    head = pl.program_id(1)
    n_super = q_ref.shape[0] // SUPER
    ops = (sets[0:7], sets[7:14])
    res = (sets[14:21], sets[21:28])
    whole = pl.ds(0, SUPER)
    tn = functools.partial(lax.dot_general, dimension_numbers=(((0,), (0,)), ((), ())), preferred_element_type=F32)
    nt = functools.partial(lax.dot_general, dimension_numbers=(((1,), (1,)), ((), ())), preferred_element_type=F32)

    @pl.when(head == 0)
    def _():
        dgc_ref[...] = jnp.zeros_like(dgc_ref)

    @pl.when((head == 0) & (pl.program_id(0) == 0))
    def _():
        dgn_ref[...] = jnp.zeros_like(dgn_ref)

    carry = (cq_scr, ck_scr, cv_scr)
    for ref in carry + (dwq_ref, dwk_ref, dwv_ref):
        ref[...] = jnp.zeros_like(ref)

    def common_steps(i):
        rows = _super_rows(i)
        q, k, v = q_ref[rows, :], k_ref[rows, :], v_ref[rows, :]
        t = yield from _gdn_super_steps(q, k, v, gc_ref[rows, :], gr_ref[i], head, tinv_tall=tinv_ref[rows, :])
        return rows, q, k, v, t

    def stage_p(i, parity):
        rows, q, k, _, t = yield from common_steps(i)
        _store_scan_operands(whole, q, k, t, *ops[parity][:6])
        o, d_og, gain = oa_ref[rows, :], dog_ref[rows, :], gn_ref[0:1, :]
        r = lax.rsqrt(jnp.mean(o * o, axis=1, keepdims=True) + EPS)
        silu, dsilu = _silu_parts(z_ref[rows, :])
        xr = o * r
        d_on = d_og * silu
        dz_ref[rows, :] = (d_og * (xr * gain) * dsilu).astype(dz_ref.dtype)
        u = d_on * gain
        ops[parity][6][...] = r * u - xr * (r * r) * jnp.mean(u * o, axis=1, keepdims=True)
        dgn_ref[0:1, :] += jnp.sum(d_on * xr, axis=0, keepdims=True)

    def stage_s(i, parity):
        u_scr, w_scr, p_scr, qg_scr, ke_scr, gl_scr, do_scr = ops[parity]
        vn_scr, dvn_scr, dqg_scr, dw_scr, dkt_scr, sds_scr, dof_scr = res[parity]
        for j in reversed(range(GDN_GROUP)):
            n = i * GDN_GROUP + j
            local = pl.ds(j * CHUNK, CHUNK)
            ds_next = ds_scr[...]
            dsb = _bf(ds_next)
            sb = st_ref[n]
            s = sb.astype(F32)
            d_o = do_scr[local, :]
            dof_scr[local, :] = d_o
            d_ob = _bf(d_o)
            w_s = jnp.dot(w_scr[local, :], sb, preferred_element_type=F32)
            d_vn = tn(p_scr[local, :], d_ob) + jnp.dot(ke_scr[local, :], dsb, preferred_element_type=F32)
            d_qg = nt(d_ob, sb)
            qg_do = tn(qg_scr[local, :], d_ob)
            yield
            v_new = u_scr[local, :] - w_s
            d_vnb = _bf(d_vn)
            d_w = -nt(d_vnb, sb)
            d_kt = nt(_bf(v_new), dsb)
            w_dvn = tn(w_scr[local, :], d_vnb)
            yield
            vn_scr[local, :] = v_new
            dvn_scr[local, :] = d_vn
            dqg_scr[local, :] = d_qg
            dw_scr[local, :] = d_w
            dkt_scr[local, :] = d_kt
            sds = jnp.sum(jnp.sum(s * ds_next, axis=1, keepdims=True), axis=0, keepdims=True)
            sds_scr[local, :] = jnp.broadcast_to(sds, (CHUNK, 128))
            ds_scr[...] = qg_do + gl_scr[local, :][0:1, 0:1] * ds_next - w_dvn

    def conv_bwd(i, rows, x_ref, w_ref, dy, norm_scale, dx_ref, dw_ref, dc_above):
        x, w = x_ref[rows, :], w_ref[...]
        above = x_ref[pl.ds(pl.multiple_of(jnp.maximum(i * SUPER - 8, 0), 8), 8), :]
        ext = jnp.concatenate([jnp.where(i > 0, above, 0.0), x], axis=0)
        c = x * w[A_CONV - 1:A_CONV, :]
        for j in range(1, A_CONV):
            c = c + pltpu.roll(ext, j, 0)[8:, :] * w[A_CONV - 1 - j:A_CONV - j, :]
        sig = _sigmoid(c)
        a = c * sig
        if norm_scale is None:
            da = dy
        else:
            rn = lax.rsqrt(jnp.sum(a * a, axis=1, keepdims=True) + EPS)
            da = norm_scale * (rn * dy - a * (rn * rn * rn) * jnp.sum(dy * a, axis=1, keepdims=True))
        dc = da * (sig * (1.0 + c * (1.0 - sig)))
        ext_dc = jnp.concatenate([dc, dc_above[...]], axis=0)
        dc_above[...] = dc[0:8, :]
        dx = dc * w[A_CONV - 1:A_CONV, :]
        dw_ref[A_CONV - 1:A_CONV, :] += jnp.sum(dc * x, axis=0, keepdims=True)
        for j in range(1, A_CONV):
            dcs = pltpu.roll(ext_dc, SUPER + 8 - j, 0)[:SUPER, :]
            dx = dx + dcs * w[A_CONV - 1 - j:A_CONV - j, :]
            dw_ref[A_CONV - 1 - j:A_CONV - j, :] += jnp.sum(dcs * x, axis=0, keepdims=True)
        dx_ref[rows, :] = dx.astype(dx_ref.dtype)

    def stage_f(i, parity):
        vn_scr, dvn_scr, dqg_scr, dw_scr, dkt_scr, sds_scr, dof_scr = res[parity]
        rows, q, k, v, t = yield from common_steps(i)
        beta, gamma, decay, kb, e_tail = t["beta"], t["gamma"], t["decay"], t["kb"], t["e_tail"]
        row, col, lane, same = t["row"], t["col"], t["lane"], t["same"]
        d_o = dof_scr[...]
        v_new, d_vn = vn_scr[...], dvn_scr[...]
        d_qg, d_w, d_kt = dqg_scr[...], dw_scr[...], dkt_scr[...]
        gamma_last = jnp.exp(t["g_last"])

        d_p = jnp.where(same & (row >= col), _mm_nt(d_o, v_new), 0.0)
        d_ru = _mm_tn(t["tinv"], d_vn)
        d_rw = _mm_tn(t["tinv"], d_w)
        yield
        d_m = jnp.where(same & (row > col), -(_mm_nt(d_ru, t["u"]) + _mm_nt(d_rw, t["w"])), 0.0)
        yield

        x_p = d_p * decay
        y_m = d_m * decay
        d_kb = _mm(y_m, k) + d_rw * gamma
        d_q = _mm(x_p, k) + d_qg * gamma
        d_k = _mm_tn(x_p, q) + _mm_tn(y_m, kb) + d_kb * beta + d_kt * e_tail
        d_v = d_ru * beta
        conv_bwd(i, rows, xq_ref, wq_ref, d_q, A_DK ** -0.5, dxq_ref, dwq_ref, carry[0])
        conv_bwd(i, rows, xk_ref, wk_ref, d_k, 1.0, dxk_ref, dwk_ref, carry[1])
        conv_bwd(i, rows, xv_ref, wv_ref, d_v, None, dxv_ref, dwv_ref, carry[2])

        d_beta = (jnp.sum(d_ru * v, axis=1, keepdims=True)
                  + jnp.sum(d_kb * k, axis=1, keepdims=True))
        z = d_p * t["p"] + d_m * t["m"]
        eps_tail = jnp.sum(d_kt * k, axis=1, keepdims=True) * e_tail
        d_gc = (jnp.sum(z, axis=1, keepdims=True) - _colsum_as_col(z)
                + jnp.sum(d_qg * q, axis=1, keepdims=True) * gamma
                + jnp.sum(d_rw * kb, axis=1, keepdims=True) * gamma
                - eps_tail)
        d_glast = _block_sum(same, eps_tail) + gamma_last * sds_scr[...][:, 0:1]
        yield
        rcol = lax.broadcasted_iota(jnp.int32, (SUPER, 1), 0)
        d_gc = d_gc + jnp.where(rcol % CHUNK == CHUNK - 1, d_glast, 0.0)
        dgc_ref[rows, :] += (jnp.where(lane == head, d_beta, 0.0)
                             + jnp.where(lane == A_HEADS + head, d_gc, 0.0))

    last = n_super - 1
    _drain(stage_p(last, 1))
    ds_scr[...] = jnp.zeros_like(ds_scr)
    _interleave(stage_s(last, 1), stage_p(last - 1, 0))

    def pair(k, carry):
        i = last - 1 - 2 * k
        _interleave(stage_s(i, 0), stage_f(i + 1, 1), stage_p(i - 1, 1))
        _interleave(stage_s(i - 1, 1), stage_f(i, 0), stage_p(i - 2, 0))
        return carry

    lax.fori_loop(0, n_super // 2 - 1, pair, 0)
    _interleave(stage_s(0, 0), stage_f(1, 1))
    _drain(stage_f(0, 0))


def gdn_bwd(q, k, v, gates_col, gates_row, tinv, states, d_og, o, proj_a, conv_w8, norm_g8, n_seq):
    t_rows = q.shape[0]
    s_len = t_rows // n_seq
    n_chunks = s_len // CHUNK
    qk_spec = pl.BlockSpec((s_len, A_DK), lambda b, h: (b, h))
    v_spec = pl.BlockSpec((s_len, A_DV), lambda b, h: (b, h))
    gate_spec = pl.BlockSpec((s_len, 128), lambda b, h: (b, 0))
    gain_spec = pl.BlockSpec((8, A_DV), lambda b, h: (0, 0))
    dw_qk_spec = pl.BlockSpec((None, 8, A_DK), lambda b, h: (b, 0, h))
    dw_v_spec = pl.BlockSpec((None, 8, A_DV), lambda b, h: (b, 0, h))
    ops_set = _gdn_scan_scratch(SUPER)[1:] + [pltpu.VMEM((SUPER, A_DV), F32)]
    res_set = [pltpu.VMEM((SUPER, A_DV), F32), pltpu.VMEM((SUPER, A_DV), F32),
               pltpu.VMEM((SUPER, A_DK), F32), pltpu.VMEM((SUPER, A_DK), F32),
               pltpu.VMEM((SUPER, A_DK), F32), pltpu.VMEM((SUPER, 128), F32), pltpu.VMEM((SUPER, A_DV), F32)]
    return pl.pallas_call(
        _gdn_bwd_body, name="gdn_bwd", grid=(n_seq, A_HEADS),
        in_specs=_gdn_in_specs(s_len, s_len // SUPER, False) + [
            pl.BlockSpec((s_len, CHUNK), lambda b, h: (b * A_HEADS + h, 0)),
            pl.BlockSpec((None, n_chunks, A_DK, A_DV), lambda b, h: (b * A_HEADS + h, 0, 0, 0)),
            v_spec, v_spec,
            pl.BlockSpec((s_len, A_DV), lambda b, h: (b, A_Z_COL + h)),
            gain_spec,
            pl.BlockSpec((s_len, A_DK), lambda b, h: (b, h)),
            pl.BlockSpec((s_len, A_DK), lambda b, h: (b, A_K_COL + h)),
            pl.BlockSpec((s_len, A_DV), lambda b, h: (b, A_V_COL + h)),
            pl.BlockSpec((8, A_DK), lambda b, h: (0, h)),
            pl.BlockSpec((8, A_DK), lambda b, h: (0, A_K_COL + h)),
            pl.BlockSpec((8, A_DV), lambda b, h: (0, A_V_COL + h)),
        ],
        out_specs=[qk_spec, qk_spec, v_spec, gate_spec, v_spec, gain_spec, dw_qk_spec, dw_qk_spec, dw_v_spec],
        out_shape=[
            jax.ShapeDtypeStruct((t_rows, A_QK), BF16),
            jax.ShapeDtypeStruct((t_rows, A_QK), BF16),
            jax.ShapeDtypeStruct((t_rows, A_VW), BF16),
            jax.ShapeDtypeStruct((t_rows, 128), F32),
            jax.ShapeDtypeStruct((t_rows, A_VW), BF16),
            jax.ShapeDtypeStruct((8, A_DV), F32),
            jax.ShapeDtypeStruct((n_seq, 8, A_QK), F32),
            jax.ShapeDtypeStruct((n_seq, 8, A_QK), F32),
            jax.ShapeDtypeStruct((n_seq, 8, A_VW), F32),
        ],
        scratch_shapes=(_gdn_scan_scratch(SUPER)[:1]
                        + [pltpu.VMEM((8, A_DK), F32), pltpu.VMEM((8, A_DK), F32), pltpu.VMEM((8, A_DV), F32)]
                        + 2 * ops_set + 2 * res_set),
        compiler_params=_params(("arbitrary", "arbitrary")),
    )(q, k, v, gates_col, gates_row, tinv, states, d_og, o, proj_a, norm_g8,
      proj_a, proj_a, proj_a, conv_w8, conv_w8, conv_w8)


GATE_TILE = 512


def _softplus(y):
    return jnp.maximum(y, 0.0) + jnp.log1p(jnp.exp(-jnp.abs(y)))


def _gate_values(x, prm):
    beta = _sigmoid(x)
    y = x + prm[1:2, :]
    neg_a = -jnp.exp(prm[0:1, :])
    g = neg_a * _softplus(y)
    return beta, y, neg_a, g


def _a_gates_fwd_body(x_ref, prm_ref, gc_ref, gr_ref):
    x = x_ref[...]
    tm = x.shape[0]
    beta, _, _, g = _gate_values(x, prm_ref[...])
    in_chunk = lax.broadcasted_iota(jnp.int32, (tm, 1), 0) % CHUNK
    s = 1
    while s < CHUNK:
        g = g + jnp.where(in_chunk >= s, pltpu.roll(g, s, 0), 0.0)
        s *= 2
    lane = lax.broadcasted_iota(jnp.int32, x.shape, 1)
    out = jnp.where(lane < A_HEADS, beta, jnp.where(lane < 2 * A_HEADS, g, 0.0))
    gc_ref[...] = out
    gr_ref[...] = out.T[0:2 * A_HEADS, :]


def a_gates_fwd(proj_a, prm, n_seq):
    t_rows = proj_a.shape[0]
    s_len = t_rows // n_seq
    tm = min(GATE_TILE, s_len)
    n_t = s_len // tm
    return pl.pallas_call(
        _a_gates_fwd_body, name="a_gates_fwd", grid=(n_seq, n_t),
        in_specs=[pl.BlockSpec((tm, 128), lambda b, i: (b * n_t + i, A_GATE_COL)),
                  pl.BlockSpec((8, 128), lambda b, i: (0, 0))],
        out_specs=[pl.BlockSpec((tm, 128), lambda b, i: (b * n_t + i, 0)),
                   pl.BlockSpec((None, 2 * A_HEADS, tm), lambda b, i: (b, 0, i))],
        out_shape=[jax.ShapeDtypeStruct((t_rows, 128), F32),
                   jax.ShapeDtypeStruct((n_seq, 2 * A_HEADS, s_len), F32)],
        compiler_params=_params(("arbitrary", "arbitrary")),
    )(proj_a, prm)


def _a_gates_bwd_body(x_ref, prm_ref, dgc_ref, dx_ref, dprm_ref):
    first = (pl.program_id(0) == 0) & (pl.program_id(1) == 0)
    x = x_ref[...]
    tm = x.shape[0]
    beta, y, neg_a, g = _gate_values(x, prm_ref[...])
    d = dgc_ref[...]
    in_chunk = lax.broadcasted_iota(jnp.int32, (tm, 1), 0) % CHUNK
    dg = d
    s = 1
    while s < CHUNK:
        dg = dg + jnp.where(in_chunk < CHUNK - s, pltpu.roll(dg, tm - s, 0), 0.0)
        s *= 2
    lane = lax.broadcasted_iota(jnp.int32, x.shape, 1)
    is_decay = (lane >= A_HEADS) & (lane < 2 * A_HEADS)
    d_alogit = jnp.where(is_decay, dg * neg_a * _sigmoid(y), 0.0)
    dx_ref[...] = jnp.where(lane < A_HEADS, d * beta * (1.0 - beta), d_alogit).astype(dx_ref.dtype)

    @pl.when(first)
    def _():
        dprm_ref[...] = jnp.zeros_like(dprm_ref)

    dprm_ref[0:1, :] += jnp.sum(jnp.where(is_decay, dg * g, 0.0), axis=0, keepdims=True)
    dprm_ref[1:2, :] += jnp.sum(d_alogit, axis=0, keepdims=True)


def a_gates_bwd(proj_a, prm, dgates_col, n_seq):
    t_rows = proj_a.shape[0]
    s_len = t_rows // n_seq
    tm = min(GATE_TILE, s_len)
    n_t = s_len // tm
    return pl.pallas_call(
        _a_gates_bwd_body, name="a_gates_bwd", grid=(n_seq, n_t),
        in_specs=[pl.BlockSpec((tm, 128), lambda b, i: (b * n_t + i, A_GATE_COL)),
                  pl.BlockSpec((8, 128), lambda b, i: (0, 0)),
                  pl.BlockSpec((tm, 128), lambda b, i: (b * n_t + i, 0))],
        out_specs=[pl.BlockSpec((tm, 128), lambda b, i: (b * n_t + i, 0)),
                   pl.BlockSpec((8, 128), lambda b, i: (0, 0))],
        out_shape=[jax.ShapeDtypeStruct((t_rows, 128), BF16),
                   jax.ShapeDtypeStruct((8, 128), F32)],
        compiler_params=_params(("arbitrary", "arbitrary")),
    )(proj_a, prm, dgates_col)


ROW_TILE = 512
A_Z_COL = (2 * A_QK + A_VW) // A_DV


def _silu_parts(z):
    sig = _sigmoid(z)
    return z * sig, sig * (1.0 + z * (1.0 - sig))


NEG_BIG = -1e30
ATT_SCALE = B_DH ** -0.5


def _swap_rope_halves(x):
    src = lax.broadcasted_iota(jnp.int32, (B_DH, B_DH), 0)
    dst = lax.broadcasted_iota(jnp.int32, (B_DH, B_DH), 1)
    pick = ((dst < ROPE_HALF) & (src == dst + ROPE_HALF)) | (
        (dst >= ROPE_HALF) & (dst < ROPE_DIMS) & (src == dst - ROPE_HALF))
    return jnp.dot(_bf(x), pick.astype(BF16), preferred_element_type=F32)


def _norm_rope(x, gain, cos_t, sin_t):
    r = lax.rsqrt(jnp.mean(x * x, axis=1, keepdims=True) + EPS)
    xn = x * r * gain
    return xn * cos_t + _swap_rope_halves(xn) * sin_t, r


def _norm_rope_bwd(x, r, gain, cos_t, sin_t, dy):
    d_xn = dy * cos_t + _swap_rope_halves(dy * sin_t)
    xr = x * r
    u = d_xn * gain
    dx = r * u - xr * (r * r) * jnp.mean(u * x, axis=1, keepdims=True)
    return dx, jnp.sum(d_xn * xr, axis=0, keepdims=True)


def _stream_rows(idx, dilation, s_len):
    nb = s_len // dilation // B_BLOCK
    r = idx // nb
    m = idx % nb
    cur = r + m * (B_BLOCK * dilation)
    prev = r + jnp.maximum(m - 1, 0) * (B_BLOCK * dilation)
    return cur, prev, m > 0


def _rows(start, dilation):
    if dilation == 1:
        return pl.ds(start, B_BLOCK)
    return pl.ds(start, B_BLOCK, stride=dilation)


ATT_UNROLL = 16


def _band_mask(has_prev):
    qi = lax.broadcasted_iota(jnp.int32, (B_BLOCK, 2 * B_BLOCK), 0)
    kj = lax.broadcasted_iota(jnp.int32, (B_BLOCK, 2 * B_BLOCK), 1)
    return ((kj < B_BLOCK) & (kj >= qi) & has_prev) | ((kj >= B_BLOCK) & (kj - B_BLOCK <= qi))


def _attn_fwd_body(qkv_ref, z_ref, cos_ref, sin_ref, gain_ref, og_ref, o_ref, lse_ref,
                   qn_scr, kn_scr, og_scr, lg_scr):
    head, grp = pl.program_id(1), pl.program_id(2)
    s_len = z_ref.shape[0]
    n_blocks = s_len // B_BLOCK
    cos_t, sin_t = cos_ref[...], sin_ref[...]

    for gi, dil in enumerate(B_DILATIONS):
        @pl.when(grp == gi)
        def _(gi=gi, dil=dil):
            qn_scr[...], _ = _norm_rope(qkv_ref[0], gain_ref[gi:gi + 1, :], cos_t, sin_t)
            kn_scr[...], _ = _norm_rope(qkv_ref[1], gain_ref[B_GROUPS + gi:B_GROUPS + gi + 1, :], cos_t, sin_t)

            ones = jnp.ones((2 * B_BLOCK, B_DH), BF16)

            def blocks(it, carry):
                scored = []
                for j in range(ATT_UNROLL):
                    cur, prev, has_prev = _stream_rows(it * ATT_UNROLL + j, dil, s_len)
                    rc, rp = _rows(cur, dil), _rows(prev, dil)
                    k2 = jnp.concatenate([kn_scr[rp, :], kn_scr[rc, :]], axis=0)
                    scored.append((rc, rp, has_prev, _mm_nt(qn_scr[rc, :], k2) * ATT_SCALE))
                summed = []
                for rc, rp, has_prev, s in scored:
                    s = jnp.where(_band_mask(has_prev), s, NEG_BIG)
                    mx = jnp.max(s, axis=1, keepdims=True)
                    v2 = jnp.concatenate([qkv_ref.at[2][rp, :], qkv_ref.at[2][rc, :]], axis=0)
                    acc = jnp.dot(_bf(jnp.exp(s - mx)), jnp.concatenate([_bf(v2), ones], axis=1),
                                  preferred_element_type=F32)
                    summed.append((rc, mx, acc))
                for rc, mx, acc in summed:
                    den = acc[:, B_DH:B_DH + 1]
                    og_scr.at[gi][rc, :] = acc[:, :B_DH] / den
                    lg_scr.at[gi][rc, :] = jnp.broadcast_to(mx + jnp.log(den), (B_BLOCK, B_DH))
                return carry

            lax.fori_loop(0, n_blocks // ATT_UNROLL, blocks, 0)

    @pl.when(grp == B_GROUPS - 1)
    def _():
        l0, l1, l2 = lg_scr[0], lg_scr[1], lg_scr[2]
        mx = jnp.maximum(jnp.maximum(l0, l1), l2)
        w0, w1, w2 = jnp.exp(l0 - mx), jnp.exp(l1 - mx), jnp.exp(l2 - mx)
        den = w0 + w1 + w2
        o = (w0 * og_scr[0] + w1 * og_scr[1] + w2 * og_scr[2]) / den
        silu, _ = _silu_parts(z_ref[...])
        o_ref[...] = o
        og_ref[...] = (o * silu).astype(og_ref.dtype)
        @pl.when(head == 0)
        def _():
            lse_ref[...] = jnp.zeros_like(lse_ref)

        lane = lax.broadcasted_iota(jnp.int32, o.shape, 1)
        lse_ref[...] = jnp.where(lane == head, mx + jnp.log(den), lse_ref[...])


def attn_fwd(proj_b, cos_t, sin_t, gains8, n_seq):
    t_rows = proj_b.shape[1]
    s_len = t_rows // n_seq
    head_blk = pl.BlockSpec((s_len, B_DH), lambda b, h, g: (b, h))
    seq_blk = pl.BlockSpec((s_len, 128), lambda b, h, g: (b, 0))
    return pl.pallas_call(
        _attn_fwd_body, name="attn_fwd", grid=(n_seq, B_HEADS, B_GROUPS),
        in_specs=[
            pl.BlockSpec((3, s_len, B_DH), lambda b, h, g: (g, b, h)),
            pl.BlockSpec((None, s_len, B_DH), lambda b, h, g: (B_PIECES - 1, b, h)),
            seq_blk, seq_blk,
            pl.BlockSpec((8, 128), lambda b, h, g: (0, 0)),
        ],
        out_specs=[head_blk, head_blk, seq_blk],
        out_shape=[jax.ShapeDtypeStruct((t_rows, B_W), BF16),
                   jax.ShapeDtypeStruct((t_rows, B_W), F32),
                   jax.ShapeDtypeStruct((t_rows, 128), F32)],
        scratch_shapes=[pltpu.VMEM((s_len, B_DH), F32), pltpu.VMEM((s_len, B_DH), F32),
                        pltpu.VMEM((B_GROUPS, s_len, B_DH), F32), pltpu.VMEM((B_GROUPS, s_len, B_DH), F32)],
        compiler_params=_params(("arbitrary", "arbitrary", "arbitrary")),
    )(proj_b, proj_b, cos_t, sin_t, gains8)


def _attn_bwd_body(qkv_ref, z_ref, cos_ref, sin_ref, gain_ref, dog_ref, o_ref, lse_ref,
                   dqkv_ref, dz_ref, dgain_ref,
                   qn_scr, kn_scr, dqn_scr, dkn_scr, do_scr, dl_scr, ls_scr, dv_scr):
    head, grp = pl.program_id(1), pl.program_id(2)
    first = (pl.program_id(0) == 0) & (head == 0) & (grp == 0)
    s_len = z_ref.shape[0]
    n_blocks = s_len // B_BLOCK
    cos_t, sin_t = cos_ref[...], sin_ref[...]

    @pl.when(first)
    def _():
        dgain_ref[...] = jnp.zeros_like(dgain_ref)

    @pl.when(grp == 0)
    def _():
        d_og, o = dog_ref[...], o_ref[...]
        silu, dsilu = _silu_parts(z_ref[...])
        d_o = d_og * silu
        dz_ref[...] = (d_og * o * dsilu).astype(dz_ref.dtype)
        do_scr[...] = d_o
        dl_scr[...] = jnp.broadcast_to(jnp.sum(d_o * o, axis=1, keepdims=True), o.shape)
        lane = lax.broadcasted_iota(jnp.int32, o.shape, 1)
        ls_scr[...] = jnp.broadcast_to(
            jnp.sum(jnp.where(lane == head, lse_ref[...], 0.0), axis=1, keepdims=True), o.shape)

    for gi, dil in enumerate(B_DILATIONS):
        @pl.when(grp == gi)
        def _(gi=gi, dil=dil):
            q_raw, k_raw = qkv_ref[0], qkv_ref[1]
            gq = gain_ref[gi:gi + 1, :]
            gk = gain_ref[B_GROUPS + gi:B_GROUPS + gi + 1, :]
            qn_scr[...], rq = _norm_rope(q_raw, gq, cos_t, sin_t)
            kn_scr[...], rk = _norm_rope(k_raw, gk, cos_t, sin_t)
            def blocks(it, carry):
                scored = []
                for j in range(ATT_UNROLL):
                    cur, prev, has_prev = _stream_rows(it * ATT_UNROLL + j, dil, s_len)
                    rc, rp = _rows(cur, dil), _rows(prev, dil)
                    qb, d_ob = _bf(qn_scr[rc, :]), _bf(do_scr[rc, :])
                    k2 = _bf(jnp.concatenate([kn_scr[rp, :], kn_scr[rc, :]], axis=0))
                    v2 = _bf(jnp.concatenate([qkv_ref.at[2][rp, :], qkv_ref.at[2][rc, :]], axis=0))
                    scored.append((rc, rp, has_prev, qb, d_ob, k2,
                                   _mm_nt(qb, k2) * ATT_SCALE, _mm_nt(d_ob, v2)))
                grads = []
                for rc, rp, has_prev, qb, d_ob, k2, s, d_p in scored:
                    p = jnp.exp(jnp.where(_band_mask(has_prev), s - ls_scr[rc, :][:, 0:1], NEG_BIG))
                    ds = _bf(p * (d_p - dl_scr[rc, :][:, 0:1]))
                    grads.append((rc, rp, has_prev,
                                  _mm(ds, k2) * ATT_SCALE, _mm_tn(ds, qb) * ATT_SCALE, _mm_tn(_bf(p), d_ob)))
                for j, (rc, rp, has_prev, dq, dk2, dv2) in enumerate(grads):
                    dqn_scr[rc, :] = dq
                    if j == 0:
                        @pl.when(has_prev)
                        def _():
                            dkn_scr[rp, :] += dk2[:B_BLOCK]
                            dv_scr[rp, :] += dv2[:B_BLOCK]
                    if j + 1 < ATT_UNROLL:
                        dkn_scr[rc, :] = dk2[B_BLOCK:] + grads[j + 1][4][:B_BLOCK]
                        dv_scr[rc, :] = dv2[B_BLOCK:] + grads[j + 1][5][:B_BLOCK]
                    else:
                        dkn_scr[rc, :] = dk2[B_BLOCK:]
                        dv_scr[rc, :] = dv2[B_BLOCK:]
                return carry

            lax.fori_loop(0, n_blocks // ATT_UNROLL, blocks, 0)
            dq, dgq = _norm_rope_bwd(q_raw, rq, gq, cos_t, sin_t, dqn_scr[...])
            dk, dgk = _norm_rope_bwd(k_raw, rk, gk, cos_t, sin_t, dkn_scr[...])
            dqkv_ref[0] = dq.astype(dqkv_ref.dtype)
            dqkv_ref[1] = dk.astype(dqkv_ref.dtype)
            dqkv_ref[2] = dv_scr[...].astype(dqkv_ref.dtype)
            dgain_ref[gi:gi + 1, :] += dgq
            dgain_ref[B_GROUPS + gi:B_GROUPS + gi + 1, :] += dgk


def attn_bwd(proj_b, cos_t, sin_t, gains8, d_og, o, lse, n_seq):
    t_rows = proj_b.shape[1]
    s_len = t_rows // n_seq
    head_blk = pl.BlockSpec((s_len, B_DH), lambda b, h, g: (b, h))
    seq_blk = pl.BlockSpec((s_len, 128), lambda b, h, g: (b, 0))
    grp_blk = pl.BlockSpec((3, s_len, B_DH), lambda b, h, g: (g, b, h))
    gain_blk = pl.BlockSpec((8, 128), lambda b, h, g: (0, 0))
    return pl.pallas_call(
        _attn_bwd_body, name="attn_bwd", grid=(n_seq, B_HEADS, B_GROUPS),
        in_specs=[
            grp_blk,
            pl.BlockSpec((None, s_len, B_DH), lambda b, h, g: (B_PIECES - 1, b, h)),
            seq_blk, seq_blk, gain_blk, head_blk, head_blk, seq_blk,
        ],
        out_specs=[grp_blk, head_blk, gain_blk],
        out_shape=[jax.ShapeDtypeStruct((3 * B_GROUPS, t_rows, B_W), BF16),
                   jax.ShapeDtypeStruct((t_rows, B_W), BF16),
                   jax.ShapeDtypeStruct((8, 128), F32)],
        scratch_shapes=[pltpu.VMEM((s_len, B_DH), F32) for _ in range(8)],
        compiler_params=_params(("arbitrary", "arbitrary", "arbitrary")),
    )(proj_b, proj_b, cos_t, sin_t, gains8, d_og, o, lse)


def rope_tables(positions):
    inv_freq = ROPE_THETA ** (-jnp.arange(0, ROPE_DIMS, 2, dtype=F32) / ROPE_DIMS)
    ang = positions.astype(F32)[:, None] * inv_freq
    cos, sin = jnp.cos(ang), jnp.sin(ang)
    t_rows = positions.shape[0]
    rest = B_DH - ROPE_DIMS
    cos_t = jnp.concatenate([cos, cos, jnp.ones((t_rows, rest), F32)], axis=1)
    sin_t = jnp.concatenate([-sin, sin, jnp.zeros((t_rows, rest), F32)], axis=1)
    return cos_t, sin_t


def _rms_fwd_body(x_ref, g_ref, *rest, layer):
    h_ref, ht_ref = rest[-2:]
    x = x_ref[...]
    r = lax.rsqrt(jnp.mean(x * x, axis=1, keepdims=True) + EPS)
    h = x * r * g_ref[layer:layer + 1, :]
    h_ref[...] = h.astype(h_ref.dtype)
    ht_ref[...] = h.T.astype(ht_ref.dtype)


def rms_fwd(x, gains8, layer, after=None):
    t_rows, d = x.shape
    tm = min(ROW_TILE, t_rows)
    in_specs = [pl.BlockSpec((tm, d), lambda i: (i, 0)), pl.BlockSpec((8, d), lambda i: (0, 0))]
    args = [x, gains8]
    if after is not None:
        in_specs.append(HBM_SPEC)
        args.append(after)
    return pl.pallas_call(
        functools.partial(_rms_fwd_body, layer=layer), name=f"rms_fwd_{layer}", grid=(t_rows // tm,),
        in_specs=in_specs,
        out_specs=[pl.BlockSpec((tm, d), lambda i: (i, 0)), pl.BlockSpec((d, tm), lambda i: (0, i))],
        out_shape=[jax.ShapeDtypeStruct((t_rows, d), BF16), jax.ShapeDtypeStruct((d, t_rows), BF16)],
        compiler_params=_params(("arbitrary",)),
    )(*args)


def _rms_bwd_body(x_ref, g_ref, dh_ref, res_ref, dx_ref, dg_ref, *, layer):
    x, dh = x_ref[...], dh_ref[...]
    r = lax.rsqrt(jnp.mean(x * x, axis=1, keepdims=True) + EPS)
    xr = x * r
    u = dh * g_ref[layer:layer + 1, :]
    dx_ref[...] = res_ref[...] + r * u - xr * (r * r) * jnp.mean(u * x, axis=1, keepdims=True)

    @pl.when(pl.program_id(0) == 0)
    def _():
        dg_ref[...] = jnp.zeros_like(dg_ref)

    dg_ref[0:1, :] += jnp.sum(dh * xr, axis=0, keepdims=True)


def rms_bwd(x, gains8, layer, dh, d_res):
    t_rows, d = x.shape
    tm = min(ROW_TILE, t_rows)
    blk = pl.BlockSpec((tm, d), lambda i: (i, 0))
    gblk = pl.BlockSpec((8, d), lambda i: (0, 0))
    return pl.pallas_call(
        functools.partial(_rms_bwd_body, layer=layer), name=f"rms_bwd_{layer}", grid=(t_rows // tm,),
        in_specs=[blk, gblk, blk, blk],
        out_specs=[blk, gblk],
        out_shape=[jax.ShapeDtypeStruct((t_rows, d), F32), jax.ShapeDtypeStruct((8, d), F32)],
        compiler_params=_params(("arbitrary",)),
    )(x, gains8, dh, d_res)


def _piece_col(p):
    return jnp.where(p < 3 * B_GROUPS, (p % 3) * B_GROUPS + p // 3, 3 * B_GROUPS)


def _mm_nn_body(a_ref, w_ref, *rest, has_res):
    o_ref = rest[-1]
    acc = jnp.dot(a_ref[...], w_ref[...], preferred_element_type=F32)
    if has_res:
        acc = acc + rest[0][...]
    o_ref[...] = acc


def mm_nn(a, w, residual=None, *, tn, name):
    m, k = a.shape
    n = w.shape[1]
    tm = min(NT_ROW_TILE, m)
    in_specs = [pl.BlockSpec((tm, k), lambda j, i: (i, 0)), pl.BlockSpec((k, tn), lambda j, i: (0, j))]
    args = [a, w]
    if residual is not None:
        in_specs.append(pl.BlockSpec((tm, tn), lambda j, i: (i, j)))
        args.append(residual)
    return pl.pallas_call(
        functools.partial(_mm_nn_body, has_res=residual is not None), name=name, grid=(n // tn, m // tm),
        in_specs=in_specs,
        out_specs=pl.BlockSpec((tm, tn), lambda j, i: (i, j)),
        out_shape=jax.ShapeDtypeStruct((m, n), F32),
        compiler_params=_params(("arbitrary", "arbitrary")),
    )(*args)


def mm_nn_pieces(a, w, *, name):
    m, k = a.shape
    tm = min(NT_ROW_TILE, m)
    return pl.pallas_call(
        functools.partial(_mm_nn_body, has_res=False), name=name, grid=(B_PIECES, m // tm),
        in_specs=[pl.BlockSpec((tm, k), lambda p, i: (i, 0)),
                  pl.BlockSpec((k, B_W), lambda p, i: (0, _piece_col(p)))],
        out_specs=pl.BlockSpec((None, tm, B_W), lambda p, i: (p, i, 0)),
        out_shape=jax.ShapeDtypeStruct((B_PIECES, m, B_W), F32),
        compiler_params=_params(("arbitrary", "arbitrary")),
    )(a, w)


NT_ROW_TILE = 1024


def _mm_nt_body(g_ref, w_ref, *rest, has_init):
    o_ref = rest[-1]
    j = pl.program_id(1)
    part = lax.dot_general(_bf(g_ref[...]), w_ref[...], (((1,), (1,)), ((), ())), preferred_element_type=F32)

    @pl.when(j == 0)
    def _():
        o_ref[...] = part + rest[0][...] if has_init else part

    @pl.when(j > 0)
    def _():
        o_ref[...] += part


def mm_nt(g, w, init=None, *, tn, col_off=0, name, after=None):
    m, n = g.shape
    k = w.shape[0]
    tm = min(NT_ROW_TILE, m)
    in_specs = [pl.BlockSpec((tm, tn), lambda i, j: (i, j)),
                pl.BlockSpec((k, tn), lambda i, j: (0, col_off + j))]
    args = [g, w]
    if init is not None:
        in_specs.append(pl.BlockSpec((tm, k), lambda i, j: (i, 0)))
        args.append(init)
    if after is not None:
        in_specs.append(HBM_SPEC)
        args.append(after)
    return pl.pallas_call(
        functools.partial(_mm_nt_body, has_init=init is not None), name=name, grid=(m // tm, n // tn),
        in_specs=in_specs,
        out_specs=pl.BlockSpec((tm, k), lambda i, j: (i, 0)),
        out_shape=jax.ShapeDtypeStruct((m, k), F32),
        compiler_params=_params(("arbitrary", "arbitrary")),
    )(*args)


def mm_nt_pieces(g9, w, *, name):
    n_p, m, _ = g9.shape
    k = w.shape[0]
    tm = min(NT_ROW_TILE, m)
    return pl.pallas_call(
        functools.partial(_mm_nt_body, has_init=False), name=name, grid=(m // tm, n_p),
        in_specs=[pl.BlockSpec((None, tm, B_W), lambda i, p: (p, i, 0)),
                  pl.BlockSpec((k, B_W), lambda i, p: (0, _piece_col(p)))],
        out_specs=pl.BlockSpec((tm, k), lambda i, p: (i, 0)),
        out_shape=jax.ShapeDtypeStruct((m, k), F32),
        compiler_params=_params(("arbitrary", "arbitrary")),
    )(g9, w)


def _mm_tn_body(a_ref, g_ref, o_ref, *, a_is_transposed):
    lhs_dim = 1 if a_is_transposed else 0
    o_ref[...] = lax.dot_general(a_ref[...], _bf(g_ref[...]), (((lhs_dim,), (0,)), ((), ())),
                                 preferred_element_type=F32).astype(o_ref.dtype)


def mm_tn(a, g, *, tn, out_dtype, name, a_is_transposed=False):
    k = a.shape[0] if a_is_transposed else a.shape[1]
    m, n = g.shape
    return pl.pallas_call(
        functools.partial(_mm_tn_body, a_is_transposed=a_is_transposed), name=name, grid=(n // tn,),
        in_specs=[pl.BlockSpec(a.shape, lambda j: (0, 0)), pl.BlockSpec((m, tn), lambda j: (0, j))],
        out_specs=pl.BlockSpec((k, tn), lambda j: (0, j)),
        out_shape=jax.ShapeDtypeStruct((k, n), out_dtype),
        compiler_params=_params(("arbitrary",)),
    )(a, g)


B_UNIT = 256
B_IN_COLS = B_PIECES * B_W
B_SHARD_UNITS = B_IN_COLS // N_DEV // B_UNIT


def mm_tn_b_in(at, g9, gz, *, out_dtype, name):
    k, m = at.shape
    per_piece = B_W // B_UNIT
    body_one = functools.partial(_mm_tn_body, a_is_transposed=True)
    n_units = B_IN_COLS // B_UNIT

    def g_map(u):
        nat = jnp.minimum(u // per_piece, 3 * B_GROUPS - 1)
        piece = (nat % B_GROUPS) * 3 + nat // B_GROUPS
        return (piece, 0, u % per_piece)

    def body(a_ref, g_ref, z_ref, o_ref):
        u = pl.program_id(0)

        @pl.when(u < 3 * B_GROUPS * per_piece)
        def _():
            body_one(a_ref, g_ref, o_ref)

        @pl.when(u >= 3 * B_GROUPS * per_piece)
        def _():
            body_one(a_ref, z_ref, o_ref)

    return pl.pallas_call(
        body, name=name, grid=(n_units,),
        in_specs=[pl.BlockSpec((k, m), lambda u: (0, 0)),
                  pl.BlockSpec((None, m, B_UNIT), g_map),
                  pl.BlockSpec((m, B_UNIT), lambda u: (0, jnp.where(u < 3 * B_GROUPS * per_piece, 0, u % per_piece)))],
        out_specs=pl.BlockSpec((None, k, B_UNIT), lambda u: (u // B_SHARD_UNITS, 0, u % B_SHARD_UNITS)),
        out_shape=jax.ShapeDtypeStruct((N_DEV, k, B_IN_COLS // N_DEV), out_dtype),
        compiler_params=_params(("arbitrary",)),
    )(at, g9, gz)


def _loss_body(y_ref, t_ref, dy_ref, loss_ref, acc):
    i = pl.program_id(0)
    d = y_ref.shape[1]
    err = y_ref[...] - t_ref[...]
    dy_ref[...] = err * (1.0 / d)

    @pl.when(i == 0)
    def _():
        acc[...] = jnp.zeros_like(acc)

    acc[...] += jnp.sum(err * err, axis=0, keepdims=True)

    @pl.when(i == pl.num_programs(0) - 1)
    def _():
        total = jnp.sum(acc[...], axis=1, keepdims=True) * (0.5 / d)
        loss_ref[...] = jnp.broadcast_to(total, loss_ref.shape)


def loss_head(y, target):
    t_rows, d = y.shape
    tm = min(ROW_TILE, t_rows)
    blk = pl.BlockSpec((tm, d), lambda i: (i, 0))
    return pl.pallas_call(
        _loss_body, name="loss_head", grid=(t_rows // tm,),
        in_specs=[blk, blk],
        out_specs=[blk, pl.BlockSpec((8, 128), lambda i: (0, 0))],
        out_shape=[jax.ShapeDtypeStruct((t_rows, d), F32), jax.ShapeDtypeStruct((8, 128), F32)],
        scratch_shapes=[pltpu.VMEM((1, d), F32)],
        compiler_params=_params(("arbitrary",)),
    )(y, target)


def _adamw_body(p_ref, w_ref, m_ref, v_ref, g_ref, d_ref, nm_ref, nv_ref):
    g = p_ref[0].astype(F32)
    for s in range(1, N_DEV):
        g = g + p_ref[s].astype(F32)
    w = w_ref[...]
    m = ADAM_B1 * m_ref[...] + (1.0 - ADAM_B1) * g
    v = ADAM_B2 * v_ref[...] + (1.0 - ADAM_B2) * (g * g)
    m_hat = m / (1.0 - ADAM_B1 ** ADAM_STEP)
    v_hat = v / (1.0 - ADAM_B2 ** ADAM_STEP)
    g_ref[...] = g
    d_ref[...] = -ADAM_LR * (m_hat / (jnp.sqrt(v_hat) + ADAM_EPS) + ADAM_WD * w)
    nm_ref[...] = m
    nv_ref[...] = v


def adamw(parts, w, m, v, *, name):
    _, r, c = w.shape
    tr = r if r <= 256 else 256
    blk = pl.BlockSpec((None, tr, c), lambda i: (0, i, 0))
    out = jax.ShapeDtypeStruct((1, r, c), F32)
    return pl.pallas_call(
        _adamw_body, name=name, grid=(r // tr,),
        in_specs=[pl.BlockSpec((N_DEV, tr, c), lambda i: (0, i, 0)), blk, blk, blk],
        out_specs=[blk, blk, blk, blk],
        out_shape=[out, out, out, out],
        compiler_params=_params(("arbitrary",)),
    )(parts, w, m, v)


MESH_ID = pl.DeviceIdType.MESH
HBM_SPEC = pl.BlockSpec(memory_space=pl.ANY)


def _my_place():
    return lax.axis_index("x"), lax.axis_index("y"), lax.axis_index("c")


def _flat(x, y, c):
    return 4 * x + 2 * y + c


def _all_gather_body(*refs, n):
    ins, outs = refs[:n], refs[n:2 * n]
    send_sems, recv_sems, local_sems = refs[2 * n:]
    x, y, c = _my_place()
    me, sibling = (x, y, c), (x, y, 1 - c)
    chips = [(1 - x, y), (x, 1 - y), (1 - x, 1 - y)]
    pending = []
    for a in range(n):
        src, out = ins[a], outs[a]

        def copy(k, block, to, from_input=False, a=a, src=src, out=out):
            slot = out.at[_flat(*block)]
            return pltpu.make_async_remote_copy(
                src_ref=src if from_input else slot, dst_ref=slot,
                send_sem=send_sems.at[7 * a + k], recv_sem=recv_sems.at[7 * a + k],
                device_id=to, device_id_type=MESH_ID)

        mine = pltpu.make_async_copy(src, out.at[_flat(*me)], local_sems.at[a])
        mine.start()
        first = [copy(0, me, sibling, True)] + [copy(1 + j, me, (*chip, c), True) for j, chip in enumerate(chips)]
        for cp in first:
            cp.start()
        pending.append((copy, mine, first))
    for copy, mine, first in pending:
        passed = [copy(4 + j, (*chip, c), sibling) for j, chip in enumerate(chips)]
        for j, chip in enumerate(chips):
            copy(1 + j, (*chip, c), me).wait_recv()
            passed[j].start()
        copy(0, sibling, me).wait_recv()
        for j, chip in enumerate(chips):
            copy(4 + j, (*chip, 1 - c), me).wait_recv()
        for cp in first + passed:
            cp.wait_send()
        mine.wait()


def all_gather(shards, *, name):
    n = len(shards)
    return pl.pallas_call(
        functools.partial(_all_gather_body, n=n), name=name,
        in_specs=[HBM_SPEC] * n, out_specs=[HBM_SPEC] * n,
        out_shape=[jax.ShapeDtypeStruct((N_DEV,) + s.shape, s.dtype) for s in shards],
        scratch_shapes=[pltpu.SemaphoreType.DMA((7 * n,)), pltpu.SemaphoreType.DMA((7 * n,)),
                        pltpu.SemaphoreType.DMA((n,))],
    )(*shards)


PEER_FLIPS = [(0, 0, 1), (1, 0, 0), (0, 1, 0), (1, 1, 0), (1, 0, 1), (0, 1, 1), (1, 1, 1)]


def _all_to_all_body(*refs, n):
    ins, outs = refs[:n], refs[n:2 * n]
    send_sems, recv_sems, local_sems = refs[2 * n:]
    x, y, c = _my_place()
    me = _flat(x, y, c)
    waits = []
    for a in range(n):
        src, out = ins[a], outs[a]
        mine = pltpu.make_async_copy(src.at[me], out.at[me], local_sems.at[a])
        mine.start()
        waits.append(mine)
        for k, (fx, fy, fc) in enumerate(PEER_FLIPS):
            peer = (1 - x if fx else x, 1 - y if fy else y, 1 - c if fc else c)
            theirs = _flat(*peer)
            sems = dict(send_sem=send_sems.at[7 * a + k], recv_sem=recv_sems.at[7 * a + k],
                        device_id=peer, device_id_type=MESH_ID)
            send = pltpu.make_async_remote_copy(src_ref=src.at[theirs], dst_ref=out.at[me], **sems)
            send.start()
            recv = pltpu.make_async_remote_copy(src_ref=src.at[theirs], dst_ref=out.at[theirs], **sems)
            waits.append((send, recv))
    for w in waits:
        if isinstance(w, tuple):
            w[0].wait_send()
            w[1].wait_recv()
        else:
            w.wait()


def all_to_all(parts, *, name):
    n = len(parts)
    return pl.pallas_call(
        functools.partial(_all_to_all_body, n=n), name=name,
        in_specs=[HBM_SPEC] * n, out_specs=[HBM_SPEC] * n,
        out_shape=[jax.ShapeDtypeStruct(p.shape, p.dtype) for p in parts],
        scratch_shapes=[pltpu.SemaphoreType.DMA((7 * n,)), pltpu.SemaphoreType.DMA((7 * n,)),
                        pltpu.SemaphoreType.DMA((n,))],
    )(*parts)


HBM_ONLY = pl.BlockSpec(memory_space=pltpu.HBM)
SEM_SPEC = pl.BlockSpec(memory_space=pltpu.SEMAPHORE)
DATAFLOW_EFFECT = pltpu.SideEffectType.DATAFLOW_SIDE_EFFECTING


def _split_copies(srcs, lands, send_sems, recv_sems, n, scatter):
    x, y, c = _my_place()
    me = _flat(x, y, c)
    pairs = []
    for a in range(n):
        for k, (fx, fy, fc) in enumerate(PEER_FLIPS):
            peer = (1 - x if fx else x, 1 - y if fy else y, 1 - c if fc else c)
            theirs = _flat(*peer)
            src = srcs[a].at[theirs] if scatter else srcs[a]
            sems = dict(send_sem=send_sems.at[7 * a + k], recv_sem=recv_sems.at[7 * a + k],
                        device_id=peer, device_id_type=MESH_ID)
            pairs.append((pltpu.make_async_remote_copy(src_ref=src, dst_ref=lands[a].at[me], **sems),
                          pltpu.make_async_remote_copy(src_ref=src, dst_ref=lands[a].at[theirs], **sems)))
    return pairs


def _exchange_start_body(*refs, n, scatter):
    srcs, lands = refs[:n], refs[n:2 * n]
    send_sems, recv_sems = refs[2 * n], refs[2 * n + 1]
    token = refs[-1]
    for send, _ in _split_copies(srcs, lands, send_sems, recv_sems, n, scatter):
        send.start()
    token[...] = jnp.zeros_like(token)


def exchange_start(srcs, lands, *, scatter, name):
    n = len(srcs)
    args = [pltpu.with_memory_space_constraint(t, pltpu.HBM) for t in list(srcs) + list(lands)]
    outs = pl.pallas_call(
        functools.partial(_exchange_start_body, n=n, scatter=scatter), name=name,
        out_shape=(pltpu.SemaphoreType.DMA((7 * n,)), pltpu.SemaphoreType.DMA((7 * n,)),
                   *[pltpu.HBM(t.shape, t.dtype) for t in args],
                   jax.ShapeDtypeStruct((8, 128), F32)),
        in_specs=[HBM_ONLY] * (2 * n),
        out_specs=(SEM_SPEC, SEM_SPEC, *[HBM_ONLY] * (2 * n), pl.BlockSpec(memory_space=pltpu.VMEM)),
        input_output_aliases={i: 2 + i for i in range(2 * n)},
        compiler_params=pltpu.CompilerParams(has_side_effects=DATAFLOW_EFFECT),
    )(*args)
    return outs[0], outs[1], outs[2:2 + n], outs[2 + n:2 + 2 * n], outs[-1]


def _exchange_wait_body(*refs, n, scatter):
    srcs, lands = refs[:n], refs[n:2 * n]
    send_sems, recv_sems = refs[2 * n], refs[2 * n + 1]
    for send, recv in _split_copies(srcs, lands, send_sems, recv_sems, n, scatter):
        send.wait_send()
        recv.wait_recv()


def exchange_wait(send_sems, recv_sems, srcs, lands, after, *, scatter, name):
    n = len(srcs)
    outs = pl.pallas_call(
        functools.partial(_exchange_wait_body, n=n, scatter=scatter), name=name,
        out_shape=tuple(pltpu.HBM(t.shape, t.dtype) for t in list(srcs) + list(lands)),
        in_specs=[HBM_ONLY] * (2 * n) + [SEM_SPEC, SEM_SPEC, HBM_SPEC],
        out_specs=tuple([HBM_ONLY] * (2 * n)),
        input_output_aliases={i: i for i in range(2 * n)},
        compiler_params=pltpu.CompilerParams(has_side_effects=DATAFLOW_EFFECT),
    )(*srcs, *lands, send_sems, recv_sems, after)
    return outs[n:]


def _own_slot_only(shape_dtype, own, slot):
    land = lax.empty(shape_dtype.shape, shape_dtype.dtype)
    return lax.dynamic_update_slice(land, own[None], (slot,) + (0,) * own.ndim)


def _pad_rows(a, rows=8):
    return jnp.pad(a, ((0, rows - a.shape[0]), (0, 0)))


def _gate_rows(a_log, dt_bias):
    z = jnp.zeros((8, 128), F32)
    return z.at[0, A_HEADS:2 * A_HEADS].set(a_log[0]).at[1, A_HEADS:2 * A_HEADS].set(dt_bias[0])


def _pack_small(norm_g, a_log, a_dt_bias, a_norm_g, b_q_norm_g, b_k_norm_g):
    return jnp.concatenate([
        norm_g[0].reshape(8, 128), norm_g[1].reshape(8, 128),
        _gate_rows(a_log, a_dt_bias),
        _pad_rows(a_norm_g[0].reshape(2, 128)),
        _pad_rows(jnp.concatenate([b_q_norm_g[0], b_k_norm_g[0]], axis=0)),
    ], axis=0)


def _unpack_small(p):
    return (p[0:16].reshape(2, D_MODEL), p[16:17, A_HEADS:2 * A_HEADS], p[17:18, A_HEADS:2 * A_HEADS],
            p[24:26].reshape(1, A_DV), p[32:35][None], p[35:38][None])


def kernel(x, positions, norm_g, a_w_in, a_conv_w, a_log, a_dt_bias, a_norm_g, a_w_out, b_w_in, b_q_norm_g, b_k_norm_g, b_w_out, loss_target, m_norm_g, m_a_w_in, m_a_conv_w, m_a_log, m_a_dt_bias, m_a_norm_g, m_a_w_out, m_b_w_in, m_b_q_norm_g, m_b_k_norm_g, m_b_w_out, v_norm_g, v_a_w_in, v_a_conv_w, v_a_log, v_a_dt_bias, v_a_norm_g, v_a_w_out, v_b_w_in, v_b_q_norm_g, v_b_k_norm_g, v_b_w_out):
    n_seq, s_len, d = x.shape
    t_rows = n_seq * s_len
    n_chunks = s_len // CHUNK
    x0 = x.reshape(t_rows, d)
    target = loss_target.reshape(t_rows, d)
    my_slot = _flat(*_my_place())

    g_a_in, g_conv = all_gather([a_w_in[0].astype(BF16), _pad_rows(a_conv_w[0])], name="gather_weights_first")
    later = [a_w_out[0].astype(BF16), b_w_in[0].astype(BF16), b_w_out[0].astype(BF16)]
    lands = [_own_slot_only(jax.ShapeDtypeStruct((N_DEV,) + t.shape, t.dtype), t, my_slot) for t in later]
    w_send, w_recv, later, lands, w_token = exchange_start(later, lands, scatter=False, name="gather_weights_start")
    w_a_in = jnp.pad(g_a_in.transpose(1, 0, 2).reshape(d, A_IN), ((0, 0), (0, A_IN_PAD - A_IN)))
    conv_w8 = g_conv.transpose(1, 0, 2).reshape(8, 2 * A_QK + A_VW)

    gains_model = _pad_rows(norm_g)
    gate_prm = _gate_rows(a_log, a_dt_bias)
    gain_a_out = _pad_rows(a_norm_g)
    gains_qk = _pad_rows(jnp.concatenate([b_q_norm_g[0], b_k_norm_g[0]], axis=0))
    cos_t, sin_t = rope_tables(positions.reshape(t_rows))

    h0, h0_t = rms_fwd(x0, gains_model, 0, after=w_token)
    proj_a = mm_nn(h0, w_a_in, tn=896, name="proj_a")
    gates_col, gates_row = a_gates_fwd(proj_a, gate_prm, n_seq)
    gates_row = gates_row.reshape(n_seq, 2 * A_HEADS, s_len // SUPER, 1, SUPER)
    o_a, tinv, states, og_a, q_a, k_a, v_a = gdn_fwd(proj_a, conv_w8, gates_col, gates_row, gain_a_out, n_seq)
    g_a_out, g_b_in, g_b_out = exchange_wait(w_send, w_recv, later, lands, og_a, scatter=False,
                                             name="gather_weights_wait")
    w_a_out = g_a_out.reshape(A_VW, d)
    w_b_in = g_b_in.transpose(1, 0, 2).reshape(d, B_IN_COLS)
    w_b_out = g_b_out.reshape(B_W, d)
    x1 = mm_nn(og_a, w_a_out, x0, tn=1024, name="out_a")

    h1, h1_t = rms_fwd(x1, gains_model, 1)
    proj_b = mm_nn_pieces(h1, w_b_in, name="proj_b")
    og_b, o_b, lse = attn_fwd(proj_b, cos_t, sin_t, gains_qk, n_seq)
    y = mm_nn(og_b, w_b_out, x1, tn=1024, name="out_b")

    dy, loss_blk = loss_head(y, target)
    loss = lax.psum(loss_blk[0, 0], ("x", "y", "c"))

    d_og_b = mm_nt(dy, w_b_out, tn=512, name="d_og_b")
    dw_b_out = mm_tn(og_b, dy, tn=256, out_dtype=BF16, name="dw_b_out")
    dqkv_b, dz_b, d_gains_qk = attn_bwd(proj_b, cos_t, sin_t, gains_qk, d_og_b, o_b, lse, n_seq)
    dh1 = mm_nt_pieces(dqkv_b, w_b_in, name="dh1_qkv")
    dh1 = mm_nt(dz_b, w_b_in, dh1, tn=B_W, col_off=3 * B_GROUPS, name="dh1_z")
    dw_b_in = mm_tn_b_in(h1_t, dqkv_b, dz_b, out_dtype=BF16, name="dw_b_in")
    dx1, d_gain1 = rms_bwd(x1, gains_model, 1, dh1, dy)

    dw_a_out = mm_tn(og_a, dx1, tn=256, out_dtype=BF16, name="dw_a_out")
    early = [dw_b_in, dw_b_out.reshape(N_DEV, B_W // N_DEV, d), dw_a_out.reshape(N_DEV, A_VW // N_DEV, d)]
    lands = [_own_slot_only(t, lax.dynamic_index_in_dim(t, my_slot, 0, keepdims=False), my_slot) for t in early]
    g_send, g_recv, early, lands, g_token = exchange_start(early, lands, scatter=True, name="scatter_grads_start")

    d_og_a = mm_nt(dx1, w_a_out, tn=512, name="d_og_a", after=g_token)
    d_xq, d_xk, d_xv, dgates, dz_a, d_gain_a_out, d_cq, d_ck, d_cv = gdn_bwd(
        q_a, k_a, v_a, gates_col, gates_row, tinv, states, d_og_a, o_a, proj_a, conv_w8, gain_a_out, n_seq)
    d_conv = jnp.concatenate([d_cq.sum(axis=0), d_ck.sum(axis=0), d_cv.sum(axis=0)], axis=1)
    d_gate_logits, d_gate_prm = a_gates_bwd(proj_a, gate_prm, dgates, n_seq)
    dw_a_in = jnp.concatenate([
        mm_tn(h0_t, piece, tn=min(256, piece.shape[1]), out_dtype=BF16, name=f"dw_a_in_{nm}", a_is_transposed=True)
        for nm, piece in (("q", d_xq), ("k", d_xk), ("v", d_xv), ("z", dz_a), ("gates", d_gate_logits))
    ], axis=1)[:, :A_IN]
    shard_a_in = A_IN // N_DEV
    last = [dw_a_in.reshape(d, N_DEV, shard_a_in).transpose(1, 0, 2)]
    last_lands = [_own_slot_only(t, lax.dynamic_index_in_dim(t, my_slot, 0, keepdims=False), my_slot) for t in last]
    l_send, l_recv, last, last_lands, l_token = exchange_start(last, last_lands, scatter=True,
                                                               name="scatter_last_start")
    dh0 = mm_nt(d_xq, w_a_in, tn=512, name="dh0_q", after=l_token)
    dh0 = mm_nt(d_xk, w_a_in, dh0, tn=512, col_off=A_QK // 512, name="dh0_k")
    dh0 = mm_nt(d_xv, w_a_in, dh0, tn=512, col_off=2 * A_QK // 512, name="dh0_v")
    dh0 = mm_nt(dz_a, w_a_in, dh0, tn=512, col_off=(2 * A_QK + A_VW) // 512, name="dh0_z")
    dh0 = mm_nt(d_gate_logits, w_a_in, dh0, tn=128, col_off=A_GATE_COL, name="dh0_gates")
    dx0, d_gain0 = rms_bwd(x0, gains_model, 0, dh0, dx1)

    small = jnp.concatenate([
        d_gain0[0].reshape(8, 128), d_gain1[0].reshape(8, 128), d_gate_prm,
        _pad_rows(d_gain_a_out[0].reshape(2, 128)), d_gains_qk], axis=0)
    r_small, r_conv = all_gather([small, d_conv], name="gather_small_grads")
    conv_cols = a_conv_w.shape[2]
    r_conv = lax.dynamic_slice(r_conv, (0, 0, my_slot * conv_cols), (N_DEV, 8, conv_cols))

    r_b_in, r_b_out, r_a_out = exchange_wait(g_send, g_recv, early, lands, r_small, scatter=True,
                                             name="scatter_grads_wait")
    (r_a_in,) = exchange_wait(l_send, l_recv, last, last_lands, r_small, scatter=True, name="scatter_last_wait")

    upd = {}
    upd["a_w_in"] = adamw(r_a_in, a_w_in, m_a_w_in, v_a_w_in, name="adamw_a_w_in")
    upd["a_w_out"] = adamw(r_a_out, a_w_out, m_a_w_out, v_a_w_out, name="adamw_a_w_out")
    upd["b_w_in"] = adamw(r_b_in, b_w_in, m_b_w_in, v_b_w_in, name="adamw_b_w_in")
    upd["b_w_out"] = adamw(r_b_out, b_w_out, m_b_w_out, v_b_w_out, name="adamw_b_w_out")
    upd["a_conv_w"] = [t[:, :A_CONV] for t in adamw(
        r_conv, _pad_rows(a_conv_w[0])[None], _pad_rows(m_a_conv_w[0])[None], _pad_rows(v_a_conv_w[0])[None],
        name="adamw_a_conv_w")]
    small_upd = adamw(
        r_small,
        _pack_small(norm_g, a_log, a_dt_bias, a_norm_g, b_q_norm_g, b_k_norm_g)[None],
        _pack_small(m_norm_g, m_a_log, m_a_dt_bias, m_a_norm_g, m_b_q_norm_g, m_b_k_norm_g)[None],
        _pack_small(v_norm_g, v_a_log, v_a_dt_bias, v_a_norm_g, v_b_q_norm_g, v_b_k_norm_g)[None],
        name="adamw_small")
    small_names = ("norm_g", "a_log", "a_dt_bias", "a_norm_g", "b_q_norm_g", "b_k_norm_g")
    unpacked = [_unpack_small(t[0]) for t in small_upd]
    for i, nm in enumerate(small_names):
        upd[nm] = [u[i] for u in unpacked]

    order = ("norm_g", "a_w_in", "a_conv_w", "a_log", "a_dt_bias", "a_norm_g", "a_w_out",
             "b_w_in", "b_q_norm_g", "b_k_norm_g", "b_w_out")
    outs = [loss, dx0.reshape(n_seq, s_len, d)]
    for kind in range(4):
        for nm in order:
            outs.append(upd[nm][kind])
    return tuple(outs)
```

```python
import functools
import math

import jax
import jax.numpy as jnp
from jax import lax
from jax.experimental import pallas as pl
from jax.experimental.pallas import tpu as pltpu

F32 = jnp.float32
BF16 = jnp.bfloat16

D_MODEL = 1024
EPS = 1e-6
N_DEV = 8

A_HEADS = 8
A_DK = 128
A_DV = 256
A_QK = A_HEADS * A_DK
A_VW = A_HEADS * A_DV
A_CONV = 4
CHUNK = 64
A_IN = 2 * A_QK + 2 * A_VW + 2 * A_HEADS
A_IN_PAD = 2 * A_QK + 2 * A_VW + 128
A_GATE_COL = (2 * A_QK + 2 * A_VW) // 128

B_DILATIONS = (1, 4, 16)
B_GROUPS = 3
B_HEADS = 8
B_DH = 128
B_W = B_HEADS * B_DH
B_BLOCK = 128
B_PIECES = 3 * B_GROUPS + 1
ROPE_THETA = 500000.0
ROPE_DIMS = B_DH // 4
ROPE_HALF = ROPE_DIMS // 2

ADAM_LR = 0.001
ADAM_B1 = 0.9
ADAM_B2 = 0.999
ADAM_EPS = 1e-08
ADAM_WD = 0.01
ADAM_STEP = 10

VMEM_LIMIT = 60 * 1024 * 1024


def _params(sem):
    return pltpu.CompilerParams(dimension_semantics=sem, vmem_limit_bytes=VMEM_LIMIT)


def _bf(x):
    return x.astype(BF16)


def _mm(a, b):
    return jnp.dot(_bf(a), _bf(b), preferred_element_type=F32)


def _mm_nt(a, b):
    return lax.dot_general(_bf(a), _bf(b), (((1,), (1,)), ((), ())), preferred_element_type=F32)


def _mm_tn(a, b):
    return lax.dot_general(_bf(a), _bf(b), (((0,), (0,)), ((), ())), preferred_element_type=F32)


def _split(x):
    hi = _bf(x)
    return hi, _bf(x - hi.astype(F32))


def _mm3(a, b):
    ah, al = _split(a)
    bh, bl = _split(b)
    d = functools.partial(jnp.dot, preferred_element_type=F32)
    return d(ah, bh) + (d(ah, bl) + d(al, bh))


def _colsum_as_col(z):
    zh, zl = _split(z)
    ones = jnp.ones((z.shape[0], 128), BF16)
    tn = functools.partial(lax.dot_general, dimension_numbers=(((0,), (0,)), ((), ())),
                           preferred_element_type=F32)
    return (tn(zh, ones) + tn(zl, ones))[:, 0:1]


def _sigmoid(x):
    return 0.5 * jnp.tanh(0.5 * x) + 0.5


INV_BASE = 8
INV_NEWTON = 2
GDN_GROUP = 4
SUPER = GDN_GROUP * CHUNK
GDN_WIDTH = 2

A_K_COL = A_QK // A_DK
A_V_COL = 2 * A_QK // A_DV


def _inverse_steps(m, row, col):
    eye = (row == col).astype(F32)
    d = jnp.where(row // INV_BASE == col // INV_BASE, m, 0.0)
    x = eye - d
    p = _mm(d, d)
    yield
    steps = int(math.log2(INV_BASE)) - 1
    for i in range(steps):
        x = x + _mm(x, p)
        if i + 1 < steps:
            p = _mm(p, p)
        yield
    size = INV_BASE
    while size < CHUNK:
        c = jnp.where((row // (2 * size) == col // (2 * size)) & (row // size != col // size), m, 0.0)
        xc = _mm(x, c)
        yield
        x = x - _mm(xc, x)
        yield
        size *= 2
    for _ in range(INV_NEWTON):
        r = eye - x - _mm3(m, x)
        yield
        x = x + _mm(x, r)
        yield
    return x


def _drain(gen):
    while True:
        try:
            next(gen)
        except StopIteration as stop:
            return stop.value


def _interleave(*gens):
    live = list(gens)
    while live:
        for g in list(live):
            try:
                next(g)
            except StopIteration:
                live.remove(g)


def _diag_blocks_tall(x):
    return jnp.concatenate([x[i * CHUNK:(i + 1) * CHUNK, i * CHUNK:(i + 1) * CHUNK] for i in range(GDN_GROUP)], axis=0)


def _tall_to_block_diag(t, same):
    return jnp.where(same, jnp.concatenate([t] * GDN_GROUP, axis=1), 0.0)


def _block_sum(same, x):
    xh, xl = _split(jnp.broadcast_to(x, (SUPER, 128)))
    ones = same.astype(BF16)
    d = functools.partial(jnp.dot, preferred_element_type=F32)
    return (d(ones, xh) + d(ones, xl))[:, 0:1]


def _aligned_rows(index, size):
    start = index * size
    return pl.ds(start if isinstance(start, int) else pl.multiple_of(start, size), size)


def _super_rows(i):
    return _aligned_rows(i, SUPER)


def _chunk_rows(n):
    return _aligned_rows(n, CHUNK)


def _gdn_super_steps(q, k, v, gcb, gr, head, tinv_tall=None):
    lane = lax.broadcasted_iota(jnp.int32, (SUPER, 128), 1)
    row = lax.broadcasted_iota(jnp.int32, (SUPER, SUPER), 0)
    col = lax.broadcasted_iota(jnp.int32, (SUPER, SUPER), 1)
    same = row // CHUNK == col // CHUNK
    beta = jnp.sum(jnp.where(lane == head, gcb, 0.0), axis=1, keepdims=True)
    gc = jnp.sum(jnp.where(lane == A_HEADS + head, gcb, 0.0), axis=1, keepdims=True)
    g_last = jnp.sum(jnp.where(col == (row // CHUNK) * CHUNK + (CHUNK - 1), gr, 0.0), axis=1, keepdims=True)
    gamma = jnp.exp(gc)
    decay = jnp.where(same & (row >= col), jnp.exp(jnp.minimum(gc - gr, 0.0)), 0.0)
    kb = k * beta
    m = jnp.where(same & (row > col), _mm_nt(kb, k) * decay, 0.0)
    p = jnp.where(same & (row >= col), _mm_nt(q, k) * decay, 0.0)
    yield
    if tinv_tall is None:
        tinv = yield from _inverse_steps(m, row, col)
    else:
        tinv = _tall_to_block_diag(tinv_tall, same)
    u = _mm(tinv, v * beta)
    w = _mm(tinv, kb * gamma)
    yield
    e_tail = jnp.exp(g_last - gc)
    return dict(beta=beta, gc=gc, g_last=g_last, gamma=gamma, decay=decay, kb=kb, m=m,
                tinv=tinv, u=u, w=w, p=p, e_tail=e_tail, row=row, col=col, lane=lane, same=same)


def _gdn_super_common(q, k, v, gcb, gr, head, tinv_tall=None):
    return _drain(_gdn_super_steps(q, k, v, gcb, gr, head, tinv_tall))


def _store_scan_operands(rows, q, k, t, u_scr, w_scr, p_scr, qg_scr, ke_scr, gl_scr):
    u_scr[rows, :] = t["u"]
    w_scr[rows, :] = _bf(t["w"])
    p_scr[rows, :] = _bf(_diag_blocks_tall(t["p"]))
    qg_scr[rows, :] = _bf(q * t["gamma"])
    ke_scr[rows, :] = _bf(k * t["e_tail"])
    gl_scr[rows, :] = jnp.broadcast_to(jnp.exp(t["g_last"]), (SUPER, 128))


def _gdn_fwd_body(q_ref, k_ref, v_ref, gc_ref, gr_ref, wq_ref, wk_ref, wv_ref, z_ref, gn_ref,
                  o_ref, tinv_ref, st_ref, og_ref, qo_ref, ko_ref, vo_ref, s_scr, *sets):
    head = pl.program_id(1)
    n_super = q_ref.shape[0] // SUPER
    all_sets = [sets[6 * i:6 * i + 6] for i in range(2 * GDN_WIDTH)]
    whole = pl.ds(0, SUPER)

    def conv_silu(x_ref, w_ref, i):
        rows = _super_rows(i)
        x, w = x_ref[rows, :], w_ref[...]
        halo = jnp.zeros((8, x.shape[1]), F32) if i == 0 else x_ref[pl.ds(i * SUPER - 8, 8), :]
        ext = jnp.concatenate([halo, x], axis=0)
        c = x * w[A_CONV - 1:A_CONV, :]
        for j in range(1, A_CONV):
            c = c + pltpu.roll(ext, j, 0)[8:, :] * w[A_CONV - 1 - j:A_CONV - j, :]
        return c * _sigmoid(c)

    def unit(a):
        return a * lax.rsqrt(jnp.sum(a * a, axis=1, keepdims=True) + EPS)

    def prepare_steps(i, dst):
        rows = _super_rows(i)
        q = unit(conv_silu(q_ref, wq_ref, i)) * A_DK ** -0.5
        k = unit(conv_silu(k_ref, wk_ref, i))
        v = conv_silu(v_ref, wv_ref, i)
        qo_ref[rows, :], ko_ref[rows, :], vo_ref[rows, :] = q, k, v
        t = yield from _gdn_super_steps(q, k, v, gc_ref[rows, :], gr_ref[i], head)
        tinv_ref[rows, :] = _diag_blocks_tall(t["tinv"])
        _store_scan_operands(whole, q, k, t, *dst)

    def scan_steps(i, src):
        u_scr, w_scr, p_scr, qg_scr, ke_scr, gl_scr = src
        for j in range(GDN_GROUP):
            n = i * GDN_GROUP + j
            local = pl.ds(j * CHUNK, CHUNK)
            s = s_scr[...]
            sb = _bf(s)
            st_ref[n] = sb
            ws = jnp.dot(w_scr[local, :], sb, preferred_element_type=F32)
            yield
            vb = _bf(u_scr[local, :] - ws)
            o = (jnp.dot(qg_scr[local, :], sb, preferred_element_type=F32)
                 + jnp.dot(p_scr[local, :], vb, preferred_element_type=F32))
            s_new = s * gl_scr[local, :][0:1, 0:1] + lax.dot_general(
                ke_scr[local, :], vb, (((0,), (0,)), ((), ())), preferred_element_type=F32)
            yield
            rows = _chunk_rows(n)
            o_ref[rows, :] = o
            s_scr[...] = s_new
            silu, _ = _silu_parts(z_ref[rows, :])
            r = lax.rsqrt(jnp.mean(o * o, axis=1, keepdims=True) + EPS)
            og_ref[rows, :] = ((o * r * gn_ref[0:1, :]) * silu).astype(og_ref.dtype)

    def scan_many(first, srcs):
        for j, src in enumerate(srcs):
            yield from scan_steps(first + j, src)

    groups = [all_sets[:GDN_WIDTH], all_sets[GDN_WIDTH:]]
    _interleave(*[prepare_steps(j, groups[0][j]) for j in range(GDN_WIDTH)])
    s_scr[...] = jnp.zeros_like(s_scr)
    for g in range(n_super // GDN_WIDTH):
        cur, nxt = groups[g % 2], groups[(g + 1) % 2]
        first = g * GDN_WIDTH
        following = [prepare_steps(first + GDN_WIDTH + j, nxt[j]) for j in range(GDN_WIDTH)
                     if first + GDN_WIDTH + j < n_super]
        _interleave(scan_many(first, cur), *following)


def _gdn_in_specs(s_len, n_super, from_proj):
    k_col, v_col = (A_K_COL, A_V_COL) if from_proj else (0, 0)
    return [
        pl.BlockSpec((s_len, A_DK), lambda b, h: (b, h)),
        pl.BlockSpec((s_len, A_DK), lambda b, h: (b, k_col + h)),
        pl.BlockSpec((s_len, A_DV), lambda b, h: (b, v_col + h)),
        pl.BlockSpec((s_len, 128), lambda b, h: (b, 0)),
        pl.BlockSpec((None, None, n_super, 1, SUPER), lambda b, h: (b, A_HEADS + h, 0, 0, 0)),
    ]


def _gdn_scan_scratch(s_len):
    return [pltpu.VMEM((A_DK, A_DV), F32), pltpu.VMEM((s_len, A_DV), F32),
            pltpu.VMEM((s_len, A_DK), BF16), pltpu.VMEM((s_len, CHUNK), BF16),
            pltpu.VMEM((s_len, A_DK), BF16), pltpu.VMEM((s_len, A_DK), BF16),
            pltpu.VMEM((s_len, 128), F32)]


def gdn_fwd(proj_a, conv_w8, gates_col, gates_row, norm_g8, n_seq):
    t_rows = proj_a.shape[0]
    s_len = t_rows // n_seq
    n_chunks = s_len // CHUNK
    qk_spec = pl.BlockSpec((s_len, A_DK), lambda b, h: (b, h))
    v_spec = pl.BlockSpec((s_len, A_DV), lambda b, h: (b, h))
    return pl.pallas_call(
        _gdn_fwd_body, name="gdn_fwd", grid=(n_seq, A_HEADS),
        in_specs=_gdn_in_specs(s_len, s_len // SUPER, True) + [
            pl.BlockSpec((8, A_DK), lambda b, h: (0, h)),
            pl.BlockSpec((8, A_DK), lambda b, h: (0, A_K_COL + h)),
            pl.BlockSpec((8, A_DV), lambda b, h: (0, A_V_COL + h)),
            pl.BlockSpec((s_len, A_DV), lambda b, h: (b, A_Z_COL + h)),
            pl.BlockSpec((8, A_DV), lambda b, h: (0, 0)),
        ],
        out_specs=[
            v_spec,
            pl.BlockSpec((s_len, CHUNK), lambda b, h: (b * A_HEADS + h, 0)),
            pl.BlockSpec((None, n_chunks, A_DK, A_DV), lambda b, h: (b * A_HEADS + h, 0, 0, 0)),
            v_spec, qk_spec, qk_spec, v_spec,
        ],
        out_shape=[
            jax.ShapeDtypeStruct((t_rows, A_VW), F32),
            jax.ShapeDtypeStruct((n_seq * A_HEADS * s_len, CHUNK), F32),
            jax.ShapeDtypeStruct((n_seq * A_HEADS, n_chunks, A_DK, A_DV), BF16),
            jax.ShapeDtypeStruct((t_rows, A_VW), BF16),
            jax.ShapeDtypeStruct((t_rows, A_QK), F32),
            jax.ShapeDtypeStruct((t_rows, A_QK), F32),
            jax.ShapeDtypeStruct((t_rows, A_VW), F32),
        ],
        scratch_shapes=_gdn_scan_scratch(SUPER) + (2 * GDN_WIDTH - 1) * _gdn_scan_scratch(SUPER)[1:],
        compiler_params=_params(("arbitrary", "arbitrary")),
    )(proj_a, proj_a, proj_a, gates_col, gates_row, conv_w8, conv_w8, conv_w8, proj_a, norm_g8)


def _gdn_bwd_body(q_ref, k_ref, v_ref, gc_ref, gr_ref, tinv_ref, st_ref, dog_ref, oa_ref, z_ref, gn_ref,
                  xq_ref, xk_ref, xv_ref, wq_ref, wk_ref, wv_ref,
                  dxq_ref, dxk_ref, dxv_ref, dgc_ref, dz_ref, dgn_ref, dwq_ref, dwk_ref, dwv_ref,
                  ds_scr, cq_scr, ck_scr, cv_scr, *sets):
    head = pl.program_id(1)
    n_super = q_ref.shape[0] // SUPER
    ops = (sets[0:7], sets[7:14])
    res = (sets[14:21], sets[21:28])
    whole = pl.ds(0, SUPER)
    tn = functools.partial(lax.dot_general, dimension_numbers=(((0,), (0,)), ((), ())), preferred_element_type=F32)
    nt = functools.partial(lax.dot_general, dimension_numbers=(((1,), (1,)), ((), ())), preferred_element_type=F32)

    @pl.when(head == 0)
    def _():
        dgc_ref[...] = jnp.zeros_like(dgc_ref)

    @pl.when((head == 0) & (pl.program_id(0) == 0))
    def _():
        dgn_ref[...] = jnp.zeros_like(dgn_ref)

    carry = (cq_scr, ck_scr, cv_scr)
    for ref in carry + (dwq_ref, dwk_ref, dwv_ref):
        ref[...] = jnp.zeros_like(ref)

    def common_steps(i):
        rows = _super_rows(i)
        q, k, v = q_ref[rows, :], k_ref[rows, :], v_ref[rows, :]
        t = yield from _gdn_super_steps(q, k, v, gc_ref[rows, :], gr_ref[i], head, tinv_tall=tinv_ref[rows, :])
        return rows, q, k, v, t

    def stage_p(i, parity):
        rows, q, k, _, t = yield from common_steps(i)
        _store_scan_operands(whole, q, k, t, *ops[parity][:6])
        o, d_og, gain = oa_ref[rows, :], dog_ref[rows, :], gn_ref[0:1, :]
        r = lax.rsqrt(jnp.mean(o * o, axis=1, keepdims=True) + EPS)
        silu, dsilu = _silu_parts(z_ref[rows, :])
        xr = o * r
        d_on = d_og * silu
        dz_ref[rows, :] = (d_og * (xr * gain) * dsilu).astype(dz_ref.dtype)
        u = d_on * gain
        ops[parity][6][...] = r * u - xr * (r * r) * jnp.mean(u * o, axis=1, keepdims=True)
        dgn_ref[0:1, :] += jnp.sum(d_on * xr, axis=0, keepdims=True)

    def stage_s(i, parity):
        u_scr, w_scr, p_scr, qg_scr, ke_scr, gl_scr, do_scr = ops[parity]
        vn_scr, dvn_scr, dqg_scr, dw_scr, dkt_scr, sds_scr, dof_scr = res[parity]
        for j in reversed(range(GDN_GROUP)):
            n = i * GDN_GROUP + j
            local = pl.ds(j * CHUNK, CHUNK)
            ds_next = ds_scr[...]
            dsb = _bf(ds_next)
            sb = st_ref[n]
            s = sb.astype(F32)
            d_o = do_scr[local, :]
            dof_scr[local, :] = d_o
            d_ob = _bf(d_o)
            w_s = jnp.dot(w_scr[local, :], sb, preferred_element_type=F32)
            d_vn = tn(p_scr[local, :], d_ob) + jnp.dot(ke_scr[local, :], dsb, preferred_element_type=F32)
            d_qg = nt(d_ob, sb)
            qg_do = tn(qg_scr[local, :], d_ob)
            yield
            v_new = u_scr[local, :] - w_s
            d_vnb = _bf(d_vn)
            d_w = -nt(d_vnb, sb)
            d_kt = nt(_bf(v_new), dsb)
            w_dvn = tn(w_scr[local, :], d_vnb)
            yield
            vn_scr[local, :] = v_new
            dvn_scr[local, :] = d_vn
            dqg_scr[local, :] = d_qg
            dw_scr[local, :] = d_w
            dkt_scr[local, :] = d_kt
            sds = jnp.sum(jnp.sum(s * ds_next, axis=1, keepdims=True), axis=0, keepdims=True)
            sds_scr[local, :] = jnp.broadcast_to(sds, (CHUNK, 128))
            ds_scr[...] = qg_do + gl_scr[local, :][0:1, 0:1] * ds_next - w_dvn

    def conv_bwd(i, rows, x_ref, w_ref, dy, norm_scale, dx_ref, dw_ref, dc_above):
        x, w = x_ref[rows, :], w_ref[...]
        above = x_ref[pl.ds(pl.multiple_of(jnp.maximum(i * SUPER - 8, 0), 8), 8), :]
        ext = jnp.concatenate([jnp.where(i > 0, above, 0.0), x], axis=0)
        c = x * w[A_CONV - 1:A_CONV, :]
        for j in range(1, A_CONV):
            c = c + pltpu.roll(ext, j, 0)[8:, :] * w[A_CONV - 1 - j:A_CONV - j, :]
        sig = _sigmoid(c)
        a = c * sig
        if norm_scale is None:
            da = dy
        else:
            rn = lax.rsqrt(jnp.sum(a * a, axis=1, keepdims=True) + EPS)
            da = norm_scale * (rn * dy - a * (rn * rn * rn) * jnp.sum(dy * a, axis=1, keepdims=True))
        dc = da * (sig * (1.0 + c * (1.0 - sig)))
        ext_dc = jnp.concatenate([dc, dc_above[...]], axis=0)
        dc_above[...] = dc[0:8, :]
        dx = dc * w[A_CONV - 1:A_CONV, :]
        dw_ref[A_CONV - 1:A_CONV, :] += jnp.sum(dc * x, axis=0, keepdims=True)
        for j in range(1, A_CONV):
            dcs = pltpu.roll(ext_dc, SUPER + 8 - j, 0)[:SUPER, :]
            dx = dx + dcs * w[A_CONV - 1 - j:A_CONV - j, :]
            dw_ref[A_CONV - 1 - j:A_CONV - j, :] += jnp.sum(dcs * x, axis=0, keepdims=True)
        dx_ref[rows, :] = dx.astype(dx_ref.dtype)

    def stage_f(i, parity):
        vn_scr, dvn_scr, dqg_scr, dw_scr, dkt_scr, sds_scr, dof_scr = res[parity]
        rows, q, k, v, t = yield from common_steps(i)
        beta, gamma, decay, kb, e_tail = t["beta"], t["gamma"], t["decay"], t["kb"], t["e_tail"]
        row, col, lane, same = t["row"], t["col"], t["lane"], t["same"]
        d_o = dof_scr[...]
        v_new, d_vn = vn_scr[...], dvn_scr[...]
        d_qg, d_w, d_kt = dqg_scr[...], dw_scr[...], dkt_scr[...]
        gamma_last = jnp.exp(t["g_last"])

        d_p = jnp.where(same & (row >= col), _mm_nt(d_o, v_new), 0.0)
        d_ru = _mm_tn(t["tinv"], d_vn)
        d_rw = _mm_tn(t["tinv"], d_w)
        yield
        d_m = jnp.where(same & (row > col), -(_mm_nt(d_ru, t["u"]) + _mm_nt(d_rw, t["w"])), 0.0)
        yield

        x_p = d_p * decay
        y_m = d_m * decay
        d_kb = _mm(y_m, k) + d_rw * gamma
        d_q = _mm(x_p, k) + d_qg * gamma
        d_k = _mm_tn(x_p, q) + _mm_tn(y_m, kb) + d_kb * beta + d_kt * e_tail
        d_v = d_ru * beta
        conv_bwd(i, rows, xq_ref, wq_ref, d_q, A_DK ** -0.5, dxq_ref, dwq_ref, carry[0])
        conv_bwd(i, rows, xk_ref, wk_ref, d_k, 1.0, dxk_ref, dwk_ref, carry[1])
        conv_bwd(i, rows, xv_ref, wv_ref, d_v, None, dxv_ref, dwv_ref, carry[2])

        d_beta = (jnp.sum(d_ru * v, axis=1, keepdims=True)
                  + jnp.sum(d_kb * k, axis=1, keepdims=True))
        z = d_p * t["p"] + d_m * t["m"]
        eps_tail = jnp.sum(d_kt * k, axis=1, keepdims=True) * e_tail
        d_gc = (jnp.sum(z, axis=1, keepdims=True) - _colsum_as_col(z)
                + jnp.sum(d_qg * q, axis=1, keepdims=True) * gamma
                + jnp.sum(d_rw * kb, axis=1, keepdims=True) * gamma
                - eps_tail)
        d_glast = _block_sum(same, eps_tail) + gamma_last * sds_scr[...][:, 0:1]
        yield
        rcol = lax.broadcasted_iota(jnp.int32, (SUPER, 1), 0)
        d_gc = d_gc + jnp.where(rcol % CHUNK == CHUNK - 1, d_glast, 0.0)
        dgc_ref[rows, :] += (jnp.where(lane == head, d_beta, 0.0)
                             + jnp.where(lane == A_HEADS + head, d_gc, 0.0))

    last = n_super - 1
    _drain(stage_p(last, 1))
    ds_scr[...] = jnp.zeros_like(ds_scr)
    _interleave(stage_s(last, 1), stage_p(last - 1, 0))

    def pair(k, carry):
        i = last - 1 - 2 * k
        _interleave(stage_s(i, 0), stage_f(i + 1, 1), stage_p(i - 1, 1))
        _interleave(stage_s(i - 1, 1), stage_f(i, 0), stage_p(i - 2, 0))
        return carry

    lax.fori_loop(0, n_super // 2 - 1, pair, 0)
    _interleave(stage_s(0, 0), stage_f(1, 1))
    _drain(stage_f(0, 0))


def gdn_bwd(q, k, v, gates_col, gates_row, tinv, states, d_og, o, proj_a, conv_w8, norm_g8, n_seq):
    t_rows = q.shape[0]
    s_len = t_rows // n_seq
    n_chunks = s_len // CHUNK
    qk_spec = pl.BlockSpec((s_len, A_DK), lambda b, h: (b, h))
    v_spec = pl.BlockSpec((s_len, A_DV), lambda b, h: (b, h))
    gate_spec = pl.BlockSpec((s_len, 128), lambda b, h: (b, 0))
    gain_spec = pl.BlockSpec((8, A_DV), lambda b, h: (0, 0))
    dw_qk_spec = pl.BlockSpec((None, 8, A_DK), lambda b, h: (b, 0, h))
    dw_v_spec = pl.BlockSpec((None, 8, A_DV), lambda b, h: (b, 0, h))
    ops_set = _gdn_scan_scratch(SUPER)[1:] + [pltpu.VMEM((SUPER, A_DV), F32)]
    res_set = [pltpu.VMEM((SUPER, A_DV), F32), pltpu.VMEM((SUPER, A_DV), F32),
               pltpu.VMEM((SUPER, A_DK), F32), pltpu.VMEM((SUPER, A_DK), F32),
               pltpu.VMEM((SUPER, A_DK), F32), pltpu.VMEM((SUPER, 128), F32), pltpu.VMEM((SUPER, A_DV), F32)]
    return pl.pallas_call(
        _gdn_bwd_body, name="gdn_bwd", grid=(n_seq, A_HEADS),
        in_specs=_gdn_in_specs(s_len, s_len // SUPER, False) + [
            pl.BlockSpec((s_len, CHUNK), lambda b, h: (b * A_HEADS + h, 0)),
            pl.BlockSpec((None, n_chunks, A_DK, A_DV), lambda b, h: (b * A_HEADS + h, 0, 0, 0)),
            v_spec, v_spec,
            pl.BlockSpec((s_len, A_DV), lambda b, h: (b, A_Z_COL + h)),
            gain_spec,
            pl.BlockSpec((s_len, A_DK), lambda b, h: (b, h)),
            pl.BlockSpec((s_len, A_DK), lambda b, h: (b, A_K_COL + h)),
            pl.BlockSpec((s_len, A_DV), lambda b, h: (b, A_V_COL + h)),
            pl.BlockSpec((8, A_DK), lambda b, h: (0, h)),
            pl.BlockSpec((8, A_DK), lambda b, h: (0, A_K_COL + h)),
            pl.BlockSpec((8, A_DV), lambda b, h: (0, A_V_COL + h)),
        ],
        out_specs=[qk_spec, qk_spec, v_spec, gate_spec, v_spec, gain_spec, dw_qk_spec, dw_qk_spec, dw_v_spec],
        out_shape=[
            jax.ShapeDtypeStruct((t_rows, A_QK), BF16),
            jax.ShapeDtypeStruct((t_rows, A_QK), BF16),
            jax.ShapeDtypeStruct((t_rows, A_VW), BF16),
            jax.ShapeDtypeStruct((t_rows, 128), F32),
            jax.ShapeDtypeStruct((t_rows, A_VW), BF16),
            jax.ShapeDtypeStruct((8, A_DV), F32),
            jax.ShapeDtypeStruct((n_seq, 8, A_QK), F32),
            jax.ShapeDtypeStruct((n_seq, 8, A_QK), F32),
            jax.ShapeDtypeStruct((n_seq, 8, A_VW), F32),
        ],
        scratch_shapes=(_gdn_scan_scratch(SUPER)[:1]
                        + [pltpu.VMEM((8, A_DK), F32), pltpu.VMEM((8, A_DK), F32), pltpu.VMEM((8, A_DV), F32)]
                        + 2 * ops_set + 2 * res_set),
        compiler_params=_params(("arbitrary", "arbitrary")),
    )(q, k, v, gates_col, gates_row, tinv, states, d_og, o, proj_a, norm_g8,
      proj_a, proj_a, proj_a, conv_w8, conv_w8, conv_w8)


GATE_TILE = 512


def _softplus(y):
    return jnp.maximum(y, 0.0) + jnp.log1p(jnp.exp(-jnp.abs(y)))


def _gate_values(x, prm):
    beta = _sigmoid(x)
    y = x + prm[1:2, :]
    neg_a = -jnp.exp(prm[0:1, :])
    g = neg_a * _softplus(y)
    return beta, y, neg_a, g


def _a_gates_fwd_body(x_ref, prm_ref, gc_ref, gr_ref):
    x = x_ref[...]
    tm = x.shape[0]
    beta, _, _, g = _gate_values(x, prm_ref[...])
    in_chunk = lax.broadcasted_iota(jnp.int32, (tm, 1), 0) % CHUNK
    s = 1
    while s < CHUNK:
        g = g + jnp.where(in_chunk >= s, pltpu.roll(g, s, 0), 0.0)
        s *= 2
    lane = lax.broadcasted_iota(jnp.int32, x.shape, 1)
    out = jnp.where(lane < A_HEADS, beta, jnp.where(lane < 2 * A_HEADS, g, 0.0))
    gc_ref[...] = out
    gr_ref[...] = out.T[0:2 * A_HEADS, :]


def a_gates_fwd(proj_a, prm, n_seq):
    t_rows = proj_a.shape[0]
    s_len = t_rows // n_seq
    tm = min(GATE_TILE, s_len)
    n_t = s_len // tm
    return pl.pallas_call(
        _a_gates_fwd_body, name="a_gates_fwd", grid=(n_seq, n_t),
        in_specs=[pl.BlockSpec((tm, 128), lambda b, i: (b * n_t + i, A_GATE_COL)),
                  pl.BlockSpec((8, 128), lambda b, i: (0, 0))],
        out_specs=[pl.BlockSpec((tm, 128), lambda b, i: (b * n_t + i, 0)),
                   pl.BlockSpec((None, 2 * A_HEADS, tm), lambda b, i: (b, 0, i))],
        out_shape=[jax.ShapeDtypeStruct((t_rows, 128), F32),
                   jax.ShapeDtypeStruct((n_seq, 2 * A_HEADS, s_len), F32)],
        compiler_params=_params(("arbitrary", "arbitrary")),
    )(proj_a, prm)


def _a_gates_bwd_body(x_ref, prm_ref, dgc_ref, dx_ref, dprm_ref):
    first = (pl.program_id(0) == 0) & (pl.program_id(1) == 0)
    x = x_ref[...]
    tm = x.shape[0]
    beta, y, neg_a, g = _gate_values(x, prm_ref[...])
    d = dgc_ref[...]
    in_chunk = lax.broadcasted_iota(jnp.int32, (tm, 1), 0) % CHUNK
    dg = d
    s = 1
    while s < CHUNK:
        dg = dg + jnp.where(in_chunk < CHUNK - s, pltpu.roll(dg, tm - s, 0), 0.0)
        s *= 2
    lane = lax.broadcasted_iota(jnp.int32, x.shape, 1)
    is_decay = (lane >= A_HEADS) & (lane < 2 * A_HEADS)
    d_alogit = jnp.where(is_decay, dg * neg_a * _sigmoid(y), 0.0)
    dx_ref[...] = jnp.where(lane < A_HEADS, d * beta * (1.0 - beta), d_alogit).astype(dx_ref.dtype)

    @pl.when(first)
    def _():
        dprm_ref[...] = jnp.zeros_like(dprm_ref)

    dprm_ref[0:1, :] += jnp.sum(jnp.where(is_decay, dg * g, 0.0), axis=0, keepdims=True)
    dprm_ref[1:2, :] += jnp.sum(d_alogit, axis=0, keepdims=True)


def a_gates_bwd(proj_a, prm, dgates_col, n_seq):
    t_rows = proj_a.shape[0]
    s_len = t_rows // n_seq
    tm = min(GATE_TILE, s_len)
    n_t = s_len // tm
    return pl.pallas_call(
        _a_gates_bwd_body, name="a_gates_bwd", grid=(n_seq, n_t),
        in_specs=[pl.BlockSpec((tm, 128), lambda b, i: (b * n_t + i, A_GATE_COL)),
                  pl.BlockSpec((8, 128), lambda b, i: (0, 0)),
                  pl.BlockSpec((tm, 128), lambda b, i: (b * n_t + i, 0))],
        out_specs=[pl.BlockSpec((tm, 128), lambda b, i: (b * n_t + i, 0)),
                   pl.BlockSpec((8, 128), lambda b, i: (0, 0))],
        out_shape=[jax.ShapeDtypeStruct((t_rows, 128), BF16),
                   jax.ShapeDtypeStruct((8, 128), F32)],
        compiler_params=_params(("arbitrary", "arbitrary")),
    )(proj_a, prm, dgates_col)


ROW_TILE = 512
A_Z_COL = (2 * A_QK + A_VW) // A_DV


def _silu_parts(z):
    sig = _sigmoid(z)
    return z * sig, sig * (1.0 + z * (1.0 - sig))


NEG_BIG = -1e30
ATT_SCALE = B_DH ** -0.5


def _swap_rope_halves(x):
    src = lax.broadcasted_iota(jnp.int32, (B_DH, B_DH), 0)
    dst = lax.broadcasted_iota(jnp.int32, (B_DH, B_DH), 1)
    pick = ((dst < ROPE_HALF) & (src == dst + ROPE_HALF)) | (
        (dst >= ROPE_HALF) & (dst < ROPE_DIMS) & (src == dst - ROPE_HALF))
    return jnp.dot(_bf(x), pick.astype(BF16), preferred_element_type=F32)


def _norm_rope(x, gain, cos_t, sin_t):
    r = lax.rsqrt(jnp.mean(x * x, axis=1, keepdims=True) + EPS)
    xn = x * r * gain
    return xn * cos_t + _swap_rope_halves(xn) * sin_t, r


def _norm_rope_bwd(x, r, gain, cos_t, sin_t, dy):
    d_xn = dy * cos_t + _swap_rope_halves(dy * sin_t)
    xr = x * r
    u = d_xn * gain
    dx = r * u - xr * (r * r) * jnp.mean(u * x, axis=1, keepdims=True)
    return dx, jnp.sum(d_xn * xr, axis=0, keepdims=True)


def _stream_rows(idx, dilation, s_len):
    nb = s_len // dilation // B_BLOCK
    r = idx // nb
    m = idx % nb
    cur = r + m * (B_BLOCK * dilation)
    prev = r + jnp.maximum(m - 1, 0) * (B_BLOCK * dilation)
    return cur, prev, m > 0


def _rows(start, dilation):
    if dilation == 1:
        return pl.ds(start, B_BLOCK)
    return pl.ds(start, B_BLOCK, stride=dilation)


ATT_UNROLL = 16


def _band_mask(has_prev):
    qi = lax.broadcasted_iota(jnp.int32, (B_BLOCK, 2 * B_BLOCK), 0)
    kj = lax.broadcasted_iota(jnp.int32, (B_BLOCK, 2 * B_BLOCK), 1)
    return ((kj < B_BLOCK) & (kj >= qi) & has_prev) | ((kj >= B_BLOCK) & (kj - B_BLOCK <= qi))


def _attn_fwd_body(qkv_ref, z_ref, cos_ref, sin_ref, gain_ref, og_ref, o_ref, lse_ref,
                   qn_scr, kn_scr, og_scr, lg_scr):
    head, grp = pl.program_id(1), pl.program_id(2)
    s_len = z_ref.shape[0]
    n_blocks = s_len // B_BLOCK
    cos_t, sin_t = cos_ref[...], sin_ref[...]

    for gi, dil in enumerate(B_DILATIONS):
        @pl.when(grp == gi)
        def _(gi=gi, dil=dil):
            qn_scr[...], _ = _norm_rope(qkv_ref[0], gain_ref[gi:gi + 1, :], cos_t, sin_t)
            kn_scr[...], _ = _norm_rope(qkv_ref[1], gain_ref[B_GROUPS + gi:B_GROUPS + gi + 1, :], cos_t, sin_t)

            ones = jnp.ones((2 * B_BLOCK, B_DH), BF16)

            def blocks(it, carry):
                scored = []
                for j in range(ATT_UNROLL):
                    cur, prev, has_prev = _stream_rows(it * ATT_UNROLL + j, dil, s_len)
                    rc, rp = _rows(cur, dil), _rows(prev, dil)
                    k2 = jnp.concatenate([kn_scr[rp, :], kn_scr[rc, :]], axis=0)
                    scored.append((rc, rp, has_prev, _mm_nt(qn_scr[rc, :], k2) * ATT_SCALE))
                summed = []
                for rc, rp, has_prev, s in scored:
                    s = jnp.where(_band_mask(has_prev), s, NEG_BIG)
                    mx = jnp.max(s, axis=1, keepdims=True)
                    v2 = jnp.concatenate([qkv_ref.at[2][rp, :], qkv_ref.at[2][rc, :]], axis=0)
                    acc = jnp.dot(_bf(jnp.exp(s - mx)), jnp.concatenate([_bf(v2), ones], axis=1),
                                  preferred_element_type=F32)
                    summed.append((rc, mx, acc))
                for rc, mx, acc in summed:
                    den = acc[:, B_DH:B_DH + 1]
                    og_scr.at[gi][rc, :] = acc[:, :B_DH] / den
                    lg_scr.at[gi][rc, :] = jnp.broadcast_to(mx + jnp.log(den), (B_BLOCK, B_DH))
                return carry

            lax.fori_loop(0, n_blocks // ATT_UNROLL, blocks, 0)

    @pl.when(grp == B_GROUPS - 1)
    def _():
        l0, l1, l2 = lg_scr[0], lg_scr[1], lg_scr[2]
        mx = jnp.maximum(jnp.maximum(l0, l1), l2)
        w0, w1, w2 = jnp.exp(l0 - mx), jnp.exp(l1 - mx), jnp.exp(l2 - mx)
        den = w0 + w1 + w2
        o = (w0 * og_scr[0] + w1 * og_scr[1] + w2 * og_scr[2]) / den
        silu, _ = _silu_parts(z_ref[...])
        o_ref[...] = o
        og_ref[...] = (o * silu).astype(og_ref.dtype)
        @pl.when(head == 0)
        def _():
            lse_ref[...] = jnp.zeros_like(lse_ref)

        lane = lax.broadcasted_iota(jnp.int32, o.shape, 1)
        lse_ref[...] = jnp.where(lane == head, mx + jnp.log(den), lse_ref[...])


def attn_fwd(proj_b, cos_t, sin_t, gains8, n_seq):
    t_rows = proj_b.shape[1]
    s_len = t_rows // n_seq
    head_blk = pl.BlockSpec((s_len, B_DH), lambda b, h, g: (b, h))
    seq_blk = pl.BlockSpec((s_len, 128), lambda b, h, g: (b, 0))
    return pl.pallas_call(
        _attn_fwd_body, name="attn_fwd", grid=(n_seq, B_HEADS, B_GROUPS),
        in_specs=[
            pl.BlockSpec((3, s_len, B_DH), lambda b, h, g: (g, b, h)),
            pl.BlockSpec((None, s_len, B_DH), lambda b, h, g: (B_PIECES - 1, b, h)),
            seq_blk, seq_blk,
            pl.BlockSpec((8, 128), lambda b, h, g: (0, 0)),
        ],
        out_specs=[head_blk, head_blk, seq_blk],
        out_shape=[jax.ShapeDtypeStruct((t_rows, B_W), BF16),
                   jax.ShapeDtypeStruct((t_rows, B_W), F32),
                   jax.ShapeDtypeStruct((t_rows, 128), F32)],
        scratch_shapes=[pltpu.VMEM((s_len, B_DH), F32), pltpu.VMEM((s_len, B_DH), F32),
                        pltpu.VMEM((B_GROUPS, s_len, B_DH), F32), pltpu.VMEM((B_GROUPS, s_len, B_DH), F32)],
        compiler_params=_params(("arbitrary", "arbitrary", "arbitrary")),
    )(proj_b, proj_b, cos_t, sin_t, gains8)


def _attn_bwd_body(qkv_ref, z_ref, cos_ref, sin_ref, gain_ref, dog_ref, o_ref, lse_ref,
                   dqkv_ref, dz_ref, dgain_ref,
                   qn_scr, kn_scr, dqn_scr, dkn_scr, do_scr, dl_scr, ls_scr, dv_scr):
    head, grp = pl.program_id(1), pl.program_id(2)
    first = (pl.program_id(0) == 0) & (head == 0) & (grp == 0)
    s_len = z_ref.shape[0]
    n_blocks = s_len // B_BLOCK
    cos_t, sin_t = cos_ref[...], sin_ref[...]

    @pl.when(first)
    def _():
        dgain_ref[...] = jnp.zeros_like(dgain_ref)

    @pl.when(grp == 0)
    def _():
        d_og, o = dog_ref[...], o_ref[...]
        silu, dsilu = _silu_parts(z_ref[...])
        d_o = d_og * silu
        dz_ref[...] = (d_og * o * dsilu).astype(dz_ref.dtype)
        do_scr[...] = d_o
        dl_scr[...] = jnp.broadcast_to(jnp.sum(d_o * o, axis=1, keepdims=True), o.shape)
        lane = lax.broadcasted_iota(jnp.int32, o.shape, 1)
        ls_scr[...] = jnp.broadcast_to(
            jnp.sum(jnp.where(lane == head, lse_ref[...], 0.0), axis=1, keepdims=True), o.shape)

    for gi, dil in enumerate(B_DILATIONS):
        @pl.when(grp == gi)
        def _(gi=gi, dil=dil):
            q_raw, k_raw = qkv_ref[0], qkv_ref[1]
            gq = gain_ref[gi:gi + 1, :]
            gk = gain_ref[B_GROUPS + gi:B_GROUPS + gi + 1, :]
            qn_scr[...], rq = _norm_rope(q_raw, gq, cos_t, sin_t)
            kn_scr[...], rk = _norm_rope(k_raw, gk, cos_t, sin_t)
            def blocks(it, carry):
                scored = []
                for j in range(ATT_UNROLL):
                    cur, prev, has_prev = _stream_rows(it * ATT_UNROLL + j, dil, s_len)
                    rc, rp = _rows(cur, dil), _rows(prev, dil)
                    qb, d_ob = _bf(qn_scr[rc, :]), _bf(do_scr[rc, :])
                    k2 = _bf(jnp.concatenate([kn_scr[rp, :], kn_scr[rc, :]], axis=0))
                    v2 = _bf(jnp.concatenate([qkv_ref.at[2][rp, :], qkv_ref.at[2][rc, :]], axis=0))
                    scored.append((rc, rp, has_prev, qb, d_ob, k2,
                                   _mm_nt(qb, k2) * ATT_SCALE, _mm_nt(d_ob, v2)))
                grads = []
                for rc, rp, has_prev, qb, d_ob, k2, s, d_p in scored:
                    p = jnp.exp(jnp.where(_band_mask(has_prev), s - ls_scr[rc, :][:, 0:1], NEG_BIG))
                    ds = _bf(p * (d_p - dl_scr[rc, :][:, 0:1]))
                    grads.append((rc, rp, has_prev,
                                  _mm(ds, k2) * ATT_SCALE, _mm_tn(ds, qb) * ATT_SCALE, _mm_tn(_bf(p), d_ob)))
                for j, (rc, rp, has_prev, dq, dk2, dv2) in enumerate(grads):
                    dqn_scr[rc, :] = dq
                    if j == 0:
                        @pl.when(has_prev)
                        def _():
                            dkn_scr[rp, :] += dk2[:B_BLOCK]
                            dv_scr[rp, :] += dv2[:B_BLOCK]
                    if j + 1 < ATT_UNROLL:
                        dkn_scr[rc, :] = dk2[B_BLOCK:] + grads[j + 1][4][:B_BLOCK]
                        dv_scr[rc, :] = dv2[B_BLOCK:] + grads[j + 1][5][:B_BLOCK]
                    else:
                        dkn_scr[rc, :] = dk2[B_BLOCK:]
                        dv_scr[rc, :] = dv2[B_BLOCK:]
                return carry

            lax.fori_loop(0, n_blocks // ATT_UNROLL, blocks, 0)
            dq, dgq = _norm_rope_bwd(q_raw, rq, gq, cos_t, sin_t, dqn_scr[...])
            dk, dgk = _norm_rope_bwd(k_raw, rk, gk, cos_t, sin_t, dkn_scr[...])
            dqkv_ref[0] = dq.astype(dqkv_ref.dtype)
            dqkv_ref[1] = dk.astype(dqkv_ref.dtype)
            dqkv_ref[2] = dv_scr[...].astype(dqkv_ref.dtype)
            dgain_ref[gi:gi + 1, :] += dgq
            dgain_ref[B_GROUPS + gi:B_GROUPS + gi + 1, :] += dgk


def attn_bwd(proj_b, cos_t, sin_t, gains8, d_og, o, lse, n_seq):
    t_rows = proj_b.shape[1]
    s_len = t_rows // n_seq
    head_blk = pl.BlockSpec((s_len, B_DH), lambda b, h, g: (b, h))
    seq_blk = pl.BlockSpec((s_len, 128), lambda b, h, g: (b, 0))
    grp_blk = pl.BlockSpec((3, s_len, B_DH), lambda b, h, g: (g, b, h))
    gain_blk = pl.BlockSpec((8, 128), lambda b, h, g: (0, 0))
    return pl.pallas_call(
        _attn_bwd_body, name="attn_bwd", grid=(n_seq, B_HEADS, B_GROUPS),
        in_specs=[
            grp_blk,
            pl.BlockSpec((None, s_len, B_DH), lambda b, h, g: (B_PIECES - 1, b, h)),
            seq_blk, seq_blk, gain_blk, head_blk, head_blk, seq_blk,
        ],
        out_specs=[grp_blk, head_blk, gain_blk],
        out_shape=[jax.ShapeDtypeStruct((3 * B_GROUPS, t_rows, B_W), BF16),
                   jax.ShapeDtypeStruct((t_rows, B_W), BF16),
                   jax.ShapeDtypeStruct((8, 128), F32)],
        scratch_shapes=[pltpu.VMEM((s_len, B_DH), F32) for _ in range(8)],
        compiler_params=_params(("arbitrary", "arbitrary", "arbitrary")),
    )(proj_b, proj_b, cos_t, sin_t, gains8, d_og, o, lse)


def rope_tables(positions):
    inv_freq = ROPE_THETA ** (-jnp.arange(0, ROPE_DIMS, 2, dtype=F32) / ROPE_DIMS)
    ang = positions.astype(F32)[:, None] * inv_freq
    cos, sin = jnp.cos(ang), jnp.sin(ang)
    t_rows = positions.shape[0]
    rest = B_DH - ROPE_DIMS
    cos_t = jnp.concatenate([cos, cos, jnp.ones((t_rows, rest), F32)], axis=1)
    sin_t = jnp.concatenate([-sin, sin, jnp.zeros((t_rows, rest), F32)], axis=1)
    return cos_t, sin_t


def _rms_fwd_body(x_ref, g_ref, *rest, layer):
    h_ref, ht_ref = rest[-2:]
    x = x_ref[...]
    r = lax.rsqrt(jnp.mean(x * x, axis=1, keepdims=True) + EPS)
    h = x * r * g_ref[layer:layer + 1, :]
    h_ref[...] = h.astype(h_ref.dtype)
    ht_ref[...] = h.T.astype(ht_ref.dtype)


def rms_fwd(x, gains8, layer, after=None):
    t_rows, d = x.shape
    tm = min(ROW_TILE, t_rows)
    in_specs = [pl.BlockSpec((tm, d), lambda i: (i, 0)), pl.BlockSpec((8, d), lambda i: (0, 0))]
    args = [x, gains8]
    if after is not None:
        in_specs.append(HBM_SPEC)
        args.append(after)
    return pl.pallas_call(
        functools.partial(_rms_fwd_body, layer=layer), name=f"rms_fwd_{layer}", grid=(t_rows // tm,),
        in_specs=in_specs,
        out_specs=[pl.BlockSpec((tm, d), lambda i: (i, 0)), pl.BlockSpec((d, tm), lambda i: (0, i))],
        out_shape=[jax.ShapeDtypeStruct((t_rows, d), BF16), jax.ShapeDtypeStruct((d, t_rows), BF16)],
        compiler_params=_params(("arbitrary",)),
    )(*args)


def _rms_bwd_body(x_ref, g_ref, dh_ref, res_ref, dx_ref, dg_ref, *, layer):
    x, dh = x_ref[...], dh_ref[...]
    r = lax.rsqrt(jnp.mean(x * x, axis=1, keepdims=True) + EPS)
    xr = x * r
    u = dh * g_ref[layer:layer + 1, :]
    dx_ref[...] = res_ref[...] + r * u - xr * (r * r) * jnp.mean(u * x, axis=1, keepdims=True)

    @pl.when(pl.program_id(0) == 0)
    def _():
        dg_ref[...] = jnp.zeros_like(dg_ref)

    dg_ref[0:1, :] += jnp.sum(dh * xr, axis=0, keepdims=True)


def rms_bwd(x, gains8, layer, dh, d_res):
    t_rows, d = x.shape
    tm = min(ROW_TILE, t_rows)
    blk = pl.BlockSpec((tm, d), lambda i: (i, 0))
    gblk = pl.BlockSpec((8, d), lambda i: (0, 0))
    return pl.pallas_call(
        functools.partial(_rms_bwd_body, layer=layer), name=f"rms_bwd_{layer}", grid=(t_rows // tm,),
        in_specs=[blk, gblk, blk, blk],
        out_specs=[blk, gblk],
        out_shape=[jax.ShapeDtypeStruct((t_rows, d), F32), jax.ShapeDtypeStruct((8, d), F32)],
        compiler_params=_params(("arbitrary",)),
    )(x, gains8, dh, d_res)


def _piece_col(p):
    return jnp.where(p < 3 * B_GROUPS, (p % 3) * B_GROUPS + p // 3, 3 * B_GROUPS)


def _mm_nn_body(a_ref, w_ref, *rest, has_res):
    o_ref = rest[-1]
    acc = jnp.dot(a_ref[...], w_ref[...], preferred_element_type=F32)
    if has_res:
        acc = acc + rest[0][...]
    o_ref[...] = acc


def mm_nn(a, w, residual=None, *, tn, name):
    m, k = a.shape
    n = w.shape[1]
    tm = min(NT_ROW_TILE, m)
    in_specs = [pl.BlockSpec((tm, k), lambda j, i: (i, 0)), pl.BlockSpec((k, tn), lambda j, i: (0, j))]
    args = [a, w]
    if residual is not None:
        in_specs.append(pl.BlockSpec((tm, tn), lambda j, i: (i, j)))
        args.append(residual)
    return pl.pallas_call(
        functools.partial(_mm_nn_body, has_res=residual is not None), name=name, grid=(n // tn, m // tm),
        in_specs=in_specs,
        out_specs=pl.BlockSpec((tm, tn), lambda j, i: (i, j)),
        out_shape=jax.ShapeDtypeStruct((m, n), F32),
        compiler_params=_params(("arbitrary", "arbitrary")),
    )(*args)


def mm_nn_pieces(a, w, *, name):
    m, k = a.shape
    tm = min(NT_ROW_TILE, m)
    return pl.pallas_call(
        functools.partial(_mm_nn_body, has_res=False), name=name, grid=(B_PIECES, m // tm),
        in_specs=[pl.BlockSpec((tm, k), lambda p, i: (i, 0)),
                  pl.BlockSpec((k, B_W), lambda p, i: (0, _piece_col(p)))],
        out_specs=pl.BlockSpec((None, tm, B_W), lambda p, i: (p, i, 0)),
        out_shape=jax.ShapeDtypeStruct((B_PIECES, m, B_W), F32),
        compiler_params=_params(("arbitrary", "arbitrary")),
    )(a, w)


NT_ROW_TILE = 1024


def _mm_nt_body(g_ref, w_ref, *rest, has_init):
    o_ref = rest[-1]
    j = pl.program_id(1)
    part = lax.dot_general(_bf(g_ref[...]), w_ref[...], (((1,), (1,)), ((), ())), preferred_element_type=F32)

    @pl.when(j == 0)
    def _():
        o_ref[...] = part + rest[0][...] if has_init else part

    @pl.when(j > 0)
    def _():
        o_ref[...] += part


def mm_nt(g, w, init=None, *, tn, col_off=0, name, after=None):
    m, n = g.shape
    k = w.shape[0]
    tm = min(NT_ROW_TILE, m)
    in_specs = [pl.BlockSpec((tm, tn), lambda i, j: (i, j)),
                pl.BlockSpec((k, tn), lambda i, j: (0, col_off + j))]
    args = [g, w]
    if init is not None:
        in_specs.append(pl.BlockSpec((tm, k), lambda i, j: (i, 0)))
        args.append(init)
    if after is not None:
        in_specs.append(HBM_SPEC)
        args.append(after)
    return pl.pallas_call(
        functools.partial(_mm_nt_body, has_init=init is not None), name=name, grid=(m // tm, n // tn),
        in_specs=in_specs,
        out_specs=pl.BlockSpec((tm, k), lambda i, j: (i, 0)),
        out_shape=jax.ShapeDtypeStruct((m, k), F32),
        compiler_params=_params(("arbitrary", "arbitrary")),
    )(*args)


def mm_nt_multi(gs, w, *, tn, name, after=None):
    m = gs[0].shape[0]
    k = w.shape[0]
    tm = min(NT_ROW_TILE, m)
    tiles = [g.shape[1] // tn for g in gs]
    starts = [sum(tiles[:i]) for i in range(len(gs))]

    def body(*refs):
        g_refs, w_ref, o_ref = refs[:len(gs)], refs[len(gs)], refs[-1]
        j = pl.program_id(1)
        for g_ref, lo, cnt in zip(g_refs, starts, tiles):
            @pl.when((j >= lo) & (j < lo + cnt))
            def _(g_ref=g_ref):
                part = lax.dot_general(_bf(g_ref[...]), w_ref[...], (((1,), (1,)), ((), ())),
                                       preferred_element_type=F32)

                @pl.when(j == 0)
                def _():
                    o_ref[...] = part

                @pl.when(j > 0)
                def _():
                    o_ref[...] += part

    def g_spec(lo, cnt):
        return pl.BlockSpec((tm, tn), lambda i, j: (i, jnp.clip(j - lo, 0, cnt - 1)))

    in_specs = [g_spec(lo, cnt) for lo, cnt in zip(starts, tiles)] + [pl.BlockSpec((k, tn), lambda i, j: (0, j))]
    args = list(gs) + [w]
    if after is not None:
        in_specs.append(HBM_SPEC)
        args.append(after)
    return pl.pallas_call(
        body, name=name, grid=(m // tm, sum(tiles)),
        in_specs=in_specs,
        out_specs=pl.BlockSpec((tm, k), lambda i, j: (i, 0)),
        out_shape=jax.ShapeDtypeStruct((m, k), F32),
        compiler_params=_params(("arbitrary", "arbitrary")),
    )(*args)


def mm_nt_pieces(g9, gz, w, *, name):
    n_p, m, _ = g9.shape
    k = w.shape[0]
    tm = min(NT_ROW_TILE, m)

    def body(g_ref, z_ref, w_ref, o_ref):
        p = pl.program_id(1)

        def accumulate(src):
            part = lax.dot_general(_bf(src[...]), w_ref[...], (((1,), (1,)), ((), ())), preferred_element_type=F32)

            @pl.when(p == 0)
            def _():
                o_ref[...] = part

            @pl.when(p > 0)
            def _():
                o_ref[...] += part

        @pl.when(p < n_p)
        def _():
            accumulate(g_ref)

        @pl.when(p == n_p)
        def _():
            accumulate(z_ref)

    return pl.pallas_call(
        body, name=name, grid=(m // tm, n_p + 1),
        in_specs=[pl.BlockSpec((None, tm, B_W), lambda i, p: (jnp.minimum(p, n_p - 1), i, 0)),
                  pl.BlockSpec((tm, B_W), lambda i, p: (i, 0)),
                  pl.BlockSpec((k, B_W), lambda i, p: (0, _piece_col(p)))],
        out_specs=pl.BlockSpec((tm, k), lambda i, p: (i, 0)),
        out_shape=jax.ShapeDtypeStruct((m, k), F32),
        compiler_params=_params(("arbitrary", "arbitrary")),
    )(g9, gz, w)


def _mm_tn_body(a_ref, g_ref, o_ref, *, a_is_transposed):
    lhs_dim = 1 if a_is_transposed else 0
    o_ref[...] = lax.dot_general(a_ref[...], _bf(g_ref[...]), (((lhs_dim,), (0,)), ((), ())),
                                 preferred_element_type=F32).astype(o_ref.dtype)


def mm_tn(a, g, *, tn, out_dtype, name, a_is_transposed=False):
    k = a.shape[0] if a_is_transposed else a.shape[1]
    m, n = g.shape
    return pl.pallas_call(
        functools.partial(_mm_tn_body, a_is_transposed=a_is_transposed), name=name, grid=(n // tn,),
        in_specs=[pl.BlockSpec(a.shape, lambda j: (0, 0)), pl.BlockSpec((m, tn), lambda j: (0, j))],
        out_specs=pl.BlockSpec((k, tn), lambda j: (0, j)),
        out_shape=jax.ShapeDtypeStruct((k, n), out_dtype),
        compiler_params=_params(("arbitrary",)),
    )(a, g)


B_UNIT = 256
B_IN_COLS = B_PIECES * B_W
B_SHARD_UNITS = B_IN_COLS // N_DEV // B_UNIT


def mm_tn_b_in(at, g9, gz, *, out_dtype, name):
    k, m = at.shape
    per_piece = B_W // B_UNIT
    body_one = functools.partial(_mm_tn_body, a_is_transposed=True)
    n_units = B_IN_COLS // B_UNIT

    def g_map(u):
        nat = jnp.minimum(u // per_piece, 3 * B_GROUPS - 1)
        piece = (nat % B_GROUPS) * 3 + nat // B_GROUPS
        return (piece, 0, u % per_piece)

    def body(a_ref, g_ref, z_ref, o_ref):
        u = pl.program_id(0)

        @pl.when(u < 3 * B_GROUPS * per_piece)
        def _():
            body_one(a_ref, g_ref, o_ref)

        @pl.when(u >= 3 * B_GROUPS * per_piece)
        def _():
            body_one(a_ref, z_ref, o_ref)

    return pl.pallas_call(
        body, name=name, grid=(n_units,),
        in_specs=[pl.BlockSpec((k, m), lambda u: (0, 0)),
                  pl.BlockSpec((None, m, B_UNIT), g_map),
                  pl.BlockSpec((m, B_UNIT), lambda u: (0, jnp.where(u < 3 * B_GROUPS * per_piece, 0, u % per_piece)))],
        out_specs=pl.BlockSpec((None, k, B_UNIT), lambda u: (u // B_SHARD_UNITS, 0, u % B_SHARD_UNITS)),
        out_shape=jax.ShapeDtypeStruct((N_DEV, k, B_IN_COLS // N_DEV), out_dtype),
        compiler_params=_params(("arbitrary",)),
    )(at, g9, gz)


def _loss_body(y_ref, t_ref, dy_ref, loss_ref, acc):
    i = pl.program_id(0)
    d = y_ref.shape[1]
    err = y_ref[...] - t_ref[...]
    dy_ref[...] = err * (1.0 / d)

    @pl.when(i == 0)
    def _():
        acc[...] = jnp.zeros_like(acc)

    acc[...] += jnp.sum(err * err, axis=0, keepdims=True)

    @pl.when(i == pl.num_programs(0) - 1)
    def _():
        total = jnp.sum(acc[...], axis=1, keepdims=True) * (0.5 / d)
        loss_ref[...] = jnp.broadcast_to(total, loss_ref.shape)


def loss_head(y, target):
    t_rows, d = y.shape
    tm = min(ROW_TILE, t_rows)
    blk = pl.BlockSpec((tm, d), lambda i: (i, 0))
    return pl.pallas_call(
        _loss_body, name="loss_head", grid=(t_rows // tm,),
        in_specs=[blk, blk],
        out_specs=[blk, pl.BlockSpec((8, 128), lambda i: (0, 0))],
        out_shape=[jax.ShapeDtypeStruct((t_rows, d), F32), jax.ShapeDtypeStruct((8, 128), F32)],
        scratch_shapes=[pltpu.VMEM((1, d), F32)],
        compiler_params=_params(("arbitrary",)),
    )(y, target)


def _adamw_body(p_ref, w_ref, m_ref, v_ref, g_ref, d_ref, nm_ref, nv_ref):
    g = p_ref[0].astype(F32)
    for s in range(1, N_DEV):
        g = g + p_ref[s].astype(F32)
    w = w_ref[...]
    m = ADAM_B1 * m_ref[...] + (1.0 - ADAM_B1) * g
    v = ADAM_B2 * v_ref[...] + (1.0 - ADAM_B2) * (g * g)
    m_hat = m / (1.0 - ADAM_B1 ** ADAM_STEP)
    v_hat = v / (1.0 - ADAM_B2 ** ADAM_STEP)
    g_ref[...] = g
    d_ref[...] = -ADAM_LR * (m_hat / (jnp.sqrt(v_hat) + ADAM_EPS) + ADAM_WD * w)
    nm_ref[...] = m
    nv_ref[...] = v


def adamw(parts, w, m, v, *, name):
    _, r, c = w.shape
    tr = r if r <= 256 else 256
    blk = pl.BlockSpec((None, tr, c), lambda i: (0, i, 0))
    out = jax.ShapeDtypeStruct((1, r, c), F32)
    return pl.pallas_call(
        _adamw_body, name=name, grid=(r // tr,),
        in_specs=[pl.BlockSpec((N_DEV, tr, c), lambda i: (0, i, 0)), blk, blk, blk],
        out_specs=[blk, blk, blk, blk],
        out_shape=[out, out, out, out],
        compiler_params=_params(("arbitrary",)),
    )(parts, w, m, v)


MESH_ID = pl.DeviceIdType.MESH
HBM_SPEC = pl.BlockSpec(memory_space=pl.ANY)


def _my_place():
    return lax.axis_index("x"), lax.axis_index("y"), lax.axis_index("c")


def _flat(x, y, c):
    return 4 * x + 2 * y + c


def _all_gather_body(*refs, n):
    ins, outs = refs[:n], refs[n:2 * n]
    send_sems, recv_sems, local_sems = refs[2 * n:]
    x, y, c = _my_place()
    me, sibling = (x, y, c), (x, y, 1 - c)
    chips = [(1 - x, y), (x, 1 - y), (1 - x, 1 - y)]
    pending = []
    for a in range(n):
        src, out = ins[a], outs[a]

        def copy(k, block, to, from_input=False, a=a, src=src, out=out):
            slot = out.at[_flat(*block)]
            return pltpu.make_async_remote_copy(
                src_ref=src if from_input else slot, dst_ref=slot,
                send_sem=send_sems.at[7 * a + k], recv_sem=recv_sems.at[7 * a + k],
                device_id=to, device_id_type=MESH_ID)

        mine = pltpu.make_async_copy(src, out.at[_flat(*me)], local_sems.at[a])
        mine.start()
        first = [copy(0, me, sibling, True)] + [copy(1 + j, me, (*chip, c), True) for j, chip in enumerate(chips)]
        for cp in first:
            cp.start()
        pending.append((copy, mine, first))
    for copy, mine, first in pending:
        passed = [copy(4 + j, (*chip, c), sibling) for j, chip in enumerate(chips)]
        for j, chip in enumerate(chips):
            copy(1 + j, (*chip, c), me).wait_recv()
            passed[j].start()
        copy(0, sibling, me).wait_recv()
        for j, chip in enumerate(chips):
            copy(4 + j, (*chip, 1 - c), me).wait_recv()
        for cp in first + passed:
            cp.wait_send()
        mine.wait()


def all_gather(shards, *, name):
    n = len(shards)
    return pl.pallas_call(
        functools.partial(_all_gather_body, n=n), name=name,
        in_specs=[HBM_SPEC] * n, out_specs=[HBM_SPEC] * n,
        out_shape=[jax.ShapeDtypeStruct((N_DEV,) + s.shape, s.dtype) for s in shards],
        scratch_shapes=[pltpu.SemaphoreType.DMA((7 * n,)), pltpu.SemaphoreType.DMA((7 * n,)),
                        pltpu.SemaphoreType.DMA((n,))],
    )(*shards)


PEER_FLIPS = [(0, 0, 1), (1, 0, 0), (0, 1, 0), (1, 1, 0), (1, 0, 1), (0, 1, 1), (1, 1, 1)]


def _all_to_all_body(*refs, n):
    ins, outs = refs[:n], refs[n:2 * n]
    send_sems, recv_sems, local_sems = refs[2 * n:]
    x, y, c = _my_place()
    me = _flat(x, y, c)
    waits = []
    for a in range(n):
        src, out = ins[a], outs[a]
        mine = pltpu.make_async_copy(src.at[me], out.at[me], local_sems.at[a])
        mine.start()
        waits.append(mine)
        for k, (fx, fy, fc) in enumerate(PEER_FLIPS):
            peer = (1 - x if fx else x, 1 - y if fy else y, 1 - c if fc else c)
            theirs = _flat(*peer)
            sems = dict(send_sem=send_sems.at[7 * a + k], recv_sem=recv_sems.at[7 * a + k],
                        device_id=peer, device_id_type=MESH_ID)
            send = pltpu.make_async_remote_copy(src_ref=src.at[theirs], dst_ref=out.at[me], **sems)
            send.start()
            recv = pltpu.make_async_remote_copy(src_ref=src.at[theirs], dst_ref=out.at[theirs], **sems)
            waits.append((send, recv))
    for w in waits:
        if isinstance(w, tuple):
            w[0].wait_send()
            w[1].wait_recv()
        else:
            w.wait()


def all_to_all(parts, *, name):
    n = len(parts)
    return pl.pallas_call(
        functools.partial(_all_to_all_body, n=n), name=name,
        in_specs=[HBM_SPEC] * n, out_specs=[HBM_SPEC] * n,
        out_shape=[jax.ShapeDtypeStruct(p.shape, p.dtype) for p in parts],
        scratch_shapes=[pltpu.SemaphoreType.DMA((7 * n,)), pltpu.SemaphoreType.DMA((7 * n,)),
                        pltpu.SemaphoreType.DMA((n,))],
    )(*parts)


HBM_ONLY = pl.BlockSpec(memory_space=pltpu.HBM)
SEM_SPEC = pl.BlockSpec(memory_space=pltpu.SEMAPHORE)
DATAFLOW_EFFECT = pltpu.SideEffectType.DATAFLOW_SIDE_EFFECTING


def _split_copies(srcs, lands, send_sems, recv_sems, n, scatter):
    x, y, c = _my_place()
    me = _flat(x, y, c)
    pairs = []
    for a in range(n):
        for k, (fx, fy, fc) in enumerate(PEER_FLIPS):
            peer = (1 - x if fx else x, 1 - y if fy else y, 1 - c if fc else c)
            theirs = _flat(*peer)
            src = srcs[a].at[theirs] if scatter else srcs[a]
            sems = dict(send_sem=send_sems.at[7 * a + k], recv_sem=recv_sems.at[7 * a + k],
                        device_id=peer, device_id_type=MESH_ID)
            pairs.append((pltpu.make_async_remote_copy(src_ref=src, dst_ref=lands[a].at[me], **sems),
                          pltpu.make_async_remote_copy(src_ref=src, dst_ref=lands[a].at[theirs], **sems)))
    return pairs


def _exchange_start_body(*refs, n, scatter):
    srcs, lands = refs[:n], refs[n:2 * n]
    send_sems, recv_sems = refs[2 * n], refs[2 * n + 1]
    token = refs[-1]
    for send, _ in _split_copies(srcs, lands, send_sems, recv_sems, n, scatter):
        send.start()
    token[...] = jnp.zeros_like(token)


def exchange_start(srcs, lands, *, scatter, name):
    n = len(srcs)
    args = [pltpu.with_memory_space_constraint(t, pltpu.HBM) for t in list(srcs) + list(lands)]
    outs = pl.pallas_call(
        functools.partial(_exchange_start_body, n=n, scatter=scatter), name=name,
        out_shape=(pltpu.SemaphoreType.DMA((7 * n,)), pltpu.SemaphoreType.DMA((7 * n,)),
                   *[pltpu.HBM(t.shape, t.dtype) for t in args],
                   jax.ShapeDtypeStruct((8, 128), F32)),
        in_specs=[HBM_ONLY] * (2 * n),
        out_specs=(SEM_SPEC, SEM_SPEC, *[HBM_ONLY] * (2 * n), pl.BlockSpec(memory_space=pltpu.VMEM)),
        input_output_aliases={i: 2 + i for i in range(2 * n)},
        compiler_params=pltpu.CompilerParams(has_side_effects=DATAFLOW_EFFECT),
    )(*args)
    return outs[0], outs[1], outs[2:2 + n], outs[2 + n:2 + 2 * n], outs[-1]


def _exchange_wait_body(*refs, n, scatter):
    srcs, lands = refs[:n], refs[n:2 * n]
    send_sems, recv_sems = refs[2 * n], refs[2 * n + 1]
    for send, recv in _split_copies(srcs, lands, send_sems, recv_sems, n, scatter):
        send.wait_send()
        recv.wait_recv()


def exchange_wait(send_sems, recv_sems, srcs, lands, after, *, scatter, name):
    n = len(srcs)
    outs = pl.pallas_call(
        functools.partial(_exchange_wait_body, n=n, scatter=scatter), name=name,
        out_shape=tuple(pltpu.HBM(t.shape, t.dtype) for t in list(srcs) + list(lands)),
        in_specs=[HBM_ONLY] * (2 * n) + [SEM_SPEC, SEM_SPEC, HBM_SPEC],
        out_specs=tuple([HBM_ONLY] * (2 * n)),
        input_output_aliases={i: i for i in range(2 * n)},
        compiler_params=pltpu.CompilerParams(has_side_effects=DATAFLOW_EFFECT),
    )(*srcs, *lands, send_sems, recv_sems, after)
    return outs[n:]


def _own_slot_only(shape_dtype, own, slot):
    land = lax.empty(shape_dtype.shape, shape_dtype.dtype)
    return lax.dynamic_update_slice(land, own[None], (slot,) + (0,) * own.ndim)


def _pad_rows(a, rows=8):
    return jnp.pad(a, ((0, rows - a.shape[0]), (0, 0)))


def _gate_rows(a_log, dt_bias):
    z = jnp.zeros((8, 128), F32)
    return z.at[0, A_HEADS:2 * A_HEADS].set(a_log[0]).at[1, A_HEADS:2 * A_HEADS].set(dt_bias[0])


def _pack_small(norm_g, a_log, a_dt_bias, a_norm_g, b_q_norm_g, b_k_norm_g):
    return jnp.concatenate([
        norm_g[0].reshape(8, 128), norm_g[1].reshape(8, 128),
        _gate_rows(a_log, a_dt_bias),
        _pad_rows(a_norm_g[0].reshape(2, 128)),
        _pad_rows(jnp.concatenate([b_q_norm_g[0], b_k_norm_g[0]], axis=0)),
    ], axis=0)


def _unpack_small(p):
    return (p[0:16].reshape(2, D_MODEL), p[16:17, A_HEADS:2 * A_HEADS], p[17:18, A_HEADS:2 * A_HEADS],
            p[24:26].reshape(1, A_DV), p[32:35][None], p[35:38][None])


def kernel(x, positions, norm_g, a_w_in, a_conv_w, a_log, a_dt_bias, a_norm_g, a_w_out, b_w_in, b_q_norm_g, b_k_norm_g, b_w_out, loss_target, m_norm_g, m_a_w_in, m_a_conv_w, m_a_log, m_a_dt_bias, m_a_norm_g, m_a_w_out, m_b_w_in, m_b_q_norm_g, m_b_k_norm_g, m_b_w_out, v_norm_g, v_a_w_in, v_a_conv_w, v_a_log, v_a_dt_bias, v_a_norm_g, v_a_w_out, v_b_w_in, v_b_q_norm_g, v_b_k_norm_g, v_b_w_out):
    n_seq, s_len, d = x.shape
    t_rows = n_seq * s_len
    n_chunks = s_len // CHUNK
    x0 = x.reshape(t_rows, d)
    target = loss_target.reshape(t_rows, d)
    my_slot = _flat(*_my_place())

    g_a_in, g_conv = all_gather([a_w_in[0].astype(BF16), _pad_rows(a_conv_w[0])], name="gather_weights_first")
    later = [a_w_out[0].astype(BF16), b_w_in[0].astype(BF16), b_w_out[0].astype(BF16)]
    lands = [_own_slot_only(jax.ShapeDtypeStruct((N_DEV,) + t.shape, t.dtype), t, my_slot) for t in later]
    w_send, w_recv, later, lands, w_token = exchange_start(later, lands, scatter=False, name="gather_weights_start")
    w_a_in = jnp.pad(g_a_in.transpose(1, 0, 2).reshape(d, A_IN), ((0, 0), (0, A_IN_PAD - A_IN)))
    conv_w8 = g_conv.transpose(1, 0, 2).reshape(8, 2 * A_QK + A_VW)

    gains_model = _pad_rows(norm_g)
    gate_prm = _gate_rows(a_log, a_dt_bias)
    gain_a_out = _pad_rows(a_norm_g)
    gains_qk = _pad_rows(jnp.concatenate([b_q_norm_g[0], b_k_norm_g[0]], axis=0))
    cos_t, sin_t = rope_tables(positions.reshape(t_rows))

    h0, h0_t = rms_fwd(x0, gains_model, 0, after=w_token)
    proj_a = mm_nn(h0, w_a_in, tn=896, name="proj_a")
    gates_col, gates_row = a_gates_fwd(proj_a, gate_prm, n_seq)
    gates_row = gates_row.reshape(n_seq, 2 * A_HEADS, s_len // SUPER, 1, SUPER)
    o_a, tinv, states, og_a, q_a, k_a, v_a = gdn_fwd(proj_a, conv_w8, gates_col, gates_row, gain_a_out, n_seq)
    g_a_out, g_b_in, g_b_out = exchange_wait(w_send, w_recv, later, lands, og_a, scatter=False,
                                             name="gather_weights_wait")
    w_a_out = g_a_out.reshape(A_VW, d)
    w_b_in = g_b_in.transpose(1, 0, 2).reshape(d, B_IN_COLS)
    w_b_out = g_b_out.reshape(B_W, d)
    x1 = mm_nn(og_a, w_a_out, x0, tn=1024, name="out_a")

    h1, h1_t = rms_fwd(x1, gains_model, 1)
    proj_b = mm_nn_pieces(h1, w_b_in, name="proj_b")
    og_b, o_b, lse = attn_fwd(proj_b, cos_t, sin_t, gains_qk, n_seq)
    y = mm_nn(og_b, w_b_out, x1, tn=1024, name="out_b")

    dy, loss_blk = loss_head(y, target)
    loss = lax.psum(loss_blk[0, 0], ("x", "y", "c"))

    d_og_b = mm_nt(dy, w_b_out, tn=512, name="d_og_b")
    dw_b_out = mm_tn(og_b, dy, tn=256, out_dtype=BF16, name="dw_b_out")
    dqkv_b, dz_b, d_gains_qk = attn_bwd(proj_b, cos_t, sin_t, gains_qk, d_og_b, o_b, lse, n_seq)
    dh1 = mm_nt_pieces(dqkv_b, dz_b, w_b_in, name="dh1")
    dw_b_in = mm_tn_b_in(h1_t, dqkv_b, dz_b, out_dtype=BF16, name="dw_b_in")
    dx1, d_gain1 = rms_bwd(x1, gains_model, 1, dh1, dy)

    dw_a_out = mm_tn(og_a, dx1, tn=256, out_dtype=BF16, name="dw_a_out")
    early = [dw_b_in, dw_b_out.reshape(N_DEV, B_W // N_DEV, d), dw_a_out.reshape(N_DEV, A_VW // N_DEV, d)]
    lands = [_own_slot_only(t, lax.dynamic_index_in_dim(t, my_slot, 0, keepdims=False), my_slot) for t in early]
    g_send, g_recv, early, lands, g_token = exchange_start(early, lands, scatter=True, name="scatter_grads_start")

    d_og_a = mm_nt(dx1, w_a_out, tn=512, name="d_og_a", after=g_token)
    d_xq, d_xk, d_xv, dgates, dz_a, d_gain_a_out, d_cq, d_ck, d_cv = gdn_bwd(
        q_a, k_a, v_a, gates_col, gates_row, tinv, states, d_og_a, o_a, proj_a, conv_w8, gain_a_out, n_seq)
    d_conv = jnp.concatenate([d_cq.sum(axis=0), d_ck.sum(axis=0), d_cv.sum(axis=0)], axis=1)
    d_gate_logits, d_gate_prm = a_gates_bwd(proj_a, gate_prm, dgates, n_seq)
    dw_a_in = jnp.concatenate([
        mm_tn(h0_t, piece, tn=min(256, piece.shape[1]), out_dtype=BF16, name=f"dw_a_in_{nm}", a_is_transposed=True)
        for nm, piece in (("q", d_xq), ("k", d_xk), ("v", d_xv), ("z", dz_a), ("gates", d_gate_logits))
    ], axis=1)[:, :A_IN]
    shard_a_in = A_IN // N_DEV
    last = [dw_a_in.reshape(d, N_DEV, shard_a_in).transpose(1, 0, 2)]
    last_lands = [_own_slot_only(t, lax.dynamic_index_in_dim(t, my_slot, 0, keepdims=False), my_slot) for t in last]
    l_send, l_recv, last, last_lands, l_token = exchange_start(last, last_lands, scatter=True,
                                                               name="scatter_last_start")
    dh0 = mm_nt_multi([d_xq, d_xk, d_xv, dz_a], w_a_in, tn=512, name="dh0_qkvz", after=l_token)
    dh0 = mm_nt(d_gate_logits, w_a_in, dh0, tn=128, col_off=A_GATE_COL, name="dh0_gates")
    dx0, d_gain0 = rms_bwd(x0, gains_model, 0, dh0, dx1)

    small = jnp.concatenate([
        d_gain0[0].reshape(8, 128), d_gain1[0].reshape(8, 128), d_gate_prm,
        _pad_rows(d_gain_a_out[0].reshape(2, 128)), d_gains_qk], axis=0)
    r_small, r_conv = all_gather([small, d_conv], name="gather_small_grads")
    conv_cols = a_conv_w.shape[2]
    r_conv = lax.dynamic_slice(r_conv, (0, 0, my_slot * conv_cols), (N_DEV, 8, conv_cols))

    r_b_in, r_b_out, r_a_out = exchange_wait(g_send, g_recv, early, lands, r_small, scatter=True,
                                             name="scatter_grads_wait")
    (r_a_in,) = exchange_wait(l_send, l_recv, last, last_lands, r_small, scatter=True, name="scatter_last_wait")

    upd = {}
    upd["a_w_in"] = adamw(r_a_in, a_w_in, m_a_w_in, v_a_w_in, name="adamw_a_w_in")
    upd["a_w_out"] = adamw(r_a_out, a_w_out, m_a_w_out, v_a_w_out, name="adamw_a_w_out")
    upd["b_w_in"] = adamw(r_b_in, b_w_in, m_b_w_in, v_b_w_in, name="adamw_b_w_in")
    upd["b_w_out"] = adamw(r_b_out, b_w_out, m_b_w_out, v_b_w_out, name="adamw_b_w_out")
    upd["a_conv_w"] = [t[:, :A_CONV] for t in adamw(
        r_conv, _pad_rows(a_conv_w[0])[None], _pad_rows(m_a_conv_w[0])[None], _pad_rows(v_a_conv_w[0])[None],
        name="adamw_a_conv_w")]
    small_upd = adamw(
        r_small,
        _pack_small(norm_g, a_log, a_dt_bias, a_norm_g, b_q_norm_g, b_k_norm_g)[None],
        _pack_small(m_norm_g, m_a_log, m_a_dt_bias, m_a_norm_g, m_b_q_norm_g, m_b_k_norm_g)[None],
        _pack_small(v_norm_g, v_a_log, v_a_dt_bias, v_a_norm_g, v_b_q_norm_g, v_b_k_norm_g)[None],
        name="adamw_small")
    small_names = ("norm_g", "a_log", "a_dt_bias", "a_norm_g", "b_q_norm_g", "b_k_norm_g")
    unpacked = [_unpack_small(t[0]) for t in small_upd]
    for i, nm in enumerate(small_names):
        upd[nm] = [u[i] for u in unpacked]

    order = ("norm_g", "a_w_in", "a_conv_w", "a_log", "a_dt_bias", "a_norm_g", "a_w_out",
             "b_w_in", "b_q_norm_g", "b_k_norm_g", "b_w_out")
    outs = [loss, dx0.reshape(n_seq, s_len, d)]
    for kind in range(4):
        for nm in order:
            outs.append(upd[nm][kind])
    return tuple(outs)
```

```python
import functools
import math

import jax
import jax.numpy as jnp
from jax import lax
from jax.experimental import pallas as pl
from jax.experimental.pallas import tpu as pltpu

F32 = jnp.float32
BF16 = jnp.bfloat16

D_MODEL = 1024
EPS = 1e-6
N_DEV = 8

A_HEADS = 8
A_DK = 128
A_DV = 256
A_QK = A_HEADS * A_DK
A_VW = A_HEADS * A_DV
A_CONV = 4
CHUNK = 64
A_IN = 2 * A_QK + 2 * A_VW + 2 * A_HEADS
A_IN_PAD = 2 * A_QK + 2 * A_VW + 128
A_GATE_COL = (2 * A_QK + 2 * A_VW) // 128

B_DILATIONS = (1, 4, 16)
B_GROUPS = 3
B_HEADS = 8
B_DH = 128
B_W = B_HEADS * B_DH
B_BLOCK = 128
B_PIECES = 3 * B_GROUPS + 1
ROPE_THETA = 500000.0
ROPE_DIMS = B_DH // 4
ROPE_HALF = ROPE_DIMS // 2

ADAM_LR = 0.001
ADAM_B1 = 0.9
ADAM_B2 = 0.999
ADAM_EPS = 1e-08
ADAM_WD = 0.01
ADAM_STEP = 10

VMEM_LIMIT = 60 * 1024 * 1024


def _params(sem):
    return pltpu.CompilerParams(dimension_semantics=sem, vmem_limit_bytes=VMEM_LIMIT)


def _bf(x):
    return x.astype(BF16)


def _mm(a, b):
    return jnp.dot(_bf(a), _bf(b), preferred_element_type=F32)


def _mm_nt(a, b):
    return lax.dot_general(_bf(a), _bf(b), (((1,), (1,)), ((), ())), preferred_element_type=F32)


def _mm_tn(a, b):
    return lax.dot_general(_bf(a), _bf(b), (((0,), (0,)), ((), ())), preferred_element_type=F32)


def _split(x):
    hi = _bf(x)
    return hi, _bf(x - hi.astype(F32))


def _mm3(a, b):
    ah, al = _split(a)
    bh, bl = _split(b)
    d = functools.partial(jnp.dot, preferred_element_type=F32)
    return d(ah, bh) + (d(ah, bl) + d(al, bh))


def _colsum_as_col(z):
    zh, zl = _split(z)
    ones = jnp.ones((z.shape[0], 128), BF16)
    tn = functools.partial(lax.dot_general, dimension_numbers=(((0,), (0,)), ((), ())),
                           preferred_element_type=F32)
    return (tn(zh, ones) + tn(zl, ones))[:, 0:1]


def _sigmoid(x):
    return 0.5 * jnp.tanh(0.5 * x) + 0.5


INV_BASE = 8
INV_NEWTON = 2
GDN_GROUP = 4
SUPER = GDN_GROUP * CHUNK
GDN_WIDTH = 2

A_K_COL = A_QK // A_DK
A_V_COL = 2 * A_QK // A_DV


def _inverse_steps(m, row, col):
    eye = (row == col).astype(F32)
    d = jnp.where(row // INV_BASE == col // INV_BASE, m, 0.0)
    x = eye - d
    p = _mm(d, d)
    yield
    steps = int(math.log2(INV_BASE)) - 1
    for i in range(steps):
        x = x + _mm(x, p)
        if i + 1 < steps:
            p = _mm(p, p)
        yield
    size = INV_BASE
    while size < CHUNK:
        c = jnp.where((row // (2 * size) == col // (2 * size)) & (row // size != col // size), m, 0.0)
        xc = _mm(x, c)
        yield
        x = x - _mm(xc, x)
        yield
        size *= 2
    for _ in range(INV_NEWTON):
        r = eye - x - _mm3(m, x)
        yield
        x = x + _mm(x, r)
        yield
    return x


def _drain(gen):
    while True:
        try:
            next(gen)
        except StopIteration as stop:
            return stop.value


def _interleave(*gens):
    live = list(gens)
    while live:
        for g in list(live):
            try:
                next(g)
            except StopIteration:
                live.remove(g)


def _diag_blocks_tall(x):
    return jnp.concatenate([x[i * CHUNK:(i + 1) * CHUNK, i * CHUNK:(i + 1) * CHUNK] for i in range(GDN_GROUP)], axis=0)


def _tall_to_block_diag(t, same):
    return jnp.where(same, jnp.concatenate([t] * GDN_GROUP, axis=1), 0.0)


def _block_sum(same, x):
    xh, xl = _split(jnp.broadcast_to(x, (SUPER, 128)))
    ones = same.astype(BF16)
    d = functools.partial(jnp.dot, preferred_element_type=F32)
    return (d(ones, xh) + d(ones, xl))[:, 0:1]


def _aligned_rows(index, size):
    start = index * size
    return pl.ds(start if isinstance(start, int) else pl.multiple_of(start, size), size)


def _super_rows(i):
    return _aligned_rows(i, SUPER)


def _chunk_rows(n):
    return _aligned_rows(n, CHUNK)


def _gdn_super_steps(q, k, v, gcb, gr, head, tinv_tall=None):
    lane = lax.broadcasted_iota(jnp.int32, (SUPER, 128), 1)
    row = lax.broadcasted_iota(jnp.int32, (SUPER, SUPER), 0)
    col = lax.broadcasted_iota(jnp.int32, (SUPER, SUPER), 1)
    same = row // CHUNK == col // CHUNK
    beta = jnp.sum(jnp.where(lane == head, gcb, 0.0), axis=1, keepdims=True)
    gc = jnp.sum(jnp.where(lane == A_HEADS + head, gcb, 0.0), axis=1, keepdims=True)
    g_last = jnp.sum(jnp.where(col == (row // CHUNK) * CHUNK + (CHUNK - 1), gr, 0.0), axis=1, keepdims=True)
    gamma = jnp.exp(gc)
    decay = jnp.where(same & (row >= col), jnp.exp(jnp.minimum(gc - gr, 0.0)), 0.0)
    kb = k * beta
    m = jnp.where(same & (row > col), _mm_nt(kb, k) * decay, 0.0)
    p = jnp.where(same & (row >= col), _mm_nt(q, k) * decay, 0.0)
    yield
    if tinv_tall is None:
        tinv = yield from _inverse_steps(m, row, col)
    else:
        tinv = _tall_to_block_diag(tinv_tall, same)
    u = _mm(tinv, v * beta)
    w = _mm(tinv, kb * gamma)
    yield
    e_tail = jnp.exp(g_last - gc)
    return dict(beta=beta, gc=gc, g_last=g_last, gamma=gamma, decay=decay, kb=kb, m=m,
                tinv=tinv, u=u, w=w, p=p, e_tail=e_tail, row=row, col=col, lane=lane, same=same)


def _gdn_super_common(q, k, v, gcb, gr, head, tinv_tall=None):
    return _drain(_gdn_super_steps(q, k, v, gcb, gr, head, tinv_tall))


def _store_scan_operands(rows, q, k, t, u_scr, w_scr, p_scr, qg_scr, ke_scr, gl_scr):
    u_scr[rows, :] = t["u"]
    w_scr[rows, :] = _bf(t["w"])
    p_scr[rows, :] = _bf(_diag_blocks_tall(t["p"]))
    qg_scr[rows, :] = _bf(q * t["gamma"])
    ke_scr[rows, :] = _bf(k * t["e_tail"])
    gl_scr[rows, :] = jnp.broadcast_to(jnp.exp(t["g_last"]), (SUPER, 128))


def _gdn_fwd_body(q_ref, k_ref, v_ref, gc_ref, gr_ref, wq_ref, wk_ref, wv_ref, z_ref, gn_ref,
                  o_ref, tinv_ref, st_ref, og_ref, qo_ref, ko_ref, vo_ref, s_scr, *sets):
    head = pl.program_id(1)
    n_super = q_ref.shape[0] // SUPER
    all_sets = [sets[6 * i:6 * i + 6] for i in range(2 * GDN_WIDTH)]
    whole = pl.ds(0, SUPER)

    def conv_silu(x_ref, w_ref, i):
        rows = _super_rows(i)
        x, w = x_ref[rows, :], w_ref[...]
        halo = jnp.zeros((8, x.shape[1]), F32) if i == 0 else x_ref[pl.ds(i * SUPER - 8, 8), :]
        ext = jnp.concatenate([halo, x], axis=0)
        c = x * w[A_CONV - 1:A_CONV, :]
        for j in range(1, A_CONV):
            c = c + pltpu.roll(ext, j, 0)[8:, :] * w[A_CONV - 1 - j:A_CONV - j, :]
        return c * _sigmoid(c)

    def unit(a):
        return a * lax.rsqrt(jnp.sum(a * a, axis=1, keepdims=True) + EPS)

    def prepare_steps(i, dst):
        rows = _super_rows(i)
        q = unit(conv_silu(q_ref, wq_ref, i)) * A_DK ** -0.5
        k = unit(conv_silu(k_ref, wk_ref, i))
        v = conv_silu(v_ref, wv_ref, i)
        qo_ref[rows, :], ko_ref[rows, :], vo_ref[rows, :] = q, k, v
        t = yield from _gdn_super_steps(q, k, v, gc_ref[rows, :], gr_ref[i], head)
        tinv_ref[rows, :] = _diag_blocks_tall(t["tinv"])
        _store_scan_operands(whole, q, k, t, *dst)

    def scan_steps(i, src):
        u_scr, w_scr, p_scr, qg_scr, ke_scr, gl_scr = src
        for j in range(GDN_GROUP):
            n = i * GDN_GROUP + j
            local = pl.ds(j * CHUNK, CHUNK)
            s = s_scr[...]
            sb = _bf(s)
            st_ref[n] = sb
            ws = jnp.dot(w_scr[local, :], sb, preferred_element_type=F32)
            yield
            vb = _bf(u_scr[local, :] - ws)
            o = (jnp.dot(qg_scr[local, :], sb, preferred_element_type=F32)
                 + jnp.dot(p_scr[local, :], vb, preferred_element_type=F32))
            s_new = s * gl_scr[local, :][0:1, 0:1] + lax.dot_general(
                ke_scr[local, :], vb, (((0,), (0,)), ((), ())), preferred_element_type=F32)
            yield
            rows = _chunk_rows(n)
            o_ref[rows, :] = o
            s_scr[...] = s_new
            silu, _ = _silu_parts(z_ref[rows, :])
            r = lax.rsqrt(jnp.mean(o * o, axis=1, keepdims=True) + EPS)
            og_ref[rows, :] = ((o * r * gn_ref[0:1, :]) * silu).astype(og_ref.dtype)

    def scan_many(first, srcs):
        for j, src in enumerate(srcs):
            yield from scan_steps(first + j, src)

    groups = [all_sets[:GDN_WIDTH], all_sets[GDN_WIDTH:]]
    _interleave(*[prepare_steps(j, groups[0][j]) for j in range(GDN_WIDTH)])
    s_scr[...] = jnp.zeros_like(s_scr)
    for g in range(n_super // GDN_WIDTH):
        cur, nxt = groups[g % 2], groups[(g + 1) % 2]
        first = g * GDN_WIDTH
        following = [prepare_steps(first + GDN_WIDTH + j, nxt[j]) for j in range(GDN_WIDTH)
                     if first + GDN_WIDTH + j < n_super]
        _interleave(scan_many(first, cur), *following)


def _gdn_in_specs(s_len, n_super, from_proj):
    k_col, v_col = (A_K_COL, A_V_COL) if from_proj else (0, 0)
    return [
        pl.BlockSpec((s_len, A_DK), lambda b, h: (b, h)),
        pl.BlockSpec((s_len, A_DK), lambda b, h: (b, k_col + h)),
        pl.BlockSpec((s_len, A_DV), lambda b, h: (b, v_col + h)),
        pl.BlockSpec((s_len, 128), lambda b, h: (b, 0)),
        pl.BlockSpec((None, None, n_super, 1, SUPER), lambda b, h: (b, A_HEADS + h, 0, 0, 0)),
    ]


def _gdn_scan_scratch(s_len):
    return [pltpu.VMEM((A_DK, A_DV), F32), pltpu.VMEM((s_len, A_DV), F32),
            pltpu.VMEM((s_len, A_DK), BF16), pltpu.VMEM((s_len, CHUNK), BF16),
            pltpu.VMEM((s_len, A_DK), BF16), pltpu.VMEM((s_len, A_DK), BF16),
            pltpu.VMEM((s_len, 128), F32)]


def gdn_fwd(proj_a, conv_w8, gates_col, gates_row, norm_g8, n_seq):
    t_rows = proj_a.shape[0]
    s_len = t_rows // n_seq
    n_chunks = s_len // CHUNK
    qk_spec = pl.BlockSpec((s_len, A_DK), lambda b, h: (b, h))
    v_spec = pl.BlockSpec((s_len, A_DV), lambda b, h: (b, h))
    return pl.pallas_call(
        _gdn_fwd_body, name="gdn_fwd", grid=(n_seq, A_HEADS),
        in_specs=_gdn_in_specs(s_len, s_len // SUPER, True) + [
            pl.BlockSpec((8, A_DK), lambda b, h: (0, h)),
            pl.BlockSpec((8, A_DK), lambda b, h: (0, A_K_COL + h)),
            pl.BlockSpec((8, A_DV), lambda b, h: (0, A_V_COL + h)),
            pl.BlockSpec((s_len, A_DV), lambda b, h: (b, A_Z_COL + h)),
            pl.BlockSpec((8, A_DV), lambda b, h: (0, 0)),
        ],
        out_specs=[
            v_spec,
            pl.BlockSpec((s_len, CHUNK), lambda b, h: (b * A_HEADS + h, 0)),
            pl.BlockSpec((None, n_chunks, A_DK, A_DV), lambda b, h: (b * A_HEADS + h, 0, 0, 0)),
            v_spec, qk_spec, qk_spec, v_spec,
        ],
        out_shape=[
            jax.ShapeDtypeStruct((t_rows, A_VW), F32),
            jax.ShapeDtypeStruct((n_seq * A_HEADS * s_len, CHUNK), F32),
            jax.ShapeDtypeStruct((n_seq * A_HEADS, n_chunks, A_DK, A_DV), BF16),
            jax.ShapeDtypeStruct((t_rows, A_VW), BF16),
            jax.ShapeDtypeStruct((t_rows, A_QK), F32),
            jax.ShapeDtypeStruct((t_rows, A_QK), F32),
            jax.ShapeDtypeStruct((t_rows, A_VW), F32),
        ],
        scratch_shapes=_gdn_scan_scratch(SUPER) + (2 * GDN_WIDTH - 1) * _gdn_scan_scratch(SUPER)[1:],
        compiler_params=_params(("arbitrary", "arbitrary")),
    )(proj_a, proj_a, proj_a, gates_col, gates_row, conv_w8, conv_w8, conv_w8, proj_a, norm_g8)


def _gdn_bwd_body(q_ref, k_ref, v_ref, gc_ref, gr_ref, tinv_ref, st_ref, dog_ref, oa_ref, z_ref, gn_ref,
                  xq_ref, xk_ref, xv_ref, wq_ref, wk_ref, wv_ref,
                  dxq_ref, dxk_ref, dxv_ref, dgc_ref, dz_ref, dgn_ref, dwq_ref, dwk_ref, dwv_ref,
                  ds_scr, cq_scr, ck_scr, cv_scr, *sets):
    head = pl.program_id(1)
    n_super = q_ref.shape[0] // SUPER
    ops = (sets[0:7], sets[7:14])
    res = (sets[14:21], sets[21:28])
    whole = pl.ds(0, SUPER)
    tn = functools.partial(lax.dot_general, dimension_numbers=(((0,), (0,)), ((), ())), preferred_element_type=F32)
    nt = functools.partial(lax.dot_general, dimension_numbers=(((1,), (1,)), ((), ())), preferred_element_type=F32)

    @pl.when(head == 0)
    def _():
        dgc_ref[...] = jnp.zeros_like(dgc_ref)

    @pl.when((head == 0) & (pl.program_id(0) == 0))
    def _():
        dgn_ref[...] = jnp.zeros_like(dgn_ref)

    carry = (cq_scr, ck_scr, cv_scr)
    for ref in carry + (dwq_ref, dwk_ref, dwv_ref):
        ref[...] = jnp.zeros_like(ref)

    def common_steps(i):
        rows = _super_rows(i)
        q, k, v = q_ref[rows, :], k_ref[rows, :], v_ref[rows, :]
        t = yield from _gdn_super_steps(q, k, v, gc_ref[rows, :], gr_ref[i], head, tinv_tall=tinv_ref[rows, :])
        return rows, q, k, v, t

    def stage_p(i, parity):
        rows, q, k, _, t = yield from common_steps(i)
        _store_scan_operands(whole, q, k, t, *ops[parity][:6])
        o, d_og, gain = oa_ref[rows, :], dog_ref[rows, :], gn_ref[0:1, :]
        r = lax.rsqrt(jnp.mean(o * o, axis=1, keepdims=True) + EPS)
        silu, dsilu = _silu_parts(z_ref[rows, :])
        xr = o * r
        d_on = d_og * silu
        dz_ref[rows, :] = (d_og * (xr * gain) * dsilu).astype(dz_ref.dtype)
        u = d_on * gain
        ops[parity][6][...] = r * u - xr * (r * r) * jnp.mean(u * o, axis=1, keepdims=True)
        dgn_ref[0:1, :] += jnp.sum(d_on * xr, axis=0, keepdims=True)

    def stage_s(i, parity):
        u_scr, w_scr, p_scr, qg_scr, ke_scr, gl_scr, do_scr = ops[parity]
        vn_scr, dvn_scr, dqg_scr, dw_scr, dkt_scr, sds_scr, dof_scr = res[parity]
        for j in reversed(range(GDN_GROUP)):
            n = i * GDN_GROUP + j
            local = pl.ds(j * CHUNK, CHUNK)
            ds_next = ds_scr[...]
            dsb = _bf(ds_next)
            sb = st_ref[n]
            s = sb.astype(F32)
            d_o = do_scr[local, :]
            dof_scr[local, :] = d_o
            d_ob = _bf(d_o)
            w_s = jnp.dot(w_scr[local, :], sb, preferred_element_type=F32)
            d_vn = tn(p_scr[local, :], d_ob) + jnp.dot(ke_scr[local, :], dsb, preferred_element_type=F32)
            d_qg = nt(d_ob, sb)
            qg_do = tn(qg_scr[local, :], d_ob)
            yield
            v_new = u_scr[local, :] - w_s
            d_vnb = _bf(d_vn)
            d_w = -nt(d_vnb, sb)
            d_kt = nt(_bf(v_new), dsb)
            w_dvn = tn(w_scr[local, :], d_vnb)
            yield
            vn_scr[local, :] = v_new
            dvn_scr[local, :] = d_vn
            dqg_scr[local, :] = d_qg
            dw_scr[local, :] = d_w
            dkt_scr[local, :] = d_kt
            sds = jnp.sum(jnp.sum(s * ds_next, axis=1, keepdims=True), axis=0, keepdims=True)
            sds_scr[local, :] = jnp.broadcast_to(sds, (CHUNK, 128))
            ds_scr[...] = qg_do + gl_scr[local, :][0:1, 0:1] * ds_next - w_dvn

    def conv_bwd(i, rows, x_ref, w_ref, dy, norm_scale, dx_ref, dw_ref, dc_above):
        x, w = x_ref[rows, :], w_ref[...]
        above = x_ref[pl.ds(pl.multiple_of(jnp.maximum(i * SUPER - 8, 0), 8), 8), :]
        ext = jnp.concatenate([jnp.where(i > 0, above, 0.0), x], axis=0)
        c = x * w[A_CONV - 1:A_CONV, :]
        for j in range(1, A_CONV):
            c = c + pltpu.roll(ext, j, 0)[8:, :] * w[A_CONV - 1 - j:A_CONV - j, :]
        sig = _sigmoid(c)
        a = c * sig
        if norm_scale is None:
            da = dy
        else:
            rn = lax.rsqrt(jnp.sum(a * a, axis=1, keepdims=True) + EPS)
            da = norm_scale * (rn * dy - a * (rn * rn * rn) * jnp.sum(dy * a, axis=1, keepdims=True))
        dc = da * (sig * (1.0 + c * (1.0 - sig)))
        ext_dc = jnp.concatenate([dc, dc_above[...]], axis=0)
        dc_above[...] = dc[0:8, :]
        dx = dc * w[A_CONV - 1:A_CONV, :]
        dw_ref[A_CONV - 1:A_CONV, :] += jnp.sum(dc * x, axis=0, keepdims=True)
        for j in range(1, A_CONV):
            dcs = pltpu.roll(ext_dc, SUPER + 8 - j, 0)[:SUPER, :]
            dx = dx + dcs * w[A_CONV - 1 - j:A_CONV - j, :]
            dw_ref[A_CONV - 1 - j:A_CONV - j, :] += jnp.sum(dcs * x, axis=0, keepdims=True)
        dx_ref[rows, :] = dx.astype(dx_ref.dtype)

    def stage_f(i, parity):
        vn_scr, dvn_scr, dqg_scr, dw_scr, dkt_scr, sds_scr, dof_scr = res[parity]
        rows, q, k, v, t = yield from common_steps(i)
        beta, gamma, decay, kb, e_tail = t["beta"], t["gamma"], t["decay"], t["kb"], t["e_tail"]
        row, col, lane, same = t["row"], t["col"], t["lane"], t["same"]
        d_o = dof_scr[...]
        v_new, d_vn = vn_scr[...], dvn_scr[...]
        d_qg, d_w, d_kt = dqg_scr[...], dw_scr[...], dkt_scr[...]
        gamma_last = jnp.exp(t["g_last"])

        d_p = jnp.where(same & (row >= col), _mm_nt(d_o, v_new), 0.0)
        d_ru = _mm_tn(t["tinv"], d_vn)
        d_rw = _mm_tn(t["tinv"], d_w)
        yield
        d_m = jnp.where(same & (row > col), -(_mm_nt(d_ru, t["u"]) + _mm_nt(d_rw, t["w"])), 0.0)
        yield

        x_p = d_p * decay
        y_m = d_m * decay
        d_kb = _mm(y_m, k) + d_rw * gamma
        d_q = _mm(x_p, k) + d_qg * gamma
        d_k = _mm_tn(x_p, q) + _mm_tn(y_m, kb) + d_kb * beta + d_kt * e_tail
        d_v = d_ru * beta
        conv_bwd(i, rows, xq_ref, wq_ref, d_q, A_DK ** -0.5, dxq_ref, dwq_ref, carry[0])
        conv_bwd(i, rows, xk_ref, wk_ref, d_k, 1.0, dxk_ref, dwk_ref, carry[1])
        conv_bwd(i, rows, xv_ref, wv_ref, d_v, None, dxv_ref, dwv_ref, carry[2])

        d_beta = (jnp.sum(d_ru * v, axis=1, keepdims=True)
                  + jnp.sum(d_kb * k, axis=1, keepdims=True))
        z = d_p * t["p"] + d_m * t["m"]
        eps_tail = jnp.sum(d_kt * k, axis=1, keepdims=True) * e_tail
        d_gc = (jnp.sum(z, axis=1, keepdims=True) - _colsum_as_col(z)
                + jnp.sum(d_qg * q, axis=1, keepdims=True) * gamma
                + jnp.sum(d_rw * kb, axis=1, keepdims=True) * gamma
                - eps_tail)
        d_glast = _block_sum(same, eps_tail) + gamma_last * sds_scr[...][:, 0:1]
        yield
        rcol = lax.broadcasted_iota(jnp.int32, (SUPER, 1), 0)
        d_gc = d_gc + jnp.where(rcol % CHUNK == CHUNK - 1, d_glast, 0.0)
        dgc_ref[rows, :] += (jnp.where(lane == head, d_beta, 0.0)
                             + jnp.where(lane == A_HEADS + head, d_gc, 0.0))

    last = n_super - 1
    _drain(stage_p(last, 1))
    ds_scr[...] = jnp.zeros_like(ds_scr)
    _interleave(stage_s(last, 1), stage_p(last - 1, 0))

    def pair(k, carry):
        i = last - 1 - 2 * k
        _interleave(stage_s(i, 0), stage_f(i + 1, 1), stage_p(i - 1, 1))
        _interleave(stage_s(i - 1, 1), stage_f(i, 0), stage_p(i - 2, 0))
        return carry

    lax.fori_loop(0, n_super // 2 - 1, pair, 0)
    _interleave(stage_s(0, 0), stage_f(1, 1))
    _drain(stage_f(0, 0))


def gdn_bwd(q, k, v, gates_col, gates_row, tinv, states, d_og, o, proj_a, conv_w8, norm_g8, n_seq):
    t_rows = q.shape[0]
    s_len = t_rows // n_seq
    n_chunks = s_len // CHUNK
    qk_spec = pl.BlockSpec((s_len, A_DK), lambda b, h: (b, h))
    v_spec = pl.BlockSpec((s_len, A_DV), lambda b, h: (b, h))
    gate_spec = pl.BlockSpec((s_len, 128), lambda b, h: (b, 0))
    gain_spec = pl.BlockSpec((8, A_DV), lambda b, h: (0, 0))
    dw_qk_spec = pl.BlockSpec((None, 8, A_DK), lambda b, h: (b, 0, h))
    dw_v_spec = pl.BlockSpec((None, 8, A_DV), lambda b, h: (b, 0, h))
    ops_set = _gdn_scan_scratch(SUPER)[1:] + [pltpu.VMEM((SUPER, A_DV), F32)]
    res_set = [pltpu.VMEM((SUPER, A_DV), F32), pltpu.VMEM((SUPER, A_DV), F32),
               pltpu.VMEM((SUPER, A_DK), F32), pltpu.VMEM((SUPER, A_DK), F32),
               pltpu.VMEM((SUPER, A_DK), F32), pltpu.VMEM((SUPER, 128), F32), pltpu.VMEM((SUPER, A_DV), F32)]
    return pl.pallas_call(
        _gdn_bwd_body, name="gdn_bwd", grid=(n_seq, A_HEADS),
        in_specs=_gdn_in_specs(s_len, s_len // SUPER, False) + [
            pl.BlockSpec((s_len, CHUNK), lambda b, h: (b * A_HEADS + h, 0)),
            pl.BlockSpec((None, n_chunks, A_DK, A_DV), lambda b, h: (b * A_HEADS + h, 0, 0, 0)),
            v_spec, v_spec,
            pl.BlockSpec((s_len, A_DV), lambda b, h: (b, A_Z_COL + h)),
            gain_spec,
            pl.BlockSpec((s_len, A_DK), lambda b, h: (b, h)),
            pl.BlockSpec((s_len, A_DK), lambda b, h: (b, A_K_COL + h)),
            pl.BlockSpec((s_len, A_DV), lambda b, h: (b, A_V_COL + h)),
            pl.BlockSpec((8, A_DK), lambda b, h: (0, h)),
            pl.BlockSpec((8, A_DK), lambda b, h: (0, A_K_COL + h)),
            pl.BlockSpec((8, A_DV), lambda b, h: (0, A_V_COL + h)),
        ],
        out_specs=[qk_spec, qk_spec, v_spec, gate_spec, v_spec, gain_spec, dw_qk_spec, dw_qk_spec, dw_v_spec],
        out_shape=[
            jax.ShapeDtypeStruct((t_rows, A_QK), BF16),
            jax.ShapeDtypeStruct((t_rows, A_QK), BF16),
            jax.ShapeDtypeStruct((t_rows, A_VW), BF16),
            jax.ShapeDtypeStruct((t_rows, 128), F32),
            jax.ShapeDtypeStruct((t_rows, A_VW), BF16),
            jax.ShapeDtypeStruct((8, A_DV), F32),
            jax.ShapeDtypeStruct((n_seq, 8, A_QK), F32),
            jax.ShapeDtypeStruct((n_seq, 8, A_QK), F32),
            jax.ShapeDtypeStruct((n_seq, 8, A_VW), F32),
        ],
        scratch_shapes=(_gdn_scan_scratch(SUPER)[:1]
                        + [pltpu.VMEM((8, A_DK), F32), pltpu.VMEM((8, A_DK), F32), pltpu.VMEM((8, A_DV), F32)]
                        + 2 * ops_set + 2 * res_set),
        compiler_params=_params(("arbitrary", "arbitrary")),
    )(q, k, v, gates_col, gates_row, tinv, states, d_og, o, proj_a, norm_g8,
      proj_a, proj_a, proj_a, conv_w8, conv_w8, conv_w8)


GATE_TILE = 512


def _softplus(y):
    return jnp.maximum(y, 0.0) + jnp.log1p(jnp.exp(-jnp.abs(y)))


def _gate_values(x, prm):
    beta = _sigmoid(x)
    y = x + prm[1:2, :]
    neg_a = -jnp.exp(prm[0:1, :])
    g = neg_a * _softplus(y)
    return beta, y, neg_a, g


def _a_gates_fwd_body(x_ref, prm_ref, gc_ref, gr_ref):
    x = x_ref[...]
    tm = x.shape[0]
    beta, _, _, g = _gate_values(x, prm_ref[...])
    in_chunk = lax.broadcasted_iota(jnp.int32, (tm, 1), 0) % CHUNK
    s = 1
    while s < CHUNK:
        g = g + jnp.where(in_chunk >= s, pltpu.roll(g, s, 0), 0.0)
        s *= 2
    lane = lax.broadcasted_iota(jnp.int32, x.shape, 1)
    out = jnp.where(lane < A_HEADS, beta, jnp.where(lane < 2 * A_HEADS, g, 0.0))
    gc_ref[...] = out
    gr_ref[...] = out.T[0:2 * A_HEADS, :]


def a_gates_fwd(proj_a, prm, n_seq):
    t_rows = proj_a.shape[0]
    s_len = t_rows // n_seq
    tm = min(GATE_TILE, s_len)
    n_t = s_len // tm
    return pl.pallas_call(
        _a_gates_fwd_body, name="a_gates_fwd", grid=(n_seq, n_t),
        in_specs=[pl.BlockSpec((tm, 128), lambda b, i: (b * n_t + i, A_GATE_COL)),
                  pl.BlockSpec((8, 128), lambda b, i: (0, 0))],
        out_specs=[pl.BlockSpec((tm, 128), lambda b, i: (b * n_t + i, 0)),
                   pl.BlockSpec((None, 2 * A_HEADS, tm), lambda b, i: (b, 0, i))],
        out_shape=[jax.ShapeDtypeStruct((t_rows, 128), F32),
                   jax.ShapeDtypeStruct((n_seq, 2 * A_HEADS, s_len), F32)],
        compiler_params=_params(("arbitrary", "arbitrary")),
    )(proj_a, prm)


def _a_gates_bwd_body(x_ref, prm_ref, dgc_ref, dx_ref, dprm_ref):
    first = (pl.program_id(0) == 0) & (pl.program_id(1) == 0)
    x = x_ref[...]
    tm = x.shape[0]
    beta, y, neg_a, g = _gate_values(x, prm_ref[...])
    d = dgc_ref[...]
    in_chunk = lax.broadcasted_iota(jnp.int32, (tm, 1), 0) % CHUNK
    dg = d
    s = 1
    while s < CHUNK:
        dg = dg + jnp.where(in_chunk < CHUNK - s, pltpu.roll(dg, tm - s, 0), 0.0)
        s *= 2
    lane = lax.broadcasted_iota(jnp.int32, x.shape, 1)
    is_decay = (lane >= A_HEADS) & (lane < 2 * A_HEADS)
    d_alogit = jnp.where(is_decay, dg * neg_a * _sigmoid(y), 0.0)
    dx_ref[...] = jnp.where(lane < A_HEADS, d * beta * (1.0 - beta), d_alogit).astype(dx_ref.dtype)

    @pl.when(first)
    def _():
        dprm_ref[...] = jnp.zeros_like(dprm_ref)

    dprm_ref[0:1, :] += jnp.sum(jnp.where(is_decay, dg * g, 0.0), axis=0, keepdims=True)
    dprm_ref[1:2, :] += jnp.sum(d_alogit, axis=0, keepdims=True)


def a_gates_bwd(proj_a, prm, dgates_col, n_seq):
    t_rows = proj_a.shape[0]
    s_len = t_rows // n_seq
    tm = min(GATE_TILE, s_len)
    n_t = s_len // tm
    return pl.pallas_call(
        _a_gates_bwd_body, name="a_gates_bwd", grid=(n_seq, n_t),
        in_specs=[pl.BlockSpec((tm, 128), lambda b, i: (b * n_t + i, A_GATE_COL)),
                  pl.BlockSpec((8, 128), lambda b, i: (0, 0)),
                  pl.BlockSpec((tm, 128), lambda b, i: (b * n_t + i, 0))],
        out_specs=[pl.BlockSpec((tm, 128), lambda b, i: (b * n_t + i, 0)),
                   pl.BlockSpec((8, 128), lambda b, i: (0, 0))],
        out_shape=[jax.ShapeDtypeStruct((t_rows, 128), BF16),
                   jax.ShapeDtypeStruct((8, 128), F32)],
        compiler_params=_params(("arbitrary", "arbitrary")),
    )(proj_a, prm, dgates_col)


ROW_TILE = 512
A_Z_COL = (2 * A_QK + A_VW) // A_DV


def _silu_parts(z):
    sig = _sigmoid(z)
    return z * sig, sig * (1.0 + z * (1.0 - sig))


NEG_BIG = -1e30
ATT_SCALE = B_DH ** -0.5


def _swap_rope_halves(x):
    src = lax.broadcasted_iota(jnp.int32, (B_DH, B_DH), 0)
    dst = lax.broadcasted_iota(jnp.int32, (B_DH, B_DH), 1)
    pick = ((dst < ROPE_HALF) & (src == dst + ROPE_HALF)) | (
        (dst >= ROPE_HALF) & (dst < ROPE_DIMS) & (src == dst - ROPE_HALF))
    return jnp.dot(_bf(x), pick.astype(BF16), preferred_element_type=F32)


def _norm_rope(x, gain, cos_t, sin_t):
    r = lax.rsqrt(jnp.mean(x * x, axis=1, keepdims=True) + EPS)
    xn = x * r * gain
    return xn * cos_t + _swap_rope_halves(xn) * sin_t, r


def _norm_rope_bwd(x, r, gain, cos_t, sin_t, dy):
    d_xn = dy * cos_t + _swap_rope_halves(dy * sin_t)
    xr = x * r
    u = d_xn * gain
    dx = r * u - xr * (r * r) * jnp.mean(u * x, axis=1, keepdims=True)
    return dx, jnp.sum(d_xn * xr, axis=0, keepdims=True)


def _stream_rows(idx, dilation, s_len):
    nb = s_len // dilation // B_BLOCK
    r = idx // nb
    m = idx % nb
    cur = r + m * (B_BLOCK * dilation)
    prev = r + jnp.maximum(m - 1, 0) * (B_BLOCK * dilation)
    return cur, prev, m > 0


def _rows(start, dilation):
    if dilation == 1:
        return pl.ds(start, B_BLOCK)
    return pl.ds(start, B_BLOCK, stride=dilation)


ATT_UNROLL = 16


def _band_mask(has_prev):
    qi = lax.broadcasted_iota(jnp.int32, (B_BLOCK, 2 * B_BLOCK), 0)
    kj = lax.broadcasted_iota(jnp.int32, (B_BLOCK, 2 * B_BLOCK), 1)
    return ((kj < B_BLOCK) & (kj >= qi) & has_prev) | ((kj >= B_BLOCK) & (kj - B_BLOCK <= qi))


def _attn_fwd_body(qkv_ref, z_ref, cos_ref, sin_ref, gain_ref, og_ref, o_ref, lse_ref,
                   qn_scr, kn_scr, og_scr, lg_scr):
    head, grp = pl.program_id(1), pl.program_id(2)
    s_len = z_ref.shape[0]
    n_blocks = s_len // B_BLOCK
    cos_t, sin_t = cos_ref[...], sin_ref[...]

    for gi, dil in enumerate(B_DILATIONS):
        @pl.when(grp == gi)
        def _(gi=gi, dil=dil):
            qn_scr[...], _ = _norm_rope(qkv_ref[0], gain_ref[gi:gi + 1, :], cos_t, sin_t)
            kn_scr[...], _ = _norm_rope(qkv_ref[1], gain_ref[B_GROUPS + gi:B_GROUPS + gi + 1, :], cos_t, sin_t)

            ones = jnp.ones((2 * B_BLOCK, B_DH), BF16)

            def blocks(it, carry):
                scored = []
                for j in range(ATT_UNROLL):
                    cur, prev, has_prev = _stream_rows(it * ATT_UNROLL + j, dil, s_len)
                    rc, rp = _rows(cur, dil), _rows(prev, dil)
                    k2 = jnp.concatenate([kn_scr[rp, :], kn_scr[rc, :]], axis=0)
                    scored.append((rc, rp, has_prev, _mm_nt(qn_scr[rc, :], k2) * ATT_SCALE))
                summed = []
                for rc, rp, has_prev, s in scored:
                    s = jnp.where(_band_mask(has_prev), s, NEG_BIG)
                    mx = jnp.max(s, axis=1, keepdims=True)
                    v2 = jnp.concatenate([qkv_ref.at[2][rp, :], qkv_ref.at[2][rc, :]], axis=0)
                    acc = jnp.dot(_bf(jnp.exp(s - mx)), jnp.concatenate([_bf(v2), ones], axis=1),
                                  preferred_element_type=F32)
                    summed.append((rc, mx, acc))
                for rc, mx, acc in summed:
                    den = acc[:, B_DH:B_DH + 1]
                    og_scr.at[gi][rc, :] = acc[:, :B_DH] / den
                    lg_scr.at[gi][rc, :] = jnp.broadcast_to(mx + jnp.log(den), (B_BLOCK, B_DH))
                return carry

            lax.fori_loop(0, n_blocks // ATT_UNROLL, blocks, 0)

    @pl.when(grp == B_GROUPS - 1)
    def _():
        l0, l1, l2 = lg_scr[0], lg_scr[1], lg_scr[2]
        mx = jnp.maximum(jnp.maximum(l0, l1), l2)
        w0, w1, w2 = jnp.exp(l0 - mx), jnp.exp(l1 - mx), jnp.exp(l2 - mx)
        den = w0 + w1 + w2
        o = (w0 * og_scr[0] + w1 * og_scr[1] + w2 * og_scr[2]) / den
        silu, _ = _silu_parts(z_ref[...])
        o_ref[...] = o
        og_ref[...] = (o * silu).astype(og_ref.dtype)
        @pl.when(head == 0)
        def _():
            lse_ref[...] = jnp.zeros_like(lse_ref)

        lane = lax.broadcasted_iota(jnp.int32, o.shape, 1)
        lse_ref[...] = jnp.where(lane == head, mx + jnp.log(den), lse_ref[...])


def attn_fwd(proj_b, cos_t, sin_t, gains8, n_seq):
    t_rows = proj_b.shape[1]
    s_len = t_rows // n_seq
    head_blk = pl.BlockSpec((s_len, B_DH), lambda b, h, g: (b, h))
    seq_blk = pl.BlockSpec((s_len, 128), lambda b, h, g: (b, 0))
    return pl.pallas_call(
        _attn_fwd_body, name="attn_fwd", grid=(n_seq, B_HEADS, B_GROUPS),
        in_specs=[
            pl.BlockSpec((3, s_len, B_DH), lambda b, h, g: (g, b, h)),
            pl.BlockSpec((None, s_len, B_DH), lambda b, h, g: (B_PIECES - 1, b, h)),
            seq_blk, seq_blk,
            pl.BlockSpec((8, 128), lambda b, h, g: (0, 0)),
        ],
        out_specs=[head_blk, head_blk, seq_blk],
        out_shape=[jax.ShapeDtypeStruct((t_rows, B_W), BF16),
                   jax.ShapeDtypeStruct((t_rows, B_W), F32),
                   jax.ShapeDtypeStruct((t_rows, 128), F32)],
        scratch_shapes=[pltpu.VMEM((s_len, B_DH), F32), pltpu.VMEM((s_len, B_DH), F32),
                        pltpu.VMEM((B_GROUPS, s_len, B_DH), F32), pltpu.VMEM((B_GROUPS, s_len, B_DH), F32)],
        compiler_params=_params(("arbitrary", "arbitrary", "arbitrary")),
    )(proj_b, proj_b, cos_t, sin_t, gains8)


def _attn_bwd_body(qkv_ref, z_ref, cos_ref, sin_ref, gain_ref, dog_ref, o_ref, lse_ref,
                   dqkv_ref, dz_ref, dgain_ref,
                   qn_scr, kn_scr, dqn_scr, dkn_scr, do_scr, dl_scr, ls_scr, dv_scr):
    head, grp = pl.program_id(1), pl.program_id(2)
    first = (pl.program_id(0) == 0) & (head == 0) & (grp == 0)
    s_len = z_ref.shape[0]
    n_blocks = s_len // B_BLOCK
    cos_t, sin_t = cos_ref[...], sin_ref[...]

    @pl.when(first)
    def _():
        dgain_ref[...] = jnp.zeros_like(dgain_ref)

    @pl.when(grp == 0)
    def _():
        d_og, o = dog_ref[...], o_ref[...]
        silu, dsilu = _silu_parts(z_ref[...])
        d_o = d_og * silu
        dz_ref[...] = (d_og * o * dsilu).astype(dz_ref.dtype)
        do_scr[...] = d_o
        dl_scr[...] = jnp.broadcast_to(jnp.sum(d_o * o, axis=1, keepdims=True), o.shape)
        lane = lax.broadcasted_iota(jnp.int32, o.shape, 1)
        ls_scr[...] = jnp.broadcast_to(
            jnp.sum(jnp.where(lane == head, lse_ref[...], 0.0), axis=1, keepdims=True), o.shape)

    for gi, dil in enumerate(B_DILATIONS):
        @pl.when(grp == gi)
        def _(gi=gi, dil=dil):
            q_raw, k_raw = qkv_ref[0], qkv_ref[1]
            gq = gain_ref[gi:gi + 1, :]
            gk = gain_ref[B_GROUPS + gi:B_GROUPS + gi + 1, :]
            qn_scr[...], rq = _norm_rope(q_raw, gq, cos_t, sin_t)
            kn_scr[...], rk = _norm_rope(k_raw, gk, cos_t, sin_t)
            def blocks(it, carry):
                scored = []
                for j in range(ATT_UNROLL):
                    cur, prev, has_prev = _stream_rows(it * ATT_UNROLL + j, dil, s_len)
                    rc, rp = _rows(cur, dil), _rows(prev, dil)
                    qb, d_ob = _bf(qn_scr[rc, :]), _bf(do_scr[rc, :])
                    k2 = _bf(jnp.concatenate([kn_scr[rp, :], kn_scr[rc, :]], axis=0))
                    v2 = _bf(jnp.concatenate([qkv_ref.at[2][rp, :], qkv_ref.at[2][rc, :]], axis=0))
                    scored.append((rc, rp, has_prev, qb, d_ob, k2,
                                   _mm_nt(qb, k2) * ATT_SCALE, _mm_nt(d_ob, v2)))
                grads = []
                for rc, rp, has_prev, qb, d_ob, k2, s, d_p in scored:
                    p = jnp.exp(jnp.where(_band_mask(has_prev), s - ls_scr[rc, :][:, 0:1], NEG_BIG))
                    ds = _bf(p * (d_p - dl_scr[rc, :][:, 0:1]))
                    grads.append((rc, rp, has_prev,
                                  _mm(ds, k2) * ATT_SCALE, _mm_tn(ds, qb) * ATT_SCALE, _mm_tn(_bf(p), d_ob)))
                for j, (rc, rp, has_prev, dq, dk2, dv2) in enumerate(grads):
                    dqn_scr[rc, :] = dq
                    if j == 0:
                        @pl.when(has_prev)
                        def _():
                            dkn_scr[rp, :] += dk2[:B_BLOCK]
                            dv_scr[rp, :] += dv2[:B_BLOCK]
                    if j + 1 < ATT_UNROLL:
                        dkn_scr[rc, :] = dk2[B_BLOCK:] + grads[j + 1][4][:B_BLOCK]
                        dv_scr[rc, :] = dv2[B_BLOCK:] + grads[j + 1][5][:B_BLOCK]
                    else:
                        dkn_scr[rc, :] = dk2[B_BLOCK:]
                        dv_scr[rc, :] = dv2[B_BLOCK:]
                return carry

            lax.fori_loop(0, n_blocks // ATT_UNROLL, blocks, 0)
            dq, dgq = _norm_rope_bwd(q_raw, rq, gq, cos_t, sin_t, dqn_scr[...])
            dk, dgk = _norm_rope_bwd(k_raw, rk, gk, cos_t, sin_t, dkn_scr[...])
            dqkv_ref[0] = dq.astype(dqkv_ref.dtype)
            dqkv_ref[1] = dk.astype(dqkv_ref.dtype)
            dqkv_ref[2] = dv_scr[...].astype(dqkv_ref.dtype)
            dgain_ref[gi:gi + 1, :] += dgq
            dgain_ref[B_GROUPS + gi:B_GROUPS + gi + 1, :] += dgk


def attn_bwd(proj_b, cos_t, sin_t, gains8, d_og, o, lse, n_seq):
    t_rows = proj_b.shape[1]
    s_len = t_rows // n_seq
    head_blk = pl.BlockSpec((s_len, B_DH), lambda b, h, g: (b, h))
    seq_blk = pl.BlockSpec((s_len, 128), lambda b, h, g: (b, 0))
    grp_blk = pl.BlockSpec((3, s_len, B_DH), lambda b, h, g: (g, b, h))
    gain_blk = pl.BlockSpec((8, 128), lambda b, h, g: (0, 0))
    return pl.pallas_call(
        _attn_bwd_body, name="attn_bwd", grid=(n_seq, B_HEADS, B_GROUPS),
        in_specs=[
            grp_blk,
            pl.BlockSpec((None, s_len, B_DH), lambda b, h, g: (B_PIECES - 1, b, h)),
            seq_blk, seq_blk, gain_blk, head_blk, head_blk, seq_blk,
        ],
        out_specs=[grp_blk, head_blk, gain_blk],
        out_shape=[jax.ShapeDtypeStruct((3 * B_GROUPS, t_rows, B_W), BF16),
                   jax.ShapeDtypeStruct((t_rows, B_W), BF16),
                   jax.ShapeDtypeStruct((8, 128), F32)],
        scratch_shapes=[pltpu.VMEM((s_len, B_DH), F32) for _ in range(8)],
        compiler_params=_params(("arbitrary", "arbitrary", "arbitrary")),
    )(proj_b, proj_b, cos_t, sin_t, gains8, d_og, o, lse)


def rope_tables(positions):
    inv_freq = ROPE_THETA ** (-jnp.arange(0, ROPE_DIMS, 2, dtype=F32) / ROPE_DIMS)
    ang = positions.astype(F32)[:, None] * inv_freq
    cos, sin = jnp.cos(ang), jnp.sin(ang)
    t_rows = positions.shape[0]
    rest = B_DH - ROPE_DIMS
    cos_t = jnp.concatenate([cos, cos, jnp.ones((t_rows, rest), F32)], axis=1)
    sin_t = jnp.concatenate([-sin, sin, jnp.zeros((t_rows, rest), F32)], axis=1)
    return cos_t, sin_t


def _rms_fwd_body(x_ref, g_ref, *rest, layer):
    h_ref, ht_ref = rest[-2:]
    x = x_ref[...]
    r = lax.rsqrt(jnp.mean(x * x, axis=1, keepdims=True) + EPS)
    h = x * r * g_ref[layer:layer + 1, :]
    h_ref[...] = h.astype(h_ref.dtype)
    ht_ref[...] = h.T.astype(ht_ref.dtype)


def rms_fwd(x, gains8, layer, after=None):
    t_rows, d = x.shape
    tm = min(ROW_TILE, t_rows)
    in_specs = [pl.BlockSpec((tm, d), lambda i: (i, 0)), pl.BlockSpec((8, d), lambda i: (0, 0))]
    args = [x, gains8]
    if after is not None:
        in_specs.append(HBM_SPEC)
        args.append(after)
    return pl.pallas_call(
        functools.partial(_rms_fwd_body, layer=layer), name=f"rms_fwd_{layer}", grid=(t_rows // tm,),
        in_specs=in_specs,
        out_specs=[pl.BlockSpec((tm, d), lambda i: (i, 0)), pl.BlockSpec((d, tm), lambda i: (0, i))],
        out_shape=[jax.ShapeDtypeStruct((t_rows, d), BF16), jax.ShapeDtypeStruct((d, t_rows), BF16)],
        compiler_params=_params(("arbitrary",)),
    )(*args)


def _rms_bwd_body(x_ref, g_ref, dh_ref, res_ref, dx_ref, dg_ref, *, layer):
    x, dh = x_ref[...], dh_ref[...]
    r = lax.rsqrt(jnp.mean(x * x, axis=1, keepdims=True) + EPS)
    xr = x * r
    u = dh * g_ref[layer:layer + 1, :]
    dx_ref[...] = res_ref[...] + r * u - xr * (r * r) * jnp.mean(u * x, axis=1, keepdims=True)

    @pl.when(pl.program_id(0) == 0)
    def _():
        dg_ref[...] = jnp.zeros_like(dg_ref)

    dg_ref[0:1, :] += jnp.sum(dh * xr, axis=0, keepdims=True)


def rms_bwd(x, gains8, layer, dh, d_res):
    t_rows, d = x.shape
    tm = min(ROW_TILE, t_rows)
    blk = pl.BlockSpec((tm, d), lambda i: (i, 0))
    gblk = pl.BlockSpec((8, d), lambda i: (0, 0))
    return pl.pallas_call(
        functools.partial(_rms_bwd_body, layer=layer), name=f"rms_bwd_{layer}", grid=(t_rows // tm,),
        in_specs=[blk, gblk, blk, blk],
        out_specs=[blk, gblk],
        out_shape=[jax.ShapeDtypeStruct((t_rows, d), F32), jax.ShapeDtypeStruct((8, d), F32)],
        compiler_params=_params(("arbitrary",)),
    )(x, gains8, dh, d_res)


def _piece_col(p):
    return jnp.where(p < 3 * B_GROUPS, (p % 3) * B_GROUPS + p // 3, 3 * B_GROUPS)


def _mm_nn_body(a_ref, w_ref, *rest, has_res):
    o_ref = rest[-1]
    acc = jnp.dot(a_ref[...], w_ref[...], preferred_element_type=F32)
    if has_res:
        acc = acc + rest[0][...]
    o_ref[...] = acc


def mm_nn(a, w, residual=None, *, tn, name):
    m, k = a.shape
    n = w.shape[1]
    tm = min(NT_ROW_TILE, m)
    in_specs = [pl.BlockSpec((tm, k), lambda j, i: (i, 0)), pl.BlockSpec((k, tn), lambda j, i: (0, j))]
    args = [a, w]
    if residual is not None:
        in_specs.append(pl.BlockSpec((tm, tn), lambda j, i: (i, j)))
        args.append(residual)
    return pl.pallas_call(
        functools.partial(_mm_nn_body, has_res=residual is not None), name=name, grid=(n // tn, m // tm),
        in_specs=in_specs,
        out_specs=pl.BlockSpec((tm, tn), lambda j, i: (i, j)),
        out_shape=jax.ShapeDtypeStruct((m, n), F32),
        compiler_params=_params(("arbitrary", "arbitrary")),
    )(*args)


def mm_nn_pieces(a, w, *, name):
    m, k = a.shape
    tm = min(NT_ROW_TILE, m)
    return pl.pallas_call(
        functools.partial(_mm_nn_body, has_res=False), name=name, grid=(B_PIECES, m // tm),
        in_specs=[pl.BlockSpec((tm, k), lambda p, i: (i, 0)),
                  pl.BlockSpec((k, B_W), lambda p, i: (0, _piece_col(p)))],
        out_specs=pl.BlockSpec((None, tm, B_W), lambda p, i: (p, i, 0)),
        out_shape=jax.ShapeDtypeStruct((B_PIECES, m, B_W), F32),
        compiler_params=_params(("arbitrary", "arbitrary")),
    )(a, w)


NT_ROW_TILE = 1024


def _mm_nt_body(g_ref, w_ref, *rest, has_init):
    o_ref = rest[-1]
    j = pl.program_id(1)
    part = lax.dot_general(_bf(g_ref[...]), w_ref[...], (((1,), (1,)), ((), ())), preferred_element_type=F32)

    @pl.when(j == 0)
    def _():
        o_ref[...] = part + rest[0][...] if has_init else part

    @pl.when(j > 0)
    def _():
        o_ref[...] += part


def mm_nt(g, w, init=None, *, tn, col_off=0, name, after=None):
    m, n = g.shape
    k = w.shape[0]
    tm = min(NT_ROW_TILE, m)
    in_specs = [pl.BlockSpec((tm, tn), lambda i, j: (i, j)),
                pl.BlockSpec((k, tn), lambda i, j: (0, col_off + j))]
    args = [g, w]
    if init is not None:
        in_specs.append(pl.BlockSpec((tm, k), lambda i, j: (i, 0)))
        args.append(init)
    if after is not None:
        in_specs.append(HBM_SPEC)
        args.append(after)
    return pl.pallas_call(
        functools.partial(_mm_nt_body, has_init=init is not None), name=name, grid=(m // tm, n // tn),
        in_specs=in_specs,
        out_specs=pl.BlockSpec((tm, k), lambda i, j: (i, 0)),
        out_shape=jax.ShapeDtypeStruct((m, k), F32),
        compiler_params=_params(("arbitrary", "arbitrary")),
    )(*args)


def mm_nt_multi(gs, w, *, tn, name, after=None):
    m = gs[0].shape[0]
    k = w.shape[0]
    tm = min(NT_ROW_TILE, m)
    tiles = [g.shape[1] // tn for g in gs]
    starts = [sum(tiles[:i]) for i in range(len(gs))]

    def body(*refs):
        g_refs, w_ref, o_ref = refs[:len(gs)], refs[len(gs)], refs[-1]
        j = pl.program_id(1)
        for g_ref, lo, cnt in zip(g_refs, starts, tiles):
            @pl.when((j >= lo) & (j < lo + cnt))
            def _(g_ref=g_ref):
                part = lax.dot_general(_bf(g_ref[...]), w_ref[...], (((1,), (1,)), ((), ())),
                                       preferred_element_type=F32)

                @pl.when(j == 0)
                def _():
                    o_ref[...] = part

                @pl.when(j > 0)
                def _():
                    o_ref[...] += part

    def g_spec(lo, cnt):
        return pl.BlockSpec((tm, tn), lambda i, j: (i, jnp.clip(j - lo, 0, cnt - 1)))

    in_specs = [g_spec(lo, cnt) for lo, cnt in zip(starts, tiles)] + [pl.BlockSpec((k, tn), lambda i, j: (0, j))]
    args = list(gs) + [w]
    if after is not None:
        in_specs.append(HBM_SPEC)
        args.append(after)
    return pl.pallas_call(
        body, name=name, grid=(m // tm, sum(tiles)),
        in_specs=in_specs,
        out_specs=pl.BlockSpec((tm, k), lambda i, j: (i, 0)),
        out_shape=jax.ShapeDtypeStruct((m, k), F32),
        compiler_params=_params(("arbitrary", "arbitrary")),
    )(*args)


def mm_nt_pieces(g9, gz, w, *, name):
    n_p, m, _ = g9.shape
    k = w.shape[0]
    tm = min(NT_ROW_TILE, m)

    def body(g_ref, z_ref, w_ref, o_ref):
        p = pl.program_id(1)

        def accumulate(src):
            part = lax.dot_general(_bf(src[...]), w_ref[...], (((1,), (1,)), ((), ())), preferred_element_type=F32)

            @pl.when(p == 0)
            def _():
                o_ref[...] = part

            @pl.when(p > 0)
            def _():
                o_ref[...] += part

        @pl.when(p < n_p)
        def _():
            accumulate(g_ref)

        @pl.when(p == n_p)
        def _():
            accumulate(z_ref)

    return pl.pallas_call(
        body, name=name, grid=(m // tm, n_p + 1),
        in_specs=[pl.BlockSpec((None, tm, B_W), lambda i, p: (jnp.minimum(p, n_p - 1), i, 0)),
                  pl.BlockSpec((tm, B_W), lambda i, p: (i, 0)),
                  pl.BlockSpec((k, B_W), lambda i, p: (0, _piece_col(p)))],
        out_specs=pl.BlockSpec((tm, k), lambda i, p: (i, 0)),
        out_shape=jax.ShapeDtypeStruct((m, k), F32),
        compiler_params=_params(("arbitrary", "arbitrary")),
    )(g9, gz, w)


def _mm_tn_body(a_ref, g_ref, o_ref, *, a_is_transposed):
    lhs_dim = 1 if a_is_transposed else 0
    o_ref[...] = lax.dot_general(a_ref[...], _bf(g_ref[...]), (((lhs_dim,), (0,)), ((), ())),
                                 preferred_element_type=F32).astype(o_ref.dtype)


def mm_tn(a, g, *, tn, out_dtype, name, a_is_transposed=False):
    k = a.shape[0] if a_is_transposed else a.shape[1]
    m, n = g.shape
    return pl.pallas_call(
        functools.partial(_mm_tn_body, a_is_transposed=a_is_transposed), name=name, grid=(n // tn,),
        in_specs=[pl.BlockSpec(a.shape, lambda j: (0, 0)), pl.BlockSpec((m, tn), lambda j: (0, j))],
        out_specs=pl.BlockSpec((k, tn), lambda j: (0, j)),
        out_shape=jax.ShapeDtypeStruct((k, n), out_dtype),
        compiler_params=_params(("arbitrary",)),
    )(a, g)


B_UNIT = 256
B_IN_COLS = B_PIECES * B_W
B_SHARD_UNITS = B_IN_COLS // N_DEV // B_UNIT


def mm_tn_b_in(at, g9, gz, *, out_dtype, name):
    k, m = at.shape
    per_piece = B_W // B_UNIT
    body_one = functools.partial(_mm_tn_body, a_is_transposed=True)
    n_units = B_IN_COLS // B_UNIT

    def g_map(u):
        nat = jnp.minimum(u // per_piece, 3 * B_GROUPS - 1)
        piece = (nat % B_GROUPS) * 3 + nat // B_GROUPS
        return (piece, 0, u % per_piece)

    def body(a_ref, g_ref, z_ref, o_ref):
        u = pl.program_id(0)

        @pl.when(u < 3 * B_GROUPS * per_piece)
        def _():
            body_one(a_ref, g_ref, o_ref)

        @pl.when(u >= 3 * B_GROUPS * per_piece)
        def _():
            body_one(a_ref, z_ref, o_ref)

    return pl.pallas_call(
        body, name=name, grid=(n_units,),
        in_specs=[pl.BlockSpec((k, m), lambda u: (0, 0)),
                  pl.BlockSpec((None, m, B_UNIT), g_map),
                  pl.BlockSpec((m, B_UNIT), lambda u: (0, jnp.where(u < 3 * B_GROUPS * per_piece, 0, u % per_piece)))],
        out_specs=pl.BlockSpec((None, k, B_UNIT), lambda u: (u // B_SHARD_UNITS, 0, u % B_SHARD_UNITS)),
        out_shape=jax.ShapeDtypeStruct((N_DEV, k, B_IN_COLS // N_DEV), out_dtype),
        compiler_params=_params(("arbitrary",)),
    )(at, g9, gz)


def _loss_body(y_ref, t_ref, dy_ref, loss_ref, acc):
    i = pl.program_id(0)
    d = y_ref.shape[1]
    err = y_ref[...] - t_ref[...]
    dy_ref[...] = err * (1.0 / d)

    @pl.when(i == 0)
    def _():
        acc[...] = jnp.zeros_like(acc)

    acc[...] += jnp.sum(err * err, axis=0, keepdims=True)

    @pl.when(i == pl.num_programs(0) - 1)
    def _():
        total = jnp.sum(acc[...], axis=1, keepdims=True) * (0.5 / d)
        loss_ref[...] = jnp.broadcast_to(total, loss_ref.shape)


def loss_head(y, target):
    t_rows, d = y.shape
    tm = min(ROW_TILE, t_rows)
    blk = pl.BlockSpec((tm, d), lambda i: (i, 0))
    return pl.pallas_call(
        _loss_body, name="loss_head", grid=(t_rows // tm,),
        in_specs=[blk, blk],
        out_specs=[blk, pl.BlockSpec((8, 128), lambda i: (0, 0))],
        out_shape=[jax.ShapeDtypeStruct((t_rows, d), F32), jax.ShapeDtypeStruct((8, 128), F32)],
        scratch_shapes=[pltpu.VMEM((1, d), F32)],
        compiler_params=_params(("arbitrary",)),
    )(y, target)


def _adamw_body(p_ref, w_ref, m_ref, v_ref, g_ref, d_ref, nm_ref, nv_ref):
    g = p_ref[0].astype(F32)
    for s in range(1, N_DEV):
        g = g + p_ref[s].astype(F32)
    w = w_ref[...]
    m = ADAM_B1 * m_ref[...] + (1.0 - ADAM_B1) * g
    v = ADAM_B2 * v_ref[...] + (1.0 - ADAM_B2) * (g * g)
    m_hat = m / (1.0 - ADAM_B1 ** ADAM_STEP)
    v_hat = v / (1.0 - ADAM_B2 ** ADAM_STEP)
    g_ref[...] = g
    d_ref[...] = -ADAM_LR * (m_hat / (jnp.sqrt(v_hat) + ADAM_EPS) + ADAM_WD * w)
    nm_ref[...] = m
    nv_ref[...] = v


def adamw(parts, w, m, v, *, name):
    _, r, c = w.shape
    tr = r if r <= 256 else 256
    blk = pl.BlockSpec((None, tr, c), lambda i: (0, i, 0))
    out = jax.ShapeDtypeStruct((1, r, c), F32)
    return pl.pallas_call(
        _adamw_body, name=name, grid=(r // tr,),
        in_specs=[pl.BlockSpec((N_DEV, tr, c), lambda i: (0, i, 0)), blk, blk, blk],
        out_specs=[blk, blk, blk, blk],
        out_shape=[out, out, out, out],
        compiler_params=_params(("arbitrary",)),
    )(parts, w, m, v)


MESH_ID = pl.DeviceIdType.MESH
HBM_SPEC = pl.BlockSpec(memory_space=pl.ANY)


def _my_place():
    return lax.axis_index("x"), lax.axis_index("y"), lax.axis_index("c")


def _flat(x, y, c):
    return 4 * x + 2 * y + c


def _all_gather_body(*refs, n):
    ins, outs = refs[:n], refs[n:2 * n]
    send_sems, recv_sems, local_sems = refs[2 * n:]
    x, y, c = _my_place()
    me, sibling = (x, y, c), (x, y, 1 - c)
    chips = [(1 - x, y), (x, 1 - y), (1 - x, 1 - y)]
    pending = []
    for a in range(n):
        src, out = ins[a], outs[a]

        def copy(k, block, to, from_input=False, a=a, src=src, out=out):
            slot = out.at[_flat(*block)]
            return pltpu.make_async_remote_copy(
                src_ref=src if from_input else slot, dst_ref=slot,
                send_sem=send_sems.at[7 * a + k], recv_sem=recv_sems.at[7 * a + k],
                device_id=to, device_id_type=MESH_ID)

        mine = pltpu.make_async_copy(src, out.at[_flat(*me)], local_sems.at[a])
        mine.start()
        first = [copy(0, me, sibling, True)] + [copy(1 + j, me, (*chip, c), True) for j, chip in enumerate(chips)]
        for cp in first:
            cp.start()
        pending.append((copy, mine, first))
    for copy, mine, first in pending:
        passed = [copy(4 + j, (*chip, c), sibling) for j, chip in enumerate(chips)]
        for j, chip in enumerate(chips):
            copy(1 + j, (*chip, c), me).wait_recv()
            passed[j].start()
        copy(0, sibling, me).wait_recv()
        for j, chip in enumerate(chips):
            copy(4 + j, (*chip, 1 - c), me).wait_recv()
        for cp in first + passed:
            cp.wait_send()
        mine.wait()


def all_gather(shards, *, name):
    n = len(shards)
    return pl.pallas_call(
        functools.partial(_all_gather_body, n=n), name=name,
        in_specs=[HBM_SPEC] * n, out_specs=[HBM_SPEC] * n,
        out_shape=[jax.ShapeDtypeStruct((N_DEV,) + s.shape, s.dtype) for s in shards],
        scratch_shapes=[pltpu.SemaphoreType.DMA((7 * n,)), pltpu.SemaphoreType.DMA((7 * n,)),
                        pltpu.SemaphoreType.DMA((n,))],
    )(*shards)


PEER_FLIPS = [(0, 0, 1), (1, 0, 0), (0, 1, 0), (1, 1, 0), (1, 0, 1), (0, 1, 1), (1, 1, 1)]


def _all_to_all_body(*refs, n):
    ins, outs = refs[:n], refs[n:2 * n]
    send_sems, recv_sems, local_sems = refs[2 * n:]
    x, y, c = _my_place()
    me = _flat(x, y, c)
    waits = []
    for a in range(n):
        src, out = ins[a], outs[a]
        mine = pltpu.make_async_copy(src.at[me], out.at[me], local_sems.at[a])
        mine.start()
        waits.append(mine)
        for k, (fx, fy, fc) in enumerate(PEER_FLIPS):
            peer = (1 - x if fx else x, 1 - y if fy else y, 1 - c if fc else c)
            theirs = _flat(*peer)
            sems = dict(send_sem=send_sems.at[7 * a + k], recv_sem=recv_sems.at[7 * a + k],
                        device_id=peer, device_id_type=MESH_ID)
            send = pltpu.make_async_remote_copy(src_ref=src.at[theirs], dst_ref=out.at[me], **sems)
            send.start()
            recv = pltpu.make_async_remote_copy(src_ref=src.at[theirs], dst_ref=out.at[theirs], **sems)
            waits.append((send, recv))
    for w in waits:
        if isinstance(w, tuple):
            w[0].wait_send()
            w[1].wait_recv()
        else:
            w.wait()


def all_to_all(parts, *, name):
    n = len(parts)
    return pl.pallas_call(
        functools.partial(_all_to_all_body, n=n), name=name,
        in_specs=[HBM_SPEC] * n, out_specs=[HBM_SPEC] * n,
        out_shape=[jax.ShapeDtypeStruct(p.shape, p.dtype) for p in parts],
        scratch_shapes=[pltpu.SemaphoreType.DMA((7 * n,)), pltpu.SemaphoreType.DMA((7 * n,)),
                        pltpu.SemaphoreType.DMA((n,))],
    )(*parts)


HBM_ONLY = pl.BlockSpec(memory_space=pltpu.HBM)
SEM_SPEC = pl.BlockSpec(memory_space=pltpu.SEMAPHORE)
DATAFLOW_EFFECT = pltpu.SideEffectType.DATAFLOW_SIDE_EFFECTING


def _split_copies(srcs, lands, send_sems, recv_sems, n, scatter):
    x, y, c = _my_place()
    me = _flat(x, y, c)
    pairs = []
    for a in range(n):
        for k, (fx, fy, fc) in enumerate(PEER_FLIPS):
            peer = (1 - x if fx else x, 1 - y if fy else y, 1 - c if fc else c)
            theirs = _flat(*peer)
            src = srcs[a].at[theirs] if scatter else srcs[a]
            sems = dict(send_sem=send_sems.at[7 * a + k], recv_sem=recv_sems.at[7 * a + k],
                        device_id=peer, device_id_type=MESH_ID)
            pairs.append((pltpu.make_async_remote_copy(src_ref=src, dst_ref=lands[a].at[me], **sems),
                          pltpu.make_async_remote_copy(src_ref=src, dst_ref=lands[a].at[theirs], **sems)))
    return pairs


def _exchange_start_body(*refs, n, scatter):
    srcs, lands = refs[:n], refs[n:2 * n]
    send_sems, recv_sems = refs[2 * n], refs[2 * n + 1]
    token = refs[-1]
    for send, _ in _split_copies(srcs, lands, send_sems, recv_sems, n, scatter):
        send.start()
    token[...] = jnp.zeros_like(token)


def exchange_start(srcs, lands, *, scatter, name):
    n = len(srcs)
    args = [pltpu.with_memory_space_constraint(t, pltpu.HBM) for t in list(srcs) + list(lands)]
    outs = pl.pallas_call(
        functools.partial(_exchange_start_body, n=n, scatter=scatter), name=name,
        out_shape=(pltpu.SemaphoreType.DMA((7 * n,)), pltpu.SemaphoreType.DMA((7 * n,)),
                   *[pltpu.HBM(t.shape, t.dtype) for t in args],
                   jax.ShapeDtypeStruct((8, 128), F32)),
        in_specs=[HBM_ONLY] * (2 * n),
        out_specs=(SEM_SPEC, SEM_SPEC, *[HBM_ONLY] * (2 * n), pl.BlockSpec(memory_space=pltpu.VMEM)),
        input_output_aliases={i: 2 + i for i in range(2 * n)},
        compiler_params=pltpu.CompilerParams(has_side_effects=DATAFLOW_EFFECT),
    )(*args)
    return outs[0], outs[1], outs[2:2 + n], outs[2 + n:2 + 2 * n], outs[-1]


def _exchange_wait_body(*refs, n, scatter):
    srcs, lands = refs[:n], refs[n:2 * n]
    send_sems, recv_sems = refs[2 * n], refs[2 * n + 1]
    for send, recv in _split_copies(srcs, lands, send_sems, recv_sems, n, scatter):
        send.wait_send()
        recv.wait_recv()


def exchange_wait(send_sems, recv_sems, srcs, lands, after, *, scatter, name):
    n = len(srcs)
    outs = pl.pallas_call(
        functools.partial(_exchange_wait_body, n=n, scatter=scatter), name=name,
        out_shape=tuple(pltpu.HBM(t.shape, t.dtype) for t in list(srcs) + list(lands)),
        in_specs=[HBM_ONLY] * (2 * n) + [SEM_SPEC, SEM_SPEC, HBM_SPEC],
        out_specs=tuple([HBM_ONLY] * (2 * n)),
        input_output_aliases={i: i for i in range(2 * n)},
        compiler_params=pltpu.CompilerParams(has_side_effects=DATAFLOW_EFFECT),
    )(*srcs, *lands, send_sems, recv_sems, after)
    return outs[n:]


def _own_slot_only(shape_dtype, own, slot):
    land = lax.empty(shape_dtype.shape, shape_dtype.dtype)
    return lax.dynamic_update_slice(land, own[None], (slot,) + (0,) * own.ndim)


def _pad_rows(a, rows=8):
    return jnp.pad(a, ((0, rows - a.shape[0]), (0, 0)))


def _gate_rows(a_log, dt_bias):
    z = jnp.zeros((8, 128), F32)
    return z.at[0, A_HEADS:2 * A_HEADS].set(a_log[0]).at[1, A_HEADS:2 * A_HEADS].set(dt_bias[0])


def _pack_small(norm_g, a_log, a_dt_bias, a_norm_g, b_q_norm_g, b_k_norm_g):
    return jnp.concatenate([
        norm_g[0].reshape(8, 128), norm_g[1].reshape(8, 128),
        _gate_rows(a_log, a_dt_bias),
        _pad_rows(a_norm_g[0].reshape(2, 128)),
        _pad_rows(jnp.concatenate([b_q_norm_g[0], b_k_norm_g[0]], axis=0)),
    ], axis=0)


def _unpack_small(p):
    return (p[0:16].reshape(2, D_MODEL), p[16:17, A_HEADS:2 * A_HEADS], p[17:18, A_HEADS:2 * A_HEADS],
            p[24:26].reshape(1, A_DV), p[32:35][None], p[35:38][None])


def kernel(x, positions, norm_g, a_w_in, a_conv_w, a_log, a_dt_bias, a_norm_g, a_w_out, b_w_in, b_q_norm_g, b_k_norm_g, b_w_out, loss_target, m_norm_g, m_a_w_in, m_a_conv_w, m_a_log, m_a_dt_bias, m_a_norm_g, m_a_w_out, m_b_w_in, m_b_q_norm_g, m_b_k_norm_g, m_b_w_out, v_norm_g, v_a_w_in, v_a_conv_w, v_a_log, v_a_dt_bias, v_a_norm_g, v_a_w_out, v_b_w_in, v_b_q_norm_g, v_b_k_norm_g, v_b_w_out):
    n_seq, s_len, d = x.shape
    t_rows = n_seq * s_len
    n_chunks = s_len // CHUNK
    x0 = x.reshape(t_rows, d)
    target = loss_target.reshape(t_rows, d)
    my_slot = _flat(*_my_place())

    g_a_in, g_conv = all_gather([a_w_in[0].astype(BF16), _pad_rows(a_conv_w[0])], name="gather_weights_first")
    later = [a_w_out[0].astype(BF16), b_w_in[0].astype(BF16), b_w_out[0].astype(BF16)]
    lands = [_own_slot_only(jax.ShapeDtypeStruct((N_DEV,) + t.shape, t.dtype), t, my_slot) for t in later]
    w_send, w_recv, later, lands, w_token = exchange_start(later, lands, scatter=False, name="gather_weights_start")
    w_a_in = jnp.pad(g_a_in.transpose(1, 0, 2).reshape(d, A_IN), ((0, 0), (0, A_IN_PAD - A_IN)))
    conv_w8 = g_conv.transpose(1, 0, 2).reshape(8, 2 * A_QK + A_VW)

    gains_model = _pad_rows(norm_g)
    gate_prm = _gate_rows(a_log, a_dt_bias)
    gain_a_out = _pad_rows(a_norm_g)
    gains_qk = _pad_rows(jnp.concatenate([b_q_norm_g[0], b_k_norm_g[0]], axis=0))
    cos_t, sin_t = rope_tables(positions.reshape(t_rows))

    h0, h0_t = rms_fwd(x0, gains_model, 0, after=w_token)
    proj_a = mm_nn(h0, w_a_in, tn=896, name="proj_a")
    gates_col, gates_row = a_gates_fwd(proj_a, gate_prm, n_seq)
    gates_row = gates_row.reshape(n_seq, 2 * A_HEADS, s_len // SUPER, 1, SUPER)
    o_a, tinv, states, og_a, q_a, k_a, v_a = gdn_fwd(proj_a, conv_w8, gates_col, gates_row, gain_a_out, n_seq)
    g_a_out, g_b_in, g_b_out = exchange_wait(w_send, w_recv, later, lands, og_a, scatter=False,
                                             name="gather_weights_wait")
    w_a_out = g_a_out.reshape(A_VW, d)
    w_b_in = g_b_in.transpose(1, 0, 2).reshape(d, B_IN_COLS)
    w_b_out = g_b_out.reshape(B_W, d)
    x1 = mm_nn(og_a, w_a_out, x0, tn=1024, name="out_a")

    h1, h1_t = rms_fwd(x1, gains_model, 1)
    proj_b = mm_nn_pieces(h1, w_b_in, name="proj_b")
    og_b, o_b, lse = attn_fwd(proj_b, cos_t, sin_t, gains_qk, n_seq)
    y = mm_nn(og_b, w_b_out, x1, tn=1024, name="out_b")

    dy, loss_blk = loss_head(y, target)
    loss = lax.psum(loss_blk[0, 0], ("x", "y", "c"))

    d_og_b = mm_nt(dy, w_b_out, tn=1024, name="d_og_b")
    dw_b_out = mm_tn(og_b, dy, tn=256, out_dtype=BF16, name="dw_b_out")
    dqkv_b, dz_b, d_gains_qk = attn_bwd(proj_b, cos_t, sin_t, gains_qk, d_og_b, o_b, lse, n_seq)
    dh1 = mm_nt_pieces(dqkv_b, dz_b, w_b_in, name="dh1")
    dw_b_in = mm_tn_b_in(h1_t, dqkv_b, dz_b, out_dtype=BF16, name="dw_b_in")
    dx1, d_gain1 = rms_bwd(x1, gains_model, 1, dh1, dy)

    dw_a_out = mm_tn(og_a, dx1, tn=256, out_dtype=BF16, name="dw_a_out")
    early = [dw_b_in, dw_b_out.reshape(N_DEV, B_W // N_DEV, d), dw_a_out.reshape(N_DEV, A_VW // N_DEV, d)]
    lands = [_own_slot_only(t, lax.dynamic_index_in_dim(t, my_slot, 0, keepdims=False), my_slot) for t in early]
    g_send, g_recv, early, lands, g_token = exchange_start(early, lands, scatter=True, name="scatter_grads_start")

    d_og_a = mm_nt(dx1, w_a_out, tn=1024, name="d_og_a", after=g_token)
    d_xq, d_xk, d_xv, dgates, dz_a, d_gain_a_out, d_cq, d_ck, d_cv = gdn_bwd(
        q_a, k_a, v_a, gates_col, gates_row, tinv, states, d_og_a, o_a, proj_a, conv_w8, gain_a_out, n_seq)
    d_conv = jnp.concatenate([d_cq.sum(axis=0), d_ck.sum(axis=0), d_cv.sum(axis=0)], axis=1)
    d_gate_logits, d_gate_prm = a_gates_bwd(proj_a, gate_prm, dgates, n_seq)
    dw_a_in = jnp.concatenate([
        mm_tn(h0_t, piece, tn=min(256, piece.shape[1]), out_dtype=BF16, name=f"dw_a_in_{nm}", a_is_transposed=True)
        for nm, piece in (("q", d_xq), ("k", d_xk), ("v", d_xv), ("z", dz_a), ("gates", d_gate_logits))
    ], axis=1)[:, :A_IN]
    shard_a_in = A_IN // N_DEV
    last = [dw_a_in.reshape(d, N_DEV, shard_a_in).transpose(1, 0, 2)]
    last_lands = [_own_slot_only(t, lax.dynamic_index_in_dim(t, my_slot, 0, keepdims=False), my_slot) for t in last]
    l_send, l_recv, last, last_lands, l_token = exchange_start(last, last_lands, scatter=True,
                                                               name="scatter_last_start")
    dh0 = mm_nt_multi([d_xq, d_xk, d_xv, dz_a], w_a_in, tn=1024, name="dh0_qkvz", after=l_token)
    dh0 = mm_nt(d_gate_logits, w_a_in, dh0, tn=128, col_off=A_GATE_COL, name="dh0_gates")
    dx0, d_gain0 = rms_bwd(x0, gains_model, 0, dh0, dx1)

    small = jnp.concatenate([
        d_gain0[0].reshape(8, 128), d_gain1[0].reshape(8, 128), d_gate_prm,
        _pad_rows(d_gain_a_out[0].reshape(2, 128)), d_gains_qk], axis=0)
    r_small, r_conv = all_gather([small, d_conv], name="gather_small_grads")
    conv_cols = a_conv_w.shape[2]
    r_conv = lax.dynamic_slice(r_conv, (0, 0, my_slot * conv_cols), (N_DEV, 8, conv_cols))

    r_b_in, r_b_out, r_a_out = exchange_wait(g_send, g_recv, early, lands, r_small, scatter=True,
                                             name="scatter_grads_wait")
    (r_a_in,) = exchange_wait(l_send, l_recv, last, last_lands, r_small, scatter=True, name="scatter_last_wait")

    upd = {}
    upd["a_w_in"] = adamw(r_a_in, a_w_in, m_a_w_in, v_a_w_in, name="adamw_a_w_in")
    upd["a_w_out"] = adamw(r_a_out, a_w_out, m_a_w_out, v_a_w_out, name="adamw_a_w_out")
    upd["b_w_in"] = adamw(r_b_in, b_w_in, m_b_w_in, v_b_w_in, name="adamw_b_w_in")
    upd["b_w_out"] = adamw(r_b_out, b_w_out, m_b_w_out, v_b_w_out, name="adamw_b_w_out")
    upd["a_conv_w"] = [t[:, :A_CONV] for t in adamw(
        r_conv, _pad_rows(a_conv_w[0])[None], _pad_rows(m_a_conv_w[0])[None], _pad_rows(v_a_conv_w[0])[None],
        name="adamw_a_conv_w")]
    small_upd = adamw(
        r_small,
        _pack_small(norm_g, a_log, a_dt_bias, a_norm_g, b_q_norm_g, b_k_norm_g)[None],
        _pack_small(m_norm_g, m_a_log, m_a_dt_bias, m_a_norm_g, m_b_q_norm_g, m_b_k_norm_g)[None],
        _pack_small(v_norm_g, v_a_log, v_a_dt_bias, v_a_norm_g, v_b_q_norm_g, v_b_k_norm_g)[None],
        name="adamw_small")
    small_names = ("norm_g", "a_log", "a_dt_bias", "a_norm_g", "b_q_norm_g", "b_k_norm_g")
    unpacked = [_unpack_small(t[0]) for t in small_upd]
    for i, nm in enumerate(small_names):
        upd[nm] = [u[i] for u in unpacked]

    order = ("norm_g", "a_w_in", "a_conv_w", "a_log", "a_dt_bias", "a_norm_g", "a_w_out",
             "b_w_in", "b_q_norm_g", "b_k_norm_g", "b_w_out")
    outs = [loss, dx0.reshape(n_seq, s_len, d)]
    for kind in range(4):
        for nm in order:
            outs.append(upd[nm][kind])
    return tuple(outs)
```

```python
import functools
import math

import jax
import jax.numpy as jnp
from jax import lax
from jax.experimental import pallas as pl
from jax.experimental.pallas import tpu as pltpu

F32 = jnp.float32
BF16 = jnp.bfloat16

D_MODEL = 1024
EPS = 1e-6
N_DEV = 8

A_HEADS = 8
A_DK = 128
A_DV = 256
A_QK = A_HEADS * A_DK
A_VW = A_HEADS * A_DV
A_CONV = 4
CHUNK = 64
A_IN = 2 * A_QK + 2 * A_VW + 2 * A_HEADS
A_IN_PAD = 2 * A_QK + 2 * A_VW + 128
A_GATE_COL = (2 * A_QK + 2 * A_VW) // 128

B_DILATIONS = (1, 4, 16)
B_GROUPS = 3
B_HEADS = 8
B_DH = 128
B_W = B_HEADS * B_DH
B_BLOCK = 128
B_PIECES = 3 * B_GROUPS + 1
ROPE_THETA = 500000.0
ROPE_DIMS = B_DH // 4
ROPE_HALF = ROPE_DIMS // 2

ADAM_LR = 0.001
ADAM_B1 = 0.9
ADAM_B2 = 0.999
ADAM_EPS = 1e-08
ADAM_WD = 0.01
ADAM_STEP = 10

VMEM_LIMIT = 60 * 1024 * 1024


def _params(sem):
    return pltpu.CompilerParams(dimension_semantics=sem, vmem_limit_bytes=VMEM_LIMIT)


def _bf(x):
    return x.astype(BF16)


def _mm(a, b):
    return jnp.dot(_bf(a), _bf(b), preferred_element_type=F32)


def _mm_nt(a, b):
    return lax.dot_general(_bf(a), _bf(b), (((1,), (1,)), ((), ())), preferred_element_type=F32)


def _mm_tn(a, b):
    return lax.dot_general(_bf(a), _bf(b), (((0,), (0,)), ((), ())), preferred_element_type=F32)


def _split(x):
    hi = _bf(x)
    return hi, _bf(x - hi.astype(F32))


def _mm3(a, b):
    ah, al = _split(a)
    bh, bl = _split(b)
    d = functools.partial(jnp.dot, preferred_element_type=F32)
    return d(ah, bh) + (d(ah, bl) + d(al, bh))


def _colsum_as_col(z):
    zh, zl = _split(z)
    ones = jnp.ones((z.shape[0], 128), BF16)
    tn = functools.partial(lax.dot_general, dimension_numbers=(((0,), (0,)), ((), ())),
                           preferred_element_type=F32)
    return (tn(zh, ones) + tn(zl, ones))[:, 0:1]


def _sigmoid(x):
    return 0.5 * jnp.tanh(0.5 * x) + 0.5


INV_BASE = 8
INV_NEWTON = 2
GDN_GROUP = 4
SUPER = GDN_GROUP * CHUNK
GDN_WIDTH = 2

A_K_COL = A_QK // A_DK
A_V_COL = 2 * A_QK // A_DV


def _inverse_steps(m, row, col):
    eye = (row == col).astype(F32)
    d = jnp.where(row // INV_BASE == col // INV_BASE, m, 0.0)
    x = eye - d
    p = _mm(d, d)
    yield
    steps = int(math.log2(INV_BASE)) - 1
    for i in range(steps):
        x = x + _mm(x, p)
        if i + 1 < steps:
            p = _mm(p, p)
        yield
    size = INV_BASE
    while size < CHUNK:
        c = jnp.where((row // (2 * size) == col // (2 * size)) & (row // size != col // size), m, 0.0)
        xc = _mm(x, c)
        yield
        x = x - _mm(xc, x)
        yield
        size *= 2
    for _ in range(INV_NEWTON):
        r = eye - x - _mm3(m, x)
        yield
        x = x + _mm(x, r)
        yield
    return x


def _drain(gen):
    while True:
        try:
            next(gen)
        except StopIteration as stop:
            return stop.value


def _interleave(*gens):
    live = list(gens)
    while live:
        for g in list(live):
            try:
                next(g)
            except StopIteration:
                live.remove(g)


def _diag_blocks_tall(x):
    return jnp.concatenate([x[i * CHUNK:(i + 1) * CHUNK, i * CHUNK:(i + 1) * CHUNK] for i in range(GDN_GROUP)], axis=0)


def _tall_to_block_diag(t, same):
    return jnp.where(same, jnp.concatenate([t] * GDN_GROUP, axis=1), 0.0)


def _block_sum(same, x):
    xh, xl = _split(jnp.broadcast_to(x, (SUPER, 128)))
    ones = same.astype(BF16)
    d = functools.partial(jnp.dot, preferred_element_type=F32)
    return (d(ones, xh) + d(ones, xl))[:, 0:1]


def _aligned_rows(index, size):
    start = index * size
    return pl.ds(start if isinstance(start, int) else pl.multiple_of(start, size), size)


def _super_rows(i):
    return _aligned_rows(i, SUPER)


def _chunk_rows(n):
    return _aligned_rows(n, CHUNK)


def _gdn_super_steps(q, k, v, gcb, gr, head, tinv_tall=None):
    lane = lax.broadcasted_iota(jnp.int32, (SUPER, 128), 1)
    row = lax.broadcasted_iota(jnp.int32, (SUPER, SUPER), 0)
    col = lax.broadcasted_iota(jnp.int32, (SUPER, SUPER), 1)
    same = row // CHUNK == col // CHUNK
    beta = jnp.sum(jnp.where(lane == head, gcb, 0.0), axis=1, keepdims=True)
    gc = jnp.sum(jnp.where(lane == A_HEADS + head, gcb, 0.0), axis=1, keepdims=True)
    g_last = jnp.sum(jnp.where(col == (row // CHUNK) * CHUNK + (CHUNK - 1), gr, 0.0), axis=1, keepdims=True)
    gamma = jnp.exp(gc)
    decay = jnp.where(same & (row >= col), jnp.exp(jnp.minimum(gc - gr, 0.0)), 0.0)
    kb = k * beta
    m = jnp.where(same & (row > col), _mm_nt(kb, k) * decay, 0.0)
    p = jnp.where(same & (row >= col), _mm_nt(q, k) * decay, 0.0)
    yield
    if tinv_tall is None:
        tinv = yield from _inverse_steps(m, row, col)
    else:
        tinv = _tall_to_block_diag(tinv_tall, same)
    u = _mm(tinv, v * beta)
    w = _mm(tinv, kb * gamma)
    yield
    e_tail = jnp.exp(g_last - gc)
    return dict(beta=beta, gc=gc, g_last=g_last, gamma=gamma, decay=decay, kb=kb, m=m,
                tinv=tinv, u=u, w=w, p=p, e_tail=e_tail, row=row, col=col, lane=lane, same=same)


def _gdn_super_common(q, k, v, gcb, gr, head, tinv_tall=None):
    return _drain(_gdn_super_steps(q, k, v, gcb, gr, head, tinv_tall))


def _store_scan_operands(rows, q, k, t, u_scr, w_scr, p_scr, qg_scr, ke_scr, gl_scr):
    u_scr[rows, :] = t["u"]
    w_scr[rows, :] = _bf(t["w"])
    p_scr[rows, :] = _bf(_diag_blocks_tall(t["p"]))
    qg_scr[rows, :] = _bf(q * t["gamma"])
    ke_scr[rows, :] = _bf(k * t["e_tail"])
    gl_scr[rows, :] = jnp.broadcast_to(jnp.exp(t["g_last"]), (SUPER, 128))


def _gdn_fwd_body(q_ref, k_ref, v_ref, gc_ref, gr_ref, wq_ref, wk_ref, wv_ref, z_ref, gn_ref,
                  o_ref, tinv_ref, st_ref, og_ref, qo_ref, ko_ref, vo_ref, s_scr, *sets):
    head = pl.program_id(1)
    n_super = q_ref.shape[0] // SUPER
    all_sets = [sets[6 * i:6 * i + 6] for i in range(2 * GDN_WIDTH)]
    whole = pl.ds(0, SUPER)

    def conv_silu(x_ref, w_ref, i):
        rows = _super_rows(i)
        x, w = x_ref[rows, :], w_ref[...]
        halo = jnp.zeros((8, x.shape[1]), F32) if i == 0 else x_ref[pl.ds(i * SUPER - 8, 8), :]
        ext = jnp.concatenate([halo, x], axis=0)
        c = x * w[A_CONV - 1:A_CONV, :]
        for j in range(1, A_CONV):
            c = c + pltpu.roll(ext, j, 0)[8:, :] * w[A_CONV - 1 - j:A_CONV - j, :]
        return c * _sigmoid(c)

    def unit(a):
        return a * lax.rsqrt(jnp.sum(a * a, axis=1, keepdims=True) + EPS)

    def prepare_steps(i, dst):
        rows = _super_rows(i)
        q = unit(conv_silu(q_ref, wq_ref, i)) * A_DK ** -0.5
        k = unit(conv_silu(k_ref, wk_ref, i))
        v = conv_silu(v_ref, wv_ref, i)
        qo_ref[rows, :], ko_ref[rows, :], vo_ref[rows, :] = q, k, v
        t = yield from _gdn_super_steps(q, k, v, gc_ref[rows, :], gr_ref[i], head)
        tinv_ref[rows, :] = _diag_blocks_tall(t["tinv"])
        _store_scan_operands(whole, q, k, t, *dst)

    def scan_steps(i, src):
        u_scr, w_scr, p_scr, qg_scr, ke_scr, gl_scr = src
        for j in range(GDN_GROUP):
            n = i * GDN_GROUP + j
            local = pl.ds(j * CHUNK, CHUNK)
            s = s_scr[...]
            sb = _bf(s)
            st_ref[n] = sb
            ws = jnp.dot(w_scr[local, :], sb, preferred_element_type=F32)
            yield
            vb = _bf(u_scr[local, :] - ws)
            o = (jnp.dot(qg_scr[local, :], sb, preferred_element_type=F32)
                 + jnp.dot(p_scr[local, :], vb, preferred_element_type=F32))
            s_new = s * gl_scr[local, :][0:1, 0:1] + lax.dot_general(
                ke_scr[local, :], vb, (((0,), (0,)), ((), ())), preferred_element_type=F32)
            yield
            rows = _chunk_rows(n)
            o_ref[rows, :] = o
            s_scr[...] = s_new
            silu, _ = _silu_parts(z_ref[rows, :])
            r = lax.rsqrt(jnp.mean(o * o, axis=1, keepdims=True) + EPS)
            og_ref[rows, :] = ((o * r * gn_ref[0:1, :]) * silu).astype(og_ref.dtype)

    def scan_many(first, srcs):
        for j, src in enumerate(srcs):
            yield from scan_steps(first + j, src)

    groups = [all_sets[:GDN_WIDTH], all_sets[GDN_WIDTH:]]
    _interleave(*[prepare_steps(j, groups[0][j]) for j in range(GDN_WIDTH)])
    s_scr[...] = jnp.zeros_like(s_scr)
    for g in range(n_super // GDN_WIDTH):
        cur, nxt = groups[g % 2], groups[(g + 1) % 2]
        first = g * GDN_WIDTH
        following = [prepare_steps(first + GDN_WIDTH + j, nxt[j]) for j in range(GDN_WIDTH)
                     if first + GDN_WIDTH + j < n_super]
        _interleave(scan_many(first, cur), *following)


def _gdn_in_specs(s_len, n_super, from_proj):
    k_col, v_col = (A_K_COL, A_V_COL) if from_proj else (0, 0)
    return [
        pl.BlockSpec((s_len, A_DK), lambda b, h: (b, h)),
        pl.BlockSpec((s_len, A_DK), lambda b, h: (b, k_col + h)),
        pl.BlockSpec((s_len, A_DV), lambda b, h: (b, v_col + h)),
        pl.BlockSpec((s_len, 128), lambda b, h: (b, 0)),
        pl.BlockSpec((None, None, n_super, 1, SUPER), lambda b, h: (b, A_HEADS + h, 0, 0, 0)),
    ]


def _gdn_scan_scratch(s_len):
    return [pltpu.VMEM((A_DK, A_DV), F32), pltpu.VMEM((s_len, A_DV), F32),
            pltpu.VMEM((s_len, A_DK), BF16), pltpu.VMEM((s_len, CHUNK), BF16),
            pltpu.VMEM((s_len, A_DK), BF16), pltpu.VMEM((s_len, A_DK), BF16),
            pltpu.VMEM((s_len, 128), F32)]


def gdn_fwd(proj_a, conv_w8, gates_col, gates_row, norm_g8, n_seq):
    t_rows = proj_a.shape[0]
    s_len = t_rows // n_seq
    n_chunks = s_len // CHUNK
    qk_spec = pl.BlockSpec((s_len, A_DK), lambda b, h: (b, h))
    v_spec = pl.BlockSpec((s_len, A_DV), lambda b, h: (b, h))
    return pl.pallas_call(
        _gdn_fwd_body, name="gdn_fwd", grid=(n_seq, A_HEADS),
        in_specs=_gdn_in_specs(s_len, s_len // SUPER, True) + [
            pl.BlockSpec((8, A_DK), lambda b, h: (0, h)),
            pl.BlockSpec((8, A_DK), lambda b, h: (0, A_K_COL + h)),
            pl.BlockSpec((8, A_DV), lambda b, h: (0, A_V_COL + h)),
            pl.BlockSpec((s_len, A_DV), lambda b, h: (b, A_Z_COL + h)),
            pl.BlockSpec((8, A_DV), lambda b, h: (0, 0)),
        ],
        out_specs=[
            v_spec,
            pl.BlockSpec((s_len, CHUNK), lambda b, h: (b * A_HEADS + h, 0)),
            pl.BlockSpec((None, n_chunks, A_DK, A_DV), lambda b, h: (b * A_HEADS + h, 0, 0, 0)),
            v_spec, qk_spec, qk_spec, v_spec,
        ],
        out_shape=[
            jax.ShapeDtypeStruct((t_rows, A_VW), F32),
            jax.ShapeDtypeStruct((n_seq * A_HEADS * s_len, CHUNK), F32),
            jax.ShapeDtypeStruct((n_seq * A_HEADS, n_chunks, A_DK, A_DV), BF16),
            jax.ShapeDtypeStruct((t_rows, A_VW), BF16),
            jax.ShapeDtypeStruct((t_rows, A_QK), F32),
            jax.ShapeDtypeStruct((t_rows, A_QK), F32),
            jax.ShapeDtypeStruct((t_rows, A_VW), F32),
        ],
        scratch_shapes=_gdn_scan_scratch(SUPER) + (2 * GDN_WIDTH - 1) * _gdn_scan_scratch(SUPER)[1:],
        compiler_params=_params(("arbitrary", "arbitrary")),
    )(proj_a, proj_a, proj_a, gates_col, gates_row, conv_w8, conv_w8, conv_w8, proj_a, norm_g8)


def _gdn_bwd_body(q_ref, k_ref, v_ref, gc_ref, gr_ref, tinv_ref, st_ref, dog_ref, oa_ref, z_ref, gn_ref,
                  xq_ref, xk_ref, xv_ref, wq_ref, wk_ref, wv_ref,
                  dxq_ref, dxk_ref, dxv_ref, dgc_ref, dz_ref, dgn_ref, dwq_ref, dwk_ref, dwv_ref,
                  ds_scr, cq_scr, ck_scr, cv_scr, *sets):
    head = pl.program_id(1)
    n_super = q_ref.shape[0] // SUPER
    ops = (sets[0:7], sets[7:14])
    res = (sets[14:21], sets[21:28])
    whole = pl.ds(0, SUPER)
    tn = functools.partial(lax.dot_general, dimension_numbers=(((0,), (0,)), ((), ())), preferred_element_type=F32)
    nt = functools.partial(lax.dot_general, dimension_numbers=(((1,), (1,)), ((), ())), preferred_element_type=F32)

    @pl.when(head == 0)
    def _():
        dgc_ref[...] = jnp.zeros_like(dgc_ref)

    @pl.when((head == 0) & (pl.program_id(0) == 0))
    def _():
        dgn_ref[...] = jnp.zeros_like(dgn_ref)

    carry = (cq_scr, ck_scr, cv_scr)
    for ref in carry + (dwq_ref, dwk_ref, dwv_ref):
        ref[...] = jnp.zeros_like(ref)

    def common_steps(i):
        rows = _super_rows(i)
        q, k, v = q_ref[rows, :], k_ref[rows, :], v_ref[rows, :]
        t = yield from _gdn_super_steps(q, k, v, gc_ref[rows, :], gr_ref[i], head, tinv_tall=tinv_ref[rows, :])
        return rows, q, k, v, t

    def stage_p(i, parity):
        rows, q, k, _, t = yield from common_steps(i)
        _store_scan_operands(whole, q, k, t, *ops[parity][:6])
        o, d_og, gain = oa_ref[rows, :], dog_ref[rows, :], gn_ref[0:1, :]
        r = lax.rsqrt(jnp.mean(o * o, axis=1, keepdims=True) + EPS)
        silu, dsilu = _silu_parts(z_ref[rows, :])
        xr = o * r
        d_on = d_og * silu
        dz_ref[rows, :] = (d_og * (xr * gain) * dsilu).astype(dz_ref.dtype)
        u = d_on * gain
        ops[parity][6][...] = r * u - xr * (r * r) * jnp.mean(u * o, axis=1, keepdims=True)
        dgn_ref[0:1, :] += jnp.sum(d_on * xr, axis=0, keepdims=True)

    def stage_s(i, parity):
        u_scr, w_scr, p_scr, qg_scr, ke_scr, gl_scr, do_scr = ops[parity]
        vn_scr, dvn_scr, dqg_scr, dw_scr, dkt_scr, sds_scr, dof_scr = res[parity]
        for j in reversed(range(GDN_GROUP)):
            n = i * GDN_GROUP + j
            local = pl.ds(j * CHUNK, CHUNK)
            ds_next = ds_scr[...]
            dsb = _bf(ds_next)
            sb = st_ref[n]
            s = sb.astype(F32)
            d_o = do_scr[local, :]
            dof_scr[local, :] = d_o
            d_ob = _bf(d_o)
            w_s = jnp.dot(w_scr[local, :], sb, preferred_element_type=F32)
            d_vn = tn(p_scr[local, :], d_ob) + jnp.dot(ke_scr[local, :], dsb, preferred_element_type=F32)
            d_qg = nt(d_ob, sb)
            qg_do = tn(qg_scr[local, :], d_ob)
            yield
            v_new = u_scr[local, :] - w_s
            d_vnb = _bf(d_vn)
            d_w = -nt(d_vnb, sb)
            d_kt = nt(_bf(v_new), dsb)
            w_dvn = tn(w_scr[local, :], d_vnb)
            yield
            vn_scr[local, :] = v_new
            dvn_scr[local, :] = d_vn
            dqg_scr[local, :] = d_qg
            dw_scr[local, :] = d_w
            dkt_scr[local, :] = d_kt
            sds = jnp.sum(jnp.sum(s * ds_next, axis=1, keepdims=True), axis=0, keepdims=True)
            sds_scr[local, :] = jnp.broadcast_to(sds, (CHUNK, 128))
            ds_scr[...] = qg_do + gl_scr[local, :][0:1, 0:1] * ds_next - w_dvn

    def conv_bwd(i, rows, x_ref, w_ref, dy, norm_scale, dx_ref, dw_ref, dc_above):
        x, w = x_ref[rows, :], w_ref[...]
        above = x_ref[pl.ds(pl.multiple_of(jnp.maximum(i * SUPER - 8, 0), 8), 8), :]
        ext = jnp.concatenate([jnp.where(i > 0, above, 0.0), x], axis=0)
        c = x * w[A_CONV - 1:A_CONV, :]
        for j in range(1, A_CONV):
            c = c + pltpu.roll(ext, j, 0)[8:, :] * w[A_CONV - 1 - j:A_CONV - j, :]
        sig = _sigmoid(c)
        a = c * sig
        if norm_scale is None:
            da = dy
        else:
            rn = lax.rsqrt(jnp.sum(a * a, axis=1, keepdims=True) + EPS)
            da = norm_scale * (rn * dy - a * (rn * rn * rn) * jnp.sum(dy * a, axis=1, keepdims=True))
        dc = da * (sig * (1.0 + c * (1.0 - sig)))
        ext_dc = jnp.concatenate([dc, dc_above[...]], axis=0)
        dc_above[...] = dc[0:8, :]
        dx = dc * w[A_CONV - 1:A_CONV, :]
        dw_ref[A_CONV - 1:A_CONV, :] += jnp.sum(dc * x, axis=0, keepdims=True)
        for j in range(1, A_CONV):
            dcs = pltpu.roll(ext_dc, SUPER + 8 - j, 0)[:SUPER, :]
            dx = dx + dcs * w[A_CONV - 1 - j:A_CONV - j, :]
            dw_ref[A_CONV - 1 - j:A_CONV - j, :] += jnp.sum(dcs * x, axis=0, keepdims=True)
        dx_ref[rows, :] = dx.astype(dx_ref.dtype)

    def stage_f(i, parity):
        vn_scr, dvn_scr, dqg_scr, dw_scr, dkt_scr, sds_scr, dof_scr = res[parity]
        rows, q, k, v, t = yield from common_steps(i)
        beta, gamma, decay, kb, e_tail = t["beta"], t["gamma"], t["decay"], t["kb"], t["e_tail"]
        row, col, lane, same = t["row"], t["col"], t["lane"], t["same"]
        d_o = dof_scr[...]
        v_new, d_vn = vn_scr[...], dvn_scr[...]
        d_qg, d_w, d_kt = dqg_scr[...], dw_scr[...], dkt_scr[...]
        gamma_last = jnp.exp(t["g_last"])

        d_p = jnp.where(same & (row >= col), _mm_nt(d_o, v_new), 0.0)
        d_ru = _mm_tn(t["tinv"], d_vn)
        d_rw = _mm_tn(t["tinv"], d_w)
        yield
        d_m = jnp.where(same & (row > col), -(_mm_nt(d_ru, t["u"]) + _mm_nt(d_rw, t["w"])), 0.0)
        yield

        x_p = d_p * decay
        y_m = d_m * decay
        d_kb = _mm(y_m, k) + d_rw * gamma
        d_q = _mm(x_p, k) + d_qg * gamma
        d_k = _mm_tn(x_p, q) + _mm_tn(y_m, kb) + d_kb * beta + d_kt * e_tail
        d_v = d_ru * beta
        conv_bwd(i, rows, xq_ref, wq_ref, d_q, A_DK ** -0.5, dxq_ref, dwq_ref, carry[0])
        conv_bwd(i, rows, xk_ref, wk_ref, d_k, 1.0, dxk_ref, dwk_ref, carry[1])
        conv_bwd(i, rows, xv_ref, wv_ref, d_v, None, dxv_ref, dwv_ref, carry[2])

        d_beta = (jnp.sum(d_ru * v, axis=1, keepdims=True)
                  + jnp.sum(d_kb * k, axis=1, keepdims=True))
        z = d_p * t["p"] + d_m * t["m"]
        eps_tail = jnp.sum(d_kt * k, axis=1, keepdims=True) * e_tail
        d_gc = (jnp.sum(z, axis=1, keepdims=True) - _colsum_as_col(z)
                + jnp.sum(d_qg * q, axis=1, keepdims=True) * gamma
                + jnp.sum(d_rw * kb, axis=1, keepdims=True) * gamma
                - eps_tail)
        d_glast = _block_sum(same, eps_tail) + gamma_last * sds_scr[...][:, 0:1]
        yield
        rcol = lax.broadcasted_iota(jnp.int32, (SUPER, 1), 0)
        d_gc = d_gc + jnp.where(rcol % CHUNK == CHUNK - 1, d_glast, 0.0)
        dgc_ref[rows, :] += (jnp.where(lane == head, d_beta, 0.0)
                             + jnp.where(lane == A_HEADS + head, d_gc, 0.0))

    last = n_super - 1
    _drain(stage_p(last, 1))
    ds_scr[...] = jnp.zeros_like(ds_scr)
    _interleave(stage_s(last, 1), stage_p(last - 1, 0))

    def pair(k, carry):
        i = last - 1 - 2 * k
        _interleave(stage_s(i, 0), stage_f(i + 1, 1), stage_p(i - 1, 1))
        _interleave(stage_s(i - 1, 1), stage_f(i, 0), stage_p(i - 2, 0))
        return carry

    lax.fori_loop(0, n_super // 2 - 1, pair, 0)
    _interleave(stage_s(0, 0), stage_f(1, 1))
    _drain(stage_f(0, 0))


def gdn_bwd(q, k, v, gates_col, gates_row, tinv, states, d_og, o, proj_a, conv_w8, norm_g8, n_seq):
    t_rows = q.shape[0]
    s_len = t_rows // n_seq
    n_chunks = s_len // CHUNK
    qk_spec = pl.BlockSpec((s_len, A_DK), lambda b, h: (b, h))
    v_spec = pl.BlockSpec((s_len, A_DV), lambda b, h: (b, h))
    gate_spec = pl.BlockSpec((s_len, 128), lambda b, h: (b, 0))
    gain_spec = pl.BlockSpec((8, A_DV), lambda b, h: (0, 0))
    dw_qk_spec = pl.BlockSpec((None, 8, A_DK), lambda b, h: (b, 0, h))
    dw_v_spec = pl.BlockSpec((None, 8, A_DV), lambda b, h: (b, 0, h))
    ops_set = _gdn_scan_scratch(SUPER)[1:] + [pltpu.VMEM((SUPER, A_DV), F32)]
    res_set = [pltpu.VMEM((SUPER, A_DV), F32), pltpu.VMEM((SUPER, A_DV), F32),
               pltpu.VMEM((SUPER, A_DK), F32), pltpu.VMEM((SUPER, A_DK), F32),
               pltpu.VMEM((SUPER, A_DK), F32), pltpu.VMEM((SUPER, 128), F32), pltpu.VMEM((SUPER, A_DV), F32)]
    return pl.pallas_call(
        _gdn_bwd_body, name="gdn_bwd", grid=(n_seq, A_HEADS),
        in_specs=_gdn_in_specs(s_len, s_len // SUPER, False) + [
            pl.BlockSpec((s_len, CHUNK), lambda b, h: (b * A_HEADS + h, 0)),
            pl.BlockSpec((None, n_chunks, A_DK, A_DV), lambda b, h: (b * A_HEADS + h, 0, 0, 0)),
            v_spec, v_spec,
            pl.BlockSpec((s_len, A_DV), lambda b, h: (b, A_Z_COL + h)),
            gain_spec,
            pl.BlockSpec((s_len, A_DK), lambda b, h: (b, h)),
            pl.BlockSpec((s_len, A_DK), lambda b, h: (b, A_K_COL + h)),
            pl.BlockSpec((s_len, A_DV), lambda b, h: (b, A_V_COL + h)),
            pl.BlockSpec((8, A_DK), lambda b, h: (0, h)),
            pl.BlockSpec((8, A_DK), lambda b, h: (0, A_K_COL + h)),
            pl.BlockSpec((8, A_DV), lambda b, h: (0, A_V_COL + h)),
        ],
        out_specs=[qk_spec, qk_spec, v_spec, gate_spec, v_spec, gain_spec, dw_qk_spec, dw_qk_spec, dw_v_spec],
        out_shape=[
            jax.ShapeDtypeStruct((t_rows, A_QK), BF16),
            jax.ShapeDtypeStruct((t_rows, A_QK), BF16),
            jax.ShapeDtypeStruct((t_rows, A_VW), BF16),
            jax.ShapeDtypeStruct((t_rows, 128), F32),
            jax.ShapeDtypeStruct((t_rows, A_VW), BF16),
            jax.ShapeDtypeStruct((8, A_DV), F32),
            jax.ShapeDtypeStruct((n_seq, 8, A_QK), F32),
            jax.ShapeDtypeStruct((n_seq, 8, A_QK), F32),
            jax.ShapeDtypeStruct((n_seq, 8, A_VW), F32),
        ],
        scratch_shapes=(_gdn_scan_scratch(SUPER)[:1]
                        + [pltpu.VMEM((8, A_DK), F32), pltpu.VMEM((8, A_DK), F32), pltpu.VMEM((8, A_DV), F32)]
                        + 2 * ops_set + 2 * res_set),
        compiler_params=_params(("arbitrary", "arbitrary")),
    )(q, k, v, gates_col, gates_row, tinv, states, d_og, o, proj_a, norm_g8,
      proj_a, proj_a, proj_a, conv_w8, conv_w8, conv_w8)


GATE_TILE = 512


def _softplus(y):
    return jnp.maximum(y, 0.0) + jnp.log1p(jnp.exp(-jnp.abs(y)))


def _gate_values(x, prm):
    beta = _sigmoid(x)
    y = x + prm[1:2, :]
    neg_a = -jnp.exp(prm[0:1, :])
    g = neg_a * _softplus(y)
    return beta, y, neg_a, g


def _a_gates_fwd_body(x_ref, prm_ref, gc_ref, gr_ref):
    x = x_ref[...]
    tm = x.shape[0]
    beta, _, _, g = _gate_values(x, prm_ref[...])
    in_chunk = lax.broadcasted_iota(jnp.int32, (tm, 1), 0) % CHUNK
    s = 1
    while s < CHUNK:
        g = g + jnp.where(in_chunk >= s, pltpu.roll(g, s, 0), 0.0)
        s *= 2
    lane = lax.broadcasted_iota(jnp.int32, x.shape, 1)
    out = jnp.where(lane < A_HEADS, beta, jnp.where(lane < 2 * A_HEADS, g, 0.0))
    gc_ref[...] = out
    gr_ref[...] = out.T[0:2 * A_HEADS, :]


def a_gates_fwd(proj_a, prm, n_seq):
    t_rows = proj_a.shape[0]
    s_len = t_rows // n_seq
    tm = min(GATE_TILE, s_len)
    n_t = s_len // tm
    return pl.pallas_call(
        _a_gates_fwd_body, name="a_gates_fwd", grid=(n_seq, n_t),
        in_specs=[pl.BlockSpec((tm, 128), lambda b, i: (b * n_t + i, A_GATE_COL)),
                  pl.BlockSpec((8, 128), lambda b, i: (0, 0))],
        out_specs=[pl.BlockSpec((tm, 128), lambda b, i: (b * n_t + i, 0)),
                   pl.BlockSpec((None, 2 * A_HEADS, tm), lambda b, i: (b, 0, i))],
        out_shape=[jax.ShapeDtypeStruct((t_rows, 128), F32),
                   jax.ShapeDtypeStruct((n_seq, 2 * A_HEADS, s_len), F32)],
        compiler_params=_params(("arbitrary", "arbitrary")),
    )(proj_a, prm)


def _a_gates_bwd_body(x_ref, prm_ref, dgc_ref, dx_ref, dprm_ref):
    first = (pl.program_id(0) == 0) & (pl.program_id(1) == 0)
    x = x_ref[...]
    tm = x.shape[0]
    beta, y, neg_a, g = _gate_values(x, prm_ref[...])
    d = dgc_ref[...]
    in_chunk = lax.broadcasted_iota(jnp.int32, (tm, 1), 0) % CHUNK
    dg = d
    s = 1
    while s < CHUNK:
        dg = dg + jnp.where(in_chunk < CHUNK - s, pltpu.roll(dg, tm - s, 0), 0.0)
        s *= 2
    lane = lax.broadcasted_iota(jnp.int32, x.shape, 1)
    is_decay = (lane >= A_HEADS) & (lane < 2 * A_HEADS)
    d_alogit = jnp.where(is_decay, dg * neg_a * _sigmoid(y), 0.0)
    dx_ref[...] = jnp.where(lane < A_HEADS, d * beta * (1.0 - beta), d_alogit).astype(dx_ref.dtype)

    @pl.when(first)
    def _():
        dprm_ref[...] = jnp.zeros_like(dprm_ref)

    dprm_ref[0:1, :] += jnp.sum(jnp.where(is_decay, dg * g, 0.0), axis=0, keepdims=True)
    dprm_ref[1:2, :] += jnp.sum(d_alogit, axis=0, keepdims=True)


def a_gates_bwd(proj_a, prm, dgates_col, n_seq):
    t_rows = proj_a.shape[0]
    s_len = t_rows // n_seq
    tm = min(GATE_TILE, s_len)
    n_t = s_len // tm
    return pl.pallas_call(
        _a_gates_bwd_body, name="a_gates_bwd", grid=(n_seq, n_t),
        in_specs=[pl.BlockSpec((tm, 128), lambda b, i: (b * n_t + i, A_GATE_COL)),
                  pl.BlockSpec((8, 128), lambda b, i: (0, 0)),
                  pl.BlockSpec((tm, 128), lambda b, i: (b * n_t + i, 0))],
        out_specs=[pl.BlockSpec((tm, 128), lambda b, i: (b * n_t + i, 0)),
                   pl.BlockSpec((8, 128), lambda b, i: (0, 0))],
        out_shape=[jax.ShapeDtypeStruct((t_rows, 128), BF16),
                   jax.ShapeDtypeStruct((8, 128), F32)],
        compiler_params=_params(("arbitrary", "arbitrary")),
    )(proj_a, prm, dgates_col)


ROW_TILE = 512
A_Z_COL = (2 * A_QK + A_VW) // A_DV


def _silu_parts(z):
    sig = _sigmoid(z)
    return z * sig, sig * (1.0 + z * (1.0 - sig))


NEG_BIG = -1e30
ATT_SCALE = B_DH ** -0.5


def _swap_rope_halves(x):
    src = lax.broadcasted_iota(jnp.int32, (B_DH, B_DH), 0)
    dst = lax.broadcasted_iota(jnp.int32, (B_DH, B_DH), 1)
    pick = ((dst < ROPE_HALF) & (src == dst + ROPE_HALF)) | (
        (dst >= ROPE_HALF) & (dst < ROPE_DIMS) & (src == dst - ROPE_HALF))
    return jnp.dot(_bf(x), pick.astype(BF16), preferred_element_type=F32)


def _norm_rope(x, gain, cos_t, sin_t):
    r = lax.rsqrt(jnp.mean(x * x, axis=1, keepdims=True) + EPS)
    xn = x * r * gain
    return xn * cos_t + _swap_rope_halves(xn) * sin_t, r


def _norm_rope_bwd(x, r, gain, cos_t, sin_t, dy):
    d_xn = dy * cos_t + _swap_rope_halves(dy * sin_t)
    xr = x * r
    u = d_xn * gain
    dx = r * u - xr * (r * r) * jnp.mean(u * x, axis=1, keepdims=True)
    return dx, jnp.sum(d_xn * xr, axis=0, keepdims=True)


def _stream_rows(idx, dilation, s_len):
    nb = s_len // dilation // B_BLOCK
    r = idx // nb
    m = idx % nb
    cur = r + m * (B_BLOCK * dilation)
    prev = r + jnp.maximum(m - 1, 0) * (B_BLOCK * dilation)
    return cur, prev, m > 0


def _rows(start, dilation):
    if dilation == 1:
        return pl.ds(start, B_BLOCK)
    return pl.ds(start, B_BLOCK, stride=dilation)


ATT_UNROLL = 16


def _band_mask(has_prev):
    qi = lax.broadcasted_iota(jnp.int32, (B_BLOCK, 2 * B_BLOCK), 0)
    kj = lax.broadcasted_iota(jnp.int32, (B_BLOCK, 2 * B_BLOCK), 1)
    return ((kj < B_BLOCK) & (kj >= qi) & has_prev) | ((kj >= B_BLOCK) & (kj - B_BLOCK <= qi))


def _attn_fwd_body(qkv_ref, z_ref, cos_ref, sin_ref, gain_ref, og_ref, o_ref, lse_ref,
                   qn_scr, kn_scr, og_scr, lg_scr):
    head, grp = pl.program_id(1), pl.program_id(2)
    s_len = z_ref.shape[0]
    n_blocks = s_len // B_BLOCK
    cos_t, sin_t = cos_ref[...], sin_ref[...]

    for gi, dil in enumerate(B_DILATIONS):
        @pl.when(grp == gi)
        def _(gi=gi, dil=dil):
            qn_scr[...], _ = _norm_rope(qkv_ref[0], gain_ref[gi:gi + 1, :], cos_t, sin_t)
            kn_scr[...], _ = _norm_rope(qkv_ref[1], gain_ref[B_GROUPS + gi:B_GROUPS + gi + 1, :], cos_t, sin_t)

            ones = jnp.ones((2 * B_BLOCK, B_DH), BF16)

            def blocks(it, carry):
                scored = []
                for j in range(ATT_UNROLL):
                    cur, prev, has_prev = _stream_rows(it * ATT_UNROLL + j, dil, s_len)
                    rc, rp = _rows(cur, dil), _rows(prev, dil)
                    k2 = jnp.concatenate([kn_scr[rp, :], kn_scr[rc, :]], axis=0)
                    scored.append((rc, rp, has_prev, _mm_nt(qn_scr[rc, :], k2) * ATT_SCALE))
                summed = []
                for rc, rp, has_prev, s in scored:
                    s = jnp.where(_band_mask(has_prev), s, NEG_BIG)
                    mx = jnp.max(s, axis=1, keepdims=True)
                    v2 = jnp.concatenate([qkv_ref.at[2][rp, :], qkv_ref.at[2][rc, :]], axis=0)
                    acc = jnp.dot(_bf(jnp.exp(s - mx)), jnp.concatenate([_bf(v2), ones], axis=1),
                                  preferred_element_type=F32)
                    summed.append((rc, mx, acc))
                for rc, mx, acc in summed:
                    den = acc[:, B_DH:B_DH + 1]
                    og_scr.at[gi][rc, :] = acc[:, :B_DH] / den
                    lg_scr.at[gi][rc, :] = jnp.broadcast_to(mx + jnp.log(den), (B_BLOCK, B_DH))
                return carry

            lax.fori_loop(0, n_blocks // ATT_UNROLL, blocks, 0)

    @pl.when(grp == B_GROUPS - 1)
    def _():
        l0, l1, l2 = lg_scr[0], lg_scr[1], lg_scr[2]
        mx = jnp.maximum(jnp.maximum(l0, l1), l2)
        w0, w1, w2 = jnp.exp(l0 - mx), jnp.exp(l1 - mx), jnp.exp(l2 - mx)
        den = w0 + w1 + w2
        o = (w0 * og_scr[0] + w1 * og_scr[1] + w2 * og_scr[2]) / den
        silu, _ = _silu_parts(z_ref[...])
        o_ref[...] = o
        og_ref[...] = (o * silu).astype(og_ref.dtype)
        @pl.when(head == 0)
        def _():
            lse_ref[...] = jnp.zeros_like(lse_ref)

        lane = lax.broadcasted_iota(jnp.int32, o.shape, 1)
        lse_ref[...] = jnp.where(lane == head, mx + jnp.log(den), lse_ref[...])


def attn_fwd(proj_b, cos_t, sin_t, gains8, n_seq):
    t_rows = proj_b.shape[1]
    s_len = t_rows // n_seq
    head_blk = pl.BlockSpec((s_len, B_DH), lambda b, h, g: (b, h))
    seq_blk = pl.BlockSpec((s_len, 128), lambda b, h, g: (b, 0))
    return pl.pallas_call(
        _attn_fwd_body, name="attn_fwd", grid=(n_seq, B_HEADS, B_GROUPS),
        in_specs=[
            pl.BlockSpec((3, s_len, B_DH), lambda b, h, g: (g, b, h)),
            pl.BlockSpec((None, s_len, B_DH), lambda b, h, g: (B_PIECES - 1, b, h)),
            seq_blk, seq_blk,
            pl.BlockSpec((8, 128), lambda b, h, g: (0, 0)),
        ],
        out_specs=[head_blk, head_blk, seq_blk],
        out_shape=[jax.ShapeDtypeStruct((t_rows, B_W), BF16),
                   jax.ShapeDtypeStruct((t_rows, B_W), F32),
                   jax.ShapeDtypeStruct((t_rows, 128), F32)],
        scratch_shapes=[pltpu.VMEM((s_len, B_DH), F32), pltpu.VMEM((s_len, B_DH), F32),
                        pltpu.VMEM((B_GROUPS, s_len, B_DH), F32), pltpu.VMEM((B_GROUPS, s_len, B_DH), F32)],
        compiler_params=_params(("arbitrary", "arbitrary", "arbitrary")),
    )(proj_b, proj_b, cos_t, sin_t, gains8)


def _attn_bwd_body(qkv_ref, z_ref, cos_ref, sin_ref, gain_ref, dog_ref, o_ref, lse_ref,
                   dqkv_ref, dz_ref, dgain_ref,
                   qn_scr, kn_scr, dqn_scr, dkn_scr, do_scr, dl_scr, ls_scr, dv_scr):
    head, grp = pl.program_id(1), pl.program_id(2)
    first = (pl.program_id(0) == 0) & (head == 0) & (grp == 0)
    s_len = z_ref.shape[0]
    n_blocks = s_len // B_BLOCK
    cos_t, sin_t = cos_ref[...], sin_ref[...]

    @pl.when(first)
    def _():
        dgain_ref[...] = jnp.zeros_like(dgain_ref)

    @pl.when(grp == 0)
    def _():
        d_og, o = dog_ref[...], o_ref[...]
        silu, dsilu = _silu_parts(z_ref[...])
        d_o = d_og * silu
        dz_ref[...] = (d_og * o * dsilu).astype(dz_ref.dtype)
        do_scr[...] = d_o
        dl_scr[...] = jnp.broadcast_to(jnp.sum(d_o * o, axis=1, keepdims=True), o.shape)
        lane = lax.broadcasted_iota(jnp.int32, o.shape, 1)
        ls_scr[...] = jnp.broadcast_to(
            jnp.sum(jnp.where(lane == head, lse_ref[...], 0.0), axis=1, keepdims=True), o.shape)

    for gi, dil in enumerate(B_DILATIONS):
        @pl.when(grp == gi)
        def _(gi=gi, dil=dil):
            q_raw, k_raw = qkv_ref[0], qkv_ref[1]
            gq = gain_ref[gi:gi + 1, :]
            gk = gain_ref[B_GROUPS + gi:B_GROUPS + gi + 1, :]
            qn_scr[...], rq = _norm_rope(q_raw, gq, cos_t, sin_t)
            kn_scr[...], rk = _norm_rope(k_raw, gk, cos_t, sin_t)
            def blocks(it, carry):
                scored = []
                for j in range(ATT_UNROLL):
                    cur, prev, has_prev = _stream_rows(it * ATT_UNROLL + j, dil, s_len)
                    rc, rp = _rows(cur, dil), _rows(prev, dil)
                    qb, d_ob = _bf(qn_scr[rc, :]), _bf(do_scr[rc, :])
                    k2 = _bf(jnp.concatenate([kn_scr[rp, :], kn_scr[rc, :]], axis=0))
                    v2 = _bf(jnp.concatenate([qkv_ref.at[2][rp, :], qkv_ref.at[2][rc, :]], axis=0))
                    scored.append((rc, rp, has_prev, qb, d_ob, k2,
                                   _mm_nt(qb, k2) * ATT_SCALE, _mm_nt(d_ob, v2)))
                grads = []
                for rc, rp, has_prev, qb, d_ob, k2, s, d_p in scored:
                    p = jnp.exp(jnp.where(_band_mask(has_prev), s - ls_scr[rc, :][:, 0:1], NEG_BIG))
                    ds = _bf(p * (d_p - dl_scr[rc, :][:, 0:1]))
                    grads.append((rc, rp, has_prev,
                                  _mm(ds, k2) * ATT_SCALE, _mm_tn(ds, qb) * ATT_SCALE, _mm_tn(_bf(p), d_ob)))
                for j, (rc, rp, has_prev, dq, dk2, dv2) in enumerate(grads):
                    dqn_scr[rc, :] = dq
                    if j == 0:
                        @pl.when(has_prev)
                        def _():
                            dkn_scr[rp, :] += dk2[:B_BLOCK]
                            dv_scr[rp, :] += dv2[:B_BLOCK]
                    if j + 1 < ATT_UNROLL:
                        dkn_scr[rc, :] = dk2[B_BLOCK:] + grads[j + 1][4][:B_BLOCK]
                        dv_scr[rc, :] = dv2[B_BLOCK:] + grads[j + 1][5][:B_BLOCK]
                    else:
                        dkn_scr[rc, :] = dk2[B_BLOCK:]
                        dv_scr[rc, :] = dv2[B_BLOCK:]
                return carry

            lax.fori_loop(0, n_blocks // ATT_UNROLL, blocks, 0)
            dq, dgq = _norm_rope_bwd(q_raw, rq, gq, cos_t, sin_t, dqn_scr[...])
            dk, dgk = _norm_rope_bwd(k_raw, rk, gk, cos_t, sin_t, dkn_scr[...])
            dqkv_ref[0] = dq.astype(dqkv_ref.dtype)
            dqkv_ref[1] = dk.astype(dqkv_ref.dtype)
            dqkv_ref[2] = dv_scr[...].astype(dqkv_ref.dtype)
            dgain_ref[gi:gi + 1, :] += dgq
            dgain_ref[B_GROUPS + gi:B_GROUPS + gi + 1, :] += dgk


def attn_bwd(proj_b, cos_t, sin_t, gains8, d_og, o, lse, n_seq):
    t_rows = proj_b.shape[1]
    s_len = t_rows // n_seq
    head_blk = pl.BlockSpec((s_len, B_DH), lambda b, h, g: (b, h))
    seq_blk = pl.BlockSpec((s_len, 128), lambda b, h, g: (b, 0))
    grp_blk = pl.BlockSpec((3, s_len, B_DH), lambda b, h, g: (g, b, h))
    gain_blk = pl.BlockSpec((8, 128), lambda b, h, g: (0, 0))
    return pl.pallas_call(
        _attn_bwd_body, name="attn_bwd", grid=(n_seq, B_HEADS, B_GROUPS),
        in_specs=[
            grp_blk,
            pl.BlockSpec((None, s_len, B_DH), lambda b, h, g: (B_PIECES - 1, b, h)),
            seq_blk, seq_blk, gain_blk, head_blk, head_blk, seq_blk,
        ],
        out_specs=[grp_blk, head_blk, gain_blk],
        out_shape=[jax.ShapeDtypeStruct((3 * B_GROUPS, t_rows, B_W), BF16),
                   jax.ShapeDtypeStruct((t_rows, B_W), BF16),
                   jax.ShapeDtypeStruct((8, 128), F32)],
        scratch_shapes=[pltpu.VMEM((s_len, B_DH), F32) for _ in range(8)],
        compiler_params=_params(("arbitrary", "arbitrary", "arbitrary")),
    )(proj_b, proj_b, cos_t, sin_t, gains8, d_og, o, lse)


def rope_tables(positions):
    inv_freq = ROPE_THETA ** (-jnp.arange(0, ROPE_DIMS, 2, dtype=F32) / ROPE_DIMS)
    ang = positions.astype(F32)[:, None] * inv_freq
    cos, sin = jnp.cos(ang), jnp.sin(ang)
    t_rows = positions.shape[0]
    rest = B_DH - ROPE_DIMS
    cos_t = jnp.concatenate([cos, cos, jnp.ones((t_rows, rest), F32)], axis=1)
    sin_t = jnp.concatenate([-sin, sin, jnp.zeros((t_rows, rest), F32)], axis=1)
    return cos_t, sin_t


def _rms_fwd_body(x_ref, g_ref, *rest, layer):
    h_ref, ht_ref = rest[-2:]
    x = x_ref[...]
    r = lax.rsqrt(jnp.mean(x * x, axis=1, keepdims=True) + EPS)
    h = x * r * g_ref[layer:layer + 1, :]
    h_ref[...] = h.astype(h_ref.dtype)
    ht_ref[...] = h.T.astype(ht_ref.dtype)


def rms_fwd(x, gains8, layer, after=None):
    t_rows, d = x.shape
    tm = min(ROW_TILE, t_rows)
    in_specs = [pl.BlockSpec((tm, d), lambda i: (i, 0)), pl.BlockSpec((8, d), lambda i: (0, 0))]
    args = [x, gains8]
    if after is not None:
        in_specs.append(HBM_SPEC)
        args.append(after)
    return pl.pallas_call(
        functools.partial(_rms_fwd_body, layer=layer), name=f"rms_fwd_{layer}", grid=(t_rows // tm,),
        in_specs=in_specs,
        out_specs=[pl.BlockSpec((tm, d), lambda i: (i, 0)), pl.BlockSpec((d, tm), lambda i: (0, i))],
        out_shape=[jax.ShapeDtypeStruct((t_rows, d), BF16), jax.ShapeDtypeStruct((d, t_rows), BF16)],
        compiler_params=_params(("arbitrary",)),
    )(*args)


def _rms_bwd_body(x_ref, g_ref, dh_ref, res_ref, dx_ref, dg_ref, *, layer):
    x, dh = x_ref[...], dh_ref[...]
    r = lax.rsqrt(jnp.mean(x * x, axis=1, keepdims=True) + EPS)
    xr = x * r
    u = dh * g_ref[layer:layer + 1, :]
    dx_ref[...] = res_ref[...] + r * u - xr * (r * r) * jnp.mean(u * x, axis=1, keepdims=True)

    @pl.when(pl.program_id(0) == 0)
    def _():
        dg_ref[...] = jnp.zeros_like(dg_ref)

    dg_ref[0:1, :] += jnp.sum(dh * xr, axis=0, keepdims=True)


def rms_bwd(x, gains8, layer, dh, d_res):
    t_rows, d = x.shape
    tm = min(ROW_TILE, t_rows)
    blk = pl.BlockSpec((tm, d), lambda i: (i, 0))
    gblk = pl.BlockSpec((8, d), lambda i: (0, 0))
    return pl.pallas_call(
        functools.partial(_rms_bwd_body, layer=layer), name=f"rms_bwd_{layer}", grid=(t_rows // tm,),
        in_specs=[blk, gblk, blk, blk],
        out_specs=[blk, gblk],
        out_shape=[jax.ShapeDtypeStruct((t_rows, d), F32), jax.ShapeDtypeStruct((8, d), F32)],
        compiler_params=_params(("arbitrary",)),
    )(x, gains8, dh, d_res)


def _piece_col(p):
    return jnp.where(p < 3 * B_GROUPS, (p % 3) * B_GROUPS + p // 3, 3 * B_GROUPS)


def _mm_nn_body(a_ref, w_ref, *rest, has_res, a_resident):
    o_ref = rest[-1]
    tm = o_ref.shape[0]
    a = a_ref[pl.ds(pl.multiple_of(pl.program_id(1) * tm, tm), tm), :] if a_resident else a_ref[...]
    acc = jnp.dot(a, w_ref[...], preferred_element_type=F32)
    if has_res:
        acc = acc + rest[0][...]
    o_ref[...] = acc


def mm_nn(a, w, residual=None, *, tn, name, a_resident=False):
    m, k = a.shape
    n = w.shape[1]
    tm = min(NT_ROW_TILE, m)
    a_spec = pl.BlockSpec((m, k), lambda j, i: (0, 0)) if a_resident else pl.BlockSpec((tm, k), lambda j, i: (i, 0))
    in_specs = [a_spec, pl.BlockSpec((k, tn), lambda j, i: (0, j))]
    args = [a, w]
    if residual is not None:
        in_specs.append(pl.BlockSpec((tm, tn), lambda j, i: (i, j)))
        args.append(residual)
    return pl.pallas_call(
        functools.partial(_mm_nn_body, has_res=residual is not None, a_resident=a_resident), name=name,
        grid=(n // tn, m // tm),
        in_specs=in_specs,
        out_specs=pl.BlockSpec((tm, tn), lambda j, i: (i, j)),
        out_shape=jax.ShapeDtypeStruct((m, n), F32),
        compiler_params=_params(("arbitrary", "arbitrary")),
    )(*args)


def mm_nn_pieces(a, w, *, name):
    m, k = a.shape
    tm = min(NT_ROW_TILE, m)
    return pl.pallas_call(
        functools.partial(_mm_nn_body, has_res=False, a_resident=True), name=name, grid=(B_PIECES, m // tm),
        in_specs=[pl.BlockSpec((m, k), lambda p, i: (0, 0)),
                  pl.BlockSpec((k, B_W), lambda p, i: (0, _piece_col(p)))],
        out_specs=pl.BlockSpec((None, tm, B_W), lambda p, i: (p, i, 0)),
        out_shape=jax.ShapeDtypeStruct((B_PIECES, m, B_W), F32),
        compiler_params=_params(("arbitrary", "arbitrary")),
    )(a, w)


NT_ROW_TILE = 1024


def _mm_nt_body(g_ref, w_ref, *rest, has_init):
    o_ref = rest[-1]
    j = pl.program_id(1)
    part = lax.dot_general(_bf(g_ref[...]), w_ref[...], (((1,), (1,)), ((), ())), preferred_element_type=F32)

    @pl.when(j == 0)
    def _():
        o_ref[...] = part + rest[0][...] if has_init else part

    @pl.when(j > 0)
    def _():
        o_ref[...] += part


def mm_nt(g, w, init=None, *, tn, col_off=0, name, after=None):
    m, n = g.shape
    k = w.shape[0]
    tm = min(NT_ROW_TILE, m)
    in_specs = [pl.BlockSpec((tm, tn), lambda i, j: (i, j)),
                pl.BlockSpec((k, tn), lambda i, j: (0, col_off + j))]
    args = [g, w]
    if init is not None:
        in_specs.append(pl.BlockSpec((tm, k), lambda i, j: (i, 0)))
        args.append(init)
    if after is not None:
        in_specs.append(HBM_SPEC)
        args.append(after)
    return pl.pallas_call(
        functools.partial(_mm_nt_body, has_init=init is not None), name=name, grid=(m // tm, n // tn),
        in_specs=in_specs,
        out_specs=pl.BlockSpec((tm, k), lambda i, j: (i, 0)),
        out_shape=jax.ShapeDtypeStruct((m, k), F32),
        compiler_params=_params(("arbitrary", "arbitrary")),
    )(*args)


def mm_nt_multi(gs, w, *, tn, name, after=None):
    m = gs[0].shape[0]
    k = w.shape[0]
    tm = min(NT_ROW_TILE, m)
    tiles = [g.shape[1] // tn for g in gs]
    starts = [sum(tiles[:i]) for i in range(len(gs))]

    def body(*refs):
        g_refs, w_ref, o_ref = refs[:len(gs)], refs[len(gs)], refs[-1]
        j = pl.program_id(1)
        for g_ref, lo, cnt in zip(g_refs, starts, tiles):
            @pl.when((j >= lo) & (j < lo + cnt))
            def _(g_ref=g_ref):
                part = lax.dot_general(_bf(g_ref[...]), w_ref[...], (((1,), (1,)), ((), ())),
                                       preferred_element_type=F32)

                @pl.when(j == 0)
                def _():
                    o_ref[...] = part

                @pl.when(j > 0)
                def _():
                    o_ref[...] += part

    def g_spec(lo, cnt):
        return pl.BlockSpec((tm, tn), lambda i, j: (i, jnp.clip(j - lo, 0, cnt - 1)))

    in_specs = [g_spec(lo, cnt) for lo, cnt in zip(starts, tiles)] + [pl.BlockSpec((k, tn), lambda i, j: (0, j))]
    args = list(gs) + [w]
    if after is not None:
        in_specs.append(HBM_SPEC)
        args.append(after)
    return pl.pallas_call(
        body, name=name, grid=(m // tm, sum(tiles)),
        in_specs=in_specs,
        out_specs=pl.BlockSpec((tm, k), lambda i, j: (i, 0)),
        out_shape=jax.ShapeDtypeStruct((m, k), F32),
        compiler_params=_params(("arbitrary", "arbitrary")),
    )(*args)


def mm_nt_pieces(g9, gz, w, *, name):
    n_p, m, _ = g9.shape
    k = w.shape[0]
    tm = min(NT_ROW_TILE, m)

    def body(g_ref, z_ref, w_ref, o_ref):
        p = pl.program_id(1)

        def accumulate(src):
            part = lax.dot_general(_bf(src[...]), w_ref[...], (((1,), (1,)), ((), ())), preferred_element_type=F32)

            @pl.when(p == 0)
            def _():
                o_ref[...] = part

            @pl.when(p > 0)
            def _():
                o_ref[...] += part

        @pl.when(p < n_p)
        def _():
            accumulate(g_ref)

        @pl.when(p == n_p)
        def _():
            accumulate(z_ref)

    return pl.pallas_call(
        body, name=name, grid=(m // tm, n_p + 1),
        in_specs=[pl.BlockSpec((None, tm, B_W), lambda i, p: (jnp.minimum(p, n_p - 1), i, 0)),
                  pl.BlockSpec((tm, B_W), lambda i, p: (i, 0)),
                  pl.BlockSpec((k, B_W), lambda i, p: (0, _piece_col(p)))],
        out_specs=pl.BlockSpec((tm, k), lambda i, p: (i, 0)),
        out_shape=jax.ShapeDtypeStruct((m, k), F32),
        compiler_params=_params(("arbitrary", "arbitrary")),
    )(g9, gz, w)


def _mm_tn_body(a_ref, g_ref, o_ref, *, a_is_transposed):
    lhs_dim = 1 if a_is_transposed else 0
    o_ref[...] = lax.dot_general(a_ref[...], _bf(g_ref[...]), (((lhs_dim,), (0,)), ((), ())),
                                 preferred_element_type=F32).astype(o_ref.dtype)


def mm_tn(a, g, *, tn, out_dtype, name, a_is_transposed=False):
    k = a.shape[0] if a_is_transposed else a.shape[1]
    m, n = g.shape
    return pl.pallas_call(
        functools.partial(_mm_tn_body, a_is_transposed=a_is_transposed), name=name, grid=(n // tn,),
        in_specs=[pl.BlockSpec(a.shape, lambda j: (0, 0)), pl.BlockSpec((m, tn), lambda j: (0, j))],
        out_specs=pl.BlockSpec((k, tn), lambda j: (0, j)),
        out_shape=jax.ShapeDtypeStruct((k, n), out_dtype),
        compiler_params=_params(("arbitrary",)),
    )(a, g)


B_UNIT = 256
B_IN_COLS = B_PIECES * B_W
B_SHARD_UNITS = B_IN_COLS // N_DEV // B_UNIT


def mm_tn_b_in(at, g9, gz, *, out_dtype, name):
    k, m = at.shape
    per_piece = B_W // B_UNIT
    body_one = functools.partial(_mm_tn_body, a_is_transposed=True)
    n_units = B_IN_COLS // B_UNIT

    def g_map(u):
        nat = jnp.minimum(u // per_piece, 3 * B_GROUPS - 1)
        piece = (nat % B_GROUPS) * 3 + nat // B_GROUPS
        return (piece, 0, u % per_piece)

    def body(a_ref, g_ref, z_ref, o_ref):
        u = pl.program_id(0)

        @pl.when(u < 3 * B_GROUPS * per_piece)
        def _():
            body_one(a_ref, g_ref, o_ref)

        @pl.when(u >= 3 * B_GROUPS * per_piece)
        def _():
            body_one(a_ref, z_ref, o_ref)

    return pl.pallas_call(
        body, name=name, grid=(n_units,),
        in_specs=[pl.BlockSpec((k, m), lambda u: (0, 0)),
                  pl.BlockSpec((None, m, B_UNIT), g_map),
                  pl.BlockSpec((m, B_UNIT), lambda u: (0, jnp.where(u < 3 * B_GROUPS * per_piece, 0, u % per_piece)))],
        out_specs=pl.BlockSpec((None, k, B_UNIT), lambda u: (u // B_SHARD_UNITS, 0, u % B_SHARD_UNITS)),
        out_shape=jax.ShapeDtypeStruct((N_DEV, k, B_IN_COLS // N_DEV), out_dtype),
        compiler_params=_params(("arbitrary",)),
    )(at, g9, gz)


def _loss_body(y_ref, t_ref, dy_ref, loss_ref, acc):
    i = pl.program_id(0)
    d = y_ref.shape[1]
    err = y_ref[...] - t_ref[...]
    dy_ref[...] = err * (1.0 / d)

    @pl.when(i == 0)
    def _():
        acc[...] = jnp.zeros_like(acc)

    acc[...] += jnp.sum(err * err, axis=0, keepdims=True)

    @pl.when(i == pl.num_programs(0) - 1)
    def _():
        total = jnp.sum(acc[...], axis=1, keepdims=True) * (0.5 / d)
        loss_ref[...] = jnp.broadcast_to(total, loss_ref.shape)


def loss_head(y, target):
    t_rows, d = y.shape
    tm = min(ROW_TILE, t_rows)
    blk = pl.BlockSpec((tm, d), lambda i: (i, 0))
    return pl.pallas_call(
        _loss_body, name="loss_head", grid=(t_rows // tm,),
        in_specs=[blk, blk],
        out_specs=[blk, pl.BlockSpec((8, 128), lambda i: (0, 0))],
        out_shape=[jax.ShapeDtypeStruct((t_rows, d), F32), jax.ShapeDtypeStruct((8, 128), F32)],
        scratch_shapes=[pltpu.VMEM((1, d), F32)],
        compiler_params=_params(("arbitrary",)),
    )(y, target)


def _adamw_body(p_ref, w_ref, m_ref, v_ref, g_ref, d_ref, nm_ref, nv_ref):
    g = p_ref[0].astype(F32)
    for s in range(1, N_DEV):
        g = g + p_ref[s].astype(F32)
    w = w_ref[...]
    m = ADAM_B1 * m_ref[...] + (1.0 - ADAM_B1) * g
    v = ADAM_B2 * v_ref[...] + (1.0 - ADAM_B2) * (g * g)
    m_hat = m / (1.0 - ADAM_B1 ** ADAM_STEP)
    v_hat = v / (1.0 - ADAM_B2 ** ADAM_STEP)
    g_ref[...] = g
    d_ref[...] = -ADAM_LR * (m_hat / (jnp.sqrt(v_hat) + ADAM_EPS) + ADAM_WD * w)
    nm_ref[...] = m
    nv_ref[...] = v


def adamw(parts, w, m, v, *, name):
    _, r, c = w.shape
    tr = r if r <= 256 else 256
    blk = pl.BlockSpec((None, tr, c), lambda i: (0, i, 0))
    out = jax.ShapeDtypeStruct((1, r, c), F32)
    return pl.pallas_call(
        _adamw_body, name=name, grid=(r // tr,),
        in_specs=[pl.BlockSpec((N_DEV, tr, c), lambda i: (0, i, 0)), blk, blk, blk],
        out_specs=[blk, blk, blk, blk],
        out_shape=[out, out, out, out],
        compiler_params=_params(("arbitrary",)),
    )(parts, w, m, v)


MESH_ID = pl.DeviceIdType.MESH
HBM_SPEC = pl.BlockSpec(memory_space=pl.ANY)


def _my_place():
    return lax.axis_index("x"), lax.axis_index("y"), lax.axis_index("c")


def _flat(x, y, c):
    return 4 * x + 2 * y + c


def _all_gather_body(*refs, n):
    ins, outs = refs[:n], refs[n:2 * n]
    send_sems, recv_sems, local_sems = refs[2 * n:]
    x, y, c = _my_place()
    me, sibling = (x, y, c), (x, y, 1 - c)
    chips = [(1 - x, y), (x, 1 - y), (1 - x, 1 - y)]
    pending = []
    for a in range(n):
        src, out = ins[a], outs[a]

        def copy(k, block, to, from_input=False, a=a, src=src, out=out):
            slot = out.at[_flat(*block)]
            return pltpu.make_async_remote_copy(
                src_ref=src if from_input else slot, dst_ref=slot,
                send_sem=send_sems.at[7 * a + k], recv_sem=recv_sems.at[7 * a + k],
                device_id=to, device_id_type=MESH_ID)

        mine = pltpu.make_async_copy(src, out.at[_flat(*me)], local_sems.at[a])
        mine.start()
        first = [copy(0, me, sibling, True)] + [copy(1 + j, me, (*chip, c), True) for j, chip in enumerate(chips)]
        for cp in first:
            cp.start()
        pending.append((copy, mine, first))
    for copy, mine, first in pending:
        passed = [copy(4 + j, (*chip, c), sibling) for j, chip in enumerate(chips)]
        for j, chip in enumerate(chips):
            copy(1 + j, (*chip, c), me).wait_recv()
            passed[j].start()
        copy(0, sibling, me).wait_recv()
        for j, chip in enumerate(chips):
            copy(4 + j, (*chip, 1 - c), me).wait_recv()
        for cp in first + passed:
            cp.wait_send()
        mine.wait()


def all_gather(shards, *, name):
    n = len(shards)
    return pl.pallas_call(
        functools.partial(_all_gather_body, n=n), name=name,
        in_specs=[HBM_SPEC] * n, out_specs=[HBM_SPEC] * n,
        out_shape=[jax.ShapeDtypeStruct((N_DEV,) + s.shape, s.dtype) for s in shards],
        scratch_shapes=[pltpu.SemaphoreType.DMA((7 * n,)), pltpu.SemaphoreType.DMA((7 * n,)),
                        pltpu.SemaphoreType.DMA((n,))],
    )(*shards)


PEER_FLIPS = [(0, 0, 1), (1, 0, 0), (0, 1, 0), (1, 1, 0), (1, 0, 1), (0, 1, 1), (1, 1, 1)]


def _all_to_all_body(*refs, n):
    ins, outs = refs[:n], refs[n:2 * n]
    send_sems, recv_sems, local_sems = refs[2 * n:]
    x, y, c = _my_place()
    me = _flat(x, y, c)
    waits = []
    for a in range(n):
        src, out = ins[a], outs[a]
        mine = pltpu.make_async_copy(src.at[me], out.at[me], local_sems.at[a])
        mine.start()
        waits.append(mine)
        for k, (fx, fy, fc) in enumerate(PEER_FLIPS):
            peer = (1 - x if fx else x, 1 - y if fy else y, 1 - c if fc else c)
            theirs = _flat(*peer)
            sems = dict(send_sem=send_sems.at[7 * a + k], recv_sem=recv_sems.at[7 * a + k],
                        device_id=peer, device_id_type=MESH_ID)
            send = pltpu.make_async_remote_copy(src_ref=src.at[theirs], dst_ref=out.at[me], **sems)
            send.start()
            recv = pltpu.make_async_remote_copy(src_ref=src.at[theirs], dst_ref=out.at[theirs], **sems)
            waits.append((send, recv))
    for w in waits:
        if isinstance(w, tuple):
            w[0].wait_send()
            w[1].wait_recv()
        else:
            w.wait()


def all_to_all(parts, *, name):
    n = len(parts)
    return pl.pallas_call(
        functools.partial(_all_to_all_body, n=n), name=name,
        in_specs=[HBM_SPEC] * n, out_specs=[HBM_SPEC] * n,
        out_shape=[jax.ShapeDtypeStruct(p.shape, p.dtype) for p in parts],
        scratch_shapes=[pltpu.SemaphoreType.DMA((7 * n,)), pltpu.SemaphoreType.DMA((7 * n,)),
                        pltpu.SemaphoreType.DMA((n,))],
    )(*parts)


HBM_ONLY = pl.BlockSpec(memory_space=pltpu.HBM)
SEM_SPEC = pl.BlockSpec(memory_space=pltpu.SEMAPHORE)
DATAFLOW_EFFECT = pltpu.SideEffectType.DATAFLOW_SIDE_EFFECTING


def _split_copies(srcs, lands, send_sems, recv_sems, n, scatter):
    x, y, c = _my_place()
    me = _flat(x, y, c)
    pairs = []
    for a in range(n):
        for k, (fx, fy, fc) in enumerate(PEER_FLIPS):
            peer = (1 - x if fx else x, 1 - y if fy else y, 1 - c if fc else c)
            theirs = _flat(*peer)
            src = srcs[a].at[theirs] if scatter else srcs[a]
            sems = dict(send_sem=send_sems.at[7 * a + k], recv_sem=recv_sems.at[7 * a + k],
                        device_id=peer, device_id_type=MESH_ID)
            pairs.append((pltpu.make_async_remote_copy(src_ref=src, dst_ref=lands[a].at[me], **sems),
                          pltpu.make_async_remote_copy(src_ref=src, dst_ref=lands[a].at[theirs], **sems)))
    return pairs


def _exchange_start_body(*refs, n, scatter):
    srcs, lands = refs[:n], refs[n:2 * n]
    send_sems, recv_sems = refs[2 * n], refs[2 * n + 1]
    token = refs[-1]
    for send, _ in _split_copies(srcs, lands, send_sems, recv_sems, n, scatter):
        send.start()
    token[...] = jnp.zeros_like(token)


def exchange_start(srcs, lands, *, scatter, name):
    n = len(srcs)
    args = [pltpu.with_memory_space_constraint(t, pltpu.HBM) for t in list(srcs) + list(lands)]
    outs = pl.pallas_call(
        functools.partial(_exchange_start_body, n=n, scatter=scatter), name=name,
        out_shape=(pltpu.SemaphoreType.DMA((7 * n,)), pltpu.SemaphoreType.DMA((7 * n,)),
                   *[pltpu.HBM(t.shape, t.dtype) for t in args],
                   jax.ShapeDtypeStruct((8, 128), F32)),
        in_specs=[HBM_ONLY] * (2 * n),
        out_specs=(SEM_SPEC, SEM_SPEC, *[HBM_ONLY] * (2 * n), pl.BlockSpec(memory_space=pltpu.VMEM)),
        input_output_aliases={i: 2 + i for i in range(2 * n)},
        compiler_params=pltpu.CompilerParams(has_side_effects=DATAFLOW_EFFECT),
    )(*args)
    return outs[0], outs[1], outs[2:2 + n], outs[2 + n:2 + 2 * n], outs[-1]


def _exchange_wait_body(*refs, n, scatter):
    srcs, lands = refs[:n], refs[n:2 * n]
    send_sems, recv_sems = refs[2 * n], refs[2 * n + 1]
    for send, recv in _split_copies(srcs, lands, send_sems, recv_sems, n, scatter):
        send.wait_send()
        recv.wait_recv()


def exchange_wait(send_sems, recv_sems, srcs, lands, after, *, scatter, name):
    n = len(srcs)
    outs = pl.pallas_call(
        functools.partial(_exchange_wait_body, n=n, scatter=scatter), name=name,
        out_shape=tuple(pltpu.HBM(t.shape, t.dtype) for t in list(srcs) + list(lands)),
        in_specs=[HBM_ONLY] * (2 * n) + [SEM_SPEC, SEM_SPEC, HBM_SPEC],
        out_specs=tuple([HBM_ONLY] * (2 * n)),
        input_output_aliases={i: i for i in range(2 * n)},
        compiler_params=pltpu.CompilerParams(has_side_effects=DATAFLOW_EFFECT),
    )(*srcs, *lands, send_sems, recv_sems, after)
    return outs[n:]


def _own_slot_only(shape_dtype, own, slot):
    land = lax.empty(shape_dtype.shape, shape_dtype.dtype)
    return lax.dynamic_update_slice(land, own[None], (slot,) + (0,) * own.ndim)


def _pad_rows(a, rows=8):
    return jnp.pad(a, ((0, rows - a.shape[0]), (0, 0)))


def _gate_rows(a_log, dt_bias):
    z = jnp.zeros((8, 128), F32)
    return z.at[0, A_HEADS:2 * A_HEADS].set(a_log[0]).at[1, A_HEADS:2 * A_HEADS].set(dt_bias[0])


def _pack_small(norm_g, a_log, a_dt_bias, a_norm_g, b_q_norm_g, b_k_norm_g):
    return jnp.concatenate([
        norm_g[0].reshape(8, 128), norm_g[1].reshape(8, 128),
        _gate_rows(a_log, a_dt_bias),
        _pad_rows(a_norm_g[0].reshape(2, 128)),
        _pad_rows(jnp.concatenate([b_q_norm_g[0], b_k_norm_g[0]], axis=0)),
    ], axis=0)


def _unpack_small(p):
    return (p[0:16].reshape(2, D_MODEL), p[16:17, A_HEADS:2 * A_HEADS], p[17:18, A_HEADS:2 * A_HEADS],
            p[24:26].reshape(1, A_DV), p[32:35][None], p[35:38][None])


def kernel(x, positions, norm_g, a_w_in, a_conv_w, a_log, a_dt_bias, a_norm_g, a_w_out, b_w_in, b_q_norm_g, b_k_norm_g, b_w_out, loss_target, m_norm_g, m_a_w_in, m_a_conv_w, m_a_log, m_a_dt_bias, m_a_norm_g, m_a_w_out, m_b_w_in, m_b_q_norm_g, m_b_k_norm_g, m_b_w_out, v_norm_g, v_a_w_in, v_a_conv_w, v_a_log, v_a_dt_bias, v_a_norm_g, v_a_w_out, v_b_w_in, v_b_q_norm_g, v_b_k_norm_g, v_b_w_out):
    n_seq, s_len, d = x.shape
    t_rows = n_seq * s_len
    n_chunks = s_len // CHUNK
    x0 = x.reshape(t_rows, d)
    target = loss_target.reshape(t_rows, d)
    my_slot = _flat(*_my_place())

    g_a_in, g_conv = all_gather([a_w_in[0].astype(BF16), _pad_rows(a_conv_w[0])], name="gather_weights_first")
    later = [a_w_out[0].astype(BF16), b_w_in[0].astype(BF16), b_w_out[0].astype(BF16)]
    lands = [_own_slot_only(jax.ShapeDtypeStruct((N_DEV,) + t.shape, t.dtype), t, my_slot) for t in later]
    w_send, w_recv, later, lands, w_token = exchange_start(later, lands, scatter=False, name="gather_weights_start")
    w_a_in = jnp.pad(g_a_in.transpose(1, 0, 2).reshape(d, A_IN), ((0, 0), (0, A_IN_PAD - A_IN)))
    conv_w8 = g_conv.transpose(1, 0, 2).reshape(8, 2 * A_QK + A_VW)

    gains_model = _pad_rows(norm_g)
    gate_prm = _gate_rows(a_log, a_dt_bias)
    gain_a_out = _pad_rows(a_norm_g)
    gains_qk = _pad_rows(jnp.concatenate([b_q_norm_g[0], b_k_norm_g[0]], axis=0))
    cos_t, sin_t = rope_tables(positions.reshape(t_rows))

    h0, h0_t = rms_fwd(x0, gains_model, 0, after=w_token)
    proj_a = mm_nn(h0, w_a_in, tn=896, name="proj_a", a_resident=True)
    gates_col, gates_row = a_gates_fwd(proj_a, gate_prm, n_seq)
    gates_row = gates_row.reshape(n_seq, 2 * A_HEADS, s_len // SUPER, 1, SUPER)
    o_a, tinv, states, og_a, q_a, k_a, v_a = gdn_fwd(proj_a, conv_w8, gates_col, gates_row, gain_a_out, n_seq)
    g_a_out, g_b_in, g_b_out = exchange_wait(w_send, w_recv, later, lands, og_a, scatter=False,
                                             name="gather_weights_wait")
    w_a_out = g_a_out.reshape(A_VW, d)
    w_b_in = g_b_in.transpose(1, 0, 2).reshape(d, B_IN_COLS)
    w_b_out = g_b_out.reshape(B_W, d)
    x1 = mm_nn(og_a, w_a_out, x0, tn=1024, name="out_a")

    h1, h1_t = rms_fwd(x1, gains_model, 1)
    proj_b = mm_nn_pieces(h1, w_b_in, name="proj_b")
    og_b, o_b, lse = attn_fwd(proj_b, cos_t, sin_t, gains_qk, n_seq)
    y = mm_nn(og_b, w_b_out, x1, tn=1024, name="out_b")

    dy, loss_blk = loss_head(y, target)
    loss = lax.psum(loss_blk[0, 0], ("x", "y", "c"))

    d_og_b = mm_nt(dy, w_b_out, tn=1024, name="d_og_b")
    dw_b_out = mm_tn(og_b, dy, tn=256, out_dtype=BF16, name="dw_b_out")
    dqkv_b, dz_b, d_gains_qk = attn_bwd(proj_b, cos_t, sin_t, gains_qk, d_og_b, o_b, lse, n_seq)
    dh1 = mm_nt_pieces(dqkv_b, dz_b, w_b_in, name="dh1")
    dw_b_in = mm_tn_b_in(h1_t, dqkv_b, dz_b, out_dtype=BF16, name="dw_b_in")
    dx1, d_gain1 = rms_bwd(x1, gains_model, 1, dh1, dy)

    dw_a_out = mm_tn(og_a, dx1, tn=256, out_dtype=BF16, name="dw_a_out")
    early = [dw_b_in, dw_b_out.reshape(N_DEV, B_W // N_DEV, d), dw_a_out.reshape(N_DEV, A_VW // N_DEV, d)]
    lands = [_own_slot_only(t, lax.dynamic_index_in_dim(t, my_slot, 0, keepdims=False), my_slot) for t in early]
    g_send, g_recv, early, lands, g_token = exchange_start(early, lands, scatter=True, name="scatter_grads_start")

    d_og_a = mm_nt(dx1, w_a_out, tn=1024, name="d_og_a", after=g_token)
    d_xq, d_xk, d_xv, dgates, dz_a, d_gain_a_out, d_cq, d_ck, d_cv = gdn_bwd(
        q_a, k_a, v_a, gates_col, gates_row, tinv, states, d_og_a, o_a, proj_a, conv_w8, gain_a_out, n_seq)
    d_conv = jnp.concatenate([d_cq.sum(axis=0), d_ck.sum(axis=0), d_cv.sum(axis=0)], axis=1)
    d_gate_logits, d_gate_prm = a_gates_bwd(proj_a, gate_prm, dgates, n_seq)
    dw_a_in = jnp.concatenate([
        mm_tn(h0_t, piece, tn=min(256, piece.shape[1]), out_dtype=BF16, name=f"dw_a_in_{nm}", a_is_transposed=True)
        for nm, piece in (("q", d_xq), ("k", d_xk), ("v", d_xv), ("z", dz_a), ("gates", d_gate_logits))
    ], axis=1)[:, :A_IN]
    shard_a_in = A_IN // N_DEV
    last = [dw_a_in.reshape(d, N_DEV, shard_a_in).transpose(1, 0, 2)]
    last_lands = [_own_slot_only(t, lax.dynamic_index_in_dim(t, my_slot, 0, keepdims=False), my_slot) for t in last]
    l_send, l_recv, last, last_lands, l_token = exchange_start(last, last_lands, scatter=True,
                                                               name="scatter_last_start")
    dh0 = mm_nt_multi([d_xq, d_xk, d_xv, dz_a], w_a_in, tn=1024, name="dh0_qkvz", after=l_token)
    dh0 = mm_nt(d_gate_logits, w_a_in, dh0, tn=128, col_off=A_GATE_COL, name="dh0_gates")
    dx0, d_gain0 = rms_bwd(x0, gains_model, 0, dh0, dx1)

    small = jnp.concatenate([
        d_gain0[0].reshape(8, 128), d_gain1[0].reshape(8, 128), d_gate_prm,
        _pad_rows(d_gain_a_out[0].reshape(2, 128)), d_gains_qk], axis=0)
    r_small, r_conv = all_gather([small, d_conv], name="gather_small_grads")
    conv_cols = a_conv_w.shape[2]
    r_conv = lax.dynamic_slice(r_conv, (0, 0, my_slot * conv_cols), (N_DEV, 8, conv_cols))

    r_b_in, r_b_out, r_a_out = exchange_wait(g_send, g_recv, early, lands, r_small, scatter=True,
                                             name="scatter_grads_wait")
    (r_a_in,) = exchange_wait(l_send, l_recv, last, last_lands, r_small, scatter=True, name="scatter_last_wait")

    upd = {}
    upd["a_w_in"] = adamw(r_a_in, a_w_in, m_a_w_in, v_a_w_in, name="adamw_a_w_in")
    upd["a_w_out"] = adamw(r_a_out, a_w_out, m_a_w_out, v_a_w_out, name="adamw_a_w_out")
    upd["b_w_in"] = adamw(r_b_in, b_w_in, m_b_w_in, v_b_w_in, name="adamw_b_w_in")
    upd["b_w_out"] = adamw(r_b_out, b_w_out, m_b_w_out, v_b_w_out, name="adamw_b_w_out")
    upd["a_conv_w"] = [t[:, :A_CONV] for t in adamw(
        r_conv, _pad_rows(a_conv_w[0])[None], _pad_rows(m_a_conv_w[0])[None], _pad_rows(v_a_conv_w[0])[None],
        name="adamw_a_conv_w")]
    small_upd = adamw(
        r_small,
        _pack_small(norm_g, a_log, a_dt_bias, a_norm_g, b_q_norm_g, b_k_norm_g)[None],
        _pack_small(m_norm_g, m_a_log, m_a_dt_bias, m_a_norm_g, m_b_q_norm_g, m_b_k_norm_g)[None],
        _pack_small(v_norm_g, v_a_log, v_a_dt_bias, v_a_norm_g, v_b_q_norm_g, v_b_k_norm_g)[None],
        name="adamw_small")
    small_names = ("norm_g", "a_log", "a_dt_bias", "a_norm_g", "b_q_norm_g", "b_k_norm_g")
    unpacked = [_unpack_small(t[0]) for t in small_upd]
    for i, nm in enumerate(small_names):
        upd[nm] = [u[i] for u in unpacked]

    order = ("norm_g", "a_w_in", "a_conv_w", "a_log", "a_dt_bias", "a_norm_g", "a_w_out",
             "b_w_in", "b_q_norm_g", "b_k_norm_g", "b_w_out")
    outs = [loss, dx0.reshape(n_seq, s_len, d)]
    for kind in range(4):
        for nm in order:
            outs.append(upd[nm][kind])
    return tuple(outs)
```

```python
import functools
import math

import jax
import jax.numpy as jnp
from jax import lax
from jax.experimental import pallas as pl
from jax.experimental.pallas import tpu as pltpu

F32 = jnp.float32
BF16 = jnp.bfloat16

D_MODEL = 1024
EPS = 1e-6
N_DEV = 8

A_HEADS = 8
A_DK = 128
A_DV = 256
A_QK = A_HEADS * A_DK
A_VW = A_HEADS * A_DV
A_CONV = 4
CHUNK = 64
A_IN = 2 * A_QK + 2 * A_VW + 2 * A_HEADS
A_IN_PAD = 2 * A_QK + 2 * A_VW + 128
A_GATE_COL = (2 * A_QK + 2 * A_VW) // 128

B_DILATIONS = (1, 4, 16)
B_GROUPS = 3
B_HEADS = 8
B_DH = 128
B_W = B_HEADS * B_DH
B_BLOCK = 128
B_PIECES = 3 * B_GROUPS + 1
ROPE_THETA = 500000.0
ROPE_DIMS = B_DH // 4
ROPE_HALF = ROPE_DIMS // 2

ADAM_LR = 0.001
ADAM_B1 = 0.9
ADAM_B2 = 0.999
ADAM_EPS = 1e-08
ADAM_WD = 0.01
ADAM_STEP = 10

VMEM_LIMIT = 60 * 1024 * 1024


def _params(sem):
    return pltpu.CompilerParams(dimension_semantics=sem, vmem_limit_bytes=VMEM_LIMIT)


def _bf(x):
    return x.astype(BF16)


def _mm(a, b):
    return jnp.dot(_bf(a), _bf(b), preferred_element_type=F32)


def _mm_nt(a, b):
    return lax.dot_general(_bf(a), _bf(b), (((1,), (1,)), ((), ())), preferred_element_type=F32)


def _mm_tn(a, b):
    return lax.dot_general(_bf(a), _bf(b), (((0,), (0,)), ((), ())), preferred_element_type=F32)


def _split(x):
    hi = _bf(x)
    return hi, _bf(x - hi.astype(F32))


def _mm3(a, b):
    ah, al = _split(a)
    bh, bl = _split(b)
    d = functools.partial(jnp.dot, preferred_element_type=F32)
    return d(ah, bh) + (d(ah, bl) + d(al, bh))


def _colsum_as_col(z):
    zh, zl = _split(z)
    ones = jnp.ones((z.shape[0], 128), BF16)
    tn = functools.partial(lax.dot_general, dimension_numbers=(((0,), (0,)), ((), ())),
                           preferred_element_type=F32)
    return (tn(zh, ones) + tn(zl, ones))[:, 0:1]


def _sigmoid(x):
    return 0.5 * jnp.tanh(0.5 * x) + 0.5


INV_BASE = 8
INV_NEWTON = 2
GDN_GROUP = 4
SUPER = GDN_GROUP * CHUNK
GDN_WIDTH = 2

A_K_COL = A_QK // A_DK
A_V_COL = 2 * A_QK // A_DV


def _inverse_steps(m, row, col):
    eye = (row == col).astype(F32)
    d = jnp.where(row // INV_BASE == col // INV_BASE, m, 0.0)
    x = eye - d
    p = _mm(d, d)
    yield
    steps = int(math.log2(INV_BASE)) - 1
    for i in range(steps):
        x = x + _mm(x, p)
        if i + 1 < steps:
            p = _mm(p, p)
        yield
    size = INV_BASE
    while size < CHUNK:
        c = jnp.where((row // (2 * size) == col // (2 * size)) & (row // size != col // size), m, 0.0)
        xc = _mm(x, c)
        yield
        x = x - _mm(xc, x)
        yield
        size *= 2
    for _ in range(INV_NEWTON):
        r = eye - x - _mm3(m, x)
        yield
        x = x + _mm(x, r)
        yield
    return x


def _drain(gen):
    while True:
        try:
            next(gen)
        except StopIteration as stop:
            return stop.value


def _interleave(*gens):
    live = list(gens)
    while live:
        for g in list(live):
            try:
                next(g)
            except StopIteration:
                live.remove(g)


def _diag_blocks_tall(x):
    return jnp.concatenate([x[i * CHUNK:(i + 1) * CHUNK, i * CHUNK:(i + 1) * CHUNK] for i in range(GDN_GROUP)], axis=0)


def _tall_to_block_diag(t, same):
    return jnp.where(same, jnp.concatenate([t] * GDN_GROUP, axis=1), 0.0)


def _block_sum(same, x):
    xh, xl = _split(jnp.broadcast_to(x, (SUPER, 128)))
    ones = same.astype(BF16)
    d = functools.partial(jnp.dot, preferred_element_type=F32)
    return (d(ones, xh) + d(ones, xl))[:, 0:1]


def _aligned_rows(index, size):
    start = index * size
    return pl.ds(start if isinstance(start, int) else pl.multiple_of(start, size), size)


def _super_rows(i):
    return _aligned_rows(i, SUPER)


def _chunk_rows(n):
    return _aligned_rows(n, CHUNK)


def _gdn_super_steps(q, k, v, gcb, gr, head, tinv_tall=None):
    lane = lax.broadcasted_iota(jnp.int32, (SUPER, 128), 1)
    row = lax.broadcasted_iota(jnp.int32, (SUPER, SUPER), 0)
    col = lax.broadcasted_iota(jnp.int32, (SUPER, SUPER), 1)
    same = row // CHUNK == col // CHUNK
    beta = jnp.sum(jnp.where(lane == head, gcb, 0.0), axis=1, keepdims=True)
    gc = jnp.sum(jnp.where(lane == A_HEADS + head, gcb, 0.0), axis=1, keepdims=True)
    g_last = jnp.sum(jnp.where(col == (row // CHUNK) * CHUNK + (CHUNK - 1), gr, 0.0), axis=1, keepdims=True)
    gamma = jnp.exp(gc)
    decay = jnp.where(same & (row >= col), jnp.exp(jnp.minimum(gc - gr, 0.0)), 0.0)
    kb = k * beta
    m = jnp.where(same & (row > col), _mm_nt(kb, k) * decay, 0.0)
    p = jnp.where(same & (row >= col), _mm_nt(q, k) * decay, 0.0)
    yield
    if tinv_tall is None:
        tinv = yield from _inverse_steps(m, row, col)
    else:
        tinv = _tall_to_block_diag(tinv_tall, same)
    u = _mm(tinv, v * beta)
    w = _mm(tinv, kb * gamma)
    yield
    e_tail = jnp.exp(g_last - gc)
    return dict(beta=beta, gc=gc, g_last=g_last, gamma=gamma, decay=decay, kb=kb, m=m,
                tinv=tinv, u=u, w=w, p=p, e_tail=e_tail, row=row, col=col, lane=lane, same=same)


def _gdn_super_common(q, k, v, gcb, gr, head, tinv_tall=None):
    return _drain(_gdn_super_steps(q, k, v, gcb, gr, head, tinv_tall))


def _store_scan_operands(rows, q, k, t, u_scr, w_scr, p_scr, qg_scr, ke_scr, gl_scr):
    u_scr[rows, :] = t["u"]
    w_scr[rows, :] = _bf(t["w"])
    p_scr[rows, :] = _bf(_diag_blocks_tall(t["p"]))
    qg_scr[rows, :] = _bf(q * t["gamma"])
    ke_scr[rows, :] = _bf(k * t["e_tail"])
    gl_scr[rows, :] = jnp.broadcast_to(jnp.exp(t["g_last"]), (SUPER, 128))


def _gdn_fwd_body(q_ref, k_ref, v_ref, gc_ref, gr_ref, wq_ref, wk_ref, wv_ref, z_ref, gn_ref,
                  o_ref, tinv_ref, st_ref, og_ref, qo_ref, ko_ref, vo_ref, s_scr, *sets):
    head = pl.program_id(1)
    n_super = q_ref.shape[0] // SUPER
    all_sets = [sets[6 * i:6 * i + 6] for i in range(2 * GDN_WIDTH)]
    whole = pl.ds(0, SUPER)

    def conv_silu(x_ref, w_ref, i):
        rows = _super_rows(i)
        x, w = x_ref[rows, :], w_ref[...]
        halo = jnp.zeros((8, x.shape[1]), F32) if i == 0 else x_ref[pl.ds(i * SUPER - 8, 8), :]
        ext = jnp.concatenate([halo, x], axis=0)
        c = x * w[A_CONV - 1:A_CONV, :]
        for j in range(1, A_CONV):
            c = c + pltpu.roll(ext, j, 0)[8:, :] * w[A_CONV - 1 - j:A_CONV - j, :]
        return c * _sigmoid(c)

    def unit(a):
        return a * lax.rsqrt(jnp.sum(a * a, axis=1, keepdims=True) + EPS)

    def prepare_steps(i, dst):
        rows = _super_rows(i)
        q = unit(conv_silu(q_ref, wq_ref, i)) * A_DK ** -0.5
        k = unit(conv_silu(k_ref, wk_ref, i))
        v = conv_silu(v_ref, wv_ref, i)
        qo_ref[rows, :], ko_ref[rows, :], vo_ref[rows, :] = q, k, v
        t = yield from _gdn_super_steps(q, k, v, gc_ref[rows, :], gr_ref[i], head)
        tinv_ref[rows, :] = _diag_blocks_tall(t["tinv"])
        _store_scan_operands(whole, q, k, t, *dst)

    def scan_steps(i, src):
        u_scr, w_scr, p_scr, qg_scr, ke_scr, gl_scr = src
        for j in range(GDN_GROUP):
            n = i * GDN_GROUP + j
            local = pl.ds(j * CHUNK, CHUNK)
            s = s_scr[...]
            sb = _bf(s)
            st_ref[n] = sb
            ws = jnp.dot(w_scr[local, :], sb, preferred_element_type=F32)
            yield
            vb = _bf(u_scr[local, :] - ws)
            o = (jnp.dot(qg_scr[local, :], sb, preferred_element_type=F32)
                 + jnp.dot(p_scr[local, :], vb, preferred_element_type=F32))
            s_new = s * gl_scr[local, :][0:1, 0:1] + lax.dot_general(
                ke_scr[local, :], vb, (((0,), (0,)), ((), ())), preferred_element_type=F32)
            yield
            rows = _chunk_rows(n)
            o_ref[rows, :] = o
            s_scr[...] = s_new
            silu, _ = _silu_parts(z_ref[rows, :])
            r = lax.rsqrt(jnp.mean(o * o, axis=1, keepdims=True) + EPS)
            og_ref[rows, :] = ((o * r * gn_ref[0:1, :]) * silu).astype(og_ref.dtype)

    def scan_many(first, srcs):
        for j, src in enumerate(srcs):
            yield from scan_steps(first + j, src)

    groups = [all_sets[:GDN_WIDTH], all_sets[GDN_WIDTH:]]
    _interleave(*[prepare_steps(j, groups[0][j]) for j in range(GDN_WIDTH)])
    s_scr[...] = jnp.zeros_like(s_scr)
    for g in range(n_super // GDN_WIDTH):
        cur, nxt = groups[g % 2], groups[(g + 1) % 2]
        first = g * GDN_WIDTH
        following = [prepare_steps(first + GDN_WIDTH + j, nxt[j]) for j in range(GDN_WIDTH)
                     if first + GDN_WIDTH + j < n_super]
        _interleave(scan_many(first, cur), *following)


def _gdn_in_specs(s_len, n_super, from_proj):
    k_col, v_col = (A_K_COL, A_V_COL) if from_proj else (0, 0)
    return [
        pl.BlockSpec((s_len, A_DK), lambda b, h: (b, h)),
        pl.BlockSpec((s_len, A_DK), lambda b, h: (b, k_col + h)),
        pl.BlockSpec((s_len, A_DV), lambda b, h: (b, v_col + h)),
        pl.BlockSpec((s_len, 128), lambda b, h: (b, 0)),
        pl.BlockSpec((None, None, n_super, 1, SUPER), lambda b, h: (b, A_HEADS + h, 0, 0, 0)),
    ]


def _gdn_scan_scratch(s_len):
    return [pltpu.VMEM((A_DK, A_DV), F32), pltpu.VMEM((s_len, A_DV), F32),
            pltpu.VMEM((s_len, A_DK), BF16), pltpu.VMEM((s_len, CHUNK), BF16),
            pltpu.VMEM((s_len, A_DK), BF16), pltpu.VMEM((s_len, A_DK), BF16),
            pltpu.VMEM((s_len, 128), F32)]


def gdn_fwd(proj_a, conv_w8, gates_col, gates_row, norm_g8, n_seq):
    t_rows = proj_a.shape[0]
    s_len = t_rows // n_seq
    n_chunks = s_len // CHUNK
    qk_spec = pl.BlockSpec((s_len, A_DK), lambda b, h: (b, h))
    v_spec = pl.BlockSpec((s_len, A_DV), lambda b, h: (b, h))
    return pl.pallas_call(
        _gdn_fwd_body, name="gdn_fwd", grid=(n_seq, A_HEADS),
        in_specs=_gdn_in_specs(s_len, s_len // SUPER, True) + [
            pl.BlockSpec((8, A_DK), lambda b, h: (0, h)),
            pl.BlockSpec((8, A_DK), lambda b, h: (0, A_K_COL + h)),
            pl.BlockSpec((8, A_DV), lambda b, h: (0, A_V_COL + h)),
            pl.BlockSpec((s_len, A_DV), lambda b, h: (b, A_Z_COL + h)),
            pl.BlockSpec((8, A_DV), lambda b, h: (0, 0)),
        ],
        out_specs=[
            v_spec,
            pl.BlockSpec((s_len, CHUNK), lambda b, h: (b * A_HEADS + h, 0)),
            pl.BlockSpec((None, n_chunks, A_DK, A_DV), lambda b, h: (b * A_HEADS + h, 0, 0, 0)),
            v_spec, qk_spec, qk_spec, v_spec,
        ],
        out_shape=[
            jax.ShapeDtypeStruct((t_rows, A_VW), F32),
            jax.ShapeDtypeStruct((n_seq * A_HEADS * s_len, CHUNK), F32),
            jax.ShapeDtypeStruct((n_seq * A_HEADS, n_chunks, A_DK, A_DV), BF16),
            jax.ShapeDtypeStruct((t_rows, A_VW), BF16),
            jax.ShapeDtypeStruct((t_rows, A_QK), F32),
            jax.ShapeDtypeStruct((t_rows, A_QK), F32),
            jax.ShapeDtypeStruct((t_rows, A_VW), F32),
        ],
        scratch_shapes=_gdn_scan_scratch(SUPER) + (2 * GDN_WIDTH - 1) * _gdn_scan_scratch(SUPER)[1:],
        compiler_params=_params(("arbitrary", "arbitrary")),
    )(proj_a, proj_a, proj_a, gates_col, gates_row, conv_w8, conv_w8, conv_w8, proj_a, norm_g8)


def _gdn_bwd_body(q_ref, k_ref, v_ref, gc_ref, gr_ref, tinv_ref, st_ref, dog_ref, oa_ref, z_ref, gn_ref,
                  xq_ref, xk_ref, xv_ref, wq_ref, wk_ref, wv_ref,
                  dxq_ref, dxk_ref, dxv_ref, dgc_ref, dz_ref, dgn_ref, dwq_ref, dwk_ref, dwv_ref,
                  ds_scr, cq_scr, ck_scr, cv_scr, *sets):
    head = pl.program_id(1)
    n_super = q_ref.shape[0] // SUPER
    ops = (sets[0:7], sets[7:14])
    res = (sets[14:21], sets[21:28])
    whole = pl.ds(0, SUPER)
    tn = functools.partial(lax.dot_general, dimension_numbers=(((0,), (0,)), ((), ())), preferred_element_type=F32)
    nt = functools.partial(lax.dot_general, dimension_numbers=(((1,), (1,)), ((), ())), preferred_element_type=F32)

    @pl.when(head == 0)
    def _():
        dgc_ref[...] = jnp.zeros_like(dgc_ref)

    @pl.when((head == 0) & (pl.program_id(0) == 0))
    def _():
        dgn_ref[...] = jnp.zeros_like(dgn_ref)

    carry = (cq_scr, ck_scr, cv_scr)
    for ref in carry + (dwq_ref, dwk_ref, dwv_ref):
        ref[...] = jnp.zeros_like(ref)

    def common_steps(i):
        rows = _super_rows(i)
        q, k, v = q_ref[rows, :], k_ref[rows, :], v_ref[rows, :]
        t = yield from _gdn_super_steps(q, k, v, gc_ref[rows, :], gr_ref[i], head, tinv_tall=tinv_ref[rows, :])
        return rows, q, k, v, t

    def stage_p(i, parity):
        rows, q, k, _, t = yield from common_steps(i)
        _store_scan_operands(whole, q, k, t, *ops[parity][:6])
        o, d_og, gain = oa_ref[rows, :], dog_ref[rows, :], gn_ref[0:1, :]
        r = lax.rsqrt(jnp.mean(o * o, axis=1, keepdims=True) + EPS)
        silu, dsilu = _silu_parts(z_ref[rows, :])
        xr = o * r
        d_on = d_og * silu
        dz_ref[rows, :] = (d_og * (xr * gain) * dsilu).astype(dz_ref.dtype)
        u = d_on * gain
        ops[parity][6][...] = r * u - xr * (r * r) * jnp.mean(u * o, axis=1, keepdims=True)
        dgn_ref[0:1, :] += jnp.sum(d_on * xr, axis=0, keepdims=True)

    def stage_s(i, parity):
        u_scr, w_scr, p_scr, qg_scr, ke_scr, gl_scr, do_scr = ops[parity]
        vn_scr, dvn_scr, dqg_scr, dw_scr, dkt_scr, sds_scr, dof_scr = res[parity]
        for j in reversed(range(GDN_GROUP)):
            n = i * GDN_GROUP + j
            local = pl.ds(j * CHUNK, CHUNK)
            ds_next = ds_scr[...]
            dsb = _bf(ds_next)
            sb = st_ref[n]
            s = sb.astype(F32)
            d_o = do_scr[local, :]
            dof_scr[local, :] = d_o
            d_ob = _bf(d_o)
            w_s = jnp.dot(w_scr[local, :], sb, preferred_element_type=F32)
            d_vn = tn(p_scr[local, :], d_ob) + jnp.dot(ke_scr[local, :], dsb, preferred_element_type=F32)
            d_qg = nt(d_ob, sb)
            qg_do = tn(qg_scr[local, :], d_ob)
            yield
            v_new = u_scr[local, :] - w_s
            d_vnb = _bf(d_vn)
            d_w = -nt(d_vnb, sb)
            d_kt = nt(_bf(v_new), dsb)
            w_dvn = tn(w_scr[local, :], d_vnb)
            yield
            vn_scr[local, :] = v_new
            dvn_scr[local, :] = d_vn
            dqg_scr[local, :] = d_qg
            dw_scr[local, :] = d_w
            dkt_scr[local, :] = d_kt
            sds = jnp.sum(jnp.sum(s * ds_next, axis=1, keepdims=True), axis=0, keepdims=True)
            sds_scr[local, :] = jnp.broadcast_to(sds, (CHUNK, 128))
            ds_scr[...] = qg_do + gl_scr[local, :][0:1, 0:1] * ds_next - w_dvn

    def conv_bwd(i, rows, x_ref, w_ref, dy, norm_scale, dx_ref, dw_ref, dc_above):
        x, w = x_ref[rows, :], w_ref[...]
        above = x_ref[pl.ds(pl.multiple_of(jnp.maximum(i * SUPER - 8, 0), 8), 8), :]
        ext = jnp.concatenate([jnp.where(i > 0, above, 0.0), x], axis=0)
        c = x * w[A_CONV - 1:A_CONV, :]
        for j in range(1, A_CONV):
            c = c + pltpu.roll(ext, j, 0)[8:, :] * w[A_CONV - 1 - j:A_CONV - j, :]
        sig = _sigmoid(c)
        a = c * sig
        if norm_scale is None:
            da = dy
        else:
            rn = lax.rsqrt(jnp.sum(a * a, axis=1, keepdims=True) + EPS)
            da = norm_scale * (rn * dy - a * (rn * rn * rn) * jnp.sum(dy * a, axis=1, keepdims=True))
        dc = da * (sig * (1.0 + c * (1.0 - sig)))
        ext_dc = jnp.concatenate([dc, dc_above[...]], axis=0)
        dc_above[...] = dc[0:8, :]
        dx = dc * w[A_CONV - 1:A_CONV, :]
        dw_ref[A_CONV - 1:A_CONV, :] += jnp.sum(dc * x, axis=0, keepdims=True)
        for j in range(1, A_CONV):
            dcs = pltpu.roll(ext_dc, SUPER + 8 - j, 0)[:SUPER, :]
            dx = dx + dcs * w[A_CONV - 1 - j:A_CONV - j, :]
            dw_ref[A_CONV - 1 - j:A_CONV - j, :] += jnp.sum(dcs * x, axis=0, keepdims=True)
        dx_ref[rows, :] = dx.astype(dx_ref.dtype)

    def stage_f(i, parity):
        vn_scr, dvn_scr, dqg_scr, dw_scr, dkt_scr, sds_scr, dof_scr = res[parity]
        rows, q, k, v, t = yield from common_steps(i)
        beta, gamma, decay, kb, e_tail = t["beta"], t["gamma"], t["decay"], t["kb"], t["e_tail"]
        row, col, lane, same = t["row"], t["col"], t["lane"], t["same"]
        d_o = dof_scr[...]
        v_new, d_vn = vn_scr[...], dvn_scr[...]
        d_qg, d_w, d_kt = dqg_scr[...], dw_scr[...], dkt_scr[...]
        gamma_last = jnp.exp(t["g_last"])

        d_p = jnp.where(same & (row >= col), _mm_nt(d_o, v_new), 0.0)
        d_ru = _mm_tn(t["tinv"], d_vn)
        d_rw = _mm_tn(t["tinv"], d_w)
        yield
        d_m = jnp.where(same & (row > col), -(_mm_nt(d_ru, t["u"]) + _mm_nt(d_rw, t["w"])), 0.0)
        yield

        x_p = d_p * decay
        y_m = d_m * decay
        d_kb = _mm(y_m, k) + d_rw * gamma
        d_q = _mm(x_p, k) + d_qg * gamma
        d_k = _mm_tn(x_p, q) + _mm_tn(y_m, kb) + d_kb * beta + d_kt * e_tail
        d_v = d_ru * beta
        conv_bwd(i, rows, xq_ref, wq_ref, d_q, A_DK ** -0.5, dxq_ref, dwq_ref, carry[0])
        conv_bwd(i, rows, xk_ref, wk_ref, d_k, 1.0, dxk_ref, dwk_ref, carry[1])
        conv_bwd(i, rows, xv_ref, wv_ref, d_v, None, dxv_ref, dwv_ref, carry[2])

        d_beta = (jnp.sum(d_ru * v, axis=1, keepdims=True)
                  + jnp.sum(d_kb * k, axis=1, keepdims=True))
        z = d_p * t["p"] + d_m * t["m"]
        eps_tail = jnp.sum(d_kt * k, axis=1, keepdims=True) * e_tail
        d_gc = (jnp.sum(z, axis=1, keepdims=True) - _colsum_as_col(z)
                + jnp.sum(d_qg * q, axis=1, keepdims=True) * gamma
                + jnp.sum(d_rw * kb, axis=1, keepdims=True) * gamma
                - eps_tail)
        d_glast = _block_sum(same, eps_tail) + gamma_last * sds_scr[...][:, 0:1]
        yield
        rcol = lax.broadcasted_iota(jnp.int32, (SUPER, 1), 0)
        d_gc = d_gc + jnp.where(rcol % CHUNK == CHUNK - 1, d_glast, 0.0)
        dgc_ref[rows, :] += (jnp.where(lane == head, d_beta, 0.0)
                             + jnp.where(lane == A_HEADS + head, d_gc, 0.0))

    last = n_super - 1
    _drain(stage_p(last, 1))
    ds_scr[...] = jnp.zeros_like(ds_scr)
    _interleave(stage_s(last, 1), stage_p(last - 1, 0))

    def pair(k, carry):
        i = last - 1 - 2 * k
        _interleave(stage_s(i, 0), stage_f(i + 1, 1), stage_p(i - 1, 1))
        _interleave(stage_s(i - 1, 1), stage_f(i, 0), stage_p(i - 2, 0))
        return carry

    lax.fori_loop(0, n_super // 2 - 1, pair, 0)
    _interleave(stage_s(0, 0), stage_f(1, 1))
    _drain(stage_f(0, 0))


def gdn_bwd(q, k, v, gates_col, gates_row, tinv, states, d_og, o, proj_a, conv_w8, norm_g8, n_seq):
    t_rows = q.shape[0]
    s_len = t_rows // n_seq
    n_chunks = s_len // CHUNK
    qk_spec = pl.BlockSpec((s_len, A_DK), lambda b, h: (b, h))
    v_spec = pl.BlockSpec((s_len, A_DV), lambda b, h: (b, h))
    gate_spec = pl.BlockSpec((s_len, 128), lambda b, h: (b, 0))
    gain_spec = pl.BlockSpec((8, A_DV), lambda b, h: (0, 0))
    dw_qk_spec = pl.BlockSpec((None, 8, A_DK), lambda b, h: (b, 0, h))
    dw_v_spec = pl.BlockSpec((None, 8, A_DV), lambda b, h: (b, 0, h))
    ops_set = _gdn_scan_scratch(SUPER)[1:] + [pltpu.VMEM((SUPER, A_DV), F32)]
    res_set = [pltpu.VMEM((SUPER, A_DV), F32), pltpu.VMEM((SUPER, A_DV), F32),
               pltpu.VMEM((SUPER, A_DK), F32), pltpu.VMEM((SUPER, A_DK), F32),
               pltpu.VMEM((SUPER, A_DK), F32), pltpu.VMEM((SUPER, 128), F32), pltpu.VMEM((SUPER, A_DV), F32)]
    return pl.pallas_call(
        _gdn_bwd_body, name="gdn_bwd", grid=(n_seq, A_HEADS),
        in_specs=_gdn_in_specs(s_len, s_len // SUPER, False) + [
            pl.BlockSpec((s_len, CHUNK), lambda b, h: (b * A_HEADS + h, 0)),
            pl.BlockSpec((None, n_chunks, A_DK, A_DV), lambda b, h: (b * A_HEADS + h, 0, 0, 0)),
            v_spec, v_spec,
            pl.BlockSpec((s_len, A_DV), lambda b, h: (b, A_Z_COL + h)),
            gain_spec,
            pl.BlockSpec((s_len, A_DK), lambda b, h: (b, h)),
            pl.BlockSpec((s_len, A_DK), lambda b, h: (b, A_K_COL + h)),
            pl.BlockSpec((s_len, A_DV), lambda b, h: (b, A_V_COL + h)),
            pl.BlockSpec((8, A_DK), lambda b, h: (0, h)),
            pl.BlockSpec((8, A_DK), lambda b, h: (0, A_K_COL + h)),
            pl.BlockSpec((8, A_DV), lambda b, h: (0, A_V_COL + h)),
        ],
        out_specs=[qk_spec, qk_spec, v_spec, gate_spec, v_spec, gain_spec, dw_qk_spec, dw_qk_spec, dw_v_spec],
        out_shape=[
            jax.ShapeDtypeStruct((t_rows, A_QK), BF16),
            jax.ShapeDtypeStruct((t_rows, A_QK), BF16),
            jax.ShapeDtypeStruct((t_rows, A_VW), BF16),
            jax.ShapeDtypeStruct((t_rows, 128), F32),
            jax.ShapeDtypeStruct((t_rows, A_VW), BF16),
            jax.ShapeDtypeStruct((8, A_DV), F32),
            jax.ShapeDtypeStruct((n_seq, 8, A_QK), F32),
            jax.ShapeDtypeStruct((n_seq, 8, A_QK), F32),
            jax.ShapeDtypeStruct((n_seq, 8, A_VW), F32),
        ],
        scratch_shapes=(_gdn_scan_scratch(SUPER)[:1]
                        + [pltpu.VMEM((8, A_DK), F32), pltpu.VMEM((8, A_DK), F32), pltpu.VMEM((8, A_DV), F32)]
                        + 2 * ops_set + 2 * res_set),
        compiler_params=_params(("arbitrary", "arbitrary")),
    )(q, k, v, gates_col, gates_row, tinv, states, d_og, o, proj_a, norm_g8,
      proj_a, proj_a, proj_a, conv_w8, conv_w8, conv_w8)


GATE_TILE = 512


def _softplus(y):
    return jnp.maximum(y, 0.0) + jnp.log1p(jnp.exp(-jnp.abs(y)))


def _gate_values(x, prm):
    beta = _sigmoid(x)
    y = x + prm[1:2, :]
    neg_a = -jnp.exp(prm[0:1, :])
    g = neg_a * _softplus(y)
    return beta, y, neg_a, g


def _a_gates_fwd_body(x_ref, prm_ref, gc_ref, gr_ref):
    x = x_ref[...]
    tm = x.shape[0]
    beta, _, _, g = _gate_values(x, prm_ref[...])
    in_chunk = lax.broadcasted_iota(jnp.int32, (tm, 1), 0) % CHUNK
    s = 1
    while s < CHUNK:
        g = g + jnp.where(in_chunk >= s, pltpu.roll(g, s, 0), 0.0)
        s *= 2
    lane = lax.broadcasted_iota(jnp.int32, x.shape, 1)
    out = jnp.where(lane < A_HEADS, beta, jnp.where(lane < 2 * A_HEADS, g, 0.0))
    gc_ref[...] = out
    gr_ref[...] = out.T[0:2 * A_HEADS, :]


def a_gates_fwd(proj_a, prm, n_seq):
    t_rows = proj_a.shape[0]
    s_len = t_rows // n_seq
    tm = min(GATE_TILE, s_len)
    n_t = s_len // tm
    return pl.pallas_call(
        _a_gates_fwd_body, name="a_gates_fwd", grid=(n_seq, n_t),
        in_specs=[pl.BlockSpec((tm, 128), lambda b, i: (b * n_t + i, A_GATE_COL)),
                  pl.BlockSpec((8, 128), lambda b, i: (0, 0))],
        out_specs=[pl.BlockSpec((tm, 128), lambda b, i: (b * n_t + i, 0)),
                   pl.BlockSpec((None, 2 * A_HEADS, tm), lambda b, i: (b, 0, i))],
        out_shape=[jax.ShapeDtypeStruct((t_rows, 128), F32),
                   jax.ShapeDtypeStruct((n_seq, 2 * A_HEADS, s_len), F32)],
        compiler_params=_params(("arbitrary", "arbitrary")),
    )(proj_a, prm)


def _a_gates_bwd_body(x_ref, prm_ref, dgc_ref, dx_ref, dprm_ref):
    first = (pl.program_id(0) == 0) & (pl.program_id(1) == 0)
    x = x_ref[...]
    tm = x.shape[0]
    beta, y, neg_a, g = _gate_values(x, prm_ref[...])
    d = dgc_ref[...]
    in_chunk = lax.broadcasted_iota(jnp.int32, (tm, 1), 0) % CHUNK
    dg = d
    s = 1
    while s < CHUNK:
        dg = dg + jnp.where(in_chunk < CHUNK - s, pltpu.roll(dg, tm - s, 0), 0.0)
        s *= 2
    lane = lax.broadcasted_iota(jnp.int32, x.shape, 1)
    is_decay = (lane >= A_HEADS) & (lane < 2 * A_HEADS)
    d_alogit = jnp.where(is_decay, dg * neg_a * _sigmoid(y), 0.0)
    dx_ref[...] = jnp.where(lane < A_HEADS, d * beta * (1.0 - beta), d_alogit).astype(dx_ref.dtype)

    @pl.when(first)
    def _():
        dprm_ref[...] = jnp.zeros_like(dprm_ref)

    dprm_ref[0:1, :] += jnp.sum(jnp.where(is_decay, dg * g, 0.0), axis=0, keepdims=True)
    dprm_ref[1:2, :] += jnp.sum(d_alogit, axis=0, keepdims=True)


def a_gates_bwd(proj_a, prm, dgates_col, n_seq):
    t_rows = proj_a.shape[0]
    s_len = t_rows // n_seq
    tm = min(GATE_TILE, s_len)
    n_t = s_len // tm
    return pl.pallas_call(
        _a_gates_bwd_body, name="a_gates_bwd", grid=(n_seq, n_t),
        in_specs=[pl.BlockSpec((tm, 128), lambda b, i: (b * n_t + i, A_GATE_COL)),
                  pl.BlockSpec((8, 128), lambda b, i: (0, 0)),
                  pl.BlockSpec((tm, 128), lambda b, i: (b * n_t + i, 0))],
        out_specs=[pl.BlockSpec((tm, 128), lambda b, i: (b * n_t + i, 0)),
                   pl.BlockSpec((8, 128), lambda b, i: (0, 0))],
        out_shape=[jax.ShapeDtypeStruct((t_rows, 128), BF16),
                   jax.ShapeDtypeStruct((8, 128), F32)],
        compiler_params=_params(("arbitrary", "arbitrary")),
    )(proj_a, prm, dgates_col)


ROW_TILE = 512
A_Z_COL = (2 * A_QK + A_VW) // A_DV


def _silu_parts(z):
    sig = _sigmoid(z)
    return z * sig, sig * (1.0 + z * (1.0 - sig))


NEG_BIG = -1e30
ATT_SCALE = B_DH ** -0.5


def _swap_rope_halves(x):
    src = lax.broadcasted_iota(jnp.int32, (B_DH, B_DH), 0)
    dst = lax.broadcasted_iota(jnp.int32, (B_DH, B_DH), 1)
    pick = ((dst < ROPE_HALF) & (src == dst + ROPE_HALF)) | (
        (dst >= ROPE_HALF) & (dst < ROPE_DIMS) & (src == dst - ROPE_HALF))
    return jnp.dot(_bf(x), pick.astype(BF16), preferred_element_type=F32)


def _norm_rope(x, gain, cos_t, sin_t):
    r = lax.rsqrt(jnp.mean(x * x, axis=1, keepdims=True) + EPS)
    xn = x * r * gain
    return xn * cos_t + _swap_rope_halves(xn) * sin_t, r


def _norm_rope_bwd(x, r, gain, cos_t, sin_t, dy):
    d_xn = dy * cos_t + _swap_rope_halves(dy * sin_t)
    xr = x * r
    u = d_xn * gain
    dx = r * u - xr * (r * r) * jnp.mean(u * x, axis=1, keepdims=True)
    return dx, jnp.sum(d_xn * xr, axis=0, keepdims=True)


def _stream_rows(idx, dilation, s_len):
    nb = s_len // dilation // B_BLOCK
    r = idx // nb
    m = idx % nb
    cur = r + m * (B_BLOCK * dilation)
    prev = r + jnp.maximum(m - 1, 0) * (B_BLOCK * dilation)
    return cur, prev, m > 0


def _rows(start, dilation):
    if dilation == 1:
        return pl.ds(start, B_BLOCK)
    return pl.ds(start, B_BLOCK, stride=dilation)


ATT_UNROLL = 16


def _band_mask(has_prev):
    qi = lax.broadcasted_iota(jnp.int32, (B_BLOCK, 2 * B_BLOCK), 0)
    kj = lax.broadcasted_iota(jnp.int32, (B_BLOCK, 2 * B_BLOCK), 1)
    return ((kj < B_BLOCK) & (kj >= qi) & has_prev) | ((kj >= B_BLOCK) & (kj - B_BLOCK <= qi))


def _attn_fwd_body(qkv_ref, z_ref, cos_ref, sin_ref, gain_ref, og_ref, o_ref, lse_ref,
                   qn_scr, kn_scr, og_scr, lg_scr):
    head, grp = pl.program_id(1), pl.program_id(2)
    s_len = z_ref.shape[0]
    n_blocks = s_len // B_BLOCK
    cos_t, sin_t = cos_ref[...], sin_ref[...]

    for gi, dil in enumerate(B_DILATIONS):
        @pl.when(grp == gi)
        def _(gi=gi, dil=dil):
            qn_scr[...], _ = _norm_rope(qkv_ref[0], gain_ref[gi:gi + 1, :], cos_t, sin_t)
            kn_scr[...], _ = _norm_rope(qkv_ref[1], gain_ref[B_GROUPS + gi:B_GROUPS + gi + 1, :], cos_t, sin_t)

            ones = jnp.ones((2 * B_BLOCK, B_DH), BF16)

            def blocks(it, carry):
                scored = []
                for j in range(ATT_UNROLL):
                    cur, prev, has_prev = _stream_rows(it * ATT_UNROLL + j, dil, s_len)
                    rc, rp = _rows(cur, dil), _rows(prev, dil)
                    k2 = jnp.concatenate([kn_scr[rp, :], kn_scr[rc, :]], axis=0)
                    scored.append((rc, rp, has_prev, _mm_nt(qn_scr[rc, :], k2) * ATT_SCALE))
                summed = []
                for rc, rp, has_prev, s in scored:
                    s = jnp.where(_band_mask(has_prev), s, NEG_BIG)
                    mx = jnp.max(s, axis=1, keepdims=True)
                    v2 = jnp.concatenate([qkv_ref.at[2][rp, :], qkv_ref.at[2][rc, :]], axis=0)
                    acc = jnp.dot(_bf(jnp.exp(s - mx)), jnp.concatenate([_bf(v2), ones], axis=1),
                                  preferred_element_type=F32)
                    summed.append((rc, mx, acc))
                for rc, mx, acc in summed:
                    den = acc[:, B_DH:B_DH + 1]
                    og_scr.at[gi][rc, :] = acc[:, :B_DH] / den
                    lg_scr.at[gi][rc, :] = jnp.broadcast_to(mx + jnp.log(den), (B_BLOCK, B_DH))
                return carry

            lax.fori_loop(0, n_blocks // ATT_UNROLL, blocks, 0)

    @pl.when(grp == B_GROUPS - 1)
    def _():
        l0, l1, l2 = lg_scr[0], lg_scr[1], lg_scr[2]
        mx = jnp.maximum(jnp.maximum(l0, l1), l2)
        w0, w1, w2 = jnp.exp(l0 - mx), jnp.exp(l1 - mx), jnp.exp(l2 - mx)
        den = w0 + w1 + w2
        o = (w0 * og_scr[0] + w1 * og_scr[1] + w2 * og_scr[2]) / den
        silu, _ = _silu_parts(z_ref[...])
        o_ref[...] = o
        og_ref[...] = (o * silu).astype(og_ref.dtype)
        @pl.when(head == 0)
        def _():
            lse_ref[...] = jnp.zeros_like(lse_ref)

        lane = lax.broadcasted_iota(jnp.int32, o.shape, 1)
        lse_ref[...] = jnp.where(lane == head, mx + jnp.log(den), lse_ref[...])


def attn_fwd(proj_b, cos_t, sin_t, gains8, n_seq):
    t_rows = proj_b.shape[1]
    s_len = t_rows // n_seq
    head_blk = pl.BlockSpec((s_len, B_DH), lambda b, h, g: (b, h))
    seq_blk = pl.BlockSpec((s_len, 128), lambda b, h, g: (b, 0))
    return pl.pallas_call(
        _attn_fwd_body, name="attn_fwd", grid=(n_seq, B_HEADS, B_GROUPS),
        in_specs=[
            pl.BlockSpec((3, s_len, B_DH), lambda b, h, g: (g, b, h)),
            pl.BlockSpec((None, s_len, B_DH), lambda b, h, g: (B_PIECES - 1, b, h)),
            seq_blk, seq_blk,
            pl.BlockSpec((8, 128), lambda b, h, g: (0, 0)),
        ],
        out_specs=[head_blk, head_blk, seq_blk],
        out_shape=[jax.ShapeDtypeStruct((t_rows, B_W), BF16),
                   jax.ShapeDtypeStruct((t_rows, B_W), F32),
                   jax.ShapeDtypeStruct((t_rows, 128), F32)],
        scratch_shapes=[pltpu.VMEM((s_len, B_DH), F32), pltpu.VMEM((s_len, B_DH), F32),
                        pltpu.VMEM((B_GROUPS, s_len, B_DH), F32), pltpu.VMEM((B_GROUPS, s_len, B_DH), F32)],
        compiler_params=_params(("arbitrary", "arbitrary", "arbitrary")),
    )(proj_b, proj_b, cos_t, sin_t, gains8)


def _attn_bwd_body(qkv_ref, z_ref, cos_ref, sin_ref, gain_ref, dog_ref, o_ref, lse_ref,
                   dqkv_ref, dz_ref, dgain_ref,
                   qn_scr, kn_scr, dqn_scr, dkn_scr, do_scr, dl_scr, ls_scr, dv_scr):
    head, grp = pl.program_id(1), pl.program_id(2)
    first = (pl.program_id(0) == 0) & (head == 0) & (grp == 0)
    s_len = z_ref.shape[0]
    n_blocks = s_len // B_BLOCK
    cos_t, sin_t = cos_ref[...], sin_ref[...]

    @pl.when(first)
    def _():
        dgain_ref[...] = jnp.zeros_like(dgain_ref)

    @pl.when(grp == 0)
    def _():
        d_og, o = dog_ref[...], o_ref[...]
        silu, dsilu = _silu_parts(z_ref[...])
        d_o = d_og * silu
        dz_ref[...] = (d_og * o * dsilu).astype(dz_ref.dtype)
        do_scr[...] = d_o
        dl_scr[...] = jnp.broadcast_to(jnp.sum(d_o * o, axis=1, keepdims=True), o.shape)
        lane = lax.broadcasted_iota(jnp.int32, o.shape, 1)
        ls_scr[...] = jnp.broadcast_to(
            jnp.sum(jnp.where(lane == head, lse_ref[...], 0.0), axis=1, keepdims=True), o.shape)

    for gi, dil in enumerate(B_DILATIONS):
        @pl.when(grp == gi)
        def _(gi=gi, dil=dil):
            q_raw, k_raw = qkv_ref[0], qkv_ref[1]
            gq = gain_ref[gi:gi + 1, :]
            gk = gain_ref[B_GROUPS + gi:B_GROUPS + gi + 1, :]
            qn_scr[...], rq = _norm_rope(q_raw, gq, cos_t, sin_t)
            kn_scr[...], rk = _norm_rope(k_raw, gk, cos_t, sin_t)
            def blocks(it, carry):
                scored = []
                for j in range(ATT_UNROLL):
                    cur, prev, has_prev = _stream_rows(it * ATT_UNROLL + j, dil, s_len)
                    rc, rp = _rows(cur, dil), _rows(prev, dil)
                    qb, d_ob = _bf(qn_scr[rc, :]), _bf(do_scr[rc, :])
                    k2 = _bf(jnp.concatenate([kn_scr[rp, :], kn_scr[rc, :]], axis=0))
                    v2 = _bf(jnp.concatenate([qkv_ref.at[2][rp, :], qkv_ref.at[2][rc, :]], axis=0))
                    scored.append((rc, rp, has_prev, qb, d_ob, k2,
                                   _mm_nt(qb, k2) * ATT_SCALE, _mm_nt(d_ob, v2)))
                grads = []
                for rc, rp, has_prev, qb, d_ob, k2, s, d_p in scored:
                    p = jnp.exp(jnp.where(_band_mask(has_prev), s - ls_scr[rc, :][:, 0:1], NEG_BIG))
                    ds = _bf(p * (d_p - dl_scr[rc, :][:, 0:1]))
                    grads.append((rc, rp, has_prev,
                                  _mm(ds, k2) * ATT_SCALE, _mm_tn(ds, qb) * ATT_SCALE, _mm_tn(_bf(p), d_ob)))
                for j, (rc, rp, has_prev, dq, dk2, dv2) in enumerate(grads):
                    dqn_scr[rc, :] = dq
                    if j == 0:
                        @pl.when(has_prev)
                        def _():
                            dkn_scr[rp, :] += dk2[:B_BLOCK]
                            dv_scr[rp, :] += dv2[:B_BLOCK]
                    if j + 1 < ATT_UNROLL:
                        dkn_scr[rc, :] = dk2[B_BLOCK:] + grads[j + 1][4][:B_BLOCK]
                        dv_scr[rc, :] = dv2[B_BLOCK:] + grads[j + 1][5][:B_BLOCK]
                    else:
                        dkn_scr[rc, :] = dk2[B_BLOCK:]
                        dv_scr[rc, :] = dv2[B_BLOCK:]
                return carry

            lax.fori_loop(0, n_blocks // ATT_UNROLL, blocks, 0)
            dq, dgq = _norm_rope_bwd(q_raw, rq, gq, cos_t, sin_t, dqn_scr[...])
            dk, dgk = _norm_rope_bwd(k_raw, rk, gk, cos_t, sin_t, dkn_scr[...])
            dqkv_ref[0] = dq.astype(dqkv_ref.dtype)
            dqkv_ref[1] = dk.astype(dqkv_ref.dtype)
            dqkv_ref[2] = dv_scr[...].astype(dqkv_ref.dtype)
            dgain_ref[gi:gi + 1, :] += dgq
            dgain_ref[B_GROUPS + gi:B_GROUPS + gi + 1, :] += dgk


def attn_bwd(proj_b, cos_t, sin_t, gains8, d_og, o, lse, n_seq):
    t_rows = proj_b.shape[1]
    s_len = t_rows // n_seq
    head_blk = pl.BlockSpec((s_len, B_DH), lambda b, h, g: (b, h))
    seq_blk = pl.BlockSpec((s_len, 128), lambda b, h, g: (b, 0))
    grp_blk = pl.BlockSpec((3, s_len, B_DH), lambda b, h, g: (g, b, h))
    gain_blk = pl.BlockSpec((8, 128), lambda b, h, g: (0, 0))
    return pl.pallas_call(
        _attn_bwd_body, name="attn_bwd", grid=(n_seq, B_HEADS, B_GROUPS),
        in_specs=[
            grp_blk,
            pl.BlockSpec((None, s_len, B_DH), lambda b, h, g: (B_PIECES - 1, b, h)),
            seq_blk, seq_blk, gain_blk, head_blk, head_blk, seq_blk,
        ],
        out_specs=[grp_blk, head_blk, gain_blk],
        out_shape=[jax.ShapeDtypeStruct((3 * B_GROUPS, t_rows, B_W), BF16),
                   jax.ShapeDtypeStruct((t_rows, B_W), BF16),
                   jax.ShapeDtypeStruct((8, 128), F32)],
        scratch_shapes=[pltpu.VMEM((s_len, B_DH), F32) for _ in range(8)],
        compiler_params=_params(("arbitrary", "arbitrary", "arbitrary")),
    )(proj_b, proj_b, cos_t, sin_t, gains8, d_og, o, lse)


def rope_tables(positions):
    inv_freq = ROPE_THETA ** (-jnp.arange(0, ROPE_DIMS, 2, dtype=F32) / ROPE_DIMS)
    ang = positions.astype(F32)[:, None] * inv_freq
    cos, sin = jnp.cos(ang), jnp.sin(ang)
    t_rows = positions.shape[0]
    rest = B_DH - ROPE_DIMS
    cos_t = jnp.concatenate([cos, cos, jnp.ones((t_rows, rest), F32)], axis=1)
    sin_t = jnp.concatenate([-sin, sin, jnp.zeros((t_rows, rest), F32)], axis=1)
    return cos_t, sin_t


def _rms_fwd_body(x_ref, g_ref, *rest, layer):
    h_ref, ht_ref = rest[-2:]
    x = x_ref[...]
    r = lax.rsqrt(jnp.mean(x * x, axis=1, keepdims=True) + EPS)
    h = x * r * g_ref[layer:layer + 1, :]
    h_ref[...] = h.astype(h_ref.dtype)
    ht_ref[...] = h.T.astype(ht_ref.dtype)


def rms_fwd(x, gains8, layer, after=None):
    t_rows, d = x.shape
    tm = min(ROW_TILE, t_rows)
    in_specs = [pl.BlockSpec((tm, d), lambda i: (i, 0)), pl.BlockSpec((8, d), lambda i: (0, 0))]
    args = [x, gains8]
    if after is not None:
        in_specs.append(HBM_SPEC)
        args.append(after)
    return pl.pallas_call(
        functools.partial(_rms_fwd_body, layer=layer), name=f"rms_fwd_{layer}", grid=(t_rows // tm,),
        in_specs=in_specs,
        out_specs=[pl.BlockSpec((tm, d), lambda i: (i, 0)), pl.BlockSpec((d, tm), lambda i: (0, i))],
        out_shape=[jax.ShapeDtypeStruct((t_rows, d), BF16), jax.ShapeDtypeStruct((d, t_rows), BF16)],
        compiler_params=_params(("arbitrary",)),
    )(*args)


def _rms_bwd_body(x_ref, g_ref, dh_ref, res_ref, dx_ref, dg_ref, *, layer):
    x, dh = x_ref[...], dh_ref[...]
    r = lax.rsqrt(jnp.mean(x * x, axis=1, keepdims=True) + EPS)
    xr = x * r
    u = dh * g_ref[layer:layer + 1, :]
    dx_ref[...] = res_ref[...] + r * u - xr * (r * r) * jnp.mean(u * x, axis=1, keepdims=True)

    @pl.when(pl.program_id(0) == 0)
    def _():
        dg_ref[...] = jnp.zeros_like(dg_ref)

    dg_ref[0:1, :] += jnp.sum(dh * xr, axis=0, keepdims=True)


def rms_bwd(x, gains8, layer, dh, d_res):
    t_rows, d = x.shape
    tm = min(ROW_TILE, t_rows)
    blk = pl.BlockSpec((tm, d), lambda i: (i, 0))
    gblk = pl.BlockSpec((8, d), lambda i: (0, 0))
    return pl.pallas_call(
        functools.partial(_rms_bwd_body, layer=layer), name=f"rms_bwd_{layer}", grid=(t_rows // tm,),
        in_specs=[blk, gblk, blk, blk],
        out_specs=[blk, gblk],
        out_shape=[jax.ShapeDtypeStruct((t_rows, d), F32), jax.ShapeDtypeStruct((8, d), F32)],
        compiler_params=_params(("arbitrary",)),
    )(x, gains8, dh, d_res)


def _piece_col(p):
    return jnp.where(p < 3 * B_GROUPS, (p % 3) * B_GROUPS + p // 3, 3 * B_GROUPS)


def _mm_nn_body(a_ref, w_ref, *rest, has_res, a_resident):
    o_ref = rest[-1]
    tm = o_ref.shape[0]
    a = a_ref[pl.ds(pl.multiple_of(pl.program_id(1) * tm, tm), tm), :] if a_resident else a_ref[...]
    acc = jnp.dot(a, w_ref[...], preferred_element_type=F32)
    if has_res:
        acc = acc + rest[0][...]
    o_ref[...] = acc


def mm_nn(a, w, residual=None, *, tn, name, a_resident=False):
    m, k = a.shape
    n = w.shape[1]
    tm = min(NT_ROW_TILE, m)
    a_spec = pl.BlockSpec((m, k), lambda j, i: (0, 0)) if a_resident else pl.BlockSpec((tm, k), lambda j, i: (i, 0))
    in_specs = [a_spec, pl.BlockSpec((k, tn), lambda j, i: (0, j))]
    args = [a, w]
    if residual is not None:
        in_specs.append(pl.BlockSpec((tm, tn), lambda j, i: (i, j)))
        args.append(residual)
    return pl.pallas_call(
        functools.partial(_mm_nn_body, has_res=residual is not None, a_resident=a_resident), name=name,
        grid=(n // tn, m // tm),
        in_specs=in_specs,
        out_specs=pl.BlockSpec((tm, tn), lambda j, i: (i, j)),
        out_shape=jax.ShapeDtypeStruct((m, n), F32),
        compiler_params=_params(("arbitrary", "arbitrary")),
    )(*args)


def mm_nn_pieces(a, w, *, name):
    m, k = a.shape
    tm = min(NT_ROW_TILE, m)
    return pl.pallas_call(
        functools.partial(_mm_nn_body, has_res=False, a_resident=True), name=name, grid=(B_PIECES, m // tm),
        in_specs=[pl.BlockSpec((m, k), lambda p, i: (0, 0)),
                  pl.BlockSpec((k, B_W), lambda p, i: (0, _piece_col(p)))],
        out_specs=pl.BlockSpec((None, tm, B_W), lambda p, i: (p, i, 0)),
        out_shape=jax.ShapeDtypeStruct((B_PIECES, m, B_W), F32),
        compiler_params=_params(("arbitrary", "arbitrary")),
    )(a, w)


NT_ROW_TILE = 1024


def _mm_nt_body(g_ref, w_ref, *rest, has_init):
    o_ref = rest[-1]
    j = pl.program_id(1)
    part = lax.dot_general(_bf(g_ref[...]), w_ref[...], (((1,), (1,)), ((), ())), preferred_element_type=F32)

    @pl.when(j == 0)
    def _():
        o_ref[...] = part + rest[0][...] if has_init else part

    @pl.when(j > 0)
    def _():
        o_ref[...] += part


def mm_nt(g, w, init=None, *, tn, col_off=0, name, after=None):
    m, n = g.shape
    k = w.shape[0]
    tm = min(NT_ROW_TILE, m)
    in_specs = [pl.BlockSpec((tm, tn), lambda i, j: (i, j)),
                pl.BlockSpec((k, tn), lambda i, j: (0, col_off + j))]
    args = [g, w]
    if init is not None:
        in_specs.append(pl.BlockSpec((tm, k), lambda i, j: (i, 0)))
        args.append(init)
    if after is not None:
        in_specs.append(HBM_SPEC)
        args.append(after)
    return pl.pallas_call(
        functools.partial(_mm_nt_body, has_init=init is not None), name=name, grid=(m // tm, n // tn),
        in_specs=in_specs,
        out_specs=pl.BlockSpec((tm, k), lambda i, j: (i, 0)),
        out_shape=jax.ShapeDtypeStruct((m, k), F32),
        compiler_params=_params(("arbitrary", "arbitrary")),
    )(*args)


def mm_nt_multi(gs, w, *, tn, name, after=None):
    m = gs[0].shape[0]
    k = w.shape[0]
    tm = min(NT_ROW_TILE, m)
    tiles = [g.shape[1] // tn for g in gs]
    starts = [sum(tiles[:i]) for i in range(len(gs))]

    def body(*refs):
        g_refs, w_ref, o_ref = refs[:len(gs)], refs[len(gs)], refs[-1]
        j = pl.program_id(1)
        for g_ref, lo, cnt in zip(g_refs, starts, tiles):
            @pl.when((j >= lo) & (j < lo + cnt))
            def _(g_ref=g_ref):
                part = lax.dot_general(_bf(g_ref[...]), w_ref[...], (((1,), (1,)), ((), ())),
                                       preferred_element_type=F32)

                @pl.when(j == 0)
                def _():
                    o_ref[...] = part

                @pl.when(j > 0)
                def _():
                    o_ref[...] += part

    def g_spec(lo, cnt):
        return pl.BlockSpec((tm, tn), lambda i, j: (i, jnp.clip(j - lo, 0, cnt - 1)))

    in_specs = [g_spec(lo, cnt) for lo, cnt in zip(starts, tiles)] + [pl.BlockSpec((k, tn), lambda i, j: (0, j))]
    args = list(gs) + [w]
    if after is not None:
        in_specs.append(HBM_SPEC)
        args.append(after)
    return pl.pallas_call(
        body, name=name, grid=(m // tm, sum(tiles)),
        in_specs=in_specs,
        out_specs=pl.BlockSpec((tm, k), lambda i, j: (i, 0)),
        out_shape=jax.ShapeDtypeStruct((m, k), F32),
        compiler_params=_params(("arbitrary", "arbitrary")),
    )(*args)


def mm_nt_pieces(g9, gz, w, *, name):
    n_p, m, _ = g9.shape
    k = w.shape[0]
    tm = min(2 * NT_ROW_TILE, m)

    def body(g_ref, z_ref, w_ref, o_ref):
        p = pl.program_id(1)

        def accumulate(src):
            part = lax.dot_general(_bf(src[...]), w_ref[...], (((1,), (1,)), ((), ())), preferred_element_type=F32)

            @pl.when(p == 0)
            def _():
                o_ref[...] = part

            @pl.when(p > 0)
            def _():
                o_ref[...] += part

        @pl.when(p < n_p)
        def _():
            accumulate(g_ref)

        @pl.when(p == n_p)
        def _():
            accumulate(z_ref)

    return pl.pallas_call(
        body, name=name, grid=(m // tm, n_p + 1),
        in_specs=[pl.BlockSpec((None, tm, B_W), lambda i, p: (jnp.minimum(p, n_p - 1), i, 0)),
                  pl.BlockSpec((tm, B_W), lambda i, p: (i, 0)),
                  pl.BlockSpec((k, B_W), lambda i, p: (0, _piece_col(p)))],
        out_specs=pl.BlockSpec((tm, k), lambda i, p: (i, 0)),
        out_shape=jax.ShapeDtypeStruct((m, k), F32),
        compiler_params=_params(("arbitrary", "arbitrary")),
    )(g9, gz, w)


def _mm_tn_body(a_ref, g_ref, o_ref, *, a_is_transposed):
    lhs_dim = 1 if a_is_transposed else 0
    o_ref[...] = lax.dot_general(a_ref[...], _bf(g_ref[...]), (((lhs_dim,), (0,)), ((), ())),
                                 preferred_element_type=F32).astype(o_ref.dtype)


def mm_tn(a, g, *, tn, out_dtype, name, a_is_transposed=False):
    k = a.shape[0] if a_is_transposed else a.shape[1]
    m, n = g.shape
    return pl.pallas_call(
        functools.partial(_mm_tn_body, a_is_transposed=a_is_transposed), name=name, grid=(n // tn,),
        in_specs=[pl.BlockSpec(a.shape, lambda j: (0, 0)), pl.BlockSpec((m, tn), lambda j: (0, j))],
        out_specs=pl.BlockSpec((k, tn), lambda j: (0, j)),
        out_shape=jax.ShapeDtypeStruct((k, n), out_dtype),
        compiler_params=_params(("arbitrary",)),
    )(a, g)


B_UNIT = 256
B_IN_COLS = B_PIECES * B_W
B_SHARD_UNITS = B_IN_COLS // N_DEV // B_UNIT


def mm_tn_b_in(at, g9, gz, *, out_dtype, name):
    k, m = at.shape
    per_piece = B_W // B_UNIT
    body_one = functools.partial(_mm_tn_body, a_is_transposed=True)
    n_units = B_IN_COLS // B_UNIT

    def g_map(u):
        nat = jnp.minimum(u // per_piece, 3 * B_GROUPS - 1)
        piece = (nat % B_GROUPS) * 3 + nat // B_GROUPS
        return (piece, 0, u % per_piece)

    def body(a_ref, g_ref, z_ref, o_ref):
        u = pl.program_id(0)

        @pl.when(u < 3 * B_GROUPS * per_piece)
        def _():
            body_one(a_ref, g_ref, o_ref)

        @pl.when(u >= 3 * B_GROUPS * per_piece)
        def _():
            body_one(a_ref, z_ref, o_ref)

    return pl.pallas_call(
        body, name=name, grid=(n_units,),
        in_specs=[pl.BlockSpec((k, m), lambda u: (0, 0)),
                  pl.BlockSpec((None, m, B_UNIT), g_map),
                  pl.BlockSpec((m, B_UNIT), lambda u: (0, jnp.where(u < 3 * B_GROUPS * per_piece, 0, u % per_piece)))],
        out_specs=pl.BlockSpec((None, k, B_UNIT), lambda u: (u // B_SHARD_UNITS, 0, u % B_SHARD_UNITS)),
        out_shape=jax.ShapeDtypeStruct((N_DEV, k, B_IN_COLS // N_DEV), out_dtype),
        compiler_params=_params(("arbitrary",)),
    )(at, g9, gz)


def _loss_body(y_ref, t_ref, dy_ref, loss_ref, acc):
    i = pl.program_id(0)
    d = y_ref.shape[1]
    err = y_ref[...] - t_ref[...]
    dy_ref[...] = err * (1.0 / d)

    @pl.when(i == 0)
    def _():
        acc[...] = jnp.zeros_like(acc)

    acc[...] += jnp.sum(err * err, axis=0, keepdims=True)

    @pl.when(i == pl.num_programs(0) - 1)
    def _():
        total = jnp.sum(acc[...], axis=1, keepdims=True) * (0.5 / d)
        loss_ref[...] = jnp.broadcast_to(total, loss_ref.shape)


def loss_head(y, target):
    t_rows, d = y.shape
    tm = min(ROW_TILE, t_rows)
    blk = pl.BlockSpec((tm, d), lambda i: (i, 0))
    return pl.pallas_call(
        _loss_body, name="loss_head", grid=(t_rows // tm,),
        in_specs=[blk, blk],
        out_specs=[blk, pl.BlockSpec((8, 128), lambda i: (0, 0))],
        out_shape=[jax.ShapeDtypeStruct((t_rows, d), F32), jax.ShapeDtypeStruct((8, 128), F32)],
        scratch_shapes=[pltpu.VMEM((1, d), F32)],
        compiler_params=_params(("arbitrary",)),
    )(y, target)


def _adamw_body(p_ref, w_ref, m_ref, v_ref, g_ref, d_ref, nm_ref, nv_ref):
    g = p_ref[0].astype(F32)
    for s in range(1, N_DEV):
        g = g + p_ref[s].astype(F32)
    w = w_ref[...]
    m = ADAM_B1 * m_ref[...] + (1.0 - ADAM_B1) * g
    v = ADAM_B2 * v_ref[...] + (1.0 - ADAM_B2) * (g * g)
    m_hat = m / (1.0 - ADAM_B1 ** ADAM_STEP)
    v_hat = v / (1.0 - ADAM_B2 ** ADAM_STEP)
    g_ref[...] = g
    d_ref[...] = -ADAM_LR * (m_hat / (jnp.sqrt(v_hat) + ADAM_EPS) + ADAM_WD * w)
    nm_ref[...] = m
    nv_ref[...] = v


def adamw(parts, w, m, v, *, name):
    _, r, c = w.shape
    tr = r if r <= 256 else 256
    blk = pl.BlockSpec((None, tr, c), lambda i: (0, i, 0))
    out = jax.ShapeDtypeStruct((1, r, c), F32)
    return pl.pallas_call(
        _adamw_body, name=name, grid=(r // tr,),
        in_specs=[pl.BlockSpec((N_DEV, tr, c), lambda i: (0, i, 0)), blk, blk, blk],
        out_specs=[blk, blk, blk, blk],
        out_shape=[out, out, out, out],
        compiler_params=_params(("arbitrary",)),
    )(parts, w, m, v)


MESH_ID = pl.DeviceIdType.MESH
HBM_SPEC = pl.BlockSpec(memory_space=pl.ANY)


def _my_place():
    return lax.axis_index("x"), lax.axis_index("y"), lax.axis_index("c")


def _flat(x, y, c):
    return 4 * x + 2 * y + c


def _all_gather_body(*refs, n):
    ins, outs = refs[:n], refs[n:2 * n]
    send_sems, recv_sems, local_sems = refs[2 * n:]
    x, y, c = _my_place()
    me, sibling = (x, y, c), (x, y, 1 - c)
    chips = [(1 - x, y), (x, 1 - y), (1 - x, 1 - y)]
    pending = []
    for a in range(n):
        src, out = ins[a], outs[a]

        def copy(k, block, to, from_input=False, a=a, src=src, out=out):
            slot = out.at[_flat(*block)]
            return pltpu.make_async_remote_copy(
                src_ref=src if from_input else slot, dst_ref=slot,
                send_sem=send_sems.at[7 * a + k], recv_sem=recv_sems.at[7 * a + k],
                device_id=to, device_id_type=MESH_ID)

        mine = pltpu.make_async_copy(src, out.at[_flat(*me)], local_sems.at[a])
        mine.start()
        first = [copy(0, me, sibling, True)] + [copy(1 + j, me, (*chip, c), True) for j, chip in enumerate(chips)]
        for cp in first:
            cp.start()
        pending.append((copy, mine, first))
    for copy, mine, first in pending:
        passed = [copy(4 + j, (*chip, c), sibling) for j, chip in enumerate(chips)]
        for j, chip in enumerate(chips):
            copy(1 + j, (*chip, c), me).wait_recv()
            passed[j].start()
        copy(0, sibling, me).wait_recv()
        for j, chip in enumerate(chips):
            copy(4 + j, (*chip, 1 - c), me).wait_recv()
        for cp in first + passed:
            cp.wait_send()
        mine.wait()


def all_gather(shards, *, name):
    n = len(shards)
    return pl.pallas_call(
        functools.partial(_all_gather_body, n=n), name=name,
        in_specs=[HBM_SPEC] * n, out_specs=[HBM_SPEC] * n,
        out_shape=[jax.ShapeDtypeStruct((N_DEV,) + s.shape, s.dtype) for s in shards],
        scratch_shapes=[pltpu.SemaphoreType.DMA((7 * n,)), pltpu.SemaphoreType.DMA((7 * n,)),
                        pltpu.SemaphoreType.DMA((n,))],
    )(*shards)


PEER_FLIPS = [(0, 0, 1), (1, 0, 0), (0, 1, 0), (1, 1, 0), (1, 0, 1), (0, 1, 1), (1, 1, 1)]


def _all_to_all_body(*refs, n):
    ins, outs = refs[:n], refs[n:2 * n]
    send_sems, recv_sems, local_sems = refs[2 * n:]
    x, y, c = _my_place()
    me = _flat(x, y, c)
    waits = []
    for a in range(n):
        src, out = ins[a], outs[a]
        mine = pltpu.make_async_copy(src.at[me], out.at[me], local_sems.at[a])
        mine.start()
        waits.append(mine)
        for k, (fx, fy, fc) in enumerate(PEER_FLIPS):
            peer = (1 - x if fx else x, 1 - y if fy else y, 1 - c if fc else c)
            theirs = _flat(*peer)
            sems = dict(send_sem=send_sems.at[7 * a + k], recv_sem=recv_sems.at[7 * a + k],
                        device_id=peer, device_id_type=MESH_ID)
            send = pltpu.make_async_remote_copy(src_ref=src.at[theirs], dst_ref=out.at[me], **sems)
            send.start()
            recv = pltpu.make_async_remote_copy(src_ref=src.at[theirs], dst_ref=out.at[theirs], **sems)
            waits.append((send, recv))
    for w in waits:
        if isinstance(w, tuple):
            w[0].wait_send()
            w[1].wait_recv()
        else:
            w.wait()


def all_to_all(parts, *, name):
    n = len(parts)
    return pl.pallas_call(
        functools.partial(_all_to_all_body, n=n), name=name,
        in_specs=[HBM_SPEC] * n, out_specs=[HBM_SPEC] * n,
        out_shape=[jax.ShapeDtypeStruct(p.shape, p.dtype) for p in parts],
        scratch_shapes=[pltpu.SemaphoreType.DMA((7 * n,)), pltpu.SemaphoreType.DMA((7 * n,)),
                        pltpu.SemaphoreType.DMA((n,))],
    )(*parts)


HBM_ONLY = pl.BlockSpec(memory_space=pltpu.HBM)
SEM_SPEC = pl.BlockSpec(memory_space=pltpu.SEMAPHORE)
DATAFLOW_EFFECT = pltpu.SideEffectType.DATAFLOW_SIDE_EFFECTING


def _split_copies(srcs, lands, send_sems, recv_sems, n, scatter):
    x, y, c = _my_place()
    me = _flat(x, y, c)
    pairs = []
    for a in range(n):
        for k, (fx, fy, fc) in enumerate(PEER_FLIPS):
            peer = (1 - x if fx else x, 1 - y if fy else y, 1 - c if fc else c)
            theirs = _flat(*peer)
            src = srcs[a].at[theirs] if scatter else srcs[a]
            sems = dict(send_sem=send_sems.at[7 * a + k], recv_sem=recv_sems.at[7 * a + k],
                        device_id=peer, device_id_type=MESH_ID)
            pairs.append((pltpu.make_async_remote_copy(src_ref=src, dst_ref=lands[a].at[me], **sems),
                          pltpu.make_async_remote_copy(src_ref=src, dst_ref=lands[a].at[theirs], **sems)))
    return pairs


def _exchange_start_body(*refs, n, scatter):
    srcs, lands = refs[:n], refs[n:2 * n]
    send_sems, recv_sems = refs[2 * n], refs[2 * n + 1]
    token = refs[-1]
    for send, _ in _split_copies(srcs, lands, send_sems, recv_sems, n, scatter):
        send.start()
    token[...] = jnp.zeros_like(token)


def exchange_start(srcs, lands, *, scatter, name):
    n = len(srcs)
    args = [pltpu.with_memory_space_constraint(t, pltpu.HBM) for t in list(srcs) + list(lands)]
    outs = pl.pallas_call(
        functools.partial(_exchange_start_body, n=n, scatter=scatter), name=name,
        out_shape=(pltpu.SemaphoreType.DMA((7 * n,)), pltpu.SemaphoreType.DMA((7 * n,)),
                   *[pltpu.HBM(t.shape, t.dtype) for t in args],
                   jax.ShapeDtypeStruct((8, 128), F32)),
        in_specs=[HBM_ONLY] * (2 * n),
        out_specs=(SEM_SPEC, SEM_SPEC, *[HBM_ONLY] * (2 * n), pl.BlockSpec(memory_space=pltpu.VMEM)),
        input_output_aliases={i: 2 + i for i in range(2 * n)},
        compiler_params=pltpu.CompilerParams(has_side_effects=DATAFLOW_EFFECT),
    )(*args)
    return outs[0], outs[1], outs[2:2 + n], outs[2 + n:2 + 2 * n], outs[-1]


def _exchange_wait_body(*refs, n, scatter):
    srcs, lands = refs[:n], refs[n:2 * n]
    send_sems, recv_sems = refs[2 * n], refs[2 * n + 1]
    for send, recv in _split_copies(srcs, lands, send_sems, recv_sems, n, scatter):
        send.wait_send()
        recv.wait_recv()


def exchange_wait(send_sems, recv_sems, srcs, lands, after, *, scatter, name):
    n = len(srcs)
    outs = pl.pallas_call(
        functools.partial(_exchange_wait_body, n=n, scatter=scatter), name=name,
        out_shape=tuple(pltpu.HBM(t.shape, t.dtype) for t in list(srcs) + list(lands)),
        in_specs=[HBM_ONLY] * (2 * n) + [SEM_SPEC, SEM_SPEC, HBM_SPEC],
        out_specs=tuple([HBM_ONLY] * (2 * n)),
        input_output_aliases={i: i for i in range(2 * n)},
        compiler_params=pltpu.CompilerParams(has_side_effects=DATAFLOW_EFFECT),
    )(*srcs, *lands, send_sems, recv_sems, after)
    return outs[n:]


def _own_slot_only(shape_dtype, own, slot):
    land = lax.empty(shape_dtype.shape, shape_dtype.dtype)
    return lax.dynamic_update_slice(land, own[None], (slot,) + (0,) * own.ndim)


def _pad_rows(a, rows=8):
    return jnp.pad(a, ((0, rows - a.shape[0]), (0, 0)))


def _gate_rows(a_log, dt_bias):
    z = jnp.zeros((8, 128), F32)
    return z.at[0, A_HEADS:2 * A_HEADS].set(a_log[0]).at[1, A_HEADS:2 * A_HEADS].set(dt_bias[0])


def _pack_small(norm_g, a_log, a_dt_bias, a_norm_g, b_q_norm_g, b_k_norm_g):
    return jnp.concatenate([
        norm_g[0].reshape(8, 128), norm_g[1].reshape(8, 128),
        _gate_rows(a_log, a_dt_bias),
        _pad_rows(a_norm_g[0].reshape(2, 128)),
        _pad_rows(jnp.concatenate([b_q_norm_g[0], b_k_norm_g[0]], axis=0)),
    ], axis=0)


def _unpack_small(p):
    return (p[0:16].reshape(2, D_MODEL), p[16:17, A_HEADS:2 * A_HEADS], p[17:18, A_HEADS:2 * A_HEADS],
            p[24:26].reshape(1, A_DV), p[32:35][None], p[35:38][None])


def kernel(x, positions, norm_g, a_w_in, a_conv_w, a_log, a_dt_bias, a_norm_g, a_w_out, b_w_in, b_q_norm_g, b_k_norm_g, b_w_out, loss_target, m_norm_g, m_a_w_in, m_a_conv_w, m_a_log, m_a_dt_bias, m_a_norm_g, m_a_w_out, m_b_w_in, m_b_q_norm_g, m_b_k_norm_g, m_b_w_out, v_norm_g, v_a_w_in, v_a_conv_w, v_a_log, v_a_dt_bias, v_a_norm_g, v_a_w_out, v_b_w_in, v_b_q_norm_g, v_b_k_norm_g, v_b_w_out):
    n_seq, s_len, d = x.shape
    t_rows = n_seq * s_len
    n_chunks = s_len // CHUNK
    x0 = x.reshape(t_rows, d)
    target = loss_target.reshape(t_rows, d)
    my_slot = _flat(*_my_place())

    g_a_in, g_conv = all_gather([a_w_in[0].astype(BF16), _pad_rows(a_conv_w[0])], name="gather_weights_first")
    later = [a_w_out[0].astype(BF16), b_w_in[0].astype(BF16), b_w_out[0].astype(BF16)]
    lands = [_own_slot_only(jax.ShapeDtypeStruct((N_DEV,) + t.shape, t.dtype), t, my_slot) for t in later]
    w_send, w_recv, later, lands, w_token = exchange_start(later, lands, scatter=False, name="gather_weights_start")
    w_a_in = jnp.pad(g_a_in.transpose(1, 0, 2).reshape(d, A_IN), ((0, 0), (0, A_IN_PAD - A_IN)))
    conv_w8 = g_conv.transpose(1, 0, 2).reshape(8, 2 * A_QK + A_VW)

    gains_model = _pad_rows(norm_g)
    gate_prm = _gate_rows(a_log, a_dt_bias)
    gain_a_out = _pad_rows(a_norm_g)
    gains_qk = _pad_rows(jnp.concatenate([b_q_norm_g[0], b_k_norm_g[0]], axis=0))
    cos_t, sin_t = rope_tables(positions.reshape(t_rows))

    h0, h0_t = rms_fwd(x0, gains_model, 0, after=w_token)
    proj_a = mm_nn(h0, w_a_in, tn=896, name="proj_a", a_resident=True)
    gates_col, gates_row = a_gates_fwd(proj_a, gate_prm, n_seq)
    gates_row = gates_row.reshape(n_seq, 2 * A_HEADS, s_len // SUPER, 1, SUPER)
    o_a, tinv, states, og_a, q_a, k_a, v_a = gdn_fwd(proj_a, conv_w8, gates_col, gates_row, gain_a_out, n_seq)
    g_a_out, g_b_in, g_b_out = exchange_wait(w_send, w_recv, later, lands, og_a, scatter=False,
                                             name="gather_weights_wait")
    w_a_out = g_a_out.reshape(A_VW, d)
    w_b_in = g_b_in.transpose(1, 0, 2).reshape(d, B_IN_COLS)
    w_b_out = g_b_out.reshape(B_W, d)
    x1 = mm_nn(og_a, w_a_out, x0, tn=1024, name="out_a")

    h1, h1_t = rms_fwd(x1, gains_model, 1)
    proj_b = mm_nn_pieces(h1, w_b_in, name="proj_b")
    og_b, o_b, lse = attn_fwd(proj_b, cos_t, sin_t, gains_qk, n_seq)
    y = mm_nn(og_b, w_b_out, x1, tn=1024, name="out_b")

    dy, loss_blk = loss_head(y, target)
    loss = lax.psum(loss_blk[0, 0], ("x", "y", "c"))

    d_og_b = mm_nt(dy, w_b_out, tn=1024, name="d_og_b")
    dw_b_out = mm_tn(og_b, dy, tn=256, out_dtype=BF16, name="dw_b_out")
    dqkv_b, dz_b, d_gains_qk = attn_bwd(proj_b, cos_t, sin_t, gains_qk, d_og_b, o_b, lse, n_seq)
    dh1 = mm_nt_pieces(dqkv_b, dz_b, w_b_in, name="dh1")
    dw_b_in = mm_tn_b_in(h1_t, dqkv_b, dz_b, out_dtype=BF16, name="dw_b_in")
    dx1, d_gain1 = rms_bwd(x1, gains_model, 1, dh1, dy)

    dw_a_out = mm_tn(og_a, dx1, tn=256, out_dtype=BF16, name="dw_a_out")
    early = [dw_b_in, dw_b_out.reshape(N_DEV, B_W // N_DEV, d), dw_a_out.reshape(N_DEV, A_VW // N_DEV, d)]
    lands = [_own_slot_only(t, lax.dynamic_index_in_dim(t, my_slot, 0, keepdims=False), my_slot) for t in early]
    g_send, g_recv, early, lands, g_token = exchange_start(early, lands, scatter=True, name="scatter_grads_start")

    d_og_a = mm_nt(dx1, w_a_out, tn=1024, name="d_og_a", after=g_token)
    d_xq, d_xk, d_xv, dgates, dz_a, d_gain_a_out, d_cq, d_ck, d_cv = gdn_bwd(
        q_a, k_a, v_a, gates_col, gates_row, tinv, states, d_og_a, o_a, proj_a, conv_w8, gain_a_out, n_seq)
    d_conv = jnp.concatenate([d_cq.sum(axis=0), d_ck.sum(axis=0), d_cv.sum(axis=0)], axis=1)
    d_gate_logits, d_gate_prm = a_gates_bwd(proj_a, gate_prm, dgates, n_seq)
    dw_a_in = jnp.concatenate([
        mm_tn(h0_t, piece, tn=min(256, piece.shape[1]), out_dtype=BF16, name=f"dw_a_in_{nm}", a_is_transposed=True)
        for nm, piece in (("q", d_xq), ("k", d_xk), ("v", d_xv), ("z", dz_a), ("gates", d_gate_logits))
    ], axis=1)[:, :A_IN]
    shard_a_in = A_IN // N_DEV
    last = [dw_a_in.reshape(d, N_DEV, shard_a_in).transpose(1, 0, 2)]
    last_lands = [_own_slot_only(t, lax.dynamic_index_in_dim(t, my_slot, 0, keepdims=False), my_slot) for t in last]
    l_send, l_recv, last, last_lands, l_token = exchange_start(last, last_lands, scatter=True,
                                                               name="scatter_last_start")
    dh0 = mm_nt_multi([d_xq, d_xk, d_xv, dz_a], w_a_in, tn=1024, name="dh0_qkvz", after=l_token)
    dh0 = mm_nt(d_gate_logits, w_a_in, dh0, tn=128, col_off=A_GATE_COL, name="dh0_gates")
    dx0, d_gain0 = rms_bwd(x0, gains_model, 0, dh0, dx1)

    small = jnp.concatenate([
        d_gain0[0].reshape(8, 128), d_gain1[0].reshape(8, 128), d_gate_prm,
        _pad_rows(d_gain_a_out[0].reshape(2, 128)), d_gains_qk], axis=0)
    small_srcs = [small, d_conv]
    small_lands = [_own_slot_only(jax.ShapeDtypeStruct((N_DEV,) + t.shape, t.dtype), t, my_slot)
                   for t in small_srcs]
    s_send, s_recv, small_srcs, small_lands, s_token = exchange_start(small_srcs, small_lands, scatter=False,
                                                                      name="gather_small_start")

    r_b_in, r_b_out, r_a_out = exchange_wait(g_send, g_recv, early, lands, s_token, scatter=True,
                                             name="scatter_grads_wait")
    (r_a_in,) = exchange_wait(l_send, l_recv, last, last_lands, s_token, scatter=True, name="scatter_last_wait")

    upd = {}
    upd["a_w_in"] = adamw(r_a_in, a_w_in, m_a_w_in, v_a_w_in, name="adamw_a_w_in")
    upd["a_w_out"] = adamw(r_a_out, a_w_out, m_a_w_out, v_a_w_out, name="adamw_a_w_out")
    upd["b_w_in"] = adamw(r_b_in, b_w_in, m_b_w_in, v_b_w_in, name="adamw_b_w_in")
    upd["b_w_out"] = adamw(r_b_out, b_w_out, m_b_w_out, v_b_w_out, name="adamw_b_w_out")
    r_small, r_conv = exchange_wait(s_send, s_recv, small_srcs, small_lands, upd["b_w_in"][0], scatter=False,
                                    name="gather_small_wait")
    conv_cols = a_conv_w.shape[2]
    r_conv = lax.dynamic_slice(r_conv, (0, 0, my_slot * conv_cols), (N_DEV, 8, conv_cols))
    upd["a_conv_w"] = [t[:, :A_CONV] for t in adamw(
        r_conv, _pad_rows(a_conv_w[0])[None], _pad_rows(m_a_conv_w[0])[None], _pad_rows(v_a_conv_w[0])[None],
        name="adamw_a_conv_w")]
    small_upd = adamw(
        r_small,
        _pack_small(norm_g, a_log, a_dt_bias, a_norm_g, b_q_norm_g, b_k_norm_g)[None],
        _pack_small(m_norm_g, m_a_log, m_a_dt_bias, m_a_norm_g, m_b_q_norm_g, m_b_k_norm_g)[None],
        _pack_small(v_norm_g, v_a_log, v_a_dt_bias, v_a_norm_g, v_b_q_norm_g, v_b_k_norm_g)[None],
        name="adamw_small")
    small_names = ("norm_g", "a_log", "a_dt_bias", "a_norm_g", "b_q_norm_g", "b_k_norm_g")
    unpacked = [_unpack_small(t[0]) for t in small_upd]
    for i, nm in enumerate(small_names):
        upd[nm] = [u[i] for u in unpacked]

    order = ("norm_g", "a_w_in", "a_conv_w", "a_log", "a_dt_bias", "a_norm_g", "a_w_out",
             "b_w_in", "b_q_norm_g", "b_k_norm_g", "b_w_out")
    outs = [loss, dx0.reshape(n_seq, s_len, d)]
    for kind in range(4):
        for nm in order:
            outs.append(upd[nm][kind])
    return tuple(outs)
```

```python
import functools
import math

import jax
import jax.numpy as jnp
from jax import lax
from jax.experimental import pallas as pl
from jax.experimental.pallas import tpu as pltpu

F32 = jnp.float32
BF16 = jnp.bfloat16

D_MODEL = 1024
EPS = 1e-6
N_DEV = 8

A_HEADS = 8
A_DK = 128
A_DV = 256
A_QK = A_HEADS * A_DK
A_VW = A_HEADS * A_DV
A_CONV = 4
CHUNK = 64
A_IN = 2 * A_QK + 2 * A_VW + 2 * A_HEADS
A_IN_PAD = 2 * A_QK + 2 * A_VW + 128
A_GATE_COL = (2 * A_QK + 2 * A_VW) // 128

B_DILATIONS = (1, 4, 16)
B_GROUPS = 3
B_HEADS = 8
B_DH = 128
B_W = B_HEADS * B_DH
B_BLOCK = 128
B_PIECES = 3 * B_GROUPS + 1
ROPE_THETA = 500000.0
ROPE_DIMS = B_DH // 4
ROPE_HALF = ROPE_DIMS // 2

ADAM_LR = 0.001
ADAM_B1 = 0.9
ADAM_B2 = 0.999
ADAM_EPS = 1e-08
ADAM_WD = 0.01
ADAM_STEP = 10

VMEM_LIMIT = 60 * 1024 * 1024


def _params(sem):
    return pltpu.CompilerParams(dimension_semantics=sem, vmem_limit_bytes=VMEM_LIMIT)


def _bf(x):
    return x.astype(BF16)


def _mm(a, b):
    return jnp.dot(_bf(a), _bf(b), preferred_element_type=F32)


def _mm_nt(a, b):
    return lax.dot_general(_bf(a), _bf(b), (((1,), (1,)), ((), ())), preferred_element_type=F32)


def _mm_tn(a, b):
    return lax.dot_general(_bf(a), _bf(b), (((0,), (0,)), ((), ())), preferred_element_type=F32)


def _split(x):
    hi = _bf(x)
    return hi, _bf(x - hi.astype(F32))


def _mm3(a, b):
    ah, al = _split(a)
    bh, bl = _split(b)
    d = functools.partial(jnp.dot, preferred_element_type=F32)
    return d(ah, bh) + (d(ah, bl) + d(al, bh))


def _colsum_as_col(z):
    zh, zl = _split(z)
    ones = jnp.ones((z.shape[0], 128), BF16)
    tn = functools.partial(lax.dot_general, dimension_numbers=(((0,), (0,)), ((), ())),
                           preferred_element_type=F32)
    return (tn(zh, ones) + tn(zl, ones))[:, 0:1]


def _sigmoid(x):
    return 0.5 * jnp.tanh(0.5 * x) + 0.5


INV_BASE = 8
INV_NEWTON = 2
GDN_GROUP = 4
SUPER = GDN_GROUP * CHUNK
GDN_WIDTH = 2

A_K_COL = A_QK // A_DK
A_V_COL = 2 * A_QK // A_DV


def _inverse_steps(m, row, col):
    eye = (row == col).astype(F32)
    d = jnp.where(row // INV_BASE == col // INV_BASE, m, 0.0)
    x = eye - d
    p = _mm(d, d)
    yield
    steps = int(math.log2(INV_BASE)) - 1
    for i in range(steps):
        x = x + _mm(x, p)
        if i + 1 < steps:
            p = _mm(p, p)
        yield
    size = INV_BASE
    while size < CHUNK:
        c = jnp.where((row // (2 * size) == col // (2 * size)) & (row // size != col // size), m, 0.0)
        xc = _mm(x, c)
        yield
        x = x - _mm(xc, x)
        yield
        size *= 2
    for _ in range(INV_NEWTON):
        r = eye - x - _mm3(m, x)
        yield
        x = x + _mm(x, r)
        yield
    return x


def _drain(gen):
    while True:
        try:
            next(gen)
        except StopIteration as stop:
            return stop.value


def _interleave(*gens):
    live = list(gens)
    while live:
        for g in list(live):
            try:
                next(g)
            except StopIteration:
                live.remove(g)


def _diag_blocks_tall(x):
    return jnp.concatenate([x[i * CHUNK:(i + 1) * CHUNK, i * CHUNK:(i + 1) * CHUNK] for i in range(GDN_GROUP)], axis=0)


def _tall_to_block_diag(t, same):
    return jnp.where(same, jnp.concatenate([t] * GDN_GROUP, axis=1), 0.0)


def _block_sum(same, x):
    xh, xl = _split(jnp.broadcast_to(x, (SUPER, 128)))
    ones = same.astype(BF16)
    d = functools.partial(jnp.dot, preferred_element_type=F32)
    return (d(ones, xh) + d(ones, xl))[:, 0:1]


def _aligned_rows(index, size):
    start = index * size
    return pl.ds(start if isinstance(start, int) else pl.multiple_of(start, size), size)


def _super_rows(i):
    return _aligned_rows(i, SUPER)


def _chunk_rows(n):
    return _aligned_rows(n, CHUNK)


def _gdn_super_steps(q, k, v, gcb, gr, head, tinv_tall=None):
    lane = lax.broadcasted_iota(jnp.int32, (SUPER, 128), 1)
    row = lax.broadcasted_iota(jnp.int32, (SUPER, SUPER), 0)
    col = lax.broadcasted_iota(jnp.int32, (SUPER, SUPER), 1)
    same = row // CHUNK == col // CHUNK
    beta = jnp.sum(jnp.where(lane == head, gcb, 0.0), axis=1, keepdims=True)
    gc = jnp.sum(jnp.where(lane == A_HEADS + head, gcb, 0.0), axis=1, keepdims=True)
    g_last = jnp.sum(jnp.where(col == (row // CHUNK) * CHUNK + (CHUNK - 1), gr, 0.0), axis=1, keepdims=True)
    gamma = jnp.exp(gc)
    decay = jnp.where(same & (row >= col), jnp.exp(jnp.minimum(gc - gr, 0.0)), 0.0)
    kb = k * beta
    m = jnp.where(same & (row > col), _mm_nt(kb, k) * decay, 0.0)
    p = jnp.where(same & (row >= col), _mm_nt(q, k) * decay, 0.0)
    yield
    if tinv_tall is None:
        tinv = yield from _inverse_steps(m, row, col)
    else:
        tinv = _tall_to_block_diag(tinv_tall, same)
    u = _mm(tinv, v * beta)
    w = _mm(tinv, kb * gamma)
    yield
    e_tail = jnp.exp(g_last - gc)
    return dict(beta=beta, gc=gc, g_last=g_last, gamma=gamma, decay=decay, kb=kb, m=m,
                tinv=tinv, u=u, w=w, p=p, e_tail=e_tail, row=row, col=col, lane=lane, same=same)


def _gdn_super_common(q, k, v, gcb, gr, head, tinv_tall=None):
    return _drain(_gdn_super_steps(q, k, v, gcb, gr, head, tinv_tall))


def _store_scan_operands(rows, q, k, t, u_scr, w_scr, p_scr, qg_scr, ke_scr, gl_scr):
    u_scr[rows, :] = t["u"]
    w_scr[rows, :] = _bf(t["w"])
    p_scr[rows, :] = _bf(_diag_blocks_tall(t["p"]))
    qg_scr[rows, :] = _bf(q * t["gamma"])
    ke_scr[rows, :] = _bf(k * t["e_tail"])
    gl_scr[rows, :] = jnp.broadcast_to(jnp.exp(t["g_last"]), (SUPER, 128))


def _gdn_fwd_body(q_ref, k_ref, v_ref, gc_ref, gr_ref, wq_ref, wk_ref, wv_ref, z_ref, gn_ref,
                  o_ref, tinv_ref, st_ref, og_ref, qo_ref, ko_ref, vo_ref, s_scr, *sets):
    head = pl.program_id(1)
    n_super = q_ref.shape[0] // SUPER
    all_sets = [sets[6 * i:6 * i + 6] for i in range(2 * GDN_WIDTH)]
    whole = pl.ds(0, SUPER)

    def conv_silu(x_ref, w_ref, i):
        rows = _super_rows(i)
        x, w = x_ref[rows, :], w_ref[...]
        halo = jnp.zeros((8, x.shape[1]), F32) if i == 0 else x_ref[pl.ds(i * SUPER - 8, 8), :]
        ext = jnp.concatenate([halo, x], axis=0)
        c = x * w[A_CONV - 1:A_CONV, :]
        for j in range(1, A_CONV):
            c = c + pltpu.roll(ext, j, 0)[8:, :] * w[A_CONV - 1 - j:A_CONV - j, :]
        return c * _sigmoid(c)

    def unit(a):
        return a * lax.rsqrt(jnp.sum(a * a, axis=1, keepdims=True) + EPS)

    def prepare_steps(i, dst):
        rows = _super_rows(i)
        q = unit(conv_silu(q_ref, wq_ref, i)) * A_DK ** -0.5
        k = unit(conv_silu(k_ref, wk_ref, i))
        v = conv_silu(v_ref, wv_ref, i)
        qo_ref[rows, :], ko_ref[rows, :], vo_ref[rows, :] = q, k, v
        t = yield from _gdn_super_steps(q, k, v, gc_ref[rows, :], gr_ref[i], head)
        tinv_ref[rows, :] = _diag_blocks_tall(t["tinv"])
        _store_scan_operands(whole, q, k, t, *dst)

    def scan_steps(i, src):
        u_scr, w_scr, p_scr, qg_scr, ke_scr, gl_scr = src
        for j in range(GDN_GROUP):
            n = i * GDN_GROUP + j
            local = pl.ds(j * CHUNK, CHUNK)
            s = s_scr[...]
            sb = _bf(s)
            st_ref[n] = sb
            ws = jnp.dot(w_scr[local, :], sb, preferred_element_type=F32)
            yield
            vb = _bf(u_scr[local, :] - ws)
            o = (jnp.dot(qg_scr[local, :], sb, preferred_element_type=F32)
                 + jnp.dot(p_scr[local, :], vb, preferred_element_type=F32))
            s_new = s * gl_scr[local, :][0:1, 0:1] + lax.dot_general(
                ke_scr[local, :], vb, (((0,), (0,)), ((), ())), preferred_element_type=F32)
            yield
            rows = _chunk_rows(n)
            o_ref[rows, :] = o
            s_scr[...] = s_new
            silu, _ = _silu_parts(z_ref[rows, :])
            r = lax.rsqrt(jnp.mean(o * o, axis=1, keepdims=True) + EPS)
            og_ref[rows, :] = ((o * r * gn_ref[0:1, :]) * silu).astype(og_ref.dtype)

    def scan_many(first, srcs):
        for j, src in enumerate(srcs):
            yield from scan_steps(first + j, src)

    groups = [all_sets[:GDN_WIDTH], all_sets[GDN_WIDTH:]]
    _interleave(*[prepare_steps(j, groups[0][j]) for j in range(GDN_WIDTH)])
    s_scr[...] = jnp.zeros_like(s_scr)
    for g in range(n_super // GDN_WIDTH):
        cur, nxt = groups[g % 2], groups[(g + 1) % 2]
        first = g * GDN_WIDTH
        following = [prepare_steps(first + GDN_WIDTH + j, nxt[j]) for j in range(GDN_WIDTH)
                     if first + GDN_WIDTH + j < n_super]
        _interleave(scan_many(first, cur), *following)


def _gdn_in_specs(s_len, n_super, from_proj):
    k_col, v_col = (A_K_COL, A_V_COL) if from_proj else (0, 0)
    return [
        pl.BlockSpec((s_len, A_DK), lambda b, h: (b, h)),
        pl.BlockSpec((s_len, A_DK), lambda b, h: (b, k_col + h)),
        pl.BlockSpec((s_len, A_DV), lambda b, h: (b, v_col + h)),
        pl.BlockSpec((s_len, 128), lambda b, h: (b, 0)),
        pl.BlockSpec((None, None, n_super, 1, SUPER), lambda b, h: (b, A_HEADS + h, 0, 0, 0)),
    ]


def _gdn_scan_scratch(s_len):
    return [pltpu.VMEM((A_DK, A_DV), F32), pltpu.VMEM((s_len, A_DV), F32),
            pltpu.VMEM((s_len, A_DK), BF16), pltpu.VMEM((s_len, CHUNK), BF16),
            pltpu.VMEM((s_len, A_DK), BF16), pltpu.VMEM((s_len, A_DK), BF16),
            pltpu.VMEM((s_len, 128), F32)]


def gdn_fwd(proj_a, conv_w8, gates_col, gates_row, norm_g8, n_seq):
    t_rows = proj_a.shape[0]
    s_len = t_rows // n_seq
    n_chunks = s_len // CHUNK
    qk_spec = pl.BlockSpec((s_len, A_DK), lambda b, h: (b, h))
    v_spec = pl.BlockSpec((s_len, A_DV), lambda b, h: (b, h))
    return pl.pallas_call(
        _gdn_fwd_body, name="gdn_fwd", grid=(n_seq, A_HEADS),
        in_specs=_gdn_in_specs(s_len, s_len // SUPER, True) + [
            pl.BlockSpec((8, A_DK), lambda b, h: (0, h)),
            pl.BlockSpec((8, A_DK), lambda b, h: (0, A_K_COL + h)),
            pl.BlockSpec((8, A_DV), lambda b, h: (0, A_V_COL + h)),
            pl.BlockSpec((s_len, A_DV), lambda b, h: (b, A_Z_COL + h)),
            pl.BlockSpec((8, A_DV), lambda b, h: (0, 0)),
        ],
        out_specs=[
            v_spec,
            pl.BlockSpec((s_len, CHUNK), lambda b, h: (b * A_HEADS + h, 0)),
            pl.BlockSpec((None, n_chunks, A_DK, A_DV), lambda b, h: (b * A_HEADS + h, 0, 0, 0)),
            v_spec, qk_spec, qk_spec, v_spec,
        ],
        out_shape=[
            jax.ShapeDtypeStruct((t_rows, A_VW), F32),
            jax.ShapeDtypeStruct((n_seq * A_HEADS * s_len, CHUNK), F32),
            jax.ShapeDtypeStruct((n_seq * A_HEADS, n_chunks, A_DK, A_DV), BF16),
            jax.ShapeDtypeStruct((t_rows, A_VW), BF16),
            jax.ShapeDtypeStruct((t_rows, A_QK), F32),
            jax.ShapeDtypeStruct((t_rows, A_QK), F32),
            jax.ShapeDtypeStruct((t_rows, A_VW), F32),
        ],
        scratch_shapes=_gdn_scan_scratch(SUPER) + (2 * GDN_WIDTH - 1) * _gdn_scan_scratch(SUPER)[1:],
        compiler_params=_params(("arbitrary", "arbitrary")),
    )(proj_a, proj_a, proj_a, gates_col, gates_row, conv_w8, conv_w8, conv_w8, proj_a, norm_g8)


def _gdn_bwd_body(q_ref, k_ref, v_ref, gc_ref, gr_ref, tinv_ref, st_ref, dog_ref, oa_ref, z_ref, gn_ref,
                  xq_ref, xk_ref, xv_ref, wq_ref, wk_ref, wv_ref,
                  dxq_ref, dxk_ref, dxv_ref, dgc_ref, dz_ref, dgn_ref, dwq_ref, dwk_ref, dwv_ref,
                  ds_scr, cq_scr, ck_scr, cv_scr, *sets):
    head = pl.program_id(1)
    n_super = q_ref.shape[0] // SUPER
    ops = (sets[0:7], sets[7:14])
    res = (sets[14:21], sets[21:28])
    whole = pl.ds(0, SUPER)
    tn = functools.partial(lax.dot_general, dimension_numbers=(((0,), (0,)), ((), ())), preferred_element_type=F32)
    nt = functools.partial(lax.dot_general, dimension_numbers=(((1,), (1,)), ((), ())), preferred_element_type=F32)

    @pl.when(head == 0)
    def _():
        dgc_ref[...] = jnp.zeros_like(dgc_ref)

    @pl.when((head == 0) & (pl.program_id(0) == 0))
    def _():
        dgn_ref[...] = jnp.zeros_like(dgn_ref)

    carry = (cq_scr, ck_scr, cv_scr)
    for ref in carry + (dwq_ref, dwk_ref, dwv_ref):
        ref[...] = jnp.zeros_like(ref)

    def common_steps(i):
        rows = _super_rows(i)
        q, k, v = q_ref[rows, :], k_ref[rows, :], v_ref[rows, :]
        t = yield from _gdn_super_steps(q, k, v, gc_ref[rows, :], gr_ref[i], head, tinv_tall=tinv_ref[rows, :])
        return rows, q, k, v, t

    def stage_p(i, parity):
        rows, q, k, _, t = yield from common_steps(i)
        _store_scan_operands(whole, q, k, t, *ops[parity][:6])
        o, d_og, gain = oa_ref[rows, :], dog_ref[rows, :], gn_ref[0:1, :]
        r = lax.rsqrt(jnp.mean(o * o, axis=1, keepdims=True) + EPS)
        silu, dsilu = _silu_parts(z_ref[rows, :])
        xr = o * r
        d_on = d_og * silu
        dz_ref[rows, :] = (d_og * (xr * gain) * dsilu).astype(dz_ref.dtype)
        u = d_on * gain
        ops[parity][6][...] = r * u - xr * (r * r) * jnp.mean(u * o, axis=1, keepdims=True)
        dgn_ref[0:1, :] += jnp.sum(d_on * xr, axis=0, keepdims=True)

    def stage_s(i, parity):
        u_scr, w_scr, p_scr, qg_scr, ke_scr, gl_scr, do_scr = ops[parity]
        vn_scr, dvn_scr, dqg_scr, dw_scr, dkt_scr, sds_scr, dof_scr = res[parity]
        for j in reversed(range(GDN_GROUP)):
            n = i * GDN_GROUP + j
            local = pl.ds(j * CHUNK, CHUNK)
            ds_next = ds_scr[...]
            dsb = _bf(ds_next)
            sb = st_ref[n]
            s = sb.astype(F32)
            d_o = do_scr[local, :]
            dof_scr[local, :] = d_o
            d_ob = _bf(d_o)
            w_s = jnp.dot(w_scr[local, :], sb, preferred_element_type=F32)
            d_vn = tn(p_scr[local, :], d_ob) + jnp.dot(ke_scr[local, :], dsb, preferred_element_type=F32)
            d_qg = nt(d_ob, sb)
            qg_do = tn(qg_scr[local, :], d_ob)
            yield
            v_new = u_scr[local, :] - w_s
            d_vnb = _bf(d_vn)
            d_w = -nt(d_vnb, sb)
            d_kt = nt(_bf(v_new), dsb)
            w_dvn = tn(w_scr[local, :], d_vnb)
            yield
            vn_scr[local, :] = v_new
            dvn_scr[local, :] = d_vn
            dqg_scr[local, :] = d_qg
            dw_scr[local, :] = d_w
            dkt_scr[local, :] = d_kt
            sds = jnp.sum(jnp.sum(s * ds_next, axis=1, keepdims=True), axis=0, keepdims=True)
            sds_scr[local, :] = jnp.broadcast_to(sds, (CHUNK, 128))
            ds_scr[...] = qg_do + gl_scr[local, :][0:1, 0:1] * ds_next - w_dvn

    def conv_bwd(i, rows, x_ref, w_ref, dy, norm_scale, dx_ref, dw_ref, dc_above):
        x, w = x_ref[rows, :], w_ref[...]
        above = x_ref[pl.ds(pl.multiple_of(jnp.maximum(i * SUPER - 8, 0), 8), 8), :]
        ext = jnp.concatenate([jnp.where(i > 0, above, 0.0), x], axis=0)
        c = x * w[A_CONV - 1:A_CONV, :]
        for j in range(1, A_CONV):
            c = c + pltpu.roll(ext, j, 0)[8:, :] * w[A_CONV - 1 - j:A_CONV - j, :]
        sig = _sigmoid(c)
        a = c * sig
        if norm_scale is None:
            da = dy
        else:
            rn = lax.rsqrt(jnp.sum(a * a, axis=1, keepdims=True) + EPS)
            da = norm_scale * (rn * dy - a * (rn * rn * rn) * jnp.sum(dy * a, axis=1, keepdims=True))
        dc = da * (sig * (1.0 + c * (1.0 - sig)))
        ext_dc = jnp.concatenate([dc, dc_above[...]], axis=0)
        dc_above[...] = dc[0:8, :]
        dx = dc * w[A_CONV - 1:A_CONV, :]
        dw_ref[A_CONV - 1:A_CONV, :] += jnp.sum(dc * x, axis=0, keepdims=True)
        for j in range(1, A_CONV):
            dcs = pltpu.roll(ext_dc, SUPER + 8 - j, 0)[:SUPER, :]
            dx = dx + dcs * w[A_CONV - 1 - j:A_CONV - j, :]
            dw_ref[A_CONV - 1 - j:A_CONV - j, :] += jnp.sum(dcs * x, axis=0, keepdims=True)
        dx_ref[rows, :] = dx.astype(dx_ref.dtype)

    def stage_f(i, parity):
        vn_scr, dvn_scr, dqg_scr, dw_scr, dkt_scr, sds_scr, dof_scr = res[parity]
        rows, q, k, v, t = yield from common_steps(i)
        beta, gamma, decay, kb, e_tail = t["beta"], t["gamma"], t["decay"], t["kb"], t["e_tail"]
        row, col, lane, same = t["row"], t["col"], t["lane"], t["same"]
        d_o = dof_scr[...]
        v_new, d_vn = vn_scr[...], dvn_scr[...]
        d_qg, d_w, d_kt = dqg_scr[...], dw_scr[...], dkt_scr[...]
        gamma_last = jnp.exp(t["g_last"])

        d_p = jnp.where(same & (row >= col), _mm_nt(d_o, v_new), 0.0)
        d_ru = _mm_tn(t["tinv"], d_vn)
        d_rw = _mm_tn(t["tinv"], d_w)
        yield
        d_m = jnp.where(same & (row > col), -(_mm_nt(d_ru, t["u"]) + _mm_nt(d_rw, t["w"])), 0.0)
        yield

        x_p = d_p * decay
        y_m = d_m * decay
        d_kb = _mm(y_m, k) + d_rw * gamma
        d_q = _mm(x_p, k) + d_qg * gamma
        d_k = _mm_tn(x_p, q) + _mm_tn(y_m, kb) + d_kb * beta + d_kt * e_tail
        d_v = d_ru * beta
        conv_bwd(i, rows, xq_ref, wq_ref, d_q, A_DK ** -0.5, dxq_ref, dwq_ref, carry[0])
        conv_bwd(i, rows, xk_ref, wk_ref, d_k, 1.0, dxk_ref, dwk_ref, carry[1])
        conv_bwd(i, rows, xv_ref, wv_ref, d_v, None, dxv_ref, dwv_ref, carry[2])

        d_beta = (jnp.sum(d_ru * v, axis=1, keepdims=True)
                  + jnp.sum(d_kb * k, axis=1, keepdims=True))
        z = d_p * t["p"] + d_m * t["m"]
        eps_tail = jnp.sum(d_kt * k, axis=1, keepdims=True) * e_tail
        d_gc = (jnp.sum(z, axis=1, keepdims=True) - _colsum_as_col(z)
                + jnp.sum(d_qg * q, axis=1, keepdims=True) * gamma
                + jnp.sum(d_rw * kb, axis=1, keepdims=True) * gamma
                - eps_tail)
        d_glast = _block_sum(same, eps_tail) + gamma_last * sds_scr[...][:, 0:1]
        yield
        rcol = lax.broadcasted_iota(jnp.int32, (SUPER, 1), 0)
        d_gc = d_gc + jnp.where(rcol % CHUNK == CHUNK - 1, d_glast, 0.0)
        dgc_ref[rows, :] += (jnp.where(lane == head, d_beta, 0.0)
                             + jnp.where(lane == A_HEADS + head, d_gc, 0.0))

    last = n_super - 1
    _drain(stage_p(last, 1))
    ds_scr[...] = jnp.zeros_like(ds_scr)
    _interleave(stage_s(last, 1), stage_p(last - 1, 0))

    def pair(k, carry):
        i = last - 1 - 2 * k
        _interleave(stage_s(i, 0), stage_f(i + 1, 1), stage_p(i - 1, 1))
        _interleave(stage_s(i - 1, 1), stage_f(i, 0), stage_p(i - 2, 0))
        return carry

    lax.fori_loop(0, n_super // 2 - 1, pair, 0)
    _interleave(stage_s(0, 0), stage_f(1, 1))
    _drain(stage_f(0, 0))


def gdn_bwd(q, k, v, gates_col, gates_row, tinv, states, d_og, o, proj_a, conv_w8, norm_g8, n_seq):
    t_rows = q.shape[0]
    s_len = t_rows // n_seq
    n_chunks = s_len // CHUNK
    qk_spec = pl.BlockSpec((s_len, A_DK), lambda b, h: (b, h))
    v_spec = pl.BlockSpec((s_len, A_DV), lambda b, h: (b, h))
    gate_spec = pl.BlockSpec((s_len, 128), lambda b, h: (b, 0))
    gain_spec = pl.BlockSpec((8, A_DV), lambda b, h: (0, 0))
    dw_qk_spec = pl.BlockSpec((None, 8, A_DK), lambda b, h: (b, 0, h))
    dw_v_spec = pl.BlockSpec((None, 8, A_DV), lambda b, h: (b, 0, h))
    ops_set = _gdn_scan_scratch(SUPER)[1:] + [pltpu.VMEM((SUPER, A_DV), F32)]
    res_set = [pltpu.VMEM((SUPER, A_DV), F32), pltpu.VMEM((SUPER, A_DV), F32),
               pltpu.VMEM((SUPER, A_DK), F32), pltpu.VMEM((SUPER, A_DK), F32),
               pltpu.VMEM((SUPER, A_DK), F32), pltpu.VMEM((SUPER, 128), F32), pltpu.VMEM((SUPER, A_DV), F32)]
    return pl.pallas_call(
        _gdn_bwd_body, name="gdn_bwd", grid=(n_seq, A_HEADS),
        in_specs=_gdn_in_specs(s_len, s_len // SUPER, False) + [
            pl.BlockSpec((s_len, CHUNK), lambda b, h: (b * A_HEADS + h, 0)),
            pl.BlockSpec((None, n_chunks, A_DK, A_DV), lambda b, h: (b * A_HEADS + h, 0, 0, 0)),
            v_spec, v_spec,
            pl.BlockSpec((s_len, A_DV), lambda b, h: (b, A_Z_COL + h)),
            gain_spec,
            pl.BlockSpec((s_len, A_DK), lambda b, h: (b, h)),
            pl.BlockSpec((s_len, A_DK), lambda b, h: (b, A_K_COL + h)),
            pl.BlockSpec((s_len, A_DV), lambda b, h: (b, A_V_COL + h)),
            pl.BlockSpec((8, A_DK), lambda b, h: (0, h)),
            pl.BlockSpec((8, A_DK), lambda b, h: (0, A_K_COL + h)),
            pl.BlockSpec((8, A_DV), lambda b, h: (0, A_V_COL + h)),
        ],
        out_specs=[qk_spec, qk_spec, v_spec, gate_spec, v_spec, gain_spec, dw_qk_spec, dw_qk_spec, dw_v_spec],
        out_shape=[
            jax.ShapeDtypeStruct((t_rows, A_QK), BF16),
            jax.ShapeDtypeStruct((t_rows, A_QK), BF16),
            jax.ShapeDtypeStruct((t_rows, A_VW), BF16),
            jax.ShapeDtypeStruct((t_rows, 128), F32),
            jax.ShapeDtypeStruct((t_rows, A_VW), BF16),
            jax.ShapeDtypeStruct((8, A_DV), F32),
            jax.ShapeDtypeStruct((n_seq, 8, A_QK), F32),
            jax.ShapeDtypeStruct((n_seq, 8, A_QK), F32),
            jax.ShapeDtypeStruct((n_seq, 8, A_VW), F32),
        ],
        scratch_shapes=(_gdn_scan_scratch(SUPER)[:1]
                        + [pltpu.VMEM((8, A_DK), F32), pltpu.VMEM((8, A_DK), F32), pltpu.VMEM((8, A_DV), F32)]
                        + 2 * ops_set + 2 * res_set),
        compiler_params=_params(("arbitrary", "arbitrary")),
    )(q, k, v, gates_col, gates_row, tinv, states, d_og, o, proj_a, norm_g8,
      proj_a, proj_a, proj_a, conv_w8, conv_w8, conv_w8)


GATE_TILE = 512


def _softplus(y):
    return jnp.maximum(y, 0.0) + jnp.log1p(jnp.exp(-jnp.abs(y)))


def _gate_values(x, prm):
    beta = _sigmoid(x)
    y = x + prm[1:2, :]
    neg_a = -jnp.exp(prm[0:1, :])
    g = neg_a * _softplus(y)
    return beta, y, neg_a, g


def _a_gates_fwd_body(x_ref, prm_ref, gc_ref, gr_ref):
    x = x_ref[...]
    tm = x.shape[0]
    beta, _, _, g = _gate_values(x, prm_ref[...])
    in_chunk = lax.broadcasted_iota(jnp.int32, (tm, 1), 0) % CHUNK
    s = 1
    while s < CHUNK:
        g = g + jnp.where(in_chunk >= s, pltpu.roll(g, s, 0), 0.0)
        s *= 2
    lane = lax.broadcasted_iota(jnp.int32, x.shape, 1)
    out = jnp.where(lane < A_HEADS, beta, jnp.where(lane < 2 * A_HEADS, g, 0.0))
    gc_ref[...] = out
    gr_ref[...] = out.T[0:2 * A_HEADS, :]


def a_gates_fwd(proj_a, prm, n_seq):
    t_rows = proj_a.shape[0]
    s_len = t_rows // n_seq
    tm = min(GATE_TILE, s_len)
    n_t = s_len // tm
    return pl.pallas_call(
        _a_gates_fwd_body, name="a_gates_fwd", grid=(n_seq, n_t),
        in_specs=[pl.BlockSpec((tm, 128), lambda b, i: (b * n_t + i, A_GATE_COL)),
                  pl.BlockSpec((8, 128), lambda b, i: (0, 0))],
        out_specs=[pl.BlockSpec((tm, 128), lambda b, i: (b * n_t + i, 0)),
                   pl.BlockSpec((None, 2 * A_HEADS, tm), lambda b, i: (b, 0, i))],
        out_shape=[jax.ShapeDtypeStruct((t_rows, 128), F32),
                   jax.ShapeDtypeStruct((n_seq, 2 * A_HEADS, s_len), F32)],
        compiler_params=_params(("arbitrary", "arbitrary")),
    )(proj_a, prm)


def _a_gates_bwd_body(x_ref, prm_ref, dgc_ref, dx_ref, dprm_ref):
    first = (pl.program_id(0) == 0) & (pl.program_id(1) == 0)
    x = x_ref[...]
    tm = x.shape[0]
    beta, y, neg_a, g = _gate_values(x, prm_ref[...])
    d = dgc_ref[...]
    in_chunk = lax.broadcasted_iota(jnp.int32, (tm, 1), 0) % CHUNK
    dg = d
    s = 1
    while s < CHUNK:
        dg = dg + jnp.where(in_chunk < CHUNK - s, pltpu.roll(dg, tm - s, 0), 0.0)
        s *= 2
    lane = lax.broadcasted_iota(jnp.int32, x.shape, 1)
    is_decay = (lane >= A_HEADS) & (lane < 2 * A_HEADS)
    d_alogit = jnp.where(is_decay, dg * neg_a * _sigmoid(y), 0.0)
    dx_ref[...] = jnp.where(lane < A_HEADS, d * beta * (1.0 - beta), d_alogit).astype(dx_ref.dtype)

    @pl.when(first)
    def _():
        dprm_ref[...] = jnp.zeros_like(dprm_ref)

    dprm_ref[0:1, :] += jnp.sum(jnp.where(is_decay, dg * g, 0.0), axis=0, keepdims=True)
    dprm_ref[1:2, :] += jnp.sum(d_alogit, axis=0, keepdims=True)


def a_gates_bwd(proj_a, prm, dgates_col, n_seq):
    t_rows = proj_a.shape[0]
    s_len = t_rows // n_seq
    tm = min(GATE_TILE, s_len)
    n_t = s_len // tm
    return pl.pallas_call(
        _a_gates_bwd_body, name="a_gates_bwd", grid=(n_seq, n_t),
        in_specs=[pl.BlockSpec((tm, 128), lambda b, i: (b * n_t + i, A_GATE_COL)),
                  pl.BlockSpec((8, 128), lambda b, i: (0, 0)),
                  pl.BlockSpec((tm, 128), lambda b, i: (b * n_t + i, 0))],
        out_specs=[pl.BlockSpec((tm, 128), lambda b, i: (b * n_t + i, 0)),
                   pl.BlockSpec((8, 128), lambda b, i: (0, 0))],
        out_shape=[jax.ShapeDtypeStruct((t_rows, 128), BF16),
                   jax.ShapeDtypeStruct((8, 128), F32)],
        compiler_params=_params(("arbitrary", "arbitrary")),
    )(proj_a, prm, dgates_col)


ROW_TILE = 512
A_Z_COL = (2 * A_QK + A_VW) // A_DV


def _silu_parts(z):
    sig = _sigmoid(z)
    return z * sig, sig * (1.0 + z * (1.0 - sig))


NEG_BIG = -1e30
ATT_SCALE = B_DH ** -0.5


def _swap_rope_halves(x):
    src = lax.broadcasted_iota(jnp.int32, (B_DH, B_DH), 0)
    dst = lax.broadcasted_iota(jnp.int32, (B_DH, B_DH), 1)
    pick = ((dst < ROPE_HALF) & (src == dst + ROPE_HALF)) | (
        (dst >= ROPE_HALF) & (dst < ROPE_DIMS) & (src == dst - ROPE_HALF))
    return jnp.dot(_bf(x), pick.astype(BF16), preferred_element_type=F32)


def _norm_rope(x, gain, cos_t, sin_t):
    r = lax.rsqrt(jnp.mean(x * x, axis=1, keepdims=True) + EPS)
    xn = x * r * gain
    return xn * cos_t + _swap_rope_halves(xn) * sin_t, r


def _norm_rope_bwd(x, r, gain, cos_t, sin_t, dy):
    d_xn = dy * cos_t + _swap_rope_halves(dy * sin_t)
    xr = x * r
    u = d_xn * gain
    dx = r * u - xr * (r * r) * jnp.mean(u * x, axis=1, keepdims=True)
    return dx, jnp.sum(d_xn * xr, axis=0, keepdims=True)


def _stream_rows(idx, dilation, s_len):
    nb = s_len // dilation // B_BLOCK
    r = idx // nb
    m = idx % nb
    cur = r + m * (B_BLOCK * dilation)
    prev = r + jnp.maximum(m - 1, 0) * (B_BLOCK * dilation)
    return cur, prev, m > 0


def _rows(start, dilation):
    if dilation == 1:
        return pl.ds(start, B_BLOCK)
    return pl.ds(start, B_BLOCK, stride=dilation)


ATT_UNROLL = 16


def _band_mask(has_prev):
    qi = lax.broadcasted_iota(jnp.int32, (B_BLOCK, 2 * B_BLOCK), 0)
    kj = lax.broadcasted_iota(jnp.int32, (B_BLOCK, 2 * B_BLOCK), 1)
    return ((kj < B_BLOCK) & (kj >= qi) & has_prev) | ((kj >= B_BLOCK) & (kj - B_BLOCK <= qi))


def _attn_fwd_body(qkv_ref, z_ref, cos_ref, sin_ref, gain_ref, og_ref, o_ref, lse_ref,
                   qn_scr, kn_scr, og_scr, lg_scr):
    head, grp = pl.program_id(1), pl.program_id(2)
    s_len = z_ref.shape[0]
    n_blocks = s_len // B_BLOCK
    cos_t, sin_t = cos_ref[...], sin_ref[...]

    for gi, dil in enumerate(B_DILATIONS):
        @pl.when(grp == gi)
        def _(gi=gi, dil=dil):
            qn_scr[...], _ = _norm_rope(qkv_ref[0], gain_ref[gi:gi + 1, :], cos_t, sin_t)
            kn_scr[...], _ = _norm_rope(qkv_ref[1], gain_ref[B_GROUPS + gi:B_GROUPS + gi + 1, :], cos_t, sin_t)

            ones = jnp.ones((2 * B_BLOCK, B_DH), BF16)

            def blocks(it, carry):
                scored = []
                for j in range(ATT_UNROLL):
                    cur, prev, has_prev = _stream_rows(it * ATT_UNROLL + j, dil, s_len)
                    rc, rp = _rows(cur, dil), _rows(prev, dil)
                    k2 = jnp.concatenate([kn_scr[rp, :], kn_scr[rc, :]], axis=0)
                    scored.append((rc, rp, has_prev, _mm_nt(qn_scr[rc, :], k2) * ATT_SCALE))
                summed = []
                for rc, rp, has_prev, s in scored:
                    s = jnp.where(_band_mask(has_prev), s, NEG_BIG)
                    mx = jnp.max(s, axis=1, keepdims=True)
                    v2 = jnp.concatenate([qkv_ref.at[2][rp, :], qkv_ref.at[2][rc, :]], axis=0)
                    acc = jnp.dot(_bf(jnp.exp(s - mx)), jnp.concatenate([_bf(v2), ones], axis=1),
                                  preferred_element_type=F32)
                    summed.append((rc, mx, acc))
                for rc, mx, acc in summed:
                    den = acc[:, B_DH:B_DH + 1]
                    og_scr.at[gi][rc, :] = acc[:, :B_DH] / den
                    lg_scr.at[gi][rc, :] = jnp.broadcast_to(mx + jnp.log(den), (B_BLOCK, B_DH))
                return carry

            lax.fori_loop(0, n_blocks // ATT_UNROLL, blocks, 0)

    @pl.when(grp == B_GROUPS - 1)
    def _():
        l0, l1, l2 = lg_scr[0], lg_scr[1], lg_scr[2]
        mx = jnp.maximum(jnp.maximum(l0, l1), l2)
        w0, w1, w2 = jnp.exp(l0 - mx), jnp.exp(l1 - mx), jnp.exp(l2 - mx)
        den = w0 + w1 + w2
        o = (w0 * og_scr[0] + w1 * og_scr[1] + w2 * og_scr[2]) / den
        silu, _ = _silu_parts(z_ref[...])
        o_ref[...] = o
        og_ref[...] = (o * silu).astype(og_ref.dtype)
        @pl.when(head == 0)
        def _():
            lse_ref[...] = jnp.zeros_like(lse_ref)

        lane = lax.broadcasted_iota(jnp.int32, o.shape, 1)
        lse_ref[...] = jnp.where(lane == head, mx + jnp.log(den), lse_ref[...])


def attn_fwd(proj_b, cos_t, sin_t, gains8, n_seq):
    t_rows = proj_b.shape[1]
    s_len = t_rows // n_seq
    head_blk = pl.BlockSpec((s_len, B_DH), lambda b, h, g: (b, h))
    seq_blk = pl.BlockSpec((s_len, 128), lambda b, h, g: (b, 0))
    return pl.pallas_call(
        _attn_fwd_body, name="attn_fwd", grid=(n_seq, B_HEADS, B_GROUPS),
        in_specs=[
            pl.BlockSpec((3, s_len, B_DH), lambda b, h, g: (g, b, h)),
            pl.BlockSpec((None, s_len, B_DH), lambda b, h, g: (B_PIECES - 1, b, h)),
            seq_blk, seq_blk,
            pl.BlockSpec((8, 128), lambda b, h, g: (0, 0)),
        ],
        out_specs=[head_blk, head_blk, seq_blk],
        out_shape=[jax.ShapeDtypeStruct((t_rows, B_W), BF16),
                   jax.ShapeDtypeStruct((t_rows, B_W), F32),
                   jax.ShapeDtypeStruct((t_rows, 128), F32)],
        scratch_shapes=[pltpu.VMEM((s_len, B_DH), F32), pltpu.VMEM((s_len, B_DH), F32),
                        pltpu.VMEM((B_GROUPS, s_len, B_DH), F32), pltpu.VMEM((B_GROUPS, s_len, B_DH), F32)],
        compiler_params=_params(("arbitrary", "arbitrary", "arbitrary")),
    )(proj_b, proj_b, cos_t, sin_t, gains8)


def _attn_bwd_body(qkv_ref, z_ref, cos_ref, sin_ref, gain_ref, dog_ref, o_ref, lse_ref,
                   dqkv_ref, dz_ref, dgain_ref,
                   qn_scr, kn_scr, dqn_scr, dkn_scr, do_scr, dl_scr, ls_scr, dv_scr):
    head, grp = pl.program_id(1), pl.program_id(2)
    first = (pl.program_id(0) == 0) & (head == 0) & (grp == 0)
    s_len = z_ref.shape[0]
    n_blocks = s_len // B_BLOCK
    cos_t, sin_t = cos_ref[...], sin_ref[...]

    @pl.when(first)
    def _():
        dgain_ref[...] = jnp.zeros_like(dgain_ref)

    @pl.when(grp == 0)
    def _():
        d_og, o = dog_ref[...], o_ref[...]
        silu, dsilu = _silu_parts(z_ref[...])
        d_o = d_og * silu
        dz_ref[...] = (d_og * o * dsilu).astype(dz_ref.dtype)
        do_scr[...] = d_o
        dl_scr[...] = jnp.broadcast_to(jnp.sum(d_o * o, axis=1, keepdims=True), o.shape)
        lane = lax.broadcasted_iota(jnp.int32, o.shape, 1)
        ls_scr[...] = jnp.broadcast_to(
            jnp.sum(jnp.where(lane == head, lse_ref[...], 0.0), axis=1, keepdims=True), o.shape)

    for gi, dil in enumerate(B_DILATIONS):
        @pl.when(grp == gi)
        def _(gi=gi, dil=dil):
            q_raw, k_raw = qkv_ref[0], qkv_ref[1]
            gq = gain_ref[gi:gi + 1, :]
            gk = gain_ref[B_GROUPS + gi:B_GROUPS + gi + 1, :]
            qn_scr[...], rq = _norm_rope(q_raw, gq, cos_t, sin_t)
            kn_scr[...], rk = _norm_rope(k_raw, gk, cos_t, sin_t)
            def blocks(it, carry):
                scored = []
                for j in range(ATT_UNROLL):
                    cur, prev, has_prev = _stream_rows(it * ATT_UNROLL + j, dil, s_len)
                    rc, rp = _rows(cur, dil), _rows(prev, dil)
                    qb, d_ob = _bf(qn_scr[rc, :]), _bf(do_scr[rc, :])
                    k2 = _bf(jnp.concatenate([kn_scr[rp, :], kn_scr[rc, :]], axis=0))
                    v2 = _bf(jnp.concatenate([qkv_ref.at[2][rp, :], qkv_ref.at[2][rc, :]], axis=0))
                    scored.append((rc, rp, has_prev, qb, d_ob, k2,
                                   _mm_nt(qb, k2) * ATT_SCALE, _mm_nt(d_ob, v2)))
                grads = []
                for rc, rp, has_prev, qb, d_ob, k2, s, d_p in scored:
                    p = jnp.exp(jnp.where(_band_mask(has_prev), s - ls_scr[rc, :][:, 0:1], NEG_BIG))
                    ds = _bf(p * (d_p - dl_scr[rc, :][:, 0:1]))
                    grads.append((rc, rp, has_prev,
                                  _mm(ds, k2) * ATT_SCALE, _mm_tn(ds, qb) * ATT_SCALE, _mm_tn(_bf(p), d_ob)))
                for j, (rc, rp, has_prev, dq, dk2, dv2) in enumerate(grads):
                    dqn_scr[rc, :] = dq
                    if j == 0:
                        @pl.when(has_prev)
                        def _():
                            dkn_scr[rp, :] += dk2[:B_BLOCK]
                            dv_scr[rp, :] += dv2[:B_BLOCK]
                    if j + 1 < ATT_UNROLL:
                        dkn_scr[rc, :] = dk2[B_BLOCK:] + grads[j + 1][4][:B_BLOCK]
                        dv_scr[rc, :] = dv2[B_BLOCK:] + grads[j + 1][5][:B_BLOCK]
                    else:
                        dkn_scr[rc, :] = dk2[B_BLOCK:]
                        dv_scr[rc, :] = dv2[B_BLOCK:]
                return carry

            lax.fori_loop(0, n_blocks // ATT_UNROLL, blocks, 0)
            dq, dgq = _norm_rope_bwd(q_raw, rq, gq, cos_t, sin_t, dqn_scr[...])
            dk, dgk = _norm_rope_bwd(k_raw, rk, gk, cos_t, sin_t, dkn_scr[...])
            dqkv_ref[0] = dq.astype(dqkv_ref.dtype)
            dqkv_ref[1] = dk.astype(dqkv_ref.dtype)
            dqkv_ref[2] = dv_scr[...].astype(dqkv_ref.dtype)
            dgain_ref[gi:gi + 1, :] += dgq
            dgain_ref[B_GROUPS + gi:B_GROUPS + gi + 1, :] += dgk


def attn_bwd(proj_b, cos_t, sin_t, gains8, d_og, o, lse, n_seq):
    t_rows = proj_b.shape[1]
    s_len = t_rows // n_seq
    head_blk = pl.BlockSpec((s_len, B_DH), lambda b, h, g: (b, h))
    seq_blk = pl.BlockSpec((s_len, 128), lambda b, h, g: (b, 0))
    grp_blk = pl.BlockSpec((3, s_len, B_DH), lambda b, h, g: (g, b, h))
    gain_blk = pl.BlockSpec((8, 128), lambda b, h, g: (0, 0))
    return pl.pallas_call(
        _attn_bwd_body, name="attn_bwd", grid=(n_seq, B_HEADS, B_GROUPS),
        in_specs=[
            grp_blk,
            pl.BlockSpec((None, s_len, B_DH), lambda b, h, g: (B_PIECES - 1, b, h)),
            seq_blk, seq_blk, gain_blk, head_blk, head_blk, seq_blk,
        ],
        out_specs=[grp_blk, head_blk, gain_blk],
        out_shape=[jax.ShapeDtypeStruct((3 * B_GROUPS, t_rows, B_W), BF16),
                   jax.ShapeDtypeStruct((t_rows, B_W), BF16),
                   jax.ShapeDtypeStruct((8, 128), F32)],
        scratch_shapes=[pltpu.VMEM((s_len, B_DH), F32) for _ in range(8)],
        compiler_params=_params(("arbitrary", "arbitrary", "arbitrary")),
    )(proj_b, proj_b, cos_t, sin_t, gains8, d_og, o, lse)


def rope_tables(positions):
    inv_freq = ROPE_THETA ** (-jnp.arange(0, ROPE_DIMS, 2, dtype=F32) / ROPE_DIMS)
    ang = positions.astype(F32)[:, None] * inv_freq
    cos, sin = jnp.cos(ang), jnp.sin(ang)
    t_rows = positions.shape[0]
    rest = B_DH - ROPE_DIMS
    cos_t = jnp.concatenate([cos, cos, jnp.ones((t_rows, rest), F32)], axis=1)
    sin_t = jnp.concatenate([-sin, sin, jnp.zeros((t_rows, rest), F32)], axis=1)
    return cos_t, sin_t


def _rms_fwd_body(x_ref, g_ref, *rest, layer):
    h_ref, ht_ref = rest[-2:]
    x = x_ref[...]
    r = lax.rsqrt(jnp.mean(x * x, axis=1, keepdims=True) + EPS)
    h = x * r * g_ref[layer:layer + 1, :]
    h_ref[...] = h.astype(h_ref.dtype)
    ht_ref[...] = h.T.astype(ht_ref.dtype)


def rms_fwd(x, gains8, layer, after=None):
    t_rows, d = x.shape
    tm = min(ROW_TILE, t_rows)
    in_specs = [pl.BlockSpec((tm, d), lambda i: (i, 0)), pl.BlockSpec((8, d), lambda i: (0, 0))]
    args = [x, gains8]
    if after is not None:
        in_specs.append(HBM_SPEC)
        args.append(after)
    return pl.pallas_call(
        functools.partial(_rms_fwd_body, layer=layer), name=f"rms_fwd_{layer}", grid=(t_rows // tm,),
        in_specs=in_specs,
        out_specs=[pl.BlockSpec((tm, d), lambda i: (i, 0)), pl.BlockSpec((d, tm), lambda i: (0, i))],
        out_shape=[jax.ShapeDtypeStruct((t_rows, d), BF16), jax.ShapeDtypeStruct((d, t_rows), BF16)],
        compiler_params=_params(("arbitrary",)),
    )(*args)


def _rms_bwd_body(x_ref, g_ref, dh_ref, res_ref, dx_ref, dg_ref, *, layer):
    x, dh = x_ref[...], dh_ref[...]
    r = lax.rsqrt(jnp.mean(x * x, axis=1, keepdims=True) + EPS)
    xr = x * r
    u = dh * g_ref[layer:layer + 1, :]
    dx_ref[...] = res_ref[...] + r * u - xr * (r * r) * jnp.mean(u * x, axis=1, keepdims=True)

    @pl.when(pl.program_id(0) == 0)
    def _():
        dg_ref[...] = jnp.zeros_like(dg_ref)

    dg_ref[0:1, :] += jnp.sum(dh * xr, axis=0, keepdims=True)


def rms_bwd(x, gains8, layer, dh, d_res):
    t_rows, d = x.shape
    tm = min(ROW_TILE, t_rows)
    blk = pl.BlockSpec((tm, d), lambda i: (i, 0))
    gblk = pl.BlockSpec((8, d), lambda i: (0, 0))
    return pl.pallas_call(
        functools.partial(_rms_bwd_body, layer=layer), name=f"rms_bwd_{layer}", grid=(t_rows // tm,),
        in_specs=[blk, gblk, blk, blk],
        out_specs=[blk, gblk],
        out_shape=[jax.ShapeDtypeStruct((t_rows, d), F32), jax.ShapeDtypeStruct((8, d), F32)],
        compiler_params=_params(("arbitrary",)),
    )(x, gains8, dh, d_res)


def _piece_col(p):
    return jnp.where(p < 3 * B_GROUPS, (p % 3) * B_GROUPS + p // 3, 3 * B_GROUPS)


def _mm_nn_body(a_ref, w_ref, *rest, has_res, a_resident):
    o_ref = rest[-1]
    tm = o_ref.shape[0]
    a = a_ref[pl.ds(pl.multiple_of(pl.program_id(1) * tm, tm), tm), :] if a_resident else a_ref[...]
    acc = jnp.dot(a, w_ref[...], preferred_element_type=F32)
    if has_res:
        acc = acc + rest[0][...]
    o_ref[...] = acc


def mm_nn(a, w, residual=None, *, tn, name, a_resident=False):
    m, k = a.shape
    n = w.shape[1]
    tm = min(NT_ROW_TILE, m)
    a_spec = pl.BlockSpec((m, k), lambda j, i: (0, 0)) if a_resident else pl.BlockSpec((tm, k), lambda j, i: (i, 0))
    in_specs = [a_spec, pl.BlockSpec((k, tn), lambda j, i: (0, j))]
    args = [a, w]
    if residual is not None:
        in_specs.append(pl.BlockSpec((tm, tn), lambda j, i: (i, j)))
        args.append(residual)
    return pl.pallas_call(
        functools.partial(_mm_nn_body, has_res=residual is not None, a_resident=a_resident), name=name,
        grid=(n // tn, m // tm),
        in_specs=in_specs,
        out_specs=pl.BlockSpec((tm, tn), lambda j, i: (i, j)),
        out_shape=jax.ShapeDtypeStruct((m, n), F32),
        compiler_params=_params(("arbitrary", "arbitrary")),
    )(*args)


def mm_nn_pieces(a, w, *, name):
    m, k = a.shape
    tm = min(NT_ROW_TILE, m)
    return pl.pallas_call(
        functools.partial(_mm_nn_body, has_res=False, a_resident=True), name=name, grid=(B_PIECES, m // tm),
        in_specs=[pl.BlockSpec((m, k), lambda p, i: (0, 0)),
                  pl.BlockSpec((k, B_W), lambda p, i: (0, _piece_col(p)))],
        out_specs=pl.BlockSpec((None, tm, B_W), lambda p, i: (p, i, 0)),
        out_shape=jax.ShapeDtypeStruct((B_PIECES, m, B_W), F32),
        compiler_params=_params(("arbitrary", "arbitrary")),
    )(a, w)


NT_ROW_TILE = 1024


def _mm_nt_body(g_ref, w_ref, *rest, has_init):
    o_ref = rest[-1]
    j = pl.program_id(1)
    part = lax.dot_general(_bf(g_ref[...]), w_ref[...], (((1,), (1,)), ((), ())), preferred_element_type=F32)

    @pl.when(j == 0)
    def _():
        o_ref[...] = part + rest[0][...] if has_init else part

    @pl.when(j > 0)
    def _():
        o_ref[...] += part


def mm_nt(g, w, init=None, *, tn, col_off=0, name, after=None):
    m, n = g.shape
    k = w.shape[0]
    tm = min(NT_ROW_TILE, m)
    in_specs = [pl.BlockSpec((tm, tn), lambda i, j: (i, j)),
                pl.BlockSpec((k, tn), lambda i, j: (0, col_off + j))]
    args = [g, w]
    if init is not None:
        in_specs.append(pl.BlockSpec((tm, k), lambda i, j: (i, 0)))
        args.append(init)
    if after is not None:
        in_specs.append(HBM_SPEC)
        args.append(after)
    return pl.pallas_call(
        functools.partial(_mm_nt_body, has_init=init is not None), name=name, grid=(m // tm, n // tn),
        in_specs=in_specs,
        out_specs=pl.BlockSpec((tm, k), lambda i, j: (i, 0)),
        out_shape=jax.ShapeDtypeStruct((m, k), F32),
        compiler_params=_params(("arbitrary", "arbitrary")),
    )(*args)


def mm_nt_multi(gs, w, *, tn, name, after=None):
    m = gs[0].shape[0]
    k = w.shape[0]
    tm = min(NT_ROW_TILE, m)
    tiles = [g.shape[1] // tn for g in gs]
    starts = [sum(tiles[:i]) for i in range(len(gs))]

    def body(*refs):
        g_refs, w_ref, o_ref = refs[:len(gs)], refs[len(gs)], refs[-1]
        j = pl.program_id(1)
        for g_ref, lo, cnt in zip(g_refs, starts, tiles):
            @pl.when((j >= lo) & (j < lo + cnt))
            def _(g_ref=g_ref):
                part = lax.dot_general(_bf(g_ref[...]), w_ref[...], (((1,), (1,)), ((), ())),
                                       preferred_element_type=F32)

                @pl.when(j == 0)
                def _():
                    o_ref[...] = part

                @pl.when(j > 0)
                def _():
                    o_ref[...] += part

    def g_spec(lo, cnt):
        return pl.BlockSpec((tm, tn), lambda i, j: (i, jnp.clip(j - lo, 0, cnt - 1)))

    in_specs = [g_spec(lo, cnt) for lo, cnt in zip(starts, tiles)] + [pl.BlockSpec((k, tn), lambda i, j: (0, j))]
    args = list(gs) + [w]
    if after is not None:
        in_specs.append(HBM_SPEC)
        args.append(after)
    return pl.pallas_call(
        body, name=name, grid=(m // tm, sum(tiles)),
        in_specs=in_specs,
        out_specs=pl.BlockSpec((tm, k), lambda i, j: (i, 0)),
        out_shape=jax.ShapeDtypeStruct((m, k), F32),
        compiler_params=_params(("arbitrary", "arbitrary")),
    )(*args)


def mm_nt_pieces(g9, gz, w, *, name):
    n_p, m, _ = g9.shape
    k = w.shape[0]
    tm = min(2 * NT_ROW_TILE, m)

    def body(g_ref, z_ref, w_ref, o_ref):
        p = pl.program_id(1)

        def accumulate(src):
            part = lax.dot_general(_bf(src[...]), w_ref[...], (((1,), (1,)), ((), ())), preferred_element_type=F32)

            @pl.when(p == 0)
            def _():
                o_ref[...] = part

            @pl.when(p > 0)
            def _():
                o_ref[...] += part

        @pl.when(p < n_p)
        def _():
            accumulate(g_ref)

        @pl.when(p == n_p)
        def _():
            accumulate(z_ref)

    return pl.pallas_call(
        body, name=name, grid=(m // tm, n_p + 1),
        in_specs=[pl.BlockSpec((None, tm, B_W), lambda i, p: (jnp.minimum(p, n_p - 1), i, 0)),
                  pl.BlockSpec((tm, B_W), lambda i, p: (i, 0)),
                  pl.BlockSpec((k, B_W), lambda i, p: (0, _piece_col(p)))],
        out_specs=pl.BlockSpec((tm, k), lambda i, p: (i, 0)),
        out_shape=jax.ShapeDtypeStruct((m, k), F32),
        compiler_params=_params(("arbitrary", "arbitrary")),
    )(g9, gz, w)


def _mm_tn_body(a_ref, g_ref, o_ref, *, a_is_transposed):
    lhs_dim = 1 if a_is_transposed else 0
    o_ref[...] = lax.dot_general(a_ref[...], _bf(g_ref[...]), (((lhs_dim,), (0,)), ((), ())),
                                 preferred_element_type=F32).astype(o_ref.dtype)


def mm_tn(a, g, *, tn, out_dtype, name, a_is_transposed=False):
    k = a.shape[0] if a_is_transposed else a.shape[1]
    m, n = g.shape
    return pl.pallas_call(
        functools.partial(_mm_tn_body, a_is_transposed=a_is_transposed), name=name, grid=(n // tn,),
        in_specs=[pl.BlockSpec(a.shape, lambda j: (0, 0)), pl.BlockSpec((m, tn), lambda j: (0, j))],
        out_specs=pl.BlockSpec((k, tn), lambda j: (0, j)),
        out_shape=jax.ShapeDtypeStruct((k, n), out_dtype),
        compiler_params=_params(("arbitrary",)),
    )(a, g)


B_UNIT = 256
B_IN_COLS = B_PIECES * B_W
B_SHARD_UNITS = B_IN_COLS // N_DEV // B_UNIT


def mm_tn_b_in(at, g9, gz, *, out_dtype, name):
    k, m = at.shape
    per_piece = B_W // B_UNIT
    body_one = functools.partial(_mm_tn_body, a_is_transposed=True)
    n_units = B_IN_COLS // B_UNIT

    def g_map(u):
        nat = jnp.minimum(u // per_piece, 3 * B_GROUPS - 1)
        piece = (nat % B_GROUPS) * 3 + nat // B_GROUPS
        return (piece, 0, u % per_piece)

    def body(a_ref, g_ref, z_ref, o_ref):
        u = pl.program_id(0)

        @pl.when(u < 3 * B_GROUPS * per_piece)
        def _():
            body_one(a_ref, g_ref, o_ref)

        @pl.when(u >= 3 * B_GROUPS * per_piece)
        def _():
            body_one(a_ref, z_ref, o_ref)

    return pl.pallas_call(
        body, name=name, grid=(n_units,),
        in_specs=[pl.BlockSpec((k, m), lambda u: (0, 0)),
                  pl.BlockSpec((None, m, B_UNIT), g_map),
                  pl.BlockSpec((m, B_UNIT), lambda u: (0, jnp.where(u < 3 * B_GROUPS * per_piece, 0, u % per_piece)))],
        out_specs=pl.BlockSpec((None, k, B_UNIT), lambda u: (u // B_SHARD_UNITS, 0, u % B_SHARD_UNITS)),
        out_shape=jax.ShapeDtypeStruct((N_DEV, k, B_IN_COLS // N_DEV), out_dtype),
        compiler_params=_params(("arbitrary",)),
    )(at, g9, gz)


def _loss_body(y_ref, t_ref, dy_ref, loss_ref, acc):
    i = pl.program_id(0)
    d = y_ref.shape[1]
    err = y_ref[...] - t_ref[...]
    dy_ref[...] = err * (1.0 / d)

    @pl.when(i == 0)
    def _():
        acc[...] = jnp.zeros_like(acc)

    acc[...] += jnp.sum(err * err, axis=0, keepdims=True)

    @pl.when(i == pl.num_programs(0) - 1)
    def _():
        total = jnp.sum(acc[...], axis=1, keepdims=True) * (0.5 / d)
        loss_ref[...] = jnp.broadcast_to(total, loss_ref.shape)


def loss_head(y, target):
    t_rows, d = y.shape
    tm = min(ROW_TILE, t_rows)
    blk = pl.BlockSpec((tm, d), lambda i: (i, 0))
    return pl.pallas_call(
        _loss_body, name="loss_head", grid=(t_rows // tm,),
        in_specs=[blk, blk],
        out_specs=[blk, pl.BlockSpec((8, 128), lambda i: (0, 0))],
        out_shape=[jax.ShapeDtypeStruct((t_rows, d), F32), jax.ShapeDtypeStruct((8, 128), F32)],
        scratch_shapes=[pltpu.VMEM((1, d), F32)],
        compiler_params=_params(("arbitrary",)),
    )(y, target)


def _adamw_body(p_ref, w_ref, m_ref, v_ref, g_ref, d_ref, nm_ref, nv_ref):
    g = p_ref[0].astype(F32)
    for s in range(1, N_DEV):
        g = g + p_ref[s].astype(F32)
    w = w_ref[...]
    m = ADAM_B1 * m_ref[...] + (1.0 - ADAM_B1) * g
    v = ADAM_B2 * v_ref[...] + (1.0 - ADAM_B2) * (g * g)
    m_hat = m / (1.0 - ADAM_B1 ** ADAM_STEP)
    v_hat = v / (1.0 - ADAM_B2 ** ADAM_STEP)
    g_ref[...] = g
    d_ref[...] = -ADAM_LR * (m_hat / (jnp.sqrt(v_hat) + ADAM_EPS) + ADAM_WD * w)
    nm_ref[...] = m
    nv_ref[...] = v


def adamw(parts, w, m, v, *, name):
    _, r, c = w.shape
    tr = r if r <= 256 else 256
    blk = pl.BlockSpec((None, tr, c), lambda i: (0, i, 0))
    out = jax.ShapeDtypeStruct((1, r, c), F32)
    return pl.pallas_call(
        _adamw_body, name=name, grid=(r // tr,),
        in_specs=[pl.BlockSpec((N_DEV, tr, c), lambda i: (0, i, 0)), blk, blk, blk],
        out_specs=[blk, blk, blk, blk],
        out_shape=[out, out, out, out],
        compiler_params=_params(("arbitrary",)),
    )(parts, w, m, v)


MESH_ID = pl.DeviceIdType.MESH
HBM_SPEC = pl.BlockSpec(memory_space=pl.ANY)


def _my_place():
    return lax.axis_index("x"), lax.axis_index("y"), lax.axis_index("c")


def _flat(x, y, c):
    return 4 * x + 2 * y + c


def _all_gather_body(*refs, n):
    ins, outs = refs[:n], refs[n:2 * n]
    send_sems, recv_sems, local_sems = refs[2 * n:]
    x, y, c = _my_place()
    me, sibling = (x, y, c), (x, y, 1 - c)
    chips = [(1 - x, y), (x, 1 - y), (1 - x, 1 - y)]
    pending = []
    for a in range(n):
        src, out = ins[a], outs[a]

        def copy(k, block, to, from_input=False, a=a, src=src, out=out):
            slot = out.at[_flat(*block)]
            return pltpu.make_async_remote_copy(
                src_ref=src if from_input else slot, dst_ref=slot,
                send_sem=send_sems.at[7 * a + k], recv_sem=recv_sems.at[7 * a + k],
                device_id=to, device_id_type=MESH_ID)

        mine = pltpu.make_async_copy(src, out.at[_flat(*me)], local_sems.at[a])
        mine.start()
        first = [copy(0, me, sibling, True)] + [copy(1 + j, me, (*chip, c), True) for j, chip in enumerate(chips)]
        for cp in first:
            cp.start()
        pending.append((copy, mine, first))
    for copy, mine, first in pending:
        passed = [copy(4 + j, (*chip, c), sibling) for j, chip in enumerate(chips)]
        for j, chip in enumerate(chips):
            copy(1 + j, (*chip, c), me).wait_recv()
            passed[j].start()
        copy(0, sibling, me).wait_recv()
        for j, chip in enumerate(chips):
            copy(4 + j, (*chip, 1 - c), me).wait_recv()
        for cp in first + passed:
            cp.wait_send()
        mine.wait()


def all_gather(shards, *, name):
    n = len(shards)
    return pl.pallas_call(
        functools.partial(_all_gather_body, n=n), name=name,
        in_specs=[HBM_SPEC] * n, out_specs=[HBM_SPEC] * n,
        out_shape=[jax.ShapeDtypeStruct((N_DEV,) + s.shape, s.dtype) for s in shards],
        scratch_shapes=[pltpu.SemaphoreType.DMA((7 * n,)), pltpu.SemaphoreType.DMA((7 * n,)),
                        pltpu.SemaphoreType.DMA((n,))],
    )(*shards)


PEER_FLIPS = [(0, 0, 1), (1, 0, 0), (0, 1, 0), (1, 1, 0), (1, 0, 1), (0, 1, 1), (1, 1, 1)]


def _all_to_all_body(*refs, n):
    ins, outs = refs[:n], refs[n:2 * n]
    send_sems, recv_sems, local_sems = refs[2 * n:]
    x, y, c = _my_place()
    me = _flat(x, y, c)
    waits = []
    for a in range(n):
        src, out = ins[a], outs[a]
        mine = pltpu.make_async_copy(src.at[me], out.at[me], local_sems.at[a])
        mine.start()
        waits.append(mine)
        for k, (fx, fy, fc) in enumerate(PEER_FLIPS):
            peer = (1 - x if fx else x, 1 - y if fy else y, 1 - c if fc else c)
            theirs = _flat(*peer)
            sems = dict(send_sem=send_sems.at[7 * a + k], recv_sem=recv_sems.at[7 * a + k],
                        device_id=peer, device_id_type=MESH_ID)
            send = pltpu.make_async_remote_copy(src_ref=src.at[theirs], dst_ref=out.at[me], **sems)
            send.start()
            recv = pltpu.make_async_remote_copy(src_ref=src.at[theirs], dst_ref=out.at[theirs], **sems)
            waits.append((send, recv))
    for w in waits:
        if isinstance(w, tuple):
            w[0].wait_send()
            w[1].wait_recv()
        else:
            w.wait()


def all_to_all(parts, *, name):
    n = len(parts)
    return pl.pallas_call(
        functools.partial(_all_to_all_body, n=n), name=name,
        in_specs=[HBM_SPEC] * n, out_specs=[HBM_SPEC] * n,
        out_shape=[jax.ShapeDtypeStruct(p.shape, p.dtype) for p in parts],
        scratch_shapes=[pltpu.SemaphoreType.DMA((7 * n,)), pltpu.SemaphoreType.DMA((7 * n,)),
                        pltpu.SemaphoreType.DMA((n,))],
    )(*parts)


HBM_ONLY = pl.BlockSpec(memory_space=pltpu.HBM)
SEM_SPEC = pl.BlockSpec(memory_space=pltpu.SEMAPHORE)
DATAFLOW_EFFECT = pltpu.SideEffectType.DATAFLOW_SIDE_EFFECTING


def _split_copies(srcs, lands, send_sems, recv_sems, n, scatter):
    x, y, c = _my_place()
    me = _flat(x, y, c)
    pairs = []
    for a in range(n):
        for k, (fx, fy, fc) in enumerate(PEER_FLIPS):
            peer = (1 - x if fx else x, 1 - y if fy else y, 1 - c if fc else c)
            theirs = _flat(*peer)
            src = srcs[a].at[theirs] if scatter else srcs[a]
            sems = dict(send_sem=send_sems.at[7 * a + k], recv_sem=recv_sems.at[7 * a + k],
                        device_id=peer, device_id_type=MESH_ID)
            pairs.append((pltpu.make_async_remote_copy(src_ref=src, dst_ref=lands[a].at[me], **sems),
                          pltpu.make_async_remote_copy(src_ref=src, dst_ref=lands[a].at[theirs], **sems)))
    return pairs


def _exchange_start_body(*refs, n, scatter):
    srcs, lands = refs[:n], refs[n:2 * n]
    send_sems, recv_sems = refs[2 * n], refs[2 * n + 1]
    token = refs[-1]
    for send, _ in _split_copies(srcs, lands, send_sems, recv_sems, n, scatter):
        send.start()
    token[...] = jnp.zeros_like(token)


def exchange_start(srcs, lands, *, scatter, name):
    n = len(srcs)
    args = [pltpu.with_memory_space_constraint(t, pltpu.HBM) for t in list(srcs) + list(lands)]
    outs = pl.pallas_call(
        functools.partial(_exchange_start_body, n=n, scatter=scatter), name=name,
        out_shape=(pltpu.SemaphoreType.DMA((7 * n,)), pltpu.SemaphoreType.DMA((7 * n,)),
                   *[pltpu.HBM(t.shape, t.dtype) for t in args],
                   jax.ShapeDtypeStruct((8, 128), F32)),
        in_specs=[HBM_ONLY] * (2 * n),
        out_specs=(SEM_SPEC, SEM_SPEC, *[HBM_ONLY] * (2 * n), pl.BlockSpec(memory_space=pltpu.VMEM)),
        input_output_aliases={i: 2 + i for i in range(2 * n)},
        compiler_params=pltpu.CompilerParams(has_side_effects=DATAFLOW_EFFECT),
    )(*args)
    return outs[0], outs[1], outs[2:2 + n], outs[2 + n:2 + 2 * n], outs[-1]


def _exchange_wait_body(*refs, n, scatter):
    srcs, lands = refs[:n], refs[n:2 * n]
    send_sems, recv_sems = refs[2 * n], refs[2 * n + 1]
    for send, recv in _split_copies(srcs, lands, send_sems, recv_sems, n, scatter):
        send.wait_send()
        recv.wait_recv()


def exchange_wait(send_sems, recv_sems, srcs, lands, after, *, scatter, name):
    n = len(srcs)
    outs = pl.pallas_call(
        functools.partial(_exchange_wait_body, n=n, scatter=scatter), name=name,
        out_shape=tuple(pltpu.HBM(t.shape, t.dtype) for t in list(srcs) + list(lands)),
        in_specs=[HBM_ONLY] * (2 * n) + [SEM_SPEC, SEM_SPEC, HBM_SPEC],
        out_specs=tuple([HBM_ONLY] * (2 * n)),
        input_output_aliases={i: i for i in range(2 * n)},
        compiler_params=pltpu.CompilerParams(has_side_effects=DATAFLOW_EFFECT),
    )(*srcs, *lands, send_sems, recv_sems, after)
    return outs[n:]


def _own_slot_only(shape_dtype, own, slot):
    land = lax.empty(shape_dtype.shape, shape_dtype.dtype)
    return lax.dynamic_update_slice(land, own[None], (slot,) + (0,) * own.ndim)


def _pad_rows(a, rows=8):
    return jnp.pad(a, ((0, rows - a.shape[0]), (0, 0)))


def _gate_rows(a_log, dt_bias):
    z = jnp.zeros((8, 128), F32)
    return z.at[0, A_HEADS:2 * A_HEADS].set(a_log[0]).at[1, A_HEADS:2 * A_HEADS].set(dt_bias[0])


def _pack_small(norm_g, a_log, a_dt_bias, a_norm_g, b_q_norm_g, b_k_norm_g):
    return jnp.concatenate([
        norm_g[0].reshape(8, 128), norm_g[1].reshape(8, 128),
        _gate_rows(a_log, a_dt_bias),
        _pad_rows(a_norm_g[0].reshape(2, 128)),
        _pad_rows(jnp.concatenate([b_q_norm_g[0], b_k_norm_g[0]], axis=0)),
    ], axis=0)


def _unpack_small(p):
    return (p[0:16].reshape(2, D_MODEL), p[16:17, A_HEADS:2 * A_HEADS], p[17:18, A_HEADS:2 * A_HEADS],
            p[24:26].reshape(1, A_DV), p[32:35][None], p[35:38][None])


def kernel(x, positions, norm_g, a_w_in, a_conv_w, a_log, a_dt_bias, a_norm_g, a_w_out, b_w_in, b_q_norm_g, b_k_norm_g, b_w_out, loss_target, m_norm_g, m_a_w_in, m_a_conv_w, m_a_log, m_a_dt_bias, m_a_norm_g, m_a_w_out, m_b_w_in, m_b_q_norm_g, m_b_k_norm_g, m_b_w_out, v_norm_g, v_a_w_in, v_a_conv_w, v_a_log, v_a_dt_bias, v_a_norm_g, v_a_w_out, v_b_w_in, v_b_q_norm_g, v_b_k_norm_g, v_b_w_out):
    n_seq, s_len, d = x.shape
    t_rows = n_seq * s_len
    n_chunks = s_len // CHUNK
    x0 = x.reshape(t_rows, d)
    target = loss_target.reshape(t_rows, d)
    my_slot = _flat(*_my_place())

    g_a_in, g_conv = all_gather([a_w_in[0].astype(BF16), _pad_rows(a_conv_w[0])], name="gather_weights_first")
    later = [a_w_out[0].astype(BF16), b_w_in[0].astype(BF16), b_w_out[0].astype(BF16)]
    lands = [_own_slot_only(jax.ShapeDtypeStruct((N_DEV,) + t.shape, t.dtype), t, my_slot) for t in later]
    w_send, w_recv, later, lands, w_token = exchange_start(later, lands, scatter=False, name="gather_weights_start")
    w_a_in = jnp.pad(g_a_in.transpose(1, 0, 2).reshape(d, A_IN), ((0, 0), (0, A_IN_PAD - A_IN)))
    conv_w8 = g_conv.transpose(1, 0, 2).reshape(8, 2 * A_QK + A_VW)

    gains_model = _pad_rows(norm_g)
    gate_prm = _gate_rows(a_log, a_dt_bias)
    gain_a_out = _pad_rows(a_norm_g)
    gains_qk = _pad_rows(jnp.concatenate([b_q_norm_g[0], b_k_norm_g[0]], axis=0))
    cos_t, sin_t = rope_tables(positions.reshape(t_rows))

    h0, h0_t = rms_fwd(x0, gains_model, 0, after=w_token)
    proj_a = mm_nn(h0, w_a_in, tn=896, name="proj_a", a_resident=True)
    gates_col, gates_row = a_gates_fwd(proj_a, gate_prm, n_seq)
    gates_row = gates_row.reshape(n_seq, 2 * A_HEADS, s_len // SUPER, 1, SUPER)
    o_a, tinv, states, og_a, q_a, k_a, v_a = gdn_fwd(proj_a, conv_w8, gates_col, gates_row, gain_a_out, n_seq)
    g_a_out, g_b_in, g_b_out = exchange_wait(w_send, w_recv, later, lands, og_a, scatter=False,
                                             name="gather_weights_wait")
    w_a_out = g_a_out.reshape(A_VW, d)
    w_b_in = g_b_in.transpose(1, 0, 2).reshape(d, B_IN_COLS)
    w_b_out = g_b_out.reshape(B_W, d)
    x1 = mm_nn(og_a, w_a_out, x0, tn=1024, name="out_a")

    h1, h1_t = rms_fwd(x1, gains_model, 1)
    proj_b = mm_nn_pieces(h1, w_b_in, name="proj_b")
    og_b, o_b, lse = attn_fwd(proj_b, cos_t, sin_t, gains_qk, n_seq)
    y = mm_nn(og_b, w_b_out, x1, tn=1024, name="out_b")

    dy, loss_blk = loss_head(y, target)

    d_og_b = mm_nt(dy, w_b_out, tn=1024, name="d_og_b")
    dw_b_out = mm_tn(og_b, dy, tn=256, out_dtype=BF16, name="dw_b_out")
    dqkv_b, dz_b, d_gains_qk = attn_bwd(proj_b, cos_t, sin_t, gains_qk, d_og_b, o_b, lse, n_seq)
    dh1 = mm_nt_pieces(dqkv_b, dz_b, w_b_in, name="dh1")
    dw_b_in = mm_tn_b_in(h1_t, dqkv_b, dz_b, out_dtype=BF16, name="dw_b_in")
    dx1, d_gain1 = rms_bwd(x1, gains_model, 1, dh1, dy)

    dw_a_out = mm_tn(og_a, dx1, tn=256, out_dtype=BF16, name="dw_a_out")
    early = [dw_b_in, dw_b_out.reshape(N_DEV, B_W // N_DEV, d), dw_a_out.reshape(N_DEV, A_VW // N_DEV, d)]
    lands = [_own_slot_only(t, lax.dynamic_index_in_dim(t, my_slot, 0, keepdims=False), my_slot) for t in early]
    g_send, g_recv, early, lands, g_token = exchange_start(early, lands, scatter=True, name="scatter_grads_start")

    d_og_a = mm_nt(dx1, w_a_out, tn=1024, name="d_og_a", after=g_token)
    d_xq, d_xk, d_xv, dgates, dz_a, d_gain_a_out, d_cq, d_ck, d_cv = gdn_bwd(
        q_a, k_a, v_a, gates_col, gates_row, tinv, states, d_og_a, o_a, proj_a, conv_w8, gain_a_out, n_seq)
    d_conv = jnp.concatenate([d_cq.sum(axis=0), d_ck.sum(axis=0), d_cv.sum(axis=0)], axis=1)
    d_gate_logits, d_gate_prm = a_gates_bwd(proj_a, gate_prm, dgates, n_seq)
    dw_a_in = jnp.concatenate([
        mm_tn(h0_t, piece, tn=min(256, piece.shape[1]), out_dtype=BF16, name=f"dw_a_in_{nm}", a_is_transposed=True)
        for nm, piece in (("q", d_xq), ("k", d_xk), ("v", d_xv), ("z", dz_a), ("gates", d_gate_logits))
    ], axis=1)[:, :A_IN]
    shard_a_in = A_IN // N_DEV
    last = [dw_a_in.reshape(d, N_DEV, shard_a_in).transpose(1, 0, 2)]
    last_lands = [_own_slot_only(t, lax.dynamic_index_in_dim(t, my_slot, 0, keepdims=False), my_slot) for t in last]
    l_send, l_recv, last, last_lands, l_token = exchange_start(last, last_lands, scatter=True,
                                                               name="scatter_last_start")
    dh0 = mm_nt_multi([d_xq, d_xk, d_xv, dz_a], w_a_in, tn=1024, name="dh0_qkvz", after=l_token)
    dh0 = mm_nt(d_gate_logits, w_a_in, dh0, tn=128, col_off=A_GATE_COL, name="dh0_gates")
    dx0, d_gain0 = rms_bwd(x0, gains_model, 0, dh0, dx1)

    small = jnp.concatenate([
        d_gain0[0].reshape(8, 128), d_gain1[0].reshape(8, 128), d_gate_prm,
        _pad_rows(d_gain_a_out[0].reshape(2, 128)), d_gains_qk, loss_blk], axis=0)
    n_small = small.shape[0] - loss_blk.shape[0]
    small_srcs = [small, d_conv]
    small_lands = [_own_slot_only(jax.ShapeDtypeStruct((N_DEV,) + t.shape, t.dtype), t, my_slot)
                   for t in small_srcs]
    s_send, s_recv, small_srcs, small_lands, s_token = exchange_start(small_srcs, small_lands, scatter=False,
                                                                      name="gather_small_start")

    r_b_in, r_b_out, r_a_out = exchange_wait(g_send, g_recv, early, lands, s_token, scatter=True,
                                             name="scatter_grads_wait")
    (r_a_in,) = exchange_wait(l_send, l_recv, last, last_lands, s_token, scatter=True, name="scatter_last_wait")

    upd = {}
    upd["a_w_in"] = adamw(r_a_in, a_w_in, m_a_w_in, v_a_w_in, name="adamw_a_w_in")
    upd["a_w_out"] = adamw(r_a_out, a_w_out, m_a_w_out, v_a_w_out, name="adamw_a_w_out")
    upd["b_w_in"] = adamw(r_b_in, b_w_in, m_b_w_in, v_b_w_in, name="adamw_b_w_in")
    upd["b_w_out"] = adamw(r_b_out, b_w_out, m_b_w_out, v_b_w_out, name="adamw_b_w_out")
    r_small, r_conv = exchange_wait(s_send, s_recv, small_srcs, small_lands, upd["b_w_in"][0], scatter=False,
                                    name="gather_small_wait")
    conv_cols = a_conv_w.shape[2]
    r_conv = lax.dynamic_slice(r_conv, (0, 0, my_slot * conv_cols), (N_DEV, 8, conv_cols))
    loss = jnp.sum(r_small[:, n_small, 0])
    r_small = r_small[:, :n_small]
    upd["a_conv_w"] = [t[:, :A_CONV] for t in adamw(
        r_conv, _pad_rows(a_conv_w[0])[None], _pad_rows(m_a_conv_w[0])[None], _pad_rows(v_a_conv_w[0])[None],
        name="adamw_a_conv_w")]
    small_upd = adamw(
        r_small,
        _pack_small(norm_g, a_log, a_dt_bias, a_norm_g, b_q_norm_g, b_k_norm_g)[None],
        _pack_small(m_norm_g, m_a_log, m_a_dt_bias, m_a_norm_g, m_b_q_norm_g, m_b_k_norm_g)[None],
        _pack_small(v_norm_g, v_a_log, v_a_dt_bias, v_a_norm_g, v_b_q_norm_g, v_b_k_norm_g)[None],
        name="adamw_small")
    small_names = ("norm_g", "a_log", "a_dt_bias", "a_norm_g", "b_q_norm_g", "b_k_norm_g")
    unpacked = [_unpack_small(t[0]) for t in small_upd]
    for i, nm in enumerate(small_names):
        upd[nm] = [u[i] for u in unpacked]

    order = ("norm_g", "a_w_in", "a_conv_w", "a_log", "a_dt_bias", "a_norm_g", "a_w_out",
             "b_w_in", "b_q_norm_g", "b_k_norm_g", "b_w_out")
    outs = [loss, dx0.reshape(n_seq, s_len, d)]
    for kind in range(4):
        for nm in order:
            outs.append(upd[nm][kind])
    return tuple(outs)
```

```python
import functools
import math

import jax
import jax.numpy as jnp
from jax import lax
from jax.experimental import pallas as pl
from jax.experimental.pallas import tpu as pltpu

F32 = jnp.float32
BF16 = jnp.bfloat16

D_MODEL = 1024
EPS = 1e-6
N_DEV = 8

A_HEADS = 8
A_DK = 128
A_DV = 256
A_QK = A_HEADS * A_DK
A_VW = A_HEADS * A_DV
A_CONV = 4
CHUNK = 64
A_IN = 2 * A_QK + 2 * A_VW + 2 * A_HEADS
A_IN_PAD = 2 * A_QK + 2 * A_VW + 128
A_GATE_COL = (2 * A_QK + 2 * A_VW) // 128

B_DILATIONS = (1, 4, 16)
B_GROUPS = 3
B_HEADS = 8
B_DH = 128
B_W = B_HEADS * B_DH
B_BLOCK = 128
B_PIECES = 3 * B_GROUPS + 1
ROPE_THETA = 500000.0
ROPE_DIMS = B_DH // 4
ROPE_HALF = ROPE_DIMS // 2

ADAM_LR = 0.001
ADAM_B1 = 0.9
ADAM_B2 = 0.999
ADAM_EPS = 1e-08
ADAM_WD = 0.01
ADAM_STEP = 10

VMEM_LIMIT = 60 * 1024 * 1024


def _params(sem):
    return pltpu.CompilerParams(dimension_semantics=sem, vmem_limit_bytes=VMEM_LIMIT)


def _bf(x):
    return x.astype(BF16)


def _mm(a, b):
    return jnp.dot(_bf(a), _bf(b), preferred_element_type=F32)


def _mm_nt(a, b):
    return lax.dot_general(_bf(a), _bf(b), (((1,), (1,)), ((), ())), preferred_element_type=F32)


def _mm_tn(a, b):
    return lax.dot_general(_bf(a), _bf(b), (((0,), (0,)), ((), ())), preferred_element_type=F32)


def _split(x):
    hi = _bf(x)
    return hi, _bf(x - hi.astype(F32))


def _mm3(a, b):
    ah, al = _split(a)
    bh, bl = _split(b)
    d = functools.partial(jnp.dot, preferred_element_type=F32)
    return d(ah, bh) + (d(ah, bl) + d(al, bh))


def _colsum_as_col(z):
    zh, zl = _split(z)
    ones = jnp.ones((z.shape[0], 128), BF16)
    tn = functools.partial(lax.dot_general, dimension_numbers=(((0,), (0,)), ((), ())),
                           preferred_element_type=F32)
    return (tn(zh, ones) + tn(zl, ones))[:, 0:1]


def _sigmoid(x):
    return 0.5 * jnp.tanh(0.5 * x) + 0.5


INV_BASE = 8
INV_NEWTON = 2
GDN_GROUP = 4
SUPER = GDN_GROUP * CHUNK
GDN_WIDTH = 2

A_K_COL = A_QK // A_DK
A_V_COL = 2 * A_QK // A_DV


def _inverse_steps(m, row, col):
    eye = (row == col).astype(F32)
    d = jnp.where(row // INV_BASE == col // INV_BASE, m, 0.0)
    x = eye - d
    p = _mm(d, d)
    yield
    steps = int(math.log2(INV_BASE)) - 1
    for i in range(steps):
        x = x + _mm(x, p)
        if i + 1 < steps:
            p = _mm(p, p)
        yield
    size = INV_BASE
    while size < CHUNK:
        c = jnp.where((row // (2 * size) == col // (2 * size)) & (row // size != col // size), m, 0.0)
        xc = _mm(x, c)
        yield
        x = x - _mm(xc, x)
        yield
        size *= 2
    for _ in range(INV_NEWTON):
        r = eye - x - _mm3(m, x)
        yield
        x = x + _mm(x, r)
        yield
    return x


def _drain(gen):
    while True:
        try:
            next(gen)
        except StopIteration as stop:
            return stop.value


def _interleave(*gens):
    live = list(gens)
    while live:
        for g in list(live):
            try:
                next(g)
            except StopIteration:
                live.remove(g)


def _diag_blocks_tall(x):
    return jnp.concatenate([x[i * CHUNK:(i + 1) * CHUNK, i * CHUNK:(i + 1) * CHUNK] for i in range(GDN_GROUP)], axis=0)


def _tall_to_block_diag(t, same):
    return jnp.where(same, jnp.concatenate([t] * GDN_GROUP, axis=1), 0.0)


def _block_sum(same, x):
    xh, xl = _split(jnp.broadcast_to(x, (SUPER, 128)))
    ones = same.astype(BF16)
    d = functools.partial(jnp.dot, preferred_element_type=F32)
    return (d(ones, xh) + d(ones, xl))[:, 0:1]


def _aligned_rows(index, size):
    start = index * size
    return pl.ds(start if isinstance(start, int) else pl.multiple_of(start, size), size)


def _super_rows(i):
    return _aligned_rows(i, SUPER)


def _chunk_rows(n):
    return _aligned_rows(n, CHUNK)


def _gdn_super_steps(q, k, v, gcb, gr, head, tinv_tall=None, need_m=True):
    lane = lax.broadcasted_iota(jnp.int32, (SUPER, 128), 1)
    row = lax.broadcasted_iota(jnp.int32, (SUPER, SUPER), 0)
    col = lax.broadcasted_iota(jnp.int32, (SUPER, SUPER), 1)
    same = row // CHUNK == col // CHUNK
    beta = jnp.sum(jnp.where(lane == head, gcb, 0.0), axis=1, keepdims=True)
    gc = jnp.sum(jnp.where(lane == A_HEADS + head, gcb, 0.0), axis=1, keepdims=True)
    g_last = jnp.sum(jnp.where(col == (row // CHUNK) * CHUNK + (CHUNK - 1), gr, 0.0), axis=1, keepdims=True)
    gamma = jnp.exp(gc)
    decay = jnp.where(same & (row >= col), jnp.exp(jnp.minimum(gc - gr, 0.0)), 0.0)
    kb = k * beta
    m = jnp.where(same & (row > col), _mm_nt(kb, k) * decay, 0.0) if need_m else None
    p = jnp.where(same & (row >= col), _mm_nt(q, k) * decay, 0.0)
    yield
    if tinv_tall is None:
        tinv = yield from _inverse_steps(m, row, col)
    else:
        tinv = _tall_to_block_diag(tinv_tall, same)
    u = _mm(tinv, v * beta)
    w = _mm(tinv, kb * gamma)
    yield
    e_tail = jnp.exp(g_last - gc)
    return dict(beta=beta, gc=gc, g_last=g_last, gamma=gamma, decay=decay, kb=kb, m=m,
                tinv=tinv, u=u, w=w, p=p, e_tail=e_tail, row=row, col=col, lane=lane, same=same)


def _store_scan_operands(rows, q, k, t, u_scr, w_scr, p_scr, qg_scr, ke_scr, gl_scr):
    u_scr[rows, :] = t["u"]
    w_scr[rows, :] = _bf(t["w"])
    p_scr[rows, :] = _bf(_diag_blocks_tall(t["p"]))
    qg_scr[rows, :] = _bf(q * t["gamma"])
    ke_scr[rows, :] = _bf(k * t["e_tail"])
    gl_scr[rows, :] = jnp.broadcast_to(jnp.exp(t["g_last"]), (SUPER, 128))


def _gdn_fwd_body(q_ref, k_ref, v_ref, gc_ref, gr_ref, wq_ref, wk_ref, wv_ref, z_ref, gn_ref,
                  o_ref, tinv_ref, st_ref, og_ref, qo_ref, ko_ref, vo_ref, s_scr, *sets):
    head = pl.program_id(1)
    n_super = q_ref.shape[0] // SUPER
    all_sets = [sets[6 * i:6 * i + 6] for i in range(2 * GDN_WIDTH)]
    whole = pl.ds(0, SUPER)

    def conv_silu(x_ref, w_ref, i):
        rows = _super_rows(i)
        x, w = x_ref[rows, :], w_ref[...]
        halo = jnp.zeros((8, x.shape[1]), F32) if i == 0 else x_ref[pl.ds(i * SUPER - 8, 8), :]
        ext = jnp.concatenate([halo, x], axis=0)
        c = x * w[A_CONV - 1:A_CONV, :]
        for j in range(1, A_CONV):
            c = c + pltpu.roll(ext, j, 0)[8:, :] * w[A_CONV - 1 - j:A_CONV - j, :]
        return c * _sigmoid(c)

    def unit(a):
        return a * lax.rsqrt(jnp.sum(a * a, axis=1, keepdims=True) + EPS)

    def prepare_steps(i, dst):
        rows = _super_rows(i)
        q = unit(conv_silu(q_ref, wq_ref, i)) * A_DK ** -0.5
        k = unit(conv_silu(k_ref, wk_ref, i))
        v = conv_silu(v_ref, wv_ref, i)
        qo_ref[rows, :], ko_ref[rows, :], vo_ref[rows, :] = q, k, v
        t = yield from _gdn_super_steps(q, k, v, gc_ref[rows, :], gr_ref[i], head)
        tinv_ref[rows, :] = _diag_blocks_tall(t["tinv"])
        _store_scan_operands(whole, q, k, t, *dst)

    def scan_steps(i, src):
        u_scr, w_scr, p_scr, qg_scr, ke_scr, gl_scr = src
        for j in range(GDN_GROUP):
            n = i * GDN_GROUP + j
            local = pl.ds(j * CHUNK, CHUNK)
            s = s_scr[...]
            sb = _bf(s)
            st_ref[n] = sb
            ws = jnp.dot(w_scr[local, :], sb, preferred_element_type=F32)
            yield
            vb = _bf(u_scr[local, :] - ws)
            o = (jnp.dot(qg_scr[local, :], sb, preferred_element_type=F32)
                 + jnp.dot(p_scr[local, :], vb, preferred_element_type=F32))
            s_new = s * gl_scr[local, :][0:1, 0:1] + lax.dot_general(
                ke_scr[local, :], vb, (((0,), (0,)), ((), ())), preferred_element_type=F32)
            yield
            rows = _chunk_rows(n)
            o_ref[rows, :] = o
            s_scr[...] = s_new
            silu, _ = _silu_parts(z_ref[rows, :])
            r = lax.rsqrt(jnp.mean(o * o, axis=1, keepdims=True) + EPS)
            og_ref[rows, :] = ((o * r * gn_ref[0:1, :]) * silu).astype(og_ref.dtype)

    def scan_many(first, srcs):
        for j, src in enumerate(srcs):
            yield from scan_steps(first + j, src)

    groups = [all_sets[:GDN_WIDTH], all_sets[GDN_WIDTH:]]
    _interleave(*[prepare_steps(j, groups[0][j]) for j in range(GDN_WIDTH)])
    s_scr[...] = jnp.zeros_like(s_scr)
    for g in range(n_super // GDN_WIDTH):
        cur, nxt = groups[g % 2], groups[(g + 1) % 2]
        first = g * GDN_WIDTH
        following = [prepare_steps(first + GDN_WIDTH + j, nxt[j]) for j in range(GDN_WIDTH)
                     if first + GDN_WIDTH + j < n_super]
        _interleave(scan_many(first, cur), *following)


def _gdn_in_specs(s_len, n_super, from_proj):
    k_col, v_col = (A_K_COL, A_V_COL) if from_proj else (0, 0)
    return [
        pl.BlockSpec((s_len, A_DK), lambda b, h: (b, h)),
        pl.BlockSpec((s_len, A_DK), lambda b, h: (b, k_col + h)),
        pl.BlockSpec((s_len, A_DV), lambda b, h: (b, v_col + h)),
        pl.BlockSpec((s_len, 128), lambda b, h: (b, 0)),
        pl.BlockSpec((None, None, n_super, 1, SUPER), lambda b, h: (b, A_HEADS + h, 0, 0, 0)),
    ]


def _gdn_scan_scratch(s_len):
    return [pltpu.VMEM((A_DK, A_DV), F32), pltpu.VMEM((s_len, A_DV), F32),
            pltpu.VMEM((s_len, A_DK), BF16), pltpu.VMEM((s_len, CHUNK), BF16),
            pltpu.VMEM((s_len, A_DK), BF16), pltpu.VMEM((s_len, A_DK), BF16),
            pltpu.VMEM((s_len, 128), F32)]


def gdn_fwd(proj_a, conv_w8, gates_col, gates_row, norm_g8, n_seq):
    t_rows = proj_a.shape[0]
    s_len = t_rows // n_seq
    n_chunks = s_len // CHUNK
    qk_spec = pl.BlockSpec((s_len, A_DK), lambda b, h: (b, h))
    v_spec = pl.BlockSpec((s_len, A_DV), lambda b, h: (b, h))
    return pl.pallas_call(
        _gdn_fwd_body, name="gdn_fwd", grid=(n_seq, A_HEADS),
        in_specs=_gdn_in_specs(s_len, s_len // SUPER, True) + [
            pl.BlockSpec((8, A_DK), lambda b, h: (0, h)),
            pl.BlockSpec((8, A_DK), lambda b, h: (0, A_K_COL + h)),
            pl.BlockSpec((8, A_DV), lambda b, h: (0, A_V_COL + h)),
            pl.BlockSpec((s_len, A_DV), lambda b, h: (b, A_Z_COL + h)),
            pl.BlockSpec((8, A_DV), lambda b, h: (0, 0)),
        ],
        out_specs=[
            v_spec,
            pl.BlockSpec((s_len, CHUNK), lambda b, h: (b * A_HEADS + h, 0)),
            pl.BlockSpec((None, n_chunks, A_DK, A_DV), lambda b, h: (b * A_HEADS + h, 0, 0, 0)),
            v_spec, qk_spec, qk_spec, v_spec,
        ],
        out_shape=[
            jax.ShapeDtypeStruct((t_rows, A_VW), F32),
            jax.ShapeDtypeStruct((n_seq * A_HEADS * s_len, CHUNK), F32),
            jax.ShapeDtypeStruct((n_seq * A_HEADS, n_chunks, A_DK, A_DV), BF16),
            jax.ShapeDtypeStruct((t_rows, A_VW), BF16),
            jax.ShapeDtypeStruct((t_rows, A_QK), F32),
            jax.ShapeDtypeStruct((t_rows, A_QK), F32),
            jax.ShapeDtypeStruct((t_rows, A_VW), F32),
        ],
        scratch_shapes=_gdn_scan_scratch(SUPER) + (2 * GDN_WIDTH - 1) * _gdn_scan_scratch(SUPER)[1:],
        compiler_params=_params(("arbitrary", "arbitrary")),
    )(proj_a, proj_a, proj_a, gates_col, gates_row, conv_w8, conv_w8, conv_w8, proj_a, norm_g8)


def _gdn_bwd_body(q_ref, k_ref, v_ref, gc_ref, gr_ref, tinv_ref, st_ref, dog_ref, oa_ref, z_ref, gn_ref,
                  xq_ref, xk_ref, xv_ref, wq_ref, wk_ref, wv_ref,
                  dxq_ref, dxk_ref, dxv_ref, dgc_ref, dz_ref, dgn_ref, dwq_ref, dwk_ref, dwv_ref,
                  ds_scr, cq_scr, ck_scr, cv_scr, *sets):
    head = pl.program_id(1)
    n_super = q_ref.shape[0] // SUPER
    ops = (sets[0:7], sets[7:14])
    res = (sets[14:21], sets[21:28])
    whole = pl.ds(0, SUPER)
    tn = functools.partial(lax.dot_general, dimension_numbers=(((0,), (0,)), ((), ())), preferred_element_type=F32)
    nt = functools.partial(lax.dot_general, dimension_numbers=(((1,), (1,)), ((), ())), preferred_element_type=F32)

    @pl.when(head == 0)
    def _():
        dgc_ref[...] = jnp.zeros_like(dgc_ref)

    @pl.when((head == 0) & (pl.program_id(0) == 0))
    def _():
        dgn_ref[...] = jnp.zeros_like(dgn_ref)

    carry = (cq_scr, ck_scr, cv_scr)
    for ref in carry + (dwq_ref, dwk_ref, dwv_ref):
        ref[...] = jnp.zeros_like(ref)

    def common_steps(i, need_m=True):
        rows = _super_rows(i)
        q, k, v = q_ref[rows, :], k_ref[rows, :], v_ref[rows, :]
        t = yield from _gdn_super_steps(q, k, v, gc_ref[rows, :], gr_ref[i], head, tinv_tall=tinv_ref[rows, :],
                                        need_m=need_m)
        return rows, q, k, v, t

    def stage_p(i, parity):
        rows, q, k, _, t = yield from common_steps(i, need_m=False)
        _store_scan_operands(whole, q, k, t, *ops[parity][:6])
        o, d_og, gain = oa_ref[rows, :], dog_ref[rows, :], gn_ref[0:1, :]
        r = lax.rsqrt(jnp.mean(o * o, axis=1, keepdims=True) + EPS)
        silu, dsilu = _silu_parts(z_ref[rows, :])
        xr = o * r
        d_on = d_og * silu
        dz_ref[rows, :] = (d_og * (xr * gain) * dsilu).astype(dz_ref.dtype)
        u = d_on * gain
        ops[parity][6][...] = r * u - xr * (r * r) * jnp.mean(u * o, axis=1, keepdims=True)
        dgn_ref[0:1, :] += jnp.sum(d_on * xr, axis=0, keepdims=True)

    def stage_s(i, parity):
        u_scr, w_scr, p_scr, qg_scr, ke_scr, gl_scr, do_scr = ops[parity]
        vn_scr, dvn_scr, dqg_scr, dw_scr, dkt_scr, sds_scr, dof_scr = res[parity]
        for j in reversed(range(GDN_GROUP)):
            n = i * GDN_GROUP + j
            local = pl.ds(j * CHUNK, CHUNK)
            ds_next = ds_scr[...]
            dsb = _bf(ds_next)
            sb = st_ref[n]
            s = sb.astype(F32)
            d_o = do_scr[local, :]
            dof_scr[local, :] = d_o
            d_ob = _bf(d_o)
            w_s = jnp.dot(w_scr[local, :], sb, preferred_element_type=F32)
            d_vn = tn(p_scr[local, :], d_ob) + jnp.dot(ke_scr[local, :], dsb, preferred_element_type=F32)
            d_qg = nt(d_ob, sb)
            qg_do = tn(qg_scr[local, :], d_ob)
            yield
            v_new = u_scr[local, :] - w_s
            d_vnb = _bf(d_vn)
            d_w = -nt(d_vnb, sb)
            d_kt = nt(_bf(v_new), dsb)
            w_dvn = tn(w_scr[local, :], d_vnb)
            yield
            vn_scr[local, :] = v_new
            dvn_scr[local, :] = d_vn
            dqg_scr[local, :] = d_qg
            dw_scr[local, :] = d_w
            dkt_scr[local, :] = d_kt
            sds = jnp.sum(jnp.sum(s * ds_next, axis=1, keepdims=True), axis=0, keepdims=True)
            sds_scr[local, :] = jnp.broadcast_to(sds, (CHUNK, 128))
            ds_scr[...] = qg_do + gl_scr[local, :][0:1, 0:1] * ds_next - w_dvn

    def conv_bwd(i, rows, x_ref, w_ref, dy, norm_scale, dx_ref, dw_ref, dc_above):
        x, w = x_ref[rows, :], w_ref[...]
        above = x_ref[pl.ds(pl.multiple_of(jnp.maximum(i * SUPER - 8, 0), 8), 8), :]
        ext = jnp.concatenate([jnp.where(i > 0, above, 0.0), x], axis=0)
        c = x * w[A_CONV - 1:A_CONV, :]
        for j in range(1, A_CONV):
            c = c + pltpu.roll(ext, j, 0)[8:, :] * w[A_CONV - 1 - j:A_CONV - j, :]
        sig = _sigmoid(c)
        a = c * sig
        if norm_scale is None:
            da = dy
        else:
            rn = lax.rsqrt(jnp.sum(a * a, axis=1, keepdims=True) + EPS)
            da = norm_scale * (rn * dy - a * (rn * rn * rn) * jnp.sum(dy * a, axis=1, keepdims=True))
        dc = da * (sig * (1.0 + c * (1.0 - sig)))
        ext_dc = jnp.concatenate([dc, dc_above[...]], axis=0)
        dc_above[...] = dc[0:8, :]
        dx = dc * w[A_CONV - 1:A_CONV, :]
        dw_ref[A_CONV - 1:A_CONV, :] += jnp.sum(dc * x, axis=0, keepdims=True)
        for j in range(1, A_CONV):
            dcs = pltpu.roll(ext_dc, SUPER + 8 - j, 0)[:SUPER, :]
            dx = dx + dcs * w[A_CONV - 1 - j:A_CONV - j, :]
            dw_ref[A_CONV - 1 - j:A_CONV - j, :] += jnp.sum(dcs * x, axis=0, keepdims=True)
        dx_ref[rows, :] = dx.astype(dx_ref.dtype)

    def stage_f(i, parity):
        vn_scr, dvn_scr, dqg_scr, dw_scr, dkt_scr, sds_scr, dof_scr = res[parity]
        rows, q, k, v, t = yield from common_steps(i)
        beta, gamma, decay, kb, e_tail = t["beta"], t["gamma"], t["decay"], t["kb"], t["e_tail"]
        row, col, lane, same = t["row"], t["col"], t["lane"], t["same"]
        d_o = dof_scr[...]
        v_new, d_vn = vn_scr[...], dvn_scr[...]
        d_qg, d_w, d_kt = dqg_scr[...], dw_scr[...], dkt_scr[...]
        gamma_last = jnp.exp(t["g_last"])

        d_p = jnp.where(same & (row >= col), _mm_nt(d_o, v_new), 0.0)
        d_ru = _mm_tn(t["tinv"], d_vn)
        d_rw = _mm_tn(t["tinv"], d_w)
        yield
        d_m = jnp.where(same & (row > col), -(_mm_nt(d_ru, t["u"]) + _mm_nt(d_rw, t["w"])), 0.0)
        yield

        x_p = d_p * decay
        y_m = d_m * decay
        d_kb = _mm(y_m, k) + d_rw * gamma
        d_q = _mm(x_p, k) + d_qg * gamma
        d_k = _mm_tn(x_p, q) + _mm_tn(y_m, kb) + d_kb * beta + d_kt * e_tail
        d_v = d_ru * beta
        conv_bwd(i, rows, xq_ref, wq_ref, d_q, A_DK ** -0.5, dxq_ref, dwq_ref, carry[0])
        conv_bwd(i, rows, xk_ref, wk_ref, d_k, 1.0, dxk_ref, dwk_ref, carry[1])
        conv_bwd(i, rows, xv_ref, wv_ref, d_v, None, dxv_ref, dwv_ref, carry[2])

        d_beta = (jnp.sum(d_ru * v, axis=1, keepdims=True)
                  + jnp.sum(d_kb * k, axis=1, keepdims=True))
        z = d_p * t["p"] + d_m * t["m"]
        eps_tail = jnp.sum(d_kt * k, axis=1, keepdims=True) * e_tail
        d_gc = (jnp.sum(z, axis=1, keepdims=True) - _colsum_as_col(z)
                + jnp.sum(d_qg * q, axis=1, keepdims=True) * gamma
                + jnp.sum(d_rw * kb, axis=1, keepdims=True) * gamma
                - eps_tail)
        d_glast = _block_sum(same, eps_tail) + gamma_last * sds_scr[...][:, 0:1]
        yield
        rcol = lax.broadcasted_iota(jnp.int32, (SUPER, 1), 0)
        d_gc = d_gc + jnp.where(rcol % CHUNK == CHUNK - 1, d_glast, 0.0)
        dgc_ref[rows, :] += (jnp.where(lane == head, d_beta, 0.0)
                             + jnp.where(lane == A_HEADS + head, d_gc, 0.0))

    last = n_super - 1
    _drain(stage_p(last, 1))
    ds_scr[...] = jnp.zeros_like(ds_scr)
    _interleave(stage_s(last, 1), stage_p(last - 1, 0))

    def pair(k, carry):
        i = last - 1 - 2 * k
        _interleave(stage_s(i, 0), stage_f(i + 1, 1), stage_p(i - 1, 1))
        _interleave(stage_s(i - 1, 1), stage_f(i, 0), stage_p(i - 2, 0))
        return carry

    lax.fori_loop(0, n_super // 2 - 1, pair, 0)
    _interleave(stage_s(0, 0), stage_f(1, 1))
    _drain(stage_f(0, 0))


def gdn_bwd(q, k, v, gates_col, gates_row, tinv, states, d_og, o, proj_a, conv_w8, norm_g8, n_seq):
    t_rows = q.shape[0]
    s_len = t_rows // n_seq
    n_chunks = s_len // CHUNK
    qk_spec = pl.BlockSpec((s_len, A_DK), lambda b, h: (b, h))
    v_spec = pl.BlockSpec((s_len, A_DV), lambda b, h: (b, h))
    gate_spec = pl.BlockSpec((s_len, 128), lambda b, h: (b, 0))
    gain_spec = pl.BlockSpec((8, A_DV), lambda b, h: (0, 0))
    dw_qk_spec = pl.BlockSpec((None, 8, A_DK), lambda b, h: (b, 0, h))
    dw_v_spec = pl.BlockSpec((None, 8, A_DV), lambda b, h: (b, 0, h))
    ops_set = _gdn_scan_scratch(SUPER)[1:] + [pltpu.VMEM((SUPER, A_DV), F32)]
    res_set = [pltpu.VMEM((SUPER, A_DV), F32), pltpu.VMEM((SUPER, A_DV), F32),
               pltpu.VMEM((SUPER, A_DK), F32), pltpu.VMEM((SUPER, A_DK), F32),
               pltpu.VMEM((SUPER, A_DK), F32), pltpu.VMEM((SUPER, 128), F32), pltpu.VMEM((SUPER, A_DV), F32)]
    return pl.pallas_call(
        _gdn_bwd_body, name="gdn_bwd", grid=(n_seq, A_HEADS),
        in_specs=_gdn_in_specs(s_len, s_len // SUPER, False) + [
            pl.BlockSpec((s_len, CHUNK), lambda b, h: (b * A_HEADS + h, 0)),
            pl.BlockSpec((None, n_chunks, A_DK, A_DV), lambda b, h: (b * A_HEADS + h, 0, 0, 0)),
            v_spec, v_spec,
            pl.BlockSpec((s_len, A_DV), lambda b, h: (b, A_Z_COL + h)),
            gain_spec,
            pl.BlockSpec((s_len, A_DK), lambda b, h: (b, h)),
            pl.BlockSpec((s_len, A_DK), lambda b, h: (b, A_K_COL + h)),
            pl.BlockSpec((s_len, A_DV), lambda b, h: (b, A_V_COL + h)),
            pl.BlockSpec((8, A_DK), lambda b, h: (0, h)),
            pl.BlockSpec((8, A_DK), lambda b, h: (0, A_K_COL + h)),
            pl.BlockSpec((8, A_DV), lambda b, h: (0, A_V_COL + h)),
        ],
        out_specs=[qk_spec, qk_spec, v_spec, gate_spec, v_spec, gain_spec, dw_qk_spec, dw_qk_spec, dw_v_spec],
        out_shape=[
            jax.ShapeDtypeStruct((t_rows, A_QK), BF16),
            jax.ShapeDtypeStruct((t_rows, A_QK), BF16),
            jax.ShapeDtypeStruct((t_rows, A_VW), BF16),
            jax.ShapeDtypeStruct((t_rows, 128), F32),
            jax.ShapeDtypeStruct((t_rows, A_VW), BF16),
            jax.ShapeDtypeStruct((8, A_DV), F32),
            jax.ShapeDtypeStruct((n_seq, 8, A_QK), F32),
            jax.ShapeDtypeStruct((n_seq, 8, A_QK), F32),
            jax.ShapeDtypeStruct((n_seq, 8, A_VW), F32),
        ],
        scratch_shapes=(_gdn_scan_scratch(SUPER)[:1]
                        + [pltpu.VMEM((8, A_DK), F32), pltpu.VMEM((8, A_DK), F32), pltpu.VMEM((8, A_DV), F32)]
                        + 2 * ops_set + 2 * res_set),
        compiler_params=_params(("arbitrary", "arbitrary")),
    )(q, k, v, gates_col, gates_row, tinv, states, d_og, o, proj_a, norm_g8,
      proj_a, proj_a, proj_a, conv_w8, conv_w8, conv_w8)


GATE_TILE = 512


def _softplus(y):
    return jnp.maximum(y, 0.0) + jnp.log1p(jnp.exp(-jnp.abs(y)))


def _gate_values(x, prm):
    beta = _sigmoid(x)
    y = x + prm[1:2, :]
    neg_a = -jnp.exp(prm[0:1, :])
    g = neg_a * _softplus(y)
    return beta, y, neg_a, g


def _a_gates_fwd_body(x_ref, prm_ref, gc_ref, gr_ref):
    x = x_ref[...]
    tm = x.shape[0]
    beta, _, _, g = _gate_values(x, prm_ref[...])
    in_chunk = lax.broadcasted_iota(jnp.int32, (tm, 1), 0) % CHUNK
    s = 1
    while s < CHUNK:
        g = g + jnp.where(in_chunk >= s, pltpu.roll(g, s, 0), 0.0)
        s *= 2
    lane = lax.broadcasted_iota(jnp.int32, x.shape, 1)
    out = jnp.where(lane < A_HEADS, beta, jnp.where(lane < 2 * A_HEADS, g, 0.0))
    gc_ref[...] = out
    gr_ref[...] = out.T[0:2 * A_HEADS, :]


def a_gates_fwd(proj_a, prm, n_seq):
    t_rows = proj_a.shape[0]
    s_len = t_rows // n_seq
    tm = min(GATE_TILE, s_len)
    n_t = s_len // tm
    return pl.pallas_call(
        _a_gates_fwd_body, name="a_gates_fwd", grid=(n_seq, n_t),
        in_specs=[pl.BlockSpec((tm, 128), lambda b, i: (b * n_t + i, A_GATE_COL)),
                  pl.BlockSpec((8, 128), lambda b, i: (0, 0))],
        out_specs=[pl.BlockSpec((tm, 128), lambda b, i: (b * n_t + i, 0)),
                   pl.BlockSpec((None, 2 * A_HEADS, tm), lambda b, i: (b, 0, i))],
        out_shape=[jax.ShapeDtypeStruct((t_rows, 128), F32),
                   jax.ShapeDtypeStruct((n_seq, 2 * A_HEADS, s_len), F32)],
        compiler_params=_params(("arbitrary", "arbitrary")),
    )(proj_a, prm)


def _a_gates_bwd_body(x_ref, prm_ref, dgc_ref, dx_ref, dprm_ref):
    first = (pl.program_id(0) == 0) & (pl.program_id(1) == 0)
    x = x_ref[...]
    tm = x.shape[0]
    beta, y, neg_a, g = _gate_values(x, prm_ref[...])
    d = dgc_ref[...]
    in_chunk = lax.broadcasted_iota(jnp.int32, (tm, 1), 0) % CHUNK
    dg = d
    s = 1
    while s < CHUNK:
        dg = dg + jnp.where(in_chunk < CHUNK - s, pltpu.roll(dg, tm - s, 0), 0.0)
        s *= 2
    lane = lax.broadcasted_iota(jnp.int32, x.shape, 1)
    is_decay = (lane >= A_HEADS) & (lane < 2 * A_HEADS)
    d_alogit = jnp.where(is_decay, dg * neg_a * _sigmoid(y), 0.0)
    dx_ref[...] = jnp.where(lane < A_HEADS, d * beta * (1.0 - beta), d_alogit).astype(dx_ref.dtype)

    @pl.when(first)
    def _():
        dprm_ref[...] = jnp.zeros_like(dprm_ref)

    dprm_ref[0:1, :] += jnp.sum(jnp.where(is_decay, dg * g, 0.0), axis=0, keepdims=True)
    dprm_ref[1:2, :] += jnp.sum(d_alogit, axis=0, keepdims=True)


def a_gates_bwd(proj_a, prm, dgates_col, n_seq):
    t_rows = proj_a.shape[0]
    s_len = t_rows // n_seq
    tm = min(GATE_TILE, s_len)
    n_t = s_len // tm
    return pl.pallas_call(
        _a_gates_bwd_body, name="a_gates_bwd", grid=(n_seq, n_t),
        in_specs=[pl.BlockSpec((tm, 128), lambda b, i: (b * n_t + i, A_GATE_COL)),
                  pl.BlockSpec((8, 128), lambda b, i: (0, 0)),
                  pl.BlockSpec((tm, 128), lambda b, i: (b * n_t + i, 0))],
        out_specs=[pl.BlockSpec((tm, 128), lambda b, i: (b * n_t + i, 0)),
                   pl.BlockSpec((8, 128), lambda b, i: (0, 0))],
        out_shape=[jax.ShapeDtypeStruct((t_rows, 128), BF16),
                   jax.ShapeDtypeStruct((8, 128), F32)],
        compiler_params=_params(("arbitrary", "arbitrary")),
    )(proj_a, prm, dgates_col)


ROW_TILE = 1024
A_Z_COL = (2 * A_QK + A_VW) // A_DV


def _silu_parts(z):
    sig = _sigmoid(z)
    return z * sig, sig * (1.0 + z * (1.0 - sig))


NEG_BIG = -1e30
ATT_SCALE = B_DH ** -0.5


def _swap_rope_halves(x):
    src = lax.broadcasted_iota(jnp.int32, (B_DH, B_DH), 0)
    dst = lax.broadcasted_iota(jnp.int32, (B_DH, B_DH), 1)
    pick = ((dst < ROPE_HALF) & (src == dst + ROPE_HALF)) | (
        (dst >= ROPE_HALF) & (dst < ROPE_DIMS) & (src == dst - ROPE_HALF))
    return jnp.dot(_bf(x), pick.astype(BF16), preferred_element_type=F32)


def _norm_rope(x, gain, cos_t, sin_t):
    r = lax.rsqrt(jnp.mean(x * x, axis=1, keepdims=True) + EPS)
    xn = x * r * gain
    return xn * cos_t + _swap_rope_halves(xn) * sin_t, r


def _norm_rope_bwd(x, r, gain, cos_t, sin_t, dy):
    d_xn = dy * cos_t + _swap_rope_halves(dy * sin_t)
    xr = x * r
    u = d_xn * gain
    dx = r * u - xr * (r * r) * jnp.mean(u * x, axis=1, keepdims=True)
    return dx, jnp.sum(d_xn * xr, axis=0, keepdims=True)


def _stream_rows(idx, dilation, s_len):
    nb = s_len // dilation // B_BLOCK
    r = idx // nb
    m = idx % nb
    cur = r + m * (B_BLOCK * dilation)
    prev = r + jnp.maximum(m - 1, 0) * (B_BLOCK * dilation)
    return cur, prev, m > 0


def _rows(start, dilation):
    if dilation == 1:
        return pl.ds(start, B_BLOCK)
    return pl.ds(start, B_BLOCK, stride=dilation)


ATT_UNROLL = 16


def _band_mask(has_prev):
    qi = lax.broadcasted_iota(jnp.int32, (B_BLOCK, 2 * B_BLOCK), 0)
    kj = lax.broadcasted_iota(jnp.int32, (B_BLOCK, 2 * B_BLOCK), 1)
    return ((kj < B_BLOCK) & (kj >= qi) & has_prev) | ((kj >= B_BLOCK) & (kj - B_BLOCK <= qi))


def _attn_fwd_body(qkv_ref, z_ref, cos_ref, sin_ref, gain_ref, og_ref, o_ref, lse_ref,
                   qn_scr, kn_scr, og_scr, lg_scr):
    head, grp = pl.program_id(1), pl.program_id(2)
    s_len = z_ref.shape[0]
    n_blocks = s_len // B_BLOCK
    cos_t, sin_t = cos_ref[...], sin_ref[...]

    for gi, dil in enumerate(B_DILATIONS):
        @pl.when(grp == gi)
        def _(gi=gi, dil=dil):
            qn_scr[...], _ = _norm_rope(qkv_ref[0], gain_ref[gi:gi + 1, :], cos_t, sin_t)
            kn_scr[...], _ = _norm_rope(qkv_ref[1], gain_ref[B_GROUPS + gi:B_GROUPS + gi + 1, :], cos_t, sin_t)

            ones = jnp.ones((2 * B_BLOCK, B_DH), BF16)

            def blocks(it, carry):
                scored = []
                for j in range(ATT_UNROLL):
                    cur, prev, has_prev = _stream_rows(it * ATT_UNROLL + j, dil, s_len)
                    rc, rp = _rows(cur, dil), _rows(prev, dil)
                    k2 = jnp.concatenate([kn_scr[rp, :], kn_scr[rc, :]], axis=0)
                    scored.append((rc, rp, has_prev, _mm_nt(qn_scr[rc, :], k2) * ATT_SCALE))
                summed = []
                for rc, rp, has_prev, s in scored:
                    s = jnp.where(_band_mask(has_prev), s, NEG_BIG)
                    mx = jnp.max(s, axis=1, keepdims=True)
                    v2 = jnp.concatenate([qkv_ref.at[2][rp, :], qkv_ref.at[2][rc, :]], axis=0)
                    acc = jnp.dot(_bf(jnp.exp(s - mx)), jnp.concatenate([_bf(v2), ones], axis=1),
                                  preferred_element_type=F32)
                    summed.append((rc, mx, acc))
                for rc, mx, acc in summed:
                    den = acc[:, B_DH:B_DH + 1]
                    og_scr.at[gi][rc, :] = acc[:, :B_DH] / den
                    lg_scr.at[gi][rc, :] = jnp.broadcast_to(mx + jnp.log(den), (B_BLOCK, B_DH))
                return carry

            lax.fori_loop(0, n_blocks // ATT_UNROLL, blocks, 0)

    @pl.when(grp == B_GROUPS - 1)
    def _():
        l0, l1, l2 = lg_scr[0], lg_scr[1], lg_scr[2]
        mx = jnp.maximum(jnp.maximum(l0, l1), l2)
        w0, w1, w2 = jnp.exp(l0 - mx), jnp.exp(l1 - mx), jnp.exp(l2 - mx)
        den = w0 + w1 + w2
        o = (w0 * og_scr[0] + w1 * og_scr[1] + w2 * og_scr[2]) / den
        silu, _ = _silu_parts(z_ref[...])
        o_ref[...] = o
        og_ref[...] = (o * silu).astype(og_ref.dtype)
        @pl.when(head == 0)
        def _():
            lse_ref[...] = jnp.zeros_like(lse_ref)

        lane = lax.broadcasted_iota(jnp.int32, o.shape, 1)
        lse_ref[...] = jnp.where(lane == head, mx + jnp.log(den), lse_ref[...])


def attn_fwd(proj_b, cos_t, sin_t, gains8, n_seq):
    t_rows = proj_b.shape[1]
    s_len = t_rows // n_seq
    head_blk = pl.BlockSpec((s_len, B_DH), lambda b, h, g: (b, h))
    seq_blk = pl.BlockSpec((s_len, 128), lambda b, h, g: (b, 0))
    return pl.pallas_call(
        _attn_fwd_body, name="attn_fwd", grid=(n_seq, B_HEADS, B_GROUPS),
        in_specs=[
            pl.BlockSpec((3, s_len, B_DH), lambda b, h, g: (g, b, h)),
            pl.BlockSpec((None, s_len, B_DH), lambda b, h, g: (B_PIECES - 1, b, h)),
            seq_blk, seq_blk,
            pl.BlockSpec((8, 128), lambda b, h, g: (0, 0)),
        ],
        out_specs=[head_blk, head_blk, seq_blk],
        out_shape=[jax.ShapeDtypeStruct((t_rows, B_W), BF16),
                   jax.ShapeDtypeStruct((t_rows, B_W), F32),
                   jax.ShapeDtypeStruct((t_rows, 128), F32)],
        scratch_shapes=[pltpu.VMEM((s_len, B_DH), F32), pltpu.VMEM((s_len, B_DH), F32),
                        pltpu.VMEM((B_GROUPS, s_len, B_DH), F32), pltpu.VMEM((B_GROUPS, s_len, B_DH), F32)],
        compiler_params=_params(("arbitrary", "arbitrary", "arbitrary")),
    )(proj_b, proj_b, cos_t, sin_t, gains8)


def _attn_bwd_body(qkv_ref, z_ref, cos_ref, sin_ref, gain_ref, dog_ref, o_ref, lse_ref,
                   dqkv_ref, dz_ref, dgain_ref,
                   qn_scr, kn_scr, dqn_scr, dkn_scr, do_scr, dl_scr, ls_scr, dv_scr):
    head, grp = pl.program_id(1), pl.program_id(2)
    first = (pl.program_id(0) == 0) & (head == 0) & (grp == 0)
    s_len = z_ref.shape[0]
    n_blocks = s_len // B_BLOCK
    cos_t, sin_t = cos_ref[...], sin_ref[...]

    @pl.when(first)
    def _():
        dgain_ref[...] = jnp.zeros_like(dgain_ref)

    @pl.when(grp == 0)
    def _():
        d_og, o = dog_ref[...], o_ref[...]
        silu, dsilu = _silu_parts(z_ref[...])
        d_o = d_og * silu
        dz_ref[...] = (d_og * o * dsilu).astype(dz_ref.dtype)
        do_scr[...] = d_o
        dl_scr[...] = jnp.broadcast_to(jnp.sum(d_o * o, axis=1, keepdims=True), o.shape)
        lane = lax.broadcasted_iota(jnp.int32, o.shape, 1)
        ls_scr[...] = jnp.broadcast_to(
            jnp.sum(jnp.where(lane == head, lse_ref[...], 0.0), axis=1, keepdims=True), o.shape)

    for gi, dil in enumerate(B_DILATIONS):
        @pl.when(grp == gi)
        def _(gi=gi, dil=dil):
            q_raw, k_raw = qkv_ref[0], qkv_ref[1]
            gq = gain_ref[gi:gi + 1, :]
            gk = gain_ref[B_GROUPS + gi:B_GROUPS + gi + 1, :]
            qn_scr[...], rq = _norm_rope(q_raw, gq, cos_t, sin_t)
            kn_scr[...], rk = _norm_rope(k_raw, gk, cos_t, sin_t)
            def blocks(it, carry):
                scored = []
                for j in range(ATT_UNROLL):
                    cur, prev, has_prev = _stream_rows(it * ATT_UNROLL + j, dil, s_len)
                    rc, rp = _rows(cur, dil), _rows(prev, dil)
                    qb, d_ob = _bf(qn_scr[rc, :]), _bf(do_scr[rc, :])
                    k2 = _bf(jnp.concatenate([kn_scr[rp, :], kn_scr[rc, :]], axis=0))
                    v2 = _bf(jnp.concatenate([qkv_ref.at[2][rp, :], qkv_ref.at[2][rc, :]], axis=0))
                    scored.append((rc, rp, has_prev, qb, d_ob, k2,
                                   _mm_nt(qb, k2) * ATT_SCALE, _mm_nt(d_ob, v2)))
                grads = []
                for rc, rp, has_prev, qb, d_ob, k2, s, d_p in scored:
                    p = jnp.exp(jnp.where(_band_mask(has_prev), s - ls_scr[rc, :][:, 0:1], NEG_BIG))
                    ds = _bf(p * (d_p - dl_scr[rc, :][:, 0:1]))
                    grads.append((rc, rp, has_prev,
                                  _mm(ds, k2) * ATT_SCALE, _mm_tn(ds, qb) * ATT_SCALE, _mm_tn(_bf(p), d_ob)))
                for j, (rc, rp, has_prev, dq, dk2, dv2) in enumerate(grads):
                    dqn_scr[rc, :] = dq
                    if j == 0:
                        @pl.when(has_prev)
                        def _():
                            dkn_scr[rp, :] += dk2[:B_BLOCK]
                            dv_scr[rp, :] += dv2[:B_BLOCK]
                    if j + 1 < ATT_UNROLL:
                        dkn_scr[rc, :] = dk2[B_BLOCK:] + grads[j + 1][4][:B_BLOCK]
                        dv_scr[rc, :] = dv2[B_BLOCK:] + grads[j + 1][5][:B_BLOCK]
                    else:
                        dkn_scr[rc, :] = dk2[B_BLOCK:]
                        dv_scr[rc, :] = dv2[B_BLOCK:]
                return carry

            lax.fori_loop(0, n_blocks // ATT_UNROLL, blocks, 0)
            dq, dgq = _norm_rope_bwd(q_raw, rq, gq, cos_t, sin_t, dqn_scr[...])
            dk, dgk = _norm_rope_bwd(k_raw, rk, gk, cos_t, sin_t, dkn_scr[...])
            dqkv_ref[0] = dq.astype(dqkv_ref.dtype)
            dqkv_ref[1] = dk.astype(dqkv_ref.dtype)
            dqkv_ref[2] = dv_scr[...].astype(dqkv_ref.dtype)
            dgain_ref[gi:gi + 1, :] += dgq
            dgain_ref[B_GROUPS + gi:B_GROUPS + gi + 1, :] += dgk


def attn_bwd(proj_b, cos_t, sin_t, gains8, d_og, o, lse, n_seq):
    t_rows = proj_b.shape[1]
    s_len = t_rows // n_seq
    head_blk = pl.BlockSpec((s_len, B_DH), lambda b, h, g: (b, h))
    seq_blk = pl.BlockSpec((s_len, 128), lambda b, h, g: (b, 0))
    grp_blk = pl.BlockSpec((3, s_len, B_DH), lambda b, h, g: (g, b, h))
    gain_blk = pl.BlockSpec((8, 128), lambda b, h, g: (0, 0))
    return pl.pallas_call(
        _attn_bwd_body, name="attn_bwd", grid=(n_seq, B_HEADS, B_GROUPS),
        in_specs=[
            grp_blk,
            pl.BlockSpec((None, s_len, B_DH), lambda b, h, g: (B_PIECES - 1, b, h)),
            seq_blk, seq_blk, gain_blk, head_blk, head_blk, seq_blk,
        ],
        out_specs=[grp_blk, head_blk, gain_blk],
        out_shape=[jax.ShapeDtypeStruct((3 * B_GROUPS, t_rows, B_W), BF16),
                   jax.ShapeDtypeStruct((t_rows, B_W), BF16),
                   jax.ShapeDtypeStruct((8, 128), F32)],
        scratch_shapes=[pltpu.VMEM((s_len, B_DH), F32) for _ in range(8)],
        compiler_params=_params(("arbitrary", "arbitrary", "arbitrary")),
    )(proj_b, proj_b, cos_t, sin_t, gains8, d_og, o, lse)


def rope_tables(positions):
    inv_freq = ROPE_THETA ** (-jnp.arange(0, ROPE_DIMS, 2, dtype=F32) / ROPE_DIMS)
    ang = positions.astype(F32)[:, None] * inv_freq
    cos, sin = jnp.cos(ang), jnp.sin(ang)
    t_rows = positions.shape[0]
    rest = B_DH - ROPE_DIMS
    cos_t = jnp.concatenate([cos, cos, jnp.ones((t_rows, rest), F32)], axis=1)
    sin_t = jnp.concatenate([-sin, sin, jnp.zeros((t_rows, rest), F32)], axis=1)
    return cos_t, sin_t


def _rms_fwd_body(x_ref, g_ref, *rest, layer):
    h_ref, ht_ref = rest[-2:]
    x = x_ref[...]
    r = lax.rsqrt(jnp.mean(x * x, axis=1, keepdims=True) + EPS)
    h = x * r * g_ref[layer:layer + 1, :]
    h_ref[...] = h.astype(h_ref.dtype)
    ht_ref[...] = h.T.astype(ht_ref.dtype)


def rms_fwd(x, gains8, layer, after=None):
    t_rows, d = x.shape
    tm = min(ROW_TILE, t_rows)
    in_specs = [pl.BlockSpec((tm, d), lambda i: (i, 0)), pl.BlockSpec((8, d), lambda i: (0, 0))]
    args = [x, gains8]
    if after is not None:
        in_specs.append(HBM_SPEC)
        args.append(after)
    return pl.pallas_call(
        functools.partial(_rms_fwd_body, layer=layer), name=f"rms_fwd_{layer}", grid=(t_rows // tm,),
        in_specs=in_specs,
        out_specs=[pl.BlockSpec((tm, d), lambda i: (i, 0)), pl.BlockSpec((d, tm), lambda i: (0, i))],
        out_shape=[jax.ShapeDtypeStruct((t_rows, d), BF16), jax.ShapeDtypeStruct((d, t_rows), BF16)],
        compiler_params=_params(("arbitrary",)),
    )(*args)


def _rms_bwd_body(x_ref, g_ref, dh_ref, res_ref, dx_ref, dg_ref, *, layer):
    x, dh = x_ref[...], dh_ref[...]
    r = lax.rsqrt(jnp.mean(x * x, axis=1, keepdims=True) + EPS)
    xr = x * r
    u = dh * g_ref[layer:layer + 1, :]
    dx_ref[...] = res_ref[...] + r * u - xr * (r * r) * jnp.mean(u * x, axis=1, keepdims=True)

    @pl.when(pl.program_id(0) == 0)
    def _():
        dg_ref[...] = jnp.zeros_like(dg_ref)

    dg_ref[0:1, :] += jnp.sum(dh * xr, axis=0, keepdims=True)


def rms_bwd(x, gains8, layer, dh, d_res):
    t_rows, d = x.shape
    tm = min(ROW_TILE, t_rows)
    blk = pl.BlockSpec((tm, d), lambda i: (i, 0))
    gblk = pl.BlockSpec((8, d), lambda i: (0, 0))
    return pl.pallas_call(
        functools.partial(_rms_bwd_body, layer=layer), name=f"rms_bwd_{layer}", grid=(t_rows // tm,),
        in_specs=[blk, gblk, blk, blk],
        out_specs=[blk, gblk],
        out_shape=[jax.ShapeDtypeStruct((t_rows, d), F32), jax.ShapeDtypeStruct((8, d), F32)],
        compiler_params=_params(("arbitrary",)),
    )(x, gains8, dh, d_res)


def _piece_col(p):
    return jnp.where(p < 3 * B_GROUPS, (p % 3) * B_GROUPS + p // 3, 3 * B_GROUPS)


def _mm_nn_body(a_ref, w_ref, *rest, has_res, a_resident):
    o_ref = rest[-1]
    tm = o_ref.shape[0]
    a = a_ref[pl.ds(pl.multiple_of(pl.program_id(1) * tm, tm), tm), :] if a_resident else a_ref[...]
    acc = jnp.dot(a, w_ref[...], preferred_element_type=F32)
    if has_res:
        acc = acc + rest[0][...]
    o_ref[...] = acc


def mm_nn(a, w, residual=None, *, tn, name, a_resident=False):
    m, k = a.shape
    n = w.shape[1]
    tm = min(NT_ROW_TILE, m)
    a_spec = pl.BlockSpec((m, k), lambda j, i: (0, 0)) if a_resident else pl.BlockSpec((tm, k), lambda j, i: (i, 0))
    in_specs = [a_spec, pl.BlockSpec((k, tn), lambda j, i: (0, j))]
    args = [a, w]
    if residual is not None:
        in_specs.append(pl.BlockSpec((tm, tn), lambda j, i: (i, j)))
        args.append(residual)
    return pl.pallas_call(
        functools.partial(_mm_nn_body, has_res=residual is not None, a_resident=a_resident), name=name,
        grid=(n // tn, m // tm),
        in_specs=in_specs,
        out_specs=pl.BlockSpec((tm, tn), lambda j, i: (i, j)),
        out_shape=jax.ShapeDtypeStruct((m, n), F32),
        compiler_params=_params(("arbitrary", "arbitrary")),
    )(*args)


def mm_nn_pieces(a, w, *, name):
    m, k = a.shape
    tm = min(NT_ROW_TILE, m)
    return pl.pallas_call(
        functools.partial(_mm_nn_body, has_res=False, a_resident=True), name=name, grid=(B_PIECES, m // tm),
        in_specs=[pl.BlockSpec((m, k), lambda p, i: (0, 0)),
                  pl.BlockSpec((k, B_W), lambda p, i: (0, _piece_col(p)))],
        out_specs=pl.BlockSpec((None, tm, B_W), lambda p, i: (p, i, 0)),
        out_shape=jax.ShapeDtypeStruct((B_PIECES, m, B_W), F32),
        compiler_params=_params(("arbitrary", "arbitrary")),
    )(a, w)


NT_ROW_TILE = 1024


def _mm_nt_body(g_ref, w_ref, *rest, has_init):
    o_ref = rest[-1]
    j = pl.program_id(1)
    part = lax.dot_general(_bf(g_ref[...]), w_ref[...], (((1,), (1,)), ((), ())), preferred_element_type=F32)

    @pl.when(j == 0)
    def _():
        o_ref[...] = part + rest[0][...] if has_init else part

    @pl.when(j > 0)
    def _():
        o_ref[...] += part


def mm_nt(g, w, init=None, *, tn, col_off=0, name, after=None):
    m, n = g.shape
    k = w.shape[0]
    tm = min(NT_ROW_TILE, m)
    in_specs = [pl.BlockSpec((tm, tn), lambda i, j: (i, j)),
                pl.BlockSpec((k, tn), lambda i, j: (0, col_off + j))]
    args = [g, w]
    if init is not None:
        in_specs.append(pl.BlockSpec((tm, k), lambda i, j: (i, 0)))
        args.append(init)
    if after is not None:
        in_specs.append(HBM_SPEC)
        args.append(after)
    return pl.pallas_call(
        functools.partial(_mm_nt_body, has_init=init is not None), name=name, grid=(m // tm, n // tn),
        in_specs=in_specs,
        out_specs=pl.BlockSpec((tm, k), lambda i, j: (i, 0)),
        out_shape=jax.ShapeDtypeStruct((m, k), F32),
        compiler_params=_params(("arbitrary", "arbitrary")),
    )(*args)


def mm_nt_multi(gs, w, *, tn, name, after=None):
    m = gs[0].shape[0]
    k = w.shape[0]
    tm = min(NT_ROW_TILE, m)
    tiles = [g.shape[1] // tn for g in gs]
    starts = [sum(tiles[:i]) for i in range(len(gs))]

    def body(*refs):
        g_refs, w_ref, o_ref = refs[:len(gs)], refs[len(gs)], refs[-1]
        j = pl.program_id(1)
        for g_ref, lo, cnt in zip(g_refs, starts, tiles):
            @pl.when((j >= lo) & (j < lo + cnt))
            def _(g_ref=g_ref):
                part = lax.dot_general(_bf(g_ref[...]), w_ref[...], (((1,), (1,)), ((), ())),
                                       preferred_element_type=F32)

                @pl.when(j == 0)
                def _():
                    o_ref[...] = part

                @pl.when(j > 0)
                def _():
                    o_ref[...] += part

    def g_spec(lo, cnt):
        return pl.BlockSpec((tm, tn), lambda i, j: (i, jnp.clip(j - lo, 0, cnt - 1)))

    in_specs = [g_spec(lo, cnt) for lo, cnt in zip(starts, tiles)] + [pl.BlockSpec((k, tn), lambda i, j: (0, j))]
    args = list(gs) + [w]
    if after is not None:
        in_specs.append(HBM_SPEC)
        args.append(after)
    return pl.pallas_call(
        body, name=name, grid=(m // tm, sum(tiles)),
        in_specs=in_specs,
        out_specs=pl.BlockSpec((tm, k), lambda i, j: (i, 0)),
        out_shape=jax.ShapeDtypeStruct((m, k), F32),
        compiler_params=_params(("arbitrary", "arbitrary")),
    )(*args)


def mm_nt_pieces(g9, gz, w, *, name):
    n_p, m, _ = g9.shape
    k = w.shape[0]
    tm = min(2 * NT_ROW_TILE, m)

    def body(g_ref, z_ref, w_ref, o_ref):
        p = pl.program_id(1)

        def accumulate(src):
            part = lax.dot_general(_bf(src[...]), w_ref[...], (((1,), (1,)), ((), ())), preferred_element_type=F32)

            @pl.when(p == 0)
            def _():
                o_ref[...] = part

            @pl.when(p > 0)
            def _():
                o_ref[...] += part

        @pl.when(p < n_p)
        def _():
            accumulate(g_ref)

        @pl.when(p == n_p)
        def _():
            accumulate(z_ref)

    return pl.pallas_call(
        body, name=name, grid=(m // tm, n_p + 1),
        in_specs=[pl.BlockSpec((None, tm, B_W), lambda i, p: (jnp.minimum(p, n_p - 1), i, 0)),
                  pl.BlockSpec((tm, B_W), lambda i, p: (i, 0)),
                  pl.BlockSpec((k, B_W), lambda i, p: (0, _piece_col(p)))],
        out_specs=pl.BlockSpec((tm, k), lambda i, p: (i, 0)),
        out_shape=jax.ShapeDtypeStruct((m, k), F32),
        compiler_params=_params(("arbitrary", "arbitrary")),
    )(g9, gz, w)


def _mm_tn_body(a_ref, g_ref, o_ref, *, a_is_transposed):
    lhs_dim = 1 if a_is_transposed else 0
    o_ref[...] = lax.dot_general(a_ref[...], _bf(g_ref[...]), (((lhs_dim,), (0,)), ((), ())),
                                 preferred_element_type=F32).astype(o_ref.dtype)


def mm_tn(a, g, *, tn, out_dtype, name, a_is_transposed=False):
    k = a.shape[0] if a_is_transposed else a.shape[1]
    m, n = g.shape
    return pl.pallas_call(
        functools.partial(_mm_tn_body, a_is_transposed=a_is_transposed), name=name, grid=(n // tn,),
        in_specs=[pl.BlockSpec(a.shape, lambda j: (0, 0)), pl.BlockSpec((m, tn), lambda j: (0, j))],
        out_specs=pl.BlockSpec((k, tn), lambda j: (0, j)),
        out_shape=jax.ShapeDtypeStruct((k, n), out_dtype),
        compiler_params=_params(("arbitrary",)),
    )(a, g)


B_UNIT = 256
B_IN_COLS = B_PIECES * B_W
B_SHARD_UNITS = B_IN_COLS // N_DEV // B_UNIT


def mm_tn_b_in(at, g9, gz, *, out_dtype, name):
    k, m = at.shape
    per_piece = B_W // B_UNIT
    body_one = functools.partial(_mm_tn_body, a_is_transposed=True)
    n_units = B_IN_COLS // B_UNIT

    def g_map(u):
        nat = jnp.minimum(u // per_piece, 3 * B_GROUPS - 1)
        piece = (nat % B_GROUPS) * 3 + nat // B_GROUPS
        return (piece, 0, u % per_piece)

    def body(a_ref, g_ref, z_ref, o_ref):
        u = pl.program_id(0)

        @pl.when(u < 3 * B_GROUPS * per_piece)
        def _():
            body_one(a_ref, g_ref, o_ref)

        @pl.when(u >= 3 * B_GROUPS * per_piece)
        def _():
            body_one(a_ref, z_ref, o_ref)

    return pl.pallas_call(
        body, name=name, grid=(n_units,),
        in_specs=[pl.BlockSpec((k, m), lambda u: (0, 0)),
                  pl.BlockSpec((None, m, B_UNIT), g_map),
                  pl.BlockSpec((m, B_UNIT), lambda u: (0, jnp.where(u < 3 * B_GROUPS * per_piece, 0, u % per_piece)))],
        out_specs=pl.BlockSpec((None, k, B_UNIT), lambda u: (u // B_SHARD_UNITS, 0, u % B_SHARD_UNITS)),
        out_shape=jax.ShapeDtypeStruct((N_DEV, k, B_IN_COLS // N_DEV), out_dtype),
        compiler_params=_params(("arbitrary",)),
    )(at, g9, gz)


def _loss_body(y_ref, t_ref, dy_ref, loss_ref, acc):
    i = pl.program_id(0)
    d = y_ref.shape[1]
    err = y_ref[...] - t_ref[...]
    dy_ref[...] = err * (1.0 / d)

    @pl.when(i == 0)
    def _():
        acc[...] = jnp.zeros_like(acc)

    acc[...] += jnp.sum(err * err, axis=0, keepdims=True)

    @pl.when(i == pl.num_programs(0) - 1)
    def _():
        total = jnp.sum(acc[...], axis=1, keepdims=True) * (0.5 / d)
        loss_ref[...] = jnp.broadcast_to(total, loss_ref.shape)


def loss_head(y, target):
    t_rows, d = y.shape
    tm = min(ROW_TILE, t_rows)
    blk = pl.BlockSpec((tm, d), lambda i: (i, 0))
    return pl.pallas_call(
        _loss_body, name="loss_head", grid=(t_rows // tm,),
        in_specs=[blk, blk],
        out_specs=[blk, pl.BlockSpec((8, 128), lambda i: (0, 0))],
        out_shape=[jax.ShapeDtypeStruct((t_rows, d), F32), jax.ShapeDtypeStruct((8, 128), F32)],
        scratch_shapes=[pltpu.VMEM((1, d), F32)],
        compiler_params=_params(("arbitrary",)),
    )(y, target)


def _adamw_body(p_ref, w_ref, m_ref, v_ref, g_ref, d_ref, nm_ref, nv_ref):
    g = p_ref[0].astype(F32)
    for s in range(1, N_DEV):
        g = g + p_ref[s].astype(F32)
    w = w_ref[...]
    m = ADAM_B1 * m_ref[...] + (1.0 - ADAM_B1) * g
    v = ADAM_B2 * v_ref[...] + (1.0 - ADAM_B2) * (g * g)
    m_hat = m / (1.0 - ADAM_B1 ** ADAM_STEP)
    v_hat = v / (1.0 - ADAM_B2 ** ADAM_STEP)
    g_ref[...] = g
    d_ref[...] = -ADAM_LR * (m_hat / (jnp.sqrt(v_hat) + ADAM_EPS) + ADAM_WD * w)
    nm_ref[...] = m
    nv_ref[...] = v


def adamw(parts, w, m, v, *, name):
    _, r, c = w.shape
    tr = r if r <= 256 else 256
    blk = pl.BlockSpec((None, tr, c), lambda i: (0, i, 0))
    out = jax.ShapeDtypeStruct((1, r, c), F32)
    return pl.pallas_call(
        _adamw_body, name=name, grid=(r // tr,),
        in_specs=[pl.BlockSpec((N_DEV, tr, c), lambda i: (0, i, 0)), blk, blk, blk],
        out_specs=[blk, blk, blk, blk],
        out_shape=[out, out, out, out],
        compiler_params=_params(("arbitrary",)),
    )(parts, w, m, v)


MESH_ID = pl.DeviceIdType.MESH
HBM_SPEC = pl.BlockSpec(memory_space=pl.ANY)


def _my_place():
    return lax.axis_index("x"), lax.axis_index("y"), lax.axis_index("c")


def _flat(x, y, c):
    return 4 * x + 2 * y + c


def _all_gather_body(*refs, n):
    ins, outs = refs[:n], refs[n:2 * n]
    send_sems, recv_sems, local_sems = refs[2 * n:]
    x, y, c = _my_place()
    me, sibling = (x, y, c), (x, y, 1 - c)
    chips = [(1 - x, y), (x, 1 - y), (1 - x, 1 - y)]
    pending = []
    for a in range(n):
        src, out = ins[a], outs[a]

        def copy(k, block, to, from_input=False, a=a, src=src, out=out):
            slot = out.at[_flat(*block)]
            return pltpu.make_async_remote_copy(
                src_ref=src if from_input else slot, dst_ref=slot,
                send_sem=send_sems.at[7 * a + k], recv_sem=recv_sems.at[7 * a + k],
                device_id=to, device_id_type=MESH_ID)

        mine = pltpu.make_async_copy(src, out.at[_flat(*me)], local_sems.at[a])
        mine.start()
        first = [copy(0, me, sibling, True)] + [copy(1 + j, me, (*chip, c), True) for j, chip in enumerate(chips)]
        for cp in first:
            cp.start()
        pending.append((copy, mine, first))
    for copy, mine, first in pending:
        passed = [copy(4 + j, (*chip, c), sibling) for j, chip in enumerate(chips)]
        for j, chip in enumerate(chips):
            copy(1 + j, (*chip, c), me).wait_recv()
            passed[j].start()
        copy(0, sibling, me).wait_recv()
        for j, chip in enumerate(chips):
            copy(4 + j, (*chip, 1 - c), me).wait_recv()
        for cp in first + passed:
            cp.wait_send()
        mine.wait()


def all_gather(shards, *, name):
    n = len(shards)
    return pl.pallas_call(
        functools.partial(_all_gather_body, n=n), name=name,
        in_specs=[HBM_SPEC] * n, out_specs=[HBM_SPEC] * n,
        out_shape=[jax.ShapeDtypeStruct((N_DEV,) + s.shape, s.dtype) for s in shards],
        scratch_shapes=[pltpu.SemaphoreType.DMA((7 * n,)), pltpu.SemaphoreType.DMA((7 * n,)),
                        pltpu.SemaphoreType.DMA((n,))],
    )(*shards)


PEER_FLIPS = [(0, 0, 1), (1, 0, 0), (0, 1, 0), (1, 1, 0), (1, 0, 1), (0, 1, 1), (1, 1, 1)]


def _all_to_all_body(*refs, n):
    ins, outs = refs[:n], refs[n:2 * n]
    send_sems, recv_sems, local_sems = refs[2 * n:]
    x, y, c = _my_place()
    me = _flat(x, y, c)
    waits = []
    for a in range(n):
        src, out = ins[a], outs[a]
        mine = pltpu.make_async_copy(src.at[me], out.at[me], local_sems.at[a])
        mine.start()
        waits.append(mine)
        for k, (fx, fy, fc) in enumerate(PEER_FLIPS):
            peer = (1 - x if fx else x, 1 - y if fy else y, 1 - c if fc else c)
            theirs = _flat(*peer)
            sems = dict(send_sem=send_sems.at[7 * a + k], recv_sem=recv_sems.at[7 * a + k],
                        device_id=peer, device_id_type=MESH_ID)
            send = pltpu.make_async_remote_copy(src_ref=src.at[theirs], dst_ref=out.at[me], **sems)
            send.start()
            recv = pltpu.make_async_remote_copy(src_ref=src.at[theirs], dst_ref=out.at[theirs], **sems)
            waits.append((send, recv))
    for w in waits:
        if isinstance(w, tuple):
            w[0].wait_send()
            w[1].wait_recv()
        else:
            w.wait()


def all_to_all(parts, *, name):
    n = len(parts)
    return pl.pallas_call(
        functools.partial(_all_to_all_body, n=n), name=name,
        in_specs=[HBM_SPEC] * n, out_specs=[HBM_SPEC] * n,
        out_shape=[jax.ShapeDtypeStruct(p.shape, p.dtype) for p in parts],
        scratch_shapes=[pltpu.SemaphoreType.DMA((7 * n,)), pltpu.SemaphoreType.DMA((7 * n,)),
                        pltpu.SemaphoreType.DMA((n,))],
    )(*parts)


HBM_ONLY = pl.BlockSpec(memory_space=pltpu.HBM)
SEM_SPEC = pl.BlockSpec(memory_space=pltpu.SEMAPHORE)
DATAFLOW_EFFECT = pltpu.SideEffectType.DATAFLOW_SIDE_EFFECTING


def _split_copies(srcs, lands, send_sems, recv_sems, n, scatter):
    x, y, c = _my_place()
    me = _flat(x, y, c)
    pairs = []
    for a in range(n):
        for k, (fx, fy, fc) in enumerate(PEER_FLIPS):
            peer = (1 - x if fx else x, 1 - y if fy else y, 1 - c if fc else c)
            theirs = _flat(*peer)
            src = srcs[a].at[theirs] if scatter else srcs[a]
            sems = dict(send_sem=send_sems.at[7 * a + k], recv_sem=recv_sems.at[7 * a + k],
                        device_id=peer, device_id_type=MESH_ID)
            pairs.append((pltpu.make_async_remote_copy(src_ref=src, dst_ref=lands[a].at[me], **sems),
                          pltpu.make_async_remote_copy(src_ref=src, dst_ref=lands[a].at[theirs], **sems)))
    return pairs


def _exchange_start_body(*refs, n, scatter):
    srcs, lands = refs[:n], refs[n:2 * n]
    send_sems, recv_sems = refs[2 * n], refs[2 * n + 1]
    token = refs[-1]
    for send, _ in _split_copies(srcs, lands, send_sems, recv_sems, n, scatter):
        send.start()
    token[...] = jnp.zeros_like(token)


def exchange_start(srcs, lands, *, scatter, name):
    n = len(srcs)
    args = [pltpu.with_memory_space_constraint(t, pltpu.HBM) for t in list(srcs) + list(lands)]
    outs = pl.pallas_call(
        functools.partial(_exchange_start_body, n=n, scatter=scatter), name=name,
        out_shape=(pltpu.SemaphoreType.DMA((7 * n,)), pltpu.SemaphoreType.DMA((7 * n,)),
                   *[pltpu.HBM(t.shape, t.dtype) for t in args],
                   jax.ShapeDtypeStruct((8, 128), F32)),
        in_specs=[HBM_ONLY] * (2 * n),
        out_specs=(SEM_SPEC, SEM_SPEC, *[HBM_ONLY] * (2 * n), pl.BlockSpec(memory_space=pltpu.VMEM)),
        input_output_aliases={i: 2 + i for i in range(2 * n)},
        compiler_params=pltpu.CompilerParams(has_side_effects=DATAFLOW_EFFECT),
    )(*args)
    return outs[0], outs[1], outs[2:2 + n], outs[2 + n:2 + 2 * n], outs[-1]


def _exchange_wait_body(*refs, n, scatter):
    srcs, lands = refs[:n], refs[n:2 * n]
    send_sems, recv_sems = refs[2 * n], refs[2 * n + 1]
    for send, recv in _split_copies(srcs, lands, send_sems, recv_sems, n, scatter):
        send.wait_send()
        recv.wait_recv()


def exchange_wait(send_sems, recv_sems, srcs, lands, after, *, scatter, name):
    n = len(srcs)
    outs = pl.pallas_call(
        functools.partial(_exchange_wait_body, n=n, scatter=scatter), name=name,
        out_shape=tuple(pltpu.HBM(t.shape, t.dtype) for t in list(srcs) + list(lands)),
        in_specs=[HBM_ONLY] * (2 * n) + [SEM_SPEC, SEM_SPEC, HBM_SPEC],
        out_specs=tuple([HBM_ONLY] * (2 * n)),
        input_output_aliases={i: i for i in range(2 * n)},
        compiler_params=pltpu.CompilerParams(has_side_effects=DATAFLOW_EFFECT),
    )(*srcs, *lands, send_sems, recv_sems, after)
    return outs[n:]


def _own_slot_only(shape_dtype, own, slot):
    land = lax.empty(shape_dtype.shape, shape_dtype.dtype)
    return lax.dynamic_update_slice(land, own[None], (slot,) + (0,) * own.ndim)


def _pad_rows(a, rows=8):
    return jnp.pad(a, ((0, rows - a.shape[0]), (0, 0)))


def _gate_rows(a_log, dt_bias):
    z = jnp.zeros((8, 128), F32)
    return z.at[0, A_HEADS:2 * A_HEADS].set(a_log[0]).at[1, A_HEADS:2 * A_HEADS].set(dt_bias[0])


def _pack_small(norm_g, a_log, a_dt_bias, a_norm_g, b_q_norm_g, b_k_norm_g):
    return jnp.concatenate([
        norm_g[0].reshape(8, 128), norm_g[1].reshape(8, 128),
        _gate_rows(a_log, a_dt_bias),
        _pad_rows(a_norm_g[0].reshape(2, 128)),
        _pad_rows(jnp.concatenate([b_q_norm_g[0], b_k_norm_g[0]], axis=0)),
    ], axis=0)


def _unpack_small(p):
    return (p[0:16].reshape(2, D_MODEL), p[16:17, A_HEADS:2 * A_HEADS], p[17:18, A_HEADS:2 * A_HEADS],
            p[24:26].reshape(1, A_DV), p[32:35][None], p[35:38][None])


def kernel(x, positions, norm_g, a_w_in, a_conv_w, a_log, a_dt_bias, a_norm_g, a_w_out, b_w_in, b_q_norm_g, b_k_norm_g, b_w_out, loss_target, m_norm_g, m_a_w_in, m_a_conv_w, m_a_log, m_a_dt_bias, m_a_norm_g, m_a_w_out, m_b_w_in, m_b_q_norm_g, m_b_k_norm_g, m_b_w_out, v_norm_g, v_a_w_in, v_a_conv_w, v_a_log, v_a_dt_bias, v_a_norm_g, v_a_w_out, v_b_w_in, v_b_q_norm_g, v_b_k_norm_g, v_b_w_out):
    n_seq, s_len, d = x.shape
    t_rows = n_seq * s_len
    n_chunks = s_len // CHUNK
    x0 = x.reshape(t_rows, d)
    target = loss_target.reshape(t_rows, d)
    my_slot = _flat(*_my_place())

    g_a_in, g_conv = all_gather([a_w_in[0].astype(BF16), _pad_rows(a_conv_w[0])], name="gather_weights_first")
    later = [a_w_out[0].astype(BF16), b_w_in[0].astype(BF16), b_w_out[0].astype(BF16)]
    lands = [_own_slot_only(jax.ShapeDtypeStruct((N_DEV,) + t.shape, t.dtype), t, my_slot) for t in later]
    w_send, w_recv, later, lands, w_token = exchange_start(later, lands, scatter=False, name="gather_weights_start")
    w_a_in = jnp.pad(g_a_in.transpose(1, 0, 2).reshape(d, A_IN), ((0, 0), (0, A_IN_PAD - A_IN)))
    conv_w8 = g_conv.transpose(1, 0, 2).reshape(8, 2 * A_QK + A_VW)

    gains_model = _pad_rows(norm_g)
    gate_prm = _gate_rows(a_log, a_dt_bias)
    gain_a_out = _pad_rows(a_norm_g)
    gains_qk = _pad_rows(jnp.concatenate([b_q_norm_g[0], b_k_norm_g[0]], axis=0))
    cos_t, sin_t = rope_tables(positions.reshape(t_rows))

    h0, h0_t = rms_fwd(x0, gains_model, 0, after=w_token)
    proj_a = mm_nn(h0, w_a_in, tn=896, name="proj_a", a_resident=True)
    gates_col, gates_row = a_gates_fwd(proj_a, gate_prm, n_seq)
    gates_row = gates_row.reshape(n_seq, 2 * A_HEADS, s_len // SUPER, 1, SUPER)
    o_a, tinv, states, og_a, q_a, k_a, v_a = gdn_fwd(proj_a, conv_w8, gates_col, gates_row, gain_a_out, n_seq)
    g_a_out, g_b_in, g_b_out = exchange_wait(w_send, w_recv, later, lands, og_a, scatter=False,
                                             name="gather_weights_wait")
    w_a_out = g_a_out.reshape(A_VW, d)
    w_b_in = g_b_in.transpose(1, 0, 2).reshape(d, B_IN_COLS)
    w_b_out = g_b_out.reshape(B_W, d)
    x1 = mm_nn(og_a, w_a_out, x0, tn=1024, name="out_a")

    h1, h1_t = rms_fwd(x1, gains_model, 1)
    proj_b = mm_nn_pieces(h1, w_b_in, name="proj_b")
    og_b, o_b, lse = attn_fwd(proj_b, cos_t, sin_t, gains_qk, n_seq)
    y = mm_nn(og_b, w_b_out, x1, tn=1024, name="out_b")

    dy, loss_blk = loss_head(y, target)

    d_og_b = mm_nt(dy, w_b_out, tn=1024, name="d_og_b")
    dw_b_out = mm_tn(og_b, dy, tn=256, out_dtype=BF16, name="dw_b_out")
    dqkv_b, dz_b, d_gains_qk = attn_bwd(proj_b, cos_t, sin_t, gains_qk, d_og_b, o_b, lse, n_seq)
    dh1 = mm_nt_pieces(dqkv_b, dz_b, w_b_in, name="dh1")
    dw_b_in = mm_tn_b_in(h1_t, dqkv_b, dz_b, out_dtype=BF16, name="dw_b_in")
    dx1, d_gain1 = rms_bwd(x1, gains_model, 1, dh1, dy)

    dw_a_out = mm_tn(og_a, dx1, tn=256, out_dtype=BF16, name="dw_a_out")
    early = [dw_b_in, dw_b_out.reshape(N_DEV, B_W // N_DEV, d), dw_a_out.reshape(N_DEV, A_VW // N_DEV, d)]
    lands = [_own_slot_only(t, lax.dynamic_index_in_dim(t, my_slot, 0, keepdims=False), my_slot) for t in early]
    g_send, g_recv, early, lands, g_token = exchange_start(early, lands, scatter=True, name="scatter_grads_start")

    d_og_a = mm_nt(dx1, w_a_out, tn=1024, name="d_og_a", after=g_token)
    d_xq, d_xk, d_xv, dgates, dz_a, d_gain_a_out, d_cq, d_ck, d_cv = gdn_bwd(
        q_a, k_a, v_a, gates_col, gates_row, tinv, states, d_og_a, o_a, proj_a, conv_w8, gain_a_out, n_seq)
    d_conv = jnp.concatenate([d_cq.sum(axis=0), d_ck.sum(axis=0), d_cv.sum(axis=0)], axis=1)
    d_gate_logits, d_gate_prm = a_gates_bwd(proj_a, gate_prm, dgates, n_seq)
    dw_a_in = jnp.concatenate([
        mm_tn(h0_t, piece, tn=min(256, piece.shape[1]), out_dtype=BF16, name=f"dw_a_in_{nm}", a_is_transposed=True)
        for nm, piece in (("q", d_xq), ("k", d_xk), ("v", d_xv), ("z", dz_a), ("gates", d_gate_logits))
    ], axis=1)[:, :A_IN]
    shard_a_in = A_IN // N_DEV
    last = [dw_a_in.reshape(d, N_DEV, shard_a_in).transpose(1, 0, 2)]
    last_lands = [_own_slot_only(t, lax.dynamic_index_in_dim(t, my_slot, 0, keepdims=False), my_slot) for t in last]
    l_send, l_recv, last, last_lands, l_token = exchange_start(last, last_lands, scatter=True,
                                                               name="scatter_last_start")
    dh0 = mm_nt_multi([d_xq, d_xk, d_xv, dz_a], w_a_in, tn=1024, name="dh0_qkvz", after=l_token)
    dh0 = mm_nt(d_gate_logits, w_a_in, dh0, tn=128, col_off=A_GATE_COL, name="dh0_gates")
    dx0, d_gain0 = rms_bwd(x0, gains_model, 0, dh0, dx1)

    small = jnp.concatenate([
        d_gain0[0].reshape(8, 128), d_gain1[0].reshape(8, 128), d_gate_prm,
        _pad_rows(d_gain_a_out[0].reshape(2, 128)), d_gains_qk, loss_blk], axis=0)
    n_small = small.shape[0] - loss_blk.shape[0]
    small_srcs = [small, d_conv]
    small_lands = [_own_slot_only(jax.ShapeDtypeStruct((N_DEV,) + t.shape, t.dtype), t, my_slot)
                   for t in small_srcs]
    s_send, s_recv, small_srcs, small_lands, s_token = exchange_start(small_srcs, small_lands, scatter=False,
                                                                      name="gather_small_start")

    r_b_in, r_b_out, r_a_out = exchange_wait(g_send, g_recv, early, lands, s_token, scatter=True,
                                             name="scatter_grads_wait")
    (r_a_in,) = exchange_wait(l_send, l_recv, last, last_lands, s_token, scatter=True, name="scatter_last_wait")

    upd = {}
    upd["a_w_in"] = adamw(r_a_in, a_w_in, m_a_w_in, v_a_w_in, name="adamw_a_w_in")
    upd["a_w_out"] = adamw(r_a_out, a_w_out, m_a_w_out, v_a_w_out, name="adamw_a_w_out")
    upd["b_w_in"] = adamw(r_b_in, b_w_in, m_b_w_in, v_b_w_in, name="adamw_b_w_in")
    upd["b_w_out"] = adamw(r_b_out, b_w_out, m_b_w_out, v_b_w_out, name="adamw_b_w_out")
    r_small, r_conv = exchange_wait(s_send, s_recv, small_srcs, small_lands, upd["b_w_in"][0], scatter=False,
                                    name="gather_small_wait")
    conv_cols = a_conv_w.shape[2]
    r_conv = lax.dynamic_slice(r_conv, (0, 0, my_slot * conv_cols), (N_DEV, 8, conv_cols))
    loss = jnp.sum(r_small[:, n_small, 0])
    r_small = r_small[:, :n_small]
    upd["a_conv_w"] = [t[:, :A_CONV] for t in adamw(
        r_conv, _pad_rows(a_conv_w[0])[None], _pad_rows(m_a_conv_w[0])[None], _pad_rows(v_a_conv_w[0])[None],
        name="adamw_a_conv_w")]
    small_upd = adamw(
        r_small,
        _pack_small(norm_g, a_log, a_dt_bias, a_norm_g, b_q_norm_g, b_k_norm_g)[None],
        _pack_small(m_norm_g, m_a_log, m_a_dt_bias, m_a_norm_g, m_b_q_norm_g, m_b_k_norm_g)[None],
        _pack_small(v_norm_g, v_a_log, v_a_dt_bias, v_a_norm_g, v_b_q_norm_g, v_b_k_norm_g)[None],
        name="adamw_small")
    small_names = ("norm_g", "a_log", "a_dt_bias", "a_norm_g", "b_q_norm_g", "b_k_norm_g")
    unpacked = [_unpack_small(t[0]) for t in small_upd]
    for i, nm in enumerate(small_names):
        upd[nm] = [u[i] for u in unpacked]

    order = ("norm_g", "a_w_in", "a_conv_w", "a_log", "a_dt_bias", "a_norm_g", "a_w_out",
             "b_w_in", "b_q_norm_g", "b_k_norm_g", "b_w_out")
    outs = [loss, dx0.reshape(n_seq, s_len, d)]
    for kind in range(4):
        for nm in order:
            outs.append(upd[nm][kind])
    return tuple(outs)
```

```python
import functools
import math

import jax
import jax.numpy as jnp
from jax import lax
from jax.experimental import pallas as pl
from jax.experimental.pallas import tpu as pltpu

F32 = jnp.float32
BF16 = jnp.bfloat16

D_MODEL = 1024
EPS = 1e-6
N_DEV = 8

A_HEADS = 8
A_DK = 128
A_DV = 256
A_QK = A_HEADS * A_DK
A_VW = A_HEADS * A_DV
A_CONV = 4
CHUNK = 64
A_IN = 2 * A_QK + 2 * A_VW + 2 * A_HEADS
A_IN_PAD = 2 * A_QK + 2 * A_VW + 128
A_GATE_COL = (2 * A_QK + 2 * A_VW) // 128

B_DILATIONS = (1, 4, 16)
B_GROUPS = 3
B_HEADS = 8
B_DH = 128
B_W = B_HEADS * B_DH
B_BLOCK = 128
B_PIECES = 3 * B_GROUPS + 1
ROPE_THETA = 500000.0
ROPE_DIMS = B_DH // 4
ROPE_HALF = ROPE_DIMS // 2

ADAM_LR = 0.001
ADAM_B1 = 0.9
ADAM_B2 = 0.999
ADAM_EPS = 1e-08
ADAM_WD = 0.01
ADAM_STEP = 10

VMEM_LIMIT = 60 * 1024 * 1024


def _params(sem):
    return pltpu.CompilerParams(dimension_semantics=sem, vmem_limit_bytes=VMEM_LIMIT)


def _bf(x):
    return x.astype(BF16)


def _mm(a, b):
    return jnp.dot(_bf(a), _bf(b), preferred_element_type=F32)


def _mm_nt(a, b):
    return lax.dot_general(_bf(a), _bf(b), (((1,), (1,)), ((), ())), preferred_element_type=F32)


def _mm_tn(a, b):
    return lax.dot_general(_bf(a), _bf(b), (((0,), (0,)), ((), ())), preferred_element_type=F32)


def _split(x):
    hi = _bf(x)
    return hi, _bf(x - hi.astype(F32))


def _mm3(a, b):
    ah, al = _split(a)
    bh, bl = _split(b)
    d = functools.partial(jnp.dot, preferred_element_type=F32)
    return d(ah, bh) + (d(ah, bl) + d(al, bh))


def _colsum_as_col(z):
    zh, zl = _split(z)
    ones = jnp.ones((z.shape[0], 128), BF16)
    tn = functools.partial(lax.dot_general, dimension_numbers=(((0,), (0,)), ((), ())),
                           preferred_element_type=F32)
    return (tn(zh, ones) + tn(zl, ones))[:, 0:1]


def _sigmoid(x):
    return 0.5 * jnp.tanh(0.5 * x) + 0.5


INV_BASE = 8
INV_NEWTON = 2
GDN_GROUP = 4
SUPER = GDN_GROUP * CHUNK
GDN_WIDTH = 2

A_K_COL = A_QK // A_DK
A_V_COL = 2 * A_QK // A_DV


def _inverse_steps(m, row, col):
    eye = (row == col).astype(F32)
    d = jnp.where(row // INV_BASE == col // INV_BASE, m, 0.0)
    x = eye - d
    p = _mm(d, d)
    yield
    steps = int(math.log2(INV_BASE)) - 1
    for i in range(steps):
        x = x + _mm(x, p)
        if i + 1 < steps:
            p = _mm(p, p)
        yield
    size = INV_BASE
    while size < CHUNK:
        c = jnp.where((row // (2 * size) == col // (2 * size)) & (row // size != col // size), m, 0.0)
        xc = _mm(x, c)
        yield
        x = x - _mm(xc, x)
        yield
        size *= 2
    for _ in range(INV_NEWTON):
        r = eye - x - _mm3(m, x)
        yield
        x = x + _mm(x, r)
        yield
    return x


def _drain(gen):
    while True:
        try:
            next(gen)
        except StopIteration as stop:
            return stop.value


def _interleave(*gens):
    live = list(gens)
    while live:
        for g in list(live):
            try:
                next(g)
            except StopIteration:
                live.remove(g)


def _diag_blocks_tall(x):
    return jnp.concatenate([x[i * CHUNK:(i + 1) * CHUNK, i * CHUNK:(i + 1) * CHUNK] for i in range(GDN_GROUP)], axis=0)


def _tall_to_block_diag(t, same):
    return jnp.where(same, jnp.concatenate([t] * GDN_GROUP, axis=1), 0.0)


def _block_sum(same, x):
    xh, xl = _split(jnp.broadcast_to(x, (SUPER, 128)))
    ones = same.astype(BF16)
    d = functools.partial(jnp.dot, preferred_element_type=F32)
    return (d(ones, xh) + d(ones, xl))[:, 0:1]


def _aligned_rows(index, size):
    start = index * size
    return pl.ds(start if isinstance(start, int) else pl.multiple_of(start, size), size)


def _super_rows(i):
    return _aligned_rows(i, SUPER)


def _chunk_rows(n):
    return _aligned_rows(n, CHUNK)


def _gdn_super_steps(q, k, v, gcb, gr, head, tinv_tall=None, need_m=True):
    lane = lax.broadcasted_iota(jnp.int32, (SUPER, 128), 1)
    row = lax.broadcasted_iota(jnp.int32, (SUPER, SUPER), 0)
    col = lax.broadcasted_iota(jnp.int32, (SUPER, SUPER), 1)
    same = row // CHUNK == col // CHUNK
    beta = jnp.sum(jnp.where(lane == head, gcb, 0.0), axis=1, keepdims=True)
    gc = jnp.sum(jnp.where(lane == A_HEADS + head, gcb, 0.0), axis=1, keepdims=True)
    g_last = jnp.sum(jnp.where(col == (row // CHUNK) * CHUNK + (CHUNK - 1), gr, 0.0), axis=1, keepdims=True)
    gamma = jnp.exp(gc)
    decay = jnp.where(same & (row >= col), jnp.exp(jnp.minimum(gc - gr, 0.0)), 0.0)
    kb = k * beta
    m = jnp.where(same & (row > col), _mm_nt(kb, k) * decay, 0.0) if need_m else None
    p = jnp.where(same & (row >= col), _mm_nt(q, k) * decay, 0.0)
    yield
    if tinv_tall is None:
        tinv = yield from _inverse_steps(m, row, col)
    else:
        tinv = _tall_to_block_diag(tinv_tall, same)
    u = _mm(tinv, v * beta)
    w = _mm(tinv, kb * gamma)
    yield
    e_tail = jnp.exp(g_last - gc)
    return dict(beta=beta, gc=gc, g_last=g_last, gamma=gamma, decay=decay, kb=kb, m=m,
                tinv=tinv, u=u, w=w, p=p, e_tail=e_tail, row=row, col=col, lane=lane, same=same)


def _store_scan_operands(rows, q, k, t, u_scr, w_scr, p_scr, qg_scr, ke_scr, gl_scr):
    u_scr[rows, :] = t["u"]
    w_scr[rows, :] = _bf(t["w"])
    p_scr[rows, :] = _bf(_diag_blocks_tall(t["p"]))
    qg_scr[rows, :] = _bf(q * t["gamma"])
    ke_scr[rows, :] = _bf(k * t["e_tail"])
    gl_scr[rows, :] = jnp.broadcast_to(jnp.exp(t["g_last"]), (SUPER, 128))


def _gdn_fwd_body(q_ref, k_ref, v_ref, gc_ref, gr_ref, wq_ref, wk_ref, wv_ref, z_ref, gn_ref,
                  o_ref, tinv_ref, st_ref, og_ref, qo_ref, ko_ref, vo_ref, s_scr, *sets):
    head = pl.program_id(1)
    n_super = q_ref.shape[0] // SUPER
    all_sets = [sets[6 * i:6 * i + 6] for i in range(2 * GDN_WIDTH)]
    whole = pl.ds(0, SUPER)

    def conv_silu(x_ref, w_ref, i):
        rows = _super_rows(i)
        x, w = x_ref[rows, :], w_ref[...]
        halo = jnp.zeros((8, x.shape[1]), F32) if i == 0 else x_ref[pl.ds(i * SUPER - 8, 8), :]
        ext = jnp.concatenate([halo, x], axis=0)
        c = x * w[A_CONV - 1:A_CONV, :]
        for j in range(1, A_CONV):
            c = c + pltpu.roll(ext, j, 0)[8:, :] * w[A_CONV - 1 - j:A_CONV - j, :]
        return c * _sigmoid(c)

    def unit(a):
        return a * lax.rsqrt(jnp.sum(a * a, axis=1, keepdims=True) + EPS)

    def prepare_steps(i, dst):
        rows = _super_rows(i)
        q = unit(conv_silu(q_ref, wq_ref, i)) * A_DK ** -0.5
        k = unit(conv_silu(k_ref, wk_ref, i))
        v = conv_silu(v_ref, wv_ref, i)
        qo_ref[rows, :], ko_ref[rows, :], vo_ref[rows, :] = q, k, v
        t = yield from _gdn_super_steps(q, k, v, gc_ref[rows, :], gr_ref[i], head)
        tinv_ref[rows, :] = _diag_blocks_tall(t["tinv"])
        _store_scan_operands(whole, q, k, t, *dst)

    def scan_steps(i, src):
        u_scr, w_scr, p_scr, qg_scr, ke_scr, gl_scr = src
        for j in range(GDN_GROUP):
            n = i * GDN_GROUP + j
            local = pl.ds(j * CHUNK, CHUNK)
            s = s_scr[...]
            sb = _bf(s)
            st_ref[n] = sb
            ws = jnp.dot(w_scr[local, :], sb, preferred_element_type=F32)
            yield
            vb = _bf(u_scr[local, :] - ws)
            o = (jnp.dot(qg_scr[local, :], sb, preferred_element_type=F32)
                 + jnp.dot(p_scr[local, :], vb, preferred_element_type=F32))
            s_new = s * gl_scr[local, :][0:1, 0:1] + lax.dot_general(
                ke_scr[local, :], vb, (((0,), (0,)), ((), ())), preferred_element_type=F32)
            yield
            rows = _chunk_rows(n)
            o_ref[rows, :] = o
            s_scr[...] = s_new
            silu, _ = _silu_parts(z_ref[rows, :])
            r = lax.rsqrt(jnp.mean(o * o, axis=1, keepdims=True) + EPS)
            og_ref[rows, :] = ((o * r * gn_ref[0:1, :]) * silu).astype(og_ref.dtype)

    def scan_many(first, srcs):
        for j, src in enumerate(srcs):
            yield from scan_steps(first + j, src)

    groups = [all_sets[:GDN_WIDTH], all_sets[GDN_WIDTH:]]
    _interleave(*[prepare_steps(j, groups[0][j]) for j in range(GDN_WIDTH)])
    s_scr[...] = jnp.zeros_like(s_scr)
    for g in range(n_super // GDN_WIDTH):
        cur, nxt = groups[g % 2], groups[(g + 1) % 2]
        first = g * GDN_WIDTH
        following = [prepare_steps(first + GDN_WIDTH + j, nxt[j]) for j in range(GDN_WIDTH)
                     if first + GDN_WIDTH + j < n_super]
        _interleave(scan_many(first, cur), *following)


def _gdn_in_specs(s_len, n_super, from_proj):
    k_col, v_col = (A_K_COL, A_V_COL) if from_proj else (0, 0)
    return [
        pl.BlockSpec((s_len, A_DK), lambda b, h: (b, h)),
        pl.BlockSpec((s_len, A_DK), lambda b, h: (b, k_col + h)),
        pl.BlockSpec((s_len, A_DV), lambda b, h: (b, v_col + h)),
        pl.BlockSpec((s_len, 128), lambda b, h: (b, 0)),
        pl.BlockSpec((None, None, n_super, 1, SUPER), lambda b, h: (b, A_HEADS + h, 0, 0, 0)),
    ]


def _gdn_scan_scratch(s_len):
    return [pltpu.VMEM((A_DK, A_DV), F32), pltpu.VMEM((s_len, A_DV), F32),
            pltpu.VMEM((s_len, A_DK), BF16), pltpu.VMEM((s_len, CHUNK), BF16),
            pltpu.VMEM((s_len, A_DK), BF16), pltpu.VMEM((s_len, A_DK), BF16),
            pltpu.VMEM((s_len, 128), F32)]


def gdn_fwd(proj_a, conv_w8, gates_col, gates_row, norm_g8, n_seq):
    t_rows = proj_a.shape[0]
    s_len = t_rows // n_seq
    n_chunks = s_len // CHUNK
    qk_spec = pl.BlockSpec((s_len, A_DK), lambda b, h: (b, h))
    v_spec = pl.BlockSpec((s_len, A_DV), lambda b, h: (b, h))
    return pl.pallas_call(
        _gdn_fwd_body, name="gdn_fwd", grid=(n_seq, A_HEADS),
        in_specs=_gdn_in_specs(s_len, s_len // SUPER, True) + [
            pl.BlockSpec((8, A_DK), lambda b, h: (0, h)),
            pl.BlockSpec((8, A_DK), lambda b, h: (0, A_K_COL + h)),
            pl.BlockSpec((8, A_DV), lambda b, h: (0, A_V_COL + h)),
            pl.BlockSpec((s_len, A_DV), lambda b, h: (b, A_Z_COL + h)),
            pl.BlockSpec((8, A_DV), lambda b, h: (0, 0)),
        ],
        out_specs=[
            v_spec,
            pl.BlockSpec((s_len, CHUNK), lambda b, h: (b * A_HEADS + h, 0)),
            pl.BlockSpec((None, n_chunks, A_DK, A_DV), lambda b, h: (b * A_HEADS + h, 0, 0, 0)),
            v_spec, qk_spec, qk_spec, v_spec,
        ],
        out_shape=[
            jax.ShapeDtypeStruct((t_rows, A_VW), F32),
            jax.ShapeDtypeStruct((n_seq * A_HEADS * s_len, CHUNK), F32),
            jax.ShapeDtypeStruct((n_seq * A_HEADS, n_chunks, A_DK, A_DV), BF16),
            jax.ShapeDtypeStruct((t_rows, A_VW), BF16),
            jax.ShapeDtypeStruct((t_rows, A_QK), F32),
            jax.ShapeDtypeStruct((t_rows, A_QK), F32),
            jax.ShapeDtypeStruct((t_rows, A_VW), F32),
        ],
        scratch_shapes=_gdn_scan_scratch(SUPER) + (2 * GDN_WIDTH - 1) * _gdn_scan_scratch(SUPER)[1:],
        compiler_params=_params(("arbitrary", "arbitrary")),
    )(proj_a, proj_a, proj_a, gates_col, gates_row, conv_w8, conv_w8, conv_w8, proj_a, norm_g8)


def _gdn_bwd_body(q_ref, k_ref, v_ref, gc_ref, gr_ref, tinv_ref, st_ref, dog_ref, oa_ref, z_ref, gn_ref,
                  xq_ref, xk_ref, xv_ref, wq_ref, wk_ref, wv_ref,
                  dxq_ref, dxk_ref, dxv_ref, dgc_ref, dz_ref, dgn_ref, dwq_ref, dwk_ref, dwv_ref,
                  ds_scr, cq_scr, ck_scr, cv_scr, *sets):
    head = pl.program_id(1)
    n_super = q_ref.shape[0] // SUPER
    ops = (sets[0:7], sets[7:14])
    res = (sets[14:21], sets[21:28])
    whole = pl.ds(0, SUPER)
    tn = functools.partial(lax.dot_general, dimension_numbers=(((0,), (0,)), ((), ())), preferred_element_type=F32)
    nt = functools.partial(lax.dot_general, dimension_numbers=(((1,), (1,)), ((), ())), preferred_element_type=F32)

    @pl.when(head == 0)
    def _():
        dgc_ref[...] = jnp.zeros_like(dgc_ref)

    @pl.when((head == 0) & (pl.program_id(0) == 0))
    def _():
        dgn_ref[...] = jnp.zeros_like(dgn_ref)

    carry = (cq_scr, ck_scr, cv_scr)
    for ref in carry + (dwq_ref, dwk_ref, dwv_ref):
        ref[...] = jnp.zeros_like(ref)

    def common_steps(i, need_m=True):
        rows = _super_rows(i)
        q, k, v = q_ref[rows, :], k_ref[rows, :], v_ref[rows, :]
        t = yield from _gdn_super_steps(q, k, v, gc_ref[rows, :], gr_ref[i], head, tinv_tall=tinv_ref[rows, :],
                                        need_m=need_m)
        return rows, q, k, v, t

    def stage_p(i, parity):
        rows, q, k, _, t = yield from common_steps(i, need_m=False)
        _store_scan_operands(whole, q, k, t, *ops[parity][:6])
        o, d_og, gain = oa_ref[rows, :], dog_ref[rows, :], gn_ref[0:1, :]
        r = lax.rsqrt(jnp.mean(o * o, axis=1, keepdims=True) + EPS)
        silu, dsilu = _silu_parts(z_ref[rows, :])
        xr = o * r
        d_on = d_og * silu
        dz_ref[rows, :] = (d_og * (xr * gain) * dsilu).astype(dz_ref.dtype)
        u = d_on * gain
        ops[parity][6][...] = r * u - xr * (r * r) * jnp.mean(u * o, axis=1, keepdims=True)
        dgn_ref[0:1, :] += jnp.sum(d_on * xr, axis=0, keepdims=True)

    def stage_s(i, parity):
        u_scr, w_scr, p_scr, qg_scr, ke_scr, gl_scr, do_scr = ops[parity]
        vn_scr, dvn_scr, dqg_scr, dw_scr, dkt_scr, sds_scr, dof_scr = res[parity]
        for j in reversed(range(GDN_GROUP)):
            n = i * GDN_GROUP + j
            local = pl.ds(j * CHUNK, CHUNK)
            ds_next = ds_scr[...]
            dsb = _bf(ds_next)
            sb = st_ref[n]
            s = sb.astype(F32)
            d_o = do_scr[local, :]
            dof_scr[local, :] = d_o
            d_ob = _bf(d_o)
            w_s = jnp.dot(w_scr[local, :], sb, preferred_element_type=F32)
            d_vn = tn(p_scr[local, :], d_ob) + jnp.dot(ke_scr[local, :], dsb, preferred_element_type=F32)
            d_qg = nt(d_ob, sb)
            qg_do = tn(qg_scr[local, :], d_ob)
            yield
            v_new = u_scr[local, :] - w_s
            d_vnb = _bf(d_vn)
            d_w = -nt(d_vnb, sb)
            d_kt = nt(_bf(v_new), dsb)
            w_dvn = tn(w_scr[local, :], d_vnb)
            yield
            vn_scr[local, :] = v_new
            dvn_scr[local, :] = d_vn
            dqg_scr[local, :] = d_qg
            dw_scr[local, :] = d_w
            dkt_scr[local, :] = d_kt
            sds = jnp.sum(jnp.sum(s * ds_next, axis=1, keepdims=True), axis=0, keepdims=True)
            sds_scr[local, :] = jnp.broadcast_to(sds, (CHUNK, 128))
            ds_scr[...] = qg_do + gl_scr[local, :][0:1, 0:1] * ds_next - w_dvn

    def conv_bwd(i, rows, x_ref, w_ref, dy, norm_scale, dx_ref, dw_ref, dc_above):
        x, w = x_ref[rows, :], w_ref[...]
        above = x_ref[pl.ds(pl.multiple_of(jnp.maximum(i * SUPER - 8, 0), 8), 8), :]
        ext = jnp.concatenate([jnp.where(i > 0, above, 0.0), x], axis=0)
        c = x * w[A_CONV - 1:A_CONV, :]
        for j in range(1, A_CONV):
            c = c + pltpu.roll(ext, j, 0)[8:, :] * w[A_CONV - 1 - j:A_CONV - j, :]
        sig = _sigmoid(c)
        a = c * sig
        if norm_scale is None:
            da = dy
        else:
            rn = lax.rsqrt(jnp.sum(a * a, axis=1, keepdims=True) + EPS)
            da = norm_scale * (rn * dy - a * (rn * rn * rn) * jnp.sum(dy * a, axis=1, keepdims=True))
        dc = da * (sig * (1.0 + c * (1.0 - sig)))
        ext_dc = jnp.concatenate([dc, dc_above[...]], axis=0)
        dc_above[...] = dc[0:8, :]
        dx = dc * w[A_CONV - 1:A_CONV, :]
        dw_ref[A_CONV - 1:A_CONV, :] += jnp.sum(dc * x, axis=0, keepdims=True)
        for j in range(1, A_CONV):
            dcs = pltpu.roll(ext_dc, SUPER + 8 - j, 0)[:SUPER, :]
            dx = dx + dcs * w[A_CONV - 1 - j:A_CONV - j, :]
            dw_ref[A_CONV - 1 - j:A_CONV - j, :] += jnp.sum(dcs * x, axis=0, keepdims=True)
        dx_ref[rows, :] = dx.astype(dx_ref.dtype)

    def stage_f(i, parity):
        vn_scr, dvn_scr, dqg_scr, dw_scr, dkt_scr, sds_scr, dof_scr = res[parity]
        rows, q, k, v, t = yield from common_steps(i)
        beta, gamma, decay, kb, e_tail = t["beta"], t["gamma"], t["decay"], t["kb"], t["e_tail"]
        row, col, lane, same = t["row"], t["col"], t["lane"], t["same"]
        d_o = dof_scr[...]
        v_new, d_vn = vn_scr[...], dvn_scr[...]
        d_qg, d_w, d_kt = dqg_scr[...], dw_scr[...], dkt_scr[...]
        gamma_last = jnp.exp(t["g_last"])

        d_p = jnp.where(same & (row >= col), _mm_nt(d_o, v_new), 0.0)
        d_ru = _mm_tn(t["tinv"], d_vn)
        d_rw = _mm_tn(t["tinv"], d_w)
        yield
        d_m = jnp.where(same & (row > col), -(_mm_nt(d_ru, t["u"]) + _mm_nt(d_rw, t["w"])), 0.0)
        yield

        x_p = d_p * decay
        y_m = d_m * decay
        d_kb = _mm(y_m, k) + d_rw * gamma
        d_q = _mm(x_p, k) + d_qg * gamma
        d_k = _mm_tn(x_p, q) + _mm_tn(y_m, kb) + d_kb * beta + d_kt * e_tail
        d_v = d_ru * beta
        conv_bwd(i, rows, xq_ref, wq_ref, d_q, A_DK ** -0.5, dxq_ref, dwq_ref, carry[0])
        conv_bwd(i, rows, xk_ref, wk_ref, d_k, 1.0, dxk_ref, dwk_ref, carry[1])
        conv_bwd(i, rows, xv_ref, wv_ref, d_v, None, dxv_ref, dwv_ref, carry[2])

        d_beta = (jnp.sum(d_ru * v, axis=1, keepdims=True)
                  + jnp.sum(d_kb * k, axis=1, keepdims=True))
        z = d_p * t["p"] + d_m * t["m"]
        eps_tail = jnp.sum(d_kt * k, axis=1, keepdims=True) * e_tail
        d_gc = (jnp.sum(z, axis=1, keepdims=True) - _colsum_as_col(z)
                + jnp.sum(d_qg * q, axis=1, keepdims=True) * gamma
                + jnp.sum(d_rw * kb, axis=1, keepdims=True) * gamma
                - eps_tail)
        d_glast = _block_sum(same, eps_tail) + gamma_last * sds_scr[...][:, 0:1]
        yield
        rcol = lax.broadcasted_iota(jnp.int32, (SUPER, 1), 0)
        d_gc = d_gc + jnp.where(rcol % CHUNK == CHUNK - 1, d_glast, 0.0)
        dgc_ref[rows, :] += (jnp.where(lane == head, d_beta, 0.0)
                             + jnp.where(lane == A_HEADS + head, d_gc, 0.0))

    last = n_super - 1
    _drain(stage_p(last, 1))
    ds_scr[...] = jnp.zeros_like(ds_scr)
    _interleave(stage_s(last, 1), stage_p(last - 1, 0))

    def pair(k, carry):
        i = last - 1 - 2 * k
        _interleave(stage_s(i, 0), stage_f(i + 1, 1), stage_p(i - 1, 1))
        _interleave(stage_s(i - 1, 1), stage_f(i, 0), stage_p(i - 2, 0))
        return carry

    lax.fori_loop(0, n_super // 2 - 1, pair, 0)
    _interleave(stage_s(0, 0), stage_f(1, 1))
    _drain(stage_f(0, 0))


def gdn_bwd(q, k, v, gates_col, gates_row, tinv, states, d_og, o, proj_a, conv_w8, norm_g8, n_seq):
    t_rows = q.shape[0]
    s_len = t_rows // n_seq
    n_chunks = s_len // CHUNK
    qk_spec = pl.BlockSpec((s_len, A_DK), lambda b, h: (b, h))
    v_spec = pl.BlockSpec((s_len, A_DV), lambda b, h: (b, h))
    gate_spec = pl.BlockSpec((s_len, 128), lambda b, h: (b, 0))
    gain_spec = pl.BlockSpec((8, A_DV), lambda b, h: (0, 0))
    dw_qk_spec = pl.BlockSpec((None, 8, A_DK), lambda b, h: (b, 0, h))
    dw_v_spec = pl.BlockSpec((None, 8, A_DV), lambda b, h: (b, 0, h))
    ops_set = _gdn_scan_scratch(SUPER)[1:] + [pltpu.VMEM((SUPER, A_DV), F32)]
    res_set = [pltpu.VMEM((SUPER, A_DV), F32), pltpu.VMEM((SUPER, A_DV), F32),
               pltpu.VMEM((SUPER, A_DK), F32), pltpu.VMEM((SUPER, A_DK), F32),
               pltpu.VMEM((SUPER, A_DK), F32), pltpu.VMEM((SUPER, 128), F32), pltpu.VMEM((SUPER, A_DV), F32)]
    return pl.pallas_call(
        _gdn_bwd_body, name="gdn_bwd", grid=(n_seq, A_HEADS),
        in_specs=_gdn_in_specs(s_len, s_len // SUPER, False) + [
            pl.BlockSpec((s_len, CHUNK), lambda b, h: (b * A_HEADS + h, 0)),
            pl.BlockSpec((None, n_chunks, A_DK, A_DV), lambda b, h: (b * A_HEADS + h, 0, 0, 0)),
            v_spec, v_spec,
            pl.BlockSpec((s_len, A_DV), lambda b, h: (b, A_Z_COL + h)),
            gain_spec,
            pl.BlockSpec((s_len, A_DK), lambda b, h: (b, h)),
            pl.BlockSpec((s_len, A_DK), lambda b, h: (b, A_K_COL + h)),
            pl.BlockSpec((s_len, A_DV), lambda b, h: (b, A_V_COL + h)),
            pl.BlockSpec((8, A_DK), lambda b, h: (0, h)),
            pl.BlockSpec((8, A_DK), lambda b, h: (0, A_K_COL + h)),
            pl.BlockSpec((8, A_DV), lambda b, h: (0, A_V_COL + h)),
        ],
        out_specs=[qk_spec, qk_spec, v_spec, gate_spec, v_spec, gain_spec, dw_qk_spec, dw_qk_spec, dw_v_spec],
        out_shape=[
            jax.ShapeDtypeStruct((t_rows, A_QK), BF16),
            jax.ShapeDtypeStruct((t_rows, A_QK), BF16),
            jax.ShapeDtypeStruct((t_rows, A_VW), BF16),
            jax.ShapeDtypeStruct((t_rows, 128), F32),
            jax.ShapeDtypeStruct((t_rows, A_VW), BF16),
            jax.ShapeDtypeStruct((8, A_DV), F32),
            jax.ShapeDtypeStruct((n_seq, 8, A_QK), F32),
            jax.ShapeDtypeStruct((n_seq, 8, A_QK), F32),
            jax.ShapeDtypeStruct((n_seq, 8, A_VW), F32),
        ],
        scratch_shapes=(_gdn_scan_scratch(SUPER)[:1]
                        + [pltpu.VMEM((8, A_DK), F32), pltpu.VMEM((8, A_DK), F32), pltpu.VMEM((8, A_DV), F32)]
                        + 2 * ops_set + 2 * res_set),
        compiler_params=_params(("arbitrary", "arbitrary")),
    )(q, k, v, gates_col, gates_row, tinv, states, d_og, o, proj_a, norm_g8,
      proj_a, proj_a, proj_a, conv_w8, conv_w8, conv_w8)


GATE_TILE = 512


def _softplus(y):
    return jnp.maximum(y, 0.0) + jnp.log1p(jnp.exp(-jnp.abs(y)))


def _gate_values(x, prm):
    beta = _sigmoid(x)
    y = x + prm[1:2, :]
    neg_a = -jnp.exp(prm[0:1, :])
    g = neg_a * _softplus(y)
    return beta, y, neg_a, g


def _a_gates_fwd_body(x_ref, prm_ref, gc_ref, gr_ref):
    x = x_ref[...]
    tm = x.shape[0]
    beta, _, _, g = _gate_values(x, prm_ref[...])
    in_chunk = lax.broadcasted_iota(jnp.int32, (tm, 1), 0) % CHUNK
    s = 1
    while s < CHUNK:
        g = g + jnp.where(in_chunk >= s, pltpu.roll(g, s, 0), 0.0)
        s *= 2
    lane = lax.broadcasted_iota(jnp.int32, x.shape, 1)
    out = jnp.where(lane < A_HEADS, beta, jnp.where(lane < 2 * A_HEADS, g, 0.0))
    gc_ref[...] = out
    gr_ref[...] = out.T[0:2 * A_HEADS, :]


def a_gates_fwd(proj_a, prm, n_seq):
    t_rows = proj_a.shape[0]
    s_len = t_rows // n_seq
    tm = min(GATE_TILE, s_len)
    n_t = s_len // tm
    return pl.pallas_call(
        _a_gates_fwd_body, name="a_gates_fwd", grid=(n_seq, n_t),
        in_specs=[pl.BlockSpec((tm, 128), lambda b, i: (b * n_t + i, A_GATE_COL)),
                  pl.BlockSpec((8, 128), lambda b, i: (0, 0))],
        out_specs=[pl.BlockSpec((tm, 128), lambda b, i: (b * n_t + i, 0)),
                   pl.BlockSpec((None, 2 * A_HEADS, tm), lambda b, i: (b, 0, i))],
        out_shape=[jax.ShapeDtypeStruct((t_rows, 128), F32),
                   jax.ShapeDtypeStruct((n_seq, 2 * A_HEADS, s_len), F32)],
        compiler_params=_params(("arbitrary", "arbitrary")),
    )(proj_a, prm)


def _a_gates_bwd_body(x_ref, prm_ref, dgc_ref, dx_ref, dprm_ref):
    first = (pl.program_id(0) == 0) & (pl.program_id(1) == 0)
    x = x_ref[...]
    tm = x.shape[0]
    beta, y, neg_a, g = _gate_values(x, prm_ref[...])
    d = dgc_ref[...]
    in_chunk = lax.broadcasted_iota(jnp.int32, (tm, 1), 0) % CHUNK
    dg = d
    s = 1
    while s < CHUNK:
        dg = dg + jnp.where(in_chunk < CHUNK - s, pltpu.roll(dg, tm - s, 0), 0.0)
        s *= 2
    lane = lax.broadcasted_iota(jnp.int32, x.shape, 1)
    is_decay = (lane >= A_HEADS) & (lane < 2 * A_HEADS)
    d_alogit = jnp.where(is_decay, dg * neg_a * _sigmoid(y), 0.0)
    dx_ref[...] = jnp.where(lane < A_HEADS, d * beta * (1.0 - beta), d_alogit).astype(dx_ref.dtype)

    @pl.when(first)
    def _():
        dprm_ref[...] = jnp.zeros_like(dprm_ref)

    dprm_ref[0:1, :] += jnp.sum(jnp.where(is_decay, dg * g, 0.0), axis=0, keepdims=True)
    dprm_ref[1:2, :] += jnp.sum(d_alogit, axis=0, keepdims=True)


def a_gates_bwd(proj_a, prm, dgates_col, n_seq):
    t_rows = proj_a.shape[0]
    s_len = t_rows // n_seq
    tm = min(GATE_TILE, s_len)
    n_t = s_len // tm
    return pl.pallas_call(
        _a_gates_bwd_body, name="a_gates_bwd", grid=(n_seq, n_t),
        in_specs=[pl.BlockSpec((tm, 128), lambda b, i: (b * n_t + i, A_GATE_COL)),
                  pl.BlockSpec((8, 128), lambda b, i: (0, 0)),
                  pl.BlockSpec((tm, 128), lambda b, i: (b * n_t + i, 0))],
        out_specs=[pl.BlockSpec((tm, 128), lambda b, i: (b * n_t + i, 0)),
                   pl.BlockSpec((8, 128), lambda b, i: (0, 0))],
        out_shape=[jax.ShapeDtypeStruct((t_rows, 128), BF16),
                   jax.ShapeDtypeStruct((8, 128), F32)],
        compiler_params=_params(("arbitrary", "arbitrary")),
    )(proj_a, prm, dgates_col)


ROW_TILE = 1024
A_Z_COL = (2 * A_QK + A_VW) // A_DV


def _silu_parts(z):
    sig = _sigmoid(z)
    return z * sig, sig * (1.0 + z * (1.0 - sig))


NEG_BIG = -1e30
ATT_SCALE = B_DH ** -0.5


def _swap_rope_halves(x):
    src = lax.broadcasted_iota(jnp.int32, (B_DH, B_DH), 0)
    dst = lax.broadcasted_iota(jnp.int32, (B_DH, B_DH), 1)
    pick = ((dst < ROPE_HALF) & (src == dst + ROPE_HALF)) | (
        (dst >= ROPE_HALF) & (dst < ROPE_DIMS) & (src == dst - ROPE_HALF))
    return jnp.dot(_bf(x), pick.astype(BF16), preferred_element_type=F32)


def _norm_rope(x, gain, cos_t, sin_t):
    r = lax.rsqrt(jnp.mean(x * x, axis=1, keepdims=True) + EPS)
    xn = x * r * gain
    return xn * cos_t + _swap_rope_halves(xn) * sin_t, r


def _norm_rope_bwd(x, r, gain, cos_t, sin_t, dy):
    d_xn = dy * cos_t + _swap_rope_halves(dy * sin_t)
    xr = x * r
    u = d_xn * gain
    dx = r * u - xr * (r * r) * jnp.mean(u * x, axis=1, keepdims=True)
    return dx, jnp.sum(d_xn * xr, axis=0, keepdims=True)


def _stream_rows(idx, dilation, s_len):
    nb = s_len // dilation // B_BLOCK
    r = idx // nb
    m = idx % nb
    cur = r + m * (B_BLOCK * dilation)
    prev = r + jnp.maximum(m - 1, 0) * (B_BLOCK * dilation)
    return cur, prev, m > 0


def _rows(start, dilation):
    if dilation == 1:
        return pl.ds(start, B_BLOCK)
    return pl.ds(start, B_BLOCK, stride=dilation)


ATT_UNROLL = 16


def _band_mask(has_prev):
    qi = lax.broadcasted_iota(jnp.int32, (B_BLOCK, 2 * B_BLOCK), 0)
    kj = lax.broadcasted_iota(jnp.int32, (B_BLOCK, 2 * B_BLOCK), 1)
    return ((kj < B_BLOCK) & (kj >= qi) & has_prev) | ((kj >= B_BLOCK) & (kj - B_BLOCK <= qi))


def _attn_fwd_body(qkv_ref, z_ref, cos_ref, sin_ref, gain_ref, og_ref, o_ref, lse_ref,
                   qn_scr, kn_scr, og_scr, lg_scr):
    head, grp = pl.program_id(1), pl.program_id(2)
    s_len = z_ref.shape[0]
    n_blocks = s_len // B_BLOCK
    cos_t, sin_t = cos_ref[...], sin_ref[...]

    for gi, dil in enumerate(B_DILATIONS):
        @pl.when(grp == gi)
        def _(gi=gi, dil=dil):
            qn_scr[...], _ = _norm_rope(qkv_ref[0], gain_ref[gi:gi + 1, :], cos_t, sin_t)
            kn_scr[...], _ = _norm_rope(qkv_ref[1], gain_ref[B_GROUPS + gi:B_GROUPS + gi + 1, :], cos_t, sin_t)

            ones = jnp.ones((2 * B_BLOCK, B_DH), BF16)

            def blocks(it, carry):
                scored = []
                for j in range(ATT_UNROLL):
                    cur, prev, has_prev = _stream_rows(it * ATT_UNROLL + j, dil, s_len)
                    rc, rp = _rows(cur, dil), _rows(prev, dil)
                    k2 = jnp.concatenate([kn_scr[rp, :], kn_scr[rc, :]], axis=0)
                    scored.append((rc, rp, has_prev, _mm_nt(qn_scr[rc, :], k2) * ATT_SCALE))
                summed = []
                for rc, rp, has_prev, s in scored:
                    s = jnp.where(_band_mask(has_prev), s, NEG_BIG)
                    mx = jnp.max(s, axis=1, keepdims=True)
                    v2 = jnp.concatenate([qkv_ref.at[2][rp, :], qkv_ref.at[2][rc, :]], axis=0)
                    acc = jnp.dot(_bf(jnp.exp(s - mx)), jnp.concatenate([_bf(v2), ones], axis=1),
                                  preferred_element_type=F32)
                    summed.append((rc, mx, acc))
                for rc, mx, acc in summed:
                    den = acc[:, B_DH:B_DH + 1]
                    og_scr.at[gi][rc, :] = acc[:, :B_DH] / den
                    lg_scr.at[gi][rc, :] = jnp.broadcast_to(mx + jnp.log(den), (B_BLOCK, B_DH))
                return carry

            lax.fori_loop(0, n_blocks // ATT_UNROLL, blocks, 0)

    @pl.when(grp == B_GROUPS - 1)
    def _():
        l0, l1, l2 = lg_scr[0], lg_scr[1], lg_scr[2]
        mx = jnp.maximum(jnp.maximum(l0, l1), l2)
        w0, w1, w2 = jnp.exp(l0 - mx), jnp.exp(l1 - mx), jnp.exp(l2 - mx)
        den = w0 + w1 + w2
        o = (w0 * og_scr[0] + w1 * og_scr[1] + w2 * og_scr[2]) / den
        silu, _ = _silu_parts(z_ref[...])
        o_ref[...] = o
        og_ref[...] = (o * silu).astype(og_ref.dtype)
        @pl.when(head == 0)
        def _():
            lse_ref[...] = jnp.zeros_like(lse_ref)

        lane = lax.broadcasted_iota(jnp.int32, o.shape, 1)
        lse_ref[...] = jnp.where(lane == head, mx + jnp.log(den), lse_ref[...])


def attn_fwd(proj_b, cos_t, sin_t, gains8, n_seq):
    t_rows = proj_b.shape[1]
    s_len = t_rows // n_seq
    head_blk = pl.BlockSpec((s_len, B_DH), lambda b, h, g: (b, h))
    seq_blk = pl.BlockSpec((s_len, 128), lambda b, h, g: (b, 0))
    return pl.pallas_call(
        _attn_fwd_body, name="attn_fwd", grid=(n_seq, B_HEADS, B_GROUPS),
        in_specs=[
            pl.BlockSpec((3, s_len, B_DH), lambda b, h, g: (g, b, h)),
            pl.BlockSpec((None, s_len, B_DH), lambda b, h, g: (B_PIECES - 1, b, h)),
            seq_blk, seq_blk,
            pl.BlockSpec((8, 128), lambda b, h, g: (0, 0)),
        ],
        out_specs=[head_blk, head_blk, seq_blk],
        out_shape=[jax.ShapeDtypeStruct((t_rows, B_W), BF16),
                   jax.ShapeDtypeStruct((t_rows, B_W), F32),
                   jax.ShapeDtypeStruct((t_rows, 128), F32)],
        scratch_shapes=[pltpu.VMEM((s_len, B_DH), F32), pltpu.VMEM((s_len, B_DH), F32),
                        pltpu.VMEM((B_GROUPS, s_len, B_DH), F32), pltpu.VMEM((B_GROUPS, s_len, B_DH), F32)],
        compiler_params=_params(("arbitrary", "arbitrary", "arbitrary")),
    )(proj_b, proj_b, cos_t, sin_t, gains8)


def _attn_bwd_body(qkv_ref, z_ref, cos_ref, sin_ref, gain_ref, dog_ref, o_ref, lse_ref,
                   dqkv_ref, dz_ref, dgain_ref,
                   qn_scr, kn_scr, dqn_scr, dkn_scr, do_scr, dl_scr, ls_scr, dv_scr):
    head, grp = pl.program_id(1), pl.program_id(2)
    first = (pl.program_id(0) == 0) & (head == 0) & (grp == 0)
    s_len = z_ref.shape[0]
    n_blocks = s_len // B_BLOCK
    cos_t, sin_t = cos_ref[...], sin_ref[...]

    @pl.when(first)
    def _():
        dgain_ref[...] = jnp.zeros_like(dgain_ref)

    @pl.when(grp == 0)
    def _():
        d_og, o = dog_ref[...], o_ref[...]
        silu, dsilu = _silu_parts(z_ref[...])
        d_o = d_og * silu
        dz_ref[...] = (d_og * o * dsilu).astype(dz_ref.dtype)
        do_scr[...] = d_o
        dl_scr[...] = jnp.broadcast_to(jnp.sum(d_o * o, axis=1, keepdims=True), o.shape)
        lane = lax.broadcasted_iota(jnp.int32, o.shape, 1)
        ls_scr[...] = jnp.broadcast_to(
            jnp.sum(jnp.where(lane == head, lse_ref[...], 0.0), axis=1, keepdims=True), o.shape)

    for gi, dil in enumerate(B_DILATIONS):
        @pl.when(grp == gi)
        def _(gi=gi, dil=dil):
            q_raw, k_raw = qkv_ref[0], qkv_ref[1]
            gq = gain_ref[gi:gi + 1, :]
            gk = gain_ref[B_GROUPS + gi:B_GROUPS + gi + 1, :]
            qn_scr[...], rq = _norm_rope(q_raw, gq, cos_t, sin_t)
            kn_scr[...], rk = _norm_rope(k_raw, gk, cos_t, sin_t)
            def blocks(it, carry):
                scored = []
                for j in range(ATT_UNROLL):
                    cur, prev, has_prev = _stream_rows(it * ATT_UNROLL + j, dil, s_len)
                    rc, rp = _rows(cur, dil), _rows(prev, dil)
                    qb, d_ob = _bf(qn_scr[rc, :]), _bf(do_scr[rc, :])
                    k2 = _bf(jnp.concatenate([kn_scr[rp, :], kn_scr[rc, :]], axis=0))
                    v2 = _bf(jnp.concatenate([qkv_ref.at[2][rp, :], qkv_ref.at[2][rc, :]], axis=0))
                    scored.append((rc, rp, has_prev, qb, d_ob, k2,
                                   _mm_nt(qb, k2) * ATT_SCALE, _mm_nt(d_ob, v2)))
                grads = []
                for rc, rp, has_prev, qb, d_ob, k2, s, d_p in scored:
                    p = jnp.exp(jnp.where(_band_mask(has_prev), s - ls_scr[rc, :][:, 0:1], NEG_BIG))
                    ds = _bf(p * (d_p - dl_scr[rc, :][:, 0:1]))
                    grads.append((rc, rp, has_prev,
                                  _mm(ds, k2) * ATT_SCALE, _mm_tn(ds, qb) * ATT_SCALE, _mm_tn(_bf(p), d_ob)))
                for j, (rc, rp, has_prev, dq, dk2, dv2) in enumerate(grads):
                    dqn_scr[rc, :] = dq
                    if j == 0:
                        @pl.when(has_prev)
                        def _():
                            dkn_scr[rp, :] += dk2[:B_BLOCK]
                            dv_scr[rp, :] += dv2[:B_BLOCK]
                    if j + 1 < ATT_UNROLL:
                        dkn_scr[rc, :] = dk2[B_BLOCK:] + grads[j + 1][4][:B_BLOCK]
                        dv_scr[rc, :] = dv2[B_BLOCK:] + grads[j + 1][5][:B_BLOCK]
                    else:
                        dkn_scr[rc, :] = dk2[B_BLOCK:]
                        dv_scr[rc, :] = dv2[B_BLOCK:]
                return carry

            lax.fori_loop(0, n_blocks // ATT_UNROLL, blocks, 0)
            dq, dgq = _norm_rope_bwd(q_raw, rq, gq, cos_t, sin_t, dqn_scr[...])
            dk, dgk = _norm_rope_bwd(k_raw, rk, gk, cos_t, sin_t, dkn_scr[...])
            dqkv_ref[0] = dq.astype(dqkv_ref.dtype)
            dqkv_ref[1] = dk.astype(dqkv_ref.dtype)
            dqkv_ref[2] = dv_scr[...].astype(dqkv_ref.dtype)
            dgain_ref[gi:gi + 1, :] += dgq
            dgain_ref[B_GROUPS + gi:B_GROUPS + gi + 1, :] += dgk


def attn_bwd(proj_b, cos_t, sin_t, gains8, d_og, o, lse, n_seq):
    t_rows = proj_b.shape[1]
    s_len = t_rows // n_seq
    head_blk = pl.BlockSpec((s_len, B_DH), lambda b, h, g: (b, h))
    seq_blk = pl.BlockSpec((s_len, 128), lambda b, h, g: (b, 0))
    grp_blk = pl.BlockSpec((3, s_len, B_DH), lambda b, h, g: (g, b, h))
    gain_blk = pl.BlockSpec((8, 128), lambda b, h, g: (0, 0))
    return pl.pallas_call(
        _attn_bwd_body, name="attn_bwd", grid=(n_seq, B_HEADS, B_GROUPS),
        in_specs=[
            grp_blk,
            pl.BlockSpec((None, s_len, B_DH), lambda b, h, g: (B_PIECES - 1, b, h)),
            seq_blk, seq_blk, gain_blk, head_blk, head_blk, seq_blk,
        ],
        out_specs=[grp_blk, head_blk, gain_blk],
        out_shape=[jax.ShapeDtypeStruct((3 * B_GROUPS, t_rows, B_W), BF16),
                   jax.ShapeDtypeStruct((t_rows, B_W), BF16),
                   jax.ShapeDtypeStruct((8, 128), F32)],
        scratch_shapes=[pltpu.VMEM((s_len, B_DH), F32) for _ in range(8)],
        compiler_params=_params(("arbitrary", "arbitrary", "arbitrary")),
    )(proj_b, proj_b, cos_t, sin_t, gains8, d_og, o, lse)


def rope_tables(positions):
    inv_freq = ROPE_THETA ** (-jnp.arange(0, ROPE_DIMS, 2, dtype=F32) / ROPE_DIMS)
    ang = positions.astype(F32)[:, None] * inv_freq
    cos, sin = jnp.cos(ang), jnp.sin(ang)
    t_rows = positions.shape[0]
    rest = B_DH - ROPE_DIMS
    cos_t = jnp.concatenate([cos, cos, jnp.ones((t_rows, rest), F32)], axis=1)
    sin_t = jnp.concatenate([-sin, sin, jnp.zeros((t_rows, rest), F32)], axis=1)
    return cos_t, sin_t


def _rms_fwd_body(x_ref, g_ref, *rest, layer):
    h_ref, ht_ref = rest[-2:]
    x = x_ref[...]
    r = lax.rsqrt(jnp.mean(x * x, axis=1, keepdims=True) + EPS)
    h = x * r * g_ref[layer:layer + 1, :]
    h_ref[...] = h.astype(h_ref.dtype)
    ht_ref[...] = h.T.astype(ht_ref.dtype)


def rms_fwd(x, gains8, layer, after=None):
    t_rows, d = x.shape
    tm = min(ROW_TILE, t_rows)
    in_specs = [pl.BlockSpec((tm, d), lambda i: (i, 0)), pl.BlockSpec((8, d), lambda i: (0, 0))]
    args = [x, gains8]
    if after is not None:
        in_specs.append(HBM_SPEC)
        args.append(after)
    return pl.pallas_call(
        functools.partial(_rms_fwd_body, layer=layer), name=f"rms_fwd_{layer}", grid=(t_rows // tm,),
        in_specs=in_specs,
        out_specs=[pl.BlockSpec((tm, d), lambda i: (i, 0)), pl.BlockSpec((d, tm), lambda i: (0, i))],
        out_shape=[jax.ShapeDtypeStruct((t_rows, d), BF16), jax.ShapeDtypeStruct((d, t_rows), BF16)],
        compiler_params=_params(("arbitrary",)),
    )(*args)


def _rms_bwd_body(x_ref, g_ref, dh_ref, res_ref, dx_ref, dg_ref, *, layer):
    x, dh = x_ref[...], dh_ref[...]
    r = lax.rsqrt(jnp.mean(x * x, axis=1, keepdims=True) + EPS)
    xr = x * r
    u = dh * g_ref[layer:layer + 1, :]
    dx_ref[...] = res_ref[...] + r * u - xr * (r * r) * jnp.mean(u * x, axis=1, keepdims=True)

    @pl.when(pl.program_id(0) == 0)
    def _():
        dg_ref[...] = jnp.zeros_like(dg_ref)

    dg_ref[0:1, :] += jnp.sum(dh * xr, axis=0, keepdims=True)


def rms_bwd(x, gains8, layer, dh, d_res):
    t_rows, d = x.shape
    tm = min(ROW_TILE, t_rows)
    blk = pl.BlockSpec((tm, d), lambda i: (i, 0))
    gblk = pl.BlockSpec((8, d), lambda i: (0, 0))
    return pl.pallas_call(
        functools.partial(_rms_bwd_body, layer=layer), name=f"rms_bwd_{layer}", grid=(t_rows // tm,),
        in_specs=[blk, gblk, blk, blk],
        out_specs=[blk, gblk],
        out_shape=[jax.ShapeDtypeStruct((t_rows, d), F32), jax.ShapeDtypeStruct((8, d), F32)],
        compiler_params=_params(("arbitrary",)),
    )(x, gains8, dh, d_res)


def _piece_col(p):
    return jnp.where(p < 3 * B_GROUPS, (p % 3) * B_GROUPS + p // 3, 3 * B_GROUPS)


def _mm_nn_body(a_ref, w_ref, *rest, has_res, a_resident):
    o_ref = rest[-1]
    tm = o_ref.shape[0]
    a = a_ref[pl.ds(pl.multiple_of(pl.program_id(1) * tm, tm), tm), :] if a_resident else a_ref[...]
    acc = jnp.dot(a, w_ref[...], preferred_element_type=F32)
    if has_res:
        acc = acc + rest[0][...]
    o_ref[...] = acc


def mm_nn(a, w, residual=None, *, tn, name, a_resident=False):
    m, k = a.shape
    n = w.shape[1]
    tm = min(NT_ROW_TILE, m)
    a_spec = pl.BlockSpec((m, k), lambda j, i: (0, 0)) if a_resident else pl.BlockSpec((tm, k), lambda j, i: (i, 0))
    in_specs = [a_spec, pl.BlockSpec((k, tn), lambda j, i: (0, j))]
    args = [a, w]
    if residual is not None:
        in_specs.append(pl.BlockSpec((tm, tn), lambda j, i: (i, j)))
        args.append(residual)
    return pl.pallas_call(
        functools.partial(_mm_nn_body, has_res=residual is not None, a_resident=a_resident), name=name,
        grid=(n // tn, m // tm),
        in_specs=in_specs,
        out_specs=pl.BlockSpec((tm, tn), lambda j, i: (i, j)),
        out_shape=jax.ShapeDtypeStruct((m, n), F32),
        compiler_params=_params(("arbitrary", "arbitrary")),
    )(*args)


def mm_nn_pieces(a, w, *, name):
    m, k = a.shape
    tm = min(NT_ROW_TILE, m)
    return pl.pallas_call(
        functools.partial(_mm_nn_body, has_res=False, a_resident=True), name=name, grid=(B_PIECES, m // tm),
        in_specs=[pl.BlockSpec((m, k), lambda p, i: (0, 0)),
                  pl.BlockSpec((k, B_W), lambda p, i: (0, _piece_col(p)))],
        out_specs=pl.BlockSpec((None, tm, B_W), lambda p, i: (p, i, 0)),
        out_shape=jax.ShapeDtypeStruct((B_PIECES, m, B_W), F32),
        compiler_params=_params(("arbitrary", "arbitrary")),
    )(a, w)


NT_ROW_TILE = 1024


def _mm_nt_body(g_ref, w_ref, *rest, has_init):
    o_ref = rest[-1]
    j = pl.program_id(1)
    part = lax.dot_general(_bf(g_ref[...]), w_ref[...], (((1,), (1,)), ((), ())), preferred_element_type=F32)

    @pl.when(j == 0)
    def _():
        o_ref[...] = part + rest[0][...] if has_init else part

    @pl.when(j > 0)
    def _():
        o_ref[...] += part


def mm_nt(g, w, init=None, *, tn, col_off=0, name, after=None):
    m, n = g.shape
    k = w.shape[0]
    tm = min(NT_ROW_TILE, m)
    in_specs = [pl.BlockSpec((tm, tn), lambda i, j: (i, j)),
                pl.BlockSpec((k, tn), lambda i, j: (0, col_off + j))]
    args = [g, w]
    if init is not None:
        in_specs.append(pl.BlockSpec((tm, k), lambda i, j: (i, 0)))
        args.append(init)
    if after is not None:
        in_specs.append(HBM_SPEC)
        args.append(after)
    return pl.pallas_call(
        functools.partial(_mm_nt_body, has_init=init is not None), name=name, grid=(m // tm, n // tn),
        in_specs=in_specs,
        out_specs=pl.BlockSpec((tm, k), lambda i, j: (i, 0)),
        out_shape=jax.ShapeDtypeStruct((m, k), F32),
        compiler_params=_params(("arbitrary", "arbitrary")),
    )(*args)


def mm_nt_multi(gs, w, *, tn, name, after=None):
    m = gs[0].shape[0]
    k = w.shape[0]
    tm = min(NT_ROW_TILE, m)
    tiles = [g.shape[1] // tn for g in gs]
    starts = [sum(tiles[:i]) for i in range(len(gs))]

    def body(*refs):
        g_refs, w_ref, o_ref = refs[:len(gs)], refs[len(gs)], refs[-1]
        j = pl.program_id(1)
        for g_ref, lo, cnt in zip(g_refs, starts, tiles):
            @pl.when((j >= lo) & (j < lo + cnt))
            def _(g_ref=g_ref):
                part = lax.dot_general(_bf(g_ref[...]), w_ref[...], (((1,), (1,)), ((), ())),
                                       preferred_element_type=F32)

                @pl.when(j == 0)
                def _():
                    o_ref[...] = part

                @pl.when(j > 0)
                def _():
                    o_ref[...] += part

    def g_spec(lo, cnt):
        return pl.BlockSpec((tm, tn), lambda i, j: (i, jnp.clip(j - lo, 0, cnt - 1)))

    in_specs = [g_spec(lo, cnt) for lo, cnt in zip(starts, tiles)] + [pl.BlockSpec((k, tn), lambda i, j: (0, j))]
    args = list(gs) + [w]
    if after is not None:
        in_specs.append(HBM_SPEC)
        args.append(after)
    return pl.pallas_call(
        body, name=name, grid=(m // tm, sum(tiles)),
        in_specs=in_specs,
        out_specs=pl.BlockSpec((tm, k), lambda i, j: (i, 0)),
        out_shape=jax.ShapeDtypeStruct((m, k), F32),
        compiler_params=_params(("arbitrary", "arbitrary")),
    )(*args)


def mm_nt_pieces(g9, gz, w, *, name):
    n_p, m, _ = g9.shape
    k = w.shape[0]
    tm = min(2 * NT_ROW_TILE, m)

    def body(g_ref, z_ref, w_ref, o_ref):
        p = pl.program_id(1)

        def accumulate(src):
            part = lax.dot_general(_bf(src[...]), w_ref[...], (((1,), (1,)), ((), ())), preferred_element_type=F32)

            @pl.when(p == 0)
            def _():
                o_ref[...] = part

            @pl.when(p > 0)
            def _():
                o_ref[...] += part

        @pl.when(p < n_p)
        def _():
            accumulate(g_ref)

        @pl.when(p == n_p)
        def _():
            accumulate(z_ref)

    return pl.pallas_call(
        body, name=name, grid=(m // tm, n_p + 1),
        in_specs=[pl.BlockSpec((None, tm, B_W), lambda i, p: (jnp.minimum(p, n_p - 1), i, 0)),
                  pl.BlockSpec((tm, B_W), lambda i, p: (i, 0)),
                  pl.BlockSpec((k, B_W), lambda i, p: (0, _piece_col(p)))],
        out_specs=pl.BlockSpec((tm, k), lambda i, p: (i, 0)),
        out_shape=jax.ShapeDtypeStruct((m, k), F32),
        compiler_params=_params(("arbitrary", "arbitrary")),
    )(g9, gz, w)


def _mm_tn_body(a_ref, g_ref, o_ref, *, a_is_transposed):
    lhs_dim = 1 if a_is_transposed else 0
    o_ref[...] = lax.dot_general(a_ref[...], _bf(g_ref[...]), (((lhs_dim,), (0,)), ((), ())),
                                 preferred_element_type=F32).astype(o_ref.dtype)


def mm_tn(a, g, *, tn, out_dtype, name, a_is_transposed=False):
    k = a.shape[0] if a_is_transposed else a.shape[1]
    m, n = g.shape
    return pl.pallas_call(
        functools.partial(_mm_tn_body, a_is_transposed=a_is_transposed), name=name, grid=(n // tn,),
        in_specs=[pl.BlockSpec(a.shape, lambda j: (0, 0)), pl.BlockSpec((m, tn), lambda j: (0, j))],
        out_specs=pl.BlockSpec((k, tn), lambda j: (0, j)),
        out_shape=jax.ShapeDtypeStruct((k, n), out_dtype),
        compiler_params=_params(("arbitrary",)),
    )(a, g)


def mm_tn_multi(at, gs, *, unit, out_dtype, name):
    k, m = at.shape
    counts = [g.shape[1] // unit for g in gs]
    starts = [sum(counts[:i]) for i in range(len(gs))]

    def body(at_ref, *refs):
        g_refs, o_ref = refs[:-1], refs[-1]
        u = pl.program_id(0)
        for g_ref, lo, cnt in zip(g_refs, starts, counts):
            @pl.when((u >= lo) & (u < lo + cnt))
            def _(g_ref=g_ref):
                _mm_tn_body(at_ref, g_ref, o_ref, a_is_transposed=True)

    def g_spec(lo, cnt):
        return pl.BlockSpec((m, unit), lambda u: (0, jnp.clip(u - lo, 0, cnt - 1)))

    return pl.pallas_call(
        body, name=name, grid=(sum(counts),),
        in_specs=[pl.BlockSpec((k, m), lambda u: (0, 0))] + [g_spec(lo, cnt) for lo, cnt in zip(starts, counts)],
        out_specs=pl.BlockSpec((k, unit), lambda u: (0, u)),
        out_shape=jax.ShapeDtypeStruct((k, sum(counts) * unit), out_dtype),
        compiler_params=_params(("arbitrary",)),
    )(at, *gs)


B_UNIT = 256
B_IN_COLS = B_PIECES * B_W
B_SHARD_UNITS = B_IN_COLS // N_DEV // B_UNIT


def mm_tn_b_in(at, g9, gz, *, out_dtype, name):
    k, m = at.shape
    per_piece = B_W // B_UNIT
    body_one = functools.partial(_mm_tn_body, a_is_transposed=True)
    n_units = B_IN_COLS // B_UNIT

    def g_map(u):
        nat = jnp.minimum(u // per_piece, 3 * B_GROUPS - 1)
        piece = (nat % B_GROUPS) * 3 + nat // B_GROUPS
        return (piece, 0, u % per_piece)

    def body(a_ref, g_ref, z_ref, o_ref):
        u = pl.program_id(0)

        @pl.when(u < 3 * B_GROUPS * per_piece)
        def _():
            body_one(a_ref, g_ref, o_ref)

        @pl.when(u >= 3 * B_GROUPS * per_piece)
        def _():
            body_one(a_ref, z_ref, o_ref)

    return pl.pallas_call(
        body, name=name, grid=(n_units,),
        in_specs=[pl.BlockSpec((k, m), lambda u: (0, 0)),
                  pl.BlockSpec((None, m, B_UNIT), g_map),
                  pl.BlockSpec((m, B_UNIT), lambda u: (0, jnp.where(u < 3 * B_GROUPS * per_piece, 0, u % per_piece)))],
        out_specs=pl.BlockSpec((None, k, B_UNIT), lambda u: (u // B_SHARD_UNITS, 0, u % B_SHARD_UNITS)),
        out_shape=jax.ShapeDtypeStruct((N_DEV, k, B_IN_COLS // N_DEV), out_dtype),
        compiler_params=_params(("arbitrary",)),
    )(at, g9, gz)


def _loss_body(y_ref, t_ref, dy_ref, loss_ref, acc):
    i = pl.program_id(0)
    d = y_ref.shape[1]
    err = y_ref[...] - t_ref[...]
    dy_ref[...] = err * (1.0 / d)

    @pl.when(i == 0)
    def _():
        acc[...] = jnp.zeros_like(acc)

    acc[...] += jnp.sum(err * err, axis=0, keepdims=True)

    @pl.when(i == pl.num_programs(0) - 1)
    def _():
        total = jnp.sum(acc[...], axis=1, keepdims=True) * (0.5 / d)
        loss_ref[...] = jnp.broadcast_to(total, loss_ref.shape)


def loss_head(y, target):
    t_rows, d = y.shape
    tm = min(ROW_TILE, t_rows)
    blk = pl.BlockSpec((tm, d), lambda i: (i, 0))
    return pl.pallas_call(
        _loss_body, name="loss_head", grid=(t_rows // tm,),
        in_specs=[blk, blk],
        out_specs=[blk, pl.BlockSpec((8, 128), lambda i: (0, 0))],
        out_shape=[jax.ShapeDtypeStruct((t_rows, d), F32), jax.ShapeDtypeStruct((8, 128), F32)],
        scratch_shapes=[pltpu.VMEM((1, d), F32)],
        compiler_params=_params(("arbitrary",)),
    )(y, target)


def _adamw_body(p_ref, w_ref, m_ref, v_ref, g_ref, d_ref, nm_ref, nv_ref):
    g = p_ref[0].astype(F32)
    for s in range(1, N_DEV):
        g = g + p_ref[s].astype(F32)
    w = w_ref[...]
    m = ADAM_B1 * m_ref[...] + (1.0 - ADAM_B1) * g
    v = ADAM_B2 * v_ref[...] + (1.0 - ADAM_B2) * (g * g)
    m_hat = m / (1.0 - ADAM_B1 ** ADAM_STEP)
    v_hat = v / (1.0 - ADAM_B2 ** ADAM_STEP)
    g_ref[...] = g
    d_ref[...] = -ADAM_LR * (m_hat / (jnp.sqrt(v_hat) + ADAM_EPS) + ADAM_WD * w)
    nm_ref[...] = m
    nv_ref[...] = v


def adamw(parts, w, m, v, *, name):
    _, r, c = w.shape
    tr = r if r <= 256 else 256
    blk = pl.BlockSpec((None, tr, c), lambda i: (0, i, 0))
    out = jax.ShapeDtypeStruct((1, r, c), F32)
    return pl.pallas_call(
        _adamw_body, name=name, grid=(r // tr,),
        in_specs=[pl.BlockSpec((N_DEV, tr, c), lambda i: (0, i, 0)), blk, blk, blk],
        out_specs=[blk, blk, blk, blk],
        out_shape=[out, out, out, out],
        compiler_params=_params(("arbitrary",)),
    )(parts, w, m, v)


MESH_ID = pl.DeviceIdType.MESH
HBM_SPEC = pl.BlockSpec(memory_space=pl.ANY)


def _my_place():
    return lax.axis_index("x"), lax.axis_index("y"), lax.axis_index("c")


def _flat(x, y, c):
    return 4 * x + 2 * y + c


def _all_gather_body(*refs, n):
    ins, outs = refs[:n], refs[n:2 * n]
    send_sems, recv_sems, local_sems = refs[2 * n:]
    x, y, c = _my_place()
    me, sibling = (x, y, c), (x, y, 1 - c)
    chips = [(1 - x, y), (x, 1 - y), (1 - x, 1 - y)]
    pending = []
    for a in range(n):
        src, out = ins[a], outs[a]

        def copy(k, block, to, from_input=False, a=a, src=src, out=out):
            slot = out.at[_flat(*block)]
            return pltpu.make_async_remote_copy(
                src_ref=src if from_input else slot, dst_ref=slot,
                send_sem=send_sems.at[7 * a + k], recv_sem=recv_sems.at[7 * a + k],
                device_id=to, device_id_type=MESH_ID)

        mine = pltpu.make_async_copy(src, out.at[_flat(*me)], local_sems.at[a])
        mine.start()
        first = [copy(0, me, sibling, True)] + [copy(1 + j, me, (*chip, c), True) for j, chip in enumerate(chips)]
        for cp in first:
            cp.start()
        pending.append((copy, mine, first))
    for copy, mine, first in pending:
        passed = [copy(4 + j, (*chip, c), sibling) for j, chip in enumerate(chips)]
        for j, chip in enumerate(chips):
            copy(1 + j, (*chip, c), me).wait_recv()
            passed[j].start()
        copy(0, sibling, me).wait_recv()
        for j, chip in enumerate(chips):
            copy(4 + j, (*chip, 1 - c), me).wait_recv()
        for cp in first + passed:
            cp.wait_send()
        mine.wait()


def all_gather(shards, *, name):
    n = len(shards)
    return pl.pallas_call(
        functools.partial(_all_gather_body, n=n), name=name,
        in_specs=[HBM_SPEC] * n, out_specs=[HBM_SPEC] * n,
        out_shape=[jax.ShapeDtypeStruct((N_DEV,) + s.shape, s.dtype) for s in shards],
        scratch_shapes=[pltpu.SemaphoreType.DMA((7 * n,)), pltpu.SemaphoreType.DMA((7 * n,)),
                        pltpu.SemaphoreType.DMA((n,))],
    )(*shards)


PEER_FLIPS = [(0, 0, 1), (1, 0, 0), (0, 1, 0), (1, 1, 0), (1, 0, 1), (0, 1, 1), (1, 1, 1)]


def _all_to_all_body(*refs, n):
    ins, outs = refs[:n], refs[n:2 * n]
    send_sems, recv_sems, local_sems = refs[2 * n:]
    x, y, c = _my_place()
    me = _flat(x, y, c)
    waits = []
    for a in range(n):
        src, out = ins[a], outs[a]
        mine = pltpu.make_async_copy(src.at[me], out.at[me], local_sems.at[a])
        mine.start()
        waits.append(mine)
        for k, (fx, fy, fc) in enumerate(PEER_FLIPS):
            peer = (1 - x if fx else x, 1 - y if fy else y, 1 - c if fc else c)
            theirs = _flat(*peer)
            sems = dict(send_sem=send_sems.at[7 * a + k], recv_sem=recv_sems.at[7 * a + k],
                        device_id=peer, device_id_type=MESH_ID)
            send = pltpu.make_async_remote_copy(src_ref=src.at[theirs], dst_ref=out.at[me], **sems)
            send.start()
            recv = pltpu.make_async_remote_copy(src_ref=src.at[theirs], dst_ref=out.at[theirs], **sems)
            waits.append((send, recv))
    for w in waits:
        if isinstance(w, tuple):
            w[0].wait_send()
            w[1].wait_recv()
        else:
            w.wait()


def all_to_all(parts, *, name):
    n = len(parts)
    return pl.pallas_call(
        functools.partial(_all_to_all_body, n=n), name=name,
        in_specs=[HBM_SPEC] * n, out_specs=[HBM_SPEC] * n,
        out_shape=[jax.ShapeDtypeStruct(p.shape, p.dtype) for p in parts],
        scratch_shapes=[pltpu.SemaphoreType.DMA((7 * n,)), pltpu.SemaphoreType.DMA((7 * n,)),
                        pltpu.SemaphoreType.DMA((n,))],
    )(*parts)


HBM_ONLY = pl.BlockSpec(memory_space=pltpu.HBM)
SEM_SPEC = pl.BlockSpec(memory_space=pltpu.SEMAPHORE)
DATAFLOW_EFFECT = pltpu.SideEffectType.DATAFLOW_SIDE_EFFECTING


def _split_copies(srcs, lands, send_sems, recv_sems, n, scatter):
    x, y, c = _my_place()
    me = _flat(x, y, c)
    pairs = []
    for a in range(n):
        for k, (fx, fy, fc) in enumerate(PEER_FLIPS):
            peer = (1 - x if fx else x, 1 - y if fy else y, 1 - c if fc else c)
            theirs = _flat(*peer)
            src = srcs[a].at[theirs] if scatter else srcs[a]
            sems = dict(send_sem=send_sems.at[7 * a + k], recv_sem=recv_sems.at[7 * a + k],
                        device_id=peer, device_id_type=MESH_ID)
            pairs.append((pltpu.make_async_remote_copy(src_ref=src, dst_ref=lands[a].at[me], **sems),
                          pltpu.make_async_remote_copy(src_ref=src, dst_ref=lands[a].at[theirs], **sems)))
    return pairs


def _exchange_start_body(*refs, n, scatter):
    srcs, lands = refs[:n], refs[n:2 * n]
    send_sems, recv_sems = refs[2 * n], refs[2 * n + 1]
    token = refs[-1]
    for send, _ in _split_copies(srcs, lands, send_sems, recv_sems, n, scatter):
        send.start()
    token[...] = jnp.zeros_like(token)


def exchange_start(srcs, lands, *, scatter, name):
    n = len(srcs)
    args = [pltpu.with_memory_space_constraint(t, pltpu.HBM) for t in list(srcs) + list(lands)]
    outs = pl.pallas_call(
        functools.partial(_exchange_start_body, n=n, scatter=scatter), name=name,
        out_shape=(pltpu.SemaphoreType.DMA((7 * n,)), pltpu.SemaphoreType.DMA((7 * n,)),
                   *[pltpu.HBM(t.shape, t.dtype) for t in args],
                   jax.ShapeDtypeStruct((8, 128), F32)),
        in_specs=[HBM_ONLY] * (2 * n),
        out_specs=(SEM_SPEC, SEM_SPEC, *[HBM_ONLY] * (2 * n), pl.BlockSpec(memory_space=pltpu.VMEM)),
        input_output_aliases={i: 2 + i for i in range(2 * n)},
        compiler_params=pltpu.CompilerParams(has_side_effects=DATAFLOW_EFFECT),
    )(*args)
    return outs[0], outs[1], outs[2:2 + n], outs[2 + n:2 + 2 * n], outs[-1]


def _exchange_wait_body(*refs, n, scatter):
    srcs, lands = refs[:n], refs[n:2 * n]
    send_sems, recv_sems = refs[2 * n], refs[2 * n + 1]
    for send, recv in _split_copies(srcs, lands, send_sems, recv_sems, n, scatter):
        send.wait_send()
        recv.wait_recv()


def exchange_wait(send_sems, recv_sems, srcs, lands, after, *, scatter, name):
    n = len(srcs)
    outs = pl.pallas_call(
        functools.partial(_exchange_wait_body, n=n, scatter=scatter), name=name,
        out_shape=tuple(pltpu.HBM(t.shape, t.dtype) for t in list(srcs) + list(lands)),
        in_specs=[HBM_ONLY] * (2 * n) + [SEM_SPEC, SEM_SPEC, HBM_SPEC],
        out_specs=tuple([HBM_ONLY] * (2 * n)),
        input_output_aliases={i: i for i in range(2 * n)},
        compiler_params=pltpu.CompilerParams(has_side_effects=DATAFLOW_EFFECT),
    )(*srcs, *lands, send_sems, recv_sems, after)
    return outs[n:]


def _own_slot_only(shape_dtype, own, slot):
    land = lax.empty(shape_dtype.shape, shape_dtype.dtype)
    return lax.dynamic_update_slice(land, own[None], (slot,) + (0,) * own.ndim)


def _pad_rows(a, rows=8):
    return jnp.pad(a, ((0, rows - a.shape[0]), (0, 0)))


def _gate_rows(a_log, dt_bias):
    z = jnp.zeros((8, 128), F32)
    return z.at[0, A_HEADS:2 * A_HEADS].set(a_log[0]).at[1, A_HEADS:2 * A_HEADS].set(dt_bias[0])


def _pack_small(norm_g, a_log, a_dt_bias, a_norm_g, b_q_norm_g, b_k_norm_g):
    return jnp.concatenate([
        norm_g[0].reshape(8, 128), norm_g[1].reshape(8, 128),
        _gate_rows(a_log, a_dt_bias),
        _pad_rows(a_norm_g[0].reshape(2, 128)),
        _pad_rows(jnp.concatenate([b_q_norm_g[0], b_k_norm_g[0]], axis=0)),
    ], axis=0)


def _unpack_small(p):
    return (p[0:16].reshape(2, D_MODEL), p[16:17, A_HEADS:2 * A_HEADS], p[17:18, A_HEADS:2 * A_HEADS],
            p[24:26].reshape(1, A_DV), p[32:35][None], p[35:38][None])


def kernel(x, positions, norm_g, a_w_in, a_conv_w, a_log, a_dt_bias, a_norm_g, a_w_out, b_w_in, b_q_norm_g, b_k_norm_g, b_w_out, loss_target, m_norm_g, m_a_w_in, m_a_conv_w, m_a_log, m_a_dt_bias, m_a_norm_g, m_a_w_out, m_b_w_in, m_b_q_norm_g, m_b_k_norm_g, m_b_w_out, v_norm_g, v_a_w_in, v_a_conv_w, v_a_log, v_a_dt_bias, v_a_norm_g, v_a_w_out, v_b_w_in, v_b_q_norm_g, v_b_k_norm_g, v_b_w_out):
    n_seq, s_len, d = x.shape
    t_rows = n_seq * s_len
    n_chunks = s_len // CHUNK
    x0 = x.reshape(t_rows, d)
    target = loss_target.reshape(t_rows, d)
    my_slot = _flat(*_my_place())

    g_a_in, g_conv = all_gather([a_w_in[0].astype(BF16), _pad_rows(a_conv_w[0])], name="gather_weights_first")
    later = [a_w_out[0].astype(BF16), b_w_in[0].astype(BF16), b_w_out[0].astype(BF16)]
    lands = [_own_slot_only(jax.ShapeDtypeStruct((N_DEV,) + t.shape, t.dtype), t, my_slot) for t in later]
    w_send, w_recv, later, lands, w_token = exchange_start(later, lands, scatter=False, name="gather_weights_start")
    w_a_in = jnp.pad(g_a_in.transpose(1, 0, 2).reshape(d, A_IN), ((0, 0), (0, A_IN_PAD - A_IN)))
    conv_w8 = g_conv.transpose(1, 0, 2).reshape(8, 2 * A_QK + A_VW)

    gains_model = _pad_rows(norm_g)
    gate_prm = _gate_rows(a_log, a_dt_bias)
    gain_a_out = _pad_rows(a_norm_g)
    gains_qk = _pad_rows(jnp.concatenate([b_q_norm_g[0], b_k_norm_g[0]], axis=0))
    cos_t, sin_t = rope_tables(positions.reshape(t_rows))

    h0, h0_t = rms_fwd(x0, gains_model, 0, after=w_token)
    proj_a = mm_nn(h0, w_a_in, tn=896, name="proj_a", a_resident=True)
    gates_col, gates_row = a_gates_fwd(proj_a, gate_prm, n_seq)
    gates_row = gates_row.reshape(n_seq, 2 * A_HEADS, s_len // SUPER, 1, SUPER)
    o_a, tinv, states, og_a, q_a, k_a, v_a = gdn_fwd(proj_a, conv_w8, gates_col, gates_row, gain_a_out, n_seq)
    g_a_out, g_b_in, g_b_out = exchange_wait(w_send, w_recv, later, lands, og_a, scatter=False,
                                             name="gather_weights_wait")
    w_a_out = g_a_out.reshape(A_VW, d)
    w_b_in = g_b_in.transpose(1, 0, 2).reshape(d, B_IN_COLS)
    w_b_out = g_b_out.reshape(B_W, d)
    x1 = mm_nn(og_a, w_a_out, x0, tn=1024, name="out_a")

    h1, h1_t = rms_fwd(x1, gains_model, 1)
    proj_b = mm_nn_pieces(h1, w_b_in, name="proj_b")
    og_b, o_b, lse = attn_fwd(proj_b, cos_t, sin_t, gains_qk, n_seq)
    y = mm_nn(og_b, w_b_out, x1, tn=1024, name="out_b")

    dy, loss_blk = loss_head(y, target)

    d_og_b = mm_nt(dy, w_b_out, tn=1024, name="d_og_b")
    dw_b_out = mm_tn(og_b, dy, tn=256, out_dtype=BF16, name="dw_b_out")
    dqkv_b, dz_b, d_gains_qk = attn_bwd(proj_b, cos_t, sin_t, gains_qk, d_og_b, o_b, lse, n_seq)
    dh1 = mm_nt_pieces(dqkv_b, dz_b, w_b_in, name="dh1")
    dw_b_in = mm_tn_b_in(h1_t, dqkv_b, dz_b, out_dtype=BF16, name="dw_b_in")
    dx1, d_gain1 = rms_bwd(x1, gains_model, 1, dh1, dy)

    dw_a_out = mm_tn(og_a, dx1, tn=256, out_dtype=BF16, name="dw_a_out")
    early = [dw_b_in, dw_b_out.reshape(N_DEV, B_W // N_DEV, d), dw_a_out.reshape(N_DEV, A_VW // N_DEV, d)]
    lands = [_own_slot_only(t, lax.dynamic_index_in_dim(t, my_slot, 0, keepdims=False), my_slot) for t in early]
    g_send, g_recv, early, lands, g_token = exchange_start(early, lands, scatter=True, name="scatter_grads_start")

    d_og_a = mm_nt(dx1, w_a_out, tn=1024, name="d_og_a", after=g_token)
    d_xq, d_xk, d_xv, dgates, dz_a, d_gain_a_out, d_cq, d_ck, d_cv = gdn_bwd(
        q_a, k_a, v_a, gates_col, gates_row, tinv, states, d_og_a, o_a, proj_a, conv_w8, gain_a_out, n_seq)
    d_conv = jnp.concatenate([d_cq.sum(axis=0), d_ck.sum(axis=0), d_cv.sum(axis=0)], axis=1)
    d_gate_logits, d_gate_prm = a_gates_bwd(proj_a, gate_prm, dgates, n_seq)
    dw_a_in = jnp.concatenate([
        mm_tn_multi(h0_t, [d_xq, d_xk, d_xv, dz_a], unit=256, out_dtype=BF16, name="dw_a_in_qkvz"),
        mm_tn(h0_t, d_gate_logits, tn=128, out_dtype=BF16, name="dw_a_in_gates", a_is_transposed=True),
    ], axis=1)[:, :A_IN]
    shard_a_in = A_IN // N_DEV
    last = [dw_a_in.reshape(d, N_DEV, shard_a_in).transpose(1, 0, 2)]
    last_lands = [_own_slot_only(t, lax.dynamic_index_in_dim(t, my_slot, 0, keepdims=False), my_slot) for t in last]
    l_send, l_recv, last, last_lands, l_token = exchange_start(last, last_lands, scatter=True,
                                                               name="scatter_last_start")
    dh0 = mm_nt_multi([d_xq, d_xk, d_xv, dz_a], w_a_in, tn=1024, name="dh0_qkvz", after=l_token)
    dh0 = mm_nt(d_gate_logits, w_a_in, dh0, tn=128, col_off=A_GATE_COL, name="dh0_gates")
    dx0, d_gain0 = rms_bwd(x0, gains_model, 0, dh0, dx1)

    small = jnp.concatenate([
        d_gain0[0].reshape(8, 128), d_gain1[0].reshape(8, 128), d_gate_prm,
        _pad_rows(d_gain_a_out[0].reshape(2, 128)), d_gains_qk, loss_blk], axis=0)
    n_small = small.shape[0] - loss_blk.shape[0]
    small_srcs = [small, d_conv]
    small_lands = [_own_slot_only(jax.ShapeDtypeStruct((N_DEV,) + t.shape, t.dtype), t, my_slot)
                   for t in small_srcs]
    s_send, s_recv, small_srcs, small_lands, s_token = exchange_start(small_srcs, small_lands, scatter=False,
                                                                      name="gather_small_start")

    r_b_in, r_b_out, r_a_out = exchange_wait(g_send, g_recv, early, lands, s_token, scatter=True,
                                             name="scatter_grads_wait")
    (r_a_in,) = exchange_wait(l_send, l_recv, last, last_lands, s_token, scatter=True, name="scatter_last_wait")

    upd = {}
    upd["a_w_in"] = adamw(r_a_in, a_w_in, m_a_w_in, v_a_w_in, name="adamw_a_w_in")
    upd["a_w_out"] = adamw(r_a_out, a_w_out, m_a_w_out, v_a_w_out, name="adamw_a_w_out")
    upd["b_w_in"] = adamw(r_b_in, b_w_in, m_b_w_in, v_b_w_in, name="adamw_b_w_in")
    upd["b_w_out"] = adamw(r_b_out, b_w_out, m_b_w_out, v_b_w_out, name="adamw_b_w_out")
    r_small, r_conv = exchange_wait(s_send, s_recv, small_srcs, small_lands, upd["b_w_in"][0], scatter=False,
                                    name="gather_small_wait")
    conv_cols = a_conv_w.shape[2]
    r_conv = lax.dynamic_slice(r_conv, (0, 0, my_slot * conv_cols), (N_DEV, 8, conv_cols))
    loss = jnp.sum(r_small[:, n_small, 0])
    r_small = r_small[:, :n_small]
    upd["a_conv_w"] = [t[:, :A_CONV] for t in adamw(
        r_conv, _pad_rows(a_conv_w[0])[None], _pad_rows(m_a_conv_w[0])[None], _pad_rows(v_a_conv_w[0])[None],
        name="adamw_a_conv_w")]
    small_upd = adamw(
        r_small,
        _pack_small(norm_g, a_log, a_dt_bias, a_norm_g, b_q_norm_g, b_k_norm_g)[None],
        _pack_small(m_norm_g, m_a_log, m_a_dt_bias, m_a_norm_g, m_b_q_norm_g, m_b_k_norm_g)[None],
        _pack_small(v_norm_g, v_a_log, v_a_dt_bias, v_a_norm_g, v_b_q_norm_g, v_b_k_norm_g)[None],
        name="adamw_small")
    small_names = ("norm_g", "a_log", "a_dt_bias", "a_norm_g", "b_q_norm_g", "b_k_norm_g")
    unpacked = [_unpack_small(t[0]) for t in small_upd]
    for i, nm in enumerate(small_names):
        upd[nm] = [u[i] for u in unpacked]

    order = ("norm_g", "a_w_in", "a_conv_w", "a_log", "a_dt_bias", "a_norm_g", "a_w_out",
             "b_w_in", "b_q_norm_g", "b_k_norm_g", "b_w_out")
    outs = [loss, dx0.reshape(n_seq, s_len, d)]
    for kind in range(4):
        for nm in order:
            outs.append(upd[nm][kind])
    return tuple(outs)
```

```python
import functools
import math

import jax
import jax.numpy as jnp
from jax import lax
from jax.experimental import pallas as pl
from jax.experimental.pallas import tpu as pltpu

F32 = jnp.float32
BF16 = jnp.bfloat16

D_MODEL = 1024
EPS = 1e-6
N_DEV = 8

A_HEADS = 8
A_DK = 128
A_DV = 256
A_QK = A_HEADS * A_DK
A_VW = A_HEADS * A_DV
A_CONV = 4
CHUNK = 64
A_IN = 2 * A_QK + 2 * A_VW + 2 * A_HEADS
A_IN_PAD = 2 * A_QK + 2 * A_VW + 128
A_GATE_COL = (2 * A_QK + 2 * A_VW) // 128

B_DILATIONS = (1, 4, 16)
B_GROUPS = 3
B_HEADS = 8
B_DH = 128
B_W = B_HEADS * B_DH
B_BLOCK = 128
B_PIECES = 3 * B_GROUPS + 1
ROPE_THETA = 500000.0
ROPE_DIMS = B_DH // 4
ROPE_HALF = ROPE_DIMS // 2

ADAM_LR = 0.001
ADAM_B1 = 0.9
ADAM_B2 = 0.999
ADAM_EPS = 1e-08
ADAM_WD = 0.01
ADAM_STEP = 10

VMEM_LIMIT = 60 * 1024 * 1024


def _params(sem):
    return pltpu.CompilerParams(dimension_semantics=sem, vmem_limit_bytes=VMEM_LIMIT)


def _bf(x):
    return x.astype(BF16)


def _mm(a, b):
    return jnp.dot(_bf(a), _bf(b), preferred_element_type=F32)


def _mm_nt(a, b):
    return lax.dot_general(_bf(a), _bf(b), (((1,), (1,)), ((), ())), preferred_element_type=F32)


def _mm_tn(a, b):
    return lax.dot_general(_bf(a), _bf(b), (((0,), (0,)), ((), ())), preferred_element_type=F32)


def _split(x):
    hi = _bf(x)
    return hi, _bf(x - hi.astype(F32))


def _mm3(a, b):
    ah, al = _split(a)
    bh, bl = _split(b)
    d = functools.partial(jnp.dot, preferred_element_type=F32)
    return d(ah, bh) + (d(ah, bl) + d(al, bh))


def _colsum_as_col(z):
    zh, zl = _split(z)
    ones = jnp.ones((z.shape[0], 128), BF16)
    tn = functools.partial(lax.dot_general, dimension_numbers=(((0,), (0,)), ((), ())),
                           preferred_element_type=F32)
    return (tn(zh, ones) + tn(zl, ones))[:, 0:1]


def _sigmoid(x):
    return 0.5 * jnp.tanh(0.5 * x) + 0.5


INV_BASE = 8
INV_NEWTON = 2
GDN_GROUP = 4
SUPER = GDN_GROUP * CHUNK
GDN_WIDTH = 2

A_K_COL = A_QK // A_DK
A_V_COL = 2 * A_QK // A_DV


def _inverse_steps(m, row, col):
    eye = (row == col).astype(F32)
    d = jnp.where(row // INV_BASE == col // INV_BASE, m, 0.0)
    x = eye - d
    p = _mm(d, d)
    yield
    steps = int(math.log2(INV_BASE)) - 1
    for i in range(steps):
        x = x + _mm(x, p)
        if i + 1 < steps:
            p = _mm(p, p)
        yield
    size = INV_BASE
    while size < CHUNK:
        c = jnp.where((row // (2 * size) == col // (2 * size)) & (row // size != col // size), m, 0.0)
        xc = _mm(x, c)
        yield
        x = x - _mm(xc, x)
        yield
        size *= 2
    for _ in range(INV_NEWTON):
        r = eye - x - _mm3(m, x)
        yield
        x = x + _mm(x, r)
        yield
    return x


def _drain(gen):
    while True:
        try:
            next(gen)
        except StopIteration as stop:
            return stop.value


def _interleave(*gens):
    live = list(gens)
    while live:
        for g in list(live):
            try:
                next(g)
            except StopIteration:
                live.remove(g)


def _diag_blocks_tall(x):
    return jnp.concatenate([x[i * CHUNK:(i + 1) * CHUNK, i * CHUNK:(i + 1) * CHUNK] for i in range(GDN_GROUP)], axis=0)


def _tall_to_block_diag(t, same):
    return jnp.where(same, jnp.concatenate([t] * GDN_GROUP, axis=1), 0.0)


def _block_sum(same, x):
    xh, xl = _split(jnp.broadcast_to(x, (SUPER, 128)))
    ones = same.astype(BF16)
    d = functools.partial(jnp.dot, preferred_element_type=F32)
    return (d(ones, xh) + d(ones, xl))[:, 0:1]


def _aligned_rows(index, size):
    start = index * size
    return pl.ds(start if isinstance(start, int) else pl.multiple_of(start, size), size)


def _super_rows(i):
    return _aligned_rows(i, SUPER)


def _chunk_rows(n):
    return _aligned_rows(n, CHUNK)


def _gdn_super_steps(q, k, v, gcb, gr, head, tinv_tall=None, need_m=True):
    lane = lax.broadcasted_iota(jnp.int32, (SUPER, 128), 1)
    row = lax.broadcasted_iota(jnp.int32, (SUPER, SUPER), 0)
    col = lax.broadcasted_iota(jnp.int32, (SUPER, SUPER), 1)
    same = row // CHUNK == col // CHUNK
    beta = jnp.sum(jnp.where(lane == head, gcb, 0.0), axis=1, keepdims=True)
    gc = jnp.sum(jnp.where(lane == A_HEADS + head, gcb, 0.0), axis=1, keepdims=True)
    g_last = jnp.sum(jnp.where(col == (row // CHUNK) * CHUNK + (CHUNK - 1), gr, 0.0), axis=1, keepdims=True)
    gamma = jnp.exp(gc)
    decay = jnp.where(same & (row >= col), jnp.exp(jnp.minimum(gc - gr, 0.0)), 0.0)
    kb = k * beta
    m = jnp.where(same & (row > col), _mm_nt(kb, k) * decay, 0.0) if need_m else None
    p = jnp.where(same & (row >= col), _mm_nt(q, k) * decay, 0.0)
    yield
    if tinv_tall is None:
        tinv = yield from _inverse_steps(m, row, col)
    else:
        tinv = _tall_to_block_diag(tinv_tall, same)
    u = _mm(tinv, v * beta)
    w = _mm(tinv, kb * gamma)
    yield
    e_tail = jnp.exp(g_last - gc)
    return dict(beta=beta, gc=gc, g_last=g_last, gamma=gamma, decay=decay, kb=kb, m=m,
                tinv=tinv, u=u, w=w, p=p, e_tail=e_tail, row=row, col=col, lane=lane, same=same)


def _store_scan_operands(rows, q, k, t, u_scr, w_scr, p_scr, qg_scr, ke_scr, gl_scr):
    u_scr[rows, :] = t["u"]
    w_scr[rows, :] = _bf(t["w"])
    p_scr[rows, :] = _bf(_diag_blocks_tall(t["p"]))
    qg_scr[rows, :] = _bf(q * t["gamma"])
    ke_scr[rows, :] = _bf(k * t["e_tail"])
    gl_scr[rows, :] = jnp.broadcast_to(jnp.exp(t["g_last"]), (SUPER, 128))


def _gdn_fwd_body(q_ref, k_ref, v_ref, gc_ref, gr_ref, wq_ref, wk_ref, wv_ref, z_ref, gn_ref,
                  o_ref, tinv_ref, st_ref, og_ref, qo_ref, ko_ref, vo_ref, s_scr, *sets):
    head = pl.program_id(1)
    n_super = q_ref.shape[0] // SUPER
    all_sets = [sets[6 * i:6 * i + 6] for i in range(2 * GDN_WIDTH)]
    whole = pl.ds(0, SUPER)

    def conv_silu(x_ref, w_ref, i):
        rows = _super_rows(i)
        x, w = x_ref[rows, :], w_ref[...]
        halo = jnp.zeros((8, x.shape[1]), F32) if i == 0 else x_ref[pl.ds(i * SUPER - 8, 8), :]
        ext = jnp.concatenate([halo, x], axis=0)
        c = x * w[A_CONV - 1:A_CONV, :]
        for j in range(1, A_CONV):
            c = c + pltpu.roll(ext, j, 0)[8:, :] * w[A_CONV - 1 - j:A_CONV - j, :]
        return c * _sigmoid(c)

    def unit(a):
        return a * lax.rsqrt(jnp.sum(a * a, axis=1, keepdims=True) + EPS)

    def prepare_steps(i, dst):
        rows = _super_rows(i)
        q = unit(conv_silu(q_ref, wq_ref, i)) * A_DK ** -0.5
        k = unit(conv_silu(k_ref, wk_ref, i))
        v = conv_silu(v_ref, wv_ref, i)
        qo_ref[rows, :], ko_ref[rows, :], vo_ref[rows, :] = q, k, v
        t = yield from _gdn_super_steps(q, k, v, gc_ref[rows, :], gr_ref[i], head)
        tinv_ref[rows, :] = _diag_blocks_tall(t["tinv"])
        _store_scan_operands(whole, q, k, t, *dst)

    def scan_steps(i, src):
        u_scr, w_scr, p_scr, qg_scr, ke_scr, gl_scr = src
        for j in range(GDN_GROUP):
            n = i * GDN_GROUP + j
            local = pl.ds(j * CHUNK, CHUNK)
            s = s_scr[...]
            sb = _bf(s)
            st_ref[n] = sb
            ws = jnp.dot(w_scr[local, :], sb, preferred_element_type=F32)
            yield
            vb = _bf(u_scr[local, :] - ws)
            o = (jnp.dot(qg_scr[local, :], sb, preferred_element_type=F32)
                 + jnp.dot(p_scr[local, :], vb, preferred_element_type=F32))
            s_new = s * gl_scr[local, :][0:1, 0:1] + lax.dot_general(
                ke_scr[local, :], vb, (((0,), (0,)), ((), ())), preferred_element_type=F32)
            yield
            rows = _chunk_rows(n)
            o_ref[rows, :] = o
            s_scr[...] = s_new
            silu, _ = _silu_parts(z_ref[rows, :])
            r = lax.rsqrt(jnp.mean(o * o, axis=1, keepdims=True) + EPS)
            og_ref[rows, :] = ((o * r * gn_ref[0:1, :]) * silu).astype(og_ref.dtype)

    def scan_many(first, srcs):
        for j, src in enumerate(srcs):
            yield from scan_steps(first + j, src)

    groups = [all_sets[:GDN_WIDTH], all_sets[GDN_WIDTH:]]
    _interleave(*[prepare_steps(j, groups[0][j]) for j in range(GDN_WIDTH)])
    s_scr[...] = jnp.zeros_like(s_scr)
    for g in range(n_super // GDN_WIDTH):
        cur, nxt = groups[g % 2], groups[(g + 1) % 2]
        first = g * GDN_WIDTH
        following = [prepare_steps(first + GDN_WIDTH + j, nxt[j]) for j in range(GDN_WIDTH)
                     if first + GDN_WIDTH + j < n_super]
        _interleave(scan_many(first, cur), *following)


def _gdn_in_specs(s_len, n_super, from_proj):
    k_col, v_col = (A_K_COL, A_V_COL) if from_proj else (0, 0)
    return [
        pl.BlockSpec((s_len, A_DK), lambda b, h: (b, h)),
        pl.BlockSpec((s_len, A_DK), lambda b, h: (b, k_col + h)),
        pl.BlockSpec((s_len, A_DV), lambda b, h: (b, v_col + h)),
        pl.BlockSpec((s_len, 128), lambda b, h: (b, 0)),
        pl.BlockSpec((None, None, n_super, 1, SUPER), lambda b, h: (b, A_HEADS + h, 0, 0, 0)),
    ]


def _gdn_scan_scratch(s_len):
    return [pltpu.VMEM((A_DK, A_DV), F32), pltpu.VMEM((s_len, A_DV), F32),
            pltpu.VMEM((s_len, A_DK), BF16), pltpu.VMEM((s_len, CHUNK), BF16),
            pltpu.VMEM((s_len, A_DK), BF16), pltpu.VMEM((s_len, A_DK), BF16),
            pltpu.VMEM((s_len, 128), F32)]


def gdn_fwd(proj_a, conv_w8, gates_col, gates_row, norm_g8, n_seq):
    t_rows = proj_a.shape[0]
    s_len = t_rows // n_seq
    n_chunks = s_len // CHUNK
    qk_spec = pl.BlockSpec((s_len, A_DK), lambda b, h: (b, h))
    v_spec = pl.BlockSpec((s_len, A_DV), lambda b, h: (b, h))
    return pl.pallas_call(
        _gdn_fwd_body, name="gdn_fwd", grid=(n_seq, A_HEADS),
        in_specs=_gdn_in_specs(s_len, s_len // SUPER, True) + [
            pl.BlockSpec((8, A_DK), lambda b, h: (0, h)),
            pl.BlockSpec((8, A_DK), lambda b, h: (0, A_K_COL + h)),
            pl.BlockSpec((8, A_DV), lambda b, h: (0, A_V_COL + h)),
            pl.BlockSpec((s_len, A_DV), lambda b, h: (b, A_Z_COL + h)),
            pl.BlockSpec((8, A_DV), lambda b, h: (0, 0)),
        ],
        out_specs=[
            v_spec,
            pl.BlockSpec((s_len, CHUNK), lambda b, h: (b * A_HEADS + h, 0)),
            pl.BlockSpec((None, n_chunks, A_DK, A_DV), lambda b, h: (b * A_HEADS + h, 0, 0, 0)),
            v_spec, qk_spec, qk_spec, v_spec,
        ],
        out_shape=[
            jax.ShapeDtypeStruct((t_rows, A_VW), F32),
            jax.ShapeDtypeStruct((n_seq * A_HEADS * s_len, CHUNK), F32),
            jax.ShapeDtypeStruct((n_seq * A_HEADS, n_chunks, A_DK, A_DV), BF16),
            jax.ShapeDtypeStruct((t_rows, A_VW), BF16),
            jax.ShapeDtypeStruct((t_rows, A_QK), F32),
            jax.ShapeDtypeStruct((t_rows, A_QK), F32),
            jax.ShapeDtypeStruct((t_rows, A_VW), F32),
        ],
        scratch_shapes=_gdn_scan_scratch(SUPER) + (2 * GDN_WIDTH - 1) * _gdn_scan_scratch(SUPER)[1:],
        compiler_params=_params(("arbitrary", "arbitrary")),
    )(proj_a, proj_a, proj_a, gates_col, gates_row, conv_w8, conv_w8, conv_w8, proj_a, norm_g8)


def _gdn_bwd_body(q_ref, k_ref, v_ref, gc_ref, gr_ref, tinv_ref, st_ref, dog_ref, oa_ref, z_ref, gn_ref,
                  xq_ref, xk_ref, xv_ref, wq_ref, wk_ref, wv_ref,
                  dxq_ref, dxk_ref, dxv_ref, dgc_ref, dz_ref, dgn_ref, dwq_ref, dwk_ref, dwv_ref,
                  ds_scr, cq_scr, ck_scr, cv_scr, *sets):
    head = pl.program_id(1)
    n_super = q_ref.shape[0] // SUPER
    ops = (sets[0:7], sets[7:14])
    res = (sets[14:21], sets[21:28])
    whole = pl.ds(0, SUPER)
    tn = functools.partial(lax.dot_general, dimension_numbers=(((0,), (0,)), ((), ())), preferred_element_type=F32)
    nt = functools.partial(lax.dot_general, dimension_numbers=(((1,), (1,)), ((), ())), preferred_element_type=F32)

    @pl.when(head == 0)
    def _():
        dgc_ref[...] = jnp.zeros_like(dgc_ref)

    @pl.when((head == 0) & (pl.program_id(0) == 0))
    def _():
        dgn_ref[...] = jnp.zeros_like(dgn_ref)

    carry = (cq_scr, ck_scr, cv_scr)
    for ref in carry + (dwq_ref, dwk_ref, dwv_ref):
        ref[...] = jnp.zeros_like(ref)

    def common_steps(i, need_m=True):
        rows = _super_rows(i)
        q, k, v = q_ref[rows, :], k_ref[rows, :], v_ref[rows, :]
        t = yield from _gdn_super_steps(q, k, v, gc_ref[rows, :], gr_ref[i], head, tinv_tall=tinv_ref[rows, :],
                                        need_m=need_m)
        return rows, q, k, v, t

    def stage_p(i, parity):
        rows, q, k, _, t = yield from common_steps(i, need_m=False)
        _store_scan_operands(whole, q, k, t, *ops[parity][:6])
        o, d_og, gain = oa_ref[rows, :], dog_ref[rows, :], gn_ref[0:1, :]
        r = lax.rsqrt(jnp.mean(o * o, axis=1, keepdims=True) + EPS)
        silu, dsilu = _silu_parts(z_ref[rows, :])
        xr = o * r
        d_on = d_og * silu
        dz_ref[rows, :] = (d_og * (xr * gain) * dsilu).astype(dz_ref.dtype)
        u = d_on * gain
        ops[parity][6][...] = r * u - xr * (r * r) * jnp.mean(u * o, axis=1, keepdims=True)
        dgn_ref[0:1, :] += jnp.sum(d_on * xr, axis=0, keepdims=True)

    def stage_s(i, parity):
        u_scr, w_scr, p_scr, qg_scr, ke_scr, gl_scr, do_scr = ops[parity]
        vn_scr, dvn_scr, dqg_scr, dw_scr, dkt_scr, sds_scr, dof_scr = res[parity]
        for j in reversed(range(GDN_GROUP)):
            n = i * GDN_GROUP + j
            local = pl.ds(j * CHUNK, CHUNK)
            ds_next = ds_scr[...]
            dsb = _bf(ds_next)
            sb = st_ref[n]
            s = sb.astype(F32)
            d_o = do_scr[local, :]
            dof_scr[local, :] = d_o
            d_ob = _bf(d_o)
            w_s = jnp.dot(w_scr[local, :], sb, preferred_element_type=F32)
            d_vn = tn(p_scr[local, :], d_ob) + jnp.dot(ke_scr[local, :], dsb, preferred_element_type=F32)
            d_qg = nt(d_ob, sb)
            qg_do = tn(qg_scr[local, :], d_ob)
            yield
            v_new = u_scr[local, :] - w_s
            d_vnb = _bf(d_vn)
            d_w = -nt(d_vnb, sb)
            d_kt = nt(_bf(v_new), dsb)
            w_dvn = tn(w_scr[local, :], d_vnb)
            yield
            vn_scr[local, :] = v_new
            dvn_scr[local, :] = d_vn
            dqg_scr[local, :] = d_qg
            dw_scr[local, :] = d_w
            dkt_scr[local, :] = d_kt
            sds = jnp.sum(jnp.sum(s * ds_next, axis=1, keepdims=True), axis=0, keepdims=True)
            sds_scr[local, :] = jnp.broadcast_to(sds, (CHUNK, 128))
            ds_scr[...] = qg_do + gl_scr[local, :][0:1, 0:1] * ds_next - w_dvn

    def conv_bwd(i, rows, x_ref, w_ref, dy, norm_scale, dx_ref, dw_ref, dc_above):
        x, w = x_ref[rows, :], w_ref[...]
        above = x_ref[pl.ds(pl.multiple_of(jnp.maximum(i * SUPER - 8, 0), 8), 8), :]
        ext = jnp.concatenate([jnp.where(i > 0, above, 0.0), x], axis=0)
        c = x * w[A_CONV - 1:A_CONV, :]
        for j in range(1, A_CONV):
            c = c + pltpu.roll(ext, j, 0)[8:, :] * w[A_CONV - 1 - j:A_CONV - j, :]
        sig = _sigmoid(c)
        a = c * sig
        if norm_scale is None:
            da = dy
        else:
            rn = lax.rsqrt(jnp.sum(a * a, axis=1, keepdims=True) + EPS)
            da = norm_scale * (rn * dy - a * (rn * rn * rn) * jnp.sum(dy * a, axis=1, keepdims=True))
        dc = da * (sig * (1.0 + c * (1.0 - sig)))
        ext_dc = jnp.concatenate([dc, dc_above[...]], axis=0)
        dc_above[...] = dc[0:8, :]
        dx = dc * w[A_CONV - 1:A_CONV, :]
        dw_ref[A_CONV - 1:A_CONV, :] += jnp.sum(dc * x, axis=0, keepdims=True)
        for j in range(1, A_CONV):
            dcs = pltpu.roll(ext_dc, SUPER + 8 - j, 0)[:SUPER, :]
            dx = dx + dcs * w[A_CONV - 1 - j:A_CONV - j, :]
            dw_ref[A_CONV - 1 - j:A_CONV - j, :] += jnp.sum(dcs * x, axis=0, keepdims=True)
        dx_ref[rows, :] = dx.astype(dx_ref.dtype)

    def stage_f(i, parity):
        vn_scr, dvn_scr, dqg_scr, dw_scr, dkt_scr, sds_scr, dof_scr = res[parity]
        rows, q, k, v, t = yield from common_steps(i)
        beta, gamma, decay, kb, e_tail = t["beta"], t["gamma"], t["decay"], t["kb"], t["e_tail"]
        row, col, lane, same = t["row"], t["col"], t["lane"], t["same"]
        d_o = dof_scr[...]
        v_new, d_vn = vn_scr[...], dvn_scr[...]
        d_qg, d_w, d_kt = dqg_scr[...], dw_scr[...], dkt_scr[...]
        gamma_last = jnp.exp(t["g_last"])

        d_p = jnp.where(same & (row >= col), _mm_nt(d_o, v_new), 0.0)
        d_ru = _mm_tn(t["tinv"], d_vn)
        d_rw = _mm_tn(t["tinv"], d_w)
        yield
        d_m = jnp.where(same & (row > col), -(_mm_nt(d_ru, t["u"]) + _mm_nt(d_rw, t["w"])), 0.0)
        yield

        x_p = d_p * decay
        y_m = d_m * decay
        d_kb = _mm(y_m, k) + d_rw * gamma
        d_q = _mm(x_p, k) + d_qg * gamma
        d_k = _mm_tn(x_p, q) + _mm_tn(y_m, kb) + d_kb * beta + d_kt * e_tail
        d_v = d_ru * beta
        conv_bwd(i, rows, xq_ref, wq_ref, d_q, A_DK ** -0.5, dxq_ref, dwq_ref, carry[0])
        conv_bwd(i, rows, xk_ref, wk_ref, d_k, 1.0, dxk_ref, dwk_ref, carry[1])
        conv_bwd(i, rows, xv_ref, wv_ref, d_v, None, dxv_ref, dwv_ref, carry[2])

        d_beta = (jnp.sum(d_ru * v, axis=1, keepdims=True)
                  + jnp.sum(d_kb * k, axis=1, keepdims=True))
        z = d_p * t["p"] + d_m * t["m"]
        eps_tail = jnp.sum(d_kt * k, axis=1, keepdims=True) * e_tail
        d_gc = (jnp.sum(z, axis=1, keepdims=True) - _colsum_as_col(z)
                + jnp.sum(d_qg * q, axis=1, keepdims=True) * gamma
                + jnp.sum(d_rw * kb, axis=1, keepdims=True) * gamma
                - eps_tail)
        d_glast = _block_sum(same, eps_tail) + gamma_last * sds_scr[...][:, 0:1]
        yield
        rcol = lax.broadcasted_iota(jnp.int32, (SUPER, 1), 0)
        d_gc = d_gc + jnp.where(rcol % CHUNK == CHUNK - 1, d_glast, 0.0)
        dgc_ref[rows, :] += (jnp.where(lane == head, d_beta, 0.0)
                             + jnp.where(lane == A_HEADS + head, d_gc, 0.0))

    last = n_super - 1
    _drain(stage_p(last, 1))
    ds_scr[...] = jnp.zeros_like(ds_scr)
    _interleave(stage_s(last, 1), stage_p(last - 1, 0))

    def pair(k, carry):
        i = last - 1 - 2 * k
        _interleave(stage_s(i, 0), stage_f(i + 1, 1), stage_p(i - 1, 1))
        _interleave(stage_s(i - 1, 1), stage_f(i, 0), stage_p(i - 2, 0))
        return carry

    lax.fori_loop(0, n_super // 2 - 1, pair, 0)
    _interleave(stage_s(0, 0), stage_f(1, 1))
    _drain(stage_f(0, 0))


def gdn_bwd(q, k, v, gates_col, gates_row, tinv, states, d_og, o, proj_a, conv_w8, norm_g8, n_seq):
    t_rows = q.shape[0]
    s_len = t_rows // n_seq
    n_chunks = s_len // CHUNK
    qk_spec = pl.BlockSpec((s_len, A_DK), lambda b, h: (b, h))
    v_spec = pl.BlockSpec((s_len, A_DV), lambda b, h: (b, h))
    gate_spec = pl.BlockSpec((s_len, 128), lambda b, h: (b, 0))
    gain_spec = pl.BlockSpec((8, A_DV), lambda b, h: (0, 0))
    dw_qk_spec = pl.BlockSpec((None, 8, A_DK), lambda b, h: (b, 0, h))
    dw_v_spec = pl.BlockSpec((None, 8, A_DV), lambda b, h: (b, 0, h))
    ops_set = _gdn_scan_scratch(SUPER)[1:] + [pltpu.VMEM((SUPER, A_DV), F32)]
    res_set = [pltpu.VMEM((SUPER, A_DV), F32), pltpu.VMEM((SUPER, A_DV), F32),
               pltpu.VMEM((SUPER, A_DK), F32), pltpu.VMEM((SUPER, A_DK), F32),
               pltpu.VMEM((SUPER, A_DK), F32), pltpu.VMEM((SUPER, 128), F32), pltpu.VMEM((SUPER, A_DV), F32)]
    return pl.pallas_call(
        _gdn_bwd_body, name="gdn_bwd", grid=(n_seq, A_HEADS),
        in_specs=_gdn_in_specs(s_len, s_len // SUPER, False) + [
            pl.BlockSpec((s_len, CHUNK), lambda b, h: (b * A_HEADS + h, 0)),
            pl.BlockSpec((None, n_chunks, A_DK, A_DV), lambda b, h: (b * A_HEADS + h, 0, 0, 0)),
            v_spec, v_spec,
            pl.BlockSpec((s_len, A_DV), lambda b, h: (b, A_Z_COL + h)),
            gain_spec,
            pl.BlockSpec((s_len, A_DK), lambda b, h: (b, h)),
            pl.BlockSpec((s_len, A_DK), lambda b, h: (b, A_K_COL + h)),
            pl.BlockSpec((s_len, A_DV), lambda b, h: (b, A_V_COL + h)),
            pl.BlockSpec((8, A_DK), lambda b, h: (0, h)),
            pl.BlockSpec((8, A_DK), lambda b, h: (0, A_K_COL + h)),
            pl.BlockSpec((8, A_DV), lambda b, h: (0, A_V_COL + h)),
        ],
        out_specs=[qk_spec, qk_spec, v_spec, gate_spec, v_spec, gain_spec, dw_qk_spec, dw_qk_spec, dw_v_spec],
        out_shape=[
            jax.ShapeDtypeStruct((t_rows, A_QK), BF16),
            jax.ShapeDtypeStruct((t_rows, A_QK), BF16),
            jax.ShapeDtypeStruct((t_rows, A_VW), BF16),
            jax.ShapeDtypeStruct((t_rows, 128), F32),
            jax.ShapeDtypeStruct((t_rows, A_VW), BF16),
            jax.ShapeDtypeStruct((8, A_DV), F32),
            jax.ShapeDtypeStruct((n_seq, 8, A_QK), F32),
            jax.ShapeDtypeStruct((n_seq, 8, A_QK), F32),
            jax.ShapeDtypeStruct((n_seq, 8, A_VW), F32),
        ],
        scratch_shapes=(_gdn_scan_scratch(SUPER)[:1]
                        + [pltpu.VMEM((8, A_DK), F32), pltpu.VMEM((8, A_DK), F32), pltpu.VMEM((8, A_DV), F32)]
                        + 2 * ops_set + 2 * res_set),
        compiler_params=_params(("arbitrary", "arbitrary")),
    )(q, k, v, gates_col, gates_row, tinv, states, d_og, o, proj_a, norm_g8,
      proj_a, proj_a, proj_a, conv_w8, conv_w8, conv_w8)


GATE_TILE = 512


def _softplus(y):
    return jnp.maximum(y, 0.0) + jnp.log1p(jnp.exp(-jnp.abs(y)))


def _gate_values(x, prm):
    beta = _sigmoid(x)
    y = x + prm[1:2, :]
    neg_a = -jnp.exp(prm[0:1, :])
    g = neg_a * _softplus(y)
    return beta, y, neg_a, g


def _a_gates_fwd_body(x_ref, prm_ref, gc_ref, gr_ref):
    x = x_ref[...]
    tm = x.shape[0]
    beta, _, _, g = _gate_values(x, prm_ref[...])
    in_chunk = lax.broadcasted_iota(jnp.int32, (tm, 1), 0) % CHUNK
    s = 1
    while s < CHUNK:
        g = g + jnp.where(in_chunk >= s, pltpu.roll(g, s, 0), 0.0)
        s *= 2
    lane = lax.broadcasted_iota(jnp.int32, x.shape, 1)
    out = jnp.where(lane < A_HEADS, beta, jnp.where(lane < 2 * A_HEADS, g, 0.0))
    gc_ref[...] = out
    gr_ref[...] = out.T[0:2 * A_HEADS, :]


def a_gates_fwd(proj_a, prm, n_seq):
    t_rows = proj_a.shape[0]
    s_len = t_rows // n_seq
    tm = min(GATE_TILE, s_len)
    n_t = s_len // tm
    return pl.pallas_call(
        _a_gates_fwd_body, name="a_gates_fwd", grid=(n_seq, n_t),
        in_specs=[pl.BlockSpec((tm, 128), lambda b, i: (b * n_t + i, A_GATE_COL)),
                  pl.BlockSpec((8, 128), lambda b, i: (0, 0))],
        out_specs=[pl.BlockSpec((tm, 128), lambda b, i: (b * n_t + i, 0)),
                   pl.BlockSpec((None, 2 * A_HEADS, tm), lambda b, i: (b, 0, i))],
        out_shape=[jax.ShapeDtypeStruct((t_rows, 128), F32),
                   jax.ShapeDtypeStruct((n_seq, 2 * A_HEADS, s_len), F32)],
        compiler_params=_params(("arbitrary", "arbitrary")),
    )(proj_a, prm)


def _a_gates_bwd_body(x_ref, prm_ref, dgc_ref, dx_ref, dprm_ref):
    first = (pl.program_id(0) == 0) & (pl.program_id(1) == 0)
    x = x_ref[...]
    tm = x.shape[0]
    beta, y, neg_a, g = _gate_values(x, prm_ref[...])
    d = dgc_ref[...]
    in_chunk = lax.broadcasted_iota(jnp.int32, (tm, 1), 0) % CHUNK
    dg = d
    s = 1
    while s < CHUNK:
        dg = dg + jnp.where(in_chunk < CHUNK - s, pltpu.roll(dg, tm - s, 0), 0.0)
        s *= 2
    lane = lax.broadcasted_iota(jnp.int32, x.shape, 1)
    is_decay = (lane >= A_HEADS) & (lane < 2 * A_HEADS)
    d_alogit = jnp.where(is_decay, dg * neg_a * _sigmoid(y), 0.0)
    dx_ref[...] = jnp.where(lane < A_HEADS, d * beta * (1.0 - beta), d_alogit).astype(dx_ref.dtype)

    @pl.when(first)
    def _():
        dprm_ref[...] = jnp.zeros_like(dprm_ref)

    dprm_ref[0:1, :] += jnp.sum(jnp.where(is_decay, dg * g, 0.0), axis=0, keepdims=True)
    dprm_ref[1:2, :] += jnp.sum(d_alogit, axis=0, keepdims=True)


def a_gates_bwd(proj_a, prm, dgates_col, n_seq):
    t_rows = proj_a.shape[0]
    s_len = t_rows // n_seq
    tm = min(GATE_TILE, s_len)
    n_t = s_len // tm
    return pl.pallas_call(
        _a_gates_bwd_body, name="a_gates_bwd", grid=(n_seq, n_t),
        in_specs=[pl.BlockSpec((tm, 128), lambda b, i: (b * n_t + i, A_GATE_COL)),
                  pl.BlockSpec((8, 128), lambda b, i: (0, 0)),
                  pl.BlockSpec((tm, 128), lambda b, i: (b * n_t + i, 0))],
        out_specs=[pl.BlockSpec((tm, 128), lambda b, i: (b * n_t + i, 0)),
                   pl.BlockSpec((8, 128), lambda b, i: (0, 0))],
        out_shape=[jax.ShapeDtypeStruct((t_rows, 128), BF16),
                   jax.ShapeDtypeStruct((8, 128), F32)],
        compiler_params=_params(("arbitrary", "arbitrary")),
    )(proj_a, prm, dgates_col)


ROW_TILE = 1024
A_Z_COL = (2 * A_QK + A_VW) // A_DV


def _silu_parts(z):
    sig = _sigmoid(z)
    return z * sig, sig * (1.0 + z * (1.0 - sig))


NEG_BIG = -1e30
ATT_SCALE = B_DH ** -0.5


def _swap_rope_halves(x):
    src = lax.broadcasted_iota(jnp.int32, (B_DH, B_DH), 0)
    dst = lax.broadcasted_iota(jnp.int32, (B_DH, B_DH), 1)
    pick = ((dst < ROPE_HALF) & (src == dst + ROPE_HALF)) | (
        (dst >= ROPE_HALF) & (dst < ROPE_DIMS) & (src == dst - ROPE_HALF))
    return jnp.dot(_bf(x), pick.astype(BF16), preferred_element_type=F32)


def _norm_rope(x, gain, cos_t, sin_t):
    r = lax.rsqrt(jnp.mean(x * x, axis=1, keepdims=True) + EPS)
    xn = x * r * gain
    return xn * cos_t + _swap_rope_halves(xn) * sin_t, r


def _norm_rope_bwd(x, r, gain, cos_t, sin_t, dy):
    d_xn = dy * cos_t + _swap_rope_halves(dy * sin_t)
    xr = x * r
    u = d_xn * gain
    dx = r * u - xr * (r * r) * jnp.mean(u * x, axis=1, keepdims=True)
    return dx, jnp.sum(d_xn * xr, axis=0, keepdims=True)


def _stream_rows(idx, dilation, s_len):
    nb = s_len // dilation // B_BLOCK
    r = idx // nb
    m = idx % nb
    cur = r + m * (B_BLOCK * dilation)
    prev = r + jnp.maximum(m - 1, 0) * (B_BLOCK * dilation)
    return cur, prev, m > 0


def _rows(start, dilation):
    if dilation == 1:
        return pl.ds(start, B_BLOCK)
    return pl.ds(start, B_BLOCK, stride=dilation)


ATT_UNROLL = 16


def _band_mask(has_prev):
    qi = lax.broadcasted_iota(jnp.int32, (B_BLOCK, 2 * B_BLOCK), 0)
    kj = lax.broadcasted_iota(jnp.int32, (B_BLOCK, 2 * B_BLOCK), 1)
    return ((kj < B_BLOCK) & (kj >= qi) & has_prev) | ((kj >= B_BLOCK) & (kj - B_BLOCK <= qi))


def _attn_fwd_body(qkv_ref, z_ref, cos_ref, sin_ref, gain_ref, og_ref, o_ref, lse_ref,
                   qn_scr, kn_scr, og_scr, lg_scr):
    head, grp = pl.program_id(1), pl.program_id(2)
    s_len = z_ref.shape[0]
    n_blocks = s_len // B_BLOCK
    cos_t, sin_t = cos_ref[...], sin_ref[...]

    for gi, dil in enumerate(B_DILATIONS):
        @pl.when(grp == gi)
        def _(gi=gi, dil=dil):
            qn_scr[...], _ = _norm_rope(qkv_ref[0], gain_ref[gi:gi + 1, :], cos_t, sin_t)
            kn_scr[...], _ = _norm_rope(qkv_ref[1], gain_ref[B_GROUPS + gi:B_GROUPS + gi + 1, :], cos_t, sin_t)

            ones = jnp.ones((2 * B_BLOCK, B_DH), BF16)

            def blocks(it, carry):
                scored = []
                for j in range(ATT_UNROLL):
                    cur, prev, has_prev = _stream_rows(it * ATT_UNROLL + j, dil, s_len)
                    rc, rp = _rows(cur, dil), _rows(prev, dil)
                    k2 = jnp.concatenate([kn_scr[rp, :], kn_scr[rc, :]], axis=0)
                    scored.append((rc, rp, has_prev, _mm_nt(qn_scr[rc, :], k2) * ATT_SCALE))
                summed = []
                for rc, rp, has_prev, s in scored:
                    s = jnp.where(_band_mask(has_prev), s, NEG_BIG)
                    mx = jnp.max(s, axis=1, keepdims=True)
                    v2 = jnp.concatenate([qkv_ref.at[2][rp, :], qkv_ref.at[2][rc, :]], axis=0)
                    acc = jnp.dot(_bf(jnp.exp(s - mx)), jnp.concatenate([_bf(v2), ones], axis=1),
                                  preferred_element_type=F32)
                    summed.append((rc, mx, acc))
                for rc, mx, acc in summed:
                    den = acc[:, B_DH:B_DH + 1]
                    og_scr.at[gi][rc, :] = acc[:, :B_DH] / den
                    lg_scr.at[gi][rc, :] = jnp.broadcast_to(mx + jnp.log(den), (B_BLOCK, B_DH))
                return carry

            lax.fori_loop(0, n_blocks // ATT_UNROLL, blocks, 0)

    @pl.when(grp == B_GROUPS - 1)
    def _():
        l0, l1, l2 = lg_scr[0], lg_scr[1], lg_scr[2]
        mx = jnp.maximum(jnp.maximum(l0, l1), l2)
        w0, w1, w2 = jnp.exp(l0 - mx), jnp.exp(l1 - mx), jnp.exp(l2 - mx)
        den = w0 + w1 + w2
        o = (w0 * og_scr[0] + w1 * og_scr[1] + w2 * og_scr[2]) / den
        silu, _ = _silu_parts(z_ref[...])
        o_ref[...] = o
        og_ref[...] = (o * silu).astype(og_ref.dtype)
        @pl.when(head == 0)
        def _():
            lse_ref[...] = jnp.zeros_like(lse_ref)

        lane = lax.broadcasted_iota(jnp.int32, o.shape, 1)
        lse_ref[...] = jnp.where(lane == head, mx + jnp.log(den), lse_ref[...])


def attn_fwd(proj_b, cos_t, sin_t, gains8, n_seq):
    t_rows = proj_b.shape[1]
    s_len = t_rows // n_seq
    head_blk = pl.BlockSpec((s_len, B_DH), lambda b, h, g: (b, h))
    seq_blk = pl.BlockSpec((s_len, 128), lambda b, h, g: (b, 0))
    return pl.pallas_call(
        _attn_fwd_body, name="attn_fwd", grid=(n_seq, B_HEADS, B_GROUPS),
        in_specs=[
            pl.BlockSpec((3, s_len, B_DH), lambda b, h, g: (g, b, h)),
            pl.BlockSpec((None, s_len, B_DH), lambda b, h, g: (B_PIECES - 1, b, h)),
            seq_blk, seq_blk,
            pl.BlockSpec((8, 128), lambda b, h, g: (0, 0)),
        ],
        out_specs=[head_blk, head_blk, seq_blk],
        out_shape=[jax.ShapeDtypeStruct((t_rows, B_W), BF16),
                   jax.ShapeDtypeStruct((t_rows, B_W), F32),
                   jax.ShapeDtypeStruct((t_rows, 128), F32)],
        scratch_shapes=[pltpu.VMEM((s_len, B_DH), F32), pltpu.VMEM((s_len, B_DH), F32),
                        pltpu.VMEM((B_GROUPS, s_len, B_DH), F32), pltpu.VMEM((B_GROUPS, s_len, B_DH), F32)],
        compiler_params=_params(("arbitrary", "arbitrary", "arbitrary")),
    )(proj_b, proj_b, cos_t, sin_t, gains8)


def _attn_bwd_body(qkv_ref, z_ref, cos_ref, sin_ref, gain_ref, dog_ref, o_ref, lse_ref,
                   dqkv_ref, dz_ref, dgain_ref,
                   qn_scr, kn_scr, dqn_scr, dkn_scr, do_scr, dl_scr, ls_scr, dv_scr):
    head, grp = pl.program_id(1), pl.program_id(2)
    first = (pl.program_id(0) == 0) & (head == 0) & (grp == 0)
    s_len = z_ref.shape[0]
    n_blocks = s_len // B_BLOCK
    cos_t, sin_t = cos_ref[...], sin_ref[...]

    @pl.when(first)
    def _():
        dgain_ref[...] = jnp.zeros_like(dgain_ref)

    @pl.when(grp == 0)
    def _():
        d_og, o = dog_ref[...], o_ref[...]
        silu, dsilu = _silu_parts(z_ref[...])
        d_o = d_og * silu
        dz_ref[...] = (d_og * o * dsilu).astype(dz_ref.dtype)
        do_scr[...] = d_o
        dl_scr[...] = jnp.broadcast_to(jnp.sum(d_o * o, axis=1, keepdims=True), o.shape)
        lane = lax.broadcasted_iota(jnp.int32, o.shape, 1)
        ls_scr[...] = jnp.broadcast_to(
            jnp.sum(jnp.where(lane == head, lse_ref[...], 0.0), axis=1, keepdims=True), o.shape)

    for gi, dil in enumerate(B_DILATIONS):
        @pl.when(grp == gi)
        def _(gi=gi, dil=dil):
            q_raw, k_raw = qkv_ref[0], qkv_ref[1]
            gq = gain_ref[gi:gi + 1, :]
            gk = gain_ref[B_GROUPS + gi:B_GROUPS + gi + 1, :]
            qn_scr[...], rq = _norm_rope(q_raw, gq, cos_t, sin_t)
            kn_scr[...], rk = _norm_rope(k_raw, gk, cos_t, sin_t)
            def blocks(it, carry):
                scored = []
                for j in range(ATT_UNROLL):
                    cur, prev, has_prev = _stream_rows(it * ATT_UNROLL + j, dil, s_len)
                    rc, rp = _rows(cur, dil), _rows(prev, dil)
                    qb, d_ob = _bf(qn_scr[rc, :]), _bf(do_scr[rc, :])
                    k2 = _bf(jnp.concatenate([kn_scr[rp, :], kn_scr[rc, :]], axis=0))
                    v2 = _bf(jnp.concatenate([qkv_ref.at[2][rp, :], qkv_ref.at[2][rc, :]], axis=0))
                    scored.append((rc, rp, has_prev, qb, d_ob, k2,
                                   _mm_nt(qb, k2) * ATT_SCALE, _mm_nt(d_ob, v2)))
                grads = []
                for rc, rp, has_prev, qb, d_ob, k2, s, d_p in scored:
                    p = jnp.exp(jnp.where(_band_mask(has_prev), s - ls_scr[rc, :][:, 0:1], NEG_BIG))
                    ds = _bf(p * (d_p - dl_scr[rc, :][:, 0:1]))
                    grads.append((rc, rp, has_prev,
                                  _mm(ds, k2) * ATT_SCALE, _mm_tn(ds, qb) * ATT_SCALE, _mm_tn(_bf(p), d_ob)))
                for j, (rc, rp, has_prev, dq, dk2, dv2) in enumerate(grads):
                    dqn_scr[rc, :] = dq
                    if j == 0:
                        @pl.when(has_prev)
                        def _():
                            dkn_scr[rp, :] += dk2[:B_BLOCK]
                            dv_scr[rp, :] += dv2[:B_BLOCK]
                    if j + 1 < ATT_UNROLL:
                        dkn_scr[rc, :] = dk2[B_BLOCK:] + grads[j + 1][4][:B_BLOCK]
                        dv_scr[rc, :] = dv2[B_BLOCK:] + grads[j + 1][5][:B_BLOCK]
                    else:
                        dkn_scr[rc, :] = dk2[B_BLOCK:]
                        dv_scr[rc, :] = dv2[B_BLOCK:]
                return carry

            lax.fori_loop(0, n_blocks // ATT_UNROLL, blocks, 0)
            dq, dgq = _norm_rope_bwd(q_raw, rq, gq, cos_t, sin_t, dqn_scr[...])
            dk, dgk = _norm_rope_bwd(k_raw, rk, gk, cos_t, sin_t, dkn_scr[...])
            dqkv_ref[0] = dq.astype(dqkv_ref.dtype)
            dqkv_ref[1] = dk.astype(dqkv_ref.dtype)
            dqkv_ref[2] = dv_scr[...].astype(dqkv_ref.dtype)
            dgain_ref[gi:gi + 1, :] += dgq
            dgain_ref[B_GROUPS + gi:B_GROUPS + gi + 1, :] += dgk


def attn_bwd(proj_b, cos_t, sin_t, gains8, d_og, o, lse, n_seq):
    t_rows = proj_b.shape[1]
    s_len = t_rows // n_seq
    head_blk = pl.BlockSpec((s_len, B_DH), lambda b, h, g: (b, h))
    seq_blk = pl.BlockSpec((s_len, 128), lambda b, h, g: (b, 0))
    grp_blk = pl.BlockSpec((3, s_len, B_DH), lambda b, h, g: (g, b, h))
    gain_blk = pl.BlockSpec((8, 128), lambda b, h, g: (0, 0))
    return pl.pallas_call(
        _attn_bwd_body, name="attn_bwd", grid=(n_seq, B_HEADS, B_GROUPS),
        in_specs=[
            grp_blk,
            pl.BlockSpec((None, s_len, B_DH), lambda b, h, g: (B_PIECES - 1, b, h)),
            seq_blk, seq_blk, gain_blk, head_blk, head_blk, seq_blk,
        ],
        out_specs=[grp_blk, head_blk, gain_blk],
        out_shape=[jax.ShapeDtypeStruct((3 * B_GROUPS, t_rows, B_W), BF16),
                   jax.ShapeDtypeStruct((t_rows, B_W), BF16),
                   jax.ShapeDtypeStruct((8, 128), F32)],
        scratch_shapes=[pltpu.VMEM((s_len, B_DH), F32) for _ in range(8)],
        compiler_params=_params(("arbitrary", "arbitrary", "arbitrary")),
    )(proj_b, proj_b, cos_t, sin_t, gains8, d_og, o, lse)


def rope_tables(positions):
    inv_freq = ROPE_THETA ** (-jnp.arange(0, ROPE_DIMS, 2, dtype=F32) / ROPE_DIMS)
    ang = positions.astype(F32)[:, None] * inv_freq
    cos, sin = jnp.cos(ang), jnp.sin(ang)
    t_rows = positions.shape[0]
    rest = B_DH - ROPE_DIMS
    cos_t = jnp.concatenate([cos, cos, jnp.ones((t_rows, rest), F32)], axis=1)
    sin_t = jnp.concatenate([-sin, sin, jnp.zeros((t_rows, rest), F32)], axis=1)
    return cos_t, sin_t


def _rms_fwd_body(x_ref, g_ref, *rest, layer):
    h_ref, ht_ref = rest[-2:]
    x = x_ref[...]
    r = lax.rsqrt(jnp.mean(x * x, axis=1, keepdims=True) + EPS)
    h = x * r * g_ref[layer:layer + 1, :]
    h_ref[...] = h.astype(h_ref.dtype)
    ht_ref[...] = h.T.astype(ht_ref.dtype)


def rms_fwd(x, gains8, layer, after=None):
    t_rows, d = x.shape
    tm = min(ROW_TILE, t_rows)
    in_specs = [pl.BlockSpec((tm, d), lambda i: (i, 0)), pl.BlockSpec((8, d), lambda i: (0, 0))]
    args = [x, gains8]
    if after is not None:
        in_specs.append(HBM_SPEC)
        args.append(after)
    return pl.pallas_call(
        functools.partial(_rms_fwd_body, layer=layer), name=f"rms_fwd_{layer}", grid=(t_rows // tm,),
        in_specs=in_specs,
        out_specs=[pl.BlockSpec((tm, d), lambda i: (i, 0)), pl.BlockSpec((d, tm), lambda i: (0, i))],
        out_shape=[jax.ShapeDtypeStruct((t_rows, d), BF16), jax.ShapeDtypeStruct((d, t_rows), BF16)],
        compiler_params=_params(("arbitrary",)),
    )(*args)


def _rms_bwd_body(x_ref, g_ref, dh_ref, res_ref, dx_ref, dg_ref, *, layer):
    x, dh = x_ref[...], dh_ref[...]
    r = lax.rsqrt(jnp.mean(x * x, axis=1, keepdims=True) + EPS)
    xr = x * r
    u = dh * g_ref[layer:layer + 1, :]
    dx_ref[...] = res_ref[...] + r * u - xr * (r * r) * jnp.mean(u * x, axis=1, keepdims=True)

    @pl.when(pl.program_id(0) == 0)
    def _():
        dg_ref[...] = jnp.zeros_like(dg_ref)

    dg_ref[0:1, :] += jnp.sum(dh * xr, axis=0, keepdims=True)


def rms_bwd(x, gains8, layer, dh, d_res):
    t_rows, d = x.shape
    tm = min(ROW_TILE, t_rows)
    blk = pl.BlockSpec((tm, d), lambda i: (i, 0))
    gblk = pl.BlockSpec((8, d), lambda i: (0, 0))
    return pl.pallas_call(
        functools.partial(_rms_bwd_body, layer=layer), name=f"rms_bwd_{layer}", grid=(t_rows // tm,),
        in_specs=[blk, gblk, blk, blk],
        out_specs=[blk, gblk],
        out_shape=[jax.ShapeDtypeStruct((t_rows, d), F32), jax.ShapeDtypeStruct((8, d), F32)],
        compiler_params=_params(("arbitrary",)),
    )(x, gains8, dh, d_res)


def _piece_col(p):
    return jnp.where(p < 3 * B_GROUPS, (p % 3) * B_GROUPS + p // 3, 3 * B_GROUPS)


def _mm_nn_body(a_ref, w_ref, *rest, has_res, a_resident):
    o_ref = rest[-1]
    tm = o_ref.shape[0]
    a = a_ref[pl.ds(pl.multiple_of(pl.program_id(1) * tm, tm), tm), :] if a_resident else a_ref[...]
    acc = jnp.dot(a, w_ref[...], preferred_element_type=F32)
    if has_res:
        acc = acc + rest[0][...]
    o_ref[...] = acc


def mm_nn(a, w, residual=None, *, tn, name, a_resident=False):
    m, k = a.shape
    n = w.shape[1]
    tm = min(NT_ROW_TILE, m)
    a_spec = pl.BlockSpec((m, k), lambda j, i: (0, 0)) if a_resident else pl.BlockSpec((tm, k), lambda j, i: (i, 0))
    in_specs = [a_spec, pl.BlockSpec((k, tn), lambda j, i: (0, j))]
    args = [a, w]
    if residual is not None:
        in_specs.append(pl.BlockSpec((tm, tn), lambda j, i: (i, j)))
        args.append(residual)
    return pl.pallas_call(
        functools.partial(_mm_nn_body, has_res=residual is not None, a_resident=a_resident), name=name,
        grid=(n // tn, m // tm),
        in_specs=in_specs,
        out_specs=pl.BlockSpec((tm, tn), lambda j, i: (i, j)),
        out_shape=jax.ShapeDtypeStruct((m, n), F32),
        compiler_params=_params(("arbitrary", "arbitrary")),
    )(*args)


def mm_nn_pieces(a, w, *, name):
    m, k = a.shape
    tm = min(NT_ROW_TILE, m)
    return pl.pallas_call(
        functools.partial(_mm_nn_body, has_res=False, a_resident=True), name=name, grid=(B_PIECES, m // tm),
        in_specs=[pl.BlockSpec((m, k), lambda p, i: (0, 0)),
                  pl.BlockSpec((k, B_W), lambda p, i: (0, _piece_col(p)))],
        out_specs=pl.BlockSpec((None, tm, B_W), lambda p, i: (p, i, 0)),
        out_shape=jax.ShapeDtypeStruct((B_PIECES, m, B_W), F32),
        compiler_params=_params(("arbitrary", "arbitrary")),
    )(a, w)


NT_ROW_TILE = 1024


def _mm_nt_body(g_ref, w_ref, *rest, has_init):
    o_ref = rest[-1]
    j = pl.program_id(1)
    part = lax.dot_general(_bf(g_ref[...]), w_ref[...], (((1,), (1,)), ((), ())), preferred_element_type=F32)

    @pl.when(j == 0)
    def _():
        o_ref[...] = part + rest[0][...] if has_init else part

    @pl.when(j > 0)
    def _():
        o_ref[...] += part


def mm_nt(g, w, init=None, *, tn, col_off=0, name, after=None):
    m, n = g.shape
    k = w.shape[0]
    tm = min(NT_ROW_TILE, m)
    in_specs = [pl.BlockSpec((tm, tn), lambda i, j: (i, j)),
                pl.BlockSpec((k, tn), lambda i, j: (0, col_off + j))]
    args = [g, w]
    if init is not None:
        in_specs.append(pl.BlockSpec((tm, k), lambda i, j: (i, 0)))
        args.append(init)
    if after is not None:
        in_specs.append(HBM_SPEC)
        args.append(after)
    return pl.pallas_call(
        functools.partial(_mm_nt_body, has_init=init is not None), name=name, grid=(m // tm, n // tn),
        in_specs=in_specs,
        out_specs=pl.BlockSpec((tm, k), lambda i, j: (i, 0)),
        out_shape=jax.ShapeDtypeStruct((m, k), F32),
        compiler_params=_params(("arbitrary", "arbitrary")),
    )(*args)


def mm_nt_multi(gs, w, *, tn, name, after=None):
    m = gs[0].shape[0]
    k = w.shape[0]
    tm = min(NT_ROW_TILE, m)
    tiles = [g.shape[1] // tn for g in gs]
    starts = [sum(tiles[:i]) for i in range(len(gs))]

    def body(*refs):
        g_refs, w_ref, o_ref = refs[:len(gs)], refs[len(gs)], refs[-1]
        j = pl.program_id(1)
        for g_ref, lo, cnt in zip(g_refs, starts, tiles):
            @pl.when((j >= lo) & (j < lo + cnt))
            def _(g_ref=g_ref):
                part = lax.dot_general(_bf(g_ref[...]), w_ref[...], (((1,), (1,)), ((), ())),
                                       preferred_element_type=F32)

                @pl.when(j == 0)
                def _():
                    o_ref[...] = part

                @pl.when(j > 0)
                def _():
                    o_ref[...] += part

    def g_spec(lo, cnt):
        return pl.BlockSpec((tm, tn), lambda i, j: (i, jnp.clip(j - lo, 0, cnt - 1)))

    in_specs = [g_spec(lo, cnt) for lo, cnt in zip(starts, tiles)] + [pl.BlockSpec((k, tn), lambda i, j: (0, j))]
    args = list(gs) + [w]
    if after is not None:
        in_specs.append(HBM_SPEC)
        args.append(after)
    return pl.pallas_call(
        body, name=name, grid=(m // tm, sum(tiles)),
        in_specs=in_specs,
        out_specs=pl.BlockSpec((tm, k), lambda i, j: (i, 0)),
        out_shape=jax.ShapeDtypeStruct((m, k), F32),
        compiler_params=_params(("arbitrary", "arbitrary")),
    )(*args)


def mm_nt_pieces(g9, gz, w, *, name):
    n_p, m, _ = g9.shape
    k = w.shape[0]
    tm = min(2 * NT_ROW_TILE, m)

    def body(g_ref, z_ref, w_ref, o_ref):
        p = pl.program_id(1)

        def accumulate(src):
            part = lax.dot_general(_bf(src[...]), w_ref[...], (((1,), (1,)), ((), ())), preferred_element_type=F32)

            @pl.when(p == 0)
            def _():
                o_ref[...] = part

            @pl.when(p > 0)
            def _():
                o_ref[...] += part

        @pl.when(p < n_p)
        def _():
            accumulate(g_ref)

        @pl.when(p == n_p)
        def _():
            accumulate(z_ref)

    return pl.pallas_call(
        body, name=name, grid=(m // tm, n_p + 1),
        in_specs=[pl.BlockSpec((None, tm, B_W), lambda i, p: (jnp.minimum(p, n_p - 1), i, 0)),
                  pl.BlockSpec((tm, B_W), lambda i, p: (i, 0)),
                  pl.BlockSpec((k, B_W), lambda i, p: (0, _piece_col(p)))],
        out_specs=pl.BlockSpec((tm, k), lambda i, p: (i, 0)),
        out_shape=jax.ShapeDtypeStruct((m, k), F32),
        compiler_params=_params(("arbitrary", "arbitrary")),
    )(g9, gz, w)


def _mm_tn_body(a_ref, g_ref, o_ref, *, a_is_transposed):
    lhs_dim = 1 if a_is_transposed else 0
    o_ref[...] = lax.dot_general(a_ref[...], _bf(g_ref[...]), (((lhs_dim,), (0,)), ((), ())),
                                 preferred_element_type=F32).astype(o_ref.dtype)


def mm_tn(a, g, *, tn, out_dtype, name, a_is_transposed=False):
    k = a.shape[0] if a_is_transposed else a.shape[1]
    m, n = g.shape
    return pl.pallas_call(
        functools.partial(_mm_tn_body, a_is_transposed=a_is_transposed), name=name, grid=(n // tn,),
        in_specs=[pl.BlockSpec(a.shape, lambda j: (0, 0)), pl.BlockSpec((m, tn), lambda j: (0, j))],
        out_specs=pl.BlockSpec((k, tn), lambda j: (0, j)),
        out_shape=jax.ShapeDtypeStruct((k, n), out_dtype),
        compiler_params=_params(("arbitrary",)),
    )(a, g)


def mm_tn_multi(at, gs, *, unit, out_dtype, name):
    k, m = at.shape
    counts = [g.shape[1] // unit for g in gs]
    starts = [sum(counts[:i]) for i in range(len(gs))]

    def body(at_ref, *refs):
        g_refs, o_ref = refs[:-1], refs[-1]
        u = pl.program_id(0)
        for g_ref, lo, cnt in zip(g_refs, starts, counts):
            @pl.when((u >= lo) & (u < lo + cnt))
            def _(g_ref=g_ref):
                _mm_tn_body(at_ref, g_ref, o_ref, a_is_transposed=True)

    def g_spec(lo, cnt):
        return pl.BlockSpec((m, unit), lambda u: (0, jnp.clip(u - lo, 0, cnt - 1)))

    return pl.pallas_call(
        body, name=name, grid=(sum(counts),),
        in_specs=[pl.BlockSpec((k, m), lambda u: (0, 0))] + [g_spec(lo, cnt) for lo, cnt in zip(starts, counts)],
        out_specs=pl.BlockSpec((k, unit), lambda u: (0, u)),
        out_shape=jax.ShapeDtypeStruct((k, sum(counts) * unit), out_dtype),
        compiler_params=_params(("arbitrary",)),
    )(at, *gs)


B_UNIT = 256
B_IN_COLS = B_PIECES * B_W
B_SHARD_UNITS = B_IN_COLS // N_DEV // B_UNIT


def mm_tn_b_in(at, g9, gz, *, out_dtype, name):
    k, m = at.shape
    per_piece = B_W // B_UNIT
    body_one = functools.partial(_mm_tn_body, a_is_transposed=True)
    n_units = B_IN_COLS // B_UNIT

    def g_map(u):
        nat = jnp.minimum(u // per_piece, 3 * B_GROUPS - 1)
        piece = (nat % B_GROUPS) * 3 + nat // B_GROUPS
        return (piece, 0, u % per_piece)

    def body(a_ref, g_ref, z_ref, o_ref):
        u = pl.program_id(0)

        @pl.when(u < 3 * B_GROUPS * per_piece)
        def _():
            body_one(a_ref, g_ref, o_ref)

        @pl.when(u >= 3 * B_GROUPS * per_piece)
        def _():
            body_one(a_ref, z_ref, o_ref)

    return pl.pallas_call(
        body, name=name, grid=(n_units,),
        in_specs=[pl.BlockSpec((k, m), lambda u: (0, 0)),
                  pl.BlockSpec((None, m, B_UNIT), g_map),
                  pl.BlockSpec((m, B_UNIT), lambda u: (0, jnp.where(u < 3 * B_GROUPS * per_piece, 0, u % per_piece)))],
        out_specs=pl.BlockSpec((None, k, B_UNIT), lambda u: (u // B_SHARD_UNITS, 0, u % B_SHARD_UNITS)),
        out_shape=jax.ShapeDtypeStruct((N_DEV, k, B_IN_COLS // N_DEV), out_dtype),
        compiler_params=_params(("arbitrary",)),
    )(at, g9, gz)


def _out_loss_body(a_ref, w_ref, res_ref, t_ref, dy_ref, loss_ref, acc, *, n_steps):
    i = pl.program_id(0)
    d = w_ref.shape[1]
    y = jnp.dot(a_ref[...], w_ref[...], preferred_element_type=F32) + res_ref[...]
    err = y - t_ref[...]
    dy_ref[...] = err * (1.0 / d)

    @pl.when(i == 0)
    def _():
        acc[...] = jnp.zeros_like(acc)

    acc[...] += jnp.sum(err * err, axis=0, keepdims=True)

    @pl.when(i == n_steps - 1)
    def _():
        total = jnp.sum(acc[...], axis=1, keepdims=True) * (0.5 / d)
        loss_ref[...] = jnp.broadcast_to(total, loss_ref.shape)


def out_and_loss(a, w, residual, target):
    t_rows, k = a.shape
    d = w.shape[1]
    tm = min(ROW_TILE, t_rows)
    n_steps = t_rows // tm
    blk = pl.BlockSpec((tm, d), lambda i: (i, 0))
    return pl.pallas_call(
        functools.partial(_out_loss_body, n_steps=n_steps), name="out_b_loss", grid=(n_steps,),
        in_specs=[pl.BlockSpec((tm, k), lambda i: (i, 0)), pl.BlockSpec((k, d), lambda i: (0, 0)), blk, blk],
        out_specs=[blk, pl.BlockSpec((8, 128), lambda i: (0, 0))],
        out_shape=[jax.ShapeDtypeStruct((t_rows, d), F32), jax.ShapeDtypeStruct((8, 128), F32)],
        scratch_shapes=[pltpu.VMEM((1, d), F32)],
        compiler_params=_params(("arbitrary",)),
    )(a, w, residual, target)


def _adamw_body(p_ref, w_ref, m_ref, v_ref, g_ref, d_ref, nm_ref, nv_ref):
    g = p_ref[0].astype(F32)
    for s in range(1, N_DEV):
        g = g + p_ref[s].astype(F32)
    w = w_ref[...]
    m = ADAM_B1 * m_ref[...] + (1.0 - ADAM_B1) * g
    v = ADAM_B2 * v_ref[...] + (1.0 - ADAM_B2) * (g * g)
    m_hat = m / (1.0 - ADAM_B1 ** ADAM_STEP)
    v_hat = v / (1.0 - ADAM_B2 ** ADAM_STEP)
    g_ref[...] = g
    d_ref[...] = -ADAM_LR * (m_hat / (jnp.sqrt(v_hat) + ADAM_EPS) + ADAM_WD * w)
    nm_ref[...] = m
    nv_ref[...] = v


def adamw(parts, w, m, v, *, name):
    _, r, c = w.shape
    tr = r if r <= 256 else 256
    blk = pl.BlockSpec((None, tr, c), lambda i: (0, i, 0))
    out = jax.ShapeDtypeStruct((1, r, c), F32)
    return pl.pallas_call(
        _adamw_body, name=name, grid=(r // tr,),
        in_specs=[pl.BlockSpec((N_DEV, tr, c), lambda i: (0, i, 0)), blk, blk, blk],
        out_specs=[blk, blk, blk, blk],
        out_shape=[out, out, out, out],
        compiler_params=_params(("arbitrary",)),
    )(parts, w, m, v)


MESH_ID = pl.DeviceIdType.MESH
HBM_SPEC = pl.BlockSpec(memory_space=pl.ANY)


def _my_place():
    return lax.axis_index("x"), lax.axis_index("y"), lax.axis_index("c")


def _flat(x, y, c):
    return 4 * x + 2 * y + c


def _all_gather_body(*refs, n):
    ins, outs = refs[:n], refs[n:2 * n]
    send_sems, recv_sems, local_sems = refs[2 * n:]
    x, y, c = _my_place()
    me, sibling = (x, y, c), (x, y, 1 - c)
    chips = [(1 - x, y), (x, 1 - y), (1 - x, 1 - y)]
    pending = []
    for a in range(n):
        src, out = ins[a], outs[a]

        def copy(k, block, to, from_input=False, a=a, src=src, out=out):
            slot = out.at[_flat(*block)]
            return pltpu.make_async_remote_copy(
                src_ref=src if from_input else slot, dst_ref=slot,
                send_sem=send_sems.at[7 * a + k], recv_sem=recv_sems.at[7 * a + k],
                device_id=to, device_id_type=MESH_ID)

        mine = pltpu.make_async_copy(src, out.at[_flat(*me)], local_sems.at[a])
        mine.start()
        first = [copy(0, me, sibling, True)] + [copy(1 + j, me, (*chip, c), True) for j, chip in enumerate(chips)]
        for cp in first:
            cp.start()
        pending.append((copy, mine, first))
    for copy, mine, first in pending:
        passed = [copy(4 + j, (*chip, c), sibling) for j, chip in enumerate(chips)]
        for j, chip in enumerate(chips):
            copy(1 + j, (*chip, c), me).wait_recv()
            passed[j].start()
        copy(0, sibling, me).wait_recv()
        for j, chip in enumerate(chips):
            copy(4 + j, (*chip, 1 - c), me).wait_recv()
        for cp in first + passed:
            cp.wait_send()
        mine.wait()


def all_gather(shards, *, name):
    n = len(shards)
    return pl.pallas_call(
        functools.partial(_all_gather_body, n=n), name=name,
        in_specs=[HBM_SPEC] * n, out_specs=[HBM_SPEC] * n,
        out_shape=[jax.ShapeDtypeStruct((N_DEV,) + s.shape, s.dtype) for s in shards],
        scratch_shapes=[pltpu.SemaphoreType.DMA((7 * n,)), pltpu.SemaphoreType.DMA((7 * n,)),
                        pltpu.SemaphoreType.DMA((n,))],
    )(*shards)


PEER_FLIPS = [(0, 0, 1), (1, 0, 0), (0, 1, 0), (1, 1, 0), (1, 0, 1), (0, 1, 1), (1, 1, 1)]


def _all_to_all_body(*refs, n):
    ins, outs = refs[:n], refs[n:2 * n]
    send_sems, recv_sems, local_sems = refs[2 * n:]
    x, y, c = _my_place()
    me = _flat(x, y, c)
    waits = []
    for a in range(n):
        src, out = ins[a], outs[a]
        mine = pltpu.make_async_copy(src.at[me], out.at[me], local_sems.at[a])
        mine.start()
        waits.append(mine)
        for k, (fx, fy, fc) in enumerate(PEER_FLIPS):
            peer = (1 - x if fx else x, 1 - y if fy else y, 1 - c if fc else c)
            theirs = _flat(*peer)
            sems = dict(send_sem=send_sems.at[7 * a + k], recv_sem=recv_sems.at[7 * a + k],
                        device_id=peer, device_id_type=MESH_ID)
            send = pltpu.make_async_remote_copy(src_ref=src.at[theirs], dst_ref=out.at[me], **sems)
            send.start()
            recv = pltpu.make_async_remote_copy(src_ref=src.at[theirs], dst_ref=out.at[theirs], **sems)
            waits.append((send, recv))
    for w in waits:
        if isinstance(w, tuple):
            w[0].wait_send()
            w[1].wait_recv()
        else:
            w.wait()


def all_to_all(parts, *, name):
    n = len(parts)
    return pl.pallas_call(
        functools.partial(_all_to_all_body, n=n), name=name,
        in_specs=[HBM_SPEC] * n, out_specs=[HBM_SPEC] * n,
        out_shape=[jax.ShapeDtypeStruct(p.shape, p.dtype) for p in parts],
        scratch_shapes=[pltpu.SemaphoreType.DMA((7 * n,)), pltpu.SemaphoreType.DMA((7 * n,)),
                        pltpu.SemaphoreType.DMA((n,))],
    )(*parts)


HBM_ONLY = pl.BlockSpec(memory_space=pltpu.HBM)
SEM_SPEC = pl.BlockSpec(memory_space=pltpu.SEMAPHORE)
DATAFLOW_EFFECT = pltpu.SideEffectType.DATAFLOW_SIDE_EFFECTING


def _split_copies(srcs, lands, send_sems, recv_sems, n, scatter):
    x, y, c = _my_place()
    me = _flat(x, y, c)
    pairs = []
    for a in range(n):
        for k, (fx, fy, fc) in enumerate(PEER_FLIPS):
            peer = (1 - x if fx else x, 1 - y if fy else y, 1 - c if fc else c)
            theirs = _flat(*peer)
            src = srcs[a].at[theirs] if scatter else srcs[a]
            sems = dict(send_sem=send_sems.at[7 * a + k], recv_sem=recv_sems.at[7 * a + k],
                        device_id=peer, device_id_type=MESH_ID)
            pairs.append((pltpu.make_async_remote_copy(src_ref=src, dst_ref=lands[a].at[me], **sems),
                          pltpu.make_async_remote_copy(src_ref=src, dst_ref=lands[a].at[theirs], **sems)))
    return pairs


def _exchange_start_body(*refs, n, scatter):
    srcs, lands = refs[:n], refs[n:2 * n]
    send_sems, recv_sems = refs[2 * n], refs[2 * n + 1]
    token = refs[-1]
    for send, _ in _split_copies(srcs, lands, send_sems, recv_sems, n, scatter):
        send.start()
    token[...] = jnp.zeros_like(token)


def exchange_start(srcs, lands, *, scatter, name):
    n = len(srcs)
    args = [pltpu.with_memory_space_constraint(t, pltpu.HBM) for t in list(srcs) + list(lands)]
    outs = pl.pallas_call(
        functools.partial(_exchange_start_body, n=n, scatter=scatter), name=name,
        out_shape=(pltpu.SemaphoreType.DMA((7 * n,)), pltpu.SemaphoreType.DMA((7 * n,)),
                   *[pltpu.HBM(t.shape, t.dtype) for t in args],
                   jax.ShapeDtypeStruct((8, 128), F32)),
        in_specs=[HBM_ONLY] * (2 * n),
        out_specs=(SEM_SPEC, SEM_SPEC, *[HBM_ONLY] * (2 * n), pl.BlockSpec(memory_space=pltpu.VMEM)),
        input_output_aliases={i: 2 + i for i in range(2 * n)},
        compiler_params=pltpu.CompilerParams(has_side_effects=DATAFLOW_EFFECT),
    )(*args)
    return outs[0], outs[1], outs[2:2 + n], outs[2 + n:2 + 2 * n], outs[-1]


def _exchange_wait_body(*refs, n, scatter):
    srcs, lands = refs[:n], refs[n:2 * n]
    send_sems, recv_sems = refs[2 * n], refs[2 * n + 1]
    for send, recv in _split_copies(srcs, lands, send_sems, recv_sems, n, scatter):
        send.wait_send()
        recv.wait_recv()


def exchange_wait(send_sems, recv_sems, srcs, lands, after, *, scatter, name):
    n = len(srcs)
    outs = pl.pallas_call(
        functools.partial(_exchange_wait_body, n=n, scatter=scatter), name=name,
        out_shape=tuple(pltpu.HBM(t.shape, t.dtype) for t in list(srcs) + list(lands)),
        in_specs=[HBM_ONLY] * (2 * n) + [SEM_SPEC, SEM_SPEC, HBM_SPEC],
        out_specs=tuple([HBM_ONLY] * (2 * n)),
        input_output_aliases={i: i for i in range(2 * n)},
        compiler_params=pltpu.CompilerParams(has_side_effects=DATAFLOW_EFFECT),
    )(*srcs, *lands, send_sems, recv_sems, after)
    return outs[n:]


def _own_slot_only(shape_dtype, own, slot):
    land = lax.empty(shape_dtype.shape, shape_dtype.dtype)
    return lax.dynamic_update_slice(land, own[None], (slot,) + (0,) * own.ndim)


def _pad_rows(a, rows=8):
    return jnp.pad(a, ((0, rows - a.shape[0]), (0, 0)))


def _gate_rows(a_log, dt_bias):
    z = jnp.zeros((8, 128), F32)
    return z.at[0, A_HEADS:2 * A_HEADS].set(a_log[0]).at[1, A_HEADS:2 * A_HEADS].set(dt_bias[0])


def _pack_small(norm_g, a_log, a_dt_bias, a_norm_g, b_q_norm_g, b_k_norm_g):
    return jnp.concatenate([
        norm_g[0].reshape(8, 128), norm_g[1].reshape(8, 128),
        _gate_rows(a_log, a_dt_bias),
        _pad_rows(a_norm_g[0].reshape(2, 128)),
        _pad_rows(jnp.concatenate([b_q_norm_g[0], b_k_norm_g[0]], axis=0)),
    ], axis=0)


def _unpack_small(p):
    return (p[0:16].reshape(2, D_MODEL), p[16:17, A_HEADS:2 * A_HEADS], p[17:18, A_HEADS:2 * A_HEADS],
            p[24:26].reshape(1, A_DV), p[32:35][None], p[35:38][None])


def kernel(x, positions, norm_g, a_w_in, a_conv_w, a_log, a_dt_bias, a_norm_g, a_w_out, b_w_in, b_q_norm_g, b_k_norm_g, b_w_out, loss_target, m_norm_g, m_a_w_in, m_a_conv_w, m_a_log, m_a_dt_bias, m_a_norm_g, m_a_w_out, m_b_w_in, m_b_q_norm_g, m_b_k_norm_g, m_b_w_out, v_norm_g, v_a_w_in, v_a_conv_w, v_a_log, v_a_dt_bias, v_a_norm_g, v_a_w_out, v_b_w_in, v_b_q_norm_g, v_b_k_norm_g, v_b_w_out):
    n_seq, s_len, d = x.shape
    t_rows = n_seq * s_len
    n_chunks = s_len // CHUNK
    x0 = x.reshape(t_rows, d)
    target = loss_target.reshape(t_rows, d)
    my_slot = _flat(*_my_place())

    g_a_in, g_conv = all_gather([a_w_in[0].astype(BF16), _pad_rows(a_conv_w[0])], name="gather_weights_first")
    later = [a_w_out[0].astype(BF16), b_w_in[0].astype(BF16), b_w_out[0].astype(BF16)]
    lands = [_own_slot_only(jax.ShapeDtypeStruct((N_DEV,) + t.shape, t.dtype), t, my_slot) for t in later]
    w_send, w_recv, later, lands, w_token = exchange_start(later, lands, scatter=False, name="gather_weights_start")
    w_a_in = jnp.pad(g_a_in.transpose(1, 0, 2).reshape(d, A_IN), ((0, 0), (0, A_IN_PAD - A_IN)))
    conv_w8 = g_conv.transpose(1, 0, 2).reshape(8, 2 * A_QK + A_VW)

    gains_model = _pad_rows(norm_g)
    gate_prm = _gate_rows(a_log, a_dt_bias)
    gain_a_out = _pad_rows(a_norm_g)
    gains_qk = _pad_rows(jnp.concatenate([b_q_norm_g[0], b_k_norm_g[0]], axis=0))
    cos_t, sin_t = rope_tables(positions.reshape(t_rows))

    h0, h0_t = rms_fwd(x0, gains_model, 0, after=w_token)
    proj_a = mm_nn(h0, w_a_in, tn=896, name="proj_a", a_resident=True)
    gates_col, gates_row = a_gates_fwd(proj_a, gate_prm, n_seq)
    gates_row = gates_row.reshape(n_seq, 2 * A_HEADS, s_len // SUPER, 1, SUPER)
    o_a, tinv, states, og_a, q_a, k_a, v_a = gdn_fwd(proj_a, conv_w8, gates_col, gates_row, gain_a_out, n_seq)
    g_a_out, g_b_in, g_b_out = exchange_wait(w_send, w_recv, later, lands, og_a, scatter=False,
                                             name="gather_weights_wait")
    w_a_out = g_a_out.reshape(A_VW, d)
    w_b_in = g_b_in.transpose(1, 0, 2).reshape(d, B_IN_COLS)
    w_b_out = g_b_out.reshape(B_W, d)
    x1 = mm_nn(og_a, w_a_out, x0, tn=1024, name="out_a")

    h1, h1_t = rms_fwd(x1, gains_model, 1)
    proj_b = mm_nn_pieces(h1, w_b_in, name="proj_b")
    og_b, o_b, lse = attn_fwd(proj_b, cos_t, sin_t, gains_qk, n_seq)
    dy, loss_blk = out_and_loss(og_b, w_b_out, x1, target)

    d_og_b = mm_nt(dy, w_b_out, tn=1024, name="d_og_b")
    dw_b_out = mm_tn(og_b, dy, tn=256, out_dtype=BF16, name="dw_b_out")
    dqkv_b, dz_b, d_gains_qk = attn_bwd(proj_b, cos_t, sin_t, gains_qk, d_og_b, o_b, lse, n_seq)
    dh1 = mm_nt_pieces(dqkv_b, dz_b, w_b_in, name="dh1")
    dw_b_in = mm_tn_b_in(h1_t, dqkv_b, dz_b, out_dtype=BF16, name="dw_b_in")
    dx1, d_gain1 = rms_bwd(x1, gains_model, 1, dh1, dy)

    dw_a_out = mm_tn(og_a, dx1, tn=256, out_dtype=BF16, name="dw_a_out")
    early = [dw_b_in, dw_b_out.reshape(N_DEV, B_W // N_DEV, d), dw_a_out.reshape(N_DEV, A_VW // N_DEV, d)]
    lands = [_own_slot_only(t, lax.dynamic_index_in_dim(t, my_slot, 0, keepdims=False), my_slot) for t in early]
    g_send, g_recv, early, lands, g_token = exchange_start(early, lands, scatter=True, name="scatter_grads_start")

    d_og_a = mm_nt(dx1, w_a_out, tn=1024, name="d_og_a", after=g_token)
    d_xq, d_xk, d_xv, dgates, dz_a, d_gain_a_out, d_cq, d_ck, d_cv = gdn_bwd(
        q_a, k_a, v_a, gates_col, gates_row, tinv, states, d_og_a, o_a, proj_a, conv_w8, gain_a_out, n_seq)
    d_conv = jnp.concatenate([d_cq.sum(axis=0), d_ck.sum(axis=0), d_cv.sum(axis=0)], axis=1)
    d_gate_logits, d_gate_prm = a_gates_bwd(proj_a, gate_prm, dgates, n_seq)
    dw_a_in = jnp.concatenate([
        mm_tn_multi(h0_t, [d_xq, d_xk, d_xv, dz_a], unit=256, out_dtype=BF16, name="dw_a_in_qkvz"),
        mm_tn(h0_t, d_gate_logits, tn=128, out_dtype=BF16, name="dw_a_in_gates", a_is_transposed=True),
    ], axis=1)[:, :A_IN]
    shard_a_in = A_IN // N_DEV
    last = [dw_a_in.reshape(d, N_DEV, shard_a_in).transpose(1, 0, 2)]
    last_lands = [_own_slot_only(t, lax.dynamic_index_in_dim(t, my_slot, 0, keepdims=False), my_slot) for t in last]
    l_send, l_recv, last, last_lands, l_token = exchange_start(last, last_lands, scatter=True,
                                                               name="scatter_last_start")
    dh0 = mm_nt_multi([d_xq, d_xk, d_xv, dz_a], w_a_in, tn=1024, name="dh0_qkvz", after=l_token)
    dh0 = mm_nt(d_gate_logits, w_a_in, dh0, tn=128, col_off=A_GATE_COL, name="dh0_gates")
    dx0, d_gain0 = rms_bwd(x0, gains_model, 0, dh0, dx1)

    small = jnp.concatenate([
        d_gain0[0].reshape(8, 128), d_gain1[0].reshape(8, 128), d_gate_prm,
        _pad_rows(d_gain_a_out[0].reshape(2, 128)), d_gains_qk, loss_blk], axis=0)
    n_small = small.shape[0] - loss_blk.shape[0]
    small_srcs = [small, d_conv]
    small_lands = [_own_slot_only(jax.ShapeDtypeStruct((N_DEV,) + t.shape, t.dtype), t, my_slot)
                   for t in small_srcs]
    s_send, s_recv, small_srcs, small_lands, s_token = exchange_start(small_srcs, small_lands, scatter=False,
                                                                      name="gather_small_start")

    r_b_in, r_b_out, r_a_out = exchange_wait(g_send, g_recv, early, lands, s_token, scatter=True,
                                             name="scatter_grads_wait")
    (r_a_in,) = exchange_wait(l_send, l_recv, last, last_lands, s_token, scatter=True, name="scatter_last_wait")

    upd = {}
    upd["a_w_in"] = adamw(r_a_in, a_w_in, m_a_w_in, v_a_w_in, name="adamw_a_w_in")
    upd["a_w_out"] = adamw(r_a_out, a_w_out, m_a_w_out, v_a_w_out, name="adamw_a_w_out")
    upd["b_w_in"] = adamw(r_b_in, b_w_in, m_b_w_in, v_b_w_in, name="adamw_b_w_in")
    upd["b_w_out"] = adamw(r_b_out, b_w_out, m_b_w_out, v_b_w_out, name="adamw_b_w_out")
    r_small, r_conv = exchange_wait(s_send, s_recv, small_srcs, small_lands, upd["b_w_in"][0], scatter=False,
                                    name="gather_small_wait")
    conv_cols = a_conv_w.shape[2]
    r_conv = lax.dynamic_slice(r_conv, (0, 0, my_slot * conv_cols), (N_DEV, 8, conv_cols))
    loss = jnp.sum(r_small[:, n_small, 0])
    r_small = r_small[:, :n_small]
    upd["a_conv_w"] = [t[:, :A_CONV] for t in adamw(
        r_conv, _pad_rows(a_conv_w[0])[None], _pad_rows(m_a_conv_w[0])[None], _pad_rows(v_a_conv_w[0])[None],
        name="adamw_a_conv_w")]
    small_upd = adamw(
        r_small,
        _pack_small(norm_g, a_log, a_dt_bias, a_norm_g, b_q_norm_g, b_k_norm_g)[None],
        _pack_small(m_norm_g, m_a_log, m_a_dt_bias, m_a_norm_g, m_b_q_norm_g, m_b_k_norm_g)[None],
        _pack_small(v_norm_g, v_a_log, v_a_dt_bias, v_a_norm_g, v_b_q_norm_g, v_b_k_norm_g)[None],
        name="adamw_small")
    small_names = ("norm_g", "a_log", "a_dt_bias", "a_norm_g", "b_q_norm_g", "b_k_norm_g")
    unpacked = [_unpack_small(t[0]) for t in small_upd]
    for i, nm in enumerate(small_names):
        upd[nm] = [u[i] for u in unpacked]

    order = ("norm_g", "a_w_in", "a_conv_w", "a_log", "a_dt_bias", "a_norm_g", "a_w_out",
             "b_w_in", "b_q_norm_g", "b_k_norm_g", "b_w_out")
    outs = [loss, dx0.reshape(n_seq, s_len, d)]
    for kind in range(4):
        for nm in order:
            outs.append(upd[nm][kind])
    return tuple(outs)
```

```python
import functools
import math

import jax
import jax.numpy as jnp
from jax import lax
from jax.experimental import pallas as pl
from jax.experimental.pallas import tpu as pltpu

F32 = jnp.float32
BF16 = jnp.bfloat16

D_MODEL = 1024
EPS = 1e-6
N_DEV = 8

A_HEADS = 8
A_DK = 128
A_DV = 256
A_QK = A_HEADS * A_DK
A_VW = A_HEADS * A_DV
A_CONV = 4
CHUNK = 64
A_IN = 2 * A_QK + 2 * A_VW + 2 * A_HEADS
A_IN_PAD = 2 * A_QK + 2 * A_VW + 128
A_GATE_COL = (2 * A_QK + 2 * A_VW) // 128

B_DILATIONS = (1, 4, 16)
B_GROUPS = 3
B_HEADS = 8
B_DH = 128
B_W = B_HEADS * B_DH
B_BLOCK = 128
B_PIECES = 3 * B_GROUPS + 1
ROPE_THETA = 500000.0
ROPE_DIMS = B_DH // 4
ROPE_HALF = ROPE_DIMS // 2

ADAM_LR = 0.001
ADAM_B1 = 0.9
ADAM_B2 = 0.999
ADAM_EPS = 1e-08
ADAM_WD = 0.01
ADAM_STEP = 10

VMEM_LIMIT = 60 * 1024 * 1024


def _params(sem):
    return pltpu.CompilerParams(dimension_semantics=sem, vmem_limit_bytes=VMEM_LIMIT)


def _bf(x):
    return x.astype(BF16)


def _mm(a, b):
    return jnp.dot(_bf(a), _bf(b), preferred_element_type=F32)


def _mm_nt(a, b):
    return lax.dot_general(_bf(a), _bf(b), (((1,), (1,)), ((), ())), preferred_element_type=F32)


def _mm_tn(a, b):
    return lax.dot_general(_bf(a), _bf(b), (((0,), (0,)), ((), ())), preferred_element_type=F32)


def _split(x):
    hi = _bf(x)
    return hi, _bf(x - hi.astype(F32))


def _mm3(a, b):
    ah, al = _split(a)
    bh, bl = _split(b)
    d = functools.partial(jnp.dot, preferred_element_type=F32)
    return d(ah, bh) + (d(ah, bl) + d(al, bh))


def _colsum_as_col(z):
    zh, zl = _split(z)
    ones = jnp.ones((z.shape[0], 128), BF16)
    tn = functools.partial(lax.dot_general, dimension_numbers=(((0,), (0,)), ((), ())),
                           preferred_element_type=F32)
    return (tn(zh, ones) + tn(zl, ones))[:, 0:1]


def _sigmoid(x):
    return 0.5 * jnp.tanh(0.5 * x) + 0.5


INV_BASE = 8
INV_NEWTON = 2
GDN_GROUP = 4
SUPER = GDN_GROUP * CHUNK
GDN_WIDTH = 2

A_K_COL = A_QK // A_DK
A_V_COL = 2 * A_QK // A_DV


def _inverse_steps(m, row, col):
    eye = (row == col).astype(F32)
    d = jnp.where(row // INV_BASE == col // INV_BASE, m, 0.0)
    x = eye - d
    p = _mm(d, d)
    yield
    steps = int(math.log2(INV_BASE)) - 1
    for i in range(steps):
        x = x + _mm(x, p)
        if i + 1 < steps:
            p = _mm(p, p)
        yield
    size = INV_BASE
    while size < CHUNK:
        c = jnp.where((row // (2 * size) == col // (2 * size)) & (row // size != col // size), m, 0.0)
        xc = _mm(x, c)
        yield
        x = x - _mm(xc, x)
        yield
        size *= 2
    for _ in range(INV_NEWTON):
        r = eye - x - _mm3(m, x)
        yield
        x = x + _mm(x, r)
        yield
    return x


def _drain(gen):
    while True:
        try:
            next(gen)
        except StopIteration as stop:
            return stop.value


def _interleave(*gens):
    live = list(gens)
    while live:
        for g in list(live):
            try:
                next(g)
            except StopIteration:
                live.remove(g)


def _diag_blocks_tall(x):
    return jnp.concatenate([x[i * CHUNK:(i + 1) * CHUNK, i * CHUNK:(i + 1) * CHUNK] for i in range(GDN_GROUP)], axis=0)


def _tall_to_block_diag(t, same):
    return jnp.where(same, jnp.concatenate([t] * GDN_GROUP, axis=1), 0.0)


def _block_sum(same, x):
    xh, xl = _split(jnp.broadcast_to(x, (SUPER, 128)))
    ones = same.astype(BF16)
    d = functools.partial(jnp.dot, preferred_element_type=F32)
    return (d(ones, xh) + d(ones, xl))[:, 0:1]


def _aligned_rows(index, size):
    start = index * size
    return pl.ds(start if isinstance(start, int) else pl.multiple_of(start, size), size)


def _super_rows(i):
    return _aligned_rows(i, SUPER)


def _chunk_rows(n):
    return _aligned_rows(n, CHUNK)


def _gdn_super_steps(q, k, v, gcb, gr, head, tinv_tall=None, need_m=True):
    lane = lax.broadcasted_iota(jnp.int32, (SUPER, 128), 1)
    row = lax.broadcasted_iota(jnp.int32, (SUPER, SUPER), 0)
    col = lax.broadcasted_iota(jnp.int32, (SUPER, SUPER), 1)
    same = row // CHUNK == col // CHUNK
    beta = jnp.sum(jnp.where(lane == head, gcb, 0.0), axis=1, keepdims=True)
    gc = jnp.sum(jnp.where(lane == A_HEADS + head, gcb, 0.0), axis=1, keepdims=True)
    g_last = jnp.sum(jnp.where(col == (row // CHUNK) * CHUNK + (CHUNK - 1), gr, 0.0), axis=1, keepdims=True)
    gamma = jnp.exp(gc)
    decay = jnp.where(same & (row >= col), jnp.exp(jnp.minimum(gc - gr, 0.0)), 0.0)
    kb = k * beta
    m = jnp.where(same & (row > col), _mm_nt(kb, k) * decay, 0.0) if need_m else None
    p = jnp.where(same & (row >= col), _mm_nt(q, k) * decay, 0.0)
    yield
    if tinv_tall is None:
        tinv = yield from _inverse_steps(m, row, col)
    else:
        tinv = _tall_to_block_diag(tinv_tall, same)
    u = _mm(tinv, v * beta)
    w = _mm(tinv, kb * gamma)
    yield
    e_tail = jnp.exp(g_last - gc)
    return dict(beta=beta, gc=gc, g_last=g_last, gamma=gamma, decay=decay, kb=kb, m=m,
                tinv=tinv, u=u, w=w, p=p, e_tail=e_tail, row=row, col=col, lane=lane, same=same)


def _store_scan_operands(rows, q, k, t, u_scr, w_scr, p_scr, qg_scr, ke_scr, gl_scr):
    u_scr[rows, :] = t["u"]
    w_scr[rows, :] = _bf(t["w"])
    p_scr[rows, :] = _bf(_diag_blocks_tall(t["p"]))
    qg_scr[rows, :] = _bf(q * t["gamma"])
    ke_scr[rows, :] = _bf(k * t["e_tail"])
    gl_scr[rows, :] = jnp.broadcast_to(jnp.exp(t["g_last"]), (SUPER, 128))


def _gdn_fwd_body(q_ref, k_ref, v_ref, gc_ref, gr_ref, wq_ref, wk_ref, wv_ref, z_ref, gn_ref,
                  o_ref, tinv_ref, st_ref, og_ref, qo_ref, ko_ref, vo_ref, s_scr, *sets):
    head = pl.program_id(1)
    n_super = q_ref.shape[0] // SUPER
    all_sets = [sets[6 * i:6 * i + 6] for i in range(2 * GDN_WIDTH)]
    whole = pl.ds(0, SUPER)

    def conv_silu(x_ref, w_ref, i):
        rows = _super_rows(i)
        x, w = x_ref[rows, :], w_ref[...]
        halo = jnp.zeros((8, x.shape[1]), F32) if i == 0 else x_ref[pl.ds(i * SUPER - 8, 8), :]
        ext = jnp.concatenate([halo, x], axis=0)
        c = x * w[A_CONV - 1:A_CONV, :]
        for j in range(1, A_CONV):
            c = c + pltpu.roll(ext, j, 0)[8:, :] * w[A_CONV - 1 - j:A_CONV - j, :]
        return c * _sigmoid(c)

    def unit(a):
        return a * lax.rsqrt(jnp.sum(a * a, axis=1, keepdims=True) + EPS)

    def prepare_steps(i, dst):
        rows = _super_rows(i)
        q = unit(conv_silu(q_ref, wq_ref, i)) * A_DK ** -0.5
        k = unit(conv_silu(k_ref, wk_ref, i))
        v = conv_silu(v_ref, wv_ref, i)
        qo_ref[rows, :], ko_ref[rows, :], vo_ref[rows, :] = q, k, v
        t = yield from _gdn_super_steps(q, k, v, gc_ref[rows, :], gr_ref[i], head)
        tinv_ref[rows, :] = _diag_blocks_tall(t["tinv"])
        _store_scan_operands(whole, q, k, t, *dst)

    def scan_steps(i, src):
        u_scr, w_scr, p_scr, qg_scr, ke_scr, gl_scr = src
        for j in range(GDN_GROUP):
            n = i * GDN_GROUP + j
            local = pl.ds(j * CHUNK, CHUNK)
            s = s_scr[...]
            sb = _bf(s)
            st_ref[n] = sb
            ws = jnp.dot(w_scr[local, :], sb, preferred_element_type=F32)
            yield
            vb = _bf(u_scr[local, :] - ws)
            o = (jnp.dot(qg_scr[local, :], sb, preferred_element_type=F32)
                 + jnp.dot(p_scr[local, :], vb, preferred_element_type=F32))
            s_new = s * gl_scr[local, :][0:1, 0:1] + lax.dot_general(
                ke_scr[local, :], vb, (((0,), (0,)), ((), ())), preferred_element_type=F32)
            yield
            rows = _chunk_rows(n)
            o_ref[rows, :] = o
            s_scr[...] = s_new
            silu, _ = _silu_parts(z_ref[rows, :])
            r = lax.rsqrt(jnp.mean(o * o, axis=1, keepdims=True) + EPS)
            og_ref[rows, :] = ((o * r * gn_ref[0:1, :]) * silu).astype(og_ref.dtype)

    def scan_many(first, srcs):
        for j, src in enumerate(srcs):
            yield from scan_steps(first + j, src)

    groups = [all_sets[:GDN_WIDTH], all_sets[GDN_WIDTH:]]
    _interleave(*[prepare_steps(j, groups[0][j]) for j in range(GDN_WIDTH)])
    s_scr[...] = jnp.zeros_like(s_scr)
    for g in range(n_super // GDN_WIDTH):
        cur, nxt = groups[g % 2], groups[(g + 1) % 2]
        first = g * GDN_WIDTH
        following = [prepare_steps(first + GDN_WIDTH + j, nxt[j]) for j in range(GDN_WIDTH)
                     if first + GDN_WIDTH + j < n_super]
        _interleave(scan_many(first, cur), *following)


def _gdn_in_specs(s_len, n_super, from_proj):
    k_col, v_col = (A_K_COL, A_V_COL) if from_proj else (0, 0)
    return [
        pl.BlockSpec((s_len, A_DK), lambda b, h: (b, h)),
        pl.BlockSpec((s_len, A_DK), lambda b, h: (b, k_col + h)),
        pl.BlockSpec((s_len, A_DV), lambda b, h: (b, v_col + h)),
        pl.BlockSpec((s_len, 128), lambda b, h: (b, 0)),
        pl.BlockSpec((None, None, n_super, 1, SUPER), lambda b, h: (b, A_HEADS + h, 0, 0, 0)),
    ]


def _gdn_scan_scratch(s_len):
    return [pltpu.VMEM((A_DK, A_DV), F32), pltpu.VMEM((s_len, A_DV), F32),
            pltpu.VMEM((s_len, A_DK), BF16), pltpu.VMEM((s_len, CHUNK), BF16),
            pltpu.VMEM((s_len, A_DK), BF16), pltpu.VMEM((s_len, A_DK), BF16),
            pltpu.VMEM((s_len, 128), F32)]


def gdn_fwd(proj_a, conv_w8, gates_col, gates_row, norm_g8, n_seq):
    t_rows = proj_a.shape[0]
    s_len = t_rows // n_seq
    n_chunks = s_len // CHUNK
    qk_spec = pl.BlockSpec((s_len, A_DK), lambda b, h: (b, h))
    v_spec = pl.BlockSpec((s_len, A_DV), lambda b, h: (b, h))
    return pl.pallas_call(
        _gdn_fwd_body, name="gdn_fwd", grid=(n_seq, A_HEADS),
        in_specs=_gdn_in_specs(s_len, s_len // SUPER, True) + [
            pl.BlockSpec((8, A_DK), lambda b, h: (0, h)),
            pl.BlockSpec((8, A_DK), lambda b, h: (0, A_K_COL + h)),
            pl.BlockSpec((8, A_DV), lambda b, h: (0, A_V_COL + h)),
            pl.BlockSpec((s_len, A_DV), lambda b, h: (b, A_Z_COL + h)),
            pl.BlockSpec((8, A_DV), lambda b, h: (0, 0)),
        ],
        out_specs=[
            v_spec,
            pl.BlockSpec((s_len, CHUNK), lambda b, h: (b * A_HEADS + h, 0)),
            pl.BlockSpec((None, n_chunks, A_DK, A_DV), lambda b, h: (b * A_HEADS + h, 0, 0, 0)),
            v_spec, qk_spec, qk_spec, v_spec,
        ],
        out_shape=[
            jax.ShapeDtypeStruct((t_rows, A_VW), F32),
            jax.ShapeDtypeStruct((n_seq * A_HEADS * s_len, CHUNK), F32),
            jax.ShapeDtypeStruct((n_seq * A_HEADS, n_chunks, A_DK, A_DV), BF16),
            jax.ShapeDtypeStruct((t_rows, A_VW), BF16),
            jax.ShapeDtypeStruct((t_rows, A_QK), F32),
            jax.ShapeDtypeStruct((t_rows, A_QK), F32),
            jax.ShapeDtypeStruct((t_rows, A_VW), F32),
        ],
        scratch_shapes=_gdn_scan_scratch(SUPER) + (2 * GDN_WIDTH - 1) * _gdn_scan_scratch(SUPER)[1:],
        compiler_params=_params(("arbitrary", "arbitrary")),
    )(proj_a, proj_a, proj_a, gates_col, gates_row, conv_w8, conv_w8, conv_w8, proj_a, norm_g8)


def _gdn_bwd_body(q_ref, k_ref, v_ref, gc_ref, gr_ref, tinv_ref, st_ref, dog_ref, oa_ref, z_ref, gn_ref,
                  xq_ref, xk_ref, xv_ref, wq_ref, wk_ref, wv_ref,
                  dxq_ref, dxk_ref, dxv_ref, dgc_ref, dz_ref, dgn_ref, dwq_ref, dwk_ref, dwv_ref,
                  ds_scr, cq_scr, ck_scr, cv_scr, *sets):
    head = pl.program_id(1)
    n_super = q_ref.shape[0] // SUPER
    ops = (sets[0:7], sets[7:14])
    res = (sets[14:21], sets[21:28])
    whole = pl.ds(0, SUPER)
    tn = functools.partial(lax.dot_general, dimension_numbers=(((0,), (0,)), ((), ())), preferred_element_type=F32)
    nt = functools.partial(lax.dot_general, dimension_numbers=(((1,), (1,)), ((), ())), preferred_element_type=F32)

    @pl.when(head == 0)
    def _():
        dgc_ref[...] = jnp.zeros_like(dgc_ref)

    @pl.when((head == 0) & (pl.program_id(0) == 0))
    def _():
        dgn_ref[...] = jnp.zeros_like(dgn_ref)

    carry = (cq_scr, ck_scr, cv_scr)
    for ref in carry + (dwq_ref, dwk_ref, dwv_ref):
        ref[...] = jnp.zeros_like(ref)

    def common_steps(i, need_m=True):
        rows = _super_rows(i)
        q, k, v = q_ref[rows, :], k_ref[rows, :], v_ref[rows, :]
        t = yield from _gdn_super_steps(q, k, v, gc_ref[rows, :], gr_ref[i], head, tinv_tall=tinv_ref[rows, :],
                                        need_m=need_m)
        return rows, q, k, v, t

    def stage_p(i, parity):
        rows, q, k, _, t = yield from common_steps(i, need_m=False)
        _store_scan_operands(whole, q, k, t, *ops[parity][:6])
        o, d_og, gain = oa_ref[rows, :], dog_ref[rows, :], gn_ref[0:1, :]
        r = lax.rsqrt(jnp.mean(o * o, axis=1, keepdims=True) + EPS)
        silu, dsilu = _silu_parts(z_ref[rows, :])
        xr = o * r
        d_on = d_og * silu
        dz_ref[rows, :] = (d_og * (xr * gain) * dsilu).astype(dz_ref.dtype)
        u = d_on * gain
        ops[parity][6][...] = r * u - xr * (r * r) * jnp.mean(u * o, axis=1, keepdims=True)
        dgn_ref[0:1, :] += jnp.sum(d_on * xr, axis=0, keepdims=True)

    def stage_s(i, parity):
        u_scr, w_scr, p_scr, qg_scr, ke_scr, gl_scr, do_scr = ops[parity]
        vn_scr, dvn_scr, dqg_scr, dw_scr, dkt_scr, sds_scr, dof_scr = res[parity]
        for j in reversed(range(GDN_GROUP)):
            n = i * GDN_GROUP + j
            local = pl.ds(j * CHUNK, CHUNK)
            ds_next = ds_scr[...]
            dsb = _bf(ds_next)
            sb = st_ref[n]
            s = sb.astype(F32)
            d_o = do_scr[local, :]
            dof_scr[local, :] = d_o
            d_ob = _bf(d_o)
            w_s = jnp.dot(w_scr[local, :], sb, preferred_element_type=F32)
            d_vn = tn(p_scr[local, :], d_ob) + jnp.dot(ke_scr[local, :], dsb, preferred_element_type=F32)
            d_qg = nt(d_ob, sb)
            qg_do = tn(qg_scr[local, :], d_ob)
            yield
            v_new = u_scr[local, :] - w_s
            d_vnb = _bf(d_vn)
            d_w = -nt(d_vnb, sb)
            d_kt = nt(_bf(v_new), dsb)
            w_dvn = tn(w_scr[local, :], d_vnb)
            yield
            vn_scr[local, :] = v_new
            dvn_scr[local, :] = d_vn
            dqg_scr[local, :] = d_qg
            dw_scr[local, :] = d_w
            dkt_scr[local, :] = d_kt
            sds = jnp.sum(jnp.sum(s * ds_next, axis=1, keepdims=True), axis=0, keepdims=True)
            sds_scr[local, :] = jnp.broadcast_to(sds, (CHUNK, 128))
            ds_scr[...] = qg_do + gl_scr[local, :][0:1, 0:1] * ds_next - w_dvn

    def conv_bwd(i, rows, x_ref, w_ref, dy, norm_scale, dx_ref, dw_ref, dc_above):
        x, w = x_ref[rows, :], w_ref[...]
        above = x_ref[pl.ds(pl.multiple_of(jnp.maximum(i * SUPER - 8, 0), 8), 8), :]
        ext = jnp.concatenate([jnp.where(i > 0, above, 0.0), x], axis=0)
        c = x * w[A_CONV - 1:A_CONV, :]
        for j in range(1, A_CONV):
            c = c + pltpu.roll(ext, j, 0)[8:, :] * w[A_CONV - 1 - j:A_CONV - j, :]
        sig = _sigmoid(c)
        a = c * sig
        if norm_scale is None:
            da = dy
        else:
            rn = lax.rsqrt(jnp.sum(a * a, axis=1, keepdims=True) + EPS)
            da = norm_scale * (rn * dy - a * (rn * rn * rn) * jnp.sum(dy * a, axis=1, keepdims=True))
        dc = da * (sig * (1.0 + c * (1.0 - sig)))
        ext_dc = jnp.concatenate([dc, dc_above[...]], axis=0)
        dc_above[...] = dc[0:8, :]
        dx = dc * w[A_CONV - 1:A_CONV, :]
        dw_ref[A_CONV - 1:A_CONV, :] += jnp.sum(dc * x, axis=0, keepdims=True)
        for j in range(1, A_CONV):
            dcs = pltpu.roll(ext_dc, SUPER + 8 - j, 0)[:SUPER, :]
            dx = dx + dcs * w[A_CONV - 1 - j:A_CONV - j, :]
            dw_ref[A_CONV - 1 - j:A_CONV - j, :] += jnp.sum(dcs * x, axis=0, keepdims=True)
        dx_ref[rows, :] = dx.astype(dx_ref.dtype)

    def stage_f(i, parity):
        vn_scr, dvn_scr, dqg_scr, dw_scr, dkt_scr, sds_scr, dof_scr = res[parity]
        rows, q, k, v, t = yield from common_steps(i)
        beta, gamma, decay, kb, e_tail = t["beta"], t["gamma"], t["decay"], t["kb"], t["e_tail"]
        row, col, lane, same = t["row"], t["col"], t["lane"], t["same"]
        d_o = dof_scr[...]
        v_new, d_vn = vn_scr[...], dvn_scr[...]
        d_qg, d_w, d_kt = dqg_scr[...], dw_scr[...], dkt_scr[...]
        gamma_last = jnp.exp(t["g_last"])

        d_p = jnp.where(same & (row >= col), _mm_nt(d_o, v_new), 0.0)
        d_ru = _mm_tn(t["tinv"], d_vn)
        d_rw = _mm_tn(t["tinv"], d_w)
        yield
        d_m = jnp.where(same & (row > col), -(_mm_nt(d_ru, t["u"]) + _mm_nt(d_rw, t["w"])), 0.0)
        yield

        x_p = d_p * decay
        y_m = d_m * decay
        d_kb = _mm(y_m, k) + d_rw * gamma
        d_q = _mm(x_p, k) + d_qg * gamma
        d_k = _mm_tn(x_p, q) + _mm_tn(y_m, kb) + d_kb * beta + d_kt * e_tail
        d_v = d_ru * beta
        conv_bwd(i, rows, xq_ref, wq_ref, d_q, A_DK ** -0.5, dxq_ref, dwq_ref, carry[0])
        conv_bwd(i, rows, xk_ref, wk_ref, d_k, 1.0, dxk_ref, dwk_ref, carry[1])
        conv_bwd(i, rows, xv_ref, wv_ref, d_v, None, dxv_ref, dwv_ref, carry[2])

        d_beta = (jnp.sum(d_ru * v, axis=1, keepdims=True)
                  + jnp.sum(d_kb * k, axis=1, keepdims=True))
        z = d_p * t["p"] + d_m * t["m"]
        eps_tail = jnp.sum(d_kt * k, axis=1, keepdims=True) * e_tail
        d_gc = (jnp.sum(z, axis=1, keepdims=True) - _colsum_as_col(z)
                + jnp.sum(d_qg * q, axis=1, keepdims=True) * gamma
                + jnp.sum(d_rw * kb, axis=1, keepdims=True) * gamma
                - eps_tail)
        d_glast = _block_sum(same, eps_tail) + gamma_last * sds_scr[...][:, 0:1]
        yield
        rcol = lax.broadcasted_iota(jnp.int32, (SUPER, 1), 0)
        d_gc = d_gc + jnp.where(rcol % CHUNK == CHUNK - 1, d_glast, 0.0)
        dgc_ref[rows, :] += (jnp.where(lane == head, d_beta, 0.0)
                             + jnp.where(lane == A_HEADS + head, d_gc, 0.0))

    last = n_super - 1
    _drain(stage_p(last, 1))
    ds_scr[...] = jnp.zeros_like(ds_scr)
    _interleave(stage_s(last, 1), stage_p(last - 1, 0))

    def pair(k, carry):
        i = last - 1 - 2 * k
        _interleave(stage_s(i, 0), stage_f(i + 1, 1), stage_p(i - 1, 1))
        _interleave(stage_s(i - 1, 1), stage_f(i, 0), stage_p(i - 2, 0))
        return carry

    lax.fori_loop(0, n_super // 2 - 1, pair, 0)
    _interleave(stage_s(0, 0), stage_f(1, 1))
    _drain(stage_f(0, 0))


def gdn_bwd(q, k, v, gates_col, gates_row, tinv, states, d_og, o, proj_a, conv_w8, norm_g8, n_seq):
    t_rows = q.shape[0]
    s_len = t_rows // n_seq
    n_chunks = s_len // CHUNK
    qk_spec = pl.BlockSpec((s_len, A_DK), lambda b, h: (b, h))
    v_spec = pl.BlockSpec((s_len, A_DV), lambda b, h: (b, h))
    gate_spec = pl.BlockSpec((s_len, 128), lambda b, h: (b, 0))
    gain_spec = pl.BlockSpec((8, A_DV), lambda b, h: (0, 0))
    dw_qk_spec = pl.BlockSpec((None, 8, A_DK), lambda b, h: (b, 0, h))
    dw_v_spec = pl.BlockSpec((None, 8, A_DV), lambda b, h: (b, 0, h))
    ops_set = _gdn_scan_scratch(SUPER)[1:] + [pltpu.VMEM((SUPER, A_DV), F32)]
    res_set = [pltpu.VMEM((SUPER, A_DV), F32), pltpu.VMEM((SUPER, A_DV), F32),
               pltpu.VMEM((SUPER, A_DK), F32), pltpu.VMEM((SUPER, A_DK), F32),
               pltpu.VMEM((SUPER, A_DK), F32), pltpu.VMEM((SUPER, 128), F32), pltpu.VMEM((SUPER, A_DV), F32)]
    return pl.pallas_call(
        _gdn_bwd_body, name="gdn_bwd", grid=(n_seq, A_HEADS),
        in_specs=_gdn_in_specs(s_len, s_len // SUPER, False) + [
            pl.BlockSpec((s_len, CHUNK), lambda b, h: (b * A_HEADS + h, 0)),
            pl.BlockSpec((None, n_chunks, A_DK, A_DV), lambda b, h: (b * A_HEADS + h, 0, 0, 0)),
            v_spec, v_spec,
            pl.BlockSpec((s_len, A_DV), lambda b, h: (b, A_Z_COL + h)),
            gain_spec,
            pl.BlockSpec((s_len, A_DK), lambda b, h: (b, h)),
            pl.BlockSpec((s_len, A_DK), lambda b, h: (b, A_K_COL + h)),
            pl.BlockSpec((s_len, A_DV), lambda b, h: (b, A_V_COL + h)),
            pl.BlockSpec((8, A_DK), lambda b, h: (0, h)),
            pl.BlockSpec((8, A_DK), lambda b, h: (0, A_K_COL + h)),
            pl.BlockSpec((8, A_DV), lambda b, h: (0, A_V_COL + h)),
        ],
        out_specs=[qk_spec, qk_spec, v_spec, gate_spec, v_spec, gain_spec, dw_qk_spec, dw_qk_spec, dw_v_spec],
        out_shape=[
            jax.ShapeDtypeStruct((t_rows, A_QK), BF16),
            jax.ShapeDtypeStruct((t_rows, A_QK), BF16),
            jax.ShapeDtypeStruct((t_rows, A_VW), BF16),
            jax.ShapeDtypeStruct((t_rows, 128), F32),
            jax.ShapeDtypeStruct((t_rows, A_VW), BF16),
            jax.ShapeDtypeStruct((8, A_DV), F32),
            jax.ShapeDtypeStruct((n_seq, 8, A_QK), F32),
            jax.ShapeDtypeStruct((n_seq, 8, A_QK), F32),
            jax.ShapeDtypeStruct((n_seq, 8, A_VW), F32),
        ],
        scratch_shapes=(_gdn_scan_scratch(SUPER)[:1]
                        + [pltpu.VMEM((8, A_DK), F32), pltpu.VMEM((8, A_DK), F32), pltpu.VMEM((8, A_DV), F32)]
                        + 2 * ops_set + 2 * res_set),
        compiler_params=_params(("arbitrary", "arbitrary")),
    )(q, k, v, gates_col, gates_row, tinv, states, d_og, o, proj_a, norm_g8,
      proj_a, proj_a, proj_a, conv_w8, conv_w8, conv_w8)


GATE_TILE = 512


def _softplus(y):
    return jnp.maximum(y, 0.0) + jnp.log1p(jnp.exp(-jnp.abs(y)))


def _gate_values(x, prm):
    beta = _sigmoid(x)
    y = x + prm[1:2, :]
    neg_a = -jnp.exp(prm[0:1, :])
    g = neg_a * _softplus(y)
    return beta, y, neg_a, g


def _a_gates_fwd_body(x_ref, prm_ref, gc_ref, gr_ref):
    x = x_ref[...]
    tm = x.shape[0]
    beta, _, _, g = _gate_values(x, prm_ref[...])
    in_chunk = lax.broadcasted_iota(jnp.int32, (tm, 1), 0) % CHUNK
    s = 1
    while s < CHUNK:
        g = g + jnp.where(in_chunk >= s, pltpu.roll(g, s, 0), 0.0)
        s *= 2
    lane = lax.broadcasted_iota(jnp.int32, x.shape, 1)
    out = jnp.where(lane < A_HEADS, beta, jnp.where(lane < 2 * A_HEADS, g, 0.0))
    gc_ref[...] = out
    gr_ref[...] = out.T[0:2 * A_HEADS, :]


def a_gates_fwd(proj_a, prm, n_seq):
    t_rows = proj_a.shape[0]
    s_len = t_rows // n_seq
    tm = min(GATE_TILE, s_len)
    n_t = s_len // tm
    return pl.pallas_call(
        _a_gates_fwd_body, name="a_gates_fwd", grid=(n_seq, n_t),
        in_specs=[pl.BlockSpec((tm, 128), lambda b, i: (b * n_t + i, A_GATE_COL)),
                  pl.BlockSpec((8, 128), lambda b, i: (0, 0))],
        out_specs=[pl.BlockSpec((tm, 128), lambda b, i: (b * n_t + i, 0)),
                   pl.BlockSpec((None, 2 * A_HEADS, tm), lambda b, i: (b, 0, i))],
        out_shape=[jax.ShapeDtypeStruct((t_rows, 128), F32),
                   jax.ShapeDtypeStruct((n_seq, 2 * A_HEADS, s_len), F32)],
        compiler_params=_params(("arbitrary", "arbitrary")),
    )(proj_a, prm)


def _a_gates_bwd_body(x_ref, prm_ref, dgc_ref, dx_ref, dprm_ref):
    first = (pl.program_id(0) == 0) & (pl.program_id(1) == 0)
    x = x_ref[...]
    tm = x.shape[0]
    beta, y, neg_a, g = _gate_values(x, prm_ref[...])
    d = dgc_ref[...]
    in_chunk = lax.broadcasted_iota(jnp.int32, (tm, 1), 0) % CHUNK
    dg = d
    s = 1
    while s < CHUNK:
        dg = dg + jnp.where(in_chunk < CHUNK - s, pltpu.roll(dg, tm - s, 0), 0.0)
        s *= 2
    lane = lax.broadcasted_iota(jnp.int32, x.shape, 1)
    is_decay = (lane >= A_HEADS) & (lane < 2 * A_HEADS)
    d_alogit = jnp.where(is_decay, dg * neg_a * _sigmoid(y), 0.0)
    dx_ref[...] = jnp.where(lane < A_HEADS, d * beta * (1.0 - beta), d_alogit).astype(dx_ref.dtype)

    @pl.when(first)
    def _():
        dprm_ref[...] = jnp.zeros_like(dprm_ref)

    dprm_ref[0:1, :] += jnp.sum(jnp.where(is_decay, dg * g, 0.0), axis=0, keepdims=True)
    dprm_ref[1:2, :] += jnp.sum(d_alogit, axis=0, keepdims=True)


def a_gates_bwd(proj_a, prm, dgates_col, n_seq):
    t_rows = proj_a.shape[0]
    s_len = t_rows // n_seq
    tm = min(GATE_TILE, s_len)
    n_t = s_len // tm
    return pl.pallas_call(
        _a_gates_bwd_body, name="a_gates_bwd", grid=(n_seq, n_t),
        in_specs=[pl.BlockSpec((tm, 128), lambda b, i: (b * n_t + i, A_GATE_COL)),
                  pl.BlockSpec((8, 128), lambda b, i: (0, 0)),
                  pl.BlockSpec((tm, 128), lambda b, i: (b * n_t + i, 0))],
        out_specs=[pl.BlockSpec((tm, 128), lambda b, i: (b * n_t + i, 0)),
                   pl.BlockSpec((8, 128), lambda b, i: (0, 0))],
        out_shape=[jax.ShapeDtypeStruct((t_rows, 128), BF16),
                   jax.ShapeDtypeStruct((8, 128), F32)],
        compiler_params=_params(("arbitrary", "arbitrary")),
    )(proj_a, prm, dgates_col)


ROW_TILE = 1024
A_Z_COL = (2 * A_QK + A_VW) // A_DV


def _silu_parts(z):
    sig = _sigmoid(z)
    return z * sig, sig * (1.0 + z * (1.0 - sig))


NEG_BIG = -1e30
ATT_SCALE = B_DH ** -0.5


def _swap_rope_halves(x):
    src = lax.broadcasted_iota(jnp.int32, (B_DH, B_DH), 0)
    dst = lax.broadcasted_iota(jnp.int32, (B_DH, B_DH), 1)
    pick = ((dst < ROPE_HALF) & (src == dst + ROPE_HALF)) | (
        (dst >= ROPE_HALF) & (dst < ROPE_DIMS) & (src == dst - ROPE_HALF))
    return jnp.dot(_bf(x), pick.astype(BF16), preferred_element_type=F32)


def _norm_rope(x, gain, cos_t, sin_t):
    r = lax.rsqrt(jnp.mean(x * x, axis=1, keepdims=True) + EPS)
    xn = x * r * gain
    return xn * cos_t + _swap_rope_halves(xn) * sin_t, r


def _norm_rope_bwd(x, r, gain, cos_t, sin_t, dy):
    d_xn = dy * cos_t + _swap_rope_halves(dy * sin_t)
    xr = x * r
    u = d_xn * gain
    dx = r * u - xr * (r * r) * jnp.mean(u * x, axis=1, keepdims=True)
    return dx, jnp.sum(d_xn * xr, axis=0, keepdims=True)


def _stream_rows(idx, dilation, s_len):
    nb = s_len // dilation // B_BLOCK
    r = idx // nb
    m = idx % nb
    cur = r + m * (B_BLOCK * dilation)
    prev = r + jnp.maximum(m - 1, 0) * (B_BLOCK * dilation)
    return cur, prev, m > 0


def _rows(start, dilation):
    if dilation == 1:
        return pl.ds(start, B_BLOCK)
    return pl.ds(start, B_BLOCK, stride=dilation)


ATT_UNROLL = 16


def _band_mask(has_prev):
    qi = lax.broadcasted_iota(jnp.int32, (B_BLOCK, 2 * B_BLOCK), 0)
    kj = lax.broadcasted_iota(jnp.int32, (B_BLOCK, 2 * B_BLOCK), 1)
    return ((kj < B_BLOCK) & (kj >= qi) & has_prev) | ((kj >= B_BLOCK) & (kj - B_BLOCK <= qi))


def _attn_fwd_body(qkv_ref, z_ref, cos_ref, sin_ref, gain_ref, og_ref, o_ref, lse_ref,
                   qn_scr, kn_scr, og_scr, lg_scr):
    head, grp = pl.program_id(1), pl.program_id(2)
    s_len = z_ref.shape[0]
    n_blocks = s_len // B_BLOCK
    cos_t, sin_t = cos_ref[...], sin_ref[...]

    for gi, dil in enumerate(B_DILATIONS):
        @pl.when(grp == gi)
        def _(gi=gi, dil=dil):
            qn_scr[...], _ = _norm_rope(qkv_ref[0], gain_ref[gi:gi + 1, :], cos_t, sin_t)
            kn_scr[...], _ = _norm_rope(qkv_ref[1], gain_ref[B_GROUPS + gi:B_GROUPS + gi + 1, :], cos_t, sin_t)

            ones = jnp.ones((2 * B_BLOCK, B_DH), BF16)

            def blocks(it, carry):
                scored = []
                for j in range(ATT_UNROLL):
                    cur, prev, has_prev = _stream_rows(it * ATT_UNROLL + j, dil, s_len)
                    rc, rp = _rows(cur, dil), _rows(prev, dil)
                    k2 = jnp.concatenate([kn_scr[rp, :], kn_scr[rc, :]], axis=0)
                    scored.append((rc, rp, has_prev, _mm_nt(qn_scr[rc, :], k2) * ATT_SCALE))
                summed = []
                for rc, rp, has_prev, s in scored:
                    s = jnp.where(_band_mask(has_prev), s, NEG_BIG)
                    mx = jnp.max(s, axis=1, keepdims=True)
                    v2 = jnp.concatenate([qkv_ref.at[2][rp, :], qkv_ref.at[2][rc, :]], axis=0)
                    acc = jnp.dot(_bf(jnp.exp(s - mx)), jnp.concatenate([_bf(v2), ones], axis=1),
                                  preferred_element_type=F32)
                    summed.append((rc, mx, acc))
                for rc, mx, acc in summed:
                    den = acc[:, B_DH:B_DH + 1]
                    og_scr.at[gi][rc, :] = acc[:, :B_DH] / den
                    lg_scr.at[gi][rc, :] = jnp.broadcast_to(mx + jnp.log(den), (B_BLOCK, B_DH))
                return carry

            lax.fori_loop(0, n_blocks // ATT_UNROLL, blocks, 0)

    @pl.when(grp == B_GROUPS - 1)
    def _():
        l0, l1, l2 = lg_scr[0], lg_scr[1], lg_scr[2]
        mx = jnp.maximum(jnp.maximum(l0, l1), l2)
        w0, w1, w2 = jnp.exp(l0 - mx), jnp.exp(l1 - mx), jnp.exp(l2 - mx)
        den = w0 + w1 + w2
        o = (w0 * og_scr[0] + w1 * og_scr[1] + w2 * og_scr[2]) / den
        silu, _ = _silu_parts(z_ref[...])
        o_ref[...] = o
        og_ref[...] = (o * silu).astype(og_ref.dtype)
        @pl.when(head == 0)
        def _():
            lse_ref[...] = jnp.zeros_like(lse_ref)

        lane = lax.broadcasted_iota(jnp.int32, o.shape, 1)
        lse_ref[...] = jnp.where(lane == head, mx + jnp.log(den), lse_ref[...])


def attn_fwd(proj_b, cos_t, sin_t, gains8, n_seq):
    t_rows = proj_b.shape[1]
    s_len = t_rows // n_seq
    head_blk = pl.BlockSpec((s_len, B_DH), lambda b, h, g: (b, h))
    seq_blk = pl.BlockSpec((s_len, 128), lambda b, h, g: (b, 0))
    return pl.pallas_call(
        _attn_fwd_body, name="attn_fwd", grid=(n_seq, B_HEADS, B_GROUPS),
        in_specs=[
            pl.BlockSpec((3, s_len, B_DH), lambda b, h, g: (g, b, h)),
            pl.BlockSpec((None, s_len, B_DH), lambda b, h, g: (B_PIECES - 1, b, h)),
            seq_blk, seq_blk,
            pl.BlockSpec((8, 128), lambda b, h, g: (0, 0)),
        ],
        out_specs=[head_blk, head_blk, seq_blk],
        out_shape=[jax.ShapeDtypeStruct((t_rows, B_W), BF16),
                   jax.ShapeDtypeStruct((t_rows, B_W), F32),
                   jax.ShapeDtypeStruct((t_rows, 128), F32)],
        scratch_shapes=[pltpu.VMEM((s_len, B_DH), F32), pltpu.VMEM((s_len, B_DH), F32),
                        pltpu.VMEM((B_GROUPS, s_len, B_DH), F32), pltpu.VMEM((B_GROUPS, s_len, B_DH), F32)],
        compiler_params=_params(("arbitrary", "arbitrary", "arbitrary")),
    )(proj_b, proj_b, cos_t, sin_t, gains8)


def _attn_bwd_body(qkv_ref, z_ref, cos_ref, sin_ref, gain_ref, dog_ref, o_ref, lse_ref,
                   dqkv_ref, dz_ref, dgain_ref,
                   qn_scr, kn_scr, dqn_scr, dkn_scr, do_scr, dl_scr, ls_scr, dv_scr):
    head, grp = pl.program_id(1), pl.program_id(2)
    first = (pl.program_id(0) == 0) & (head == 0) & (grp == 0)
    s_len = z_ref.shape[0]
    n_blocks = s_len // B_BLOCK
    cos_t, sin_t = cos_ref[...], sin_ref[...]

    @pl.when(first)
    def _():
        dgain_ref[...] = jnp.zeros_like(dgain_ref)

    @pl.when(grp == 0)
    def _():
        d_og, o = dog_ref[...], o_ref[...]
        silu, dsilu = _silu_parts(z_ref[...])
        d_o = d_og * silu
        dz_ref[...] = (d_og * o * dsilu).astype(dz_ref.dtype)
        do_scr[...] = d_o
        dl_scr[...] = jnp.broadcast_to(jnp.sum(d_o * o, axis=1, keepdims=True), o.shape)
        lane = lax.broadcasted_iota(jnp.int32, o.shape, 1)
        ls_scr[...] = jnp.broadcast_to(
            jnp.sum(jnp.where(lane == head, lse_ref[...], 0.0), axis=1, keepdims=True), o.shape)

    for gi, dil in enumerate(B_DILATIONS):
        @pl.when(grp == gi)
        def _(gi=gi, dil=dil):
            q_raw, k_raw = qkv_ref[0], qkv_ref[1]
            gq = gain_ref[gi:gi + 1, :]
            gk = gain_ref[B_GROUPS + gi:B_GROUPS + gi + 1, :]
            qn_scr[...], rq = _norm_rope(q_raw, gq, cos_t, sin_t)
            kn_scr[...], rk = _norm_rope(k_raw, gk, cos_t, sin_t)
            def blocks(it, carry):
                scored = []
                for j in range(ATT_UNROLL):
                    cur, prev, has_prev = _stream_rows(it * ATT_UNROLL + j, dil, s_len)
                    rc, rp = _rows(cur, dil), _rows(prev, dil)
                    qb, d_ob = _bf(qn_scr[rc, :]), _bf(do_scr[rc, :])
                    k2 = _bf(jnp.concatenate([kn_scr[rp, :], kn_scr[rc, :]], axis=0))
                    v2 = _bf(jnp.concatenate([qkv_ref.at[2][rp, :], qkv_ref.at[2][rc, :]], axis=0))
                    scored.append((rc, rp, has_prev, qb, d_ob, k2,
                                   _mm_nt(qb, k2) * ATT_SCALE, _mm_nt(d_ob, v2)))
                grads = []
                for rc, rp, has_prev, qb, d_ob, k2, s, d_p in scored:
                    p = jnp.exp(jnp.where(_band_mask(has_prev), s - ls_scr[rc, :][:, 0:1], NEG_BIG))
                    ds = _bf(p * (d_p - dl_scr[rc, :][:, 0:1]))
                    grads.append((rc, rp, has_prev,
                                  _mm(ds, k2) * ATT_SCALE, _mm_tn(ds, qb) * ATT_SCALE, _mm_tn(_bf(p), d_ob)))
                for j, (rc, rp, has_prev, dq, dk2, dv2) in enumerate(grads):
                    dqn_scr[rc, :] = dq
                    if j == 0:
                        @pl.when(has_prev)
                        def _():
                            dkn_scr[rp, :] += dk2[:B_BLOCK]
                            dv_scr[rp, :] += dv2[:B_BLOCK]
                    if j + 1 < ATT_UNROLL:
                        dkn_scr[rc, :] = dk2[B_BLOCK:] + grads[j + 1][4][:B_BLOCK]
                        dv_scr[rc, :] = dv2[B_BLOCK:] + grads[j + 1][5][:B_BLOCK]
                    else:
                        dkn_scr[rc, :] = dk2[B_BLOCK:]
                        dv_scr[rc, :] = dv2[B_BLOCK:]
                return carry

            lax.fori_loop(0, n_blocks // ATT_UNROLL, blocks, 0)
            dq, dgq = _norm_rope_bwd(q_raw, rq, gq, cos_t, sin_t, dqn_scr[...])
            dk, dgk = _norm_rope_bwd(k_raw, rk, gk, cos_t, sin_t, dkn_scr[...])
            dqkv_ref[0] = dq.astype(dqkv_ref.dtype)
            dqkv_ref[1] = dk.astype(dqkv_ref.dtype)
            dqkv_ref[2] = dv_scr[...].astype(dqkv_ref.dtype)
            dgain_ref[gi:gi + 1, :] += dgq
            dgain_ref[B_GROUPS + gi:B_GROUPS + gi + 1, :] += dgk


def attn_bwd(proj_b, cos_t, sin_t, gains8, d_og, o, lse, n_seq):
    t_rows = proj_b.shape[1]
    s_len = t_rows // n_seq
    head_blk = pl.BlockSpec((s_len, B_DH), lambda b, h, g: (b, h))
    seq_blk = pl.BlockSpec((s_len, 128), lambda b, h, g: (b, 0))
    grp_blk = pl.BlockSpec((3, s_len, B_DH), lambda b, h, g: (g, b, h))
    gain_blk = pl.BlockSpec((8, 128), lambda b, h, g: (0, 0))
    return pl.pallas_call(
        _attn_bwd_body, name="attn_bwd", grid=(n_seq, B_HEADS, B_GROUPS),
        in_specs=[
            grp_blk,
            pl.BlockSpec((None, s_len, B_DH), lambda b, h, g: (B_PIECES - 1, b, h)),
            seq_blk, seq_blk, gain_blk, head_blk, head_blk, seq_blk,
        ],
        out_specs=[grp_blk, head_blk, gain_blk],
        out_shape=[jax.ShapeDtypeStruct((3 * B_GROUPS, t_rows, B_W), BF16),
                   jax.ShapeDtypeStruct((t_rows, B_W), BF16),
                   jax.ShapeDtypeStruct((8, 128), F32)],
        scratch_shapes=[pltpu.VMEM((s_len, B_DH), F32) for _ in range(8)],
        compiler_params=_params(("arbitrary", "arbitrary", "arbitrary")),
    )(proj_b, proj_b, cos_t, sin_t, gains8, d_og, o, lse)


def rope_tables(positions):
    inv_freq = ROPE_THETA ** (-jnp.arange(0, ROPE_DIMS, 2, dtype=F32) / ROPE_DIMS)
    ang = positions.astype(F32)[:, None] * inv_freq
    cos, sin = jnp.cos(ang), jnp.sin(ang)
    t_rows = positions.shape[0]
    rest = B_DH - ROPE_DIMS
    cos_t = jnp.concatenate([cos, cos, jnp.ones((t_rows, rest), F32)], axis=1)
    sin_t = jnp.concatenate([-sin, sin, jnp.zeros((t_rows, rest), F32)], axis=1)
    return cos_t, sin_t


def _rms_fwd_body(x_ref, g_ref, *rest, layer):
    h_ref, ht_ref = rest[-2:]
    x = x_ref[...]
    r = lax.rsqrt(jnp.mean(x * x, axis=1, keepdims=True) + EPS)
    h = x * r * g_ref[layer:layer + 1, :]
    h_ref[...] = h.astype(h_ref.dtype)
    ht_ref[...] = h.T.astype(ht_ref.dtype)


def rms_fwd(x, gains8, layer, after=None):
    t_rows, d = x.shape
    tm = min(ROW_TILE, t_rows)
    in_specs = [pl.BlockSpec((tm, d), lambda i: (i, 0)), pl.BlockSpec((8, d), lambda i: (0, 0))]
    args = [x, gains8]
    if after is not None:
        in_specs.append(HBM_SPEC)
        args.append(after)
    return pl.pallas_call(
        functools.partial(_rms_fwd_body, layer=layer), name=f"rms_fwd_{layer}", grid=(t_rows // tm,),
        in_specs=in_specs,
        out_specs=[pl.BlockSpec((tm, d), lambda i: (i, 0)), pl.BlockSpec((d, tm), lambda i: (0, i))],
        out_shape=[jax.ShapeDtypeStruct((t_rows, d), BF16), jax.ShapeDtypeStruct((d, t_rows), BF16)],
        compiler_params=_params(("arbitrary",)),
    )(*args)


def _rms_bwd_body(x_ref, g_ref, dh_ref, res_ref, dx_ref, dg_ref, *, layer):
    x, dh = x_ref[...], dh_ref[...]
    r = lax.rsqrt(jnp.mean(x * x, axis=1, keepdims=True) + EPS)
    xr = x * r
    u = dh * g_ref[layer:layer + 1, :]
    dx_ref[...] = res_ref[...] + r * u - xr * (r * r) * jnp.mean(u * x, axis=1, keepdims=True)

    @pl.when(pl.program_id(0) == 0)
    def _():
        dg_ref[...] = jnp.zeros_like(dg_ref)

    dg_ref[0:1, :] += jnp.sum(dh * xr, axis=0, keepdims=True)


def rms_bwd(x, gains8, layer, dh, d_res):
    t_rows, d = x.shape
    tm = min(ROW_TILE, t_rows)
    blk = pl.BlockSpec((tm, d), lambda i: (i, 0))
    gblk = pl.BlockSpec((8, d), lambda i: (0, 0))
    return pl.pallas_call(
        functools.partial(_rms_bwd_body, layer=layer), name=f"rms_bwd_{layer}", grid=(t_rows // tm,),
        in_specs=[blk, gblk, blk, blk],
        out_specs=[blk, gblk],
        out_shape=[jax.ShapeDtypeStruct((t_rows, d), F32), jax.ShapeDtypeStruct((8, d), F32)],
        compiler_params=_params(("arbitrary",)),
    )(x, gains8, dh, d_res)


def _piece_col(p):
    return jnp.where(p < 3 * B_GROUPS, (p % 3) * B_GROUPS + p // 3, 3 * B_GROUPS)


def _mm_nn_body(a_ref, w_ref, *rest, has_res, a_resident):
    o_ref = rest[-1]
    tm = o_ref.shape[0]
    a = a_ref[pl.ds(pl.multiple_of(pl.program_id(1) * tm, tm), tm), :] if a_resident else a_ref[...]
    acc = jnp.dot(a, w_ref[...], preferred_element_type=F32)
    if has_res:
        acc = acc + rest[0][...]
    o_ref[...] = acc


def mm_nn(a, w, residual=None, *, tn, name, a_resident=False):
    m, k = a.shape
    n = w.shape[1]
    tm = min(NT_ROW_TILE, m)
    a_spec = pl.BlockSpec((m, k), lambda j, i: (0, 0)) if a_resident else pl.BlockSpec((tm, k), lambda j, i: (i, 0))
    in_specs = [a_spec, pl.BlockSpec((k, tn), lambda j, i: (0, j))]
    args = [a, w]
    if residual is not None:
        in_specs.append(pl.BlockSpec((tm, tn), lambda j, i: (i, j)))
        args.append(residual)
    return pl.pallas_call(
        functools.partial(_mm_nn_body, has_res=residual is not None, a_resident=a_resident), name=name,
        grid=(n // tn, m // tm),
        in_specs=in_specs,
        out_specs=pl.BlockSpec((tm, tn), lambda j, i: (i, j)),
        out_shape=jax.ShapeDtypeStruct((m, n), F32),
        compiler_params=_params(("arbitrary", "arbitrary")),
    )(*args)


def mm_nn_pieces(a, w, *, name):
    m, k = a.shape
    tm = min(NT_ROW_TILE, m)
    return pl.pallas_call(
        functools.partial(_mm_nn_body, has_res=False, a_resident=True), name=name, grid=(B_PIECES, m // tm),
        in_specs=[pl.BlockSpec((m, k), lambda p, i: (0, 0)),
                  pl.BlockSpec((k, B_W), lambda p, i: (0, _piece_col(p)))],
        out_specs=pl.BlockSpec((None, tm, B_W), lambda p, i: (p, i, 0)),
        out_shape=jax.ShapeDtypeStruct((B_PIECES, m, B_W), F32),
        compiler_params=_params(("arbitrary", "arbitrary")),
    )(a, w)


NT_ROW_TILE = 1024


def _mm_nt_body(g_ref, w_ref, *rest, has_init):
    o_ref = rest[-1]
    j = pl.program_id(1)
    part = lax.dot_general(_bf(g_ref[...]), w_ref[...], (((1,), (1,)), ((), ())), preferred_element_type=F32)

    @pl.when(j == 0)
    def _():
        o_ref[...] = part + rest[0][...] if has_init else part

    @pl.when(j > 0)
    def _():
        o_ref[...] += part


def mm_nt(g, w, init=None, *, tn, col_off=0, name, after=None):
    m, n = g.shape
    k = w.shape[0]
    tm = min(NT_ROW_TILE, m)
    in_specs = [pl.BlockSpec((tm, tn), lambda i, j: (i, j)),
                pl.BlockSpec((k, tn), lambda i, j: (0, col_off + j))]
    args = [g, w]
    if init is not None:
        in_specs.append(pl.BlockSpec((tm, k), lambda i, j: (i, 0)))
        args.append(init)
    if after is not None:
        in_specs.append(HBM_SPEC)
        args.append(after)
    return pl.pallas_call(
        functools.partial(_mm_nt_body, has_init=init is not None), name=name, grid=(m // tm, n // tn),
        in_specs=in_specs,
        out_specs=pl.BlockSpec((tm, k), lambda i, j: (i, 0)),
        out_shape=jax.ShapeDtypeStruct((m, k), F32),
        compiler_params=_params(("arbitrary", "arbitrary")),
    )(*args)


def mm_nt_multi(gs, w, *, tn, name, after=None):
    m = gs[0].shape[0]
    k = w.shape[0]
    tm = min(NT_ROW_TILE, m)
    tiles = [g.shape[1] // tn for g in gs]
    starts = [sum(tiles[:i]) for i in range(len(gs))]

    def body(*refs):
        g_refs, w_ref, o_ref = refs[:len(gs)], refs[len(gs)], refs[-1]
        j = pl.program_id(1)
        for g_ref, lo, cnt in zip(g_refs, starts, tiles):
            @pl.when((j >= lo) & (j < lo + cnt))
            def _(g_ref=g_ref):
                part = lax.dot_general(_bf(g_ref[...]), w_ref[...], (((1,), (1,)), ((), ())),
                                       preferred_element_type=F32)

                @pl.when(j == 0)
                def _():
                    o_ref[...] = part

                @pl.when(j > 0)
                def _():
                    o_ref[...] += part

    def g_spec(lo, cnt):
        return pl.BlockSpec((tm, tn), lambda i, j: (i, jnp.clip(j - lo, 0, cnt - 1)))

    in_specs = [g_spec(lo, cnt) for lo, cnt in zip(starts, tiles)] + [pl.BlockSpec((k, tn), lambda i, j: (0, j))]
    args = list(gs) + [w]
    if after is not None:
        in_specs.append(HBM_SPEC)
        args.append(after)
    return pl.pallas_call(
        body, name=name, grid=(m // tm, sum(tiles)),
        in_specs=in_specs,
        out_specs=pl.BlockSpec((tm, k), lambda i, j: (i, 0)),
        out_shape=jax.ShapeDtypeStruct((m, k), F32),
        compiler_params=_params(("arbitrary", "arbitrary")),
    )(*args)


def mm_nt_pieces(g9, gz, w, *, name):
    n_p, m, _ = g9.shape
    k = w.shape[0]
    tm = min(2 * NT_ROW_TILE, m)

    def body(g_ref, z_ref, w_ref, o_ref):
        p = pl.program_id(1)

        def accumulate(src):
            part = lax.dot_general(_bf(src[...]), w_ref[...], (((1,), (1,)), ((), ())), preferred_element_type=F32)

            @pl.when(p == 0)
            def _():
                o_ref[...] = part

            @pl.when(p > 0)
            def _():
                o_ref[...] += part

        @pl.when(p < n_p)
        def _():
            accumulate(g_ref)

        @pl.when(p == n_p)
        def _():
            accumulate(z_ref)

    return pl.pallas_call(
        body, name=name, grid=(m // tm, n_p + 1),
        in_specs=[pl.BlockSpec((None, tm, B_W), lambda i, p: (jnp.minimum(p, n_p - 1), i, 0)),
                  pl.BlockSpec((tm, B_W), lambda i, p: (i, 0)),
                  pl.BlockSpec((k, B_W), lambda i, p: (0, _piece_col(p)))],
        out_specs=pl.BlockSpec((tm, k), lambda i, p: (i, 0)),
        out_shape=jax.ShapeDtypeStruct((m, k), F32),
        compiler_params=_params(("arbitrary", "arbitrary")),
    )(g9, gz, w)


def _mm_tn_body(a_ref, g_ref, o_ref, *, a_is_transposed):
    lhs_dim = 1 if a_is_transposed else 0
    o_ref[...] = lax.dot_general(a_ref[...], _bf(g_ref[...]), (((lhs_dim,), (0,)), ((), ())),
                                 preferred_element_type=F32).astype(o_ref.dtype)


def mm_tn(a, g, *, tn, out_dtype, name, a_is_transposed=False):
    k = a.shape[0] if a_is_transposed else a.shape[1]
    m, n = g.shape
    return pl.pallas_call(
        functools.partial(_mm_tn_body, a_is_transposed=a_is_transposed), name=name, grid=(n // tn,),
        in_specs=[pl.BlockSpec(a.shape, lambda j: (0, 0)), pl.BlockSpec((m, tn), lambda j: (0, j))],
        out_specs=pl.BlockSpec((k, tn), lambda j: (0, j)),
        out_shape=jax.ShapeDtypeStruct((k, n), out_dtype),
        compiler_params=_params(("arbitrary",)),
    )(a, g)


def mm_tn_multi(at, gs, *, unit, out_dtype, name):
    k, m = at.shape
    counts = [g.shape[1] // unit for g in gs]
    starts = [sum(counts[:i]) for i in range(len(gs))]

    def body(at_ref, *refs):
        g_refs, o_ref = refs[:-1], refs[-1]
        u = pl.program_id(0)
        for g_ref, lo, cnt in zip(g_refs, starts, counts):
            @pl.when((u >= lo) & (u < lo + cnt))
            def _(g_ref=g_ref):
                _mm_tn_body(at_ref, g_ref, o_ref, a_is_transposed=True)

    def g_spec(lo, cnt):
        return pl.BlockSpec((m, unit), lambda u: (0, jnp.clip(u - lo, 0, cnt - 1)))

    return pl.pallas_call(
        body, name=name, grid=(sum(counts),),
        in_specs=[pl.BlockSpec((k, m), lambda u: (0, 0))] + [g_spec(lo, cnt) for lo, cnt in zip(starts, counts)],
        out_specs=pl.BlockSpec((k, unit), lambda u: (0, u)),
        out_shape=jax.ShapeDtypeStruct((k, sum(counts) * unit), out_dtype),
        compiler_params=_params(("arbitrary",)),
    )(at, *gs)


B_UNIT = 256
B_IN_COLS = B_PIECES * B_W
B_SHARD_UNITS = B_IN_COLS // N_DEV // B_UNIT


def mm_tn_b_in(at, g9, gz, *, out_dtype, name):
    k, m = at.shape
    per_piece = B_W // B_UNIT
    body_one = functools.partial(_mm_tn_body, a_is_transposed=True)
    n_units = B_IN_COLS // B_UNIT

    def g_map(u):
        nat = jnp.minimum(u // per_piece, 3 * B_GROUPS - 1)
        piece = (nat % B_GROUPS) * 3 + nat // B_GROUPS
        return (piece, 0, u % per_piece)

    def body(a_ref, g_ref, z_ref, o_ref):
        u = pl.program_id(0)

        @pl.when(u < 3 * B_GROUPS * per_piece)
        def _():
            body_one(a_ref, g_ref, o_ref)

        @pl.when(u >= 3 * B_GROUPS * per_piece)
        def _():
            body_one(a_ref, z_ref, o_ref)

    return pl.pallas_call(
        body, name=name, grid=(n_units,),
        in_specs=[pl.BlockSpec((k, m), lambda u: (0, 0)),
                  pl.BlockSpec((None, m, B_UNIT), g_map),
                  pl.BlockSpec((m, B_UNIT), lambda u: (0, jnp.where(u < 3 * B_GROUPS * per_piece, 0, u % per_piece)))],
        out_specs=pl.BlockSpec((None, k, B_UNIT), lambda u: (u // B_SHARD_UNITS, 0, u % B_SHARD_UNITS)),
        out_shape=jax.ShapeDtypeStruct((N_DEV, k, B_IN_COLS // N_DEV), out_dtype),
        compiler_params=_params(("arbitrary",)),
    )(at, g9, gz)


def _out_norm_body(a_ref, w_ref, res_ref, g_ref, x_ref, h_ref, ht_ref, *, layer):
    x = jnp.dot(a_ref[...], w_ref[...], preferred_element_type=F32) + res_ref[...]
    x_ref[...] = x
    r = lax.rsqrt(jnp.mean(x * x, axis=1, keepdims=True) + EPS)
    h = x * r * g_ref[layer:layer + 1, :]
    h_ref[...] = h.astype(h_ref.dtype)
    ht_ref[...] = h.T.astype(ht_ref.dtype)


def out_and_norm(a, w, residual, gains8, layer):
    t_rows, k = a.shape
    d = w.shape[1]
    tm = min(ROW_TILE, t_rows)
    blk = pl.BlockSpec((tm, d), lambda i: (i, 0))
    return pl.pallas_call(
        functools.partial(_out_norm_body, layer=layer), name=f"out_norm_{layer}", grid=(t_rows // tm,),
        in_specs=[pl.BlockSpec((tm, k), lambda i: (i, 0)), pl.BlockSpec((k, d), lambda i: (0, 0)), blk,
                  pl.BlockSpec((8, d), lambda i: (0, 0))],
        out_specs=[blk, blk, pl.BlockSpec((d, tm), lambda i: (0, i))],
        out_shape=[jax.ShapeDtypeStruct((t_rows, d), F32), jax.ShapeDtypeStruct((t_rows, d), BF16),
                   jax.ShapeDtypeStruct((d, t_rows), BF16)],
        compiler_params=_params(("arbitrary",)),
    )(a, w, residual, gains8)


def _out_loss_body(a_ref, w_ref, res_ref, t_ref, dy_ref, loss_ref, acc, *, n_steps):
    i = pl.program_id(0)
    d = w_ref.shape[1]
    y = jnp.dot(a_ref[...], w_ref[...], preferred_element_type=F32) + res_ref[...]
    err = y - t_ref[...]
    dy_ref[...] = err * (1.0 / d)

    @pl.when(i == 0)
    def _():
        acc[...] = jnp.zeros_like(acc)

    acc[...] += jnp.sum(err * err, axis=0, keepdims=True)

    @pl.when(i == n_steps - 1)
    def _():
        total = jnp.sum(acc[...], axis=1, keepdims=True) * (0.5 / d)
        loss_ref[...] = jnp.broadcast_to(total, loss_ref.shape)


def out_and_loss(a, w, residual, target):
    t_rows, k = a.shape
    d = w.shape[1]
    tm = min(ROW_TILE, t_rows)
    n_steps = t_rows // tm
    blk = pl.BlockSpec((tm, d), lambda i: (i, 0))
    return pl.pallas_call(
        functools.partial(_out_loss_body, n_steps=n_steps), name="out_b_loss", grid=(n_steps,),
        in_specs=[pl.BlockSpec((tm, k), lambda i: (i, 0)), pl.BlockSpec((k, d), lambda i: (0, 0)), blk, blk],
        out_specs=[blk, pl.BlockSpec((8, 128), lambda i: (0, 0))],
        out_shape=[jax.ShapeDtypeStruct((t_rows, d), F32), jax.ShapeDtypeStruct((8, 128), F32)],
        scratch_shapes=[pltpu.VMEM((1, d), F32)],
        compiler_params=_params(("arbitrary",)),
    )(a, w, residual, target)


def _adamw_body(p_ref, w_ref, m_ref, v_ref, g_ref, d_ref, nm_ref, nv_ref):
    g = p_ref[0].astype(F32)
    for s in range(1, N_DEV):
        g = g + p_ref[s].astype(F32)
    w = w_ref[...]
    m = ADAM_B1 * m_ref[...] + (1.0 - ADAM_B1) * g
    v = ADAM_B2 * v_ref[...] + (1.0 - ADAM_B2) * (g * g)
    m_hat = m / (1.0 - ADAM_B1 ** ADAM_STEP)
    v_hat = v / (1.0 - ADAM_B2 ** ADAM_STEP)
    g_ref[...] = g
    d_ref[...] = -ADAM_LR * (m_hat / (jnp.sqrt(v_hat) + ADAM_EPS) + ADAM_WD * w)
    nm_ref[...] = m
    nv_ref[...] = v


def adamw(parts, w, m, v, *, name):
    _, r, c = w.shape
    tr = r if r <= 256 else 256
    blk = pl.BlockSpec((None, tr, c), lambda i: (0, i, 0))
    out = jax.ShapeDtypeStruct((1, r, c), F32)
    return pl.pallas_call(
        _adamw_body, name=name, grid=(r // tr,),
        in_specs=[pl.BlockSpec((N_DEV, tr, c), lambda i: (0, i, 0)), blk, blk, blk],
        out_specs=[blk, blk, blk, blk],
        out_shape=[out, out, out, out],
        compiler_params=_params(("arbitrary",)),
    )(parts, w, m, v)


MESH_ID = pl.DeviceIdType.MESH
HBM_SPEC = pl.BlockSpec(memory_space=pl.ANY)


def _my_place():
    return lax.axis_index("x"), lax.axis_index("y"), lax.axis_index("c")


def _flat(x, y, c):
    return 4 * x + 2 * y + c


def _all_gather_body(*refs, n):
    ins, outs = refs[:n], refs[n:2 * n]
    send_sems, recv_sems, local_sems = refs[2 * n:]
    x, y, c = _my_place()
    me, sibling = (x, y, c), (x, y, 1 - c)
    chips = [(1 - x, y), (x, 1 - y), (1 - x, 1 - y)]
    pending = []
    for a in range(n):
        src, out = ins[a], outs[a]

        def copy(k, block, to, from_input=False, a=a, src=src, out=out):
            slot = out.at[_flat(*block)]
            return pltpu.make_async_remote_copy(
                src_ref=src if from_input else slot, dst_ref=slot,
                send_sem=send_sems.at[7 * a + k], recv_sem=recv_sems.at[7 * a + k],
                device_id=to, device_id_type=MESH_ID)

        mine = pltpu.make_async_copy(src, out.at[_flat(*me)], local_sems.at[a])
        mine.start()
        first = [copy(0, me, sibling, True)] + [copy(1 + j, me, (*chip, c), True) for j, chip in enumerate(chips)]
        for cp in first:
            cp.start()
        pending.append((copy, mine, first))
    for copy, mine, first in pending:
        passed = [copy(4 + j, (*chip, c), sibling) for j, chip in enumerate(chips)]
        for j, chip in enumerate(chips):
            copy(1 + j, (*chip, c), me).wait_recv()
            passed[j].start()
        copy(0, sibling, me).wait_recv()
        for j, chip in enumerate(chips):
            copy(4 + j, (*chip, 1 - c), me).wait_recv()
        for cp in first + passed:
            cp.wait_send()
        mine.wait()


def all_gather(shards, *, name):
    n = len(shards)
    return pl.pallas_call(
        functools.partial(_all_gather_body, n=n), name=name,
        in_specs=[HBM_SPEC] * n, out_specs=[HBM_SPEC] * n,
        out_shape=[jax.ShapeDtypeStruct((N_DEV,) + s.shape, s.dtype) for s in shards],
        scratch_shapes=[pltpu.SemaphoreType.DMA((7 * n,)), pltpu.SemaphoreType.DMA((7 * n,)),
                        pltpu.SemaphoreType.DMA((n,))],
    )(*shards)


PEER_FLIPS = [(0, 0, 1), (1, 0, 0), (0, 1, 0), (1, 1, 0), (1, 0, 1), (0, 1, 1), (1, 1, 1)]


def _all_to_all_body(*refs, n):
    ins, outs = refs[:n], refs[n:2 * n]
    send_sems, recv_sems, local_sems = refs[2 * n:]
    x, y, c = _my_place()
    me = _flat(x, y, c)
    waits = []
    for a in range(n):
        src, out = ins[a], outs[a]
        mine = pltpu.make_async_copy(src.at[me], out.at[me], local_sems.at[a])
        mine.start()
        waits.append(mine)
        for k, (fx, fy, fc) in enumerate(PEER_FLIPS):
            peer = (1 - x if fx else x, 1 - y if fy else y, 1 - c if fc else c)
            theirs = _flat(*peer)
            sems = dict(send_sem=send_sems.at[7 * a + k], recv_sem=recv_sems.at[7 * a + k],
                        device_id=peer, device_id_type=MESH_ID)
            send = pltpu.make_async_remote_copy(src_ref=src.at[theirs], dst_ref=out.at[me], **sems)
            send.start()
            recv = pltpu.make_async_remote_copy(src_ref=src.at[theirs], dst_ref=out.at[theirs], **sems)
            waits.append((send, recv))
    for w in waits:
        if isinstance(w, tuple):
            w[0].wait_send()
            w[1].wait_recv()
        else:
            w.wait()


def all_to_all(parts, *, name):
    n = len(parts)
    return pl.pallas_call(
        functools.partial(_all_to_all_body, n=n), name=name,
        in_specs=[HBM_SPEC] * n, out_specs=[HBM_SPEC] * n,
        out_shape=[jax.ShapeDtypeStruct(p.shape, p.dtype) for p in parts],
        scratch_shapes=[pltpu.SemaphoreType.DMA((7 * n,)), pltpu.SemaphoreType.DMA((7 * n,)),
                        pltpu.SemaphoreType.DMA((n,))],
    )(*parts)


HBM_ONLY = pl.BlockSpec(memory_space=pltpu.HBM)
SEM_SPEC = pl.BlockSpec(memory_space=pltpu.SEMAPHORE)
DATAFLOW_EFFECT = pltpu.SideEffectType.DATAFLOW_SIDE_EFFECTING


def _split_copies(srcs, lands, send_sems, recv_sems, n, scatter):
    x, y, c = _my_place()
    me = _flat(x, y, c)
    pairs = []
    for a in range(n):
        for k, (fx, fy, fc) in enumerate(PEER_FLIPS):
            peer = (1 - x if fx else x, 1 - y if fy else y, 1 - c if fc else c)
            theirs = _flat(*peer)
            src = srcs[a].at[theirs] if scatter else srcs[a]
            sems = dict(send_sem=send_sems.at[7 * a + k], recv_sem=recv_sems.at[7 * a + k],
                        device_id=peer, device_id_type=MESH_ID)
            pairs.append((pltpu.make_async_remote_copy(src_ref=src, dst_ref=lands[a].at[me], **sems),
                          pltpu.make_async_remote_copy(src_ref=src, dst_ref=lands[a].at[theirs], **sems)))
    return pairs


def _exchange_start_body(*refs, n, scatter):
    srcs, lands = refs[:n], refs[n:2 * n]
    send_sems, recv_sems = refs[2 * n], refs[2 * n + 1]
    token = refs[-1]
    for send, _ in _split_copies(srcs, lands, send_sems, recv_sems, n, scatter):
        send.start()
    token[...] = jnp.zeros_like(token)


def exchange_start(srcs, lands, *, scatter, name):
    n = len(srcs)
    args = [pltpu.with_memory_space_constraint(t, pltpu.HBM) for t in list(srcs) + list(lands)]
    outs = pl.pallas_call(
        functools.partial(_exchange_start_body, n=n, scatter=scatter), name=name,
        out_shape=(pltpu.SemaphoreType.DMA((7 * n,)), pltpu.SemaphoreType.DMA((7 * n,)),
                   *[pltpu.HBM(t.shape, t.dtype) for t in args],
                   jax.ShapeDtypeStruct((8, 128), F32)),
        in_specs=[HBM_ONLY] * (2 * n),
        out_specs=(SEM_SPEC, SEM_SPEC, *[HBM_ONLY] * (2 * n), pl.BlockSpec(memory_space=pltpu.VMEM)),
        input_output_aliases={i: 2 + i for i in range(2 * n)},
        compiler_params=pltpu.CompilerParams(has_side_effects=DATAFLOW_EFFECT),
    )(*args)
    return outs[0], outs[1], outs[2:2 + n], outs[2 + n:2 + 2 * n], outs[-1]


def _exchange_wait_body(*refs, n, scatter):
    srcs, lands = refs[:n], refs[n:2 * n]
    send_sems, recv_sems = refs[2 * n], refs[2 * n + 1]
    for send, recv in _split_copies(srcs, lands, send_sems, recv_sems, n, scatter):
        send.wait_send()
        recv.wait_recv()


def exchange_wait(send_sems, recv_sems, srcs, lands, after, *, scatter, name):
    n = len(srcs)
    outs = pl.pallas_call(
        functools.partial(_exchange_wait_body, n=n, scatter=scatter), name=name,
        out_shape=tuple(pltpu.HBM(t.shape, t.dtype) for t in list(srcs) + list(lands)),
        in_specs=[HBM_ONLY] * (2 * n) + [SEM_SPEC, SEM_SPEC, HBM_SPEC],
        out_specs=tuple([HBM_ONLY] * (2 * n)),
        input_output_aliases={i: i for i in range(2 * n)},
        compiler_params=pltpu.CompilerParams(has_side_effects=DATAFLOW_EFFECT),
    )(*srcs, *lands, send_sems, recv_sems, after)
    return outs[n:]


def _own_slot_only(shape_dtype, own, slot):
    land = lax.empty(shape_dtype.shape, shape_dtype.dtype)
    return lax.dynamic_update_slice(land, own[None], (slot,) + (0,) * own.ndim)


def _pad_rows(a, rows=8):
    return jnp.pad(a, ((0, rows - a.shape[0]), (0, 0)))


def _gate_rows(a_log, dt_bias):
    z = jnp.zeros((8, 128), F32)
    return z.at[0, A_HEADS:2 * A_HEADS].set(a_log[0]).at[1, A_HEADS:2 * A_HEADS].set(dt_bias[0])


def _pack_small(norm_g, a_log, a_dt_bias, a_norm_g, b_q_norm_g, b_k_norm_g):
    return jnp.concatenate([
        norm_g[0].reshape(8, 128), norm_g[1].reshape(8, 128),
        _gate_rows(a_log, a_dt_bias),
        _pad_rows(a_norm_g[0].reshape(2, 128)),
        _pad_rows(jnp.concatenate([b_q_norm_g[0], b_k_norm_g[0]], axis=0)),
    ], axis=0)


def _unpack_small(p):
    return (p[0:16].reshape(2, D_MODEL), p[16:17, A_HEADS:2 * A_HEADS], p[17:18, A_HEADS:2 * A_HEADS],
            p[24:26].reshape(1, A_DV), p[32:35][None], p[35:38][None])


def kernel(x, positions, norm_g, a_w_in, a_conv_w, a_log, a_dt_bias, a_norm_g, a_w_out, b_w_in, b_q_norm_g, b_k_norm_g, b_w_out, loss_target, m_norm_g, m_a_w_in, m_a_conv_w, m_a_log, m_a_dt_bias, m_a_norm_g, m_a_w_out, m_b_w_in, m_b_q_norm_g, m_b_k_norm_g, m_b_w_out, v_norm_g, v_a_w_in, v_a_conv_w, v_a_log, v_a_dt_bias, v_a_norm_g, v_a_w_out, v_b_w_in, v_b_q_norm_g, v_b_k_norm_g, v_b_w_out):
    n_seq, s_len, d = x.shape
    t_rows = n_seq * s_len
    n_chunks = s_len // CHUNK
    x0 = x.reshape(t_rows, d)
    target = loss_target.reshape(t_rows, d)
    my_slot = _flat(*_my_place())

    g_a_in, g_conv = all_gather([a_w_in[0].astype(BF16), _pad_rows(a_conv_w[0])], name="gather_weights_first")
    later = [a_w_out[0].astype(BF16), b_w_in[0].astype(BF16), b_w_out[0].astype(BF16)]
    lands = [_own_slot_only(jax.ShapeDtypeStruct((N_DEV,) + t.shape, t.dtype), t, my_slot) for t in later]
    w_send, w_recv, later, lands, w_token = exchange_start(later, lands, scatter=False, name="gather_weights_start")
    w_a_in = jnp.pad(g_a_in.transpose(1, 0, 2).reshape(d, A_IN), ((0, 0), (0, A_IN_PAD - A_IN)))
    conv_w8 = g_conv.transpose(1, 0, 2).reshape(8, 2 * A_QK + A_VW)

    gains_model = _pad_rows(norm_g)
    gate_prm = _gate_rows(a_log, a_dt_bias)
    gain_a_out = _pad_rows(a_norm_g)
    gains_qk = _pad_rows(jnp.concatenate([b_q_norm_g[0], b_k_norm_g[0]], axis=0))
    cos_t, sin_t = rope_tables(positions.reshape(t_rows))

    h0, h0_t = rms_fwd(x0, gains_model, 0, after=w_token)
    proj_a = mm_nn(h0, w_a_in, tn=896, name="proj_a", a_resident=True)
    gates_col, gates_row = a_gates_fwd(proj_a, gate_prm, n_seq)
    gates_row = gates_row.reshape(n_seq, 2 * A_HEADS, s_len // SUPER, 1, SUPER)
    o_a, tinv, states, og_a, q_a, k_a, v_a = gdn_fwd(proj_a, conv_w8, gates_col, gates_row, gain_a_out, n_seq)
    g_a_out, g_b_in, g_b_out = exchange_wait(w_send, w_recv, later, lands, og_a, scatter=False,
                                             name="gather_weights_wait")
    w_a_out = g_a_out.reshape(A_VW, d)
    w_b_in = g_b_in.transpose(1, 0, 2).reshape(d, B_IN_COLS)
    w_b_out = g_b_out.reshape(B_W, d)
    x1, h1, h1_t = out_and_norm(og_a, w_a_out, x0, gains_model, 1)

    proj_b = mm_nn_pieces(h1, w_b_in, name="proj_b")
    og_b, o_b, lse = attn_fwd(proj_b, cos_t, sin_t, gains_qk, n_seq)
    dy, loss_blk = out_and_loss(og_b, w_b_out, x1, target)

    d_og_b = mm_nt(dy, w_b_out, tn=1024, name="d_og_b")
    dw_b_out = mm_tn(og_b, dy, tn=256, out_dtype=BF16, name="dw_b_out")
    dqkv_b, dz_b, d_gains_qk = attn_bwd(proj_b, cos_t, sin_t, gains_qk, d_og_b, o_b, lse, n_seq)
    dh1 = mm_nt_pieces(dqkv_b, dz_b, w_b_in, name="dh1")
    dw_b_in = mm_tn_b_in(h1_t, dqkv_b, dz_b, out_dtype=BF16, name="dw_b_in")
    dx1, d_gain1 = rms_bwd(x1, gains_model, 1, dh1, dy)

    dw_a_out = mm_tn(og_a, dx1, tn=256, out_dtype=BF16, name="dw_a_out")
    early = [dw_b_in, dw_b_out.reshape(N_DEV, B_W // N_DEV, d), dw_a_out.reshape(N_DEV, A_VW // N_DEV, d)]
    lands = [_own_slot_only(t, lax.dynamic_index_in_dim(t, my_slot, 0, keepdims=False), my_slot) for t in early]
    g_send, g_recv, early, lands, g_token = exchange_start(early, lands, scatter=True, name="scatter_grads_start")

    d_og_a = mm_nt(dx1, w_a_out, tn=1024, name="d_og_a", after=g_token)
    d_xq, d_xk, d_xv, dgates, dz_a, d_gain_a_out, d_cq, d_ck, d_cv = gdn_bwd(
        q_a, k_a, v_a, gates_col, gates_row, tinv, states, d_og_a, o_a, proj_a, conv_w8, gain_a_out, n_seq)
    d_conv = jnp.concatenate([d_cq.sum(axis=0), d_ck.sum(axis=0), d_cv.sum(axis=0)], axis=1)
    d_gate_logits, d_gate_prm = a_gates_bwd(proj_a, gate_prm, dgates, n_seq)
    dw_a_in = jnp.concatenate([
        mm_tn_multi(h0_t, [d_xq, d_xk, d_xv, dz_a], unit=256, out_dtype=BF16, name="dw_a_in_qkvz"),
        mm_tn(h0_t, d_gate_logits, tn=128, out_dtype=BF16, name="dw_a_in_gates", a_is_transposed=True),
    ], axis=1)[:, :A_IN]
    shard_a_in = A_IN // N_DEV
    last = [dw_a_in.reshape(d, N_DEV, shard_a_in).transpose(1, 0, 2)]
    last_lands = [_own_slot_only(t, lax.dynamic_index_in_dim(t, my_slot, 0, keepdims=False), my_slot) for t in last]
    l_send, l_recv, last, last_lands, l_token = exchange_start(last, last_lands, scatter=True,
                                                               name="scatter_last_start")
    dh0 = mm_nt_multi([d_xq, d_xk, d_xv, dz_a], w_a_in, tn=1024, name="dh0_qkvz", after=l_token)
    dh0 = mm_nt(d_gate_logits, w_a_in, dh0, tn=128, col_off=A_GATE_COL, name="dh0_gates")
    dx0, d_gain0 = rms_bwd(x0, gains_model, 0, dh0, dx1)

    small = jnp.concatenate([
        d_gain0[0].reshape(8, 128), d_gain1[0].reshape(8, 128), d_gate_prm,
        _pad_rows(d_gain_a_out[0].reshape(2, 128)), d_gains_qk, loss_blk], axis=0)
    n_small = small.shape[0] - loss_blk.shape[0]
    small_srcs = [small, d_conv]
    small_lands = [_own_slot_only(jax.ShapeDtypeStruct((N_DEV,) + t.shape, t.dtype), t, my_slot)
                   for t in small_srcs]
    s_send, s_recv, small_srcs, small_lands, s_token = exchange_start(small_srcs, small_lands, scatter=False,
                                                                      name="gather_small_start")

    r_b_in, r_b_out, r_a_out = exchange_wait(g_send, g_recv, early, lands, s_token, scatter=True,
                                             name="scatter_grads_wait")
    (r_a_in,) = exchange_wait(l_send, l_recv, last, last_lands, s_token, scatter=True, name="scatter_last_wait")

    upd = {}
    upd["a_w_in"] = adamw(r_a_in, a_w_in, m_a_w_in, v_a_w_in, name="adamw_a_w_in")
    upd["a_w_out"] = adamw(r_a_out, a_w_out, m_a_w_out, v_a_w_out, name="adamw_a_w_out")
    upd["b_w_in"] = adamw(r_b_in, b_w_in, m_b_w_in, v_b_w_in, name="adamw_b_w_in")
    upd["b_w_out"] = adamw(r_b_out, b_w_out, m_b_w_out, v_b_w_out, name="adamw_b_w_out")
    r_small, r_conv = exchange_wait(s_send, s_recv, small_srcs, small_lands, upd["b_w_in"][0], scatter=False,
                                    name="gather_small_wait")
    conv_cols = a_conv_w.shape[2]
    r_conv = lax.dynamic_slice(r_conv, (0, 0, my_slot * conv_cols), (N_DEV, 8, conv_cols))
    loss = jnp.sum(r_small[:, n_small, 0])
    r_small = r_small[:, :n_small]
    upd["a_conv_w"] = [t[:, :A_CONV] for t in adamw(
        r_conv, _pad_rows(a_conv_w[0])[None], _pad_rows(m_a_conv_w[0])[None], _pad_rows(v_a_conv_w[0])[None],
        name="adamw_a_conv_w")]
    small_upd = adamw(
        r_small,
        _pack_small(norm_g, a_log, a_dt_bias, a_norm_g, b_q_norm_g, b_k_norm_g)[None],
        _pack_small(m_norm_g, m_a_log, m_a_dt_bias, m_a_norm_g, m_b_q_norm_g, m_b_k_norm_g)[None],
        _pack_small(v_norm_g, v_a_log, v_a_dt_bias, v_a_norm_g, v_b_q_norm_g, v_b_k_norm_g)[None],
        name="adamw_small")
    small_names = ("norm_g", "a_log", "a_dt_bias", "a_norm_g", "b_q_norm_g", "b_k_norm_g")
    unpacked = [_unpack_small(t[0]) for t in small_upd]
    for i, nm in enumerate(small_names):
        upd[nm] = [u[i] for u in unpacked]

    order = ("norm_g", "a_w_in", "a_conv_w", "a_log", "a_dt_bias", "a_norm_g", "a_w_out",
             "b_w_in", "b_q_norm_g", "b_k_norm_g", "b_w_out")
    outs = [loss, dx0.reshape(n_seq, s_len, d)]
    for kind in range(4):
        for nm in order:
            outs.append(upd[nm][kind])
    return tuple(outs)
```

```python
import functools
import math

import jax
import jax.numpy as jnp
from jax import lax
from jax.experimental import pallas as pl
from jax.experimental.pallas import tpu as pltpu

F32 = jnp.float32
BF16 = jnp.bfloat16

D_MODEL = 1024
EPS = 1e-6
N_DEV = 8

A_HEADS = 8
A_DK = 128
A_DV = 256
A_QK = A_HEADS * A_DK
A_VW = A_HEADS * A_DV
A_CONV = 4
CHUNK = 64
A_IN = 2 * A_QK + 2 * A_VW + 2 * A_HEADS
A_IN_PAD = 2 * A_QK + 2 * A_VW + 128
A_GATE_COL = (2 * A_QK + 2 * A_VW) // 128

B_DILATIONS = (1, 4, 16)
B_GROUPS = 3
B_HEADS = 8
B_DH = 128
B_W = B_HEADS * B_DH
B_BLOCK = 128
B_PIECES = 3 * B_GROUPS + 1
ROPE_THETA = 500000.0
ROPE_DIMS = B_DH // 4
ROPE_HALF = ROPE_DIMS // 2

ADAM_LR = 0.001
ADAM_B1 = 0.9
ADAM_B2 = 0.999
ADAM_EPS = 1e-08
ADAM_WD = 0.01
ADAM_STEP = 10

VMEM_LIMIT = 60 * 1024 * 1024


def _params(sem):
    return pltpu.CompilerParams(dimension_semantics=sem, vmem_limit_bytes=VMEM_LIMIT)


def _bf(x):
    return x.astype(BF16)


def _mm(a, b):
    return jnp.dot(_bf(a), _bf(b), preferred_element_type=F32)


def _mm_nt(a, b):
    return lax.dot_general(_bf(a), _bf(b), (((1,), (1,)), ((), ())), preferred_element_type=F32)


def _mm_tn(a, b):
    return lax.dot_general(_bf(a), _bf(b), (((0,), (0,)), ((), ())), preferred_element_type=F32)


def _split(x):
    hi = _bf(x)
    return hi, _bf(x - hi.astype(F32))


def _mm3(a, b):
    ah, al = _split(a)
    bh, bl = _split(b)
    d = functools.partial(jnp.dot, preferred_element_type=F32)
    return d(ah, bh) + (d(ah, bl) + d(al, bh))


def _colsum_as_col(z):
    zh, zl = _split(z)
    ones = jnp.ones((z.shape[0], 128), BF16)
    tn = functools.partial(lax.dot_general, dimension_numbers=(((0,), (0,)), ((), ())),
                           preferred_element_type=F32)
    return (tn(zh, ones) + tn(zl, ones))[:, 0:1]


def _sigmoid(x):
    return 0.5 * jnp.tanh(0.5 * x) + 0.5


INV_BASE = 8
INV_NEWTON = 2
GDN_GROUP = 4
SUPER = GDN_GROUP * CHUNK
GDN_WIDTH = 2

A_K_COL = A_QK // A_DK
A_V_COL = 2 * A_QK // A_DV


def _inverse_steps(m, row, col):
    eye = (row == col).astype(F32)
    d = jnp.where(row // INV_BASE == col // INV_BASE, m, 0.0)
    x = eye - d
    p = _mm(d, d)
    yield
    steps = int(math.log2(INV_BASE)) - 1
    for i in range(steps):
        x = x + _mm(x, p)
        if i + 1 < steps:
            p = _mm(p, p)
        yield
    size = INV_BASE
    while size < CHUNK:
        c = jnp.where((row // (2 * size) == col // (2 * size)) & (row // size != col // size), m, 0.0)
        xc = _mm(x, c)
        yield
        x = x - _mm(xc, x)
        yield
        size *= 2
    for _ in range(INV_NEWTON):
        r = eye - x - _mm3(m, x)
        yield
        x = x + _mm(x, r)
        yield
    return x


def _drain(gen):
    while True:
        try:
            next(gen)
        except StopIteration as stop:
            return stop.value


def _interleave(*gens):
    live = list(gens)
    while live:
        for g in list(live):
            try:
                next(g)
            except StopIteration:
                live.remove(g)


def _diag_blocks_tall(x):
    return jnp.concatenate([x[i * CHUNK:(i + 1) * CHUNK, i * CHUNK:(i + 1) * CHUNK] for i in range(GDN_GROUP)], axis=0)


def _tall_to_block_diag(t, same):
    return jnp.where(same, jnp.concatenate([t] * GDN_GROUP, axis=1), 0.0)


def _block_sum(same, x):
    xh, xl = _split(jnp.broadcast_to(x, (SUPER, 128)))
    ones = same.astype(BF16)
    d = functools.partial(jnp.dot, preferred_element_type=F32)
    return (d(ones, xh) + d(ones, xl))[:, 0:1]


def _aligned_rows(index, size):
    start = index * size
    return pl.ds(start if isinstance(start, int) else pl.multiple_of(start, size), size)


def _super_rows(i):
    return _aligned_rows(i, SUPER)


def _chunk_rows(n):
    return _aligned_rows(n, CHUNK)


def _gdn_super_steps(q, k, v, gcb, gr, head, tinv_tall=None, need_m=True):
    lane = lax.broadcasted_iota(jnp.int32, (SUPER, 128), 1)
    row = lax.broadcasted_iota(jnp.int32, (SUPER, SUPER), 0)
    col = lax.broadcasted_iota(jnp.int32, (SUPER, SUPER), 1)
    same = row // CHUNK == col // CHUNK
    beta = jnp.sum(jnp.where(lane == head, gcb, 0.0), axis=1, keepdims=True)
    gc = jnp.sum(jnp.where(lane == A_HEADS + head, gcb, 0.0), axis=1, keepdims=True)
    g_last = jnp.sum(jnp.where(col == (row // CHUNK) * CHUNK + (CHUNK - 1), gr, 0.0), axis=1, keepdims=True)
    gamma = jnp.exp(gc)
    decay = jnp.where(same & (row >= col), jnp.exp(jnp.minimum(gc - gr, 0.0)), 0.0)
    kb = k * beta
    m = jnp.where(same & (row > col), _mm_nt(kb, k) * decay, 0.0) if need_m else None
    p = jnp.where(same & (row >= col), _mm_nt(q, k) * decay, 0.0)
    yield
    if tinv_tall is None:
        tinv = yield from _inverse_steps(m, row, col)
    else:
        tinv = _tall_to_block_diag(tinv_tall, same)
    u = _mm(tinv, v * beta)
    w = _mm(tinv, kb * gamma)
    yield
    e_tail = jnp.exp(g_last - gc)
    return dict(beta=beta, gc=gc, g_last=g_last, gamma=gamma, decay=decay, kb=kb, m=m,
                tinv=tinv, u=u, w=w, p=p, e_tail=e_tail, row=row, col=col, lane=lane, same=same)


def _store_scan_operands(rows, q, k, t, u_scr, w_scr, p_scr, qg_scr, ke_scr, gl_scr):
    u_scr[rows, :] = t["u"]
    w_scr[rows, :] = _bf(t["w"])
    p_scr[rows, :] = _bf(_diag_blocks_tall(t["p"]))
    qg_scr[rows, :] = _bf(q * t["gamma"])
    ke_scr[rows, :] = _bf(k * t["e_tail"])
    gl_scr[rows, :] = jnp.broadcast_to(jnp.exp(t["g_last"]), (SUPER, 128))


def _gdn_fwd_body(q_ref, k_ref, v_ref, gc_ref, gr_ref, wq_ref, wk_ref, wv_ref, z_ref, gn_ref,
                  o_ref, tinv_ref, st_ref, og_ref, qo_ref, ko_ref, vo_ref, s_scr, *sets):
    head = pl.program_id(1)
    n_super = q_ref.shape[0] // SUPER
    all_sets = [sets[6 * i:6 * i + 6] for i in range(2 * GDN_WIDTH)]
    whole = pl.ds(0, SUPER)

    def conv_silu(x_ref, w_ref, i):
        rows = _super_rows(i)
        x, w = x_ref[rows, :], w_ref[...]
        halo = jnp.zeros((8, x.shape[1]), F32) if i == 0 else x_ref[pl.ds(i * SUPER - 8, 8), :]
        ext = jnp.concatenate([halo, x], axis=0)
        c = x * w[A_CONV - 1:A_CONV, :]
        for j in range(1, A_CONV):
            c = c + pltpu.roll(ext, j, 0)[8:, :] * w[A_CONV - 1 - j:A_CONV - j, :]
        return c * _sigmoid(c)

    def unit(a):
        return a * lax.rsqrt(jnp.sum(a * a, axis=1, keepdims=True) + EPS)

    def prepare_steps(i, dst):
        rows = _super_rows(i)
        q = unit(conv_silu(q_ref, wq_ref, i)) * A_DK ** -0.5
        k = unit(conv_silu(k_ref, wk_ref, i))
        v = conv_silu(v_ref, wv_ref, i)
        qo_ref[rows, :], ko_ref[rows, :], vo_ref[rows, :] = q, k, v
        t = yield from _gdn_super_steps(q, k, v, gc_ref[rows, :], gr_ref[i], head)
        tinv_ref[rows, :] = _diag_blocks_tall(t["tinv"])
        _store_scan_operands(whole, q, k, t, *dst)

    def scan_steps(i, src):
        u_scr, w_scr, p_scr, qg_scr, ke_scr, gl_scr = src
        for j in range(GDN_GROUP):
            n = i * GDN_GROUP + j
            local = pl.ds(j * CHUNK, CHUNK)
            s = s_scr[...]
            sb = _bf(s)
            st_ref[n] = sb
            ws = jnp.dot(w_scr[local, :], sb, preferred_element_type=F32)
            yield
            vb = _bf(u_scr[local, :] - ws)
            o = (jnp.dot(qg_scr[local, :], sb, preferred_element_type=F32)
                 + jnp.dot(p_scr[local, :], vb, preferred_element_type=F32))
            s_new = s * gl_scr[local, :][0:1, 0:1] + lax.dot_general(
                ke_scr[local, :], vb, (((0,), (0,)), ((), ())), preferred_element_type=F32)
            yield
            rows = _chunk_rows(n)
            o_ref[rows, :] = o
            s_scr[...] = s_new
            silu, _ = _silu_parts(z_ref[rows, :])
            r = lax.rsqrt(jnp.mean(o * o, axis=1, keepdims=True) + EPS)
            og_ref[rows, :] = ((o * r * gn_ref[0:1, :]) * silu).astype(og_ref.dtype)

    def scan_many(first, srcs):
        for j, src in enumerate(srcs):
            yield from scan_steps(first + j, src)

    groups = [all_sets[:GDN_WIDTH], all_sets[GDN_WIDTH:]]
    _interleave(*[prepare_steps(j, groups[0][j]) for j in range(GDN_WIDTH)])
    s_scr[...] = jnp.zeros_like(s_scr)
    for g in range(n_super // GDN_WIDTH):
        cur, nxt = groups[g % 2], groups[(g + 1) % 2]
        first = g * GDN_WIDTH
        following = [prepare_steps(first + GDN_WIDTH + j, nxt[j]) for j in range(GDN_WIDTH)
                     if first + GDN_WIDTH + j < n_super]
        _interleave(scan_many(first, cur), *following)


def _gdn_in_specs(s_len, n_super, from_proj):
    k_col, v_col = (A_K_COL, A_V_COL) if from_proj else (0, 0)
    return [
        pl.BlockSpec((s_len, A_DK), lambda b, h: (b, h)),
        pl.BlockSpec((s_len, A_DK), lambda b, h: (b, k_col + h)),
        pl.BlockSpec((s_len, A_DV), lambda b, h: (b, v_col + h)),
        pl.BlockSpec((s_len, 128), lambda b, h: (b, 0)),
        pl.BlockSpec((None, None, n_super, 1, SUPER), lambda b, h: (b, A_HEADS + h, 0, 0, 0)),
    ]


def _gdn_scan_scratch(s_len):
    return [pltpu.VMEM((A_DK, A_DV), F32), pltpu.VMEM((s_len, A_DV), F32),
            pltpu.VMEM((s_len, A_DK), BF16), pltpu.VMEM((s_len, CHUNK), BF16),
            pltpu.VMEM((s_len, A_DK), BF16), pltpu.VMEM((s_len, A_DK), BF16),
            pltpu.VMEM((s_len, 128), F32)]


def gdn_fwd(proj_a, conv_w8, gates_col, gates_row, norm_g8, n_seq):
    t_rows = proj_a.shape[0]
    s_len = t_rows // n_seq
    n_chunks = s_len // CHUNK
    qk_spec = pl.BlockSpec((s_len, A_DK), lambda b, h: (b, h))
    v_spec = pl.BlockSpec((s_len, A_DV), lambda b, h: (b, h))
    return pl.pallas_call(
        _gdn_fwd_body, name="gdn_fwd", grid=(n_seq, A_HEADS),
        in_specs=_gdn_in_specs(s_len, s_len // SUPER, True) + [
            pl.BlockSpec((8, A_DK), lambda b, h: (0, h)),
            pl.BlockSpec((8, A_DK), lambda b, h: (0, A_K_COL + h)),
            pl.BlockSpec((8, A_DV), lambda b, h: (0, A_V_COL + h)),
            pl.BlockSpec((s_len, A_DV), lambda b, h: (b, A_Z_COL + h)),
            pl.BlockSpec((8, A_DV), lambda b, h: (0, 0)),
        ],
        out_specs=[
            v_spec,
            pl.BlockSpec((s_len, CHUNK), lambda b, h: (b * A_HEADS + h, 0)),
            pl.BlockSpec((None, n_chunks, A_DK, A_DV), lambda b, h: (b * A_HEADS + h, 0, 0, 0)),
            v_spec, qk_spec, qk_spec, v_spec,
        ],
        out_shape=[
            jax.ShapeDtypeStruct((t_rows, A_VW), F32),
            jax.ShapeDtypeStruct((n_seq * A_HEADS * s_len, CHUNK), F32),
            jax.ShapeDtypeStruct((n_seq * A_HEADS, n_chunks, A_DK, A_DV), BF16),
            jax.ShapeDtypeStruct((t_rows, A_VW), BF16),
            jax.ShapeDtypeStruct((t_rows, A_QK), F32),
            jax.ShapeDtypeStruct((t_rows, A_QK), F32),
            jax.ShapeDtypeStruct((t_rows, A_VW), F32),
        ],
        scratch_shapes=_gdn_scan_scratch(SUPER) + (2 * GDN_WIDTH - 1) * _gdn_scan_scratch(SUPER)[1:],
        compiler_params=_params(("arbitrary", "arbitrary")),
    )(proj_a, proj_a, proj_a, gates_col, gates_row, conv_w8, conv_w8, conv_w8, proj_a, norm_g8)


def _gdn_bwd_body(q_ref, k_ref, v_ref, gc_ref, gr_ref, tinv_ref, st_ref, dog_ref, oa_ref, z_ref, gn_ref,
                  xq_ref, xk_ref, xv_ref, wq_ref, wk_ref, wv_ref,
                  dxq_ref, dxk_ref, dxv_ref, dgc_ref, dz_ref, dgn_ref, dwq_ref, dwk_ref, dwv_ref,
                  ds_scr, cq_scr, ck_scr, cv_scr, *sets):
    head = pl.program_id(1)
    n_super = q_ref.shape[0] // SUPER
    ops = (sets[0:7], sets[7:14])
    res = (sets[14:21], sets[21:28])
    whole = pl.ds(0, SUPER)
    tn = functools.partial(lax.dot_general, dimension_numbers=(((0,), (0,)), ((), ())), preferred_element_type=F32)
    nt = functools.partial(lax.dot_general, dimension_numbers=(((1,), (1,)), ((), ())), preferred_element_type=F32)

    @pl.when(head == 0)
    def _():
        dgc_ref[...] = jnp.zeros_like(dgc_ref)

    @pl.when((head == 0) & (pl.program_id(0) == 0))
    def _():
        dgn_ref[...] = jnp.zeros_like(dgn_ref)

    carry = (cq_scr, ck_scr, cv_scr)
    for ref in carry + (dwq_ref, dwk_ref, dwv_ref):
        ref[...] = jnp.zeros_like(ref)

    def common_steps(i, need_m=True):
        rows = _super_rows(i)
        q, k, v = q_ref[rows, :], k_ref[rows, :], v_ref[rows, :]
        t = yield from _gdn_super_steps(q, k, v, gc_ref[rows, :], gr_ref[i], head, tinv_tall=tinv_ref[rows, :],
                                        need_m=need_m)
        return rows, q, k, v, t

    def stage_p(i, parity):
        rows, q, k, _, t = yield from common_steps(i, need_m=False)
        _store_scan_operands(whole, q, k, t, *ops[parity][:6])
        o, d_og, gain = oa_ref[rows, :], dog_ref[rows, :], gn_ref[0:1, :]
        r = lax.rsqrt(jnp.mean(o * o, axis=1, keepdims=True) + EPS)
        silu, dsilu = _silu_parts(z_ref[rows, :])
        xr = o * r
        d_on = d_og * silu
        dz_ref[rows, :] = (d_og * (xr * gain) * dsilu).astype(dz_ref.dtype)
        u = d_on * gain
        ops[parity][6][...] = r * u - xr * (r * r) * jnp.mean(u * o, axis=1, keepdims=True)
        dgn_ref[0:1, :] += jnp.sum(d_on * xr, axis=0, keepdims=True)

    def stage_s(i, parity):
        u_scr, w_scr, p_scr, qg_scr, ke_scr, gl_scr, do_scr = ops[parity]
        vn_scr, dvn_scr, dqg_scr, dw_scr, dkt_scr, sds_scr, dof_scr = res[parity]
        for j in reversed(range(GDN_GROUP)):
            n = i * GDN_GROUP + j
            local = pl.ds(j * CHUNK, CHUNK)
            ds_next = ds_scr[...]
            dsb = _bf(ds_next)
            sb = st_ref[n]
            s = sb.astype(F32)
            d_o = do_scr[local, :]
            dof_scr[local, :] = d_o
            d_ob = _bf(d_o)
            w_s = jnp.dot(w_scr[local, :], sb, preferred_element_type=F32)
            d_vn = tn(p_scr[local, :], d_ob) + jnp.dot(ke_scr[local, :], dsb, preferred_element_type=F32)
            d_qg = nt(d_ob, sb)
            qg_do = tn(qg_scr[local, :], d_ob)
            yield
            v_new = u_scr[local, :] - w_s
            d_vnb = _bf(d_vn)
            d_w = -nt(d_vnb, sb)
            d_kt = nt(_bf(v_new), dsb)
            w_dvn = tn(w_scr[local, :], d_vnb)
            yield
            vn_scr[local, :] = v_new
            dvn_scr[local, :] = d_vn
            dqg_scr[local, :] = d_qg
            dw_scr[local, :] = d_w
            dkt_scr[local, :] = d_kt
            sds = jnp.sum(jnp.sum(s * ds_next, axis=1, keepdims=True), axis=0, keepdims=True)
            sds_scr[local, :] = jnp.broadcast_to(sds, (CHUNK, 128))
            ds_scr[...] = qg_do + gl_scr[local, :][0:1, 0:1] * ds_next - w_dvn

    def conv_bwd(i, rows, x_ref, w_ref, dy, norm_scale, dx_ref, dw_ref, dc_above):
        x, w = x_ref[rows, :], w_ref[...]
        above = x_ref[pl.ds(pl.multiple_of(jnp.maximum(i * SUPER - 8, 0), 8), 8), :]
        ext = jnp.concatenate([jnp.where(i > 0, above, 0.0), x], axis=0)
        c = x * w[A_CONV - 1:A_CONV, :]
        for j in range(1, A_CONV):
            c = c + pltpu.roll(ext, j, 0)[8:, :] * w[A_CONV - 1 - j:A_CONV - j, :]
        sig = _sigmoid(c)
        a = c * sig
        if norm_scale is None:
            da = dy
        else:
            rn = lax.rsqrt(jnp.sum(a * a, axis=1, keepdims=True) + EPS)
            da = norm_scale * (rn * dy - a * (rn * rn * rn) * jnp.sum(dy * a, axis=1, keepdims=True))
        dc = da * (sig * (1.0 + c * (1.0 - sig)))
        ext_dc = jnp.concatenate([dc, dc_above[...]], axis=0)
        dc_above[...] = dc[0:8, :]
        dx = dc * w[A_CONV - 1:A_CONV, :]
        dw_ref[A_CONV - 1:A_CONV, :] += jnp.sum(dc * x, axis=0, keepdims=True)
        for j in range(1, A_CONV):
            dcs = pltpu.roll(ext_dc, SUPER + 8 - j, 0)[:SUPER, :]
            dx = dx + dcs * w[A_CONV - 1 - j:A_CONV - j, :]
            dw_ref[A_CONV - 1 - j:A_CONV - j, :] += jnp.sum(dcs * x, axis=0, keepdims=True)
        dx_ref[rows, :] = dx.astype(dx_ref.dtype)

    def stage_f(i, parity):
        vn_scr, dvn_scr, dqg_scr, dw_scr, dkt_scr, sds_scr, dof_scr = res[parity]
        rows, q, k, v, t = yield from common_steps(i)
        beta, gamma, decay, kb, e_tail = t["beta"], t["gamma"], t["decay"], t["kb"], t["e_tail"]
        row, col, lane, same = t["row"], t["col"], t["lane"], t["same"]
        d_o = dof_scr[...]
        v_new, d_vn = vn_scr[...], dvn_scr[...]
        d_qg, d_w, d_kt = dqg_scr[...], dw_scr[...], dkt_scr[...]
        gamma_last = jnp.exp(t["g_last"])

        d_p = jnp.where(same & (row >= col), _mm_nt(d_o, v_new), 0.0)
        d_ru = _mm_tn(t["tinv"], d_vn)
        d_rw = _mm_tn(t["tinv"], d_w)
        yield
        d_m = jnp.where(same & (row > col), -(_mm_nt(d_ru, t["u"]) + _mm_nt(d_rw, t["w"])), 0.0)
        yield

        x_p = d_p * decay
        y_m = d_m * decay
        d_kb = _mm(y_m, k) + d_rw * gamma
        d_q = _mm(x_p, k) + d_qg * gamma
        d_k = _mm_tn(x_p, q) + _mm_tn(y_m, kb) + d_kb * beta + d_kt * e_tail
        d_v = d_ru * beta
        conv_bwd(i, rows, xq_ref, wq_ref, d_q, A_DK ** -0.5, dxq_ref, dwq_ref, carry[0])
        conv_bwd(i, rows, xk_ref, wk_ref, d_k, 1.0, dxk_ref, dwk_ref, carry[1])
        conv_bwd(i, rows, xv_ref, wv_ref, d_v, None, dxv_ref, dwv_ref, carry[2])

        d_beta = (jnp.sum(d_ru * v, axis=1, keepdims=True)
                  + jnp.sum(d_kb * k, axis=1, keepdims=True))
        z = d_p * t["p"] + d_m * t["m"]
        eps_tail = jnp.sum(d_kt * k, axis=1, keepdims=True) * e_tail
        d_gc = (jnp.sum(z, axis=1, keepdims=True) - _colsum_as_col(z)
                + jnp.sum(d_qg * q, axis=1, keepdims=True) * gamma
                + jnp.sum(d_rw * kb, axis=1, keepdims=True) * gamma
                - eps_tail)
        d_glast = _block_sum(same, eps_tail) + gamma_last * sds_scr[...][:, 0:1]
        yield
        rcol = lax.broadcasted_iota(jnp.int32, (SUPER, 1), 0)
        d_gc = d_gc + jnp.where(rcol % CHUNK == CHUNK - 1, d_glast, 0.0)
        dgc_ref[rows, :] += (jnp.where(lane == head, d_beta, 0.0)
                             + jnp.where(lane == A_HEADS + head, d_gc, 0.0))

    last = n_super - 1
    _drain(stage_p(last, 1))
    ds_scr[...] = jnp.zeros_like(ds_scr)
    _interleave(stage_s(last, 1), stage_p(last - 1, 0))

    def pair(k, carry):
        i = last - 1 - 2 * k
        _interleave(stage_s(i, 0), stage_f(i + 1, 1), stage_p(i - 1, 1))
        _interleave(stage_s(i - 1, 1), stage_f(i, 0), stage_p(i - 2, 0))
        return carry

    lax.fori_loop(0, n_super // 2 - 1, pair, 0)
    _interleave(stage_s(0, 0), stage_f(1, 1))
    _drain(stage_f(0, 0))


def gdn_bwd(q, k, v, gates_col, gates_row, tinv, states, d_og, o, proj_a, conv_w8, norm_g8, n_seq):
    t_rows = q.shape[0]
    s_len = t_rows // n_seq
    n_chunks = s_len // CHUNK
    qk_spec = pl.BlockSpec((s_len, A_DK), lambda b, h: (b, h))
    v_spec = pl.BlockSpec((s_len, A_DV), lambda b, h: (b, h))
    gate_spec = pl.BlockSpec((s_len, 128), lambda b, h: (b, 0))
    gain_spec = pl.BlockSpec((8, A_DV), lambda b, h: (0, 0))
    dw_qk_spec = pl.BlockSpec((None, 8, A_DK), lambda b, h: (b, 0, h))
    dw_v_spec = pl.BlockSpec((None, 8, A_DV), lambda b, h: (b, 0, h))
    ops_set = _gdn_scan_scratch(SUPER)[1:] + [pltpu.VMEM((SUPER, A_DV), F32)]
    res_set = [pltpu.VMEM((SUPER, A_DV), F32), pltpu.VMEM((SUPER, A_DV), F32),
               pltpu.VMEM((SUPER, A_DK), F32), pltpu.VMEM((SUPER, A_DK), F32),
               pltpu.VMEM((SUPER, A_DK), F32), pltpu.VMEM((SUPER, 128), F32), pltpu.VMEM((SUPER, A_DV), F32)]
    return pl.pallas_call(
        _gdn_bwd_body, name="gdn_bwd", grid=(n_seq, A_HEADS),
        in_specs=_gdn_in_specs(s_len, s_len // SUPER, False) + [
            pl.BlockSpec((s_len, CHUNK), lambda b, h: (b * A_HEADS + h, 0)),
            pl.BlockSpec((None, n_chunks, A_DK, A_DV), lambda b, h: (b * A_HEADS + h, 0, 0, 0)),
            v_spec, v_spec,
            pl.BlockSpec((s_len, A_DV), lambda b, h: (b, A_Z_COL + h)),
            gain_spec,
            pl.BlockSpec((s_len, A_DK), lambda b, h: (b, h)),
            pl.BlockSpec((s_len, A_DK), lambda b, h: (b, A_K_COL + h)),
            pl.BlockSpec((s_len, A_DV), lambda b, h: (b, A_V_COL + h)),
            pl.BlockSpec((8, A_DK), lambda b, h: (0, h)),
            pl.BlockSpec((8, A_DK), lambda b, h: (0, A_K_COL + h)),
            pl.BlockSpec((8, A_DV), lambda b, h: (0, A_V_COL + h)),
        ],
        out_specs=[qk_spec, qk_spec, v_spec, gate_spec, v_spec, gain_spec, dw_qk_spec, dw_qk_spec, dw_v_spec],
        out_shape=[
            jax.ShapeDtypeStruct((t_rows, A_QK), BF16),
            jax.ShapeDtypeStruct((t_rows, A_QK), BF16),
            jax.ShapeDtypeStruct((t_rows, A_VW), BF16),
            jax.ShapeDtypeStruct((t_rows, 128), F32),
            jax.ShapeDtypeStruct((t_rows, A_VW), BF16),
            jax.ShapeDtypeStruct((8, A_DV), F32),
            jax.ShapeDtypeStruct((n_seq, 8, A_QK), F32),
            jax.ShapeDtypeStruct((n_seq, 8, A_QK), F32),
            jax.ShapeDtypeStruct((n_seq, 8, A_VW), F32),
        ],
        scratch_shapes=(_gdn_scan_scratch(SUPER)[:1]
                        + [pltpu.VMEM((8, A_DK), F32), pltpu.VMEM((8, A_DK), F32), pltpu.VMEM((8, A_DV), F32)]
                        + 2 * ops_set + 2 * res_set),
        compiler_params=_params(("arbitrary", "arbitrary")),
    )(q, k, v, gates_col, gates_row, tinv, states, d_og, o, proj_a, norm_g8,
      proj_a, proj_a, proj_a, conv_w8, conv_w8, conv_w8)


GATE_TILE = 512


def _softplus(y):
    return jnp.maximum(y, 0.0) + jnp.log1p(jnp.exp(-jnp.abs(y)))


def _gate_values(x, prm):
    beta = _sigmoid(x)
    y = x + prm[1:2, :]
    neg_a = -jnp.exp(prm[0:1, :])
    g = neg_a * _softplus(y)
    return beta, y, neg_a, g


def _a_gates_fwd_body(x_ref, prm_ref, gc_ref, gr_ref):
    x = x_ref[...]
    tm = x.shape[0]
    beta, _, _, g = _gate_values(x, prm_ref[...])
    in_chunk = lax.broadcasted_iota(jnp.int32, (tm, 1), 0) % CHUNK
    s = 1
    while s < CHUNK:
        g = g + jnp.where(in_chunk >= s, pltpu.roll(g, s, 0), 0.0)
        s *= 2
    lane = lax.broadcasted_iota(jnp.int32, x.shape, 1)
    out = jnp.where(lane < A_HEADS, beta, jnp.where(lane < 2 * A_HEADS, g, 0.0))
    gc_ref[...] = out
    gr_ref[...] = out.T[0:2 * A_HEADS, :]


def a_gates_fwd(proj_a, prm, n_seq):
    t_rows = proj_a.shape[0]
    s_len = t_rows // n_seq
    tm = min(GATE_TILE, s_len)
    n_t = s_len // tm
    return pl.pallas_call(
        _a_gates_fwd_body, name="a_gates_fwd", grid=(n_seq, n_t),
        in_specs=[pl.BlockSpec((tm, 128), lambda b, i: (b * n_t + i, A_GATE_COL)),
                  pl.BlockSpec((8, 128), lambda b, i: (0, 0))],
        out_specs=[pl.BlockSpec((tm, 128), lambda b, i: (b * n_t + i, 0)),
                   pl.BlockSpec((None, 2 * A_HEADS, tm), lambda b, i: (b, 0, i))],
        out_shape=[jax.ShapeDtypeStruct((t_rows, 128), F32),
                   jax.ShapeDtypeStruct((n_seq, 2 * A_HEADS, s_len), F32)],
        compiler_params=_params(("arbitrary", "arbitrary")),
    )(proj_a, prm)


def _a_gates_bwd_body(x_ref, prm_ref, dgc_ref, dx_ref, dprm_ref):
    first = (pl.program_id(0) == 0) & (pl.program_id(1) == 0)
    x = x_ref[...]
    tm = x.shape[0]
    beta, y, neg_a, g = _gate_values(x, prm_ref[...])
    d = dgc_ref[...]
    in_chunk = lax.broadcasted_iota(jnp.int32, (tm, 1), 0) % CHUNK
    dg = d
    s = 1
    while s < CHUNK:
        dg = dg + jnp.where(in_chunk < CHUNK - s, pltpu.roll(dg, tm - s, 0), 0.0)
        s *= 2
    lane = lax.broadcasted_iota(jnp.int32, x.shape, 1)
    is_decay = (lane >= A_HEADS) & (lane < 2 * A_HEADS)
    d_alogit = jnp.where(is_decay, dg * neg_a * _sigmoid(y), 0.0)
    dx_ref[...] = jnp.where(lane < A_HEADS, d * beta * (1.0 - beta), d_alogit).astype(dx_ref.dtype)

    @pl.when(first)
    def _():
        dprm_ref[...] = jnp.zeros_like(dprm_ref)

    dprm_ref[0:1, :] += jnp.sum(jnp.where(is_decay, dg * g, 0.0), axis=0, keepdims=True)
    dprm_ref[1:2, :] += jnp.sum(d_alogit, axis=0, keepdims=True)


def a_gates_bwd(proj_a, prm, dgates_col, n_seq):
    t_rows = proj_a.shape[0]
    s_len = t_rows // n_seq
    tm = min(GATE_TILE, s_len)
    n_t = s_len // tm
    return pl.pallas_call(
        _a_gates_bwd_body, name="a_gates_bwd", grid=(n_seq, n_t),
        in_specs=[pl.BlockSpec((tm, 128), lambda b, i: (b * n_t + i, A_GATE_COL)),
                  pl.BlockSpec((8, 128), lambda b, i: (0, 0)),
                  pl.BlockSpec((tm, 128), lambda b, i: (b * n_t + i, 0))],
        out_specs=[pl.BlockSpec((tm, 128), lambda b, i: (b * n_t + i, 0)),
                   pl.BlockSpec((8, 128), lambda b, i: (0, 0))],
        out_shape=[jax.ShapeDtypeStruct((t_rows, 128), BF16),
                   jax.ShapeDtypeStruct((8, 128), F32)],
        compiler_params=_params(("arbitrary", "arbitrary")),
    )(proj_a, prm, dgates_col)


ROW_TILE = 1024
A_Z_COL = (2 * A_QK + A_VW) // A_DV


def _silu_parts(z):
    sig = _sigmoid(z)
    return z * sig, sig * (1.0 + z * (1.0 - sig))


NEG_BIG = -1e30
ATT_SCALE = B_DH ** -0.5


def _swap_rope_halves(x):
    src = lax.broadcasted_iota(jnp.int32, (B_DH, B_DH), 0)
    dst = lax.broadcasted_iota(jnp.int32, (B_DH, B_DH), 1)
    pick = ((dst < ROPE_HALF) & (src == dst + ROPE_HALF)) | (
        (dst >= ROPE_HALF) & (dst < ROPE_DIMS) & (src == dst - ROPE_HALF))
    return jnp.dot(_bf(x), pick.astype(BF16), preferred_element_type=F32)


def _norm_rope(x, gain, cos_t, sin_t):
    r = lax.rsqrt(jnp.mean(x * x, axis=1, keepdims=True) + EPS)
    xn = x * r * gain
    return xn * cos_t + _swap_rope_halves(xn) * sin_t, r


def _norm_rope_bwd(x, r, gain, cos_t, sin_t, dy):
    d_xn = dy * cos_t + _swap_rope_halves(dy * sin_t)
    xr = x * r
    u = d_xn * gain
    dx = r * u - xr * (r * r) * jnp.mean(u * x, axis=1, keepdims=True)
    return dx, jnp.sum(d_xn * xr, axis=0, keepdims=True)


def _stream_rows(idx, dilation, s_len):
    nb = s_len // dilation // B_BLOCK
    r = idx // nb
    m = idx % nb
    cur = r + m * (B_BLOCK * dilation)
    prev = r + jnp.maximum(m - 1, 0) * (B_BLOCK * dilation)
    return cur, prev, m > 0


def _rows(start, dilation):
    if dilation == 1:
        return pl.ds(start, B_BLOCK)
    return pl.ds(start, B_BLOCK, stride=dilation)


ATT_UNROLL = 16


def _band_mask(has_prev):
    qi = lax.broadcasted_iota(jnp.int32, (B_BLOCK, 2 * B_BLOCK), 0)
    kj = lax.broadcasted_iota(jnp.int32, (B_BLOCK, 2 * B_BLOCK), 1)
    return ((kj < B_BLOCK) & (kj >= qi) & has_prev) | ((kj >= B_BLOCK) & (kj - B_BLOCK <= qi))


def _attn_fwd_body(qkv_ref, z_ref, cos_ref, sin_ref, gain_ref, og_ref, o_ref, lse_ref,
                   qn_scr, kn_scr, og_scr, lg_scr):
    head, grp = pl.program_id(1), pl.program_id(2)
    s_len = z_ref.shape[0]
    n_blocks = s_len // B_BLOCK
    cos_t, sin_t = cos_ref[...], sin_ref[...]

    for gi, dil in enumerate(B_DILATIONS):
        @pl.when(grp == gi)
        def _(gi=gi, dil=dil):
            qn_scr[...], _ = _norm_rope(qkv_ref[0], gain_ref[gi:gi + 1, :], cos_t, sin_t)
            kn_scr[...], _ = _norm_rope(qkv_ref[1], gain_ref[B_GROUPS + gi:B_GROUPS + gi + 1, :], cos_t, sin_t)

            ones = jnp.ones((2 * B_BLOCK, B_DH), BF16)

            def blocks(it, carry):
                scored = []
                for j in range(ATT_UNROLL):
                    cur, prev, has_prev = _stream_rows(it * ATT_UNROLL + j, dil, s_len)
                    rc, rp = _rows(cur, dil), _rows(prev, dil)
                    k2 = jnp.concatenate([kn_scr[rp, :], kn_scr[rc, :]], axis=0)
                    scored.append((rc, rp, has_prev, _mm_nt(qn_scr[rc, :], k2) * ATT_SCALE))
                summed = []
                for rc, rp, has_prev, s in scored:
                    s = jnp.where(_band_mask(has_prev), s, NEG_BIG)
                    mx = jnp.max(s, axis=1, keepdims=True)
                    v2 = jnp.concatenate([qkv_ref.at[2][rp, :], qkv_ref.at[2][rc, :]], axis=0)
                    acc = jnp.dot(_bf(jnp.exp(s - mx)), jnp.concatenate([_bf(v2), ones], axis=1),
                                  preferred_element_type=F32)
                    summed.append((rc, mx, acc))
                for rc, mx, acc in summed:
                    den = acc[:, B_DH:B_DH + 1]
                    og_scr.at[gi][rc, :] = acc[:, :B_DH] / den
                    lg_scr.at[gi][rc, :] = jnp.broadcast_to(mx + jnp.log(den), (B_BLOCK, B_DH))
                return carry

            lax.fori_loop(0, n_blocks // ATT_UNROLL, blocks, 0)

    @pl.when(grp == B_GROUPS - 1)
    def _():
        l0, l1, l2 = lg_scr[0], lg_scr[1], lg_scr[2]
        mx = jnp.maximum(jnp.maximum(l0, l1), l2)
        w0, w1, w2 = jnp.exp(l0 - mx), jnp.exp(l1 - mx), jnp.exp(l2 - mx)
        den = w0 + w1 + w2
        o = (w0 * og_scr[0] + w1 * og_scr[1] + w2 * og_scr[2]) / den
        silu, _ = _silu_parts(z_ref[...])
        o_ref[...] = o
        og_ref[...] = (o * silu).astype(og_ref.dtype)
        @pl.when(head == 0)
        def _():
            lse_ref[...] = jnp.zeros_like(lse_ref)

        lane = lax.broadcasted_iota(jnp.int32, o.shape, 1)
        lse_ref[...] = jnp.where(lane == head, mx + jnp.log(den), lse_ref[...])


def attn_fwd(proj_b, cos_t, sin_t, gains8, n_seq):
    t_rows = proj_b.shape[1]
    s_len = t_rows // n_seq
    head_blk = pl.BlockSpec((s_len, B_DH), lambda b, h, g: (b, h))
    seq_blk = pl.BlockSpec((s_len, 128), lambda b, h, g: (b, 0))
    return pl.pallas_call(
        _attn_fwd_body, name="attn_fwd", grid=(n_seq, B_HEADS, B_GROUPS),
        in_specs=[
            pl.BlockSpec((3, s_len, B_DH), lambda b, h, g: (g, b, h)),
            pl.BlockSpec((None, s_len, B_DH), lambda b, h, g: (B_PIECES - 1, b, h)),
            seq_blk, seq_blk,
            pl.BlockSpec((8, 128), lambda b, h, g: (0, 0)),
        ],
        out_specs=[head_blk, head_blk, seq_blk],
        out_shape=[jax.ShapeDtypeStruct((t_rows, B_W), BF16),
                   jax.ShapeDtypeStruct((t_rows, B_W), F32),
                   jax.ShapeDtypeStruct((t_rows, 128), F32)],
        scratch_shapes=[pltpu.VMEM((s_len, B_DH), F32), pltpu.VMEM((s_len, B_DH), F32),
                        pltpu.VMEM((B_GROUPS, s_len, B_DH), F32), pltpu.VMEM((B_GROUPS, s_len, B_DH), F32)],
        compiler_params=_params(("arbitrary", "arbitrary", "arbitrary")),
    )(proj_b, proj_b, cos_t, sin_t, gains8)


def _attn_bwd_body(qkv_ref, z_ref, cos_ref, sin_ref, gain_ref, dog_ref, o_ref, lse_ref,
                   dqkv_ref, dz_ref, dgain_ref,
                   qn_scr, kn_scr, dqn_scr, dkn_scr, do_scr, dl_scr, ls_scr, dv_scr):
    head, grp = pl.program_id(1), pl.program_id(2)
    first = (pl.program_id(0) == 0) & (head == 0) & (grp == 0)
    s_len = z_ref.shape[0]
    n_blocks = s_len // B_BLOCK
    cos_t, sin_t = cos_ref[...], sin_ref[...]

    @pl.when(first)
    def _():
        dgain_ref[...] = jnp.zeros_like(dgain_ref)

    @pl.when(grp == 0)
    def _():
        d_og, o = dog_ref[...], o_ref[...]
        silu, dsilu = _silu_parts(z_ref[...])
        d_o = d_og * silu
        dz_ref[...] = (d_og * o * dsilu).astype(dz_ref.dtype)
        do_scr[...] = d_o
        dl_scr[...] = jnp.broadcast_to(jnp.sum(d_o * o, axis=1, keepdims=True), o.shape)
        lane = lax.broadcasted_iota(jnp.int32, o.shape, 1)
        ls_scr[...] = jnp.broadcast_to(
            jnp.sum(jnp.where(lane == head, lse_ref[...], 0.0), axis=1, keepdims=True), o.shape)

    for gi, dil in enumerate(B_DILATIONS):
        @pl.when(grp == gi)
        def _(gi=gi, dil=dil):
            q_raw, k_raw = qkv_ref[0], qkv_ref[1]
            gq = gain_ref[gi:gi + 1, :]
            gk = gain_ref[B_GROUPS + gi:B_GROUPS + gi + 1, :]
            qn_scr[...], rq = _norm_rope(q_raw, gq, cos_t, sin_t)
            kn_scr[...], rk = _norm_rope(k_raw, gk, cos_t, sin_t)
            def blocks(it, carry):
                scored = []
                for j in range(ATT_UNROLL):
                    cur, prev, has_prev = _stream_rows(it * ATT_UNROLL + j, dil, s_len)
                    rc, rp = _rows(cur, dil), _rows(prev, dil)
                    qb, d_ob = _bf(qn_scr[rc, :]), _bf(do_scr[rc, :])
                    k2 = _bf(jnp.concatenate([kn_scr[rp, :], kn_scr[rc, :]], axis=0))
                    v2 = _bf(jnp.concatenate([qkv_ref.at[2][rp, :], qkv_ref.at[2][rc, :]], axis=0))
                    scored.append((rc, rp, has_prev, qb, d_ob, k2,
                                   _mm_nt(qb, k2) * ATT_SCALE, _mm_nt(d_ob, v2)))
                grads = []
                for rc, rp, has_prev, qb, d_ob, k2, s, d_p in scored:
                    p = jnp.exp(jnp.where(_band_mask(has_prev), s - ls_scr[rc, :][:, 0:1], NEG_BIG))
                    ds = _bf(p * (d_p - dl_scr[rc, :][:, 0:1]))
                    grads.append((rc, rp, has_prev,
                                  _mm(ds, k2) * ATT_SCALE, _mm_tn(ds, qb) * ATT_SCALE, _mm_tn(_bf(p), d_ob)))
                for j, (rc, rp, has_prev, dq, dk2, dv2) in enumerate(grads):
                    dqn_scr[rc, :] = dq
                    if j == 0:
                        @pl.when(has_prev)
                        def _():
                            dkn_scr[rp, :] += dk2[:B_BLOCK]
                            dv_scr[rp, :] += dv2[:B_BLOCK]
                    if j + 1 < ATT_UNROLL:
                        dkn_scr[rc, :] = dk2[B_BLOCK:] + grads[j + 1][4][:B_BLOCK]
                        dv_scr[rc, :] = dv2[B_BLOCK:] + grads[j + 1][5][:B_BLOCK]
                    else:
                        dkn_scr[rc, :] = dk2[B_BLOCK:]
                        dv_scr[rc, :] = dv2[B_BLOCK:]
                return carry

            lax.fori_loop(0, n_blocks // ATT_UNROLL, blocks, 0)
            dq, dgq = _norm_rope_bwd(q_raw, rq, gq, cos_t, sin_t, dqn_scr[...])
            dk, dgk = _norm_rope_bwd(k_raw, rk, gk, cos_t, sin_t, dkn_scr[...])
            dqkv_ref[0] = dq.astype(dqkv_ref.dtype)
            dqkv_ref[1] = dk.astype(dqkv_ref.dtype)
            dqkv_ref[2] = dv_scr[...].astype(dqkv_ref.dtype)
            dgain_ref[gi:gi + 1, :] += dgq
            dgain_ref[B_GROUPS + gi:B_GROUPS + gi + 1, :] += dgk


def attn_bwd(proj_b, cos_t, sin_t, gains8, d_og, o, lse, n_seq):
    t_rows = proj_b.shape[1]
    s_len = t_rows // n_seq
    head_blk = pl.BlockSpec((s_len, B_DH), lambda b, h, g: (b, h))
    seq_blk = pl.BlockSpec((s_len, 128), lambda b, h, g: (b, 0))
    grp_blk = pl.BlockSpec((3, s_len, B_DH), lambda b, h, g: (g, b, h))
    gain_blk = pl.BlockSpec((8, 128), lambda b, h, g: (0, 0))
    return pl.pallas_call(
        _attn_bwd_body, name="attn_bwd", grid=(n_seq, B_HEADS, B_GROUPS),
        in_specs=[
            grp_blk,
            pl.BlockSpec((None, s_len, B_DH), lambda b, h, g: (B_PIECES - 1, b, h)),
            seq_blk, seq_blk, gain_blk, head_blk, head_blk, seq_blk,
        ],
        out_specs=[grp_blk, head_blk, gain_blk],
        out_shape=[jax.ShapeDtypeStruct((3 * B_GROUPS, t_rows, B_W), BF16),
                   jax.ShapeDtypeStruct((t_rows, B_W), BF16),
                   jax.ShapeDtypeStruct((8, 128), F32)],
        scratch_shapes=[pltpu.VMEM((s_len, B_DH), F32) for _ in range(8)],
        compiler_params=_params(("arbitrary", "arbitrary", "arbitrary")),
    )(proj_b, proj_b, cos_t, sin_t, gains8, d_og, o, lse)


def rope_tables(positions):
    inv_freq = ROPE_THETA ** (-jnp.arange(0, ROPE_DIMS, 2, dtype=F32) / ROPE_DIMS)
    ang = positions.astype(F32)[:, None] * inv_freq
    cos, sin = jnp.cos(ang), jnp.sin(ang)
    t_rows = positions.shape[0]
    rest = B_DH - ROPE_DIMS
    cos_t = jnp.concatenate([cos, cos, jnp.ones((t_rows, rest), F32)], axis=1)
    sin_t = jnp.concatenate([-sin, sin, jnp.zeros((t_rows, rest), F32)], axis=1)
    return cos_t, sin_t


def _rms_fwd_body(x_ref, g_ref, *rest, layer):
    h_ref, ht_ref = rest[-2:]
    x = x_ref[...]
    r = lax.rsqrt(jnp.mean(x * x, axis=1, keepdims=True) + EPS)
    h = x * r * g_ref[layer:layer + 1, :]
    h_ref[...] = h.astype(h_ref.dtype)
    ht_ref[...] = h.T.astype(ht_ref.dtype)


def rms_fwd(x, gains8, layer, after=None):
    t_rows, d = x.shape
    tm = min(ROW_TILE, t_rows)
    in_specs = [pl.BlockSpec((tm, d), lambda i: (i, 0)), pl.BlockSpec((8, d), lambda i: (0, 0))]
    args = [x, gains8]
    if after is not None:
        in_specs.append(HBM_SPEC)
        args.append(after)
    return pl.pallas_call(
        functools.partial(_rms_fwd_body, layer=layer), name=f"rms_fwd_{layer}", grid=(t_rows // tm,),
        in_specs=in_specs,
        out_specs=[pl.BlockSpec((tm, d), lambda i: (i, 0)), pl.BlockSpec((d, tm), lambda i: (0, i))],
        out_shape=[jax.ShapeDtypeStruct((t_rows, d), BF16), jax.ShapeDtypeStruct((d, t_rows), BF16)],
        compiler_params=_params(("arbitrary",)),
    )(*args)


def _rms_bwd_body(x_ref, g_ref, dh_ref, res_ref, dx_ref, dg_ref, *, layer):
    x, dh = x_ref[...], dh_ref[...]
    r = lax.rsqrt(jnp.mean(x * x, axis=1, keepdims=True) + EPS)
    xr = x * r
    u = dh * g_ref[layer:layer + 1, :]
    dx_ref[...] = res_ref[...] + r * u - xr * (r * r) * jnp.mean(u * x, axis=1, keepdims=True)

    @pl.when(pl.program_id(0) == 0)
    def _():
        dg_ref[...] = jnp.zeros_like(dg_ref)

    dg_ref[0:1, :] += jnp.sum(dh * xr, axis=0, keepdims=True)


def rms_bwd(x, gains8, layer, dh, d_res):
    t_rows, d = x.shape
    tm = min(ROW_TILE, t_rows)
    blk = pl.BlockSpec((tm, d), lambda i: (i, 0))
    gblk = pl.BlockSpec((8, d), lambda i: (0, 0))
    return pl.pallas_call(
        functools.partial(_rms_bwd_body, layer=layer), name=f"rms_bwd_{layer}", grid=(t_rows // tm,),
        in_specs=[blk, gblk, blk, blk],
        out_specs=[blk, gblk],
        out_shape=[jax.ShapeDtypeStruct((t_rows, d), F32), jax.ShapeDtypeStruct((8, d), F32)],
        compiler_params=_params(("arbitrary",)),
    )(x, gains8, dh, d_res)


def _piece_col(p):
    return jnp.where(p < 3 * B_GROUPS, (p % 3) * B_GROUPS + p // 3, 3 * B_GROUPS)


def _mm_nn_body(a_ref, w_ref, *rest, has_res, a_resident):
    o_ref = rest[-1]
    tm = o_ref.shape[0]
    a = a_ref[pl.ds(pl.multiple_of(pl.program_id(1) * tm, tm), tm), :] if a_resident else a_ref[...]
    acc = jnp.dot(a, w_ref[...], preferred_element_type=F32)
    if has_res:
        acc = acc + rest[0][...]
    o_ref[...] = acc


def mm_nn(a, w, residual=None, *, tn, name, a_resident=False):
    m, k = a.shape
    n = w.shape[1]
    tm = min(NT_ROW_TILE, m)
    a_spec = pl.BlockSpec((m, k), lambda j, i: (0, 0)) if a_resident else pl.BlockSpec((tm, k), lambda j, i: (i, 0))
    in_specs = [a_spec, pl.BlockSpec((k, tn), lambda j, i: (0, j))]
    args = [a, w]
    if residual is not None:
        in_specs.append(pl.BlockSpec((tm, tn), lambda j, i: (i, j)))
        args.append(residual)
    return pl.pallas_call(
        functools.partial(_mm_nn_body, has_res=residual is not None, a_resident=a_resident), name=name,
        grid=(n // tn, m // tm),
        in_specs=in_specs,
        out_specs=pl.BlockSpec((tm, tn), lambda j, i: (i, j)),
        out_shape=jax.ShapeDtypeStruct((m, n), F32),
        compiler_params=_params(("arbitrary", "arbitrary")),
    )(*args)


def mm_nn_pieces(a, w, *, name):
    m, k = a.shape
    tm = min(NT_ROW_TILE, m)
    return pl.pallas_call(
        functools.partial(_mm_nn_body, has_res=False, a_resident=True), name=name, grid=(B_PIECES, m // tm),
        in_specs=[pl.BlockSpec((m, k), lambda p, i: (0, 0)),
                  pl.BlockSpec((k, B_W), lambda p, i: (0, _piece_col(p)))],
        out_specs=pl.BlockSpec((None, tm, B_W), lambda p, i: (p, i, 0)),
        out_shape=jax.ShapeDtypeStruct((B_PIECES, m, B_W), F32),
        compiler_params=_params(("arbitrary", "arbitrary")),
    )(a, w)


NT_ROW_TILE = 1024


def _mm_nt_body(g_ref, w_ref, *rest, has_init):
    o_ref = rest[-1]
    j = pl.program_id(1)
    part = lax.dot_general(_bf(g_ref[...]), w_ref[...], (((1,), (1,)), ((), ())), preferred_element_type=F32)

    @pl.when(j == 0)
    def _():
        o_ref[...] = part + rest[0][...] if has_init else part

    @pl.when(j > 0)
    def _():
        o_ref[...] += part


def mm_nt(g, w, init=None, *, tn, col_off=0, name, after=None):
    m, n = g.shape
    k = w.shape[0]
    tm = min(NT_ROW_TILE, m)
    in_specs = [pl.BlockSpec((tm, tn), lambda i, j: (i, j)),
                pl.BlockSpec((k, tn), lambda i, j: (0, col_off + j))]
    args = [g, w]
    if init is not None:
        in_specs.append(pl.BlockSpec((tm, k), lambda i, j: (i, 0)))
        args.append(init)
    if after is not None:
        in_specs.append(HBM_SPEC)
        args.append(after)
    return pl.pallas_call(
        functools.partial(_mm_nt_body, has_init=init is not None), name=name, grid=(m // tm, n // tn),
        in_specs=in_specs,
        out_specs=pl.BlockSpec((tm, k), lambda i, j: (i, 0)),
        out_shape=jax.ShapeDtypeStruct((m, k), F32),
        compiler_params=_params(("arbitrary", "arbitrary")),
    )(*args)


def mm_nt_multi(gs, w, *, tn, name, after=None):
    m = gs[0].shape[0]
    k = w.shape[0]
    tm = min(NT_ROW_TILE, m)
    tiles = [g.shape[1] // tn for g in gs]
    starts = [sum(tiles[:i]) for i in range(len(gs))]

    def body(*refs):
        g_refs, w_ref, o_ref = refs[:len(gs)], refs[len(gs)], refs[-1]
        j = pl.program_id(1)
        for g_ref, lo, cnt in zip(g_refs, starts, tiles):
            @pl.when((j >= lo) & (j < lo + cnt))
            def _(g_ref=g_ref):
                part = lax.dot_general(_bf(g_ref[...]), w_ref[...], (((1,), (1,)), ((), ())),
                                       preferred_element_type=F32)

                @pl.when(j == 0)
                def _():
                    o_ref[...] = part

                @pl.when(j > 0)
                def _():
                    o_ref[...] += part

    def g_spec(lo, cnt):
        return pl.BlockSpec((tm, tn), lambda i, j: (i, jnp.clip(j - lo, 0, cnt - 1)))

    in_specs = [g_spec(lo, cnt) for lo, cnt in zip(starts, tiles)] + [pl.BlockSpec((k, tn), lambda i, j: (0, j))]
    args = list(gs) + [w]
    if after is not None:
        in_specs.append(HBM_SPEC)
        args.append(after)
    return pl.pallas_call(
        body, name=name, grid=(m // tm, sum(tiles)),
        in_specs=in_specs,
        out_specs=pl.BlockSpec((tm, k), lambda i, j: (i, 0)),
        out_shape=jax.ShapeDtypeStruct((m, k), F32),
        compiler_params=_params(("arbitrary", "arbitrary")),
    )(*args)


def mm_nt_pieces(g9, gz, w, *, name):
    n_p, m, _ = g9.shape
    k = w.shape[0]
    tm = min(2 * NT_ROW_TILE, m)

    def body(g_ref, z_ref, w_ref, o_ref):
        p = pl.program_id(1)

        def accumulate(src):
            part = lax.dot_general(_bf(src[...]), w_ref[...], (((1,), (1,)), ((), ())), preferred_element_type=F32)

            @pl.when(p == 0)
            def _():
                o_ref[...] = part

            @pl.when(p > 0)
            def _():
                o_ref[...] += part

        @pl.when(p < n_p)
        def _():
            accumulate(g_ref)

        @pl.when(p == n_p)
        def _():
            accumulate(z_ref)

    return pl.pallas_call(
        body, name=name, grid=(m // tm, n_p + 1),
        in_specs=[pl.BlockSpec((None, tm, B_W), lambda i, p: (jnp.minimum(p, n_p - 1), i, 0)),
                  pl.BlockSpec((tm, B_W), lambda i, p: (i, 0)),
                  pl.BlockSpec((k, B_W), lambda i, p: (0, _piece_col(p)))],
        out_specs=pl.BlockSpec((tm, k), lambda i, p: (i, 0)),
        out_shape=jax.ShapeDtypeStruct((m, k), F32),
        compiler_params=_params(("arbitrary", "arbitrary")),
    )(g9, gz, w)


def _mm_tn_body(a_ref, g_ref, o_ref, *, a_is_transposed):
    lhs_dim = 1 if a_is_transposed else 0
    o_ref[...] = lax.dot_general(a_ref[...], _bf(g_ref[...]), (((lhs_dim,), (0,)), ((), ())),
                                 preferred_element_type=F32).astype(o_ref.dtype)


def mm_tn(a, g, *, tn, out_dtype, name, a_is_transposed=False):
    k = a.shape[0] if a_is_transposed else a.shape[1]
    m, n = g.shape
    return pl.pallas_call(
        functools.partial(_mm_tn_body, a_is_transposed=a_is_transposed), name=name, grid=(n // tn,),
        in_specs=[pl.BlockSpec(a.shape, lambda j: (0, 0)), pl.BlockSpec((m, tn), lambda j: (0, j))],
        out_specs=pl.BlockSpec((k, tn), lambda j: (0, j)),
        out_shape=jax.ShapeDtypeStruct((k, n), out_dtype),
        compiler_params=_params(("arbitrary",)),
    )(a, g)


def mm_tn_multi(at, gs, *, unit, out_dtype, name):
    k, m = at.shape
    counts = [g.shape[1] // unit for g in gs]
    starts = [sum(counts[:i]) for i in range(len(gs))]

    def body(at_ref, *refs):
        g_refs, o_ref = refs[:-1], refs[-1]
        u = pl.program_id(0)
        for g_ref, lo, cnt in zip(g_refs, starts, counts):
            @pl.when((u >= lo) & (u < lo + cnt))
            def _(g_ref=g_ref):
                _mm_tn_body(at_ref, g_ref, o_ref, a_is_transposed=True)

    def g_spec(lo, cnt):
        return pl.BlockSpec((m, unit), lambda u: (0, jnp.clip(u - lo, 0, cnt - 1)))

    return pl.pallas_call(
        body, name=name, grid=(sum(counts),),
        in_specs=[pl.BlockSpec((k, m), lambda u: (0, 0))] + [g_spec(lo, cnt) for lo, cnt in zip(starts, counts)],
        out_specs=pl.BlockSpec((k, unit), lambda u: (0, u)),
        out_shape=jax.ShapeDtypeStruct((k, sum(counts) * unit), out_dtype),
        compiler_params=_params(("arbitrary",)),
    )(at, *gs)


B_UNIT = 256
B_IN_COLS = B_PIECES * B_W
B_SHARD_UNITS = B_IN_COLS // N_DEV // B_UNIT


def mm_tn_b_in(at, g9, gz, *, out_dtype, name):
    k, m = at.shape
    per_piece = B_W // B_UNIT
    body_one = functools.partial(_mm_tn_body, a_is_transposed=True)
    n_units = B_IN_COLS // B_UNIT

    def g_map(u):
        nat = jnp.minimum(u // per_piece, 3 * B_GROUPS - 1)
        piece = (nat % B_GROUPS) * 3 + nat // B_GROUPS
        return (piece, 0, u % per_piece)

    def body(a_ref, g_ref, z_ref, o_ref):
        u = pl.program_id(0)

        @pl.when(u < 3 * B_GROUPS * per_piece)
        def _():
            body_one(a_ref, g_ref, o_ref)

        @pl.when(u >= 3 * B_GROUPS * per_piece)
        def _():
            body_one(a_ref, z_ref, o_ref)

    return pl.pallas_call(
        body, name=name, grid=(n_units,),
        in_specs=[pl.BlockSpec((k, m), lambda u: (0, 0)),
                  pl.BlockSpec((None, m, B_UNIT), g_map),
                  pl.BlockSpec((m, B_UNIT), lambda u: (0, jnp.where(u < 3 * B_GROUPS * per_piece, 0, u % per_piece)))],
        out_specs=pl.BlockSpec((None, k, B_UNIT), lambda u: (u // B_SHARD_UNITS, 0, u % B_SHARD_UNITS)),
        out_shape=jax.ShapeDtypeStruct((N_DEV, k, B_IN_COLS // N_DEV), out_dtype),
        compiler_params=_params(("arbitrary",)),
    )(at, g9, gz)


def _out_norm_body(a_ref, w_ref, res_ref, g_ref, x_ref, h_ref, ht_ref, *, layer):
    x = jnp.dot(a_ref[...], w_ref[...], preferred_element_type=F32) + res_ref[...]
    x_ref[...] = x
    r = lax.rsqrt(jnp.mean(x * x, axis=1, keepdims=True) + EPS)
    h = x * r * g_ref[layer:layer + 1, :]
    h_ref[...] = h.astype(h_ref.dtype)
    ht_ref[...] = h.T.astype(ht_ref.dtype)


def out_and_norm(a, w, residual, gains8, layer):
    t_rows, k = a.shape
    d = w.shape[1]
    tm = min(ROW_TILE, t_rows)
    blk = pl.BlockSpec((tm, d), lambda i: (i, 0))
    return pl.pallas_call(
        functools.partial(_out_norm_body, layer=layer), name=f"out_norm_{layer}", grid=(t_rows // tm,),
        in_specs=[pl.BlockSpec((tm, k), lambda i: (i, 0)), pl.BlockSpec((k, d), lambda i: (0, 0)), blk,
                  pl.BlockSpec((8, d), lambda i: (0, 0))],
        out_specs=[blk, blk, pl.BlockSpec((d, tm), lambda i: (0, i))],
        out_shape=[jax.ShapeDtypeStruct((t_rows, d), F32), jax.ShapeDtypeStruct((t_rows, d), BF16),
                   jax.ShapeDtypeStruct((d, t_rows), BF16)],
        compiler_params=_params(("arbitrary",)),
    )(a, w, residual, gains8)


def _out_loss_body(a_ref, w_ref, res_ref, t_ref, dy_ref, loss_ref, acc, *, n_steps):
    i = pl.program_id(0)
    d = w_ref.shape[1]
    y = jnp.dot(a_ref[...], w_ref[...], preferred_element_type=F32) + res_ref[...]
    err = y - t_ref[...]
    dy_ref[...] = err * (1.0 / d)

    @pl.when(i == 0)
    def _():
        acc[...] = jnp.zeros_like(acc)

    acc[...] += jnp.sum(err * err, axis=0, keepdims=True)

    @pl.when(i == n_steps - 1)
    def _():
        total = jnp.sum(acc[...], axis=1, keepdims=True) * (0.5 / d)
        loss_ref[...] = jnp.broadcast_to(total, loss_ref.shape)


def out_and_loss(a, w, residual, target):
    t_rows, k = a.shape
    d = w.shape[1]
    tm = min(ROW_TILE, t_rows)
    n_steps = t_rows // tm
    blk = pl.BlockSpec((tm, d), lambda i: (i, 0))
    return pl.pallas_call(
        functools.partial(_out_loss_body, n_steps=n_steps), name="out_b_loss", grid=(n_steps,),
        in_specs=[pl.BlockSpec((tm, k), lambda i: (i, 0)), pl.BlockSpec((k, d), lambda i: (0, 0)), blk, blk],
        out_specs=[blk, pl.BlockSpec((8, 128), lambda i: (0, 0))],
        out_shape=[jax.ShapeDtypeStruct((t_rows, d), F32), jax.ShapeDtypeStruct((8, 128), F32)],
        scratch_shapes=[pltpu.VMEM((1, d), F32)],
        compiler_params=_params(("arbitrary",)),
    )(a, w, residual, target)


def _adamw_body(p_ref, w_ref, m_ref, v_ref, g_ref, d_ref, nm_ref, nv_ref):
    g = p_ref[0].astype(F32)
    for s in range(1, N_DEV):
        g = g + p_ref[s].astype(F32)
    w = w_ref[...]
    m = ADAM_B1 * m_ref[...] + (1.0 - ADAM_B1) * g
    v = ADAM_B2 * v_ref[...] + (1.0 - ADAM_B2) * (g * g)
    m_hat = m / (1.0 - ADAM_B1 ** ADAM_STEP)
    v_hat = v / (1.0 - ADAM_B2 ** ADAM_STEP)
    g_ref[...] = g
    d_ref[...] = -ADAM_LR * (m_hat / (jnp.sqrt(v_hat) + ADAM_EPS) + ADAM_WD * w)
    nm_ref[...] = m
    nv_ref[...] = v


def adamw(parts, w, m, v, *, name):
    _, r, c = w.shape
    tr = r if r <= 256 else 256
    blk = pl.BlockSpec((None, tr, c), lambda i: (0, i, 0))
    out = jax.ShapeDtypeStruct((1, r, c), F32)
    return pl.pallas_call(
        _adamw_body, name=name, grid=(r // tr,),
        in_specs=[pl.BlockSpec((N_DEV, tr, c), lambda i: (0, i, 0)), blk, blk, blk],
        out_specs=[blk, blk, blk, blk],
        out_shape=[out, out, out, out],
        compiler_params=_params(("arbitrary",)),
    )(parts, w, m, v)


MESH_ID = pl.DeviceIdType.MESH
HBM_SPEC = pl.BlockSpec(memory_space=pl.ANY)


def _my_place():
    return lax.axis_index("x"), lax.axis_index("y"), lax.axis_index("c")


def _flat(x, y, c):
    return 4 * x + 2 * y + c


def _all_gather_body(*refs, n, pieces):
    ins, outs = refs[:n], refs[n:2 * n]
    send_sems, recv_sems, local_sems = refs[2 * n:]
    x, y, c = _my_place()
    me, sibling = (x, y, c), (x, y, 1 - c)
    chips = [(1 - x, y), (x, 1 - y), (1 - x, 1 - y)]
    pending = []
    unit = 0
    for a in range(n):
        src, out = ins[a], outs[a]
        mine = pltpu.make_async_copy(src, out.at[_flat(*me)], local_sems.at[a])
        mine.start()
        rows_p = src.shape[0] // pieces[a]
        for p in range(pieces[a]):
            rows = pl.ds(p * rows_p, rows_p)

            def copy(k, block, to, from_input=False, unit=unit, src=src, out=out, rows=rows):
                slot = out.at[_flat(*block)].at[rows]
                return pltpu.make_async_remote_copy(
                    src_ref=src.at[rows] if from_input else slot, dst_ref=slot,
                    send_sem=send_sems.at[7 * unit + k], recv_sem=recv_sems.at[7 * unit + k],
                    device_id=to, device_id_type=MESH_ID)

            first = [copy(0, me, sibling, True)] + [copy(1 + j, me, (*chip, c), True)
                                                    for j, chip in enumerate(chips)]
            for cp in first:
                cp.start()
            pending.append((copy, first))
            unit += 1
        pending.append(mine)
    for item in pending:
        if not isinstance(item, tuple):
            item.wait()
            continue
        copy, first = item
        passed = [copy(4 + j, (*chip, c), sibling) for j, chip in enumerate(chips)]
        for j, chip in enumerate(chips):
            copy(1 + j, (*chip, c), me).wait_recv()
            passed[j].start()
        copy(0, sibling, me).wait_recv()
        for j, chip in enumerate(chips):
            copy(4 + j, (*chip, 1 - c), me).wait_recv()
        for cp in first + passed:
            cp.wait_send()


GATHER_PIECE_ROWS = 256


def all_gather(shards, *, name):
    n = len(shards)
    pieces = [max(1, s.shape[0] // GATHER_PIECE_ROWS) for s in shards]
    units = sum(pieces)
    return pl.pallas_call(
        functools.partial(_all_gather_body, n=n, pieces=pieces), name=name,
        in_specs=[HBM_SPEC] * n, out_specs=[HBM_SPEC] * n,
        out_shape=[jax.ShapeDtypeStruct((N_DEV,) + s.shape, s.dtype) for s in shards],
        scratch_shapes=[pltpu.SemaphoreType.DMA((7 * units,)), pltpu.SemaphoreType.DMA((7 * units,)),
                        pltpu.SemaphoreType.DMA((n,))],
    )(*shards)


PEER_FLIPS = [(0, 0, 1), (1, 0, 0), (0, 1, 0), (1, 1, 0), (1, 0, 1), (0, 1, 1), (1, 1, 1)]


def _all_to_all_body(*refs, n):
    ins, outs = refs[:n], refs[n:2 * n]
    send_sems, recv_sems, local_sems = refs[2 * n:]
    x, y, c = _my_place()
    me = _flat(x, y, c)
    waits = []
    for a in range(n):
        src, out = ins[a], outs[a]
        mine = pltpu.make_async_copy(src.at[me], out.at[me], local_sems.at[a])
        mine.start()
        waits.append(mine)
        for k, (fx, fy, fc) in enumerate(PEER_FLIPS):
            peer = (1 - x if fx else x, 1 - y if fy else y, 1 - c if fc else c)
            theirs = _flat(*peer)
            sems = dict(send_sem=send_sems.at[7 * a + k], recv_sem=recv_sems.at[7 * a + k],
                        device_id=peer, device_id_type=MESH_ID)
            send = pltpu.make_async_remote_copy(src_ref=src.at[theirs], dst_ref=out.at[me], **sems)
            send.start()
            recv = pltpu.make_async_remote_copy(src_ref=src.at[theirs], dst_ref=out.at[theirs], **sems)
            waits.append((send, recv))
    for w in waits:
        if isinstance(w, tuple):
            w[0].wait_send()
            w[1].wait_recv()
        else:
            w.wait()


def all_to_all(parts, *, name):
    n = len(parts)
    return pl.pallas_call(
        functools.partial(_all_to_all_body, n=n), name=name,
        in_specs=[HBM_SPEC] * n, out_specs=[HBM_SPEC] * n,
        out_shape=[jax.ShapeDtypeStruct(p.shape, p.dtype) for p in parts],
        scratch_shapes=[pltpu.SemaphoreType.DMA((7 * n,)), pltpu.SemaphoreType.DMA((7 * n,)),
                        pltpu.SemaphoreType.DMA((n,))],
    )(*parts)


HBM_ONLY = pl.BlockSpec(memory_space=pltpu.HBM)
SEM_SPEC = pl.BlockSpec(memory_space=pltpu.SEMAPHORE)
DATAFLOW_EFFECT = pltpu.SideEffectType.DATAFLOW_SIDE_EFFECTING


def _split_copies(srcs, lands, send_sems, recv_sems, n, scatter):
    x, y, c = _my_place()
    me = _flat(x, y, c)
    pairs = []
    for a in range(n):
        for k, (fx, fy, fc) in enumerate(PEER_FLIPS):
            peer = (1 - x if fx else x, 1 - y if fy else y, 1 - c if fc else c)
            theirs = _flat(*peer)
            src = srcs[a].at[theirs] if scatter else srcs[a]
            sems = dict(send_sem=send_sems.at[7 * a + k], recv_sem=recv_sems.at[7 * a + k],
                        device_id=peer, device_id_type=MESH_ID)
            pairs.append((pltpu.make_async_remote_copy(src_ref=src, dst_ref=lands[a].at[me], **sems),
                          pltpu.make_async_remote_copy(src_ref=src, dst_ref=lands[a].at[theirs], **sems)))
    return pairs


def _exchange_start_body(*refs, n, scatter):
    srcs, lands = refs[:n], refs[n:2 * n]
    send_sems, recv_sems = refs[2 * n], refs[2 * n + 1]
    token = refs[-1]
    for send, _ in _split_copies(srcs, lands, send_sems, recv_sems, n, scatter):
        send.start()
    token[...] = jnp.zeros_like(token)


def exchange_start(srcs, lands, *, scatter, name):
    n = len(srcs)
    args = [pltpu.with_memory_space_constraint(t, pltpu.HBM) for t in list(srcs) + list(lands)]
    outs = pl.pallas_call(
        functools.partial(_exchange_start_body, n=n, scatter=scatter), name=name,
        out_shape=(pltpu.SemaphoreType.DMA((7 * n,)), pltpu.SemaphoreType.DMA((7 * n,)),
                   *[pltpu.HBM(t.shape, t.dtype) for t in args],
                   jax.ShapeDtypeStruct((8, 128), F32)),
        in_specs=[HBM_ONLY] * (2 * n),
        out_specs=(SEM_SPEC, SEM_SPEC, *[HBM_ONLY] * (2 * n), pl.BlockSpec(memory_space=pltpu.VMEM)),
        input_output_aliases={i: 2 + i for i in range(2 * n)},
        compiler_params=pltpu.CompilerParams(has_side_effects=DATAFLOW_EFFECT),
    )(*args)
    return outs[0], outs[1], outs[2:2 + n], outs[2 + n:2 + 2 * n], outs[-1]


def _exchange_wait_body(*refs, n, scatter):
    srcs, lands = refs[:n], refs[n:2 * n]
    send_sems, recv_sems = refs[2 * n], refs[2 * n + 1]
    for send, recv in _split_copies(srcs, lands, send_sems, recv_sems, n, scatter):
        send.wait_send()
        recv.wait_recv()


def exchange_wait(send_sems, recv_sems, srcs, lands, after, *, scatter, name):
    n = len(srcs)
    outs = pl.pallas_call(
        functools.partial(_exchange_wait_body, n=n, scatter=scatter), name=name,
        out_shape=tuple(pltpu.HBM(t.shape, t.dtype) for t in list(srcs) + list(lands)),
        in_specs=[HBM_ONLY] * (2 * n) + [SEM_SPEC, SEM_SPEC, HBM_SPEC],
        out_specs=tuple([HBM_ONLY] * (2 * n)),
        input_output_aliases={i: i for i in range(2 * n)},
        compiler_params=pltpu.CompilerParams(has_side_effects=DATAFLOW_EFFECT),
    )(*srcs, *lands, send_sems, recv_sems, after)
    return outs[n:]


def _own_slot_only(shape_dtype, own, slot):
    land = lax.empty(shape_dtype.shape, shape_dtype.dtype)
    return lax.dynamic_update_slice(land, own[None], (slot,) + (0,) * own.ndim)


def _pad_rows(a, rows=8):
    return jnp.pad(a, ((0, rows - a.shape[0]), (0, 0)))


def _gate_rows(a_log, dt_bias):
    z = jnp.zeros((8, 128), F32)
    return z.at[0, A_HEADS:2 * A_HEADS].set(a_log[0]).at[1, A_HEADS:2 * A_HEADS].set(dt_bias[0])


def _pack_small(norm_g, a_log, a_dt_bias, a_norm_g, b_q_norm_g, b_k_norm_g):
    return jnp.concatenate([
        norm_g[0].reshape(8, 128), norm_g[1].reshape(8, 128),
        _gate_rows(a_log, a_dt_bias),
        _pad_rows(a_norm_g[0].reshape(2, 128)),
        _pad_rows(jnp.concatenate([b_q_norm_g[0], b_k_norm_g[0]], axis=0)),
    ], axis=0)


def _unpack_small(p):
    return (p[0:16].reshape(2, D_MODEL), p[16:17, A_HEADS:2 * A_HEADS], p[17:18, A_HEADS:2 * A_HEADS],
            p[24:26].reshape(1, A_DV), p[32:35][None], p[35:38][None])


def kernel(x, positions, norm_g, a_w_in, a_conv_w, a_log, a_dt_bias, a_norm_g, a_w_out, b_w_in, b_q_norm_g, b_k_norm_g, b_w_out, loss_target, m_norm_g, m_a_w_in, m_a_conv_w, m_a_log, m_a_dt_bias, m_a_norm_g, m_a_w_out, m_b_w_in, m_b_q_norm_g, m_b_k_norm_g, m_b_w_out, v_norm_g, v_a_w_in, v_a_conv_w, v_a_log, v_a_dt_bias, v_a_norm_g, v_a_w_out, v_b_w_in, v_b_q_norm_g, v_b_k_norm_g, v_b_w_out):
    n_seq, s_len, d = x.shape
    t_rows = n_seq * s_len
    n_chunks = s_len // CHUNK
    x0 = x.reshape(t_rows, d)
    target = loss_target.reshape(t_rows, d)
    my_slot = _flat(*_my_place())

    g_a_in, g_conv = all_gather([a_w_in[0].astype(BF16), _pad_rows(a_conv_w[0])], name="gather_weights_first")
    later = [a_w_out[0].astype(BF16), b_w_in[0].astype(BF16), b_w_out[0].astype(BF16)]
    lands = [_own_slot_only(jax.ShapeDtypeStruct((N_DEV,) + t.shape, t.dtype), t, my_slot) for t in later]
    w_send, w_recv, later, lands, w_token = exchange_start(later, lands, scatter=False, name="gather_weights_start")
    w_a_in = jnp.pad(g_a_in.transpose(1, 0, 2).reshape(d, A_IN), ((0, 0), (0, A_IN_PAD - A_IN)))
    conv_w8 = g_conv.transpose(1, 0, 2).reshape(8, 2 * A_QK + A_VW)

    gains_model = _pad_rows(norm_g)
    gate_prm = _gate_rows(a_log, a_dt_bias)
    gain_a_out = _pad_rows(a_norm_g)
    gains_qk = _pad_rows(jnp.concatenate([b_q_norm_g[0], b_k_norm_g[0]], axis=0))
    cos_t, sin_t = rope_tables(positions.reshape(t_rows))

    h0, h0_t = rms_fwd(x0, gains_model, 0, after=w_token)
    proj_a = mm_nn(h0, w_a_in, tn=896, name="proj_a", a_resident=True)
    gates_col, gates_row = a_gates_fwd(proj_a, gate_prm, n_seq)
    gates_row = gates_row.reshape(n_seq, 2 * A_HEADS, s_len // SUPER, 1, SUPER)
    o_a, tinv, states, og_a, q_a, k_a, v_a = gdn_fwd(proj_a, conv_w8, gates_col, gates_row, gain_a_out, n_seq)
    g_a_out, g_b_in, g_b_out = exchange_wait(w_send, w_recv, later, lands, og_a, scatter=False,
                                             name="gather_weights_wait")
    w_a_out = g_a_out.reshape(A_VW, d)
    w_b_in = g_b_in.transpose(1, 0, 2).reshape(d, B_IN_COLS)
    w_b_out = g_b_out.reshape(B_W, d)
    x1, h1, h1_t = out_and_norm(og_a, w_a_out, x0, gains_model, 1)

    proj_b = mm_nn_pieces(h1, w_b_in, name="proj_b")
    og_b, o_b, lse = attn_fwd(proj_b, cos_t, sin_t, gains_qk, n_seq)
    dy, loss_blk = out_and_loss(og_b, w_b_out, x1, target)

    d_og_b = mm_nt(dy, w_b_out, tn=1024, name="d_og_b")
    dw_b_out = mm_tn(og_b, dy, tn=256, out_dtype=BF16, name="dw_b_out")
    dqkv_b, dz_b, d_gains_qk = attn_bwd(proj_b, cos_t, sin_t, gains_qk, d_og_b, o_b, lse, n_seq)
    dh1 = mm_nt_pieces(dqkv_b, dz_b, w_b_in, name="dh1")
    dw_b_in = mm_tn_b_in(h1_t, dqkv_b, dz_b, out_dtype=BF16, name="dw_b_in")
    dx1, d_gain1 = rms_bwd(x1, gains_model, 1, dh1, dy)

    dw_a_out = mm_tn(og_a, dx1, tn=256, out_dtype=BF16, name="dw_a_out")
    early = [dw_b_in, dw_b_out.reshape(N_DEV, B_W // N_DEV, d), dw_a_out.reshape(N_DEV, A_VW // N_DEV, d)]
    lands = [_own_slot_only(t, lax.dynamic_index_in_dim(t, my_slot, 0, keepdims=False), my_slot) for t in early]
    g_send, g_recv, early, lands, g_token = exchange_start(early, lands, scatter=True, name="scatter_grads_start")

    d_og_a = mm_nt(dx1, w_a_out, tn=1024, name="d_og_a", after=g_token)
    d_xq, d_xk, d_xv, dgates, dz_a, d_gain_a_out, d_cq, d_ck, d_cv = gdn_bwd(
        q_a, k_a, v_a, gates_col, gates_row, tinv, states, d_og_a, o_a, proj_a, conv_w8, gain_a_out, n_seq)
    d_conv = jnp.concatenate([d_cq.sum(axis=0), d_ck.sum(axis=0), d_cv.sum(axis=0)], axis=1)
    d_gate_logits, d_gate_prm = a_gates_bwd(proj_a, gate_prm, dgates, n_seq)
    dw_a_in = jnp.concatenate([
        mm_tn_multi(h0_t, [d_xq, d_xk, d_xv, dz_a], unit=256, out_dtype=BF16, name="dw_a_in_qkvz"),
        mm_tn(h0_t, d_gate_logits, tn=128, out_dtype=BF16, name="dw_a_in_gates", a_is_transposed=True),
    ], axis=1)[:, :A_IN]
    shard_a_in = A_IN // N_DEV
    last = [dw_a_in.reshape(d, N_DEV, shard_a_in).transpose(1, 0, 2)]
    last_lands = [_own_slot_only(t, lax.dynamic_index_in_dim(t, my_slot, 0, keepdims=False), my_slot) for t in last]
    l_send, l_recv, last, last_lands, l_token = exchange_start(last, last_lands, scatter=True,
                                                               name="scatter_last_start")
    dh0 = mm_nt_multi([d_xq, d_xk, d_xv, dz_a], w_a_in, tn=1024, name="dh0_qkvz", after=l_token)
    dh0 = mm_nt(d_gate_logits, w_a_in, dh0, tn=128, col_off=A_GATE_COL, name="dh0_gates")
    dx0, d_gain0 = rms_bwd(x0, gains_model, 0, dh0, dx1)

    small = jnp.concatenate([
        d_gain0[0].reshape(8, 128), d_gain1[0].reshape(8, 128), d_gate_prm,
        _pad_rows(d_gain_a_out[0].reshape(2, 128)), d_gains_qk, loss_blk], axis=0)
    n_small = small.shape[0] - loss_blk.shape[0]
    small_srcs = [small, d_conv]
    small_lands = [_own_slot_only(jax.ShapeDtypeStruct((N_DEV,) + t.shape, t.dtype), t, my_slot)
                   for t in small_srcs]
    s_send, s_recv, small_srcs, small_lands, s_token = exchange_start(small_srcs, small_lands, scatter=False,
                                                                      name="gather_small_start")

    r_b_in, r_b_out, r_a_out = exchange_wait(g_send, g_recv, early, lands, s_token, scatter=True,
                                             name="scatter_grads_wait")
    (r_a_in,) = exchange_wait(l_send, l_recv, last, last_lands, s_token, scatter=True, name="scatter_last_wait")

    upd = {}
    upd["a_w_in"] = adamw(r_a_in, a_w_in, m_a_w_in, v_a_w_in, name="adamw_a_w_in")
    upd["a_w_out"] = adamw(r_a_out, a_w_out, m_a_w_out, v_a_w_out, name="adamw_a_w_out")
    upd["b_w_in"] = adamw(r_b_in, b_w_in, m_b_w_in, v_b_w_in, name="adamw_b_w_in")
    upd["b_w_out"] = adamw(r_b_out, b_w_out, m_b_w_out, v_b_w_out, name="adamw_b_w_out")
    r_small, r_conv = exchange_wait(s_send, s_recv, small_srcs, small_lands, upd["b_w_in"][0], scatter=False,
                                    name="gather_small_wait")
    conv_cols = a_conv_w.shape[2]
    r_conv = lax.dynamic_slice(r_conv, (0, 0, my_slot * conv_cols), (N_DEV, 8, conv_cols))
    loss = jnp.sum(r_small[:, n_small, 0])
    r_small = r_small[:, :n_small]
    upd["a_conv_w"] = [t[:, :A_CONV] for t in adamw(
        r_conv, _pad_rows(a_conv_w[0])[None], _pad_rows(m_a_conv_w[0])[None], _pad_rows(v_a_conv_w[0])[None],
        name="adamw_a_conv_w")]
    small_upd = adamw(
        r_small,
        _pack_small(norm_g, a_log, a_dt_bias, a_norm_g, b_q_norm_g, b_k_norm_g)[None],
        _pack_small(m_norm_g, m_a_log, m_a_dt_bias, m_a_norm_g, m_b_q_norm_g, m_b_k_norm_g)[None],
        _pack_small(v_norm_g, v_a_log, v_a_dt_bias, v_a_norm_g, v_b_q_norm_g, v_b_k_norm_g)[None],
        name="adamw_small")
    small_names = ("norm_g", "a_log", "a_dt_bias", "a_norm_g", "b_q_norm_g", "b_k_norm_g")
    unpacked = [_unpack_small(t[0]) for t in small_upd]
    for i, nm in enumerate(small_names):
        upd[nm] = [u[i] for u in unpacked]

    order = ("norm_g", "a_w_in", "a_conv_w", "a_log", "a_dt_bias", "a_norm_g", "a_w_out",
             "b_w_in", "b_q_norm_g", "b_k_norm_g", "b_w_out")
    outs = [loss, dx0.reshape(n_seq, s_len, d)]
    for kind in range(4):
        for nm in order:
            outs.append(upd[nm][kind])
    return tuple(outs)
```

```python
import functools
import math

import jax
import jax.numpy as jnp
from jax import lax
from jax.experimental import pallas as pl
from jax.experimental.pallas import tpu as pltpu

F32 = jnp.float32
BF16 = jnp.bfloat16

D_MODEL = 1024
EPS = 1e-6
N_DEV = 8

A_HEADS = 8
A_DK = 128
A_DV = 256
A_QK = A_HEADS * A_DK
A_VW = A_HEADS * A_DV
A_CONV = 4
CHUNK = 64
A_IN = 2 * A_QK + 2 * A_VW + 2 * A_HEADS
A_IN_PAD = 2 * A_QK + 2 * A_VW + 128
A_GATE_COL = (2 * A_QK + 2 * A_VW) // 128

B_DILATIONS = (1, 4, 16)
B_GROUPS = 3
B_HEADS = 8
B_DH = 128
B_W = B_HEADS * B_DH
B_BLOCK = 128
B_PIECES = 3 * B_GROUPS + 1
ROPE_THETA = 500000.0
ROPE_DIMS = B_DH // 4
ROPE_HALF = ROPE_DIMS // 2

ADAM_LR = 0.001
ADAM_B1 = 0.9
ADAM_B2 = 0.999
ADAM_EPS = 1e-08
ADAM_WD = 0.01
ADAM_STEP = 10

VMEM_LIMIT = 60 * 1024 * 1024


def _params(sem):
    return pltpu.CompilerParams(dimension_semantics=sem, vmem_limit_bytes=VMEM_LIMIT)


def _bf(x):
    return x.astype(BF16)


def _mm(a, b):
    return jnp.dot(_bf(a), _bf(b), preferred_element_type=F32)


def _mm_nt(a, b):
    return lax.dot_general(_bf(a), _bf(b), (((1,), (1,)), ((), ())), preferred_element_type=F32)


def _mm_tn(a, b):
    return lax.dot_general(_bf(a), _bf(b), (((0,), (0,)), ((), ())), preferred_element_type=F32)


def _split(x):
    hi = _bf(x)
    return hi, _bf(x - hi.astype(F32))


def _mm3(a, b):
    ah, al = _split(a)
    bh, bl = _split(b)
    d = functools.partial(jnp.dot, preferred_element_type=F32)
    return d(ah, bh) + (d(ah, bl) + d(al, bh))


def _colsum_as_col(z):
    zh, zl = _split(z)
    ones = jnp.ones((z.shape[0], 128), BF16)
    tn = functools.partial(lax.dot_general, dimension_numbers=(((0,), (0,)), ((), ())),
                           preferred_element_type=F32)
    return (tn(zh, ones) + tn(zl, ones))[:, 0:1]


def _sigmoid(x):
    return 0.5 * jnp.tanh(0.5 * x) + 0.5


INV_BASE = 8
INV_NEWTON = 2
GDN_GROUP = 4
SUPER = GDN_GROUP * CHUNK
GDN_WIDTH = 2

A_K_COL = A_QK // A_DK
A_V_COL = 2 * A_QK // A_DV


def _inverse_steps(m, row, col):
    eye = (row == col).astype(F32)
    d = jnp.where(row // INV_BASE == col // INV_BASE, m, 0.0)
    x = eye - d
    p = _mm(d, d)
    yield
    steps = int(math.log2(INV_BASE)) - 1
    for i in range(steps):
        x = x + _mm(x, p)
        if i + 1 < steps:
            p = _mm(p, p)
        yield
    size = INV_BASE
    while size < CHUNK:
        c = jnp.where((row // (2 * size) == col // (2 * size)) & (row // size != col // size), m, 0.0)
        xc = _mm(x, c)
        yield
        x = x - _mm(xc, x)
        yield
        size *= 2
    for _ in range(INV_NEWTON):
        r = eye - x - _mm3(m, x)
        yield
        x = x + _mm(x, r)
        yield
    return x


def _drain(gen):
    while True:
        try:
            next(gen)
        except StopIteration as stop:
            return stop.value


def _interleave(*gens):
    live = list(gens)
    while live:
        for g in list(live):
            try:
                next(g)
            except StopIteration:
                live.remove(g)


def _diag_blocks_tall(x):
    return jnp.concatenate([x[i * CHUNK:(i + 1) * CHUNK, i * CHUNK:(i + 1) * CHUNK] for i in range(GDN_GROUP)], axis=0)


def _tall_to_block_diag(t, same):
    return jnp.where(same, jnp.concatenate([t] * GDN_GROUP, axis=1), 0.0)


def _block_sum(same, x):
    xh, xl = _split(jnp.broadcast_to(x, (SUPER, 128)))
    ones = same.astype(BF16)
    d = functools.partial(jnp.dot, preferred_element_type=F32)
    return (d(ones, xh) + d(ones, xl))[:, 0:1]


def _aligned_rows(index, size):
    start = index * size
    return pl.ds(start if isinstance(start, int) else pl.multiple_of(start, size), size)


def _super_rows(i):
    return _aligned_rows(i, SUPER)


def _chunk_rows(n):
    return _aligned_rows(n, CHUNK)


def _gdn_super_steps(q, k, v, gcb, gr, head, tinv_tall=None, need_m=True):
    lane = lax.broadcasted_iota(jnp.int32, (SUPER, 128), 1)
    row = lax.broadcasted_iota(jnp.int32, (SUPER, SUPER), 0)
    col = lax.broadcasted_iota(jnp.int32, (SUPER, SUPER), 1)
    same = row // CHUNK == col // CHUNK
    beta = jnp.sum(jnp.where(lane == head, gcb, 0.0), axis=1, keepdims=True)
    gc = jnp.sum(jnp.where(lane == A_HEADS + head, gcb, 0.0), axis=1, keepdims=True)
    g_last = jnp.sum(jnp.where(col == (row // CHUNK) * CHUNK + (CHUNK - 1), gr, 0.0), axis=1, keepdims=True)
    gamma = jnp.exp(gc)
    decay = jnp.where(same & (row >= col), jnp.exp(jnp.minimum(gc - gr, 0.0)), 0.0)
    kb = k * beta
    m = jnp.where(same & (row > col), _mm_nt(kb, k) * decay, 0.0) if need_m else None
    p = jnp.where(same & (row >= col), _mm_nt(q, k) * decay, 0.0)
    yield
    if tinv_tall is None:
        tinv = yield from _inverse_steps(m, row, col)
    else:
        tinv = _tall_to_block_diag(tinv_tall, same)
    u = _mm(tinv, v * beta)
    w = _mm(tinv, kb * gamma)
    yield
    e_tail = jnp.exp(g_last - gc)
    return dict(beta=beta, gc=gc, g_last=g_last, gamma=gamma, decay=decay, kb=kb, m=m,
                tinv=tinv, u=u, w=w, p=p, e_tail=e_tail, row=row, col=col, lane=lane, same=same)


def _store_scan_operands(rows, q, k, t, u_scr, w_scr, p_scr, qg_scr, ke_scr, gl_scr):
    u_scr[rows, :] = t["u"]
    w_scr[rows, :] = _bf(t["w"])
    p_scr[rows, :] = _bf(_diag_blocks_tall(t["p"]))
    qg_scr[rows, :] = _bf(q * t["gamma"])
    ke_scr[rows, :] = _bf(k * t["e_tail"])
    gl_scr[rows, :] = jnp.broadcast_to(jnp.exp(t["g_last"]), (SUPER, 128))


def _gdn_fwd_body(q_ref, k_ref, v_ref, gc_ref, gr_ref, wq_ref, wk_ref, wv_ref, z_ref, gn_ref,
                  o_ref, tinv_ref, st_ref, og_ref, qo_ref, ko_ref, vo_ref, s_scr, *sets):
    head = pl.program_id(1)
    n_super = q_ref.shape[0] // SUPER
    all_sets = [sets[6 * i:6 * i + 6] for i in range(2 * GDN_WIDTH)]
    whole = pl.ds(0, SUPER)

    def conv_silu(x_ref, w_ref, i):
        rows = _super_rows(i)
        x, w = x_ref[rows, :], w_ref[...]
        halo = jnp.zeros((8, x.shape[1]), F32) if i == 0 else x_ref[pl.ds(i * SUPER - 8, 8), :]
        ext = jnp.concatenate([halo, x], axis=0)
        c = x * w[A_CONV - 1:A_CONV, :]
        for j in range(1, A_CONV):
            c = c + pltpu.roll(ext, j, 0)[8:, :] * w[A_CONV - 1 - j:A_CONV - j, :]
        return c * _sigmoid(c)

    def unit(a):
        return a * lax.rsqrt(jnp.sum(a * a, axis=1, keepdims=True) + EPS)

    def prepare_steps(i, dst):
        rows = _super_rows(i)
        q = unit(conv_silu(q_ref, wq_ref, i)) * A_DK ** -0.5
        k = unit(conv_silu(k_ref, wk_ref, i))
        v = conv_silu(v_ref, wv_ref, i)
        qo_ref[rows, :], ko_ref[rows, :], vo_ref[rows, :] = q, k, v
        t = yield from _gdn_super_steps(q, k, v, gc_ref[rows, :], gr_ref[i], head)
        tinv_ref[rows, :] = _diag_blocks_tall(t["tinv"])
        _store_scan_operands(whole, q, k, t, *dst)

    def scan_steps(i, src):
        u_scr, w_scr, p_scr, qg_scr, ke_scr, gl_scr = src
        for j in range(GDN_GROUP):
            n = i * GDN_GROUP + j
            local = pl.ds(j * CHUNK, CHUNK)
            s = s_scr[...]
            sb = _bf(s)
            st_ref[n] = sb
            ws = jnp.dot(w_scr[local, :], sb, preferred_element_type=F32)
            yield
            vb = _bf(u_scr[local, :] - ws)
            o = (jnp.dot(qg_scr[local, :], sb, preferred_element_type=F32)
                 + jnp.dot(p_scr[local, :], vb, preferred_element_type=F32))
            s_new = s * gl_scr[local, :][0:1, 0:1] + lax.dot_general(
                ke_scr[local, :], vb, (((0,), (0,)), ((), ())), preferred_element_type=F32)
            yield
            rows = _chunk_rows(n)
            o_ref[rows, :] = o
            s_scr[...] = s_new
            silu, _ = _silu_parts(z_ref[rows, :])
            r = lax.rsqrt(jnp.mean(o * o, axis=1, keepdims=True) + EPS)
            og_ref[rows, :] = ((o * r * gn_ref[0:1, :]) * silu).astype(og_ref.dtype)

    def scan_many(first, srcs):
        for j, src in enumerate(srcs):
            yield from scan_steps(first + j, src)

    groups = [all_sets[:GDN_WIDTH], all_sets[GDN_WIDTH:]]
    _interleave(*[prepare_steps(j, groups[0][j]) for j in range(GDN_WIDTH)])
    s_scr[...] = jnp.zeros_like(s_scr)
    for g in range(n_super // GDN_WIDTH):
        cur, nxt = groups[g % 2], groups[(g + 1) % 2]
        first = g * GDN_WIDTH
        following = [prepare_steps(first + GDN_WIDTH + j, nxt[j]) for j in range(GDN_WIDTH)
                     if first + GDN_WIDTH + j < n_super]
        _interleave(scan_many(first, cur), *following)


def _gdn_in_specs(s_len, n_super, from_proj):
    k_col, v_col = (A_K_COL, A_V_COL) if from_proj else (0, 0)
    return [
        pl.BlockSpec((s_len, A_DK), lambda b, h: (b, h)),
        pl.BlockSpec((s_len, A_DK), lambda b, h: (b, k_col + h)),
        pl.BlockSpec((s_len, A_DV), lambda b, h: (b, v_col + h)),
        pl.BlockSpec((s_len, 128), lambda b, h: (b, 0)),
        pl.BlockSpec((None, None, n_super, 1, SUPER), lambda b, h: (b, A_HEADS + h, 0, 0, 0)),
    ]


def _gdn_scan_scratch(s_len):
    return [pltpu.VMEM((A_DK, A_DV), F32), pltpu.VMEM((s_len, A_DV), F32),
            pltpu.VMEM((s_len, A_DK), BF16), pltpu.VMEM((s_len, CHUNK), BF16),
            pltpu.VMEM((s_len, A_DK), BF16), pltpu.VMEM((s_len, A_DK), BF16),
            pltpu.VMEM((s_len, 128), F32)]


def gdn_fwd(proj_a, conv_w8, gates_col, gates_row, norm_g8, n_seq):
    t_rows = proj_a.shape[0]
    s_len = t_rows // n_seq
    n_chunks = s_len // CHUNK
    qk_spec = pl.BlockSpec((s_len, A_DK), lambda b, h: (b, h))
    v_spec = pl.BlockSpec((s_len, A_DV), lambda b, h: (b, h))
    return pl.pallas_call(
        _gdn_fwd_body, name="gdn_fwd", grid=(n_seq, A_HEADS),
        in_specs=_gdn_in_specs(s_len, s_len // SUPER, True) + [
            pl.BlockSpec((8, A_DK), lambda b, h: (0, h)),
            pl.BlockSpec((8, A_DK), lambda b, h: (0, A_K_COL + h)),
            pl.BlockSpec((8, A_DV), lambda b, h: (0, A_V_COL + h)),
            pl.BlockSpec((s_len, A_DV), lambda b, h: (b, A_Z_COL + h)),
            pl.BlockSpec((8, A_DV), lambda b, h: (0, 0)),
        ],
        out_specs=[
            v_spec,
            pl.BlockSpec((s_len, CHUNK), lambda b, h: (b * A_HEADS + h, 0)),
            pl.BlockSpec((None, n_chunks, A_DK, A_DV), lambda b, h: (b * A_HEADS + h, 0, 0, 0)),
            v_spec, qk_spec, qk_spec, v_spec,
        ],
        out_shape=[
            jax.ShapeDtypeStruct((t_rows, A_VW), F32),
            jax.ShapeDtypeStruct((n_seq * A_HEADS * s_len, CHUNK), F32),
            jax.ShapeDtypeStruct((n_seq * A_HEADS, n_chunks, A_DK, A_DV), BF16),
            jax.ShapeDtypeStruct((t_rows, A_VW), BF16),
            jax.ShapeDtypeStruct((t_rows, A_QK), F32),
            jax.ShapeDtypeStruct((t_rows, A_QK), F32),
            jax.ShapeDtypeStruct((t_rows, A_VW), F32),
        ],
        scratch_shapes=_gdn_scan_scratch(SUPER) + (2 * GDN_WIDTH - 1) * _gdn_scan_scratch(SUPER)[1:],
        compiler_params=_params(("arbitrary", "arbitrary")),
    )(proj_a, proj_a, proj_a, gates_col, gates_row, conv_w8, conv_w8, conv_w8, proj_a, norm_g8)


def _gdn_bwd_body(q_ref, k_ref, v_ref, gc_ref, gr_ref, tinv_ref, st_ref, dog_ref, oa_ref, z_ref, gn_ref,
                  xq_ref, xk_ref, xv_ref, wq_ref, wk_ref, wv_ref,
                  dxq_ref, dxk_ref, dxv_ref, dgc_ref, dz_ref, dgn_ref, dwq_ref, dwk_ref, dwv_ref,
                  ds_scr, cq_scr, ck_scr, cv_scr, *sets):
    head = pl.program_id(1)
    n_super = q_ref.shape[0] // SUPER
    ops = (sets[0:7], sets[7:14])
    res = (sets[14:21], sets[21:28])
    whole = pl.ds(0, SUPER)
    tn = functools.partial(lax.dot_general, dimension_numbers=(((0,), (0,)), ((), ())), preferred_element_type=F32)
    nt = functools.partial(lax.dot_general, dimension_numbers=(((1,), (1,)), ((), ())), preferred_element_type=F32)

    @pl.when(head == 0)
    def _():
        dgc_ref[...] = jnp.zeros_like(dgc_ref)

    @pl.when((head == 0) & (pl.program_id(0) == 0))
    def _():
        dgn_ref[...] = jnp.zeros_like(dgn_ref)

    carry = (cq_scr, ck_scr, cv_scr)
    for ref in carry + (dwq_ref, dwk_ref, dwv_ref):
        ref[...] = jnp.zeros_like(ref)

    def common_steps(i, need_m=True):
        rows = _super_rows(i)
        q, k, v = q_ref[rows, :], k_ref[rows, :], v_ref[rows, :]
        t = yield from _gdn_super_steps(q, k, v, gc_ref[rows, :], gr_ref[i], head, tinv_tall=tinv_ref[rows, :],
                                        need_m=need_m)
        return rows, q, k, v, t

    def stage_p(i, parity):
        rows, q, k, _, t = yield from common_steps(i, need_m=False)
        _store_scan_operands(whole, q, k, t, *ops[parity][:6])
        o, d_og, gain = oa_ref[rows, :], dog_ref[rows, :], gn_ref[0:1, :]
        r = lax.rsqrt(jnp.mean(o * o, axis=1, keepdims=True) + EPS)
        silu, dsilu = _silu_parts(z_ref[rows, :])
        xr = o * r
        d_on = d_og * silu
        dz_ref[rows, :] = (d_og * (xr * gain) * dsilu).astype(dz_ref.dtype)
        u = d_on * gain
        ops[parity][6][...] = r * u - xr * (r * r) * jnp.mean(u * o, axis=1, keepdims=True)
        dgn_ref[0:1, :] += jnp.sum(d_on * xr, axis=0, keepdims=True)

    def stage_s(i, parity):
        u_scr, w_scr, p_scr, qg_scr, ke_scr, gl_scr, do_scr = ops[parity]
        vn_scr, dvn_scr, dqg_scr, dw_scr, dkt_scr, sds_scr, dof_scr = res[parity]
        for j in reversed(range(GDN_GROUP)):
            n = i * GDN_GROUP + j
            local = pl.ds(j * CHUNK, CHUNK)
            ds_next = ds_scr[...]
            dsb = _bf(ds_next)
            sb = st_ref[n]
            s = sb.astype(F32)
            d_o = do_scr[local, :]
            dof_scr[local, :] = d_o
            d_ob = _bf(d_o)
            w_s = jnp.dot(w_scr[local, :], sb, preferred_element_type=F32)
            d_vn = tn(p_scr[local, :], d_ob) + jnp.dot(ke_scr[local, :], dsb, preferred_element_type=F32)
            d_qg = nt(d_ob, sb)
            qg_do = tn(qg_scr[local, :], d_ob)
            yield
            v_new = u_scr[local, :] - w_s
            d_vnb = _bf(d_vn)
            d_w = -nt(d_vnb, sb)
            d_kt = nt(_bf(v_new), dsb)
            w_dvn = tn(w_scr[local, :], d_vnb)
            yield
            vn_scr[local, :] = v_new
            dvn_scr[local, :] = d_vn
            dqg_scr[local, :] = d_qg
            dw_scr[local, :] = d_w
            dkt_scr[local, :] = d_kt
            sds = jnp.sum(jnp.sum(s * ds_next, axis=1, keepdims=True), axis=0, keepdims=True)
            sds_scr[local, :] = jnp.broadcast_to(sds, (CHUNK, 128))
            ds_scr[...] = qg_do + gl_scr[local, :][0:1, 0:1] * ds_next - w_dvn

    def conv_bwd(i, rows, x_ref, w_ref, dy, norm_scale, dx_ref, dw_ref, dc_above):
        x, w = x_ref[rows, :], w_ref[...]
        above = x_ref[pl.ds(pl.multiple_of(jnp.maximum(i * SUPER - 8, 0), 8), 8), :]
        ext = jnp.concatenate([jnp.where(i > 0, above, 0.0), x], axis=0)
        c = x * w[A_CONV - 1:A_CONV, :]
        for j in range(1, A_CONV):
            c = c + pltpu.roll(ext, j, 0)[8:, :] * w[A_CONV - 1 - j:A_CONV - j, :]
        sig = _sigmoid(c)
        a = c * sig
        if norm_scale is None:
            da = dy
        else:
            rn = lax.rsqrt(jnp.sum(a * a, axis=1, keepdims=True) + EPS)
            da = norm_scale * (rn * dy - a * (rn * rn * rn) * jnp.sum(dy * a, axis=1, keepdims=True))
        dc = da * (sig * (1.0 + c * (1.0 - sig)))
        ext_dc = jnp.concatenate([dc, dc_above[...]], axis=0)
        dc_above[...] = dc[0:8, :]
        dx = dc * w[A_CONV - 1:A_CONV, :]
        dw_ref[A_CONV - 1:A_CONV, :] += jnp.sum(dc * x, axis=0, keepdims=True)
        for j in range(1, A_CONV):
            dcs = pltpu.roll(ext_dc, SUPER + 8 - j, 0)[:SUPER, :]
            dx = dx + dcs * w[A_CONV - 1 - j:A_CONV - j, :]
            dw_ref[A_CONV - 1 - j:A_CONV - j, :] += jnp.sum(dcs * x, axis=0, keepdims=True)
        dx_ref[rows, :] = dx.astype(dx_ref.dtype)

    def stage_f(i, parity):
        vn_scr, dvn_scr, dqg_scr, dw_scr, dkt_scr, sds_scr, dof_scr = res[parity]
        rows, q, k, v, t = yield from common_steps(i)
        beta, gamma, decay, kb, e_tail = t["beta"], t["gamma"], t["decay"], t["kb"], t["e_tail"]
        row, col, lane, same = t["row"], t["col"], t["lane"], t["same"]
        d_o = dof_scr[...]
        v_new, d_vn = vn_scr[...], dvn_scr[...]
        d_qg, d_w, d_kt = dqg_scr[...], dw_scr[...], dkt_scr[...]
        gamma_last = jnp.exp(t["g_last"])

        d_p = jnp.where(same & (row >= col), _mm_nt(d_o, v_new), 0.0)
        d_ru = _mm_tn(t["tinv"], d_vn)
        d_rw = _mm_tn(t["tinv"], d_w)
        yield
        d_m = jnp.where(same & (row > col), -(_mm_nt(d_ru, t["u"]) + _mm_nt(d_rw, t["w"])), 0.0)
        yield

        x_p = d_p * decay
        y_m = d_m * decay
        d_kb = _mm(y_m, k) + d_rw * gamma
        d_q = _mm(x_p, k) + d_qg * gamma
        d_k = _mm_tn(x_p, q) + _mm_tn(y_m, kb) + d_kb * beta + d_kt * e_tail
        d_v = d_ru * beta
        conv_bwd(i, rows, xq_ref, wq_ref, d_q, A_DK ** -0.5, dxq_ref, dwq_ref, carry[0])
        conv_bwd(i, rows, xk_ref, wk_ref, d_k, 1.0, dxk_ref, dwk_ref, carry[1])
        conv_bwd(i, rows, xv_ref, wv_ref, d_v, None, dxv_ref, dwv_ref, carry[2])

        d_beta = (jnp.sum(d_ru * v, axis=1, keepdims=True)
                  + jnp.sum(d_kb * k, axis=1, keepdims=True))
        z = d_p * t["p"] + d_m * t["m"]
        eps_tail = jnp.sum(d_kt * k, axis=1, keepdims=True) * e_tail
        d_gc = (jnp.sum(z, axis=1, keepdims=True) - _colsum_as_col(z)
                + jnp.sum(d_qg * q, axis=1, keepdims=True) * gamma
                + jnp.sum(d_rw * kb, axis=1, keepdims=True) * gamma
                - eps_tail)
        d_glast = _block_sum(same, eps_tail) + gamma_last * sds_scr[...][:, 0:1]
        yield
        rcol = lax.broadcasted_iota(jnp.int32, (SUPER, 1), 0)
        d_gc = d_gc + jnp.where(rcol % CHUNK == CHUNK - 1, d_glast, 0.0)
        dgc_ref[rows, :] += (jnp.where(lane == head, d_beta, 0.0)
                             + jnp.where(lane == A_HEADS + head, d_gc, 0.0))

    last = n_super - 1
    _drain(stage_p(last, 1))
    ds_scr[...] = jnp.zeros_like(ds_scr)
    _interleave(stage_s(last, 1), stage_p(last - 1, 0))

    def pair(k, carry):
        i = last - 1 - 2 * k
        _interleave(stage_s(i, 0), stage_f(i + 1, 1), stage_p(i - 1, 1))
        _interleave(stage_s(i - 1, 1), stage_f(i, 0), stage_p(i - 2, 0))
        return carry

    lax.fori_loop(0, n_super // 2 - 1, pair, 0)
    _interleave(stage_s(0, 0), stage_f(1, 1))
    _drain(stage_f(0, 0))


def gdn_bwd(q, k, v, gates_col, gates_row, tinv, states, d_og, o, proj_a, conv_w8, norm_g8, n_seq):
    t_rows = q.shape[0]
    s_len = t_rows // n_seq
    n_chunks = s_len // CHUNK
    qk_spec = pl.BlockSpec((s_len, A_DK), lambda b, h: (b, h))
    v_spec = pl.BlockSpec((s_len, A_DV), lambda b, h: (b, h))
    gate_spec = pl.BlockSpec((s_len, 128), lambda b, h: (b, 0))
    gain_spec = pl.BlockSpec((8, A_DV), lambda b, h: (0, 0))
    dw_qk_spec = pl.BlockSpec((None, 8, A_DK), lambda b, h: (b, 0, h))
    dw_v_spec = pl.BlockSpec((None, 8, A_DV), lambda b, h: (b, 0, h))
    ops_set = _gdn_scan_scratch(SUPER)[1:] + [pltpu.VMEM((SUPER, A_DV), F32)]
    res_set = [pltpu.VMEM((SUPER, A_DV), F32), pltpu.VMEM((SUPER, A_DV), F32),
               pltpu.VMEM((SUPER, A_DK), F32), pltpu.VMEM((SUPER, A_DK), F32),
               pltpu.VMEM((SUPER, A_DK), F32), pltpu.VMEM((SUPER, 128), F32), pltpu.VMEM((SUPER, A_DV), F32)]
    return pl.pallas_call(
        _gdn_bwd_body, name="gdn_bwd", grid=(n_seq, A_HEADS),
        in_specs=_gdn_in_specs(s_len, s_len // SUPER, False) + [
            pl.BlockSpec((s_len, CHUNK), lambda b, h: (b * A_HEADS + h, 0)),
            pl.BlockSpec((None, n_chunks, A_DK, A_DV), lambda b, h: (b * A_HEADS + h, 0, 0, 0)),
            v_spec, v_spec,
            pl.BlockSpec((s_len, A_DV), lambda b, h: (b, A_Z_COL + h)),
            gain_spec,
            pl.BlockSpec((s_len, A_DK), lambda b, h: (b, h)),
            pl.BlockSpec((s_len, A_DK), lambda b, h: (b, A_K_COL + h)),
            pl.BlockSpec((s_len, A_DV), lambda b, h: (b, A_V_COL + h)),
            pl.BlockSpec((8, A_DK), lambda b, h: (0, h)),
            pl.BlockSpec((8, A_DK), lambda b, h: (0, A_K_COL + h)),
            pl.BlockSpec((8, A_DV), lambda b, h: (0, A_V_COL + h)),
        ],
        out_specs=[qk_spec, qk_spec, v_spec, gate_spec, v_spec, gain_spec, dw_qk_spec, dw_qk_spec, dw_v_spec],
        out_shape=[
            jax.ShapeDtypeStruct((t_rows, A_QK), BF16),
            jax.ShapeDtypeStruct((t_rows, A_QK), BF16),
            jax.ShapeDtypeStruct((t_rows, A_VW), BF16),
            jax.ShapeDtypeStruct((t_rows, 128), F32),
            jax.ShapeDtypeStruct((t_rows, A_VW), BF16),
            jax.ShapeDtypeStruct((8, A_DV), F32),
            jax.ShapeDtypeStruct((n_seq, 8, A_QK), F32),
            jax.ShapeDtypeStruct((n_seq, 8, A_QK), F32),
            jax.ShapeDtypeStruct((n_seq, 8, A_VW), F32),
        ],
        scratch_shapes=(_gdn_scan_scratch(SUPER)[:1]
                        + [pltpu.VMEM((8, A_DK), F32), pltpu.VMEM((8, A_DK), F32), pltpu.VMEM((8, A_DV), F32)]
                        + 2 * ops_set + 2 * res_set),
        compiler_params=_params(("arbitrary", "arbitrary")),
    )(q, k, v, gates_col, gates_row, tinv, states, d_og, o, proj_a, norm_g8,
      proj_a, proj_a, proj_a, conv_w8, conv_w8, conv_w8)


GATE_TILE = 512


def _softplus(y):
    return jnp.maximum(y, 0.0) + jnp.log1p(jnp.exp(-jnp.abs(y)))


def _gate_values(x, prm):
    beta = _sigmoid(x)
    y = x + prm[1:2, :]
    neg_a = -jnp.exp(prm[0:1, :])
    g = neg_a * _softplus(y)
    return beta, y, neg_a, g


def _a_gates_fwd_body(x_ref, prm_ref, gc_ref, gr_ref):
    x = x_ref[...]
    tm = x.shape[0]
    beta, _, _, g = _gate_values(x, prm_ref[...])
    in_chunk = lax.broadcasted_iota(jnp.int32, (tm, 1), 0) % CHUNK
    s = 1
    while s < CHUNK:
        g = g + jnp.where(in_chunk >= s, pltpu.roll(g, s, 0), 0.0)
        s *= 2
    lane = lax.broadcasted_iota(jnp.int32, x.shape, 1)
    out = jnp.where(lane < A_HEADS, beta, jnp.where(lane < 2 * A_HEADS, g, 0.0))
    gc_ref[...] = out
    gr_ref[...] = out.T[0:2 * A_HEADS, :]


def a_gates_fwd(proj_a, prm, n_seq):
    t_rows = proj_a.shape[0]
    s_len = t_rows // n_seq
    tm = min(GATE_TILE, s_len)
    n_t = s_len // tm
    return pl.pallas_call(
        _a_gates_fwd_body, name="a_gates_fwd", grid=(n_seq, n_t),
        in_specs=[pl.BlockSpec((tm, 128), lambda b, i: (b * n_t + i, A_GATE_COL)),
                  pl.BlockSpec((8, 128), lambda b, i: (0, 0))],
        out_specs=[pl.BlockSpec((tm, 128), lambda b, i: (b * n_t + i, 0)),
                   pl.BlockSpec((None, 2 * A_HEADS, tm), lambda b, i: (b, 0, i))],
        out_shape=[jax.ShapeDtypeStruct((t_rows, 128), F32),
                   jax.ShapeDtypeStruct((n_seq, 2 * A_HEADS, s_len), F32)],
        compiler_params=_params(("arbitrary", "arbitrary")),
    )(proj_a, prm)


def _a_gates_bwd_body(x_ref, prm_ref, dgc_ref, dx_ref, dprm_ref):
    first = (pl.program_id(0) == 0) & (pl.program_id(1) == 0)
    x = x_ref[...]
    tm = x.shape[0]
    beta, y, neg_a, g = _gate_values(x, prm_ref[...])
    d = dgc_ref[...]
    in_chunk = lax.broadcasted_iota(jnp.int32, (tm, 1), 0) % CHUNK
    dg = d
    s = 1
    while s < CHUNK:
        dg = dg + jnp.where(in_chunk < CHUNK - s, pltpu.roll(dg, tm - s, 0), 0.0)
        s *= 2
    lane = lax.broadcasted_iota(jnp.int32, x.shape, 1)
    is_decay = (lane >= A_HEADS) & (lane < 2 * A_HEADS)
    d_alogit = jnp.where(is_decay, dg * neg_a * _sigmoid(y), 0.0)
    dx_ref[...] = jnp.where(lane < A_HEADS, d * beta * (1.0 - beta), d_alogit).astype(dx_ref.dtype)

    @pl.when(first)
    def _():
        dprm_ref[...] = jnp.zeros_like(dprm_ref)

    dprm_ref[0:1, :] += jnp.sum(jnp.where(is_decay, dg * g, 0.0), axis=0, keepdims=True)
    dprm_ref[1:2, :] += jnp.sum(d_alogit, axis=0, keepdims=True)


def a_gates_bwd(proj_a, prm, dgates_col, n_seq):
    t_rows = proj_a.shape[0]
    s_len = t_rows // n_seq
    tm = min(GATE_TILE, s_len)
    n_t = s_len // tm
    return pl.pallas_call(
        _a_gates_bwd_body, name="a_gates_bwd", grid=(n_seq, n_t),
        in_specs=[pl.BlockSpec((tm, 128), lambda b, i: (b * n_t + i, A_GATE_COL)),
                  pl.BlockSpec((8, 128), lambda b, i: (0, 0)),
                  pl.BlockSpec((tm, 128), lambda b, i: (b * n_t + i, 0))],
        out_specs=[pl.BlockSpec((tm, 128), lambda b, i: (b * n_t + i, 0)),
                   pl.BlockSpec((8, 128), lambda b, i: (0, 0))],
        out_shape=[jax.ShapeDtypeStruct((t_rows, 128), BF16),
                   jax.ShapeDtypeStruct((8, 128), F32)],
        compiler_params=_params(("arbitrary", "arbitrary")),
    )(proj_a, prm, dgates_col)


ROW_TILE = 1024
A_Z_COL = (2 * A_QK + A_VW) // A_DV


def _silu_parts(z):
    sig = _sigmoid(z)
    return z * sig, sig * (1.0 + z * (1.0 - sig))


NEG_BIG = -1e30
ATT_SCALE = B_DH ** -0.5


def _swap_rope_halves(x):
    src = lax.broadcasted_iota(jnp.int32, (B_DH, B_DH), 0)
    dst = lax.broadcasted_iota(jnp.int32, (B_DH, B_DH), 1)
    pick = ((dst < ROPE_HALF) & (src == dst + ROPE_HALF)) | (
        (dst >= ROPE_HALF) & (dst < ROPE_DIMS) & (src == dst - ROPE_HALF))
    return jnp.dot(_bf(x), pick.astype(BF16), preferred_element_type=F32)


def _norm_rope(x, gain, cos_t, sin_t):
    r = lax.rsqrt(jnp.mean(x * x, axis=1, keepdims=True) + EPS)
    xn = x * r * gain
    return xn * cos_t + _swap_rope_halves(xn) * sin_t, r


def _norm_rope_bwd(x, r, gain, cos_t, sin_t, dy):
    d_xn = dy * cos_t + _swap_rope_halves(dy * sin_t)
    xr = x * r
    u = d_xn * gain
    dx = r * u - xr * (r * r) * jnp.mean(u * x, axis=1, keepdims=True)
    return dx, jnp.sum(d_xn * xr, axis=0, keepdims=True)


def _stream_rows(idx, dilation, s_len):
    nb = s_len // dilation // B_BLOCK
    r = idx // nb
    m = idx % nb
    cur = r + m * (B_BLOCK * dilation)
    prev = r + jnp.maximum(m - 1, 0) * (B_BLOCK * dilation)
    return cur, prev, m > 0


def _rows(start, dilation):
    if dilation == 1:
        return pl.ds(start, B_BLOCK)
    return pl.ds(start, B_BLOCK, stride=dilation)


ATT_UNROLL = 16


def _band_mask(has_prev):
    qi = lax.broadcasted_iota(jnp.int32, (B_BLOCK, 2 * B_BLOCK), 0)
    kj = lax.broadcasted_iota(jnp.int32, (B_BLOCK, 2 * B_BLOCK), 1)
    return ((kj < B_BLOCK) & (kj >= qi) & has_prev) | ((kj >= B_BLOCK) & (kj - B_BLOCK <= qi))


def _attn_fwd_body(qkv_ref, z_ref, cos_ref, sin_ref, gain_ref, og_ref, o_ref, lse_ref,
                   qn_scr, kn_scr, og_scr, lg_scr):
    head, grp = pl.program_id(1), pl.program_id(2)
    s_len = z_ref.shape[0]
    n_blocks = s_len // B_BLOCK
    cos_t, sin_t = cos_ref[...], sin_ref[...]

    for gi, dil in enumerate(B_DILATIONS):
        @pl.when(grp == gi)
        def _(gi=gi, dil=dil):
            qn_scr[...], _ = _norm_rope(qkv_ref[0], gain_ref[gi:gi + 1, :], cos_t, sin_t)
            kn_scr[...], _ = _norm_rope(qkv_ref[1], gain_ref[B_GROUPS + gi:B_GROUPS + gi + 1, :], cos_t, sin_t)

            ones = jnp.ones((2 * B_BLOCK, B_DH), BF16)

            def blocks(it, carry):
                scored = []
                for j in range(ATT_UNROLL):
                    cur, prev, has_prev = _stream_rows(it * ATT_UNROLL + j, dil, s_len)
                    rc, rp = _rows(cur, dil), _rows(prev, dil)
                    k2 = jnp.concatenate([kn_scr[rp, :], kn_scr[rc, :]], axis=0)
                    scored.append((rc, rp, has_prev, _mm_nt(qn_scr[rc, :], k2) * ATT_SCALE))
                summed = []
                for rc, rp, has_prev, s in scored:
                    s = jnp.where(_band_mask(has_prev), s, NEG_BIG)
                    mx = jnp.max(s, axis=1, keepdims=True)
                    v2 = jnp.concatenate([qkv_ref.at[2][rp, :], qkv_ref.at[2][rc, :]], axis=0)
                    acc = jnp.dot(_bf(jnp.exp(s - mx)), jnp.concatenate([_bf(v2), ones], axis=1),
                                  preferred_element_type=F32)
                    summed.append((rc, mx, acc))
                for rc, mx, acc in summed:
                    den = acc[:, B_DH:B_DH + 1]
                    og_scr.at[gi][rc, :] = acc[:, :B_DH] * pl.reciprocal(den, approx=True)
                    lg_scr.at[gi][rc, :] = jnp.broadcast_to(mx + jnp.log(den), (B_BLOCK, B_DH))
                return carry

            lax.fori_loop(0, n_blocks // ATT_UNROLL, blocks, 0)

    @pl.when(grp == B_GROUPS - 1)
    def _():
        l0, l1, l2 = lg_scr[0], lg_scr[1], lg_scr[2]
        mx = jnp.maximum(jnp.maximum(l0, l1), l2)
        w0, w1, w2 = jnp.exp(l0 - mx), jnp.exp(l1 - mx), jnp.exp(l2 - mx)
        den = w0 + w1 + w2
        o = (w0 * og_scr[0] + w1 * og_scr[1] + w2 * og_scr[2]) / den
        silu, _ = _silu_parts(z_ref[...])
        o_ref[...] = o
        og_ref[...] = (o * silu).astype(og_ref.dtype)
        @pl.when(head == 0)
        def _():
            lse_ref[...] = jnp.zeros_like(lse_ref)

        lane = lax.broadcasted_iota(jnp.int32, o.shape, 1)
        lse_ref[...] = jnp.where(lane == head, mx + jnp.log(den), lse_ref[...])


def attn_fwd(proj_b, cos_t, sin_t, gains8, n_seq):
    t_rows = proj_b.shape[1]
    s_len = t_rows // n_seq
    head_blk = pl.BlockSpec((s_len, B_DH), lambda b, h, g: (b, h))
    seq_blk = pl.BlockSpec((s_len, 128), lambda b, h, g: (b, 0))
    return pl.pallas_call(
        _attn_fwd_body, name="attn_fwd", grid=(n_seq, B_HEADS, B_GROUPS),
        in_specs=[
            pl.BlockSpec((3, s_len, B_DH), lambda b, h, g: (g, b, h)),
            pl.BlockSpec((None, s_len, B_DH), lambda b, h, g: (B_PIECES - 1, b, h)),
            seq_blk, seq_blk,
            pl.BlockSpec((8, 128), lambda b, h, g: (0, 0)),
        ],
        out_specs=[head_blk, head_blk, seq_blk],
        out_shape=[jax.ShapeDtypeStruct((t_rows, B_W), BF16),
                   jax.ShapeDtypeStruct((t_rows, B_W), F32),
                   jax.ShapeDtypeStruct((t_rows, 128), F32)],
        scratch_shapes=[pltpu.VMEM((s_len, B_DH), F32), pltpu.VMEM((s_len, B_DH), F32),
                        pltpu.VMEM((B_GROUPS, s_len, B_DH), F32), pltpu.VMEM((B_GROUPS, s_len, B_DH), F32)],
        compiler_params=_params(("arbitrary", "arbitrary", "arbitrary")),
    )(proj_b, proj_b, cos_t, sin_t, gains8)


def _attn_bwd_body(qkv_ref, z_ref, cos_ref, sin_ref, gain_ref, dog_ref, o_ref, lse_ref,
                   dqkv_ref, dz_ref, dgain_ref,
                   qn_scr, kn_scr, dqn_scr, dkn_scr, do_scr, dl_scr, ls_scr, dv_scr):
    head, grp = pl.program_id(1), pl.program_id(2)
    first = (pl.program_id(0) == 0) & (head == 0) & (grp == 0)
    s_len = z_ref.shape[0]
    n_blocks = s_len // B_BLOCK
    cos_t, sin_t = cos_ref[...], sin_ref[...]

    @pl.when(first)
    def _():
        dgain_ref[...] = jnp.zeros_like(dgain_ref)

    @pl.when(grp == 0)
    def _():
        d_og, o = dog_ref[...], o_ref[...]
        silu, dsilu = _silu_parts(z_ref[...])
        d_o = d_og * silu
        dz_ref[...] = (d_og * o * dsilu).astype(dz_ref.dtype)
        do_scr[...] = d_o
        dl_scr[...] = jnp.broadcast_to(jnp.sum(d_o * o, axis=1, keepdims=True), o.shape)
        lane = lax.broadcasted_iota(jnp.int32, o.shape, 1)
        ls_scr[...] = jnp.broadcast_to(
            jnp.sum(jnp.where(lane == head, lse_ref[...], 0.0), axis=1, keepdims=True), o.shape)

    for gi, dil in enumerate(B_DILATIONS):
        @pl.when(grp == gi)
        def _(gi=gi, dil=dil):
            q_raw, k_raw = qkv_ref[0], qkv_ref[1]
            gq = gain_ref[gi:gi + 1, :]
            gk = gain_ref[B_GROUPS + gi:B_GROUPS + gi + 1, :]
            qn_scr[...], rq = _norm_rope(q_raw, gq, cos_t, sin_t)
            kn_scr[...], rk = _norm_rope(k_raw, gk, cos_t, sin_t)
            def blocks(it, carry):
                scored = []
                for j in range(ATT_UNROLL):
                    cur, prev, has_prev = _stream_rows(it * ATT_UNROLL + j, dil, s_len)
                    rc, rp = _rows(cur, dil), _rows(prev, dil)
                    qb, d_ob = _bf(qn_scr[rc, :]), _bf(do_scr[rc, :])
                    k2 = _bf(jnp.concatenate([kn_scr[rp, :], kn_scr[rc, :]], axis=0))
                    v2 = _bf(jnp.concatenate([qkv_ref.at[2][rp, :], qkv_ref.at[2][rc, :]], axis=0))
                    scored.append((rc, rp, has_prev, qb, d_ob, k2,
                                   _mm_nt(qb, k2) * ATT_SCALE, _mm_nt(d_ob, v2)))
                grads = []
                for rc, rp, has_prev, qb, d_ob, k2, s, d_p in scored:
                    p = jnp.exp(jnp.where(_band_mask(has_prev), s - ls_scr[rc, :][:, 0:1], NEG_BIG))
                    ds = _bf(p * (d_p - dl_scr[rc, :][:, 0:1]))
                    grads.append((rc, rp, has_prev,
                                  _mm(ds, k2) * ATT_SCALE, _mm_tn(ds, qb) * ATT_SCALE, _mm_tn(_bf(p), d_ob)))
                for j, (rc, rp, has_prev, dq, dk2, dv2) in enumerate(grads):
                    dqn_scr[rc, :] = dq
                    if j == 0:
                        @pl.when(has_prev)
                        def _():
                            dkn_scr[rp, :] += dk2[:B_BLOCK]
                            dv_scr[rp, :] += dv2[:B_BLOCK]
                    if j + 1 < ATT_UNROLL:
                        dkn_scr[rc, :] = dk2[B_BLOCK:] + grads[j + 1][4][:B_BLOCK]
                        dv_scr[rc, :] = dv2[B_BLOCK:] + grads[j + 1][5][:B_BLOCK]
                    else:
                        dkn_scr[rc, :] = dk2[B_BLOCK:]
                        dv_scr[rc, :] = dv2[B_BLOCK:]
                return carry

            lax.fori_loop(0, n_blocks // ATT_UNROLL, blocks, 0)
            dq, dgq = _norm_rope_bwd(q_raw, rq, gq, cos_t, sin_t, dqn_scr[...])
            dk, dgk = _norm_rope_bwd(k_raw, rk, gk, cos_t, sin_t, dkn_scr[...])
            dqkv_ref[0] = dq.astype(dqkv_ref.dtype)
            dqkv_ref[1] = dk.astype(dqkv_ref.dtype)
            dqkv_ref[2] = dv_scr[...].astype(dqkv_ref.dtype)
            dgain_ref[gi:gi + 1, :] += dgq
            dgain_ref[B_GROUPS + gi:B_GROUPS + gi + 1, :] += dgk


def attn_bwd(proj_b, cos_t, sin_t, gains8, d_og, o, lse, n_seq):
    t_rows = proj_b.shape[1]
    s_len = t_rows // n_seq
    head_blk = pl.BlockSpec((s_len, B_DH), lambda b, h, g: (b, h))
    seq_blk = pl.BlockSpec((s_len, 128), lambda b, h, g: (b, 0))
    grp_blk = pl.BlockSpec((3, s_len, B_DH), lambda b, h, g: (g, b, h))
    gain_blk = pl.BlockSpec((8, 128), lambda b, h, g: (0, 0))
    return pl.pallas_call(
        _attn_bwd_body, name="attn_bwd", grid=(n_seq, B_HEADS, B_GROUPS),
        in_specs=[
            grp_blk,
            pl.BlockSpec((None, s_len, B_DH), lambda b, h, g: (B_PIECES - 1, b, h)),
            seq_blk, seq_blk, gain_blk, head_blk, head_blk, seq_blk,
        ],
        out_specs=[grp_blk, head_blk, gain_blk],
        out_shape=[jax.ShapeDtypeStruct((3 * B_GROUPS, t_rows, B_W), BF16),
                   jax.ShapeDtypeStruct((t_rows, B_W), BF16),
                   jax.ShapeDtypeStruct((8, 128), F32)],
        scratch_shapes=[pltpu.VMEM((s_len, B_DH), F32) for _ in range(8)],
        compiler_params=_params(("arbitrary", "arbitrary", "arbitrary")),
    )(proj_b, proj_b, cos_t, sin_t, gains8, d_og, o, lse)


def rope_tables(positions):
    inv_freq = ROPE_THETA ** (-jnp.arange(0, ROPE_DIMS, 2, dtype=F32) / ROPE_DIMS)
    ang = positions.astype(F32)[:, None] * inv_freq
    cos, sin = jnp.cos(ang), jnp.sin(ang)
    t_rows = positions.shape[0]
    rest = B_DH - ROPE_DIMS
    cos_t = jnp.concatenate([cos, cos, jnp.ones((t_rows, rest), F32)], axis=1)
    sin_t = jnp.concatenate([-sin, sin, jnp.zeros((t_rows, rest), F32)], axis=1)
    return cos_t, sin_t


def _rms_fwd_body(x_ref, g_ref, *rest, layer):
    h_ref, ht_ref = rest[-2:]
    x = x_ref[...]
    r = lax.rsqrt(jnp.mean(x * x, axis=1, keepdims=True) + EPS)
    h = x * r * g_ref[layer:layer + 1, :]
    h_ref[...] = h.astype(h_ref.dtype)
    ht_ref[...] = h.T.astype(ht_ref.dtype)


def rms_fwd(x, gains8, layer, after=None):
    t_rows, d = x.shape
    tm = min(ROW_TILE, t_rows)
    in_specs = [pl.BlockSpec((tm, d), lambda i: (i, 0)), pl.BlockSpec((8, d), lambda i: (0, 0))]
    args = [x, gains8]
    if after is not None:
        in_specs.append(HBM_SPEC)
        args.append(after)
    return pl.pallas_call(
        functools.partial(_rms_fwd_body, layer=layer), name=f"rms_fwd_{layer}", grid=(t_rows // tm,),
        in_specs=in_specs,
        out_specs=[pl.BlockSpec((tm, d), lambda i: (i, 0)), pl.BlockSpec((d, tm), lambda i: (0, i))],
        out_shape=[jax.ShapeDtypeStruct((t_rows, d), BF16), jax.ShapeDtypeStruct((d, t_rows), BF16)],
        compiler_params=_params(("arbitrary",)),
    )(*args)


def _rms_bwd_body(x_ref, g_ref, dh_ref, res_ref, dx_ref, dg_ref, *, layer):
    x, dh = x_ref[...], dh_ref[...]
    r = lax.rsqrt(jnp.mean(x * x, axis=1, keepdims=True) + EPS)
    xr = x * r
    u = dh * g_ref[layer:layer + 1, :]
    dx_ref[...] = res_ref[...] + r * u - xr * (r * r) * jnp.mean(u * x, axis=1, keepdims=True)

    @pl.when(pl.program_id(0) == 0)
    def _():
        dg_ref[...] = jnp.zeros_like(dg_ref)

    dg_ref[0:1, :] += jnp.sum(dh * xr, axis=0, keepdims=True)


def rms_bwd(x, gains8, layer, dh, d_res):
    t_rows, d = x.shape
    tm = min(ROW_TILE, t_rows)
    blk = pl.BlockSpec((tm, d), lambda i: (i, 0))
    gblk = pl.BlockSpec((8, d), lambda i: (0, 0))
    return pl.pallas_call(
        functools.partial(_rms_bwd_body, layer=layer), name=f"rms_bwd_{layer}", grid=(t_rows // tm,),
        in_specs=[blk, gblk, blk, blk],
        out_specs=[blk, gblk],
        out_shape=[jax.ShapeDtypeStruct((t_rows, d), F32), jax.ShapeDtypeStruct((8, d), F32)],
        compiler_params=_params(("arbitrary",)),
    )(x, gains8, dh, d_res)


def _piece_col(p):
    return jnp.where(p < 3 * B_GROUPS, (p % 3) * B_GROUPS + p // 3, 3 * B_GROUPS)


def _mm_nn_body(a_ref, w_ref, *rest, has_res, a_resident):
    o_ref = rest[-1]
    tm = o_ref.shape[0]
    a = a_ref[pl.ds(pl.multiple_of(pl.program_id(1) * tm, tm), tm), :] if a_resident else a_ref[...]
    acc = jnp.dot(a, w_ref[...], preferred_element_type=F32)
    if has_res:
        acc = acc + rest[0][...]
    o_ref[...] = acc


def mm_nn(a, w, residual=None, *, tn, name, a_resident=False):
    m, k = a.shape
    n = w.shape[1]
    tm = min(NT_ROW_TILE, m)
    a_spec = pl.BlockSpec((m, k), lambda j, i: (0, 0)) if a_resident else pl.BlockSpec((tm, k), lambda j, i: (i, 0))
    in_specs = [a_spec, pl.BlockSpec((k, tn), lambda j, i: (0, j))]
    args = [a, w]
    if residual is not None:
        in_specs.append(pl.BlockSpec((tm, tn), lambda j, i: (i, j)))
        args.append(residual)
    return pl.pallas_call(
        functools.partial(_mm_nn_body, has_res=residual is not None, a_resident=a_resident), name=name,
        grid=(n // tn, m // tm),
        in_specs=in_specs,
        out_specs=pl.BlockSpec((tm, tn), lambda j, i: (i, j)),
        out_shape=jax.ShapeDtypeStruct((m, n), F32),
        compiler_params=_params(("arbitrary", "arbitrary")),
    )(*args)


def mm_nn_pieces(a, w, *, name):
    m, k = a.shape
    tm = min(NT_ROW_TILE, m)
    return pl.pallas_call(
        functools.partial(_mm_nn_body, has_res=False, a_resident=True), name=name, grid=(B_PIECES, m // tm),
        in_specs=[pl.BlockSpec((m, k), lambda p, i: (0, 0)),
                  pl.BlockSpec((k, B_W), lambda p, i: (0, _piece_col(p)))],
        out_specs=pl.BlockSpec((None, tm, B_W), lambda p, i: (p, i, 0)),
        out_shape=jax.ShapeDtypeStruct((B_PIECES, m, B_W), F32),
        compiler_params=_params(("arbitrary", "arbitrary")),
    )(a, w)


NT_ROW_TILE = 1024


def _mm_nt_body(g_ref, w_ref, *rest, has_init):
    o_ref = rest[-1]
    j = pl.program_id(1)
    part = lax.dot_general(_bf(g_ref[...]), w_ref[...], (((1,), (1,)), ((), ())), preferred_element_type=F32)

    @pl.when(j == 0)
    def _():
        o_ref[...] = part + rest[0][...] if has_init else part

    @pl.when(j > 0)
    def _():
        o_ref[...] += part


def mm_nt(g, w, init=None, *, tn, col_off=0, name, after=None):
    m, n = g.shape
    k = w.shape[0]
    tm = min(NT_ROW_TILE, m)
    in_specs = [pl.BlockSpec((tm, tn), lambda i, j: (i, j)),
                pl.BlockSpec((k, tn), lambda i, j: (0, col_off + j))]
    args = [g, w]
    if init is not None:
        in_specs.append(pl.BlockSpec((tm, k), lambda i, j: (i, 0)))
        args.append(init)
    if after is not None:
        in_specs.append(HBM_SPEC)
        args.append(after)
    return pl.pallas_call(
        functools.partial(_mm_nt_body, has_init=init is not None), name=name, grid=(m // tm, n // tn),
        in_specs=in_specs,
        out_specs=pl.BlockSpec((tm, k), lambda i, j: (i, 0)),
        out_shape=jax.ShapeDtypeStruct((m, k), F32),
        compiler_params=_params(("arbitrary", "arbitrary")),
    )(*args)


def mm_nt_multi(gs, w, *, tn, name, after=None):
    m = gs[0].shape[0]
    k = w.shape[0]
    tm = min(NT_ROW_TILE, m)
    tiles = [g.shape[1] // tn for g in gs]
    starts = [sum(tiles[:i]) for i in range(len(gs))]

    def body(*refs):
        g_refs, w_ref, o_ref = refs[:len(gs)], refs[len(gs)], refs[-1]
        j = pl.program_id(1)
        for g_ref, lo, cnt in zip(g_refs, starts, tiles):
            @pl.when((j >= lo) & (j < lo + cnt))
            def _(g_ref=g_ref):
                part = lax.dot_general(_bf(g_ref[...]), w_ref[...], (((1,), (1,)), ((), ())),
                                       preferred_element_type=F32)

                @pl.when(j == 0)
                def _():
                    o_ref[...] = part

                @pl.when(j > 0)
                def _():
                    o_ref[...] += part

    def g_spec(lo, cnt):
        return pl.BlockSpec((tm, tn), lambda i, j: (i, jnp.clip(j - lo, 0, cnt - 1)))

    in_specs = [g_spec(lo, cnt) for lo, cnt in zip(starts, tiles)] + [pl.BlockSpec((k, tn), lambda i, j: (0, j))]
    args = list(gs) + [w]
    if after is not None:
        in_specs.append(HBM_SPEC)
        args.append(after)
    return pl.pallas_call(
        body, name=name, grid=(m // tm, sum(tiles)),
        in_specs=in_specs,
        out_specs=pl.BlockSpec((tm, k), lambda i, j: (i, 0)),
        out_shape=jax.ShapeDtypeStruct((m, k), F32),
        compiler_params=_params(("arbitrary", "arbitrary")),
    )(*args)


def mm_nt_pieces(g9, gz, w, *, name):
    n_p, m, _ = g9.shape
    k = w.shape[0]
    tm = min(2 * NT_ROW_TILE, m)

    def body(g_ref, z_ref, w_ref, o_ref):
        p = pl.program_id(1)

        def accumulate(src):
            part = lax.dot_general(_bf(src[...]), w_ref[...], (((1,), (1,)), ((), ())), preferred_element_type=F32)

            @pl.when(p == 0)
            def _():
                o_ref[...] = part

            @pl.when(p > 0)
            def _():
                o_ref[...] += part

        @pl.when(p < n_p)
        def _():
            accumulate(g_ref)

        @pl.when(p == n_p)
        def _():
            accumulate(z_ref)

    return pl.pallas_call(
        body, name=name, grid=(m // tm, n_p + 1),
        in_specs=[pl.BlockSpec((None, tm, B_W), lambda i, p: (jnp.minimum(p, n_p - 1), i, 0)),
                  pl.BlockSpec((tm, B_W), lambda i, p: (i, 0)),
                  pl.BlockSpec((k, B_W), lambda i, p: (0, _piece_col(p)))],
        out_specs=pl.BlockSpec((tm, k), lambda i, p: (i, 0)),
        out_shape=jax.ShapeDtypeStruct((m, k), F32),
        compiler_params=_params(("arbitrary", "arbitrary")),
    )(g9, gz, w)


def _mm_tn_body(a_ref, g_ref, o_ref, *, a_is_transposed):
    lhs_dim = 1 if a_is_transposed else 0
    o_ref[...] = lax.dot_general(a_ref[...], _bf(g_ref[...]), (((lhs_dim,), (0,)), ((), ())),
                                 preferred_element_type=F32).astype(o_ref.dtype)


def mm_tn(a, g, *, tn, out_dtype, name, a_is_transposed=False):
    k = a.shape[0] if a_is_transposed else a.shape[1]
    m, n = g.shape
    return pl.pallas_call(
        functools.partial(_mm_tn_body, a_is_transposed=a_is_transposed), name=name, grid=(n // tn,),
        in_specs=[pl.BlockSpec(a.shape, lambda j: (0, 0)), pl.BlockSpec((m, tn), lambda j: (0, j))],
        out_specs=pl.BlockSpec((k, tn), lambda j: (0, j)),
        out_shape=jax.ShapeDtypeStruct((k, n), out_dtype),
        compiler_params=_params(("arbitrary",)),
    )(a, g)


def mm_tn_multi(at, gs, *, unit, out_dtype, name):
    k, m = at.shape
    counts = [g.shape[1] // unit for g in gs]
    starts = [sum(counts[:i]) for i in range(len(gs))]

    def body(at_ref, *refs):
        g_refs, o_ref = refs[:-1], refs[-1]
        u = pl.program_id(0)
        for g_ref, lo, cnt in zip(g_refs, starts, counts):
            @pl.when((u >= lo) & (u < lo + cnt))
            def _(g_ref=g_ref):
                _mm_tn_body(at_ref, g_ref, o_ref, a_is_transposed=True)

    def g_spec(lo, cnt):
        return pl.BlockSpec((m, unit), lambda u: (0, jnp.clip(u - lo, 0, cnt - 1)))

    return pl.pallas_call(
        body, name=name, grid=(sum(counts),),
        in_specs=[pl.BlockSpec((k, m), lambda u: (0, 0))] + [g_spec(lo, cnt) for lo, cnt in zip(starts, counts)],
        out_specs=pl.BlockSpec((k, unit), lambda u: (0, u)),
        out_shape=jax.ShapeDtypeStruct((k, sum(counts) * unit), out_dtype),
        compiler_params=_params(("arbitrary",)),
    )(at, *gs)


B_UNIT = 256
B_IN_COLS = B_PIECES * B_W
B_SHARD_UNITS = B_IN_COLS // N_DEV // B_UNIT


def mm_tn_b_in(at, g9, gz, *, out_dtype, name):
    k, m = at.shape
    per_piece = B_W // B_UNIT
    body_one = functools.partial(_mm_tn_body, a_is_transposed=True)
    n_units = B_IN_COLS // B_UNIT

    def g_map(u):
        nat = jnp.minimum(u // per_piece, 3 * B_GROUPS - 1)
        piece = (nat % B_GROUPS) * 3 + nat // B_GROUPS
        return (piece, 0, u % per_piece)

    def body(a_ref, g_ref, z_ref, o_ref):
        u = pl.program_id(0)

        @pl.when(u < 3 * B_GROUPS * per_piece)
        def _():
            body_one(a_ref, g_ref, o_ref)

        @pl.when(u >= 3 * B_GROUPS * per_piece)
        def _():
            body_one(a_ref, z_ref, o_ref)

    return pl.pallas_call(
        body, name=name, grid=(n_units,),
        in_specs=[pl.BlockSpec((k, m), lambda u: (0, 0)),
                  pl.BlockSpec((None, m, B_UNIT), g_map),
                  pl.BlockSpec((m, B_UNIT), lambda u: (0, jnp.where(u < 3 * B_GROUPS * per_piece, 0, u % per_piece)))],
        out_specs=pl.BlockSpec((None, k, B_UNIT), lambda u: (u // B_SHARD_UNITS, 0, u % B_SHARD_UNITS)),
        out_shape=jax.ShapeDtypeStruct((N_DEV, k, B_IN_COLS // N_DEV), out_dtype),
        compiler_params=_params(("arbitrary",)),
    )(at, g9, gz)


def _out_norm_body(a_ref, w_ref, res_ref, g_ref, x_ref, h_ref, ht_ref, *, layer):
    x = jnp.dot(a_ref[...], w_ref[...], preferred_element_type=F32) + res_ref[...]
    x_ref[...] = x
    r = lax.rsqrt(jnp.mean(x * x, axis=1, keepdims=True) + EPS)
    h = x * r * g_ref[layer:layer + 1, :]
    h_ref[...] = h.astype(h_ref.dtype)
    ht_ref[...] = h.T.astype(ht_ref.dtype)


def out_and_norm(a, w, residual, gains8, layer):
    t_rows, k = a.shape
    d = w.shape[1]
    tm = min(ROW_TILE, t_rows)
    blk = pl.BlockSpec((tm, d), lambda i: (i, 0))
    return pl.pallas_call(
        functools.partial(_out_norm_body, layer=layer), name=f"out_norm_{layer}", grid=(t_rows // tm,),
        in_specs=[pl.BlockSpec((tm, k), lambda i: (i, 0)), pl.BlockSpec((k, d), lambda i: (0, 0)), blk,
                  pl.BlockSpec((8, d), lambda i: (0, 0))],
        out_specs=[blk, blk, pl.BlockSpec((d, tm), lambda i: (0, i))],
        out_shape=[jax.ShapeDtypeStruct((t_rows, d), F32), jax.ShapeDtypeStruct((t_rows, d), BF16),
                   jax.ShapeDtypeStruct((d, t_rows), BF16)],
        compiler_params=_params(("arbitrary",)),
    )(a, w, residual, gains8)


def _out_loss_body(a_ref, w_ref, res_ref, t_ref, dy_ref, loss_ref, acc, *, n_steps):
    i = pl.program_id(0)
    d = w_ref.shape[1]
    y = jnp.dot(a_ref[...], w_ref[...], preferred_element_type=F32) + res_ref[...]
    err = y - t_ref[...]
    dy_ref[...] = err * (1.0 / d)

    @pl.when(i == 0)
    def _():
        acc[...] = jnp.zeros_like(acc)

    acc[...] += jnp.sum(err * err, axis=0, keepdims=True)

    @pl.when(i == n_steps - 1)
    def _():
        total = jnp.sum(acc[...], axis=1, keepdims=True) * (0.5 / d)
        loss_ref[...] = jnp.broadcast_to(total, loss_ref.shape)


def out_and_loss(a, w, residual, target):
    t_rows, k = a.shape
    d = w.shape[1]
    tm = min(ROW_TILE, t_rows)
    n_steps = t_rows // tm
    blk = pl.BlockSpec((tm, d), lambda i: (i, 0))
    return pl.pallas_call(
        functools.partial(_out_loss_body, n_steps=n_steps), name="out_b_loss", grid=(n_steps,),
        in_specs=[pl.BlockSpec((tm, k), lambda i: (i, 0)), pl.BlockSpec((k, d), lambda i: (0, 0)), blk, blk],
        out_specs=[blk, pl.BlockSpec((8, 128), lambda i: (0, 0))],
        out_shape=[jax.ShapeDtypeStruct((t_rows, d), F32), jax.ShapeDtypeStruct((8, 128), F32)],
        scratch_shapes=[pltpu.VMEM((1, d), F32)],
        compiler_params=_params(("arbitrary",)),
    )(a, w, residual, target)


def _adamw_body(p_ref, w_ref, m_ref, v_ref, g_ref, d_ref, nm_ref, nv_ref):
    g = p_ref[0].astype(F32)
    for s in range(1, N_DEV):
        g = g + p_ref[s].astype(F32)
    w = w_ref[...]
    m = ADAM_B1 * m_ref[...] + (1.0 - ADAM_B1) * g
    v = ADAM_B2 * v_ref[...] + (1.0 - ADAM_B2) * (g * g)
    m_hat = m / (1.0 - ADAM_B1 ** ADAM_STEP)
    v_hat = v / (1.0 - ADAM_B2 ** ADAM_STEP)
    g_ref[...] = g
    d_ref[...] = -ADAM_LR * (m_hat / (jnp.sqrt(v_hat) + ADAM_EPS) + ADAM_WD * w)
    nm_ref[...] = m
    nv_ref[...] = v


def adamw(parts, w, m, v, *, name):
    _, r, c = w.shape
    tr = r if r <= 256 else 256
    blk = pl.BlockSpec((None, tr, c), lambda i: (0, i, 0))
    out = jax.ShapeDtypeStruct((1, r, c), F32)
    return pl.pallas_call(
        _adamw_body, name=name, grid=(r // tr,),
        in_specs=[pl.BlockSpec((N_DEV, tr, c), lambda i: (0, i, 0)), blk, blk, blk],
        out_specs=[blk, blk, blk, blk],
        out_shape=[out, out, out, out],
        compiler_params=_params(("arbitrary",)),
    )(parts, w, m, v)


MESH_ID = pl.DeviceIdType.MESH
HBM_SPEC = pl.BlockSpec(memory_space=pl.ANY)


def _my_place():
    return lax.axis_index("x"), lax.axis_index("y"), lax.axis_index("c")


def _flat(x, y, c):
    return 4 * x + 2 * y + c


def _all_gather_body(*refs, n):
    ins, outs = refs[:n], refs[n:2 * n]
    send_sems, recv_sems, local_sems = refs[2 * n:]
    x, y, c = _my_place()
    me, sibling = (x, y, c), (x, y, 1 - c)
    chips = [(1 - x, y), (x, 1 - y), (1 - x, 1 - y)]
    pending = []
    for a in range(n):
        src, out = ins[a], outs[a]

        def copy(k, block, to, from_input=False, a=a, src=src, out=out):
            slot = out.at[_flat(*block)]
            return pltpu.make_async_remote_copy(
                src_ref=src if from_input else slot, dst_ref=slot,
                send_sem=send_sems.at[7 * a + k], recv_sem=recv_sems.at[7 * a + k],
                device_id=to, device_id_type=MESH_ID)

        mine = pltpu.make_async_copy(src, out.at[_flat(*me)], local_sems.at[a])
        mine.start()
        first = [copy(0, me, sibling, True)] + [copy(1 + j, me, (*chip, c), True) for j, chip in enumerate(chips)]
        for cp in first:
            cp.start()
        pending.append((copy, mine, first))
    for copy, mine, first in pending:
        passed = [copy(4 + j, (*chip, c), sibling) for j, chip in enumerate(chips)]
        for j, chip in enumerate(chips):
            copy(1 + j, (*chip, c), me).wait_recv()
            passed[j].start()
        copy(0, sibling, me).wait_recv()
        for j, chip in enumerate(chips):
            copy(4 + j, (*chip, 1 - c), me).wait_recv()
        for cp in first + passed:
            cp.wait_send()
        mine.wait()


def all_gather(shards, *, name):
    n = len(shards)
    return pl.pallas_call(
        functools.partial(_all_gather_body, n=n), name=name,
        in_specs=[HBM_SPEC] * n, out_specs=[HBM_SPEC] * n,
        out_shape=[jax.ShapeDtypeStruct((N_DEV,) + s.shape, s.dtype) for s in shards],
        scratch_shapes=[pltpu.SemaphoreType.DMA((7 * n,)), pltpu.SemaphoreType.DMA((7 * n,)),
                        pltpu.SemaphoreType.DMA((n,))],
    )(*shards)


PEER_FLIPS = [(0, 0, 1), (1, 0, 0), (0, 1, 0), (1, 1, 0), (1, 0, 1), (0, 1, 1), (1, 1, 1)]


def _all_to_all_body(*refs, n):
    ins, outs = refs[:n], refs[n:2 * n]
    send_sems, recv_sems, local_sems = refs[2 * n:]
    x, y, c = _my_place()
    me = _flat(x, y, c)
    waits = []
    for a in range(n):
        src, out = ins[a], outs[a]
        mine = pltpu.make_async_copy(src.at[me], out.at[me], local_sems.at[a])
        mine.start()
        waits.append(mine)
        for k, (fx, fy, fc) in enumerate(PEER_FLIPS):
            peer = (1 - x if fx else x, 1 - y if fy else y, 1 - c if fc else c)
            theirs = _flat(*peer)
            sems = dict(send_sem=send_sems.at[7 * a + k], recv_sem=recv_sems.at[7 * a + k],
                        device_id=peer, device_id_type=MESH_ID)
            send = pltpu.make_async_remote_copy(src_ref=src.at[theirs], dst_ref=out.at[me], **sems)
            send.start()
            recv = pltpu.make_async_remote_copy(src_ref=src.at[theirs], dst_ref=out.at[theirs], **sems)
            waits.append((send, recv))
    for w in waits:
        if isinstance(w, tuple):
            w[0].wait_send()
            w[1].wait_recv()
        else:
            w.wait()


def all_to_all(parts, *, name):
    n = len(parts)
    return pl.pallas_call(
        functools.partial(_all_to_all_body, n=n), name=name,
        in_specs=[HBM_SPEC] * n, out_specs=[HBM_SPEC] * n,
        out_shape=[jax.ShapeDtypeStruct(p.shape, p.dtype) for p in parts],
        scratch_shapes=[pltpu.SemaphoreType.DMA((7 * n,)), pltpu.SemaphoreType.DMA((7 * n,)),
                        pltpu.SemaphoreType.DMA((n,))],
    )(*parts)


HBM_ONLY = pl.BlockSpec(memory_space=pltpu.HBM)
SEM_SPEC = pl.BlockSpec(memory_space=pltpu.SEMAPHORE)
DATAFLOW_EFFECT = pltpu.SideEffectType.DATAFLOW_SIDE_EFFECTING


def _split_copies(srcs, lands, send_sems, recv_sems, n, scatter):
    x, y, c = _my_place()
    me = _flat(x, y, c)
    pairs = []
    for a in range(n):
        for k, (fx, fy, fc) in enumerate(PEER_FLIPS):
            peer = (1 - x if fx else x, 1 - y if fy else y, 1 - c if fc else c)
            theirs = _flat(*peer)
            src = srcs[a].at[theirs] if scatter else srcs[a]
            sems = dict(send_sem=send_sems.at[7 * a + k], recv_sem=recv_sems.at[7 * a + k],
                        device_id=peer, device_id_type=MESH_ID)
            pairs.append((pltpu.make_async_remote_copy(src_ref=src, dst_ref=lands[a].at[me], **sems),
                          pltpu.make_async_remote_copy(src_ref=src, dst_ref=lands[a].at[theirs], **sems)))
    return pairs


def _exchange_start_body(*refs, n, scatter):
    srcs, lands = refs[:n], refs[n:2 * n]
    send_sems, recv_sems = refs[2 * n], refs[2 * n + 1]
    token = refs[-1]
    for send, _ in _split_copies(srcs, lands, send_sems, recv_sems, n, scatter):
        send.start()
    token[...] = jnp.zeros_like(token)


def exchange_start(srcs, lands, *, scatter, name):
    n = len(srcs)
    args = [pltpu.with_memory_space_constraint(t, pltpu.HBM) for t in list(srcs) + list(lands)]
    outs = pl.pallas_call(
        functools.partial(_exchange_start_body, n=n, scatter=scatter), name=name,
        out_shape=(pltpu.SemaphoreType.DMA((7 * n,)), pltpu.SemaphoreType.DMA((7 * n,)),
                   *[pltpu.HBM(t.shape, t.dtype) for t in args],
                   jax.ShapeDtypeStruct((8, 128), F32)),
        in_specs=[HBM_ONLY] * (2 * n),
        out_specs=(SEM_SPEC, SEM_SPEC, *[HBM_ONLY] * (2 * n), pl.BlockSpec(memory_space=pltpu.VMEM)),
        input_output_aliases={i: 2 + i for i in range(2 * n)},
        compiler_params=pltpu.CompilerParams(has_side_effects=DATAFLOW_EFFECT),
    )(*args)
    return outs[0], outs[1], outs[2:2 + n], outs[2 + n:2 + 2 * n], outs[-1]


def _exchange_wait_body(*refs, n, scatter):
    srcs, lands = refs[:n], refs[n:2 * n]
    send_sems, recv_sems = refs[2 * n], refs[2 * n + 1]
    for send, recv in _split_copies(srcs, lands, send_sems, recv_sems, n, scatter):
        send.wait_send()
        recv.wait_recv()


def exchange_wait(send_sems, recv_sems, srcs, lands, after, *, scatter, name):
    n = len(srcs)
    outs = pl.pallas_call(
        functools.partial(_exchange_wait_body, n=n, scatter=scatter), name=name,
        out_shape=tuple(pltpu.HBM(t.shape, t.dtype) for t in list(srcs) + list(lands)),
        in_specs=[HBM_ONLY] * (2 * n) + [SEM_SPEC, SEM_SPEC, HBM_SPEC],
        out_specs=tuple([HBM_ONLY] * (2 * n)),
        input_output_aliases={i: i for i in range(2 * n)},
        compiler_params=pltpu.CompilerParams(has_side_effects=DATAFLOW_EFFECT),
    )(*srcs, *lands, send_sems, recv_sems, after)
    return outs[n:]


def _own_slot_only(shape_dtype, own, slot):
    land = lax.empty(shape_dtype.shape, shape_dtype.dtype)
    return lax.dynamic_update_slice(land, own[None], (slot,) + (0,) * own.ndim)


def _pad_rows(a, rows=8):
    return jnp.pad(a, ((0, rows - a.shape[0]), (0, 0)))


def _gate_rows(a_log, dt_bias):
    z = jnp.zeros((8, 128), F32)
    return z.at[0, A_HEADS:2 * A_HEADS].set(a_log[0]).at[1, A_HEADS:2 * A_HEADS].set(dt_bias[0])


def _pack_small(norm_g, a_log, a_dt_bias, a_norm_g, b_q_norm_g, b_k_norm_g):
    return jnp.concatenate([
        norm_g[0].reshape(8, 128), norm_g[1].reshape(8, 128),
        _gate_rows(a_log, a_dt_bias),
        _pad_rows(a_norm_g[0].reshape(2, 128)),
        _pad_rows(jnp.concatenate([b_q_norm_g[0], b_k_norm_g[0]], axis=0)),
    ], axis=0)


def _unpack_small(p):
    return (p[0:16].reshape(2, D_MODEL), p[16:17, A_HEADS:2 * A_HEADS], p[17:18, A_HEADS:2 * A_HEADS],
            p[24:26].reshape(1, A_DV), p[32:35][None], p[35:38][None])


def kernel(x, positions, norm_g, a_w_in, a_conv_w, a_log, a_dt_bias, a_norm_g, a_w_out, b_w_in, b_q_norm_g, b_k_norm_g, b_w_out, loss_target, m_norm_g, m_a_w_in, m_a_conv_w, m_a_log, m_a_dt_bias, m_a_norm_g, m_a_w_out, m_b_w_in, m_b_q_norm_g, m_b_k_norm_g, m_b_w_out, v_norm_g, v_a_w_in, v_a_conv_w, v_a_log, v_a_dt_bias, v_a_norm_g, v_a_w_out, v_b_w_in, v_b_q_norm_g, v_b_k_norm_g, v_b_w_out):
    n_seq, s_len, d = x.shape
    t_rows = n_seq * s_len
    n_chunks = s_len // CHUNK
    x0 = x.reshape(t_rows, d)
    target = loss_target.reshape(t_rows, d)
    my_slot = _flat(*_my_place())

    g_a_in, g_conv = all_gather([a_w_in[0].astype(BF16), _pad_rows(a_conv_w[0])], name="gather_weights_first")
    later = [a_w_out[0].astype(BF16), b_w_in[0].astype(BF16), b_w_out[0].astype(BF16)]
    lands = [_own_slot_only(jax.ShapeDtypeStruct((N_DEV,) + t.shape, t.dtype), t, my_slot) for t in later]
    w_send, w_recv, later, lands, w_token = exchange_start(later, lands, scatter=False, name="gather_weights_start")
    w_a_in = jnp.pad(g_a_in.transpose(1, 0, 2).reshape(d, A_IN), ((0, 0), (0, A_IN_PAD - A_IN)))
    conv_w8 = g_conv.transpose(1, 0, 2).reshape(8, 2 * A_QK + A_VW)

    gains_model = _pad_rows(norm_g)
    gate_prm = _gate_rows(a_log, a_dt_bias)
    gain_a_out = _pad_rows(a_norm_g)
    gains_qk = _pad_rows(jnp.concatenate([b_q_norm_g[0], b_k_norm_g[0]], axis=0))
    cos_t, sin_t = rope_tables(positions.reshape(t_rows))

    h0, h0_t = rms_fwd(x0, gains_model, 0, after=w_token)
    proj_a = mm_nn(h0, w_a_in, tn=896, name="proj_a", a_resident=True)
    gates_col, gates_row = a_gates_fwd(proj_a, gate_prm, n_seq)
    gates_row = gates_row.reshape(n_seq, 2 * A_HEADS, s_len // SUPER, 1, SUPER)
    o_a, tinv, states, og_a, q_a, k_a, v_a = gdn_fwd(proj_a, conv_w8, gates_col, gates_row, gain_a_out, n_seq)
    g_a_out, g_b_in, g_b_out = exchange_wait(w_send, w_recv, later, lands, og_a, scatter=False,
                                             name="gather_weights_wait")
    w_a_out = g_a_out.reshape(A_VW, d)
    w_b_in = g_b_in.transpose(1, 0, 2).reshape(d, B_IN_COLS)
    w_b_out = g_b_out.reshape(B_W, d)
    x1, h1, h1_t = out_and_norm(og_a, w_a_out, x0, gains_model, 1)

    proj_b = mm_nn_pieces(h1, w_b_in, name="proj_b")
    og_b, o_b, lse = attn_fwd(proj_b, cos_t, sin_t, gains_qk, n_seq)
    dy, loss_blk = out_and_loss(og_b, w_b_out, x1, target)

    d_og_b = mm_nt(dy, w_b_out, tn=1024, name="d_og_b")
    dw_b_out = mm_tn(og_b, dy, tn=256, out_dtype=BF16, name="dw_b_out")
    dqkv_b, dz_b, d_gains_qk = attn_bwd(proj_b, cos_t, sin_t, gains_qk, d_og_b, o_b, lse, n_seq)
    dh1 = mm_nt_pieces(dqkv_b, dz_b, w_b_in, name="dh1")
    dw_b_in = mm_tn_b_in(h1_t, dqkv_b, dz_b, out_dtype=BF16, name="dw_b_in")
    dx1, d_gain1 = rms_bwd(x1, gains_model, 1, dh1, dy)

    dw_a_out = mm_tn(og_a, dx1, tn=256, out_dtype=BF16, name="dw_a_out")
    early = [dw_b_in, dw_b_out.reshape(N_DEV, B_W // N_DEV, d), dw_a_out.reshape(N_DEV, A_VW // N_DEV, d)]
    lands = [_own_slot_only(t, lax.dynamic_index_in_dim(t, my_slot, 0, keepdims=False), my_slot) for t in early]
    g_send, g_recv, early, lands, g_token = exchange_start(early, lands, scatter=True, name="scatter_grads_start")

    d_og_a = mm_nt(dx1, w_a_out, tn=1024, name="d_og_a", after=g_token)
    d_xq, d_xk, d_xv, dgates, dz_a, d_gain_a_out, d_cq, d_ck, d_cv = gdn_bwd(
        q_a, k_a, v_a, gates_col, gates_row, tinv, states, d_og_a, o_a, proj_a, conv_w8, gain_a_out, n_seq)
    d_conv = jnp.concatenate([d_cq.sum(axis=0), d_ck.sum(axis=0), d_cv.sum(axis=0)], axis=1)
    d_gate_logits, d_gate_prm = a_gates_bwd(proj_a, gate_prm, dgates, n_seq)
    dw_a_in = jnp.concatenate([
        mm_tn_multi(h0_t, [d_xq, d_xk, d_xv, dz_a], unit=256, out_dtype=BF16, name="dw_a_in_qkvz"),
        mm_tn(h0_t, d_gate_logits, tn=128, out_dtype=BF16, name="dw_a_in_gates", a_is_transposed=True),
    ], axis=1)[:, :A_IN]
    shard_a_in = A_IN // N_DEV
    last = [dw_a_in.reshape(d, N_DEV, shard_a_in).transpose(1, 0, 2)]
    last_lands = [_own_slot_only(t, lax.dynamic_index_in_dim(t, my_slot, 0, keepdims=False), my_slot) for t in last]
    l_send, l_recv, last, last_lands, l_token = exchange_start(last, last_lands, scatter=True,
                                                               name="scatter_last_start")
    dh0 = mm_nt_multi([d_xq, d_xk, d_xv, dz_a], w_a_in, tn=1024, name="dh0_qkvz", after=l_token)
    dh0 = mm_nt(d_gate_logits, w_a_in, dh0, tn=128, col_off=A_GATE_COL, name="dh0_gates")
    dx0, d_gain0 = rms_bwd(x0, gains_model, 0, dh0, dx1)

    small = jnp.concatenate([
        d_gain0[0].reshape(8, 128), d_gain1[0].reshape(8, 128), d_gate_prm,
        _pad_rows(d_gain_a_out[0].reshape(2, 128)), d_gains_qk, loss_blk], axis=0)
    n_small = small.shape[0] - loss_blk.shape[0]
    small_srcs = [small, d_conv]
    small_lands = [_own_slot_only(jax.ShapeDtypeStruct((N_DEV,) + t.shape, t.dtype), t, my_slot)
                   for t in small_srcs]
    s_send, s_recv, small_srcs, small_lands, s_token = exchange_start(small_srcs, small_lands, scatter=False,
                                                                      name="gather_small_start")

    r_b_in, r_b_out, r_a_out = exchange_wait(g_send, g_recv, early, lands, s_token, scatter=True,
                                             name="scatter_grads_wait")
    (r_a_in,) = exchange_wait(l_send, l_recv, last, last_lands, s_token, scatter=True, name="scatter_last_wait")

    upd = {}
    upd["a_w_in"] = adamw(r_a_in, a_w_in, m_a_w_in, v_a_w_in, name="adamw_a_w_in")
    upd["a_w_out"] = adamw(r_a_out, a_w_out, m_a_w_out, v_a_w_out, name="adamw_a_w_out")
    upd["b_w_in"] = adamw(r_b_in, b_w_in, m_b_w_in, v_b_w_in, name="adamw_b_w_in")
    upd["b_w_out"] = adamw(r_b_out, b_w_out, m_b_w_out, v_b_w_out, name="adamw_b_w_out")
    r_small, r_conv = exchange_wait(s_send, s_recv, small_srcs, small_lands, upd["b_w_in"][0], scatter=False,
                                    name="gather_small_wait")
    conv_cols = a_conv_w.shape[2]
    r_conv = lax.dynamic_slice(r_conv, (0, 0, my_slot * conv_cols), (N_DEV, 8, conv_cols))
    loss = jnp.sum(r_small[:, n_small, 0])
    r_small = r_small[:, :n_small]
    upd["a_conv_w"] = [t[:, :A_CONV] for t in adamw(
        r_conv, _pad_rows(a_conv_w[0])[None], _pad_rows(m_a_conv_w[0])[None], _pad_rows(v_a_conv_w[0])[None],
        name="adamw_a_conv_w")]
    small_upd = adamw(
        r_small,
        _pack_small(norm_g, a_log, a_dt_bias, a_norm_g, b_q_norm_g, b_k_norm_g)[None],
        _pack_small(m_norm_g, m_a_log, m_a_dt_bias, m_a_norm_g, m_b_q_norm_g, m_b_k_norm_g)[None],
        _pack_small(v_norm_g, v_a_log, v_a_dt_bias, v_a_norm_g, v_b_q_norm_g, v_b_k_norm_g)[None],
        name="adamw_small")
    small_names = ("norm_g", "a_log", "a_dt_bias", "a_norm_g", "b_q_norm_g", "b_k_norm_g")
    unpacked = [_unpack_small(t[0]) for t in small_upd]
    for i, nm in enumerate(small_names):
        upd[nm] = [u[i] for u in unpacked]

    order = ("norm_g", "a_w_in", "a_conv_w", "a_log", "a_dt_bias", "a_norm_g", "a_w_out",
             "b_w_in", "b_q_norm_g", "b_k_norm_g", "b_w_out")
    outs = [loss, dx0.reshape(n_seq, s_len, d)]
    for kind in range(4):
        for nm in order:
            outs.append(upd[nm][kind])
    return tuple(outs)
```
